```python
import jax, jax.numpy as jnp
from jax import lax
import numpy as np

D_MODEL = 1024
BATCH = 8
SEQ = 4096
DEPTH = 2

EPS = 1e-6
CHUNK = 128
HEAD_DIM = 128
A_HEADS = 4
B_HEADS = 4
D_A = A_HEADS * HEAD_DIM
D_B = B_HEADS * HEAD_DIM
D_AB = D_A + D_B
B_CONV = 31
D_C = D_MODEL
C_CONV = 3
D_FF = 2816
F_CONV = 3
N_EVEN = (DEPTH + 1) // 2
N_ODD = DEPTH // 2

kernel_name = "hybrid_sgu_conformer_shortconv_convffn"


def _rmsnorm(x, g):
    xf = x.astype(jnp.float32)
    y = xf * lax.rsqrt(jnp.mean(xf * xf, axis=-1, keepdims=True) + EPS)
    return (y * g.astype(jnp.float32)).astype(x.dtype)


def _layernorm(x, g, b):
    xf = x.astype(jnp.float32)
    mu = jnp.mean(xf, axis=-1, keepdims=True)
    xc = xf - mu
    var = jnp.mean(xc * xc, axis=-1, keepdims=True)
    y = xc * lax.rsqrt(var + EPS) * g.astype(jnp.float32) + b.astype(jnp.float32)
    return y.astype(x.dtype)


def _causal_dwconv(x, w):
    k, c = w.shape
    return lax.conv_general_dilated(
        x, w[:, None, :].astype(x.dtype), window_strides=(1,),
        padding=[(k - 1, 0)], dimension_numbers=("NWC", "WIO", "NWC"),
        feature_group_count=c)


def _chunked_sgu(u, v, ln_g, ln_b, w_s, b_s):
    bsz, s, _ = v.shape
    v = _layernorm(v, ln_g, ln_b)
    vc = v.reshape(bsz, s // CHUNK, CHUNK, A_HEADS, HEAD_DIM)
    causal = jnp.tril(jnp.ones((CHUNK, CHUNK), dtype=w_s.dtype))
    mixed = jnp.einsum("hts,bnshd->bnthd", w_s * causal, vc)
    mixed = mixed + b_s.T[None, None, :, :, None]
    return u * mixed.reshape(bsz, s, D_A)


def _even_mixer(h, w_in, a_ln_g, a_ln_b, a_w_s, a_b_s, b_conv_w, b_conv_b, b_ln_g, b_ln_b, w_out):
    z = h @ w_in
    ua, va, xb, gb = jnp.split(z, [D_A, 2 * D_A, 2 * D_A + D_B], axis=-1)
    ya = _chunked_sgu(jax.nn.gelu(ua, approximate=False), jax.nn.gelu(va, approximate=False),
                      a_ln_g, a_ln_b, a_w_s, a_b_s)
    yb = xb * jax.nn.sigmoid(gb)
    yb = _causal_dwconv(yb, b_conv_w) + b_conv_b
    yb = jax.nn.silu(_layernorm(yb, b_ln_g, b_ln_b))
    return jnp.concatenate([ya, yb], axis=-1) @ w_out


def _odd_mixer(h, w_in, conv_w, w_out):
    b_gate, c_gate, xv = jnp.split(h @ w_in, 3, axis=-1)
    return (b_gate * _causal_dwconv(c_gate * xv, conv_w)) @ w_out


def _conv_ffn(h, w_up, conv_w, w_down):
    up = _causal_dwconv(h @ w_up, conv_w)
    g, v = jnp.split(up, 2, axis=-1)
    return (jax.nn.silu(g) * v) @ w_down


def _fwd_setup_inputs(seed: int = 0) -> dict:
    key = jax.random.key(seed)
    ks = jax.random.split(key, 24)
    f32 = jnp.float32

    def nrm(k, shape, scale):
        return jax.random.normal(k, shape, f32) * scale

    def gain(k, shape):
        return 1.0 + 0.02 * jax.random.normal(k, shape, f32)

    return {
        "x": jax.random.normal(ks[0], (BATCH, SEQ, D_MODEL), f32),
        "norm_mix": gain(ks[1], (DEPTH, D_MODEL)),
        "norm_ffn": gain(ks[2], (DEPTH, D_MODEL)),
        "norm_final": gain(ks[3], (D_MODEL,)),
        "ab_w_in": nrm(ks[4], (N_EVEN, D_MODEL, 2 * D_AB), D_MODEL ** -0.5),
        "a_ln_g": gain(ks[5], (N_EVEN, D_A)),
        "a_ln_b": nrm(ks[6], (N_EVEN, D_A), 0.02),
        "a_w_s": nrm(ks[7], (N_EVEN, A_HEADS, CHUNK, CHUNK), CHUNK ** -0.5),
        "a_b_s": gain(ks[8], (N_EVEN, A_HEADS, CHUNK)),
        "b_conv_w": nrm(ks[9], (N_EVEN, B_CONV, D_B), B_CONV ** -0.5),
        "b_conv_b": nrm(ks[10], (N_EVEN, D_B), 0.02),
        "b_ln_g": gain(ks[11], (N_EVEN, D_B)),
        "b_ln_b": nrm(ks[12], (N_EVEN, D_B), 0.02),
        "ab_w_out": nrm(ks[13], (N_EVEN, D_AB, D_MODEL), D_AB ** -0.5),
        "c_w_in": nrm(ks[14], (N_ODD, D_MODEL, 3 * D_C), D_MODEL ** -0.5),
        "c_conv_w": nrm(ks[15], (N_ODD, C_CONV, D_C), C_CONV ** -0.5),
        "c_w_out": nrm(ks[16], (N_ODD, D_C, D_MODEL), D_C ** -0.5),
        "f_w_up": nrm(ks[17], (DEPTH, D_MODEL, 2 * D_FF), D_MODEL ** -0.5),
        "f_conv_w": nrm(ks[18], (DEPTH, F_CONV, 2 * D_FF), F_CONV ** -0.5),
        "f_w_down": nrm(ks[19], (DEPTH, D_FF, D_MODEL), D_FF ** -0.5),
    }


def _fwd_reference(x, norm_mix, norm_ffn, norm_final, ab_w_in, a_ln_g, a_ln_b, a_w_s, a_b_s,
              b_conv_w, b_conv_b, b_ln_g, b_ln_b, ab_w_out, c_w_in, c_conv_w, c_w_out,
              f_w_up, f_conv_w, f_w_down):
    for layer in range(DEPTH):
        h = _rmsnorm(x, norm_mix[layer])
        if layer % 2 == 0:
            i = layer // 2
            y = _even_mixer(h, ab_w_in[i], a_ln_g[i], a_ln_b[i], a_w_s[i], a_b_s[i],
                            b_conv_w[i], b_conv_b[i], b_ln_g[i], b_ln_b[i], ab_w_out[i])
        else:
            i = layer // 2
            y = _odd_mixer(h, c_w_in[i], c_conv_w[i], c_w_out[i])
        x = x + y
        h = _rmsnorm(x, norm_ffn[layer])
        x = x + _conv_ffn(h, f_w_up[layer], f_conv_w[layer], f_w_down[layer])
    return _rmsnorm(x, norm_final)


import jax as _jax
import jax.numpy as _jnp

TWIN_FORMAT = 'train_step'
FWD_PARAMS = ['x', 'norm_mix', 'norm_ffn', 'norm_final', 'ab_w_in', 'a_ln_g', 'a_ln_b', 'a_w_s', 'a_b_s', 'b_conv_w', 'b_conv_b', 'b_ln_g', 'b_ln_b', 'ab_w_out', 'c_w_in', 'c_conv_w', 'c_w_out', 'f_w_up', 'f_conv_w', 'f_w_down']
TWIN_WEIGHTS = ['norm_mix', 'norm_ffn', 'norm_final', 'ab_w_in', 'a_ln_g', 'a_ln_b', 'a_w_s', 'a_b_s', 'b_conv_w', 'b_conv_b', 'b_ln_g', 'b_ln_b', 'ab_w_out', 'c_w_in', 'c_conv_w', 'c_w_out', 'f_w_up', 'f_conv_w', 'f_w_down']
TWIN_DIFF_INPUT = 'x'
TWIN_INPUTS = ['x', 'norm_mix', 'norm_ffn', 'norm_final', 'ab_w_in', 'a_ln_g', 'a_ln_b', 'a_w_s', 'a_b_s', 'b_conv_w', 'b_conv_b', 'b_ln_g', 'b_ln_b', 'ab_w_out', 'c_w_in', 'c_conv_w', 'c_w_out', 'f_w_up', 'f_conv_w', 'f_w_down', 'loss_target', 'm_norm_mix', 'm_norm_ffn', 'm_norm_final', 'm_ab_w_in', 'm_a_ln_g', 'm_a_ln_b', 'm_a_w_s', 'm_a_b_s', 'm_b_conv_w', 'm_b_conv_b', 'm_b_ln_g', 'm_b_ln_b', 'm_ab_w_out', 'm_c_w_in', 'm_c_conv_w', 'm_c_w_out', 'm_f_w_up', 'm_f_conv_w', 'm_f_w_down', 'v_norm_mix', 'v_norm_ffn', 'v_norm_final', 'v_ab_w_in', 'v_a_ln_g', 'v_a_ln_b', 'v_a_w_s', 'v_a_b_s', 'v_b_conv_w', 'v_b_conv_b', 'v_b_ln_g', 'v_b_ln_b', 'v_ab_w_out', 'v_c_w_in', 'v_c_conv_w', 'v_c_w_out', 'v_f_w_up', 'v_f_conv_w', 'v_f_w_down']
TWIN_OUTPUTS = ['loss', 'grad_x', 'grad_norm_mix', 'grad_norm_ffn', 'grad_norm_final', 'grad_ab_w_in', 'grad_a_ln_g', 'grad_a_ln_b', 'grad_a_w_s', 'grad_a_b_s', 'grad_b_conv_w', 'grad_b_conv_b', 'grad_b_ln_g', 'grad_b_ln_b', 'grad_ab_w_out', 'grad_c_w_in', 'grad_c_conv_w', 'grad_c_w_out', 'grad_f_w_up', 'grad_f_conv_w', 'grad_f_w_down', 'delta_norm_mix', 'delta_norm_ffn', 'delta_norm_final', 'delta_ab_w_in', 'delta_a_ln_g', 'delta_a_ln_b', 'delta_a_w_s', 'delta_a_b_s', 'delta_b_conv_w', 'delta_b_conv_b', 'delta_b_ln_g', 'delta_b_ln_b', 'delta_ab_w_out', 'delta_c_w_in', 'delta_c_conv_w', 'delta_c_w_out', 'delta_f_w_up', 'delta_f_conv_w', 'delta_f_w_down', 'new_m_norm_mix', 'new_m_norm_ffn', 'new_m_norm_final', 'new_m_ab_w_in', 'new_m_a_ln_g', 'new_m_a_ln_b', 'new_m_a_w_s', 'new_m_a_b_s', 'new_m_b_conv_w', 'new_m_b_conv_b', 'new_m_b_ln_g', 'new_m_b_ln_b', 'new_m_ab_w_out', 'new_m_c_w_in', 'new_m_c_conv_w', 'new_m_c_w_out', 'new_m_f_w_up', 'new_m_f_conv_w', 'new_m_f_w_down', 'new_v_norm_mix', 'new_v_norm_ffn', 'new_v_norm_final', 'new_v_ab_w_in', 'new_v_a_ln_g', 'new_v_a_ln_b', 'new_v_a_w_s', 'new_v_a_b_s', 'new_v_b_conv_w', 'new_v_b_conv_b', 'new_v_b_ln_g', 'new_v_b_ln_b', 'new_v_ab_w_out', 'new_v_c_w_in', 'new_v_c_conv_w', 'new_v_c_w_out', 'new_v_f_w_up', 'new_v_f_conv_w', 'new_v_f_w_down']
TWIN_LEAF_KINDS = {'loss': 'loss', 'grad_x': 'grad_x', 'grad_norm_mix': 'grad_w', 'grad_norm_ffn': 'grad_w', 'grad_norm_final': 'grad_w', 'grad_ab_w_in': 'grad_w', 'grad_a_ln_g': 'grad_w', 'grad_a_ln_b': 'grad_w', 'grad_a_w_s': 'grad_w', 'grad_a_b_s': 'grad_w', 'grad_b_conv_w': 'grad_w', 'grad_b_conv_b': 'grad_w', 'grad_b_ln_g': 'grad_w', 'grad_b_ln_b': 'grad_w', 'grad_ab_w_out': 'grad_w', 'grad_c_w_in': 'grad_w', 'grad_c_conv_w': 'grad_w', 'grad_c_w_out': 'grad_w', 'grad_f_w_up': 'grad_w', 'grad_f_conv_w': 'grad_w', 'grad_f_w_down': 'grad_w', 'delta_norm_mix': 'delta_w', 'delta_norm_ffn': 'delta_w', 'delta_norm_final': 'delta_w', 'delta_ab_w_in': 'delta_w', 'delta_a_ln_g': 'delta_w', 'delta_a_ln_b': 'delta_w', 'delta_a_w_s': 'delta_w', 'delta_a_b_s': 'delta_w', 'delta_b_conv_w': 'delta_w', 'delta_b_conv_b': 'delta_w', 'delta_b_ln_g': 'delta_w', 'delta_b_ln_b': 'delta_w', 'delta_ab_w_out': 'delta_w', 'delta_c_w_in': 'delta_w', 'delta_c_conv_w': 'delta_w', 'delta_c_w_out': 'delta_w', 'delta_f_w_up': 'delta_w', 'delta_f_conv_w': 'delta_w', 'delta_f_w_down': 'delta_w', 'new_m_norm_mix': 'new_m', 'new_m_norm_ffn': 'new_m', 'new_m_norm_final': 'new_m', 'new_m_ab_w_in': 'new_m', 'new_m_a_ln_g': 'new_m', 'new_m_a_ln_b': 'new_m', 'new_m_a_w_s': 'new_m', 'new_m_a_b_s': 'new_m', 'new_m_b_conv_w': 'new_m', 'new_m_b_conv_b': 'new_m', 'new_m_b_ln_g': 'new_m', 'new_m_b_ln_b': 'new_m', 'new_m_ab_w_out': 'new_m', 'new_m_c_w_in': 'new_m', 'new_m_c_conv_w': 'new_m', 'new_m_c_w_out': 'new_m', 'new_m_f_w_up': 'new_m', 'new_m_f_conv_w': 'new_m', 'new_m_f_w_down': 'new_m', 'new_v_norm_mix': 'new_v', 'new_v_norm_ffn': 'new_v', 'new_v_norm_final': 'new_v', 'new_v_ab_w_in': 'new_v', 'new_v_a_ln_g': 'new_v', 'new_v_a_ln_b': 'new_v', 'new_v_a_w_s': 'new_v', 'new_v_a_b_s': 'new_v', 'new_v_b_conv_w': 'new_v', 'new_v_b_conv_b': 'new_v', 'new_v_b_ln_g': 'new_v', 'new_v_b_ln_b': 'new_v', 'new_v_ab_w_out': 'new_v', 'new_v_c_w_in': 'new_v', 'new_v_c_conv_w': 'new_v', 'new_v_c_w_out': 'new_v', 'new_v_f_w_up': 'new_v', 'new_v_f_conv_w': 'new_v', 'new_v_f_w_down': 'new_v'}


def _forward(args):
    return _fwd_reference(*[args[k] for k in FWD_PARAMS])


def _output_shape():
    def fwd():
        inp = _fwd_setup_inputs(0)
        return _fwd_reference(*[inp[k] for k in FWD_PARAMS])
    out = _jax.eval_shape(fwd)
    return out.shape, out.dtype

N_MICROBATCH = 1
ADAM_LR = 0.001
ADAM_B1 = 0.9
ADAM_B2 = 0.999
ADAM_EPS = 1e-08
ADAM_WD = 0.01
ADAM_STEP = 10
PER_EXAMPLE_BATCH_AXIS = {'x': 0, 'loss_target': 0}
SHARED_INPUTS = []
_WEIGHT_DTYPES = {'norm_mix': _jnp.float32, 'norm_ffn': _jnp.float32, 'norm_final': _jnp.float32, 'ab_w_in': _jnp.float32, 'a_ln_g': _jnp.float32, 'a_ln_b': _jnp.float32, 'a_w_s': _jnp.float32, 'a_b_s': _jnp.float32, 'b_conv_w': _jnp.float32, 'b_conv_b': _jnp.float32, 'b_ln_g': _jnp.float32, 'b_ln_b': _jnp.float32, 'ab_w_out': _jnp.float32, 'c_w_in': _jnp.float32, 'c_conv_w': _jnp.float32, 'c_w_out': _jnp.float32, 'f_w_up': _jnp.float32, 'f_conv_w': _jnp.float32, 'f_w_down': _jnp.float32}
MOMENT_SCALE = {'norm_mix': 1.898258e-01, 'norm_ffn': 1.263863e-01, 'norm_final': 3.199297e+01, 'ab_w_in': 1.300047e-01, 'a_ln_g': 1.004669e-01, 'a_ln_b': 1.123782e-01, 'a_w_s': 9.937840e-02, 'a_b_s': 1.454300e-01, 'b_conv_w': 1.362496e-01, 'b_conv_b': 2.894476e-01, 'b_ln_g': 1.588723e-01, 'b_ln_b': 1.540432e-01, 'ab_w_out': 1.559634e-01, 'c_w_in': 1.115113e-01, 'c_conv_w': 1.107553e-01, 'c_w_out': 1.115303e-01, 'f_w_up': 5.304850e-02, 'f_conv_w': 5.389155e-02, 'f_w_down': 8.697904e-02}


def _to_microbatches(a, axis):
    t = _jnp.moveaxis(a, axis, 0)
    t = t.reshape((N_MICROBATCH, t.shape[0] // N_MICROBATCH) + t.shape[1:])
    return _jnp.moveaxis(t, 1, axis + 1)


def setup_inputs(seed: int = 0) -> dict:
    inp = _fwd_setup_inputs(seed)
    key = _jax.random.fold_in(_jax.random.key(seed), 7919)
    shape, _ = _output_shape()
    out = dict(inp)
    out["loss_target"] = _jax.random.normal(_jax.random.fold_in(key, 0), shape, _jnp.float32)
    for i, name in enumerate(TWIN_WEIGHTS):
        w = inp[name].astype(_jnp.float32)
        if MOMENT_SCALE is None:
            s = _jnp.sqrt(_jnp.mean(_jnp.square(w)) + 1e-30)
        else:
            s = MOMENT_SCALE[name]
        km, kv = _jax.random.split(_jax.random.fold_in(key, i + 1))
        out[name] = w
        out["m_" + name] = s * _jax.random.normal(km, w.shape, _jnp.float32)
        out["v_" + name] = (s * s) * _jax.random.uniform(kv, w.shape, _jnp.float32, 0.5, 1.5)
    if N_MICROBATCH > 1:
        for name, axis in PER_EXAMPLE_BATCH_AXIS.items():
            out[name] = _to_microbatches(out[name], axis)
    return {'x': out['x'], 'norm_mix': out['norm_mix'], 'norm_ffn': out['norm_ffn'], 'norm_final': out['norm_final'], 'ab_w_in': out['ab_w_in'], 'a_ln_g': out['a_ln_g'], 'a_ln_b': out['a_ln_b'], 'a_w_s': out['a_w_s'], 'a_b_s': out['a_b_s'], 'b_conv_w': out['b_conv_w'], 'b_conv_b': out['b_conv_b'], 'b_ln_g': out['b_ln_g'], 'b_ln_b': out['b_ln_b'], 'ab_w_out': out['ab_w_out'], 'c_w_in': out['c_w_in'], 'c_conv_w': out['c_conv_w'], 'c_w_out': out['c_w_out'], 'f_w_up': out['f_w_up'], 'f_conv_w': out['f_conv_w'], 'f_w_down': out['f_w_down'], 'loss_target': out['loss_target'], 'm_norm_mix': out['m_norm_mix'], 'm_norm_ffn': out['m_norm_ffn'], 'm_norm_final': out['m_norm_final'], 'm_ab_w_in': out['m_ab_w_in'], 'm_a_ln_g': out['m_a_ln_g'], 'm_a_ln_b': out['m_a_ln_b'], 'm_a_w_s': out['m_a_w_s'], 'm_a_b_s': out['m_a_b_s'], 'm_b_conv_w': out['m_b_conv_w'], 'm_b_conv_b': out['m_b_conv_b'], 'm_b_ln_g': out['m_b_ln_g'], 'm_b_ln_b': out['m_b_ln_b'], 'm_ab_w_out': out['m_ab_w_out'], 'm_c_w_in': out['m_c_w_in'], 'm_c_conv_w': out['m_c_conv_w'], 'm_c_w_out': out['m_c_w_out'], 'm_f_w_up': out['m_f_w_up'], 'm_f_conv_w': out['m_f_conv_w'], 'm_f_w_down': out['m_f_w_down'], 'v_norm_mix': out['v_norm_mix'], 'v_norm_ffn': out['v_norm_ffn'], 'v_norm_final': out['v_norm_final'], 'v_ab_w_in': out['v_ab_w_in'], 'v_a_ln_g': out['v_a_ln_g'], 'v_a_ln_b': out['v_a_ln_b'], 'v_a_w_s': out['v_a_w_s'], 'v_a_b_s': out['v_a_b_s'], 'v_b_conv_w': out['v_b_conv_w'], 'v_b_conv_b': out['v_b_conv_b'], 'v_b_ln_g': out['v_b_ln_g'], 'v_b_ln_b': out['v_b_ln_b'], 'v_ab_w_out': out['v_ab_w_out'], 'v_c_w_in': out['v_c_w_in'], 'v_c_conv_w': out['v_c_conv_w'], 'v_c_w_out': out['v_c_w_out'], 'v_f_w_up': out['v_f_w_up'], 'v_f_conv_w': out['v_f_conv_w'], 'v_f_w_down': out['v_f_w_down']}


def _loss(weights, diff, rest, loss_target):
    with _jax.named_scope("forward"):
        args = {**rest, TWIN_DIFF_INPUT: diff, **{k: w.astype(_WEIGHT_DTYPES[k]) for k, w in weights.items()}}
        y = _forward(args)
    with _jax.named_scope("loss_head"):
        err = _jnp.square(y.astype(_jnp.float32) - loss_target)
        return 0.5 * _jnp.sum(_jnp.mean(err, axis=-1)) if err.ndim else 0.5 * err


def _adamw(w, g, m, v):
    m = ADAM_B1 * m + (1.0 - ADAM_B1) * g
    v = ADAM_B2 * v + (1.0 - ADAM_B2) * _jnp.square(g)
    m_hat = m / (1.0 - ADAM_B1 ** ADAM_STEP)
    v_hat = v / (1.0 - ADAM_B2 ** ADAM_STEP)
    delta = -ADAM_LR * (m_hat / (_jnp.sqrt(v_hat) + ADAM_EPS) + ADAM_WD * w)
    return delta, m, v


def reference(x, norm_mix, norm_ffn, norm_final, ab_w_in, a_ln_g, a_ln_b, a_w_s, a_b_s, b_conv_w, b_conv_b, b_ln_g, b_ln_b, ab_w_out, c_w_in, c_conv_w, c_w_out, f_w_up, f_conv_w, f_w_down, loss_target, m_norm_mix, m_norm_ffn, m_norm_final, m_ab_w_in, m_a_ln_g, m_a_ln_b, m_a_w_s, m_a_b_s, m_b_conv_w, m_b_conv_b, m_b_ln_g, m_b_ln_b, m_ab_w_out, m_c_w_in, m_c_conv_w, m_c_w_out, m_f_w_up, m_f_conv_w, m_f_w_down, v_norm_mix, v_norm_ffn, v_norm_final, v_ab_w_in, v_a_ln_g, v_a_ln_b, v_a_w_s, v_a_b_s, v_b_conv_w, v_b_conv_b, v_b_ln_g, v_b_ln_b, v_ab_w_out, v_c_w_in, v_c_conv_w, v_c_w_out, v_f_w_up, v_f_conv_w, v_f_w_down):
    given = dict(x=x, norm_mix=norm_mix, norm_ffn=norm_ffn, norm_final=norm_final, ab_w_in=ab_w_in, a_ln_g=a_ln_g, a_ln_b=a_ln_b, a_w_s=a_w_s, a_b_s=a_b_s, b_conv_w=b_conv_w, b_conv_b=b_conv_b, b_ln_g=b_ln_g, b_ln_b=b_ln_b, ab_w_out=ab_w_out, c_w_in=c_w_in, c_conv_w=c_conv_w, c_w_out=c_w_out, f_w_up=f_w_up, f_conv_w=f_conv_w, f_w_down=f_w_down, loss_target=loss_target, m_norm_mix=m_norm_mix, m_norm_ffn=m_norm_ffn, m_norm_final=m_norm_final, m_ab_w_in=m_ab_w_in, m_a_ln_g=m_a_ln_g, m_a_ln_b=m_a_ln_b, m_a_w_s=m_a_w_s, m_a_b_s=m_a_b_s, m_b_conv_w=m_b_conv_w, m_b_conv_b=m_b_conv_b, m_b_ln_g=m_b_ln_g, m_b_ln_b=m_b_ln_b, m_ab_w_out=m_ab_w_out, m_c_w_in=m_c_w_in, m_c_conv_w=m_c_conv_w, m_c_w_out=m_c_w_out, m_f_w_up=m_f_w_up, m_f_conv_w=m_f_conv_w, m_f_w_down=m_f_w_down, v_norm_mix=v_norm_mix, v_norm_ffn=v_norm_ffn, v_norm_final=v_norm_final, v_ab_w_in=v_ab_w_in, v_a_ln_g=v_a_ln_g, v_a_ln_b=v_a_ln_b, v_a_w_s=v_a_w_s, v_a_b_s=v_a_b_s, v_b_conv_w=v_b_conv_w, v_b_conv_b=v_b_conv_b, v_b_ln_g=v_b_ln_g, v_b_ln_b=v_b_ln_b, v_ab_w_out=v_ab_w_out, v_c_w_in=v_c_w_in, v_c_conv_w=v_c_conv_w, v_c_w_out=v_c_w_out, v_f_w_up=v_f_w_up, v_f_conv_w=v_f_conv_w, v_f_w_down=v_f_w_down)
    weights = {n: given[n] for n in TWIN_WEIGHTS}
    shared = {n: given[n] for n in SHARED_INPUTS}
    per_example = {n: given[n] for n in ['x']}
    grad_fn = _jax.value_and_grad(_loss, argnums=(0, 1))

    def one_microbatch(ex, loss_target):
        ex = dict(ex)
        diff = ex.pop(TWIN_DIFF_INPUT)
        return grad_fn(weights, diff, {**shared, **ex}, loss_target)

    if N_MICROBATCH == 1:
        loss, (grad_w, grad_x) = one_microbatch(per_example, given["loss_target"])
    else:
        def body(carry, xs):
            loss_sum, grad_sum = carry
            l_k, (gw_k, gx_k) = one_microbatch(xs[0], xs[1])
            with _jax.named_scope("update"):
                return (loss_sum + l_k, _jax.tree.map(_jnp.add, grad_sum, gw_k)), gx_k

        init = (_jnp.zeros((), _jnp.float32), _jax.tree.map(_jnp.zeros_like, weights))
        (loss, grad_w), grad_x = _jax.lax.scan(body, init, (per_example, given["loss_target"]))
    with _jax.named_scope("update"):
        delta_w, new_m, new_v = {}, {}, {}
        for n in TWIN_WEIGHTS:
            delta_w[n], new_m[n], new_v[n] = _adamw(weights[n], grad_w[n], given["m_" + n], given["v_" + n])
    return (loss, grad_x, *[grad_w[n] for n in TWIN_WEIGHTS], *[delta_w[n] for n in TWIN_WEIGHTS],
            *[new_m[n] for n in TWIN_WEIGHTS], *[new_v[n] for n in TWIN_WEIGHTS])
```

```python
import functools
import math

import jax
import jax.numpy as jnp
from jax import lax
from jax.experimental import pallas as pl
from jax.experimental.pallas import tpu as pltpu

F32 = jnp.float32
BF16 = jnp.bfloat16

D = 1024
DA = 512
HEADS = 4
CHUNK = 128
DFF = 2816
NDEV = 8
NCHIP = 4
FB = DFF * 2 // NDEV
NG = DFF // FB
BCONV = 31
EPS = 1e-6
HALO = 16
HALO_B = 32
VMEM_LIMIT = 52 * 1024 * 1024
INV_SQRT2 = 1.0 / math.sqrt(2.0)
INV_SQRT_2PI = 1.0 / math.sqrt(2.0 * math.pi)

ADAM_LR = 0.001
ADAM_B1 = 0.9
ADAM_B2 = 0.999
ADAM_EPS = 1e-08
ADAM_WD = 0.01
ADAM_STEP = 10

MESH = pl.DeviceIdType.MESH
ANY = pl.BlockSpec(memory_space=pl.ANY)
NT_DIMS = (((1,), (1,)), ((), ()))
TN_DIMS = (((0,), (0,)), ((), ()))


def _params(*sem):
    return pltpu.CompilerParams(dimension_semantics=sem, vmem_limit_bytes=VMEM_LIMIT)


def _tile(s, want):
    return min(want, s)


def _sigmoid(x):
    return jax.nn.sigmoid(x)


def _dsilu(x, sg):
    return sg * (1.0 + x * (1.0 - sg))


def _gelu(x):
    return 0.5 * x * (1.0 + lax.erf(x * INV_SQRT2))


def _dgelu(x):
    return 0.5 * (1.0 + lax.erf(x * INV_SQRT2)) + x * jnp.exp(-0.5 * x * x) * INV_SQRT_2PI


def _ln_fwd(x, g, b):
    mu = jnp.mean(x, axis=-1, keepdims=True)
    xc = x - mu
    var = jnp.mean(xc * xc, axis=-1, keepdims=True)
    rstd = lax.rsqrt(var + EPS)
    xhat = xc * rstd
    return xhat * g + b, xhat, rstd


def _ln_bwd(dy, xhat, rstd, g):
    dxh = dy * g
    m1 = jnp.mean(dxh, axis=-1, keepdims=True)
    m2 = jnp.mean(dxh * xhat, axis=-1, keepdims=True)
    return rstd * (dxh - m1 - xhat * m2)


def _rms_bwd_math(dh, x, g):
    r = lax.rsqrt(jnp.mean(x * x, axis=-1, keepdims=True) + EPS)
    xhat = x * r
    dg = jnp.sum(dh * xhat, axis=0, keepdims=True)
    u = dh * g
    dx = r * (u - xhat * jnp.mean(u * xhat, axis=-1, keepdims=True))
    return dx, dg


def _conv3(xe, cw, halo):
    x0 = xe[halo:]
    x1 = pltpu.roll(xe, 1, 0)[halo:]
    x2 = pltpu.roll(xe, 2, 0)[halo:]
    return cw[2] * x0 + cw[1] * x1 + cw[0] * x2, (x0, x1, x2)


def _conv3_bwd_in(dce, cw, ts):
    n = dce.shape[0]
    d1 = pltpu.roll(dce, n - 1, 0)[:ts]
    d2 = pltpu.roll(dce, n - 2, 0)[:ts]
    return cw[2] * dce[:ts] + cw[1] * d1 + cw[0] * d2


def _conv3_bwd_w(dc, taps):
    x0, x1, x2 = taps
    return [jnp.sum(dc * x2, axis=0, keepdims=True), jnp.sum(dc * x1, axis=0, keepdims=True),
            jnp.sum(dc * x0, axis=0, keepdims=True)]


def _rms_fwd(x, g, name):
    s = x.shape[0]
    ts = _tile(s, 512)

    def body(x_ref, g_ref, h_ref):
        xv = x_ref[...]
        r = lax.rsqrt(jnp.mean(xv * xv, axis=-1, keepdims=True) + EPS)
        h_ref[...] = (xv * r * g_ref[...]).astype(BF16)

    return pl.pallas_call(
        body, grid=(s // ts,), name=name,
        in_specs=[pl.BlockSpec((ts, D), lambda i: (i, 0)), pl.BlockSpec((1, D), lambda i: (0, 0))],
        out_specs=pl.BlockSpec((ts, D), lambda i: (i, 0)),
        out_shape=jax.ShapeDtypeStruct((s, D), BF16),
        compiler_params=_params("parallel"),
    )(x, g)


def _mm_in(h, wblk, name):
    s = h.shape[0]
    nb, _, bn = wblk.shape
    ts = _tile(s, 512)

    def body(h_ref, w_ref, o_ref):
        hv = h_ref[...]
        for b in range(nb):
            o_ref[:, b * bn:(b + 1) * bn] = jnp.dot(hv, w_ref[b], preferred_element_type=F32).astype(BF16)

    return pl.pallas_call(
        body, grid=(s // ts,), name=name,
        in_specs=[pl.BlockSpec((ts, D), lambda i: (i, 0)), pl.BlockSpec((nb, D, bn), lambda i: (0, 0, 0))],
        out_specs=pl.BlockSpec((ts, nb * bn), lambda i: (i, 0)),
        out_shape=jax.ShapeDtypeStruct((s, nb * bn), BF16),
        compiler_params=_params("parallel"),
    )(h, wblk)


def _mm_out(y, w, xres, name):
    s = y.shape[0]
    ts = _tile(s, 512)

    def body(y_ref, w_ref, x_ref, o_ref):
        o_ref[...] = x_ref[...] + jnp.dot(y_ref[...], w_ref[...], preferred_element_type=F32)

    return pl.pallas_call(
        body, grid=(s // ts,), name=name,
        in_specs=[pl.BlockSpec((ts, D), lambda i: (i, 0)), pl.BlockSpec((D, D), lambda i: (0, 0)),
                  pl.BlockSpec((ts, D), lambda i: (i, 0))],
        out_specs=pl.BlockSpec((ts, D), lambda i: (i, 0)),
        out_shape=jax.ShapeDtypeStruct((s, D), F32),
        compiler_params=_params("parallel"),
    )(y, w, xres)


def _conv31(ue, cw_ref, ts):
    acc = jnp.zeros((ts, ue.shape[1]), F32)
    for r in range(8):
        rolled = ue if r == 0 else pltpu.roll(ue, r, 0)
        for q in range(4):
            sh = 8 * q + r
            if sh >= BCONV:
                continue
            k = BCONV - 1 - sh
            acc = acc + cw_ref[k:k + 1, :] * rolled[HALO_B - 8 * q:HALO_B - 8 * q + ts]
    return acc


def _ab_fwd(z, lga, lba, wsm, bs_col, cwb, cbb, lgb, lbb, name):
    s = z.shape[0]
    ts = _tile(s, 256)
    hb = ts // HALO_B

    def body(z_ref, zh_ref, lga_ref, lba_ref, ws_ref, bs_ref, cw_ref, cb_ref, lgb_ref, lbb_ref,
             y_ref, yb2_ref):
        i = pl.program_id(0)
        z_t = z_ref[...].astype(F32)
        gu = _gelu(z_t[:, 0:DA])
        gv = _gelu(z_t[:, DA:2 * DA])
        vn, _, _ = _ln_fwd(gv, lga_ref[...], lba_ref[...])
        vnb = vn.astype(BF16)
        for c in range(ts // CHUNK):
            for h in range(HEADS):
                rs = slice(c * CHUNK, (c + 1) * CHUNK)
                cs = slice(h * CHUNK, (h + 1) * CHUNK)
                mixed = jnp.dot(ws_ref[h], vnb[rs, cs], preferred_element_type=F32) + bs_ref[h]
                y_ref[rs, cs] = (gu[rs, cs] * mixed).astype(BF16)
        zh = jnp.where(i > 0, zh_ref[...], jnp.zeros_like(zh_ref[...])).astype(F32)
        xb = jnp.concatenate([zh[:, 0:DA], z_t[:, 2 * DA:3 * DA]], axis=0)
        gb = jnp.concatenate([zh[:, DA:2 * DA], z_t[:, 3 * DA:4 * DA]], axis=0)
        u = xb * _sigmoid(gb)
        conv = _conv31(u, cw_ref, ts) + cb_ref[...]
        yb2_ref[...] = conv
        nb_, _, _ = _ln_fwd(conv, lgb_ref[...], lbb_ref[...])
        y_ref[:, DA:2 * DA] = (nb_ * _sigmoid(nb_)).astype(BF16)

    row = lambda i: (0, 0)
    return pl.pallas_call(
        body, grid=(s // ts,), name=name,
        in_specs=[pl.BlockSpec((ts, 4 * DA), lambda i: (i, 0)),
                  pl.BlockSpec((HALO_B, 2 * DA), lambda i: (jnp.maximum(i * hb - 1, 0), 1)),
                  pl.BlockSpec((1, DA), row), pl.BlockSpec((1, DA), row),
                  pl.BlockSpec((HEADS, CHUNK, CHUNK), lambda i: (0, 0, 0)),
                  pl.BlockSpec((HEADS, CHUNK, 1), lambda i: (0, 0, 0)),
                  pl.BlockSpec((BCONV, DA), row), pl.BlockSpec((1, DA), row),
                  pl.BlockSpec((1, DA), row), pl.BlockSpec((1, DA), row)],
        out_specs=[pl.BlockSpec((ts, 2 * DA), lambda i: (i, 0)), pl.BlockSpec((ts, DA), lambda i: (i, 0))],
        out_shape=[jax.ShapeDtypeStruct((s, 2 * DA), BF16), jax.ShapeDtypeStruct((s, DA), F32)],
        compiler_params=_params("parallel"),
    )(z, z, lga, lba, wsm, bs_col, cwb, cbb, lgb, lbb)


def _c_fwd(zc, cw, name):
    s = zc.shape[0]
    ts = _tile(s, 512)
    hb = ts // HALO

    def body(z_ref, ch_ref, xh_ref, cw_ref, r_ref):
        i = pl.program_id(0)
        z_t = z_ref[...].astype(F32)
        ph = jnp.where(i > 0, ch_ref[...].astype(F32) * xh_ref[...].astype(F32), 0.0)
        pe = jnp.concatenate([ph, z_t[:, D:2 * D] * z_t[:, 2 * D:3 * D]], axis=0)
        q, _ = _conv3(pe, [cw_ref[k:k + 1, :] for k in range(3)], HALO)
        r_ref[...] = (z_t[:, 0:D] * q).astype(BF16)

    halo = lambda col: pl.BlockSpec((HALO, D), lambda i: (jnp.maximum(i * hb - 1, 0), col))
    return pl.pallas_call(
        body, grid=(s // ts,), name=name,
        in_specs=[pl.BlockSpec((ts, 3 * D), lambda i: (i, 0)), halo(1), halo(2),
                  pl.BlockSpec((3, D), lambda i: (0, 0))],
        out_specs=pl.BlockSpec((ts, D), lambda i: (i, 0)),
        out_shape=jax.ShapeDtypeStruct((s, D), BF16),
        compiler_params=_params("parallel"),
    )(zc, zc, zc, cw)


def _ffn_fwd(h, xres, wup, fcw, wdn, name):
    s = h.shape[0]
    ts = _tile(s, 512)
    hb = ts // HALO

    def body(h_ref, hh_ref, w_ref, cw_ref, wd_ref, x_ref, up_ref, xo_ref):
        i = pl.program_id(0)
        m = pl.program_id(1)
        halo = jnp.where(i > 0, hh_ref[...], jnp.zeros_like(hh_ref[...]))
        hx = jnp.concatenate([halo, h_ref[...]], axis=0)
        acts = []
        for gv in range(2):
            up = jnp.dot(hx, w_ref[gv], preferred_element_type=F32)
            up_ref[gv] = up[HALO:].astype(BF16)
            upc, _ = _conv3(up, [cw_ref[gv, k:k + 1, :] for k in range(3)], HALO)
            acts.append(upc)
        a = acts[0] * _sigmoid(acts[0]) * acts[1]
        f = jnp.dot(a.astype(BF16), wd_ref[...], preferred_element_type=F32)

        @pl.when(m == 0)
        def _():
            xo_ref[...] = x_ref[...] + f

        @pl.when(m > 0)
        def _():
            xo_ref[...] += f

    return pl.pallas_call(
        body, grid=(s // ts, NG), name=name,
        in_specs=[pl.BlockSpec((ts, D), lambda i, m: (i, 0)),
                  pl.BlockSpec((HALO, D), lambda i, m: (jnp.maximum(i * hb - 1, 0), 0)),
                  pl.BlockSpec((2, None, D, FB), lambda i, m: (0, m, 0, 0)),
                  pl.BlockSpec((2, None, 3, FB), lambda i, m: (0, m, 0, 0)),
                  pl.BlockSpec((FB, D), lambda i, m: (m, 0)),
                  pl.BlockSpec((ts, D), lambda i, m: (i, 0))],
        out_specs=[pl.BlockSpec((None, 2, ts, FB), lambda i, m: (m, 0, i, 0)),
                   pl.BlockSpec((ts, D), lambda i, m: (i, 0))],
        out_shape=[jax.ShapeDtypeStruct((NG, 2, s, FB), BF16), jax.ShapeDtypeStruct((s, D), F32)],
        compiler_params=_params("arbitrary", "arbitrary"),
    )(h, h, wup, fcw, wdn, xres)


def _final(x, tgt, g, name):
    s = x.shape[0]
    ts = _tile(s, 512)

    def body(x_ref, t_ref, g_ref, dx_ref, dg_ref, loss_ref):
        i = pl.program_id(0)
        xv = x_ref[...]
        gv = g_ref[...]
        r = lax.rsqrt(jnp.mean(xv * xv, axis=-1, keepdims=True) + EPS)
        xhat = xv * r
        e = xhat * gv - t_ref[...]
        part = 0.5 * jnp.sum(jnp.mean(e * e, axis=-1, keepdims=True), axis=0, keepdims=True)
        dy = e * (1.0 / D)
        dgp = jnp.sum(dy * xhat, axis=0, keepdims=True)
        u = dy * gv
        dx_ref[...] = r * (u - xhat * jnp.mean(u * xhat, axis=-1, keepdims=True))

        @pl.when(i == 0)
        def _():
            dg_ref[...] = dgp
            loss_ref[...] = jnp.broadcast_to(part, (1, 128))

        @pl.when(i > 0)
        def _():
            dg_ref[...] += dgp
            loss_ref[...] += jnp.broadcast_to(part, (1, 128))

    return pl.pallas_call(
        body, grid=(s // ts,), name=name,
        in_specs=[pl.BlockSpec((ts, D), lambda i: (i, 0)), pl.BlockSpec((ts, D), lambda i: (i, 0)),
                  pl.BlockSpec((1, D), lambda i: (0, 0))],
        out_specs=[pl.BlockSpec((ts, D), lambda i: (i, 0)), pl.BlockSpec((1, D), lambda i: (0, 0)),
                   pl.BlockSpec((1, 128), lambda i: (0, 0))],
        out_shape=[jax.ShapeDtypeStruct((s, D), F32), jax.ShapeDtypeStruct((1, D), F32),
                   jax.ShapeDtypeStruct((1, 128), F32)],
        compiler_params=_params("arbitrary"),
    )(x, tgt, g)


def _ffn_bwd(df, up, wup, fcw, wdn, xin, g, name):
    s = df.shape[0]
    ts = _tile(s, 512)
    nt = s // ts
    hb = ts // HALO

    def body(df_ref, up_ref, uph_ref, w_ref, cw_ref, wd_ref, x_ref, g_ref,
             a_ref, dup_ref, dx_ref, dg_ref, dcw_ref, carry, acc):
        i = pl.program_id(0)
        m = pl.program_id(1)
        ti = nt - 1 - i
        first = i == 0
        dfb = df_ref[...].astype(BF16)
        cws = [[cw_ref[gv, k:k + 1, :] for k in range(3)] for gv in range(2)]
        upc, taps = [], []
        for gv in range(2):
            halo = jnp.where(ti > 0, uph_ref[gv], jnp.zeros_like(uph_ref[gv]))
            upe = jnp.concatenate([halo, up_ref[gv]], axis=0).astype(F32)
            c, t = _conv3(upe, cws[gv], HALO)
            upc.append(c)
            taps.append(t)
        sg = _sigmoid(upc[0])
        sl = upc[0] * sg
        a_ref[...] = (sl * upc[1]).astype(BF16)
        da = lax.dot_general(dfb, wd_ref[...], NT_DIMS, preferred_element_type=F32)
        dcs = [da * upc[1] * _dsilu(upc[0], sg), da * sl]

        @pl.when(first)
        def _():
            carry[m] = jnp.zeros((2, 8, FB), F32)

        dh = jnp.zeros((ts, D), F32)
        for gv in range(2):
            dc = dcs[gv]
            rows = _conv3_bwd_w(dc, taps[gv])

            @pl.when(first)
            def _():
                for k in range(3):
                    dcw_ref[m, gv, k:k + 1, :] = rows[k]

            @pl.when(jnp.logical_not(first))
            def _():
                for k in range(3):
                    dcw_ref[m, gv, k:k + 1, :] += rows[k]

            dce = jnp.concatenate([dc, carry[m, gv]], axis=0)
            du = _conv3_bwd_in(dce, cws[gv], ts)
            carry[m, gv] = dc[0:8]
            dub = du.astype(BF16)
            dup_ref[gv] = dub
            dh = dh + lax.dot_general(dub, w_ref[gv], NT_DIMS, preferred_element_type=F32)

        @pl.when(m == 0)
        def _():
            acc[...] = dh

        @pl.when(m > 0)
        def _():
            acc[...] += dh

        @pl.when(m == NG - 1)
        def _():
            dx, dgp = _rms_bwd_math(acc[...], x_ref[...], g_ref[...])
            dx_ref[...] = df_ref[...] + dx

            @pl.when(first)
            def _():
                dg_ref[...] = dgp

            @pl.when(jnp.logical_not(first))
            def _():
                dg_ref[...] += dgp

    rev = lambda i: nt - 1 - i
    return pl.pallas_call(
        body, grid=(nt, NG), name=name,
        in_specs=[pl.BlockSpec((ts, D), lambda i, m: (rev(i), 0)),
                  pl.BlockSpec((None, 2, ts, FB), lambda i, m: (m, 0, rev(i), 0)),
                  pl.BlockSpec((None, 2, HALO, FB), lambda i, m: (m, 0, jnp.maximum(rev(i) * hb - 1, 0), 0)),
                  pl.BlockSpec((2, None, D, FB), lambda i, m: (0, m, 0, 0)),
                  pl.BlockSpec((2, None, 3, FB), lambda i, m: (0, m, 0, 0)),
                  pl.BlockSpec((FB, D), lambda i, m: (m, 0)),
                  pl.BlockSpec((ts, D), lambda i, m: (rev(i), 0)),
                  pl.BlockSpec((1, D), lambda i, m: (0, 0))],
        out_specs=[pl.BlockSpec((None, ts, FB), lambda i, m: (m, rev(i), 0)),
                   pl.BlockSpec((None, 2, ts, FB), lambda i, m: (m, 0, rev(i), 0)),
                   pl.BlockSpec((ts, D), lambda i, m: (rev(i), 0)),
                   pl.BlockSpec((1, D), lambda i, m: (0, 0)),
                   pl.BlockSpec((NG, 2, 3, FB), lambda i, m: (0, 0, 0, 0))],
        out_shape=[jax.ShapeDtypeStruct((NG, s, FB), BF16), jax.ShapeDtypeStruct((NG, 2, s, FB), BF16),
                   jax.ShapeDtypeStruct((s, D), F32), jax.ShapeDtypeStruct((1, D), F32),
                   jax.ShapeDtypeStruct((NG, 2, 3, FB), F32)],
        scratch_shapes=[pltpu.VMEM((NG, 2, 8, FB), F32), pltpu.VMEM((ts, D), F32)],
        compiler_params=_params("arbitrary", "arbitrary"),
    )(df, up, up, wup, fcw, wdn, xin, g)


def _mm_nt(dy, w, name):
    s = dy.shape[0]
    ts = _tile(s, 512)

    def body(dy_ref, w_ref, o_ref):
        o_ref[...] = lax.dot_general(dy_ref[...].astype(BF16), w_ref[...], NT_DIMS,
                                     preferred_element_type=F32).astype(BF16)

    return pl.pallas_call(
        body, grid=(s // ts,), name=name,
        in_specs=[pl.BlockSpec((ts, D), lambda i: (i, 0)), pl.BlockSpec((D, D), lambda i: (0, 0))],
        out_specs=pl.BlockSpec((ts, D), lambda i: (i, 0)),
        out_shape=jax.ShapeDtypeStruct((s, D), BF16),
        compiler_params=_params("parallel"),
    )(dy, w)


def _mm_nt_rms(dy, wblk, x, g, dres, name):
    s = dy.shape[0]
    nb, _, bn = wblk.shape
    ts = _tile(s, 512)

    def body(dy_ref, w_ref, x_ref, g_ref, dr_ref, dx_ref, dg_ref):
        i = pl.program_id(0)
        acc = jnp.zeros((ts, D), F32)
        for b in range(nb):
            acc = acc + lax.dot_general(dy_ref[:, b * bn:(b + 1) * bn], w_ref[b], NT_DIMS,
                                        preferred_element_type=F32)
        dx, dgp = _rms_bwd_math(acc, x_ref[...], g_ref[...])
        dx_ref[...] = dr_ref[...] + dx

        @pl.when(i == 0)
        def _():
            dg_ref[...] = dgp

        @pl.when(i > 0)
        def _():
            dg_ref[...] += dgp

    return pl.pallas_call(
        body, grid=(s // ts,), name=name,
        in_specs=[pl.BlockSpec((ts, nb * bn), lambda i: (i, 0)), pl.BlockSpec((nb, D, bn), lambda i: (0, 0, 0)),
                  pl.BlockSpec((ts, D), lambda i: (i, 0)), pl.BlockSpec((1, D), lambda i: (0, 0)),
                  pl.BlockSpec((ts, D), lambda i: (i, 0))],
        out_specs=[pl.BlockSpec((ts, D), lambda i: (i, 0)), pl.BlockSpec((1, D), lambda i: (0, 0))],
        out_shape=[jax.ShapeDtypeStruct((s, D), F32), jax.ShapeDtypeStruct((1, D), F32)],
        compiler_params=_params("arbitrary"),
    )(dy, wblk, x, g, dres)


def _c_bwd(dr, zc, cw, name):
    s = dr.shape[0]
    ts = _tile(s, 512)
    nt = s // ts
    hb = ts // HALO

    def body(dr_ref, drf_ref, z_ref, ch_ref, xh_ref, bf_ref, cw_ref, dz_ref, dcw_ref):
        i = pl.program_id(0)
        cwv = [cw_ref[k:k + 1, :] for k in range(3)]
        z_t = z_ref[...].astype(F32)
        bg, cg, xv = z_t[:, 0:D], z_t[:, D:2 * D], z_t[:, 2 * D:3 * D]
        ph = jnp.where(i > 0, ch_ref[...].astype(F32) * xh_ref[...].astype(F32), 0.0)
        pe = jnp.concatenate([ph, cg * xv], axis=0)
        q, taps = _conv3(pe, cwv, HALO)
        drv = dr_ref[...].astype(F32)
        dq = drv * bg
        dqf = jnp.where(i < nt - 1, drf_ref[...].astype(F32) * bf_ref[...].astype(F32), 0.0)
        dp = _conv3_bwd_in(jnp.concatenate([dq, dqf], axis=0), cwv, ts)
        dz_ref[:, 0:D] = (drv * q).astype(BF16)
        dz_ref[:, D:2 * D] = (dp * xv).astype(BF16)
        dz_ref[:, 2 * D:3 * D] = (dp * cg).astype(BF16)
        rows = _conv3_bwd_w(dq, taps)

        @pl.when(i == 0)
        def _():
            for k in range(3):
                dcw_ref[k:k + 1, :] = rows[k]

        @pl.when(i > 0)
        def _():
            for k in range(3):
                dcw_ref[k:k + 1, :] += rows[k]

    past = lambda col: pl.BlockSpec((HALO, D), lambda i: (jnp.maximum(i * hb - 1, 0), col))
    nxt = lambda i: jnp.minimum((i + 1) * hb, s // HALO - 1)
    return pl.pallas_call(
        body, grid=(nt,), name=name,
        in_specs=[pl.BlockSpec((ts, D), lambda i: (i, 0)),
                  pl.BlockSpec((HALO, D), lambda i: (nxt(i), 0)),
                  pl.BlockSpec((ts, 3 * D), lambda i: (i, 0)), past(1), past(2),
                  pl.BlockSpec((HALO, D), lambda i: (nxt(i), 0)),
                  pl.BlockSpec((3, D), lambda i: (0, 0))],
        out_specs=[pl.BlockSpec((ts, 3 * D), lambda i: (i, 0)), pl.BlockSpec((3, D), lambda i: (0, 0))],
        out_shape=[jax.ShapeDtypeStruct((s, 3 * D), BF16), jax.ShapeDtypeStruct((3, D), F32)],
        compiler_params=_params("arbitrary"),
    )(dr, dr, zc, zc, zc, zc, cw)


G512_ROWS = 40


def _ab_bwd(dy, z, yb2, lga, lba, wsm, bs_col, cwb, lgb, lbb, name):
    s = z.shape[0]
    ts = _tile(s, 256)
    nt = s // ts
    hb = ts // HALO_B
    nch = ts // CHUNK
    tri = None

    def body(z_ref, zh_ref, dy_ref, dyf_ref, yb2_ref, yb2f_ref, lga_ref, lba_ref, ws_ref, bs_ref,
             cw_ref, lgb_ref, lbb_ref, dz_ref, g512_ref, dws_ref, dbs_ref, dvn_ref):
        i = pl.program_id(0)
        last = i == nt - 1

        @pl.when(i == 0)
        def _():
            g512_ref[...] = jnp.zeros((G512_ROWS, DA), F32)
            dws_ref[...] = jnp.zeros((HEADS, CHUNK, CHUNK), F32)
            dbs_ref[...] = jnp.zeros((HEADS, CHUNK, 1), F32)

        def add_row(k, v):
            g512_ref[k:k + 1, :] += v

        z_t = z_ref[...].astype(F32)
        dy_t = dy_ref[...].astype(F32)
        ua, va = z_t[:, 0:DA], z_t[:, DA:2 * DA]
        gu = _gelu(ua)
        gv = _gelu(va)
        lga_v = lga_ref[...]
        vn, xhat_a, rstd_a = _ln_fwd(gv, lga_v, lba_ref[...])
        vnb = vn.astype(BF16)
        causal = (lax.broadcasted_iota(jnp.int32, (CHUNK, CHUNK), 0)
                  >= lax.broadcasted_iota(jnp.int32, (CHUNK, CHUNK), 1)).astype(F32)
        for c in range(nch):
            for h in range(HEADS):
                rs = slice(c * CHUNK, (c + 1) * CHUNK)
                cs = slice(h * CHUNK, (h + 1) * CHUNK)
                vblk = vnb[rs, cs]
                mixed = jnp.dot(ws_ref[h], vblk, preferred_element_type=F32) + bs_ref[h]
                dyb_ = dy_t[rs, cs]
                dmix = dyb_ * gu[rs, cs]
                dmb = dmix.astype(BF16)
                dz_ref[rs, cs] = (dyb_ * mixed * _dgelu(ua[rs, cs])).astype(BF16)
                dvn_ref[rs, cs] = lax.dot_general(ws_ref[h], dmb, TN_DIMS, preferred_element_type=F32)
                dws_ref[h] += causal * lax.dot_general(dmb, vblk, NT_DIMS, preferred_element_type=F32)
                dbs_ref[h] += jnp.sum(dmix, axis=1, keepdims=True)
        dvn = dvn_ref[...]
        add_row(0, jnp.sum(dvn * xhat_a, axis=0, keepdims=True))
        add_row(1, jnp.sum(dvn, axis=0, keepdims=True))
        dgv = _ln_bwd(dvn, xhat_a, rstd_a, lga_v)
        dz_ref[:, DA:2 * DA] = (dgv * _dgelu(va)).astype(BF16)
        lgb_v = lgb_ref[...]
        dyb_e = jnp.concatenate(
            [dy_t[:, DA:2 * DA], jnp.where(last, 0.0, dyf_ref[...].astype(F32))], axis=0)
        yb2_e = jnp.concatenate([yb2_ref[...], jnp.where(last, 0.0, yb2f_ref[...])], axis=0)
        n_e, xhat_b, rstd_b = _ln_fwd(yb2_e, lgb_v, lbb_ref[...])
        sgn = _sigmoid(n_e)
        dn = dyb_e * _dsilu(n_e, sgn)
        dy2 = _ln_bwd(dn, xhat_b, rstd_b, lgb_v)
        add_row(2, jnp.sum(dy2[:ts], axis=0, keepdims=True))
        add_row(3, jnp.sum(dn[:ts] * xhat_b[:ts], axis=0, keepdims=True))
        add_row(4, jnp.sum(dn[:ts], axis=0, keepdims=True))
        zh = jnp.where(i > 0, zh_ref[...], jnp.zeros_like(zh_ref[...])).astype(F32)
        xb_t, gb_t = z_t[:, 2 * DA:3 * DA], z_t[:, 3 * DA:4 * DA]
        sgb = _sigmoid(gb_t)
        ue = jnp.concatenate([zh[:, 0:DA] * _sigmoid(zh[:, DA:2 * DA]), xb_t * sgb], axis=0)
        dy2_t = dy2[:ts]
        n_e_rows = ts + HALO_B
        du = jnp.zeros((ts, DA), F32)
        for r in range(8):
            fwd_roll = ue if r == 0 else pltpu.roll(ue, r, 0)
            bwd_roll = dy2 if r == 0 else pltpu.roll(dy2, n_e_rows - r, 0)
            for q in range(4):
                sh = 8 * q + r
                if sh >= BCONV:
                    continue
                k = BCONV - 1 - sh
                du = du + cw_ref[k:k + 1, :] * bwd_roll[8 * q:8 * q + ts]
                add_row(8 + k, jnp.sum(dy2_t * fwd_roll[HALO_B - 8 * q:HALO_B - 8 * q + ts],
                                       axis=0, keepdims=True))
        dz_ref[:, 2 * DA:3 * DA] = (du * sgb).astype(BF16)
        dz_ref[:, 3 * DA:4 * DA] = (du * xb_t * sgb * (1.0 - sgb)).astype(BF16)

    row = lambda i: (0, 0)
    nxt = lambda i: jnp.minimum((i + 1) * hb, s // HALO_B - 1)
    return pl.pallas_call(
        body, grid=(nt,), name=name,
        in_specs=[pl.BlockSpec((ts, 4 * DA), lambda i: (i, 0)),
                  pl.BlockSpec((HALO_B, 2 * DA), lambda i: (jnp.maximum(i * hb - 1, 0), 1)),
                  pl.BlockSpec((ts, 2 * DA), lambda i: (i, 0)),
                  pl.BlockSpec((HALO_B, DA), lambda i: (nxt(i), 1)),
                  pl.BlockSpec((ts, DA), lambda i: (i, 0)),
                  pl.BlockSpec((HALO_B, DA), lambda i: (nxt(i), 0)),
                  pl.BlockSpec((1, DA), row), pl.BlockSpec((1, DA), row),
                  pl.BlockSpec((HEADS, CHUNK, CHUNK), lambda i: (0, 0, 0)),
                  pl.BlockSpec((HEADS, CHUNK, 1), lambda i: (0, 0, 0)),
                  pl.BlockSpec((BCONV, DA), row), pl.BlockSpec((1, DA), row), pl.BlockSpec((1, DA), row)],
        out_specs=[pl.BlockSpec((ts, 4 * DA), lambda i: (i, 0)),
                   pl.BlockSpec((G512_ROWS, DA), row),
                   pl.BlockSpec((HEADS, CHUNK, CHUNK), lambda i: (0, 0, 0)),
                   pl.BlockSpec((HEADS, CHUNK, 1), lambda i: (0, 0, 0))],
        out_shape=[jax.ShapeDtypeStruct((s, 4 * DA), BF16), jax.ShapeDtypeStruct((G512_ROWS, DA), F32),
                   jax.ShapeDtypeStruct((HEADS, CHUNK, CHUNK), F32),
                   jax.ShapeDtypeStruct((HEADS, CHUNK, 1), F32)],
        scratch_shapes=[pltpu.VMEM((ts, DA), F32)],
        compiler_params=_params("arbitrary"),
    )(z, z, dy, dy, yb2, yb2, lga, lba, wsm, bs_col, cwb, lgb, lbb)


def _dw_cols(a, dy, nb, bn, name):
    s = a.shape[0]
    tm = _tile(s, 512)
    nt = s // tm
    cpb = 4

    def body(a_ref, dy_ref, o_ref, acc):
        t = pl.program_id(1)
        av = a_ref[...]
        for b in range(cpb):
            p = lax.dot_general(av, dy_ref[:, b * bn:(b + 1) * bn], TN_DIMS, preferred_element_type=F32)

            @pl.when(t == 0)
            def _():
                acc[b] = p

            @pl.when(t > 0)
            def _():
                acc[b] += p

        @pl.when(t == nt - 1)
        def _():
            o_ref[...] = acc[...].astype(BF16)

    return pl.pallas_call(
        body, grid=(nb // cpb, nt), name=name,
        in_specs=[pl.BlockSpec((tm, D), lambda j, t: (t, 0)), pl.BlockSpec((tm, cpb * bn), lambda j, t: (t, j))],
        out_specs=pl.BlockSpec((cpb, D, bn), lambda j, t: (j, 0, 0)),
        out_shape=jax.ShapeDtypeStruct((nb, D, bn), BF16),
        scratch_shapes=[pltpu.VMEM((cpb, D, bn), F32)],
        compiler_params=_params("arbitrary", "arbitrary"),
    )(a, dy)


def _dw_rows(a, dy, name):
    s = a.shape[0]
    tm = _tile(s, 512)
    nt = s // tm
    rb = 512

    def body(a_ref, dy_ref, o_ref, acc):
        t = pl.program_id(1)
        p = lax.dot_general(a_ref[...], dy_ref[...].astype(BF16), TN_DIMS, preferred_element_type=F32)

        @pl.when(t == 0)
        def _():
            acc[...] = p

        @pl.when(t > 0)
        def _():
            acc[...] += p

        @pl.when(t == nt - 1)
        def _():
            o_ref[...] = acc[...].astype(BF16)

    return pl.pallas_call(
        body, grid=(D // rb, nt), name=name,
        in_specs=[pl.BlockSpec((tm, rb), lambda j, t: (t, j)), pl.BlockSpec((tm, D), lambda j, t: (t, 0))],
        out_specs=pl.BlockSpec((rb, D), lambda j, t: (j, 0)),
        out_shape=jax.ShapeDtypeStruct((D, D), BF16),
        scratch_shapes=[pltpu.VMEM((rb, D), F32)],
        compiler_params=_params("arbitrary", "arbitrary"),
    )(a, dy)


def _dw_up(h, dup, name):
    s = h.shape[0]
    tm = _tile(s, 512)
    nt = s // tm

    def body(h_ref, d_ref, o_ref, acc):
        t = pl.program_id(1)
        p = lax.dot_general(h_ref[...], d_ref[...], TN_DIMS, preferred_element_type=F32)

        @pl.when(t == 0)
        def _():
            acc[...] = p

        @pl.when(t > 0)
        def _():
            acc[...] += p

        @pl.when(t == nt - 1)
        def _():
            o_ref[...] = acc[...].astype(BF16)

    return pl.pallas_call(
        body, grid=(NDEV, nt), name=name,
        in_specs=[pl.BlockSpec((tm, D), lambda b, t: (t, 0)),
                  pl.BlockSpec((None, None, tm, FB), lambda b, t: (b % NG, b // NG, t, 0))],
        out_specs=pl.BlockSpec((None, D, FB), lambda b, t: (b, 0, 0)),
        out_shape=jax.ShapeDtypeStruct((NDEV, D, FB), BF16),
        scratch_shapes=[pltpu.VMEM((D, FB), F32)],
        compiler_params=_params("arbitrary", "arbitrary"),
    )(h, dup)


def _dw_dn(a, df, name):
    s = df.shape[0]
    tm = _tile(s, 512)
    nt = s // tm

    def body(a_ref, d_ref, o_ref, acc):
        t = pl.program_id(1)
        p = lax.dot_general(a_ref[...], d_ref[...].astype(BF16), TN_DIMS, preferred_element_type=F32)

        @pl.when(t == 0)
        def _():
            acc[...] = p

        @pl.when(t > 0)
        def _():
            acc[...] += p

        @pl.when(t == nt - 1)
        def _():
            o_ref[...] = acc[...].astype(BF16)

    return pl.pallas_call(
        body, grid=(NG, nt), name=name,
        in_specs=[pl.BlockSpec((None, tm, FB), lambda m, t: (m, t, 0)), pl.BlockSpec((tm, D), lambda m, t: (t, 0))],
        out_specs=pl.BlockSpec((FB, D), lambda m, t: (m, 0)),
        out_shape=jax.ShapeDtypeStruct((DFF, D), BF16),
        scratch_shapes=[pltpu.VMEM((FB, D), F32)],
        compiler_params=_params("arbitrary", "arbitrary"),
    )(a, df)


def _place():
    x, y, c = lax.axis_index("x"), lax.axis_index("y"), lax.axis_index("c")
    chips = [(1 - x, y), (x, 1 - y), (1 - x, 1 - y)]
    return x, y, c, chips


def _all_gather(shards, name):
    nt = len(shards)

    def body(*refs):
        srcs, dsts = refs[:nt], refs[nt:2 * nt]
        send_sems, recv_sems, local_sems = refs[2 * nt:]
        x, y, c, chips = _place()
        me, sib = (x, y, c), (x, y, 1 - c)

        def blk(t, p):
            return dsts[t].at[4 * p[0] + 2 * p[1] + p[2]]

        def copy(t, k, block, to, src=None):
            return pltpu.make_async_remote_copy(
                src_ref=blk(t, block) if src is None else src, dst_ref=blk(t, block),
                send_sem=send_sems.at[t, k], recv_sem=recv_sems.at[t, k],
                device_id=to, device_id_type=MESH)

        mine = [pltpu.make_async_copy(srcs[t], blk(t, me), local_sems.at[t]) for t in range(nt)]
        for cp in mine:
            cp.start()
        first = []
        for t in range(nt):
            first.append(copy(t, 0, me, sib, src=srcs[t]))
            first += [copy(t, 1 + j, me, (*chip, c), src=srcs[t]) for j, chip in enumerate(chips)]
        for cp in first:
            cp.start()
        passed = []
        for j, chip in enumerate(chips):
            for t in range(nt):
                copy(t, 1 + j, (*chip, c), me).wait_recv()
                cp = copy(t, 4 + j, (*chip, c), sib)
                cp.start()
                passed.append(cp)
        for t in range(nt):
            copy(t, 0, sib, me).wait_recv()
            for j, chip in enumerate(chips):
                copy(t, 4 + j, (*chip, 1 - c), me).wait_recv()
        for cp in first + passed:
            cp.wait_send()
        for cp in mine:
            cp.wait()

    return pl.pallas_call(
        body, name=name,
        in_specs=[ANY] * nt, out_specs=[ANY] * nt,
        out_shape=[jax.ShapeDtypeStruct((NDEV,) + a.shape, a.dtype) for a in shards],
        scratch_shapes=[pltpu.SemaphoreType.DMA((nt, 7)), pltpu.SemaphoreType.DMA((nt, 7)),
                        pltpu.SemaphoreType.DMA((nt,))],
        compiler_params=pltpu.CompilerParams(has_side_effects=True),
    )(*shards)


def _pair_exchange(grads, name):
    nt = len(grads)

    def body(*refs):
        srcs, dsts = refs[:nt], refs[nt:2 * nt]
        send_sems, recv_sems = refs[2 * nt:]
        x, y, c, _ = _place()
        copies = []
        for t in range(nt):
            for j in range(NCHIP):
                copies.append(pltpu.make_async_remote_copy(
                    src_ref=srcs[t].at[2 * j + 1 - c], dst_ref=dsts[t].at[j],
                    send_sem=send_sems.at[t, j], recv_sem=recv_sems.at[t, j],
                    device_id=(x, y, 1 - c), device_id_type=MESH))
        for cp in copies:
            cp.start()
        for cp in copies:
            cp.wait()

    return pl.pallas_call(
        body, name=name,
        in_specs=[ANY] * nt, out_specs=[ANY] * nt,
        out_shape=[jax.ShapeDtypeStruct((NCHIP,) + a.shape[1:], a.dtype) for a in grads],
        scratch_shapes=[pltpu.SemaphoreType.DMA((nt, NCHIP)), pltpu.SemaphoreType.DMA((nt, NCHIP))],
        compiler_params=pltpu.CompilerParams(has_side_effects=True),
    )(*grads)


def _pair_sum(own, got, cidx, name):
    _, _, r, cdim = own.shape
    tr = r
    for cand in (512, 256, 128, 64, 32, 16):
        if r % cand == 0:
            tr = cand
            break

    def body(c_ref, a_ref, b_ref, o_ref):
        o_ref[...] = (a_ref[...].astype(F32) + b_ref[...].astype(F32)).astype(BF16)

    return pl.pallas_call(
        body, name=name,
        grid_spec=pltpu.PrefetchScalarGridSpec(
            num_scalar_prefetch=1, grid=(NCHIP, r // tr),
            in_specs=[pl.BlockSpec((None, None, tr, cdim), lambda j, i, c_ref: (j, c_ref[0], i, 0)),
                      pl.BlockSpec((None, tr, cdim), lambda j, i, c_ref: (j, i, 0))],
            out_specs=pl.BlockSpec((None, tr, cdim), lambda j, i, c_ref: (j, i, 0))),
        out_shape=jax.ShapeDtypeStruct((NCHIP, r, cdim), BF16),
        compiler_params=_params("arbitrary", "arbitrary"),
    )(cidx, own, got)


def _chip_scatter(sums, layout, name):
    nt = len(sums)
    nout = max(o for o, _ in layout) + 1
    out_shapes = [None] * nout
    for t, (o, l) in enumerate(layout):
        r, cdim = sums[t].shape[1:]
        nl = max(ll for oo, ll in layout if oo == o) if l is not None else None
        out_shapes[o] = (4, r, cdim) if l is None else (4, nl + 1, r, cdim)

    def body(*refs):
        srcs, dsts = refs[:nt], refs[nt:nt + nout]
        send_sems, recv_sems, local_sems = refs[nt + nout:]
        x, y, c, chips = _place()

        def slot(t, k):
            o, l = layout[t]
            return dsts[o].at[k] if l is None else dsts[o].at[k, l]

        local = [pltpu.make_async_copy(srcs[t].at[2 * x + y], slot(t, 3), local_sems.at[t]) for t in range(nt)]
        for cp in local:
            cp.start()
        copies = []
        for t in range(nt):
            for k, (cx, cy) in enumerate(chips):
                copies.append(pltpu.make_async_remote_copy(
                    src_ref=srcs[t].at[2 * cx + cy], dst_ref=slot(t, k),
                    send_sem=send_sems.at[t, k], recv_sem=recv_sems.at[t, k],
                    device_id=(cx, cy, c), device_id_type=MESH))
        for cp in copies:
            cp.start()
        for cp in copies:
            cp.wait()
        for cp in local:
            cp.wait()

    return pl.pallas_call(
        body, name=name,
        in_specs=[ANY] * nt, out_specs=[ANY] * nout,
        out_shape=[jax.ShapeDtypeStruct(sh, BF16) for sh in out_shapes],
        scratch_shapes=[pltpu.SemaphoreType.DMA((nt, 3)), pltpu.SemaphoreType.DMA((nt, 3)),
                        pltpu.SemaphoreType.DMA((nt,))],
        compiler_params=pltpu.CompilerParams(has_side_effects=True),
    )(*sums)


def _small_allreduce(parts, name):
    nt = len(parts)

    def body(*refs):
        srcs, outs, bufs = refs[:nt], refs[nt:2 * nt], refs[2 * nt:3 * nt]
        send_sems, recv_sems = refs[3 * nt:]
        x, y, c, _ = _place()
        peers = [(x, y, 1 - c), (1 - x, y, c), (x, 1 - y, c)]
        for t in range(nt):
            outs[t][...] = srcs[t][...]
        for step, peer in enumerate(peers):
            copies = [pltpu.make_async_remote_copy(
                src_ref=outs[t], dst_ref=bufs[t].at[step],
                send_sem=send_sems.at[step, t], recv_sem=recv_sems.at[step, t],
                device_id=peer, device_id_type=MESH) for t in range(nt)]
            for cp in copies:
                cp.start()
            for cp in copies:
                cp.wait()
            for t in range(nt):
                outs[t][...] = outs[t][...] + bufs[t][step]

    vm = pl.BlockSpec(memory_space=pltpu.VMEM)
    return pl.pallas_call(
        body, name=name,
        in_specs=[vm] * nt, out_specs=[vm] * nt,
        out_shape=[jax.ShapeDtypeStruct(a.shape, F32) for a in parts],
        scratch_shapes=[pltpu.VMEM((3,) + a.shape, F32) for a in parts]
        + [pltpu.SemaphoreType.DMA((3, nt)), pltpu.SemaphoreType.DMA((3, nt))],
        compiler_params=pltpu.CompilerParams(has_side_effects=True, vmem_limit_bytes=VMEM_LIMIT),
    )(*parts)


def _adam_math(w, g, m, v):
    m2 = ADAM_B1 * m + (1.0 - ADAM_B1) * g
    v2 = ADAM_B2 * v + (1.0 - ADAM_B2) * (g * g)
    m_hat = m2 / (1.0 - ADAM_B1 ** ADAM_STEP)
    v_hat = v2 / (1.0 - ADAM_B2 ** ADAM_STEP)
    delta = -ADAM_LR * (m_hat / (jnp.sqrt(v_hat) + ADAM_EPS) + ADAM_WD * w)
    return delta, m2, v2


def _adam_big(w, m, v, parts, name):
    nl, r, cdim = w.shape
    tr = r
    for cand in (256, 128, 64, 32, 16, 8):
        if r % cand == 0:
            tr = cand
            break

    def body(w_ref, m_ref, v_ref, p_ref, g_ref, d_ref, mo_ref, vo_ref):
        g = ((p_ref[0].astype(F32) + p_ref[1].astype(F32)) + p_ref[2].astype(F32)) + p_ref[3].astype(F32)
        delta, m2, v2 = _adam_math(w_ref[...], g, m_ref[...], v_ref[...])
        g_ref[...] = g
        d_ref[...] = delta
        mo_ref[...] = m2
        vo_ref[...] = v2

    spec = pl.BlockSpec((None, tr, cdim), lambda l, i: (l, i, 0))
    return pl.pallas_call(
        body, grid=(nl, r // tr), name=name,
        in_specs=[spec, spec, spec, pl.BlockSpec((4, None, tr, cdim), lambda l, i: (0, l, i, 0))],
        out_specs=[spec] * 4,
        out_shape=[jax.ShapeDtypeStruct(w.shape, F32)] * 4,
        compiler_params=_params("parallel", "parallel"),
    )(w, m, v, parts)


def _adam_small(ws, gs, ms, vs, name):
    n = len(ws)

    def body(*refs):
        w_r, g_r, m_r, v_r = refs[:n], refs[n:2 * n], refs[2 * n:3 * n], refs[3 * n:4 * n]
        d_o, m_o, v_o = refs[4 * n:5 * n], refs[5 * n:6 * n], refs[6 * n:7 * n]
        for t in range(n):
            delta, m2, v2 = _adam_math(w_r[t][...], g_r[t][...], m_r[t][...], v_r[t][...])
            d_o[t][...] = delta
            m_o[t][...] = m2
            v_o[t][...] = v2

    vm = pl.BlockSpec(memory_space=pltpu.VMEM)
    shapes = [jax.ShapeDtypeStruct(a.shape, F32) for a in ws]
    return pl.pallas_call(
        body, name=name, in_specs=[vm] * (4 * n), out_specs=[vm] * (3 * n), out_shape=shapes * 3,
        compiler_params=pltpu.CompilerParams(vmem_limit_bytes=VMEM_LIMIT),
    )(*ws, *gs, *ms, *vs)


def kernel(x, norm_mix, norm_ffn, norm_final, ab_w_in, a_ln_g, a_ln_b, a_w_s, a_b_s, b_conv_w, b_conv_b, b_ln_g, b_ln_b, ab_w_out, c_w_in, c_conv_w, c_w_out, f_w_up, f_conv_w, f_w_down, loss_target, m_norm_mix, m_norm_ffn, m_norm_final, m_ab_w_in, m_a_ln_g, m_a_ln_b, m_a_w_s, m_a_b_s, m_b_conv_w, m_b_conv_b, m_b_ln_g, m_b_ln_b, m_ab_w_out, m_c_w_in, m_c_conv_w, m_c_w_out, m_f_w_up, m_f_conv_w, m_f_w_down, v_norm_mix, v_norm_ffn, v_norm_final, v_ab_w_in, v_a_ln_g, v_a_ln_b, v_a_w_s, v_a_b_s, v_b_conv_w, v_b_conv_b, v_b_ln_g, v_b_ln_b, v_ab_w_out, v_c_w_in, v_c_conv_w, v_c_w_out, v_f_w_up, v_f_conv_w, v_f_w_down):
    s = x.shape[1]
    x0 = x.reshape(s, D)
    tgt = loss_target.reshape(s, D)
    xi, yi, ci = lax.axis_index("x"), lax.axis_index("y"), lax.axis_index("c")
    dev = 4 * xi + 2 * yi + ci
    cidx = ci.astype(jnp.int32).reshape(1)

    bf = lambda a: a.astype(BF16)
    shards = [bf(ab_w_in[0]), bf(ab_w_out[0]), bf(c_w_in[0]), bf(c_w_out[0]),
              bf(f_w_up[0]), bf(f_w_up[1]), bf(f_w_down[0]), bf(f_w_down[1]),
              b_conv_w[0], c_conv_w[0], f_conv_w.reshape(6, FB)]
    (win0, wout0, cin, cout, wup0, wup1, wdn0, wdn1, bcw_g, ccw_g, fcw_g) = _all_gather(shards, "all_gather_weights")
    wout0 = wout0.reshape(D, D)
    cout = cout.reshape(D, D)
    wups = [wup0.reshape(2, NG, D, FB), wup1.reshape(2, NG, D, FB)]
    wdns = [wdn0.reshape(DFF, D), wdn1.reshape(DFF, D)]
    bcw = jnp.transpose(bcw_g, (1, 0, 2)).reshape(BCONV, DA)
    ccw = jnp.transpose(ccw_g, (1, 0, 2)).reshape(3, D)
    fcw_g = fcw_g.reshape(2, NG, 2, 3, FB)
    fcws = [fcw_g[:, :, 0], fcw_g[:, :, 1]]

    causal = jnp.tril(jnp.ones((CHUNK, CHUNK), F32))
    wsm = (a_w_s[0] * causal).astype(BF16)
    bs_col = a_b_s.reshape(HEADS, CHUNK, 1)
    nm = [norm_mix[0:1], norm_mix[1:2]]
    nf = [norm_ffn[0:1], norm_ffn[1:2]]
    nfin = norm_final.reshape(1, D)

    h0 = _rms_fwd(x0, nm[0], "rms_mix0")
    z = _mm_in(h0, win0, "mm_ab_in")
    ycat, yb2 = _ab_fwd(z, a_ln_g, a_ln_b, wsm, bs_col, bcw, b_conv_b, b_ln_g, b_ln_b, "ab_fwd")
    x1 = _mm_out(ycat, wout0, x0, "mm_ab_out")
    h1 = _rms_fwd(x1, nf[0], "rms_ffn0")
    up0, x2 = _ffn_fwd(h1, x1, wups[0], fcws[0], wdns[0], "ffn_fwd0")
    h2 = _rms_fwd(x2, nm[1], "rms_mix1")
    zc = _mm_in(h2, cin, "mm_c_in")
    rc = _c_fwd(zc, ccw, "c_fwd")
    x3 = _mm_out(rc, cout, x2, "mm_c_out")
    h3 = _rms_fwd(x3, nf[1], "rms_ffn1")
    up1, x4 = _ffn_fwd(h3, x3, wups[1], fcws[1], wdns[1], "ffn_fwd1")
    dx4, dnfin, loss_part = _final(x4, tgt, nfin, "final_loss")
    loss = lax.psum(loss_part[0, 0], ("x", "y", "c"))

    a1, dup1, dx3, dnf1, dfcw1 = _ffn_bwd(dx4, up1, wups[1], fcws[1], wdns[1], x3, nf[1], "ffn_bwd1")
    g_wdn1 = _dw_dn(a1, dx4, "dw_dn1")
    g_wup1 = _dw_up(h3, dup1, "dw_up1")
    drc = _mm_nt(dx3, cout, "mm_c_out_bwd")
    g_cout = _dw_rows(rc, dx3, "dw_c_out")
    dzc, dccw = _c_bwd(drc, zc, ccw, "c_bwd")
    dx2, dnm1 = _mm_nt_rms(dzc, cin, x2, nm[1], dx3, "mm_c_in_bwd")
    g_cin = _dw_cols(h2, dzc, NDEV, 3 * D // NDEV, "dw_c_in")
    a0, dup0, dx1, dnf0, dfcw0 = _ffn_bwd(dx2, up0, wups[0], fcws[0], wdns[0], x1, nf[0], "ffn_bwd0")
    g_wdn0 = _dw_dn(a0, dx2, "dw_dn0")
    g_wup0 = _dw_up(h1, dup0, "dw_up0")
    dycat = _mm_nt(dx1, wout0, "mm_ab_out_bwd")
    g_wout0 = _dw_rows(ycat, dx1, "dw_ab_out")
    dz, g512, dws, dbs = _ab_bwd(dycat, z, yb2, a_ln_g, a_ln_b, wsm, bs_col, bcw, b_ln_g, b_ln_b, "ab_bwd")
    grad_x, dnm0 = _mm_nt_rms(dz, win0, x0, nm[0], dx1, "mm_ab_in_bwd")
    g_win0 = _dw_cols(h0, dz, NDEV, 2 * D // NDEV, "dw_ab_in")

    big = [g_win0, g_wout0.reshape(NDEV, D // NDEV, D), g_cin, g_cout.reshape(NDEV, D // NDEV, D),
           g_wup0, g_wup1, g_wdn0.reshape(NDEV, DFF // NDEV, D), g_wdn1.reshape(NDEV, DFF // NDEV, D)]
    got = _pair_exchange(big, "rs_pair_exchange")
    sums = [_pair_sum(b.reshape((NCHIP, 2) + b.shape[1:]), g, cidx, "rs_pair_sum%d" % t)
            for t, (b, g) in enumerate(zip(big, got))]
    layout = [(0, None), (1, None), (2, None), (3, None), (4, 0), (4, 1), (5, 0), (5, 1)]
    p_win0, p_wout0, p_cin, p_cout, p_wup, p_wdn = _chip_scatter(sums, layout, "rs_chip_scatter")

    g1024 = jnp.concatenate([dnm0, dnm1, dnf0, dnf1, dnfin, dccw], axis=0)
    gfc = jnp.concatenate([dfcw0, dfcw1], axis=0).reshape(2 * NG * 2 * 3, FB)
    g1024, g512, dws, dbs, gfc = _small_allreduce(
        [g1024, g512, dws.reshape(HEADS * CHUNK, CHUNK), dbs.reshape(HEADS, CHUNK), gfc], "small_allreduce")

    def big_update(w, m, v, parts, name):
        shp = w.shape
        w3, m3, v3 = (a.reshape((-1,) + shp[-2:]) for a in (w, m, v))
        p4 = parts.reshape((4,) + w3.shape)
        return [o.reshape(shp) for o in _adam_big(w3, m3, v3, p4, name)]

    u_win0 = big_update(ab_w_in, m_ab_w_in, v_ab_w_in, p_win0, "adam_ab_w_in")
    u_wout0 = big_update(ab_w_out, m_ab_w_out, v_ab_w_out, p_wout0, "adam_ab_w_out")
    u_cin = big_update(c_w_in, m_c_w_in, v_c_w_in, p_cin, "adam_c_w_in")
    u_cout = big_update(c_w_out, m_c_w_out, v_c_w_out, p_cout, "adam_c_w_out")
    u_wup = big_update(f_w_up, m_f_w_up, v_f_w_up, p_wup, "adam_f_w_up")
    u_wdn = big_update(f_w_down, m_f_w_down, v_f_w_down, p_wdn, "adam_f_w_down")

    g_norm_mix = g1024[0:2]
    g_norm_ffn = g1024[2:4]
    g_norm_final = g1024[4:5]
    g_ccw = lax.dynamic_slice(g1024[5:8], (0, dev * (D // NDEV)), (3, D // NDEV))
    g_bcw = lax.dynamic_slice(g512[8:8 + BCONV], (0, dev * (DA // NDEV)), (BCONV, DA // NDEV))
    gfc = gfc.reshape(2, NG, 2, 3, FB)
    g_fcw = lax.dynamic_slice(gfc, (0, dev % NG, dev // NG, 0, 0), (2, 1, 1, 3, FB)).reshape(2, 3, FB)
    small_w = [norm_mix, norm_ffn, nfin, a_ln_g, a_ln_b, a_w_s[0], a_b_s[0], b_conv_w[0], b_conv_b,
               b_ln_g, b_ln_b, c_conv_w[0], f_conv_w]
    small_g = [g_norm_mix, g_norm_ffn, g_norm_final, g512[0:1], g512[1:2],
               dws.reshape(HEADS, CHUNK, CHUNK), dbs, g_bcw, g512[2:3],
               g512[3:4], g512[4:5], g_ccw, g_fcw]
    small_m = [m_norm_mix, m_norm_ffn, m_norm_final.reshape(1, D), m_a_ln_g, m_a_ln_b, m_a_w_s[0], m_a_b_s[0],
               m_b_conv_w[0], m_b_conv_b, m_b_ln_g, m_b_ln_b, m_c_conv_w[0], m_f_conv_w]
    small_v = [v_norm_mix, v_norm_ffn, v_norm_final.reshape(1, D), v_a_ln_g, v_a_ln_b, v_a_w_s[0], v_a_b_s[0],
               v_b_conv_w[0], v_b_conv_b, v_b_ln_g, v_b_ln_b, v_c_conv_w[0], v_f_conv_w]
    upd = _adam_small(small_w, small_g, small_m, small_v, "adam_small")
    ns = len(small_w)
    orig = [norm_mix, norm_ffn, norm_final, a_ln_g, a_ln_b, a_w_s, a_b_s, b_conv_w, b_conv_b,
            b_ln_g, b_ln_b, c_conv_w, f_conv_w]
    sg_out = [g.reshape(o.shape) for g, o in zip(small_g, orig)]
    sd_out = [a.reshape(o.shape) for a, o in zip(upd[0:ns], orig)]
    sm_out = [a.reshape(o.shape) for a, o in zip(upd[ns:2 * ns], orig)]
    sv_out = [a.reshape(o.shape) for a, o in zip(upd[2 * ns:3 * ns], orig)]

    def assemble(small, k):
        return [small[0], small[1], small[2], u_win0[k], small[3], small[4], small[5], small[6], small[7],
                small[8], small[9], small[10], u_wout0[k], u_cin[k], small[11], u_cout[k], u_wup[k],
                small[12], u_wdn[k]]

    grads = assemble(sg_out, 0)
    deltas = assemble(sd_out, 1)
    new_m = assemble(sm_out, 2)
    new_v = assemble(sv_out, 3)
    return (loss, grad_x.reshape(1, s, D), *grads, *deltas, *new_m, *new_v)
```

```python
import functools
import math

import jax
import jax.numpy as jnp
from jax import lax
from jax.experimental import pallas as pl
from jax.experimental.pallas import tpu as pltpu

F32 = jnp.float32
BF16 = jnp.bfloat16

D = 1024
DA = 512
HEADS = 4
CHUNK = 128
DFF = 2816
NDEV = 8
NCHIP = 4
FB = DFF * 2 // NDEV
NG = DFF // FB
BCONV = 31
EPS = 1e-6
HALO = 16
HALO_B = 32
VMEM_LIMIT = 52 * 1024 * 1024
INV_SQRT2 = 1.0 / math.sqrt(2.0)
INV_SQRT_2PI = 1.0 / math.sqrt(2.0 * math.pi)

ADAM_LR = 0.001
ADAM_B1 = 0.9
ADAM_B2 = 0.999
ADAM_EPS = 1e-08
ADAM_WD = 0.01
ADAM_STEP = 10

MESH = pl.DeviceIdType.MESH
ANY = pl.BlockSpec(memory_space=pl.ANY)
NT_DIMS = (((1,), (1,)), ((), ()))
TN_DIMS = (((0,), (0,)), ((), ()))


def _params(*sem):
    return pltpu.CompilerParams(dimension_semantics=sem, vmem_limit_bytes=VMEM_LIMIT)


def _tile(s, want):
    return min(want, s)


def _sigmoid(x):
    return jax.nn.sigmoid(x)


def _dsilu(x, sg):
    return sg * (1.0 + x * (1.0 - sg))


def _gelu(x):
    return 0.5 * x * (1.0 + lax.erf(x * INV_SQRT2))


def _dgelu(x):
    return 0.5 * (1.0 + lax.erf(x * INV_SQRT2)) + x * jnp.exp(-0.5 * x * x) * INV_SQRT_2PI


def _ln_fwd(x, g, b):
    mu = jnp.mean(x, axis=-1, keepdims=True)
    xc = x - mu
    var = jnp.mean(xc * xc, axis=-1, keepdims=True)
    rstd = lax.rsqrt(var + EPS)
    xhat = xc * rstd
    return xhat * g + b, xhat, rstd


def _ln_bwd(dy, xhat, rstd, g):
    dxh = dy * g
    m1 = jnp.mean(dxh, axis=-1, keepdims=True)
    m2 = jnp.mean(dxh * xhat, axis=-1, keepdims=True)
    return rstd * (dxh - m1 - xhat * m2)


def _rms_bwd_math(dh, x, g):
    r = lax.rsqrt(jnp.mean(x * x, axis=-1, keepdims=True) + EPS)
    xhat = x * r
    dg = jnp.sum(dh * xhat, axis=0, keepdims=True)
    u = dh * g
    dx = r * (u - xhat * jnp.mean(u * xhat, axis=-1, keepdims=True))
    return dx, dg


def _conv3(xe, cw, halo):
    x0 = xe[halo:]
    x1 = pltpu.roll(xe, 1, 0)[halo:]
    x2 = pltpu.roll(xe, 2, 0)[halo:]
    return cw[2] * x0 + cw[1] * x1 + cw[0] * x2, (x0, x1, x2)


def _conv3_bwd_in(dce, cw, ts):
    n = dce.shape[0]
    d1 = pltpu.roll(dce, n - 1, 0)[:ts]
    d2 = pltpu.roll(dce, n - 2, 0)[:ts]
    return cw[2] * dce[:ts] + cw[1] * d1 + cw[0] * d2


def _conv3_bwd_w(dc, taps):
    x0, x1, x2 = taps
    return [jnp.sum(dc * x2, axis=0, keepdims=True), jnp.sum(dc * x1, axis=0, keepdims=True),
            jnp.sum(dc * x0, axis=0, keepdims=True)]


def _rms_fwd(x, g, name, after=None):
    s = x.shape[0]
    ts = _tile(s, 512)

    def body(x_ref, g_ref, *rest):
        h_ref = rest[-1]
        xv = x_ref[...]
        r = lax.rsqrt(jnp.mean(xv * xv, axis=-1, keepdims=True) + EPS)
        h_ref[...] = (xv * r * g_ref[...]).astype(BF16)

    extra = [] if after is None else [after]
    return pl.pallas_call(
        body, grid=(s // ts,), name=name,
        in_specs=[pl.BlockSpec((ts, D), lambda i: (i, 0)), pl.BlockSpec((1, D), lambda i: (0, 0))]
        + [ANY] * len(extra),
        out_specs=pl.BlockSpec((ts, D), lambda i: (i, 0)),
        out_shape=jax.ShapeDtypeStruct((s, D), BF16),
        compiler_params=_params("parallel"),
    )(x, g, *extra)


def _mm_in(h, wblk, name):
    s = h.shape[0]
    nb, _, bn = wblk.shape
    ts = _tile(s, 512)

    def body(h_ref, w_ref, o_ref):
        hv = h_ref[...]
        for b in range(nb):
            o_ref[:, b * bn:(b + 1) * bn] = jnp.dot(hv, w_ref[b], preferred_element_type=F32).astype(BF16)

    return pl.pallas_call(
        body, grid=(s // ts,), name=name,
        in_specs=[pl.BlockSpec((ts, D), lambda i: (i, 0)), pl.BlockSpec((nb, D, bn), lambda i: (0, 0, 0))],
        out_specs=pl.BlockSpec((ts, nb * bn), lambda i: (i, 0)),
        out_shape=jax.ShapeDtypeStruct((s, nb * bn), BF16),
        compiler_params=_params("parallel"),
    )(h, wblk)


def _mm_out(y, w, xres, name):
    s = y.shape[0]
    ts = _tile(s, 512)

    def body(y_ref, w_ref, x_ref, o_ref):
        o_ref[...] = x_ref[...] + jnp.dot(y_ref[...], w_ref[...], preferred_element_type=F32)

    return pl.pallas_call(
        body, grid=(s // ts,), name=name,
        in_specs=[pl.BlockSpec((ts, D), lambda i: (i, 0)), pl.BlockSpec((D, D), lambda i: (0, 0)),
                  pl.BlockSpec((ts, D), lambda i: (i, 0))],
        out_specs=pl.BlockSpec((ts, D), lambda i: (i, 0)),
        out_shape=jax.ShapeDtypeStruct((s, D), F32),
        compiler_params=_params("parallel"),
    )(y, w, xres)


def _conv31(ue, cw_ref, ts):
    acc = jnp.zeros((ts, ue.shape[1]), F32)
    for r in range(8):
        rolled = ue if r == 0 else pltpu.roll(ue, r, 0)
        for q in range(4):
            sh = 8 * q + r
            if sh >= BCONV:
                continue
            k = BCONV - 1 - sh
            acc = acc + cw_ref[k:k + 1, :] * rolled[HALO_B - 8 * q:HALO_B - 8 * q + ts]
    return acc


def _ab_fwd(z, lga, lba, wsm, bs_col, cwb, cbb, lgb, lbb, name):
    s = z.shape[0]
    ts = _tile(s, 256)
    hb = ts // HALO_B

    def body(z_ref, zh_ref, lga_ref, lba_ref, ws_ref, bs_ref, cw_ref, cb_ref, lgb_ref, lbb_ref,
             y_ref, yb2_ref):
        i = pl.program_id(0)
        z_t = z_ref[...].astype(F32)
        gu = _gelu(z_t[:, 0:DA])
        gv = _gelu(z_t[:, DA:2 * DA])
        vn, _, _ = _ln_fwd(gv, lga_ref[...], lba_ref[...])
        vnb = vn.astype(BF16)
        for c in range(ts // CHUNK):
            for h in range(HEADS):
                rs = slice(c * CHUNK, (c + 1) * CHUNK)
                cs = slice(h * CHUNK, (h + 1) * CHUNK)
                mixed = jnp.dot(ws_ref[h], vnb[rs, cs], preferred_element_type=F32) + bs_ref[h]
                y_ref[rs, cs] = (gu[rs, cs] * mixed).astype(BF16)
        zh = jnp.where(i > 0, zh_ref[...], jnp.zeros_like(zh_ref[...])).astype(F32)
        xb = jnp.concatenate([zh[:, 0:DA], z_t[:, 2 * DA:3 * DA]], axis=0)
        gb = jnp.concatenate([zh[:, DA:2 * DA], z_t[:, 3 * DA:4 * DA]], axis=0)
        u = xb * _sigmoid(gb)
        conv = _conv31(u, cw_ref, ts) + cb_ref[...]
        yb2_ref[...] = conv
        nb_, _, _ = _ln_fwd(conv, lgb_ref[...], lbb_ref[...])
        y_ref[:, DA:2 * DA] = (nb_ * _sigmoid(nb_)).astype(BF16)

    row = lambda i: (0, 0)
    return pl.pallas_call(
        body, grid=(s // ts,), name=name,
        in_specs=[pl.BlockSpec((ts, 4 * DA), lambda i: (i, 0)),
                  pl.BlockSpec((HALO_B, 2 * DA), lambda i: (jnp.maximum(i * hb - 1, 0), 1)),
                  pl.BlockSpec((1, DA), row), pl.BlockSpec((1, DA), row),
                  pl.BlockSpec((HEADS, CHUNK, CHUNK), lambda i: (0, 0, 0)),
                  pl.BlockSpec((HEADS, CHUNK, 1), lambda i: (0, 0, 0)),
                  pl.BlockSpec((BCONV, DA), row), pl.BlockSpec((1, DA), row),
                  pl.BlockSpec((1, DA), row), pl.BlockSpec((1, DA), row)],
        out_specs=[pl.BlockSpec((ts, 2 * DA), lambda i: (i, 0)), pl.BlockSpec((ts, DA), lambda i: (i, 0))],
        out_shape=[jax.ShapeDtypeStruct((s, 2 * DA), BF16), jax.ShapeDtypeStruct((s, DA), F32)],
        compiler_params=_params("parallel"),
    )(z, z, lga, lba, wsm, bs_col, cwb, cbb, lgb, lbb)


def _c_fwd(zc, cw, name):
    s = zc.shape[0]
    ts = _tile(s, 512)
    hb = ts // HALO

    def body(z_ref, ch_ref, xh_ref, cw_ref, r_ref):
        i = pl.program_id(0)
        z_t = z_ref[...].astype(F32)
        ph = jnp.where(i > 0, ch_ref[...].astype(F32) * xh_ref[...].astype(F32), 0.0)
        pe = jnp.concatenate([ph, z_t[:, D:2 * D] * z_t[:, 2 * D:3 * D]], axis=0)
        q, _ = _conv3(pe, [cw_ref[k:k + 1, :] for k in range(3)], HALO)
        r_ref[...] = (z_t[:, 0:D] * q).astype(BF16)

    halo = lambda col: pl.BlockSpec((HALO, D), lambda i: (jnp.maximum(i * hb - 1, 0), col))
    return pl.pallas_call(
        body, grid=(s // ts,), name=name,
        in_specs=[pl.BlockSpec((ts, 3 * D), lambda i: (i, 0)), halo(1), halo(2),
                  pl.BlockSpec((3, D), lambda i: (0, 0))],
        out_specs=pl.BlockSpec((ts, D), lambda i: (i, 0)),
        out_shape=jax.ShapeDtypeStruct((s, D), BF16),
        compiler_params=_params("parallel"),
    )(zc, zc, zc, cw)


def _ffn_fwd(h, xres, wup, fcw, wdn, name):
    s = h.shape[0]
    ts = _tile(s, 512)
    hb = ts // HALO

    def body(h_ref, hh_ref, w_ref, cw_ref, wd_ref, x_ref, up_ref, xo_ref):
        i = pl.program_id(0)
        m = pl.program_id(1)
        halo = jnp.where(i > 0, hh_ref[...], jnp.zeros_like(hh_ref[...]))
        hx = jnp.concatenate([halo, h_ref[...]], axis=0)
        acts = []
        for gv in range(2):
            up = jnp.dot(hx, w_ref[gv], preferred_element_type=F32)
            up_ref[gv] = up[HALO:].astype(BF16)
            upc, _ = _conv3(up, [cw_ref[gv, k:k + 1, :] for k in range(3)], HALO)
            acts.append(upc)
        a = acts[0] * _sigmoid(acts[0]) * acts[1]
        f = jnp.dot(a.astype(BF16), wd_ref[...], preferred_element_type=F32)

        @pl.when(m == 0)
        def _():
            xo_ref[...] = x_ref[...] + f

        @pl.when(m > 0)
        def _():
            xo_ref[...] += f

    return pl.pallas_call(
        body, grid=(s // ts, NG), name=name,
        in_specs=[pl.BlockSpec((ts, D), lambda i, m: (i, 0)),
                  pl.BlockSpec((HALO, D), lambda i, m: (jnp.maximum(i * hb - 1, 0), 0)),
                  pl.BlockSpec((2, None, D, FB), lambda i, m: (0, m, 0, 0)),
                  pl.BlockSpec((2, None, 3, FB), lambda i, m: (0, m, 0, 0)),
                  pl.BlockSpec((FB, D), lambda i, m: (m, 0)),
                  pl.BlockSpec((ts, D), lambda i, m: (i, 0))],
        out_specs=[pl.BlockSpec((None, 2, ts, FB), lambda i, m: (m, 0, i, 0)),
                   pl.BlockSpec((ts, D), lambda i, m: (i, 0))],
        out_shape=[jax.ShapeDtypeStruct((NG, 2, s, FB), BF16), jax.ShapeDtypeStruct((s, D), F32)],
        compiler_params=_params("arbitrary", "arbitrary"),
    )(h, h, wup, fcw, wdn, xres)


def _final(x, tgt, g, name):
    s = x.shape[0]
    ts = _tile(s, 512)

    def body(x_ref, t_ref, g_ref, dx_ref, dg_ref, loss_ref):
        i = pl.program_id(0)
        xv = x_ref[...]
        gv = g_ref[...]
        r = lax.rsqrt(jnp.mean(xv * xv, axis=-1, keepdims=True) + EPS)
        xhat = xv * r
        e = xhat * gv - t_ref[...]
        part = 0.5 * jnp.sum(jnp.mean(e * e, axis=-1, keepdims=True), axis=0, keepdims=True)
        dy = e * (1.0 / D)
        dgp = jnp.sum(dy * xhat, axis=0, keepdims=True)
        u = dy * gv
        dx_ref[...] = r * (u - xhat * jnp.mean(u * xhat, axis=-1, keepdims=True))

        @pl.when(i == 0)
        def _():
            dg_ref[...] = dgp
            loss_ref[...] = jnp.broadcast_to(part, (1, 128))

        @pl.when(i > 0)
        def _():
            dg_ref[...] += dgp
            loss_ref[...] += jnp.broadcast_to(part, (1, 128))

    return pl.pallas_call(
        body, grid=(s // ts,), name=name,
        in_specs=[pl.BlockSpec((ts, D), lambda i: (i, 0)), pl.BlockSpec((ts, D), lambda i: (i, 0)),
                  pl.BlockSpec((1, D), lambda i: (0, 0))],
        out_specs=[pl.BlockSpec((ts, D), lambda i: (i, 0)), pl.BlockSpec((1, D), lambda i: (0, 0)),
                   pl.BlockSpec((1, 128), lambda i: (0, 0))],
        out_shape=[jax.ShapeDtypeStruct((s, D), F32), jax.ShapeDtypeStruct((1, D), F32),
                   jax.ShapeDtypeStruct((1, 128), F32)],
        compiler_params=_params("arbitrary"),
    )(x, tgt, g)


def _ffn_bwd(df, up, wup, fcw, wdn, xin, g, name):
    s = df.shape[0]
    ts = _tile(s, 512)
    nt = s // ts
    hb = ts // HALO

    def body(df_ref, up_ref, uph_ref, w_ref, cw_ref, wd_ref, x_ref, g_ref,
             a_ref, dup_ref, dx_ref, dg_ref, dcw_ref, carry, acc):
        i = pl.program_id(0)
        m = pl.program_id(1)
        ti = nt - 1 - i
        first = i == 0
        dfb = df_ref[...].astype(BF16)
        cws = [[cw_ref[gv, k:k + 1, :] for k in range(3)] for gv in range(2)]
        upc, taps = [], []
        for gv in range(2):
            halo = jnp.where(ti > 0, uph_ref[gv], jnp.zeros_like(uph_ref[gv]))
            upe = jnp.concatenate([halo, up_ref[gv]], axis=0).astype(F32)
            c, t = _conv3(upe, cws[gv], HALO)
            upc.append(c)
            taps.append(t)
        sg = _sigmoid(upc[0])
        sl = upc[0] * sg
        a_ref[...] = (sl * upc[1]).astype(BF16)
        da = lax.dot_general(dfb, wd_ref[...], NT_DIMS, preferred_element_type=F32)
        dcs = [da * upc[1] * _dsilu(upc[0], sg), da * sl]

        @pl.when(first)
        def _():
            carry[m] = jnp.zeros((2, 8, FB), F32)

        dh = jnp.zeros((ts, D), F32)
        for gv in range(2):
            dc = dcs[gv]
            rows = _conv3_bwd_w(dc, taps[gv])

            @pl.when(first)
            def _():
                for k in range(3):
                    dcw_ref[m, gv, k:k + 1, :] = rows[k]

            @pl.when(jnp.logical_not(first))
            def _():
                for k in range(3):
                    dcw_ref[m, gv, k:k + 1, :] += rows[k]

            dce = jnp.concatenate([dc, carry[m, gv]], axis=0)
            du = _conv3_bwd_in(dce, cws[gv], ts)
            carry[m, gv] = dc[0:8]
            dub = du.astype(BF16)
            dup_ref[gv] = dub
            dh = dh + lax.dot_general(dub, w_ref[gv], NT_DIMS, preferred_element_type=F32)

        @pl.when(m == 0)
        def _():
            acc[...] = dh

        @pl.when(m > 0)
        def _():
            acc[...] += dh

        @pl.when(m == NG - 1)
        def _():
            dx, dgp = _rms_bwd_math(acc[...], x_ref[...], g_ref[...])
            dx_ref[...] = df_ref[...] + dx

            @pl.when(first)
            def _():
                dg_ref[...] = dgp

            @pl.when(jnp.logical_not(first))
            def _():
                dg_ref[...] += dgp

    rev = lambda i: nt - 1 - i
    return pl.pallas_call(
        body, grid=(nt, NG), name=name,
        in_specs=[pl.BlockSpec((ts, D), lambda i, m: (rev(i), 0)),
                  pl.BlockSpec((None, 2, ts, FB), lambda i, m: (m, 0, rev(i), 0)),
                  pl.BlockSpec((None, 2, HALO, FB), lambda i, m: (m, 0, jnp.maximum(rev(i) * hb - 1, 0), 0)),
                  pl.BlockSpec((2, None, D, FB), lambda i, m: (0, m, 0, 0)),
                  pl.BlockSpec((2, None, 3, FB), lambda i, m: (0, m, 0, 0)),
                  pl.BlockSpec((FB, D), lambda i, m: (m, 0)),
                  pl.BlockSpec((ts, D), lambda i, m: (rev(i), 0)),
                  pl.BlockSpec((1, D), lambda i, m: (0, 0))],
        out_specs=[pl.BlockSpec((None, ts, FB), lambda i, m: (m, rev(i), 0)),
                   pl.BlockSpec((None, 2, ts, FB), lambda i, m: (m, 0, rev(i), 0)),
                   pl.BlockSpec((ts, D), lambda i, m: (rev(i), 0)),
                   pl.BlockSpec((1, D), lambda i, m: (0, 0)),
                   pl.BlockSpec((NG, 2, 3, FB), lambda i, m: (0, 0, 0, 0))],
        out_shape=[jax.ShapeDtypeStruct((NG, s, FB), BF16), jax.ShapeDtypeStruct((NG, 2, s, FB), BF16),
                   jax.ShapeDtypeStruct((s, D), F32), jax.ShapeDtypeStruct((1, D), F32),
                   jax.ShapeDtypeStruct((NG, 2, 3, FB), F32)],
        scratch_shapes=[pltpu.VMEM((NG, 2, 8, FB), F32), pltpu.VMEM((ts, D), F32)],
        compiler_params=_params("arbitrary", "arbitrary"),
    )(df, up, up, wup, fcw, wdn, xin, g)


def _mm_nt(dy, w, name):
    s = dy.shape[0]
    ts = _tile(s, 512)

    def body(dy_ref, w_ref, o_ref):
        o_ref[...] = lax.dot_general(dy_ref[...].astype(BF16), w_ref[...], NT_DIMS,
                                     preferred_element_type=F32).astype(BF16)

    return pl.pallas_call(
        body, grid=(s // ts,), name=name,
        in_specs=[pl.BlockSpec((ts, D), lambda i: (i, 0)), pl.BlockSpec((D, D), lambda i: (0, 0))],
        out_specs=pl.BlockSpec((ts, D), lambda i: (i, 0)),
        out_shape=jax.ShapeDtypeStruct((s, D), BF16),
        compiler_params=_params("parallel"),
    )(dy, w)


def _mm_nt_rms(dy, wblk, x, g, dres, name):
    s = dy.shape[0]
    nb, _, bn = wblk.shape
    ts = _tile(s, 512)

    def body(dy_ref, w_ref, x_ref, g_ref, dr_ref, dx_ref, dg_ref):
        i = pl.program_id(0)
        acc = jnp.zeros((ts, D), F32)
        for b in range(nb):
            acc = acc + lax.dot_general(dy_ref[:, b * bn:(b + 1) * bn], w_ref[b], NT_DIMS,
                                        preferred_element_type=F32)
        dx, dgp = _rms_bwd_math(acc, x_ref[...], g_ref[...])
        dx_ref[...] = dr_ref[...] + dx

        @pl.when(i == 0)
        def _():
            dg_ref[...] = dgp

        @pl.when(i > 0)
        def _():
            dg_ref[...] += dgp

    return pl.pallas_call(
        body, grid=(s // ts,), name=name,
        in_specs=[pl.BlockSpec((ts, nb * bn), lambda i: (i, 0)), pl.BlockSpec((nb, D, bn), lambda i: (0, 0, 0)),
                  pl.BlockSpec((ts, D), lambda i: (i, 0)), pl.BlockSpec((1, D), lambda i: (0, 0)),
                  pl.BlockSpec((ts, D), lambda i: (i, 0))],
        out_specs=[pl.BlockSpec((ts, D), lambda i: (i, 0)), pl.BlockSpec((1, D), lambda i: (0, 0))],
        out_shape=[jax.ShapeDtypeStruct((s, D), F32), jax.ShapeDtypeStruct((1, D), F32)],
        compiler_params=_params("arbitrary"),
    )(dy, wblk, x, g, dres)


def _c_bwd(dr, zc, cw, name):
    s = dr.shape[0]
    ts = _tile(s, 512)
    nt = s // ts
    hb = ts // HALO

    def body(dr_ref, drf_ref, z_ref, ch_ref, xh_ref, bf_ref, cw_ref, dz_ref, dcw_ref):
        i = pl.program_id(0)
        cwv = [cw_ref[k:k + 1, :] for k in range(3)]
        z_t = z_ref[...].astype(F32)
        bg, cg, xv = z_t[:, 0:D], z_t[:, D:2 * D], z_t[:, 2 * D:3 * D]
        ph = jnp.where(i > 0, ch_ref[...].astype(F32) * xh_ref[...].astype(F32), 0.0)
        pe = jnp.concatenate([ph, cg * xv], axis=0)
        q, taps = _conv3(pe, cwv, HALO)
        drv = dr_ref[...].astype(F32)
        dq = drv * bg
        dqf = jnp.where(i < nt - 1, drf_ref[...].astype(F32) * bf_ref[...].astype(F32), 0.0)
        dp = _conv3_bwd_in(jnp.concatenate([dq, dqf], axis=0), cwv, ts)
        dz_ref[:, 0:D] = (drv * q).astype(BF16)
        dz_ref[:, D:2 * D] = (dp * xv).astype(BF16)
        dz_ref[:, 2 * D:3 * D] = (dp * cg).astype(BF16)
        rows = _conv3_bwd_w(dq, taps)

        @pl.when(i == 0)
        def _():
            for k in range(3):
                dcw_ref[k:k + 1, :] = rows[k]

        @pl.when(i > 0)
        def _():
            for k in range(3):
                dcw_ref[k:k + 1, :] += rows[k]

    past = lambda col: pl.BlockSpec((HALO, D), lambda i: (jnp.maximum(i * hb - 1, 0), col))
    nxt = lambda i: jnp.minimum((i + 1) * hb, s // HALO - 1)
    return pl.pallas_call(
        body, grid=(nt,), name=name,
        in_specs=[pl.BlockSpec((ts, D), lambda i: (i, 0)),
                  pl.BlockSpec((HALO, D), lambda i: (nxt(i), 0)),
                  pl.BlockSpec((ts, 3 * D), lambda i: (i, 0)), past(1), past(2),
                  pl.BlockSpec((HALO, D), lambda i: (nxt(i), 0)),
                  pl.BlockSpec((3, D), lambda i: (0, 0))],
        out_specs=[pl.BlockSpec((ts, 3 * D), lambda i: (i, 0)), pl.BlockSpec((3, D), lambda i: (0, 0))],
        out_shape=[jax.ShapeDtypeStruct((s, 3 * D), BF16), jax.ShapeDtypeStruct((3, D), F32)],
        compiler_params=_params("arbitrary"),
    )(dr, dr, zc, zc, zc, zc, cw)


G512_ROWS = 40


def _ab_bwd(dy, z, yb2, lga, lba, wsm, bs_col, cwb, lgb, lbb, name):
    s = z.shape[0]
    ts = _tile(s, 256)
    nt = s // ts
    hb = ts // HALO_B
    nch = ts // CHUNK
    tri = None

    def body(z_ref, zh_ref, dy_ref, dyf_ref, yb2_ref, yb2f_ref, lga_ref, lba_ref, ws_ref, bs_ref,
             cw_ref, lgb_ref, lbb_ref, dz_ref, g512_ref, dws_ref, dbs_ref, dvn_ref):
        i = pl.program_id(0)
        last = i == nt - 1

        @pl.when(i == 0)
        def _():
            g512_ref[...] = jnp.zeros((G512_ROWS, DA), F32)
            dws_ref[...] = jnp.zeros((HEADS, CHUNK, CHUNK), F32)
            dbs_ref[...] = jnp.zeros((HEADS, CHUNK, 1), F32)

        def add_row(k, v):
            g512_ref[k:k + 1, :] += v

        z_t = z_ref[...].astype(F32)
        dy_t = dy_ref[...].astype(F32)
        ua, va = z_t[:, 0:DA], z_t[:, DA:2 * DA]
        gu = _gelu(ua)
        gv = _gelu(va)
        lga_v = lga_ref[...]
        vn, xhat_a, rstd_a = _ln_fwd(gv, lga_v, lba_ref[...])
        vnb = vn.astype(BF16)
        causal = (lax.broadcasted_iota(jnp.int32, (CHUNK, CHUNK), 0)
                  >= lax.broadcasted_iota(jnp.int32, (CHUNK, CHUNK), 1)).astype(F32)
        for c in range(nch):
            for h in range(HEADS):
                rs = slice(c * CHUNK, (c + 1) * CHUNK)
                cs = slice(h * CHUNK, (h + 1) * CHUNK)
                vblk = vnb[rs, cs]
                mixed = jnp.dot(ws_ref[h], vblk, preferred_element_type=F32) + bs_ref[h]
                dyb_ = dy_t[rs, cs]
                dmix = dyb_ * gu[rs, cs]
                dmb = dmix.astype(BF16)
                dz_ref[rs, cs] = (dyb_ * mixed * _dgelu(ua[rs, cs])).astype(BF16)
                dvn_ref[rs, cs] = lax.dot_general(ws_ref[h], dmb, TN_DIMS, preferred_element_type=F32)
                dws_ref[h] += causal * lax.dot_general(dmb, vblk, NT_DIMS, preferred_element_type=F32)
                dbs_ref[h] += jnp.sum(dmix, axis=1, keepdims=True)
        dvn = dvn_ref[...]
        add_row(0, jnp.sum(dvn * xhat_a, axis=0, keepdims=True))
        add_row(1, jnp.sum(dvn, axis=0, keepdims=True))
        dgv = _ln_bwd(dvn, xhat_a, rstd_a, lga_v)
        dz_ref[:, DA:2 * DA] = (dgv * _dgelu(va)).astype(BF16)
        lgb_v = lgb_ref[...]
        dyb_e = jnp.concatenate(
            [dy_t[:, DA:2 * DA], jnp.where(last, 0.0, dyf_ref[...].astype(F32))], axis=0)
        yb2_e = jnp.concatenate([yb2_ref[...], jnp.where(last, 0.0, yb2f_ref[...])], axis=0)
        n_e, xhat_b, rstd_b = _ln_fwd(yb2_e, lgb_v, lbb_ref[...])
        sgn = _sigmoid(n_e)
        dn = dyb_e * _dsilu(n_e, sgn)
        dy2 = _ln_bwd(dn, xhat_b, rstd_b, lgb_v)
        add_row(2, jnp.sum(dy2[:ts], axis=0, keepdims=True))
        add_row(3, jnp.sum(dn[:ts] * xhat_b[:ts], axis=0, keepdims=True))
        add_row(4, jnp.sum(dn[:ts], axis=0, keepdims=True))
        zh = jnp.where(i > 0, zh_ref[...], jnp.zeros_like(zh_ref[...])).astype(F32)
        xb_t, gb_t = z_t[:, 2 * DA:3 * DA], z_t[:, 3 * DA:4 * DA]
        sgb = _sigmoid(gb_t)
        ue = jnp.concatenate([zh[:, 0:DA] * _sigmoid(zh[:, DA:2 * DA]), xb_t * sgb], axis=0)
        dy2_t = dy2[:ts]
        n_e_rows = ts + HALO_B
        du = jnp.zeros((ts, DA), F32)
        for r in range(8):
            fwd_roll = ue if r == 0 else pltpu.roll(ue, r, 0)
            bwd_roll = dy2 if r == 0 else pltpu.roll(dy2, n_e_rows - r, 0)
            for q in range(4):
                sh = 8 * q + r
                if sh >= BCONV:
                    continue
                k = BCONV - 1 - sh
                du = du + cw_ref[k:k + 1, :] * bwd_roll[8 * q:8 * q + ts]
                add_row(8 + k, jnp.sum(dy2_t * fwd_roll[HALO_B - 8 * q:HALO_B - 8 * q + ts],
                                       axis=0, keepdims=True))
        dz_ref[:, 2 * DA:3 * DA] = (du * sgb).astype(BF16)
        dz_ref[:, 3 * DA:4 * DA] = (du * xb_t * sgb * (1.0 - sgb)).astype(BF16)

    row = lambda i: (0, 0)
    nxt = lambda i: jnp.minimum((i + 1) * hb, s // HALO_B - 1)
    return pl.pallas_call(
        body, grid=(nt,), name=name,
        in_specs=[pl.BlockSpec((ts, 4 * DA), lambda i: (i, 0)),
                  pl.BlockSpec((HALO_B, 2 * DA), lambda i: (jnp.maximum(i * hb - 1, 0), 1)),
                  pl.BlockSpec((ts, 2 * DA), lambda i: (i, 0)),
                  pl.BlockSpec((HALO_B, DA), lambda i: (nxt(i), 1)),
                  pl.BlockSpec((ts, DA), lambda i: (i, 0)),
                  pl.BlockSpec((HALO_B, DA), lambda i: (nxt(i), 0)),
                  pl.BlockSpec((1, DA), row), pl.BlockSpec((1, DA), row),
                  pl.BlockSpec((HEADS, CHUNK, CHUNK), lambda i: (0, 0, 0)),
                  pl.BlockSpec((HEADS, CHUNK, 1), lambda i: (0, 0, 0)),
                  pl.BlockSpec((BCONV, DA), row), pl.BlockSpec((1, DA), row), pl.BlockSpec((1, DA), row)],
        out_specs=[pl.BlockSpec((ts, 4 * DA), lambda i: (i, 0)),
                   pl.BlockSpec((G512_ROWS, DA), row),
                   pl.BlockSpec((HEADS, CHUNK, CHUNK), lambda i: (0, 0, 0)),
                   pl.BlockSpec((HEADS, CHUNK, 1), lambda i: (0, 0, 0))],
        out_shape=[jax.ShapeDtypeStruct((s, 4 * DA), BF16), jax.ShapeDtypeStruct((G512_ROWS, DA), F32),
                   jax.ShapeDtypeStruct((HEADS, CHUNK, CHUNK), F32),
                   jax.ShapeDtypeStruct((HEADS, CHUNK, 1), F32)],
        scratch_shapes=[pltpu.VMEM((ts, DA), F32)],
        compiler_params=_params("arbitrary"),
    )(z, z, dy, dy, yb2, yb2, lga, lba, wsm, bs_col, cwb, lgb, lbb)


def _dw_cols(a, dy, nb, bn, name):
    s = a.shape[0]
    tm = _tile(s, 2048)
    nt = s // tm
    cpb = 4

    def body(a_ref, dy_ref, o_ref, acc):
        t = pl.program_id(1)
        av = a_ref[...]
        for b in range(cpb):
            p = lax.dot_general(av, dy_ref[:, b * bn:(b + 1) * bn], TN_DIMS, preferred_element_type=F32)

            @pl.when(t == 0)
            def _():
                acc[b] = p

            @pl.when(t > 0)
            def _():
                acc[b] += p

        @pl.when(t == nt - 1)
        def _():
            o_ref[...] = acc[...].astype(BF16)

    return pl.pallas_call(
        body, grid=(nb // cpb, nt), name=name,
        in_specs=[pl.BlockSpec((tm, D), lambda j, t: (t, 0)), pl.BlockSpec((tm, cpb * bn), lambda j, t: (t, j))],
        out_specs=pl.BlockSpec((cpb, D, bn), lambda j, t: (j, 0, 0)),
        out_shape=jax.ShapeDtypeStruct((nb, D, bn), BF16),
        scratch_shapes=[pltpu.VMEM((cpb, D, bn), F32)],
        compiler_params=_params("arbitrary", "arbitrary"),
    )(a, dy)


def _dw_rows(a, dy, name):
    s = a.shape[0]
    tm = _tile(s, 2048)
    nt = s // tm
    rb = 512

    def body(a_ref, dy_ref, o_ref, acc):
        t = pl.program_id(1)
        p = lax.dot_general(a_ref[...], dy_ref[...].astype(BF16), TN_DIMS, preferred_element_type=F32)

        @pl.when(t == 0)
        def _():
            acc[...] = p

        @pl.when(t > 0)
        def _():
            acc[...] += p

        @pl.when(t == nt - 1)
        def _():
            o_ref[...] = acc[...].astype(BF16)

    return pl.pallas_call(
        body, grid=(D // rb, nt), name=name,
        in_specs=[pl.BlockSpec((tm, rb), lambda j, t: (t, j)), pl.BlockSpec((tm, D), lambda j, t: (t, 0))],
        out_specs=pl.BlockSpec((rb, D), lambda j, t: (j, 0)),
        out_shape=jax.ShapeDtypeStruct((D, D), BF16),
        scratch_shapes=[pltpu.VMEM((rb, D), F32)],
        compiler_params=_params("arbitrary", "arbitrary"),
    )(a, dy)


def _dw_up(h, dup, name):
    s = h.shape[0]
    tm = _tile(s, 2048)
    nt = s // tm

    def body(h_ref, d_ref, o_ref, acc):
        t = pl.program_id(1)
        p = lax.dot_general(d_ref[...], h_ref[...], TN_DIMS, preferred_element_type=F32)

        @pl.when(t == 0)
        def _():
            acc[...] = p

        @pl.when(t > 0)
        def _():
            acc[...] += p

        @pl.when(t == nt - 1)
        def _():
            o_ref[...] = acc[...].astype(BF16)

    return pl.pallas_call(
        body, grid=(NDEV, nt), name=name,
        in_specs=[pl.BlockSpec((tm, D), lambda b, t: (t, 0)),
                  pl.BlockSpec((None, None, tm, FB), lambda b, t: (b % NG, b // NG, t, 0))],
        out_specs=pl.BlockSpec((None, FB, D), lambda b, t: (b, 0, 0)),
        out_shape=jax.ShapeDtypeStruct((NDEV, FB, D), BF16),
        scratch_shapes=[pltpu.VMEM((FB, D), F32)],
        compiler_params=_params("arbitrary", "arbitrary"),
    )(h, dup)


def _dw_dn(a, df, name):
    s = df.shape[0]
    tm = _tile(s, 2048)
    nt = s // tm

    def body(a_ref, d_ref, o_ref, acc):
        t = pl.program_id(1)
        p = lax.dot_general(a_ref[...], d_ref[...].astype(BF16), TN_DIMS, preferred_element_type=F32)

        @pl.when(t == 0)
        def _():
            acc[...] = p

        @pl.when(t > 0)
        def _():
            acc[...] += p

        @pl.when(t == nt - 1)
        def _():
            o_ref[...] = acc[...].astype(BF16)

    return pl.pallas_call(
        body, grid=(NG, nt), name=name,
        in_specs=[pl.BlockSpec((None, tm, FB), lambda m, t: (m, t, 0)), pl.BlockSpec((tm, D), lambda m, t: (t, 0))],
        out_specs=pl.BlockSpec((FB, D), lambda m, t: (m, 0)),
        out_shape=jax.ShapeDtypeStruct((DFF, D), BF16),
        scratch_shapes=[pltpu.VMEM((FB, D), F32)],
        compiler_params=_params("arbitrary", "arbitrary"),
    )(a, df)


def _place():
    x, y, c = lax.axis_index("x"), lax.axis_index("y"), lax.axis_index("c")
    chips = [(1 - x, y), (x, 1 - y), (1 - x, 1 - y)]
    return x, y, c, chips


def _all_gather(shards, later, name):
    nt = len(shards)
    nl = len(later)

    def body(*refs):
        srcs, lsrcs = refs[:nt], refs[nt:nt + nl]
        dsts, ldsts = refs[nt + nl:2 * nt + nl], refs[2 * nt + nl:2 * (nt + nl)]
        send_sems, recv_sems, local_sems = refs[2 * (nt + nl):]
        x, y, c, chips = _place()
        me, sib = (x, y, c), (x, y, 1 - c)

        def blk(t, p):
            return dsts[t].at[4 * p[0] + 2 * p[1] + p[2]]

        def copy(t, k, block, to, src=None):
            return pltpu.make_async_remote_copy(
                src_ref=blk(t, block) if src is None else src, dst_ref=blk(t, block),
                send_sem=send_sems.at[t, k], recv_sem=recv_sems.at[t, k],
                device_id=to, device_id_type=MESH)

        mine = [pltpu.make_async_copy(srcs[t], blk(t, me), local_sems.at[t]) for t in range(nt)]
        mine += [pltpu.make_async_copy(lsrcs[t], ldsts[t].at[4 * x + 2 * y + c], local_sems.at[nt + t])
                 for t in range(nl)]
        for cp in mine:
            cp.start()
        first = []
        for t in range(nt):
            first.append(copy(t, 0, me, sib, src=srcs[t]))
            first += [copy(t, 1 + j, me, (*chip, c), src=srcs[t]) for j, chip in enumerate(chips)]
        for cp in first:
            cp.start()
        passed = []
        for j, chip in enumerate(chips):
            for t in range(nt):
                copy(t, 1 + j, (*chip, c), me).wait_recv()
                cp = copy(t, 4 + j, (*chip, c), sib)
                cp.start()
                passed.append(cp)
        for t in range(nt):
            copy(t, 0, sib, me).wait_recv()
            for j, chip in enumerate(chips):
                copy(t, 4 + j, (*chip, 1 - c), me).wait_recv()
        for cp in first + passed:
            cp.wait_send()
        for cp in mine:
            cp.wait()

    res = pl.pallas_call(
        body, name=name,
        in_specs=[ANY] * (nt + nl), out_specs=[ANY] * (nt + nl),
        out_shape=[jax.ShapeDtypeStruct((NDEV,) + a.shape, a.dtype) for a in list(shards) + list(later)],
        scratch_shapes=[pltpu.SemaphoreType.DMA((nt, 7)), pltpu.SemaphoreType.DMA((nt, 7)),
                        pltpu.SemaphoreType.DMA((nt + nl,))],
        compiler_params=pltpu.CompilerParams(has_side_effects=True),
    )(*shards, *later)
    return res[:nt], res[nt:]


HBM_SPEC = pl.BlockSpec(memory_space=pltpu.HBM)
SEM_SPEC = pl.BlockSpec(memory_space=pltpu.SEMAPHORE)
DATAFLOW = pltpu.SideEffectType.DATAFLOW_SIDE_EFFECTING


def _hbm(a):
    return pltpu.with_memory_space_constraint(a, pltpu.HBM)


def _hbm_like(arrs):
    return [pltpu.HBM(a.shape, a.dtype) for a in arrs]


def _ag_start(srcs, lands, groups, name):
    del groups
    n = len(srcs)
    ns = 8 * n

    def body(*refs):
        src, land = refs[:n], refs[n:2 * n]
        sems = refs[2 * n:2 * n + ns]
        token = refs[-1]
        x, y, c, chips = _place()
        peers = [(x, y, 1 - c)] + [(*chip, c) for chip in chips]
        for t in range(n):
            for k, to in enumerate(peers):
                pltpu.make_async_remote_copy(
                    src_ref=src[t], dst_ref=land[t].at[4 * x + 2 * y + c],
                    send_sem=sems[2 * (4 * t + k)], recv_sem=sems[2 * (4 * t + k) + 1],
                    device_id=to, device_id_type=MESH).start()
        token[...] = jnp.zeros_like(token)

    res = pl.pallas_call(
        body, name=name,
        in_specs=[HBM_SPEC] * (2 * n),
        out_specs=[SEM_SPEC] * ns + [HBM_SPEC] * (2 * n) + [pl.BlockSpec(memory_space=pltpu.VMEM)],
        out_shape=[pltpu.SemaphoreType.DMA(())] * ns + _hbm_like(srcs) + _hbm_like(lands)
        + [jax.ShapeDtypeStruct((8, 128), F32)],
        input_output_aliases={i: ns + i for i in range(2 * n)},
        compiler_params=pltpu.CompilerParams(has_side_effects=DATAFLOW),
    )(*[_hbm(a) for a in srcs], *[_hbm(a) for a in lands])
    sems = [[(res[2 * (4 * t + k)], res[2 * (4 * t + k) + 1]) for k in range(4)] for t in range(n)]
    return sems, res[ns:ns + n], res[ns + n:ns + 2 * n], res[-1]


def _ag_forward(srcs, lands, sems1, after, name):
    n = len(srcs)
    flat1 = [s for t in range(n) for k in range(1, 4) for s in sems1[t][k]]
    n1 = len(flat1)

    def body(*refs):
        src, land = refs[:n], refs[n:2 * n]
        s1 = refs[2 * n:2 * n + n1]
        s2 = refs[2 * n + n1 + 1:2 * n + n1 + 1 + 6 * n]
        x, y, c, chips = _place()
        for j, (cx, cy) in enumerate(chips):
            for t in range(n):
                blk = land[t].at[4 * cx + 2 * cy + c]
                pltpu.make_async_remote_copy(
                    src_ref=src[t], dst_ref=blk, send_sem=s1[2 * (3 * t + j)], recv_sem=s1[2 * (3 * t + j) + 1],
                    device_id=(cx, cy, c), device_id_type=MESH).wait_recv()
                pltpu.make_async_remote_copy(
                    src_ref=blk, dst_ref=blk, send_sem=s2[2 * (3 * t + j)], recv_sem=s2[2 * (3 * t + j) + 1],
                    device_id=(x, y, 1 - c), device_id_type=MESH).start()

    res = pl.pallas_call(
        body, name=name,
        in_specs=[HBM_SPEC] * (2 * n) + [SEM_SPEC] * n1 + [ANY],
        out_specs=[SEM_SPEC] * (6 * n) + [HBM_SPEC] * n,
        out_shape=[pltpu.SemaphoreType.DMA(())] * (6 * n) + _hbm_like(lands),
        input_output_aliases={n + i: 6 * n + i for i in range(n)},
        compiler_params=pltpu.CompilerParams(has_side_effects=DATAFLOW),
    )(*srcs, *lands, *flat1, after)
    sems2 = [[(res[2 * (3 * t + j)], res[2 * (3 * t + j) + 1]) for j in range(3)] for t in range(n)]
    return sems2, res[6 * n:]


def _ag_finish(srcs, lands, sems1, sems2, after, name):
    n = len(srcs)
    flat1 = [s for t in range(n) for k in range(4) for s in sems1[t][k]]
    flat2 = [s for t in range(n) for j in range(3) for s in sems2[t][j]]
    n1, n2 = len(flat1), len(flat2)

    def body(*refs):
        src, land = refs[:n], refs[n:2 * n]
        s1 = refs[2 * n:2 * n + n1]
        s2 = refs[2 * n + n1:2 * n + n1 + n2]
        x, y, c, chips = _place()
        sib = (x, y, 1 - c)
        for t in range(n):
            own = land[t].at[4 * x + 2 * y + 1 - c]
            pltpu.make_async_remote_copy(
                src_ref=src[t], dst_ref=own, send_sem=s1[8 * t], recv_sem=s1[8 * t + 1],
                device_id=sib, device_id_type=MESH).wait_recv()
            for k in range(4):
                pltpu.make_async_remote_copy(
                    src_ref=src[t], dst_ref=own, send_sem=s1[2 * (4 * t + k)], recv_sem=s1[2 * (4 * t + k) + 1],
                    device_id=sib, device_id_type=MESH).wait_send()
            for j, (cx, cy) in enumerate(chips):
                blk = land[t].at[4 * cx + 2 * cy + 1 - c]
                cp = pltpu.make_async_remote_copy(
                    src_ref=blk, dst_ref=blk, send_sem=s2[2 * (3 * t + j)], recv_sem=s2[2 * (3 * t + j) + 1],
                    device_id=sib, device_id_type=MESH)
                cp.wait_send()
                cp.wait_recv()

    return pl.pallas_call(
        body, name=name,
        in_specs=[HBM_SPEC] * (2 * n) + [SEM_SPEC] * (n1 + n2) + [ANY],
        out_specs=[HBM_SPEC] * n,
        out_shape=_hbm_like(lands),
        input_output_aliases={n + i: i for i in range(n)},
        compiler_params=pltpu.CompilerParams(has_side_effects=DATAFLOW),
    )(*srcs, *lands, *flat1, *flat2, after)


def _pair_exchange(grads, name):
    nt = len(grads)

    def body(*refs):
        srcs, dsts = refs[:nt], refs[nt:2 * nt]
        send_sems, recv_sems = refs[2 * nt:]
        x, y, c, _ = _place()
        copies = []
        for t in range(nt):
            for j in range(NCHIP):
                copies.append(pltpu.make_async_remote_copy(
                    src_ref=srcs[t].at[2 * j + 1 - c], dst_ref=dsts[t].at[j],
                    send_sem=send_sems.at[t, j], recv_sem=recv_sems.at[t, j],
                    device_id=(x, y, 1 - c), device_id_type=MESH))
        for cp in copies:
            cp.start()
        for cp in copies:
            cp.wait()

    return pl.pallas_call(
        body, name=name,
        in_specs=[ANY] * nt, out_specs=[ANY] * nt,
        out_shape=[jax.ShapeDtypeStruct((NCHIP,) + a.shape[1:], a.dtype) for a in grads],
        scratch_shapes=[pltpu.SemaphoreType.DMA((nt, NCHIP)), pltpu.SemaphoreType.DMA((nt, NCHIP))],
        compiler_params=pltpu.CompilerParams(has_side_effects=True),
    )(*grads)


def _pair_sum(own, got, cidx, name):
    _, _, r, cdim = own.shape
    tr = r
    for cand in (512, 256, 128, 64, 32, 16):
        if r % cand == 0:
            tr = cand
            break

    def body(c_ref, a_ref, b_ref, o_ref):
        o_ref[...] = (a_ref[...].astype(F32) + b_ref[...].astype(F32)).astype(BF16)

    return pl.pallas_call(
        body, name=name,
        grid_spec=pltpu.PrefetchScalarGridSpec(
            num_scalar_prefetch=1, grid=(NCHIP, r // tr),
            in_specs=[pl.BlockSpec((None, None, tr, cdim), lambda j, i, c_ref: (j, c_ref[0], i, 0)),
                      pl.BlockSpec((None, tr, cdim), lambda j, i, c_ref: (j, i, 0))],
            out_specs=pl.BlockSpec((None, tr, cdim), lambda j, i, c_ref: (j, i, 0))),
        out_shape=jax.ShapeDtypeStruct((NCHIP, r, cdim), BF16),
        compiler_params=_params("arbitrary", "arbitrary"),
    )(cidx, own, got)


def _chip_scatter(sums, layout, name):
    nt = len(sums)
    nout = max(o for o, _ in layout) + 1
    out_shapes = [None] * nout
    for t, (o, l) in enumerate(layout):
        r, cdim = sums[t].shape[1:]
        nl = max(ll for oo, ll in layout if oo == o) if l is not None else None
        out_shapes[o] = (4, r, cdim) if l is None else (4, nl + 1, r, cdim)

    def body(*refs):
        srcs, dsts = refs[:nt], refs[nt:nt + nout]
        send_sems, recv_sems, local_sems = refs[nt + nout:]
        x, y, c, chips = _place()

        def slot(t, k):
            o, l = layout[t]
            return dsts[o].at[k] if l is None else dsts[o].at[k, l]

        local = [pltpu.make_async_copy(srcs[t].at[2 * x + y], slot(t, 3), local_sems.at[t]) for t in range(nt)]
        for cp in local:
            cp.start()
        copies = []
        for t in range(nt):
            for k, (cx, cy) in enumerate(chips):
                copies.append(pltpu.make_async_remote_copy(
                    src_ref=srcs[t].at[2 * cx + cy], dst_ref=slot(t, k),
                    send_sem=send_sems.at[t, k], recv_sem=recv_sems.at[t, k],
                    device_id=(cx, cy, c), device_id_type=MESH))
        for cp in copies:
            cp.start()
        for cp in copies:
            cp.wait()
        for cp in local:
            cp.wait()

    return pl.pallas_call(
        body, name=name,
        in_specs=[ANY] * nt, out_specs=[ANY] * nout,
        out_shape=[jax.ShapeDtypeStruct(sh, BF16) for sh in out_shapes],
        scratch_shapes=[pltpu.SemaphoreType.DMA((nt, 3)), pltpu.SemaphoreType.DMA((nt, 3)),
                        pltpu.SemaphoreType.DMA((nt,))],
        compiler_params=pltpu.CompilerParams(has_side_effects=True),
    )(*sums)


def _small_allreduce(parts, name):
    nt = len(parts)

    def body(*refs):
        srcs, outs, bufs = refs[:nt], refs[nt:2 * nt], refs[2 * nt:3 * nt]
        send_sems, recv_sems = refs[3 * nt:]
        x, y, c, _ = _place()
        peers = [(x, y, 1 - c), (1 - x, y, c), (x, 1 - y, c)]
        for t in range(nt):
            outs[t][...] = srcs[t][...]
        for step, peer in enumerate(peers):
            copies = [pltpu.make_async_remote_copy(
                src_ref=outs[t], dst_ref=bufs[t].at[step],
                send_sem=send_sems.at[step, t], recv_sem=recv_sems.at[step, t],
                device_id=peer, device_id_type=MESH) for t in range(nt)]
            for cp in copies:
                cp.start()
            for cp in copies:
                cp.wait()
            for t in range(nt):
                outs[t][...] = outs[t][...] + bufs[t][step]

    vm = pl.BlockSpec(memory_space=pltpu.VMEM)
    return pl.pallas_call(
        body, name=name,
        in_specs=[vm] * nt, out_specs=[vm] * nt,
        out_shape=[jax.ShapeDtypeStruct(a.shape, F32) for a in parts],
        scratch_shapes=[pltpu.VMEM((3,) + a.shape, F32) for a in parts]
        + [pltpu.SemaphoreType.DMA((3, nt)), pltpu.SemaphoreType.DMA((3, nt))],
        compiler_params=pltpu.CompilerParams(has_side_effects=True, vmem_limit_bytes=VMEM_LIMIT),
    )(*parts)


def _adam_math(w, g, m, v):
    m2 = ADAM_B1 * m + (1.0 - ADAM_B1) * g
    v2 = ADAM_B2 * v + (1.0 - ADAM_B2) * (g * g)
    m_hat = m2 / (1.0 - ADAM_B1 ** ADAM_STEP)
    v_hat = v2 / (1.0 - ADAM_B2 ** ADAM_STEP)
    delta = -ADAM_LR * (m_hat / (jnp.sqrt(v_hat) + ADAM_EPS) + ADAM_WD * w)
    return delta, m2, v2


def _adam_big(w, m, v, parts, name):
    nl, r, cdim = w.shape
    tr = r
    for cand in (256, 128, 64, 32, 16, 8):
        if r % cand == 0:
            tr = cand
            break

    def body(w_ref, m_ref, v_ref, p_ref, g_ref, d_ref, mo_ref, vo_ref):
        g = ((p_ref[0].astype(F32) + p_ref[1].astype(F32)) + p_ref[2].astype(F32)) + p_ref[3].astype(F32)
        delta, m2, v2 = _adam_math(w_ref[...], g, m_ref[...], v_ref[...])
        g_ref[...] = g
        d_ref[...] = delta
        mo_ref[...] = m2
        vo_ref[...] = v2

    spec = pl.BlockSpec((None, tr, cdim), lambda l, i: (l, i, 0))
    return pl.pallas_call(
        body, grid=(nl, r // tr), name=name,
        in_specs=[spec, spec, spec, pl.BlockSpec((4, None, tr, cdim), lambda l, i: (0, l, i, 0))],
        out_specs=[spec] * 4,
        out_shape=[jax.ShapeDtypeStruct(w.shape, F32)] * 4,
        compiler_params=_params("parallel", "parallel"),
    )(w, m, v, parts)


def _adam_small(ws, gs, ms, vs, name):
    n = len(ws)

    def body(*refs):
        w_r, g_r, m_r, v_r = refs[:n], refs[n:2 * n], refs[2 * n:3 * n], refs[3 * n:4 * n]
        d_o, m_o, v_o = refs[4 * n:5 * n], refs[5 * n:6 * n], refs[6 * n:7 * n]
        for t in range(n):
            delta, m2, v2 = _adam_math(w_r[t][...], g_r[t][...], m_r[t][...], v_r[t][...])
            d_o[t][...] = delta
            m_o[t][...] = m2
            v_o[t][...] = v2

    vm = pl.BlockSpec(memory_space=pltpu.VMEM)
    shapes = [jax.ShapeDtypeStruct(a.shape, F32) for a in ws]
    return pl.pallas_call(
        body, name=name, in_specs=[vm] * (4 * n), out_specs=[vm] * (3 * n), out_shape=shapes * 3,
        compiler_params=pltpu.CompilerParams(vmem_limit_bytes=VMEM_LIMIT),
    )(*ws, *gs, *ms, *vs)


def kernel(x, norm_mix, norm_ffn, norm_final, ab_w_in, a_ln_g, a_ln_b, a_w_s, a_b_s, b_conv_w, b_conv_b, b_ln_g, b_ln_b, ab_w_out, c_w_in, c_conv_w, c_w_out, f_w_up, f_conv_w, f_w_down, loss_target, m_norm_mix, m_norm_ffn, m_norm_final, m_ab_w_in, m_a_ln_g, m_a_ln_b, m_a_w_s, m_a_b_s, m_b_conv_w, m_b_conv_b, m_b_ln_g, m_b_ln_b, m_ab_w_out, m_c_w_in, m_c_conv_w, m_c_w_out, m_f_w_up, m_f_conv_w, m_f_w_down, v_norm_mix, v_norm_ffn, v_norm_final, v_ab_w_in, v_a_ln_g, v_a_ln_b, v_a_w_s, v_a_b_s, v_b_conv_w, v_b_conv_b, v_b_ln_g, v_b_ln_b, v_ab_w_out, v_c_w_in, v_c_conv_w, v_c_w_out, v_f_w_up, v_f_conv_w, v_f_w_down):
    s = x.shape[1]
    x0 = x.reshape(s, D)
    tgt = loss_target.reshape(s, D)
    xi, yi, ci = lax.axis_index("x"), lax.axis_index("y"), lax.axis_index("c")
    dev = 4 * xi + 2 * yi + ci
    cidx = ci.astype(jnp.int32).reshape(1)

    bf = lambda a: a.astype(BF16)
    now = [bf(ab_w_in[0]), bf(ab_w_out[0]), b_conv_w[0], c_conv_w[0], f_conv_w.reshape(6, FB)]
    later = [bf(f_w_up[0]), bf(f_w_down[0]), bf(c_w_in[0]), bf(c_w_out[0]), bf(f_w_up[1]), bf(f_w_down[1])]
    (win0, wout0, bcw_g, ccw_g, fcw_g), lands = _all_gather(now, later, "all_gather_first")
    groups = [[0, 1], [2, 3], [4, 5]]
    ag_sems, later, lands, ag_token = _ag_start(later, lands, groups, "ag_start")
    wout0 = wout0.reshape(D, D)
    bcw = jnp.transpose(bcw_g, (1, 0, 2)).reshape(BCONV, DA)
    ccw = jnp.transpose(ccw_g, (1, 0, 2)).reshape(3, D)
    fcw_g = fcw_g.reshape(2, NG, 2, 3, FB)
    fcws = [fcw_g[:, :, 0], fcw_g[:, :, 1]]

    causal = jnp.tril(jnp.ones((CHUNK, CHUNK), F32))
    wsm = (a_w_s[0] * causal).astype(BF16)
    bs_col = a_b_s.reshape(HEADS, CHUNK, 1)
    nm = [norm_mix[0:1], norm_mix[1:2]]
    nf = [norm_ffn[0:1], norm_ffn[1:2]]
    nfin = norm_final.reshape(1, D)

    def arrive(g, after_ici, after_d2d, tag):
        srcs = [later[t] for t in groups[g]]
        zone = [lands[t] for t in groups[g]]
        sems1 = [ag_sems[t] for t in groups[g]]
        sems2, zone = _ag_forward(srcs, zone, sems1, after_ici, "ag_forward_" + tag)
        return _ag_finish(srcs, zone, sems1, sems2, after_d2d, "ag_finish_" + tag)

    h0 = _rms_fwd(x0, nm[0], "rms_mix0", after=ag_token)
    z = _mm_in(h0, win0, "mm_ab_in")
    ycat, yb2 = _ab_fwd(z, a_ln_g, a_ln_b, wsm, bs_col, bcw, b_conv_b, b_ln_g, b_ln_b, "ab_fwd")
    x1 = _mm_out(ycat, wout0, x0, "mm_ab_out")
    h1 = _rms_fwd(x1, nf[0], "rms_ffn0")
    wup0, wdn0 = arrive(0, x1, h1, "ffn0")
    up0, x2 = _ffn_fwd(h1, x1, wup0.reshape(2, NG, D, FB), fcws[0], wdn0.reshape(DFF, D), "ffn_fwd0")
    h2 = _rms_fwd(x2, nm[1], "rms_mix1")
    cin, cout = arrive(1, x2, h2, "c")
    cout = cout.reshape(D, D)
    zc = _mm_in(h2, cin, "mm_c_in")
    rc = _c_fwd(zc, ccw, "c_fwd")
    x3 = _mm_out(rc, cout, x2, "mm_c_out")
    h3 = _rms_fwd(x3, nf[1], "rms_ffn1")
    wup1, wdn1 = arrive(2, x3, h3, "ffn1")
    wups = [wup0.reshape(2, NG, D, FB), wup1.reshape(2, NG, D, FB)]
    wdns = [wdn0.reshape(DFF, D), wdn1.reshape(DFF, D)]
    up1, x4 = _ffn_fwd(h3, x3, wups[1], fcws[1], wdns[1], "ffn_fwd1")
    dx4, dnfin, loss_part = _final(x4, tgt, nfin, "final_loss")
    loss = lax.psum(loss_part[0, 0], ("x", "y", "c"))

    a1, dup1, dx3, dnf1, dfcw1 = _ffn_bwd(dx4, up1, wups[1], fcws[1], wdns[1], x3, nf[1], "ffn_bwd1")
    g_wdn1 = _dw_dn(a1, dx4, "dw_dn1")
    g_wup1 = _dw_up(h3, dup1, "dw_up1")
    drc = _mm_nt(dx3, cout, "mm_c_out_bwd")
    g_cout = _dw_rows(rc, dx3, "dw_c_out")
    dzc, dccw = _c_bwd(drc, zc, ccw, "c_bwd")
    dx2, dnm1 = _mm_nt_rms(dzc, cin, x2, nm[1], dx3, "mm_c_in_bwd")
    g_cin = _dw_cols(h2, dzc, NDEV, 3 * D // NDEV, "dw_c_in")
    a0, dup0, dx1, dnf0, dfcw0 = _ffn_bwd(dx2, up0, wups[0], fcws[0], wdns[0], x1, nf[0], "ffn_bwd0")
    g_wdn0 = _dw_dn(a0, dx2, "dw_dn0")
    g_wup0 = _dw_up(h1, dup0, "dw_up0")
    dycat = _mm_nt(dx1, wout0, "mm_ab_out_bwd")
    g_wout0 = _dw_rows(ycat, dx1, "dw_ab_out")
    dz, g512, dws, dbs = _ab_bwd(dycat, z, yb2, a_ln_g, a_ln_b, wsm, bs_col, bcw, b_ln_g, b_ln_b, "ab_bwd")
    grad_x, dnm0 = _mm_nt_rms(dz, win0, x0, nm[0], dx1, "mm_ab_in_bwd")
    g_win0 = _dw_cols(h0, dz, NDEV, 2 * D // NDEV, "dw_ab_in")

    big = [g_win0, g_wout0.reshape(NDEV, D // NDEV, D), g_cin, g_cout.reshape(NDEV, D // NDEV, D),
           g_wup0, g_wup1, g_wdn0.reshape(NDEV, DFF // NDEV, D), g_wdn1.reshape(NDEV, DFF // NDEV, D)]
    got = _pair_exchange(big, "rs_pair_exchange")
    sums = [_pair_sum(b.reshape((NCHIP, 2) + b.shape[1:]), g, cidx, "rs_pair_sum%d" % t)
            for t, (b, g) in enumerate(zip(big, got))]
    layout = [(0, None), (1, None), (2, None), (3, None), (4, 0), (4, 1), (5, 0), (5, 1)]
    p_win0, p_wout0, p_cin, p_cout, p_wup, p_wdn = _chip_scatter(sums, layout, "rs_chip_scatter")

    g1024 = jnp.concatenate([dnm0, dnm1, dnf0, dnf1, dnfin, dccw], axis=0)
    gfc = jnp.concatenate([dfcw0, dfcw1], axis=0).reshape(2 * NG * 2 * 3, FB)
    g1024, g512, dws, dbs, gfc = _small_allreduce(
        [g1024, g512, dws.reshape(HEADS * CHUNK, CHUNK), dbs.reshape(HEADS, CHUNK), gfc], "small_allreduce")

    def big_update(w, m, v, parts, name):
        shp = w.shape
        w3, m3, v3 = (a.reshape((-1,) + shp[-2:]) for a in (w, m, v))
        p4 = parts.reshape((4,) + w3.shape)
        return [o.reshape(shp) for o in _adam_big(w3, m3, v3, p4, name)]

    u_win0 = big_update(ab_w_in, m_ab_w_in, v_ab_w_in, p_win0, "adam_ab_w_in")
    u_wout0 = big_update(ab_w_out, m_ab_w_out, v_ab_w_out, p_wout0, "adam_ab_w_out")
    u_cin = big_update(c_w_in, m_c_w_in, v_c_w_in, p_cin, "adam_c_w_in")
    u_cout = big_update(c_w_out, m_c_w_out, v_c_w_out, p_cout, "adam_c_w_out")
    tr_ = lambda a: jnp.swapaxes(a, 1, 2)
    u_wup = [tr_(o) for o in big_update(tr_(f_w_up), tr_(m_f_w_up), tr_(v_f_w_up), p_wup, "adam_f_w_up")]
    u_wdn = big_update(f_w_down, m_f_w_down, v_f_w_down, p_wdn, "adam_f_w_down")

    g_norm_mix = g1024[0:2]
    g_norm_ffn = g1024[2:4]
    g_norm_final = g1024[4:5]
    g_ccw = lax.dynamic_slice(g1024[5:8], (0, dev * (D // NDEV)), (3, D // NDEV))
    g_bcw = lax.dynamic_slice(g512[8:8 + BCONV], (0, dev * (DA // NDEV)), (BCONV, DA // NDEV))
    gfc = gfc.reshape(2, NG, 2, 3, FB)
    g_fcw = lax.dynamic_slice(gfc, (0, dev % NG, dev // NG, 0, 0), (2, 1, 1, 3, FB)).reshape(2, 3, FB)
    small_w = [norm_mix, norm_ffn, nfin, a_ln_g, a_ln_b, a_w_s[0], a_b_s[0], b_conv_w[0], b_conv_b,
               b_ln_g, b_ln_b, c_conv_w[0], f_conv_w]
    small_g = [g_norm_mix, g_norm_ffn, g_norm_final, g512[0:1], g512[1:2],
               dws.reshape(HEADS, CHUNK, CHUNK), dbs, g_bcw, g512[2:3],
               g512[3:4], g512[4:5], g_ccw, g_fcw]
    small_m = [m_norm_mix, m_norm_ffn, m_norm_final.reshape(1, D), m_a_ln_g, m_a_ln_b, m_a_w_s[0], m_a_b_s[0],
               m_b_conv_w[0], m_b_conv_b, m_b_ln_g, m_b_ln_b, m_c_conv_w[0], m_f_conv_w]
    small_v = [v_norm_mix, v_norm_ffn, v_norm_final.reshape(1, D), v_a_ln_g, v_a_ln_b, v_a_w_s[0], v_a_b_s[0],
               v_b_conv_w[0], v_b_conv_b, v_b_ln_g, v_b_ln_b, v_c_conv_w[0], v_f_conv_w]
    upd = _adam_small(small_w, small_g, small_m, small_v, "adam_small")
    ns = len(small_w)
    orig = [norm_mix, norm_ffn, norm_final, a_ln_g, a_ln_b, a_w_s, a_b_s, b_conv_w, b_conv_b,
            b_ln_g, b_ln_b, c_conv_w, f_conv_w]
    sg_out = [g.reshape(o.shape) for g, o in zip(small_g, orig)]
    sd_out = [a.reshape(o.shape) for a, o in zip(upd[0:ns], orig)]
    sm_out = [a.reshape(o.shape) for a, o in zip(upd[ns:2 * ns], orig)]
    sv_out = [a.reshape(o.shape) for a, o in zip(upd[2 * ns:3 * ns], orig)]

    def assemble(small, k):
        return [small[0], small[1], small[2], u_win0[k], small[3], small[4], small[5], small[6], small[7],
                small[8], small[9], small[10], u_wout0[k], u_cin[k], small[11], u_cout[k], u_wup[k],
                small[12], u_wdn[k]]

    grads = assemble(sg_out, 0)
    deltas = assemble(sd_out, 1)
    new_m = assemble(sm_out, 2)
    new_v = assemble(sv_out, 3)
    return (loss, grad_x.reshape(1, s, D), *grads, *deltas, *new_m, *new_v)
```

```python
import functools
import math

import jax
import jax.numpy as jnp
from jax import lax
from jax.experimental import pallas as pl
from jax.experimental.pallas import tpu as pltpu

F32 = jnp.float32
BF16 = jnp.bfloat16

D = 1024
DA = 512
HEADS = 4
CHUNK = 128
DFF = 2816
NDEV = 8
NCHIP = 4
FB = DFF * 2 // NDEV
NG = DFF // FB
BCONV = 31
EPS = 1e-6
HALO = 16
HALO_B = 32
VMEM_LIMIT = 52 * 1024 * 1024
INV_SQRT2 = 1.0 / math.sqrt(2.0)
INV_SQRT_2PI = 1.0 / math.sqrt(2.0 * math.pi)

ADAM_LR = 0.001
ADAM_B1 = 0.9
ADAM_B2 = 0.999
ADAM_EPS = 1e-08
ADAM_WD = 0.01
ADAM_STEP = 10

MESH = pl.DeviceIdType.MESH
ANY = pl.BlockSpec(memory_space=pl.ANY)
NT_DIMS = (((1,), (1,)), ((), ()))
TN_DIMS = (((0,), (0,)), ((), ()))


def _params(*sem):
    return pltpu.CompilerParams(dimension_semantics=sem, vmem_limit_bytes=VMEM_LIMIT)


def _tile(s, want):
    return min(want, s)


def _sigmoid(x):
    return jax.nn.sigmoid(x)


def _dsilu(x, sg):
    return sg * (1.0 + x * (1.0 - sg))


def _gelu(x):
    return 0.5 * x * (1.0 + lax.erf(x * INV_SQRT2))


def _dgelu(x):
    return 0.5 * (1.0 + lax.erf(x * INV_SQRT2)) + x * jnp.exp(-0.5 * x * x) * INV_SQRT_2PI


def _ln_fwd(x, g, b):
    mu = jnp.mean(x, axis=-1, keepdims=True)
    xc = x - mu
    var = jnp.mean(xc * xc, axis=-1, keepdims=True)
    rstd = lax.rsqrt(var + EPS)
    xhat = xc * rstd
    return xhat * g + b, xhat, rstd


def _ln_bwd(dy, xhat, rstd, g):
    dxh = dy * g
    m1 = jnp.mean(dxh, axis=-1, keepdims=True)
    m2 = jnp.mean(dxh * xhat, axis=-1, keepdims=True)
    return rstd * (dxh - m1 - xhat * m2)


def _rms_bwd_math(dh, x, g):
    r = lax.rsqrt(jnp.mean(x * x, axis=-1, keepdims=True) + EPS)
    xhat = x * r
    dg = jnp.sum(dh * xhat, axis=0, keepdims=True)
    u = dh * g
    dx = r * (u - xhat * jnp.mean(u * xhat, axis=-1, keepdims=True))
    return dx, dg


def _conv3(xe, cw, halo):
    x0 = xe[halo:]
    x1 = pltpu.roll(xe, 1, 0)[halo:]
    x2 = pltpu.roll(xe, 2, 0)[halo:]
    return cw[2] * x0 + cw[1] * x1 + cw[0] * x2, (x0, x1, x2)


def _conv3_bwd_in(dce, cw, ts):
    n = dce.shape[0]
    d1 = pltpu.roll(dce, n - 1, 0)[:ts]
    d2 = pltpu.roll(dce, n - 2, 0)[:ts]
    return cw[2] * dce[:ts] + cw[1] * d1 + cw[0] * d2


def _conv3_bwd_w(dc, taps):
    x0, x1, x2 = taps
    return [jnp.sum(dc * x2, axis=0, keepdims=True), jnp.sum(dc * x1, axis=0, keepdims=True),
            jnp.sum(dc * x0, axis=0, keepdims=True)]


def _rms_fwd(x, g, name, after=None):
    s = x.shape[0]
    ts = _tile(s, 512)

    def body(x_ref, g_ref, *rest):
        h_ref = rest[-1]
        xv = x_ref[...]
        r = lax.rsqrt(jnp.mean(xv * xv, axis=-1, keepdims=True) + EPS)
        h_ref[...] = (xv * r * g_ref[...]).astype(BF16)

    extra = [] if after is None else [after]
    return pl.pallas_call(
        body, grid=(s // ts,), name=name,
        in_specs=[pl.BlockSpec((ts, D), lambda i: (i, 0)), pl.BlockSpec((1, D), lambda i: (0, 0))]
        + [ANY] * len(extra),
        out_specs=pl.BlockSpec((ts, D), lambda i: (i, 0)),
        out_shape=jax.ShapeDtypeStruct((s, D), BF16),
        compiler_params=_params("parallel"),
    )(x, g, *extra)


def _mm_in(h, wblk, name):
    s = h.shape[0]
    nb, _, bn = wblk.shape
    ts = _tile(s, 512)

    def body(h_ref, w_ref, o_ref):
        hv = h_ref[...]
        for b in range(nb):
            o_ref[:, b * bn:(b + 1) * bn] = jnp.dot(hv, w_ref[b], preferred_element_type=F32).astype(BF16)

    return pl.pallas_call(
        body, grid=(s // ts,), name=name,
        in_specs=[pl.BlockSpec((ts, D), lambda i: (i, 0)), pl.BlockSpec((nb, D, bn), lambda i: (0, 0, 0))],
        out_specs=pl.BlockSpec((ts, nb * bn), lambda i: (i, 0)),
        out_shape=jax.ShapeDtypeStruct((s, nb * bn), BF16),
        compiler_params=_params("parallel"),
    )(h, wblk)


def _mm_out(y, w, xres, name):
    s = y.shape[0]
    ts = _tile(s, 512)

    def body(y_ref, w_ref, x_ref, o_ref):
        o_ref[...] = x_ref[...] + jnp.dot(y_ref[...], w_ref[...], preferred_element_type=F32)

    return pl.pallas_call(
        body, grid=(s // ts,), name=name,
        in_specs=[pl.BlockSpec((ts, D), lambda i: (i, 0)), pl.BlockSpec((D, D), lambda i: (0, 0)),
                  pl.BlockSpec((ts, D), lambda i: (i, 0))],
        out_specs=pl.BlockSpec((ts, D), lambda i: (i, 0)),
        out_shape=jax.ShapeDtypeStruct((s, D), F32),
        compiler_params=_params("parallel"),
    )(y, w, xres)


def _conv31(ue, cw_ref, ts):
    acc = jnp.zeros((ts, ue.shape[1]), F32)
    for r in range(8):
        rolled = ue if r == 0 else pltpu.roll(ue, r, 0)
        for q in range(4):
            sh = 8 * q + r
            if sh >= BCONV:
                continue
            k = BCONV - 1 - sh
            acc = acc + cw_ref[k:k + 1, :] * rolled[HALO_B - 8 * q:HALO_B - 8 * q + ts]
    return acc


def _ab_fwd(z, lga, lba, wsm, bs_col, cwb, cbb, lgb, lbb, name):
    s = z.shape[0]
    ts = _tile(s, 256)
    hb = ts // HALO_B

    def body(z_ref, zh_ref, lga_ref, lba_ref, ws_ref, bs_ref, cw_ref, cb_ref, lgb_ref, lbb_ref,
             y_ref, yb2_ref):
        i = pl.program_id(0)
        z_t = z_ref[...].astype(F32)
        gu = _gelu(z_t[:, 0:DA])
        gv = _gelu(z_t[:, DA:2 * DA])
        vn, _, _ = _ln_fwd(gv, lga_ref[...], lba_ref[...])
        vnb = vn.astype(BF16)
        for c in range(ts // CHUNK):
            for h in range(HEADS):
                rs = slice(c * CHUNK, (c + 1) * CHUNK)
                cs = slice(h * CHUNK, (h + 1) * CHUNK)
                mixed = jnp.dot(ws_ref[h], vnb[rs, cs], preferred_element_type=F32) + bs_ref[h]
                y_ref[rs, cs] = (gu[rs, cs] * mixed).astype(BF16)
        zh = jnp.where(i > 0, zh_ref[...], jnp.zeros_like(zh_ref[...])).astype(F32)
        xb = jnp.concatenate([zh[:, 0:DA], z_t[:, 2 * DA:3 * DA]], axis=0)
        gb = jnp.concatenate([zh[:, DA:2 * DA], z_t[:, 3 * DA:4 * DA]], axis=0)
        u = xb * _sigmoid(gb)
        conv = _conv31(u, cw_ref, ts) + cb_ref[...]
        yb2_ref[...] = conv
        nb_, _, _ = _ln_fwd(conv, lgb_ref[...], lbb_ref[...])
        y_ref[:, DA:2 * DA] = (nb_ * _sigmoid(nb_)).astype(BF16)

    row = lambda i: (0, 0)
    return pl.pallas_call(
        body, grid=(s // ts,), name=name,
        in_specs=[pl.BlockSpec((ts, 4 * DA), lambda i: (i, 0)),
                  pl.BlockSpec((HALO_B, 2 * DA), lambda i: (jnp.maximum(i * hb - 1, 0), 1)),
                  pl.BlockSpec((1, DA), row), pl.BlockSpec((1, DA), row),
                  pl.BlockSpec((HEADS, CHUNK, CHUNK), lambda i: (0, 0, 0)),
                  pl.BlockSpec((HEADS, CHUNK, 1), lambda i: (0, 0, 0)),
                  pl.BlockSpec((BCONV, DA), row), pl.BlockSpec((1, DA), row),
                  pl.BlockSpec((1, DA), row), pl.BlockSpec((1, DA), row)],
        out_specs=[pl.BlockSpec((ts, 2 * DA), lambda i: (i, 0)), pl.BlockSpec((ts, DA), lambda i: (i, 0))],
        out_shape=[jax.ShapeDtypeStruct((s, 2 * DA), BF16), jax.ShapeDtypeStruct((s, DA), F32)],
        compiler_params=_params("parallel"),
    )(z, z, lga, lba, wsm, bs_col, cwb, cbb, lgb, lbb)


def _c_fwd(zc, cw, name):
    s = zc.shape[0]
    ts = _tile(s, 512)
    hb = ts // HALO

    def body(z_ref, ch_ref, xh_ref, cw_ref, r_ref):
        i = pl.program_id(0)
        z_t = z_ref[...].astype(F32)
        ph = jnp.where(i > 0, ch_ref[...].astype(F32) * xh_ref[...].astype(F32), 0.0)
        pe = jnp.concatenate([ph, z_t[:, D:2 * D] * z_t[:, 2 * D:3 * D]], axis=0)
        q, _ = _conv3(pe, [cw_ref[k:k + 1, :] for k in range(3)], HALO)
        r_ref[...] = (z_t[:, 0:D] * q).astype(BF16)

    halo = lambda col: pl.BlockSpec((HALO, D), lambda i: (jnp.maximum(i * hb - 1, 0), col))
    return pl.pallas_call(
        body, grid=(s // ts,), name=name,
        in_specs=[pl.BlockSpec((ts, 3 * D), lambda i: (i, 0)), halo(1), halo(2),
                  pl.BlockSpec((3, D), lambda i: (0, 0))],
        out_specs=pl.BlockSpec((ts, D), lambda i: (i, 0)),
        out_shape=jax.ShapeDtypeStruct((s, D), BF16),
        compiler_params=_params("parallel"),
    )(zc, zc, zc, cw)


def _ffn_fwd(h, xres, wup, fcw, wdn, name):
    s = h.shape[0]
    ts = _tile(s, 512)
    hb = ts // HALO

    def body(h_ref, hh_ref, w_ref, cw_ref, wd_ref, x_ref, up_ref, xo_ref):
        i = pl.program_id(0)
        m = pl.program_id(1)
        halo = jnp.where(i > 0, hh_ref[...], jnp.zeros_like(hh_ref[...]))
        hx = jnp.concatenate([halo, h_ref[...]], axis=0)
        acts = []
        for gv in range(2):
            up = jnp.dot(hx, w_ref[gv], preferred_element_type=F32)
            up_ref[gv] = up[HALO:].astype(BF16)
            upc, _ = _conv3(up, [cw_ref[gv, k:k + 1, :] for k in range(3)], HALO)
            acts.append(upc)
        a = acts[0] * _sigmoid(acts[0]) * acts[1]
        f = jnp.dot(a.astype(BF16), wd_ref[...], preferred_element_type=F32)

        @pl.when(m == 0)
        def _():
            xo_ref[...] = x_ref[...] + f

        @pl.when(m > 0)
        def _():
            xo_ref[...] += f

    return pl.pallas_call(
        body, grid=(s // ts, NG), name=name,
        in_specs=[pl.BlockSpec((ts, D), lambda i, m: (i, 0)),
                  pl.BlockSpec((HALO, D), lambda i, m: (jnp.maximum(i * hb - 1, 0), 0)),
                  pl.BlockSpec((2, None, D, FB), lambda i, m: (0, m, 0, 0)),
                  pl.BlockSpec((2, None, 3, FB), lambda i, m: (0, m, 0, 0)),
                  pl.BlockSpec((FB, D), lambda i, m: (m, 0)),
                  pl.BlockSpec((ts, D), lambda i, m: (i, 0))],
        out_specs=[pl.BlockSpec((None, 2, ts, FB), lambda i, m: (m, 0, i, 0)),
                   pl.BlockSpec((ts, D), lambda i, m: (i, 0))],
        out_shape=[jax.ShapeDtypeStruct((NG, 2, s, FB), BF16), jax.ShapeDtypeStruct((s, D), F32)],
        compiler_params=_params("arbitrary", "arbitrary"),
    )(h, h, wup, fcw, wdn, xres)


def _final(x, tgt, g, name):
    s = x.shape[0]
    ts = _tile(s, 512)

    def body(x_ref, t_ref, g_ref, dx_ref, dg_ref, loss_ref):
        i = pl.program_id(0)
        xv = x_ref[...]
        gv = g_ref[...]
        r = lax.rsqrt(jnp.mean(xv * xv, axis=-1, keepdims=True) + EPS)
        xhat = xv * r
        e = xhat * gv - t_ref[...]
        part = 0.5 * jnp.sum(jnp.mean(e * e, axis=-1, keepdims=True), axis=0, keepdims=True)
        dy = e * (1.0 / D)
        dgp = jnp.sum(dy * xhat, axis=0, keepdims=True)
        u = dy * gv
        dx_ref[...] = r * (u - xhat * jnp.mean(u * xhat, axis=-1, keepdims=True))

        @pl.when(i == 0)
        def _():
            dg_ref[...] = dgp
            loss_ref[...] = jnp.broadcast_to(part, (1, 128))

        @pl.when(i > 0)
        def _():
            dg_ref[...] += dgp
            loss_ref[...] += jnp.broadcast_to(part, (1, 128))

    return pl.pallas_call(
        body, grid=(s // ts,), name=name,
        in_specs=[pl.BlockSpec((ts, D), lambda i: (i, 0)), pl.BlockSpec((ts, D), lambda i: (i, 0)),
                  pl.BlockSpec((1, D), lambda i: (0, 0))],
        out_specs=[pl.BlockSpec((ts, D), lambda i: (i, 0)), pl.BlockSpec((1, D), lambda i: (0, 0)),
                   pl.BlockSpec((1, 128), lambda i: (0, 0))],
        out_shape=[jax.ShapeDtypeStruct((s, D), F32), jax.ShapeDtypeStruct((1, D), F32),
                   jax.ShapeDtypeStruct((1, 128), F32)],
        compiler_params=_params("arbitrary"),
    )(x, tgt, g)


def _ffn_bwd(df, up, wup, fcw, wdn, xin, g, name):
    s = df.shape[0]
    ts = _tile(s, 512)
    nt = s // ts
    hb = ts // HALO

    def body(df_ref, up_ref, uph_ref, w_ref, cw_ref, wd_ref, x_ref, g_ref,
             a_ref, dup_ref, dx_ref, dg_ref, dcw_ref, carry, acc):
        i = pl.program_id(0)
        m = pl.program_id(1)
        ti = nt - 1 - i
        first = i == 0
        dfb = df_ref[...].astype(BF16)
        cws = [[cw_ref[gv, k:k + 1, :] for k in range(3)] for gv in range(2)]
        upc, taps = [], []
        for gv in range(2):
            halo = jnp.where(ti > 0, uph_ref[gv], jnp.zeros_like(uph_ref[gv]))
            upe = jnp.concatenate([halo, up_ref[gv]], axis=0).astype(F32)
            c, t = _conv3(upe, cws[gv], HALO)
            upc.append(c)
            taps.append(t)
        sg = _sigmoid(upc[0])
        sl = upc[0] * sg
        a_ref[...] = (sl * upc[1]).astype(BF16)
        da = lax.dot_general(dfb, wd_ref[...], NT_DIMS, preferred_element_type=F32)
        dcs = [da * upc[1] * _dsilu(upc[0], sg), da * sl]

        @pl.when(first)
        def _():
            carry[m] = jnp.zeros((2, 8, FB), F32)

        dh = jnp.zeros((ts, D), F32)
        for gv in range(2):
            dc = dcs[gv]
            rows = _conv3_bwd_w(dc, taps[gv])

            @pl.when(first)
            def _():
                for k in range(3):
                    dcw_ref[m, gv, k:k + 1, :] = rows[k]

            @pl.when(jnp.logical_not(first))
            def _():
                for k in range(3):
                    dcw_ref[m, gv, k:k + 1, :] += rows[k]

            dce = jnp.concatenate([dc, carry[m, gv]], axis=0)
            du = _conv3_bwd_in(dce, cws[gv], ts)
            carry[m, gv] = dc[0:8]
            dub = du.astype(BF16)
            dup_ref[gv] = dub
            dh = dh + lax.dot_general(dub, w_ref[gv], NT_DIMS, preferred_element_type=F32)

        @pl.when(m == 0)
        def _():
            acc[...] = dh

        @pl.when(m > 0)
        def _():
            acc[...] += dh

        @pl.when(m == NG - 1)
        def _():
            dx, dgp = _rms_bwd_math(acc[...], x_ref[...], g_ref[...])
            dx_ref[...] = df_ref[...] + dx

            @pl.when(first)
            def _():
                dg_ref[...] = dgp

            @pl.when(jnp.logical_not(first))
            def _():
                dg_ref[...] += dgp

    rev = lambda i: nt - 1 - i
    return pl.pallas_call(
        body, grid=(nt, NG), name=name,
        in_specs=[pl.BlockSpec((ts, D), lambda i, m: (rev(i), 0)),
                  pl.BlockSpec((None, 2, ts, FB), lambda i, m: (m, 0, rev(i), 0)),
                  pl.BlockSpec((None, 2, HALO, FB), lambda i, m: (m, 0, jnp.maximum(rev(i) * hb - 1, 0), 0)),
                  pl.BlockSpec((2, None, D, FB), lambda i, m: (0, m, 0, 0)),
                  pl.BlockSpec((2, None, 3, FB), lambda i, m: (0, m, 0, 0)),
                  pl.BlockSpec((FB, D), lambda i, m: (m, 0)),
                  pl.BlockSpec((ts, D), lambda i, m: (rev(i), 0)),
                  pl.BlockSpec((1, D), lambda i, m: (0, 0))],
        out_specs=[pl.BlockSpec((None, ts, FB), lambda i, m: (m, rev(i), 0)),
                   pl.BlockSpec((None, 2, ts, FB), lambda i, m: (m, 0, rev(i), 0)),
                   pl.BlockSpec((ts, D), lambda i, m: (rev(i), 0)),
                   pl.BlockSpec((1, D), lambda i, m: (0, 0)),
                   pl.BlockSpec((NG, 2, 3, FB), lambda i, m: (0, 0, 0, 0))],
        out_shape=[jax.ShapeDtypeStruct((NG, s, FB), BF16), jax.ShapeDtypeStruct((NG, 2, s, FB), BF16),
                   jax.ShapeDtypeStruct((s, D), F32), jax.ShapeDtypeStruct((1, D), F32),
                   jax.ShapeDtypeStruct((NG, 2, 3, FB), F32)],
        scratch_shapes=[pltpu.VMEM((NG, 2, 8, FB), F32), pltpu.VMEM((ts, D), F32)],
        compiler_params=_params("arbitrary", "arbitrary"),
    )(df, up, up, wup, fcw, wdn, xin, g)


def _mm_nt(dy, w, name):
    s = dy.shape[0]
    ts = _tile(s, 512)

    def body(dy_ref, w_ref, o_ref):
        o_ref[...] = lax.dot_general(dy_ref[...].astype(BF16), w_ref[...], NT_DIMS,
                                     preferred_element_type=F32).astype(BF16)

    return pl.pallas_call(
        body, grid=(s // ts,), name=name,
        in_specs=[pl.BlockSpec((ts, D), lambda i: (i, 0)), pl.BlockSpec((D, D), lambda i: (0, 0))],
        out_specs=pl.BlockSpec((ts, D), lambda i: (i, 0)),
        out_shape=jax.ShapeDtypeStruct((s, D), BF16),
        compiler_params=_params("parallel"),
    )(dy, w)


def _mm_nt_rms(dy, wblk, x, g, dres, name):
    s = dy.shape[0]
    nb, _, bn = wblk.shape
    ts = _tile(s, 512)

    def body(dy_ref, w_ref, x_ref, g_ref, dr_ref, dx_ref, dg_ref):
        i = pl.program_id(0)
        acc = jnp.zeros((ts, D), F32)
        for b in range(nb):
            acc = acc + lax.dot_general(dy_ref[:, b * bn:(b + 1) * bn], w_ref[b], NT_DIMS,
                                        preferred_element_type=F32)
        dx, dgp = _rms_bwd_math(acc, x_ref[...], g_ref[...])
        dx_ref[...] = dr_ref[...] + dx

        @pl.when(i == 0)
        def _():
            dg_ref[...] = dgp

        @pl.when(i > 0)
        def _():
            dg_ref[...] += dgp

    return pl.pallas_call(
        body, grid=(s // ts,), name=name,
        in_specs=[pl.BlockSpec((ts, nb * bn), lambda i: (i, 0)), pl.BlockSpec((nb, D, bn), lambda i: (0, 0, 0)),
                  pl.BlockSpec((ts, D), lambda i: (i, 0)), pl.BlockSpec((1, D), lambda i: (0, 0)),
                  pl.BlockSpec((ts, D), lambda i: (i, 0))],
        out_specs=[pl.BlockSpec((ts, D), lambda i: (i, 0)), pl.BlockSpec((1, D), lambda i: (0, 0))],
        out_shape=[jax.ShapeDtypeStruct((s, D), F32), jax.ShapeDtypeStruct((1, D), F32)],
        compiler_params=_params("arbitrary"),
    )(dy, wblk, x, g, dres)


def _c_bwd(dr, zc, cw, name):
    s = dr.shape[0]
    ts = _tile(s, 512)
    nt = s // ts
    hb = ts // HALO

    def body(dr_ref, drf_ref, z_ref, ch_ref, xh_ref, bf_ref, cw_ref, dz_ref, dcw_ref):
        i = pl.program_id(0)
        cwv = [cw_ref[k:k + 1, :] for k in range(3)]
        z_t = z_ref[...].astype(F32)
        bg, cg, xv = z_t[:, 0:D], z_t[:, D:2 * D], z_t[:, 2 * D:3 * D]
        ph = jnp.where(i > 0, ch_ref[...].astype(F32) * xh_ref[...].astype(F32), 0.0)
        pe = jnp.concatenate([ph, cg * xv], axis=0)
        q, taps = _conv3(pe, cwv, HALO)
        drv = dr_ref[...].astype(F32)
        dq = drv * bg
        dqf = jnp.where(i < nt - 1, drf_ref[...].astype(F32) * bf_ref[...].astype(F32), 0.0)
        dp = _conv3_bwd_in(jnp.concatenate([dq, dqf], axis=0), cwv, ts)
        dz_ref[:, 0:D] = (drv * q).astype(BF16)
        dz_ref[:, D:2 * D] = (dp * xv).astype(BF16)
        dz_ref[:, 2 * D:3 * D] = (dp * cg).astype(BF16)
        rows = _conv3_bwd_w(dq, taps)

        @pl.when(i == 0)
        def _():
            for k in range(3):
                dcw_ref[k:k + 1, :] = rows[k]

        @pl.when(i > 0)
        def _():
            for k in range(3):
                dcw_ref[k:k + 1, :] += rows[k]

    past = lambda col: pl.BlockSpec((HALO, D), lambda i: (jnp.maximum(i * hb - 1, 0), col))
    nxt = lambda i: jnp.minimum((i + 1) * hb, s // HALO - 1)
    return pl.pallas_call(
        body, grid=(nt,), name=name,
        in_specs=[pl.BlockSpec((ts, D), lambda i: (i, 0)),
                  pl.BlockSpec((HALO, D), lambda i: (nxt(i), 0)),
                  pl.BlockSpec((ts, 3 * D), lambda i: (i, 0)), past(1), past(2),
                  pl.BlockSpec((HALO, D), lambda i: (nxt(i), 0)),
                  pl.BlockSpec((3, D), lambda i: (0, 0))],
        out_specs=[pl.BlockSpec((ts, 3 * D), lambda i: (i, 0)), pl.BlockSpec((3, D), lambda i: (0, 0))],
        out_shape=[jax.ShapeDtypeStruct((s, 3 * D), BF16), jax.ShapeDtypeStruct((3, D), F32)],
        compiler_params=_params("arbitrary"),
    )(dr, dr, zc, zc, zc, zc, cw)


G512_ROWS = 40


def _ab_bwd(dy, z, yb2, lga, lba, wsm, bs_col, cwb, lgb, lbb, name):
    s = z.shape[0]
    ts = _tile(s, 256)
    nt = s // ts
    hb = ts // HALO_B
    nch = ts // CHUNK
    tri = None

    def body(z_ref, zh_ref, dy_ref, dyf_ref, yb2_ref, yb2f_ref, lga_ref, lba_ref, ws_ref, bs_ref,
             cw_ref, lgb_ref, lbb_ref, dz_ref, g512_ref, dws_ref, dbs_ref, dvn_ref):
        i = pl.program_id(0)
        last = i == nt - 1

        @pl.when(i == 0)
        def _():
            g512_ref[...] = jnp.zeros((G512_ROWS, DA), F32)
            dws_ref[...] = jnp.zeros((HEADS, CHUNK, CHUNK), F32)
            dbs_ref[...] = jnp.zeros((HEADS, CHUNK, 1), F32)

        def add_row(k, v):
            g512_ref[k:k + 1, :] += v

        z_t = z_ref[...].astype(F32)
        dy_t = dy_ref[...].astype(F32)
        ua, va = z_t[:, 0:DA], z_t[:, DA:2 * DA]
        gu = _gelu(ua)
        gv = _gelu(va)
        lga_v = lga_ref[...]
        vn, xhat_a, rstd_a = _ln_fwd(gv, lga_v, lba_ref[...])
        vnb = vn.astype(BF16)
        causal = (lax.broadcasted_iota(jnp.int32, (CHUNK, CHUNK), 0)
                  >= lax.broadcasted_iota(jnp.int32, (CHUNK, CHUNK), 1)).astype(F32)
        for c in range(nch):
            for h in range(HEADS):
                rs = slice(c * CHUNK, (c + 1) * CHUNK)
                cs = slice(h * CHUNK, (h + 1) * CHUNK)
                vblk = vnb[rs, cs]
                mixed = jnp.dot(ws_ref[h], vblk, preferred_element_type=F32) + bs_ref[h]
                dyb_ = dy_t[rs, cs]
                dmix = dyb_ * gu[rs, cs]
                dmb = dmix.astype(BF16)
                dz_ref[rs, cs] = (dyb_ * mixed * _dgelu(ua[rs, cs])).astype(BF16)
                dvn_ref[rs, cs] = lax.dot_general(ws_ref[h], dmb, TN_DIMS, preferred_element_type=F32)
                dws_ref[h] += causal * lax.dot_general(dmb, vblk, NT_DIMS, preferred_element_type=F32)
                dbs_ref[h] += jnp.sum(dmix, axis=1, keepdims=True)
        dvn = dvn_ref[...]
        add_row(0, jnp.sum(dvn * xhat_a, axis=0, keepdims=True))
        add_row(1, jnp.sum(dvn, axis=0, keepdims=True))
        dgv = _ln_bwd(dvn, xhat_a, rstd_a, lga_v)
        dz_ref[:, DA:2 * DA] = (dgv * _dgelu(va)).astype(BF16)
        lgb_v = lgb_ref[...]
        dyb_e = jnp.concatenate(
            [dy_t[:, DA:2 * DA], jnp.where(last, 0.0, dyf_ref[...].astype(F32))], axis=0)
        yb2_e = jnp.concatenate([yb2_ref[...], jnp.where(last, 0.0, yb2f_ref[...])], axis=0)
        n_e, xhat_b, rstd_b = _ln_fwd(yb2_e, lgb_v, lbb_ref[...])
        sgn = _sigmoid(n_e)
        dn = dyb_e * _dsilu(n_e, sgn)
        dy2 = _ln_bwd(dn, xhat_b, rstd_b, lgb_v)
        add_row(2, jnp.sum(dy2[:ts], axis=0, keepdims=True))
        add_row(3, jnp.sum(dn[:ts] * xhat_b[:ts], axis=0, keepdims=True))
        add_row(4, jnp.sum(dn[:ts], axis=0, keepdims=True))
        zh = jnp.where(i > 0, zh_ref[...], jnp.zeros_like(zh_ref[...])).astype(F32)
        xb_t, gb_t = z_t[:, 2 * DA:3 * DA], z_t[:, 3 * DA:4 * DA]
        sgb = _sigmoid(gb_t)
        ue = jnp.concatenate([zh[:, 0:DA] * _sigmoid(zh[:, DA:2 * DA]), xb_t * sgb], axis=0)
        dy2_t = dy2[:ts]
        n_e_rows = ts + HALO_B
        du = jnp.zeros((ts, DA), F32)
        for r in range(8):
            fwd_roll = ue if r == 0 else pltpu.roll(ue, r, 0)
            bwd_roll = dy2 if r == 0 else pltpu.roll(dy2, n_e_rows - r, 0)
            for q in range(4):
                sh = 8 * q + r
                if sh >= BCONV:
                    continue
                k = BCONV - 1 - sh
                du = du + cw_ref[k:k + 1, :] * bwd_roll[8 * q:8 * q + ts]
                add_row(8 + k, jnp.sum(dy2_t * fwd_roll[HALO_B - 8 * q:HALO_B - 8 * q + ts],
                                       axis=0, keepdims=True))
        dz_ref[:, 2 * DA:3 * DA] = (du * sgb).astype(BF16)
        dz_ref[:, 3 * DA:4 * DA] = (du * xb_t * sgb * (1.0 - sgb)).astype(BF16)

    row = lambda i: (0, 0)
    nxt = lambda i: jnp.minimum((i + 1) * hb, s // HALO_B - 1)
    return pl.pallas_call(
        body, grid=(nt,), name=name,
        in_specs=[pl.BlockSpec((ts, 4 * DA), lambda i: (i, 0)),
                  pl.BlockSpec((HALO_B, 2 * DA), lambda i: (jnp.maximum(i * hb - 1, 0), 1)),
                  pl.BlockSpec((ts, 2 * DA), lambda i: (i, 0)),
                  pl.BlockSpec((HALO_B, DA), lambda i: (nxt(i), 1)),
                  pl.BlockSpec((ts, DA), lambda i: (i, 0)),
                  pl.BlockSpec((HALO_B, DA), lambda i: (nxt(i), 0)),
                  pl.BlockSpec((1, DA), row), pl.BlockSpec((1, DA), row),
                  pl.BlockSpec((HEADS, CHUNK, CHUNK), lambda i: (0, 0, 0)),
                  pl.BlockSpec((HEADS, CHUNK, 1), lambda i: (0, 0, 0)),
                  pl.BlockSpec((BCONV, DA), row), pl.BlockSpec((1, DA), row), pl.BlockSpec((1, DA), row)],
        out_specs=[pl.BlockSpec((ts, 4 * DA), lambda i: (i, 0)),
                   pl.BlockSpec((G512_ROWS, DA), row),
                   pl.BlockSpec((HEADS, CHUNK, CHUNK), lambda i: (0, 0, 0)),
                   pl.BlockSpec((HEADS, CHUNK, 1), lambda i: (0, 0, 0))],
        out_shape=[jax.ShapeDtypeStruct((s, 4 * DA), BF16), jax.ShapeDtypeStruct((G512_ROWS, DA), F32),
                   jax.ShapeDtypeStruct((HEADS, CHUNK, CHUNK), F32),
                   jax.ShapeDtypeStruct((HEADS, CHUNK, 1), F32)],
        scratch_shapes=[pltpu.VMEM((ts, DA), F32)],
        compiler_params=_params("arbitrary"),
    )(z, z, dy, dy, yb2, yb2, lga, lba, wsm, bs_col, cwb, lgb, lbb)


def _dw_cols(a, dy, nb, bn, name):
    s = a.shape[0]
    tm = _tile(s, 2048)
    nt = s // tm
    cpb = 4

    def body(a_ref, dy_ref, o_ref, acc):
        t = pl.program_id(1)
        av = a_ref[...]
        for b in range(cpb):
            p = lax.dot_general(av, dy_ref[:, b * bn:(b + 1) * bn], TN_DIMS, preferred_element_type=F32)

            @pl.when(t == 0)
            def _():
                acc[b] = p

            @pl.when(t > 0)
            def _():
                acc[b] += p

        @pl.when(t == nt - 1)
        def _():
            o_ref[...] = acc[...].astype(BF16)

    return pl.pallas_call(
        body, grid=(nb // cpb, nt), name=name,
        in_specs=[pl.BlockSpec((tm, D), lambda j, t: (t, 0)), pl.BlockSpec((tm, cpb * bn), lambda j, t: (t, j))],
        out_specs=pl.BlockSpec((cpb, D, bn), lambda j, t: (j, 0, 0)),
        out_shape=jax.ShapeDtypeStruct((nb, D, bn), BF16),
        scratch_shapes=[pltpu.VMEM((cpb, D, bn), F32)],
        compiler_params=_params("arbitrary", "arbitrary"),
    )(a, dy)


def _dw_rows(a, dy, name):
    s = a.shape[0]
    tm = _tile(s, 2048)
    nt = s // tm
    rb = 512

    def body(a_ref, dy_ref, o_ref, acc):
        t = pl.program_id(1)
        p = lax.dot_general(a_ref[...], dy_ref[...].astype(BF16), TN_DIMS, preferred_element_type=F32)

        @pl.when(t == 0)
        def _():
            acc[...] = p

        @pl.when(t > 0)
        def _():
            acc[...] += p

        @pl.when(t == nt - 1)
        def _():
            o_ref[...] = acc[...].astype(BF16)

    return pl.pallas_call(
        body, grid=(D // rb, nt), name=name,
        in_specs=[pl.BlockSpec((tm, rb), lambda j, t: (t, j)), pl.BlockSpec((tm, D), lambda j, t: (t, 0))],
        out_specs=pl.BlockSpec((rb, D), lambda j, t: (j, 0)),
        out_shape=jax.ShapeDtypeStruct((D, D), BF16),
        scratch_shapes=[pltpu.VMEM((rb, D), F32)],
        compiler_params=_params("arbitrary", "arbitrary"),
    )(a, dy)


def _dw_up(h, dup, name):
    s = h.shape[0]
    tm = _tile(s, 2048)
    nt = s // tm

    def body(h_ref, d_ref, o_ref, acc):
        t = pl.program_id(1)
        p = lax.dot_general(d_ref[...], h_ref[...], TN_DIMS, preferred_element_type=F32)

        @pl.when(t == 0)
        def _():
            acc[...] = p

        @pl.when(t > 0)
        def _():
            acc[...] += p

        @pl.when(t == nt - 1)
        def _():
            o_ref[...] = acc[...].astype(BF16)

    return pl.pallas_call(
        body, grid=(NDEV, nt), name=name,
        in_specs=[pl.BlockSpec((tm, D), lambda b, t: (t, 0)),
                  pl.BlockSpec((None, None, tm, FB), lambda b, t: (b % NG, b // NG, t, 0))],
        out_specs=pl.BlockSpec((None, FB, D), lambda b, t: (b, 0, 0)),
        out_shape=jax.ShapeDtypeStruct((NDEV, FB, D), BF16),
        scratch_shapes=[pltpu.VMEM((FB, D), F32)],
        compiler_params=_params("arbitrary", "arbitrary"),
    )(h, dup)


def _dw_dn(a, df, name):
    s = df.shape[0]
    tm = _tile(s, 2048)
    nt = s // tm

    def body(a_ref, d_ref, o_ref, acc):
        t = pl.program_id(1)
        p = lax.dot_general(a_ref[...], d_ref[...].astype(BF16), TN_DIMS, preferred_element_type=F32)

        @pl.when(t == 0)
        def _():
            acc[...] = p

        @pl.when(t > 0)
        def _():
            acc[...] += p

        @pl.when(t == nt - 1)
        def _():
            o_ref[...] = acc[...].astype(BF16)

    return pl.pallas_call(
        body, grid=(NG, nt), name=name,
        in_specs=[pl.BlockSpec((None, tm, FB), lambda m, t: (m, t, 0)), pl.BlockSpec((tm, D), lambda m, t: (t, 0))],
        out_specs=pl.BlockSpec((FB, D), lambda m, t: (m, 0)),
        out_shape=jax.ShapeDtypeStruct((DFF, D), BF16),
        scratch_shapes=[pltpu.VMEM((FB, D), F32)],
        compiler_params=_params("arbitrary", "arbitrary"),
    )(a, df)


def _place():
    x, y, c = lax.axis_index("x"), lax.axis_index("y"), lax.axis_index("c")
    chips = [(1 - x, y), (x, 1 - y), (1 - x, 1 - y)]
    return x, y, c, chips


def _all_gather(shards, later, name):
    nt = len(shards)
    nl = len(later)

    def body(*refs):
        srcs, lsrcs = refs[:nt], refs[nt:nt + nl]
        dsts, ldsts = refs[nt + nl:2 * nt + nl], refs[2 * nt + nl:2 * (nt + nl)]
        send_sems, recv_sems, local_sems = refs[2 * (nt + nl):]
        x, y, c, chips = _place()
        me, sib = (x, y, c), (x, y, 1 - c)

        def blk(t, p):
            return dsts[t].at[4 * p[0] + 2 * p[1] + p[2]]

        def copy(t, k, block, to, src=None):
            return pltpu.make_async_remote_copy(
                src_ref=blk(t, block) if src is None else src, dst_ref=blk(t, block),
                send_sem=send_sems.at[t, k], recv_sem=recv_sems.at[t, k],
                device_id=to, device_id_type=MESH)

        mine = [pltpu.make_async_copy(srcs[t], blk(t, me), local_sems.at[t]) for t in range(nt)]
        mine += [pltpu.make_async_copy(lsrcs[t], ldsts[t].at[4 * x + 2 * y + c], local_sems.at[nt + t])
                 for t in range(nl)]
        for cp in mine:
            cp.start()
        first = []
        for t in range(nt):
            first.append(copy(t, 0, me, sib, src=srcs[t]))
            first += [copy(t, 1 + j, me, (*chip, c), src=srcs[t]) for j, chip in enumerate(chips)]
        for cp in first:
            cp.start()
        passed = []
        for j, chip in enumerate(chips):
            for t in range(nt):
                copy(t, 1 + j, (*chip, c), me).wait_recv()
                cp = copy(t, 4 + j, (*chip, c), sib)
                cp.start()
                passed.append(cp)
        for t in range(nt):
            copy(t, 0, sib, me).wait_recv()
            for j, chip in enumerate(chips):
                copy(t, 4 + j, (*chip, 1 - c), me).wait_recv()
        for cp in first + passed:
            cp.wait_send()
        for cp in mine:
            cp.wait()

    res = pl.pallas_call(
        body, name=name,
        in_specs=[ANY] * (nt + nl), out_specs=[ANY] * (nt + nl),
        out_shape=[jax.ShapeDtypeStruct((NDEV,) + a.shape, a.dtype) for a in list(shards) + list(later)],
        scratch_shapes=[pltpu.SemaphoreType.DMA((nt, 7)), pltpu.SemaphoreType.DMA((nt, 7)),
                        pltpu.SemaphoreType.DMA((nt + nl,))],
        compiler_params=pltpu.CompilerParams(has_side_effects=True),
    )(*shards, *later)
    return res[:nt], res[nt:]


HBM_SPEC = pl.BlockSpec(memory_space=pltpu.HBM)
SEM_SPEC = pl.BlockSpec(memory_space=pltpu.SEMAPHORE)
DATAFLOW = pltpu.SideEffectType.DATAFLOW_SIDE_EFFECTING


def _hbm(a):
    return pltpu.with_memory_space_constraint(a, pltpu.HBM)


def _hbm_like(arrs):
    return [pltpu.HBM(a.shape, a.dtype) for a in arrs]


def _ag_start(srcs, lands, groups, name):
    del groups
    n = len(srcs)
    ns = 8 * n

    def body(*refs):
        src, land = refs[:n], refs[n:2 * n]
        sems = refs[2 * n:2 * n + ns]
        token = refs[-1]
        x, y, c, chips = _place()
        peers = [(x, y, 1 - c)] + [(*chip, c) for chip in chips]
        for t in range(n):
            for k, to in enumerate(peers):
                pltpu.make_async_remote_copy(
                    src_ref=src[t], dst_ref=land[t].at[4 * x + 2 * y + c],
                    send_sem=sems[2 * (4 * t + k)], recv_sem=sems[2 * (4 * t + k) + 1],
                    device_id=to, device_id_type=MESH).start()
        token[...] = jnp.zeros_like(token)

    res = pl.pallas_call(
        body, name=name,
        in_specs=[HBM_SPEC] * (2 * n),
        out_specs=[SEM_SPEC] * ns + [HBM_SPEC] * (2 * n) + [pl.BlockSpec(memory_space=pltpu.VMEM)],
        out_shape=[pltpu.SemaphoreType.DMA(())] * ns + _hbm_like(srcs) + _hbm_like(lands)
        + [jax.ShapeDtypeStruct((8, 128), F32)],
        input_output_aliases={i: ns + i for i in range(2 * n)},
        compiler_params=pltpu.CompilerParams(has_side_effects=DATAFLOW),
    )(*[_hbm(a) for a in srcs], *[_hbm(a) for a in lands])
    sems = [[(res[2 * (4 * t + k)], res[2 * (4 * t + k) + 1]) for k in range(4)] for t in range(n)]
    return sems, res[ns:ns + n], res[ns + n:ns + 2 * n], res[-1]


def _ag_forward(srcs, lands, sems1, after, name):
    n = len(srcs)
    flat1 = [s for t in range(n) for k in range(1, 4) for s in sems1[t][k]]
    n1 = len(flat1)

    def body(*refs):
        src, land = refs[:n], refs[n:2 * n]
        s1 = refs[2 * n:2 * n + n1]
        s2 = refs[2 * n + n1 + 1:2 * n + n1 + 1 + 6 * n]
        x, y, c, chips = _place()
        for j, (cx, cy) in enumerate(chips):
            for t in range(n):
                blk = land[t].at[4 * cx + 2 * cy + c]
                pltpu.make_async_remote_copy(
                    src_ref=src[t], dst_ref=blk, send_sem=s1[2 * (3 * t + j)], recv_sem=s1[2 * (3 * t + j) + 1],
                    device_id=(cx, cy, c), device_id_type=MESH).wait_recv()
                pltpu.make_async_remote_copy(
                    src_ref=blk, dst_ref=blk, send_sem=s2[2 * (3 * t + j)], recv_sem=s2[2 * (3 * t + j) + 1],
                    device_id=(x, y, 1 - c), device_id_type=MESH).start()

    res = pl.pallas_call(
        body, name=name,
        in_specs=[HBM_SPEC] * (2 * n) + [SEM_SPEC] * n1 + [ANY],
        out_specs=[SEM_SPEC] * (6 * n) + [HBM_SPEC] * n,
        out_shape=[pltpu.SemaphoreType.DMA(())] * (6 * n) + _hbm_like(lands),
        input_output_aliases={n + i: 6 * n + i for i in range(n)},
        compiler_params=pltpu.CompilerParams(has_side_effects=DATAFLOW),
    )(*srcs, *lands, *flat1, after)
    sems2 = [[(res[2 * (3 * t + j)], res[2 * (3 * t + j) + 1]) for j in range(3)] for t in range(n)]
    return sems2, res[6 * n:]


def _ag_finish(srcs, lands, sems1, sems2, after, name):
    n = len(srcs)
    flat1 = [s for t in range(n) for k in range(4) for s in sems1[t][k]]
    flat2 = [s for t in range(n) for j in range(3) for s in sems2[t][j]]
    n1, n2 = len(flat1), len(flat2)

    def body(*refs):
        src, land = refs[:n], refs[n:2 * n]
        s1 = refs[2 * n:2 * n + n1]
        s2 = refs[2 * n + n1:2 * n + n1 + n2]
        x, y, c, chips = _place()
        sib = (x, y, 1 - c)
        for t in range(n):
            own = land[t].at[4 * x + 2 * y + 1 - c]
            pltpu.make_async_remote_copy(
                src_ref=src[t], dst_ref=own, send_sem=s1[8 * t], recv_sem=s1[8 * t + 1],
                device_id=sib, device_id_type=MESH).wait_recv()
            for k in range(4):
                pltpu.make_async_remote_copy(
                    src_ref=src[t], dst_ref=own, send_sem=s1[2 * (4 * t + k)], recv_sem=s1[2 * (4 * t + k) + 1],
                    device_id=sib, device_id_type=MESH).wait_send()
            for j, (cx, cy) in enumerate(chips):
                blk = land[t].at[4 * cx + 2 * cy + 1 - c]
                cp = pltpu.make_async_remote_copy(
                    src_ref=blk, dst_ref=blk, send_sem=s2[2 * (3 * t + j)], recv_sem=s2[2 * (3 * t + j) + 1],
                    device_id=sib, device_id_type=MESH)
                cp.wait_send()
                cp.wait_recv()

    return pl.pallas_call(
        body, name=name,
        in_specs=[HBM_SPEC] * (2 * n) + [SEM_SPEC] * (n1 + n2) + [ANY],
        out_specs=[HBM_SPEC] * n,
        out_shape=_hbm_like(lands),
        input_output_aliases={n + i: i for i in range(n)},
        compiler_params=pltpu.CompilerParams(has_side_effects=DATAFLOW),
    )(*srcs, *lands, *flat1, *flat2, after)


def _pair_copies(srcs, dsts, sems):
    x, y, c, _ = _place()
    nt = len(srcs)
    return [pltpu.make_async_remote_copy(
        src_ref=srcs[t].at[2 * j + 1 - c], dst_ref=dsts[t].at[j],
        send_sem=sems[2 * (NCHIP * t + j)], recv_sem=sems[2 * (NCHIP * t + j) + 1],
        device_id=(x, y, 1 - c), device_id_type=MESH) for t in range(nt) for j in range(NCHIP)]


def _pair_start(grads, carry, name):
    nt = len(grads)
    ns = 2 * NCHIP * nt
    zones = [_hbm(lax.empty((NCHIP,) + a.shape[1:], a.dtype)) for a in grads]
    extra = [] if carry is None else [_hbm(carry)]
    ne = len(extra)

    def body(*refs):
        for cp in _pair_copies(refs[:nt], refs[nt:2 * nt], refs[2 * nt + ne:2 * nt + ne + ns]):
            cp.start()

    res = pl.pallas_call(
        body, name=name,
        in_specs=[HBM_SPEC] * (2 * nt + ne),
        out_specs=[SEM_SPEC] * ns + [HBM_SPEC] * (2 * nt + ne),
        out_shape=[pltpu.SemaphoreType.DMA(())] * ns + _hbm_like(grads) + _hbm_like(zones) + _hbm_like(extra),
        input_output_aliases={i: ns + i for i in range(2 * nt + ne)},
        compiler_params=pltpu.CompilerParams(has_side_effects=DATAFLOW),
    )(*[_hbm(a) for a in grads], *zones, *extra)
    handle = (list(res[:ns]), list(res[ns:ns + nt]), list(res[ns + nt:ns + 2 * nt]))
    return handle, (res[ns + 2 * nt] if ne else None)


def _pair_wait(handle, after, name):
    sems, srcs, zones = handle
    nt, ns = len(srcs), len(sems)

    def body(*refs):
        for cp in _pair_copies(refs[:nt], refs[nt:2 * nt], refs[2 * nt:2 * nt + ns]):
            cp.wait_send()
            cp.wait_recv()

    return pl.pallas_call(
        body, name=name,
        in_specs=[HBM_SPEC] * (2 * nt) + [SEM_SPEC] * ns + [ANY],
        out_specs=[HBM_SPEC] * nt,
        out_shape=_hbm_like(zones),
        input_output_aliases={nt + i: i for i in range(nt)},
        compiler_params=pltpu.CompilerParams(has_side_effects=DATAFLOW),
    )(*srcs, *zones, *sems, after)


def _rows_tile(r, row_bytes, cap_bytes):
    best = None
    for tr in range(16, r + 1, 16):
        if r % tr == 0 and tr * row_bytes <= cap_bytes:
            best = tr
    return best if best is not None else r


def _pair_sum(own, got, cidx, name):
    _, _, r, cdim = own.shape
    tr = _rows_tile(r, 2 * cdim, 2 * 1024 * 1024)

    def body(c_ref, a_ref, b_ref, o_ref):
        o_ref[...] = (a_ref[...].astype(F32) + b_ref[...].astype(F32)).astype(BF16)

    return pl.pallas_call(
        body, name=name,
        grid_spec=pltpu.PrefetchScalarGridSpec(
            num_scalar_prefetch=1, grid=(NCHIP, r // tr),
            in_specs=[pl.BlockSpec((None, None, tr, cdim), lambda j, i, c_ref: (j, c_ref[0], i, 0)),
                      pl.BlockSpec((None, tr, cdim), lambda j, i, c_ref: (j, i, 0))],
            out_specs=pl.BlockSpec((None, tr, cdim), lambda j, i, c_ref: (j, i, 0))),
        out_shape=jax.ShapeDtypeStruct((NCHIP, r, cdim), BF16),
        compiler_params=_params("arbitrary", "arbitrary"),
    )(cidx, own, got)


def _chip_copies(srcs, zones, slots, sems):
    x, y, c, chips = _place()
    out = []
    for t, (z, l) in enumerate(slots):
        for k, (cx, cy) in enumerate(chips):
            dst = zones[z].at[k] if l is None else zones[z].at[k, l]
            out.append(pltpu.make_async_remote_copy(
                src_ref=srcs[t].at[2 * cx + cy], dst_ref=dst,
                send_sem=sems[2 * (3 * t + k)], recv_sem=sems[2 * (3 * t + k) + 1],
                device_id=(cx, cy, c), device_id_type=MESH))
    return out


def _chip_start(sums, zones, slots, carry, name):
    nt, nz = len(sums), len(zones)
    ns = 6 * nt
    extra = [] if carry is None else [_hbm(carry)]
    ne = len(extra)

    def body(*refs):
        for cp in _chip_copies(refs[:nt], refs[nt:nt + nz], slots, refs[nt + nz + ne:nt + nz + ne + ns]):
            cp.start()

    res = pl.pallas_call(
        body, name=name,
        in_specs=[HBM_SPEC] * (nt + nz + ne),
        out_specs=[SEM_SPEC] * ns + [HBM_SPEC] * (nt + nz + ne),
        out_shape=[pltpu.SemaphoreType.DMA(())] * ns + _hbm_like(sums) + _hbm_like(zones) + _hbm_like(extra),
        input_output_aliases={i: ns + i for i in range(nt + nz + ne)},
        compiler_params=pltpu.CompilerParams(has_side_effects=DATAFLOW),
    )(*[_hbm(a) for a in sums], *zones, *extra)
    return (list(res[:ns]), list(res[ns:ns + nt]), list(res[ns + nt:ns + nt + nz]),
            (res[ns + nt + nz] if ne else None))


def _chip_wait(started, zones, after, name):
    nz = len(zones)
    flat_src = [a for sums, _, _ in started for a in sums]
    flat_sem = [s for _, _, sems in started for s in sems]
    n_src, n_sem = len(flat_src), len(flat_sem)

    def body(*refs):
        srcs, zs, sems = refs[:n_src], refs[n_src:n_src + nz], refs[n_src + nz:n_src + nz + n_sem]
        so, se = 0, 0
        for sums, slots, sem_list in started:
            for cp in _chip_copies(srcs[so:so + len(sums)], zs, slots, sems[se:se + len(sem_list)]):
                cp.wait_send()
                cp.wait_recv()
            so += len(sums)
            se += len(sem_list)

    return pl.pallas_call(
        body, name=name,
        in_specs=[HBM_SPEC] * (n_src + nz) + [SEM_SPEC] * n_sem + [ANY],
        out_specs=[HBM_SPEC] * nz,
        out_shape=_hbm_like(zones),
        input_output_aliases={n_src + i: i for i in range(nz)},
        compiler_params=pltpu.CompilerParams(has_side_effects=DATAFLOW),
    )(*flat_src, *zones, *flat_sem, after)


def _small_allreduce(parts, name):
    nt = len(parts)

    def body(*refs):
        srcs, outs, bufs = refs[:nt], refs[nt:2 * nt], refs[2 * nt:3 * nt]
        send_sems, recv_sems = refs[3 * nt:]
        x, y, c, _ = _place()
        peers = [(x, y, 1 - c), (1 - x, y, c), (x, 1 - y, c)]
        for t in range(nt):
            outs[t][...] = srcs[t][...]
        for step, peer in enumerate(peers):
            copies = [pltpu.make_async_remote_copy(
                src_ref=outs[t], dst_ref=bufs[t].at[step],
                send_sem=send_sems.at[step, t], recv_sem=recv_sems.at[step, t],
                device_id=peer, device_id_type=MESH) for t in range(nt)]
            for cp in copies:
                cp.start()
            for cp in copies:
                cp.wait()
            for t in range(nt):
                outs[t][...] = outs[t][...] + bufs[t][step]

    vm = pl.BlockSpec(memory_space=pltpu.VMEM)
    return pl.pallas_call(
        body, name=name,
        in_specs=[vm] * nt, out_specs=[vm] * nt,
        out_shape=[jax.ShapeDtypeStruct(a.shape, F32) for a in parts],
        scratch_shapes=[pltpu.VMEM((3,) + a.shape, F32) for a in parts]
        + [pltpu.SemaphoreType.DMA((3, nt)), pltpu.SemaphoreType.DMA((3, nt))],
        compiler_params=pltpu.CompilerParams(has_side_effects=True, vmem_limit_bytes=VMEM_LIMIT),
    )(*parts)


def _adam_math(w, g, m, v):
    m2 = ADAM_B1 * m + (1.0 - ADAM_B1) * g
    v2 = ADAM_B2 * v + (1.0 - ADAM_B2) * (g * g)
    m_hat = m2 / (1.0 - ADAM_B1 ** ADAM_STEP)
    v_hat = v2 / (1.0 - ADAM_B2 ** ADAM_STEP)
    delta = -ADAM_LR * (m_hat / (jnp.sqrt(v_hat) + ADAM_EPS) + ADAM_WD * w)
    return delta, m2, v2


def _adam_big(w, m, v, parts, mine, chip, name):
    nl, r, cdim = w.shape
    tr = _rows_tile(r, 4 * cdim, 3 * 512 * 1024)

    def body(c_ref, w_ref, m_ref, v_ref, p_ref, *rest):
        mine_refs, (g_ref, d_ref, mo_ref, vo_ref) = rest[:nl], rest[nl:]
        own = mine_refs[0][...]
        for l in range(1, nl):
            own = jnp.where(pl.program_id(0) == l, mine_refs[l][...], own)
        g = ((p_ref[0].astype(F32) + p_ref[1].astype(F32)) + p_ref[2].astype(F32)) + own.astype(F32)
        delta, m2, v2 = _adam_math(w_ref[...], g, m_ref[...], v_ref[...])
        g_ref[...] = g
        d_ref[...] = delta
        mo_ref[...] = m2
        vo_ref[...] = v2

    spec = pl.BlockSpec((None, tr, cdim), lambda l, i, c_ref: (l, i, 0))
    mine_specs = [pl.BlockSpec((None, tr, cdim), lambda l, i, c_ref, ll=ll: (c_ref[0], jnp.where(l == ll, i, 0), 0))
                  for ll in range(nl)]
    return pl.pallas_call(
        body, name=name,
        grid_spec=pltpu.PrefetchScalarGridSpec(
            num_scalar_prefetch=1, grid=(nl, r // tr),
            in_specs=[spec, spec, spec, pl.BlockSpec((3, None, tr, cdim), lambda l, i, c_ref: (0, l, i, 0))]
            + mine_specs,
            out_specs=[spec] * 4),
        out_shape=[jax.ShapeDtypeStruct(w.shape, F32)] * 4,
        compiler_params=_params("arbitrary", "arbitrary"),
    )(chip, w, m, v, parts, *mine)


def _adam_small(ws, gs, ms, vs, name):
    n = len(ws)

    def body(*refs):
        w_r, g_r, m_r, v_r = refs[:n], refs[n:2 * n], refs[2 * n:3 * n], refs[3 * n:4 * n]
        d_o, m_o, v_o = refs[4 * n:5 * n], refs[5 * n:6 * n], refs[6 * n:7 * n]
        for t in range(n):
            delta, m2, v2 = _adam_math(w_r[t][...], g_r[t][...], m_r[t][...], v_r[t][...])
            d_o[t][...] = delta
            m_o[t][...] = m2
            v_o[t][...] = v2

    vm = pl.BlockSpec(memory_space=pltpu.VMEM)
    shapes = [jax.ShapeDtypeStruct(a.shape, F32) for a in ws]
    return pl.pallas_call(
        body, name=name, in_specs=[vm] * (4 * n), out_specs=[vm] * (3 * n), out_shape=shapes * 3,
        compiler_params=pltpu.CompilerParams(vmem_limit_bytes=VMEM_LIMIT),
    )(*ws, *gs, *ms, *vs)


def kernel(x, norm_mix, norm_ffn, norm_final, ab_w_in, a_ln_g, a_ln_b, a_w_s, a_b_s, b_conv_w, b_conv_b, b_ln_g, b_ln_b, ab_w_out, c_w_in, c_conv_w, c_w_out, f_w_up, f_conv_w, f_w_down, loss_target, m_norm_mix, m_norm_ffn, m_norm_final, m_ab_w_in, m_a_ln_g, m_a_ln_b, m_a_w_s, m_a_b_s, m_b_conv_w, m_b_conv_b, m_b_ln_g, m_b_ln_b, m_ab_w_out, m_c_w_in, m_c_conv_w, m_c_w_out, m_f_w_up, m_f_conv_w, m_f_w_down, v_norm_mix, v_norm_ffn, v_norm_final, v_ab_w_in, v_a_ln_g, v_a_ln_b, v_a_w_s, v_a_b_s, v_b_conv_w, v_b_conv_b, v_b_ln_g, v_b_ln_b, v_ab_w_out, v_c_w_in, v_c_conv_w, v_c_w_out, v_f_w_up, v_f_conv_w, v_f_w_down):
    s = x.shape[1]
    x0 = x.reshape(s, D)
    tgt = loss_target.reshape(s, D)
    xi, yi, ci = lax.axis_index("x"), lax.axis_index("y"), lax.axis_index("c")
    dev = 4 * xi + 2 * yi + ci
    cidx = ci.astype(jnp.int32).reshape(1)

    bf = lambda a: a.astype(BF16)
    now = [bf(ab_w_in[0]), bf(ab_w_out[0])]
    later = [bf(f_w_up[0]), bf(f_w_down[0]), bf(c_w_in[0]), bf(c_w_out[0]), bf(f_w_up[1]), bf(f_w_down[1])]
    (win0, wout0), lands = _all_gather(now, later, "all_gather_first")
    groups = [[0, 1], [2, 3], [4, 5]]
    ag_sems, later, lands, ag_token = _ag_start(later, lands, groups, "ag_start")
    wout0 = wout0.reshape(D, D)
    slab_w = 6 * CHUNK
    pad = lambda a, rows: jnp.pad(a, ((0, rows - a.shape[0]), (0, slab_w - a.shape[1])))
    slab = jnp.concatenate([pad(b_conv_w[0], 32), pad(c_conv_w[0], 8), pad(f_conv_w.reshape(6, FB), 8)], axis=0)
    (slab_g,), _ = _all_gather([slab + 0.0 * ag_token[0:1, 0:1]], [], "all_gather_conv_w")
    bcw = jnp.transpose(slab_g[:, 0:BCONV, 0:DA // NDEV], (1, 0, 2)).reshape(BCONV, DA)
    ccw = jnp.transpose(slab_g[:, 32:35, 0:D // NDEV], (1, 0, 2)).reshape(3, D)
    fcw_g = slab_g[:, 40:46, 0:FB].reshape(2, NG, 2, 3, FB)
    fcws = [fcw_g[:, :, 0], fcw_g[:, :, 1]]

    causal = jnp.tril(jnp.ones((CHUNK, CHUNK), F32))
    wsm = (a_w_s[0] * causal).astype(BF16)
    bs_col = a_b_s.reshape(HEADS, CHUNK, 1)
    nm = [norm_mix[0:1], norm_mix[1:2]]
    nf = [norm_ffn[0:1], norm_ffn[1:2]]
    nfin = norm_final.reshape(1, D)

    def arrive(g, after_ici, after_d2d, tag):
        srcs = [later[t] for t in groups[g]]
        zone = [lands[t] for t in groups[g]]
        sems1 = [ag_sems[t] for t in groups[g]]
        sems2, zone = _ag_forward(srcs, zone, sems1, after_ici, "ag_forward_" + tag)
        return _ag_finish(srcs, zone, sems1, sems2, after_d2d, "ag_finish_" + tag)

    h0 = _rms_fwd(x0, nm[0], "rms_mix0", after=ag_token)
    z = _mm_in(h0, win0, "mm_ab_in")
    ycat, yb2 = _ab_fwd(z, a_ln_g, a_ln_b, wsm, bs_col, bcw, b_conv_b, b_ln_g, b_ln_b, "ab_fwd")
    x1 = _mm_out(ycat, wout0, x0, "mm_ab_out")
    h1 = _rms_fwd(x1, nf[0], "rms_ffn0")
    wup0, wdn0 = arrive(0, x1, h1, "ffn0")
    up0, x2 = _ffn_fwd(h1, x1, wup0.reshape(2, NG, D, FB), fcws[0], wdn0.reshape(DFF, D), "ffn_fwd0")
    h2 = _rms_fwd(x2, nm[1], "rms_mix1")
    cin, cout = arrive(1, x2, h2, "c")
    cout = cout.reshape(D, D)
    zc = _mm_in(h2, cin, "mm_c_in")
    rc = _c_fwd(zc, ccw, "c_fwd")
    x3 = _mm_out(rc, cout, x2, "mm_c_out")
    h3 = _rms_fwd(x3, nf[1], "rms_ffn1")
    wup1, wdn1 = arrive(2, x3, h3, "ffn1")
    wups = [wup0.reshape(2, NG, D, FB), wup1.reshape(2, NG, D, FB)]
    wdns = [wdn0.reshape(DFF, D), wdn1.reshape(DFF, D)]
    up1, x4 = _ffn_fwd(h3, x3, wups[1], fcws[1], wdns[1], "ffn_fwd1")
    dx4, dnfin, loss_part = _final(x4, tgt, nfin, "final_loss")
    loss = lax.psum(loss_part[0, 0], ("x", "y", "c"))

    zshape = lambda *sh: _hbm(lax.empty((3,) + sh, BF16))
    zones = [zshape(D, 2 * D // NDEV), zshape(D // NDEV, D), zshape(D, 3 * D // NDEV), zshape(D // NDEV, D),
             zshape(2, FB, D), zshape(2, DFF // NDEV, D)]
    started = []

    def pair_sums(grads, handle, after, tag):
        del grads
        got = _pair_wait(handle, after, "rs_pair_wait_" + tag)
        return [_pair_sum(b.reshape((NCHIP, 2) + b.shape[1:]), g, cidx, "rs_pair_sum_%s%d" % (tag, t))
                for t, (b, g) in enumerate(zip(handle[1], got))]

    def chip_start(sums, slots, carry, tag):
        sems, sums, new_zones, carry = _chip_start(sums, zones, slots, carry, "rs_chip_start_" + tag)
        zones[:] = new_zones
        started.append((sums, slots, sems))
        return sums, carry

    rows8 = lambda g, r: g.reshape(NDEV, r, D)
    a1, dup1, dx3, dnf1, dfcw1 = _ffn_bwd(dx4, up1, wups[1], fcws[1], wdns[1], x3, nf[1], "ffn_bwd1")
    g_f1 = [_dw_up(h3, dup1, "dw_up1"), rows8(_dw_dn(a1, dx4, "dw_dn1"), DFF // NDEV)]
    hd_f1, dx3 = _pair_start(g_f1, dx3, "rs_pair_start_f1")
    drc = _mm_nt(dx3, cout, "mm_c_out_bwd")
    g_cout = rows8(_dw_rows(rc, dx3, "dw_c_out"), D // NDEV)
    s_f1 = pair_sums(g_f1, hd_f1, g_cout, "f1")
    s_f1, drc = chip_start(s_f1, [(4, 1), (5, 1)], drc, "f1")
    dzc, dccw = _c_bwd(drc, zc, ccw, "c_bwd")
    dx2, dnm1 = _mm_nt_rms(dzc, cin, x2, nm[1], dx3, "mm_c_in_bwd")
    g_c = [_dw_cols(h2, dzc, NDEV, 3 * D // NDEV, "dw_c_in"), g_cout]
    hd_c, dx2 = _pair_start(g_c, dx2, "rs_pair_start_c")
    a0, dup0, dx1, dnf0, dfcw0 = _ffn_bwd(dx2, up0, wups[0], fcws[0], wdns[0], x1, nf[0], "ffn_bwd0")
    s_c = pair_sums(g_c, hd_c, dx1, "c")
    s_c, dx1 = chip_start(s_c, [(2, None), (3, None)], dx1, "c")
    g_f0 = [_dw_up(h1, dup0, "dw_up0"), rows8(_dw_dn(a0, dx2, "dw_dn0"), DFF // NDEV)]
    hd_f0, dx1 = _pair_start(g_f0, dx1, "rs_pair_start_f0")
    dycat = _mm_nt(dx1, wout0, "mm_ab_out_bwd")
    g_wout0 = rows8(_dw_rows(ycat, dx1, "dw_ab_out"), D // NDEV)
    s_f0 = pair_sums(g_f0, hd_f0, g_wout0, "f0")
    s_f0, dycat = chip_start(s_f0, [(4, 0), (5, 0)], dycat, "f0")
    dz, g512, dws, dbs = _ab_bwd(dycat, z, yb2, a_ln_g, a_ln_b, wsm, bs_col, bcw, b_ln_g, b_ln_b, "ab_bwd")
    grad_x, dnm0 = _mm_nt_rms(dz, win0, x0, nm[0], dx1, "mm_ab_in_bwd")
    g_ab = [_dw_cols(h0, dz, NDEV, 2 * D // NDEV, "dw_ab_in"), g_wout0]
    hd_ab, _ = _pair_start(g_ab, None, "rs_pair_start_ab")
    s_ab = pair_sums(g_ab, hd_ab, grad_x, "ab")
    s_ab, _ = chip_start(s_ab, [(0, None), (1, None)], None, "ab")

    g1024 = jnp.concatenate([dnm0, dnm1, dnf0, dnf1, dnfin, dccw], axis=0)
    gfc = jnp.concatenate([dfcw0, dfcw1], axis=0).reshape(2 * NG * 2 * 3, FB)
    g1024, g512, dws, dbs, gfc = _small_allreduce(
        [g1024, g512, dws.reshape(HEADS * CHUNK, CHUNK), dbs.reshape(HEADS, CHUNK), gfc], "small_allreduce")
    p_win0, p_wout0, p_cin, p_cout, p_wup, p_wdn = _chip_wait(started, zones, g1024, "rs_chip_wait")

    chip = (2 * xi + yi).astype(jnp.int32).reshape(1)

    def big_update(w, m, v, parts, mine, name):
        shp = w.shape
        w3, m3, v3 = (a.reshape((-1,) + shp[-2:]) for a in (w, m, v))
        p4 = parts.reshape((3,) + w3.shape)
        return [o.reshape(shp) for o in _adam_big(w3, m3, v3, p4, mine, chip, name)]

    u_win0 = big_update(ab_w_in, m_ab_w_in, v_ab_w_in, p_win0, [s_ab[0]], "adam_ab_w_in")
    u_wout0 = big_update(ab_w_out, m_ab_w_out, v_ab_w_out, p_wout0, [s_ab[1]], "adam_ab_w_out")
    u_cin = big_update(c_w_in, m_c_w_in, v_c_w_in, p_cin, [s_c[0]], "adam_c_w_in")
    u_cout = big_update(c_w_out, m_c_w_out, v_c_w_out, p_cout, [s_c[1]], "adam_c_w_out")
    tr_ = lambda a: jnp.swapaxes(a, 1, 2)
    u_wup = [tr_(o) for o in big_update(tr_(f_w_up), tr_(m_f_w_up), tr_(v_f_w_up), p_wup,
                                        [s_f0[0], s_f1[0]], "adam_f_w_up")]
    u_wdn = big_update(f_w_down, m_f_w_down, v_f_w_down, p_wdn, [s_f0[1], s_f1[1]], "adam_f_w_down")

    g_norm_mix = g1024[0:2]
    g_norm_ffn = g1024[2:4]
    g_norm_final = g1024[4:5]
    g_ccw = lax.dynamic_slice(g1024[5:8], (0, dev * (D // NDEV)), (3, D // NDEV))
    g_bcw = lax.dynamic_slice(g512[8:8 + BCONV], (0, dev * (DA // NDEV)), (BCONV, DA // NDEV))
    gfc = gfc.reshape(2, NG, 2, 3, FB)
    g_fcw = lax.dynamic_slice(gfc, (0, dev % NG, dev // NG, 0, 0), (2, 1, 1, 3, FB)).reshape(2, 3, FB)
    small_w = [norm_mix, norm_ffn, nfin, a_ln_g, a_ln_b, a_w_s[0], a_b_s[0], b_conv_w[0], b_conv_b,
               b_ln_g, b_ln_b, c_conv_w[0], f_conv_w]
    small_g = [g_norm_mix, g_norm_ffn, g_norm_final, g512[0:1], g512[1:2],
               dws.reshape(HEADS, CHUNK, CHUNK), dbs, g_bcw, g512[2:3],
               g512[3:4], g512[4:5], g_ccw, g_fcw]
    small_m = [m_norm_mix, m_norm_ffn, m_norm_final.reshape(1, D), m_a_ln_g, m_a_ln_b, m_a_w_s[0], m_a_b_s[0],
               m_b_conv_w[0], m_b_conv_b, m_b_ln_g, m_b_ln_b, m_c_conv_w[0], m_f_conv_w]
    small_v = [v_norm_mix, v_norm_ffn, v_norm_final.reshape(1, D), v_a_ln_g, v_a_ln_b, v_a_w_s[0], v_a_b_s[0],
               v_b_conv_w[0], v_b_conv_b, v_b_ln_g, v_b_ln_b, v_c_conv_w[0], v_f_conv_w]
    upd = _adam_small(small_w, small_g, small_m, small_v, "adam_small")
    ns = len(small_w)
    orig = [norm_mix, norm_ffn, norm_final, a_ln_g, a_ln_b, a_w_s, a_b_s, b_conv_w, b_conv_b,
            b_ln_g, b_ln_b, c_conv_w, f_conv_w]
    sg_out = [g.reshape(o.shape) for g, o in zip(small_g, orig)]
    sd_out = [a.reshape(o.shape) for a, o in zip(upd[0:ns], orig)]
    sm_out = [a.reshape(o.shape) for a, o in zip(upd[ns:2 * ns], orig)]
    sv_out = [a.reshape(o.shape) for a, o in zip(upd[2 * ns:3 * ns], orig)]

    def assemble(small, k):
        return [small[0], small[1], small[2], u_win0[k], small[3], small[4], small[5], small[6], small[7],
                small[8], small[9], small[10], u_wout0[k], u_cin[k], small[11], u_cout[k], u_wup[k],
                small[12], u_wdn[k]]

    grads = assemble(sg_out, 0)
    deltas = assemble(sd_out, 1)
    new_m = assemble(sm_out, 2)
    new_v = assemble(sv_out, 3)
    return (loss, grad_x.reshape(1, s, D), *grads, *deltas, *new_m, *new_v)
```

```python
import functools
import math

import jax
import jax.numpy as jnp
from jax import lax
from jax.experimental import pallas as pl
from jax.experimental.pallas import tpu as pltpu

F32 = jnp.float32
BF16 = jnp.bfloat16

D = 1024
DA = 512
HEADS = 4
CHUNK = 128
DFF = 2816
NDEV = 8
NCHIP = 4
FB = DFF * 2 // NDEV
NG = DFF // FB
BCONV = 31
EPS = 1e-6
HALO = 16
HALO_B = 32
VMEM_LIMIT = 52 * 1024 * 1024
INV_SQRT2 = 1.0 / math.sqrt(2.0)
INV_SQRT_2PI = 1.0 / math.sqrt(2.0 * math.pi)

ADAM_LR = 0.001
ADAM_B1 = 0.9
ADAM_B2 = 0.999
ADAM_EPS = 1e-08
ADAM_WD = 0.01
ADAM_STEP = 10

MESH = pl.DeviceIdType.MESH
ANY = pl.BlockSpec(memory_space=pl.ANY)
NT_DIMS = (((1,), (1,)), ((), ()))
TN_DIMS = (((0,), (0,)), ((), ()))


def _params(*sem):
    return pltpu.CompilerParams(dimension_semantics=sem, vmem_limit_bytes=VMEM_LIMIT)


def _tile(s, want):
    return min(want, s)


def _sigmoid(x):
    return jax.nn.sigmoid(x)


def _dsilu(x, sg):
    return sg * (1.0 + x * (1.0 - sg))


def _gelu(x):
    return 0.5 * x * (1.0 + lax.erf(x * INV_SQRT2))


def _dgelu(x):
    return 0.5 * (1.0 + lax.erf(x * INV_SQRT2)) + x * jnp.exp(-0.5 * x * x) * INV_SQRT_2PI


def _ln_fwd(x, g, b):
    mu = jnp.mean(x, axis=-1, keepdims=True)
    xc = x - mu
    var = jnp.mean(xc * xc, axis=-1, keepdims=True)
    rstd = lax.rsqrt(var + EPS)
    xhat = xc * rstd
    return xhat * g + b, xhat, rstd


def _ln_bwd(dy, xhat, rstd, g):
    dxh = dy * g
    m1 = jnp.mean(dxh, axis=-1, keepdims=True)
    m2 = jnp.mean(dxh * xhat, axis=-1, keepdims=True)
    return rstd * (dxh - m1 - xhat * m2)


def _rms_bwd_math(dh, x, g):
    r = lax.rsqrt(jnp.mean(x * x, axis=-1, keepdims=True) + EPS)
    xhat = x * r
    dg = jnp.sum(dh * xhat, axis=0, keepdims=True)
    u = dh * g
    dx = r * (u - xhat * jnp.mean(u * xhat, axis=-1, keepdims=True))
    return dx, dg


def _conv3(xe, cw, halo):
    x0 = xe[halo:]
    x1 = pltpu.roll(xe, 1, 0)[halo:]
    x2 = pltpu.roll(xe, 2, 0)[halo:]
    return cw[2] * x0 + cw[1] * x1 + cw[0] * x2, (x0, x1, x2)


def _conv3_bwd_in(dce, cw, ts):
    n = dce.shape[0]
    d1 = pltpu.roll(dce, n - 1, 0)[:ts]
    d2 = pltpu.roll(dce, n - 2, 0)[:ts]
    return cw[2] * dce[:ts] + cw[1] * d1 + cw[0] * d2


def _conv3_bwd_w(dc, taps):
    x0, x1, x2 = taps
    return [jnp.sum(dc * x2, axis=0, keepdims=True), jnp.sum(dc * x1, axis=0, keepdims=True),
            jnp.sum(dc * x0, axis=0, keepdims=True)]


def _rms_fwd(x, g, name, after=None):
    s = x.shape[0]
    ts = _tile(s, 512)

    def body(x_ref, g_ref, *rest):
        h_ref = rest[-1]
        xv = x_ref[...]
        r = lax.rsqrt(jnp.mean(xv * xv, axis=-1, keepdims=True) + EPS)
        h_ref[...] = (xv * r * g_ref[...]).astype(BF16)

    extra = [] if after is None else [after]
    return pl.pallas_call(
        body, grid=(s // ts,), name=name,
        in_specs=[pl.BlockSpec((ts, D), lambda i: (i, 0)), pl.BlockSpec((1, D), lambda i: (0, 0))]
        + [ANY] * len(extra),
        out_specs=pl.BlockSpec((ts, D), lambda i: (i, 0)),
        out_shape=jax.ShapeDtypeStruct((s, D), BF16),
        compiler_params=_params("parallel"),
    )(x, g, *extra)


def _mm_in(h, wblk, name):
    s = h.shape[0]
    nb, _, bn = wblk.shape
    ts = _tile(s, 512)

    def body(h_ref, w_ref, o_ref):
        hv = h_ref[...]
        for b in range(nb):
            o_ref[:, b * bn:(b + 1) * bn] = jnp.dot(hv, w_ref[b], preferred_element_type=F32).astype(BF16)

    return pl.pallas_call(
        body, grid=(s // ts,), name=name,
        in_specs=[pl.BlockSpec((ts, D), lambda i: (i, 0)), pl.BlockSpec((nb, D, bn), lambda i: (0, 0, 0))],
        out_specs=pl.BlockSpec((ts, nb * bn), lambda i: (i, 0)),
        out_shape=jax.ShapeDtypeStruct((s, nb * bn), BF16),
        compiler_params=_params("parallel"),
    )(h, wblk)


def _mm_out(y, w, xres, name):
    s = y.shape[0]
    ts = _tile(s, 512)

    def body(y_ref, w_ref, x_ref, o_ref):
        o_ref[...] = x_ref[...] + jnp.dot(y_ref[...], w_ref[...], preferred_element_type=F32)

    return pl.pallas_call(
        body, grid=(s // ts,), name=name,
        in_specs=[pl.BlockSpec((ts, D), lambda i: (i, 0)), pl.BlockSpec((D, D), lambda i: (0, 0)),
                  pl.BlockSpec((ts, D), lambda i: (i, 0))],
        out_specs=pl.BlockSpec((ts, D), lambda i: (i, 0)),
        out_shape=jax.ShapeDtypeStruct((s, D), F32),
        compiler_params=_params("parallel"),
    )(y, w, xres)


def _conv31(ue, cw_ref, ts):
    acc = jnp.zeros((ts, ue.shape[1]), F32)
    for r in range(8):
        rolled = ue if r == 0 else pltpu.roll(ue, r, 0)
        for q in range(4):
            sh = 8 * q + r
            if sh >= BCONV:
                continue
            k = BCONV - 1 - sh
            acc = acc + cw_ref[k:k + 1, :] * rolled[HALO_B - 8 * q:HALO_B - 8 * q + ts]
    return acc


def _ab_fwd(z, lga, lba, wsm, bs_col, cwb, cbb, lgb, lbb, name):
    s = z.shape[0]
    ts = _tile(s, 256)
    hb = ts // HALO_B

    def body(z_ref, zh_ref, lga_ref, lba_ref, ws_ref, bs_ref, cw_ref, cb_ref, lgb_ref, lbb_ref,
             y_ref, yb2_ref):
        i = pl.program_id(0)
        z_t = z_ref[...].astype(F32)
        gu = _gelu(z_t[:, 0:DA])
        gv = _gelu(z_t[:, DA:2 * DA])
        vn, _, _ = _ln_fwd(gv, lga_ref[...], lba_ref[...])
        vnb = vn.astype(BF16)
        for c in range(ts // CHUNK):
            for h in range(HEADS):
                rs = slice(c * CHUNK, (c + 1) * CHUNK)
                cs = slice(h * CHUNK, (h + 1) * CHUNK)
                mixed = jnp.dot(ws_ref[h], vnb[rs, cs], preferred_element_type=F32) + bs_ref[h]
                y_ref[rs, cs] = (gu[rs, cs] * mixed).astype(BF16)
        zh = jnp.where(i > 0, zh_ref[...], jnp.zeros_like(zh_ref[...])).astype(F32)
        xb = jnp.concatenate([zh[:, 0:DA], z_t[:, 2 * DA:3 * DA]], axis=0)
        gb = jnp.concatenate([zh[:, DA:2 * DA], z_t[:, 3 * DA:4 * DA]], axis=0)
        u = xb * _sigmoid(gb)
        conv = _conv31(u, cw_ref, ts) + cb_ref[...]
        yb2_ref[...] = conv
        nb_, _, _ = _ln_fwd(conv, lgb_ref[...], lbb_ref[...])
        y_ref[:, DA:2 * DA] = (nb_ * _sigmoid(nb_)).astype(BF16)

    row = lambda i: (0, 0)
    return pl.pallas_call(
        body, grid=(s // ts,), name=name,
        in_specs=[pl.BlockSpec((ts, 4 * DA), lambda i: (i, 0)),
                  pl.BlockSpec((HALO_B, 2 * DA), lambda i: (jnp.maximum(i * hb - 1, 0), 1)),
                  pl.BlockSpec((1, DA), row), pl.BlockSpec((1, DA), row),
                  pl.BlockSpec((HEADS, CHUNK, CHUNK), lambda i: (0, 0, 0)),
                  pl.BlockSpec((HEADS, CHUNK, 1), lambda i: (0, 0, 0)),
                  pl.BlockSpec((BCONV, DA), row), pl.BlockSpec((1, DA), row),
                  pl.BlockSpec((1, DA), row), pl.BlockSpec((1, DA), row)],
        out_specs=[pl.BlockSpec((ts, 2 * DA), lambda i: (i, 0)), pl.BlockSpec((ts, DA), lambda i: (i, 0))],
        out_shape=[jax.ShapeDtypeStruct((s, 2 * DA), BF16), jax.ShapeDtypeStruct((s, DA), F32)],
        compiler_params=_params("parallel"),
    )(z, z, lga, lba, wsm, bs_col, cwb, cbb, lgb, lbb)


def _c_fwd(zc, cw, name):
    s = zc.shape[0]
    ts = _tile(s, 512)
    hb = ts // HALO

    def body(z_ref, ch_ref, xh_ref, cw_ref, r_ref):
        i = pl.program_id(0)
        z_t = z_ref[...].astype(F32)
        ph = jnp.where(i > 0, ch_ref[...].astype(F32) * xh_ref[...].astype(F32), 0.0)
        pe = jnp.concatenate([ph, z_t[:, D:2 * D] * z_t[:, 2 * D:3 * D]], axis=0)
        q, _ = _conv3(pe, [cw_ref[k:k + 1, :] for k in range(3)], HALO)
        r_ref[...] = (z_t[:, 0:D] * q).astype(BF16)

    halo = lambda col: pl.BlockSpec((HALO, D), lambda i: (jnp.maximum(i * hb - 1, 0), col))
    return pl.pallas_call(
        body, grid=(s // ts,), name=name,
        in_specs=[pl.BlockSpec((ts, 3 * D), lambda i: (i, 0)), halo(1), halo(2),
                  pl.BlockSpec((3, D), lambda i: (0, 0))],
        out_specs=pl.BlockSpec((ts, D), lambda i: (i, 0)),
        out_shape=jax.ShapeDtypeStruct((s, D), BF16),
        compiler_params=_params("parallel"),
    )(zc, zc, zc, cw)


def _ffn_fwd(h, xres, wup, fcw, wdn, name):
    s = h.shape[0]
    ts = _tile(s, 512)
    hb = ts // HALO

    def body(h_ref, hh_ref, w_ref, cw_ref, wd_ref, x_ref, up_ref, xo_ref):
        i = pl.program_id(0)
        m = pl.program_id(1)
        halo = jnp.where(i > 0, hh_ref[...], jnp.zeros_like(hh_ref[...]))
        hx = jnp.concatenate([halo, h_ref[...]], axis=0)
        acts = []
        for gv in range(2):
            up = jnp.dot(hx, w_ref[gv], preferred_element_type=F32)
            up_ref[gv] = up[HALO:].astype(BF16)
            upc, _ = _conv3(up, [cw_ref[gv, k:k + 1, :] for k in range(3)], HALO)
            acts.append(upc)
        a = acts[0] * _sigmoid(acts[0]) * acts[1]
        f = jnp.dot(a.astype(BF16), wd_ref[...], preferred_element_type=F32)

        @pl.when(m == 0)
        def _():
            xo_ref[...] = x_ref[...] + f

        @pl.when(m > 0)
        def _():
            xo_ref[...] += f

    return pl.pallas_call(
        body, grid=(s // ts, NG), name=name,
        in_specs=[pl.BlockSpec((ts, D), lambda i, m: (i, 0)),
                  pl.BlockSpec((HALO, D), lambda i, m: (jnp.maximum(i * hb - 1, 0), 0)),
                  pl.BlockSpec((2, None, D, FB), lambda i, m: (0, m, 0, 0)),
                  pl.BlockSpec((2, None, 3, FB), lambda i, m: (0, m, 0, 0)),
                  pl.BlockSpec((FB, D), lambda i, m: (m, 0)),
                  pl.BlockSpec((ts, D), lambda i, m: (i, 0))],
        out_specs=[pl.BlockSpec((None, 2, ts, FB), lambda i, m: (m, 0, i, 0)),
                   pl.BlockSpec((ts, D), lambda i, m: (i, 0))],
        out_shape=[jax.ShapeDtypeStruct((NG, 2, s, FB), BF16), jax.ShapeDtypeStruct((s, D), F32)],
        compiler_params=_params("arbitrary", "arbitrary"),
    )(h, h, wup, fcw, wdn, xres)


def _final(x, tgt, g, name):
    s = x.shape[0]
    ts = _tile(s, 512)

    def body(x_ref, t_ref, g_ref, dx_ref, dg_ref, loss_ref):
        i = pl.program_id(0)
        xv = x_ref[...]
        gv = g_ref[...]
        r = lax.rsqrt(jnp.mean(xv * xv, axis=-1, keepdims=True) + EPS)
        xhat = xv * r
        e = xhat * gv - t_ref[...]
        part = 0.5 * jnp.sum(jnp.mean(e * e, axis=-1, keepdims=True), axis=0, keepdims=True)
        dy = e * (1.0 / D)
        dgp = jnp.sum(dy * xhat, axis=0, keepdims=True)
        u = dy * gv
        dx_ref[...] = r * (u - xhat * jnp.mean(u * xhat, axis=-1, keepdims=True))

        @pl.when(i == 0)
        def _():
            dg_ref[...] = dgp
            loss_ref[...] = jnp.broadcast_to(part, (1, 128))

        @pl.when(i > 0)
        def _():
            dg_ref[...] += dgp
            loss_ref[...] += jnp.broadcast_to(part, (1, 128))

    return pl.pallas_call(
        body, grid=(s // ts,), name=name,
        in_specs=[pl.BlockSpec((ts, D), lambda i: (i, 0)), pl.BlockSpec((ts, D), lambda i: (i, 0)),
                  pl.BlockSpec((1, D), lambda i: (0, 0))],
        out_specs=[pl.BlockSpec((ts, D), lambda i: (i, 0)), pl.BlockSpec((1, D), lambda i: (0, 0)),
                   pl.BlockSpec((1, 128), lambda i: (0, 0))],
        out_shape=[jax.ShapeDtypeStruct((s, D), F32), jax.ShapeDtypeStruct((1, D), F32),
                   jax.ShapeDtypeStruct((1, 128), F32)],
        compiler_params=_params("arbitrary"),
    )(x, tgt, g)


def _ffn_bwd(df, up, wup, fcw, wdn, xin, g, name):
    s = df.shape[0]
    ts = _tile(s, 512)
    nt = s // ts
    hb = ts // HALO

    def body(df_ref, up_ref, uph_ref, w_ref, cw_ref, wd_ref, x_ref, g_ref,
             a_ref, dup_ref, dx_ref, dg_ref, dcw_ref, carry, acc):
        i = pl.program_id(0)
        m = pl.program_id(1)
        ti = nt - 1 - i
        first = i == 0
        dfb = df_ref[...].astype(BF16)
        cws = [[cw_ref[gv, k:k + 1, :] for k in range(3)] for gv in range(2)]
        upc, taps = [], []
        for gv in range(2):
            halo = jnp.where(ti > 0, uph_ref[gv], jnp.zeros_like(uph_ref[gv]))
            upe = jnp.concatenate([halo, up_ref[gv]], axis=0).astype(F32)
            c, t = _conv3(upe, cws[gv], HALO)
            upc.append(c)
            taps.append(t)
        sg = _sigmoid(upc[0])
        sl = upc[0] * sg
        a_ref[...] = (sl * upc[1]).astype(BF16)
        da = lax.dot_general(dfb, wd_ref[...], NT_DIMS, preferred_element_type=F32)
        dcs = [da * upc[1] * _dsilu(upc[0], sg), da * sl]

        @pl.when(first)
        def _():
            carry[m] = jnp.zeros((2, 8, FB), F32)

        dh = jnp.zeros((ts, D), F32)
        for gv in range(2):
            dc = dcs[gv]
            rows = _conv3_bwd_w(dc, taps[gv])

            @pl.when(first)
            def _():
                for k in range(3):
                    dcw_ref[m, gv, k:k + 1, :] = rows[k]

            @pl.when(jnp.logical_not(first))
            def _():
                for k in range(3):
                    dcw_ref[m, gv, k:k + 1, :] += rows[k]

            dce = jnp.concatenate([dc, carry[m, gv]], axis=0)
            du = _conv3_bwd_in(dce, cws[gv], ts)
            carry[m, gv] = dc[0:8]
            dub = du.astype(BF16)
            dup_ref[gv] = dub
            dh = dh + lax.dot_general(dub, w_ref[gv], NT_DIMS, preferred_element_type=F32)

        @pl.when(m == 0)
        def _():
            acc[...] = dh

        @pl.when(m > 0)
        def _():
            acc[...] += dh

        @pl.when(m == NG - 1)
        def _():
            dx, dgp = _rms_bwd_math(acc[...], x_ref[...], g_ref[...])
            dx_ref[...] = df_ref[...] + dx

            @pl.when(first)
            def _():
                dg_ref[...] = dgp

            @pl.when(jnp.logical_not(first))
            def _():
                dg_ref[...] += dgp

    rev = lambda i: nt - 1 - i
    return pl.pallas_call(
        body, grid=(nt, NG), name=name,
        in_specs=[pl.BlockSpec((ts, D), lambda i, m: (rev(i), 0)),
                  pl.BlockSpec((None, 2, ts, FB), lambda i, m: (m, 0, rev(i), 0)),
                  pl.BlockSpec((None, 2, HALO, FB), lambda i, m: (m, 0, jnp.maximum(rev(i) * hb - 1, 0), 0)),
                  pl.BlockSpec((2, None, D, FB), lambda i, m: (0, m, 0, 0)),
                  pl.BlockSpec((2, None, 3, FB), lambda i, m: (0, m, 0, 0)),
                  pl.BlockSpec((FB, D), lambda i, m: (m, 0)),
                  pl.BlockSpec((ts, D), lambda i, m: (rev(i), 0)),
                  pl.BlockSpec((1, D), lambda i, m: (0, 0))],
        out_specs=[pl.BlockSpec((None, ts, FB), lambda i, m: (m, rev(i), 0)),
                   pl.BlockSpec((None, 2, ts, FB), lambda i, m: (m, 0, rev(i), 0)),
                   pl.BlockSpec((ts, D), lambda i, m: (rev(i), 0)),
                   pl.BlockSpec((1, D), lambda i, m: (0, 0)),
                   pl.BlockSpec((NG, 2, 3, FB), lambda i, m: (0, 0, 0, 0))],
        out_shape=[jax.ShapeDtypeStruct((NG, s, FB), BF16), jax.ShapeDtypeStruct((NG, 2, s, FB), BF16),
                   jax.ShapeDtypeStruct((s, D), F32), jax.ShapeDtypeStruct((1, D), F32),
                   jax.ShapeDtypeStruct((NG, 2, 3, FB), F32)],
        scratch_shapes=[pltpu.VMEM((NG, 2, 8, FB), F32), pltpu.VMEM((ts, D), F32)],
        compiler_params=_params("arbitrary", "arbitrary"),
    )(df, up, up, wup, fcw, wdn, xin, g)


def _mm_nt(dy, w, name):
    s = dy.shape[0]
    ts = _tile(s, 512)

    def body(dy_ref, w_ref, o_ref):
        o_ref[...] = lax.dot_general(dy_ref[...].astype(BF16), w_ref[...], NT_DIMS,
                                     preferred_element_type=F32).astype(BF16)

    return pl.pallas_call(
        body, grid=(s // ts,), name=name,
        in_specs=[pl.BlockSpec((ts, D), lambda i: (i, 0)), pl.BlockSpec((D, D), lambda i: (0, 0))],
        out_specs=pl.BlockSpec((ts, D), lambda i: (i, 0)),
        out_shape=jax.ShapeDtypeStruct((s, D), BF16),
        compiler_params=_params("parallel"),
    )(dy, w)


def _mm_nt_rms(dy, wblk, x, g, dres, name):
    s = dy.shape[0]
    nb, _, bn = wblk.shape
    ts = _tile(s, 512)

    def body(dy_ref, w_ref, x_ref, g_ref, dr_ref, dx_ref, dg_ref):
        i = pl.program_id(0)
        acc = jnp.zeros((ts, D), F32)
        for b in range(nb):
            acc = acc + lax.dot_general(dy_ref[:, b * bn:(b + 1) * bn], w_ref[b], NT_DIMS,
                                        preferred_element_type=F32)
        dx, dgp = _rms_bwd_math(acc, x_ref[...], g_ref[...])
        dx_ref[...] = dr_ref[...] + dx

        @pl.when(i == 0)
        def _():
            dg_ref[...] = dgp

        @pl.when(i > 0)
        def _():
            dg_ref[...] += dgp

    return pl.pallas_call(
        body, grid=(s // ts,), name=name,
        in_specs=[pl.BlockSpec((ts, nb * bn), lambda i: (i, 0)), pl.BlockSpec((nb, D, bn), lambda i: (0, 0, 0)),
                  pl.BlockSpec((ts, D), lambda i: (i, 0)), pl.BlockSpec((1, D), lambda i: (0, 0)),
                  pl.BlockSpec((ts, D), lambda i: (i, 0))],
        out_specs=[pl.BlockSpec((ts, D), lambda i: (i, 0)), pl.BlockSpec((1, D), lambda i: (0, 0))],
        out_shape=[jax.ShapeDtypeStruct((s, D), F32), jax.ShapeDtypeStruct((1, D), F32)],
        compiler_params=_params("arbitrary"),
    )(dy, wblk, x, g, dres)


def _c_bwd(dr, zc, cw, name):
    s = dr.shape[0]
    ts = _tile(s, 512)
    nt = s // ts
    hb = ts // HALO

    def body(dr_ref, drf_ref, z_ref, ch_ref, xh_ref, bf_ref, cw_ref, dz_ref, dcw_ref):
        i = pl.program_id(0)
        cwv = [cw_ref[k:k + 1, :] for k in range(3)]
        z_t = z_ref[...].astype(F32)
        bg, cg, xv = z_t[:, 0:D], z_t[:, D:2 * D], z_t[:, 2 * D:3 * D]
        ph = jnp.where(i > 0, ch_ref[...].astype(F32) * xh_ref[...].astype(F32), 0.0)
        pe = jnp.concatenate([ph, cg * xv], axis=0)
        q, taps = _conv3(pe, cwv, HALO)
        drv = dr_ref[...].astype(F32)
        dq = drv * bg
        dqf = jnp.where(i < nt - 1, drf_ref[...].astype(F32) * bf_ref[...].astype(F32), 0.0)
        dp = _conv3_bwd_in(jnp.concatenate([dq, dqf], axis=0), cwv, ts)
        dz_ref[:, 0:D] = (drv * q).astype(BF16)
        dz_ref[:, D:2 * D] = (dp * xv).astype(BF16)
        dz_ref[:, 2 * D:3 * D] = (dp * cg).astype(BF16)
        rows = _conv3_bwd_w(dq, taps)

        @pl.when(i == 0)
        def _():
            for k in range(3):
                dcw_ref[k:k + 1, :] = rows[k]

        @pl.when(i > 0)
        def _():
            for k in range(3):
                dcw_ref[k:k + 1, :] += rows[k]

    past = lambda col: pl.BlockSpec((HALO, D), lambda i: (jnp.maximum(i * hb - 1, 0), col))
    nxt = lambda i: jnp.minimum((i + 1) * hb, s // HALO - 1)
    return pl.pallas_call(
        body, grid=(nt,), name=name,
        in_specs=[pl.BlockSpec((ts, D), lambda i: (i, 0)),
                  pl.BlockSpec((HALO, D), lambda i: (nxt(i), 0)),
                  pl.BlockSpec((ts, 3 * D), lambda i: (i, 0)), past(1), past(2),
                  pl.BlockSpec((HALO, D), lambda i: (nxt(i), 0)),
                  pl.BlockSpec((3, D), lambda i: (0, 0))],
        out_specs=[pl.BlockSpec((ts, 3 * D), lambda i: (i, 0)), pl.BlockSpec((3, D), lambda i: (0, 0))],
        out_shape=[jax.ShapeDtypeStruct((s, 3 * D), BF16), jax.ShapeDtypeStruct((3, D), F32)],
        compiler_params=_params("arbitrary"),
    )(dr, dr, zc, zc, zc, zc, cw)


G512_ROWS = 40


def _ab_bwd(dy, z, yb2, lga, lba, wsm, bs_col, cwb, lgb, lbb, name):
    s = z.shape[0]
    ts = _tile(s, 256)
    nt = s // ts
    hb = ts // HALO_B
    nch = ts // CHUNK
    tri = None

    def body(z_ref, zh_ref, dy_ref, dyf_ref, yb2_ref, yb2f_ref, lga_ref, lba_ref, ws_ref, bs_ref,
             cw_ref, lgb_ref, lbb_ref, dz_ref, g512_ref, dws_ref, dbs_ref, dvn_ref):
        i = pl.program_id(0)
        last = i == nt - 1

        @pl.when(i == 0)
        def _():
            g512_ref[...] = jnp.zeros((G512_ROWS, DA), F32)
            dws_ref[...] = jnp.zeros((HEADS, CHUNK, CHUNK), F32)
            dbs_ref[...] = jnp.zeros((HEADS, CHUNK, 1), F32)

        def add_row(k, v):
            g512_ref[k:k + 1, :] += v

        z_t = z_ref[...].astype(F32)
        dy_t = dy_ref[...].astype(F32)
        ua, va = z_t[:, 0:DA], z_t[:, DA:2 * DA]
        gu = _gelu(ua)
        gv = _gelu(va)
        lga_v = lga_ref[...]
        vn, xhat_a, rstd_a = _ln_fwd(gv, lga_v, lba_ref[...])
        vnb = vn.astype(BF16)
        causal = (lax.broadcasted_iota(jnp.int32, (CHUNK, CHUNK), 0)
                  >= lax.broadcasted_iota(jnp.int32, (CHUNK, CHUNK), 1)).astype(F32)
        for c in range(nch):
            for h in range(HEADS):
                rs = slice(c * CHUNK, (c + 1) * CHUNK)
                cs = slice(h * CHUNK, (h + 1) * CHUNK)
                vblk = vnb[rs, cs]
                mixed = jnp.dot(ws_ref[h], vblk, preferred_element_type=F32) + bs_ref[h]
                dyb_ = dy_t[rs, cs]
                dmix = dyb_ * gu[rs, cs]
                dmb = dmix.astype(BF16)
                dz_ref[rs, cs] = (dyb_ * mixed * _dgelu(ua[rs, cs])).astype(BF16)
                dvn_ref[rs, cs] = lax.dot_general(ws_ref[h], dmb, TN_DIMS, preferred_element_type=F32)
                dws_ref[h] += causal * lax.dot_general(dmb, vblk, NT_DIMS, preferred_element_type=F32)
                dbs_ref[h] += jnp.sum(dmix, axis=1, keepdims=True)
        dvn = dvn_ref[...]
        add_row(0, jnp.sum(dvn * xhat_a, axis=0, keepdims=True))
        add_row(1, jnp.sum(dvn, axis=0, keepdims=True))
        dgv = _ln_bwd(dvn, xhat_a, rstd_a, lga_v)
        dz_ref[:, DA:2 * DA] = (dgv * _dgelu(va)).astype(BF16)
        lgb_v = lgb_ref[...]
        dyb_e = jnp.concatenate(
            [dy_t[:, DA:2 * DA], jnp.where(last, 0.0, dyf_ref[...].astype(F32))], axis=0)
        yb2_e = jnp.concatenate([yb2_ref[...], jnp.where(last, 0.0, yb2f_ref[...])], axis=0)
        n_e, xhat_b, rstd_b = _ln_fwd(yb2_e, lgb_v, lbb_ref[...])
        sgn = _sigmoid(n_e)
        dn = dyb_e * _dsilu(n_e, sgn)
        dy2 = _ln_bwd(dn, xhat_b, rstd_b, lgb_v)
        add_row(2, jnp.sum(dy2[:ts], axis=0, keepdims=True))
        add_row(3, jnp.sum(dn[:ts] * xhat_b[:ts], axis=0, keepdims=True))
        add_row(4, jnp.sum(dn[:ts], axis=0, keepdims=True))
        zh = jnp.where(i > 0, zh_ref[...], jnp.zeros_like(zh_ref[...])).astype(F32)
        xb_t, gb_t = z_t[:, 2 * DA:3 * DA], z_t[:, 3 * DA:4 * DA]
        sgb = _sigmoid(gb_t)
        ue = jnp.concatenate([zh[:, 0:DA] * _sigmoid(zh[:, DA:2 * DA]), xb_t * sgb], axis=0)
        dy2_t = dy2[:ts]
        n_e_rows = ts + HALO_B
        du = jnp.zeros((ts, DA), F32)
        for r in range(8):
            fwd_roll = ue if r == 0 else pltpu.roll(ue, r, 0)
            bwd_roll = dy2 if r == 0 else pltpu.roll(dy2, n_e_rows - r, 0)
            for q in range(4):
                sh = 8 * q + r
                if sh >= BCONV:
                    continue
                k = BCONV - 1 - sh
                du = du + cw_ref[k:k + 1, :] * bwd_roll[8 * q:8 * q + ts]
                add_row(8 + k, jnp.sum(dy2_t * fwd_roll[HALO_B - 8 * q:HALO_B - 8 * q + ts],
                                       axis=0, keepdims=True))
        dz_ref[:, 2 * DA:3 * DA] = (du * sgb).astype(BF16)
        dz_ref[:, 3 * DA:4 * DA] = (du * xb_t * sgb * (1.0 - sgb)).astype(BF16)

    row = lambda i: (0, 0)
    nxt = lambda i: jnp.minimum((i + 1) * hb, s // HALO_B - 1)
    return pl.pallas_call(
        body, grid=(nt,), name=name,
        in_specs=[pl.BlockSpec((ts, 4 * DA), lambda i: (i, 0)),
                  pl.BlockSpec((HALO_B, 2 * DA), lambda i: (jnp.maximum(i * hb - 1, 0), 1)),
                  pl.BlockSpec((ts, 2 * DA), lambda i: (i, 0)),
                  pl.BlockSpec((HALO_B, DA), lambda i: (nxt(i), 1)),
                  pl.BlockSpec((ts, DA), lambda i: (i, 0)),
                  pl.BlockSpec((HALO_B, DA), lambda i: (nxt(i), 0)),
                  pl.BlockSpec((1, DA), row), pl.BlockSpec((1, DA), row),
                  pl.BlockSpec((HEADS, CHUNK, CHUNK), lambda i: (0, 0, 0)),
                  pl.BlockSpec((HEADS, CHUNK, 1), lambda i: (0, 0, 0)),
                  pl.BlockSpec((BCONV, DA), row), pl.BlockSpec((1, DA), row), pl.BlockSpec((1, DA), row)],
        out_specs=[pl.BlockSpec((ts, 4 * DA), lambda i: (i, 0)),
                   pl.BlockSpec((G512_ROWS, DA), row),
                   pl.BlockSpec((HEADS, CHUNK, CHUNK), lambda i: (0, 0, 0)),
                   pl.BlockSpec((HEADS, CHUNK, 1), lambda i: (0, 0, 0))],
        out_shape=[jax.ShapeDtypeStruct((s, 4 * DA), BF16), jax.ShapeDtypeStruct((G512_ROWS, DA), F32),
                   jax.ShapeDtypeStruct((HEADS, CHUNK, CHUNK), F32),
                   jax.ShapeDtypeStruct((HEADS, CHUNK, 1), F32)],
        scratch_shapes=[pltpu.VMEM((ts, DA), F32)],
        compiler_params=_params("arbitrary"),
    )(z, z, dy, dy, yb2, yb2, lga, lba, wsm, bs_col, cwb, lgb, lbb)


def _dw_cols(a, dy, nb, bn, name):
    s = a.shape[0]
    tm = _tile(s, 2048)
    nt = s // tm
    cpb = 4

    def body(a_ref, dy_ref, o_ref, acc):
        t = pl.program_id(1)
        av = a_ref[...]
        for b in range(cpb):
            p = lax.dot_general(av, dy_ref[:, b * bn:(b + 1) * bn], TN_DIMS, preferred_element_type=F32)

            @pl.when(t == 0)
            def _():
                acc[b] = p

            @pl.when(t > 0)
            def _():
                acc[b] += p

        @pl.when(t == nt - 1)
        def _():
            o_ref[...] = acc[...].astype(BF16)

    return pl.pallas_call(
        body, grid=(nb // cpb, nt), name=name,
        in_specs=[pl.BlockSpec((tm, D), lambda j, t: (t, 0)), pl.BlockSpec((tm, cpb * bn), lambda j, t: (t, j))],
        out_specs=pl.BlockSpec((cpb, D, bn), lambda j, t: (j, 0, 0)),
        out_shape=jax.ShapeDtypeStruct((nb, D, bn), BF16),
        scratch_shapes=[pltpu.VMEM((cpb, D, bn), F32)],
        compiler_params=_params("arbitrary", "arbitrary"),
    )(a, dy)


def _dw_rows(a, dy, name):
    s = a.shape[0]
    tm = _tile(s, 2048)
    nt = s // tm
    rb = 512

    def body(a_ref, dy_ref, o_ref, acc):
        t = pl.program_id(1)
        p = lax.dot_general(a_ref[...], dy_ref[...].astype(BF16), TN_DIMS, preferred_element_type=F32)

        @pl.when(t == 0)
        def _():
            acc[...] = p

        @pl.when(t > 0)
        def _():
            acc[...] += p

        @pl.when(t == nt - 1)
        def _():
            o_ref[...] = acc[...].astype(BF16)

    return pl.pallas_call(
        body, grid=(D // rb, nt), name=name,
        in_specs=[pl.BlockSpec((tm, rb), lambda j, t: (t, j)), pl.BlockSpec((tm, D), lambda j, t: (t, 0))],
        out_specs=pl.BlockSpec((rb, D), lambda j, t: (j, 0)),
        out_shape=jax.ShapeDtypeStruct((D, D), BF16),
        scratch_shapes=[pltpu.VMEM((rb, D), F32)],
        compiler_params=_params("arbitrary", "arbitrary"),
    )(a, dy)


def _dw_up(h, dup, name):
    s = h.shape[0]
    tm = _tile(s, 2048)
    nt = s // tm

    def body(h_ref, d_ref, o_ref, acc):
        t = pl.program_id(1)
        p = lax.dot_general(d_ref[...], h_ref[...], TN_DIMS, preferred_element_type=F32)

        @pl.when(t == 0)
        def _():
            acc[...] = p

        @pl.when(t > 0)
        def _():
            acc[...] += p

        @pl.when(t == nt - 1)
        def _():
            o_ref[...] = acc[...].astype(BF16)

    return pl.pallas_call(
        body, grid=(NDEV, nt), name=name,
        in_specs=[pl.BlockSpec((tm, D), lambda b, t: (t, 0)),
                  pl.BlockSpec((None, None, tm, FB), lambda b, t: (b % NG, b // NG, t, 0))],
        out_specs=pl.BlockSpec((None, FB, D), lambda b, t: (b, 0, 0)),
        out_shape=jax.ShapeDtypeStruct((NDEV, FB, D), BF16),
        scratch_shapes=[pltpu.VMEM((FB, D), F32)],
        compiler_params=_params("arbitrary", "arbitrary"),
    )(h, dup)


def _dw_dn(a, df, name):
    s = df.shape[0]
    tm = _tile(s, 2048)
    nt = s // tm

    def body(a_ref, d_ref, o_ref, acc):
        t = pl.program_id(1)
        p = lax.dot_general(a_ref[...], d_ref[...].astype(BF16), TN_DIMS, preferred_element_type=F32)

        @pl.when(t == 0)
        def _():
            acc[...] = p

        @pl.when(t > 0)
        def _():
            acc[...] += p

        @pl.when(t == nt - 1)
        def _():
            o_ref[...] = acc[...].astype(BF16)

    return pl.pallas_call(
        body, grid=(NG, nt), name=name,
        in_specs=[pl.BlockSpec((None, tm, FB), lambda m, t: (m, t, 0)), pl.BlockSpec((tm, D), lambda m, t: (t, 0))],
        out_specs=pl.BlockSpec((FB, D), lambda m, t: (m, 0)),
        out_shape=jax.ShapeDtypeStruct((DFF, D), BF16),
        scratch_shapes=[pltpu.VMEM((FB, D), F32)],
        compiler_params=_params("arbitrary", "arbitrary"),
    )(a, df)


def _place():
    x, y, c = lax.axis_index("x"), lax.axis_index("y"), lax.axis_index("c")
    chips = [(1 - x, y), (x, 1 - y), (1 - x, 1 - y)]
    return x, y, c, chips


def _zone(shard, dev):
    return lax.dynamic_update_slice(lax.empty((NDEV,) + shard.shape, shard.dtype), shard[None],
                                    (dev,) + (0,) * shard.ndim)


def _all_gather(zones, name):
    nt = len(zones)

    def body(*refs):
        dsts = refs[nt:2 * nt]
        send_sems, recv_sems = refs[2 * nt:]
        x, y, c, chips = _place()
        me, sib = (x, y, c), (x, y, 1 - c)

        def blk(t, p):
            return dsts[t].at[4 * p[0] + 2 * p[1] + p[2]]

        def copy(t, k, block, to):
            return pltpu.make_async_remote_copy(
                src_ref=blk(t, block), dst_ref=blk(t, block),
                send_sem=send_sems.at[t, k], recv_sem=recv_sems.at[t, k],
                device_id=to, device_id_type=MESH)

        first = []
        for t in range(nt):
            first.append(copy(t, 0, me, sib))
            first += [copy(t, 1 + j, me, (*chip, c)) for j, chip in enumerate(chips)]
        for cp in first:
            cp.start()
        passed = []
        for j, chip in enumerate(chips):
            for t in range(nt):
                copy(t, 1 + j, (*chip, c), me).wait_recv()
                cp = copy(t, 4 + j, (*chip, c), sib)
                cp.start()
                passed.append(cp)
        for t in range(nt):
            copy(t, 0, sib, me).wait_recv()
            for j, chip in enumerate(chips):
                copy(t, 4 + j, (*chip, 1 - c), me).wait_recv()
        for cp in first + passed:
            cp.wait_send()

    return pl.pallas_call(
        body, name=name,
        in_specs=[ANY] * nt, out_specs=[ANY] * nt,
        out_shape=[jax.ShapeDtypeStruct(a.shape, a.dtype) for a in zones],
        input_output_aliases={t: t for t in range(nt)},
        scratch_shapes=[pltpu.SemaphoreType.DMA((nt, 7)), pltpu.SemaphoreType.DMA((nt, 7))],
        compiler_params=pltpu.CompilerParams(has_side_effects=True),
    )(*zones)


HBM_SPEC = pl.BlockSpec(memory_space=pltpu.HBM)
SEM_SPEC = pl.BlockSpec(memory_space=pltpu.SEMAPHORE)
DATAFLOW = pltpu.SideEffectType.DATAFLOW_SIDE_EFFECTING


def _hbm(a):
    return pltpu.with_memory_space_constraint(a, pltpu.HBM)


def _hbm_like(arrs):
    return [pltpu.HBM(a.shape, a.dtype) for a in arrs]


def _ag_start(srcs, lands, after, name):
    n = len(srcs)
    ns = 8 * n

    def body(*refs):
        src, land = refs[:n], refs[n:2 * n]
        sems = refs[2 * n + 1:2 * n + 1 + ns]
        token = refs[-1]
        x, y, c, chips = _place()
        peers = [(x, y, 1 - c)] + [(*chip, c) for chip in chips]
        for t in range(n):
            for k, to in enumerate(peers):
                pltpu.make_async_remote_copy(
                    src_ref=src[t], dst_ref=land[t].at[4 * x + 2 * y + c],
                    send_sem=sems[2 * (4 * t + k)], recv_sem=sems[2 * (4 * t + k) + 1],
                    device_id=to, device_id_type=MESH).start()
        token[...] = jnp.zeros_like(token)

    res = pl.pallas_call(
        body, name=name,
        in_specs=[HBM_SPEC] * (2 * n) + [ANY],
        out_specs=[SEM_SPEC] * ns + [HBM_SPEC] * (2 * n) + [pl.BlockSpec(memory_space=pltpu.VMEM)],
        out_shape=[pltpu.SemaphoreType.DMA(())] * ns + _hbm_like(srcs) + _hbm_like(lands)
        + [jax.ShapeDtypeStruct((8, 128), F32)],
        input_output_aliases={i: ns + i for i in range(2 * n)},
        compiler_params=pltpu.CompilerParams(has_side_effects=DATAFLOW),
    )(*[_hbm(a) for a in srcs], *[_hbm(a) for a in lands], after)
    sems = [[(res[2 * (4 * t + k)], res[2 * (4 * t + k) + 1]) for k in range(4)] for t in range(n)]
    return sems, res[ns:ns + n], res[ns + n:ns + 2 * n], res[-1]


def _ag_forward(srcs, lands, sems1, after, name):
    n = len(srcs)
    flat1 = [s for t in range(n) for k in range(1, 4) for s in sems1[t][k]]
    n1 = len(flat1)

    def body(*refs):
        src, land = refs[:n], refs[n:2 * n]
        s1 = refs[2 * n:2 * n + n1]
        s2 = refs[2 * n + n1 + 1:2 * n + n1 + 1 + 6 * n]
        x, y, c, chips = _place()
        for j, (cx, cy) in enumerate(chips):
            for t in range(n):
                blk = land[t].at[4 * cx + 2 * cy + c]
                pltpu.make_async_remote_copy(
                    src_ref=src[t], dst_ref=blk, send_sem=s1[2 * (3 * t + j)], recv_sem=s1[2 * (3 * t + j) + 1],
                    device_id=(cx, cy, c), device_id_type=MESH).wait_recv()
                pltpu.make_async_remote_copy(
                    src_ref=blk, dst_ref=blk, send_sem=s2[2 * (3 * t + j)], recv_sem=s2[2 * (3 * t + j) + 1],
                    device_id=(x, y, 1 - c), device_id_type=MESH).start()

    res = pl.pallas_call(
        body, name=name,
        in_specs=[HBM_SPEC] * (2 * n) + [SEM_SPEC] * n1 + [ANY],
        out_specs=[SEM_SPEC] * (6 * n) + [HBM_SPEC] * n,
        out_shape=[pltpu.SemaphoreType.DMA(())] * (6 * n) + _hbm_like(lands),
        input_output_aliases={n + i: 6 * n + i for i in range(n)},
        compiler_params=pltpu.CompilerParams(has_side_effects=DATAFLOW),
    )(*srcs, *lands, *flat1, after)
    sems2 = [[(res[2 * (3 * t + j)], res[2 * (3 * t + j) + 1]) for j in range(3)] for t in range(n)]
    return sems2, res[6 * n:]


def _ag_finish(srcs, lands, sems1, sems2, after, name):
    n = len(srcs)
    flat1 = [s for t in range(n) for k in range(4) for s in sems1[t][k]]
    flat2 = [s for t in range(n) for j in range(3) for s in sems2[t][j]]
    n1, n2 = len(flat1), len(flat2)

    def body(*refs):
        src, land = refs[:n], refs[n:2 * n]
        s1 = refs[2 * n:2 * n + n1]
        s2 = refs[2 * n + n1:2 * n + n1 + n2]
        x, y, c, chips = _place()
        sib = (x, y, 1 - c)
        for t in range(n):
            own = land[t].at[4 * x + 2 * y + 1 - c]
            pltpu.make_async_remote_copy(
                src_ref=src[t], dst_ref=own, send_sem=s1[8 * t], recv_sem=s1[8 * t + 1],
                device_id=sib, device_id_type=MESH).wait_recv()
            for k in range(4):
                pltpu.make_async_remote_copy(
                    src_ref=src[t], dst_ref=own, send_sem=s1[2 * (4 * t + k)], recv_sem=s1[2 * (4 * t + k) + 1],
                    device_id=sib, device_id_type=MESH).wait_send()
            for j, (cx, cy) in enumerate(chips):
                blk = land[t].at[4 * cx + 2 * cy + 1 - c]
                cp = pltpu.make_async_remote_copy(
                    src_ref=blk, dst_ref=blk, send_sem=s2[2 * (3 * t + j)], recv_sem=s2[2 * (3 * t + j) + 1],
                    device_id=sib, device_id_type=MESH)
                cp.wait_send()
                cp.wait_recv()

    return pl.pallas_call(
        body, name=name,
        in_specs=[HBM_SPEC] * (2 * n) + [SEM_SPEC] * (n1 + n2) + [ANY],
        out_specs=[HBM_SPEC] * n,
        out_shape=_hbm_like(lands),
        input_output_aliases={n + i: i for i in range(n)},
        compiler_params=pltpu.CompilerParams(has_side_effects=DATAFLOW),
    )(*srcs, *lands, *flat1, *flat2, after)


def _pair_copies(srcs, dsts, sems):
    x, y, c, _ = _place()
    nt = len(srcs)
    return [pltpu.make_async_remote_copy(
        src_ref=srcs[t].at[2 * j + 1 - c], dst_ref=dsts[t].at[j],
        send_sem=sems[2 * (NCHIP * t + j)], recv_sem=sems[2 * (NCHIP * t + j) + 1],
        device_id=(x, y, 1 - c), device_id_type=MESH) for t in range(nt) for j in range(NCHIP)]


def _pair_start(grads, carry, name):
    nt = len(grads)
    ns = 2 * NCHIP * nt
    zones = [_hbm(lax.empty((NCHIP,) + a.shape[1:], a.dtype)) for a in grads]
    extra = [] if carry is None else [_hbm(carry)]
    ne = len(extra)

    def body(*refs):
        for cp in _pair_copies(refs[:nt], refs[nt:2 * nt], refs[2 * nt + ne:2 * nt + ne + ns]):
            cp.start()

    res = pl.pallas_call(
        body, name=name,
        in_specs=[HBM_SPEC] * (2 * nt + ne),
        out_specs=[SEM_SPEC] * ns + [HBM_SPEC] * (2 * nt + ne),
        out_shape=[pltpu.SemaphoreType.DMA(())] * ns + _hbm_like(grads) + _hbm_like(zones) + _hbm_like(extra),
        input_output_aliases={i: ns + i for i in range(2 * nt + ne)},
        compiler_params=pltpu.CompilerParams(has_side_effects=DATAFLOW),
    )(*[_hbm(a) for a in grads], *zones, *extra)
    handle = (list(res[:ns]), list(res[ns:ns + nt]), list(res[ns + nt:ns + 2 * nt]))
    return handle, (res[ns + 2 * nt] if ne else None)


def _pair_wait(handle, after, name):
    sems, srcs, zones = handle
    nt, ns = len(srcs), len(sems)

    def body(*refs):
        for cp in _pair_copies(refs[:nt], refs[nt:2 * nt], refs[2 * nt:2 * nt + ns]):
            cp.wait_send()
            cp.wait_recv()

    return pl.pallas_call(
        body, name=name,
        in_specs=[HBM_SPEC] * (2 * nt) + [SEM_SPEC] * ns + [ANY],
        out_specs=[HBM_SPEC] * nt,
        out_shape=_hbm_like(zones),
        input_output_aliases={nt + i: i for i in range(nt)},
        compiler_params=pltpu.CompilerParams(has_side_effects=DATAFLOW),
    )(*srcs, *zones, *sems, after)


def _rows_tile(r, row_bytes, cap_bytes):
    best = None
    for tr in range(16, r + 1, 16):
        if r % tr == 0 and tr * row_bytes <= cap_bytes:
            best = tr
    return best if best is not None else r


def _pair_sum(own, got, cidx, name):
    _, _, r, cdim = own.shape
    tr = _rows_tile(r, 2 * cdim, 2 * 1024 * 1024)

    def body(c_ref, a_ref, b_ref, o_ref):
        o_ref[...] = (a_ref[...].astype(F32) + b_ref[...].astype(F32)).astype(BF16)

    return pl.pallas_call(
        body, name=name,
        grid_spec=pltpu.PrefetchScalarGridSpec(
            num_scalar_prefetch=1, grid=(NCHIP, r // tr),
            in_specs=[pl.BlockSpec((None, None, tr, cdim), lambda j, i, c_ref: (j, c_ref[0], i, 0)),
                      pl.BlockSpec((None, tr, cdim), lambda j, i, c_ref: (j, i, 0))],
            out_specs=pl.BlockSpec((None, tr, cdim), lambda j, i, c_ref: (j, i, 0))),
        out_shape=jax.ShapeDtypeStruct((NCHIP, r, cdim), BF16),
        compiler_params=_params("arbitrary", "arbitrary"),
    )(cidx, own, got)


def _chip_copies(srcs, zones, slots, sems):
    x, y, c, chips = _place()
    out = []
    for t, (z, l) in enumerate(slots):
        for k, (cx, cy) in enumerate(chips):
            dst = zones[z].at[k] if l is None else zones[z].at[k, l]
            out.append(pltpu.make_async_remote_copy(
                src_ref=srcs[t].at[2 * cx + cy], dst_ref=dst,
                send_sem=sems[2 * (3 * t + k)], recv_sem=sems[2 * (3 * t + k) + 1],
                device_id=(cx, cy, c), device_id_type=MESH))
    return out


def _chip_start(sums, zones, slots, carry, name):
    nt, nz = len(sums), len(zones)
    ns = 6 * nt
    extra = [] if carry is None else [_hbm(carry)]
    ne = len(extra)

    def body(*refs):
        for cp in _chip_copies(refs[:nt], refs[nt:nt + nz], slots, refs[nt + nz + ne:nt + nz + ne + ns]):
            cp.start()

    res = pl.pallas_call(
        body, name=name,
        in_specs=[HBM_SPEC] * (nt + nz + ne),
        out_specs=[SEM_SPEC] * ns + [HBM_SPEC] * (nt + nz + ne),
        out_shape=[pltpu.SemaphoreType.DMA(())] * ns + _hbm_like(sums) + _hbm_like(zones) + _hbm_like(extra),
        input_output_aliases={i: ns + i for i in range(nt + nz + ne)},
        compiler_params=pltpu.CompilerParams(has_side_effects=DATAFLOW),
    )(*[_hbm(a) for a in sums], *zones, *extra)
    return (list(res[:ns]), list(res[ns:ns + nt]), list(res[ns + nt:ns + nt + nz]),
            (res[ns + nt + nz] if ne else None))


def _chip_wait(started, zones, after, name):
    nz = len(zones)
    flat_src = [a for sums, _, _ in started for a in sums]
    flat_sem = [s for _, _, sems in started for s in sems]
    n_src, n_sem = len(flat_src), len(flat_sem)

    def body(*refs):
        srcs, zs, sems = refs[:n_src], refs[n_src:n_src + nz], refs[n_src + nz:n_src + nz + n_sem]
        so, se = 0, 0
        for sums, slots, sem_list in started:
            for cp in _chip_copies(srcs[so:so + len(sums)], zs, slots, sems[se:se + len(sem_list)]):
                cp.wait_send()
                cp.wait_recv()
            so += len(sums)
            se += len(sem_list)

    return pl.pallas_call(
        body, name=name,
        in_specs=[HBM_SPEC] * (n_src + nz) + [SEM_SPEC] * n_sem + [ANY],
        out_specs=[HBM_SPEC] * nz,
        out_shape=_hbm_like(zones),
        input_output_aliases={n_src + i: i for i in range(nz)},
        compiler_params=pltpu.CompilerParams(has_side_effects=DATAFLOW),
    )(*flat_src, *zones, *flat_sem, after)


def _small_allreduce(parts, name):
    nt = len(parts)

    def body(*refs):
        srcs, outs, bufs = refs[:nt], refs[nt:2 * nt], refs[2 * nt:3 * nt]
        send_sems, recv_sems = refs[3 * nt:]
        x, y, c, _ = _place()
        peers = [(x, y, 1 - c), (1 - x, y, c), (x, 1 - y, c)]
        for t in range(nt):
            outs[t][...] = srcs[t][...]
        for step, peer in enumerate(peers):
            copies = [pltpu.make_async_remote_copy(
                src_ref=outs[t], dst_ref=bufs[t].at[step],
                send_sem=send_sems.at[step, t], recv_sem=recv_sems.at[step, t],
                device_id=peer, device_id_type=MESH) for t in range(nt)]
            for cp in copies:
                cp.start()
            for cp in copies:
                cp.wait()
            for t in range(nt):
                outs[t][...] = outs[t][...] + bufs[t][step]

    vm = pl.BlockSpec(memory_space=pltpu.VMEM)
    return pl.pallas_call(
        body, name=name,
        in_specs=[vm] * nt, out_specs=[vm] * nt,
        out_shape=[jax.ShapeDtypeStruct(a.shape, F32) for a in parts],
        scratch_shapes=[pltpu.VMEM((3,) + a.shape, F32) for a in parts]
        + [pltpu.SemaphoreType.DMA((3, nt)), pltpu.SemaphoreType.DMA((3, nt))],
        compiler_params=pltpu.CompilerParams(has_side_effects=True, vmem_limit_bytes=VMEM_LIMIT),
    )(*parts)


def _adam_math(w, g, m, v):
    m2 = ADAM_B1 * m + (1.0 - ADAM_B1) * g
    v2 = ADAM_B2 * v + (1.0 - ADAM_B2) * (g * g)
    m_hat = m2 / (1.0 - ADAM_B1 ** ADAM_STEP)
    v_hat = v2 / (1.0 - ADAM_B2 ** ADAM_STEP)
    delta = -ADAM_LR * (m_hat / (jnp.sqrt(v_hat) + ADAM_EPS) + ADAM_WD * w)
    return delta, m2, v2


def _adam_big(w, m, v, parts, mine, chip, name):
    nl, r, cdim = w.shape
    tr = _rows_tile(r, 4 * cdim, 3 * 512 * 1024)

    def body(c_ref, w_ref, m_ref, v_ref, p_ref, *rest):
        mine_refs, (g_ref, d_ref, mo_ref, vo_ref) = rest[:nl], rest[nl:]
        own = mine_refs[0][...]
        for l in range(1, nl):
            own = jnp.where(pl.program_id(0) == l, mine_refs[l][...], own)
        g = ((p_ref[0].astype(F32) + p_ref[1].astype(F32)) + p_ref[2].astype(F32)) + own.astype(F32)
        delta, m2, v2 = _adam_math(w_ref[...], g, m_ref[...], v_ref[...])
        g_ref[...] = g
        d_ref[...] = delta
        mo_ref[...] = m2
        vo_ref[...] = v2

    spec = pl.BlockSpec((None, tr, cdim), lambda l, i, c_ref: (l, i, 0))
    mine_specs = [pl.BlockSpec((None, tr, cdim), lambda l, i, c_ref, ll=ll: (c_ref[0], jnp.where(l == ll, i, 0), 0))
                  for ll in range(nl)]
    return pl.pallas_call(
        body, name=name,
        grid_spec=pltpu.PrefetchScalarGridSpec(
            num_scalar_prefetch=1, grid=(nl, r // tr),
            in_specs=[spec, spec, spec, pl.BlockSpec((3, None, tr, cdim), lambda l, i, c_ref: (0, l, i, 0))]
            + mine_specs,
            out_specs=[spec] * 4),
        out_shape=[jax.ShapeDtypeStruct(w.shape, F32)] * 4,
        compiler_params=_params("arbitrary", "arbitrary"),
    )(chip, w, m, v, parts, *mine)


def _adam_small(ws, gs, ms, vs, name):
    n = len(ws)

    def body(*refs):
        w_r, g_r, m_r, v_r = refs[:n], refs[n:2 * n], refs[2 * n:3 * n], refs[3 * n:4 * n]
        d_o, m_o, v_o = refs[4 * n:5 * n], refs[5 * n:6 * n], refs[6 * n:7 * n]
        for t in range(n):
            delta, m2, v2 = _adam_math(w_r[t][...], g_r[t][...], m_r[t][...], v_r[t][...])
            d_o[t][...] = delta
            m_o[t][...] = m2
            v_o[t][...] = v2

    vm = pl.BlockSpec(memory_space=pltpu.VMEM)
    shapes = [jax.ShapeDtypeStruct(a.shape, F32) for a in ws]
    return pl.pallas_call(
        body, name=name, in_specs=[vm] * (4 * n), out_specs=[vm] * (3 * n), out_shape=shapes * 3,
        compiler_params=pltpu.CompilerParams(vmem_limit_bytes=VMEM_LIMIT),
    )(*ws, *gs, *ms, *vs)


def kernel(x, norm_mix, norm_ffn, norm_final, ab_w_in, a_ln_g, a_ln_b, a_w_s, a_b_s, b_conv_w, b_conv_b, b_ln_g, b_ln_b, ab_w_out, c_w_in, c_conv_w, c_w_out, f_w_up, f_conv_w, f_w_down, loss_target, m_norm_mix, m_norm_ffn, m_norm_final, m_ab_w_in, m_a_ln_g, m_a_ln_b, m_a_w_s, m_a_b_s, m_b_conv_w, m_b_conv_b, m_b_ln_g, m_b_ln_b, m_ab_w_out, m_c_w_in, m_c_conv_w, m_c_w_out, m_f_w_up, m_f_conv_w, m_f_w_down, v_norm_mix, v_norm_ffn, v_norm_final, v_ab_w_in, v_a_ln_g, v_a_ln_b, v_a_w_s, v_a_b_s, v_b_conv_w, v_b_conv_b, v_b_ln_g, v_b_ln_b, v_ab_w_out, v_c_w_in, v_c_conv_w, v_c_w_out, v_f_w_up, v_f_conv_w, v_f_w_down):
    s = x.shape[1]
    x0 = x.reshape(s, D)
    tgt = loss_target.reshape(s, D)
    xi, yi, ci = lax.axis_index("x"), lax.axis_index("y"), lax.axis_index("c")
    dev = 4 * xi + 2 * yi + ci
    cidx = ci.astype(jnp.int32).reshape(1)

    bf = lambda a: a.astype(BF16)
    slab_w = 6 * CHUNK
    pad = lambda a, rows: jnp.pad(a, ((0, rows - a.shape[0]), (0, slab_w - a.shape[1])))
    slab = jnp.concatenate([pad(b_conv_w[0], 32), pad(c_conv_w[0], 8), pad(f_conv_w.reshape(6, FB), 8)], axis=0)
    win0, wout0, slab_g = _all_gather(
        [_zone(bf(ab_w_in[0]), dev), _zone(bf(ab_w_out[0]), dev), _zone(slab, dev)], "all_gather_first")
    later = [bf(f_w_up[0]), bf(f_w_down[0]), bf(c_w_in[0]), bf(c_w_out[0]), bf(f_w_up[1]), bf(f_w_down[1])]
    lands = [_zone(a, dev) for a in later]
    groups = [[0, 1], [2, 3], [4, 5]]
    ag_sems, later, lands, ag_token = _ag_start(later, lands, slab_g, "ag_start")
    wout0 = wout0.reshape(D, D)
    bcw = jnp.transpose(slab_g[:, 0:BCONV, 0:DA // NDEV], (1, 0, 2)).reshape(BCONV, DA)
    ccw = jnp.transpose(slab_g[:, 32:35, 0:D // NDEV], (1, 0, 2)).reshape(3, D)
    fcw_g = slab_g[:, 40:46, 0:FB].reshape(2, NG, 2, 3, FB)
    fcws = [fcw_g[:, :, 0], fcw_g[:, :, 1]]

    causal = jnp.tril(jnp.ones((CHUNK, CHUNK), F32))
    wsm = (a_w_s[0] * causal).astype(BF16)
    bs_col = a_b_s.reshape(HEADS, CHUNK, 1)
    nm = [norm_mix[0:1], norm_mix[1:2]]
    nf = [norm_ffn[0:1], norm_ffn[1:2]]
    nfin = norm_final.reshape(1, D)

    def arrive(g, after_ici, after_d2d, tag):
        srcs = [later[t] for t in groups[g]]
        zone = [lands[t] for t in groups[g]]
        sems1 = [ag_sems[t] for t in groups[g]]
        sems2, zone = _ag_forward(srcs, zone, sems1, after_ici, "ag_forward_" + tag)
        return _ag_finish(srcs, zone, sems1, sems2, after_d2d, "ag_finish_" + tag)

    h0 = _rms_fwd(x0, nm[0], "rms_mix0", after=ag_token)
    z = _mm_in(h0, win0, "mm_ab_in")
    ycat, yb2 = _ab_fwd(z, a_ln_g, a_ln_b, wsm, bs_col, bcw, b_conv_b, b_ln_g, b_ln_b, "ab_fwd")
    x1 = _mm_out(ycat, wout0, x0, "mm_ab_out")
    h1 = _rms_fwd(x1, nf[0], "rms_ffn0")
    wup0, wdn0 = arrive(0, x1, h1, "ffn0")
    up0, x2 = _ffn_fwd(h1, x1, wup0.reshape(2, NG, D, FB), fcws[0], wdn0.reshape(DFF, D), "ffn_fwd0")
    h2 = _rms_fwd(x2, nm[1], "rms_mix1")
    cin, cout = arrive(1, x2, h2, "c")
    cout = cout.reshape(D, D)
    zc = _mm_in(h2, cin, "mm_c_in")
    rc = _c_fwd(zc, ccw, "c_fwd")
    x3 = _mm_out(rc, cout, x2, "mm_c_out")
    h3 = _rms_fwd(x3, nf[1], "rms_ffn1")
    wup1, wdn1 = arrive(2, x3, h3, "ffn1")
    wups = [wup0.reshape(2, NG, D, FB), wup1.reshape(2, NG, D, FB)]
    wdns = [wdn0.reshape(DFF, D), wdn1.reshape(DFF, D)]
    up1, x4 = _ffn_fwd(h3, x3, wups[1], fcws[1], wdns[1], "ffn_fwd1")
    dx4, dnfin, loss_part = _final(x4, tgt, nfin, "final_loss")
    loss = lax.psum(loss_part[0, 0], ("x", "y", "c"))

    zshape = lambda *sh: _hbm(lax.empty((3,) + sh, BF16))
    zones = [zshape(D, 2 * D // NDEV), zshape(D // NDEV, D), zshape(D, 3 * D // NDEV), zshape(D // NDEV, D),
             zshape(2, FB, D), zshape(2, DFF // NDEV, D)]
    started = []

    def pair_sums(grads, handle, after, tag):
        del grads
        got = _pair_wait(handle, after, "rs_pair_wait_" + tag)
        return [_pair_sum(b.reshape((NCHIP, 2) + b.shape[1:]), g, cidx, "rs_pair_sum_%s%d" % (tag, t))
                for t, (b, g) in enumerate(zip(handle[1], got))]

    def chip_start(sums, slots, carry, tag):
        sems, sums, new_zones, carry = _chip_start(sums, zones, slots, carry, "rs_chip_start_" + tag)
        zones[:] = new_zones
        started.append((sums, slots, sems))
        return sums, carry

    rows8 = lambda g, r: g.reshape(NDEV, r, D)
    a1, dup1, dx3, dnf1, dfcw1 = _ffn_bwd(dx4, up1, wups[1], fcws[1], wdns[1], x3, nf[1], "ffn_bwd1")
    g_f1 = [_dw_up(h3, dup1, "dw_up1"), rows8(_dw_dn(a1, dx4, "dw_dn1"), DFF // NDEV)]
    hd_f1, dx3 = _pair_start(g_f1, dx3, "rs_pair_start_f1")
    drc = _mm_nt(dx3, cout, "mm_c_out_bwd")
    g_cout = rows8(_dw_rows(rc, dx3, "dw_c_out"), D // NDEV)
    s_f1 = pair_sums(g_f1, hd_f1, g_cout, "f1")
    s_f1, drc = chip_start(s_f1, [(4, 1), (5, 1)], drc, "f1")
    dzc, dccw = _c_bwd(drc, zc, ccw, "c_bwd")
    dx2, dnm1 = _mm_nt_rms(dzc, cin, x2, nm[1], dx3, "mm_c_in_bwd")
    g_c = [_dw_cols(h2, dzc, NDEV, 3 * D // NDEV, "dw_c_in"), g_cout]
    hd_c, dx2 = _pair_start(g_c, dx2, "rs_pair_start_c")
    a0, dup0, dx1, dnf0, dfcw0 = _ffn_bwd(dx2, up0, wups[0], fcws[0], wdns[0], x1, nf[0], "ffn_bwd0")
    s_c = pair_sums(g_c, hd_c, dx1, "c")
    s_c, dx1 = chip_start(s_c, [(2, None), (3, None)], dx1, "c")
    g_f0 = [_dw_up(h1, dup0, "dw_up0"), rows8(_dw_dn(a0, dx2, "dw_dn0"), DFF // NDEV)]
    hd_f0, dx1 = _pair_start(g_f0, dx1, "rs_pair_start_f0")
    dycat = _mm_nt(dx1, wout0, "mm_ab_out_bwd")
    g_wout0 = rows8(_dw_rows(ycat, dx1, "dw_ab_out"), D // NDEV)
    s_f0 = pair_sums(g_f0, hd_f0, g_wout0, "f0")
    s_f0, dycat = chip_start(s_f0, [(4, 0), (5, 0)], dycat, "f0")
    dz, g512, dws, dbs = _ab_bwd(dycat, z, yb2, a_ln_g, a_ln_b, wsm, bs_col, bcw, b_ln_g, b_ln_b, "ab_bwd")
    grad_x, dnm0 = _mm_nt_rms(dz, win0, x0, nm[0], dx1, "mm_ab_in_bwd")
    g_ab = [_dw_cols(h0, dz, NDEV, 2 * D // NDEV, "dw_ab_in"), g_wout0]
    hd_ab, _ = _pair_start(g_ab, None, "rs_pair_start_ab")
    s_ab = pair_sums(g_ab, hd_ab, grad_x, "ab")
    s_ab, _ = chip_start(s_ab, [(0, None), (1, None)], None, "ab")

    g1024 = jnp.concatenate([dnm0, dnm1, dnf0, dnf1, dnfin, dccw], axis=0)
    gfc = jnp.concatenate([dfcw0, dfcw1], axis=0).reshape(2 * NG * 2 * 3, FB)
    g1024, g512, dws, dbs, gfc = _small_allreduce(
        [g1024, g512, dws.reshape(HEADS * CHUNK, CHUNK), dbs.reshape(HEADS, CHUNK), gfc], "small_allreduce")
    p_win0, p_wout0, p_cin, p_cout, p_wup, p_wdn = _chip_wait(started, zones, g1024, "rs_chip_wait")

    chip = (2 * xi + yi).astype(jnp.int32).reshape(1)

    def big_update(w, m, v, parts, mine, name):
        shp = w.shape
        w3, m3, v3 = (a.reshape((-1,) + shp[-2:]) for a in (w, m, v))
        p4 = parts.reshape((3,) + w3.shape)
        return [o.reshape(shp) for o in _adam_big(w3, m3, v3, p4, mine, chip, name)]

    u_win0 = big_update(ab_w_in, m_ab_w_in, v_ab_w_in, p_win0, [s_ab[0]], "adam_ab_w_in")
    u_wout0 = big_update(ab_w_out, m_ab_w_out, v_ab_w_out, p_wout0, [s_ab[1]], "adam_ab_w_out")
    u_cin = big_update(c_w_in, m_c_w_in, v_c_w_in, p_cin, [s_c[0]], "adam_c_w_in")
    u_cout = big_update(c_w_out, m_c_w_out, v_c_w_out, p_cout, [s_c[1]], "adam_c_w_out")
    tr_ = lambda a: jnp.swapaxes(a, 1, 2)
    u_wup = [tr_(o) for o in big_update(tr_(f_w_up), tr_(m_f_w_up), tr_(v_f_w_up), p_wup,
                                        [s_f0[0], s_f1[0]], "adam_f_w_up")]
    u_wdn = big_update(f_w_down, m_f_w_down, v_f_w_down, p_wdn, [s_f0[1], s_f1[1]], "adam_f_w_down")

    g_norm_mix = g1024[0:2]
    g_norm_ffn = g1024[2:4]
    g_norm_final = g1024[4:5]
    g_ccw = lax.dynamic_slice(g1024[5:8], (0, dev * (D // NDEV)), (3, D // NDEV))
    g_bcw = lax.dynamic_slice(g512[8:8 + BCONV], (0, dev * (DA // NDEV)), (BCONV, DA // NDEV))
    gfc = gfc.reshape(2, NG, 2, 3, FB)
    g_fcw = lax.dynamic_slice(gfc, (0, dev % NG, dev // NG, 0, 0), (2, 1, 1, 3, FB)).reshape(2, 3, FB)
    small_w = [norm_mix, norm_ffn, nfin, a_ln_g, a_ln_b, a_w_s[0], a_b_s[0], b_conv_w[0], b_conv_b,
               b_ln_g, b_ln_b, c_conv_w[0], f_conv_w]
    small_g = [g_norm_mix, g_norm_ffn, g_norm_final, g512[0:1], g512[1:2],
               dws.reshape(HEADS, CHUNK, CHUNK), dbs, g_bcw, g512[2:3],
               g512[3:4], g512[4:5], g_ccw, g_fcw]
    small_m = [m_norm_mix, m_norm_ffn, m_norm_final.reshape(1, D), m_a_ln_g, m_a_ln_b, m_a_w_s[0], m_a_b_s[0],
               m_b_conv_w[0], m_b_conv_b, m_b_ln_g, m_b_ln_b, m_c_conv_w[0], m_f_conv_w]
    small_v = [v_norm_mix, v_norm_ffn, v_norm_final.reshape(1, D), v_a_ln_g, v_a_ln_b, v_a_w_s[0], v_a_b_s[0],
               v_b_conv_w[0], v_b_conv_b, v_b_ln_g, v_b_ln_b, v_c_conv_w[0], v_f_conv_w]
    upd = _adam_small(small_w, small_g, small_m, small_v, "adam_small")
    ns = len(small_w)
    orig = [norm_mix, norm_ffn, norm_final, a_ln_g, a_ln_b, a_w_s, a_b_s, b_conv_w, b_conv_b,
            b_ln_g, b_ln_b, c_conv_w, f_conv_w]
    sg_out = [g.reshape(o.shape) for g, o in zip(small_g, orig)]
    sd_out = [a.reshape(o.shape) for a, o in zip(upd[0:ns], orig)]
    sm_out = [a.reshape(o.shape) for a, o in zip(upd[ns:2 * ns], orig)]
    sv_out = [a.reshape(o.shape) for a, o in zip(upd[2 * ns:3 * ns], orig)]

    def assemble(small, k):
        return [small[0], small[1], small[2], u_win0[k], small[3], small[4], small[5], small[6], small[7],
                small[8], small[9], small[10], u_wout0[k], u_cin[k], small[11], u_cout[k], u_wup[k],
                small[12], u_wdn[k]]

    grads = assemble(sg_out, 0)
    deltas = assemble(sd_out, 1)
    new_m = assemble(sm_out, 2)
    new_v = assemble(sv_out, 3)
    return (loss, grad_x.reshape(1, s, D), *grads, *deltas, *new_m, *new_v)
```

```python
import functools
import math

import jax
import jax.numpy as jnp
from jax import lax
from jax.experimental import pallas as pl
from jax.experimental.pallas import tpu as pltpu

F32 = jnp.float32
BF16 = jnp.bfloat16

D = 1024
DA = 512
HEADS = 4
CHUNK = 128
DFF = 2816
NDEV = 8
NCHIP = 4
FB = DFF * 2 // NDEV
NG = DFF // FB
BCONV = 31
EPS = 1e-6
HALO = 16
HALO_B = 32
RC = 16
NPART = 2
VMEM_LIMIT = 52 * 1024 * 1024
INV_SQRT2 = 1.0 / math.sqrt(2.0)
INV_SQRT_2PI = 1.0 / math.sqrt(2.0 * math.pi)

ADAM_LR = 0.001
ADAM_B1 = 0.9
ADAM_B2 = 0.999
ADAM_EPS = 1e-08
ADAM_WD = 0.01
ADAM_STEP = 10

MESH = pl.DeviceIdType.MESH
ANY = pl.BlockSpec(memory_space=pl.ANY)
NT_DIMS = (((1,), (1,)), ((), ()))
TN_DIMS = (((0,), (0,)), ((), ()))


def _params(*sem):
    return pltpu.CompilerParams(dimension_semantics=sem, vmem_limit_bytes=VMEM_LIMIT)


def _tile(s, want):
    return min(want, s)


def _sigmoid(x):
    return jax.nn.sigmoid(x)


def _dsilu(x, sg):
    return sg * (1.0 + x * (1.0 - sg))


def _gelu(x):
    return 0.5 * x * (1.0 + lax.erf(x * INV_SQRT2))


def _dgelu(x):
    return 0.5 * (1.0 + lax.erf(x * INV_SQRT2)) + x * jnp.exp(-0.5 * x * x) * INV_SQRT_2PI


def _ln_fwd(x, g, b):
    mu = jnp.mean(x, axis=-1, keepdims=True)
    xc = x - mu
    var = jnp.mean(xc * xc, axis=-1, keepdims=True)
    rstd = lax.rsqrt(var + EPS)
    xhat = xc * rstd
    return xhat * g + b, xhat, rstd


def _ln_bwd(dy, xhat, rstd, g):
    dxh = dy * g
    m1 = jnp.mean(dxh, axis=-1, keepdims=True)
    m2 = jnp.mean(dxh * xhat, axis=-1, keepdims=True)
    return rstd * (dxh - m1 - xhat * m2)


def _rms_bwd_math(dh, x, g):
    r = lax.rsqrt(jnp.mean(x * x, axis=-1, keepdims=True) + EPS)
    xhat = x * r
    dg = jnp.sum(dh * xhat, axis=0, keepdims=True)
    u = dh * g
    dx = r * (u - xhat * jnp.mean(u * xhat, axis=-1, keepdims=True))
    return dx, dg


def _conv3(xe, cw, halo):
    x0 = xe[halo:]
    x1 = pltpu.roll(xe, 1, 0)[halo:]
    x2 = pltpu.roll(xe, 2, 0)[halo:]
    return cw[2] * x0 + cw[1] * x1 + cw[0] * x2, (x0, x1, x2)


def _conv3_bwd_in(dce, cw, ts):
    n = dce.shape[0]
    d1 = pltpu.roll(dce, n - 1, 0)[:ts]
    d2 = pltpu.roll(dce, n - 2, 0)[:ts]
    return cw[2] * dce[:ts] + cw[1] * d1 + cw[0] * d2


def _conv3_bwd_w(dc, taps):
    x0, x1, x2 = taps
    return [jnp.sum(dc * x2, axis=0, keepdims=True), jnp.sum(dc * x1, axis=0, keepdims=True),
            jnp.sum(dc * x0, axis=0, keepdims=True)]


def _rms_fwd(x, g, name, after=None):
    s = x.shape[0]
    ts = _tile(s, 512)

    def body(x_ref, g_ref, *rest):
        h_ref = rest[-1]
        xv = x_ref[...]
        r = lax.rsqrt(jnp.mean(xv * xv, axis=-1, keepdims=True) + EPS)
        h_ref[...] = (xv * r * g_ref[...]).astype(BF16)

    extra = [] if after is None else [after]
    return pl.pallas_call(
        body, grid=(s // ts,), name=name,
        in_specs=[pl.BlockSpec((ts, D), lambda i: (i, 0)), pl.BlockSpec((1, D), lambda i: (0, 0))]
        + [ANY] * len(extra),
        out_specs=pl.BlockSpec((ts, D), lambda i: (i, 0)),
        out_shape=jax.ShapeDtypeStruct((s, D), BF16),
        compiler_params=_params("parallel"),
    )(x, g, *extra)


def _mm_in(h, wblk, name):
    s = h.shape[0]
    nb, _, bn = wblk.shape
    ts = _tile(s, 512)

    def body(h_ref, w_ref, o_ref):
        hv = h_ref[...]
        for b in range(nb):
            o_ref[:, b * bn:(b + 1) * bn] = jnp.dot(hv, w_ref[b], preferred_element_type=F32).astype(BF16)

    return pl.pallas_call(
        body, grid=(s // ts,), name=name,
        in_specs=[pl.BlockSpec((ts, D), lambda i: (i, 0)), pl.BlockSpec((nb, D, bn), lambda i: (0, 0, 0))],
        out_specs=pl.BlockSpec((ts, nb * bn), lambda i: (i, 0)),
        out_shape=jax.ShapeDtypeStruct((s, nb * bn), BF16),
        compiler_params=_params("parallel"),
    )(h, wblk)


def _mm_out(y, w, xres, name):
    s = y.shape[0]
    ts = _tile(s, 512)

    def body(y_ref, w_ref, x_ref, o_ref):
        o_ref[...] = x_ref[...] + jnp.dot(y_ref[...], w_ref[...], preferred_element_type=F32)

    return pl.pallas_call(
        body, grid=(s // ts,), name=name,
        in_specs=[pl.BlockSpec((ts, D), lambda i: (i, 0)), pl.BlockSpec((D, D), lambda i: (0, 0)),
                  pl.BlockSpec((ts, D), lambda i: (i, 0))],
        out_specs=pl.BlockSpec((ts, D), lambda i: (i, 0)),
        out_shape=jax.ShapeDtypeStruct((s, D), F32),
        compiler_params=_params("parallel"),
    )(y, w, xres)


def _conv31(ue, cw_ref, ts):
    acc = jnp.zeros((ts, ue.shape[1]), F32)
    for r in range(8):
        rolled = ue if r == 0 else pltpu.roll(ue, r, 0)
        for q in range(4):
            sh = 8 * q + r
            if sh >= BCONV:
                continue
            k = BCONV - 1 - sh
            acc = acc + cw_ref[k:k + 1, :] * rolled[HALO_B - 8 * q:HALO_B - 8 * q + ts]
    return acc


def _ab_fwd(z, lga, lba, wsm, bs_col, cwb, cbb, lgb, lbb, name):
    s = z.shape[0]
    ts = _tile(s, 256)
    hb = ts // HALO_B

    def body(z_ref, zh_ref, lga_ref, lba_ref, ws_ref, bs_ref, cw_ref, cb_ref, lgb_ref, lbb_ref,
             y_ref, yb2_ref):
        i = pl.program_id(0)
        z_t = z_ref[...].astype(F32)
        gu = _gelu(z_t[:, 0:DA])
        gv = _gelu(z_t[:, DA:2 * DA])
        vn, _, _ = _ln_fwd(gv, lga_ref[...], lba_ref[...])
        vnb = vn.astype(BF16)
        for c in range(ts // CHUNK):
            for h in range(HEADS):
                rs = slice(c * CHUNK, (c + 1) * CHUNK)
                cs = slice(h * CHUNK, (h + 1) * CHUNK)
                mixed = jnp.dot(ws_ref[h], vnb[rs, cs], preferred_element_type=F32) + bs_ref[h]
                y_ref[rs, cs] = (gu[rs, cs] * mixed).astype(BF16)
        zh = jnp.where(i > 0, zh_ref[...], jnp.zeros_like(zh_ref[...])).astype(F32)
        xb = jnp.concatenate([zh[:, 0:DA], z_t[:, 2 * DA:3 * DA]], axis=0)
        gb = jnp.concatenate([zh[:, DA:2 * DA], z_t[:, 3 * DA:4 * DA]], axis=0)
        u = xb * _sigmoid(gb)
        conv = _conv31(u, cw_ref, ts) + cb_ref[...]
        yb2_ref[...] = conv
        nb_, _, _ = _ln_fwd(conv, lgb_ref[...], lbb_ref[...])
        y_ref[:, DA:2 * DA] = (nb_ * _sigmoid(nb_)).astype(BF16)

    row = lambda i: (0, 0)
    return pl.pallas_call(
        body, grid=(s // ts,), name=name,
        in_specs=[pl.BlockSpec((ts, 4 * DA), lambda i: (i, 0)),
                  pl.BlockSpec((HALO_B, 2 * DA), lambda i: (jnp.maximum(i * hb - 1, 0), 1)),
                  pl.BlockSpec((1, DA), row), pl.BlockSpec((1, DA), row),
                  pl.BlockSpec((HEADS, CHUNK, CHUNK), lambda i: (0, 0, 0)),
                  pl.BlockSpec((HEADS, CHUNK, 1), lambda i: (0, 0, 0)),
                  pl.BlockSpec((BCONV, DA), row), pl.BlockSpec((1, DA), row),
                  pl.BlockSpec((1, DA), row), pl.BlockSpec((1, DA), row)],
        out_specs=[pl.BlockSpec((ts, 2 * DA), lambda i: (i, 0)), pl.BlockSpec((ts, DA), lambda i: (i, 0))],
        out_shape=[jax.ShapeDtypeStruct((s, 2 * DA), BF16), jax.ShapeDtypeStruct((s, DA), F32)],
        compiler_params=_params("parallel"),
    )(z, z, lga, lba, wsm, bs_col, cwb, cbb, lgb, lbb)


def _c_fwd(zc, cw, name):
    s = zc.shape[0]
    ts = _tile(s, 512)
    hb = ts // HALO

    def body(z_ref, ch_ref, xh_ref, cw_ref, r_ref):
        i = pl.program_id(0)
        z_t = z_ref[...].astype(F32)
        ph = jnp.where(i > 0, ch_ref[...].astype(F32) * xh_ref[...].astype(F32), 0.0)
        pe = jnp.concatenate([ph, z_t[:, D:2 * D] * z_t[:, 2 * D:3 * D]], axis=0)
        q, _ = _conv3(pe, [cw_ref[k:k + 1, :] for k in range(3)], HALO)
        r_ref[...] = (z_t[:, 0:D] * q).astype(BF16)

    halo = lambda col: pl.BlockSpec((HALO, D), lambda i: (jnp.maximum(i * hb - 1, 0), col))
    return pl.pallas_call(
        body, grid=(s // ts,), name=name,
        in_specs=[pl.BlockSpec((ts, 3 * D), lambda i: (i, 0)), halo(1), halo(2),
                  pl.BlockSpec((3, D), lambda i: (0, 0))],
        out_specs=pl.BlockSpec((ts, D), lambda i: (i, 0)),
        out_shape=jax.ShapeDtypeStruct((s, D), BF16),
        compiler_params=_params("parallel"),
    )(zc, zc, zc, cw)


def _ffn_fwd(h, xres, wup, fcw, wdn, name):
    s = h.shape[0]
    ts = _tile(s, 512)
    hb = ts // HALO

    def body(h_ref, hh_ref, w_ref, cw_ref, wd_ref, x_ref, up_ref, upc_ref, xo_ref):
        i = pl.program_id(0)
        m = pl.program_id(1)
        halo = jnp.where(i > 0, hh_ref[...], jnp.zeros_like(hh_ref[...]))
        hx = jnp.concatenate([halo, h_ref[...]], axis=0)
        acts = []
        for gv in range(2):
            up = jnp.dot(hx, w_ref[gv], preferred_element_type=F32)
            up_ref[gv] = up[HALO:].astype(BF16)
            upc, _ = _conv3(up, [cw_ref[gv, k:k + 1, :] for k in range(3)], HALO)
            upc_ref[gv] = upc.astype(BF16)
            acts.append(upc)
        a = acts[0] * _sigmoid(acts[0]) * acts[1]
        f = jnp.dot(a.astype(BF16), wd_ref[...], preferred_element_type=F32)

        @pl.when(m == 0)
        def _():
            xo_ref[...] = x_ref[...] + f

        @pl.when(m > 0)
        def _():
            xo_ref[...] += f

    return pl.pallas_call(
        body, grid=(s // ts, NG), name=name,
        in_specs=[pl.BlockSpec((ts, D), lambda i, m: (i, 0)),
                  pl.BlockSpec((HALO, D), lambda i, m: (jnp.maximum(i * hb - 1, 0), 0)),
                  pl.BlockSpec((2, None, D, FB), lambda i, m: (0, m, 0, 0)),
                  pl.BlockSpec((2, None, 3, FB), lambda i, m: (0, m, 0, 0)),
                  pl.BlockSpec((FB, D), lambda i, m: (m, 0)),
                  pl.BlockSpec((ts, D), lambda i, m: (i, 0))],
        out_specs=[pl.BlockSpec((None, 2, ts, FB), lambda i, m: (m, 0, i, 0)),
                   pl.BlockSpec((None, 2, ts, FB), lambda i, m: (m, 0, i, 0)),
                   pl.BlockSpec((ts, D), lambda i, m: (i, 0))],
        out_shape=[jax.ShapeDtypeStruct((NG, 2, s, FB), BF16), jax.ShapeDtypeStruct((NG, 2, s, FB), BF16),
                   jax.ShapeDtypeStruct((s, D), F32)],
        compiler_params=_params("arbitrary", "arbitrary"),
    )(h, h, wup, fcw, wdn, xres)


def _final(x, tgt, g, name):
    s = x.shape[0]
    ts = _tile(s, 512)

    def body(x_ref, t_ref, g_ref, dx_ref, dg_ref, loss_ref):
        i = pl.program_id(0)
        xv = x_ref[...]
        gv = g_ref[...]
        r = lax.rsqrt(jnp.mean(xv * xv, axis=-1, keepdims=True) + EPS)
        xhat = xv * r
        e = xhat * gv - t_ref[...]
        part = 0.5 * jnp.sum(jnp.mean(e * e, axis=-1, keepdims=True), axis=0, keepdims=True)
        dy = e * (1.0 / D)
        dgp = jnp.sum(dy * xhat, axis=0, keepdims=True)
        u = dy * gv
        dx_ref[...] = r * (u - xhat * jnp.mean(u * xhat, axis=-1, keepdims=True))

        @pl.when(i == 0)
        def _():
            dg_ref[...] = dgp
            loss_ref[...] = jnp.broadcast_to(part, (1, 128))

        @pl.when(i > 0)
        def _():
            dg_ref[...] += dgp
            loss_ref[...] += jnp.broadcast_to(part, (1, 128))

    return pl.pallas_call(
        body, grid=(s // ts,), name=name,
        in_specs=[pl.BlockSpec((ts, D), lambda i: (i, 0)), pl.BlockSpec((ts, D), lambda i: (i, 0)),
                  pl.BlockSpec((1, D), lambda i: (0, 0))],
        out_specs=[pl.BlockSpec((ts, D), lambda i: (i, 0)), pl.BlockSpec((1, D), lambda i: (0, 0)),
                   pl.BlockSpec((1, 128), lambda i: (0, 0))],
        out_shape=[jax.ShapeDtypeStruct((s, D), F32), jax.ShapeDtypeStruct((1, D), F32),
                   jax.ShapeDtypeStruct((1, 128), F32)],
        compiler_params=_params("arbitrary"),
    )(x, tgt, g)


def _ffn_bwd(df, up, upc, wup, fcw, wdn, xin, g, name):
    s = df.shape[0]
    ts = _tile(s, 512)
    nt = s // ts

    def body(df_ref, up_ref, upc_ref, w_ref, cw_ref, wd_ref, x_ref, g_ref,
             a_ref, dup_ref, dx_ref, dg_ref, dcw_ref, carry, acc, tacc):
        i = pl.program_id(0)
        m = pl.program_id(1)
        first = i == 0
        @pl.when(first)
        def _():
            carry[m] = jnp.zeros((2, 8, FB), F32)
            dcw_ref[m] = jnp.zeros((2, 3, FB), F32)

        @pl.when(m == 0)
        def _():
            acc[...] = jnp.zeros((ts, D), F32)

        cws = [[cw_ref[gv, k:k + 1, :] for k in range(3)] for gv in range(2)]
        part = ts // NPART
        das = [lax.dot_general(df_ref[p * part:(p + 1) * part, :].astype(BF16), wd_ref[...], NT_DIMS,
                               preferred_element_type=F32) for p in range(NPART)]

        tacc[...] = jnp.zeros((2, 3, 8, FB), F32)
        nxt = [carry[m, 0], carry[m, 1]]
        for r in reversed(range(ts // RC)):
            rs = slice(r * RC, (r + 1) * RC)
            gate = upc_ref[0, rs, :].astype(F32)
            val = upc_ref[1, rs, :].astype(F32)
            sg = _sigmoid(gate)
            sl = gate * sg
            a_ref[rs, :] = (sl * val).astype(BF16)
            da_c = das[(r * RC) // part][(r * RC) % part:(r * RC) % part + RC]
            dcs = [da_c * val * _dsilu(gate, sg), da_c * sl]
            for gv in range(2):
                dc = dcs[gv]
                dce = jnp.concatenate([dc, nxt[gv]], axis=0)
                d1 = pltpu.roll(dce, RC + 7, 0)[:RC]
                d2 = pltpu.roll(dce, RC + 6, 0)[:RC]
                du = cws[gv][2] * dc + cws[gv][1] * d1 + cws[gv][0] * d2
                dup_ref[gv, rs, :] = du.astype(BF16)
                x0 = up_ref[gv, rs, :].astype(F32)
                for k, dk in enumerate((d2, d1, dc)):
                    p = x0 * dk
                    tacc[gv, k] += p[0:8] + p[8:16]
                nxt[gv] = dc[0:8]
            if (r * RC) % part == 0:
                ps = slice(r * RC, r * RC + part)
                acc[ps, :] += (
                    lax.dot_general(dup_ref[0, ps, :], w_ref[0], NT_DIMS, preferred_element_type=F32)
                    + lax.dot_general(dup_ref[1, ps, :], w_ref[1], NT_DIMS, preferred_element_type=F32))
        for gv in range(2):
            carry[m, gv] = nxt[gv]
            for k in range(3):
                dcw_ref[m, gv, k:k + 1, :] += jnp.sum(tacc[gv, k], axis=0, keepdims=True)

        @pl.when(m == NG - 1)
        def _():
            dx, dgp = _rms_bwd_math(acc[...], x_ref[...], g_ref[...])
            dx_ref[...] = df_ref[...] + dx

            @pl.when(first)
            def _():
                dg_ref[...] = dgp

            @pl.when(jnp.logical_not(first))
            def _():
                dg_ref[...] += dgp

    rev = lambda i: nt - 1 - i
    return pl.pallas_call(
        body, grid=(nt, NG), name=name,
        in_specs=[pl.BlockSpec((ts, D), lambda i, m: (rev(i), 0)),
                  pl.BlockSpec((None, 2, ts, FB), lambda i, m: (m, 0, rev(i), 0)),
                  pl.BlockSpec((None, 2, ts, FB), lambda i, m: (m, 0, rev(i), 0)),
                  pl.BlockSpec((2, None, D, FB), lambda i, m: (0, m, 0, 0)),
                  pl.BlockSpec((2, None, 3, FB), lambda i, m: (0, m, 0, 0)),
                  pl.BlockSpec((FB, D), lambda i, m: (m, 0)),
                  pl.BlockSpec((ts, D), lambda i, m: (rev(i), 0)),
                  pl.BlockSpec((1, D), lambda i, m: (0, 0))],
        out_specs=[pl.BlockSpec((None, ts, FB), lambda i, m: (m, rev(i), 0)),
                   pl.BlockSpec((None, 2, ts, FB), lambda i, m: (m, 0, rev(i), 0)),
                   pl.BlockSpec((ts, D), lambda i, m: (rev(i), 0)),
                   pl.BlockSpec((1, D), lambda i, m: (0, 0)),
                   pl.BlockSpec((NG, 2, 3, FB), lambda i, m: (0, 0, 0, 0))],
        out_shape=[jax.ShapeDtypeStruct((NG, s, FB), BF16), jax.ShapeDtypeStruct((NG, 2, s, FB), BF16),
                   jax.ShapeDtypeStruct((s, D), F32), jax.ShapeDtypeStruct((1, D), F32),
                   jax.ShapeDtypeStruct((NG, 2, 3, FB), F32)],
        scratch_shapes=[pltpu.VMEM((NG, 2, 8, FB), F32), pltpu.VMEM((ts, D), F32),
                        pltpu.VMEM((2, 3, 8, FB), F32)],
        compiler_params=_params("arbitrary", "arbitrary"),
    )(df, up, upc, wup, fcw, wdn, xin, g)


def _mm_nt(dy, w, name):
    s = dy.shape[0]
    ts = _tile(s, 512)

    def body(dy_ref, w_ref, o_ref):
        o_ref[...] = lax.dot_general(dy_ref[...].astype(BF16), w_ref[...], NT_DIMS,
                                     preferred_element_type=F32).astype(BF16)

    return pl.pallas_call(
        body, grid=(s // ts,), name=name,
        in_specs=[pl.BlockSpec((ts, D), lambda i: (i, 0)), pl.BlockSpec((D, D), lambda i: (0, 0))],
        out_specs=pl.BlockSpec((ts, D), lambda i: (i, 0)),
        out_shape=jax.ShapeDtypeStruct((s, D), BF16),
        compiler_params=_params("parallel"),
    )(dy, w)


def _mm_nt_rms(dy, wblk, x, g, dres, name):
    s = dy.shape[0]
    nb, _, bn = wblk.shape
    ts = _tile(s, 512)

    def body(dy_ref, w_ref, x_ref, g_ref, dr_ref, dx_ref, dg_ref):
        i = pl.program_id(0)
        acc = jnp.zeros((ts, D), F32)
        for b in range(nb):
            acc = acc + lax.dot_general(dy_ref[:, b * bn:(b + 1) * bn], w_ref[b], NT_DIMS,
                                        preferred_element_type=F32)
        dx, dgp = _rms_bwd_math(acc, x_ref[...], g_ref[...])
        dx_ref[...] = dr_ref[...] + dx

        @pl.when(i == 0)
        def _():
            dg_ref[...] = dgp

        @pl.when(i > 0)
        def _():
            dg_ref[...] += dgp

    return pl.pallas_call(
        body, grid=(s // ts,), name=name,
        in_specs=[pl.BlockSpec((ts, nb * bn), lambda i: (i, 0)), pl.BlockSpec((nb, D, bn), lambda i: (0, 0, 0)),
                  pl.BlockSpec((ts, D), lambda i: (i, 0)), pl.BlockSpec((1, D), lambda i: (0, 0)),
                  pl.BlockSpec((ts, D), lambda i: (i, 0))],
        out_specs=[pl.BlockSpec((ts, D), lambda i: (i, 0)), pl.BlockSpec((1, D), lambda i: (0, 0))],
        out_shape=[jax.ShapeDtypeStruct((s, D), F32), jax.ShapeDtypeStruct((1, D), F32)],
        compiler_params=_params("arbitrary"),
    )(dy, wblk, x, g, dres)


def _c_bwd(dr, zc, cw, name):
    s = dr.shape[0]
    ts = _tile(s, 512)
    nt = s // ts
    hb = ts // HALO

    def body(dr_ref, drf_ref, z_ref, ch_ref, xh_ref, bf_ref, cw_ref, dz_ref, dcw_ref):
        i = pl.program_id(0)
        cwv = [cw_ref[k:k + 1, :] for k in range(3)]
        z_t = z_ref[...].astype(F32)
        bg, cg, xv = z_t[:, 0:D], z_t[:, D:2 * D], z_t[:, 2 * D:3 * D]
        ph = jnp.where(i > 0, ch_ref[...].astype(F32) * xh_ref[...].astype(F32), 0.0)
        pe = jnp.concatenate([ph, cg * xv], axis=0)
        q, taps = _conv3(pe, cwv, HALO)
        drv = dr_ref[...].astype(F32)
        dq = drv * bg
        dqf = jnp.where(i < nt - 1, drf_ref[...].astype(F32) * bf_ref[...].astype(F32), 0.0)
        dp = _conv3_bwd_in(jnp.concatenate([dq, dqf], axis=0), cwv, ts)
        dz_ref[:, 0:D] = (drv * q).astype(BF16)
        dz_ref[:, D:2 * D] = (dp * xv).astype(BF16)
        dz_ref[:, 2 * D:3 * D] = (dp * cg).astype(BF16)
        rows = _conv3_bwd_w(dq, taps)

        @pl.when(i == 0)
        def _():
            for k in range(3):
                dcw_ref[k:k + 1, :] = rows[k]

        @pl.when(i > 0)
        def _():
            for k in range(3):
                dcw_ref[k:k + 1, :] += rows[k]

    past = lambda col: pl.BlockSpec((HALO, D), lambda i: (jnp.maximum(i * hb - 1, 0), col))
    nxt = lambda i: jnp.minimum((i + 1) * hb, s // HALO - 1)
    return pl.pallas_call(
        body, grid=(nt,), name=name,
        in_specs=[pl.BlockSpec((ts, D), lambda i: (i, 0)),
                  pl.BlockSpec((HALO, D), lambda i: (nxt(i), 0)),
                  pl.BlockSpec((ts, 3 * D), lambda i: (i, 0)), past(1), past(2),
                  pl.BlockSpec((HALO, D), lambda i: (nxt(i), 0)),
                  pl.BlockSpec((3, D), lambda i: (0, 0))],
        out_specs=[pl.BlockSpec((ts, 3 * D), lambda i: (i, 0)), pl.BlockSpec((3, D), lambda i: (0, 0))],
        out_shape=[jax.ShapeDtypeStruct((s, 3 * D), BF16), jax.ShapeDtypeStruct((3, D), F32)],
        compiler_params=_params("arbitrary"),
    )(dr, dr, zc, zc, zc, zc, cw)


G512_ROWS = 40


def _ab_bwd(dy, z, yb2, lga, lba, wsm, bs_col, cwb, lgb, lbb, name):
    s = z.shape[0]
    ts = _tile(s, 256)
    nt = s // ts
    hb = ts // HALO_B
    nch = ts // CHUNK
    tri = None

    def body(z_ref, zh_ref, dy_ref, dyf_ref, yb2_ref, yb2f_ref, lga_ref, lba_ref, ws_ref, bs_ref,
             cw_ref, lgb_ref, lbb_ref, dz_ref, g512_ref, dws_ref, dbs_ref, dvn_ref):
        i = pl.program_id(0)
        last = i == nt - 1

        @pl.when(i == 0)
        def _():
            g512_ref[...] = jnp.zeros((G512_ROWS, DA), F32)
            dws_ref[...] = jnp.zeros((HEADS, CHUNK, CHUNK), F32)
            dbs_ref[...] = jnp.zeros((HEADS, CHUNK, 1), F32)

        def add_row(k, v):
            g512_ref[k:k + 1, :] += v

        z_t = z_ref[...].astype(F32)
        dy_t = dy_ref[...].astype(F32)
        ua, va = z_t[:, 0:DA], z_t[:, DA:2 * DA]
        gu = _gelu(ua)
        gv = _gelu(va)
        lga_v = lga_ref[...]
        vn, xhat_a, rstd_a = _ln_fwd(gv, lga_v, lba_ref[...])
        vnb = vn.astype(BF16)
        causal = (lax.broadcasted_iota(jnp.int32, (CHUNK, CHUNK), 0)
                  >= lax.broadcasted_iota(jnp.int32, (CHUNK, CHUNK), 1)).astype(F32)
        for c in range(nch):
            for h in range(HEADS):
                rs = slice(c * CHUNK, (c + 1) * CHUNK)
                cs = slice(h * CHUNK, (h + 1) * CHUNK)
                vblk = vnb[rs, cs]
                mixed = jnp.dot(ws_ref[h], vblk, preferred_element_type=F32) + bs_ref[h]
                dyb_ = dy_t[rs, cs]
                dmix = dyb_ * gu[rs, cs]
                dmb = dmix.astype(BF16)
                dz_ref[rs, cs] = (dyb_ * mixed * _dgelu(ua[rs, cs])).astype(BF16)
                dvn_ref[rs, cs] = lax.dot_general(ws_ref[h], dmb, TN_DIMS, preferred_element_type=F32)
                dws_ref[h] += causal * lax.dot_general(dmb, vblk, NT_DIMS, preferred_element_type=F32)
                dbs_ref[h] += jnp.sum(dmix, axis=1, keepdims=True)
        dvn = dvn_ref[...]
        add_row(0, jnp.sum(dvn * xhat_a, axis=0, keepdims=True))
        add_row(1, jnp.sum(dvn, axis=0, keepdims=True))
        dgv = _ln_bwd(dvn, xhat_a, rstd_a, lga_v)
        dz_ref[:, DA:2 * DA] = (dgv * _dgelu(va)).astype(BF16)
        lgb_v = lgb_ref[...]
        dyb_e = jnp.concatenate(
            [dy_t[:, DA:2 * DA], jnp.where(last, 0.0, dyf_ref[...].astype(F32))], axis=0)
        yb2_e = jnp.concatenate([yb2_ref[...], jnp.where(last, 0.0, yb2f_ref[...])], axis=0)
        n_e, xhat_b, rstd_b = _ln_fwd(yb2_e, lgb_v, lbb_ref[...])
        sgn = _sigmoid(n_e)
        dn = dyb_e * _dsilu(n_e, sgn)
        dy2 = _ln_bwd(dn, xhat_b, rstd_b, lgb_v)
        add_row(2, jnp.sum(dy2[:ts], axis=0, keepdims=True))
        add_row(3, jnp.sum(dn[:ts] * xhat_b[:ts], axis=0, keepdims=True))
        add_row(4, jnp.sum(dn[:ts], axis=0, keepdims=True))
        zh = jnp.where(i > 0, zh_ref[...], jnp.zeros_like(zh_ref[...])).astype(F32)
        xb_t, gb_t = z_t[:, 2 * DA:3 * DA], z_t[:, 3 * DA:4 * DA]
        sgb = _sigmoid(gb_t)
        ue = jnp.concatenate([zh[:, 0:DA] * _sigmoid(zh[:, DA:2 * DA]), xb_t * sgb], axis=0)
        dy2_t = dy2[:ts]
        n_e_rows = ts + HALO_B
        du = jnp.zeros((ts, DA), F32)
        for r in range(8):
            fwd_roll = ue if r == 0 else pltpu.roll(ue, r, 0)
            bwd_roll = dy2 if r == 0 else pltpu.roll(dy2, n_e_rows - r, 0)
            for q in range(4):
                sh = 8 * q + r
                if sh >= BCONV:
                    continue
                k = BCONV - 1 - sh
                du = du + cw_ref[k:k + 1, :] * bwd_roll[8 * q:8 * q + ts]
                add_row(8 + k, jnp.sum(dy2_t * fwd_roll[HALO_B - 8 * q:HALO_B - 8 * q + ts],
                                       axis=0, keepdims=True))
        dz_ref[:, 2 * DA:3 * DA] = (du * sgb).astype(BF16)
        dz_ref[:, 3 * DA:4 * DA] = (du * xb_t * sgb * (1.0 - sgb)).astype(BF16)

    row = lambda i: (0, 0)
    nxt = lambda i: jnp.minimum((i + 1) * hb, s // HALO_B - 1)
    return pl.pallas_call(
        body, grid=(nt,), name=name,
        in_specs=[pl.BlockSpec((ts, 4 * DA), lambda i: (i, 0)),
                  pl.BlockSpec((HALO_B, 2 * DA), lambda i: (jnp.maximum(i * hb - 1, 0), 1)),
                  pl.BlockSpec((ts, 2 * DA), lambda i: (i, 0)),
                  pl.BlockSpec((HALO_B, DA), lambda i: (nxt(i), 1)),
                  pl.BlockSpec((ts, DA), lambda i: (i, 0)),
                  pl.BlockSpec((HALO_B, DA), lambda i: (nxt(i), 0)),
                  pl.BlockSpec((1, DA), row), pl.BlockSpec((1, DA), row),
                  pl.BlockSpec((HEADS, CHUNK, CHUNK), lambda i: (0, 0, 0)),
                  pl.BlockSpec((HEADS, CHUNK, 1), lambda i: (0, 0, 0)),
                  pl.BlockSpec((BCONV, DA), row), pl.BlockSpec((1, DA), row), pl.BlockSpec((1, DA), row)],
        out_specs=[pl.BlockSpec((ts, 4 * DA), lambda i: (i, 0)),
                   pl.BlockSpec((G512_ROWS, DA), row),
                   pl.BlockSpec((HEADS, CHUNK, CHUNK), lambda i: (0, 0, 0)),
                   pl.BlockSpec((HEADS, CHUNK, 1), lambda i: (0, 0, 0))],
        out_shape=[jax.ShapeDtypeStruct((s, 4 * DA), BF16), jax.ShapeDtypeStruct((G512_ROWS, DA), F32),
                   jax.ShapeDtypeStruct((HEADS, CHUNK, CHUNK), F32),
                   jax.ShapeDtypeStruct((HEADS, CHUNK, 1), F32)],
        scratch_shapes=[pltpu.VMEM((ts, DA), F32)],
        compiler_params=_params("arbitrary"),
    )(z, z, dy, dy, yb2, yb2, lga, lba, wsm, bs_col, cwb, lgb, lbb)


def _dw_cols(a, dy, nb, bn, name):
    s = a.shape[0]
    tm = _tile(s, 2048)
    nt = s // tm
    cpb = 4

    def body(a_ref, dy_ref, o_ref, acc):
        t = pl.program_id(1)
        av = a_ref[...]
        for b in range(cpb):
            p = lax.dot_general(av, dy_ref[:, b * bn:(b + 1) * bn], TN_DIMS, preferred_element_type=F32)

            @pl.when(t == 0)
            def _():
                acc[b] = p

            @pl.when(t > 0)
            def _():
                acc[b] += p

        @pl.when(t == nt - 1)
        def _():
            o_ref[...] = acc[...].astype(BF16)

    return pl.pallas_call(
        body, grid=(nb // cpb, nt), name=name,
        in_specs=[pl.BlockSpec((tm, D), lambda j, t: (t, 0)), pl.BlockSpec((tm, cpb * bn), lambda j, t: (t, j))],
        out_specs=pl.BlockSpec((cpb, D, bn), lambda j, t: (j, 0, 0)),
        out_shape=jax.ShapeDtypeStruct((nb, D, bn), BF16),
        scratch_shapes=[pltpu.VMEM((cpb, D, bn), F32)],
        compiler_params=_params("arbitrary", "arbitrary"),
    )(a, dy)


def _dw_rows(a, dy, name):
    s = a.shape[0]
    tm = _tile(s, 2048)
    nt = s // tm
    rb = 512

    def body(a_ref, dy_ref, o_ref, acc):
        t = pl.program_id(1)
        p = lax.dot_general(a_ref[...], dy_ref[...].astype(BF16), TN_DIMS, preferred_element_type=F32)

        @pl.when(t == 0)
        def _():
            acc[...] = p

        @pl.when(t > 0)
        def _():
            acc[...] += p

        @pl.when(t == nt - 1)
        def _():
            o_ref[...] = acc[...].astype(BF16)

    return pl.pallas_call(
        body, grid=(D // rb, nt), name=name,
        in_specs=[pl.BlockSpec((tm, rb), lambda j, t: (t, j)), pl.BlockSpec((tm, D), lambda j, t: (t, 0))],
        out_specs=pl.BlockSpec((rb, D), lambda j, t: (j, 0)),
        out_shape=jax.ShapeDtypeStruct((D, D), BF16),
        scratch_shapes=[pltpu.VMEM((rb, D), F32)],
        compiler_params=_params("arbitrary", "arbitrary"),
    )(a, dy)


def _dw_up(h, dup, name):
    s = h.shape[0]
    tm = _tile(s, 2048)
    nt = s // tm

    def body(h_ref, d_ref, o_ref, acc):
        t = pl.program_id(1)
        p = lax.dot_general(d_ref[...], h_ref[...], TN_DIMS, preferred_element_type=F32)

        @pl.when(t == 0)
        def _():
            acc[...] = p

        @pl.when(t > 0)
        def _():
            acc[...] += p

        @pl.when(t == nt - 1)
        def _():
            o_ref[...] = acc[...].astype(BF16)

    return pl.pallas_call(
        body, grid=(NDEV, nt), name=name,
        in_specs=[pl.BlockSpec((tm, D), lambda b, t: (t, 0)),
                  pl.BlockSpec((None, None, tm, FB), lambda b, t: (b % NG, b // NG, t, 0))],
        out_specs=pl.BlockSpec((None, FB, D), lambda b, t: (b, 0, 0)),
        out_shape=jax.ShapeDtypeStruct((NDEV, FB, D), BF16),
        scratch_shapes=[pltpu.VMEM((FB, D), F32)],
        compiler_params=_params("arbitrary", "arbitrary"),
    )(h, dup)


def _dw_dn(a, df, name):
    s = df.shape[0]
    tm = _tile(s, 2048)
    nt = s // tm

    def body(a_ref, d_ref, o_ref, acc):
        t = pl.program_id(1)
        p = lax.dot_general(a_ref[...], d_ref[...].astype(BF16), TN_DIMS, preferred_element_type=F32)

        @pl.when(t == 0)
        def _():
            acc[...] = p

        @pl.when(t > 0)
        def _():
            acc[...] += p

        @pl.when(t == nt - 1)
        def _():
            o_ref[...] = acc[...].astype(BF16)

    return pl.pallas_call(
        body, grid=(NG, nt), name=name,
        in_specs=[pl.BlockSpec((None, tm, FB), lambda m, t: (m, t, 0)), pl.BlockSpec((tm, D), lambda m, t: (t, 0))],
        out_specs=pl.BlockSpec((FB, D), lambda m, t: (m, 0)),
        out_shape=jax.ShapeDtypeStruct((DFF, D), BF16),
        scratch_shapes=[pltpu.VMEM((FB, D), F32)],
        compiler_params=_params("arbitrary", "arbitrary"),
    )(a, df)


def _place():
    x, y, c = lax.axis_index("x"), lax.axis_index("y"), lax.axis_index("c")
    chips = [(1 - x, y), (x, 1 - y), (1 - x, 1 - y)]
    return x, y, c, chips


def _zone(shard, dev):
    return lax.dynamic_update_slice(lax.empty((NDEV,) + shard.shape, shard.dtype), shard[None],
                                    (dev,) + (0,) * shard.ndim)


def _all_gather(zones, name):
    nt = len(zones)

    def body(*refs):
        dsts = refs[nt:2 * nt]
        send_sems, recv_sems = refs[2 * nt:]
        x, y, c, chips = _place()
        me, sib = (x, y, c), (x, y, 1 - c)

        def blk(t, p):
            return dsts[t].at[4 * p[0] + 2 * p[1] + p[2]]

        def copy(t, k, block, to):
            return pltpu.make_async_remote_copy(
                src_ref=blk(t, block), dst_ref=blk(t, block),
                send_sem=send_sems.at[t, k], recv_sem=recv_sems.at[t, k],
                device_id=to, device_id_type=MESH)

        first = []
        for t in range(nt):
            first.append(copy(t, 0, me, sib))
            first += [copy(t, 1 + j, me, (*chip, c)) for j, chip in enumerate(chips)]
        for cp in first:
            cp.start()
        passed = []
        for j, chip in enumerate(chips):
            for t in range(nt):
                copy(t, 1 + j, (*chip, c), me).wait_recv()
                cp = copy(t, 4 + j, (*chip, c), sib)
                cp.start()
                passed.append(cp)
        for t in range(nt):
            copy(t, 0, sib, me).wait_recv()
            for j, chip in enumerate(chips):
                copy(t, 4 + j, (*chip, 1 - c), me).wait_recv()
        for cp in first + passed:
            cp.wait_send()

    return pl.pallas_call(
        body, name=name,
        in_specs=[ANY] * nt, out_specs=[ANY] * nt,
        out_shape=[jax.ShapeDtypeStruct(a.shape, a.dtype) for a in zones],
        input_output_aliases={t: t for t in range(nt)},
        scratch_shapes=[pltpu.SemaphoreType.DMA((nt, 7)), pltpu.SemaphoreType.DMA((nt, 7))],
        compiler_params=pltpu.CompilerParams(has_side_effects=True),
    )(*zones)


HBM_SPEC = pl.BlockSpec(memory_space=pltpu.HBM)
SEM_SPEC = pl.BlockSpec(memory_space=pltpu.SEMAPHORE)
DATAFLOW = pltpu.SideEffectType.DATAFLOW_SIDE_EFFECTING


def _hbm(a):
    return pltpu.with_memory_space_constraint(a, pltpu.HBM)


def _hbm_like(arrs):
    return [pltpu.HBM(a.shape, a.dtype) for a in arrs]


def _ag_start(srcs, lands, after, name):
    n = len(srcs)
    ns = 8 * n

    def body(*refs):
        src, land = refs[:n], refs[n:2 * n]
        sems = refs[2 * n + 1:2 * n + 1 + ns]
        token = refs[-1]
        x, y, c, chips = _place()
        peers = [(x, y, 1 - c)] + [(*chip, c) for chip in chips]
        for t in range(n):
            for k, to in enumerate(peers):
                pltpu.make_async_remote_copy(
                    src_ref=src[t], dst_ref=land[t].at[4 * x + 2 * y + c],
                    send_sem=sems[2 * (4 * t + k)], recv_sem=sems[2 * (4 * t + k) + 1],
                    device_id=to, device_id_type=MESH).start()
        token[...] = jnp.zeros_like(token)

    res = pl.pallas_call(
        body, name=name,
        in_specs=[HBM_SPEC] * (2 * n) + [ANY],
        out_specs=[SEM_SPEC] * ns + [HBM_SPEC] * (2 * n) + [pl.BlockSpec(memory_space=pltpu.VMEM)],
        out_shape=[pltpu.SemaphoreType.DMA(())] * ns + _hbm_like(srcs) + _hbm_like(lands)
        + [jax.ShapeDtypeStruct((8, 128), F32)],
        input_output_aliases={i: ns + i for i in range(2 * n)},
        compiler_params=pltpu.CompilerParams(has_side_effects=DATAFLOW),
    )(*[_hbm(a) for a in srcs], *[_hbm(a) for a in lands], after)
    sems = [[(res[2 * (4 * t + k)], res[2 * (4 * t + k) + 1]) for k in range(4)] for t in range(n)]
    return sems, res[ns:ns + n], res[ns + n:ns + 2 * n], res[-1]


def _ag_forward(srcs, lands, sems1, after, name):
    n = len(srcs)
    flat1 = [s for t in range(n) for k in range(1, 4) for s in sems1[t][k]]
    n1 = len(flat1)

    def body(*refs):
        src, land = refs[:n], refs[n:2 * n]
        s1 = refs[2 * n:2 * n + n1]
        s2 = refs[2 * n + n1 + 1:2 * n + n1 + 1 + 6 * n]
        x, y, c, chips = _place()
        for j, (cx, cy) in enumerate(chips):
            for t in range(n):
                blk = land[t].at[4 * cx + 2 * cy + c]
                pltpu.make_async_remote_copy(
                    src_ref=src[t], dst_ref=blk, send_sem=s1[2 * (3 * t + j)], recv_sem=s1[2 * (3 * t + j) + 1],
                    device_id=(cx, cy, c), device_id_type=MESH).wait_recv()
                pltpu.make_async_remote_copy(
                    src_ref=blk, dst_ref=blk, send_sem=s2[2 * (3 * t + j)], recv_sem=s2[2 * (3 * t + j) + 1],
                    device_id=(x, y, 1 - c), device_id_type=MESH).start()

    res = pl.pallas_call(
        body, name=name,
        in_specs=[HBM_SPEC] * (2 * n) + [SEM_SPEC] * n1 + [ANY],
        out_specs=[SEM_SPEC] * (6 * n) + [HBM_SPEC] * n,
        out_shape=[pltpu.SemaphoreType.DMA(())] * (6 * n) + _hbm_like(lands),
        input_output_aliases={n + i: 6 * n + i for i in range(n)},
        compiler_params=pltpu.CompilerParams(has_side_effects=DATAFLOW),
    )(*srcs, *lands, *flat1, after)
    sems2 = [[(res[2 * (3 * t + j)], res[2 * (3 * t + j) + 1]) for j in range(3)] for t in range(n)]
    return sems2, res[6 * n:]


def _ag_finish(srcs, lands, sems1, sems2, after, name):
    n = len(srcs)
    flat1 = [s for t in range(n) for k in range(4) for s in sems1[t][k]]
    flat2 = [s for t in range(n) for j in range(3) for s in sems2[t][j]]
    n1, n2 = len(flat1), len(flat2)

    def body(*refs):
        src, land = refs[:n], refs[n:2 * n]
        s1 = refs[2 * n:2 * n + n1]
        s2 = refs[2 * n + n1:2 * n + n1 + n2]
        x, y, c, chips = _place()
        sib = (x, y, 1 - c)
        for t in range(n):
            own = land[t].at[4 * x + 2 * y + 1 - c]
            pltpu.make_async_remote_copy(
                src_ref=src[t], dst_ref=own, send_sem=s1[8 * t], recv_sem=s1[8 * t + 1],
                device_id=sib, device_id_type=MESH).wait_recv()
            for k in range(4):
                pltpu.make_async_remote_copy(
                    src_ref=src[t], dst_ref=own, send_sem=s1[2 * (4 * t + k)], recv_sem=s1[2 * (4 * t + k) + 1],
                    device_id=sib, device_id_type=MESH).wait_send()
            for j, (cx, cy) in enumerate(chips):
                blk = land[t].at[4 * cx + 2 * cy + 1 - c]
                cp = pltpu.make_async_remote_copy(
                    src_ref=blk, dst_ref=blk, send_sem=s2[2 * (3 * t + j)], recv_sem=s2[2 * (3 * t + j) + 1],
                    device_id=sib, device_id_type=MESH)
                cp.wait_send()
                cp.wait_recv()

    return pl.pallas_call(
        body, name=name,
        in_specs=[HBM_SPEC] * (2 * n) + [SEM_SPEC] * (n1 + n2) + [ANY],
        out_specs=[HBM_SPEC] * n,
        out_shape=_hbm_like(lands),
        input_output_aliases={n + i: i for i in range(n)},
        compiler_params=pltpu.CompilerParams(has_side_effects=DATAFLOW),
    )(*srcs, *lands, *flat1, *flat2, after)


def _pair_copies(srcs, dsts, sems):
    x, y, c, _ = _place()
    nt = len(srcs)
    return [pltpu.make_async_remote_copy(
        src_ref=srcs[t].at[2 * j + 1 - c], dst_ref=dsts[t].at[j],
        send_sem=sems[2 * (NCHIP * t + j)], recv_sem=sems[2 * (NCHIP * t + j) + 1],
        device_id=(x, y, 1 - c), device_id_type=MESH) for t in range(nt) for j in range(NCHIP)]


def _pair_start(grads, carry, name):
    nt = len(grads)
    ns = 2 * NCHIP * nt
    zones = [_hbm(lax.empty((NCHIP,) + a.shape[1:], a.dtype)) for a in grads]
    extra = [] if carry is None else [_hbm(carry)]
    ne = len(extra)

    def body(*refs):
        for cp in _pair_copies(refs[:nt], refs[nt:2 * nt], refs[2 * nt + ne:2 * nt + ne + ns]):
            cp.start()

    res = pl.pallas_call(
        body, name=name,
        in_specs=[HBM_SPEC] * (2 * nt + ne),
        out_specs=[SEM_SPEC] * ns + [HBM_SPEC] * (2 * nt + ne),
        out_shape=[pltpu.SemaphoreType.DMA(())] * ns + _hbm_like(grads) + _hbm_like(zones) + _hbm_like(extra),
        input_output_aliases={i: ns + i for i in range(2 * nt + ne)},
        compiler_params=pltpu.CompilerParams(has_side_effects=DATAFLOW),
    )(*[_hbm(a) for a in grads], *zones, *extra)
    handle = (list(res[:ns]), list(res[ns:ns + nt]), list(res[ns + nt:ns + 2 * nt]))
    return handle, (res[ns + 2 * nt] if ne else None)


def _pair_wait(handle, after, name):
    sems, srcs, zones = handle
    nt, ns = len(srcs), len(sems)

    def body(*refs):
        for cp in _pair_copies(refs[:nt], refs[nt:2 * nt], refs[2 * nt:2 * nt + ns]):
            cp.wait_send()
            cp.wait_recv()

    return pl.pallas_call(
        body, name=name,
        in_specs=[HBM_SPEC] * (2 * nt) + [SEM_SPEC] * ns + [ANY],
        out_specs=[HBM_SPEC] * nt,
        out_shape=_hbm_like(zones),
        input_output_aliases={nt + i: i for i in range(nt)},
        compiler_params=pltpu.CompilerParams(has_side_effects=DATAFLOW),
    )(*srcs, *zones, *sems, after)


def _rows_tile(r, row_bytes, cap_bytes):
    best = None
    for tr in range(16, r + 1, 16):
        if r % tr == 0 and tr * row_bytes <= cap_bytes:
            best = tr
    return best if best is not None else r


def _pair_sum(own, got, cidx, name):
    _, _, r, cdim = own.shape
    tr = _rows_tile(r, 2 * cdim, 2 * 1024 * 1024)

    def body(c_ref, a_ref, b_ref, o_ref):
        o_ref[...] = (a_ref[...].astype(F32) + b_ref[...].astype(F32)).astype(BF16)

    return pl.pallas_call(
        body, name=name,
        grid_spec=pltpu.PrefetchScalarGridSpec(
            num_scalar_prefetch=1, grid=(NCHIP, r // tr),
            in_specs=[pl.BlockSpec((None, None, tr, cdim), lambda j, i, c_ref: (j, c_ref[0], i, 0)),
                      pl.BlockSpec((None, tr, cdim), lambda j, i, c_ref: (j, i, 0))],
            out_specs=pl.BlockSpec((None, tr, cdim), lambda j, i, c_ref: (j, i, 0))),
        out_shape=jax.ShapeDtypeStruct((NCHIP, r, cdim), BF16),
        compiler_params=_params("arbitrary", "arbitrary"),
    )(cidx, own, got)


def _chip_copies(srcs, zones, slots, sems):
    x, y, c, chips = _place()
    out = []
    for t, (z, l) in enumerate(slots):
        for k, (cx, cy) in enumerate(chips):
            dst = zones[z].at[k] if l is None else zones[z].at[k, l]
            out.append(pltpu.make_async_remote_copy(
                src_ref=srcs[t].at[2 * cx + cy], dst_ref=dst,
                send_sem=sems[2 * (3 * t + k)], recv_sem=sems[2 * (3 * t + k) + 1],
                device_id=(cx, cy, c), device_id_type=MESH))
    return out


def _chip_start(sums, zones, slots, carry, name):
    nt, nz = len(sums), len(zones)
    ns = 6 * nt
    extra = [] if carry is None else [_hbm(carry)]
    ne = len(extra)

    def body(*refs):
        for cp in _chip_copies(refs[:nt], refs[nt:nt + nz], slots, refs[nt + nz + ne:nt + nz + ne + ns]):
            cp.start()

    res = pl.pallas_call(
        body, name=name,
        in_specs=[HBM_SPEC] * (nt + nz + ne),
        out_specs=[SEM_SPEC] * ns + [HBM_SPEC] * (nt + nz + ne),
        out_shape=[pltpu.SemaphoreType.DMA(())] * ns + _hbm_like(sums) + _hbm_like(zones) + _hbm_like(extra),
        input_output_aliases={i: ns + i for i in range(nt + nz + ne)},
        compiler_params=pltpu.CompilerParams(has_side_effects=DATAFLOW),
    )(*[_hbm(a) for a in sums], *zones, *extra)
    return (list(res[:ns]), list(res[ns:ns + nt]), list(res[ns + nt:ns + nt + nz]),
            (res[ns + nt + nz] if ne else None))


def _chip_wait(started, zones, after, name):
    nz = len(zones)
    flat_src = [a for sums, _, _ in started for a in sums]
    flat_sem = [s for _, _, sems in started for s in sems]
    n_src, n_sem = len(flat_src), len(flat_sem)

    def body(*refs):
        srcs, zs, sems = refs[:n_src], refs[n_src:n_src + nz], refs[n_src + nz:n_src + nz + n_sem]
        so, se = 0, 0
        for sums, slots, sem_list in started:
            for cp in _chip_copies(srcs[so:so + len(sums)], zs, slots, sems[se:se + len(sem_list)]):
                cp.wait_send()
                cp.wait_recv()
            so += len(sums)
            se += len(sem_list)

    return pl.pallas_call(
        body, name=name,
        in_specs=[HBM_SPEC] * (n_src + nz) + [SEM_SPEC] * n_sem + [ANY],
        out_specs=[HBM_SPEC] * nz,
        out_shape=_hbm_like(zones),
        input_output_aliases={n_src + i: i for i in range(nz)},
        compiler_params=pltpu.CompilerParams(has_side_effects=DATAFLOW),
    )(*flat_src, *zones, *flat_sem, after)


def _small_allreduce(parts, name):
    nt = len(parts)

    def body(*refs):
        srcs, outs, bufs = refs[:nt], refs[nt:2 * nt], refs[2 * nt:3 * nt]
        send_sems, recv_sems = refs[3 * nt:]
        x, y, c, _ = _place()
        peers = [(x, y, 1 - c), (1 - x, y, c), (x, 1 - y, c)]
        for t in range(nt):
            outs[t][...] = srcs[t][...]
        for step, peer in enumerate(peers):
            copies = [pltpu.make_async_remote_copy(
                src_ref=outs[t], dst_ref=bufs[t].at[step],
                send_sem=send_sems.at[step, t], recv_sem=recv_sems.at[step, t],
                device_id=peer, device_id_type=MESH) for t in range(nt)]
            for cp in copies:
                cp.start()
            for cp in copies:
                cp.wait()
            for t in range(nt):
                outs[t][...] = outs[t][...] + bufs[t][step]

    vm = pl.BlockSpec(memory_space=pltpu.VMEM)
    return pl.pallas_call(
        body, name=name,
        in_specs=[vm] * nt, out_specs=[vm] * nt,
        out_shape=[jax.ShapeDtypeStruct(a.shape, F32) for a in parts],
        scratch_shapes=[pltpu.VMEM((3,) + a.shape, F32) for a in parts]
        + [pltpu.SemaphoreType.DMA((3, nt)), pltpu.SemaphoreType.DMA((3, nt))],
        compiler_params=pltpu.CompilerParams(has_side_effects=True, vmem_limit_bytes=VMEM_LIMIT),
    )(*parts)


def _adam_math(w, g, m, v):
    m2 = ADAM_B1 * m + (1.0 - ADAM_B1) * g
    v2 = ADAM_B2 * v + (1.0 - ADAM_B2) * (g * g)
    m_hat = m2 / (1.0 - ADAM_B1 ** ADAM_STEP)
    v_hat = v2 / (1.0 - ADAM_B2 ** ADAM_STEP)
    delta = -ADAM_LR * (m_hat / (jnp.sqrt(v_hat) + ADAM_EPS) + ADAM_WD * w)
    return delta, m2, v2


def _adam_big(w, m, v, parts, mine, chip, name):
    nl, r, cdim = w.shape
    tr = _rows_tile(r, 4 * cdim, 3 * 512 * 1024)

    def body(c_ref, w_ref, m_ref, v_ref, p_ref, *rest):
        mine_refs, (g_ref, d_ref, mo_ref, vo_ref) = rest[:nl], rest[nl:]
        own = mine_refs[0][...]
        for l in range(1, nl):
            own = jnp.where(pl.program_id(0) == l, mine_refs[l][...], own)
        g = ((p_ref[0].astype(F32) + p_ref[1].astype(F32)) + p_ref[2].astype(F32)) + own.astype(F32)
        delta, m2, v2 = _adam_math(w_ref[...], g, m_ref[...], v_ref[...])
        g_ref[...] = g
        d_ref[...] = delta
        mo_ref[...] = m2
        vo_ref[...] = v2

    spec = pl.BlockSpec((None, tr, cdim), lambda l, i, c_ref: (l, i, 0))
    mine_specs = [pl.BlockSpec((None, tr, cdim), lambda l, i, c_ref, ll=ll: (c_ref[0], jnp.where(l == ll, i, 0), 0))
                  for ll in range(nl)]
    return pl.pallas_call(
        body, name=name,
        grid_spec=pltpu.PrefetchScalarGridSpec(
            num_scalar_prefetch=1, grid=(nl, r // tr),
            in_specs=[spec, spec, spec, pl.BlockSpec((3, None, tr, cdim), lambda l, i, c_ref: (0, l, i, 0))]
            + mine_specs,
            out_specs=[spec] * 4),
        out_shape=[jax.ShapeDtypeStruct(w.shape, F32)] * 4,
        compiler_params=_params("arbitrary", "arbitrary"),
    )(chip, w, m, v, parts, *mine)


def _adam_small(ws, gs, ms, vs, name):
    n = len(ws)

    def body(*refs):
        w_r, g_r, m_r, v_r = refs[:n], refs[n:2 * n], refs[2 * n:3 * n], refs[3 * n:4 * n]
        d_o, m_o, v_o = refs[4 * n:5 * n], refs[5 * n:6 * n], refs[6 * n:7 * n]
        for t in range(n):
            delta, m2, v2 = _adam_math(w_r[t][...], g_r[t][...], m_r[t][...], v_r[t][...])
            d_o[t][...] = delta
            m_o[t][...] = m2
            v_o[t][...] = v2

    vm = pl.BlockSpec(memory_space=pltpu.VMEM)
    shapes = [jax.ShapeDtypeStruct(a.shape, F32) for a in ws]
    return pl.pallas_call(
        body, name=name, in_specs=[vm] * (4 * n), out_specs=[vm] * (3 * n), out_shape=shapes * 3,
        compiler_params=pltpu.CompilerParams(vmem_limit_bytes=VMEM_LIMIT),
    )(*ws, *gs, *ms, *vs)


def kernel(x, norm_mix, norm_ffn, norm_final, ab_w_in, a_ln_g, a_ln_b, a_w_s, a_b_s, b_conv_w, b_conv_b, b_ln_g, b_ln_b, ab_w_out, c_w_in, c_conv_w, c_w_out, f_w_up, f_conv_w, f_w_down, loss_target, m_norm_mix, m_norm_ffn, m_norm_final, m_ab_w_in, m_a_ln_g, m_a_ln_b, m_a_w_s, m_a_b_s, m_b_conv_w, m_b_conv_b, m_b_ln_g, m_b_ln_b, m_ab_w_out, m_c_w_in, m_c_conv_w, m_c_w_out, m_f_w_up, m_f_conv_w, m_f_w_down, v_norm_mix, v_norm_ffn, v_norm_final, v_ab_w_in, v_a_ln_g, v_a_ln_b, v_a_w_s, v_a_b_s, v_b_conv_w, v_b_conv_b, v_b_ln_g, v_b_ln_b, v_ab_w_out, v_c_w_in, v_c_conv_w, v_c_w_out, v_f_w_up, v_f_conv_w, v_f_w_down):
    s = x.shape[1]
    x0 = x.reshape(s, D)
    tgt = loss_target.reshape(s, D)
    xi, yi, ci = lax.axis_index("x"), lax.axis_index("y"), lax.axis_index("c")
    dev = 4 * xi + 2 * yi + ci
    cidx = ci.astype(jnp.int32).reshape(1)

    bf = lambda a: a.astype(BF16)
    slab_w = 6 * CHUNK
    pad = lambda a, rows: jnp.pad(a, ((0, rows - a.shape[0]), (0, slab_w - a.shape[1])))
    slab = jnp.concatenate([pad(b_conv_w[0], 32), pad(c_conv_w[0], 8), pad(f_conv_w.reshape(6, FB), 8)], axis=0)
    win0, wout0, slab_g = _all_gather(
        [_zone(bf(ab_w_in[0]), dev), _zone(bf(ab_w_out[0]), dev), _zone(slab, dev)], "all_gather_first")
    later = [bf(f_w_up[0]), bf(f_w_down[0]), bf(c_w_in[0]), bf(c_w_out[0]), bf(f_w_up[1]), bf(f_w_down[1])]
    lands = [_zone(a, dev) for a in later]
    groups = [[0, 1], [2, 3], [4, 5]]
    ag_sems, later, lands, ag_token = _ag_start(later, lands, slab_g, "ag_start")
    wout0 = wout0.reshape(D, D)
    bcw = jnp.transpose(slab_g[:, 0:BCONV, 0:DA // NDEV], (1, 0, 2)).reshape(BCONV, DA)
    ccw = jnp.transpose(slab_g[:, 32:35, 0:D // NDEV], (1, 0, 2)).reshape(3, D)
    fcw_g = slab_g[:, 40:46, 0:FB].reshape(2, NG, 2, 3, FB)
    fcws = [fcw_g[:, :, 0], fcw_g[:, :, 1]]

    causal = jnp.tril(jnp.ones((CHUNK, CHUNK), F32))
    wsm = (a_w_s[0] * causal).astype(BF16)
    bs_col = a_b_s.reshape(HEADS, CHUNK, 1)
    nm = [norm_mix[0:1], norm_mix[1:2]]
    nf = [norm_ffn[0:1], norm_ffn[1:2]]
    nfin = norm_final.reshape(1, D)

    def arrive(g, after_ici, after_d2d, tag):
        srcs = [later[t] for t in groups[g]]
        zone = [lands[t] for t in groups[g]]
        sems1 = [ag_sems[t] for t in groups[g]]
        sems2, zone = _ag_forward(srcs, zone, sems1, after_ici, "ag_forward_" + tag)
        return _ag_finish(srcs, zone, sems1, sems2, after_d2d, "ag_finish_" + tag)

    h0 = _rms_fwd(x0, nm[0], "rms_mix0", after=ag_token)
    z = _mm_in(h0, win0, "mm_ab_in")
    ycat, yb2 = _ab_fwd(z, a_ln_g, a_ln_b, wsm, bs_col, bcw, b_conv_b, b_ln_g, b_ln_b, "ab_fwd")
    x1 = _mm_out(ycat, wout0, x0, "mm_ab_out")
    h1 = _rms_fwd(x1, nf[0], "rms_ffn0")
    wup0, wdn0 = arrive(0, x1, h1, "ffn0")
    up0, upc0, x2 = _ffn_fwd(h1, x1, wup0.reshape(2, NG, D, FB), fcws[0], wdn0.reshape(DFF, D), "ffn_fwd0")
    h2 = _rms_fwd(x2, nm[1], "rms_mix1")
    cin, cout = arrive(1, x2, h2, "c")
    cout = cout.reshape(D, D)
    zc = _mm_in(h2, cin, "mm_c_in")
    rc = _c_fwd(zc, ccw, "c_fwd")
    x3 = _mm_out(rc, cout, x2, "mm_c_out")
    h3 = _rms_fwd(x3, nf[1], "rms_ffn1")
    wup1, wdn1 = arrive(2, x3, h3, "ffn1")
    wups = [wup0.reshape(2, NG, D, FB), wup1.reshape(2, NG, D, FB)]
    wdns = [wdn0.reshape(DFF, D), wdn1.reshape(DFF, D)]
    up1, upc1, x4 = _ffn_fwd(h3, x3, wups[1], fcws[1], wdns[1], "ffn_fwd1")
    dx4, dnfin, loss_part = _final(x4, tgt, nfin, "final_loss")
    loss = lax.psum(loss_part[0, 0], ("x", "y", "c"))

    zshape = lambda *sh: _hbm(lax.empty((3,) + sh, BF16))
    zones = [zshape(D, 2 * D // NDEV), zshape(D // NDEV, D), zshape(D, 3 * D // NDEV), zshape(D // NDEV, D),
             zshape(2, FB, D), zshape(2, DFF // NDEV, D)]
    started = []

    def pair_sums(grads, handle, after, tag):
        del grads
        got = _pair_wait(handle, after, "rs_pair_wait_" + tag)
        return [_pair_sum(b.reshape((NCHIP, 2) + b.shape[1:]), g, cidx, "rs_pair_sum_%s%d" % (tag, t))
                for t, (b, g) in enumerate(zip(handle[1], got))]

    def chip_start(sums, slots, carry, tag):
        sems, sums, new_zones, carry = _chip_start(sums, zones, slots, carry, "rs_chip_start_" + tag)
        zones[:] = new_zones
        started.append((sums, slots, sems))
        return sums, carry

    rows8 = lambda g, r: g.reshape(NDEV, r, D)
    a1, dup1, dx3, dnf1, dfcw1 = _ffn_bwd(dx4, up1, upc1, wups[1], fcws[1], wdns[1], x3, nf[1], "ffn_bwd1")
    g_f1 = [_dw_up(h3, dup1, "dw_up1"), rows8(_dw_dn(a1, dx4, "dw_dn1"), DFF // NDEV)]
    hd_f1, dx3 = _pair_start(g_f1, dx3, "rs_pair_start_f1")
    drc = _mm_nt(dx3, cout, "mm_c_out_bwd")
    g_cout = rows8(_dw_rows(rc, dx3, "dw_c_out"), D // NDEV)
    s_f1 = pair_sums(g_f1, hd_f1, g_cout, "f1")
    s_f1, drc = chip_start(s_f1, [(4, 1), (5, 1)], drc, "f1")
    dzc, dccw = _c_bwd(drc, zc, ccw, "c_bwd")
    dx2, dnm1 = _mm_nt_rms(dzc, cin, x2, nm[1], dx3, "mm_c_in_bwd")
    g_c = [_dw_cols(h2, dzc, NDEV, 3 * D // NDEV, "dw_c_in"), g_cout]
    hd_c, dx2 = _pair_start(g_c, dx2, "rs_pair_start_c")
    a0, dup0, dx1, dnf0, dfcw0 = _ffn_bwd(dx2, up0, upc0, wups[0], fcws[0], wdns[0], x1, nf[0], "ffn_bwd0")
    s_c = pair_sums(g_c, hd_c, dx1, "c")
    s_c, dx1 = chip_start(s_c, [(2, None), (3, None)], dx1, "c")
    g_f0 = [_dw_up(h1, dup0, "dw_up0"), rows8(_dw_dn(a0, dx2, "dw_dn0"), DFF // NDEV)]
    hd_f0, dx1 = _pair_start(g_f0, dx1, "rs_pair_start_f0")
    dycat = _mm_nt(dx1, wout0, "mm_ab_out_bwd")
    g_wout0 = rows8(_dw_rows(ycat, dx1, "dw_ab_out"), D // NDEV)
    s_f0 = pair_sums(g_f0, hd_f0, g_wout0, "f0")
    s_f0, dycat = chip_start(s_f0, [(4, 0), (5, 0)], dycat, "f0")
    dz, g512, dws, dbs = _ab_bwd(dycat, z, yb2, a_ln_g, a_ln_b, wsm, bs_col, bcw, b_ln_g, b_ln_b, "ab_bwd")
    grad_x, dnm0 = _mm_nt_rms(dz, win0, x0, nm[0], dx1, "mm_ab_in_bwd")
    g_ab = [_dw_cols(h0, dz, NDEV, 2 * D // NDEV, "dw_ab_in"), g_wout0]
    hd_ab, _ = _pair_start(g_ab, None, "rs_pair_start_ab")
    s_ab = pair_sums(g_ab, hd_ab, grad_x, "ab")
    s_ab, _ = chip_start(s_ab, [(0, None), (1, None)], None, "ab")

    g1024 = jnp.concatenate([dnm0, dnm1, dnf0, dnf1, dnfin, dccw], axis=0)
    gfc = jnp.concatenate([dfcw0, dfcw1], axis=0).reshape(2 * NG * 2 * 3, FB)
    g1024, g512, dws, dbs, gfc = _small_allreduce(
        [g1024, g512, dws.reshape(HEADS * CHUNK, CHUNK), dbs.reshape(HEADS, CHUNK), gfc], "small_allreduce")
    p_win0, p_wout0, p_cin, p_cout, p_wup, p_wdn = _chip_wait(started, zones, g1024, "rs_chip_wait")

    chip = (2 * xi + yi).astype(jnp.int32).reshape(1)

    def big_update(w, m, v, parts, mine, name):
        shp = w.shape
        w3, m3, v3 = (a.reshape((-1,) + shp[-2:]) for a in (w, m, v))
        p4 = parts.reshape((3,) + w3.shape)
        return [o.reshape(shp) for o in _adam_big(w3, m3, v3, p4, mine, chip, name)]

    u_win0 = big_update(ab_w_in, m_ab_w_in, v_ab_w_in, p_win0, [s_ab[0]], "adam_ab_w_in")
    u_wout0 = big_update(ab_w_out, m_ab_w_out, v_ab_w_out, p_wout0, [s_ab[1]], "adam_ab_w_out")
    u_cin = big_update(c_w_in, m_c_w_in, v_c_w_in, p_cin, [s_c[0]], "adam_c_w_in")
    u_cout = big_update(c_w_out, m_c_w_out, v_c_w_out, p_cout, [s_c[1]], "adam_c_w_out")
    tr_ = lambda a: jnp.swapaxes(a, 1, 2)
    u_wup = [tr_(o) for o in big_update(tr_(f_w_up), tr_(m_f_w_up), tr_(v_f_w_up), p_wup,
                                        [s_f0[0], s_f1[0]], "adam_f_w_up")]
    u_wdn = big_update(f_w_down, m_f_w_down, v_f_w_down, p_wdn, [s_f0[1], s_f1[1]], "adam_f_w_down")

    g_norm_mix = g1024[0:2]
    g_norm_ffn = g1024[2:4]
    g_norm_final = g1024[4:5]
    g_ccw = lax.dynamic_slice(g1024[5:8], (0, dev * (D // NDEV)), (3, D // NDEV))
    g_bcw = lax.dynamic_slice(g512[8:8 + BCONV], (0, dev * (DA // NDEV)), (BCONV, DA // NDEV))
    gfc = gfc.reshape(2, NG, 2, 3, FB)
    g_fcw = lax.dynamic_slice(gfc, (0, dev % NG, dev // NG, 0, 0), (2, 1, 1, 3, FB)).reshape(2, 3, FB)
    small_w = [norm_mix, norm_ffn, nfin, a_ln_g, a_ln_b, a_w_s[0], a_b_s[0], b_conv_w[0], b_conv_b,
               b_ln_g, b_ln_b, c_conv_w[0], f_conv_w]
    small_g = [g_norm_mix, g_norm_ffn, g_norm_final, g512[0:1], g512[1:2],
               dws.reshape(HEADS, CHUNK, CHUNK), dbs, g_bcw, g512[2:3],
               g512[3:4], g512[4:5], g_ccw, g_fcw]
    small_m = [m_norm_mix, m_norm_ffn, m_norm_final.reshape(1, D), m_a_ln_g, m_a_ln_b, m_a_w_s[0], m_a_b_s[0],
               m_b_conv_w[0], m_b_conv_b, m_b_ln_g, m_b_ln_b, m_c_conv_w[0], m_f_conv_w]
    small_v = [v_norm_mix, v_norm_ffn, v_norm_final.reshape(1, D), v_a_ln_g, v_a_ln_b, v_a_w_s[0], v_a_b_s[0],
               v_b_conv_w[0], v_b_conv_b, v_b_ln_g, v_b_ln_b, v_c_conv_w[0], v_f_conv_w]
    upd = _adam_small(small_w, small_g, small_m, small_v, "adam_small")
    ns = len(small_w)
    orig = [norm_mix, norm_ffn, norm_final, a_ln_g, a_ln_b, a_w_s, a_b_s, b_conv_w, b_conv_b,
            b_ln_g, b_ln_b, c_conv_w, f_conv_w]
    sg_out = [g.reshape(o.shape) for g, o in zip(small_g, orig)]
    sd_out = [a.reshape(o.shape) for a, o in zip(upd[0:ns], orig)]
    sm_out = [a.reshape(o.shape) for a, o in zip(upd[ns:2 * ns], orig)]
    sv_out = [a.reshape(o.shape) for a, o in zip(upd[2 * ns:3 * ns], orig)]

    def assemble(small, k):
        return [small[0], small[1], small[2], u_win0[k], small[3], small[4], small[5], small[6], small[7],
                small[8], small[9], small[10], u_wout0[k], u_cin[k], small[11], u_cout[k], u_wup[k],
                small[12], u_wdn[k]]

    grads = assemble(sg_out, 0)
    deltas = assemble(sd_out, 1)
    new_m = assemble(sm_out, 2)
    new_v = assemble(sv_out, 3)
    return (loss, grad_x.reshape(1, s, D), *grads, *deltas, *new_m, *new_v)
```

```python
import functools
import math

import jax
import jax.numpy as jnp
from jax import lax
from jax.experimental import pallas as pl
from jax.experimental.pallas import tpu as pltpu

F32 = jnp.float32
BF16 = jnp.bfloat16

D = 1024
DA = 512
HEADS = 4
CHUNK = 128
DFF = 2816
NDEV = 8
NCHIP = 4
FB = DFF * 2 // NDEV
NG = DFF // FB
BCONV = 31
EPS = 1e-6
HALO = 16
HALO_B = 32
RC = 16
NPART = 2
VMEM_LIMIT = 52 * 1024 * 1024
INV_SQRT2 = 1.0 / math.sqrt(2.0)
INV_SQRT_2PI = 1.0 / math.sqrt(2.0 * math.pi)

ADAM_LR = 0.001
ADAM_B1 = 0.9
ADAM_B2 = 0.999
ADAM_EPS = 1e-08
ADAM_WD = 0.01
ADAM_STEP = 10

MESH = pl.DeviceIdType.MESH
ANY = pl.BlockSpec(memory_space=pl.ANY)
NT_DIMS = (((1,), (1,)), ((), ()))
TN_DIMS = (((0,), (0,)), ((), ()))


def _params(*sem):
    return pltpu.CompilerParams(dimension_semantics=sem, vmem_limit_bytes=VMEM_LIMIT)


def _tile(s, want):
    return min(want, s)


def _sigmoid(x):
    return jax.nn.sigmoid(x)


def _dsilu(x, sg):
    return sg * (1.0 + x * (1.0 - sg))


def _gelu(x):
    return 0.5 * x * (1.0 + lax.erf(x * INV_SQRT2))


def _dgelu(x):
    return 0.5 * (1.0 + lax.erf(x * INV_SQRT2)) + x * jnp.exp(-0.5 * x * x) * INV_SQRT_2PI


def _ln_fwd(x, g, b):
    mu = jnp.mean(x, axis=-1, keepdims=True)
    xc = x - mu
    var = jnp.mean(xc * xc, axis=-1, keepdims=True)
    rstd = lax.rsqrt(var + EPS)
    xhat = xc * rstd
    return xhat * g + b, xhat, rstd


def _ln_bwd(dy, xhat, rstd, g):
    dxh = dy * g
    m1 = jnp.mean(dxh, axis=-1, keepdims=True)
    m2 = jnp.mean(dxh * xhat, axis=-1, keepdims=True)
    return rstd * (dxh - m1 - xhat * m2)


def _rms_bwd_math(dh, x, g):
    r = lax.rsqrt(jnp.mean(x * x, axis=-1, keepdims=True) + EPS)
    xhat = x * r
    dg = jnp.sum(dh * xhat, axis=0, keepdims=True)
    u = dh * g
    dx = r * (u - xhat * jnp.mean(u * xhat, axis=-1, keepdims=True))
    return dx, dg


def _conv3(xe, cw, halo):
    x0 = xe[halo:]
    x1 = pltpu.roll(xe, 1, 0)[halo:]
    x2 = pltpu.roll(xe, 2, 0)[halo:]
    return cw[2] * x0 + cw[1] * x1 + cw[0] * x2, (x0, x1, x2)


def _conv3_bwd_in(dce, cw, ts):
    n = dce.shape[0]
    d1 = pltpu.roll(dce, n - 1, 0)[:ts]
    d2 = pltpu.roll(dce, n - 2, 0)[:ts]
    return cw[2] * dce[:ts] + cw[1] * d1 + cw[0] * d2


def _conv3_bwd_w(dc, taps):
    x0, x1, x2 = taps
    return [jnp.sum(dc * x2, axis=0, keepdims=True), jnp.sum(dc * x1, axis=0, keepdims=True),
            jnp.sum(dc * x0, axis=0, keepdims=True)]


def _rms_fwd(x, g, name, after=None):
    s = x.shape[0]
    ts = _tile(s, 512)

    def body(x_ref, g_ref, *rest):
        h_ref = rest[-1]
        xv = x_ref[...]
        r = lax.rsqrt(jnp.mean(xv * xv, axis=-1, keepdims=True) + EPS)
        h_ref[...] = (xv * r * g_ref[...]).astype(BF16)

    extra = [] if after is None else [after]
    return pl.pallas_call(
        body, grid=(s // ts,), name=name,
        in_specs=[pl.BlockSpec((ts, D), lambda i: (i, 0)), pl.BlockSpec((1, D), lambda i: (0, 0))]
        + [ANY] * len(extra),
        out_specs=pl.BlockSpec((ts, D), lambda i: (i, 0)),
        out_shape=jax.ShapeDtypeStruct((s, D), BF16),
        compiler_params=_params("parallel"),
    )(x, g, *extra)


MXU_COLS = 256


def _pair(bn):
    return 1 if bn % MXU_COLS == 0 else 2


def _cols(w_ref, b, pair):
    return w_ref[b] if pair == 1 else jnp.concatenate([w_ref[b + q] for q in range(pair)], axis=1)


def _mm_in(h, wblk, name):
    s = h.shape[0]
    nb, _, bn = wblk.shape
    pair = _pair(bn)
    ts = _tile(s, 512)

    def body(h_ref, w_ref, o_ref):
        hv = h_ref[...]
        for b in range(0, nb, pair):
            o_ref[:, b * bn:(b + pair) * bn] = jnp.dot(hv, _cols(w_ref, b, pair),
                                                       preferred_element_type=F32).astype(BF16)

    return pl.pallas_call(
        body, grid=(s // ts,), name=name,
        in_specs=[pl.BlockSpec((ts, D), lambda i: (i, 0)), pl.BlockSpec((nb, D, bn), lambda i: (0, 0, 0))],
        out_specs=pl.BlockSpec((ts, nb * bn), lambda i: (i, 0)),
        out_shape=jax.ShapeDtypeStruct((s, nb * bn), BF16),
        compiler_params=_params("parallel"),
    )(h, wblk)


def _rms_math(xv, g):
    r = lax.rsqrt(jnp.mean(xv * xv, axis=-1, keepdims=True) + EPS)
    return (xv * r * g).astype(BF16)


def _mm_out(y, w, xres, gnext, name):
    s = y.shape[0]
    ts = _tile(s, 512)

    def body(y_ref, w_ref, x_ref, g_ref, o_ref, h_ref):
        xn = x_ref[...] + jnp.dot(y_ref[...], w_ref[...], preferred_element_type=F32)
        o_ref[...] = xn
        h_ref[...] = _rms_math(xn, g_ref[...])

    return pl.pallas_call(
        body, grid=(s // ts,), name=name,
        in_specs=[pl.BlockSpec((ts, D), lambda i: (i, 0)), pl.BlockSpec((D, D), lambda i: (0, 0)),
                  pl.BlockSpec((ts, D), lambda i: (i, 0)), pl.BlockSpec((1, D), lambda i: (0, 0))],
        out_specs=[pl.BlockSpec((ts, D), lambda i: (i, 0)), pl.BlockSpec((ts, D), lambda i: (i, 0))],
        out_shape=[jax.ShapeDtypeStruct((s, D), F32), jax.ShapeDtypeStruct((s, D), BF16)],
        compiler_params=_params("parallel"),
    )(y, w, xres, gnext)


def _conv31(ue, cw_ref, ts):
    acc = jnp.zeros((ts, ue.shape[1]), F32)
    for r in range(8):
        rolled = ue if r == 0 else pltpu.roll(ue, r, 0)
        for q in range(4):
            sh = 8 * q + r
            if sh >= BCONV:
                continue
            k = BCONV - 1 - sh
            acc = acc + cw_ref[k:k + 1, :] * rolled[HALO_B - 8 * q:HALO_B - 8 * q + ts]
    return acc


def _ab_fwd(z, lga, lba, wsm, bs_col, cwb, cbb, lgb, lbb, name):
    s = z.shape[0]
    ts = _tile(s, 256)
    hb = ts // HALO_B

    def body(z_ref, zh_ref, lga_ref, lba_ref, ws_ref, bs_ref, cw_ref, cb_ref, lgb_ref, lbb_ref,
             y_ref, yb2_ref):
        i = pl.program_id(0)
        z_t = z_ref[...].astype(F32)
        gu = _gelu(z_t[:, 0:DA])
        gv = _gelu(z_t[:, DA:2 * DA])
        vn, _, _ = _ln_fwd(gv, lga_ref[...], lba_ref[...])
        vnb = vn.astype(BF16)
        for c in range(ts // CHUNK):
            for h in range(HEADS):
                rs = slice(c * CHUNK, (c + 1) * CHUNK)
                cs = slice(h * CHUNK, (h + 1) * CHUNK)
                mixed = jnp.dot(ws_ref[h], vnb[rs, cs], preferred_element_type=F32) + bs_ref[h]
                y_ref[rs, cs] = (gu[rs, cs] * mixed).astype(BF16)
        zh = jnp.where(i > 0, zh_ref[...], jnp.zeros_like(zh_ref[...])).astype(F32)
        xb = jnp.concatenate([zh[:, 0:DA], z_t[:, 2 * DA:3 * DA]], axis=0)
        gb = jnp.concatenate([zh[:, DA:2 * DA], z_t[:, 3 * DA:4 * DA]], axis=0)
        u = xb * _sigmoid(gb)
        conv = _conv31(u, cw_ref, ts) + cb_ref[...]
        yb2_ref[...] = conv
        nb_, _, _ = _ln_fwd(conv, lgb_ref[...], lbb_ref[...])
        y_ref[:, DA:2 * DA] = (nb_ * _sigmoid(nb_)).astype(BF16)

    row = lambda i: (0, 0)
    return pl.pallas_call(
        body, grid=(s // ts,), name=name,
        in_specs=[pl.BlockSpec((ts, 4 * DA), lambda i: (i, 0)),
                  pl.BlockSpec((HALO_B, 2 * DA), lambda i: (jnp.maximum(i * hb - 1, 0), 1)),
                  pl.BlockSpec((1, DA), row), pl.BlockSpec((1, DA), row),
                  pl.BlockSpec((HEADS, CHUNK, CHUNK), lambda i: (0, 0, 0)),
                  pl.BlockSpec((HEADS, CHUNK, 1), lambda i: (0, 0, 0)),
                  pl.BlockSpec((BCONV, DA), row), pl.BlockSpec((1, DA), row),
                  pl.BlockSpec((1, DA), row), pl.BlockSpec((1, DA), row)],
        out_specs=[pl.BlockSpec((ts, 2 * DA), lambda i: (i, 0)), pl.BlockSpec((ts, DA), lambda i: (i, 0))],
        out_shape=[jax.ShapeDtypeStruct((s, 2 * DA), BF16), jax.ShapeDtypeStruct((s, DA), F32)],
        compiler_params=_params("parallel"),
    )(z, z, lga, lba, wsm, bs_col, cwb, cbb, lgb, lbb)


def _c_fwd(zc, cw, name):
    s = zc.shape[0]
    ts = _tile(s, 512)
    hb = ts // HALO

    def body(z_ref, ch_ref, xh_ref, cw_ref, r_ref):
        i = pl.program_id(0)
        z_t = z_ref[...].astype(F32)
        ph = jnp.where(i > 0, ch_ref[...].astype(F32) * xh_ref[...].astype(F32), 0.0)
        pe = jnp.concatenate([ph, z_t[:, D:2 * D] * z_t[:, 2 * D:3 * D]], axis=0)
        q, _ = _conv3(pe, [cw_ref[k:k + 1, :] for k in range(3)], HALO)
        r_ref[...] = (z_t[:, 0:D] * q).astype(BF16)

    halo = lambda col: pl.BlockSpec((HALO, D), lambda i: (jnp.maximum(i * hb - 1, 0), col))
    return pl.pallas_call(
        body, grid=(s // ts,), name=name,
        in_specs=[pl.BlockSpec((ts, 3 * D), lambda i: (i, 0)), halo(1), halo(2),
                  pl.BlockSpec((3, D), lambda i: (0, 0))],
        out_specs=pl.BlockSpec((ts, D), lambda i: (i, 0)),
        out_shape=jax.ShapeDtypeStruct((s, D), BF16),
        compiler_params=_params("parallel"),
    )(zc, zc, zc, cw)


def _ffn_fwd(h, xres, wup, fcw, wdn, gnext, name):
    s = h.shape[0]
    ts = _tile(s, 512)
    hb = ts // HALO

    def body(h_ref, hh_ref, w_ref, cw_ref, wd_ref, x_ref, *rest):
        if gnext is not None:
            gn_ref, up_ref, upc_ref, xo_ref, hn_ref, a_s = rest
        else:
            up_ref, upc_ref, xo_ref, a_s = rest
        i = pl.program_id(0)
        m = pl.program_id(1)
        @pl.when(m == 0)
        def _():
            xo_ref[...] = x_ref[...]

        halo = jnp.where(i > 0, hh_ref[...], jnp.zeros_like(hh_ref[...]))
        hx = jnp.concatenate([halo, h_ref[...]], axis=0)
        cws = [[cw_ref[gv, k:k + 1, :] for k in range(3)] for gv in range(2)]
        part = ts // NPART
        for p in range(NPART):
            hp = hx[p * part:(p + 1) * part + HALO]
            ups = [jnp.dot(hp, w_ref[gv], preferred_element_type=F32) for gv in range(2)]
            for o in range(0, part, RC):
                rs = slice(p * part + o, p * part + o + RC)
                acts = []
                for gv in range(2):
                    slab = ups[gv][o + HALO - 8:o + HALO + RC]
                    x0 = slab[8:]
                    x1 = pltpu.roll(slab, 1, 0)[8:]
                    x2 = pltpu.roll(slab, 2, 0)[8:]
                    up_ref[gv, rs, :] = x0.astype(BF16)
                    upc = cws[gv][2] * x0 + cws[gv][1] * x1 + cws[gv][0] * x2
                    upc_ref[gv, rs, :] = upc.astype(BF16)
                    acts.append(upc)
                a_s[rs, :] = (acts[0] * _sigmoid(acts[0]) * acts[1]).astype(BF16)
            ps = slice(p * part, (p + 1) * part)
            xo_ref[ps, :] += jnp.dot(a_s[ps, :], wd_ref[...], preferred_element_type=F32)

        if gnext is not None:
            @pl.when(m == NG - 1)
            def _():
                hn_ref[...] = _rms_math(xo_ref[...], gn_ref[...])

    tile = pl.BlockSpec((ts, D), lambda i, m: (i, 0))
    nxt = gnext is not None
    return pl.pallas_call(
        body, grid=(s // ts, NG), name=name,
        in_specs=[tile,
                  pl.BlockSpec((HALO, D), lambda i, m: (jnp.maximum(i * hb - 1, 0), 0)),
                  pl.BlockSpec((2, None, D, FB), lambda i, m: (0, m, 0, 0)),
                  pl.BlockSpec((2, None, 3, FB), lambda i, m: (0, m, 0, 0)),
                  pl.BlockSpec((FB, D), lambda i, m: (m, 0)),
                  tile] + ([pl.BlockSpec((1, D), lambda i, m: (0, 0))] if nxt else []),
        out_specs=[pl.BlockSpec((None, 2, ts, FB), lambda i, m: (m, 0, i, 0)),
                   pl.BlockSpec((None, 2, ts, FB), lambda i, m: (m, 0, i, 0)),
                   tile] + ([tile] if nxt else []),
        out_shape=[jax.ShapeDtypeStruct((NG, 2, s, FB), BF16), jax.ShapeDtypeStruct((NG, 2, s, FB), BF16),
                   jax.ShapeDtypeStruct((s, D), F32)] + ([jax.ShapeDtypeStruct((s, D), BF16)] if nxt else []),
        scratch_shapes=[pltpu.VMEM((ts, FB), BF16)],
        compiler_params=_params("arbitrary", "arbitrary"),
    )(h, h, wup, fcw, wdn, xres, *([gnext] if nxt else []))


def _final(x, tgt, g, name):
    s = x.shape[0]
    ts = _tile(s, 512)

    def body(x_ref, t_ref, g_ref, dx_ref, dg_ref, loss_ref):
        i = pl.program_id(0)
        xv = x_ref[...]
        gv = g_ref[...]
        r = lax.rsqrt(jnp.mean(xv * xv, axis=-1, keepdims=True) + EPS)
        xhat = xv * r
        e = xhat * gv - t_ref[...]
        part = 0.5 * jnp.sum(jnp.mean(e * e, axis=-1, keepdims=True), axis=0, keepdims=True)
        dy = e * (1.0 / D)
        dgp = jnp.sum(dy * xhat, axis=0, keepdims=True)
        u = dy * gv
        dx_ref[...] = r * (u - xhat * jnp.mean(u * xhat, axis=-1, keepdims=True))

        @pl.when(i == 0)
        def _():
            dg_ref[...] = dgp
            loss_ref[...] = jnp.broadcast_to(part, (1, 128))

        @pl.when(i > 0)
        def _():
            dg_ref[...] += dgp
            loss_ref[...] += jnp.broadcast_to(part, (1, 128))

    return pl.pallas_call(
        body, grid=(s // ts,), name=name,
        in_specs=[pl.BlockSpec((ts, D), lambda i: (i, 0)), pl.BlockSpec((ts, D), lambda i: (i, 0)),
                  pl.BlockSpec((1, D), lambda i: (0, 0))],
        out_specs=[pl.BlockSpec((ts, D), lambda i: (i, 0)), pl.BlockSpec((1, D), lambda i: (0, 0)),
                   pl.BlockSpec((1, 128), lambda i: (0, 0))],
        out_shape=[jax.ShapeDtypeStruct((s, D), F32), jax.ShapeDtypeStruct((1, D), F32),
                   jax.ShapeDtypeStruct((1, 128), F32)],
        compiler_params=_params("arbitrary"),
    )(x, tgt, g)


def _ffn_bwd(df, up, upc, wup, fcw, wdn, xin, g, name):
    s = df.shape[0]
    ts = _tile(s, 512)
    nt = s // ts

    def body(df_ref, up_ref, upc_ref, w_ref, cw_ref, wd_ref, x_ref, g_ref,
             a_ref, dup_ref, dx_ref, dg_ref, dcw_ref, carry, acc, tacc):
        i = pl.program_id(0)
        m = pl.program_id(1)
        first = i == 0
        @pl.when(first)
        def _():
            carry[m] = jnp.zeros((2, 8, FB), F32)
            dcw_ref[m] = jnp.zeros((2, 3, FB), F32)

        @pl.when(m == 0)
        def _():
            acc[...] = jnp.zeros((ts, D), F32)

        cws = [[cw_ref[gv, k:k + 1, :] for k in range(3)] for gv in range(2)]
        part = ts // NPART
        das = [lax.dot_general(df_ref[p * part:(p + 1) * part, :].astype(BF16), wd_ref[...], NT_DIMS,
                               preferred_element_type=F32) for p in range(NPART)]

        tacc[...] = jnp.zeros((2, 3, 8, FB), F32)
        nxt = [carry[m, 0], carry[m, 1]]
        for r in reversed(range(ts // RC)):
            rs = slice(r * RC, (r + 1) * RC)
            gate = upc_ref[0, rs, :].astype(F32)
            val = upc_ref[1, rs, :].astype(F32)
            sg = _sigmoid(gate)
            sl = gate * sg
            a_ref[rs, :] = (sl * val).astype(BF16)
            da_c = das[(r * RC) // part][(r * RC) % part:(r * RC) % part + RC]
            dcs = [da_c * val * _dsilu(gate, sg), da_c * sl]
            for gv in range(2):
                dc = dcs[gv]
                dce = jnp.concatenate([dc, nxt[gv]], axis=0)
                d1 = pltpu.roll(dce, RC + 7, 0)[:RC]
                d2 = pltpu.roll(dce, RC + 6, 0)[:RC]
                du = cws[gv][2] * dc + cws[gv][1] * d1 + cws[gv][0] * d2
                dup_ref[gv, rs, :] = du.astype(BF16)
                x0 = up_ref[gv, rs, :].astype(F32)
                for k, dk in enumerate((d2, d1, dc)):
                    p = x0 * dk
                    tacc[gv, k] += p[0:8] + p[8:16]
                nxt[gv] = dc[0:8]
            if (r * RC) % part == 0:
                ps = slice(r * RC, r * RC + part)
                acc[ps, :] += (
                    lax.dot_general(dup_ref[0, ps, :], w_ref[0], NT_DIMS, preferred_element_type=F32)
                    + lax.dot_general(dup_ref[1, ps, :], w_ref[1], NT_DIMS, preferred_element_type=F32))
        for gv in range(2):
            carry[m, gv] = nxt[gv]
            for k in range(3):
                dcw_ref[m, gv, k:k + 1, :] += jnp.sum(tacc[gv, k], axis=0, keepdims=True)

        @pl.when(m == NG - 1)
        def _():
            dx, dgp = _rms_bwd_math(acc[...], x_ref[...], g_ref[...])
            dx_ref[...] = df_ref[...] + dx

            @pl.when(first)
            def _():
                dg_ref[...] = dgp

            @pl.when(jnp.logical_not(first))
            def _():
                dg_ref[...] += dgp

    rev = lambda i: nt - 1 - i
    return pl.pallas_call(
        body, grid=(nt, NG), name=name,
        in_specs=[pl.BlockSpec((ts, D), lambda i, m: (rev(i), 0)),
                  pl.BlockSpec((None, 2, ts, FB), lambda i, m: (m, 0, rev(i), 0)),
                  pl.BlockSpec((None, 2, ts, FB), lambda i, m: (m, 0, rev(i), 0)),
                  pl.BlockSpec((2, None, D, FB), lambda i, m: (0, m, 0, 0)),
                  pl.BlockSpec((2, None, 3, FB), lambda i, m: (0, m, 0, 0)),
                  pl.BlockSpec((FB, D), lambda i, m: (m, 0)),
                  pl.BlockSpec((ts, D), lambda i, m: (rev(i), 0)),
                  pl.BlockSpec((1, D), lambda i, m: (0, 0))],
        out_specs=[pl.BlockSpec((None, ts, FB), lambda i, m: (m, rev(i), 0)),
                   pl.BlockSpec((None, 2, ts, FB), lambda i, m: (m, 0, rev(i), 0)),
                   pl.BlockSpec((ts, D), lambda i, m: (rev(i), 0)),
                   pl.BlockSpec((1, D), lambda i, m: (0, 0)),
                   pl.BlockSpec((NG, 2, 3, FB), lambda i, m: (0, 0, 0, 0))],
        out_shape=[jax.ShapeDtypeStruct((NG, s, FB), BF16), jax.ShapeDtypeStruct((NG, 2, s, FB), BF16),
                   jax.ShapeDtypeStruct((s, D), F32), jax.ShapeDtypeStruct((1, D), F32),
                   jax.ShapeDtypeStruct((NG, 2, 3, FB), F32)],
        scratch_shapes=[pltpu.VMEM((NG, 2, 8, FB), F32), pltpu.VMEM((ts, D), F32),
                        pltpu.VMEM((2, 3, 8, FB), F32)],
        compiler_params=_params("arbitrary", "arbitrary"),
    )(df, up, upc, wup, fcw, wdn, xin, g)


def _mm_nt(dy, w, name):
    s = dy.shape[0]
    ts = _tile(s, 512)

    def body(dy_ref, w_ref, o_ref):
        o_ref[...] = lax.dot_general(dy_ref[...].astype(BF16), w_ref[...], NT_DIMS,
                                     preferred_element_type=F32).astype(BF16)

    return pl.pallas_call(
        body, grid=(s // ts,), name=name,
        in_specs=[pl.BlockSpec((ts, D), lambda i: (i, 0)), pl.BlockSpec((D, D), lambda i: (0, 0))],
        out_specs=pl.BlockSpec((ts, D), lambda i: (i, 0)),
        out_shape=jax.ShapeDtypeStruct((s, D), BF16),
        compiler_params=_params("parallel"),
    )(dy, w)


def _mm_nt_rms(dy, wblk, x, g, dres, name):
    s = dy.shape[0]
    nb, _, bn = wblk.shape
    pair = _pair(bn)
    ts = _tile(s, 512)

    def body(dy_ref, w_ref, x_ref, g_ref, dr_ref, dx_ref, dg_ref):
        i = pl.program_id(0)
        acc = jnp.zeros((ts, D), F32)
        for b in range(0, nb, pair):
            acc = acc + lax.dot_general(dy_ref[:, b * bn:(b + pair) * bn], _cols(w_ref, b, pair), NT_DIMS,
                                        preferred_element_type=F32)
        dx, dgp = _rms_bwd_math(acc, x_ref[...], g_ref[...])
        dx_ref[...] = dr_ref[...] + dx

        @pl.when(i == 0)
        def _():
            dg_ref[...] = dgp

        @pl.when(i > 0)
        def _():
            dg_ref[...] += dgp

    return pl.pallas_call(
        body, grid=(s // ts,), name=name,
        in_specs=[pl.BlockSpec((ts, nb * bn), lambda i: (i, 0)), pl.BlockSpec((nb, D, bn), lambda i: (0, 0, 0)),
                  pl.BlockSpec((ts, D), lambda i: (i, 0)), pl.BlockSpec((1, D), lambda i: (0, 0)),
                  pl.BlockSpec((ts, D), lambda i: (i, 0))],
        out_specs=[pl.BlockSpec((ts, D), lambda i: (i, 0)), pl.BlockSpec((1, D), lambda i: (0, 0))],
        out_shape=[jax.ShapeDtypeStruct((s, D), F32), jax.ShapeDtypeStruct((1, D), F32)],
        compiler_params=_params("arbitrary"),
    )(dy, wblk, x, g, dres)


def _c_bwd(dr, zc, cw, name):
    s = dr.shape[0]
    ts = _tile(s, 512)
    nt = s // ts
    hb = ts // HALO

    def body(dr_ref, drf_ref, z_ref, ch_ref, xh_ref, bf_ref, cw_ref, dz_ref, dcw_ref):
        i = pl.program_id(0)
        cwv = [cw_ref[k:k + 1, :] for k in range(3)]
        z_t = z_ref[...].astype(F32)
        bg, cg, xv = z_t[:, 0:D], z_t[:, D:2 * D], z_t[:, 2 * D:3 * D]
        ph = jnp.where(i > 0, ch_ref[...].astype(F32) * xh_ref[...].astype(F32), 0.0)
        pe = jnp.concatenate([ph, cg * xv], axis=0)
        q, taps = _conv3(pe, cwv, HALO)
        drv = dr_ref[...].astype(F32)
        dq = drv * bg
        dqf = jnp.where(i < nt - 1, drf_ref[...].astype(F32) * bf_ref[...].astype(F32), 0.0)
        dp = _conv3_bwd_in(jnp.concatenate([dq, dqf], axis=0), cwv, ts)
        dz_ref[:, 0:D] = (drv * q).astype(BF16)
        dz_ref[:, D:2 * D] = (dp * xv).astype(BF16)
        dz_ref[:, 2 * D:3 * D] = (dp * cg).astype(BF16)
        rows = _conv3_bwd_w(dq, taps)

        @pl.when(i == 0)
        def _():
            for k in range(3):
                dcw_ref[k:k + 1, :] = rows[k]

        @pl.when(i > 0)
        def _():
            for k in range(3):
                dcw_ref[k:k + 1, :] += rows[k]

    past = lambda col: pl.BlockSpec((HALO, D), lambda i: (jnp.maximum(i * hb - 1, 0), col))
    nxt = lambda i: jnp.minimum((i + 1) * hb, s // HALO - 1)
    return pl.pallas_call(
        body, grid=(nt,), name=name,
        in_specs=[pl.BlockSpec((ts, D), lambda i: (i, 0)),
                  pl.BlockSpec((HALO, D), lambda i: (nxt(i), 0)),
                  pl.BlockSpec((ts, 3 * D), lambda i: (i, 0)), past(1), past(2),
                  pl.BlockSpec((HALO, D), lambda i: (nxt(i), 0)),
                  pl.BlockSpec((3, D), lambda i: (0, 0))],
        out_specs=[pl.BlockSpec((ts, 3 * D), lambda i: (i, 0)), pl.BlockSpec((3, D), lambda i: (0, 0))],
        out_shape=[jax.ShapeDtypeStruct((s, 3 * D), BF16), jax.ShapeDtypeStruct((3, D), F32)],
        compiler_params=_params("arbitrary"),
    )(dr, dr, zc, zc, zc, zc, cw)


G512_ROWS = 40


def _ab_bwd(dy, z, yb2, lga, lba, wsm, bs_col, cwb, lgb, lbb, name):
    s = z.shape[0]
    ts = _tile(s, 256)
    nt = s // ts
    hb = ts // HALO_B
    nch = ts // CHUNK
    tri = None

    def body(z_ref, zh_ref, dy_ref, dyf_ref, yb2_ref, yb2f_ref, lga_ref, lba_ref, ws_ref, bs_ref,
             cw_ref, lgb_ref, lbb_ref, dz_ref, g512_ref, dws_ref, dbs_ref, dvn_ref):
        i = pl.program_id(0)
        last = i == nt - 1

        @pl.when(i == 0)
        def _():
            g512_ref[...] = jnp.zeros((G512_ROWS, DA), F32)
            dws_ref[...] = jnp.zeros((HEADS, CHUNK, CHUNK), F32)
            dbs_ref[...] = jnp.zeros((HEADS, CHUNK, 1), F32)

        def add_row(k, v):
            g512_ref[k:k + 1, :] += v

        z_t = z_ref[...].astype(F32)
        dy_t = dy_ref[...].astype(F32)
        ua, va = z_t[:, 0:DA], z_t[:, DA:2 * DA]
        gu = _gelu(ua)
        gv = _gelu(va)
        lga_v = lga_ref[...]
        vn, xhat_a, rstd_a = _ln_fwd(gv, lga_v, lba_ref[...])
        vnb = vn.astype(BF16)
        causal = (lax.broadcasted_iota(jnp.int32, (CHUNK, CHUNK), 0)
                  >= lax.broadcasted_iota(jnp.int32, (CHUNK, CHUNK), 1)).astype(F32)
        for c in range(nch):
            for h in range(HEADS):
                rs = slice(c * CHUNK, (c + 1) * CHUNK)
                cs = slice(h * CHUNK, (h + 1) * CHUNK)
                vblk = vnb[rs, cs]
                mixed = jnp.dot(ws_ref[h], vblk, preferred_element_type=F32) + bs_ref[h]
                dyb_ = dy_t[rs, cs]
                dmix = dyb_ * gu[rs, cs]
                dmb = dmix.astype(BF16)
                dz_ref[rs, cs] = (dyb_ * mixed * _dgelu(ua[rs, cs])).astype(BF16)
                dvn_ref[rs, cs] = lax.dot_general(ws_ref[h], dmb, TN_DIMS, preferred_element_type=F32)
                dws_ref[h] += causal * lax.dot_general(dmb, vblk, NT_DIMS, preferred_element_type=F32)
                dbs_ref[h] += jnp.sum(dmix, axis=1, keepdims=True)
        dvn = dvn_ref[...]
        add_row(0, jnp.sum(dvn * xhat_a, axis=0, keepdims=True))
        add_row(1, jnp.sum(dvn, axis=0, keepdims=True))
        dgv = _ln_bwd(dvn, xhat_a, rstd_a, lga_v)
        dz_ref[:, DA:2 * DA] = (dgv * _dgelu(va)).astype(BF16)
        lgb_v = lgb_ref[...]
        dyb_e = jnp.concatenate(
            [dy_t[:, DA:2 * DA], jnp.where(last, 0.0, dyf_ref[...].astype(F32))], axis=0)
        yb2_e = jnp.concatenate([yb2_ref[...], jnp.where(last, 0.0, yb2f_ref[...])], axis=0)
        n_e, xhat_b, rstd_b = _ln_fwd(yb2_e, lgb_v, lbb_ref[...])
        sgn = _sigmoid(n_e)
        dn = dyb_e * _dsilu(n_e, sgn)
        dy2 = _ln_bwd(dn, xhat_b, rstd_b, lgb_v)
        add_row(2, jnp.sum(dy2[:ts], axis=0, keepdims=True))
        add_row(3, jnp.sum(dn[:ts] * xhat_b[:ts], axis=0, keepdims=True))
        add_row(4, jnp.sum(dn[:ts], axis=0, keepdims=True))
        zh = jnp.where(i > 0, zh_ref[...], jnp.zeros_like(zh_ref[...])).astype(F32)
        xb_t, gb_t = z_t[:, 2 * DA:3 * DA], z_t[:, 3 * DA:4 * DA]
        sgb = _sigmoid(gb_t)
        ue = jnp.concatenate([zh[:, 0:DA] * _sigmoid(zh[:, DA:2 * DA]), xb_t * sgb], axis=0)
        dy2_t = dy2[:ts]
        n_e_rows = ts + HALO_B
        du = jnp.zeros((ts, DA), F32)
        for r in range(8):
            fwd_roll = ue if r == 0 else pltpu.roll(ue, r, 0)
            bwd_roll = dy2 if r == 0 else pltpu.roll(dy2, n_e_rows - r, 0)
            for q in range(4):
                sh = 8 * q + r
                if sh >= BCONV:
                    continue
                k = BCONV - 1 - sh
                du = du + cw_ref[k:k + 1, :] * bwd_roll[8 * q:8 * q + ts]
                add_row(8 + k, jnp.sum(dy2_t * fwd_roll[HALO_B - 8 * q:HALO_B - 8 * q + ts],
                                       axis=0, keepdims=True))
        dz_ref[:, 2 * DA:3 * DA] = (du * sgb).astype(BF16)
        dz_ref[:, 3 * DA:4 * DA] = (du * xb_t * sgb * (1.0 - sgb)).astype(BF16)

    row = lambda i: (0, 0)
    nxt = lambda i: jnp.minimum((i + 1) * hb, s // HALO_B - 1)
    return pl.pallas_call(
        body, grid=(nt,), name=name,
        in_specs=[pl.BlockSpec((ts, 4 * DA), lambda i: (i, 0)),
                  pl.BlockSpec((HALO_B, 2 * DA), lambda i: (jnp.maximum(i * hb - 1, 0), 1)),
                  pl.BlockSpec((ts, 2 * DA), lambda i: (i, 0)),
                  pl.BlockSpec((HALO_B, DA), lambda i: (nxt(i), 1)),
                  pl.BlockSpec((ts, DA), lambda i: (i, 0)),
                  pl.BlockSpec((HALO_B, DA), lambda i: (nxt(i), 0)),
                  pl.BlockSpec((1, DA), row), pl.BlockSpec((1, DA), row),
                  pl.BlockSpec((HEADS, CHUNK, CHUNK), lambda i: (0, 0, 0)),
                  pl.BlockSpec((HEADS, CHUNK, 1), lambda i: (0, 0, 0)),
                  pl.BlockSpec((BCONV, DA), row), pl.BlockSpec((1, DA), row), pl.BlockSpec((1, DA), row)],
        out_specs=[pl.BlockSpec((ts, 4 * DA), lambda i: (i, 0)),
                   pl.BlockSpec((G512_ROWS, DA), row),
                   pl.BlockSpec((HEADS, CHUNK, CHUNK), lambda i: (0, 0, 0)),
                   pl.BlockSpec((HEADS, CHUNK, 1), lambda i: (0, 0, 0))],
        out_shape=[jax.ShapeDtypeStruct((s, 4 * DA), BF16), jax.ShapeDtypeStruct((G512_ROWS, DA), F32),
                   jax.ShapeDtypeStruct((HEADS, CHUNK, CHUNK), F32),
                   jax.ShapeDtypeStruct((HEADS, CHUNK, 1), F32)],
        scratch_shapes=[pltpu.VMEM((ts, DA), F32)],
        compiler_params=_params("arbitrary"),
    )(z, z, dy, dy, yb2, yb2, lga, lba, wsm, bs_col, cwb, lgb, lbb)


def _dw_cols(a, dy, nb, bn, name):
    s = a.shape[0]
    tm = _tile(s, 2048)
    nt = s // tm
    cpb = 4
    pair = _pair(bn)

    def body(a_ref, dy_ref, o_ref, acc):
        t = pl.program_id(1)
        av = a_ref[...]
        for b in range(0, cpb, pair):
            p = lax.dot_general(av, dy_ref[:, b * bn:(b + pair) * bn], TN_DIMS, preferred_element_type=F32)

            @pl.when(t == 0)
            def _():
                for q in range(pair):
                    acc[b + q] = p[:, q * bn:(q + 1) * bn]

            @pl.when(t > 0)
            def _():
                for q in range(pair):
                    acc[b + q] += p[:, q * bn:(q + 1) * bn]

        @pl.when(t == nt - 1)
        def _():
            o_ref[...] = acc[...].astype(BF16)

    return pl.pallas_call(
        body, grid=(nb // cpb, nt), name=name,
        in_specs=[pl.BlockSpec((tm, D), lambda j, t: (t, 0)), pl.BlockSpec((tm, cpb * bn), lambda j, t: (t, j))],
        out_specs=pl.BlockSpec((cpb, D, bn), lambda j, t: (j, 0, 0)),
        out_shape=jax.ShapeDtypeStruct((nb, D, bn), BF16),
        scratch_shapes=[pltpu.VMEM((cpb, D, bn), F32)],
        compiler_params=_params("arbitrary", "arbitrary"),
    )(a, dy)


def _dw_rows(a, dy, name):
    s = a.shape[0]
    tm = _tile(s, 2048)
    nt = s // tm
    rb = 512

    def body(a_ref, dy_ref, o_ref, acc):
        t = pl.program_id(1)
        p = lax.dot_general(a_ref[...], dy_ref[...].astype(BF16), TN_DIMS, preferred_element_type=F32)

        @pl.when(t == 0)
        def _():
            acc[...] = p

        @pl.when(t > 0)
        def _():
            acc[...] += p

        @pl.when(t == nt - 1)
        def _():
            o_ref[...] = acc[...].astype(BF16)

    return pl.pallas_call(
        body, grid=(D // rb, nt), name=name,
        in_specs=[pl.BlockSpec((tm, rb), lambda j, t: (t, j)), pl.BlockSpec((tm, D), lambda j, t: (t, 0))],
        out_specs=pl.BlockSpec((rb, D), lambda j, t: (j, 0)),
        out_shape=jax.ShapeDtypeStruct((D, D), BF16),
        scratch_shapes=[pltpu.VMEM((rb, D), F32)],
        compiler_params=_params("arbitrary", "arbitrary"),
    )(a, dy)


def _dw_up(h, dup, name):
    s = h.shape[0]
    tm = _tile(s, 2048)
    nt = s // tm

    def body(h_ref, d_ref, o_ref, acc):
        t = pl.program_id(1)
        p = lax.dot_general(d_ref[...], h_ref[...], TN_DIMS, preferred_element_type=F32)

        @pl.when(t == 0)
        def _():
            acc[...] = p

        @pl.when(t > 0)
        def _():
            acc[...] += p

        @pl.when(t == nt - 1)
        def _():
            o_ref[...] = acc[...].astype(BF16)

    return pl.pallas_call(
        body, grid=(NDEV, nt), name=name,
        in_specs=[pl.BlockSpec((tm, D), lambda b, t: (t, 0)),
                  pl.BlockSpec((None, None, tm, FB), lambda b, t: (b % NG, b // NG, t, 0))],
        out_specs=pl.BlockSpec((None, FB, D), lambda b, t: (b, 0, 0)),
        out_shape=jax.ShapeDtypeStruct((NDEV, FB, D), BF16),
        scratch_shapes=[pltpu.VMEM((FB, D), F32)],
        compiler_params=_params("arbitrary", "arbitrary"),
    )(h, dup)


def _dw_dn(a, df, name):
    s = df.shape[0]
    tm = _tile(s, 2048)
    nt = s // tm

    def body(a_ref, d_ref, o_ref, acc):
        t = pl.program_id(1)
        p = lax.dot_general(a_ref[...], d_ref[...].astype(BF16), TN_DIMS, preferred_element_type=F32)

        @pl.when(t == 0)
        def _():
            acc[...] = p

        @pl.when(t > 0)
        def _():
            acc[...] += p

        @pl.when(t == nt - 1)
        def _():
            o_ref[...] = acc[...].astype(BF16)

    return pl.pallas_call(
        body, grid=(NG, nt), name=name,
        in_specs=[pl.BlockSpec((None, tm, FB), lambda m, t: (m, t, 0)), pl.BlockSpec((tm, D), lambda m, t: (t, 0))],
        out_specs=pl.BlockSpec((FB, D), lambda m, t: (m, 0)),
        out_shape=jax.ShapeDtypeStruct((DFF, D), BF16),
        scratch_shapes=[pltpu.VMEM((FB, D), F32)],
        compiler_params=_params("arbitrary", "arbitrary"),
    )(a, df)


def _place():
    x, y, c = lax.axis_index("x"), lax.axis_index("y"), lax.axis_index("c")
    chips = [(1 - x, y), (x, 1 - y), (1 - x, 1 - y)]
    return x, y, c, chips


def _zone(shard, dev):
    return lax.dynamic_update_slice(lax.empty((NDEV,) + shard.shape, shard.dtype), shard[None],
                                    (dev,) + (0,) * shard.ndim)


def _all_gather(zones, name):
    nt = len(zones)

    def body(*refs):
        dsts = refs[nt:2 * nt]
        send_sems, recv_sems = refs[2 * nt:]
        x, y, c, chips = _place()
        me, sib = (x, y, c), (x, y, 1 - c)

        def blk(t, p):
            return dsts[t].at[4 * p[0] + 2 * p[1] + p[2]]

        def copy(t, k, block, to):
            return pltpu.make_async_remote_copy(
                src_ref=blk(t, block), dst_ref=blk(t, block),
                send_sem=send_sems.at[t, k], recv_sem=recv_sems.at[t, k],
                device_id=to, device_id_type=MESH)

        first = []
        for t in range(nt):
            first.append(copy(t, 0, me, sib))
            first += [copy(t, 1 + j, me, (*chip, c)) for j, chip in enumerate(chips)]
        for cp in first:
            cp.start()
        passed = []
        for j, chip in enumerate(chips):
            for t in range(nt):
                copy(t, 1 + j, (*chip, c), me).wait_recv()
                cp = copy(t, 4 + j, (*chip, c), sib)
                cp.start()
                passed.append(cp)
        for t in range(nt):
            copy(t, 0, sib, me).wait_recv()
            for j, chip in enumerate(chips):
                copy(t, 4 + j, (*chip, 1 - c), me).wait_recv()
        for cp in first + passed:
            cp.wait_send()

    return pl.pallas_call(
        body, name=name,
        in_specs=[ANY] * nt, out_specs=[ANY] * nt,
        out_shape=[jax.ShapeDtypeStruct(a.shape, a.dtype) for a in zones],
        input_output_aliases={t: t for t in range(nt)},
        scratch_shapes=[pltpu.SemaphoreType.DMA((nt, 7)), pltpu.SemaphoreType.DMA((nt, 7))],
        compiler_params=pltpu.CompilerParams(has_side_effects=True),
    )(*zones)


HBM_SPEC = pl.BlockSpec(memory_space=pltpu.HBM)
SEM_SPEC = pl.BlockSpec(memory_space=pltpu.SEMAPHORE)
DATAFLOW = pltpu.SideEffectType.DATAFLOW_SIDE_EFFECTING


def _hbm(a):
    return pltpu.with_memory_space_constraint(a, pltpu.HBM)


def _hbm_like(arrs):
    return [pltpu.HBM(a.shape, a.dtype) for a in arrs]


def _ag_start(srcs, lands, after, name):
    n = len(srcs)
    ns = 8 * n

    def body(*refs):
        src, land = refs[:n], refs[n:2 * n]
        sems = refs[2 * n + 1:2 * n + 1 + ns]
        token = refs[-1]
        x, y, c, chips = _place()
        peers = [(x, y, 1 - c)] + [(*chip, c) for chip in chips]
        for t in range(n):
            for k, to in enumerate(peers):
                pltpu.make_async_remote_copy(
                    src_ref=src[t], dst_ref=land[t].at[4 * x + 2 * y + c],
                    send_sem=sems[2 * (4 * t + k)], recv_sem=sems[2 * (4 * t + k) + 1],
                    device_id=to, device_id_type=MESH).start()
        token[...] = jnp.zeros_like(token)

    res = pl.pallas_call(
        body, name=name,
        in_specs=[HBM_SPEC] * (2 * n) + [ANY],
        out_specs=[SEM_SPEC] * ns + [HBM_SPEC] * (2 * n) + [pl.BlockSpec(memory_space=pltpu.VMEM)],
        out_shape=[pltpu.SemaphoreType.DMA(())] * ns + _hbm_like(srcs) + _hbm_like(lands)
        + [jax.ShapeDtypeStruct((8, 128), F32)],
        input_output_aliases={i: ns + i for i in range(2 * n)},
        compiler_params=pltpu.CompilerParams(has_side_effects=DATAFLOW),
    )(*[_hbm(a) for a in srcs], *[_hbm(a) for a in lands], after)
    sems = [[(res[2 * (4 * t + k)], res[2 * (4 * t + k) + 1]) for k in range(4)] for t in range(n)]
    return sems, res[ns:ns + n], res[ns + n:ns + 2 * n], res[-1]


def _ag_forward(srcs, lands, sems1, after, name):
    n = len(srcs)
    flat1 = [s for t in range(n) for k in range(1, 4) for s in sems1[t][k]]
    n1 = len(flat1)

    def body(*refs):
        src, land = refs[:n], refs[n:2 * n]
        s1 = refs[2 * n:2 * n + n1]
        s2 = refs[2 * n + n1 + 1:2 * n + n1 + 1 + 6 * n]
        x, y, c, chips = _place()
        for j, (cx, cy) in enumerate(chips):
            for t in range(n):
                blk = land[t].at[4 * cx + 2 * cy + c]
                pltpu.make_async_remote_copy(
                    src_ref=src[t], dst_ref=blk, send_sem=s1[2 * (3 * t + j)], recv_sem=s1[2 * (3 * t + j) + 1],
                    device_id=(cx, cy, c), device_id_type=MESH).wait_recv()
                pltpu.make_async_remote_copy(
                    src_ref=blk, dst_ref=blk, send_sem=s2[2 * (3 * t + j)], recv_sem=s2[2 * (3 * t + j) + 1],
                    device_id=(x, y, 1 - c), device_id_type=MESH).start()

    res = pl.pallas_call(
        body, name=name,
        in_specs=[HBM_SPEC] * (2 * n) + [SEM_SPEC] * n1 + [ANY],
        out_specs=[SEM_SPEC] * (6 * n) + [HBM_SPEC] * n,
        out_shape=[pltpu.SemaphoreType.DMA(())] * (6 * n) + _hbm_like(lands),
        input_output_aliases={n + i: 6 * n + i for i in range(n)},
        compiler_params=pltpu.CompilerParams(has_side_effects=DATAFLOW),
    )(*srcs, *lands, *flat1, after)
    sems2 = [[(res[2 * (3 * t + j)], res[2 * (3 * t + j) + 1]) for j in range(3)] for t in range(n)]
    return sems2, res[6 * n:]


def _ag_finish(srcs, lands, sems1, sems2, after, name):
    n = len(srcs)
    flat1 = [s for t in range(n) for k in range(4) for s in sems1[t][k]]
    flat2 = [s for t in range(n) for j in range(3) for s in sems2[t][j]]
    n1, n2 = len(flat1), len(flat2)

    def body(*refs):
        src, land = refs[:n], refs[n:2 * n]
        s1 = refs[2 * n:2 * n + n1]
        s2 = refs[2 * n + n1:2 * n + n1 + n2]
        x, y, c, chips = _place()
        sib = (x, y, 1 - c)
        for t in range(n):
            own = land[t].at[4 * x + 2 * y + 1 - c]
            pltpu.make_async_remote_copy(
                src_ref=src[t], dst_ref=own, send_sem=s1[8 * t], recv_sem=s1[8 * t + 1],
                device_id=sib, device_id_type=MESH).wait_recv()
            for k in range(4):
                pltpu.make_async_remote_copy(
                    src_ref=src[t], dst_ref=own, send_sem=s1[2 * (4 * t + k)], recv_sem=s1[2 * (4 * t + k) + 1],
                    device_id=sib, device_id_type=MESH).wait_send()
            for j, (cx, cy) in enumerate(chips):
                blk = land[t].at[4 * cx + 2 * cy + 1 - c]
                cp = pltpu.make_async_remote_copy(
                    src_ref=blk, dst_ref=blk, send_sem=s2[2 * (3 * t + j)], recv_sem=s2[2 * (3 * t + j) + 1],
                    device_id=sib, device_id_type=MESH)
                cp.wait_send()
                cp.wait_recv()

    return pl.pallas_call(
        body, name=name,
        in_specs=[HBM_SPEC] * (2 * n) + [SEM_SPEC] * (n1 + n2) + [ANY],
        out_specs=[HBM_SPEC] * n,
        out_shape=_hbm_like(lands),
        input_output_aliases={n + i: i for i in range(n)},
        compiler_params=pltpu.CompilerParams(has_side_effects=DATAFLOW),
    )(*srcs, *lands, *flat1, *flat2, after)


def _pair_copies(srcs, dsts, sems):
    x, y, c, _ = _place()
    nt = len(srcs)
    return [pltpu.make_async_remote_copy(
        src_ref=srcs[t].at[2 * j + 1 - c], dst_ref=dsts[t].at[j],
        send_sem=sems[2 * (NCHIP * t + j)], recv_sem=sems[2 * (NCHIP * t + j) + 1],
        device_id=(x, y, 1 - c), device_id_type=MESH) for t in range(nt) for j in range(NCHIP)]


def _pair_start(grads, carry, name):
    nt = len(grads)
    ns = 2 * NCHIP * nt
    zones = [_hbm(lax.empty((NCHIP,) + a.shape[1:], a.dtype)) for a in grads]
    extra = [] if carry is None else [_hbm(carry)]
    ne = len(extra)

    def body(*refs):
        for cp in _pair_copies(refs[:nt], refs[nt:2 * nt], refs[2 * nt + ne:2 * nt + ne + ns]):
            cp.start()

    res = pl.pallas_call(
        body, name=name,
        in_specs=[HBM_SPEC] * (2 * nt + ne),
        out_specs=[SEM_SPEC] * ns + [HBM_SPEC] * (2 * nt + ne),
        out_shape=[pltpu.SemaphoreType.DMA(())] * ns + _hbm_like(grads) + _hbm_like(zones) + _hbm_like(extra),
        input_output_aliases={i: ns + i for i in range(2 * nt + ne)},
        compiler_params=pltpu.CompilerParams(has_side_effects=DATAFLOW),
    )(*[_hbm(a) for a in grads], *zones, *extra)
    handle = (list(res[:ns]), list(res[ns:ns + nt]), list(res[ns + nt:ns + 2 * nt]))
    return handle, (res[ns + 2 * nt] if ne else None)


def _pair_wait(handle, after, name):
    sems, srcs, zones = handle
    nt, ns = len(srcs), len(sems)

    def body(*refs):
        for cp in _pair_copies(refs[:nt], refs[nt:2 * nt], refs[2 * nt:2 * nt + ns]):
            cp.wait_send()
            cp.wait_recv()

    return pl.pallas_call(
        body, name=name,
        in_specs=[HBM_SPEC] * (2 * nt) + [SEM_SPEC] * ns + [ANY],
        out_specs=[HBM_SPEC] * nt,
        out_shape=_hbm_like(zones),
        input_output_aliases={nt + i: i for i in range(nt)},
        compiler_params=pltpu.CompilerParams(has_side_effects=DATAFLOW),
    )(*srcs, *zones, *sems, after)


def _rows_tile(r, row_bytes, cap_bytes):
    best = None
    for tr in range(16, r + 1, 16):
        if r % tr == 0 and tr * row_bytes <= cap_bytes:
            best = tr
    return best if best is not None else r


def _pair_sum(own, got, cidx, name):
    _, _, r, cdim = own.shape
    tr = _rows_tile(r, 2 * cdim, 2 * 1024 * 1024)

    def body(c_ref, a_ref, b_ref, o_ref):
        o_ref[...] = (a_ref[...].astype(F32) + b_ref[...].astype(F32)).astype(BF16)

    return pl.pallas_call(
        body, name=name,
        grid_spec=pltpu.PrefetchScalarGridSpec(
            num_scalar_prefetch=1, grid=(NCHIP, r // tr),
            in_specs=[pl.BlockSpec((None, None, tr, cdim), lambda j, i, c_ref: (j, c_ref[0], i, 0)),
                      pl.BlockSpec((None, tr, cdim), lambda j, i, c_ref: (j, i, 0))],
            out_specs=pl.BlockSpec((None, tr, cdim), lambda j, i, c_ref: (j, i, 0))),
        out_shape=jax.ShapeDtypeStruct((NCHIP, r, cdim), BF16),
        compiler_params=_params("arbitrary", "arbitrary"),
    )(cidx, own, got)


def _chip_copies(srcs, zones, slots, sems):
    x, y, c, chips = _place()
    out = []
    for t, (z, l) in enumerate(slots):
        for k, (cx, cy) in enumerate(chips):
            dst = zones[z].at[k] if l is None else zones[z].at[k, l]
            out.append(pltpu.make_async_remote_copy(
                src_ref=srcs[t].at[2 * cx + cy], dst_ref=dst,
                send_sem=sems[2 * (3 * t + k)], recv_sem=sems[2 * (3 * t + k) + 1],
                device_id=(cx, cy, c), device_id_type=MESH))
    return out


def _chip_start(sums, zones, slots, carry, name):
    nt, nz = len(sums), len(zones)
    ns = 6 * nt
    extra = [] if carry is None else [_hbm(carry)]
    ne = len(extra)

    def body(*refs):
        for cp in _chip_copies(refs[:nt], refs[nt:nt + nz], slots, refs[nt + nz + ne:nt + nz + ne + ns]):
            cp.start()

    res = pl.pallas_call(
        body, name=name,
        in_specs=[HBM_SPEC] * (nt + nz + ne),
        out_specs=[SEM_SPEC] * ns + [HBM_SPEC] * (nt + nz + ne),
        out_shape=[pltpu.SemaphoreType.DMA(())] * ns + _hbm_like(sums) + _hbm_like(zones) + _hbm_like(extra),
        input_output_aliases={i: ns + i for i in range(nt + nz + ne)},
        compiler_params=pltpu.CompilerParams(has_side_effects=DATAFLOW),
    )(*[_hbm(a) for a in sums], *zones, *extra)
    return (list(res[:ns]), list(res[ns:ns + nt]), list(res[ns + nt:ns + nt + nz]),
            (res[ns + nt + nz] if ne else None))


def _chip_wait(started, zones, zone_ids, after, name):
    started = [(sums, [(zone_ids.index(z), l) for z, l in slots], sems) for sums, slots, sems in started]
    nz = len(zones)
    flat_src = [a for sums, _, _ in started for a in sums]
    flat_sem = [s for _, _, sems in started for s in sems]
    n_src, n_sem = len(flat_src), len(flat_sem)

    def body(*refs):
        srcs, zs, sems = refs[:n_src], refs[n_src:n_src + nz], refs[n_src + nz:n_src + nz + n_sem]
        so, se = 0, 0
        for sums, slots, sem_list in started:
            for cp in _chip_copies(srcs[so:so + len(sums)], zs, slots, sems[se:se + len(sem_list)]):
                cp.wait_send()
                cp.wait_recv()
            so += len(sums)
            se += len(sem_list)

    return pl.pallas_call(
        body, name=name,
        in_specs=[HBM_SPEC] * (n_src + nz) + [SEM_SPEC] * n_sem + [ANY],
        out_specs=[HBM_SPEC] * nz,
        out_shape=_hbm_like(zones),
        input_output_aliases={n_src + i: i for i in range(nz)},
        compiler_params=pltpu.CompilerParams(has_side_effects=DATAFLOW),
    )(*flat_src, *zones, *flat_sem, after)


def _small_allreduce(parts, after, name):
    nt = len(parts)

    def body(*refs):
        srcs, outs, bufs = refs[:nt], refs[nt + 1:2 * nt + 1], refs[2 * nt + 1:3 * nt + 1]
        send_sems, recv_sems = refs[3 * nt + 1:]
        x, y, c, _ = _place()
        peers = [(x, y, 1 - c), (1 - x, y, c), (x, 1 - y, c)]
        for t in range(nt):
            outs[t][...] = srcs[t][...]
        for step, peer in enumerate(peers):
            copies = [pltpu.make_async_remote_copy(
                src_ref=outs[t], dst_ref=bufs[t].at[step],
                send_sem=send_sems.at[step, t], recv_sem=recv_sems.at[step, t],
                device_id=peer, device_id_type=MESH) for t in range(nt)]
            for cp in copies:
                cp.start()
            for cp in copies:
                cp.wait()
            for t in range(nt):
                outs[t][...] = outs[t][...] + bufs[t][step]

    vm = pl.BlockSpec(memory_space=pltpu.VMEM)
    return pl.pallas_call(
        body, name=name,
        in_specs=[vm] * nt + [ANY], out_specs=[vm] * nt,
        out_shape=[jax.ShapeDtypeStruct(a.shape, F32) for a in parts],
        scratch_shapes=[pltpu.VMEM((3,) + a.shape, F32) for a in parts]
        + [pltpu.SemaphoreType.DMA((3, nt)), pltpu.SemaphoreType.DMA((3, nt))],
        compiler_params=pltpu.CompilerParams(has_side_effects=True, vmem_limit_bytes=VMEM_LIMIT),
    )(*parts, after)


def _adam_math(w, g, m, v):
    m2 = ADAM_B1 * m + (1.0 - ADAM_B1) * g
    v2 = ADAM_B2 * v + (1.0 - ADAM_B2) * (g * g)
    m_hat = m2 / (1.0 - ADAM_B1 ** ADAM_STEP)
    v_hat = v2 / (1.0 - ADAM_B2 ** ADAM_STEP)
    delta = -ADAM_LR * (m_hat / (jnp.sqrt(v_hat) + ADAM_EPS) + ADAM_WD * w)
    return delta, m2, v2


def _adam_big(w, m, v, parts, mine, chip, name):
    nl, r, cdim = w.shape
    tr = _rows_tile(r, 4 * cdim, 3 * 512 * 1024)

    def body(c_ref, w_ref, m_ref, v_ref, p_ref, *rest):
        mine_refs, (g_ref, d_ref, mo_ref, vo_ref) = rest[:nl], rest[nl:]
        own = mine_refs[0][...]
        for l in range(1, nl):
            own = jnp.where(pl.program_id(0) == l, mine_refs[l][...], own)
        g = ((p_ref[0].astype(F32) + p_ref[1].astype(F32)) + p_ref[2].astype(F32)) + own.astype(F32)
        delta, m2, v2 = _adam_math(w_ref[...], g, m_ref[...], v_ref[...])
        g_ref[...] = g
        d_ref[...] = delta
        mo_ref[...] = m2
        vo_ref[...] = v2

    spec = pl.BlockSpec((None, tr, cdim), lambda l, i, c_ref: (l, i, 0))
    mine_specs = [pl.BlockSpec((None, tr, cdim), lambda l, i, c_ref, ll=ll: (c_ref[0], jnp.where(l == ll, i, 0), 0))
                  for ll in range(nl)]
    return pl.pallas_call(
        body, name=name,
        grid_spec=pltpu.PrefetchScalarGridSpec(
            num_scalar_prefetch=1, grid=(nl, r // tr),
            in_specs=[spec, spec, spec, pl.BlockSpec((3, None, tr, cdim), lambda l, i, c_ref: (0, l, i, 0))]
            + mine_specs,
            out_specs=[spec] * 4),
        out_shape=[jax.ShapeDtypeStruct(w.shape, F32)] * 4,
        compiler_params=_params("arbitrary", "arbitrary"),
    )(chip, w, m, v, parts, *mine)


def _adam_small(ws, gs, ms, vs, name):
    n = len(ws)

    def body(*refs):
        w_r, g_r, m_r, v_r = refs[:n], refs[n:2 * n], refs[2 * n:3 * n], refs[3 * n:4 * n]
        d_o, m_o, v_o = refs[4 * n:5 * n], refs[5 * n:6 * n], refs[6 * n:7 * n]
        for t in range(n):
            delta, m2, v2 = _adam_math(w_r[t][...], g_r[t][...], m_r[t][...], v_r[t][...])
            d_o[t][...] = delta
            m_o[t][...] = m2
            v_o[t][...] = v2

    vm = pl.BlockSpec(memory_space=pltpu.VMEM)
    shapes = [jax.ShapeDtypeStruct(a.shape, F32) for a in ws]
    return pl.pallas_call(
        body, name=name, in_specs=[vm] * (4 * n), out_specs=[vm] * (3 * n), out_shape=shapes * 3,
        compiler_params=pltpu.CompilerParams(vmem_limit_bytes=VMEM_LIMIT),
    )(*ws, *gs, *ms, *vs)


def kernel(x, norm_mix, norm_ffn, norm_final, ab_w_in, a_ln_g, a_ln_b, a_w_s, a_b_s, b_conv_w, b_conv_b, b_ln_g, b_ln_b, ab_w_out, c_w_in, c_conv_w, c_w_out, f_w_up, f_conv_w, f_w_down, loss_target, m_norm_mix, m_norm_ffn, m_norm_final, m_ab_w_in, m_a_ln_g, m_a_ln_b, m_a_w_s, m_a_b_s, m_b_conv_w, m_b_conv_b, m_b_ln_g, m_b_ln_b, m_ab_w_out, m_c_w_in, m_c_conv_w, m_c_w_out, m_f_w_up, m_f_conv_w, m_f_w_down, v_norm_mix, v_norm_ffn, v_norm_final, v_ab_w_in, v_a_ln_g, v_a_ln_b, v_a_w_s, v_a_b_s, v_b_conv_w, v_b_conv_b, v_b_ln_g, v_b_ln_b, v_ab_w_out, v_c_w_in, v_c_conv_w, v_c_w_out, v_f_w_up, v_f_conv_w, v_f_w_down):
    s = x.shape[1]
    x0 = x.reshape(s, D)
    tgt = loss_target.reshape(s, D)
    xi, yi, ci = lax.axis_index("x"), lax.axis_index("y"), lax.axis_index("c")
    dev = 4 * xi + 2 * yi + ci
    cidx = ci.astype(jnp.int32).reshape(1)

    bf = lambda a: a.astype(BF16)
    slab_w = 6 * CHUNK
    pad = lambda a, rows: jnp.pad(a, ((0, rows - a.shape[0]), (0, slab_w - a.shape[1])))
    slab = jnp.concatenate([pad(b_conv_w[0], 32), pad(c_conv_w[0], 8), pad(f_conv_w.reshape(6, FB), 8)], axis=0)
    win0, wout0, slab_g = _all_gather(
        [_zone(bf(ab_w_in[0]), dev), _zone(bf(ab_w_out[0]), dev), _zone(slab, dev)], "all_gather_first")
    later = [bf(f_w_up[0]), bf(f_w_down[0]), bf(c_w_in[0]), bf(c_w_out[0]), bf(f_w_up[1]), bf(f_w_down[1])]
    lands = [_zone(a, dev) for a in later]
    groups = [[0, 1], [2, 3], [4, 5]]
    ag_sems, later, lands, ag_token = _ag_start(later, lands, slab_g, "ag_start")
    wout0 = wout0.reshape(D, D)
    bcw = jnp.transpose(slab_g[:, 0:BCONV, 0:DA // NDEV], (1, 0, 2)).reshape(BCONV, DA)
    ccw = jnp.transpose(slab_g[:, 32:35, 0:D // NDEV], (1, 0, 2)).reshape(3, D)
    fcw_g = slab_g[:, 40:46, 0:FB].reshape(2, NG, 2, 3, FB)
    fcws = [fcw_g[:, :, 0], fcw_g[:, :, 1]]

    causal = jnp.tril(jnp.ones((CHUNK, CHUNK), F32))
    wsm = (a_w_s[0] * causal).astype(BF16)
    bs_col = a_b_s.reshape(HEADS, CHUNK, 1)
    nm = [norm_mix[0:1], norm_mix[1:2]]
    nf = [norm_ffn[0:1], norm_ffn[1:2]]
    nfin = norm_final.reshape(1, D)

    def arrive(g, after_ici, after_d2d, tag):
        srcs = [later[t] for t in groups[g]]
        zone = [lands[t] for t in groups[g]]
        sems1 = [ag_sems[t] for t in groups[g]]
        sems2, zone = _ag_forward(srcs, zone, sems1, after_ici, "ag_forward_" + tag)
        return _ag_finish(srcs, zone, sems1, sems2, after_d2d, "ag_finish_" + tag)

    h0 = _rms_fwd(x0, nm[0], "rms_mix0", after=ag_token)
    z = _mm_in(h0, win0, "mm_ab_in")
    ycat, yb2 = _ab_fwd(z, a_ln_g, a_ln_b, wsm, bs_col, bcw, b_conv_b, b_ln_g, b_ln_b, "ab_fwd")
    x1, h1 = _mm_out(ycat, wout0, x0, nf[0], "mm_ab_out")
    wup0, wdn0 = arrive(0, ycat, x1, "ffn0")
    up0, upc0, x2, h2 = _ffn_fwd(h1, x1, wup0.reshape(2, NG, D, FB), fcws[0], wdn0.reshape(DFF, D), nm[1],
                                 "ffn_fwd0")
    cin, cout = arrive(1, x2, x2, "c")
    cout = cout.reshape(D, D)
    zc = _mm_in(h2, cin, "mm_c_in")
    rc = _c_fwd(zc, ccw, "c_fwd")
    x3, h3 = _mm_out(rc, cout, x2, nf[1], "mm_c_out")
    wup1, wdn1 = arrive(2, rc, x3, "ffn1")
    wups = [wup0.reshape(2, NG, D, FB), wup1.reshape(2, NG, D, FB)]
    wdns = [wdn0.reshape(DFF, D), wdn1.reshape(DFF, D)]
    up1, upc1, x4 = _ffn_fwd(h3, x3, wups[1], fcws[1], wdns[1], None, "ffn_fwd1")
    dx4, dnfin, loss_part = _final(x4, tgt, nfin, "final_loss")
    loss = lax.psum(loss_part[0, 0], ("x", "y", "c"))

    zshape = lambda *sh: _hbm(lax.empty((3,) + sh, BF16))
    zones = [zshape(D, 2 * D // NDEV), zshape(D // NDEV, D), zshape(D, 3 * D // NDEV), zshape(D // NDEV, D),
             zshape(2, FB, D), zshape(2, DFF // NDEV, D)]
    started = []

    def pair_sums(grads, handle, after, tag):
        del grads
        got = _pair_wait(handle, after, "rs_pair_wait_" + tag)
        return [_pair_sum(b.reshape((NCHIP, 2) + b.shape[1:]), g, cidx, "rs_pair_sum_%s%d" % (tag, t))
                for t, (b, g) in enumerate(zip(handle[1], got))]

    def chip_start(sums, slots, carry, tag):
        sems, sums, new_zones, carry = _chip_start(sums, zones, slots, carry, "rs_chip_start_" + tag)
        zones[:] = new_zones
        started.append((sums, slots, sems))
        return sums, carry

    rows8 = lambda g, r: g.reshape(NDEV, r, D)
    a1, dup1, dx3, dnf1, dfcw1 = _ffn_bwd(dx4, up1, upc1, wups[1], fcws[1], wdns[1], x3, nf[1], "ffn_bwd1")
    g_f1 = [_dw_up(h3, dup1, "dw_up1"), rows8(_dw_dn(a1, dx4, "dw_dn1"), DFF // NDEV)]
    hd_f1, dx3 = _pair_start(g_f1, dx3, "rs_pair_start_f1")
    drc = _mm_nt(dx3, cout, "mm_c_out_bwd")
    g_cout = rows8(_dw_rows(rc, dx3, "dw_c_out"), D // NDEV)
    s_f1 = pair_sums(g_f1, hd_f1, g_cout, "f1")
    s_f1, drc = chip_start(s_f1, [(4, 1), (5, 1)], drc, "f1")
    dzc, dccw = _c_bwd(drc, zc, ccw, "c_bwd")
    dx2, dnm1 = _mm_nt_rms(dzc, cin, x2, nm[1], dx3, "mm_c_in_bwd")
    g_c = [_dw_cols(h2, dzc, NDEV, 3 * D // NDEV, "dw_c_in"), g_cout]
    hd_c, dx2 = _pair_start(g_c, dx2, "rs_pair_start_c")
    a0, dup0, dx1, dnf0, dfcw0 = _ffn_bwd(dx2, up0, upc0, wups[0], fcws[0], wdns[0], x1, nf[0], "ffn_bwd0")
    s_c = pair_sums(g_c, hd_c, dx1, "c")
    s_c, dx1 = chip_start(s_c, [(2, None), (3, None)], dx1, "c")
    g_f0 = [_dw_up(h1, dup0, "dw_up0"), rows8(_dw_dn(a0, dx2, "dw_dn0"), DFF // NDEV)]
    hd_f0, dx1 = _pair_start(g_f0, dx1, "rs_pair_start_f0")
    dycat = _mm_nt(dx1, wout0, "mm_ab_out_bwd")
    g_wout0 = rows8(_dw_rows(ycat, dx1, "dw_ab_out"), D // NDEV)
    s_f0 = pair_sums(g_f0, hd_f0, g_wout0, "f0")
    s_f0, dycat = chip_start(s_f0, [(4, 0), (5, 0)], dycat, "f0")
    dz, g512, dws, dbs = _ab_bwd(dycat, z, yb2, a_ln_g, a_ln_b, wsm, bs_col, bcw, b_ln_g, b_ln_b, "ab_bwd")
    grad_x, dnm0 = _mm_nt_rms(dz, win0, x0, nm[0], dx1, "mm_ab_in_bwd")
    g_ab = [_dw_cols(h0, dz, NDEV, 2 * D // NDEV, "dw_ab_in"), g_wout0]
    hd_ab, _ = _pair_start(g_ab, None, "rs_pair_start_ab")
    s_ab = pair_sums(g_ab, hd_ab, grad_x, "ab")
    s_ab, _ = chip_start(s_ab, [(0, None), (1, None)], None, "ab")

    g1024 = jnp.concatenate([dnm0, dnm1, dnf0, dnf1, dnfin, dccw], axis=0)
    gfc = jnp.concatenate([dfcw0, dfcw1], axis=0).reshape(2 * NG * 2 * 3, FB)
    p_cin, p_cout, p_wup, p_wdn = _chip_wait(started[:3], zones[2:], [2, 3, 4, 5], s_ab[0], "rs_chip_wait_early")

    chip = (2 * xi + yi).astype(jnp.int32).reshape(1)

    def big_update(w, m, v, parts, mine, name):
        shp = w.shape
        w3, m3, v3 = (a.reshape((-1,) + shp[-2:]) for a in (w, m, v))
        p4 = parts.reshape((3,) + w3.shape)
        return [o.reshape(shp) for o in _adam_big(w3, m3, v3, p4, mine, chip, name)]

    u_cin = big_update(c_w_in, m_c_w_in, v_c_w_in, p_cin, [s_c[0]], "adam_c_w_in")
    u_cout = big_update(c_w_out, m_c_w_out, v_c_w_out, p_cout, [s_c[1]], "adam_c_w_out")
    tr_ = lambda a: jnp.swapaxes(a, 1, 2)
    u_wup = [tr_(o) for o in big_update(tr_(f_w_up), tr_(m_f_w_up), tr_(v_f_w_up), p_wup,
                                        [s_f0[0], s_f1[0]], "adam_f_w_up")]
    u_wdn = big_update(f_w_down, m_f_w_down, v_f_w_down, p_wdn, [s_f0[1], s_f1[1]], "adam_f_w_down")
    g1024, g512, dws, dbs, gfc = _small_allreduce(
        [g1024, g512, dws.reshape(HEADS * CHUNK, CHUNK), dbs.reshape(HEADS, CHUNK), gfc], u_wdn[0],
        "small_allreduce")
    p_win0, p_wout0 = _chip_wait(started[3:], zones[:2], [0, 1], g1024, "rs_chip_wait_late")
    u_win0 = big_update(ab_w_in, m_ab_w_in, v_ab_w_in, p_win0, [s_ab[0]], "adam_ab_w_in")
    u_wout0 = big_update(ab_w_out, m_ab_w_out, v_ab_w_out, p_wout0, [s_ab[1]], "adam_ab_w_out")

    g_norm_mix = g1024[0:2]
    g_norm_ffn = g1024[2:4]
    g_norm_final = g1024[4:5]
    g_ccw = lax.dynamic_slice(g1024[5:8], (0, dev * (D // NDEV)), (3, D // NDEV))
    g_bcw = lax.dynamic_slice(g512[8:8 + BCONV], (0, dev * (DA // NDEV)), (BCONV, DA // NDEV))
    gfc = gfc.reshape(2, NG, 2, 3, FB)
    g_fcw = lax.dynamic_slice(gfc, (0, dev % NG, dev // NG, 0, 0), (2, 1, 1, 3, FB)).reshape(2, 3, FB)
    small_w = [norm_mix, norm_ffn, nfin, a_ln_g, a_ln_b, a_w_s[0], a_b_s[0], b_conv_w[0], b_conv_b,
               b_ln_g, b_ln_b, c_conv_w[0], f_conv_w]
    small_g = [g_norm_mix, g_norm_ffn, g_norm_final, g512[0:1], g512[1:2],
               dws.reshape(HEADS, CHUNK, CHUNK), dbs, g_bcw, g512[2:3],
               g512[3:4], g512[4:5], g_ccw, g_fcw]
    small_m = [m_norm_mix, m_norm_ffn, m_norm_final.reshape(1, D), m_a_ln_g, m_a_ln_b, m_a_w_s[0], m_a_b_s[0],
               m_b_conv_w[0], m_b_conv_b, m_b_ln_g, m_b_ln_b, m_c_conv_w[0], m_f_conv_w]
    small_v = [v_norm_mix, v_norm_ffn, v_norm_final.reshape(1, D), v_a_ln_g, v_a_ln_b, v_a_w_s[0], v_a_b_s[0],
               v_b_conv_w[0], v_b_conv_b, v_b_ln_g, v_b_ln_b, v_c_conv_w[0], v_f_conv_w]
    upd = _adam_small(small_w, small_g, small_m, small_v, "adam_small")
    ns = len(small_w)
    orig = [norm_mix, norm_ffn, norm_final, a_ln_g, a_ln_b, a_w_s, a_b_s, b_conv_w, b_conv_b,
            b_ln_g, b_ln_b, c_conv_w, f_conv_w]
    sg_out = [g.reshape(o.shape) for g, o in zip(small_g, orig)]
    sd_out = [a.reshape(o.shape) for a, o in zip(upd[0:ns], orig)]
    sm_out = [a.reshape(o.shape) for a, o in zip(upd[ns:2 * ns], orig)]
    sv_out = [a.reshape(o.shape) for a, o in zip(upd[2 * ns:3 * ns], orig)]

    def assemble(small, k):
        return [small[0], small[1], small[2], u_win0[k], small[3], small[4], small[5], small[6], small[7],
                small[8], small[9], small[10], u_wout0[k], u_cin[k], small[11], u_cout[k], u_wup[k],
                small[12], u_wdn[k]]

    grads = assemble(sg_out, 0)
    deltas = assemble(sd_out, 1)
    new_m = assemble(sm_out, 2)
    new_v = assemble(sv_out, 3)
    return (loss, grad_x.reshape(1, s, D), *grads, *deltas, *new_m, *new_v)
```

```python
import functools
import math

import jax
import jax.numpy as jnp
from jax import lax
from jax.experimental import pallas as pl
from jax.experimental.pallas import tpu as pltpu

F32 = jnp.float32
BF16 = jnp.bfloat16

D = 1024
DA = 512
HEADS = 4
CHUNK = 128
DFF = 2816
NDEV = 8
NCHIP = 4
FB = DFF * 2 // NDEV
NG = DFF // FB
BCONV = 31
EPS = 1e-6
HALO = 16
HALO_B = 32
RC = 16
NPART = 2
VMEM_LIMIT = 52 * 1024 * 1024
INV_SQRT2 = 1.0 / math.sqrt(2.0)
INV_SQRT_2PI = 1.0 / math.sqrt(2.0 * math.pi)

ADAM_LR = 0.001
ADAM_B1 = 0.9
ADAM_B2 = 0.999
ADAM_EPS = 1e-08
ADAM_WD = 0.01
ADAM_STEP = 10

MESH = pl.DeviceIdType.MESH
ANY = pl.BlockSpec(memory_space=pl.ANY)
NT_DIMS = (((1,), (1,)), ((), ()))
TN_DIMS = (((0,), (0,)), ((), ()))


def _params(*sem):
    return pltpu.CompilerParams(dimension_semantics=sem, vmem_limit_bytes=VMEM_LIMIT)


def _tile(s, want):
    return min(want, s)


def _sigmoid(x):
    return jax.nn.sigmoid(x)


def _dsilu(x, sg):
    return sg * (1.0 + x * (1.0 - sg))


def _gelu(x):
    return 0.5 * x * (1.0 + lax.erf(x * INV_SQRT2))


def _dgelu(x):
    return 0.5 * (1.0 + lax.erf(x * INV_SQRT2)) + x * jnp.exp(-0.5 * x * x) * INV_SQRT_2PI


def _ln_fwd(x, g, b):
    mu = jnp.mean(x, axis=-1, keepdims=True)
    xc = x - mu
    var = jnp.mean(xc * xc, axis=-1, keepdims=True)
    rstd = lax.rsqrt(var + EPS)
    xhat = xc * rstd
    return xhat * g + b, xhat, rstd


def _ln_bwd(dy, xhat, rstd, g):
    dxh = dy * g
    m1 = jnp.mean(dxh, axis=-1, keepdims=True)
    m2 = jnp.mean(dxh * xhat, axis=-1, keepdims=True)
    return rstd * (dxh - m1 - xhat * m2)


def _rms_bwd_math(dh, x, g):
    r = lax.rsqrt(jnp.mean(x * x, axis=-1, keepdims=True) + EPS)
    xhat = x * r
    dg = jnp.sum(dh * xhat, axis=0, keepdims=True)
    u = dh * g
    dx = r * (u - xhat * jnp.mean(u * xhat, axis=-1, keepdims=True))
    return dx, dg


def _conv3(xe, cw, halo):
    x0 = xe[halo:]
    x1 = pltpu.roll(xe, 1, 0)[halo:]
    x2 = pltpu.roll(xe, 2, 0)[halo:]
    return cw[2] * x0 + cw[1] * x1 + cw[0] * x2, (x0, x1, x2)


def _conv3_bwd_in(dce, cw, ts):
    n = dce.shape[0]
    d1 = pltpu.roll(dce, n - 1, 0)[:ts]
    d2 = pltpu.roll(dce, n - 2, 0)[:ts]
    return cw[2] * dce[:ts] + cw[1] * d1 + cw[0] * d2


def _conv3_bwd_w(dc, taps):
    x0, x1, x2 = taps
    return [jnp.sum(dc * x2, axis=0, keepdims=True), jnp.sum(dc * x1, axis=0, keepdims=True),
            jnp.sum(dc * x0, axis=0, keepdims=True)]


def _rms_fwd(x, g, name, after=None):
    s = x.shape[0]
    ts = _tile(s, 512)

    def body(x_ref, g_ref, *rest):
        h_ref = rest[-1]
        xv = x_ref[...]
        r = lax.rsqrt(jnp.mean(xv * xv, axis=-1, keepdims=True) + EPS)
        h_ref[...] = (xv * r * g_ref[...]).astype(BF16)

    extra = [] if after is None else [after]
    return pl.pallas_call(
        body, grid=(s // ts,), name=name,
        in_specs=[pl.BlockSpec((ts, D), lambda i: (i, 0)), pl.BlockSpec((1, D), lambda i: (0, 0))]
        + [ANY] * len(extra),
        out_specs=pl.BlockSpec((ts, D), lambda i: (i, 0)),
        out_shape=jax.ShapeDtypeStruct((s, D), BF16),
        compiler_params=_params("parallel"),
    )(x, g, *extra)


MXU_COLS = 256


def _pair(bn):
    return 1 if bn % MXU_COLS == 0 else 2


def _cols(w_ref, b, pair):
    return w_ref[b] if pair == 1 else jnp.concatenate([w_ref[b + q] for q in range(pair)], axis=1)


def _mm_in(h, wblk, name):
    s = h.shape[0]
    nb, _, bn = wblk.shape
    pair = _pair(bn)
    ts = _tile(s, 512)

    def body(h_ref, w_ref, o_ref):
        hv = h_ref[...]
        for b in range(0, nb, pair):
            o_ref[:, b * bn:(b + pair) * bn] = jnp.dot(hv, _cols(w_ref, b, pair),
                                                       preferred_element_type=F32).astype(BF16)

    return pl.pallas_call(
        body, grid=(s // ts,), name=name,
        in_specs=[pl.BlockSpec((ts, D), lambda i: (i, 0)), pl.BlockSpec((nb, D, bn), lambda i: (0, 0, 0))],
        out_specs=pl.BlockSpec((ts, nb * bn), lambda i: (i, 0)),
        out_shape=jax.ShapeDtypeStruct((s, nb * bn), BF16),
        compiler_params=_params("parallel"),
    )(h, wblk)


def _rms_math(xv, g):
    r = lax.rsqrt(jnp.mean(xv * xv, axis=-1, keepdims=True) + EPS)
    return (xv * r * g).astype(BF16)


def _mm_out(y, w, xres, gnext, name):
    s = y.shape[0]
    ts = _tile(s, 512)

    def body(y_ref, w_ref, x_ref, g_ref, o_ref, h_ref):
        xn = x_ref[...] + jnp.dot(y_ref[...], w_ref[...], preferred_element_type=F32)
        o_ref[...] = xn
        h_ref[...] = _rms_math(xn, g_ref[...])

    return pl.pallas_call(
        body, grid=(s // ts,), name=name,
        in_specs=[pl.BlockSpec((ts, D), lambda i: (i, 0)), pl.BlockSpec((D, D), lambda i: (0, 0)),
                  pl.BlockSpec((ts, D), lambda i: (i, 0)), pl.BlockSpec((1, D), lambda i: (0, 0))],
        out_specs=[pl.BlockSpec((ts, D), lambda i: (i, 0)), pl.BlockSpec((ts, D), lambda i: (i, 0))],
        out_shape=[jax.ShapeDtypeStruct((s, D), F32), jax.ShapeDtypeStruct((s, D), BF16)],
        compiler_params=_params("parallel"),
    )(y, w, xres, gnext)


def _conv31(ue, cw_ref, ts):
    acc = jnp.zeros((ts, ue.shape[1]), F32)
    for r in range(8):
        rolled = ue if r == 0 else pltpu.roll(ue, r, 0)
        for q in range(4):
            sh = 8 * q + r
            if sh >= BCONV:
                continue
            k = BCONV - 1 - sh
            acc = acc + cw_ref[k:k + 1, :] * rolled[HALO_B - 8 * q:HALO_B - 8 * q + ts]
    return acc


def _ab_fwd(z, lga, lba, wsm, bs_col, cwb, cbb, lgb, lbb, name):
    s = z.shape[0]
    ts = _tile(s, 256)
    hb = ts // HALO_B

    def body(z_ref, zh_ref, lga_ref, lba_ref, ws_ref, bs_ref, cw_ref, cb_ref, lgb_ref, lbb_ref,
             y_ref, yb2_ref):
        i = pl.program_id(0)
        z_t = z_ref[...].astype(F32)
        gu = _gelu(z_t[:, 0:DA])
        gv = _gelu(z_t[:, DA:2 * DA])
        vn, _, _ = _ln_fwd(gv, lga_ref[...], lba_ref[...])
        vnb = vn.astype(BF16)
        for c in range(ts // CHUNK):
            for h in range(HEADS):
                rs = slice(c * CHUNK, (c + 1) * CHUNK)
                cs = slice(h * CHUNK, (h + 1) * CHUNK)
                mixed = jnp.dot(ws_ref[h], vnb[rs, cs], preferred_element_type=F32) + bs_ref[h]
                y_ref[rs, cs] = (gu[rs, cs] * mixed).astype(BF16)
        zh = jnp.where(i > 0, zh_ref[...], jnp.zeros_like(zh_ref[...])).astype(F32)
        xb = jnp.concatenate([zh[:, 0:DA], z_t[:, 2 * DA:3 * DA]], axis=0)
        gb = jnp.concatenate([zh[:, DA:2 * DA], z_t[:, 3 * DA:4 * DA]], axis=0)
        u = xb * _sigmoid(gb)
        conv = _conv31(u, cw_ref, ts) + cb_ref[...]
        yb2_ref[...] = conv
        nb_, _, _ = _ln_fwd(conv, lgb_ref[...], lbb_ref[...])
        y_ref[:, DA:2 * DA] = (nb_ * _sigmoid(nb_)).astype(BF16)

    row = lambda i: (0, 0)
    return pl.pallas_call(
        body, grid=(s // ts,), name=name,
        in_specs=[pl.BlockSpec((ts, 4 * DA), lambda i: (i, 0)),
                  pl.BlockSpec((HALO_B, 2 * DA), lambda i: (jnp.maximum(i * hb - 1, 0), 1)),
                  pl.BlockSpec((1, DA), row), pl.BlockSpec((1, DA), row),
                  pl.BlockSpec((HEADS, CHUNK, CHUNK), lambda i: (0, 0, 0)),
                  pl.BlockSpec((HEADS, CHUNK, 1), lambda i: (0, 0, 0)),
                  pl.BlockSpec((BCONV, DA), row), pl.BlockSpec((1, DA), row),
                  pl.BlockSpec((1, DA), row), pl.BlockSpec((1, DA), row)],
        out_specs=[pl.BlockSpec((ts, 2 * DA), lambda i: (i, 0)), pl.BlockSpec((ts, DA), lambda i: (i, 0))],
        out_shape=[jax.ShapeDtypeStruct((s, 2 * DA), BF16), jax.ShapeDtypeStruct((s, DA), F32)],
        compiler_params=_params("parallel"),
    )(z, z, lga, lba, wsm, bs_col, cwb, cbb, lgb, lbb)


def _c_fwd(zc, cw, name):
    s = zc.shape[0]
    ts = _tile(s, 512)
    hb = ts // HALO

    def body(z_ref, ch_ref, xh_ref, cw_ref, r_ref):
        i = pl.program_id(0)
        z_t = z_ref[...].astype(F32)
        ph = jnp.where(i > 0, ch_ref[...].astype(F32) * xh_ref[...].astype(F32), 0.0)
        pe = jnp.concatenate([ph, z_t[:, D:2 * D] * z_t[:, 2 * D:3 * D]], axis=0)
        q, _ = _conv3(pe, [cw_ref[k:k + 1, :] for k in range(3)], HALO)
        r_ref[...] = (z_t[:, 0:D] * q).astype(BF16)

    halo = lambda col: pl.BlockSpec((HALO, D), lambda i: (jnp.maximum(i * hb - 1, 0), col))
    return pl.pallas_call(
        body, grid=(s // ts,), name=name,
        in_specs=[pl.BlockSpec((ts, 3 * D), lambda i: (i, 0)), halo(1), halo(2),
                  pl.BlockSpec((3, D), lambda i: (0, 0))],
        out_specs=pl.BlockSpec((ts, D), lambda i: (i, 0)),
        out_shape=jax.ShapeDtypeStruct((s, D), BF16),
        compiler_params=_params("parallel"),
    )(zc, zc, zc, cw)


def _ffn_fwd(h, xres, wup, fcw, wdn, gnext, name):
    s = h.shape[0]
    ts = _tile(s, 512)
    hb = ts // HALO

    def body(h_ref, hh_ref, w_ref, cw_ref, wd_ref, x_ref, *rest):
        if gnext is not None:
            gn_ref, up_ref, upc_ref, xo_ref, hn_ref = rest
        else:
            up_ref, upc_ref, xo_ref = rest
        i = pl.program_id(0)
        m = pl.program_id(1)
        @pl.when(m == 0)
        def _():
            xo_ref[...] = x_ref[...]

        halo = jnp.where(i > 0, hh_ref[...], jnp.zeros_like(hh_ref[...]))
        hx = jnp.concatenate([halo, h_ref[...]], axis=0)
        acts = []
        for gv in range(2):
            up = jnp.dot(hx, w_ref[gv], preferred_element_type=F32)
            up_ref[gv] = up[HALO:].astype(BF16)
            upc, _ = _conv3(up, [cw_ref[gv, k:k + 1, :] for k in range(3)], HALO)
            upc_ref[gv] = upc.astype(BF16)
            acts.append(upc)
        a = acts[0] * _sigmoid(acts[0]) * acts[1]
        xo_ref[...] += jnp.dot(a.astype(BF16), wd_ref[...], preferred_element_type=F32)

        if gnext is not None:
            @pl.when(m == NG - 1)
            def _():
                hn_ref[...] = _rms_math(xo_ref[...], gn_ref[...])

    tile = pl.BlockSpec((ts, D), lambda i, m: (i, 0))
    nxt = gnext is not None
    return pl.pallas_call(
        body, grid=(s // ts, NG), name=name,
        in_specs=[tile,
                  pl.BlockSpec((HALO, D), lambda i, m: (jnp.maximum(i * hb - 1, 0), 0)),
                  pl.BlockSpec((2, None, D, FB), lambda i, m: (0, m, 0, 0)),
                  pl.BlockSpec((2, None, 3, FB), lambda i, m: (0, m, 0, 0)),
                  pl.BlockSpec((FB, D), lambda i, m: (m, 0)),
                  tile] + ([pl.BlockSpec((1, D), lambda i, m: (0, 0))] if nxt else []),
        out_specs=[pl.BlockSpec((None, 2, ts, FB), lambda i, m: (m, 0, i, 0)),
                   pl.BlockSpec((None, 2, ts, FB), lambda i, m: (m, 0, i, 0)),
                   tile] + ([tile] if nxt else []),
        out_shape=[jax.ShapeDtypeStruct((NG, 2, s, FB), BF16), jax.ShapeDtypeStruct((NG, 2, s, FB), BF16),
                   jax.ShapeDtypeStruct((s, D), F32)] + ([jax.ShapeDtypeStruct((s, D), BF16)] if nxt else []),
        compiler_params=_params("arbitrary", "arbitrary"),
    )(h, h, wup, fcw, wdn, xres, *([gnext] if nxt else []))


def _final(x, tgt, g, name):
    s = x.shape[0]
    ts = _tile(s, 512)

    def body(x_ref, t_ref, g_ref, dx_ref, dg_ref, loss_ref):
        i = pl.program_id(0)
        xv = x_ref[...]
        gv = g_ref[...]
        r = lax.rsqrt(jnp.mean(xv * xv, axis=-1, keepdims=True) + EPS)
        xhat = xv * r
        e = xhat * gv - t_ref[...]
        part = 0.5 * jnp.sum(jnp.mean(e * e, axis=-1, keepdims=True), axis=0, keepdims=True)
        dy = e * (1.0 / D)
        dgp = jnp.sum(dy * xhat, axis=0, keepdims=True)
        u = dy * gv
        dx_ref[...] = r * (u - xhat * jnp.mean(u * xhat, axis=-1, keepdims=True))

        @pl.when(i == 0)
        def _():
            dg_ref[...] = dgp
            loss_ref[...] = jnp.broadcast_to(part, (1, 128))

        @pl.when(i > 0)
        def _():
            dg_ref[...] += dgp
            loss_ref[...] += jnp.broadcast_to(part, (1, 128))

    return pl.pallas_call(
        body, grid=(s // ts,), name=name,
        in_specs=[pl.BlockSpec((ts, D), lambda i: (i, 0)), pl.BlockSpec((ts, D), lambda i: (i, 0)),
                  pl.BlockSpec((1, D), lambda i: (0, 0))],
        out_specs=[pl.BlockSpec((ts, D), lambda i: (i, 0)), pl.BlockSpec((1, D), lambda i: (0, 0)),
                   pl.BlockSpec((1, 128), lambda i: (0, 0))],
        out_shape=[jax.ShapeDtypeStruct((s, D), F32), jax.ShapeDtypeStruct((1, D), F32),
                   jax.ShapeDtypeStruct((1, 128), F32)],
        compiler_params=_params("arbitrary"),
    )(x, tgt, g)


def _ffn_bwd(df, up, upc, wup, fcw, wdn, xin, g, name):
    s = df.shape[0]
    ts = _tile(s, 512)
    nt = s // ts

    def body(df_ref, up_ref, upc_ref, w_ref, cw_ref, wd_ref, x_ref, g_ref,
             a_ref, dup_ref, dx_ref, dg_ref, dcw_ref, carry, acc, tacc):
        i = pl.program_id(0)
        m = pl.program_id(1)
        first = i == 0
        @pl.when(first)
        def _():
            carry[m] = jnp.zeros((2, 8, FB), F32)
            dcw_ref[m] = jnp.zeros((2, 3, FB), F32)

        @pl.when(m == 0)
        def _():
            acc[...] = jnp.zeros((ts, D), F32)

        cws = [[cw_ref[gv, k:k + 1, :] for k in range(3)] for gv in range(2)]
        part = ts // NPART
        das = [lax.dot_general(df_ref[p * part:(p + 1) * part, :].astype(BF16), wd_ref[...], NT_DIMS,
                               preferred_element_type=F32) for p in range(NPART)]

        tacc[...] = jnp.zeros((2, 3, 8, FB), F32)
        nxt = [carry[m, 0], carry[m, 1]]
        for r in reversed(range(ts // RC)):
            rs = slice(r * RC, (r + 1) * RC)
            gate = upc_ref[0, rs, :].astype(F32)
            val = upc_ref[1, rs, :].astype(F32)
            sg = _sigmoid(gate)
            sl = gate * sg
            a_ref[rs, :] = (sl * val).astype(BF16)
            da_c = das[(r * RC) // part][(r * RC) % part:(r * RC) % part + RC]
            dcs = [da_c * val * _dsilu(gate, sg), da_c * sl]
            for gv in range(2):
                dc = dcs[gv]
                dce = jnp.concatenate([dc, nxt[gv]], axis=0)
                d1 = pltpu.roll(dce, RC + 7, 0)[:RC]
                d2 = pltpu.roll(dce, RC + 6, 0)[:RC]
                du = cws[gv][2] * dc + cws[gv][1] * d1 + cws[gv][0] * d2
                dup_ref[gv, rs, :] = du.astype(BF16)
                x0 = up_ref[gv, rs, :].astype(F32)
                for k, dk in enumerate((d2, d1, dc)):
                    p = x0 * dk
                    tacc[gv, k] += p[0:8] + p[8:16]
                nxt[gv] = dc[0:8]
            if (r * RC) % part == 0:
                ps = slice(r * RC, r * RC + part)
                acc[ps, :] += (
                    lax.dot_general(dup_ref[0, ps, :], w_ref[0], NT_DIMS, preferred_element_type=F32)
                    + lax.dot_general(dup_ref[1, ps, :], w_ref[1], NT_DIMS, preferred_element_type=F32))
        for gv in range(2):
            carry[m, gv] = nxt[gv]
            for k in range(3):
                dcw_ref[m, gv, k:k + 1, :] += jnp.sum(tacc[gv, k], axis=0, keepdims=True)

        @pl.when(m == NG - 1)
        def _():
            dx, dgp = _rms_bwd_math(acc[...], x_ref[...], g_ref[...])
            dx_ref[...] = df_ref[...] + dx

            @pl.when(first)
            def _():
                dg_ref[...] = dgp

            @pl.when(jnp.logical_not(first))
            def _():
                dg_ref[...] += dgp

    rev = lambda i: nt - 1 - i
    return pl.pallas_call(
        body, grid=(nt, NG), name=name,
        in_specs=[pl.BlockSpec((ts, D), lambda i, m: (rev(i), 0)),
                  pl.BlockSpec((None, 2, ts, FB), lambda i, m: (m, 0, rev(i), 0)),
                  pl.BlockSpec((None, 2, ts, FB), lambda i, m: (m, 0, rev(i), 0)),
                  pl.BlockSpec((2, None, D, FB), lambda i, m: (0, m, 0, 0)),
                  pl.BlockSpec((2, None, 3, FB), lambda i, m: (0, m, 0, 0)),
                  pl.BlockSpec((FB, D), lambda i, m: (m, 0)),
                  pl.BlockSpec((ts, D), lambda i, m: (rev(i), 0)),
                  pl.BlockSpec((1, D), lambda i, m: (0, 0))],
        out_specs=[pl.BlockSpec((None, ts, FB), lambda i, m: (m, rev(i), 0)),
                   pl.BlockSpec((None, 2, ts, FB), lambda i, m: (m, 0, rev(i), 0)),
                   pl.BlockSpec((ts, D), lambda i, m: (rev(i), 0)),
                   pl.BlockSpec((1, D), lambda i, m: (0, 0)),
                   pl.BlockSpec((NG, 2, 3, FB), lambda i, m: (0, 0, 0, 0))],
        out_shape=[jax.ShapeDtypeStruct((NG, s, FB), BF16), jax.ShapeDtypeStruct((NG, 2, s, FB), BF16),
                   jax.ShapeDtypeStruct((s, D), F32), jax.ShapeDtypeStruct((1, D), F32),
                   jax.ShapeDtypeStruct((NG, 2, 3, FB), F32)],
        scratch_shapes=[pltpu.VMEM((NG, 2, 8, FB), F32), pltpu.VMEM((ts, D), F32),
                        pltpu.VMEM((2, 3, 8, FB), F32)],
        compiler_params=_params("arbitrary", "arbitrary"),
    )(df, up, upc, wup, fcw, wdn, xin, g)


def _mm_nt(dy, w, name):
    s = dy.shape[0]
    ts = _tile(s, 512)

    def body(dy_ref, w_ref, o_ref):
        o_ref[...] = lax.dot_general(dy_ref[...].astype(BF16), w_ref[...], NT_DIMS,
                                     preferred_element_type=F32).astype(BF16)

    return pl.pallas_call(
        body, grid=(s // ts,), name=name,
        in_specs=[pl.BlockSpec((ts, D), lambda i: (i, 0)), pl.BlockSpec((D, D), lambda i: (0, 0))],
        out_specs=pl.BlockSpec((ts, D), lambda i: (i, 0)),
        out_shape=jax.ShapeDtypeStruct((s, D), BF16),
        compiler_params=_params("parallel"),
    )(dy, w)


def _mm_nt_rms(dy, wblk, x, g, dres, name):
    s = dy.shape[0]
    nb, _, bn = wblk.shape
    pair = _pair(bn)
    ts = _tile(s, 512)

    def body(dy_ref, w_ref, x_ref, g_ref, dr_ref, dx_ref, dg_ref):
        i = pl.program_id(0)
        acc = jnp.zeros((ts, D), F32)
        for b in range(0, nb, pair):
            acc = acc + lax.dot_general(dy_ref[:, b * bn:(b + pair) * bn], _cols(w_ref, b, pair), NT_DIMS,
                                        preferred_element_type=F32)
        dx, dgp = _rms_bwd_math(acc, x_ref[...], g_ref[...])
        dx_ref[...] = dr_ref[...] + dx

        @pl.when(i == 0)
        def _():
            dg_ref[...] = dgp

        @pl.when(i > 0)
        def _():
            dg_ref[...] += dgp

    return pl.pallas_call(
        body, grid=(s // ts,), name=name,
        in_specs=[pl.BlockSpec((ts, nb * bn), lambda i: (i, 0)), pl.BlockSpec((nb, D, bn), lambda i: (0, 0, 0)),
                  pl.BlockSpec((ts, D), lambda i: (i, 0)), pl.BlockSpec((1, D), lambda i: (0, 0)),
                  pl.BlockSpec((ts, D), lambda i: (i, 0))],
        out_specs=[pl.BlockSpec((ts, D), lambda i: (i, 0)), pl.BlockSpec((1, D), lambda i: (0, 0))],
        out_shape=[jax.ShapeDtypeStruct((s, D), F32), jax.ShapeDtypeStruct((1, D), F32)],
        compiler_params=_params("arbitrary"),
    )(dy, wblk, x, g, dres)


def _c_bwd(dr, zc, cw, name):
    s = dr.shape[0]
    ts = _tile(s, 512)
    nt = s // ts
    hb = ts // HALO

    def body(dr_ref, drf_ref, z_ref, ch_ref, xh_ref, bf_ref, cw_ref, dz_ref, dcw_ref):
        i = pl.program_id(0)
        cwv = [cw_ref[k:k + 1, :] for k in range(3)]
        z_t = z_ref[...].astype(F32)
        bg, cg, xv = z_t[:, 0:D], z_t[:, D:2 * D], z_t[:, 2 * D:3 * D]
        ph = jnp.where(i > 0, ch_ref[...].astype(F32) * xh_ref[...].astype(F32), 0.0)
        pe = jnp.concatenate([ph, cg * xv], axis=0)
        q, taps = _conv3(pe, cwv, HALO)
        drv = dr_ref[...].astype(F32)
        dq = drv * bg
        dqf = jnp.where(i < nt - 1, drf_ref[...].astype(F32) * bf_ref[...].astype(F32), 0.0)
        dp = _conv3_bwd_in(jnp.concatenate([dq, dqf], axis=0), cwv, ts)
        dz_ref[:, 0:D] = (drv * q).astype(BF16)
        dz_ref[:, D:2 * D] = (dp * xv).astype(BF16)
        dz_ref[:, 2 * D:3 * D] = (dp * cg).astype(BF16)
        rows = _conv3_bwd_w(dq, taps)

        @pl.when(i == 0)
        def _():
            for k in range(3):
                dcw_ref[k:k + 1, :] = rows[k]

        @pl.when(i > 0)
        def _():
            for k in range(3):
                dcw_ref[k:k + 1, :] += rows[k]

    past = lambda col: pl.BlockSpec((HALO, D), lambda i: (jnp.maximum(i * hb - 1, 0), col))
    nxt = lambda i: jnp.minimum((i + 1) * hb, s // HALO - 1)
    return pl.pallas_call(
        body, grid=(nt,), name=name,
        in_specs=[pl.BlockSpec((ts, D), lambda i: (i, 0)),
                  pl.BlockSpec((HALO, D), lambda i: (nxt(i), 0)),
                  pl.BlockSpec((ts, 3 * D), lambda i: (i, 0)), past(1), past(2),
                  pl.BlockSpec((HALO, D), lambda i: (nxt(i), 0)),
                  pl.BlockSpec((3, D), lambda i: (0, 0))],
        out_specs=[pl.BlockSpec((ts, 3 * D), lambda i: (i, 0)), pl.BlockSpec((3, D), lambda i: (0, 0))],
        out_shape=[jax.ShapeDtypeStruct((s, 3 * D), BF16), jax.ShapeDtypeStruct((3, D), F32)],
        compiler_params=_params("arbitrary"),
    )(dr, dr, zc, zc, zc, zc, cw)


G512_ROWS = 40


def _ab_bwd(dy, z, yb2, lga, lba, wsm, bs_col, cwb, lgb, lbb, name):
    s = z.shape[0]
    ts = _tile(s, 256)
    nt = s // ts
    hb = ts // HALO_B
    nch = ts // CHUNK
    tri = None

    def body(z_ref, zh_ref, dy_ref, dyf_ref, yb2_ref, yb2f_ref, lga_ref, lba_ref, ws_ref, bs_ref,
             cw_ref, lgb_ref, lbb_ref, dz_ref, g512_ref, dws_ref, dbs_ref, dvn_ref):
        i = pl.program_id(0)
        last = i == nt - 1

        @pl.when(i == 0)
        def _():
            g512_ref[...] = jnp.zeros((G512_ROWS, DA), F32)
            dws_ref[...] = jnp.zeros((HEADS, CHUNK, CHUNK), F32)
            dbs_ref[...] = jnp.zeros((HEADS, CHUNK, 1), F32)

        def add_row(k, v):
            g512_ref[k:k + 1, :] += v

        z_t = z_ref[...].astype(F32)
        dy_t = dy_ref[...].astype(F32)
        ua, va = z_t[:, 0:DA], z_t[:, DA:2 * DA]
        gu = _gelu(ua)
        gv = _gelu(va)
        lga_v = lga_ref[...]
        vn, xhat_a, rstd_a = _ln_fwd(gv, lga_v, lba_ref[...])
        vnb = vn.astype(BF16)
        causal = (lax.broadcasted_iota(jnp.int32, (CHUNK, CHUNK), 0)
                  >= lax.broadcasted_iota(jnp.int32, (CHUNK, CHUNK), 1)).astype(F32)
        for c in range(nch):
            for h in range(HEADS):
                rs = slice(c * CHUNK, (c + 1) * CHUNK)
                cs = slice(h * CHUNK, (h + 1) * CHUNK)
                vblk = vnb[rs, cs]
                mixed = jnp.dot(ws_ref[h], vblk, preferred_element_type=F32) + bs_ref[h]
                dyb_ = dy_t[rs, cs]
                dmix = dyb_ * gu[rs, cs]
                dmb = dmix.astype(BF16)
                dz_ref[rs, cs] = (dyb_ * mixed * _dgelu(ua[rs, cs])).astype(BF16)
                dvn_ref[rs, cs] = lax.dot_general(ws_ref[h], dmb, TN_DIMS, preferred_element_type=F32)
                dws_ref[h] += causal * lax.dot_general(dmb, vblk, NT_DIMS, preferred_element_type=F32)
                dbs_ref[h] += jnp.sum(dmix, axis=1, keepdims=True)
        dvn = dvn_ref[...]
        add_row(0, jnp.sum(dvn * xhat_a, axis=0, keepdims=True))
        add_row(1, jnp.sum(dvn, axis=0, keepdims=True))
        dgv = _ln_bwd(dvn, xhat_a, rstd_a, lga_v)
        dz_ref[:, DA:2 * DA] = (dgv * _dgelu(va)).astype(BF16)
        lgb_v = lgb_ref[...]
        dyb_e = jnp.concatenate(
            [dy_t[:, DA:2 * DA], jnp.where(last, 0.0, dyf_ref[...].astype(F32))], axis=0)
        yb2_e = jnp.concatenate([yb2_ref[...], jnp.where(last, 0.0, yb2f_ref[...])], axis=0)
        n_e, xhat_b, rstd_b = _ln_fwd(yb2_e, lgb_v, lbb_ref[...])
        sgn = _sigmoid(n_e)
        dn = dyb_e * _dsilu(n_e, sgn)
        dy2 = _ln_bwd(dn, xhat_b, rstd_b, lgb_v)
        add_row(2, jnp.sum(dy2[:ts], axis=0, keepdims=True))
        add_row(3, jnp.sum(dn[:ts] * xhat_b[:ts], axis=0, keepdims=True))
        add_row(4, jnp.sum(dn[:ts], axis=0, keepdims=True))
        zh = jnp.where(i > 0, zh_ref[...], jnp.zeros_like(zh_ref[...])).astype(F32)
        xb_t, gb_t = z_t[:, 2 * DA:3 * DA], z_t[:, 3 * DA:4 * DA]
        sgb = _sigmoid(gb_t)
        ue = jnp.concatenate([zh[:, 0:DA] * _sigmoid(zh[:, DA:2 * DA]), xb_t * sgb], axis=0)
        dy2_t = dy2[:ts]
        n_e_rows = ts + HALO_B
        du = jnp.zeros((ts, DA), F32)
        for r in range(8):
            fwd_roll = ue if r == 0 else pltpu.roll(ue, r, 0)
            bwd_roll = dy2 if r == 0 else pltpu.roll(dy2, n_e_rows - r, 0)
            for q in range(4):
                sh = 8 * q + r
                if sh >= BCONV:
                    continue
                k = BCONV - 1 - sh
                du = du + cw_ref[k:k + 1, :] * bwd_roll[8 * q:8 * q + ts]
                add_row(8 + k, jnp.sum(dy2_t * fwd_roll[HALO_B - 8 * q:HALO_B - 8 * q + ts],
                                       axis=0, keepdims=True))
        dz_ref[:, 2 * DA:3 * DA] = (du * sgb).astype(BF16)
        dz_ref[:, 3 * DA:4 * DA] = (du * xb_t * sgb * (1.0 - sgb)).astype(BF16)

    row = lambda i: (0, 0)
    nxt = lambda i: jnp.minimum((i + 1) * hb, s // HALO_B - 1)
    return pl.pallas_call(
        body, grid=(nt,), name=name,
        in_specs=[pl.BlockSpec((ts, 4 * DA), lambda i: (i, 0)),
                  pl.BlockSpec((HALO_B, 2 * DA), lambda i: (jnp.maximum(i * hb - 1, 0), 1)),
                  pl.BlockSpec((ts, 2 * DA), lambda i: (i, 0)),
                  pl.BlockSpec((HALO_B, DA), lambda i: (nxt(i), 1)),
                  pl.BlockSpec((ts, DA), lambda i: (i, 0)),
                  pl.BlockSpec((HALO_B, DA), lambda i: (nxt(i), 0)),
                  pl.BlockSpec((1, DA), row), pl.BlockSpec((1, DA), row),
                  pl.BlockSpec((HEADS, CHUNK, CHUNK), lambda i: (0, 0, 0)),
                  pl.BlockSpec((HEADS, CHUNK, 1), lambda i: (0, 0, 0)),
                  pl.BlockSpec((BCONV, DA), row), pl.BlockSpec((1, DA), row), pl.BlockSpec((1, DA), row)],
        out_specs=[pl.BlockSpec((ts, 4 * DA), lambda i: (i, 0)),
                   pl.BlockSpec((G512_ROWS, DA), row),
                   pl.BlockSpec((HEADS, CHUNK, CHUNK), lambda i: (0, 0, 0)),
                   pl.BlockSpec((HEADS, CHUNK, 1), lambda i: (0, 0, 0))],
        out_shape=[jax.ShapeDtypeStruct((s, 4 * DA), BF16), jax.ShapeDtypeStruct((G512_ROWS, DA), F32),
                   jax.ShapeDtypeStruct((HEADS, CHUNK, CHUNK), F32),
                   jax.ShapeDtypeStruct((HEADS, CHUNK, 1), F32)],
        scratch_shapes=[pltpu.VMEM((ts, DA), F32)],
        compiler_params=_params("arbitrary"),
    )(z, z, dy, dy, yb2, yb2, lga, lba, wsm, bs_col, cwb, lgb, lbb)


def _dw_cols(a, dy, nb, bn, name):
    s = a.shape[0]
    tm = _tile(s, 2048)
    nt = s // tm
    cpb = 4

    def body(a_ref, dy_ref, o_ref, acc):
        t = pl.program_id(1)
        p = lax.dot_general(a_ref[...], dy_ref[...], TN_DIMS, preferred_element_type=F32)

        @pl.when(t == 0)
        def _():
            for q in range(cpb):
                acc[q] = p[:, q * bn:(q + 1) * bn]

        @pl.when(t > 0)
        def _():
            for q in range(cpb):
                acc[q] += p[:, q * bn:(q + 1) * bn]

        @pl.when(t == nt - 1)
        def _():
            o_ref[...] = acc[...].astype(BF16)

    return pl.pallas_call(
        body, grid=(nb // cpb, nt), name=name,
        in_specs=[pl.BlockSpec((tm, D), lambda j, t: (t, 0)), pl.BlockSpec((tm, cpb * bn), lambda j, t: (t, j))],
        out_specs=pl.BlockSpec((cpb, D, bn), lambda j, t: (j, 0, 0)),
        out_shape=jax.ShapeDtypeStruct((nb, D, bn), BF16),
        scratch_shapes=[pltpu.VMEM((cpb, D, bn), F32)],
        compiler_params=_params("arbitrary", "arbitrary"),
    )(a, dy)


def _dw_rows(a, dy, name):
    s = a.shape[0]
    tm = _tile(s, 2048)
    nt = s // tm
    rb = 512

    def body(a_ref, dy_ref, o_ref, acc):
        t = pl.program_id(1)
        p = lax.dot_general(a_ref[...], dy_ref[...].astype(BF16), TN_DIMS, preferred_element_type=F32)

        @pl.when(t == 0)
        def _():
            acc[...] = p

        @pl.when(t > 0)
        def _():
            acc[...] += p

        @pl.when(t == nt - 1)
        def _():
            o_ref[...] = acc[...].astype(BF16)

    return pl.pallas_call(
        body, grid=(D // rb, nt), name=name,
        in_specs=[pl.BlockSpec((tm, rb), lambda j, t: (t, j)), pl.BlockSpec((tm, D), lambda j, t: (t, 0))],
        out_specs=pl.BlockSpec((rb, D), lambda j, t: (j, 0)),
        out_shape=jax.ShapeDtypeStruct((D, D), BF16),
        scratch_shapes=[pltpu.VMEM((rb, D), F32)],
        compiler_params=_params("arbitrary", "arbitrary"),
    )(a, dy)


def _dw_up(h, dup, name):
    s = h.shape[0]
    tm = _tile(s, 2048)
    nt = s // tm

    def body(h_ref, d_ref, o_ref, acc):
        t = pl.program_id(1)
        p = lax.dot_general(d_ref[...], h_ref[...], TN_DIMS, preferred_element_type=F32)

        @pl.when(t == 0)
        def _():
            acc[...] = p

        @pl.when(t > 0)
        def _():
            acc[...] += p

        @pl.when(t == nt - 1)
        def _():
            o_ref[...] = acc[...].astype(BF16)

    return pl.pallas_call(
        body, grid=(NDEV, nt), name=name,
        in_specs=[pl.BlockSpec((tm, D), lambda b, t: (t, 0)),
                  pl.BlockSpec((None, None, tm, FB), lambda b, t: (b % NG, b // NG, t, 0))],
        out_specs=pl.BlockSpec((None, FB, D), lambda b, t: (b, 0, 0)),
        out_shape=jax.ShapeDtypeStruct((NDEV, FB, D), BF16),
        scratch_shapes=[pltpu.VMEM((FB, D), F32)],
        compiler_params=_params("arbitrary", "arbitrary"),
    )(h, dup)


def _dw_dn(a, df, name):
    s = df.shape[0]
    tm = _tile(s, 2048)
    nt = s // tm

    def body(a_ref, d_ref, o_ref, acc):
        t = pl.program_id(1)
        p = lax.dot_general(a_ref[...], d_ref[...].astype(BF16), TN_DIMS, preferred_element_type=F32)

        @pl.when(t == 0)
        def _():
            acc[...] = p

        @pl.when(t > 0)
        def _():
            acc[...] += p

        @pl.when(t == nt - 1)
        def _():
            o_ref[...] = acc[...].astype(BF16)

    return pl.pallas_call(
        body, grid=(NG, nt), name=name,
        in_specs=[pl.BlockSpec((None, tm, FB), lambda m, t: (m, t, 0)), pl.BlockSpec((tm, D), lambda m, t: (t, 0))],
        out_specs=pl.BlockSpec((FB, D), lambda m, t: (m, 0)),
        out_shape=jax.ShapeDtypeStruct((DFF, D), BF16),
        scratch_shapes=[pltpu.VMEM((FB, D), F32)],
        compiler_params=_params("arbitrary", "arbitrary"),
    )(a, df)


def _place():
    x, y, c = lax.axis_index("x"), lax.axis_index("y"), lax.axis_index("c")
    chips = [(1 - x, y), (x, 1 - y), (1 - x, 1 - y)]
    return x, y, c, chips


def _zone(shard, dev):
    return lax.dynamic_update_slice(lax.empty((NDEV,) + shard.shape, shard.dtype), shard[None],
                                    (dev,) + (0,) * shard.ndim)


def _all_gather(zones, name):
    nt = len(zones)

    def body(*refs):
        dsts = refs[nt:2 * nt]
        send_sems, recv_sems = refs[2 * nt:]
        x, y, c, chips = _place()
        me, sib = (x, y, c), (x, y, 1 - c)

        def blk(t, p):
            return dsts[t].at[4 * p[0] + 2 * p[1] + p[2]]

        def copy(t, k, block, to):
            return pltpu.make_async_remote_copy(
                src_ref=blk(t, block), dst_ref=blk(t, block),
                send_sem=send_sems.at[t, k], recv_sem=recv_sems.at[t, k],
                device_id=to, device_id_type=MESH)

        first = []
        for t in range(nt):
            first.append(copy(t, 0, me, sib))
            first += [copy(t, 1 + j, me, (*chip, c)) for j, chip in enumerate(chips)]
        for cp in first:
            cp.start()
        passed = []
        for j, chip in enumerate(chips):
            for t in range(nt):
                copy(t, 1 + j, (*chip, c), me).wait_recv()
                cp = copy(t, 4 + j, (*chip, c), sib)
                cp.start()
                passed.append(cp)
        for t in range(nt):
            copy(t, 0, sib, me).wait_recv()
            for j, chip in enumerate(chips):
                copy(t, 4 + j, (*chip, 1 - c), me).wait_recv()
        for cp in first + passed:
            cp.wait_send()

    return pl.pallas_call(
        body, name=name,
        in_specs=[ANY] * nt, out_specs=[ANY] * nt,
        out_shape=[jax.ShapeDtypeStruct(a.shape, a.dtype) for a in zones],
        input_output_aliases={t: t for t in range(nt)},
        scratch_shapes=[pltpu.SemaphoreType.DMA((nt, 7)), pltpu.SemaphoreType.DMA((nt, 7))],
        compiler_params=pltpu.CompilerParams(has_side_effects=True),
    )(*zones)


HBM_SPEC = pl.BlockSpec(memory_space=pltpu.HBM)
SEM_SPEC = pl.BlockSpec(memory_space=pltpu.SEMAPHORE)
DATAFLOW = pltpu.SideEffectType.DATAFLOW_SIDE_EFFECTING


def _hbm(a):
    return pltpu.with_memory_space_constraint(a, pltpu.HBM)


def _hbm_like(arrs):
    return [pltpu.HBM(a.shape, a.dtype) for a in arrs]


def _ag_start(srcs, lands, after, name):
    n = len(srcs)
    ns = 8 * n

    def body(*refs):
        src, land = refs[:n], refs[n:2 * n]
        sems = refs[2 * n + 1:2 * n + 1 + ns]
        token = refs[-1]
        x, y, c, chips = _place()
        peers = [(x, y, 1 - c)] + [(*chip, c) for chip in chips]
        for t in range(n):
            for k, to in enumerate(peers):
                pltpu.make_async_remote_copy(
                    src_ref=src[t], dst_ref=land[t].at[4 * x + 2 * y + c],
                    send_sem=sems[2 * (4 * t + k)], recv_sem=sems[2 * (4 * t + k) + 1],
                    device_id=to, device_id_type=MESH).start()
        token[...] = jnp.zeros_like(token)

    res = pl.pallas_call(
        body, name=name,
        in_specs=[HBM_SPEC] * (2 * n) + [ANY],
        out_specs=[SEM_SPEC] * ns + [HBM_SPEC] * (2 * n) + [pl.BlockSpec(memory_space=pltpu.VMEM)],
        out_shape=[pltpu.SemaphoreType.DMA(())] * ns + _hbm_like(srcs) + _hbm_like(lands)
        + [jax.ShapeDtypeStruct((8, 128), F32)],
        input_output_aliases={i: ns + i for i in range(2 * n)},
        compiler_params=pltpu.CompilerParams(has_side_effects=DATAFLOW),
    )(*[_hbm(a) for a in srcs], *[_hbm(a) for a in lands], after)
    sems = [[(res[2 * (4 * t + k)], res[2 * (4 * t + k) + 1]) for k in range(4)] for t in range(n)]
    return sems, res[ns:ns + n], res[ns + n:ns + 2 * n], res[-1]


def _ag_forward(srcs, lands, sems1, after, name):
    n = len(srcs)
    flat1 = [s for t in range(n) for k in range(1, 4) for s in sems1[t][k]]
    n1 = len(flat1)

    def body(*refs):
        src, land = refs[:n], refs[n:2 * n]
        s1 = refs[2 * n:2 * n + n1]
        s2 = refs[2 * n + n1 + 1:2 * n + n1 + 1 + 6 * n]
        x, y, c, chips = _place()
        for j, (cx, cy) in enumerate(chips):
            for t in range(n):
                blk = land[t].at[4 * cx + 2 * cy + c]
                pltpu.make_async_remote_copy(
                    src_ref=src[t], dst_ref=blk, send_sem=s1[2 * (3 * t + j)], recv_sem=s1[2 * (3 * t + j) + 1],
                    device_id=(cx, cy, c), device_id_type=MESH).wait_recv()
                pltpu.make_async_remote_copy(
                    src_ref=blk, dst_ref=blk, send_sem=s2[2 * (3 * t + j)], recv_sem=s2[2 * (3 * t + j) + 1],
                    device_id=(x, y, 1 - c), device_id_type=MESH).start()

    res = pl.pallas_call(
        body, name=name,
        in_specs=[HBM_SPEC] * (2 * n) + [SEM_SPEC] * n1 + [ANY],
        out_specs=[SEM_SPEC] * (6 * n) + [HBM_SPEC] * n,
        out_shape=[pltpu.SemaphoreType.DMA(())] * (6 * n) + _hbm_like(lands),
        input_output_aliases={n + i: 6 * n + i for i in range(n)},
        compiler_params=pltpu.CompilerParams(has_side_effects=DATAFLOW),
    )(*srcs, *lands, *flat1, after)
    sems2 = [[(res[2 * (3 * t + j)], res[2 * (3 * t + j) + 1]) for j in range(3)] for t in range(n)]
    return sems2, res[6 * n:]


def _ag_finish(srcs, lands, sems1, sems2, after, name):
    n = len(srcs)
    flat1 = [s for t in range(n) for k in range(4) for s in sems1[t][k]]
    flat2 = [s for t in range(n) for j in range(3) for s in sems2[t][j]]
    n1, n2 = len(flat1), len(flat2)

    def body(*refs):
        src, land = refs[:n], refs[n:2 * n]
        s1 = refs[2 * n:2 * n + n1]
        s2 = refs[2 * n + n1:2 * n + n1 + n2]
        x, y, c, chips = _place()
        sib = (x, y, 1 - c)
        for t in range(n):
            own = land[t].at[4 * x + 2 * y + 1 - c]
            pltpu.make_async_remote_copy(
                src_ref=src[t], dst_ref=own, send_sem=s1[8 * t], recv_sem=s1[8 * t + 1],
                device_id=sib, device_id_type=MESH).wait_recv()
            for k in range(4):
                pltpu.make_async_remote_copy(
                    src_ref=src[t], dst_ref=own, send_sem=s1[2 * (4 * t + k)], recv_sem=s1[2 * (4 * t + k) + 1],
                    device_id=sib, device_id_type=MESH).wait_send()
            for j, (cx, cy) in enumerate(chips):
                blk = land[t].at[4 * cx + 2 * cy + 1 - c]
                cp = pltpu.make_async_remote_copy(
                    src_ref=blk, dst_ref=blk, send_sem=s2[2 * (3 * t + j)], recv_sem=s2[2 * (3 * t + j) + 1],
                    device_id=sib, device_id_type=MESH)
                cp.wait_send()
                cp.wait_recv()

    return pl.pallas_call(
        body, name=name,
        in_specs=[HBM_SPEC] * (2 * n) + [SEM_SPEC] * (n1 + n2) + [ANY],
        out_specs=[HBM_SPEC] * n,
        out_shape=_hbm_like(lands),
        input_output_aliases={n + i: i for i in range(n)},
        compiler_params=pltpu.CompilerParams(has_side_effects=DATAFLOW),
    )(*srcs, *lands, *flat1, *flat2, after)


def _pair_copies(srcs, dsts, sems):
    x, y, c, _ = _place()
    nt = len(srcs)
    return [pltpu.make_async_remote_copy(
        src_ref=srcs[t].at[2 * j + 1 - c], dst_ref=dsts[t].at[j],
        send_sem=sems[2 * (NCHIP * t + j)], recv_sem=sems[2 * (NCHIP * t + j) + 1],
        device_id=(x, y, 1 - c), device_id_type=MESH) for t in range(nt) for j in range(NCHIP)]


def _pair_start(grads, carry, name):
    nt = len(grads)
    ns = 2 * NCHIP * nt
    zones = [_hbm(lax.empty((NCHIP,) + a.shape[1:], a.dtype)) for a in grads]
    extra = [] if carry is None else [_hbm(carry)]
    ne = len(extra)

    def body(*refs):
        for cp in _pair_copies(refs[:nt], refs[nt:2 * nt], refs[2 * nt + ne:2 * nt + ne + ns]):
            cp.start()

    res = pl.pallas_call(
        body, name=name,
        in_specs=[HBM_SPEC] * (2 * nt + ne),
        out_specs=[SEM_SPEC] * ns + [HBM_SPEC] * (2 * nt + ne),
        out_shape=[pltpu.SemaphoreType.DMA(())] * ns + _hbm_like(grads) + _hbm_like(zones) + _hbm_like(extra),
        input_output_aliases={i: ns + i for i in range(2 * nt + ne)},
        compiler_params=pltpu.CompilerParams(has_side_effects=DATAFLOW),
    )(*[_hbm(a) for a in grads], *zones, *extra)
    handle = (list(res[:ns]), list(res[ns:ns + nt]), list(res[ns + nt:ns + 2 * nt]))
    return handle, (res[ns + 2 * nt] if ne else None)


def _pair_wait(handle, after, name):
    sems, srcs, zones = handle
    nt, ns = len(srcs), len(sems)

    def body(*refs):
        for cp in _pair_copies(refs[:nt], refs[nt:2 * nt], refs[2 * nt:2 * nt + ns]):
            cp.wait_send()
            cp.wait_recv()

    return pl.pallas_call(
        body, name=name,
        in_specs=[HBM_SPEC] * (2 * nt) + [SEM_SPEC] * ns + [ANY],
        out_specs=[HBM_SPEC] * nt,
        out_shape=_hbm_like(zones),
        input_output_aliases={nt + i: i for i in range(nt)},
        compiler_params=pltpu.CompilerParams(has_side_effects=DATAFLOW),
    )(*srcs, *zones, *sems, after)


def _rows_tile(r, row_bytes, cap_bytes):
    best = None
    for tr in range(16, r + 1, 16):
        if r % tr == 0 and tr * row_bytes <= cap_bytes:
            best = tr
    return best if best is not None else r


def _pair_sum(own, got, cidx, name):
    _, _, r, cdim = own.shape
    tr = _rows_tile(r, 2 * cdim, 2 * 1024 * 1024)

    def body(c_ref, a_ref, b_ref, o_ref):
        o_ref[...] = (a_ref[...].astype(F32) + b_ref[...].astype(F32)).astype(BF16)

    return pl.pallas_call(
        body, name=name,
        grid_spec=pltpu.PrefetchScalarGridSpec(
            num_scalar_prefetch=1, grid=(NCHIP, r // tr),
            in_specs=[pl.BlockSpec((None, None, tr, cdim), lambda j, i, c_ref: (j, c_ref[0], i, 0)),
                      pl.BlockSpec((None, tr, cdim), lambda j, i, c_ref: (j, i, 0))],
            out_specs=pl.BlockSpec((None, tr, cdim), lambda j, i, c_ref: (j, i, 0))),
        out_shape=jax.ShapeDtypeStruct((NCHIP, r, cdim), BF16),
        compiler_params=_params("arbitrary", "arbitrary"),
    )(cidx, own, got)


def _chip_copies(srcs, zones, slots, sems):
    x, y, c, chips = _place()
    out = []
    for t, (z, l) in enumerate(slots):
        for k, (cx, cy) in enumerate(chips):
            dst = zones[z].at[k] if l is None else zones[z].at[k, l]
            out.append(pltpu.make_async_remote_copy(
                src_ref=srcs[t].at[2 * cx + cy], dst_ref=dst,
                send_sem=sems[2 * (3 * t + k)], recv_sem=sems[2 * (3 * t + k) + 1],
                device_id=(cx, cy, c), device_id_type=MESH))
    return out


def _chip_start(sums, zones, slots, carry, name):
    nt, nz = len(sums), len(zones)
    ns = 6 * nt
    extra = [] if carry is None else [_hbm(carry)]
    ne = len(extra)

    def body(*refs):
        for cp in _chip_copies(refs[:nt], refs[nt:nt + nz], slots, refs[nt + nz + ne:nt + nz + ne + ns]):
            cp.start()

    res = pl.pallas_call(
        body, name=name,
        in_specs=[HBM_SPEC] * (nt + nz + ne),
        out_specs=[SEM_SPEC] * ns + [HBM_SPEC] * (nt + nz + ne),
        out_shape=[pltpu.SemaphoreType.DMA(())] * ns + _hbm_like(sums) + _hbm_like(zones) + _hbm_like(extra),
        input_output_aliases={i: ns + i for i in range(nt + nz + ne)},
        compiler_params=pltpu.CompilerParams(has_side_effects=DATAFLOW),
    )(*[_hbm(a) for a in sums], *zones, *extra)
    return (list(res[:ns]), list(res[ns:ns + nt]), list(res[ns + nt:ns + nt + nz]),
            (res[ns + nt + nz] if ne else None))


def _chip_wait(started, zones, zone_ids, after, name):
    started = [(sums, [(zone_ids.index(z), l) for z, l in slots], sems) for sums, slots, sems in started]
    nz = len(zones)
    flat_src = [a for sums, _, _ in started for a in sums]
    flat_sem = [s for _, _, sems in started for s in sems]
    n_src, n_sem = len(flat_src), len(flat_sem)

    def body(*refs):
        srcs, zs, sems = refs[:n_src], refs[n_src:n_src + nz], refs[n_src + nz:n_src + nz + n_sem]
        so, se = 0, 0
        for sums, slots, sem_list in started:
            for cp in _chip_copies(srcs[so:so + len(sums)], zs, slots, sems[se:se + len(sem_list)]):
                cp.wait_send()
                cp.wait_recv()
            so += len(sums)
            se += len(sem_list)

    return pl.pallas_call(
        body, name=name,
        in_specs=[HBM_SPEC] * (n_src + nz) + [SEM_SPEC] * n_sem + [ANY],
        out_specs=[HBM_SPEC] * nz,
        out_shape=_hbm_like(zones),
        input_output_aliases={n_src + i: i for i in range(nz)},
        compiler_params=pltpu.CompilerParams(has_side_effects=DATAFLOW),
    )(*flat_src, *zones, *flat_sem, after)


def _small_allreduce(parts, after, name):
    nt = len(parts)

    def body(*refs):
        srcs, outs, bufs = refs[:nt], refs[nt + 1:2 * nt + 1], refs[2 * nt + 1:3 * nt + 1]
        send_sems, recv_sems = refs[3 * nt + 1:]
        x, y, c, _ = _place()
        peers = [(x, y, 1 - c), (1 - x, y, c), (x, 1 - y, c)]
        for t in range(nt):
            outs[t][...] = srcs[t][...]
        for step, peer in enumerate(peers):
            copies = [pltpu.make_async_remote_copy(
                src_ref=outs[t], dst_ref=bufs[t].at[step],
                send_sem=send_sems.at[step, t], recv_sem=recv_sems.at[step, t],
                device_id=peer, device_id_type=MESH) for t in range(nt)]
            for cp in copies:
                cp.start()
            for cp in copies:
                cp.wait()
            for t in range(nt):
                outs[t][...] = outs[t][...] + bufs[t][step]

    vm = pl.BlockSpec(memory_space=pltpu.VMEM)
    return pl.pallas_call(
        body, name=name,
        in_specs=[vm] * nt + [ANY], out_specs=[vm] * nt,
        out_shape=[jax.ShapeDtypeStruct(a.shape, F32) for a in parts],
        scratch_shapes=[pltpu.VMEM((3,) + a.shape, F32) for a in parts]
        + [pltpu.SemaphoreType.DMA((3, nt)), pltpu.SemaphoreType.DMA((3, nt))],
        compiler_params=pltpu.CompilerParams(has_side_effects=True, vmem_limit_bytes=VMEM_LIMIT),
    )(*parts, after)


def _adam_math(w, g, m, v):
    m2 = ADAM_B1 * m + (1.0 - ADAM_B1) * g
    v2 = ADAM_B2 * v + (1.0 - ADAM_B2) * (g * g)
    m_hat = m2 / (1.0 - ADAM_B1 ** ADAM_STEP)
    v_hat = v2 / (1.0 - ADAM_B2 ** ADAM_STEP)
    delta = -ADAM_LR * (m_hat / (jnp.sqrt(v_hat) + ADAM_EPS) + ADAM_WD * w)
    return delta, m2, v2


def _adam_big(w, m, v, parts, mine, chip, name):
    nl, r, cdim = w.shape
    tr = _rows_tile(r, 4 * cdim, 3 * 512 * 1024)

    def body(c_ref, w_ref, m_ref, v_ref, p_ref, *rest):
        mine_refs, (g_ref, d_ref, mo_ref, vo_ref) = rest[:nl], rest[nl:]
        own = mine_refs[0][...]
        for l in range(1, nl):
            own = jnp.where(pl.program_id(0) == l, mine_refs[l][...], own)
        g = ((p_ref[0].astype(F32) + p_ref[1].astype(F32)) + p_ref[2].astype(F32)) + own.astype(F32)
        delta, m2, v2 = _adam_math(w_ref[...], g, m_ref[...], v_ref[...])
        g_ref[...] = g
        d_ref[...] = delta
        mo_ref[...] = m2
        vo_ref[...] = v2

    spec = pl.BlockSpec((None, tr, cdim), lambda l, i, c_ref: (l, i, 0))
    mine_specs = [pl.BlockSpec((None, tr, cdim), lambda l, i, c_ref, ll=ll: (c_ref[0], jnp.where(l == ll, i, 0), 0))
                  for ll in range(nl)]
    return pl.pallas_call(
        body, name=name,
        grid_spec=pltpu.PrefetchScalarGridSpec(
            num_scalar_prefetch=1, grid=(nl, r // tr),
            in_specs=[spec, spec, spec, pl.BlockSpec((3, None, tr, cdim), lambda l, i, c_ref: (0, l, i, 0))]
            + mine_specs,
            out_specs=[spec] * 4),
        out_shape=[jax.ShapeDtypeStruct(w.shape, F32)] * 4,
        compiler_params=_params("arbitrary", "arbitrary"),
    )(chip, w, m, v, parts, *mine)


def _adam_small(ws, gs, ms, vs, name):
    n = len(ws)

    def body(*refs):
        w_r, g_r, m_r, v_r = refs[:n], refs[n:2 * n], refs[2 * n:3 * n], refs[3 * n:4 * n]
        d_o, m_o, v_o = refs[4 * n:5 * n], refs[5 * n:6 * n], refs[6 * n:7 * n]
        for t in range(n):
            delta, m2, v2 = _adam_math(w_r[t][...], g_r[t][...], m_r[t][...], v_r[t][...])
            d_o[t][...] = delta
            m_o[t][...] = m2
            v_o[t][...] = v2

    vm = pl.BlockSpec(memory_space=pltpu.VMEM)
    shapes = [jax.ShapeDtypeStruct(a.shape, F32) for a in ws]
    return pl.pallas_call(
        body, name=name, in_specs=[vm] * (4 * n), out_specs=[vm] * (3 * n), out_shape=shapes * 3,
        compiler_params=pltpu.CompilerParams(vmem_limit_bytes=VMEM_LIMIT),
    )(*ws, *gs, *ms, *vs)


def kernel(x, norm_mix, norm_ffn, norm_final, ab_w_in, a_ln_g, a_ln_b, a_w_s, a_b_s, b_conv_w, b_conv_b, b_ln_g, b_ln_b, ab_w_out, c_w_in, c_conv_w, c_w_out, f_w_up, f_conv_w, f_w_down, loss_target, m_norm_mix, m_norm_ffn, m_norm_final, m_ab_w_in, m_a_ln_g, m_a_ln_b, m_a_w_s, m_a_b_s, m_b_conv_w, m_b_conv_b, m_b_ln_g, m_b_ln_b, m_ab_w_out, m_c_w_in, m_c_conv_w, m_c_w_out, m_f_w_up, m_f_conv_w, m_f_w_down, v_norm_mix, v_norm_ffn, v_norm_final, v_ab_w_in, v_a_ln_g, v_a_ln_b, v_a_w_s, v_a_b_s, v_b_conv_w, v_b_conv_b, v_b_ln_g, v_b_ln_b, v_ab_w_out, v_c_w_in, v_c_conv_w, v_c_w_out, v_f_w_up, v_f_conv_w, v_f_w_down):
    s = x.shape[1]
    x0 = x.reshape(s, D)
    tgt = loss_target.reshape(s, D)
    xi, yi, ci = lax.axis_index("x"), lax.axis_index("y"), lax.axis_index("c")
    dev = 4 * xi + 2 * yi + ci
    cidx = ci.astype(jnp.int32).reshape(1)

    bf = lambda a: a.astype(BF16)
    slab_w = 6 * CHUNK
    pad = lambda a, rows: jnp.pad(a, ((0, rows - a.shape[0]), (0, slab_w - a.shape[1])))
    slab = jnp.concatenate([pad(b_conv_w[0], 32), pad(c_conv_w[0], 8), pad(f_conv_w.reshape(6, FB), 8)], axis=0)
    (win0,) = _all_gather([_zone(bf(ab_w_in[0]), dev)], "all_gather_first")
    later = [bf(ab_w_out[0]), slab, bf(f_w_up[0]), bf(f_w_down[0]), bf(c_w_in[0]), bf(c_w_out[0]),
             bf(f_w_up[1]), bf(f_w_down[1])]
    lands = [_zone(a, dev) for a in later]
    groups = [[0, 1], [2, 3], [4, 5], [6, 7]]
    ag_sems, later, lands, ag_token = _ag_start(later, lands, win0, "ag_start")

    causal = jnp.tril(jnp.ones((CHUNK, CHUNK), F32))
    wsm = (a_w_s[0] * causal).astype(BF16)
    bs_col = a_b_s.reshape(HEADS, CHUNK, 1)
    nm = [norm_mix[0:1], norm_mix[1:2]]
    nf = [norm_ffn[0:1], norm_ffn[1:2]]
    nfin = norm_final.reshape(1, D)

    def arrive(g, after_ici, after_d2d, tag):
        srcs = [later[t] for t in groups[g]]
        zone = [lands[t] for t in groups[g]]
        sems1 = [ag_sems[t] for t in groups[g]]
        sems2, zone = _ag_forward(srcs, zone, sems1, after_ici, "ag_forward_" + tag)
        return _ag_finish(srcs, zone, sems1, sems2, after_d2d, "ag_finish_" + tag)

    h0 = _rms_fwd(x0, nm[0], "rms_mix0", after=ag_token)
    z = _mm_in(h0, win0, "mm_ab_in")
    wout0, slab_g = arrive(0, h0, z, "first")
    wout0 = wout0.reshape(D, D)
    bcw = jnp.transpose(slab_g[:, 0:BCONV, 0:DA // NDEV], (1, 0, 2)).reshape(BCONV, DA)
    ccw = jnp.transpose(slab_g[:, 32:35, 0:D // NDEV], (1, 0, 2)).reshape(3, D)
    fcw_g = slab_g[:, 40:46, 0:FB].reshape(2, NG, 2, 3, FB)
    fcws = [fcw_g[:, :, 0], fcw_g[:, :, 1]]
    ycat, yb2 = _ab_fwd(z, a_ln_g, a_ln_b, wsm, bs_col, bcw, b_conv_b, b_ln_g, b_ln_b, "ab_fwd")
    x1, h1 = _mm_out(ycat, wout0, x0, nf[0], "mm_ab_out")
    wup0, wdn0 = arrive(1, ycat, x1, "ffn0")
    up0, upc0, x2, h2 = _ffn_fwd(h1, x1, wup0.reshape(2, NG, D, FB), fcws[0], wdn0.reshape(DFF, D), nm[1],
                                 "ffn_fwd0")
    cin, cout = arrive(2, x2, x2, "c")
    cout = cout.reshape(D, D)
    zc = _mm_in(h2, cin, "mm_c_in")
    rc = _c_fwd(zc, ccw, "c_fwd")
    x3, h3 = _mm_out(rc, cout, x2, nf[1], "mm_c_out")
    wup1, wdn1 = arrive(3, rc, x3, "ffn1")
    wups = [wup0.reshape(2, NG, D, FB), wup1.reshape(2, NG, D, FB)]
    wdns = [wdn0.reshape(DFF, D), wdn1.reshape(DFF, D)]
    up1, upc1, x4 = _ffn_fwd(h3, x3, wups[1], fcws[1], wdns[1], None, "ffn_fwd1")
    dx4, dnfin, loss_part = _final(x4, tgt, nfin, "final_loss")

    zshape = lambda *sh: _hbm(lax.empty((3,) + sh, BF16))
    zones = [zshape(D, 2 * D // NDEV), zshape(D // NDEV, D), zshape(D, 3 * D // NDEV), zshape(D // NDEV, D),
             zshape(2, FB, D), zshape(2, DFF // NDEV, D)]
    started = []

    def pair_sums(grads, handle, after, tag):
        del grads
        got = _pair_wait(handle, after, "rs_pair_wait_" + tag)
        return [_pair_sum(b.reshape((NCHIP, 2) + b.shape[1:]), g, cidx, "rs_pair_sum_%s%d" % (tag, t))
                for t, (b, g) in enumerate(zip(handle[1], got))]

    def chip_start(sums, slots, carry, tag):
        sems, sums, new_zones, carry = _chip_start(sums, zones, slots, carry, "rs_chip_start_" + tag)
        zones[:] = new_zones
        started.append((sums, slots, sems))
        return sums, carry

    rows8 = lambda g, r: g.reshape(NDEV, r, D)
    a1, dup1, dx3, dnf1, dfcw1 = _ffn_bwd(dx4, up1, upc1, wups[1], fcws[1], wdns[1], x3, nf[1], "ffn_bwd1")
    g_f1 = [_dw_up(h3, dup1, "dw_up1"), rows8(_dw_dn(a1, dx4, "dw_dn1"), DFF // NDEV)]
    hd_f1, dx3 = _pair_start(g_f1, dx3, "rs_pair_start_f1")
    drc = _mm_nt(dx3, cout, "mm_c_out_bwd")
    g_cout = rows8(_dw_rows(rc, dx3, "dw_c_out"), D // NDEV)
    s_f1 = pair_sums(g_f1, hd_f1, g_cout, "f1")
    s_f1, drc = chip_start(s_f1, [(4, 1), (5, 1)], drc, "f1")
    dzc, dccw = _c_bwd(drc, zc, ccw, "c_bwd")
    dx2, dnm1 = _mm_nt_rms(dzc, cin, x2, nm[1], dx3, "mm_c_in_bwd")
    g_c = [_dw_cols(h2, dzc, NDEV, 3 * D // NDEV, "dw_c_in"), g_cout]
    hd_c, dx2 = _pair_start(g_c, dx2, "rs_pair_start_c")
    a0, dup0, dx1, dnf0, dfcw0 = _ffn_bwd(dx2, up0, upc0, wups[0], fcws[0], wdns[0], x1, nf[0], "ffn_bwd0")
    s_c = pair_sums(g_c, hd_c, dx1, "c")
    s_c, dx1 = chip_start(s_c, [(2, None), (3, None)], dx1, "c")
    g_f0 = [_dw_up(h1, dup0, "dw_up0"), rows8(_dw_dn(a0, dx2, "dw_dn0"), DFF // NDEV)]
    hd_f0, dx1 = _pair_start(g_f0, dx1, "rs_pair_start_f0")
    dycat = _mm_nt(dx1, wout0, "mm_ab_out_bwd")
    g_wout0 = rows8(_dw_rows(ycat, dx1, "dw_ab_out"), D // NDEV)
    s_f0 = pair_sums(g_f0, hd_f0, g_wout0, "f0")
    s_f0, dycat = chip_start(s_f0, [(4, 0), (5, 0)], dycat, "f0")
    dz, g512, dws, dbs = _ab_bwd(dycat, z, yb2, a_ln_g, a_ln_b, wsm, bs_col, bcw, b_ln_g, b_ln_b, "ab_bwd")
    grad_x, dnm0 = _mm_nt_rms(dz, win0, x0, nm[0], dx1, "mm_ab_in_bwd")
    g_ab = [_dw_cols(h0, dz, NDEV, 2 * D // NDEV, "dw_ab_in"), g_wout0]
    hd_ab, _ = _pair_start(g_ab, None, "rs_pair_start_ab")
    s_ab = pair_sums(g_ab, hd_ab, grad_x, "ab")
    s_ab, _ = chip_start(s_ab, [(0, None), (1, None)], None, "ab")

    g1024 = jnp.concatenate([dnm0, dnm1, dnf0, dnf1, dnfin, dccw], axis=0)
    gfc = jnp.concatenate([dfcw0, dfcw1], axis=0).reshape(2 * NG * 2 * 3, FB)
    p_cin, p_cout, p_wup, p_wdn = _chip_wait(started[:3], zones[2:], [2, 3, 4, 5], s_ab[0], "rs_chip_wait_early")

    chip = (2 * xi + yi).astype(jnp.int32).reshape(1)

    def big_update(w, m, v, parts, mine, name):
        shp = w.shape
        w3, m3, v3 = (a.reshape((-1,) + shp[-2:]) for a in (w, m, v))
        p4 = parts.reshape((3,) + w3.shape)
        return [o.reshape(shp) for o in _adam_big(w3, m3, v3, p4, mine, chip, name)]

    u_cin = big_update(c_w_in, m_c_w_in, v_c_w_in, p_cin, [s_c[0]], "adam_c_w_in")
    u_cout = big_update(c_w_out, m_c_w_out, v_c_w_out, p_cout, [s_c[1]], "adam_c_w_out")
    tr_ = lambda a: jnp.swapaxes(a, 1, 2)
    u_wup = [tr_(o) for o in big_update(tr_(f_w_up), tr_(m_f_w_up), tr_(v_f_w_up), p_wup,
                                        [s_f0[0], s_f1[0]], "adam_f_w_up")]
    u_wdn = big_update(f_w_down, m_f_w_down, v_f_w_down, p_wdn, [s_f0[1], s_f1[1]], "adam_f_w_down")
    g1024, g512, dws, dbs, gfc, loss_sum = _small_allreduce(
        [g1024, g512, dws.reshape(HEADS * CHUNK, CHUNK), dbs.reshape(HEADS, CHUNK), gfc, loss_part], u_wdn[0],
        "small_allreduce")
    loss = loss_sum[0, 0]
    p_win0, p_wout0 = _chip_wait(started[3:], zones[:2], [0, 1], g1024, "rs_chip_wait_late")
    u_win0 = big_update(ab_w_in, m_ab_w_in, v_ab_w_in, p_win0, [s_ab[0]], "adam_ab_w_in")
    u_wout0 = big_update(ab_w_out, m_ab_w_out, v_ab_w_out, p_wout0, [s_ab[1]], "adam_ab_w_out")

    g_norm_mix = g1024[0:2]
    g_norm_ffn = g1024[2:4]
    g_norm_final = g1024[4:5]
    g_ccw = lax.dynamic_slice(g1024[5:8], (0, dev * (D // NDEV)), (3, D // NDEV))
    g_bcw = lax.dynamic_slice(g512[8:8 + BCONV], (0, dev * (DA // NDEV)), (BCONV, DA // NDEV))
    gfc = gfc.reshape(2, NG, 2, 3, FB)
    g_fcw = lax.dynamic_slice(gfc, (0, dev % NG, dev // NG, 0, 0), (2, 1, 1, 3, FB)).reshape(2, 3, FB)
    small_w = [norm_mix, norm_ffn, nfin, a_ln_g, a_ln_b, a_w_s[0], a_b_s[0], b_conv_w[0], b_conv_b,
               b_ln_g, b_ln_b, c_conv_w[0], f_conv_w]
    small_g = [g_norm_mix, g_norm_ffn, g_norm_final, g512[0:1], g512[1:2],
               dws.reshape(HEADS, CHUNK, CHUNK), dbs, g_bcw, g512[2:3],
               g512[3:4], g512[4:5], g_ccw, g_fcw]
    small_m = [m_norm_mix, m_norm_ffn, m_norm_final.reshape(1, D), m_a_ln_g, m_a_ln_b, m_a_w_s[0], m_a_b_s[0],
               m_b_conv_w[0], m_b_conv_b, m_b_ln_g, m_b_ln_b, m_c_conv_w[0], m_f_conv_w]
    small_v = [v_norm_mix, v_norm_ffn, v_norm_final.reshape(1, D), v_a_ln_g, v_a_ln_b, v_a_w_s[0], v_a_b_s[0],
               v_b_conv_w[0], v_b_conv_b, v_b_ln_g, v_b_ln_b, v_c_conv_w[0], v_f_conv_w]
    upd = _adam_small(small_w, small_g, small_m, small_v, "adam_small")
    ns = len(small_w)
    orig = [norm_mix, norm_ffn, norm_final, a_ln_g, a_ln_b, a_w_s, a_b_s, b_conv_w, b_conv_b,
            b_ln_g, b_ln_b, c_conv_w, f_conv_w]
    sg_out = [g.reshape(o.shape) for g, o in zip(small_g, orig)]
    sd_out = [a.reshape(o.shape) for a, o in zip(upd[0:ns], orig)]
    sm_out = [a.reshape(o.shape) for a, o in zip(upd[ns:2 * ns], orig)]
    sv_out = [a.reshape(o.shape) for a, o in zip(upd[2 * ns:3 * ns], orig)]

    def assemble(small, k):
        return [small[0], small[1], small[2], u_win0[k], small[3], small[4], small[5], small[6], small[7],
                small[8], small[9], small[10], u_wout0[k], u_cin[k], small[11], u_cout[k], u_wup[k],
                small[12], u_wdn[k]]

    grads = assemble(sg_out, 0)
    deltas = assemble(sd_out, 1)
    new_m = assemble(sm_out, 2)
    new_v = assemble(sv_out, 3)
    return (loss, grad_x.reshape(1, s, D), *grads, *deltas, *new_m, *new_v)
```

```python
import functools
import math

import jax
import jax.numpy as jnp
from jax import lax
from jax.experimental import pallas as pl
from jax.experimental.pallas import tpu as pltpu

F32 = jnp.float32
BF16 = jnp.bfloat16

D = 1024
DA = 512
HEADS = 4
CHUNK = 128
DFF = 2816
NDEV = 8
NCHIP = 4
FB = DFF * 2 // NDEV
NG = DFF // FB
BCONV = 31
EPS = 1e-6
HALO = 16
HALO_B = 32
RC = 16
NPART = 2
VMEM_LIMIT = 52 * 1024 * 1024
INV_SQRT2 = 1.0 / math.sqrt(2.0)
INV_SQRT_2PI = 1.0 / math.sqrt(2.0 * math.pi)

ADAM_LR = 0.001
ADAM_B1 = 0.9
ADAM_B2 = 0.999
ADAM_EPS = 1e-08
ADAM_WD = 0.01
ADAM_STEP = 10

MESH = pl.DeviceIdType.MESH
ANY = pl.BlockSpec(memory_space=pl.ANY)
NT_DIMS = (((1,), (1,)), ((), ()))
TN_DIMS = (((0,), (0,)), ((), ()))


def _params(*sem):
    return pltpu.CompilerParams(dimension_semantics=sem, vmem_limit_bytes=VMEM_LIMIT)


def _tile(s, want):
    return min(want, s)


def _sigmoid(x):
    return jax.nn.sigmoid(x)


def _dsilu(x, sg):
    return sg * (1.0 + x * (1.0 - sg))


def _gelu(x):
    return 0.5 * x * (1.0 + lax.erf(x * INV_SQRT2))


def _dgelu(x):
    return 0.5 * (1.0 + lax.erf(x * INV_SQRT2)) + x * jnp.exp(-0.5 * x * x) * INV_SQRT_2PI


def _ln_fwd(x, g, b):
    mu = jnp.mean(x, axis=-1, keepdims=True)
    xc = x - mu
    var = jnp.mean(xc * xc, axis=-1, keepdims=True)
    rstd = lax.rsqrt(var + EPS)
    xhat = xc * rstd
    return xhat * g + b, xhat, rstd


def _ln_bwd(dy, xhat, rstd, g):
    dxh = dy * g
    m1 = jnp.mean(dxh, axis=-1, keepdims=True)
    m2 = jnp.mean(dxh * xhat, axis=-1, keepdims=True)
    return rstd * (dxh - m1 - xhat * m2)


def _rms_bwd_math(dh, x, g):
    r = lax.rsqrt(jnp.mean(x * x, axis=-1, keepdims=True) + EPS)
    xhat = x * r
    dg = jnp.sum(dh * xhat, axis=0, keepdims=True)
    u = dh * g
    dx = r * (u - xhat * jnp.mean(u * xhat, axis=-1, keepdims=True))
    return dx, dg


def _conv3(xe, cw, halo):
    x0 = xe[halo:]
    x1 = pltpu.roll(xe, 1, 0)[halo:]
    x2 = pltpu.roll(xe, 2, 0)[halo:]
    return cw[2] * x0 + cw[1] * x1 + cw[0] * x2, (x0, x1, x2)


def _conv3_bwd_in(dce, cw, ts):
    n = dce.shape[0]
    d1 = pltpu.roll(dce, n - 1, 0)[:ts]
    d2 = pltpu.roll(dce, n - 2, 0)[:ts]
    return cw[2] * dce[:ts] + cw[1] * d1 + cw[0] * d2


def _conv3_bwd_w(dc, taps):
    x0, x1, x2 = taps
    return [jnp.sum(dc * x2, axis=0, keepdims=True), jnp.sum(dc * x1, axis=0, keepdims=True),
            jnp.sum(dc * x0, axis=0, keepdims=True)]


def _rms_fwd(x, g, name, after=None):
    s = x.shape[0]
    ts = _tile(s, 512)

    def body(x_ref, g_ref, *rest):
        h_ref = rest[-1]
        xv = x_ref[...]
        r = lax.rsqrt(jnp.mean(xv * xv, axis=-1, keepdims=True) + EPS)
        h_ref[...] = (xv * r * g_ref[...]).astype(BF16)

    extra = [] if after is None else [after]
    return pl.pallas_call(
        body, grid=(s // ts,), name=name,
        in_specs=[pl.BlockSpec((ts, D), lambda i: (i, 0)), pl.BlockSpec((1, D), lambda i: (0, 0))]
        + [ANY] * len(extra),
        out_specs=pl.BlockSpec((ts, D), lambda i: (i, 0)),
        out_shape=jax.ShapeDtypeStruct((s, D), BF16),
        compiler_params=_params("parallel"),
    )(x, g, *extra)


MXU_COLS = 256


def _pair(bn):
    return 1 if bn % MXU_COLS == 0 else 2


def _cols(w_ref, b, pair):
    return w_ref[b] if pair == 1 else jnp.concatenate([w_ref[b + q] for q in range(pair)], axis=1)


def _mm_in(h, wblk, name):
    s = h.shape[0]
    nb, _, bn = wblk.shape
    pair = _pair(bn)
    ts = _tile(s, 512)

    def body(h_ref, w_ref, o_ref):
        hv = h_ref[...]
        for b in range(0, nb, pair):
            o_ref[:, b * bn:(b + pair) * bn] = jnp.dot(hv, _cols(w_ref, b, pair),
                                                       preferred_element_type=F32).astype(BF16)

    return pl.pallas_call(
        body, grid=(s // ts,), name=name,
        in_specs=[pl.BlockSpec((ts, D), lambda i: (i, 0)), pl.BlockSpec((nb, D, bn), lambda i: (0, 0, 0))],
        out_specs=pl.BlockSpec((ts, nb * bn), lambda i: (i, 0)),
        out_shape=jax.ShapeDtypeStruct((s, nb * bn), BF16),
        compiler_params=_params("parallel"),
    )(h, wblk)


def _rms_math(xv, g):
    r = lax.rsqrt(jnp.mean(xv * xv, axis=-1, keepdims=True) + EPS)
    return (xv * r * g).astype(BF16)


def _mm_out(y, w, xres, gnext, name):
    s = y.shape[0]
    ts = _tile(s, 512)

    def body(y_ref, w_ref, x_ref, g_ref, o_ref, h_ref):
        xn = x_ref[...] + jnp.dot(y_ref[...], w_ref[...], preferred_element_type=F32)
        o_ref[...] = xn
        h_ref[...] = _rms_math(xn, g_ref[...])

    return pl.pallas_call(
        body, grid=(s // ts,), name=name,
        in_specs=[pl.BlockSpec((ts, D), lambda i: (i, 0)), pl.BlockSpec((D, D), lambda i: (0, 0)),
                  pl.BlockSpec((ts, D), lambda i: (i, 0)), pl.BlockSpec((1, D), lambda i: (0, 0))],
        out_specs=[pl.BlockSpec((ts, D), lambda i: (i, 0)), pl.BlockSpec((ts, D), lambda i: (i, 0))],
        out_shape=[jax.ShapeDtypeStruct((s, D), F32), jax.ShapeDtypeStruct((s, D), BF16)],
        compiler_params=_params("parallel"),
    )(y, w, xres, gnext)


def _conv31(ue, cw_ref, ts):
    acc = jnp.zeros((ts, ue.shape[1]), F32)
    for r in range(8):
        rolled = ue if r == 0 else pltpu.roll(ue, r, 0)
        for q in range(4):
            sh = 8 * q + r
            if sh >= BCONV:
                continue
            k = BCONV - 1 - sh
            acc = acc + cw_ref[k:k + 1, :] * rolled[HALO_B - 8 * q:HALO_B - 8 * q + ts]
    return acc


def _ab_fwd(z, lga, lba, wsm, bs_col, cwb, cbb, lgb, lbb, name):
    s = z.shape[0]
    ts = _tile(s, 256)
    hb = ts // HALO_B

    def body(z_ref, zh_ref, lga_ref, lba_ref, ws_ref, bs_ref, cw_ref, cb_ref, lgb_ref, lbb_ref,
             y_ref, yb2_ref):
        i = pl.program_id(0)
        z_t = z_ref[...].astype(F32)
        gu = _gelu(z_t[:, 0:DA])
        gv = _gelu(z_t[:, DA:2 * DA])
        vn, _, _ = _ln_fwd(gv, lga_ref[...], lba_ref[...])
        vnb = vn.astype(BF16)
        for c in range(ts // CHUNK):
            for h in range(HEADS):
                rs = slice(c * CHUNK, (c + 1) * CHUNK)
                cs = slice(h * CHUNK, (h + 1) * CHUNK)
                mixed = jnp.dot(ws_ref[h], vnb[rs, cs], preferred_element_type=F32) + bs_ref[h]
                y_ref[rs, cs] = (gu[rs, cs] * mixed).astype(BF16)
        zh = jnp.where(i > 0, zh_ref[...], jnp.zeros_like(zh_ref[...])).astype(F32)
        xb = jnp.concatenate([zh[:, 0:DA], z_t[:, 2 * DA:3 * DA]], axis=0)
        gb = jnp.concatenate([zh[:, DA:2 * DA], z_t[:, 3 * DA:4 * DA]], axis=0)
        u = xb * _sigmoid(gb)
        conv = _conv31(u, cw_ref, ts) + cb_ref[...]
        yb2_ref[...] = conv
        nb_, _, _ = _ln_fwd(conv, lgb_ref[...], lbb_ref[...])
        y_ref[:, DA:2 * DA] = (nb_ * _sigmoid(nb_)).astype(BF16)

    row = lambda i: (0, 0)
    return pl.pallas_call(
        body, grid=(s // ts,), name=name,
        in_specs=[pl.BlockSpec((ts, 4 * DA), lambda i: (i, 0)),
                  pl.BlockSpec((HALO_B, 2 * DA), lambda i: (jnp.maximum(i * hb - 1, 0), 1)),
                  pl.BlockSpec((1, DA), row), pl.BlockSpec((1, DA), row),
                  pl.BlockSpec((HEADS, CHUNK, CHUNK), lambda i: (0, 0, 0)),
                  pl.BlockSpec((HEADS, CHUNK, 1), lambda i: (0, 0, 0)),
                  pl.BlockSpec((BCONV, DA), row), pl.BlockSpec((1, DA), row),
                  pl.BlockSpec((1, DA), row), pl.BlockSpec((1, DA), row)],
        out_specs=[pl.BlockSpec((ts, 2 * DA), lambda i: (i, 0)), pl.BlockSpec((ts, DA), lambda i: (i, 0))],
        out_shape=[jax.ShapeDtypeStruct((s, 2 * DA), BF16), jax.ShapeDtypeStruct((s, DA), F32)],
        compiler_params=_params("parallel"),
    )(z, z, lga, lba, wsm, bs_col, cwb, cbb, lgb, lbb)


def _c_fwd(zc, cw, name):
    s = zc.shape[0]
    ts = _tile(s, 512)
    hb = ts // HALO

    def body(z_ref, ch_ref, xh_ref, cw_ref, r_ref):
        i = pl.program_id(0)
        z_t = z_ref[...].astype(F32)
        ph = jnp.where(i > 0, ch_ref[...].astype(F32) * xh_ref[...].astype(F32), 0.0)
        pe = jnp.concatenate([ph, z_t[:, D:2 * D] * z_t[:, 2 * D:3 * D]], axis=0)
        q, _ = _conv3(pe, [cw_ref[k:k + 1, :] for k in range(3)], HALO)
        r_ref[...] = (z_t[:, 0:D] * q).astype(BF16)

    halo = lambda col: pl.BlockSpec((HALO, D), lambda i: (jnp.maximum(i * hb - 1, 0), col))
    return pl.pallas_call(
        body, grid=(s // ts,), name=name,
        in_specs=[pl.BlockSpec((ts, 3 * D), lambda i: (i, 0)), halo(1), halo(2),
                  pl.BlockSpec((3, D), lambda i: (0, 0))],
        out_specs=pl.BlockSpec((ts, D), lambda i: (i, 0)),
        out_shape=jax.ShapeDtypeStruct((s, D), BF16),
        compiler_params=_params("parallel"),
    )(zc, zc, zc, cw)


def _ffn_fwd(h, xres, wup, fcw, wdn, gnext, name):
    s = h.shape[0]
    ts = _tile(s, 512)
    hb = ts // HALO

    def body(h_ref, hh_ref, w_ref, cw_ref, wd_ref, x_ref, *rest):
        if gnext is not None:
            gn_ref, up_ref, upc_ref, xo_ref, hn_ref = rest
        else:
            up_ref, upc_ref, xo_ref = rest
        i = pl.program_id(0)
        m = pl.program_id(1)
        @pl.when(m == 0)
        def _():
            xo_ref[...] = x_ref[...]

        halo = jnp.where(i > 0, hh_ref[...], jnp.zeros_like(hh_ref[...]))
        hx = jnp.concatenate([halo, h_ref[...]], axis=0)
        acts = []
        for gv in range(2):
            up = jnp.dot(hx, w_ref[gv], preferred_element_type=F32)
            up_ref[gv] = up[HALO:].astype(BF16)
            upc, _ = _conv3(up, [cw_ref[gv, k:k + 1, :] for k in range(3)], HALO)
            upc_ref[gv] = upc.astype(BF16)
            acts.append(upc)
        a = acts[0] * _sigmoid(acts[0]) * acts[1]
        xo_ref[...] += jnp.dot(a.astype(BF16), wd_ref[...], preferred_element_type=F32)

        if gnext is not None:
            @pl.when(m == NG - 1)
            def _():
                hn_ref[...] = _rms_math(xo_ref[...], gn_ref[...])

    tile = pl.BlockSpec((ts, D), lambda i, m: (i, 0))
    nxt = gnext is not None
    return pl.pallas_call(
        body, grid=(s // ts, NG), name=name,
        in_specs=[tile,
                  pl.BlockSpec((HALO, D), lambda i, m: (jnp.maximum(i * hb - 1, 0), 0)),
                  pl.BlockSpec((2, None, D, FB), lambda i, m: (0, m, 0, 0)),
                  pl.BlockSpec((2, None, 3, FB), lambda i, m: (0, m, 0, 0)),
                  pl.BlockSpec((FB, D), lambda i, m: (m, 0)),
                  tile] + ([pl.BlockSpec((1, D), lambda i, m: (0, 0))] if nxt else []),
        out_specs=[pl.BlockSpec((None, 2, ts, FB), lambda i, m: (m, 0, i, 0)),
                   pl.BlockSpec((None, 2, ts, FB), lambda i, m: (m, 0, i, 0)),
                   tile] + ([tile] if nxt else []),
        out_shape=[jax.ShapeDtypeStruct((NG, 2, s, FB), BF16), jax.ShapeDtypeStruct((NG, 2, s, FB), BF16),
                   jax.ShapeDtypeStruct((s, D), F32)] + ([jax.ShapeDtypeStruct((s, D), BF16)] if nxt else []),
        compiler_params=_params("arbitrary", "arbitrary"),
    )(h, h, wup, fcw, wdn, xres, *([gnext] if nxt else []))


def _final(x, tgt, g, name):
    s = x.shape[0]
    ts = _tile(s, 512)

    def body(x_ref, t_ref, g_ref, dx_ref, dxb_ref, dg_ref, loss_ref):
        i = pl.program_id(0)
        xv = x_ref[...]
        gv = g_ref[...]
        r = lax.rsqrt(jnp.mean(xv * xv, axis=-1, keepdims=True) + EPS)
        xhat = xv * r
        e = xhat * gv - t_ref[...]
        part = 0.5 * jnp.sum(jnp.mean(e * e, axis=-1, keepdims=True), axis=0, keepdims=True)
        dy = e * (1.0 / D)
        dgp = jnp.sum(dy * xhat, axis=0, keepdims=True)
        u = dy * gv
        dx = r * (u - xhat * jnp.mean(u * xhat, axis=-1, keepdims=True))
        dx_ref[...] = dx
        dxb_ref[...] = dx.astype(BF16)

        @pl.when(i == 0)
        def _():
            dg_ref[...] = dgp
            loss_ref[...] = jnp.broadcast_to(part, (1, 128))

        @pl.when(i > 0)
        def _():
            dg_ref[...] += dgp
            loss_ref[...] += jnp.broadcast_to(part, (1, 128))

    return pl.pallas_call(
        body, grid=(s // ts,), name=name,
        in_specs=[pl.BlockSpec((ts, D), lambda i: (i, 0)), pl.BlockSpec((ts, D), lambda i: (i, 0)),
                  pl.BlockSpec((1, D), lambda i: (0, 0))],
        out_specs=[pl.BlockSpec((ts, D), lambda i: (i, 0)), pl.BlockSpec((ts, D), lambda i: (i, 0)),
                   pl.BlockSpec((1, D), lambda i: (0, 0)), pl.BlockSpec((1, 128), lambda i: (0, 0))],
        out_shape=[jax.ShapeDtypeStruct((s, D), F32), jax.ShapeDtypeStruct((s, D), BF16),
                   jax.ShapeDtypeStruct((1, D), F32), jax.ShapeDtypeStruct((1, 128), F32)],
        compiler_params=_params("arbitrary"),
    )(x, tgt, g)


def _ffn_bwd(df, up, upc, wup, fcw, wdn, xin, g, name):
    s = df.shape[0]
    ts = _tile(s, 512)
    nt = s // ts

    def body(df_ref, up_ref, upc_ref, w_ref, cw_ref, wd_ref, x_ref, g_ref,
             a_ref, dup_ref, dx_ref, dxb_ref, dg_ref, dcw_ref, carry, acc, tacc):
        i = pl.program_id(0)
        m = pl.program_id(1)
        first = i == 0
        @pl.when(first)
        def _():
            carry[m] = jnp.zeros((2, 8, FB), F32)
            dcw_ref[m] = jnp.zeros((2, 3, FB), F32)

        @pl.when(m == 0)
        def _():
            acc[...] = jnp.zeros((ts, D), F32)

        cws = [[cw_ref[gv, k:k + 1, :] for k in range(3)] for gv in range(2)]
        part = ts // NPART
        das = [lax.dot_general(df_ref[p * part:(p + 1) * part, :].astype(BF16), wd_ref[...], NT_DIMS,
                               preferred_element_type=F32) for p in range(NPART)]

        tacc[...] = jnp.zeros((2, 3, 8, FB), F32)
        nxt = [carry[m, 0], carry[m, 1]]
        for r in reversed(range(ts // RC)):
            rs = slice(r * RC, (r + 1) * RC)
            gate = upc_ref[0, rs, :].astype(F32)
            val = upc_ref[1, rs, :].astype(F32)
            sg = _sigmoid(gate)
            sl = gate * sg
            a_ref[rs, :] = (sl * val).astype(BF16)
            da_c = das[(r * RC) // part][(r * RC) % part:(r * RC) % part + RC]
            dcs = [da_c * val * _dsilu(gate, sg), da_c * sl]
            for gv in range(2):
                dc = dcs[gv]
                dce = jnp.concatenate([dc, nxt[gv]], axis=0)
                d1 = pltpu.roll(dce, RC + 7, 0)[:RC]
                d2 = pltpu.roll(dce, RC + 6, 0)[:RC]
                du = cws[gv][2] * dc + cws[gv][1] * d1 + cws[gv][0] * d2
                dup_ref[gv, rs, :] = du.astype(BF16)
                x0 = up_ref[gv, rs, :].astype(F32)
                for k, dk in enumerate((d2, d1, dc)):
                    p = x0 * dk
                    tacc[gv, k] += p[0:8] + p[8:16]
                nxt[gv] = dc[0:8]
            if (r * RC) % part == 0:
                ps = slice(r * RC, r * RC + part)
                acc[ps, :] += (
                    lax.dot_general(dup_ref[0, ps, :], w_ref[0], NT_DIMS, preferred_element_type=F32)
                    + lax.dot_general(dup_ref[1, ps, :], w_ref[1], NT_DIMS, preferred_element_type=F32))
        for gv in range(2):
            carry[m, gv] = nxt[gv]
            for k in range(3):
                dcw_ref[m, gv, k:k + 1, :] += jnp.sum(tacc[gv, k], axis=0, keepdims=True)

        @pl.when(m == NG - 1)
        def _():
            dx, dgp = _rms_bwd_math(acc[...], x_ref[...], g_ref[...])
            dx = df_ref[...] + dx
            dx_ref[...] = dx
            dxb_ref[...] = dx.astype(BF16)

            @pl.when(first)
            def _():
                dg_ref[...] = dgp

            @pl.when(jnp.logical_not(first))
            def _():
                dg_ref[...] += dgp

    rev = lambda i: nt - 1 - i
    return pl.pallas_call(
        body, grid=(nt, NG), name=name,
        in_specs=[pl.BlockSpec((ts, D), lambda i, m: (rev(i), 0)),
                  pl.BlockSpec((None, 2, ts, FB), lambda i, m: (m, 0, rev(i), 0)),
                  pl.BlockSpec((None, 2, ts, FB), lambda i, m: (m, 0, rev(i), 0)),
                  pl.BlockSpec((2, None, D, FB), lambda i, m: (0, m, 0, 0)),
                  pl.BlockSpec((2, None, 3, FB), lambda i, m: (0, m, 0, 0)),
                  pl.BlockSpec((FB, D), lambda i, m: (m, 0)),
                  pl.BlockSpec((ts, D), lambda i, m: (rev(i), 0)),
                  pl.BlockSpec((1, D), lambda i, m: (0, 0))],
        out_specs=[pl.BlockSpec((None, ts, FB), lambda i, m: (m, rev(i), 0)),
                   pl.BlockSpec((None, 2, ts, FB), lambda i, m: (m, 0, rev(i), 0)),
                   pl.BlockSpec((ts, D), lambda i, m: (rev(i), 0)),
                   pl.BlockSpec((ts, D), lambda i, m: (rev(i), 0)),
                   pl.BlockSpec((1, D), lambda i, m: (0, 0)),
                   pl.BlockSpec((NG, 2, 3, FB), lambda i, m: (0, 0, 0, 0))],
        out_shape=[jax.ShapeDtypeStruct((NG, s, FB), BF16), jax.ShapeDtypeStruct((NG, 2, s, FB), BF16),
                   jax.ShapeDtypeStruct((s, D), F32), jax.ShapeDtypeStruct((s, D), BF16),
                   jax.ShapeDtypeStruct((1, D), F32),
                   jax.ShapeDtypeStruct((NG, 2, 3, FB), F32)],
        scratch_shapes=[pltpu.VMEM((NG, 2, 8, FB), F32), pltpu.VMEM((ts, D), F32),
                        pltpu.VMEM((2, 3, 8, FB), F32)],
        compiler_params=_params("arbitrary", "arbitrary"),
    )(df, up, upc, wup, fcw, wdn, xin, g)


def _mm_nt(dy, w, name):
    s = dy.shape[0]
    ts = _tile(s, 512)

    def body(dy_ref, w_ref, o_ref):
        o_ref[...] = lax.dot_general(dy_ref[...], w_ref[...], NT_DIMS,
                                     preferred_element_type=F32).astype(BF16)

    return pl.pallas_call(
        body, grid=(s // ts,), name=name,
        in_specs=[pl.BlockSpec((ts, D), lambda i: (i, 0)), pl.BlockSpec((D, D), lambda i: (0, 0))],
        out_specs=pl.BlockSpec((ts, D), lambda i: (i, 0)),
        out_shape=jax.ShapeDtypeStruct((s, D), BF16),
        compiler_params=_params("parallel"),
    )(dy, w)


def _mm_nt_rms(dy, wblk, x, g, dres, bf16_copy, name):
    s = dy.shape[0]
    nb, _, bn = wblk.shape
    pair = _pair(bn)
    ts = _tile(s, 512)

    def body(dy_ref, w_ref, x_ref, g_ref, dr_ref, dx_ref, *rest):
        dg_ref = rest[-1]
        i = pl.program_id(0)
        acc = jnp.zeros((ts, D), F32)
        for b in range(0, nb, pair):
            acc = acc + lax.dot_general(dy_ref[:, b * bn:(b + pair) * bn], _cols(w_ref, b, pair), NT_DIMS,
                                        preferred_element_type=F32)
        dx, dgp = _rms_bwd_math(acc, x_ref[...], g_ref[...])
        dx = dr_ref[...] + dx
        dx_ref[...] = dx
        if bf16_copy:
            rest[0][...] = dx.astype(BF16)

        @pl.when(i == 0)
        def _():
            dg_ref[...] = dgp

        @pl.when(i > 0)
        def _():
            dg_ref[...] += dgp

    tile = pl.BlockSpec((ts, D), lambda i: (i, 0))
    return pl.pallas_call(
        body, grid=(s // ts,), name=name,
        in_specs=[pl.BlockSpec((ts, nb * bn), lambda i: (i, 0)), pl.BlockSpec((nb, D, bn), lambda i: (0, 0, 0)),
                  tile, pl.BlockSpec((1, D), lambda i: (0, 0)), tile],
        out_specs=[tile] + ([tile] if bf16_copy else []) + [pl.BlockSpec((1, D), lambda i: (0, 0))],
        out_shape=[jax.ShapeDtypeStruct((s, D), F32)] + ([jax.ShapeDtypeStruct((s, D), BF16)] if bf16_copy else [])
        + [jax.ShapeDtypeStruct((1, D), F32)],
        compiler_params=_params("arbitrary"),
    )(dy, wblk, x, g, dres)


def _c_bwd(dr, zc, cw, name):
    s = dr.shape[0]
    ts = _tile(s, 512)
    nt = s // ts
    hb = ts // HALO

    def body(dr_ref, drf_ref, z_ref, ch_ref, xh_ref, bf_ref, cw_ref, dz_ref, dcw_ref):
        i = pl.program_id(0)
        cwv = [cw_ref[k:k + 1, :] for k in range(3)]
        z_t = z_ref[...].astype(F32)
        bg, cg, xv = z_t[:, 0:D], z_t[:, D:2 * D], z_t[:, 2 * D:3 * D]
        ph = jnp.where(i > 0, ch_ref[...].astype(F32) * xh_ref[...].astype(F32), 0.0)
        pe = jnp.concatenate([ph, cg * xv], axis=0)
        q, taps = _conv3(pe, cwv, HALO)
        drv = dr_ref[...].astype(F32)
        dq = drv * bg
        dqf = jnp.where(i < nt - 1, drf_ref[...].astype(F32) * bf_ref[...].astype(F32), 0.0)
        dp = _conv3_bwd_in(jnp.concatenate([dq, dqf], axis=0), cwv, ts)
        dz_ref[:, 0:D] = (drv * q).astype(BF16)
        dz_ref[:, D:2 * D] = (dp * xv).astype(BF16)
        dz_ref[:, 2 * D:3 * D] = (dp * cg).astype(BF16)
        rows = _conv3_bwd_w(dq, taps)

        @pl.when(i == 0)
        def _():
            for k in range(3):
                dcw_ref[k:k + 1, :] = rows[k]

        @pl.when(i > 0)
        def _():
            for k in range(3):
                dcw_ref[k:k + 1, :] += rows[k]

    past = lambda col: pl.BlockSpec((HALO, D), lambda i: (jnp.maximum(i * hb - 1, 0), col))
    nxt = lambda i: jnp.minimum((i + 1) * hb, s // HALO - 1)
    return pl.pallas_call(
        body, grid=(nt,), name=name,
        in_specs=[pl.BlockSpec((ts, D), lambda i: (i, 0)),
                  pl.BlockSpec((HALO, D), lambda i: (nxt(i), 0)),
                  pl.BlockSpec((ts, 3 * D), lambda i: (i, 0)), past(1), past(2),
                  pl.BlockSpec((HALO, D), lambda i: (nxt(i), 0)),
                  pl.BlockSpec((3, D), lambda i: (0, 0))],
        out_specs=[pl.BlockSpec((ts, 3 * D), lambda i: (i, 0)), pl.BlockSpec((3, D), lambda i: (0, 0))],
        out_shape=[jax.ShapeDtypeStruct((s, 3 * D), BF16), jax.ShapeDtypeStruct((3, D), F32)],
        compiler_params=_params("arbitrary"),
    )(dr, dr, zc, zc, zc, zc, cw)


G512_ROWS = 40


def _ab_bwd(dy, z, yb2, lga, lba, wsm, bs_col, cwb, lgb, lbb, name):
    s = z.shape[0]
    ts = _tile(s, 256)
    nt = s // ts
    hb = ts // HALO_B
    nch = ts // CHUNK
    tri = None

    def body(z_ref, zh_ref, dy_ref, dyf_ref, yb2_ref, yb2f_ref, lga_ref, lba_ref, ws_ref, bs_ref,
             cw_ref, lgb_ref, lbb_ref, dz_ref, g512_ref, dws_ref, dbs_ref, dvn_ref):
        i = pl.program_id(0)
        last = i == nt - 1

        @pl.when(i == 0)
        def _():
            g512_ref[...] = jnp.zeros((G512_ROWS, DA), F32)
            dws_ref[...] = jnp.zeros((HEADS, CHUNK, CHUNK), F32)
            dbs_ref[...] = jnp.zeros((HEADS, CHUNK, 1), F32)

        def add_row(k, v):
            g512_ref[k:k + 1, :] += v

        z_t = z_ref[...].astype(F32)
        dy_t = dy_ref[...].astype(F32)
        ua, va = z_t[:, 0:DA], z_t[:, DA:2 * DA]
        gu = _gelu(ua)
        gv = _gelu(va)
        lga_v = lga_ref[...]
        vn, xhat_a, rstd_a = _ln_fwd(gv, lga_v, lba_ref[...])
        vnb = vn.astype(BF16)
        causal = (lax.broadcasted_iota(jnp.int32, (CHUNK, CHUNK), 0)
                  >= lax.broadcasted_iota(jnp.int32, (CHUNK, CHUNK), 1)).astype(F32)
        for c in range(nch):
            for h in range(HEADS):
                rs = slice(c * CHUNK, (c + 1) * CHUNK)
                cs = slice(h * CHUNK, (h + 1) * CHUNK)
                vblk = vnb[rs, cs]
                mixed = jnp.dot(ws_ref[h], vblk, preferred_element_type=F32) + bs_ref[h]
                dyb_ = dy_t[rs, cs]
                dmix = dyb_ * gu[rs, cs]
                dmb = dmix.astype(BF16)
                dz_ref[rs, cs] = (dyb_ * mixed * _dgelu(ua[rs, cs])).astype(BF16)
                dvn_ref[rs, cs] = lax.dot_general(ws_ref[h], dmb, TN_DIMS, preferred_element_type=F32)
                dws_ref[h] += causal * lax.dot_general(dmb, vblk, NT_DIMS, preferred_element_type=F32)
                dbs_ref[h] += jnp.sum(dmix, axis=1, keepdims=True)
        dvn = dvn_ref[...]
        add_row(0, jnp.sum(dvn * xhat_a, axis=0, keepdims=True))
        add_row(1, jnp.sum(dvn, axis=0, keepdims=True))
        dgv = _ln_bwd(dvn, xhat_a, rstd_a, lga_v)
        dz_ref[:, DA:2 * DA] = (dgv * _dgelu(va)).astype(BF16)
        lgb_v = lgb_ref[...]
        dyb_e = jnp.concatenate(
            [dy_t[:, DA:2 * DA], jnp.where(last, 0.0, dyf_ref[...].astype(F32))], axis=0)
        yb2_e = jnp.concatenate([yb2_ref[...], jnp.where(last, 0.0, yb2f_ref[...])], axis=0)
        n_e, xhat_b, rstd_b = _ln_fwd(yb2_e, lgb_v, lbb_ref[...])
        sgn = _sigmoid(n_e)
        dn = dyb_e * _dsilu(n_e, sgn)
        dy2 = _ln_bwd(dn, xhat_b, rstd_b, lgb_v)
        add_row(2, jnp.sum(dy2[:ts], axis=0, keepdims=True))
        add_row(3, jnp.sum(dn[:ts] * xhat_b[:ts], axis=0, keepdims=True))
        add_row(4, jnp.sum(dn[:ts], axis=0, keepdims=True))
        zh = jnp.where(i > 0, zh_ref[...], jnp.zeros_like(zh_ref[...])).astype(F32)
        xb_t, gb_t = z_t[:, 2 * DA:3 * DA], z_t[:, 3 * DA:4 * DA]
        sgb = _sigmoid(gb_t)
        ue = jnp.concatenate([zh[:, 0:DA] * _sigmoid(zh[:, DA:2 * DA]), xb_t * sgb], axis=0)
        dy2_t = dy2[:ts]
        n_e_rows = ts + HALO_B
        du = jnp.zeros((ts, DA), F32)
        for r in range(8):
            fwd_roll = ue if r == 0 else pltpu.roll(ue, r, 0)
            bwd_roll = dy2 if r == 0 else pltpu.roll(dy2, n_e_rows - r, 0)
            for q in range(4):
                sh = 8 * q + r
                if sh >= BCONV:
                    continue
                k = BCONV - 1 - sh
                du = du + cw_ref[k:k + 1, :] * bwd_roll[8 * q:8 * q + ts]
                add_row(8 + k, jnp.sum(dy2_t * fwd_roll[HALO_B - 8 * q:HALO_B - 8 * q + ts],
                                       axis=0, keepdims=True))
        dz_ref[:, 2 * DA:3 * DA] = (du * sgb).astype(BF16)
        dz_ref[:, 3 * DA:4 * DA] = (du * xb_t * sgb * (1.0 - sgb)).astype(BF16)

    row = lambda i: (0, 0)
    nxt = lambda i: jnp.minimum((i + 1) * hb, s // HALO_B - 1)
    return pl.pallas_call(
        body, grid=(nt,), name=name,
        in_specs=[pl.BlockSpec((ts, 4 * DA), lambda i: (i, 0)),
                  pl.BlockSpec((HALO_B, 2 * DA), lambda i: (jnp.maximum(i * hb - 1, 0), 1)),
                  pl.BlockSpec((ts, 2 * DA), lambda i: (i, 0)),
                  pl.BlockSpec((HALO_B, DA), lambda i: (nxt(i), 1)),
                  pl.BlockSpec((ts, DA), lambda i: (i, 0)),
                  pl.BlockSpec((HALO_B, DA), lambda i: (nxt(i), 0)),
                  pl.BlockSpec((1, DA), row), pl.BlockSpec((1, DA), row),
                  pl.BlockSpec((HEADS, CHUNK, CHUNK), lambda i: (0, 0, 0)),
                  pl.BlockSpec((HEADS, CHUNK, 1), lambda i: (0, 0, 0)),
                  pl.BlockSpec((BCONV, DA), row), pl.BlockSpec((1, DA), row), pl.BlockSpec((1, DA), row)],
        out_specs=[pl.BlockSpec((ts, 4 * DA), lambda i: (i, 0)),
                   pl.BlockSpec((G512_ROWS, DA), row),
                   pl.BlockSpec((HEADS, CHUNK, CHUNK), lambda i: (0, 0, 0)),
                   pl.BlockSpec((HEADS, CHUNK, 1), lambda i: (0, 0, 0))],
        out_shape=[jax.ShapeDtypeStruct((s, 4 * DA), BF16), jax.ShapeDtypeStruct((G512_ROWS, DA), F32),
                   jax.ShapeDtypeStruct((HEADS, CHUNK, CHUNK), F32),
                   jax.ShapeDtypeStruct((HEADS, CHUNK, 1), F32)],
        scratch_shapes=[pltpu.VMEM((ts, DA), F32)],
        compiler_params=_params("arbitrary"),
    )(z, z, dy, dy, yb2, yb2, lga, lba, wsm, bs_col, cwb, lgb, lbb)


def _dw_cols(a, dy, nb, bn, name):
    s = a.shape[0]
    tm = _tile(s, 2048)
    nt = s // tm
    cpb = 4

    def body(a_ref, dy_ref, o_ref, acc):
        t = pl.program_id(1)
        p = lax.dot_general(a_ref[...], dy_ref[...], TN_DIMS, preferred_element_type=F32)

        @pl.when(t == 0)
        def _():
            for q in range(cpb):
                acc[q] = p[:, q * bn:(q + 1) * bn]

        @pl.when(t > 0)
        def _():
            for q in range(cpb):
                acc[q] += p[:, q * bn:(q + 1) * bn]

        @pl.when(t == nt - 1)
        def _():
            o_ref[...] = acc[...].astype(BF16)

    return pl.pallas_call(
        body, grid=(nb // cpb, nt), name=name,
        in_specs=[pl.BlockSpec((tm, D), lambda j, t: (t, 0)), pl.BlockSpec((tm, cpb * bn), lambda j, t: (t, j))],
        out_specs=pl.BlockSpec((cpb, D, bn), lambda j, t: (j, 0, 0)),
        out_shape=jax.ShapeDtypeStruct((nb, D, bn), BF16),
        scratch_shapes=[pltpu.VMEM((cpb, D, bn), F32)],
        compiler_params=_params("arbitrary", "arbitrary"),
    )(a, dy)


def _dw_rows(a, dy, name):
    s = a.shape[0]
    tm = _tile(s, 4096)
    nt = s // tm
    rb = 512

    def body(a_ref, dy_ref, o_ref, acc):
        t = pl.program_id(1)
        p = lax.dot_general(a_ref[...], dy_ref[...], TN_DIMS, preferred_element_type=F32)

        @pl.when(t == 0)
        def _():
            acc[...] = p

        @pl.when(t > 0)
        def _():
            acc[...] += p

        @pl.when(t == nt - 1)
        def _():
            o_ref[...] = acc[...].astype(BF16)

    return pl.pallas_call(
        body, grid=(D // rb, nt), name=name,
        in_specs=[pl.BlockSpec((tm, rb), lambda j, t: (t, j)), pl.BlockSpec((tm, D), lambda j, t: (t, 0))],
        out_specs=pl.BlockSpec((rb, D), lambda j, t: (j, 0)),
        out_shape=jax.ShapeDtypeStruct((D, D), BF16),
        scratch_shapes=[pltpu.VMEM((rb, D), F32)],
        compiler_params=_params("arbitrary", "arbitrary"),
    )(a, dy)


def _dw_up(h, dup, name):
    s = h.shape[0]
    tm = _tile(s, 4096)
    nt = s // tm

    def body(h_ref, d_ref, o_ref, acc):
        t = pl.program_id(1)
        p = lax.dot_general(d_ref[...], h_ref[...], TN_DIMS, preferred_element_type=F32)

        @pl.when(t == 0)
        def _():
            acc[...] = p

        @pl.when(t > 0)
        def _():
            acc[...] += p

        @pl.when(t == nt - 1)
        def _():
            o_ref[...] = acc[...].astype(BF16)

    return pl.pallas_call(
        body, grid=(NDEV, nt), name=name,
        in_specs=[pl.BlockSpec((tm, D), lambda b, t: (t, 0)),
                  pl.BlockSpec((None, None, tm, FB), lambda b, t: (b % NG, b // NG, t, 0))],
        out_specs=pl.BlockSpec((None, FB, D), lambda b, t: (b, 0, 0)),
        out_shape=jax.ShapeDtypeStruct((NDEV, FB, D), BF16),
        scratch_shapes=[pltpu.VMEM((FB, D), F32)],
        compiler_params=_params("arbitrary", "arbitrary"),
    )(h, dup)


def _dw_dn(a, df, name):
    s = df.shape[0]
    tm = _tile(s, 4096)
    nt = s // tm

    def body(a_ref, d_ref, o_ref, acc):
        t = pl.program_id(1)
        p = lax.dot_general(a_ref[...], d_ref[...], TN_DIMS, preferred_element_type=F32)

        @pl.when(t == 0)
        def _():
            acc[...] = p

        @pl.when(t > 0)
        def _():
            acc[...] += p

        @pl.when(t == nt - 1)
        def _():
            o_ref[...] = acc[...].astype(BF16)

    return pl.pallas_call(
        body, grid=(NG, nt), name=name,
        in_specs=[pl.BlockSpec((None, tm, FB), lambda m, t: (m, t, 0)), pl.BlockSpec((tm, D), lambda m, t: (t, 0))],
        out_specs=pl.BlockSpec((FB, D), lambda m, t: (m, 0)),
        out_shape=jax.ShapeDtypeStruct((DFF, D), BF16),
        scratch_shapes=[pltpu.VMEM((FB, D), F32)],
        compiler_params=_params("arbitrary", "arbitrary"),
    )(a, df)


def _place():
    x, y, c = lax.axis_index("x"), lax.axis_index("y"), lax.axis_index("c")
    chips = [(1 - x, y), (x, 1 - y), (1 - x, 1 - y)]
    return x, y, c, chips


def _zone(shard, dev):
    return lax.dynamic_update_slice(lax.empty((NDEV,) + shard.shape, shard.dtype), shard[None],
                                    (dev,) + (0,) * shard.ndim)


HBM_SPEC = pl.BlockSpec(memory_space=pltpu.HBM)
SEM_SPEC = pl.BlockSpec(memory_space=pltpu.SEMAPHORE)
DATAFLOW = pltpu.SideEffectType.DATAFLOW_SIDE_EFFECTING


def _hbm(a):
    return pltpu.with_memory_space_constraint(a, pltpu.HBM)


def _hbm_like(arrs):
    return [pltpu.HBM(a.shape, a.dtype) for a in arrs]


def _ag_start(srcs, lands, after, name):
    n = len(srcs)
    ns = 8 * n

    def body(*refs):
        src, land = refs[:n], refs[n:2 * n]
        sems = refs[2 * n + 1:2 * n + 1 + ns]
        token = refs[-1]
        x, y, c, chips = _place()
        peers = [(x, y, 1 - c)] + [(*chip, c) for chip in chips]
        for t in range(n):
            for k, to in enumerate(peers):
                pltpu.make_async_remote_copy(
                    src_ref=src[t], dst_ref=land[t].at[4 * x + 2 * y + c],
                    send_sem=sems[2 * (4 * t + k)], recv_sem=sems[2 * (4 * t + k) + 1],
                    device_id=to, device_id_type=MESH).start()
        token[...] = jnp.zeros_like(token)

    res = pl.pallas_call(
        body, name=name,
        in_specs=[HBM_SPEC] * (2 * n) + [ANY],
        out_specs=[SEM_SPEC] * ns + [HBM_SPEC] * (2 * n) + [pl.BlockSpec(memory_space=pltpu.VMEM)],
        out_shape=[pltpu.SemaphoreType.DMA(())] * ns + _hbm_like(srcs) + _hbm_like(lands)
        + [jax.ShapeDtypeStruct((8, 128), F32)],
        input_output_aliases={i: ns + i for i in range(2 * n)},
        compiler_params=pltpu.CompilerParams(has_side_effects=DATAFLOW),
    )(*[_hbm(a) for a in srcs], *[_hbm(a) for a in lands], after)
    sems = [[(res[2 * (4 * t + k)], res[2 * (4 * t + k) + 1]) for k in range(4)] for t in range(n)]
    return sems, res[ns:ns + n], res[ns + n:ns + 2 * n], res[-1]


def _ag_forward(srcs, lands, sems1, after, name):
    n = len(srcs)
    flat1 = [s for t in range(n) for k in range(1, 4) for s in sems1[t][k]]
    n1 = len(flat1)

    def body(*refs):
        src, land = refs[:n], refs[n:2 * n]
        s1 = refs[2 * n:2 * n + n1]
        s2 = refs[2 * n + n1 + 1:2 * n + n1 + 1 + 6 * n]
        x, y, c, chips = _place()
        for j, (cx, cy) in enumerate(chips):
            for t in range(n):
                blk = land[t].at[4 * cx + 2 * cy + c]
                pltpu.make_async_remote_copy(
                    src_ref=src[t], dst_ref=blk, send_sem=s1[2 * (3 * t + j)], recv_sem=s1[2 * (3 * t + j) + 1],
                    device_id=(cx, cy, c), device_id_type=MESH).wait_recv()
                pltpu.make_async_remote_copy(
                    src_ref=blk, dst_ref=blk, send_sem=s2[2 * (3 * t + j)], recv_sem=s2[2 * (3 * t + j) + 1],
                    device_id=(x, y, 1 - c), device_id_type=MESH).start()

    res = pl.pallas_call(
        body, name=name,
        in_specs=[HBM_SPEC] * (2 * n) + [SEM_SPEC] * n1 + [ANY],
        out_specs=[SEM_SPEC] * (6 * n) + [HBM_SPEC] * n,
        out_shape=[pltpu.SemaphoreType.DMA(())] * (6 * n) + _hbm_like(lands),
        input_output_aliases={n + i: 6 * n + i for i in range(n)},
        compiler_params=pltpu.CompilerParams(has_side_effects=DATAFLOW),
    )(*srcs, *lands, *flat1, after)
    sems2 = [[(res[2 * (3 * t + j)], res[2 * (3 * t + j) + 1]) for j in range(3)] for t in range(n)]
    return sems2, res[6 * n:]


def _ag_finish(srcs, lands, sems1, sems2, after, name):
    n = len(srcs)
    flat1 = [s for t in range(n) for k in range(4) for s in sems1[t][k]]
    flat2 = [s for t in range(n) for j in range(3) for s in sems2[t][j]]
    n1, n2 = len(flat1), len(flat2)

    def body(*refs):
        src, land = refs[:n], refs[n:2 * n]
        s1 = refs[2 * n:2 * n + n1]
        s2 = refs[2 * n + n1:2 * n + n1 + n2]
        x, y, c, chips = _place()
        sib = (x, y, 1 - c)
        for t in range(n):
            own = land[t].at[4 * x + 2 * y + 1 - c]
            pltpu.make_async_remote_copy(
                src_ref=src[t], dst_ref=own, send_sem=s1[8 * t], recv_sem=s1[8 * t + 1],
                device_id=sib, device_id_type=MESH).wait_recv()
            for k in range(4):
                pltpu.make_async_remote_copy(
                    src_ref=src[t], dst_ref=own, send_sem=s1[2 * (4 * t + k)], recv_sem=s1[2 * (4 * t + k) + 1],
                    device_id=sib, device_id_type=MESH).wait_send()
            for j, (cx, cy) in enumerate(chips):
                blk = land[t].at[4 * cx + 2 * cy + 1 - c]
                cp = pltpu.make_async_remote_copy(
                    src_ref=blk, dst_ref=blk, send_sem=s2[2 * (3 * t + j)], recv_sem=s2[2 * (3 * t + j) + 1],
                    device_id=sib, device_id_type=MESH)
                cp.wait_send()
                cp.wait_recv()

    return pl.pallas_call(
        body, name=name,
        in_specs=[HBM_SPEC] * (2 * n) + [SEM_SPEC] * (n1 + n2) + [ANY],
        out_specs=[HBM_SPEC] * n,
        out_shape=_hbm_like(lands),
        input_output_aliases={n + i: i for i in range(n)},
        compiler_params=pltpu.CompilerParams(has_side_effects=DATAFLOW),
    )(*srcs, *lands, *flat1, *flat2, after)


def _pair_copies(srcs, dsts, sems):
    x, y, c, _ = _place()
    nt = len(srcs)
    return [pltpu.make_async_remote_copy(
        src_ref=srcs[t].at[2 * j + 1 - c], dst_ref=dsts[t].at[j],
        send_sem=sems[2 * (NCHIP * t + j)], recv_sem=sems[2 * (NCHIP * t + j) + 1],
        device_id=(x, y, 1 - c), device_id_type=MESH) for t in range(nt) for j in range(NCHIP)]


def _pair_start(grads, carry, name):
    nt = len(grads)
    ns = 2 * NCHIP * nt
    zones = [_hbm(lax.empty((NCHIP,) + a.shape[1:], a.dtype)) for a in grads]
    extra = [] if carry is None else [_hbm(carry)]
    ne = len(extra)

    def body(*refs):
        for cp in _pair_copies(refs[:nt], refs[nt:2 * nt], refs[2 * nt + ne:2 * nt + ne + ns]):
            cp.start()

    res = pl.pallas_call(
        body, name=name,
        in_specs=[HBM_SPEC] * (2 * nt + ne),
        out_specs=[SEM_SPEC] * ns + [HBM_SPEC] * (2 * nt + ne),
        out_shape=[pltpu.SemaphoreType.DMA(())] * ns + _hbm_like(grads) + _hbm_like(zones) + _hbm_like(extra),
        input_output_aliases={i: ns + i for i in range(2 * nt + ne)},
        compiler_params=pltpu.CompilerParams(has_side_effects=DATAFLOW),
    )(*[_hbm(a) for a in grads], *zones, *extra)
    handle = (list(res[:ns]), list(res[ns:ns + nt]), list(res[ns + nt:ns + 2 * nt]))
    return handle, (res[ns + 2 * nt] if ne else None)


def _pair_wait(handle, after, name):
    sems, srcs, zones = handle
    nt, ns = len(srcs), len(sems)

    def body(*refs):
        for cp in _pair_copies(refs[:nt], refs[nt:2 * nt], refs[2 * nt:2 * nt + ns]):
            cp.wait_send()
            cp.wait_recv()

    return pl.pallas_call(
        body, name=name,
        in_specs=[HBM_SPEC] * (2 * nt) + [SEM_SPEC] * ns + [ANY],
        out_specs=[HBM_SPEC] * nt,
        out_shape=_hbm_like(zones),
        input_output_aliases={nt + i: i for i in range(nt)},
        compiler_params=pltpu.CompilerParams(has_side_effects=DATAFLOW),
    )(*srcs, *zones, *sems, after)


def _rows_tile(r, row_bytes, cap_bytes):
    best = None
    for tr in range(16, r + 1, 16):
        if r % tr == 0 and tr * row_bytes <= cap_bytes:
            best = tr
    return best if best is not None else r


def _pair_sum(own, got, cidx, name):
    _, _, r, cdim = own.shape
    tr = _rows_tile(r, 2 * cdim, 2 * 1024 * 1024)

    def body(c_ref, a_ref, b_ref, o_ref):
        o_ref[...] = (a_ref[...].astype(F32) + b_ref[...].astype(F32)).astype(BF16)

    return pl.pallas_call(
        body, name=name,
        grid_spec=pltpu.PrefetchScalarGridSpec(
            num_scalar_prefetch=1, grid=(NCHIP, r // tr),
            in_specs=[pl.BlockSpec((None, None, tr, cdim), lambda j, i, c_ref: (j, c_ref[0], i, 0)),
                      pl.BlockSpec((None, tr, cdim), lambda j, i, c_ref: (j, i, 0))],
            out_specs=pl.BlockSpec((None, tr, cdim), lambda j, i, c_ref: (j, i, 0))),
        out_shape=jax.ShapeDtypeStruct((NCHIP, r, cdim), BF16),
        compiler_params=_params("arbitrary", "arbitrary"),
    )(cidx, own, got)


def _chip_copies(srcs, zones, slots, sems):
    x, y, c, chips = _place()
    out = []
    for t, (z, l) in enumerate(slots):
        for k, (cx, cy) in enumerate(chips):
            dst = zones[z].at[k] if l is None else zones[z].at[k, l]
            out.append(pltpu.make_async_remote_copy(
                src_ref=srcs[t].at[2 * cx + cy], dst_ref=dst,
                send_sem=sems[2 * (3 * t + k)], recv_sem=sems[2 * (3 * t + k) + 1],
                device_id=(cx, cy, c), device_id_type=MESH))
    return out


def _chip_start(sums, zones, slots, carry, name):
    nt, nz = len(sums), len(zones)
    ns = 6 * nt
    extra = [] if carry is None else [_hbm(carry)]
    ne = len(extra)

    def body(*refs):
        for cp in _chip_copies(refs[:nt], refs[nt:nt + nz], slots, refs[nt + nz + ne:nt + nz + ne + ns]):
            cp.start()

    res = pl.pallas_call(
        body, name=name,
        in_specs=[HBM_SPEC] * (nt + nz + ne),
        out_specs=[SEM_SPEC] * ns + [HBM_SPEC] * (nt + nz + ne),
        out_shape=[pltpu.SemaphoreType.DMA(())] * ns + _hbm_like(sums) + _hbm_like(zones) + _hbm_like(extra),
        input_output_aliases={i: ns + i for i in range(nt + nz + ne)},
        compiler_params=pltpu.CompilerParams(has_side_effects=DATAFLOW),
    )(*[_hbm(a) for a in sums], *zones, *extra)
    return (list(res[:ns]), list(res[ns:ns + nt]), list(res[ns + nt:ns + nt + nz]),
            (res[ns + nt + nz] if ne else None))


def _chip_wait(started, zones, zone_ids, after, name):
    started = [(sums, [(zone_ids.index(z), l) for z, l in slots], sems) for sums, slots, sems in started]
    nz = len(zones)
    flat_src = [a for sums, _, _ in started for a in sums]
    flat_sem = [s for _, _, sems in started for s in sems]
    n_src, n_sem = len(flat_src), len(flat_sem)

    def body(*refs):
        srcs, zs, sems = refs[:n_src], refs[n_src:n_src + nz], refs[n_src + nz:n_src + nz + n_sem]
        so, se = 0, 0
        for sums, slots, sem_list in started:
            for cp in _chip_copies(srcs[so:so + len(sums)], zs, slots, sems[se:se + len(sem_list)]):
                cp.wait_send()
                cp.wait_recv()
            so += len(sums)
            se += len(sem_list)

    return pl.pallas_call(
        body, name=name,
        in_specs=[HBM_SPEC] * (n_src + nz) + [SEM_SPEC] * n_sem + [ANY],
        out_specs=[HBM_SPEC] * nz,
        out_shape=_hbm_like(zones),
        input_output_aliases={n_src + i: i for i in range(nz)},
        compiler_params=pltpu.CompilerParams(has_side_effects=DATAFLOW),
    )(*flat_src, *zones, *flat_sem, after)


def _small_allreduce(parts, after, name):
    nt = len(parts)

    def body(*refs):
        srcs, outs, bufs = refs[:nt], refs[nt + 1:2 * nt + 1], refs[2 * nt + 1:3 * nt + 1]
        send_sems, recv_sems = refs[3 * nt + 1:]
        x, y, c, _ = _place()
        peers = [(x, y, 1 - c), (1 - x, y, c), (x, 1 - y, c)]
        for t in range(nt):
            outs[t][...] = srcs[t][...]
        for step, peer in enumerate(peers):
            copies = [pltpu.make_async_remote_copy(
                src_ref=outs[t], dst_ref=bufs[t].at[step],
                send_sem=send_sems.at[step, t], recv_sem=recv_sems.at[step, t],
                device_id=peer, device_id_type=MESH) for t in range(nt)]
            for cp in copies:
                cp.start()
            for cp in copies:
                cp.wait()
            for t in range(nt):
                outs[t][...] = outs[t][...] + bufs[t][step]

    vm = pl.BlockSpec(memory_space=pltpu.VMEM)
    return pl.pallas_call(
        body, name=name,
        in_specs=[vm] * nt + [ANY], out_specs=[vm] * nt,
        out_shape=[jax.ShapeDtypeStruct(a.shape, F32) for a in parts],
        scratch_shapes=[pltpu.VMEM((3,) + a.shape, F32) for a in parts]
        + [pltpu.SemaphoreType.DMA((3, nt)), pltpu.SemaphoreType.DMA((3, nt))],
        compiler_params=pltpu.CompilerParams(has_side_effects=True, vmem_limit_bytes=VMEM_LIMIT),
    )(*parts, after)


def _adam_math(w, g, m, v):
    m2 = ADAM_B1 * m + (1.0 - ADAM_B1) * g
    v2 = ADAM_B2 * v + (1.0 - ADAM_B2) * (g * g)
    m_hat = m2 / (1.0 - ADAM_B1 ** ADAM_STEP)
    v_hat = v2 / (1.0 - ADAM_B2 ** ADAM_STEP)
    delta = -ADAM_LR * (m_hat / (jnp.sqrt(v_hat) + ADAM_EPS) + ADAM_WD * w)
    return delta, m2, v2


def _adam_big(w, m, v, parts, mine, chip, name):
    nl, r, cdim = w.shape
    tr = _rows_tile(r, 4 * cdim, 3 * 512 * 1024)

    def body(c_ref, w_ref, m_ref, v_ref, p_ref, *rest):
        mine_refs, (g_ref, d_ref, mo_ref, vo_ref) = rest[:nl], rest[nl:]
        own = mine_refs[0][...]
        for l in range(1, nl):
            own = jnp.where(pl.program_id(0) == l, mine_refs[l][...], own)
        g = ((p_ref[0].astype(F32) + p_ref[1].astype(F32)) + p_ref[2].astype(F32)) + own.astype(F32)
        delta, m2, v2 = _adam_math(w_ref[...], g, m_ref[...], v_ref[...])
        g_ref[...] = g
        d_ref[...] = delta
        mo_ref[...] = m2
        vo_ref[...] = v2

    spec = pl.BlockSpec((None, tr, cdim), lambda l, i, c_ref: (l, i, 0))
    mine_specs = [pl.BlockSpec((None, tr, cdim), lambda l, i, c_ref, ll=ll: (c_ref[0], jnp.where(l == ll, i, 0), 0))
                  for ll in range(nl)]
    return pl.pallas_call(
        body, name=name,
        grid_spec=pltpu.PrefetchScalarGridSpec(
            num_scalar_prefetch=1, grid=(nl, r // tr),
            in_specs=[spec, spec, spec, pl.BlockSpec((3, None, tr, cdim), lambda l, i, c_ref: (0, l, i, 0))]
            + mine_specs,
            out_specs=[spec] * 4),
        out_shape=[jax.ShapeDtypeStruct(w.shape, F32)] * 4,
        compiler_params=_params("arbitrary", "arbitrary"),
    )(chip, w, m, v, parts, *mine)


def _adam_small(ws, gs, ms, vs, name):
    n = len(ws)

    def body(*refs):
        w_r, g_r, m_r, v_r = refs[:n], refs[n:2 * n], refs[2 * n:3 * n], refs[3 * n:4 * n]
        d_o, m_o, v_o = refs[4 * n:5 * n], refs[5 * n:6 * n], refs[6 * n:7 * n]
        for t in range(n):
            delta, m2, v2 = _adam_math(w_r[t][...], g_r[t][...], m_r[t][...], v_r[t][...])
            d_o[t][...] = delta
            m_o[t][...] = m2
            v_o[t][...] = v2

    vm = pl.BlockSpec(memory_space=pltpu.VMEM)
    shapes = [jax.ShapeDtypeStruct(a.shape, F32) for a in ws]
    return pl.pallas_call(
        body, name=name, in_specs=[vm] * (4 * n), out_specs=[vm] * (3 * n), out_shape=shapes * 3,
        compiler_params=pltpu.CompilerParams(vmem_limit_bytes=VMEM_LIMIT),
    )(*ws, *gs, *ms, *vs)


def kernel(x, norm_mix, norm_ffn, norm_final, ab_w_in, a_ln_g, a_ln_b, a_w_s, a_b_s, b_conv_w, b_conv_b, b_ln_g, b_ln_b, ab_w_out, c_w_in, c_conv_w, c_w_out, f_w_up, f_conv_w, f_w_down, loss_target, m_norm_mix, m_norm_ffn, m_norm_final, m_ab_w_in, m_a_ln_g, m_a_ln_b, m_a_w_s, m_a_b_s, m_b_conv_w, m_b_conv_b, m_b_ln_g, m_b_ln_b, m_ab_w_out, m_c_w_in, m_c_conv_w, m_c_w_out, m_f_w_up, m_f_conv_w, m_f_w_down, v_norm_mix, v_norm_ffn, v_norm_final, v_ab_w_in, v_a_ln_g, v_a_ln_b, v_a_w_s, v_a_b_s, v_b_conv_w, v_b_conv_b, v_b_ln_g, v_b_ln_b, v_ab_w_out, v_c_w_in, v_c_conv_w, v_c_w_out, v_f_w_up, v_f_conv_w, v_f_w_down):
    s = x.shape[1]
    x0 = x.reshape(s, D)
    tgt = loss_target.reshape(s, D)
    xi, yi, ci = lax.axis_index("x"), lax.axis_index("y"), lax.axis_index("c")
    dev = 4 * xi + 2 * yi + ci
    cidx = ci.astype(jnp.int32).reshape(1)

    bf = lambda a: a.astype(BF16)
    slab_w = 6 * CHUNK
    pad = lambda a, rows: jnp.pad(a, ((0, rows - a.shape[0]), (0, slab_w - a.shape[1])))
    slab = jnp.concatenate([pad(b_conv_w[0], 32), pad(c_conv_w[0], 8), pad(f_conv_w.reshape(6, FB), 8)], axis=0)
    later = [bf(ab_w_in[0]), bf(ab_w_out[0]), slab, bf(f_w_up[0]), bf(f_w_down[0]), bf(c_w_in[0]), bf(c_w_out[0]),
             bf(f_w_up[1]), bf(f_w_down[1])]
    lands = [_zone(a, dev) for a in later]
    groups = [[0], [1, 2], [3, 4], [5, 6], [7, 8]]
    ag_sems, later, lands, ag_token = _ag_start(later, lands, x0, "ag_start")

    causal = jnp.tril(jnp.ones((CHUNK, CHUNK), F32))
    wsm = (a_w_s[0] * causal).astype(BF16)
    bs_col = a_b_s.reshape(HEADS, CHUNK, 1)
    nm = [norm_mix[0:1], norm_mix[1:2]]
    nf = [norm_ffn[0:1], norm_ffn[1:2]]
    nfin = norm_final.reshape(1, D)

    def arrive(g, after_ici, after_d2d, tag):
        srcs = [later[t] for t in groups[g]]
        zone = [lands[t] for t in groups[g]]
        sems1 = [ag_sems[t] for t in groups[g]]
        sems2, zone = _ag_forward(srcs, zone, sems1, after_ici, "ag_forward_" + tag)
        return _ag_finish(srcs, zone, sems1, sems2, after_d2d, "ag_finish_" + tag)

    h0 = _rms_fwd(x0, nm[0], "rms_mix0", after=ag_token)
    (win0,) = arrive(0, h0, h0, "w_in")
    z = _mm_in(h0, win0, "mm_ab_in")
    wout0, slab_g = arrive(1, z, z, "first")
    wout0 = wout0.reshape(D, D)
    bcw = jnp.transpose(slab_g[:, 0:BCONV, 0:DA // NDEV], (1, 0, 2)).reshape(BCONV, DA)
    ccw = jnp.transpose(slab_g[:, 32:35, 0:D // NDEV], (1, 0, 2)).reshape(3, D)
    fcw_g = slab_g[:, 40:46, 0:FB].reshape(2, NG, 2, 3, FB)
    fcws = [fcw_g[:, :, 0], fcw_g[:, :, 1]]
    ycat, yb2 = _ab_fwd(z, a_ln_g, a_ln_b, wsm, bs_col, bcw, b_conv_b, b_ln_g, b_ln_b, "ab_fwd")
    x1, h1 = _mm_out(ycat, wout0, x0, nf[0], "mm_ab_out")
    wup0, wdn0 = arrive(2, x1, x1, "ffn0")
    up0, upc0, x2, h2 = _ffn_fwd(h1, x1, wup0.reshape(2, NG, D, FB), fcws[0], wdn0.reshape(DFF, D), nm[1],
                                 "ffn_fwd0")
    cin, cout = arrive(3, x2, x2, "c")
    cout = cout.reshape(D, D)
    zc = _mm_in(h2, cin, "mm_c_in")
    rc = _c_fwd(zc, ccw, "c_fwd")
    x3, h3 = _mm_out(rc, cout, x2, nf[1], "mm_c_out")
    wup1, wdn1 = arrive(4, rc, x3, "ffn1")
    wups = [wup0.reshape(2, NG, D, FB), wup1.reshape(2, NG, D, FB)]
    wdns = [wdn0.reshape(DFF, D), wdn1.reshape(DFF, D)]
    up1, upc1, x4 = _ffn_fwd(h3, x3, wups[1], fcws[1], wdns[1], None, "ffn_fwd1")
    dx4, dx4b, dnfin, loss_part = _final(x4, tgt, nfin, "final_loss")

    zshape = lambda *sh: _hbm(lax.empty((3,) + sh, BF16))
    zones = [zshape(D, 2 * D // NDEV), zshape(D // NDEV, D), zshape(D, 3 * D // NDEV), zshape(D // NDEV, D),
             zshape(2, FB, D), zshape(2, DFF // NDEV, D)]
    started = []

    def pair_sums(grads, handle, after, tag):
        del grads
        got = _pair_wait(handle, after, "rs_pair_wait_" + tag)
        return [_pair_sum(b.reshape((NCHIP, 2) + b.shape[1:]), g, cidx, "rs_pair_sum_%s%d" % (tag, t))
                for t, (b, g) in enumerate(zip(handle[1], got))]

    def chip_start(sums, slots, carry, tag):
        sems, sums, new_zones, carry = _chip_start(sums, zones, slots, carry, "rs_chip_start_" + tag)
        zones[:] = new_zones
        started.append((sums, slots, sems))
        return sums, carry

    rows8 = lambda g, r: g.reshape(NDEV, r, D)
    a1, dup1, dx3, dx3b, dnf1, dfcw1 = _ffn_bwd(dx4, up1, upc1, wups[1], fcws[1], wdns[1], x3, nf[1], "ffn_bwd1")
    g_f1 = [_dw_up(h3, dup1, "dw_up1"), rows8(_dw_dn(a1, dx4b, "dw_dn1"), DFF // NDEV)]
    hd_f1, dx3b = _pair_start(g_f1, dx3b, "rs_pair_start_f1")
    drc = _mm_nt(dx3b, cout, "mm_c_out_bwd")
    g_cout = rows8(_dw_rows(rc, dx3b, "dw_c_out"), D // NDEV)
    s_f1 = pair_sums(g_f1, hd_f1, g_cout, "f1")
    s_f1, drc = chip_start(s_f1, [(4, 1), (5, 1)], drc, "f1")
    dzc, dccw = _c_bwd(drc, zc, ccw, "c_bwd")
    dx2, dx2b, dnm1 = _mm_nt_rms(dzc, cin, x2, nm[1], dx3, True, "mm_c_in_bwd")
    g_c = [_dw_cols(h2, dzc, NDEV, 3 * D // NDEV, "dw_c_in"), g_cout]
    hd_c, dx2 = _pair_start(g_c, dx2, "rs_pair_start_c")
    a0, dup0, dx1, dx1b, dnf0, dfcw0 = _ffn_bwd(dx2, up0, upc0, wups[0], fcws[0], wdns[0], x1, nf[0], "ffn_bwd0")
    s_c = pair_sums(g_c, hd_c, dx1b, "c")
    s_c, dx1b = chip_start(s_c, [(2, None), (3, None)], dx1b, "c")
    g_f0 = [_dw_up(h1, dup0, "dw_up0"), rows8(_dw_dn(a0, dx2b, "dw_dn0"), DFF // NDEV)]
    hd_f0, dx1b = _pair_start(g_f0, dx1b, "rs_pair_start_f0")
    dycat = _mm_nt(dx1b, wout0, "mm_ab_out_bwd")
    g_wout0 = rows8(_dw_rows(ycat, dx1b, "dw_ab_out"), D // NDEV)
    s_f0 = pair_sums(g_f0, hd_f0, g_wout0, "f0")
    s_f0, dycat = chip_start(s_f0, [(4, 0), (5, 0)], dycat, "f0")
    dz, g512, dws, dbs = _ab_bwd(dycat, z, yb2, a_ln_g, a_ln_b, wsm, bs_col, bcw, b_ln_g, b_ln_b, "ab_bwd")
    grad_x, dnm0 = _mm_nt_rms(dz, win0, x0, nm[0], dx1, False, "mm_ab_in_bwd")
    g_ab = [_dw_cols(h0, dz, NDEV, 2 * D // NDEV, "dw_ab_in"), g_wout0]
    hd_ab, _ = _pair_start(g_ab, None, "rs_pair_start_ab")

    g1024 = jnp.concatenate([dnm0, dnm1, dnf0, dnf1, dnfin, dccw], axis=0)
    gfc = jnp.concatenate([dfcw0, dfcw1], axis=0).reshape(2 * NG * 2 * 3, FB)
    g1024, g512, dws, dbs, gfc, loss_sum = _small_allreduce(
        [g1024, g512, dws.reshape(HEADS * CHUNK, CHUNK), dbs.reshape(HEADS, CHUNK), gfc, loss_part], hd_ab[1][0],
        "small_allreduce")
    loss = loss_sum[0, 0]
    s_ab = pair_sums(g_ab, hd_ab, g1024, "ab")
    s_ab, _ = chip_start(s_ab, [(0, None), (1, None)], None, "ab")
    p_cin, p_cout, p_wup, p_wdn = _chip_wait(started[:3], zones[2:], [2, 3, 4, 5], s_ab[0], "rs_chip_wait_early")

    chip = (2 * xi + yi).astype(jnp.int32).reshape(1)

    def big_update(w, m, v, parts, mine, name):
        shp = w.shape
        w3, m3, v3 = (a.reshape((-1,) + shp[-2:]) for a in (w, m, v))
        p4 = parts.reshape((3,) + w3.shape)
        return [o.reshape(shp) for o in _adam_big(w3, m3, v3, p4, mine, chip, name)]

    u_cin = big_update(c_w_in, m_c_w_in, v_c_w_in, p_cin, [s_c[0]], "adam_c_w_in")
    u_cout = big_update(c_w_out, m_c_w_out, v_c_w_out, p_cout, [s_c[1]], "adam_c_w_out")
    tr_ = lambda a: jnp.swapaxes(a, 1, 2)
    u_wup = [tr_(o) for o in big_update(tr_(f_w_up), tr_(m_f_w_up), tr_(v_f_w_up), p_wup,
                                        [s_f0[0], s_f1[0]], "adam_f_w_up")]
    u_wdn = big_update(f_w_down, m_f_w_down, v_f_w_down, p_wdn, [s_f0[1], s_f1[1]], "adam_f_w_down")
    p_win0, p_wout0 = _chip_wait(started[3:], zones[:2], [0, 1], u_wdn[0], "rs_chip_wait_late")
    u_win0 = big_update(ab_w_in, m_ab_w_in, v_ab_w_in, p_win0, [s_ab[0]], "adam_ab_w_in")
    u_wout0 = big_update(ab_w_out, m_ab_w_out, v_ab_w_out, p_wout0, [s_ab[1]], "adam_ab_w_out")

    g_norm_mix = g1024[0:2]
    g_norm_ffn = g1024[2:4]
    g_norm_final = g1024[4:5]
    g_ccw = lax.dynamic_slice(g1024[5:8], (0, dev * (D // NDEV)), (3, D // NDEV))
    g_bcw = lax.dynamic_slice(g512[8:8 + BCONV], (0, dev * (DA // NDEV)), (BCONV, DA // NDEV))
    gfc = gfc.reshape(2, NG, 2, 3, FB)
    g_fcw = lax.dynamic_slice(gfc, (0, dev % NG, dev // NG, 0, 0), (2, 1, 1, 3, FB)).reshape(2, 3, FB)
    small_w = [norm_mix, norm_ffn, nfin, a_ln_g, a_ln_b, a_w_s[0], a_b_s[0], b_conv_w[0], b_conv_b,
               b_ln_g, b_ln_b, c_conv_w[0], f_conv_w]
    small_g = [g_norm_mix, g_norm_ffn, g_norm_final, g512[0:1], g512[1:2],
               dws.reshape(HEADS, CHUNK, CHUNK), dbs, g_bcw, g512[2:3],
               g512[3:4], g512[4:5], g_ccw, g_fcw]
    small_m = [m_norm_mix, m_norm_ffn, m_norm_final.reshape(1, D), m_a_ln_g, m_a_ln_b, m_a_w_s[0], m_a_b_s[0],
               m_b_conv_w[0], m_b_conv_b, m_b_ln_g, m_b_ln_b, m_c_conv_w[0], m_f_conv_w]
    small_v = [v_norm_mix, v_norm_ffn, v_norm_final.reshape(1, D), v_a_ln_g, v_a_ln_b, v_a_w_s[0], v_a_b_s[0],
               v_b_conv_w[0], v_b_conv_b, v_b_ln_g, v_b_ln_b, v_c_conv_w[0], v_f_conv_w]
    upd = _adam_small(small_w, small_g, small_m, small_v, "adam_small")
    ns = len(small_w)
    orig = [norm_mix, norm_ffn, norm_final, a_ln_g, a_ln_b, a_w_s, a_b_s, b_conv_w, b_conv_b,
            b_ln_g, b_ln_b, c_conv_w, f_conv_w]
    sg_out = [g.reshape(o.shape) for g, o in zip(small_g, orig)]
    sd_out = [a.reshape(o.shape) for a, o in zip(upd[0:ns], orig)]
    sm_out = [a.reshape(o.shape) for a, o in zip(upd[ns:2 * ns], orig)]
    sv_out = [a.reshape(o.shape) for a, o in zip(upd[2 * ns:3 * ns], orig)]

    def assemble(small, k):
        return [small[0], small[1], small[2], u_win0[k], small[3], small[4], small[5], small[6], small[7],
                small[8], small[9], small[10], u_wout0[k], u_cin[k], small[11], u_cout[k], u_wup[k],
                small[12], u_wdn[k]]

    grads = assemble(sg_out, 0)
    deltas = assemble(sd_out, 1)
    new_m = assemble(sm_out, 2)
    new_v = assemble(sv_out, 3)
    return (loss, grad_x.reshape(1, s, D), *grads, *deltas, *new_m, *new_v)
```

```python
import functools
import math

import jax
import jax.numpy as jnp
from jax import lax
from jax.experimental import pallas as pl
from jax.experimental.pallas import tpu as pltpu

F32 = jnp.float32
BF16 = jnp.bfloat16

D = 1024
DA = 512
HEADS = 4
CHUNK = 128
DFF = 2816
NDEV = 8
NCHIP = 4
FB = DFF * 2 // NDEV
NG = DFF // FB
BCONV = 31
EPS = 1e-6
HALO = 16
HALO_B = 32
RC = 16
NPART = 2
VMEM_LIMIT = 52 * 1024 * 1024
INV_SQRT2 = 1.0 / math.sqrt(2.0)
INV_SQRT_2PI = 1.0 / math.sqrt(2.0 * math.pi)

ADAM_LR = 0.001
ADAM_B1 = 0.9
ADAM_B2 = 0.999
ADAM_EPS = 1e-08
ADAM_WD = 0.01
ADAM_STEP = 10

MESH = pl.DeviceIdType.MESH
ANY = pl.BlockSpec(memory_space=pl.ANY)
NT_DIMS = (((1,), (1,)), ((), ()))
TN_DIMS = (((0,), (0,)), ((), ()))


def _params(*sem):
    return pltpu.CompilerParams(dimension_semantics=sem, vmem_limit_bytes=VMEM_LIMIT)


def _tile(s, want):
    return min(want, s)


def _sigmoid(x):
    return jax.nn.sigmoid(x)


def _dsilu(x, sg):
    return sg * (1.0 + x * (1.0 - sg))


def _gelu(x):
    return 0.5 * x * (1.0 + lax.erf(x * INV_SQRT2))


def _dgelu(x):
    return 0.5 * (1.0 + lax.erf(x * INV_SQRT2)) + x * jnp.exp(-0.5 * x * x) * INV_SQRT_2PI


def _ln_fwd(x, g, b):
    mu = jnp.mean(x, axis=-1, keepdims=True)
    xc = x - mu
    var = jnp.mean(xc * xc, axis=-1, keepdims=True)
    rstd = lax.rsqrt(var + EPS)
    xhat = xc * rstd
    return xhat * g + b, xhat, rstd


def _ln_bwd(dy, xhat, rstd, g):
    dxh = dy * g
    m1 = jnp.mean(dxh, axis=-1, keepdims=True)
    m2 = jnp.mean(dxh * xhat, axis=-1, keepdims=True)
    return rstd * (dxh - m1 - xhat * m2)


def _rms_bwd_math(dh, x, g):
    r = lax.rsqrt(jnp.mean(x * x, axis=-1, keepdims=True) + EPS)
    xhat = x * r
    dg = jnp.sum(dh * xhat, axis=0, keepdims=True)
    u = dh * g
    dx = r * (u - xhat * jnp.mean(u * xhat, axis=-1, keepdims=True))
    return dx, dg


def _conv3(xe, cw, halo):
    x0 = xe[halo:]
    x1 = pltpu.roll(xe, 1, 0)[halo:]
    x2 = pltpu.roll(xe, 2, 0)[halo:]
    return cw[2] * x0 + cw[1] * x1 + cw[0] * x2, (x0, x1, x2)


def _conv3_bwd_in(dce, cw, ts):
    n = dce.shape[0]
    d1 = pltpu.roll(dce, n - 1, 0)[:ts]
    d2 = pltpu.roll(dce, n - 2, 0)[:ts]
    return cw[2] * dce[:ts] + cw[1] * d1 + cw[0] * d2


def _conv3_bwd_w(dc, taps):
    x0, x1, x2 = taps
    return [jnp.sum(dc * x2, axis=0, keepdims=True), jnp.sum(dc * x1, axis=0, keepdims=True),
            jnp.sum(dc * x0, axis=0, keepdims=True)]


def _rms_fwd(x, g, name, after=None):
    s = x.shape[0]
    ts = _tile(s, 512)

    def body(x_ref, g_ref, *rest):
        h_ref = rest[-1]
        xv = x_ref[...]
        r = lax.rsqrt(jnp.mean(xv * xv, axis=-1, keepdims=True) + EPS)
        h_ref[...] = (xv * r * g_ref[...]).astype(BF16)

    extra = [] if after is None else [after]
    return pl.pallas_call(
        body, grid=(s // ts,), name=name,
        in_specs=[pl.BlockSpec((ts, D), lambda i: (i, 0)), pl.BlockSpec((1, D), lambda i: (0, 0))]
        + [ANY] * len(extra),
        out_specs=pl.BlockSpec((ts, D), lambda i: (i, 0)),
        out_shape=jax.ShapeDtypeStruct((s, D), BF16),
        compiler_params=_params("parallel"),
    )(x, g, *extra)


MXU_COLS = 256


def _pair(bn):
    return 1 if bn % MXU_COLS == 0 else 2


def _cols(w_ref, b, pair):
    return w_ref[b] if pair == 1 else jnp.concatenate([w_ref[b + q] for q in range(pair)], axis=1)


def _mm_in(h, wblk, name):
    s = h.shape[0]
    nb, _, bn = wblk.shape
    pair = _pair(bn)
    ts = _tile(s, 512)

    def body(h_ref, w_ref, o_ref):
        hv = h_ref[...]
        for b in range(0, nb, pair):
            o_ref[:, b * bn:(b + pair) * bn] = jnp.dot(hv, _cols(w_ref, b, pair),
                                                       preferred_element_type=F32).astype(BF16)

    return pl.pallas_call(
        body, grid=(s // ts,), name=name,
        in_specs=[pl.BlockSpec((ts, D), lambda i: (i, 0)), pl.BlockSpec((nb, D, bn), lambda i: (0, 0, 0))],
        out_specs=pl.BlockSpec((ts, nb * bn), lambda i: (i, 0)),
        out_shape=jax.ShapeDtypeStruct((s, nb * bn), BF16),
        compiler_params=_params("parallel"),
    )(h, wblk)


def _rms_math(xv, g):
    r = lax.rsqrt(jnp.mean(xv * xv, axis=-1, keepdims=True) + EPS)
    return (xv * r * g).astype(BF16)


def _mm_out(y, w, xres, gnext, name):
    s = y.shape[0]
    ts = _tile(s, 512)

    def body(y_ref, w_ref, x_ref, g_ref, o_ref, h_ref):
        xn = x_ref[...] + jnp.dot(y_ref[...], w_ref[...], preferred_element_type=F32)
        o_ref[...] = xn
        h_ref[...] = _rms_math(xn, g_ref[...])

    return pl.pallas_call(
        body, grid=(s // ts,), name=name,
        in_specs=[pl.BlockSpec((ts, D), lambda i: (i, 0)), pl.BlockSpec((D, D), lambda i: (0, 0)),
                  pl.BlockSpec((ts, D), lambda i: (i, 0)), pl.BlockSpec((1, D), lambda i: (0, 0))],
        out_specs=[pl.BlockSpec((ts, D), lambda i: (i, 0)), pl.BlockSpec((ts, D), lambda i: (i, 0))],
        out_shape=[jax.ShapeDtypeStruct((s, D), F32), jax.ShapeDtypeStruct((s, D), BF16)],
        compiler_params=_params("parallel"),
    )(y, w, xres, gnext)


CONV_ROWS = 32


def _rolled_copies(dst_ref, xe, back):
    n = xe.shape[0]
    dst_ref[0] = xe
    for r in range(1, 8):
        dst_ref[r] = pltpu.roll(xe, n - r if back else r, 0)


def _conv31(rolled_ref, cw_ref, ts, out_ref, bias):
    for o in range(0, ts, CONV_ROWS):
        acc = jnp.zeros((CONV_ROWS, DA), F32) + bias
        for sh in range(BCONV):
            q, r = divmod(sh, 8)
            lo = HALO_B - 8 * q + o
            acc = acc + cw_ref[BCONV - 1 - sh:BCONV - sh, :] * rolled_ref[r, lo:lo + CONV_ROWS, :]
        out_ref[o:o + CONV_ROWS, :] = acc


def _ab_fwd(z, lga, lba, wsm, bs_col, cwb, cbb, lgb, lbb, name):
    s = z.shape[0]
    ts = _tile(s, 256)
    hb = ts // HALO_B

    def body(z_ref, zh_ref, lga_ref, lba_ref, ws_ref, bs_ref, cw_ref, cb_ref, lgb_ref, lbb_ref,
             y_ref, yb2_ref, rolled):
        i = pl.program_id(0)
        z_t = z_ref[...].astype(F32)
        gu = _gelu(z_t[:, 0:DA])
        gv = _gelu(z_t[:, DA:2 * DA])
        vn, _, _ = _ln_fwd(gv, lga_ref[...], lba_ref[...])
        vnb = vn.astype(BF16)
        for c in range(ts // CHUNK):
            for h in range(HEADS):
                rs = slice(c * CHUNK, (c + 1) * CHUNK)
                cs = slice(h * CHUNK, (h + 1) * CHUNK)
                mixed = jnp.dot(ws_ref[h], vnb[rs, cs], preferred_element_type=F32) + bs_ref[h]
                y_ref[rs, cs] = (gu[rs, cs] * mixed).astype(BF16)
        zh = jnp.where(i > 0, zh_ref[...], jnp.zeros_like(zh_ref[...])).astype(F32)
        xb = jnp.concatenate([zh[:, 0:DA], z_t[:, 2 * DA:3 * DA]], axis=0)
        gb = jnp.concatenate([zh[:, DA:2 * DA], z_t[:, 3 * DA:4 * DA]], axis=0)
        _rolled_copies(rolled, xb * _sigmoid(gb), False)
        _conv31(rolled, cw_ref, ts, yb2_ref, cb_ref[...])
        nb_, _, _ = _ln_fwd(yb2_ref[...], lgb_ref[...], lbb_ref[...])
        y_ref[:, DA:2 * DA] = (nb_ * _sigmoid(nb_)).astype(BF16)

    row = lambda i: (0, 0)
    return pl.pallas_call(
        body, grid=(s // ts,), name=name,
        in_specs=[pl.BlockSpec((ts, 4 * DA), lambda i: (i, 0)),
                  pl.BlockSpec((HALO_B, 2 * DA), lambda i: (jnp.maximum(i * hb - 1, 0), 1)),
                  pl.BlockSpec((1, DA), row), pl.BlockSpec((1, DA), row),
                  pl.BlockSpec((HEADS, CHUNK, CHUNK), lambda i: (0, 0, 0)),
                  pl.BlockSpec((HEADS, CHUNK, 1), lambda i: (0, 0, 0)),
                  pl.BlockSpec((BCONV, DA), row), pl.BlockSpec((1, DA), row),
                  pl.BlockSpec((1, DA), row), pl.BlockSpec((1, DA), row)],
        out_specs=[pl.BlockSpec((ts, 2 * DA), lambda i: (i, 0)), pl.BlockSpec((ts, DA), lambda i: (i, 0))],
        out_shape=[jax.ShapeDtypeStruct((s, 2 * DA), BF16), jax.ShapeDtypeStruct((s, DA), F32)],
        scratch_shapes=[pltpu.VMEM((8, ts + HALO_B, DA), F32)],
        compiler_params=_params("parallel"),
    )(z, z, lga, lba, wsm, bs_col, cwb, cbb, lgb, lbb)


def _c_fwd(zc, cw, name):
    s = zc.shape[0]
    ts = _tile(s, 512)
    hb = ts // HALO

    def body(z_ref, ch_ref, xh_ref, cw_ref, r_ref):
        i = pl.program_id(0)
        z_t = z_ref[...].astype(F32)
        ph = jnp.where(i > 0, ch_ref[...].astype(F32) * xh_ref[...].astype(F32), 0.0)
        pe = jnp.concatenate([ph, z_t[:, D:2 * D] * z_t[:, 2 * D:3 * D]], axis=0)
        q, _ = _conv3(pe, [cw_ref[k:k + 1, :] for k in range(3)], HALO)
        r_ref[...] = (z_t[:, 0:D] * q).astype(BF16)

    halo = lambda col: pl.BlockSpec((HALO, D), lambda i: (jnp.maximum(i * hb - 1, 0), col))
    return pl.pallas_call(
        body, grid=(s // ts,), name=name,
        in_specs=[pl.BlockSpec((ts, 3 * D), lambda i: (i, 0)), halo(1), halo(2),
                  pl.BlockSpec((3, D), lambda i: (0, 0))],
        out_specs=pl.BlockSpec((ts, D), lambda i: (i, 0)),
        out_shape=jax.ShapeDtypeStruct((s, D), BF16),
        compiler_params=_params("parallel"),
    )(zc, zc, zc, cw)


def _ffn_fwd(h, xres, wup, fcw, wdn, gnext, name):
    s = h.shape[0]
    ts = _tile(s, 512)
    hb = ts // HALO

    def body(h_ref, hh_ref, w_ref, cw_ref, wd_ref, x_ref, *rest):
        if gnext is not None:
            gn_ref, up_ref, upc_ref, xo_ref, hn_ref = rest
        else:
            up_ref, upc_ref, xo_ref = rest
        i = pl.program_id(0)
        m = pl.program_id(1)
        @pl.when(m == 0)
        def _():
            xo_ref[...] = x_ref[...]

        halo = jnp.where(i > 0, hh_ref[...], jnp.zeros_like(hh_ref[...]))
        hx = jnp.concatenate([halo, h_ref[...]], axis=0)
        acts = []
        for gv in range(2):
            up = jnp.dot(hx, w_ref[gv], preferred_element_type=F32)
            up_ref[gv] = up[HALO:].astype(BF16)
            upc, _ = _conv3(up, [cw_ref[gv, k:k + 1, :] for k in range(3)], HALO)
            upc_ref[gv] = upc.astype(BF16)
            acts.append(upc)
        a = acts[0] * _sigmoid(acts[0]) * acts[1]
        xo_ref[...] += jnp.dot(a.astype(BF16), wd_ref[...], preferred_element_type=F32)

        if gnext is not None:
            @pl.when(m == NG - 1)
            def _():
                hn_ref[...] = _rms_math(xo_ref[...], gn_ref[...])

    tile = pl.BlockSpec((ts, D), lambda i, m: (i, 0))
    nxt = gnext is not None
    return pl.pallas_call(
        body, grid=(s // ts, NG), name=name,
        in_specs=[tile,
                  pl.BlockSpec((HALO, D), lambda i, m: (jnp.maximum(i * hb - 1, 0), 0)),
                  pl.BlockSpec((2, None, D, FB), lambda i, m: (0, m, 0, 0)),
                  pl.BlockSpec((2, None, 3, FB), lambda i, m: (0, m, 0, 0)),
                  pl.BlockSpec((FB, D), lambda i, m: (m, 0)),
                  tile] + ([pl.BlockSpec((1, D), lambda i, m: (0, 0))] if nxt else []),
        out_specs=[pl.BlockSpec((None, 2, ts, FB), lambda i, m: (m, 0, i, 0)),
                   pl.BlockSpec((None, 2, ts, FB), lambda i, m: (m, 0, i, 0)),
                   tile] + ([tile] if nxt else []),
        out_shape=[jax.ShapeDtypeStruct((NG, 2, s, FB), BF16), jax.ShapeDtypeStruct((NG, 2, s, FB), BF16),
                   jax.ShapeDtypeStruct((s, D), F32)] + ([jax.ShapeDtypeStruct((s, D), BF16)] if nxt else []),
        compiler_params=_params("arbitrary", "arbitrary"),
    )(h, h, wup, fcw, wdn, xres, *([gnext] if nxt else []))


def _final(x, tgt, g, name):
    s = x.shape[0]
    ts = _tile(s, 512)

    def body(x_ref, t_ref, g_ref, dx_ref, dxb_ref, dg_ref, loss_ref):
        i = pl.program_id(0)
        xv = x_ref[...]
        gv = g_ref[...]
        r = lax.rsqrt(jnp.mean(xv * xv, axis=-1, keepdims=True) + EPS)
        xhat = xv * r
        e = xhat * gv - t_ref[...]
        part = 0.5 * jnp.sum(jnp.mean(e * e, axis=-1, keepdims=True), axis=0, keepdims=True)
        dy = e * (1.0 / D)
        dgp = jnp.sum(dy * xhat, axis=0, keepdims=True)
        u = dy * gv
        dx = r * (u - xhat * jnp.mean(u * xhat, axis=-1, keepdims=True))
        dx_ref[...] = dx
        dxb_ref[...] = dx.astype(BF16)

        @pl.when(i == 0)
        def _():
            dg_ref[...] = dgp
            loss_ref[...] = jnp.broadcast_to(part, (1, 128))

        @pl.when(i > 0)
        def _():
            dg_ref[...] += dgp
            loss_ref[...] += jnp.broadcast_to(part, (1, 128))

    return pl.pallas_call(
        body, grid=(s // ts,), name=name,
        in_specs=[pl.BlockSpec((ts, D), lambda i: (i, 0)), pl.BlockSpec((ts, D), lambda i: (i, 0)),
                  pl.BlockSpec((1, D), lambda i: (0, 0))],
        out_specs=[pl.BlockSpec((ts, D), lambda i: (i, 0)), pl.BlockSpec((ts, D), lambda i: (i, 0)),
                   pl.BlockSpec((1, D), lambda i: (0, 0)), pl.BlockSpec((1, 128), lambda i: (0, 0))],
        out_shape=[jax.ShapeDtypeStruct((s, D), F32), jax.ShapeDtypeStruct((s, D), BF16),
                   jax.ShapeDtypeStruct((1, D), F32), jax.ShapeDtypeStruct((1, 128), F32)],
        compiler_params=_params("arbitrary"),
    )(x, tgt, g)


def _ffn_bwd(df, up, upc, wup, fcw, wdn, xin, g, name):
    s = df.shape[0]
    ts = _tile(s, 512)
    nt = s // ts

    def body(df_ref, up_ref, upc_ref, w_ref, cw_ref, wd_ref, x_ref, g_ref,
             a_ref, dup_ref, dx_ref, dxb_ref, dg_ref, dcw_ref, carry, acc, tacc):
        i = pl.program_id(0)
        m = pl.program_id(1)
        first = i == 0
        @pl.when(first)
        def _():
            carry[m] = jnp.zeros((2, 8, FB), F32)
            dcw_ref[m] = jnp.zeros((2, 3, FB), F32)

        @pl.when(m == 0)
        def _():
            acc[...] = jnp.zeros((ts, D), F32)

        cws = [[cw_ref[gv, k:k + 1, :] for k in range(3)] for gv in range(2)]
        part = ts // NPART
        das = [lax.dot_general(df_ref[p * part:(p + 1) * part, :].astype(BF16), wd_ref[...], NT_DIMS,
                               preferred_element_type=F32) for p in range(NPART)]

        tacc[...] = jnp.zeros((2, 3, 8, FB), F32)
        nxt = [carry[m, 0], carry[m, 1]]
        for r in reversed(range(ts // RC)):
            rs = slice(r * RC, (r + 1) * RC)
            gate = upc_ref[0, rs, :].astype(F32)
            val = upc_ref[1, rs, :].astype(F32)
            sg = _sigmoid(gate)
            sl = gate * sg
            a_ref[rs, :] = (sl * val).astype(BF16)
            da_c = das[(r * RC) // part][(r * RC) % part:(r * RC) % part + RC]
            dcs = [da_c * val * _dsilu(gate, sg), da_c * sl]
            for gv in range(2):
                dc = dcs[gv]
                dce = jnp.concatenate([dc, nxt[gv]], axis=0)
                d1 = pltpu.roll(dce, RC + 7, 0)[:RC]
                d2 = pltpu.roll(dce, RC + 6, 0)[:RC]
                du = cws[gv][2] * dc + cws[gv][1] * d1 + cws[gv][0] * d2
                dup_ref[gv, rs, :] = du.astype(BF16)
                x0 = up_ref[gv, rs, :].astype(F32)
                for k, dk in enumerate((d2, d1, dc)):
                    p = x0 * dk
                    tacc[gv, k] += p[0:8] + p[8:16]
                nxt[gv] = dc[0:8]
            if (r * RC) % part == 0:
                ps = slice(r * RC, r * RC + part)
                acc[ps, :] += (
                    lax.dot_general(dup_ref[0, ps, :], w_ref[0], NT_DIMS, preferred_element_type=F32)
                    + lax.dot_general(dup_ref[1, ps, :], w_ref[1], NT_DIMS, preferred_element_type=F32))
        for gv in range(2):
            carry[m, gv] = nxt[gv]
            for k in range(3):
                dcw_ref[m, gv, k:k + 1, :] += jnp.sum(tacc[gv, k], axis=0, keepdims=True)

        @pl.when(m == NG - 1)
        def _():
            dx, dgp = _rms_bwd_math(acc[...], x_ref[...], g_ref[...])
            dx = df_ref[...] + dx
            dx_ref[...] = dx
            dxb_ref[...] = dx.astype(BF16)

            @pl.when(first)
            def _():
                dg_ref[...] = dgp

            @pl.when(jnp.logical_not(first))
            def _():
                dg_ref[...] += dgp

    rev = lambda i: nt - 1 - i
    return pl.pallas_call(
        body, grid=(nt, NG), name=name,
        in_specs=[pl.BlockSpec((ts, D), lambda i, m: (rev(i), 0)),
                  pl.BlockSpec((None, 2, ts, FB), lambda i, m: (m, 0, rev(i), 0)),
                  pl.BlockSpec((None, 2, ts, FB), lambda i, m: (m, 0, rev(i), 0)),
                  pl.BlockSpec((2, None, D, FB), lambda i, m: (0, m, 0, 0)),
                  pl.BlockSpec((2, None, 3, FB), lambda i, m: (0, m, 0, 0)),
                  pl.BlockSpec((FB, D), lambda i, m: (m, 0)),
                  pl.BlockSpec((ts, D), lambda i, m: (rev(i), 0)),
                  pl.BlockSpec((1, D), lambda i, m: (0, 0))],
        out_specs=[pl.BlockSpec((None, ts, FB), lambda i, m: (m, rev(i), 0)),
                   pl.BlockSpec((None, 2, ts, FB), lambda i, m: (m, 0, rev(i), 0)),
                   pl.BlockSpec((ts, D), lambda i, m: (rev(i), 0)),
                   pl.BlockSpec((ts, D), lambda i, m: (rev(i), 0)),
                   pl.BlockSpec((1, D), lambda i, m: (0, 0)),
                   pl.BlockSpec((NG, 2, 3, FB), lambda i, m: (0, 0, 0, 0))],
        out_shape=[jax.ShapeDtypeStruct((NG, s, FB), BF16), jax.ShapeDtypeStruct((NG, 2, s, FB), BF16),
                   jax.ShapeDtypeStruct((s, D), F32), jax.ShapeDtypeStruct((s, D), BF16),
                   jax.ShapeDtypeStruct((1, D), F32),
                   jax.ShapeDtypeStruct((NG, 2, 3, FB), F32)],
        scratch_shapes=[pltpu.VMEM((NG, 2, 8, FB), F32), pltpu.VMEM((ts, D), F32),
                        pltpu.VMEM((2, 3, 8, FB), F32)],
        compiler_params=_params("arbitrary", "arbitrary"),
    )(df, up, upc, wup, fcw, wdn, xin, g)


def _mm_nt(dy, w, name):
    s = dy.shape[0]
    ts = _tile(s, 512)

    def body(dy_ref, w_ref, o_ref):
        o_ref[...] = lax.dot_general(dy_ref[...], w_ref[...], NT_DIMS,
                                     preferred_element_type=F32).astype(BF16)

    return pl.pallas_call(
        body, grid=(s // ts,), name=name,
        in_specs=[pl.BlockSpec((ts, D), lambda i: (i, 0)), pl.BlockSpec((D, D), lambda i: (0, 0))],
        out_specs=pl.BlockSpec((ts, D), lambda i: (i, 0)),
        out_shape=jax.ShapeDtypeStruct((s, D), BF16),
        compiler_params=_params("parallel"),
    )(dy, w)


def _mm_nt_rms(dy, wblk, x, g, dres, bf16_copy, name):
    s = dy.shape[0]
    nb, _, bn = wblk.shape
    pair = _pair(bn)
    ts = _tile(s, 512)

    def body(dy_ref, w_ref, x_ref, g_ref, dr_ref, dx_ref, *rest):
        dg_ref = rest[-1]
        i = pl.program_id(0)
        acc = jnp.zeros((ts, D), F32)
        for b in range(0, nb, pair):
            acc = acc + lax.dot_general(dy_ref[:, b * bn:(b + pair) * bn], _cols(w_ref, b, pair), NT_DIMS,
                                        preferred_element_type=F32)
        dx, dgp = _rms_bwd_math(acc, x_ref[...], g_ref[...])
        dx = dr_ref[...] + dx
        dx_ref[...] = dx
        if bf16_copy:
            rest[0][...] = dx.astype(BF16)

        @pl.when(i == 0)
        def _():
            dg_ref[...] = dgp

        @pl.when(i > 0)
        def _():
            dg_ref[...] += dgp

    tile = pl.BlockSpec((ts, D), lambda i: (i, 0))
    return pl.pallas_call(
        body, grid=(s // ts,), name=name,
        in_specs=[pl.BlockSpec((ts, nb * bn), lambda i: (i, 0)), pl.BlockSpec((nb, D, bn), lambda i: (0, 0, 0)),
                  tile, pl.BlockSpec((1, D), lambda i: (0, 0)), tile],
        out_specs=[tile] + ([tile] if bf16_copy else []) + [pl.BlockSpec((1, D), lambda i: (0, 0))],
        out_shape=[jax.ShapeDtypeStruct((s, D), F32)] + ([jax.ShapeDtypeStruct((s, D), BF16)] if bf16_copy else [])
        + [jax.ShapeDtypeStruct((1, D), F32)],
        compiler_params=_params("arbitrary"),
    )(dy, wblk, x, g, dres)


def _c_bwd(dr, zc, cw, name):
    s = dr.shape[0]
    ts = _tile(s, 512)
    nt = s // ts
    hb = ts // HALO

    def body(dr_ref, drf_ref, z_ref, ch_ref, xh_ref, bf_ref, cw_ref, dz_ref, dcw_ref):
        i = pl.program_id(0)
        cwv = [cw_ref[k:k + 1, :] for k in range(3)]
        z_t = z_ref[...].astype(F32)
        bg, cg, xv = z_t[:, 0:D], z_t[:, D:2 * D], z_t[:, 2 * D:3 * D]
        ph = jnp.where(i > 0, ch_ref[...].astype(F32) * xh_ref[...].astype(F32), 0.0)
        pe = jnp.concatenate([ph, cg * xv], axis=0)
        q, taps = _conv3(pe, cwv, HALO)
        drv = dr_ref[...].astype(F32)
        dq = drv * bg
        dqf = jnp.where(i < nt - 1, drf_ref[...].astype(F32) * bf_ref[...].astype(F32), 0.0)
        dp = _conv3_bwd_in(jnp.concatenate([dq, dqf], axis=0), cwv, ts)
        dz_ref[:, 0:D] = (drv * q).astype(BF16)
        dz_ref[:, D:2 * D] = (dp * xv).astype(BF16)
        dz_ref[:, 2 * D:3 * D] = (dp * cg).astype(BF16)
        rows = _conv3_bwd_w(dq, taps)

        @pl.when(i == 0)
        def _():
            for k in range(3):
                dcw_ref[k:k + 1, :] = rows[k]

        @pl.when(i > 0)
        def _():
            for k in range(3):
                dcw_ref[k:k + 1, :] += rows[k]

    past = lambda col: pl.BlockSpec((HALO, D), lambda i: (jnp.maximum(i * hb - 1, 0), col))
    nxt = lambda i: jnp.minimum((i + 1) * hb, s // HALO - 1)
    return pl.pallas_call(
        body, grid=(nt,), name=name,
        in_specs=[pl.BlockSpec((ts, D), lambda i: (i, 0)),
                  pl.BlockSpec((HALO, D), lambda i: (nxt(i), 0)),
                  pl.BlockSpec((ts, 3 * D), lambda i: (i, 0)), past(1), past(2),
                  pl.BlockSpec((HALO, D), lambda i: (nxt(i), 0)),
                  pl.BlockSpec((3, D), lambda i: (0, 0))],
        out_specs=[pl.BlockSpec((ts, 3 * D), lambda i: (i, 0)), pl.BlockSpec((3, D), lambda i: (0, 0))],
        out_shape=[jax.ShapeDtypeStruct((s, 3 * D), BF16), jax.ShapeDtypeStruct((3, D), F32)],
        compiler_params=_params("arbitrary"),
    )(dr, dr, zc, zc, zc, zc, cw)


G512_ROWS = 40


def _ab_bwd(dy, z, yb2, lga, lba, wsm, bs_col, cwb, lgb, lbb, name):
    s = z.shape[0]
    ts = _tile(s, 256)
    nt = s // ts
    hb = ts // HALO_B
    nch = ts // CHUNK
    tri = None

    def body(z_ref, zh_ref, dy_ref, dyf_ref, yb2_ref, yb2f_ref, lga_ref, lba_ref, ws_ref, bs_ref,
             cw_ref, lgb_ref, lbb_ref, dz_ref, g512_ref, dws_ref, dbs_ref, dvn_ref, fwd_rolled, bwd_rolled, du_s):
        i = pl.program_id(0)
        last = i == nt - 1

        @pl.when(i == 0)
        def _():
            g512_ref[...] = jnp.zeros((G512_ROWS, DA), F32)
            dws_ref[...] = jnp.zeros((HEADS, CHUNK, CHUNK), F32)
            dbs_ref[...] = jnp.zeros((HEADS, CHUNK, 1), F32)

        def add_row(k, v):
            g512_ref[k:k + 1, :] += v

        z_t = z_ref[...].astype(F32)
        dy_t = dy_ref[...].astype(F32)
        ua, va = z_t[:, 0:DA], z_t[:, DA:2 * DA]
        gu = _gelu(ua)
        gv = _gelu(va)
        lga_v = lga_ref[...]
        vn, xhat_a, rstd_a = _ln_fwd(gv, lga_v, lba_ref[...])
        vnb = vn.astype(BF16)
        causal = (lax.broadcasted_iota(jnp.int32, (CHUNK, CHUNK), 0)
                  >= lax.broadcasted_iota(jnp.int32, (CHUNK, CHUNK), 1)).astype(F32)
        for c in range(nch):
            for h in range(HEADS):
                rs = slice(c * CHUNK, (c + 1) * CHUNK)
                cs = slice(h * CHUNK, (h + 1) * CHUNK)
                vblk = vnb[rs, cs]
                mixed = jnp.dot(ws_ref[h], vblk, preferred_element_type=F32) + bs_ref[h]
                dyb_ = dy_t[rs, cs]
                dmix = dyb_ * gu[rs, cs]
                dmb = dmix.astype(BF16)
                dz_ref[rs, cs] = (dyb_ * mixed * _dgelu(ua[rs, cs])).astype(BF16)
                dvn_ref[rs, cs] = lax.dot_general(ws_ref[h], dmb, TN_DIMS, preferred_element_type=F32)
                dws_ref[h] += causal * lax.dot_general(dmb, vblk, NT_DIMS, preferred_element_type=F32)
                dbs_ref[h] += jnp.sum(dmix, axis=1, keepdims=True)
        dvn = dvn_ref[...]
        add_row(0, jnp.sum(dvn * xhat_a, axis=0, keepdims=True))
        add_row(1, jnp.sum(dvn, axis=0, keepdims=True))
        dgv = _ln_bwd(dvn, xhat_a, rstd_a, lga_v)
        dz_ref[:, DA:2 * DA] = (dgv * _dgelu(va)).astype(BF16)
        lgb_v = lgb_ref[...]
        dyb_e = jnp.concatenate(
            [dy_t[:, DA:2 * DA], jnp.where(last, 0.0, dyf_ref[...].astype(F32))], axis=0)
        yb2_e = jnp.concatenate([yb2_ref[...], jnp.where(last, 0.0, yb2f_ref[...])], axis=0)
        n_e, xhat_b, rstd_b = _ln_fwd(yb2_e, lgb_v, lbb_ref[...])
        sgn = _sigmoid(n_e)
        dn = dyb_e * _dsilu(n_e, sgn)
        dy2 = _ln_bwd(dn, xhat_b, rstd_b, lgb_v)
        add_row(2, jnp.sum(dy2[:ts], axis=0, keepdims=True))
        add_row(3, jnp.sum(dn[:ts] * xhat_b[:ts], axis=0, keepdims=True))
        add_row(4, jnp.sum(dn[:ts], axis=0, keepdims=True))
        zh = jnp.where(i > 0, zh_ref[...], jnp.zeros_like(zh_ref[...])).astype(F32)
        xb_t, gb_t = z_t[:, 2 * DA:3 * DA], z_t[:, 3 * DA:4 * DA]
        sgb = _sigmoid(gb_t)
        _rolled_copies(fwd_rolled, jnp.concatenate(
            [zh[:, 0:DA] * _sigmoid(zh[:, DA:2 * DA]), xb_t * sgb], axis=0), False)
        _rolled_copies(bwd_rolled, dy2, True)
        for o in range(0, ts, CONV_ROWS):
            acc = jnp.zeros((CONV_ROWS, DA), F32)
            for sh in range(BCONV):
                q, r = divmod(sh, 8)
                acc = acc + cw_ref[BCONV - 1 - sh:BCONV - sh, :] * bwd_rolled[r, 8 * q + o:8 * q + o + CONV_ROWS, :]
            du_s[o:o + CONV_ROWS, :] = acc
        for sh in range(BCONV):
            q, r = divmod(sh, 8)
            acc = jnp.zeros((CONV_ROWS, DA), F32)
            for o in range(0, ts, CONV_ROWS):
                lo = HALO_B - 8 * q + o
                acc = acc + bwd_rolled[0, o:o + CONV_ROWS, :] * fwd_rolled[r, lo:lo + CONV_ROWS, :]
            add_row(8 + BCONV - 1 - sh, jnp.sum(acc, axis=0, keepdims=True))
        du = du_s[...]
        dz_ref[:, 2 * DA:3 * DA] = (du * sgb).astype(BF16)
        dz_ref[:, 3 * DA:4 * DA] = (du * xb_t * sgb * (1.0 - sgb)).astype(BF16)

    row = lambda i: (0, 0)
    nxt = lambda i: jnp.minimum((i + 1) * hb, s // HALO_B - 1)
    return pl.pallas_call(
        body, grid=(nt,), name=name,
        in_specs=[pl.BlockSpec((ts, 4 * DA), lambda i: (i, 0)),
                  pl.BlockSpec((HALO_B, 2 * DA), lambda i: (jnp.maximum(i * hb - 1, 0), 1)),
                  pl.BlockSpec((ts, 2 * DA), lambda i: (i, 0)),
                  pl.BlockSpec((HALO_B, DA), lambda i: (nxt(i), 1)),
                  pl.BlockSpec((ts, DA), lambda i: (i, 0)),
                  pl.BlockSpec((HALO_B, DA), lambda i: (nxt(i), 0)),
                  pl.BlockSpec((1, DA), row), pl.BlockSpec((1, DA), row),
                  pl.BlockSpec((HEADS, CHUNK, CHUNK), lambda i: (0, 0, 0)),
                  pl.BlockSpec((HEADS, CHUNK, 1), lambda i: (0, 0, 0)),
                  pl.BlockSpec((BCONV, DA), row), pl.BlockSpec((1, DA), row), pl.BlockSpec((1, DA), row)],
        out_specs=[pl.BlockSpec((ts, 4 * DA), lambda i: (i, 0)),
                   pl.BlockSpec((G512_ROWS, DA), row),
                   pl.BlockSpec((HEADS, CHUNK, CHUNK), lambda i: (0, 0, 0)),
                   pl.BlockSpec((HEADS, CHUNK, 1), lambda i: (0, 0, 0))],
        out_shape=[jax.ShapeDtypeStruct((s, 4 * DA), BF16), jax.ShapeDtypeStruct((G512_ROWS, DA), F32),
                   jax.ShapeDtypeStruct((HEADS, CHUNK, CHUNK), F32),
                   jax.ShapeDtypeStruct((HEADS, CHUNK, 1), F32)],
        scratch_shapes=[pltpu.VMEM((ts, DA), F32), pltpu.VMEM((8, ts + HALO_B, DA), F32),
                        pltpu.VMEM((8, ts + HALO_B, DA), F32), pltpu.VMEM((ts, DA), F32)],
        compiler_params=_params("arbitrary"),
    )(z, z, dy, dy, yb2, yb2, lga, lba, wsm, bs_col, cwb, lgb, lbb)


def _dw_cols(a, dy, nb, bn, name):
    s = a.shape[0]
    tm = _tile(s, 2048)
    nt = s // tm
    cpb = 4

    def body(a_ref, dy_ref, o_ref, acc):
        t = pl.program_id(1)
        p = lax.dot_general(a_ref[...], dy_ref[...], TN_DIMS, preferred_element_type=F32)

        @pl.when(t == 0)
        def _():
            for q in range(cpb):
                acc[q] = p[:, q * bn:(q + 1) * bn]

        @pl.when(t > 0)
        def _():
            for q in range(cpb):
                acc[q] += p[:, q * bn:(q + 1) * bn]

        @pl.when(t == nt - 1)
        def _():
            o_ref[...] = acc[...].astype(BF16)

    return pl.pallas_call(
        body, grid=(nb // cpb, nt), name=name,
        in_specs=[pl.BlockSpec((tm, D), lambda j, t: (t, 0)), pl.BlockSpec((tm, cpb * bn), lambda j, t: (t, j))],
        out_specs=pl.BlockSpec((cpb, D, bn), lambda j, t: (j, 0, 0)),
        out_shape=jax.ShapeDtypeStruct((nb, D, bn), BF16),
        scratch_shapes=[pltpu.VMEM((cpb, D, bn), F32)],
        compiler_params=_params("arbitrary", "arbitrary"),
    )(a, dy)


def _dw_rows(a, dy, name):
    s = a.shape[0]
    tm = _tile(s, 4096)
    nt = s // tm
    rb = 512

    def body(a_ref, dy_ref, o_ref, acc):
        t = pl.program_id(1)
        p = lax.dot_general(a_ref[...], dy_ref[...], TN_DIMS, preferred_element_type=F32)

        @pl.when(t == 0)
        def _():
            acc[...] = p

        @pl.when(t > 0)
        def _():
            acc[...] += p

        @pl.when(t == nt - 1)
        def _():
            o_ref[...] = acc[...].astype(BF16)

    return pl.pallas_call(
        body, grid=(D // rb, nt), name=name,
        in_specs=[pl.BlockSpec((tm, rb), lambda j, t: (t, j)), pl.BlockSpec((tm, D), lambda j, t: (t, 0))],
        out_specs=pl.BlockSpec((rb, D), lambda j, t: (j, 0)),
        out_shape=jax.ShapeDtypeStruct((D, D), BF16),
        scratch_shapes=[pltpu.VMEM((rb, D), F32)],
        compiler_params=_params("arbitrary", "arbitrary"),
    )(a, dy)


def _dw_up(h, dup, name):
    s = h.shape[0]
    tm = _tile(s, 4096)
    nt = s // tm

    def body(h_ref, d_ref, o_ref, acc):
        t = pl.program_id(1)
        p = lax.dot_general(d_ref[...], h_ref[...], TN_DIMS, preferred_element_type=F32)

        @pl.when(t == 0)
        def _():
            acc[...] = p

        @pl.when(t > 0)
        def _():
            acc[...] += p

        @pl.when(t == nt - 1)
        def _():
            o_ref[...] = acc[...].astype(BF16)

    return pl.pallas_call(
        body, grid=(NDEV, nt), name=name,
        in_specs=[pl.BlockSpec((tm, D), lambda b, t: (t, 0)),
                  pl.BlockSpec((None, None, tm, FB), lambda b, t: (b % NG, b // NG, t, 0))],
        out_specs=pl.BlockSpec((None, FB, D), lambda b, t: (b, 0, 0)),
        out_shape=jax.ShapeDtypeStruct((NDEV, FB, D), BF16),
        scratch_shapes=[pltpu.VMEM((FB, D), F32)],
        compiler_params=_params("arbitrary", "arbitrary"),
    )(h, dup)


def _dw_dn(a, df, name):
    s = df.shape[0]
    tm = _tile(s, 4096)
    nt = s // tm

    def body(a_ref, d_ref, o_ref, acc):
        t = pl.program_id(1)
        p = lax.dot_general(a_ref[...], d_ref[...], TN_DIMS, preferred_element_type=F32)

        @pl.when(t == 0)
        def _():
            acc[...] = p

        @pl.when(t > 0)
        def _():
            acc[...] += p

        @pl.when(t == nt - 1)
        def _():
            o_ref[...] = acc[...].astype(BF16)

    return pl.pallas_call(
        body, grid=(NG, nt), name=name,
        in_specs=[pl.BlockSpec((None, tm, FB), lambda m, t: (m, t, 0)), pl.BlockSpec((tm, D), lambda m, t: (t, 0))],
        out_specs=pl.BlockSpec((FB, D), lambda m, t: (m, 0)),
        out_shape=jax.ShapeDtypeStruct((DFF, D), BF16),
        scratch_shapes=[pltpu.VMEM((FB, D), F32)],
        compiler_params=_params("arbitrary", "arbitrary"),
    )(a, df)


def _place():
    x, y, c = lax.axis_index("x"), lax.axis_index("y"), lax.axis_index("c")
    chips = [(1 - x, y), (x, 1 - y), (1 - x, 1 - y)]
    return x, y, c, chips


def _zone(shard, dev):
    return lax.dynamic_update_slice(lax.empty((NDEV,) + shard.shape, shard.dtype), shard[None],
                                    (dev,) + (0,) * shard.ndim)


HBM_SPEC = pl.BlockSpec(memory_space=pltpu.HBM)
SEM_SPEC = pl.BlockSpec(memory_space=pltpu.SEMAPHORE)
DATAFLOW = pltpu.SideEffectType.DATAFLOW_SIDE_EFFECTING


def _hbm(a):
    return pltpu.with_memory_space_constraint(a, pltpu.HBM)


def _hbm_like(arrs):
    return [pltpu.HBM(a.shape, a.dtype) for a in arrs]


def _ag_start(srcs, lands, after, name):
    n = len(srcs)
    ns = 8 * n

    def body(*refs):
        src, land = refs[:n], refs[n:2 * n]
        sems = refs[2 * n + 1:2 * n + 1 + ns]
        token = refs[-1]
        x, y, c, chips = _place()
        peers = [(x, y, 1 - c)] + [(*chip, c) for chip in chips]
        for t in range(n):
            for k, to in enumerate(peers):
                pltpu.make_async_remote_copy(
                    src_ref=src[t], dst_ref=land[t].at[4 * x + 2 * y + c],
                    send_sem=sems[2 * (4 * t + k)], recv_sem=sems[2 * (4 * t + k) + 1],
                    device_id=to, device_id_type=MESH).start()
        token[...] = jnp.zeros_like(token)

    res = pl.pallas_call(
        body, name=name,
        in_specs=[HBM_SPEC] * (2 * n) + [ANY],
        out_specs=[SEM_SPEC] * ns + [HBM_SPEC] * (2 * n) + [pl.BlockSpec(memory_space=pltpu.VMEM)],
        out_shape=[pltpu.SemaphoreType.DMA(())] * ns + _hbm_like(srcs) + _hbm_like(lands)
        + [jax.ShapeDtypeStruct((8, 128), F32)],
        input_output_aliases={i: ns + i for i in range(2 * n)},
        compiler_params=pltpu.CompilerParams(has_side_effects=DATAFLOW),
    )(*[_hbm(a) for a in srcs], *[_hbm(a) for a in lands], after)
    sems = [[(res[2 * (4 * t + k)], res[2 * (4 * t + k) + 1]) for k in range(4)] for t in range(n)]
    return sems, res[ns:ns + n], res[ns + n:ns + 2 * n], res[-1]


def _ag_forward(srcs, lands, sems1, after, name):
    n = len(srcs)
    flat1 = [s for t in range(n) for k in range(1, 4) for s in sems1[t][k]]
    n1 = len(flat1)

    def body(*refs):
        src, land = refs[:n], refs[n:2 * n]
        s1 = refs[2 * n:2 * n + n1]
        s2 = refs[2 * n + n1 + 1:2 * n + n1 + 1 + 6 * n]
        x, y, c, chips = _place()
        for j, (cx, cy) in enumerate(chips):
            for t in range(n):
                blk = land[t].at[4 * cx + 2 * cy + c]
                pltpu.make_async_remote_copy(
                    src_ref=src[t], dst_ref=blk, send_sem=s1[2 * (3 * t + j)], recv_sem=s1[2 * (3 * t + j) + 1],
                    device_id=(cx, cy, c), device_id_type=MESH).wait_recv()
                pltpu.make_async_remote_copy(
                    src_ref=blk, dst_ref=blk, send_sem=s2[2 * (3 * t + j)], recv_sem=s2[2 * (3 * t + j) + 1],
                    device_id=(x, y, 1 - c), device_id_type=MESH).start()

    res = pl.pallas_call(
        body, name=name,
        in_specs=[HBM_SPEC] * (2 * n) + [SEM_SPEC] * n1 + [ANY],
        out_specs=[SEM_SPEC] * (6 * n) + [HBM_SPEC] * n,
        out_shape=[pltpu.SemaphoreType.DMA(())] * (6 * n) + _hbm_like(lands),
        input_output_aliases={n + i: 6 * n + i for i in range(n)},
        compiler_params=pltpu.CompilerParams(has_side_effects=DATAFLOW),
    )(*srcs, *lands, *flat1, after)
    sems2 = [[(res[2 * (3 * t + j)], res[2 * (3 * t + j) + 1]) for j in range(3)] for t in range(n)]
    return sems2, res[6 * n:]


def _ag_finish(srcs, lands, sems1, sems2, after, name):
    n = len(srcs)
    flat1 = [s for t in range(n) for k in range(4) for s in sems1[t][k]]
    flat2 = [s for t in range(n) for j in range(3) for s in sems2[t][j]]
    n1, n2 = len(flat1), len(flat2)

    def body(*refs):
        src, land = refs[:n], refs[n:2 * n]
        s1 = refs[2 * n:2 * n + n1]
        s2 = refs[2 * n + n1:2 * n + n1 + n2]
        x, y, c, chips = _place()
        sib = (x, y, 1 - c)
        for t in range(n):
            own = land[t].at[4 * x + 2 * y + 1 - c]
            pltpu.make_async_remote_copy(
                src_ref=src[t], dst_ref=own, send_sem=s1[8 * t], recv_sem=s1[8 * t + 1],
                device_id=sib, device_id_type=MESH).wait_recv()
            for k in range(4):
                pltpu.make_async_remote_copy(
                    src_ref=src[t], dst_ref=own, send_sem=s1[2 * (4 * t + k)], recv_sem=s1[2 * (4 * t + k) + 1],
                    device_id=sib, device_id_type=MESH).wait_send()
            for j, (cx, cy) in enumerate(chips):
                blk = land[t].at[4 * cx + 2 * cy + 1 - c]
                cp = pltpu.make_async_remote_copy(
                    src_ref=blk, dst_ref=blk, send_sem=s2[2 * (3 * t + j)], recv_sem=s2[2 * (3 * t + j) + 1],
                    device_id=sib, device_id_type=MESH)
                cp.wait_send()
                cp.wait_recv()

    return pl.pallas_call(
        body, name=name,
        in_specs=[HBM_SPEC] * (2 * n) + [SEM_SPEC] * (n1 + n2) + [ANY],
        out_specs=[HBM_SPEC] * n,
        out_shape=_hbm_like(lands),
        input_output_aliases={n + i: i for i in range(n)},
        compiler_params=pltpu.CompilerParams(has_side_effects=DATAFLOW),
    )(*srcs, *lands, *flat1, *flat2, after)


def _pair_copies(srcs, dsts, sems):
    x, y, c, _ = _place()
    nt = len(srcs)
    return [pltpu.make_async_remote_copy(
        src_ref=srcs[t].at[2 * j + 1 - c], dst_ref=dsts[t].at[j],
        send_sem=sems[2 * (NCHIP * t + j)], recv_sem=sems[2 * (NCHIP * t + j) + 1],
        device_id=(x, y, 1 - c), device_id_type=MESH) for t in range(nt) for j in range(NCHIP)]


def _pair_start(grads, carry, name):
    nt = len(grads)
    ns = 2 * NCHIP * nt
    zones = [_hbm(lax.empty((NCHIP,) + a.shape[1:], a.dtype)) for a in grads]
    extra = [] if carry is None else [_hbm(carry)]
    ne = len(extra)

    def body(*refs):
        for cp in _pair_copies(refs[:nt], refs[nt:2 * nt], refs[2 * nt + ne:2 * nt + ne + ns]):
            cp.start()

    res = pl.pallas_call(
        body, name=name,
        in_specs=[HBM_SPEC] * (2 * nt + ne),
        out_specs=[SEM_SPEC] * ns + [HBM_SPEC] * (2 * nt + ne),
        out_shape=[pltpu.SemaphoreType.DMA(())] * ns + _hbm_like(grads) + _hbm_like(zones) + _hbm_like(extra),
        input_output_aliases={i: ns + i for i in range(2 * nt + ne)},
        compiler_params=pltpu.CompilerParams(has_side_effects=DATAFLOW),
    )(*[_hbm(a) for a in grads], *zones, *extra)
    handle = (list(res[:ns]), list(res[ns:ns + nt]), list(res[ns + nt:ns + 2 * nt]))
    return handle, (res[ns + 2 * nt] if ne else None)


def _pair_wait(handle, after, name):
    sems, srcs, zones = handle
    nt, ns = len(srcs), len(sems)

    def body(*refs):
        for cp in _pair_copies(refs[:nt], refs[nt:2 * nt], refs[2 * nt:2 * nt + ns]):
            cp.wait_send()
            cp.wait_recv()

    return pl.pallas_call(
        body, name=name,
        in_specs=[HBM_SPEC] * (2 * nt) + [SEM_SPEC] * ns + [ANY],
        out_specs=[HBM_SPEC] * nt,
        out_shape=_hbm_like(zones),
        input_output_aliases={nt + i: i for i in range(nt)},
        compiler_params=pltpu.CompilerParams(has_side_effects=DATAFLOW),
    )(*srcs, *zones, *sems, after)


def _rows_tile(r, row_bytes, cap_bytes):
    best = None
    for tr in range(16, r + 1, 16):
        if r % tr == 0 and tr * row_bytes <= cap_bytes:
            best = tr
    return best if best is not None else r


def _pair_sum(own, got, cidx, name):
    _, _, r, cdim = own.shape
    tr = _rows_tile(r, 2 * cdim, 2 * 1024 * 1024)

    def body(c_ref, a_ref, b_ref, o_ref):
        o_ref[...] = (a_ref[...].astype(F32) + b_ref[...].astype(F32)).astype(BF16)

    return pl.pallas_call(
        body, name=name,
        grid_spec=pltpu.PrefetchScalarGridSpec(
            num_scalar_prefetch=1, grid=(NCHIP, r // tr),
            in_specs=[pl.BlockSpec((None, None, tr, cdim), lambda j, i, c_ref: (j, c_ref[0], i, 0)),
                      pl.BlockSpec((None, tr, cdim), lambda j, i, c_ref: (j, i, 0))],
            out_specs=pl.BlockSpec((None, tr, cdim), lambda j, i, c_ref: (j, i, 0))),
        out_shape=jax.ShapeDtypeStruct((NCHIP, r, cdim), BF16),
        compiler_params=_params("arbitrary", "arbitrary"),
    )(cidx, own, got)


def _chip_copies(srcs, zones, slots, sems):
    x, y, c, chips = _place()
    out = []
    for t, (z, l) in enumerate(slots):
        for k, (cx, cy) in enumerate(chips):
            dst = zones[z].at[k] if l is None else zones[z].at[k, l]
            out.append(pltpu.make_async_remote_copy(
                src_ref=srcs[t].at[2 * cx + cy], dst_ref=dst,
                send_sem=sems[2 * (3 * t + k)], recv_sem=sems[2 * (3 * t + k) + 1],
                device_id=(cx, cy, c), device_id_type=MESH))
    return out


def _chip_start(sums, zones, slots, carry, name):
    nt, nz = len(sums), len(zones)
    ns = 6 * nt
    extra = [] if carry is None else [_hbm(carry)]
    ne = len(extra)

    def body(*refs):
        for cp in _chip_copies(refs[:nt], refs[nt:nt + nz], slots, refs[nt + nz + ne:nt + nz + ne + ns]):
            cp.start()

    res = pl.pallas_call(
        body, name=name,
        in_specs=[HBM_SPEC] * (nt + nz + ne),
        out_specs=[SEM_SPEC] * ns + [HBM_SPEC] * (nt + nz + ne),
        out_shape=[pltpu.SemaphoreType.DMA(())] * ns + _hbm_like(sums) + _hbm_like(zones) + _hbm_like(extra),
        input_output_aliases={i: ns + i for i in range(nt + nz + ne)},
        compiler_params=pltpu.CompilerParams(has_side_effects=DATAFLOW),
    )(*[_hbm(a) for a in sums], *zones, *extra)
    return (list(res[:ns]), list(res[ns:ns + nt]), list(res[ns + nt:ns + nt + nz]),
            (res[ns + nt + nz] if ne else None))


def _chip_wait(started, zones, zone_ids, after, name):
    started = [(sums, [(zone_ids.index(z), l) for z, l in slots], sems) for sums, slots, sems in started]
    nz = len(zones)
    flat_src = [a for sums, _, _ in started for a in sums]
    flat_sem = [s for _, _, sems in started for s in sems]
    n_src, n_sem = len(flat_src), len(flat_sem)

    def body(*refs):
        srcs, zs, sems = refs[:n_src], refs[n_src:n_src + nz], refs[n_src + nz:n_src + nz + n_sem]
        so, se = 0, 0
        for sums, slots, sem_list in started:
            for cp in _chip_copies(srcs[so:so + len(sums)], zs, slots, sems[se:se + len(sem_list)]):
                cp.wait_send()
                cp.wait_recv()
            so += len(sums)
            se += len(sem_list)

    return pl.pallas_call(
        body, name=name,
        in_specs=[HBM_SPEC] * (n_src + nz) + [SEM_SPEC] * n_sem + [ANY],
        out_specs=[HBM_SPEC] * nz,
        out_shape=_hbm_like(zones),
        input_output_aliases={n_src + i: i for i in range(nz)},
        compiler_params=pltpu.CompilerParams(has_side_effects=DATAFLOW),
    )(*flat_src, *zones, *flat_sem, after)


def _small_allreduce(parts, after, name):
    nt = len(parts)

    def body(*refs):
        srcs, outs, bufs = refs[:nt], refs[nt + 1:2 * nt + 1], refs[2 * nt + 1:3 * nt + 1]
        send_sems, recv_sems = refs[3 * nt + 1:]
        x, y, c, _ = _place()
        peers = [(x, y, 1 - c), (1 - x, y, c), (x, 1 - y, c)]
        for t in range(nt):
            outs[t][...] = srcs[t][...]
        for step, peer in enumerate(peers):
            copies = [pltpu.make_async_remote_copy(
                src_ref=outs[t], dst_ref=bufs[t].at[step],
                send_sem=send_sems.at[step, t], recv_sem=recv_sems.at[step, t],
                device_id=peer, device_id_type=MESH) for t in range(nt)]
            for cp in copies:
                cp.start()
            for cp in copies:
                cp.wait()
            for t in range(nt):
                outs[t][...] = outs[t][...] + bufs[t][step]

    vm = pl.BlockSpec(memory_space=pltpu.VMEM)
    return pl.pallas_call(
        body, name=name,
        in_specs=[vm] * nt + [ANY], out_specs=[vm] * nt,
        out_shape=[jax.ShapeDtypeStruct(a.shape, F32) for a in parts],
        scratch_shapes=[pltpu.VMEM((3,) + a.shape, F32) for a in parts]
        + [pltpu.SemaphoreType.DMA((3, nt)), pltpu.SemaphoreType.DMA((3, nt))],
        compiler_params=pltpu.CompilerParams(has_side_effects=True, vmem_limit_bytes=VMEM_LIMIT),
    )(*parts, after)


def _adam_math(w, g, m, v):
    m2 = ADAM_B1 * m + (1.0 - ADAM_B1) * g
    v2 = ADAM_B2 * v + (1.0 - ADAM_B2) * (g * g)
    m_hat = m2 / (1.0 - ADAM_B1 ** ADAM_STEP)
    v_hat = v2 / (1.0 - ADAM_B2 ** ADAM_STEP)
    delta = -ADAM_LR * (m_hat / (jnp.sqrt(v_hat) + ADAM_EPS) + ADAM_WD * w)
    return delta, m2, v2


def _adam_big(w, m, v, parts, mine, chip, name):
    nl, r, cdim = w.shape
    tr = _rows_tile(r, 4 * cdim, 3 * 512 * 1024)

    def body(c_ref, w_ref, m_ref, v_ref, p_ref, *rest):
        mine_refs, (g_ref, d_ref, mo_ref, vo_ref) = rest[:nl], rest[nl:]
        own = mine_refs[0][...]
        for l in range(1, nl):
            own = jnp.where(pl.program_id(0) == l, mine_refs[l][...], own)
        g = ((p_ref[0].astype(F32) + p_ref[1].astype(F32)) + p_ref[2].astype(F32)) + own.astype(F32)
        delta, m2, v2 = _adam_math(w_ref[...], g, m_ref[...], v_ref[...])
        g_ref[...] = g
        d_ref[...] = delta
        mo_ref[...] = m2
        vo_ref[...] = v2

    spec = pl.BlockSpec((None, tr, cdim), lambda l, i, c_ref: (l, i, 0))
    mine_specs = [pl.BlockSpec((None, tr, cdim), lambda l, i, c_ref, ll=ll: (c_ref[0], jnp.where(l == ll, i, 0), 0))
                  for ll in range(nl)]
    return pl.pallas_call(
        body, name=name,
        grid_spec=pltpu.PrefetchScalarGridSpec(
            num_scalar_prefetch=1, grid=(nl, r // tr),
            in_specs=[spec, spec, spec, pl.BlockSpec((3, None, tr, cdim), lambda l, i, c_ref: (0, l, i, 0))]
            + mine_specs,
            out_specs=[spec] * 4),
        out_shape=[jax.ShapeDtypeStruct(w.shape, F32)] * 4,
        compiler_params=_params("arbitrary", "arbitrary"),
    )(chip, w, m, v, parts, *mine)


def _adam_small(ws, gs, ms, vs, name):
    n = len(ws)

    def body(*refs):
        w_r, g_r, m_r, v_r = refs[:n], refs[n:2 * n], refs[2 * n:3 * n], refs[3 * n:4 * n]
        d_o, m_o, v_o = refs[4 * n:5 * n], refs[5 * n:6 * n], refs[6 * n:7 * n]
        for t in range(n):
            delta, m2, v2 = _adam_math(w_r[t][...], g_r[t][...], m_r[t][...], v_r[t][...])
            d_o[t][...] = delta
            m_o[t][...] = m2
            v_o[t][...] = v2

    vm = pl.BlockSpec(memory_space=pltpu.VMEM)
    shapes = [jax.ShapeDtypeStruct(a.shape, F32) for a in ws]
    return pl.pallas_call(
        body, name=name, in_specs=[vm] * (4 * n), out_specs=[vm] * (3 * n), out_shape=shapes * 3,
        compiler_params=pltpu.CompilerParams(vmem_limit_bytes=VMEM_LIMIT),
    )(*ws, *gs, *ms, *vs)


def kernel(x, norm_mix, norm_ffn, norm_final, ab_w_in, a_ln_g, a_ln_b, a_w_s, a_b_s, b_conv_w, b_conv_b, b_ln_g, b_ln_b, ab_w_out, c_w_in, c_conv_w, c_w_out, f_w_up, f_conv_w, f_w_down, loss_target, m_norm_mix, m_norm_ffn, m_norm_final, m_ab_w_in, m_a_ln_g, m_a_ln_b, m_a_w_s, m_a_b_s, m_b_conv_w, m_b_conv_b, m_b_ln_g, m_b_ln_b, m_ab_w_out, m_c_w_in, m_c_conv_w, m_c_w_out, m_f_w_up, m_f_conv_w, m_f_w_down, v_norm_mix, v_norm_ffn, v_norm_final, v_ab_w_in, v_a_ln_g, v_a_ln_b, v_a_w_s, v_a_b_s, v_b_conv_w, v_b_conv_b, v_b_ln_g, v_b_ln_b, v_ab_w_out, v_c_w_in, v_c_conv_w, v_c_w_out, v_f_w_up, v_f_conv_w, v_f_w_down):
    s = x.shape[1]
    x0 = x.reshape(s, D)
    tgt = loss_target.reshape(s, D)
    xi, yi, ci = lax.axis_index("x"), lax.axis_index("y"), lax.axis_index("c")
    dev = 4 * xi + 2 * yi + ci
    cidx = ci.astype(jnp.int32).reshape(1)

    bf = lambda a: a.astype(BF16)
    slab_w = 6 * CHUNK
    pad = lambda a, rows: jnp.pad(a, ((0, rows - a.shape[0]), (0, slab_w - a.shape[1])))
    slab = jnp.concatenate([pad(b_conv_w[0], 32), pad(c_conv_w[0], 8), pad(f_conv_w.reshape(6, FB), 8)], axis=0)
    later = [bf(ab_w_in[0]), bf(ab_w_out[0]), slab, bf(f_w_up[0]), bf(f_w_down[0]), bf(c_w_in[0]), bf(c_w_out[0]),
             bf(f_w_up[1]), bf(f_w_down[1])]
    lands = [_zone(a, dev) for a in later]
    groups = [[0], [1, 2], [3, 4], [5, 6], [7, 8]]
    ag_sems, later, lands, ag_token = _ag_start(later, lands, x0, "ag_start")

    causal = jnp.tril(jnp.ones((CHUNK, CHUNK), F32))
    wsm = (a_w_s[0] * causal).astype(BF16)
    bs_col = a_b_s.reshape(HEADS, CHUNK, 1)
    nm = [norm_mix[0:1], norm_mix[1:2]]
    nf = [norm_ffn[0:1], norm_ffn[1:2]]
    nfin = norm_final.reshape(1, D)

    def arrive(g, after_ici, after_d2d, tag):
        srcs = [later[t] for t in groups[g]]
        zone = [lands[t] for t in groups[g]]
        sems1 = [ag_sems[t] for t in groups[g]]
        sems2, zone = _ag_forward(srcs, zone, sems1, after_ici, "ag_forward_" + tag)
        return _ag_finish(srcs, zone, sems1, sems2, after_d2d, "ag_finish_" + tag)

    h0 = _rms_fwd(x0, nm[0], "rms_mix0", after=ag_token)
    (win0,) = arrive(0, h0, h0, "w_in")
    z = _mm_in(h0, win0, "mm_ab_in")
    wout0, slab_g = arrive(1, z, z, "first")
    wout0 = wout0.reshape(D, D)
    bcw = jnp.transpose(slab_g[:, 0:BCONV, 0:DA // NDEV], (1, 0, 2)).reshape(BCONV, DA)
    ccw = jnp.transpose(slab_g[:, 32:35, 0:D // NDEV], (1, 0, 2)).reshape(3, D)
    fcw_g = slab_g[:, 40:46, 0:FB].reshape(2, NG, 2, 3, FB)
    fcws = [fcw_g[:, :, 0], fcw_g[:, :, 1]]
    ycat, yb2 = _ab_fwd(z, a_ln_g, a_ln_b, wsm, bs_col, bcw, b_conv_b, b_ln_g, b_ln_b, "ab_fwd")
    x1, h1 = _mm_out(ycat, wout0, x0, nf[0], "mm_ab_out")
    wup0, wdn0 = arrive(2, x1, x1, "ffn0")
    up0, upc0, x2, h2 = _ffn_fwd(h1, x1, wup0.reshape(2, NG, D, FB), fcws[0], wdn0.reshape(DFF, D), nm[1],
                                 "ffn_fwd0")
    cin, cout = arrive(3, x2, x2, "c")
    cout = cout.reshape(D, D)
    zc = _mm_in(h2, cin, "mm_c_in")
    rc = _c_fwd(zc, ccw, "c_fwd")
    x3, h3 = _mm_out(rc, cout, x2, nf[1], "mm_c_out")
    wup1, wdn1 = arrive(4, rc, x3, "ffn1")
    wups = [wup0.reshape(2, NG, D, FB), wup1.reshape(2, NG, D, FB)]
    wdns = [wdn0.reshape(DFF, D), wdn1.reshape(DFF, D)]
    up1, upc1, x4 = _ffn_fwd(h3, x3, wups[1], fcws[1], wdns[1], None, "ffn_fwd1")
    dx4, dx4b, dnfin, loss_part = _final(x4, tgt, nfin, "final_loss")

    zshape = lambda *sh: _hbm(lax.empty((3,) + sh, BF16))
    zones = [zshape(D, 2 * D // NDEV), zshape(D // NDEV, D), zshape(D, 3 * D // NDEV), zshape(D // NDEV, D),
             zshape(2, FB, D), zshape(2, DFF // NDEV, D)]
    started = []

    def pair_sums(grads, handle, after, tag):
        del grads
        got = _pair_wait(handle, after, "rs_pair_wait_" + tag)
        return [_pair_sum(b.reshape((NCHIP, 2) + b.shape[1:]), g, cidx, "rs_pair_sum_%s%d" % (tag, t))
                for t, (b, g) in enumerate(zip(handle[1], got))]

    def chip_start(sums, slots, carry, tag):
        sems, sums, new_zones, carry = _chip_start(sums, zones, slots, carry, "rs_chip_start_" + tag)
        zones[:] = new_zones
        started.append((sums, slots, sems))
        return sums, carry

    rows8 = lambda g, r: g.reshape(NDEV, r, D)
    a1, dup1, dx3, dx3b, dnf1, dfcw1 = _ffn_bwd(dx4, up1, upc1, wups[1], fcws[1], wdns[1], x3, nf[1], "ffn_bwd1")
    g_f1 = [_dw_up(h3, dup1, "dw_up1"), rows8(_dw_dn(a1, dx4b, "dw_dn1"), DFF // NDEV)]
    hd_f1, dx3b = _pair_start(g_f1, dx3b, "rs_pair_start_f1")
    drc = _mm_nt(dx3b, cout, "mm_c_out_bwd")
    g_cout = rows8(_dw_rows(rc, dx3b, "dw_c_out"), D // NDEV)
    s_f1 = pair_sums(g_f1, hd_f1, g_cout, "f1")
    s_f1, drc = chip_start(s_f1, [(4, 1), (5, 1)], drc, "f1")
    dzc, dccw = _c_bwd(drc, zc, ccw, "c_bwd")
    dx2, dx2b, dnm1 = _mm_nt_rms(dzc, cin, x2, nm[1], dx3, True, "mm_c_in_bwd")
    g_c = [_dw_cols(h2, dzc, NDEV, 3 * D // NDEV, "dw_c_in"), g_cout]
    hd_c, dx2 = _pair_start(g_c, dx2, "rs_pair_start_c")
    a0, dup0, dx1, dx1b, dnf0, dfcw0 = _ffn_bwd(dx2, up0, upc0, wups[0], fcws[0], wdns[0], x1, nf[0], "ffn_bwd0")
    s_c = pair_sums(g_c, hd_c, dx1b, "c")
    s_c, dx1b = chip_start(s_c, [(2, None), (3, None)], dx1b, "c")
    g_f0 = [_dw_up(h1, dup0, "dw_up0"), rows8(_dw_dn(a0, dx2b, "dw_dn0"), DFF // NDEV)]
    hd_f0, dx1b = _pair_start(g_f0, dx1b, "rs_pair_start_f0")
    dycat = _mm_nt(dx1b, wout0, "mm_ab_out_bwd")
    g_wout0 = rows8(_dw_rows(ycat, dx1b, "dw_ab_out"), D // NDEV)
    s_f0 = pair_sums(g_f0, hd_f0, g_wout0, "f0")
    s_f0, dycat = chip_start(s_f0, [(4, 0), (5, 0)], dycat, "f0")
    dz, g512, dws, dbs = _ab_bwd(dycat, z, yb2, a_ln_g, a_ln_b, wsm, bs_col, bcw, b_ln_g, b_ln_b, "ab_bwd")
    grad_x, dnm0 = _mm_nt_rms(dz, win0, x0, nm[0], dx1, False, "mm_ab_in_bwd")
    g_ab = [_dw_cols(h0, dz, NDEV, 2 * D // NDEV, "dw_ab_in"), g_wout0]
    hd_ab, _ = _pair_start(g_ab, None, "rs_pair_start_ab")

    g1024 = jnp.concatenate([dnm0, dnm1, dnf0, dnf1, dnfin, dccw], axis=0)
    gfc = jnp.concatenate([dfcw0, dfcw1], axis=0).reshape(2 * NG * 2 * 3, FB)
    g1024, g512, dws, dbs, gfc, loss_sum = _small_allreduce(
        [g1024, g512, dws.reshape(HEADS * CHUNK, CHUNK), dbs.reshape(HEADS, CHUNK), gfc, loss_part], hd_ab[1][0],
        "small_allreduce")
    loss = loss_sum[0, 0]
    s_ab = pair_sums(g_ab, hd_ab, g1024, "ab")
    s_ab, _ = chip_start(s_ab, [(0, None), (1, None)], None, "ab")
    p_cin, p_cout, p_wup, p_wdn = _chip_wait(started[:3], zones[2:], [2, 3, 4, 5], s_ab[0], "rs_chip_wait_early")

    chip = (2 * xi + yi).astype(jnp.int32).reshape(1)

    def big_update(w, m, v, parts, mine, name):
        shp = w.shape
        w3, m3, v3 = (a.reshape((-1,) + shp[-2:]) for a in (w, m, v))
        p4 = parts.reshape((3,) + w3.shape)
        return [o.reshape(shp) for o in _adam_big(w3, m3, v3, p4, mine, chip, name)]

    u_cin = big_update(c_w_in, m_c_w_in, v_c_w_in, p_cin, [s_c[0]], "adam_c_w_in")
    u_cout = big_update(c_w_out, m_c_w_out, v_c_w_out, p_cout, [s_c[1]], "adam_c_w_out")
    tr_ = lambda a: jnp.swapaxes(a, 1, 2)
    u_wup = [tr_(o) for o in big_update(tr_(f_w_up), tr_(m_f_w_up), tr_(v_f_w_up), p_wup,
                                        [s_f0[0], s_f1[0]], "adam_f_w_up")]
    u_wdn = big_update(f_w_down, m_f_w_down, v_f_w_down, p_wdn, [s_f0[1], s_f1[1]], "adam_f_w_down")
    p_win0, p_wout0 = _chip_wait(started[3:], zones[:2], [0, 1], u_wdn[0], "rs_chip_wait_late")
    u_win0 = big_update(ab_w_in, m_ab_w_in, v_ab_w_in, p_win0, [s_ab[0]], "adam_ab_w_in")
    u_wout0 = big_update(ab_w_out, m_ab_w_out, v_ab_w_out, p_wout0, [s_ab[1]], "adam_ab_w_out")

    g_norm_mix = g1024[0:2]
    g_norm_ffn = g1024[2:4]
    g_norm_final = g1024[4:5]
    g_ccw = lax.dynamic_slice(g1024[5:8], (0, dev * (D // NDEV)), (3, D // NDEV))
    g_bcw = lax.dynamic_slice(g512[8:8 + BCONV], (0, dev * (DA // NDEV)), (BCONV, DA // NDEV))
    gfc = gfc.reshape(2, NG, 2, 3, FB)
    g_fcw = lax.dynamic_slice(gfc, (0, dev % NG, dev // NG, 0, 0), (2, 1, 1, 3, FB)).reshape(2, 3, FB)
    small_w = [norm_mix, norm_ffn, nfin, a_ln_g, a_ln_b, a_w_s[0], a_b_s[0], b_conv_w[0], b_conv_b,
               b_ln_g, b_ln_b, c_conv_w[0], f_conv_w]
    small_g = [g_norm_mix, g_norm_ffn, g_norm_final, g512[0:1], g512[1:2],
               dws.reshape(HEADS, CHUNK, CHUNK), dbs, g_bcw, g512[2:3],
               g512[3:4], g512[4:5], g_ccw, g_fcw]
    small_m = [m_norm_mix, m_norm_ffn, m_norm_final.reshape(1, D), m_a_ln_g, m_a_ln_b, m_a_w_s[0], m_a_b_s[0],
               m_b_conv_w[0], m_b_conv_b, m_b_ln_g, m_b_ln_b, m_c_conv_w[0], m_f_conv_w]
    small_v = [v_norm_mix, v_norm_ffn, v_norm_final.reshape(1, D), v_a_ln_g, v_a_ln_b, v_a_w_s[0], v_a_b_s[0],
               v_b_conv_w[0], v_b_conv_b, v_b_ln_g, v_b_ln_b, v_c_conv_w[0], v_f_conv_w]
    upd = _adam_small(small_w, small_g, small_m, small_v, "adam_small")
    ns = len(small_w)
    orig = [norm_mix, norm_ffn, norm_final, a_ln_g, a_ln_b, a_w_s, a_b_s, b_conv_w, b_conv_b,
            b_ln_g, b_ln_b, c_conv_w, f_conv_w]
    sg_out = [g.reshape(o.shape) for g, o in zip(small_g, orig)]
    sd_out = [a.reshape(o.shape) for a, o in zip(upd[0:ns], orig)]
    sm_out = [a.reshape(o.shape) for a, o in zip(upd[ns:2 * ns], orig)]
    sv_out = [a.reshape(o.shape) for a, o in zip(upd[2 * ns:3 * ns], orig)]

    def assemble(small, k):
        return [small[0], small[1], small[2], u_win0[k], small[3], small[4], small[5], small[6], small[7],
                small[8], small[9], small[10], u_wout0[k], u_cin[k], small[11], u_cout[k], u_wup[k],
                small[12], u_wdn[k]]

    grads = assemble(sg_out, 0)
    deltas = assemble(sd_out, 1)
    new_m = assemble(sm_out, 2)
    new_v = assemble(sv_out, 3)
    return (loss, grad_x.reshape(1, s, D), *grads, *deltas, *new_m, *new_v)
```

```python
import functools
import math

import jax
import jax.numpy as jnp
from jax import lax
from jax.experimental import pallas as pl
from jax.experimental.pallas import tpu as pltpu

F32 = jnp.float32
BF16 = jnp.bfloat16

D = 1024
DA = 512
HEADS = 4
CHUNK = 128
DFF = 2816
NDEV = 8
NCHIP = 4
FB = DFF * 2 // NDEV
NG = DFF // FB
BCONV = 31
EPS = 1e-6
HALO = 16
HALO_B = 32
RC = 32
NPART = 2
VMEM_LIMIT = 52 * 1024 * 1024
INV_SQRT2 = 1.0 / math.sqrt(2.0)
INV_SQRT_2PI = 1.0 / math.sqrt(2.0 * math.pi)

ADAM_LR = 0.001
ADAM_B1 = 0.9
ADAM_B2 = 0.999
ADAM_EPS = 1e-08
ADAM_WD = 0.01
ADAM_STEP = 10

MESH = pl.DeviceIdType.MESH
ANY = pl.BlockSpec(memory_space=pl.ANY)
NT_DIMS = (((1,), (1,)), ((), ()))
TN_DIMS = (((0,), (0,)), ((), ()))


def _params(*sem):
    return pltpu.CompilerParams(dimension_semantics=sem, vmem_limit_bytes=VMEM_LIMIT)


def _tile(s, want):
    return min(want, s)


def _sigmoid(x):
    return jax.nn.sigmoid(x)


def _dsilu(x, sg):
    return sg * (1.0 + x * (1.0 - sg))


def _gelu(x):
    return 0.5 * x * (1.0 + lax.erf(x * INV_SQRT2))


def _dgelu(x):
    return 0.5 * (1.0 + lax.erf(x * INV_SQRT2)) + x * jnp.exp(-0.5 * x * x) * INV_SQRT_2PI


def _ln_fwd(x, g, b):
    mu = jnp.mean(x, axis=-1, keepdims=True)
    xc = x - mu
    var = jnp.mean(xc * xc, axis=-1, keepdims=True)
    rstd = lax.rsqrt(var + EPS)
    xhat = xc * rstd
    return xhat * g + b, xhat, rstd


def _ln_bwd(dy, xhat, rstd, g):
    dxh = dy * g
    m1 = jnp.mean(dxh, axis=-1, keepdims=True)
    m2 = jnp.mean(dxh * xhat, axis=-1, keepdims=True)
    return rstd * (dxh - m1 - xhat * m2)


def _rms_bwd_math(dh, x, g):
    r = lax.rsqrt(jnp.mean(x * x, axis=-1, keepdims=True) + EPS)
    xhat = x * r
    dg = jnp.sum(dh * xhat, axis=0, keepdims=True)
    u = dh * g
    dx = r * (u - xhat * jnp.mean(u * xhat, axis=-1, keepdims=True))
    return dx, dg


def _conv3(xe, cw, halo):
    x0 = xe[halo:]
    x1 = pltpu.roll(xe, 1, 0)[halo:]
    x2 = pltpu.roll(xe, 2, 0)[halo:]
    return cw[2] * x0 + cw[1] * x1 + cw[0] * x2, (x0, x1, x2)


def _conv3_bwd_in(dce, cw, ts):
    n = dce.shape[0]
    d1 = pltpu.roll(dce, n - 1, 0)[:ts]
    d2 = pltpu.roll(dce, n - 2, 0)[:ts]
    return cw[2] * dce[:ts] + cw[1] * d1 + cw[0] * d2


def _conv3_bwd_w(dc, taps):
    x0, x1, x2 = taps
    return [jnp.sum(dc * x2, axis=0, keepdims=True), jnp.sum(dc * x1, axis=0, keepdims=True),
            jnp.sum(dc * x0, axis=0, keepdims=True)]


def _rms_fwd(x, g, name, after=None):
    s = x.shape[0]
    ts = _tile(s, 512)

    def body(x_ref, g_ref, *rest):
        h_ref = rest[-1]
        xv = x_ref[...]
        r = lax.rsqrt(jnp.mean(xv * xv, axis=-1, keepdims=True) + EPS)
        h_ref[...] = (xv * r * g_ref[...]).astype(BF16)

    extra = [] if after is None else [after]
    return pl.pallas_call(
        body, grid=(s // ts,), name=name,
        in_specs=[pl.BlockSpec((ts, D), lambda i: (i, 0)), pl.BlockSpec((1, D), lambda i: (0, 0))]
        + [ANY] * len(extra),
        out_specs=pl.BlockSpec((ts, D), lambda i: (i, 0)),
        out_shape=jax.ShapeDtypeStruct((s, D), BF16),
        compiler_params=_params("parallel"),
    )(x, g, *extra)


MXU_COLS = 256


def _pair(bn):
    return 1 if bn % MXU_COLS == 0 else 2


def _cols(w_ref, b, pair):
    return w_ref[b] if pair == 1 else jnp.concatenate([w_ref[b + q] for q in range(pair)], axis=1)


def _mm_in(h, wblk, name):
    s = h.shape[0]
    nb, _, bn = wblk.shape
    pair = _pair(bn)
    ts = _tile(s, 512)

    def body(h_ref, w_ref, o_ref):
        hv = h_ref[...]
        for b in range(0, nb, pair):
            o_ref[:, b * bn:(b + pair) * bn] = jnp.dot(hv, _cols(w_ref, b, pair),
                                                       preferred_element_type=F32).astype(BF16)

    return pl.pallas_call(
        body, grid=(s // ts,), name=name,
        in_specs=[pl.BlockSpec((ts, D), lambda i: (i, 0)), pl.BlockSpec((nb, D, bn), lambda i: (0, 0, 0))],
        out_specs=pl.BlockSpec((ts, nb * bn), lambda i: (i, 0)),
        out_shape=jax.ShapeDtypeStruct((s, nb * bn), BF16),
        compiler_params=_params("parallel"),
    )(h, wblk)


def _rms_math(xv, g):
    r = lax.rsqrt(jnp.mean(xv * xv, axis=-1, keepdims=True) + EPS)
    return (xv * r * g).astype(BF16)


def _mm_out(y, w, xres, gnext, name):
    s = y.shape[0]
    ts = _tile(s, 512)

    def body(y_ref, w_ref, x_ref, g_ref, o_ref, h_ref):
        xn = x_ref[...] + jnp.dot(y_ref[...], w_ref[...], preferred_element_type=F32)
        o_ref[...] = xn
        h_ref[...] = _rms_math(xn, g_ref[...])

    return pl.pallas_call(
        body, grid=(s // ts,), name=name,
        in_specs=[pl.BlockSpec((ts, D), lambda i: (i, 0)), pl.BlockSpec((D, D), lambda i: (0, 0)),
                  pl.BlockSpec((ts, D), lambda i: (i, 0)), pl.BlockSpec((1, D), lambda i: (0, 0))],
        out_specs=[pl.BlockSpec((ts, D), lambda i: (i, 0)), pl.BlockSpec((ts, D), lambda i: (i, 0))],
        out_shape=[jax.ShapeDtypeStruct((s, D), F32), jax.ShapeDtypeStruct((s, D), BF16)],
        compiler_params=_params("parallel"),
    )(y, w, xres, gnext)


CONV_ROWS = 32


def _rolled_copies(dst_ref, xe, back):
    n = xe.shape[0]
    dst_ref[0] = xe
    for r in range(1, 8):
        dst_ref[r] = pltpu.roll(xe, n - r if back else r, 0)


def _conv31(rolled_ref, cw_ref, ts, out_ref, bias):
    for o in range(0, ts, CONV_ROWS):
        acc = jnp.zeros((CONV_ROWS, DA), F32) + bias
        for sh in range(BCONV):
            q, r = divmod(sh, 8)
            lo = HALO_B - 8 * q + o
            acc = acc + cw_ref[BCONV - 1 - sh:BCONV - sh, :] * rolled_ref[r, lo:lo + CONV_ROWS, :]
        out_ref[o:o + CONV_ROWS, :] = acc


def _ab_fwd(z, lga, lba, wsm, bs_col, cwb, cbb, lgb, lbb, name):
    s = z.shape[0]
    ts = _tile(s, 256)
    hb = ts // HALO_B

    def body(z_ref, zh_ref, lga_ref, lba_ref, ws_ref, bs_ref, cw_ref, cb_ref, lgb_ref, lbb_ref,
             y_ref, yb2_ref, rolled):
        i = pl.program_id(0)
        z_t = z_ref[...].astype(F32)
        gu = _gelu(z_t[:, 0:DA])
        gv = _gelu(z_t[:, DA:2 * DA])
        vn, _, _ = _ln_fwd(gv, lga_ref[...], lba_ref[...])
        vnb = vn.astype(BF16)
        for c in range(ts // CHUNK):
            for h in range(HEADS):
                rs = slice(c * CHUNK, (c + 1) * CHUNK)
                cs = slice(h * CHUNK, (h + 1) * CHUNK)
                mixed = jnp.dot(ws_ref[h], vnb[rs, cs], preferred_element_type=F32) + bs_ref[h]
                y_ref[rs, cs] = (gu[rs, cs] * mixed).astype(BF16)
        zh = jnp.where(i > 0, zh_ref[...], jnp.zeros_like(zh_ref[...])).astype(F32)
        xb = jnp.concatenate([zh[:, 0:DA], z_t[:, 2 * DA:3 * DA]], axis=0)
        gb = jnp.concatenate([zh[:, DA:2 * DA], z_t[:, 3 * DA:4 * DA]], axis=0)
        _rolled_copies(rolled, xb * _sigmoid(gb), False)
        _conv31(rolled, cw_ref, ts, yb2_ref, cb_ref[...])
        nb_, _, _ = _ln_fwd(yb2_ref[...], lgb_ref[...], lbb_ref[...])
        y_ref[:, DA:2 * DA] = (nb_ * _sigmoid(nb_)).astype(BF16)

    row = lambda i: (0, 0)
    return pl.pallas_call(
        body, grid=(s // ts,), name=name,
        in_specs=[pl.BlockSpec((ts, 4 * DA), lambda i: (i, 0)),
                  pl.BlockSpec((HALO_B, 2 * DA), lambda i: (jnp.maximum(i * hb - 1, 0), 1)),
                  pl.BlockSpec((1, DA), row), pl.BlockSpec((1, DA), row),
                  pl.BlockSpec((HEADS, CHUNK, CHUNK), lambda i: (0, 0, 0)),
                  pl.BlockSpec((HEADS, CHUNK, 1), lambda i: (0, 0, 0)),
                  pl.BlockSpec((BCONV, DA), row), pl.BlockSpec((1, DA), row),
                  pl.BlockSpec((1, DA), row), pl.BlockSpec((1, DA), row)],
        out_specs=[pl.BlockSpec((ts, 2 * DA), lambda i: (i, 0)), pl.BlockSpec((ts, DA), lambda i: (i, 0))],
        out_shape=[jax.ShapeDtypeStruct((s, 2 * DA), BF16), jax.ShapeDtypeStruct((s, DA), F32)],
        scratch_shapes=[pltpu.VMEM((8, ts + HALO_B, DA), F32)],
        compiler_params=_params("parallel"),
    )(z, z, lga, lba, wsm, bs_col, cwb, cbb, lgb, lbb)


def _c_fwd(zc, cw, name):
    s = zc.shape[0]
    ts = _tile(s, 512)
    hb = ts // HALO

    def body(z_ref, ch_ref, xh_ref, cw_ref, r_ref):
        i = pl.program_id(0)
        z_t = z_ref[...].astype(F32)
        ph = jnp.where(i > 0, ch_ref[...].astype(F32) * xh_ref[...].astype(F32), 0.0)
        pe = jnp.concatenate([ph, z_t[:, D:2 * D] * z_t[:, 2 * D:3 * D]], axis=0)
        q, _ = _conv3(pe, [cw_ref[k:k + 1, :] for k in range(3)], HALO)
        r_ref[...] = (z_t[:, 0:D] * q).astype(BF16)

    halo = lambda col: pl.BlockSpec((HALO, D), lambda i: (jnp.maximum(i * hb - 1, 0), col))
    return pl.pallas_call(
        body, grid=(s // ts,), name=name,
        in_specs=[pl.BlockSpec((ts, 3 * D), lambda i: (i, 0)), halo(1), halo(2),
                  pl.BlockSpec((3, D), lambda i: (0, 0))],
        out_specs=pl.BlockSpec((ts, D), lambda i: (i, 0)),
        out_shape=jax.ShapeDtypeStruct((s, D), BF16),
        compiler_params=_params("parallel"),
    )(zc, zc, zc, cw)


def _ffn_fwd(h, xres, wup, fcw, wdn, gnext, name):
    s = h.shape[0]
    ts = _tile(s, 512)
    hb = ts // HALO

    def body(h_ref, hh_ref, w_ref, cw_ref, wd_ref, x_ref, *rest):
        if gnext is not None:
            gn_ref, up_ref, upc_ref, xo_ref, hn_ref, up_s = rest
        else:
            up_ref, upc_ref, xo_ref, up_s = rest
        i = pl.program_id(0)
        m = pl.program_id(1)
        @pl.when(m == 0)
        def _():
            xo_ref[...] = x_ref[...]

        halo = jnp.where(i > 0, hh_ref[...], jnp.zeros_like(hh_ref[...]))
        hx = jnp.concatenate([halo, h_ref[...]], axis=0)
        acts = []
        for gv in range(2):
            up_s[gv] = jnp.dot(hx, w_ref[gv], preferred_element_type=F32)
            x0 = up_s[gv, HALO:HALO + ts, :]
            up_ref[gv] = x0.astype(BF16)
            upc = (cw_ref[gv, 2:3, :] * x0 + cw_ref[gv, 1:2, :] * up_s[gv, HALO - 1:HALO - 1 + ts, :]
                   + cw_ref[gv, 0:1, :] * up_s[gv, HALO - 2:HALO - 2 + ts, :])
            upc_ref[gv] = upc.astype(BF16)
            acts.append(upc)
        a = acts[0] * _sigmoid(acts[0]) * acts[1]
        xo_ref[...] += jnp.dot(a.astype(BF16), wd_ref[...], preferred_element_type=F32)

        if gnext is not None:
            @pl.when(m == NG - 1)
            def _():
                hn_ref[...] = _rms_math(xo_ref[...], gn_ref[...])

    tile = pl.BlockSpec((ts, D), lambda i, m: (i, 0))
    nxt = gnext is not None
    return pl.pallas_call(
        body, grid=(s // ts, NG), name=name,
        in_specs=[tile,
                  pl.BlockSpec((HALO, D), lambda i, m: (jnp.maximum(i * hb - 1, 0), 0)),
                  pl.BlockSpec((2, None, D, FB), lambda i, m: (0, m, 0, 0)),
                  pl.BlockSpec((2, None, 3, FB), lambda i, m: (0, m, 0, 0)),
                  pl.BlockSpec((FB, D), lambda i, m: (m, 0)),
                  tile] + ([pl.BlockSpec((1, D), lambda i, m: (0, 0))] if nxt else []),
        out_specs=[pl.BlockSpec((None, 2, ts, FB), lambda i, m: (m, 0, i, 0)),
                   pl.BlockSpec((None, 2, ts, FB), lambda i, m: (m, 0, i, 0)),
                   tile] + ([tile] if nxt else []),
        out_shape=[jax.ShapeDtypeStruct((NG, 2, s, FB), BF16), jax.ShapeDtypeStruct((NG, 2, s, FB), BF16),
                   jax.ShapeDtypeStruct((s, D), F32)] + ([jax.ShapeDtypeStruct((s, D), BF16)] if nxt else []),
        scratch_shapes=[pltpu.VMEM((2, ts + HALO, FB), F32)],
        compiler_params=_params("arbitrary", "arbitrary"),
    )(h, h, wup, fcw, wdn, xres, *([gnext] if nxt else []))


def _final(x, tgt, g, name):
    s = x.shape[0]
    ts = _tile(s, 512)

    def body(x_ref, t_ref, g_ref, dx_ref, dxb_ref, dg_ref, loss_ref):
        i = pl.program_id(0)
        xv = x_ref[...]
        gv = g_ref[...]
        r = lax.rsqrt(jnp.mean(xv * xv, axis=-1, keepdims=True) + EPS)
        xhat = xv * r
        e = xhat * gv - t_ref[...]
        part = 0.5 * jnp.sum(jnp.mean(e * e, axis=-1, keepdims=True), axis=0, keepdims=True)
        dy = e * (1.0 / D)
        dgp = jnp.sum(dy * xhat, axis=0, keepdims=True)
        u = dy * gv
        dx = r * (u - xhat * jnp.mean(u * xhat, axis=-1, keepdims=True))
        dx_ref[...] = dx
        dxb_ref[...] = dx.astype(BF16)

        @pl.when(i == 0)
        def _():
            dg_ref[...] = dgp
            loss_ref[...] = jnp.broadcast_to(part, (1, 128))

        @pl.when(i > 0)
        def _():
            dg_ref[...] += dgp
            loss_ref[...] += jnp.broadcast_to(part, (1, 128))

    return pl.pallas_call(
        body, grid=(s // ts,), name=name,
        in_specs=[pl.BlockSpec((ts, D), lambda i: (i, 0)), pl.BlockSpec((ts, D), lambda i: (i, 0)),
                  pl.BlockSpec((1, D), lambda i: (0, 0))],
        out_specs=[pl.BlockSpec((ts, D), lambda i: (i, 0)), pl.BlockSpec((ts, D), lambda i: (i, 0)),
                   pl.BlockSpec((1, D), lambda i: (0, 0)), pl.BlockSpec((1, 128), lambda i: (0, 0))],
        out_shape=[jax.ShapeDtypeStruct((s, D), F32), jax.ShapeDtypeStruct((s, D), BF16),
                   jax.ShapeDtypeStruct((1, D), F32), jax.ShapeDtypeStruct((1, 128), F32)],
        compiler_params=_params("arbitrary"),
    )(x, tgt, g)


def _ffn_bwd(df, up, upc, wup, fcw, wdn, xin, g, name):
    s = df.shape[0]
    ts = _tile(s, 512)
    nt = s // ts

    def body(df_ref, up_ref, upc_ref, w_ref, cw_ref, wd_ref, x_ref, g_ref,
             a_ref, dup_ref, dx_ref, dxb_ref, dg_ref, dcw_ref, carry, acc, tacc, dcs_ref):
        i = pl.program_id(0)
        m = pl.program_id(1)
        first = i == 0
        @pl.when(first)
        def _():
            carry[m] = jnp.zeros((2, 8, FB), F32)
            dcw_ref[m] = jnp.zeros((2, 3, FB), F32)

        @pl.when(m == 0)
        def _():
            acc[...] = jnp.zeros((ts, D), F32)

        cws = [[cw_ref[gv, k:k + 1, :] for k in range(3)] for gv in range(2)]
        part = ts // NPART
        das = [lax.dot_general(df_ref[p * part:(p + 1) * part, :].astype(BF16), wd_ref[...], NT_DIMS,
                               preferred_element_type=F32) for p in range(NPART)]

        tacc[...] = jnp.zeros((2, 3, 8, FB), F32)
        dcs_ref[:, ts:ts + 8, :] = carry[m]
        for r in reversed(range(ts // RC)):
            rs = slice(r * RC, (r + 1) * RC)
            gate = upc_ref[0, rs, :].astype(F32)
            val = upc_ref[1, rs, :].astype(F32)
            sg = _sigmoid(gate)
            sl = gate * sg
            a_ref[rs, :] = (sl * val).astype(BF16)
            da_c = das[(r * RC) // part][(r * RC) % part:(r * RC) % part + RC]
            dcs = [da_c * val * _dsilu(gate, sg), da_c * sl]
            for gv in range(2):
                dc = dcs[gv]
                dcs_ref[gv, rs, :] = dc
                d1 = dcs_ref[gv, r * RC + 1:(r + 1) * RC + 1, :]
                d2 = dcs_ref[gv, r * RC + 2:(r + 1) * RC + 2, :]
                du = cws[gv][2] * dc + cws[gv][1] * d1 + cws[gv][0] * d2
                dup_ref[gv, rs, :] = du.astype(BF16)
                x0 = up_ref[gv, rs, :].astype(F32)
                for k, dk in enumerate((d2, d1, dc)):
                    p = x0 * dk
                    tacc[gv, k] += sum(p[j:j + 8] for j in range(0, RC, 8))
            if (r * RC) % part == 0:
                ps = slice(r * RC, r * RC + part)
                acc[ps, :] += (
                    lax.dot_general(dup_ref[0, ps, :], w_ref[0], NT_DIMS, preferred_element_type=F32)
                    + lax.dot_general(dup_ref[1, ps, :], w_ref[1], NT_DIMS, preferred_element_type=F32))
        for gv in range(2):
            carry[m, gv] = dcs_ref[gv, 0:8, :]
            for k in range(3):
                dcw_ref[m, gv, k:k + 1, :] += jnp.sum(tacc[gv, k], axis=0, keepdims=True)

        @pl.when(m == NG - 1)
        def _():
            dx, dgp = _rms_bwd_math(acc[...], x_ref[...], g_ref[...])
            dx = df_ref[...] + dx
            dx_ref[...] = dx
            dxb_ref[...] = dx.astype(BF16)

            @pl.when(first)
            def _():
                dg_ref[...] = dgp

            @pl.when(jnp.logical_not(first))
            def _():
                dg_ref[...] += dgp

    rev = lambda i: nt - 1 - i
    return pl.pallas_call(
        body, grid=(nt, NG), name=name,
        in_specs=[pl.BlockSpec((ts, D), lambda i, m: (rev(i), 0)),
                  pl.BlockSpec((None, 2, ts, FB), lambda i, m: (m, 0, rev(i), 0)),
                  pl.BlockSpec((None, 2, ts, FB), lambda i, m: (m, 0, rev(i), 0)),
                  pl.BlockSpec((2, None, D, FB), lambda i, m: (0, m, 0, 0)),
                  pl.BlockSpec((2, None, 3, FB), lambda i, m: (0, m, 0, 0)),
                  pl.BlockSpec((FB, D), lambda i, m: (m, 0)),
                  pl.BlockSpec((ts, D), lambda i, m: (rev(i), 0)),
                  pl.BlockSpec((1, D), lambda i, m: (0, 0))],
        out_specs=[pl.BlockSpec((None, ts, FB), lambda i, m: (m, rev(i), 0)),
                   pl.BlockSpec((None, 2, ts, FB), lambda i, m: (m, 0, rev(i), 0)),
                   pl.BlockSpec((ts, D), lambda i, m: (rev(i), 0)),
                   pl.BlockSpec((ts, D), lambda i, m: (rev(i), 0)),
                   pl.BlockSpec((1, D), lambda i, m: (0, 0)),
                   pl.BlockSpec((NG, 2, 3, FB), lambda i, m: (0, 0, 0, 0))],
        out_shape=[jax.ShapeDtypeStruct((NG, s, FB), BF16), jax.ShapeDtypeStruct((NG, 2, s, FB), BF16),
                   jax.ShapeDtypeStruct((s, D), F32), jax.ShapeDtypeStruct((s, D), BF16),
                   jax.ShapeDtypeStruct((1, D), F32),
                   jax.ShapeDtypeStruct((NG, 2, 3, FB), F32)],
        scratch_shapes=[pltpu.VMEM((NG, 2, 8, FB), F32), pltpu.VMEM((ts, D), F32),
                        pltpu.VMEM((2, 3, 8, FB), F32), pltpu.VMEM((2, ts + 8, FB), F32)],
        compiler_params=_params("arbitrary", "arbitrary"),
    )(df, up, upc, wup, fcw, wdn, xin, g)


def _mm_nt(dy, w, name):
    s = dy.shape[0]
    ts = _tile(s, 512)

    def body(dy_ref, w_ref, o_ref):
        o_ref[...] = lax.dot_general(dy_ref[...], w_ref[...], NT_DIMS,
                                     preferred_element_type=F32).astype(BF16)

    return pl.pallas_call(
        body, grid=(s // ts,), name=name,
        in_specs=[pl.BlockSpec((ts, D), lambda i: (i, 0)), pl.BlockSpec((D, D), lambda i: (0, 0))],
        out_specs=pl.BlockSpec((ts, D), lambda i: (i, 0)),
        out_shape=jax.ShapeDtypeStruct((s, D), BF16),
        compiler_params=_params("parallel"),
    )(dy, w)


def _mm_nt_rms(dy, wblk, x, g, dres, bf16_copy, name):
    s = dy.shape[0]
    nb, _, bn = wblk.shape
    pair = _pair(bn)
    ts = _tile(s, 512)

    def body(dy_ref, w_ref, x_ref, g_ref, dr_ref, dx_ref, *rest):
        dg_ref = rest[-1]
        i = pl.program_id(0)
        acc = jnp.zeros((ts, D), F32)
        for b in range(0, nb, pair):
            acc = acc + lax.dot_general(dy_ref[:, b * bn:(b + pair) * bn], _cols(w_ref, b, pair), NT_DIMS,
                                        preferred_element_type=F32)
        dx, dgp = _rms_bwd_math(acc, x_ref[...], g_ref[...])
        dx = dr_ref[...] + dx
        dx_ref[...] = dx
        if bf16_copy:
            rest[0][...] = dx.astype(BF16)

        @pl.when(i == 0)
        def _():
            dg_ref[...] = dgp

        @pl.when(i > 0)
        def _():
            dg_ref[...] += dgp

    tile = pl.BlockSpec((ts, D), lambda i: (i, 0))
    return pl.pallas_call(
        body, grid=(s // ts,), name=name,
        in_specs=[pl.BlockSpec((ts, nb * bn), lambda i: (i, 0)), pl.BlockSpec((nb, D, bn), lambda i: (0, 0, 0)),
                  tile, pl.BlockSpec((1, D), lambda i: (0, 0)), tile],
        out_specs=[tile] + ([tile] if bf16_copy else []) + [pl.BlockSpec((1, D), lambda i: (0, 0))],
        out_shape=[jax.ShapeDtypeStruct((s, D), F32)] + ([jax.ShapeDtypeStruct((s, D), BF16)] if bf16_copy else [])
        + [jax.ShapeDtypeStruct((1, D), F32)],
        compiler_params=_params("arbitrary"),
    )(dy, wblk, x, g, dres)


def _c_bwd(dr, zc, cw, name):
    s = dr.shape[0]
    ts = _tile(s, 512)
    nt = s // ts
    hb = ts // HALO

    def body(dr_ref, drf_ref, z_ref, ch_ref, xh_ref, bf_ref, cw_ref, dz_ref, dcw_ref):
        i = pl.program_id(0)
        cwv = [cw_ref[k:k + 1, :] for k in range(3)]
        z_t = z_ref[...].astype(F32)
        bg, cg, xv = z_t[:, 0:D], z_t[:, D:2 * D], z_t[:, 2 * D:3 * D]
        ph = jnp.where(i > 0, ch_ref[...].astype(F32) * xh_ref[...].astype(F32), 0.0)
        pe = jnp.concatenate([ph, cg * xv], axis=0)
        q, taps = _conv3(pe, cwv, HALO)
        drv = dr_ref[...].astype(F32)
        dq = drv * bg
        dqf = jnp.where(i < nt - 1, drf_ref[...].astype(F32) * bf_ref[...].astype(F32), 0.0)
        dp = _conv3_bwd_in(jnp.concatenate([dq, dqf], axis=0), cwv, ts)
        dz_ref[:, 0:D] = (drv * q).astype(BF16)
        dz_ref[:, D:2 * D] = (dp * xv).astype(BF16)
        dz_ref[:, 2 * D:3 * D] = (dp * cg).astype(BF16)
        rows = _conv3_bwd_w(dq, taps)

        @pl.when(i == 0)
        def _():
            for k in range(3):
                dcw_ref[k:k + 1, :] = rows[k]

        @pl.when(i > 0)
        def _():
            for k in range(3):
                dcw_ref[k:k + 1, :] += rows[k]

    past = lambda col: pl.BlockSpec((HALO, D), lambda i: (jnp.maximum(i * hb - 1, 0), col))
    nxt = lambda i: jnp.minimum((i + 1) * hb, s // HALO - 1)
    return pl.pallas_call(
        body, grid=(nt,), name=name,
        in_specs=[pl.BlockSpec((ts, D), lambda i: (i, 0)),
                  pl.BlockSpec((HALO, D), lambda i: (nxt(i), 0)),
                  pl.BlockSpec((ts, 3 * D), lambda i: (i, 0)), past(1), past(2),
                  pl.BlockSpec((HALO, D), lambda i: (nxt(i), 0)),
                  pl.BlockSpec((3, D), lambda i: (0, 0))],
        out_specs=[pl.BlockSpec((ts, 3 * D), lambda i: (i, 0)), pl.BlockSpec((3, D), lambda i: (0, 0))],
        out_shape=[jax.ShapeDtypeStruct((s, 3 * D), BF16), jax.ShapeDtypeStruct((3, D), F32)],
        compiler_params=_params("arbitrary"),
    )(dr, dr, zc, zc, zc, zc, cw)


G512_ROWS = 40


def _ab_bwd(dy, z, yb2, lga, lba, wsm, bs_col, cwb, lgb, lbb, name):
    s = z.shape[0]
    ts = _tile(s, 256)
    nt = s // ts
    hb = ts // HALO_B
    nch = ts // CHUNK
    tri = None

    def body(z_ref, zh_ref, dy_ref, dyf_ref, yb2_ref, yb2f_ref, lga_ref, lba_ref, ws_ref, bs_ref,
             cw_ref, lgb_ref, lbb_ref, dz_ref, g512_ref, dws_ref, dbs_ref, dvn_ref, fwd_rolled, bwd_rolled, du_s):
        i = pl.program_id(0)
        last = i == nt - 1

        @pl.when(i == 0)
        def _():
            g512_ref[...] = jnp.zeros((G512_ROWS, DA), F32)
            dws_ref[...] = jnp.zeros((HEADS, CHUNK, CHUNK), F32)
            dbs_ref[...] = jnp.zeros((HEADS, CHUNK, 1), F32)

        def add_row(k, v):
            g512_ref[k:k + 1, :] += v

        z_t = z_ref[...].astype(F32)
        dy_t = dy_ref[...].astype(F32)
        ua, va = z_t[:, 0:DA], z_t[:, DA:2 * DA]
        gu = _gelu(ua)
        gv = _gelu(va)
        lga_v = lga_ref[...]
        vn, xhat_a, rstd_a = _ln_fwd(gv, lga_v, lba_ref[...])
        vnb = vn.astype(BF16)
        causal = (lax.broadcasted_iota(jnp.int32, (CHUNK, CHUNK), 0)
                  >= lax.broadcasted_iota(jnp.int32, (CHUNK, CHUNK), 1)).astype(F32)
        for c in range(nch):
            for h in range(HEADS):
                rs = slice(c * CHUNK, (c + 1) * CHUNK)
                cs = slice(h * CHUNK, (h + 1) * CHUNK)
                vblk = vnb[rs, cs]
                mixed = jnp.dot(ws_ref[h], vblk, preferred_element_type=F32) + bs_ref[h]
                dyb_ = dy_t[rs, cs]
                dmix = dyb_ * gu[rs, cs]
                dmb = dmix.astype(BF16)
                dz_ref[rs, cs] = (dyb_ * mixed * _dgelu(ua[rs, cs])).astype(BF16)
                dvn_ref[rs, cs] = lax.dot_general(ws_ref[h], dmb, TN_DIMS, preferred_element_type=F32)
                dws_ref[h] += causal * lax.dot_general(dmb, vblk, NT_DIMS, preferred_element_type=F32)
                dbs_ref[h] += jnp.sum(dmix, axis=1, keepdims=True)
        dvn = dvn_ref[...]
        add_row(0, jnp.sum(dvn * xhat_a, axis=0, keepdims=True))
        add_row(1, jnp.sum(dvn, axis=0, keepdims=True))
        dgv = _ln_bwd(dvn, xhat_a, rstd_a, lga_v)
        dz_ref[:, DA:2 * DA] = (dgv * _dgelu(va)).astype(BF16)
        lgb_v = lgb_ref[...]
        dyb_e = jnp.concatenate(
            [dy_t[:, DA:2 * DA], jnp.where(last, 0.0, dyf_ref[...].astype(F32))], axis=0)
        yb2_e = jnp.concatenate([yb2_ref[...], jnp.where(last, 0.0, yb2f_ref[...])], axis=0)
        n_e, xhat_b, rstd_b = _ln_fwd(yb2_e, lgb_v, lbb_ref[...])
        sgn = _sigmoid(n_e)
        dn = dyb_e * _dsilu(n_e, sgn)
        dy2 = _ln_bwd(dn, xhat_b, rstd_b, lgb_v)
        add_row(2, jnp.sum(dy2[:ts], axis=0, keepdims=True))
        add_row(3, jnp.sum(dn[:ts] * xhat_b[:ts], axis=0, keepdims=True))
        add_row(4, jnp.sum(dn[:ts], axis=0, keepdims=True))
        zh = jnp.where(i > 0, zh_ref[...], jnp.zeros_like(zh_ref[...])).astype(F32)
        xb_t, gb_t = z_t[:, 2 * DA:3 * DA], z_t[:, 3 * DA:4 * DA]
        sgb = _sigmoid(gb_t)
        _rolled_copies(fwd_rolled, jnp.concatenate(
            [zh[:, 0:DA] * _sigmoid(zh[:, DA:2 * DA]), xb_t * sgb], axis=0), False)
        _rolled_copies(bwd_rolled, dy2, True)
        for o in range(0, ts, CONV_ROWS):
            acc = jnp.zeros((CONV_ROWS, DA), F32)
            for sh in range(BCONV):
                q, r = divmod(sh, 8)
                acc = acc + cw_ref[BCONV - 1 - sh:BCONV - sh, :] * bwd_rolled[r, 8 * q + o:8 * q + o + CONV_ROWS, :]
            du_s[o:o + CONV_ROWS, :] = acc
        for sh in range(BCONV):
            q, r = divmod(sh, 8)
            acc = jnp.zeros((CONV_ROWS, DA), F32)
            for o in range(0, ts, CONV_ROWS):
                lo = HALO_B - 8 * q + o
                acc = acc + bwd_rolled[0, o:o + CONV_ROWS, :] * fwd_rolled[r, lo:lo + CONV_ROWS, :]
            add_row(8 + BCONV - 1 - sh, jnp.sum(acc, axis=0, keepdims=True))
        du = du_s[...]
        dz_ref[:, 2 * DA:3 * DA] = (du * sgb).astype(BF16)
        dz_ref[:, 3 * DA:4 * DA] = (du * xb_t * sgb * (1.0 - sgb)).astype(BF16)

    row = lambda i: (0, 0)
    nxt = lambda i: jnp.minimum((i + 1) * hb, s // HALO_B - 1)
    return pl.pallas_call(
        body, grid=(nt,), name=name,
        in_specs=[pl.BlockSpec((ts, 4 * DA), lambda i: (i, 0)),
                  pl.BlockSpec((HALO_B, 2 * DA), lambda i: (jnp.maximum(i * hb - 1, 0), 1)),
                  pl.BlockSpec((ts, 2 * DA), lambda i: (i, 0)),
                  pl.BlockSpec((HALO_B, DA), lambda i: (nxt(i), 1)),
                  pl.BlockSpec((ts, DA), lambda i: (i, 0)),
                  pl.BlockSpec((HALO_B, DA), lambda i: (nxt(i), 0)),
                  pl.BlockSpec((1, DA), row), pl.BlockSpec((1, DA), row),
                  pl.BlockSpec((HEADS, CHUNK, CHUNK), lambda i: (0, 0, 0)),
                  pl.BlockSpec((HEADS, CHUNK, 1), lambda i: (0, 0, 0)),
                  pl.BlockSpec((BCONV, DA), row), pl.BlockSpec((1, DA), row), pl.BlockSpec((1, DA), row)],
        out_specs=[pl.BlockSpec((ts, 4 * DA), lambda i: (i, 0)),
                   pl.BlockSpec((G512_ROWS, DA), row),
                   pl.BlockSpec((HEADS, CHUNK, CHUNK), lambda i: (0, 0, 0)),
                   pl.BlockSpec((HEADS, CHUNK, 1), lambda i: (0, 0, 0))],
        out_shape=[jax.ShapeDtypeStruct((s, 4 * DA), BF16), jax.ShapeDtypeStruct((G512_ROWS, DA), F32),
                   jax.ShapeDtypeStruct((HEADS, CHUNK, CHUNK), F32),
                   jax.ShapeDtypeStruct((HEADS, CHUNK, 1), F32)],
        scratch_shapes=[pltpu.VMEM((ts, DA), F32), pltpu.VMEM((8, ts + HALO_B, DA), F32),
                        pltpu.VMEM((8, ts + HALO_B, DA), F32), pltpu.VMEM((ts, DA), F32)],
        compiler_params=_params("arbitrary"),
    )(z, z, dy, dy, yb2, yb2, lga, lba, wsm, bs_col, cwb, lgb, lbb)


def _dw_cols(a, dy, nb, bn, name):
    s = a.shape[0]
    tm = _tile(s, 2048)
    nt = s // tm
    cpb = 4

    def body(a_ref, dy_ref, o_ref, acc):
        t = pl.program_id(1)
        p = lax.dot_general(a_ref[...], dy_ref[...], TN_DIMS, preferred_element_type=F32)

        @pl.when(t == 0)
        def _():
            for q in range(cpb):
                acc[q] = p[:, q * bn:(q + 1) * bn]

        @pl.when(t > 0)
        def _():
            for q in range(cpb):
                acc[q] += p[:, q * bn:(q + 1) * bn]

        @pl.when(t == nt - 1)
        def _():
            o_ref[...] = acc[...].astype(BF16)

    return pl.pallas_call(
        body, grid=(nb // cpb, nt), name=name,
        in_specs=[pl.BlockSpec((tm, D), lambda j, t: (t, 0)), pl.BlockSpec((tm, cpb * bn), lambda j, t: (t, j))],
        out_specs=pl.BlockSpec((cpb, D, bn), lambda j, t: (j, 0, 0)),
        out_shape=jax.ShapeDtypeStruct((nb, D, bn), BF16),
        scratch_shapes=[pltpu.VMEM((cpb, D, bn), F32)],
        compiler_params=_params("arbitrary", "arbitrary"),
    )(a, dy)


def _dw_rows(a, dy, name):
    s = a.shape[0]
    tm = _tile(s, 4096)
    nt = s // tm
    rb = 512

    def body(a_ref, dy_ref, o_ref, acc):
        t = pl.program_id(1)
        p = lax.dot_general(a_ref[...], dy_ref[...], TN_DIMS, preferred_element_type=F32)

        @pl.when(t == 0)
        def _():
            acc[...] = p

        @pl.when(t > 0)
        def _():
            acc[...] += p

        @pl.when(t == nt - 1)
        def _():
            o_ref[...] = acc[...].astype(BF16)

    return pl.pallas_call(
        body, grid=(D // rb, nt), name=name,
        in_specs=[pl.BlockSpec((tm, rb), lambda j, t: (t, j)), pl.BlockSpec((tm, D), lambda j, t: (t, 0))],
        out_specs=pl.BlockSpec((rb, D), lambda j, t: (j, 0)),
        out_shape=jax.ShapeDtypeStruct((D, D), BF16),
        scratch_shapes=[pltpu.VMEM((rb, D), F32)],
        compiler_params=_params("arbitrary", "arbitrary"),
    )(a, dy)


def _dw_up(h, dup, name):
    s = h.shape[0]
    tm = _tile(s, 4096)
    nt = s // tm

    def body(h_ref, d_ref, o_ref, acc):
        t = pl.program_id(1)
        p = lax.dot_general(d_ref[...], h_ref[...], TN_DIMS, preferred_element_type=F32)

        @pl.when(t == 0)
        def _():
            acc[...] = p

        @pl.when(t > 0)
        def _():
            acc[...] += p

        @pl.when(t == nt - 1)
        def _():
            o_ref[...] = acc[...].astype(BF16)

    return pl.pallas_call(
        body, grid=(NDEV, nt), name=name,
        in_specs=[pl.BlockSpec((tm, D), lambda b, t: (t, 0)),
                  pl.BlockSpec((None, None, tm, FB), lambda b, t: (b % NG, b // NG, t, 0))],
        out_specs=pl.BlockSpec((None, FB, D), lambda b, t: (b, 0, 0)),
        out_shape=jax.ShapeDtypeStruct((NDEV, FB, D), BF16),
        scratch_shapes=[pltpu.VMEM((FB, D), F32)],
        compiler_params=_params("arbitrary", "arbitrary"),
    )(h, dup)


def _dw_dn(a, df, name):
    s = df.shape[0]
    tm = _tile(s, 4096)
    nt = s // tm

    def body(a_ref, d_ref, o_ref, acc):
        t = pl.program_id(1)
        p = lax.dot_general(a_ref[...], d_ref[...], TN_DIMS, preferred_element_type=F32)

        @pl.when(t == 0)
        def _():
            acc[...] = p

        @pl.when(t > 0)
        def _():
            acc[...] += p

        @pl.when(t == nt - 1)
        def _():
            o_ref[...] = acc[...].astype(BF16)

    return pl.pallas_call(
        body, grid=(NG, nt), name=name,
        in_specs=[pl.BlockSpec((None, tm, FB), lambda m, t: (m, t, 0)), pl.BlockSpec((tm, D), lambda m, t: (t, 0))],
        out_specs=pl.BlockSpec((FB, D), lambda m, t: (m, 0)),
        out_shape=jax.ShapeDtypeStruct((DFF, D), BF16),
        scratch_shapes=[pltpu.VMEM((FB, D), F32)],
        compiler_params=_params("arbitrary", "arbitrary"),
    )(a, df)


def _place():
    x, y, c = lax.axis_index("x"), lax.axis_index("y"), lax.axis_index("c")
    chips = [(1 - x, y), (x, 1 - y), (1 - x, 1 - y)]
    return x, y, c, chips


def _zone(shard, dev):
    return lax.dynamic_update_slice(lax.empty((NDEV,) + shard.shape, shard.dtype), shard[None],
                                    (dev,) + (0,) * shard.ndim)


HBM_SPEC = pl.BlockSpec(memory_space=pltpu.HBM)
SEM_SPEC = pl.BlockSpec(memory_space=pltpu.SEMAPHORE)
DATAFLOW = pltpu.SideEffectType.DATAFLOW_SIDE_EFFECTING


def _hbm(a):
    return pltpu.with_memory_space_constraint(a, pltpu.HBM)


def _hbm_like(arrs):
    return [pltpu.HBM(a.shape, a.dtype) for a in arrs]


def _ag_start(srcs, lands, after, name):
    n = len(srcs)
    ns = 8 * n

    def body(*refs):
        src, land = refs[:n], refs[n:2 * n]
        sems = refs[2 * n + 1:2 * n + 1 + ns]
        token = refs[-1]
        x, y, c, chips = _place()
        peers = [(x, y, 1 - c)] + [(*chip, c) for chip in chips]
        for t in range(n):
            for k, to in enumerate(peers):
                pltpu.make_async_remote_copy(
                    src_ref=src[t], dst_ref=land[t].at[4 * x + 2 * y + c],
                    send_sem=sems[2 * (4 * t + k)], recv_sem=sems[2 * (4 * t + k) + 1],
                    device_id=to, device_id_type=MESH).start()
        token[...] = jnp.zeros_like(token)

    res = pl.pallas_call(
        body, name=name,
        in_specs=[HBM_SPEC] * (2 * n) + [ANY],
        out_specs=[SEM_SPEC] * ns + [HBM_SPEC] * (2 * n) + [pl.BlockSpec(memory_space=pltpu.VMEM)],
        out_shape=[pltpu.SemaphoreType.DMA(())] * ns + _hbm_like(srcs) + _hbm_like(lands)
        + [jax.ShapeDtypeStruct((8, 128), F32)],
        input_output_aliases={i: ns + i for i in range(2 * n)},
        compiler_params=pltpu.CompilerParams(has_side_effects=DATAFLOW),
    )(*[_hbm(a) for a in srcs], *[_hbm(a) for a in lands], after)
    sems = [[(res[2 * (4 * t + k)], res[2 * (4 * t + k) + 1]) for k in range(4)] for t in range(n)]
    return sems, res[ns:ns + n], res[ns + n:ns + 2 * n], res[-1]


def _ag_forward(srcs, lands, sems1, after, name):
    n = len(srcs)
    flat1 = [s for t in range(n) for k in range(1, 4) for s in sems1[t][k]]
    n1 = len(flat1)

    def body(*refs):
        src, land = refs[:n], refs[n:2 * n]
        s1 = refs[2 * n:2 * n + n1]
        s2 = refs[2 * n + n1 + 1:2 * n + n1 + 1 + 6 * n]
        x, y, c, chips = _place()
        for j, (cx, cy) in enumerate(chips):
            for t in range(n):
                blk = land[t].at[4 * cx + 2 * cy + c]
                pltpu.make_async_remote_copy(
                    src_ref=src[t], dst_ref=blk, send_sem=s1[2 * (3 * t + j)], recv_sem=s1[2 * (3 * t + j) + 1],
                    device_id=(cx, cy, c), device_id_type=MESH).wait_recv()
                pltpu.make_async_remote_copy(
                    src_ref=blk, dst_ref=blk, send_sem=s2[2 * (3 * t + j)], recv_sem=s2[2 * (3 * t + j) + 1],
                    device_id=(x, y, 1 - c), device_id_type=MESH).start()

    res = pl.pallas_call(
        body, name=name,
        in_specs=[HBM_SPEC] * (2 * n) + [SEM_SPEC] * n1 + [ANY],
        out_specs=[SEM_SPEC] * (6 * n) + [HBM_SPEC] * n,
        out_shape=[pltpu.SemaphoreType.DMA(())] * (6 * n) + _hbm_like(lands),
        input_output_aliases={n + i: 6 * n + i for i in range(n)},
        compiler_params=pltpu.CompilerParams(has_side_effects=DATAFLOW),
    )(*srcs, *lands, *flat1, after)
    sems2 = [[(res[2 * (3 * t + j)], res[2 * (3 * t + j) + 1]) for j in range(3)] for t in range(n)]
    return sems2, res[6 * n:]


def _ag_finish(srcs, lands, sems1, sems2, after, name):
    n = len(srcs)
    flat1 = [s for t in range(n) for k in range(4) for s in sems1[t][k]]
    flat2 = [s for t in range(n) for j in range(3) for s in sems2[t][j]]
    n1, n2 = len(flat1), len(flat2)

    def body(*refs):
        src, land = refs[:n], refs[n:2 * n]
        s1 = refs[2 * n:2 * n + n1]
        s2 = refs[2 * n + n1:2 * n + n1 + n2]
        x, y, c, chips = _place()
        sib = (x, y, 1 - c)
        for t in range(n):
            own = land[t].at[4 * x + 2 * y + 1 - c]
            pltpu.make_async_remote_copy(
                src_ref=src[t], dst_ref=own, send_sem=s1[8 * t], recv_sem=s1[8 * t + 1],
                device_id=sib, device_id_type=MESH).wait_recv()
            for k in range(4):
                pltpu.make_async_remote_copy(
                    src_ref=src[t], dst_ref=own, send_sem=s1[2 * (4 * t + k)], recv_sem=s1[2 * (4 * t + k) + 1],
                    device_id=sib, device_id_type=MESH).wait_send()
            for j, (cx, cy) in enumerate(chips):
                blk = land[t].at[4 * cx + 2 * cy + 1 - c]
                cp = pltpu.make_async_remote_copy(
                    src_ref=blk, dst_ref=blk, send_sem=s2[2 * (3 * t + j)], recv_sem=s2[2 * (3 * t + j) + 1],
                    device_id=sib, device_id_type=MESH)
                cp.wait_send()
                cp.wait_recv()

    return pl.pallas_call(
        body, name=name,
        in_specs=[HBM_SPEC] * (2 * n) + [SEM_SPEC] * (n1 + n2) + [ANY],
        out_specs=[HBM_SPEC] * n,
        out_shape=_hbm_like(lands),
        input_output_aliases={n + i: i for i in range(n)},
        compiler_params=pltpu.CompilerParams(has_side_effects=DATAFLOW),
    )(*srcs, *lands, *flat1, *flat2, after)


def _pair_copies(srcs, dsts, sems):
    x, y, c, _ = _place()
    nt = len(srcs)
    return [pltpu.make_async_remote_copy(
        src_ref=srcs[t].at[2 * j + 1 - c], dst_ref=dsts[t].at[j],
        send_sem=sems[2 * (NCHIP * t + j)], recv_sem=sems[2 * (NCHIP * t + j) + 1],
        device_id=(x, y, 1 - c), device_id_type=MESH) for t in range(nt) for j in range(NCHIP)]


def _pair_start(grads, carry, name):
    nt = len(grads)
    ns = 2 * NCHIP * nt
    zones = [_hbm(lax.empty((NCHIP,) + a.shape[1:], a.dtype)) for a in grads]
    extra = [] if carry is None else [_hbm(carry)]
    ne = len(extra)

    def body(*refs):
        for cp in _pair_copies(refs[:nt], refs[nt:2 * nt], refs[2 * nt + ne:2 * nt + ne + ns]):
            cp.start()

    res = pl.pallas_call(
        body, name=name,
        in_specs=[HBM_SPEC] * (2 * nt + ne),
        out_specs=[SEM_SPEC] * ns + [HBM_SPEC] * (2 * nt + ne),
        out_shape=[pltpu.SemaphoreType.DMA(())] * ns + _hbm_like(grads) + _hbm_like(zones) + _hbm_like(extra),
        input_output_aliases={i: ns + i for i in range(2 * nt + ne)},
        compiler_params=pltpu.CompilerParams(has_side_effects=DATAFLOW),
    )(*[_hbm(a) for a in grads], *zones, *extra)
    handle = (list(res[:ns]), list(res[ns:ns + nt]), list(res[ns + nt:ns + 2 * nt]))
    return handle, (res[ns + 2 * nt] if ne else None)


def _pair_wait(handle, after, name):
    sems, srcs, zones = handle
    nt, ns = len(srcs), len(sems)

    def body(*refs):
        for cp in _pair_copies(refs[:nt], refs[nt:2 * nt], refs[2 * nt:2 * nt + ns]):
            cp.wait_send()
            cp.wait_recv()

    return pl.pallas_call(
        body, name=name,
        in_specs=[HBM_SPEC] * (2 * nt) + [SEM_SPEC] * ns + [ANY],
        out_specs=[HBM_SPEC] * nt,
        out_shape=_hbm_like(zones),
        input_output_aliases={nt + i: i for i in range(nt)},
        compiler_params=pltpu.CompilerParams(has_side_effects=DATAFLOW),
    )(*srcs, *zones, *sems, after)


def _rows_tile(r, row_bytes, cap_bytes):
    best = None
    for tr in range(16, r + 1, 16):
        if r % tr == 0 and tr * row_bytes <= cap_bytes:
            best = tr
    return best if best is not None else r


def _pair_sum(own, got, cidx, name):
    _, _, r, cdim = own.shape
    tr = _rows_tile(r, 2 * cdim, 2 * 1024 * 1024)

    def body(c_ref, a_ref, b_ref, o_ref):
        o_ref[...] = (a_ref[...].astype(F32) + b_ref[...].astype(F32)).astype(BF16)

    return pl.pallas_call(
        body, name=name,
        grid_spec=pltpu.PrefetchScalarGridSpec(
            num_scalar_prefetch=1, grid=(NCHIP, r // tr),
            in_specs=[pl.BlockSpec((None, None, tr, cdim), lambda j, i, c_ref: (j, c_ref[0], i, 0)),
                      pl.BlockSpec((None, tr, cdim), lambda j, i, c_ref: (j, i, 0))],
            out_specs=pl.BlockSpec((None, tr, cdim), lambda j, i, c_ref: (j, i, 0))),
        out_shape=jax.ShapeDtypeStruct((NCHIP, r, cdim), BF16),
        compiler_params=_params("arbitrary", "arbitrary"),
    )(cidx, own, got)


def _chip_copies(srcs, zones, slots, sems):
    x, y, c, chips = _place()
    out = []
    for t, (z, l) in enumerate(slots):
        for k, (cx, cy) in enumerate(chips):
            dst = zones[z].at[k] if l is None else zones[z].at[k, l]
            out.append(pltpu.make_async_remote_copy(
                src_ref=srcs[t].at[2 * cx + cy], dst_ref=dst,
                send_sem=sems[2 * (3 * t + k)], recv_sem=sems[2 * (3 * t + k) + 1],
                device_id=(cx, cy, c), device_id_type=MESH))
    return out


def _chip_start(sums, zones, slots, carry, name):
    nt, nz = len(sums), len(zones)
    ns = 6 * nt
    extra = [] if carry is None else [_hbm(carry)]
    ne = len(extra)

    def body(*refs):
        for cp in _chip_copies(refs[:nt], refs[nt:nt + nz], slots, refs[nt + nz + ne:nt + nz + ne + ns]):
            cp.start()

    res = pl.pallas_call(
        body, name=name,
        in_specs=[HBM_SPEC] * (nt + nz + ne),
        out_specs=[SEM_SPEC] * ns + [HBM_SPEC] * (nt + nz + ne),
        out_shape=[pltpu.SemaphoreType.DMA(())] * ns + _hbm_like(sums) + _hbm_like(zones) + _hbm_like(extra),
        input_output_aliases={i: ns + i for i in range(nt + nz + ne)},
        compiler_params=pltpu.CompilerParams(has_side_effects=DATAFLOW),
    )(*[_hbm(a) for a in sums], *zones, *extra)
    return (list(res[:ns]), list(res[ns:ns + nt]), list(res[ns + nt:ns + nt + nz]),
            (res[ns + nt + nz] if ne else None))


def _chip_wait(started, zones, zone_ids, after, name):
    started = [(sums, [(zone_ids.index(z), l) for z, l in slots], sems) for sums, slots, sems in started]
    nz = len(zones)
    flat_src = [a for sums, _, _ in started for a in sums]
    flat_sem = [s for _, _, sems in started for s in sems]
    n_src, n_sem = len(flat_src), len(flat_sem)

    def body(*refs):
        srcs, zs, sems = refs[:n_src], refs[n_src:n_src + nz], refs[n_src + nz:n_src + nz + n_sem]
        so, se = 0, 0
        for sums, slots, sem_list in started:
            for cp in _chip_copies(srcs[so:so + len(sums)], zs, slots, sems[se:se + len(sem_list)]):
                cp.wait_send()
                cp.wait_recv()
            so += len(sums)
            se += len(sem_list)

    return pl.pallas_call(
        body, name=name,
        in_specs=[HBM_SPEC] * (n_src + nz) + [SEM_SPEC] * n_sem + [ANY],
        out_specs=[HBM_SPEC] * nz,
        out_shape=_hbm_like(zones),
        input_output_aliases={n_src + i: i for i in range(nz)},
        compiler_params=pltpu.CompilerParams(has_side_effects=DATAFLOW),
    )(*flat_src, *zones, *flat_sem, after)


def _small_allreduce(parts, after, name):
    nt = len(parts)

    def body(*refs):
        srcs, outs, bufs = refs[:nt], refs[nt + 1:2 * nt + 1], refs[2 * nt + 1:3 * nt + 1]
        send_sems, recv_sems = refs[3 * nt + 1:]
        x, y, c, _ = _place()
        peers = [(x, y, 1 - c), (1 - x, y, c), (x, 1 - y, c)]
        for t in range(nt):
            outs[t][...] = srcs[t][...]
        for step, peer in enumerate(peers):
            copies = [pltpu.make_async_remote_copy(
                src_ref=outs[t], dst_ref=bufs[t].at[step],
                send_sem=send_sems.at[step, t], recv_sem=recv_sems.at[step, t],
                device_id=peer, device_id_type=MESH) for t in range(nt)]
            for cp in copies:
                cp.start()
            for cp in copies:
                cp.wait()
            for t in range(nt):
                outs[t][...] = outs[t][...] + bufs[t][step]

    vm = pl.BlockSpec(memory_space=pltpu.VMEM)
    return pl.pallas_call(
        body, name=name,
        in_specs=[vm] * nt + [ANY], out_specs=[vm] * nt,
        out_shape=[jax.ShapeDtypeStruct(a.shape, F32) for a in parts],
        scratch_shapes=[pltpu.VMEM((3,) + a.shape, F32) for a in parts]
        + [pltpu.SemaphoreType.DMA((3, nt)), pltpu.SemaphoreType.DMA((3, nt))],
        compiler_params=pltpu.CompilerParams(has_side_effects=True, vmem_limit_bytes=VMEM_LIMIT),
    )(*parts, after)


def _adam_math(w, g, m, v):
    m2 = ADAM_B1 * m + (1.0 - ADAM_B1) * g
    v2 = ADAM_B2 * v + (1.0 - ADAM_B2) * (g * g)
    m_hat = m2 / (1.0 - ADAM_B1 ** ADAM_STEP)
    v_hat = v2 / (1.0 - ADAM_B2 ** ADAM_STEP)
    delta = -ADAM_LR * (m_hat / (jnp.sqrt(v_hat) + ADAM_EPS) + ADAM_WD * w)
    return delta, m2, v2


def _adam_big(w, m, v, parts, mine, chip, name):
    nl, r, cdim = w.shape
    tr = _rows_tile(r, 4 * cdim, 3 * 512 * 1024)

    def body(c_ref, w_ref, m_ref, v_ref, p_ref, *rest):
        mine_refs, (g_ref, d_ref, mo_ref, vo_ref) = rest[:nl], rest[nl:]
        own = mine_refs[0][...]
        for l in range(1, nl):
            own = jnp.where(pl.program_id(0) == l, mine_refs[l][...], own)
        g = ((p_ref[0].astype(F32) + p_ref[1].astype(F32)) + p_ref[2].astype(F32)) + own.astype(F32)
        delta, m2, v2 = _adam_math(w_ref[...], g, m_ref[...], v_ref[...])
        g_ref[...] = g
        d_ref[...] = delta
        mo_ref[...] = m2
        vo_ref[...] = v2

    spec = pl.BlockSpec((None, tr, cdim), lambda l, i, c_ref: (l, i, 0))
    mine_specs = [pl.BlockSpec((None, tr, cdim), lambda l, i, c_ref, ll=ll: (c_ref[0], jnp.where(l == ll, i, 0), 0))
                  for ll in range(nl)]
    return pl.pallas_call(
        body, name=name,
        grid_spec=pltpu.PrefetchScalarGridSpec(
            num_scalar_prefetch=1, grid=(nl, r // tr),
            in_specs=[spec, spec, spec, pl.BlockSpec((3, None, tr, cdim), lambda l, i, c_ref: (0, l, i, 0))]
            + mine_specs,
            out_specs=[spec] * 4),
        out_shape=[jax.ShapeDtypeStruct(w.shape, F32)] * 4,
        compiler_params=_params("arbitrary", "arbitrary"),
    )(chip, w, m, v, parts, *mine)


def _adam_small(ws, gs, ms, vs, name):
    n = len(ws)

    def body(*refs):
        w_r, g_r, m_r, v_r = refs[:n], refs[n:2 * n], refs[2 * n:3 * n], refs[3 * n:4 * n]
        d_o, m_o, v_o = refs[4 * n:5 * n], refs[5 * n:6 * n], refs[6 * n:7 * n]
        for t in range(n):
            delta, m2, v2 = _adam_math(w_r[t][...], g_r[t][...], m_r[t][...], v_r[t][...])
            d_o[t][...] = delta
            m_o[t][...] = m2
            v_o[t][...] = v2

    vm = pl.BlockSpec(memory_space=pltpu.VMEM)
    shapes = [jax.ShapeDtypeStruct(a.shape, F32) for a in ws]
    return pl.pallas_call(
        body, name=name, in_specs=[vm] * (4 * n), out_specs=[vm] * (3 * n), out_shape=shapes * 3,
        compiler_params=pltpu.CompilerParams(vmem_limit_bytes=VMEM_LIMIT),
    )(*ws, *gs, *ms, *vs)


def kernel(x, norm_mix, norm_ffn, norm_final, ab_w_in, a_ln_g, a_ln_b, a_w_s, a_b_s, b_conv_w, b_conv_b, b_ln_g, b_ln_b, ab_w_out, c_w_in, c_conv_w, c_w_out, f_w_up, f_conv_w, f_w_down, loss_target, m_norm_mix, m_norm_ffn, m_norm_final, m_ab_w_in, m_a_ln_g, m_a_ln_b, m_a_w_s, m_a_b_s, m_b_conv_w, m_b_conv_b, m_b_ln_g, m_b_ln_b, m_ab_w_out, m_c_w_in, m_c_conv_w, m_c_w_out, m_f_w_up, m_f_conv_w, m_f_w_down, v_norm_mix, v_norm_ffn, v_norm_final, v_ab_w_in, v_a_ln_g, v_a_ln_b, v_a_w_s, v_a_b_s, v_b_conv_w, v_b_conv_b, v_b_ln_g, v_b_ln_b, v_ab_w_out, v_c_w_in, v_c_conv_w, v_c_w_out, v_f_w_up, v_f_conv_w, v_f_w_down):
    s = x.shape[1]
    x0 = x.reshape(s, D)
    tgt = loss_target.reshape(s, D)
    xi, yi, ci = lax.axis_index("x"), lax.axis_index("y"), lax.axis_index("c")
    dev = 4 * xi + 2 * yi + ci
    cidx = ci.astype(jnp.int32).reshape(1)

    bf = lambda a: a.astype(BF16)
    slab_w = 6 * CHUNK
    pad = lambda a, rows: jnp.pad(a, ((0, rows - a.shape[0]), (0, slab_w - a.shape[1])))
    slab = jnp.concatenate([pad(b_conv_w[0], 32), pad(c_conv_w[0], 8), pad(f_conv_w.reshape(6, FB), 8)], axis=0)
    later = [bf(ab_w_in[0]), bf(ab_w_out[0]), slab, bf(f_w_up[0]), bf(f_w_down[0]), bf(c_w_in[0]), bf(c_w_out[0]),
             bf(f_w_up[1]), bf(f_w_down[1])]
    lands = [_zone(a, dev) for a in later]
    groups = [[0], [1, 2], [3, 4], [5, 6], [7, 8]]
    ag_sems, later, lands, ag_token = _ag_start(later, lands, x0, "ag_start")

    causal = jnp.tril(jnp.ones((CHUNK, CHUNK), F32))
    wsm = (a_w_s[0] * causal).astype(BF16)
    bs_col = a_b_s.reshape(HEADS, CHUNK, 1)
    nm = [norm_mix[0:1], norm_mix[1:2]]
    nf = [norm_ffn[0:1], norm_ffn[1:2]]
    nfin = norm_final.reshape(1, D)

    def arrive(g, after_ici, after_d2d, tag):
        srcs = [later[t] for t in groups[g]]
        zone = [lands[t] for t in groups[g]]
        sems1 = [ag_sems[t] for t in groups[g]]
        sems2, zone = _ag_forward(srcs, zone, sems1, after_ici, "ag_forward_" + tag)
        return _ag_finish(srcs, zone, sems1, sems2, after_d2d, "ag_finish_" + tag)

    h0 = _rms_fwd(x0, nm[0], "rms_mix0", after=ag_token)
    (win0,) = arrive(0, h0, h0, "w_in")
    z = _mm_in(h0, win0, "mm_ab_in")
    wout0, slab_g = arrive(1, z, z, "first")
    wout0 = wout0.reshape(D, D)
    bcw = jnp.transpose(slab_g[:, 0:BCONV, 0:DA // NDEV], (1, 0, 2)).reshape(BCONV, DA)
    ccw = jnp.transpose(slab_g[:, 32:35, 0:D // NDEV], (1, 0, 2)).reshape(3, D)
    fcw_g = slab_g[:, 40:46, 0:FB].reshape(2, NG, 2, 3, FB)
    fcws = [fcw_g[:, :, 0], fcw_g[:, :, 1]]
    ycat, yb2 = _ab_fwd(z, a_ln_g, a_ln_b, wsm, bs_col, bcw, b_conv_b, b_ln_g, b_ln_b, "ab_fwd")
    x1, h1 = _mm_out(ycat, wout0, x0, nf[0], "mm_ab_out")
    wup0, wdn0 = arrive(2, x1, x1, "ffn0")
    up0, upc0, x2, h2 = _ffn_fwd(h1, x1, wup0.reshape(2, NG, D, FB), fcws[0], wdn0.reshape(DFF, D), nm[1],
                                 "ffn_fwd0")
    cin, cout = arrive(3, x2, x2, "c")
    cout = cout.reshape(D, D)
    zc = _mm_in(h2, cin, "mm_c_in")
    rc = _c_fwd(zc, ccw, "c_fwd")
    x3, h3 = _mm_out(rc, cout, x2, nf[1], "mm_c_out")
    wup1, wdn1 = arrive(4, rc, x3, "ffn1")
    wups = [wup0.reshape(2, NG, D, FB), wup1.reshape(2, NG, D, FB)]
    wdns = [wdn0.reshape(DFF, D), wdn1.reshape(DFF, D)]
    up1, upc1, x4 = _ffn_fwd(h3, x3, wups[1], fcws[1], wdns[1], None, "ffn_fwd1")
    dx4, dx4b, dnfin, loss_part = _final(x4, tgt, nfin, "final_loss")

    zshape = lambda *sh: _hbm(lax.empty((3,) + sh, BF16))
    zones = [zshape(D, 2 * D // NDEV), zshape(D // NDEV, D), zshape(D, 3 * D // NDEV), zshape(D // NDEV, D),
             zshape(2, FB, D), zshape(2, DFF // NDEV, D)]
    started = []

    def pair_sums(grads, handle, after, tag):
        del grads
        got = _pair_wait(handle, after, "rs_pair_wait_" + tag)
        return [_pair_sum(b.reshape((NCHIP, 2) + b.shape[1:]), g, cidx, "rs_pair_sum_%s%d" % (tag, t))
                for t, (b, g) in enumerate(zip(handle[1], got))]

    def chip_start(sums, slots, carry, tag):
        sems, sums, new_zones, carry = _chip_start(sums, zones, slots, carry, "rs_chip_start_" + tag)
        zones[:] = new_zones
        started.append((sums, slots, sems))
        return sums, carry

    rows8 = lambda g, r: g.reshape(NDEV, r, D)
    a1, dup1, dx3, dx3b, dnf1, dfcw1 = _ffn_bwd(dx4, up1, upc1, wups[1], fcws[1], wdns[1], x3, nf[1], "ffn_bwd1")
    g_f1 = [_dw_up(h3, dup1, "dw_up1"), rows8(_dw_dn(a1, dx4b, "dw_dn1"), DFF // NDEV)]
    hd_f1, dx3b = _pair_start(g_f1, dx3b, "rs_pair_start_f1")
    drc = _mm_nt(dx3b, cout, "mm_c_out_bwd")
    g_cout = rows8(_dw_rows(rc, dx3b, "dw_c_out"), D // NDEV)
    s_f1 = pair_sums(g_f1, hd_f1, g_cout, "f1")
    s_f1, drc = chip_start(s_f1, [(4, 1), (5, 1)], drc, "f1")
    dzc, dccw = _c_bwd(drc, zc, ccw, "c_bwd")
    dx2, dx2b, dnm1 = _mm_nt_rms(dzc, cin, x2, nm[1], dx3, True, "mm_c_in_bwd")
    g_c = [_dw_cols(h2, dzc, NDEV, 3 * D // NDEV, "dw_c_in"), g_cout]
    hd_c, dx2 = _pair_start(g_c, dx2, "rs_pair_start_c")
    a0, dup0, dx1, dx1b, dnf0, dfcw0 = _ffn_bwd(dx2, up0, upc0, wups[0], fcws[0], wdns[0], x1, nf[0], "ffn_bwd0")
    s_c = pair_sums(g_c, hd_c, dx1b, "c")
    s_c, dx1b = chip_start(s_c, [(2, None), (3, None)], dx1b, "c")
    g_f0 = [_dw_up(h1, dup0, "dw_up0"), rows8(_dw_dn(a0, dx2b, "dw_dn0"), DFF // NDEV)]
    hd_f0, dx1b = _pair_start(g_f0, dx1b, "rs_pair_start_f0")
    dycat = _mm_nt(dx1b, wout0, "mm_ab_out_bwd")
    g_wout0 = rows8(_dw_rows(ycat, dx1b, "dw_ab_out"), D // NDEV)
    s_f0 = pair_sums(g_f0, hd_f0, g_wout0, "f0")
    s_f0, dycat = chip_start(s_f0, [(4, 0), (5, 0)], dycat, "f0")
    dz, g512, dws, dbs = _ab_bwd(dycat, z, yb2, a_ln_g, a_ln_b, wsm, bs_col, bcw, b_ln_g, b_ln_b, "ab_bwd")
    grad_x, dnm0 = _mm_nt_rms(dz, win0, x0, nm[0], dx1, False, "mm_ab_in_bwd")
    g_ab = [_dw_cols(h0, dz, NDEV, 2 * D // NDEV, "dw_ab_in"), g_wout0]
    hd_ab, _ = _pair_start(g_ab, None, "rs_pair_start_ab")

    g1024 = jnp.concatenate([dnm0, dnm1, dnf0, dnf1, dnfin, dccw], axis=0)
    gfc = jnp.concatenate([dfcw0, dfcw1], axis=0).reshape(2 * NG * 2 * 3, FB)
    g1024, g512, dws, dbs, gfc, loss_sum = _small_allreduce(
        [g1024, g512, dws.reshape(HEADS * CHUNK, CHUNK), dbs.reshape(HEADS, CHUNK), gfc, loss_part], hd_ab[1][0],
        "small_allreduce")
    loss = loss_sum[0, 0]
    s_ab = pair_sums(g_ab, hd_ab, g1024, "ab")
    s_ab, _ = chip_start(s_ab, [(0, None), (1, None)], None, "ab")
    p_cin, p_cout, p_wup, p_wdn = _chip_wait(started[:3], zones[2:], [2, 3, 4, 5], s_ab[0], "rs_chip_wait_early")

    chip = (2 * xi + yi).astype(jnp.int32).reshape(1)

    def big_update(w, m, v, parts, mine, name):
        shp = w.shape
        w3, m3, v3 = (a.reshape((-1,) + shp[-2:]) for a in (w, m, v))
        p4 = parts.reshape((3,) + w3.shape)
        return [o.reshape(shp) for o in _adam_big(w3, m3, v3, p4, mine, chip, name)]

    u_cin = big_update(c_w_in, m_c_w_in, v_c_w_in, p_cin, [s_c[0]], "adam_c_w_in")
    u_cout = big_update(c_w_out, m_c_w_out, v_c_w_out, p_cout, [s_c[1]], "adam_c_w_out")
    tr_ = lambda a: jnp.swapaxes(a, 1, 2)
    u_wup = [tr_(o) for o in big_update(tr_(f_w_up), tr_(m_f_w_up), tr_(v_f_w_up), p_wup,
                                        [s_f0[0], s_f1[0]], "adam_f_w_up")]
    u_wdn = big_update(f_w_down, m_f_w_down, v_f_w_down, p_wdn, [s_f0[1], s_f1[1]], "adam_f_w_down")
    p_win0, p_wout0 = _chip_wait(started[3:], zones[:2], [0, 1], u_wdn[0], "rs_chip_wait_late")
    u_win0 = big_update(ab_w_in, m_ab_w_in, v_ab_w_in, p_win0, [s_ab[0]], "adam_ab_w_in")
    u_wout0 = big_update(ab_w_out, m_ab_w_out, v_ab_w_out, p_wout0, [s_ab[1]], "adam_ab_w_out")

    g_norm_mix = g1024[0:2]
    g_norm_ffn = g1024[2:4]
    g_norm_final = g1024[4:5]
    g_ccw = lax.dynamic_slice(g1024[5:8], (0, dev * (D // NDEV)), (3, D // NDEV))
    g_bcw = lax.dynamic_slice(g512[8:8 + BCONV], (0, dev * (DA // NDEV)), (BCONV, DA // NDEV))
    gfc = gfc.reshape(2, NG, 2, 3, FB)
    g_fcw = lax.dynamic_slice(gfc, (0, dev % NG, dev // NG, 0, 0), (2, 1, 1, 3, FB)).reshape(2, 3, FB)
    small_w = [norm_mix, norm_ffn, nfin, a_ln_g, a_ln_b, a_w_s[0], a_b_s[0], b_conv_w[0], b_conv_b,
               b_ln_g, b_ln_b, c_conv_w[0], f_conv_w]
    small_g = [g_norm_mix, g_norm_ffn, g_norm_final, g512[0:1], g512[1:2],
               dws.reshape(HEADS, CHUNK, CHUNK), dbs, g_bcw, g512[2:3],
               g512[3:4], g512[4:5], g_ccw, g_fcw]
    small_m = [m_norm_mix, m_norm_ffn, m_norm_final.reshape(1, D), m_a_ln_g, m_a_ln_b, m_a_w_s[0], m_a_b_s[0],
               m_b_conv_w[0], m_b_conv_b, m_b_ln_g, m_b_ln_b, m_c_conv_w[0], m_f_conv_w]
    small_v = [v_norm_mix, v_norm_ffn, v_norm_final.reshape(1, D), v_a_ln_g, v_a_ln_b, v_a_w_s[0], v_a_b_s[0],
               v_b_conv_w[0], v_b_conv_b, v_b_ln_g, v_b_ln_b, v_c_conv_w[0], v_f_conv_w]
    upd = _adam_small(small_w, small_g, small_m, small_v, "adam_small")
    ns = len(small_w)
    orig = [norm_mix, norm_ffn, norm_final, a_ln_g, a_ln_b, a_w_s, a_b_s, b_conv_w, b_conv_b,
            b_ln_g, b_ln_b, c_conv_w, f_conv_w]
    sg_out = [g.reshape(o.shape) for g, o in zip(small_g, orig)]
    sd_out = [a.reshape(o.shape) for a, o in zip(upd[0:ns], orig)]
    sm_out = [a.reshape(o.shape) for a, o in zip(upd[ns:2 * ns], orig)]
    sv_out = [a.reshape(o.shape) for a, o in zip(upd[2 * ns:3 * ns], orig)]

    def assemble(small, k):
        return [small[0], small[1], small[2], u_win0[k], small[3], small[4], small[5], small[6], small[7],
                small[8], small[9], small[10], u_wout0[k], u_cin[k], small[11], u_cout[k], u_wup[k],
                small[12], u_wdn[k]]

    grads = assemble(sg_out, 0)
    deltas = assemble(sd_out, 1)
    new_m = assemble(sm_out, 2)
    new_v = assemble(sv_out, 3)
    return (loss, grad_x.reshape(1, s, D), *grads, *deltas, *new_m, *new_v)
```

```python
import functools
import math

import jax
import jax.numpy as jnp
from jax import lax
from jax.experimental import pallas as pl
from jax.experimental.pallas import tpu as pltpu

F32 = jnp.float32
BF16 = jnp.bfloat16

D = 1024
DA = 512
HEADS = 4
CHUNK = 128
DFF = 2816
NDEV = 8
NCHIP = 4
FB = DFF * 2 // NDEV
NG = DFF // FB
BCONV = 31
EPS = 1e-6
HALO = 16
HALO_B = 32
RC = 32
NPART = 2
VMEM_LIMIT = 52 * 1024 * 1024
INV_SQRT2 = 1.0 / math.sqrt(2.0)
INV_SQRT_2PI = 1.0 / math.sqrt(2.0 * math.pi)

ADAM_LR = 0.001
ADAM_B1 = 0.9
ADAM_B2 = 0.999
ADAM_EPS = 1e-08
ADAM_WD = 0.01
ADAM_STEP = 10

MESH = pl.DeviceIdType.MESH
ANY = pl.BlockSpec(memory_space=pl.ANY)
NT_DIMS = (((1,), (1,)), ((), ()))
TN_DIMS = (((0,), (0,)), ((), ()))


def _params(*sem):
    return pltpu.CompilerParams(dimension_semantics=sem, vmem_limit_bytes=VMEM_LIMIT)


def _tile(s, want):
    return min(want, s)


def _sigmoid(x):
    return jax.nn.sigmoid(x)


def _dsilu(x, sg):
    return sg * (1.0 + x * (1.0 - sg))


def _gelu(x):
    return 0.5 * x * (1.0 + lax.erf(x * INV_SQRT2))


def _dgelu(x):
    return 0.5 * (1.0 + lax.erf(x * INV_SQRT2)) + x * jnp.exp(-0.5 * x * x) * INV_SQRT_2PI


def _ln_fwd(x, g, b):
    mu = jnp.mean(x, axis=-1, keepdims=True)
    xc = x - mu
    var = jnp.mean(xc * xc, axis=-1, keepdims=True)
    rstd = lax.rsqrt(var + EPS)
    xhat = xc * rstd
    return xhat * g + b, xhat, rstd


def _ln_bwd(dy, xhat, rstd, g):
    dxh = dy * g
    m1 = jnp.mean(dxh, axis=-1, keepdims=True)
    m2 = jnp.mean(dxh * xhat, axis=-1, keepdims=True)
    return rstd * (dxh - m1 - xhat * m2)


def _rms_bwd_math(dh, x, g):
    r = lax.rsqrt(jnp.mean(x * x, axis=-1, keepdims=True) + EPS)
    xhat = x * r
    dg = jnp.sum(dh * xhat, axis=0, keepdims=True)
    u = dh * g
    dx = r * (u - xhat * jnp.mean(u * xhat, axis=-1, keepdims=True))
    return dx, dg


def _conv3(xe, cw, halo):
    x0 = xe[halo:]
    x1 = pltpu.roll(xe, 1, 0)[halo:]
    x2 = pltpu.roll(xe, 2, 0)[halo:]
    return cw[2] * x0 + cw[1] * x1 + cw[0] * x2, (x0, x1, x2)


def _conv3_bwd_in(dce, cw, ts):
    n = dce.shape[0]
    d1 = pltpu.roll(dce, n - 1, 0)[:ts]
    d2 = pltpu.roll(dce, n - 2, 0)[:ts]
    return cw[2] * dce[:ts] + cw[1] * d1 + cw[0] * d2


def _conv3_bwd_w(dc, taps):
    x0, x1, x2 = taps
    return [jnp.sum(dc * x2, axis=0, keepdims=True), jnp.sum(dc * x1, axis=0, keepdims=True),
            jnp.sum(dc * x0, axis=0, keepdims=True)]


def _rms_fwd(x, g, name, after=None):
    s = x.shape[0]
    ts = _tile(s, 512)

    def body(x_ref, g_ref, *rest):
        h_ref = rest[-1]
        xv = x_ref[...]
        r = lax.rsqrt(jnp.mean(xv * xv, axis=-1, keepdims=True) + EPS)
        h_ref[...] = (xv * r * g_ref[...]).astype(BF16)

    extra = [] if after is None else [after]
    return pl.pallas_call(
        body, grid=(s // ts,), name=name,
        in_specs=[pl.BlockSpec((ts, D), lambda i: (i, 0)), pl.BlockSpec((1, D), lambda i: (0, 0))]
        + [ANY] * len(extra),
        out_specs=pl.BlockSpec((ts, D), lambda i: (i, 0)),
        out_shape=jax.ShapeDtypeStruct((s, D), BF16),
        compiler_params=_params("parallel"),
    )(x, g, *extra)


MXU_COLS = 256


def _pair(bn):
    return 1 if bn % MXU_COLS == 0 else 2


def _cols(w_ref, b, pair):
    return w_ref[b] if pair == 1 else jnp.concatenate([w_ref[b + q] for q in range(pair)], axis=1)


def _mm_in(h, wblk, name):
    s = h.shape[0]
    nb, _, bn = wblk.shape
    pair = _pair(bn)
    ts = _tile(s, 1024)

    def body(h_ref, w_ref, o_ref):
        hv = h_ref[...]
        for b in range(0, nb, pair):
            o_ref[:, b * bn:(b + pair) * bn] = jnp.dot(hv, _cols(w_ref, b, pair),
                                                       preferred_element_type=F32).astype(BF16)

    return pl.pallas_call(
        body, grid=(s // ts,), name=name,
        in_specs=[pl.BlockSpec((ts, D), lambda i: (i, 0)), pl.BlockSpec((nb, D, bn), lambda i: (0, 0, 0))],
        out_specs=pl.BlockSpec((ts, nb * bn), lambda i: (i, 0)),
        out_shape=jax.ShapeDtypeStruct((s, nb * bn), BF16),
        compiler_params=_params("parallel"),
    )(h, wblk)


def _rms_math(xv, g):
    r = lax.rsqrt(jnp.mean(xv * xv, axis=-1, keepdims=True) + EPS)
    return (xv * r * g).astype(BF16)


def _mm_out(y, w, xres, gnext, name):
    s = y.shape[0]
    ts = _tile(s, 1024)

    def body(y_ref, w_ref, x_ref, g_ref, o_ref, h_ref):
        xn = x_ref[...] + jnp.dot(y_ref[...], w_ref[...], preferred_element_type=F32)
        o_ref[...] = xn
        h_ref[...] = _rms_math(xn, g_ref[...])

    return pl.pallas_call(
        body, grid=(s // ts,), name=name,
        in_specs=[pl.BlockSpec((ts, D), lambda i: (i, 0)), pl.BlockSpec((D, D), lambda i: (0, 0)),
                  pl.BlockSpec((ts, D), lambda i: (i, 0)), pl.BlockSpec((1, D), lambda i: (0, 0))],
        out_specs=[pl.BlockSpec((ts, D), lambda i: (i, 0)), pl.BlockSpec((ts, D), lambda i: (i, 0))],
        out_shape=[jax.ShapeDtypeStruct((s, D), F32), jax.ShapeDtypeStruct((s, D), BF16)],
        compiler_params=_params("parallel"),
    )(y, w, xres, gnext)


CONV_ROWS = 32


def _rolled_copies(dst_ref, xe, back):
    n = xe.shape[0]
    dst_ref[0] = xe
    for r in range(1, 8):
        dst_ref[r] = pltpu.roll(xe, n - r if back else r, 0)


def _conv31(rolled_ref, cw_ref, ts, out_ref, bias):
    for o in range(0, ts, CONV_ROWS):
        acc = jnp.zeros((CONV_ROWS, DA), F32) + bias
        for sh in range(BCONV):
            q, r = divmod(sh, 8)
            lo = HALO_B - 8 * q + o
            acc = acc + cw_ref[BCONV - 1 - sh:BCONV - sh, :] * rolled_ref[r, lo:lo + CONV_ROWS, :]
        out_ref[o:o + CONV_ROWS, :] = acc


def _ab_fwd(z, lga, lba, wsm, bs_col, cwb, cbb, lgb, lbb, name):
    s = z.shape[0]
    ts = _tile(s, 256)
    hb = ts // HALO_B

    def body(z_ref, zh_ref, lga_ref, lba_ref, ws_ref, bs_ref, cw_ref, cb_ref, lgb_ref, lbb_ref,
             y_ref, yb2_ref, rolled):
        i = pl.program_id(0)
        z_t = z_ref[...].astype(F32)
        gu = _gelu(z_t[:, 0:DA])
        gv = _gelu(z_t[:, DA:2 * DA])
        vn, _, _ = _ln_fwd(gv, lga_ref[...], lba_ref[...])
        vnb = vn.astype(BF16)
        for c in range(ts // CHUNK):
            for h in range(HEADS):
                rs = slice(c * CHUNK, (c + 1) * CHUNK)
                cs = slice(h * CHUNK, (h + 1) * CHUNK)
                mixed = jnp.dot(ws_ref[h], vnb[rs, cs], preferred_element_type=F32) + bs_ref[h]
                y_ref[rs, cs] = (gu[rs, cs] * mixed).astype(BF16)
        zh = jnp.where(i > 0, zh_ref[...], jnp.zeros_like(zh_ref[...])).astype(F32)
        xb = jnp.concatenate([zh[:, 0:DA], z_t[:, 2 * DA:3 * DA]], axis=0)
        gb = jnp.concatenate([zh[:, DA:2 * DA], z_t[:, 3 * DA:4 * DA]], axis=0)
        _rolled_copies(rolled, xb * _sigmoid(gb), False)
        _conv31(rolled, cw_ref, ts, yb2_ref, cb_ref[...])
        nb_, _, _ = _ln_fwd(yb2_ref[...], lgb_ref[...], lbb_ref[...])
        y_ref[:, DA:2 * DA] = (nb_ * _sigmoid(nb_)).astype(BF16)

    row = lambda i: (0, 0)
    return pl.pallas_call(
        body, grid=(s // ts,), name=name,
        in_specs=[pl.BlockSpec((ts, 4 * DA), lambda i: (i, 0)),
                  pl.BlockSpec((HALO_B, 2 * DA), lambda i: (jnp.maximum(i * hb - 1, 0), 1)),
                  pl.BlockSpec((1, DA), row), pl.BlockSpec((1, DA), row),
                  pl.BlockSpec((HEADS, CHUNK, CHUNK), lambda i: (0, 0, 0)),
                  pl.BlockSpec((HEADS, CHUNK, 1), lambda i: (0, 0, 0)),
                  pl.BlockSpec((BCONV, DA), row), pl.BlockSpec((1, DA), row),
                  pl.BlockSpec((1, DA), row), pl.BlockSpec((1, DA), row)],
        out_specs=[pl.BlockSpec((ts, 2 * DA), lambda i: (i, 0)), pl.BlockSpec((ts, DA), lambda i: (i, 0))],
        out_shape=[jax.ShapeDtypeStruct((s, 2 * DA), BF16), jax.ShapeDtypeStruct((s, DA), F32)],
        scratch_shapes=[pltpu.VMEM((8, ts + HALO_B, DA), F32)],
        compiler_params=_params("parallel"),
    )(z, z, lga, lba, wsm, bs_col, cwb, cbb, lgb, lbb)


def _c_fwd(zc, cw, name):
    s = zc.shape[0]
    ts = _tile(s, 512)
    hb = ts // HALO

    def body(z_ref, ch_ref, xh_ref, cw_ref, r_ref):
        i = pl.program_id(0)
        z_t = z_ref[...].astype(F32)
        ph = jnp.where(i > 0, ch_ref[...].astype(F32) * xh_ref[...].astype(F32), 0.0)
        pe = jnp.concatenate([ph, z_t[:, D:2 * D] * z_t[:, 2 * D:3 * D]], axis=0)
        q, _ = _conv3(pe, [cw_ref[k:k + 1, :] for k in range(3)], HALO)
        r_ref[...] = (z_t[:, 0:D] * q).astype(BF16)

    halo = lambda col: pl.BlockSpec((HALO, D), lambda i: (jnp.maximum(i * hb - 1, 0), col))
    return pl.pallas_call(
        body, grid=(s // ts,), name=name,
        in_specs=[pl.BlockSpec((ts, 3 * D), lambda i: (i, 0)), halo(1), halo(2),
                  pl.BlockSpec((3, D), lambda i: (0, 0))],
        out_specs=pl.BlockSpec((ts, D), lambda i: (i, 0)),
        out_shape=jax.ShapeDtypeStruct((s, D), BF16),
        compiler_params=_params("parallel"),
    )(zc, zc, zc, cw)


def _ffn_fwd(h, xres, wup, fcw, wdn, gnext, name):
    s = h.shape[0]
    ts = _tile(s, 512)
    hb = ts // HALO

    def body(h_ref, hh_ref, w_ref, cw_ref, wd_ref, x_ref, *rest):
        if gnext is not None:
            gn_ref, up_ref, upc_ref, xo_ref, hn_ref, up_s = rest
        else:
            up_ref, upc_ref, xo_ref, up_s = rest
        i = pl.program_id(0)
        m = pl.program_id(1)
        @pl.when(m == 0)
        def _():
            xo_ref[...] = x_ref[...]

        halo = jnp.where(i > 0, hh_ref[...], jnp.zeros_like(hh_ref[...]))
        hx = jnp.concatenate([halo, h_ref[...]], axis=0)
        acts = []
        for gv in range(2):
            up_s[gv] = jnp.dot(hx, w_ref[gv], preferred_element_type=F32)
            x0 = up_s[gv, HALO:HALO + ts, :]
            up_ref[gv] = x0.astype(BF16)
            upc = (cw_ref[gv, 2:3, :] * x0 + cw_ref[gv, 1:2, :] * up_s[gv, HALO - 1:HALO - 1 + ts, :]
                   + cw_ref[gv, 0:1, :] * up_s[gv, HALO - 2:HALO - 2 + ts, :])
            upc_ref[gv] = upc.astype(BF16)
            acts.append(upc)
        a = acts[0] * _sigmoid(acts[0]) * acts[1]
        xo_ref[...] += jnp.dot(a.astype(BF16), wd_ref[...], preferred_element_type=F32)

        if gnext is not None:
            @pl.when(m == NG - 1)
            def _():
                hn_ref[...] = _rms_math(xo_ref[...], gn_ref[...])

    tile = pl.BlockSpec((ts, D), lambda i, m: (i, 0))
    nxt = gnext is not None
    return pl.pallas_call(
        body, grid=(s // ts, NG), name=name,
        in_specs=[tile,
                  pl.BlockSpec((HALO, D), lambda i, m: (jnp.maximum(i * hb - 1, 0), 0)),
                  pl.BlockSpec((2, None, D, FB), lambda i, m: (0, m, 0, 0)),
                  pl.BlockSpec((2, None, 3, FB), lambda i, m: (0, m, 0, 0)),
                  pl.BlockSpec((FB, D), lambda i, m: (m, 0)),
                  tile] + ([pl.BlockSpec((1, D), lambda i, m: (0, 0))] if nxt else []),
        out_specs=[pl.BlockSpec((None, 2, ts, FB), lambda i, m: (m, 0, i, 0)),
                   pl.BlockSpec((None, 2, ts, FB), lambda i, m: (m, 0, i, 0)),
                   tile] + ([tile] if nxt else []),
        out_shape=[jax.ShapeDtypeStruct((NG, 2, s, FB), BF16), jax.ShapeDtypeStruct((NG, 2, s, FB), BF16),
                   jax.ShapeDtypeStruct((s, D), F32)] + ([jax.ShapeDtypeStruct((s, D), BF16)] if nxt else []),
        scratch_shapes=[pltpu.VMEM((2, ts + HALO, FB), F32)],
        compiler_params=_params("arbitrary", "arbitrary"),
    )(h, h, wup, fcw, wdn, xres, *([gnext] if nxt else []))


def _final(x, tgt, g, name):
    s = x.shape[0]
    ts = _tile(s, 512)

    def body(x_ref, t_ref, g_ref, dx_ref, dxb_ref, dg_ref, loss_ref):
        i = pl.program_id(0)
        xv = x_ref[...]
        gv = g_ref[...]
        r = lax.rsqrt(jnp.mean(xv * xv, axis=-1, keepdims=True) + EPS)
        xhat = xv * r
        e = xhat * gv - t_ref[...]
        part = 0.5 * jnp.sum(jnp.mean(e * e, axis=-1, keepdims=True), axis=0, keepdims=True)
        dy = e * (1.0 / D)
        dgp = jnp.sum(dy * xhat, axis=0, keepdims=True)
        u = dy * gv
        dx = r * (u - xhat * jnp.mean(u * xhat, axis=-1, keepdims=True))
        dx_ref[...] = dx
        dxb_ref[...] = dx.astype(BF16)

        @pl.when(i == 0)
        def _():
            dg_ref[...] = dgp
            loss_ref[...] = jnp.broadcast_to(part, (1, 128))

        @pl.when(i > 0)
        def _():
            dg_ref[...] += dgp
            loss_ref[...] += jnp.broadcast_to(part, (1, 128))

    return pl.pallas_call(
        body, grid=(s // ts,), name=name,
        in_specs=[pl.BlockSpec((ts, D), lambda i: (i, 0)), pl.BlockSpec((ts, D), lambda i: (i, 0)),
                  pl.BlockSpec((1, D), lambda i: (0, 0))],
        out_specs=[pl.BlockSpec((ts, D), lambda i: (i, 0)), pl.BlockSpec((ts, D), lambda i: (i, 0)),
                   pl.BlockSpec((1, D), lambda i: (0, 0)), pl.BlockSpec((1, 128), lambda i: (0, 0))],
        out_shape=[jax.ShapeDtypeStruct((s, D), F32), jax.ShapeDtypeStruct((s, D), BF16),
                   jax.ShapeDtypeStruct((1, D), F32), jax.ShapeDtypeStruct((1, 128), F32)],
        compiler_params=_params("arbitrary"),
    )(x, tgt, g)


def _ffn_bwd(df, up, upc, wup, fcw, wdn, xin, g, name):
    s = df.shape[0]
    ts = _tile(s, 512)
    nt = s // ts

    def body(df_ref, up_ref, upc_ref, w_ref, cw_ref, wd_ref, x_ref, g_ref,
             a_ref, dup_ref, dx_ref, dxb_ref, dg_ref, dcw_ref, carry, acc, tacc, dcs_ref):
        i = pl.program_id(0)
        m = pl.program_id(1)
        first = i == 0
        @pl.when(first)
        def _():
            carry[m] = jnp.zeros((2, 8, FB), F32)
            dcw_ref[m] = jnp.zeros((2, 3, FB), F32)

        @pl.when(m == 0)
        def _():
            acc[...] = jnp.zeros((ts, D), F32)

        cws = [[cw_ref[gv, k:k + 1, :] for k in range(3)] for gv in range(2)]
        part = ts // NPART
        das = [lax.dot_general(df_ref[p * part:(p + 1) * part, :].astype(BF16), wd_ref[...], NT_DIMS,
                               preferred_element_type=F32) for p in range(NPART)]

        tacc[...] = jnp.zeros((2, 3, 8, FB), F32)
        dcs_ref[:, ts:ts + 8, :] = carry[m]
        for r in reversed(range(ts // RC)):
            rs = slice(r * RC, (r + 1) * RC)
            gate = upc_ref[0, rs, :].astype(F32)
            val = upc_ref[1, rs, :].astype(F32)
            sg = _sigmoid(gate)
            sl = gate * sg
            a_ref[rs, :] = (sl * val).astype(BF16)
            da_c = das[(r * RC) // part][(r * RC) % part:(r * RC) % part + RC]
            dcs = [da_c * val * _dsilu(gate, sg), da_c * sl]
            for gv in range(2):
                dc = dcs[gv]
                dcs_ref[gv, rs, :] = dc
                d1 = dcs_ref[gv, r * RC + 1:(r + 1) * RC + 1, :]
                d2 = dcs_ref[gv, r * RC + 2:(r + 1) * RC + 2, :]
                du = cws[gv][2] * dc + cws[gv][1] * d1 + cws[gv][0] * d2
                dup_ref[gv, rs, :] = du.astype(BF16)
                x0 = up_ref[gv, rs, :].astype(F32)
                for k, dk in enumerate((d2, d1, dc)):
                    p = x0 * dk
                    tacc[gv, k] += sum(p[j:j + 8] for j in range(0, RC, 8))
            if (r * RC) % part == 0:
                ps = slice(r * RC, r * RC + part)
                acc[ps, :] += (
                    lax.dot_general(dup_ref[0, ps, :], w_ref[0], NT_DIMS, preferred_element_type=F32)
                    + lax.dot_general(dup_ref[1, ps, :], w_ref[1], NT_DIMS, preferred_element_type=F32))
        for gv in range(2):
            carry[m, gv] = dcs_ref[gv, 0:8, :]
            for k in range(3):
                dcw_ref[m, gv, k:k + 1, :] += jnp.sum(tacc[gv, k], axis=0, keepdims=True)

        @pl.when(m == NG - 1)
        def _():
            dx, dgp = _rms_bwd_math(acc[...], x_ref[...], g_ref[...])
            dx = df_ref[...] + dx
            dx_ref[...] = dx
            dxb_ref[...] = dx.astype(BF16)

            @pl.when(first)
            def _():
                dg_ref[...] = dgp

            @pl.when(jnp.logical_not(first))
            def _():
                dg_ref[...] += dgp

    rev = lambda i: nt - 1 - i
    return pl.pallas_call(
        body, grid=(nt, NG), name=name,
        in_specs=[pl.BlockSpec((ts, D), lambda i, m: (rev(i), 0)),
                  pl.BlockSpec((None, 2, ts, FB), lambda i, m: (m, 0, rev(i), 0)),
                  pl.BlockSpec((None, 2, ts, FB), lambda i, m: (m, 0, rev(i), 0)),
                  pl.BlockSpec((2, None, D, FB), lambda i, m: (0, m, 0, 0)),
                  pl.BlockSpec((2, None, 3, FB), lambda i, m: (0, m, 0, 0)),
                  pl.BlockSpec((FB, D), lambda i, m: (m, 0)),
                  pl.BlockSpec((ts, D), lambda i, m: (rev(i), 0)),
                  pl.BlockSpec((1, D), lambda i, m: (0, 0))],
        out_specs=[pl.BlockSpec((None, ts, FB), lambda i, m: (m, rev(i), 0)),
                   pl.BlockSpec((None, 2, ts, FB), lambda i, m: (m, 0, rev(i), 0)),
                   pl.BlockSpec((ts, D), lambda i, m: (rev(i), 0)),
                   pl.BlockSpec((ts, D), lambda i, m: (rev(i), 0)),
                   pl.BlockSpec((1, D), lambda i, m: (0, 0)),
                   pl.BlockSpec((NG, 2, 3, FB), lambda i, m: (0, 0, 0, 0))],
        out_shape=[jax.ShapeDtypeStruct((NG, s, FB), BF16), jax.ShapeDtypeStruct((NG, 2, s, FB), BF16),
                   jax.ShapeDtypeStruct((s, D), F32), jax.ShapeDtypeStruct((s, D), BF16),
                   jax.ShapeDtypeStruct((1, D), F32),
                   jax.ShapeDtypeStruct((NG, 2, 3, FB), F32)],
        scratch_shapes=[pltpu.VMEM((NG, 2, 8, FB), F32), pltpu.VMEM((ts, D), F32),
                        pltpu.VMEM((2, 3, 8, FB), F32), pltpu.VMEM((2, ts + 8, FB), F32)],
        compiler_params=_params("arbitrary", "arbitrary"),
    )(df, up, upc, wup, fcw, wdn, xin, g)


def _mm_nt(dy, w, name):
    s = dy.shape[0]
    ts = _tile(s, 1024)

    def body(dy_ref, w_ref, o_ref):
        o_ref[...] = lax.dot_general(dy_ref[...], w_ref[...], NT_DIMS,
                                     preferred_element_type=F32).astype(BF16)

    return pl.pallas_call(
        body, grid=(s // ts,), name=name,
        in_specs=[pl.BlockSpec((ts, D), lambda i: (i, 0)), pl.BlockSpec((D, D), lambda i: (0, 0))],
        out_specs=pl.BlockSpec((ts, D), lambda i: (i, 0)),
        out_shape=jax.ShapeDtypeStruct((s, D), BF16),
        compiler_params=_params("parallel"),
    )(dy, w)


def _mm_nt_rms(dy, wblk, x, g, dres, bf16_copy, name):
    s = dy.shape[0]
    nb, _, bn = wblk.shape
    pair = _pair(bn)
    ts = _tile(s, 512)

    def body(dy_ref, w_ref, x_ref, g_ref, dr_ref, dx_ref, *rest):
        dg_ref = rest[-1]
        i = pl.program_id(0)
        acc = jnp.zeros((ts, D), F32)
        for b in range(0, nb, pair):
            acc = acc + lax.dot_general(dy_ref[:, b * bn:(b + pair) * bn], _cols(w_ref, b, pair), NT_DIMS,
                                        preferred_element_type=F32)
        dx, dgp = _rms_bwd_math(acc, x_ref[...], g_ref[...])
        dx = dr_ref[...] + dx
        dx_ref[...] = dx
        if bf16_copy:
            rest[0][...] = dx.astype(BF16)

        @pl.when(i == 0)
        def _():
            dg_ref[...] = dgp

        @pl.when(i > 0)
        def _():
            dg_ref[...] += dgp

    tile = pl.BlockSpec((ts, D), lambda i: (i, 0))
    return pl.pallas_call(
        body, grid=(s // ts,), name=name,
        in_specs=[pl.BlockSpec((ts, nb * bn), lambda i: (i, 0)), pl.BlockSpec((nb, D, bn), lambda i: (0, 0, 0)),
                  tile, pl.BlockSpec((1, D), lambda i: (0, 0)), tile],
        out_specs=[tile] + ([tile] if bf16_copy else []) + [pl.BlockSpec((1, D), lambda i: (0, 0))],
        out_shape=[jax.ShapeDtypeStruct((s, D), F32)] + ([jax.ShapeDtypeStruct((s, D), BF16)] if bf16_copy else [])
        + [jax.ShapeDtypeStruct((1, D), F32)],
        compiler_params=_params("arbitrary"),
    )(dy, wblk, x, g, dres)


def _c_bwd(dr, zc, cw, name):
    s = dr.shape[0]
    ts = _tile(s, 512)
    nt = s // ts
    hb = ts // HALO

    def body(dr_ref, drf_ref, z_ref, ch_ref, xh_ref, bf_ref, cw_ref, dz_ref, dcw_ref):
        i = pl.program_id(0)
        cwv = [cw_ref[k:k + 1, :] for k in range(3)]
        z_t = z_ref[...].astype(F32)
        bg, cg, xv = z_t[:, 0:D], z_t[:, D:2 * D], z_t[:, 2 * D:3 * D]
        ph = jnp.where(i > 0, ch_ref[...].astype(F32) * xh_ref[...].astype(F32), 0.0)
        pe = jnp.concatenate([ph, cg * xv], axis=0)
        q, taps = _conv3(pe, cwv, HALO)
        drv = dr_ref[...].astype(F32)
        dq = drv * bg
        dqf = jnp.where(i < nt - 1, drf_ref[...].astype(F32) * bf_ref[...].astype(F32), 0.0)
        dp = _conv3_bwd_in(jnp.concatenate([dq, dqf], axis=0), cwv, ts)
        dz_ref[:, 0:D] = (drv * q).astype(BF16)
        dz_ref[:, D:2 * D] = (dp * xv).astype(BF16)
        dz_ref[:, 2 * D:3 * D] = (dp * cg).astype(BF16)
        rows = _conv3_bwd_w(dq, taps)

        @pl.when(i == 0)
        def _():
            for k in range(3):
                dcw_ref[k:k + 1, :] = rows[k]

        @pl.when(i > 0)
        def _():
            for k in range(3):
                dcw_ref[k:k + 1, :] += rows[k]

    past = lambda col: pl.BlockSpec((HALO, D), lambda i: (jnp.maximum(i * hb - 1, 0), col))
    nxt = lambda i: jnp.minimum((i + 1) * hb, s // HALO - 1)
    return pl.pallas_call(
        body, grid=(nt,), name=name,
        in_specs=[pl.BlockSpec((ts, D), lambda i: (i, 0)),
                  pl.BlockSpec((HALO, D), lambda i: (nxt(i), 0)),
                  pl.BlockSpec((ts, 3 * D), lambda i: (i, 0)), past(1), past(2),
                  pl.BlockSpec((HALO, D), lambda i: (nxt(i), 0)),
                  pl.BlockSpec((3, D), lambda i: (0, 0))],
        out_specs=[pl.BlockSpec((ts, 3 * D), lambda i: (i, 0)), pl.BlockSpec((3, D), lambda i: (0, 0))],
        out_shape=[jax.ShapeDtypeStruct((s, 3 * D), BF16), jax.ShapeDtypeStruct((3, D), F32)],
        compiler_params=_params("arbitrary"),
    )(dr, dr, zc, zc, zc, zc, cw)


G512_ROWS = 40


def _ab_bwd(dy, z, yb2, lga, lba, wsm, bs_col, cwb, lgb, lbb, name):
    s = z.shape[0]
    ts = _tile(s, 256)
    nt = s // ts
    hb = ts // HALO_B
    nch = ts // CHUNK
    tri = None

    def body(z_ref, zh_ref, dy_ref, dyf_ref, yb2_ref, yb2f_ref, lga_ref, lba_ref, ws_ref, bs_ref,
             cw_ref, lgb_ref, lbb_ref, dz_ref, g512_ref, dws_ref, dbs_ref, dvn_ref, fwd_rolled, bwd_rolled, du_s):
        i = pl.program_id(0)
        last = i == nt - 1

        @pl.when(i == 0)
        def _():
            g512_ref[...] = jnp.zeros((G512_ROWS, DA), F32)
            dws_ref[...] = jnp.zeros((HEADS, CHUNK, CHUNK), F32)
            dbs_ref[...] = jnp.zeros((HEADS, CHUNK, 1), F32)

        def add_row(k, v):
            g512_ref[k:k + 1, :] += v

        z_t = z_ref[...].astype(F32)
        dy_t = dy_ref[...].astype(F32)
        ua, va = z_t[:, 0:DA], z_t[:, DA:2 * DA]
        gu = _gelu(ua)
        gv = _gelu(va)
        lga_v = lga_ref[...]
        vn, xhat_a, rstd_a = _ln_fwd(gv, lga_v, lba_ref[...])
        vnb = vn.astype(BF16)
        causal = (lax.broadcasted_iota(jnp.int32, (CHUNK, CHUNK), 0)
                  >= lax.broadcasted_iota(jnp.int32, (CHUNK, CHUNK), 1)).astype(F32)
        for c in range(nch):
            for h in range(HEADS):
                rs = slice(c * CHUNK, (c + 1) * CHUNK)
                cs = slice(h * CHUNK, (h + 1) * CHUNK)
                vblk = vnb[rs, cs]
                mixed = jnp.dot(ws_ref[h], vblk, preferred_element_type=F32) + bs_ref[h]
                dyb_ = dy_t[rs, cs]
                dmix = dyb_ * gu[rs, cs]
                dmb = dmix.astype(BF16)
                dz_ref[rs, cs] = (dyb_ * mixed * _dgelu(ua[rs, cs])).astype(BF16)
                dvn_ref[rs, cs] = lax.dot_general(ws_ref[h], dmb, TN_DIMS, preferred_element_type=F32)
                dws_ref[h] += causal * lax.dot_general(dmb, vblk, NT_DIMS, preferred_element_type=F32)
                dbs_ref[h] += jnp.sum(dmix, axis=1, keepdims=True)
        dvn = dvn_ref[...]
        add_row(0, jnp.sum(dvn * xhat_a, axis=0, keepdims=True))
        add_row(1, jnp.sum(dvn, axis=0, keepdims=True))
        dgv = _ln_bwd(dvn, xhat_a, rstd_a, lga_v)
        dz_ref[:, DA:2 * DA] = (dgv * _dgelu(va)).astype(BF16)
        lgb_v = lgb_ref[...]
        dyb_e = jnp.concatenate(
            [dy_t[:, DA:2 * DA], jnp.where(last, 0.0, dyf_ref[...].astype(F32))], axis=0)
        yb2_e = jnp.concatenate([yb2_ref[...], jnp.where(last, 0.0, yb2f_ref[...])], axis=0)
        n_e, xhat_b, rstd_b = _ln_fwd(yb2_e, lgb_v, lbb_ref[...])
        sgn = _sigmoid(n_e)
        dn = dyb_e * _dsilu(n_e, sgn)
        dy2 = _ln_bwd(dn, xhat_b, rstd_b, lgb_v)
        add_row(2, jnp.sum(dy2[:ts], axis=0, keepdims=True))
        add_row(3, jnp.sum(dn[:ts] * xhat_b[:ts], axis=0, keepdims=True))
        add_row(4, jnp.sum(dn[:ts], axis=0, keepdims=True))
        zh = jnp.where(i > 0, zh_ref[...], jnp.zeros_like(zh_ref[...])).astype(F32)
        xb_t, gb_t = z_t[:, 2 * DA:3 * DA], z_t[:, 3 * DA:4 * DA]
        sgb = _sigmoid(gb_t)
        _rolled_copies(fwd_rolled, jnp.concatenate(
            [zh[:, 0:DA] * _sigmoid(zh[:, DA:2 * DA]), xb_t * sgb], axis=0), False)
        _rolled_copies(bwd_rolled, dy2, True)
        for o in range(0, ts, CONV_ROWS):
            acc = jnp.zeros((CONV_ROWS, DA), F32)
            for sh in range(BCONV):
                q, r = divmod(sh, 8)
                acc = acc + cw_ref[BCONV - 1 - sh:BCONV - sh, :] * bwd_rolled[r, 8 * q + o:8 * q + o + CONV_ROWS, :]
            du_s[o:o + CONV_ROWS, :] = acc
        for sh in range(BCONV):
            q, r = divmod(sh, 8)
            acc = jnp.zeros((CONV_ROWS, DA), F32)
            for o in range(0, ts, CONV_ROWS):
                lo = HALO_B - 8 * q + o
                acc = acc + bwd_rolled[0, o:o + CONV_ROWS, :] * fwd_rolled[r, lo:lo + CONV_ROWS, :]
            add_row(8 + BCONV - 1 - sh, jnp.sum(acc, axis=0, keepdims=True))
        du = du_s[...]
        dz_ref[:, 2 * DA:3 * DA] = (du * sgb).astype(BF16)
        dz_ref[:, 3 * DA:4 * DA] = (du * xb_t * sgb * (1.0 - sgb)).astype(BF16)

    row = lambda i: (0, 0)
    nxt = lambda i: jnp.minimum((i + 1) * hb, s // HALO_B - 1)
    return pl.pallas_call(
        body, grid=(nt,), name=name,
        in_specs=[pl.BlockSpec((ts, 4 * DA), lambda i: (i, 0)),
                  pl.BlockSpec((HALO_B, 2 * DA), lambda i: (jnp.maximum(i * hb - 1, 0), 1)),
                  pl.BlockSpec((ts, 2 * DA), lambda i: (i, 0)),
                  pl.BlockSpec((HALO_B, DA), lambda i: (nxt(i), 1)),
                  pl.BlockSpec((ts, DA), lambda i: (i, 0)),
                  pl.BlockSpec((HALO_B, DA), lambda i: (nxt(i), 0)),
                  pl.BlockSpec((1, DA), row), pl.BlockSpec((1, DA), row),
                  pl.BlockSpec((HEADS, CHUNK, CHUNK), lambda i: (0, 0, 0)),
                  pl.BlockSpec((HEADS, CHUNK, 1), lambda i: (0, 0, 0)),
                  pl.BlockSpec((BCONV, DA), row), pl.BlockSpec((1, DA), row), pl.BlockSpec((1, DA), row)],
        out_specs=[pl.BlockSpec((ts, 4 * DA), lambda i: (i, 0)),
                   pl.BlockSpec((G512_ROWS, DA), row),
                   pl.BlockSpec((HEADS, CHUNK, CHUNK), lambda i: (0, 0, 0)),
                   pl.BlockSpec((HEADS, CHUNK, 1), lambda i: (0, 0, 0))],
        out_shape=[jax.ShapeDtypeStruct((s, 4 * DA), BF16), jax.ShapeDtypeStruct((G512_ROWS, DA), F32),
                   jax.ShapeDtypeStruct((HEADS, CHUNK, CHUNK), F32),
                   jax.ShapeDtypeStruct((HEADS, CHUNK, 1), F32)],
        scratch_shapes=[pltpu.VMEM((ts, DA), F32), pltpu.VMEM((8, ts + HALO_B, DA), F32),
                        pltpu.VMEM((8, ts + HALO_B, DA), F32), pltpu.VMEM((ts, DA), F32)],
        compiler_params=_params("arbitrary"),
    )(z, z, dy, dy, yb2, yb2, lga, lba, wsm, bs_col, cwb, lgb, lbb)


def _dw_cols(a, dy, nb, bn, name):
    s = a.shape[0]
    tm = _tile(s, 2048)
    nt = s // tm
    cpb = 4

    def body(a_ref, dy_ref, o_ref, acc):
        t = pl.program_id(1)
        p = lax.dot_general(a_ref[...], dy_ref[...], TN_DIMS, preferred_element_type=F32)

        @pl.when(t == 0)
        def _():
            for q in range(cpb):
                acc[q] = p[:, q * bn:(q + 1) * bn]

        @pl.when(t > 0)
        def _():
            for q in range(cpb):
                acc[q] += p[:, q * bn:(q + 1) * bn]

        @pl.when(t == nt - 1)
        def _():
            o_ref[...] = acc[...].astype(BF16)

    return pl.pallas_call(
        body, grid=(nb // cpb, nt), name=name,
        in_specs=[pl.BlockSpec((tm, D), lambda j, t: (t, 0)), pl.BlockSpec((tm, cpb * bn), lambda j, t: (t, j))],
        out_specs=pl.BlockSpec((cpb, D, bn), lambda j, t: (j, 0, 0)),
        out_shape=jax.ShapeDtypeStruct((nb, D, bn), BF16),
        scratch_shapes=[pltpu.VMEM((cpb, D, bn), F32)],
        compiler_params=_params("arbitrary", "arbitrary"),
    )(a, dy)


def _dw_rows(a, dy, name):
    s = a.shape[0]
    tm = _tile(s, 4096)
    nt = s // tm
    rb = 512

    def body(a_ref, dy_ref, o_ref, acc):
        t = pl.program_id(1)
        p = lax.dot_general(a_ref[...], dy_ref[...], TN_DIMS, preferred_element_type=F32)

        @pl.when(t == 0)
        def _():
            acc[...] = p

        @pl.when(t > 0)
        def _():
            acc[...] += p

        @pl.when(t == nt - 1)
        def _():
            o_ref[...] = acc[...].astype(BF16)

    return pl.pallas_call(
        body, grid=(D // rb, nt), name=name,
        in_specs=[pl.BlockSpec((tm, rb), lambda j, t: (t, j)), pl.BlockSpec((tm, D), lambda j, t: (t, 0))],
        out_specs=pl.BlockSpec((rb, D), lambda j, t: (j, 0)),
        out_shape=jax.ShapeDtypeStruct((D, D), BF16),
        scratch_shapes=[pltpu.VMEM((rb, D), F32)],
        compiler_params=_params("arbitrary", "arbitrary"),
    )(a, dy)


def _dw_up(h, dup, name):
    s = h.shape[0]
    tm = _tile(s, 4096)
    nt = s // tm

    def body(h_ref, d_ref, o_ref, acc):
        t = pl.program_id(1)
        p = lax.dot_general(d_ref[...], h_ref[...], TN_DIMS, preferred_element_type=F32)

        @pl.when(t == 0)
        def _():
            acc[...] = p

        @pl.when(t > 0)
        def _():
            acc[...] += p

        @pl.when(t == nt - 1)
        def _():
            o_ref[...] = acc[...].astype(BF16)

    return pl.pallas_call(
        body, grid=(NDEV, nt), name=name,
        in_specs=[pl.BlockSpec((tm, D), lambda b, t: (t, 0)),
                  pl.BlockSpec((None, None, tm, FB), lambda b, t: (b % NG, b // NG, t, 0))],
        out_specs=pl.BlockSpec((None, FB, D), lambda b, t: (b, 0, 0)),
        out_shape=jax.ShapeDtypeStruct((NDEV, FB, D), BF16),
        scratch_shapes=[pltpu.VMEM((FB, D), F32)],
        compiler_params=_params("arbitrary", "arbitrary"),
    )(h, dup)


def _dw_dn(a, df, name):
    s = df.shape[0]
    tm = _tile(s, 4096)
    nt = s // tm

    def body(a_ref, d_ref, o_ref, acc):
        t = pl.program_id(1)
        p = lax.dot_general(a_ref[...], d_ref[...], TN_DIMS, preferred_element_type=F32)

        @pl.when(t == 0)
        def _():
            acc[...] = p

        @pl.when(t > 0)
        def _():
            acc[...] += p

        @pl.when(t == nt - 1)
        def _():
            o_ref[...] = acc[...].astype(BF16)

    return pl.pallas_call(
        body, grid=(NG, nt), name=name,
        in_specs=[pl.BlockSpec((None, tm, FB), lambda m, t: (m, t, 0)), pl.BlockSpec((tm, D), lambda m, t: (t, 0))],
        out_specs=pl.BlockSpec((FB, D), lambda m, t: (m, 0)),
        out_shape=jax.ShapeDtypeStruct((DFF, D), BF16),
        scratch_shapes=[pltpu.VMEM((FB, D), F32)],
        compiler_params=_params("arbitrary", "arbitrary"),
    )(a, df)


def _place():
    x, y, c = lax.axis_index("x"), lax.axis_index("y"), lax.axis_index("c")
    chips = [(1 - x, y), (x, 1 - y), (1 - x, 1 - y)]
    return x, y, c, chips


def _zone(shard, dev):
    return lax.dynamic_update_slice(lax.empty((NDEV,) + shard.shape, shard.dtype), shard[None],
                                    (dev,) + (0,) * shard.ndim)


HBM_SPEC = pl.BlockSpec(memory_space=pltpu.HBM)
SEM_SPEC = pl.BlockSpec(memory_space=pltpu.SEMAPHORE)
DATAFLOW = pltpu.SideEffectType.DATAFLOW_SIDE_EFFECTING


def _hbm(a):
    return pltpu.with_memory_space_constraint(a, pltpu.HBM)


def _hbm_like(arrs):
    return [pltpu.HBM(a.shape, a.dtype) for a in arrs]


def _ag_start(srcs, lands, after, name):
    n = len(srcs)
    ns = 8 * n

    def body(*refs):
        src, land = refs[:n], refs[n:2 * n]
        sems = refs[2 * n + 1:2 * n + 1 + ns]
        token = refs[-1]
        x, y, c, chips = _place()
        peers = [(x, y, 1 - c)] + [(*chip, c) for chip in chips]
        for t in range(n):
            for k, to in enumerate(peers):
                pltpu.make_async_remote_copy(
                    src_ref=src[t], dst_ref=land[t].at[4 * x + 2 * y + c],
                    send_sem=sems[2 * (4 * t + k)], recv_sem=sems[2 * (4 * t + k) + 1],
                    device_id=to, device_id_type=MESH).start()
        token[...] = jnp.zeros_like(token)

    res = pl.pallas_call(
        body, name=name,
        in_specs=[HBM_SPEC] * (2 * n) + [ANY],
        out_specs=[SEM_SPEC] * ns + [HBM_SPEC] * (2 * n) + [pl.BlockSpec(memory_space=pltpu.VMEM)],
        out_shape=[pltpu.SemaphoreType.DMA(())] * ns + _hbm_like(srcs) + _hbm_like(lands)
        + [jax.ShapeDtypeStruct((8, 128), F32)],
        input_output_aliases={i: ns + i for i in range(2 * n)},
        compiler_params=pltpu.CompilerParams(has_side_effects=DATAFLOW),
    )(*[_hbm(a) for a in srcs], *[_hbm(a) for a in lands], after)
    sems = [[(res[2 * (4 * t + k)], res[2 * (4 * t + k) + 1]) for k in range(4)] for t in range(n)]
    return sems, res[ns:ns + n], res[ns + n:ns + 2 * n], res[-1]


def _ag_forward(srcs, lands, sems1, after, name):
    n = len(srcs)
    flat1 = [s for t in range(n) for k in range(1, 4) for s in sems1[t][k]]
    n1 = len(flat1)

    def body(*refs):
        src, land = refs[:n], refs[n:2 * n]
        s1 = refs[2 * n:2 * n + n1]
        s2 = refs[2 * n + n1 + 1:2 * n + n1 + 1 + 6 * n]
        x, y, c, chips = _place()
        for j, (cx, cy) in enumerate(chips):
            for t in range(n):
                blk = land[t].at[4 * cx + 2 * cy + c]
                pltpu.make_async_remote_copy(
                    src_ref=src[t], dst_ref=blk, send_sem=s1[2 * (3 * t + j)], recv_sem=s1[2 * (3 * t + j) + 1],
                    device_id=(cx, cy, c), device_id_type=MESH).wait_recv()
                pltpu.make_async_remote_copy(
                    src_ref=blk, dst_ref=blk, send_sem=s2[2 * (3 * t + j)], recv_sem=s2[2 * (3 * t + j) + 1],
                    device_id=(x, y, 1 - c), device_id_type=MESH).start()

    res = pl.pallas_call(
        body, name=name,
        in_specs=[HBM_SPEC] * (2 * n) + [SEM_SPEC] * n1 + [ANY],
        out_specs=[SEM_SPEC] * (6 * n) + [HBM_SPEC] * n,
        out_shape=[pltpu.SemaphoreType.DMA(())] * (6 * n) + _hbm_like(lands),
        input_output_aliases={n + i: 6 * n + i for i in range(n)},
        compiler_params=pltpu.CompilerParams(has_side_effects=DATAFLOW),
    )(*srcs, *lands, *flat1, after)
    sems2 = [[(res[2 * (3 * t + j)], res[2 * (3 * t + j) + 1]) for j in range(3)] for t in range(n)]
    return sems2, res[6 * n:]


def _ag_finish(srcs, lands, sems1, sems2, after, name):
    n = len(srcs)
    flat1 = [s for t in range(n) for k in range(4) for s in sems1[t][k]]
    flat2 = [s for t in range(n) for j in range(3) for s in sems2[t][j]]
    n1, n2 = len(flat1), len(flat2)

    def body(*refs):
        src, land = refs[:n], refs[n:2 * n]
        s1 = refs[2 * n:2 * n + n1]
        s2 = refs[2 * n + n1:2 * n + n1 + n2]
        x, y, c, chips = _place()
        sib = (x, y, 1 - c)
        for t in range(n):
            own = land[t].at[4 * x + 2 * y + 1 - c]
            pltpu.make_async_remote_copy(
                src_ref=src[t], dst_ref=own, send_sem=s1[8 * t], recv_sem=s1[8 * t + 1],
                device_id=sib, device_id_type=MESH).wait_recv()
            for k in range(4):
                pltpu.make_async_remote_copy(
                    src_ref=src[t], dst_ref=own, send_sem=s1[2 * (4 * t + k)], recv_sem=s1[2 * (4 * t + k) + 1],
                    device_id=sib, device_id_type=MESH).wait_send()
            for j, (cx, cy) in enumerate(chips):
                blk = land[t].at[4 * cx + 2 * cy + 1 - c]
                cp = pltpu.make_async_remote_copy(
                    src_ref=blk, dst_ref=blk, send_sem=s2[2 * (3 * t + j)], recv_sem=s2[2 * (3 * t + j) + 1],
                    device_id=sib, device_id_type=MESH)
                cp.wait_send()
                cp.wait_recv()

    return pl.pallas_call(
        body, name=name,
        in_specs=[HBM_SPEC] * (2 * n) + [SEM_SPEC] * (n1 + n2) + [ANY],
        out_specs=[HBM_SPEC] * n,
        out_shape=_hbm_like(lands),
        input_output_aliases={n + i: i for i in range(n)},
        compiler_params=pltpu.CompilerParams(has_side_effects=DATAFLOW),
    )(*srcs, *lands, *flat1, *flat2, after)


def _pair_copies(srcs, dsts, sems):
    x, y, c, _ = _place()
    nt = len(srcs)
    return [pltpu.make_async_remote_copy(
        src_ref=srcs[t].at[2 * j + 1 - c], dst_ref=dsts[t].at[j],
        send_sem=sems[2 * (NCHIP * t + j)], recv_sem=sems[2 * (NCHIP * t + j) + 1],
        device_id=(x, y, 1 - c), device_id_type=MESH) for t in range(nt) for j in range(NCHIP)]


def _pair_start(grads, carry, name):
    nt = len(grads)
    ns = 2 * NCHIP * nt
    zones = [_hbm(lax.empty((NCHIP,) + a.shape[1:], a.dtype)) for a in grads]
    extra = [] if carry is None else [_hbm(carry)]
    ne = len(extra)

    def body(*refs):
        for cp in _pair_copies(refs[:nt], refs[nt:2 * nt], refs[2 * nt + ne:2 * nt + ne + ns]):
            cp.start()

    res = pl.pallas_call(
        body, name=name,
        in_specs=[HBM_SPEC] * (2 * nt + ne),
        out_specs=[SEM_SPEC] * ns + [HBM_SPEC] * (2 * nt + ne),
        out_shape=[pltpu.SemaphoreType.DMA(())] * ns + _hbm_like(grads) + _hbm_like(zones) + _hbm_like(extra),
        input_output_aliases={i: ns + i for i in range(2 * nt + ne)},
        compiler_params=pltpu.CompilerParams(has_side_effects=DATAFLOW),
    )(*[_hbm(a) for a in grads], *zones, *extra)
    handle = (list(res[:ns]), list(res[ns:ns + nt]), list(res[ns + nt:ns + 2 * nt]))
    return handle, (res[ns + 2 * nt] if ne else None)


def _pair_wait(handle, after, name):
    sems, srcs, zones = handle
    nt, ns = len(srcs), len(sems)

    def body(*refs):
        for cp in _pair_copies(refs[:nt], refs[nt:2 * nt], refs[2 * nt:2 * nt + ns]):
            cp.wait_send()
            cp.wait_recv()

    return pl.pallas_call(
        body, name=name,
        in_specs=[HBM_SPEC] * (2 * nt) + [SEM_SPEC] * ns + [ANY],
        out_specs=[HBM_SPEC] * nt,
        out_shape=_hbm_like(zones),
        input_output_aliases={nt + i: i for i in range(nt)},
        compiler_params=pltpu.CompilerParams(has_side_effects=DATAFLOW),
    )(*srcs, *zones, *sems, after)


def _rows_tile(r, row_bytes, cap_bytes):
    best = None
    for tr in range(16, r + 1, 16):
        if r % tr == 0 and tr * row_bytes <= cap_bytes:
            best = tr
    return best if best is not None else r


def _pair_sum(own, got, cidx, name):
    _, _, r, cdim = own.shape
    tr = _rows_tile(r, 2 * cdim, 2 * 1024 * 1024)

    def body(c_ref, a_ref, b_ref, o_ref):
        o_ref[...] = (a_ref[...].astype(F32) + b_ref[...].astype(F32)).astype(BF16)

    return pl.pallas_call(
        body, name=name,
        grid_spec=pltpu.PrefetchScalarGridSpec(
            num_scalar_prefetch=1, grid=(NCHIP, r // tr),
            in_specs=[pl.BlockSpec((None, None, tr, cdim), lambda j, i, c_ref: (j, c_ref[0], i, 0)),
                      pl.BlockSpec((None, tr, cdim), lambda j, i, c_ref: (j, i, 0))],
            out_specs=pl.BlockSpec((None, tr, cdim), lambda j, i, c_ref: (j, i, 0))),
        out_shape=jax.ShapeDtypeStruct((NCHIP, r, cdim), BF16),
        compiler_params=_params("arbitrary", "arbitrary"),
    )(cidx, own, got)


def _chip_copies(srcs, zones, slots, sems):
    x, y, c, chips = _place()
    out = []
    for t, (z, l) in enumerate(slots):
        for k, (cx, cy) in enumerate(chips):
            dst = zones[z].at[k] if l is None else zones[z].at[k, l]
            out.append(pltpu.make_async_remote_copy(
                src_ref=srcs[t].at[2 * cx + cy], dst_ref=dst,
                send_sem=sems[2 * (3 * t + k)], recv_sem=sems[2 * (3 * t + k) + 1],
                device_id=(cx, cy, c), device_id_type=MESH))
    return out


def _chip_start(sums, zones, slots, carry, name):
    nt, nz = len(sums), len(zones)
    ns = 6 * nt
    extra = [] if carry is None else [_hbm(carry)]
    ne = len(extra)

    def body(*refs):
        for cp in _chip_copies(refs[:nt], refs[nt:nt + nz], slots, refs[nt + nz + ne:nt + nz + ne + ns]):
            cp.start()

    res = pl.pallas_call(
        body, name=name,
        in_specs=[HBM_SPEC] * (nt + nz + ne),
        out_specs=[SEM_SPEC] * ns + [HBM_SPEC] * (nt + nz + ne),
        out_shape=[pltpu.SemaphoreType.DMA(())] * ns + _hbm_like(sums) + _hbm_like(zones) + _hbm_like(extra),
        input_output_aliases={i: ns + i for i in range(nt + nz + ne)},
        compiler_params=pltpu.CompilerParams(has_side_effects=DATAFLOW),
    )(*[_hbm(a) for a in sums], *zones, *extra)
    return (list(res[:ns]), list(res[ns:ns + nt]), list(res[ns + nt:ns + nt + nz]),
            (res[ns + nt + nz] if ne else None))


def _chip_wait(started, zones, zone_ids, after, name):
    started = [(sums, [(zone_ids.index(z), l) for z, l in slots], sems) for sums, slots, sems in started]
    nz = len(zones)
    flat_src = [a for sums, _, _ in started for a in sums]
    flat_sem = [s for _, _, sems in started for s in sems]
    n_src, n_sem = len(flat_src), len(flat_sem)

    def body(*refs):
        srcs, zs, sems = refs[:n_src], refs[n_src:n_src + nz], refs[n_src + nz:n_src + nz + n_sem]
        so, se = 0, 0
        for sums, slots, sem_list in started:
            for cp in _chip_copies(srcs[so:so + len(sums)], zs, slots, sems[se:se + len(sem_list)]):
                cp.wait_send()
                cp.wait_recv()
            so += len(sums)
            se += len(sem_list)

    return pl.pallas_call(
        body, name=name,
        in_specs=[HBM_SPEC] * (n_src + nz) + [SEM_SPEC] * n_sem + [ANY],
        out_specs=[HBM_SPEC] * nz,
        out_shape=_hbm_like(zones),
        input_output_aliases={n_src + i: i for i in range(nz)},
        compiler_params=pltpu.CompilerParams(has_side_effects=DATAFLOW),
    )(*flat_src, *zones, *flat_sem, after)


def _small_allreduce(parts, after, name):
    nt = len(parts)

    def body(*refs):
        srcs, outs, bufs = refs[:nt], refs[nt + 1:2 * nt + 1], refs[2 * nt + 1:3 * nt + 1]
        send_sems, recv_sems = refs[3 * nt + 1:]
        x, y, c, _ = _place()
        peers = [(x, y, 1 - c), (1 - x, y, c), (x, 1 - y, c)]
        for t in range(nt):
            outs[t][...] = srcs[t][...]
        for step, peer in enumerate(peers):
            copies = [pltpu.make_async_remote_copy(
                src_ref=outs[t], dst_ref=bufs[t].at[step],
                send_sem=send_sems.at[step, t], recv_sem=recv_sems.at[step, t],
                device_id=peer, device_id_type=MESH) for t in range(nt)]
            for cp in copies:
                cp.start()
            for cp in copies:
                cp.wait()
            for t in range(nt):
                outs[t][...] = outs[t][...] + bufs[t][step]

    vm = pl.BlockSpec(memory_space=pltpu.VMEM)
    return pl.pallas_call(
        body, name=name,
        in_specs=[vm] * nt + [ANY], out_specs=[vm] * nt,
        out_shape=[jax.ShapeDtypeStruct(a.shape, F32) for a in parts],
        scratch_shapes=[pltpu.VMEM((3,) + a.shape, F32) for a in parts]
        + [pltpu.SemaphoreType.DMA((3, nt)), pltpu.SemaphoreType.DMA((3, nt))],
        compiler_params=pltpu.CompilerParams(has_side_effects=True, vmem_limit_bytes=VMEM_LIMIT),
    )(*parts, after)


def _adam_math(w, g, m, v):
    m2 = ADAM_B1 * m + (1.0 - ADAM_B1) * g
    v2 = ADAM_B2 * v + (1.0 - ADAM_B2) * (g * g)
    m_hat = m2 / (1.0 - ADAM_B1 ** ADAM_STEP)
    v_hat = v2 / (1.0 - ADAM_B2 ** ADAM_STEP)
    delta = -ADAM_LR * (m_hat / (jnp.sqrt(v_hat) + ADAM_EPS) + ADAM_WD * w)
    return delta, m2, v2


def _adam_big(w, m, v, parts, mine, chip, name):
    nl, r, cdim = w.shape
    tr = _rows_tile(r, 4 * cdim, 3 * 512 * 1024)

    def body(c_ref, w_ref, m_ref, v_ref, p_ref, *rest):
        mine_refs, (g_ref, d_ref, mo_ref, vo_ref) = rest[:nl], rest[nl:]
        own = mine_refs[0][...]
        for l in range(1, nl):
            own = jnp.where(pl.program_id(0) == l, mine_refs[l][...], own)
        g = ((p_ref[0].astype(F32) + p_ref[1].astype(F32)) + p_ref[2].astype(F32)) + own.astype(F32)
        delta, m2, v2 = _adam_math(w_ref[...], g, m_ref[...], v_ref[...])
        g_ref[...] = g
        d_ref[...] = delta
        mo_ref[...] = m2
        vo_ref[...] = v2

    spec = pl.BlockSpec((None, tr, cdim), lambda l, i, c_ref: (l, i, 0))
    mine_specs = [pl.BlockSpec((None, tr, cdim), lambda l, i, c_ref, ll=ll: (c_ref[0], jnp.where(l == ll, i, 0), 0))
                  for ll in range(nl)]
    return pl.pallas_call(
        body, name=name,
        grid_spec=pltpu.PrefetchScalarGridSpec(
            num_scalar_prefetch=1, grid=(nl, r // tr),
            in_specs=[spec, spec, spec, pl.BlockSpec((3, None, tr, cdim), lambda l, i, c_ref: (0, l, i, 0))]
            + mine_specs,
            out_specs=[spec] * 4),
        out_shape=[jax.ShapeDtypeStruct(w.shape, F32)] * 4,
        compiler_params=_params("arbitrary", "arbitrary"),
    )(chip, w, m, v, parts, *mine)


def _adam_small(ws, gs, ms, vs, name):
    n = len(ws)

    def body(*refs):
        w_r, g_r, m_r, v_r = refs[:n], refs[n:2 * n], refs[2 * n:3 * n], refs[3 * n:4 * n]
        d_o, m_o, v_o = refs[4 * n:5 * n], refs[5 * n:6 * n], refs[6 * n:7 * n]
        for t in range(n):
            delta, m2, v2 = _adam_math(w_r[t][...], g_r[t][...], m_r[t][...], v_r[t][...])
            d_o[t][...] = delta
            m_o[t][...] = m2
            v_o[t][...] = v2

    vm = pl.BlockSpec(memory_space=pltpu.VMEM)
    shapes = [jax.ShapeDtypeStruct(a.shape, F32) for a in ws]
    return pl.pallas_call(
        body, name=name, in_specs=[vm] * (4 * n), out_specs=[vm] * (3 * n), out_shape=shapes * 3,
        compiler_params=pltpu.CompilerParams(vmem_limit_bytes=VMEM_LIMIT),
    )(*ws, *gs, *ms, *vs)


def kernel(x, norm_mix, norm_ffn, norm_final, ab_w_in, a_ln_g, a_ln_b, a_w_s, a_b_s, b_conv_w, b_conv_b, b_ln_g, b_ln_b, ab_w_out, c_w_in, c_conv_w, c_w_out, f_w_up, f_conv_w, f_w_down, loss_target, m_norm_mix, m_norm_ffn, m_norm_final, m_ab_w_in, m_a_ln_g, m_a_ln_b, m_a_w_s, m_a_b_s, m_b_conv_w, m_b_conv_b, m_b_ln_g, m_b_ln_b, m_ab_w_out, m_c_w_in, m_c_conv_w, m_c_w_out, m_f_w_up, m_f_conv_w, m_f_w_down, v_norm_mix, v_norm_ffn, v_norm_final, v_ab_w_in, v_a_ln_g, v_a_ln_b, v_a_w_s, v_a_b_s, v_b_conv_w, v_b_conv_b, v_b_ln_g, v_b_ln_b, v_ab_w_out, v_c_w_in, v_c_conv_w, v_c_w_out, v_f_w_up, v_f_conv_w, v_f_w_down):
    s = x.shape[1]
    x0 = x.reshape(s, D)
    tgt = loss_target.reshape(s, D)
    xi, yi, ci = lax.axis_index("x"), lax.axis_index("y"), lax.axis_index("c")
    dev = 4 * xi + 2 * yi + ci
    cidx = ci.astype(jnp.int32).reshape(1)

    bf = lambda a: a.astype(BF16)
    slab_w = 6 * CHUNK
    pad = lambda a, rows: jnp.pad(a, ((0, rows - a.shape[0]), (0, slab_w - a.shape[1])))
    slab = jnp.concatenate([pad(b_conv_w[0], 32), pad(c_conv_w[0], 8), pad(f_conv_w.reshape(6, FB), 8)], axis=0)
    later = [bf(ab_w_in[0]), bf(ab_w_out[0]), slab, bf(f_w_up[0]), bf(f_w_down[0]), bf(c_w_in[0]), bf(c_w_out[0]),
             bf(f_w_up[1]), bf(f_w_down[1])]
    lands = [_zone(a, dev) for a in later]
    groups = [[0], [1, 2], [3, 4], [5, 6], [7, 8]]
    ag_sems, later, lands, ag_token = _ag_start(later, lands, x0, "ag_start")

    causal = jnp.tril(jnp.ones((CHUNK, CHUNK), F32))
    wsm = (a_w_s[0] * causal).astype(BF16)
    bs_col = a_b_s.reshape(HEADS, CHUNK, 1)
    nm = [norm_mix[0:1], norm_mix[1:2]]
    nf = [norm_ffn[0:1], norm_ffn[1:2]]
    nfin = norm_final.reshape(1, D)

    def arrive(g, after_ici, after_d2d, tag):
        srcs = [later[t] for t in groups[g]]
        zone = [lands[t] for t in groups[g]]
        sems1 = [ag_sems[t] for t in groups[g]]
        sems2, zone = _ag_forward(srcs, zone, sems1, after_ici, "ag_forward_" + tag)
        return _ag_finish(srcs, zone, sems1, sems2, after_d2d, "ag_finish_" + tag)

    h0 = _rms_fwd(x0, nm[0], "rms_mix0", after=ag_token)
    (win0,) = arrive(0, h0, h0, "w_in")
    z = _mm_in(h0, win0, "mm_ab_in")
    wout0, slab_g = arrive(1, z, z, "first")
    wout0 = wout0.reshape(D, D)
    bcw = jnp.transpose(slab_g[:, 0:BCONV, 0:DA // NDEV], (1, 0, 2)).reshape(BCONV, DA)
    ccw = jnp.transpose(slab_g[:, 32:35, 0:D // NDEV], (1, 0, 2)).reshape(3, D)
    fcw_g = slab_g[:, 40:46, 0:FB].reshape(2, NG, 2, 3, FB)
    fcws = [fcw_g[:, :, 0], fcw_g[:, :, 1]]
    ycat, yb2 = _ab_fwd(z, a_ln_g, a_ln_b, wsm, bs_col, bcw, b_conv_b, b_ln_g, b_ln_b, "ab_fwd")
    x1, h1 = _mm_out(ycat, wout0, x0, nf[0], "mm_ab_out")
    wup0, wdn0 = arrive(2, x1, x1, "ffn0")
    up0, upc0, x2, h2 = _ffn_fwd(h1, x1, wup0.reshape(2, NG, D, FB), fcws[0], wdn0.reshape(DFF, D), nm[1],
                                 "ffn_fwd0")
    cin, cout = arrive(3, x2, x2, "c")
    cout = cout.reshape(D, D)
    zc = _mm_in(h2, cin, "mm_c_in")
    rc = _c_fwd(zc, ccw, "c_fwd")
    x3, h3 = _mm_out(rc, cout, x2, nf[1], "mm_c_out")
    wup1, wdn1 = arrive(4, rc, x3, "ffn1")
    wups = [wup0.reshape(2, NG, D, FB), wup1.reshape(2, NG, D, FB)]
    wdns = [wdn0.reshape(DFF, D), wdn1.reshape(DFF, D)]
    up1, upc1, x4 = _ffn_fwd(h3, x3, wups[1], fcws[1], wdns[1], None, "ffn_fwd1")
    dx4, dx4b, dnfin, loss_part = _final(x4, tgt, nfin, "final_loss")

    zshape = lambda *sh: _hbm(lax.empty((3,) + sh, BF16))
    zones = [zshape(D, 2 * D // NDEV), zshape(D // NDEV, D), zshape(D, 3 * D // NDEV), zshape(D // NDEV, D),
             zshape(2, FB, D), zshape(2, DFF // NDEV, D)]
    started = []

    def pair_sums(grads, handle, after, tag):
        del grads
        got = _pair_wait(handle, after, "rs_pair_wait_" + tag)
        return [_pair_sum(b.reshape((NCHIP, 2) + b.shape[1:]), g, cidx, "rs_pair_sum_%s%d" % (tag, t))
                for t, (b, g) in enumerate(zip(handle[1], got))]

    def chip_start(sums, slots, carry, tag):
        sems, sums, new_zones, carry = _chip_start(sums, zones, slots, carry, "rs_chip_start_" + tag)
        zones[:] = new_zones
        started.append((sums, slots, sems))
        return sums, carry

    rows8 = lambda g, r: g.reshape(NDEV, r, D)
    a1, dup1, dx3, dx3b, dnf1, dfcw1 = _ffn_bwd(dx4, up1, upc1, wups[1], fcws[1], wdns[1], x3, nf[1], "ffn_bwd1")
    g_f1 = [_dw_up(h3, dup1, "dw_up1"), rows8(_dw_dn(a1, dx4b, "dw_dn1"), DFF // NDEV)]
    hd_f1, dx3b = _pair_start(g_f1, dx3b, "rs_pair_start_f1")
    drc = _mm_nt(dx3b, cout, "mm_c_out_bwd")
    g_cout = rows8(_dw_rows(rc, dx3b, "dw_c_out"), D // NDEV)
    s_f1 = pair_sums(g_f1, hd_f1, g_cout, "f1")
    s_f1, drc = chip_start(s_f1, [(4, 1), (5, 1)], drc, "f1")
    dzc, dccw = _c_bwd(drc, zc, ccw, "c_bwd")
    dx2, dx2b, dnm1 = _mm_nt_rms(dzc, cin, x2, nm[1], dx3, True, "mm_c_in_bwd")
    g_c = [_dw_cols(h2, dzc, NDEV, 3 * D // NDEV, "dw_c_in"), g_cout]
    hd_c, dx2 = _pair_start(g_c, dx2, "rs_pair_start_c")
    a0, dup0, dx1, dx1b, dnf0, dfcw0 = _ffn_bwd(dx2, up0, upc0, wups[0], fcws[0], wdns[0], x1, nf[0], "ffn_bwd0")
    s_c = pair_sums(g_c, hd_c, dx1b, "c")
    s_c, dx1b = chip_start(s_c, [(2, None), (3, None)], dx1b, "c")
    g_f0 = [_dw_up(h1, dup0, "dw_up0"), rows8(_dw_dn(a0, dx2b, "dw_dn0"), DFF // NDEV)]
    hd_f0, dx1b = _pair_start(g_f0, dx1b, "rs_pair_start_f0")
    dycat = _mm_nt(dx1b, wout0, "mm_ab_out_bwd")
    g_wout0 = rows8(_dw_rows(ycat, dx1b, "dw_ab_out"), D // NDEV)
    s_f0 = pair_sums(g_f0, hd_f0, g_wout0, "f0")
    s_f0, dycat = chip_start(s_f0, [(4, 0), (5, 0)], dycat, "f0")
    dz, g512, dws, dbs = _ab_bwd(dycat, z, yb2, a_ln_g, a_ln_b, wsm, bs_col, bcw, b_ln_g, b_ln_b, "ab_bwd")
    grad_x, dnm0 = _mm_nt_rms(dz, win0, x0, nm[0], dx1, False, "mm_ab_in_bwd")
    g_ab = [_dw_cols(h0, dz, NDEV, 2 * D // NDEV, "dw_ab_in"), g_wout0]
    hd_ab, _ = _pair_start(g_ab, None, "rs_pair_start_ab")

    g1024 = jnp.concatenate([dnm0, dnm1, dnf0, dnf1, dnfin, dccw], axis=0)
    gfc = jnp.concatenate([dfcw0, dfcw1], axis=0).reshape(2 * NG * 2 * 3, FB)
    g1024, g512, dws, dbs, gfc, loss_sum = _small_allreduce(
        [g1024, g512, dws.reshape(HEADS * CHUNK, CHUNK), dbs.reshape(HEADS, CHUNK), gfc, loss_part], hd_ab[1][0],
        "small_allreduce")
    loss = loss_sum[0, 0]
    s_ab = pair_sums(g_ab, hd_ab, g1024, "ab")
    s_ab, _ = chip_start(s_ab, [(0, None), (1, None)], None, "ab")
    p_cin, p_cout, p_wup, p_wdn = _chip_wait(started[:3], zones[2:], [2, 3, 4, 5], s_ab[0], "rs_chip_wait_early")

    chip = (2 * xi + yi).astype(jnp.int32).reshape(1)

    def big_update(w, m, v, parts, mine, name):
        shp = w.shape
        w3, m3, v3 = (a.reshape((-1,) + shp[-2:]) for a in (w, m, v))
        p4 = parts.reshape((3,) + w3.shape)
        return [o.reshape(shp) for o in _adam_big(w3, m3, v3, p4, mine, chip, name)]

    u_cin = big_update(c_w_in, m_c_w_in, v_c_w_in, p_cin, [s_c[0]], "adam_c_w_in")
    u_cout = big_update(c_w_out, m_c_w_out, v_c_w_out, p_cout, [s_c[1]], "adam_c_w_out")
    tr_ = lambda a: jnp.swapaxes(a, 1, 2)
    u_wup = [tr_(o) for o in big_update(tr_(f_w_up), tr_(m_f_w_up), tr_(v_f_w_up), p_wup,
                                        [s_f0[0], s_f1[0]], "adam_f_w_up")]
    u_wdn = big_update(f_w_down, m_f_w_down, v_f_w_down, p_wdn, [s_f0[1], s_f1[1]], "adam_f_w_down")
    p_win0, p_wout0 = _chip_wait(started[3:], zones[:2], [0, 1], u_wdn[0], "rs_chip_wait_late")
    u_win0 = big_update(ab_w_in, m_ab_w_in, v_ab_w_in, p_win0, [s_ab[0]], "adam_ab_w_in")
    u_wout0 = big_update(ab_w_out, m_ab_w_out, v_ab_w_out, p_wout0, [s_ab[1]], "adam_ab_w_out")

    g_norm_mix = g1024[0:2]
    g_norm_ffn = g1024[2:4]
    g_norm_final = g1024[4:5]
    g_ccw = lax.dynamic_slice(g1024[5:8], (0, dev * (D // NDEV)), (3, D // NDEV))
    g_bcw = lax.dynamic_slice(g512[8:8 + BCONV], (0, dev * (DA // NDEV)), (BCONV, DA // NDEV))
    gfc = gfc.reshape(2, NG, 2, 3, FB)
    g_fcw = lax.dynamic_slice(gfc, (0, dev % NG, dev // NG, 0, 0), (2, 1, 1, 3, FB)).reshape(2, 3, FB)
    small_w = [norm_mix, norm_ffn, nfin, a_ln_g, a_ln_b, a_w_s[0], a_b_s[0], b_conv_w[0], b_conv_b,
               b_ln_g, b_ln_b, c_conv_w[0], f_conv_w]
    small_g = [g_norm_mix, g_norm_ffn, g_norm_final, g512[0:1], g512[1:2],
               dws.reshape(HEADS, CHUNK, CHUNK), dbs, g_bcw, g512[2:3],
               g512[3:4], g512[4:5], g_ccw, g_fcw]
    small_m = [m_norm_mix, m_norm_ffn, m_norm_final.reshape(1, D), m_a_ln_g, m_a_ln_b, m_a_w_s[0], m_a_b_s[0],
               m_b_conv_w[0], m_b_conv_b, m_b_ln_g, m_b_ln_b, m_c_conv_w[0], m_f_conv_w]
    small_v = [v_norm_mix, v_norm_ffn, v_norm_final.reshape(1, D), v_a_ln_g, v_a_ln_b, v_a_w_s[0], v_a_b_s[0],
               v_b_conv_w[0], v_b_conv_b, v_b_ln_g, v_b_ln_b, v_c_conv_w[0], v_f_conv_w]
    upd = _adam_small(small_w, small_g, small_m, small_v, "adam_small")
    ns = len(small_w)
    orig = [norm_mix, norm_ffn, norm_final, a_ln_g, a_ln_b, a_w_s, a_b_s, b_conv_w, b_conv_b,
            b_ln_g, b_ln_b, c_conv_w, f_conv_w]
    sg_out = [g.reshape(o.shape) for g, o in zip(small_g, orig)]
    sd_out = [a.reshape(o.shape) for a, o in zip(upd[0:ns], orig)]
    sm_out = [a.reshape(o.shape) for a, o in zip(upd[ns:2 * ns], orig)]
    sv_out = [a.reshape(o.shape) for a, o in zip(upd[2 * ns:3 * ns], orig)]

    def assemble(small, k):
        return [small[0], small[1], small[2], u_win0[k], small[3], small[4], small[5], small[6], small[7],
                small[8], small[9], small[10], u_wout0[k], u_cin[k], small[11], u_cout[k], u_wup[k],
                small[12], u_wdn[k]]

    grads = assemble(sg_out, 0)
    deltas = assemble(sd_out, 1)
    new_m = assemble(sm_out, 2)
    new_v = assemble(sv_out, 3)
    return (loss, grad_x.reshape(1, s, D), *grads, *deltas, *new_m, *new_v)
```

```python
import math

import jax
import jax.numpy as jnp
from jax import lax
from jax.experimental import pallas as pl
from jax.experimental.pallas import tpu as pltpu

F32 = jnp.float32
BF16 = jnp.bfloat16

D = 1024
DA = 512
HEADS = 4
CHUNK = 128
DFF = 2816
NDEV = 8
NCHIP = 4
FB = DFF * 2 // NDEV
NG = DFF // FB
BCONV = 31
EPS = 1e-6
HALO = 16
HALO_B = 32
RC = 32
NPART = 2
VMEM_LIMIT = 52 * 1024 * 1024
INV_SQRT2 = 1.0 / math.sqrt(2.0)
INV_SQRT_2PI = 1.0 / math.sqrt(2.0 * math.pi)

ADAM_LR = 0.001
ADAM_B1 = 0.9
ADAM_B2 = 0.999
ADAM_EPS = 1e-08
ADAM_WD = 0.01
ADAM_STEP = 10

MESH = pl.DeviceIdType.MESH
ANY = pl.BlockSpec(memory_space=pl.ANY)
NT_DIMS = (((1,), (1,)), ((), ()))
TN_DIMS = (((0,), (0,)), ((), ()))


def _params(*sem):
    return pltpu.CompilerParams(dimension_semantics=sem, vmem_limit_bytes=VMEM_LIMIT)


def _tile(s, want):
    return min(want, s)


def _sigmoid(x):
    return jax.nn.sigmoid(x)


def _dsilu(x, sg):
    return sg * (1.0 + x * (1.0 - sg))


def _gelu(x):
    return 0.5 * x * (1.0 + lax.erf(x * INV_SQRT2))


def _dgelu(x):
    return 0.5 * (1.0 + lax.erf(x * INV_SQRT2)) + x * jnp.exp(-0.5 * x * x) * INV_SQRT_2PI


def _ln_fwd(x, g, b):
    mu = jnp.mean(x, axis=-1, keepdims=True)
    xc = x - mu
    var = jnp.mean(xc * xc, axis=-1, keepdims=True)
    rstd = lax.rsqrt(var + EPS)
    xhat = xc * rstd
    return xhat * g + b, xhat, rstd


def _ln_bwd(dy, xhat, rstd, g):
    dxh = dy * g
    m1 = jnp.mean(dxh, axis=-1, keepdims=True)
    m2 = jnp.mean(dxh * xhat, axis=-1, keepdims=True)
    return rstd * (dxh - m1 - xhat * m2)


def _rms_bwd_math(dh, x, g):
    r = lax.rsqrt(jnp.mean(x * x, axis=-1, keepdims=True) + EPS)
    xhat = x * r
    dg = jnp.sum(dh * xhat, axis=0, keepdims=True)
    u = dh * g
    dx = r * (u - xhat * jnp.mean(u * xhat, axis=-1, keepdims=True))
    return dx, dg


def _conv3(xe, cw, halo):
    x0 = xe[halo:]
    x1 = pltpu.roll(xe, 1, 0)[halo:]
    x2 = pltpu.roll(xe, 2, 0)[halo:]
    return cw[2] * x0 + cw[1] * x1 + cw[0] * x2, (x0, x1, x2)


def _conv3_bwd_in(dce, cw, ts):
    n = dce.shape[0]
    d1 = pltpu.roll(dce, n - 1, 0)[:ts]
    d2 = pltpu.roll(dce, n - 2, 0)[:ts]
    return cw[2] * dce[:ts] + cw[1] * d1 + cw[0] * d2


def _conv3_bwd_w(dc, taps):
    x0, x1, x2 = taps
    return [jnp.sum(dc * x2, axis=0, keepdims=True), jnp.sum(dc * x1, axis=0, keepdims=True),
            jnp.sum(dc * x0, axis=0, keepdims=True)]


def _rms_fwd(x, g, name, after=None):
    s = x.shape[0]
    ts = _tile(s, 512)

    def body(x_ref, g_ref, *rest):
        h_ref = rest[-1]
        xv = x_ref[...]
        r = lax.rsqrt(jnp.mean(xv * xv, axis=-1, keepdims=True) + EPS)
        h_ref[...] = (xv * r * g_ref[...]).astype(BF16)

    extra = [] if after is None else [after]
    return pl.pallas_call(
        body, grid=(s // ts,), name=name,
        in_specs=[pl.BlockSpec((ts, D), lambda i: (i, 0)), pl.BlockSpec((1, D), lambda i: (0, 0))]
        + [ANY] * len(extra),
        out_specs=pl.BlockSpec((ts, D), lambda i: (i, 0)),
        out_shape=jax.ShapeDtypeStruct((s, D), BF16),
        compiler_params=_params("parallel"),
    )(x, g, *extra)


MXU_COLS = 256


def _pair(bn):
    return 1 if bn % MXU_COLS == 0 else 2


def _cols(w_ref, b, pair):
    return w_ref[b] if pair == 1 else jnp.concatenate([w_ref[b + q] for q in range(pair)], axis=1)


def _mm_in(h, wblk, name):
    s = h.shape[0]
    nb, _, bn = wblk.shape
    pair = _pair(bn)
    ts = _tile(s, 1024)

    def body(h_ref, w_ref, o_ref):
        hv = h_ref[...]
        for b in range(0, nb, pair):
            o_ref[:, b * bn:(b + pair) * bn] = jnp.dot(hv, _cols(w_ref, b, pair),
                                                       preferred_element_type=F32).astype(BF16)

    return pl.pallas_call(
        body, grid=(s // ts,), name=name,
        in_specs=[pl.BlockSpec((ts, D), lambda i: (i, 0)), pl.BlockSpec((nb, D, bn), lambda i: (0, 0, 0))],
        out_specs=pl.BlockSpec((ts, nb * bn), lambda i: (i, 0)),
        out_shape=jax.ShapeDtypeStruct((s, nb * bn), BF16),
        compiler_params=_params("parallel"),
    )(h, wblk)


def _rms_math(xv, g):
    r = lax.rsqrt(jnp.mean(xv * xv, axis=-1, keepdims=True) + EPS)
    return (xv * r * g).astype(BF16)


def _mm_out(y, w, xres, gnext, name):
    s = y.shape[0]
    ts = _tile(s, 1024)

    def body(y_ref, w_ref, x_ref, g_ref, o_ref, h_ref):
        xn = x_ref[...] + jnp.dot(y_ref[...], w_ref[...], preferred_element_type=F32)
        o_ref[...] = xn
        h_ref[...] = _rms_math(xn, g_ref[...])

    return pl.pallas_call(
        body, grid=(s // ts,), name=name,
        in_specs=[pl.BlockSpec((ts, D), lambda i: (i, 0)), pl.BlockSpec((D, D), lambda i: (0, 0)),
                  pl.BlockSpec((ts, D), lambda i: (i, 0)), pl.BlockSpec((1, D), lambda i: (0, 0))],
        out_specs=[pl.BlockSpec((ts, D), lambda i: (i, 0)), pl.BlockSpec((ts, D), lambda i: (i, 0))],
        out_shape=[jax.ShapeDtypeStruct((s, D), F32), jax.ShapeDtypeStruct((s, D), BF16)],
        compiler_params=_params("parallel"),
    )(y, w, xres, gnext)


CONV_ROWS = 32


def _rolled_copies(dst_ref, xe, back):
    n = xe.shape[0]
    dst_ref[0] = xe
    for r in range(1, 8):
        dst_ref[r] = pltpu.roll(xe, n - r if back else r, 0)


def _conv31(rolled_ref, cw_ref, ts, out_ref, bias):
    for o in range(0, ts, CONV_ROWS):
        acc = jnp.zeros((CONV_ROWS, DA), F32) + bias
        for sh in range(BCONV):
            q, r = divmod(sh, 8)
            lo = HALO_B - 8 * q + o
            acc = acc + cw_ref[BCONV - 1 - sh:BCONV - sh, :] * rolled_ref[r, lo:lo + CONV_ROWS, :]
        out_ref[o:o + CONV_ROWS, :] = acc


def _ab_fwd(z, lga, lba, wsm, bs_col, cwb, cbb, lgb, lbb, name):
    s = z.shape[0]
    ts = _tile(s, 512)
    hb = ts // HALO_B

    def body(z_ref, zh_ref, lga_ref, lba_ref, ws_ref, bs_ref, cw_ref, cb_ref, lgb_ref, lbb_ref,
             y_ref, yb2_ref, rolled):
        i = pl.program_id(0)
        z_t = z_ref[...].astype(F32)
        gu = _gelu(z_t[:, 0:DA])
        gv = _gelu(z_t[:, DA:2 * DA])
        vn, _, _ = _ln_fwd(gv, lga_ref[...], lba_ref[...])
        vnb = vn.astype(BF16)
        for c in range(ts // CHUNK):
            for h in range(HEADS):
                rs = slice(c * CHUNK, (c + 1) * CHUNK)
                cs = slice(h * CHUNK, (h + 1) * CHUNK)
                mixed = jnp.dot(ws_ref[h], vnb[rs, cs], preferred_element_type=F32) + bs_ref[h]
                y_ref[rs, cs] = (gu[rs, cs] * mixed).astype(BF16)
        zh = jnp.where(i > 0, zh_ref[...], jnp.zeros_like(zh_ref[...])).astype(F32)
        xb = jnp.concatenate([zh[:, 0:DA], z_t[:, 2 * DA:3 * DA]], axis=0)
        gb = jnp.concatenate([zh[:, DA:2 * DA], z_t[:, 3 * DA:4 * DA]], axis=0)
        _rolled_copies(rolled, xb * _sigmoid(gb), False)
        _conv31(rolled, cw_ref, ts, yb2_ref, cb_ref[...])
        nb_, _, _ = _ln_fwd(yb2_ref[...], lgb_ref[...], lbb_ref[...])
        y_ref[:, DA:2 * DA] = (nb_ * _sigmoid(nb_)).astype(BF16)

    row = lambda i: (0, 0)
    return pl.pallas_call(
        body, grid=(s // ts,), name=name,
        in_specs=[pl.BlockSpec((ts, 4 * DA), lambda i: (i, 0)),
                  pl.BlockSpec((HALO_B, 2 * DA), lambda i: (jnp.maximum(i * hb - 1, 0), 1)),
                  pl.BlockSpec((1, DA), row), pl.BlockSpec((1, DA), row),
                  pl.BlockSpec((HEADS, CHUNK, CHUNK), lambda i: (0, 0, 0)),
                  pl.BlockSpec((HEADS, CHUNK, 1), lambda i: (0, 0, 0)),
                  pl.BlockSpec((BCONV, DA), row), pl.BlockSpec((1, DA), row),
                  pl.BlockSpec((1, DA), row), pl.BlockSpec((1, DA), row)],
        out_specs=[pl.BlockSpec((ts, 2 * DA), lambda i: (i, 0)), pl.BlockSpec((ts, DA), lambda i: (i, 0))],
        out_shape=[jax.ShapeDtypeStruct((s, 2 * DA), BF16), jax.ShapeDtypeStruct((s, DA), F32)],
        scratch_shapes=[pltpu.VMEM((8, ts + HALO_B, DA), F32)],
        compiler_params=_params("parallel"),
    )(z, z, lga, lba, wsm, bs_col, cwb, cbb, lgb, lbb)


def _c_fwd(zc, cw, name):
    s = zc.shape[0]
    ts = _tile(s, 512)
    hb = ts // HALO

    def body(z_ref, ch_ref, xh_ref, cw_ref, r_ref):
        i = pl.program_id(0)
        z_t = z_ref[...].astype(F32)
        ph = jnp.where(i > 0, ch_ref[...].astype(F32) * xh_ref[...].astype(F32), 0.0)
        pe = jnp.concatenate([ph, z_t[:, D:2 * D] * z_t[:, 2 * D:3 * D]], axis=0)
        q, _ = _conv3(pe, [cw_ref[k:k + 1, :] for k in range(3)], HALO)
        r_ref[...] = (z_t[:, 0:D] * q).astype(BF16)

    halo = lambda col: pl.BlockSpec((HALO, D), lambda i: (jnp.maximum(i * hb - 1, 0), col))
    return pl.pallas_call(
        body, grid=(s // ts,), name=name,
        in_specs=[pl.BlockSpec((ts, 3 * D), lambda i: (i, 0)), halo(1), halo(2),
                  pl.BlockSpec((3, D), lambda i: (0, 0))],
        out_specs=pl.BlockSpec((ts, D), lambda i: (i, 0)),
        out_shape=jax.ShapeDtypeStruct((s, D), BF16),
        compiler_params=_params("parallel"),
    )(zc, zc, zc, cw)


def _ffn_fwd(h, xres, wup, fcw, wdn, gnext, name):
    s = h.shape[0]
    ts = _tile(s, 512)
    hb = ts // HALO

    def body(h_ref, hh_ref, w_ref, cw_ref, wd_ref, x_ref, *rest):
        if gnext is not None:
            gn_ref, up_ref, upc_ref, xo_ref, hn_ref, up_s = rest
        else:
            up_ref, upc_ref, xo_ref, up_s = rest
        i = pl.program_id(0)
        m = pl.program_id(1)
        @pl.when(m == 0)
        def _():
            xo_ref[...] = x_ref[...]

        halo = jnp.where(i > 0, hh_ref[...], jnp.zeros_like(hh_ref[...]))
        hx = jnp.concatenate([halo, h_ref[...]], axis=0)
        acts = []
        for gv in range(2):
            up_s[gv] = jnp.dot(hx, w_ref[gv], preferred_element_type=F32)
            x0 = up_s[gv, HALO:HALO + ts, :]
            up_ref[gv] = x0.astype(BF16)
            upc = (cw_ref[gv, 2:3, :] * x0 + cw_ref[gv, 1:2, :] * up_s[gv, HALO - 1:HALO - 1 + ts, :]
                   + cw_ref[gv, 0:1, :] * up_s[gv, HALO - 2:HALO - 2 + ts, :])
            upc_ref[gv] = upc.astype(BF16)
            acts.append(upc)
        a = acts[0] * _sigmoid(acts[0]) * acts[1]
        xo_ref[...] += jnp.dot(a.astype(BF16), wd_ref[...], preferred_element_type=F32)

        if gnext is not None:
            @pl.when(m == NG - 1)
            def _():
                hn_ref[...] = _rms_math(xo_ref[...], gn_ref[...])

    tile = pl.BlockSpec((ts, D), lambda i, m: (i, 0))
    nxt = gnext is not None
    return pl.pallas_call(
        body, grid=(s // ts, NG), name=name,
        in_specs=[tile,
                  pl.BlockSpec((HALO, D), lambda i, m: (jnp.maximum(i * hb - 1, 0), 0)),
                  pl.BlockSpec((2, None, D, FB), lambda i, m: (0, m, 0, 0)),
                  pl.BlockSpec((2, None, 3, FB), lambda i, m: (0, m, 0, 0)),
                  pl.BlockSpec((FB, D), lambda i, m: (m, 0)),
                  tile] + ([pl.BlockSpec((1, D), lambda i, m: (0, 0))] if nxt else []),
        out_specs=[pl.BlockSpec((None, 2, ts, FB), lambda i, m: (m, 0, i, 0)),
                   pl.BlockSpec((None, 2, ts, FB), lambda i, m: (m, 0, i, 0)),
                   tile] + ([tile] if nxt else []),
        out_shape=[jax.ShapeDtypeStruct((NG, 2, s, FB), BF16), jax.ShapeDtypeStruct((NG, 2, s, FB), BF16),
                   jax.ShapeDtypeStruct((s, D), F32)] + ([jax.ShapeDtypeStruct((s, D), BF16)] if nxt else []),
        scratch_shapes=[pltpu.VMEM((2, ts + HALO, FB), F32)],
        compiler_params=_params("arbitrary", "arbitrary"),
    )(h, h, wup, fcw, wdn, xres, *([gnext] if nxt else []))


def _final(x, tgt, g, name):
    s = x.shape[0]
    ts = _tile(s, 512)

    def body(x_ref, t_ref, g_ref, dx_ref, dxb_ref, dg_ref, loss_ref):
        i = pl.program_id(0)
        xv = x_ref[...]
        gv = g_ref[...]
        r = lax.rsqrt(jnp.mean(xv * xv, axis=-1, keepdims=True) + EPS)
        xhat = xv * r
        e = xhat * gv - t_ref[...]
        part = 0.5 * jnp.sum(jnp.mean(e * e, axis=-1, keepdims=True), axis=0, keepdims=True)
        dy = e * (1.0 / D)
        dgp = jnp.sum(dy * xhat, axis=0, keepdims=True)
        u = dy * gv
        dx = r * (u - xhat * jnp.mean(u * xhat, axis=-1, keepdims=True))
        dx_ref[...] = dx
        dxb_ref[...] = dx.astype(BF16)

        @pl.when(i == 0)
        def _():
            dg_ref[...] = dgp
            loss_ref[...] = jnp.broadcast_to(part, (1, 128))

        @pl.when(i > 0)
        def _():
            dg_ref[...] += dgp
            loss_ref[...] += jnp.broadcast_to(part, (1, 128))

    return pl.pallas_call(
        body, grid=(s // ts,), name=name,
        in_specs=[pl.BlockSpec((ts, D), lambda i: (i, 0)), pl.BlockSpec((ts, D), lambda i: (i, 0)),
                  pl.BlockSpec((1, D), lambda i: (0, 0))],
        out_specs=[pl.BlockSpec((ts, D), lambda i: (i, 0)), pl.BlockSpec((ts, D), lambda i: (i, 0)),
                   pl.BlockSpec((1, D), lambda i: (0, 0)), pl.BlockSpec((1, 128), lambda i: (0, 0))],
        out_shape=[jax.ShapeDtypeStruct((s, D), F32), jax.ShapeDtypeStruct((s, D), BF16),
                   jax.ShapeDtypeStruct((1, D), F32), jax.ShapeDtypeStruct((1, 128), F32)],
        compiler_params=_params("arbitrary"),
    )(x, tgt, g)


def _ffn_bwd(df, up, upc, wup, fcw, wdn, xin, g, name):
    s = df.shape[0]
    ts = _tile(s, 512)
    nt = s // ts

    def body(df_ref, up_ref, upc_ref, w_ref, cw_ref, wd_ref, x_ref, g_ref,
             a_ref, dup_ref, dx_ref, dxb_ref, dg_ref, dcw_ref, carry, acc, tacc, dcs_ref):
        i = pl.program_id(0)
        m = pl.program_id(1)
        first = i == 0
        @pl.when(first)
        def _():
            carry[m] = jnp.zeros((2, 8, FB), F32)
            dcw_ref[m] = jnp.zeros((2, 3, FB), F32)

        @pl.when(m == 0)
        def _():
            acc[...] = jnp.zeros((ts, D), F32)

        cws = [[cw_ref[gv, k:k + 1, :] for k in range(3)] for gv in range(2)]
        part = ts // NPART
        das = [lax.dot_general(df_ref[p * part:(p + 1) * part, :].astype(BF16), wd_ref[...], NT_DIMS,
                               preferred_element_type=F32) for p in range(NPART)]

        tacc[...] = jnp.zeros((2, 3, 8, FB), F32)
        dcs_ref[:, ts:ts + 8, :] = carry[m]
        for r in reversed(range(ts // RC)):
            rs = slice(r * RC, (r + 1) * RC)
            gate = upc_ref[0, rs, :].astype(F32)
            val = upc_ref[1, rs, :].astype(F32)
            sg = _sigmoid(gate)
            sl = gate * sg
            a_ref[rs, :] = (sl * val).astype(BF16)
            da_c = das[(r * RC) // part][(r * RC) % part:(r * RC) % part + RC]
            dcs = [da_c * val * _dsilu(gate, sg), da_c * sl]
            for gv in range(2):
                dc = dcs[gv]
                dcs_ref[gv, rs, :] = dc
                d1 = dcs_ref[gv, r * RC + 1:(r + 1) * RC + 1, :]
                d2 = dcs_ref[gv, r * RC + 2:(r + 1) * RC + 2, :]
                du = cws[gv][2] * dc + cws[gv][1] * d1 + cws[gv][0] * d2
                dup_ref[gv, rs, :] = du.astype(BF16)
                x0 = up_ref[gv, rs, :].astype(F32)
                for k, dk in enumerate((d2, d1, dc)):
                    p = x0 * dk
                    tacc[gv, k] += sum(p[j:j + 8] for j in range(0, RC, 8))
            if (r * RC) % part == 0:
                ps = slice(r * RC, r * RC + part)
                acc[ps, :] += (
                    lax.dot_general(dup_ref[0, ps, :], w_ref[0], NT_DIMS, preferred_element_type=F32)
                    + lax.dot_general(dup_ref[1, ps, :], w_ref[1], NT_DIMS, preferred_element_type=F32))
        for gv in range(2):
            carry[m, gv] = dcs_ref[gv, 0:8, :]
            for k in range(3):
                dcw_ref[m, gv, k:k + 1, :] += jnp.sum(tacc[gv, k], axis=0, keepdims=True)

        @pl.when(m == NG - 1)
        def _():
            dx, dgp = _rms_bwd_math(acc[...], x_ref[...], g_ref[...])
            dx = df_ref[...] + dx
            dx_ref[...] = dx
            dxb_ref[...] = dx.astype(BF16)

            @pl.when(first)
            def _():
                dg_ref[...] = dgp

            @pl.when(jnp.logical_not(first))
            def _():
                dg_ref[...] += dgp

    rev = lambda i: nt - 1 - i
    return pl.pallas_call(
        body, grid=(nt, NG), name=name,
        in_specs=[pl.BlockSpec((ts, D), lambda i, m: (rev(i), 0)),
                  pl.BlockSpec((None, 2, ts, FB), lambda i, m: (m, 0, rev(i), 0)),
                  pl.BlockSpec((None, 2, ts, FB), lambda i, m: (m, 0, rev(i), 0)),
                  pl.BlockSpec((2, None, D, FB), lambda i, m: (0, m, 0, 0)),
                  pl.BlockSpec((2, None, 3, FB), lambda i, m: (0, m, 0, 0)),
                  pl.BlockSpec((FB, D), lambda i, m: (m, 0)),
                  pl.BlockSpec((ts, D), lambda i, m: (rev(i), 0)),
                  pl.BlockSpec((1, D), lambda i, m: (0, 0))],
        out_specs=[pl.BlockSpec((None, ts, FB), lambda i, m: (m, rev(i), 0)),
                   pl.BlockSpec((None, 2, ts, FB), lambda i, m: (m, 0, rev(i), 0)),
                   pl.BlockSpec((ts, D), lambda i, m: (rev(i), 0)),
                   pl.BlockSpec((ts, D), lambda i, m: (rev(i), 0)),
                   pl.BlockSpec((1, D), lambda i, m: (0, 0)),
                   pl.BlockSpec((NG, 2, 3, FB), lambda i, m: (0, 0, 0, 0))],
        out_shape=[jax.ShapeDtypeStruct((NG, s, FB), BF16), jax.ShapeDtypeStruct((NG, 2, s, FB), BF16),
                   jax.ShapeDtypeStruct((s, D), F32), jax.ShapeDtypeStruct((s, D), BF16),
                   jax.ShapeDtypeStruct((1, D), F32),
                   jax.ShapeDtypeStruct((NG, 2, 3, FB), F32)],
        scratch_shapes=[pltpu.VMEM((NG, 2, 8, FB), F32), pltpu.VMEM((ts, D), F32),
                        pltpu.VMEM((2, 3, 8, FB), F32), pltpu.VMEM((2, ts + 8, FB), F32)],
        compiler_params=_params("arbitrary", "arbitrary"),
    )(df, up, upc, wup, fcw, wdn, xin, g)


def _mm_nt(dy, w, name):
    s = dy.shape[0]
    ts = _tile(s, 1024)

    def body(dy_ref, w_ref, o_ref):
        o_ref[...] = lax.dot_general(dy_ref[...], w_ref[...], NT_DIMS,
                                     preferred_element_type=F32).astype(BF16)

    return pl.pallas_call(
        body, grid=(s // ts,), name=name,
        in_specs=[pl.BlockSpec((ts, D), lambda i: (i, 0)), pl.BlockSpec((D, D), lambda i: (0, 0))],
        out_specs=pl.BlockSpec((ts, D), lambda i: (i, 0)),
        out_shape=jax.ShapeDtypeStruct((s, D), BF16),
        compiler_params=_params("parallel"),
    )(dy, w)


def _mm_nt_rms(dy, wblk, x, g, dres, bf16_copy, name):
    s = dy.shape[0]
    nb, _, bn = wblk.shape
    pair = _pair(bn)
    ts = _tile(s, 512)

    def body(dy_ref, w_ref, x_ref, g_ref, dr_ref, dx_ref, *rest):
        dg_ref = rest[-1]
        i = pl.program_id(0)
        acc = jnp.zeros((ts, D), F32)
        for b in range(0, nb, pair):
            acc = acc + lax.dot_general(dy_ref[:, b * bn:(b + pair) * bn], _cols(w_ref, b, pair), NT_DIMS,
                                        preferred_element_type=F32)
        dx, dgp = _rms_bwd_math(acc, x_ref[...], g_ref[...])
        dx = dr_ref[...] + dx
        dx_ref[...] = dx
        if bf16_copy:
            rest[0][...] = dx.astype(BF16)

        @pl.when(i == 0)
        def _():
            dg_ref[...] = dgp

        @pl.when(i > 0)
        def _():
            dg_ref[...] += dgp

    tile = pl.BlockSpec((ts, D), lambda i: (i, 0))
    return pl.pallas_call(
        body, grid=(s // ts,), name=name,
        in_specs=[pl.BlockSpec((ts, nb * bn), lambda i: (i, 0)), pl.BlockSpec((nb, D, bn), lambda i: (0, 0, 0)),
                  tile, pl.BlockSpec((1, D), lambda i: (0, 0)), tile],
        out_specs=[tile] + ([tile] if bf16_copy else []) + [pl.BlockSpec((1, D), lambda i: (0, 0))],
        out_shape=[jax.ShapeDtypeStruct((s, D), F32)] + ([jax.ShapeDtypeStruct((s, D), BF16)] if bf16_copy else [])
        + [jax.ShapeDtypeStruct((1, D), F32)],
        compiler_params=_params("arbitrary"),
    )(dy, wblk, x, g, dres)


def _c_bwd(dr, zc, cw, name):
    s = dr.shape[0]
    ts = _tile(s, 512)
    nt = s // ts
    hb = ts // HALO

    def body(dr_ref, drf_ref, z_ref, ch_ref, xh_ref, bf_ref, cw_ref, dz_ref, dcw_ref):
        i = pl.program_id(0)
        cwv = [cw_ref[k:k + 1, :] for k in range(3)]
        z_t = z_ref[...].astype(F32)
        bg, cg, xv = z_t[:, 0:D], z_t[:, D:2 * D], z_t[:, 2 * D:3 * D]
        ph = jnp.where(i > 0, ch_ref[...].astype(F32) * xh_ref[...].astype(F32), 0.0)
        pe = jnp.concatenate([ph, cg * xv], axis=0)
        q, taps = _conv3(pe, cwv, HALO)
        drv = dr_ref[...].astype(F32)
        dq = drv * bg
        dqf = jnp.where(i < nt - 1, drf_ref[...].astype(F32) * bf_ref[...].astype(F32), 0.0)
        dp = _conv3_bwd_in(jnp.concatenate([dq, dqf], axis=0), cwv, ts)
        dz_ref[:, 0:D] = (drv * q).astype(BF16)
        dz_ref[:, D:2 * D] = (dp * xv).astype(BF16)
        dz_ref[:, 2 * D:3 * D] = (dp * cg).astype(BF16)
        rows = _conv3_bwd_w(dq, taps)

        @pl.when(i == 0)
        def _():
            for k in range(3):
                dcw_ref[k:k + 1, :] = rows[k]

        @pl.when(i > 0)
        def _():
            for k in range(3):
                dcw_ref[k:k + 1, :] += rows[k]

    past = lambda col: pl.BlockSpec((HALO, D), lambda i: (jnp.maximum(i * hb - 1, 0), col))
    nxt = lambda i: jnp.minimum((i + 1) * hb, s // HALO - 1)
    return pl.pallas_call(
        body, grid=(nt,), name=name,
        in_specs=[pl.BlockSpec((ts, D), lambda i: (i, 0)),
                  pl.BlockSpec((HALO, D), lambda i: (nxt(i), 0)),
                  pl.BlockSpec((ts, 3 * D), lambda i: (i, 0)), past(1), past(2),
                  pl.BlockSpec((HALO, D), lambda i: (nxt(i), 0)),
                  pl.BlockSpec((3, D), lambda i: (0, 0))],
        out_specs=[pl.BlockSpec((ts, 3 * D), lambda i: (i, 0)), pl.BlockSpec((3, D), lambda i: (0, 0))],
        out_shape=[jax.ShapeDtypeStruct((s, 3 * D), BF16), jax.ShapeDtypeStruct((3, D), F32)],
        compiler_params=_params("arbitrary"),
    )(dr, dr, zc, zc, zc, zc, cw)


G512_ROWS = 40


def _ab_bwd(dy, z, yb2, lga, lba, wsm, bs_col, cwb, lgb, lbb, name):
    s = z.shape[0]
    ts = _tile(s, 512)
    nt = s // ts
    hb = ts // HALO_B
    nch = ts // CHUNK

    def body(z_ref, zh_ref, dy_ref, dyf_ref, yb2_ref, yb2f_ref, lga_ref, lba_ref, ws_ref, bs_ref,
             cw_ref, lgb_ref, lbb_ref, dz_ref, g512_ref, dws_ref, dbs_ref, dvn_ref, fwd_rolled, bwd_rolled, du_s):
        i = pl.program_id(0)
        last = i == nt - 1

        @pl.when(i == 0)
        def _():
            g512_ref[...] = jnp.zeros((G512_ROWS, DA), F32)
            dws_ref[...] = jnp.zeros((HEADS, CHUNK, CHUNK), F32)
            dbs_ref[...] = jnp.zeros((HEADS, CHUNK, 1), F32)

        def add_row(k, v):
            g512_ref[k:k + 1, :] += v

        z_t = z_ref[...].astype(F32)
        dy_t = dy_ref[...].astype(F32)
        ua, va = z_t[:, 0:DA], z_t[:, DA:2 * DA]
        gu = _gelu(ua)
        gv = _gelu(va)
        lga_v = lga_ref[...]
        vn, xhat_a, rstd_a = _ln_fwd(gv, lga_v, lba_ref[...])
        vnb = vn.astype(BF16)
        causal = (lax.broadcasted_iota(jnp.int32, (CHUNK, CHUNK), 0)
                  >= lax.broadcasted_iota(jnp.int32, (CHUNK, CHUNK), 1)).astype(F32)
        for c in range(nch):
            for h in range(HEADS):
                rs = slice(c * CHUNK, (c + 1) * CHUNK)
                cs = slice(h * CHUNK, (h + 1) * CHUNK)
                vblk = vnb[rs, cs]
                mixed = jnp.dot(ws_ref[h], vblk, preferred_element_type=F32) + bs_ref[h]
                dyb_ = dy_t[rs, cs]
                dmix = dyb_ * gu[rs, cs]
                dmb = dmix.astype(BF16)
                dz_ref[rs, cs] = (dyb_ * mixed * _dgelu(ua[rs, cs])).astype(BF16)
                dvn_ref[rs, cs] = lax.dot_general(ws_ref[h], dmb, TN_DIMS, preferred_element_type=F32)
                dws_ref[h] += causal * lax.dot_general(dmb, vblk, NT_DIMS, preferred_element_type=F32)
                dbs_ref[h] += jnp.sum(dmix, axis=1, keepdims=True)
        dvn = dvn_ref[...]
        add_row(0, jnp.sum(dvn * xhat_a, axis=0, keepdims=True))
        add_row(1, jnp.sum(dvn, axis=0, keepdims=True))
        dgv = _ln_bwd(dvn, xhat_a, rstd_a, lga_v)
        dz_ref[:, DA:2 * DA] = (dgv * _dgelu(va)).astype(BF16)
        lgb_v = lgb_ref[...]
        dyb_e = jnp.concatenate(
            [dy_t[:, DA:2 * DA], jnp.where(last, 0.0, dyf_ref[...].astype(F32))], axis=0)
        yb2_e = jnp.concatenate([yb2_ref[...], jnp.where(last, 0.0, yb2f_ref[...])], axis=0)
        n_e, xhat_b, rstd_b = _ln_fwd(yb2_e, lgb_v, lbb_ref[...])
        sgn = _sigmoid(n_e)
        dn = dyb_e * _dsilu(n_e, sgn)
        dy2 = _ln_bwd(dn, xhat_b, rstd_b, lgb_v)
        add_row(2, jnp.sum(dy2[:ts], axis=0, keepdims=True))
        add_row(3, jnp.sum(dn[:ts] * xhat_b[:ts], axis=0, keepdims=True))
        add_row(4, jnp.sum(dn[:ts], axis=0, keepdims=True))
        zh = jnp.where(i > 0, zh_ref[...], jnp.zeros_like(zh_ref[...])).astype(F32)
        xb_t, gb_t = z_t[:, 2 * DA:3 * DA], z_t[:, 3 * DA:4 * DA]
        sgb = _sigmoid(gb_t)
        _rolled_copies(fwd_rolled, jnp.concatenate(
            [zh[:, 0:DA] * _sigmoid(zh[:, DA:2 * DA]), xb_t * sgb], axis=0), False)
        _rolled_copies(bwd_rolled, dy2, True)
        for o in range(0, ts, CONV_ROWS):
            acc = jnp.zeros((CONV_ROWS, DA), F32)
            for sh in range(BCONV):
                q, r = divmod(sh, 8)
                acc = acc + cw_ref[BCONV - 1 - sh:BCONV - sh, :] * bwd_rolled[r, 8 * q + o:8 * q + o + CONV_ROWS, :]
            du_s[o:o + CONV_ROWS, :] = acc
        for sh in range(BCONV):
            q, r = divmod(sh, 8)
            acc = jnp.zeros((CONV_ROWS, DA), F32)
            for o in range(0, ts, CONV_ROWS):
                lo = HALO_B - 8 * q + o
                acc = acc + bwd_rolled[0, o:o + CONV_ROWS, :] * fwd_rolled[r, lo:lo + CONV_ROWS, :]
            add_row(8 + BCONV - 1 - sh, jnp.sum(acc, axis=0, keepdims=True))
        du = du_s[...]
        dz_ref[:, 2 * DA:3 * DA] = (du * sgb).astype(BF16)
        dz_ref[:, 3 * DA:4 * DA] = (du * xb_t * sgb * (1.0 - sgb)).astype(BF16)

    row = lambda i: (0, 0)
    nxt = lambda i: jnp.minimum((i + 1) * hb, s // HALO_B - 1)
    return pl.pallas_call(
        body, grid=(nt,), name=name,
        in_specs=[pl.BlockSpec((ts, 4 * DA), lambda i: (i, 0)),
                  pl.BlockSpec((HALO_B, 2 * DA), lambda i: (jnp.maximum(i * hb - 1, 0), 1)),
                  pl.BlockSpec((ts, 2 * DA), lambda i: (i, 0)),
                  pl.BlockSpec((HALO_B, DA), lambda i: (nxt(i), 1)),
                  pl.BlockSpec((ts, DA), lambda i: (i, 0)),
                  pl.BlockSpec((HALO_B, DA), lambda i: (nxt(i), 0)),
                  pl.BlockSpec((1, DA), row), pl.BlockSpec((1, DA), row),
                  pl.BlockSpec((HEADS, CHUNK, CHUNK), lambda i: (0, 0, 0)),
                  pl.BlockSpec((HEADS, CHUNK, 1), lambda i: (0, 0, 0)),
                  pl.BlockSpec((BCONV, DA), row), pl.BlockSpec((1, DA), row), pl.BlockSpec((1, DA), row)],
        out_specs=[pl.BlockSpec((ts, 4 * DA), lambda i: (i, 0)),
                   pl.BlockSpec((G512_ROWS, DA), row),
                   pl.BlockSpec((HEADS, CHUNK, CHUNK), lambda i: (0, 0, 0)),
                   pl.BlockSpec((HEADS, CHUNK, 1), lambda i: (0, 0, 0))],
        out_shape=[jax.ShapeDtypeStruct((s, 4 * DA), BF16), jax.ShapeDtypeStruct((G512_ROWS, DA), F32),
                   jax.ShapeDtypeStruct((HEADS, CHUNK, CHUNK), F32),
                   jax.ShapeDtypeStruct((HEADS, CHUNK, 1), F32)],
        scratch_shapes=[pltpu.VMEM((ts, DA), F32), pltpu.VMEM((8, ts + HALO_B, DA), F32),
                        pltpu.VMEM((8, ts + HALO_B, DA), F32), pltpu.VMEM((ts, DA), F32)],
        compiler_params=_params("arbitrary"),
    )(z, z, dy, dy, yb2, yb2, lga, lba, wsm, bs_col, cwb, lgb, lbb)


def _dw_cols(a, dy, nb, bn, name):
    s = a.shape[0]
    tm = _tile(s, 2048)
    nt = s // tm
    cpb = 4

    def body(a_ref, dy_ref, o_ref, acc):
        t = pl.program_id(1)
        p = lax.dot_general(a_ref[...], dy_ref[...], TN_DIMS, preferred_element_type=F32)

        @pl.when(t == 0)
        def _():
            for q in range(cpb):
                acc[q] = p[:, q * bn:(q + 1) * bn]

        @pl.when(t > 0)
        def _():
            for q in range(cpb):
                acc[q] += p[:, q * bn:(q + 1) * bn]

        @pl.when(t == nt - 1)
        def _():
            o_ref[...] = acc[...].astype(BF16)

    return pl.pallas_call(
        body, grid=(nb // cpb, nt), name=name,
        in_specs=[pl.BlockSpec((tm, D), lambda j, t: (t, 0)), pl.BlockSpec((tm, cpb * bn), lambda j, t: (t, j))],
        out_specs=pl.BlockSpec((cpb, D, bn), lambda j, t: (j, 0, 0)),
        out_shape=jax.ShapeDtypeStruct((nb, D, bn), BF16),
        scratch_shapes=[pltpu.VMEM((cpb, D, bn), F32)],
        compiler_params=_params("arbitrary", "arbitrary"),
    )(a, dy)


def _dw_rows(a, dy, name):
    s = a.shape[0]
    tm = _tile(s, 4096)
    nt = s // tm
    rb = 512

    def body(a_ref, dy_ref, o_ref, acc):
        t = pl.program_id(1)
        p = lax.dot_general(a_ref[...], dy_ref[...], TN_DIMS, preferred_element_type=F32)

        @pl.when(t == 0)
        def _():
            acc[...] = p

        @pl.when(t > 0)
        def _():
            acc[...] += p

        @pl.when(t == nt - 1)
        def _():
            o_ref[...] = acc[...].astype(BF16)

    return pl.pallas_call(
        body, grid=(D // rb, nt), name=name,
        in_specs=[pl.BlockSpec((tm, rb), lambda j, t: (t, j)), pl.BlockSpec((tm, D), lambda j, t: (t, 0))],
        out_specs=pl.BlockSpec((rb, D), lambda j, t: (j, 0)),
        out_shape=jax.ShapeDtypeStruct((D, D), BF16),
        scratch_shapes=[pltpu.VMEM((rb, D), F32)],
        compiler_params=_params("arbitrary", "arbitrary"),
    )(a, dy)


def _dw_up(h, dup, name):
    s = h.shape[0]
    tm = _tile(s, 4096)
    nt = s // tm

    def body(h_ref, d_ref, o_ref, acc):
        t = pl.program_id(1)
        p = lax.dot_general(d_ref[...], h_ref[...], TN_DIMS, preferred_element_type=F32)

        @pl.when(t == 0)
        def _():
            acc[...] = p

        @pl.when(t > 0)
        def _():
            acc[...] += p

        @pl.when(t == nt - 1)
        def _():
            o_ref[...] = acc[...].astype(BF16)

    return pl.pallas_call(
        body, grid=(NDEV, nt), name=name,
        in_specs=[pl.BlockSpec((tm, D), lambda b, t: (t, 0)),
                  pl.BlockSpec((None, None, tm, FB), lambda b, t: (b % NG, b // NG, t, 0))],
        out_specs=pl.BlockSpec((None, FB, D), lambda b, t: (b, 0, 0)),
        out_shape=jax.ShapeDtypeStruct((NDEV, FB, D), BF16),
        scratch_shapes=[pltpu.VMEM((FB, D), F32)],
        compiler_params=_params("arbitrary", "arbitrary"),
    )(h, dup)


def _dw_dn(a, df, name):
    s = df.shape[0]
    tm = _tile(s, 4096)
    nt = s // tm

    def body(a_ref, d_ref, o_ref, acc):
        t = pl.program_id(1)
        p = lax.dot_general(a_ref[...], d_ref[...], TN_DIMS, preferred_element_type=F32)

        @pl.when(t == 0)
        def _():
            acc[...] = p

        @pl.when(t > 0)
        def _():
            acc[...] += p

        @pl.when(t == nt - 1)
        def _():
            o_ref[...] = acc[...].astype(BF16)

    return pl.pallas_call(
        body, grid=(NG, nt), name=name,
        in_specs=[pl.BlockSpec((None, tm, FB), lambda m, t: (m, t, 0)), pl.BlockSpec((tm, D), lambda m, t: (t, 0))],
        out_specs=pl.BlockSpec((FB, D), lambda m, t: (m, 0)),
        out_shape=jax.ShapeDtypeStruct((DFF, D), BF16),
        scratch_shapes=[pltpu.VMEM((FB, D), F32)],
        compiler_params=_params("arbitrary", "arbitrary"),
    )(a, df)


def _place():
    x, y, c = lax.axis_index("x"), lax.axis_index("y"), lax.axis_index("c")
    chips = [(1 - x, y), (x, 1 - y), (1 - x, 1 - y)]
    return x, y, c, chips


def _zone(shard, dev):
    return lax.dynamic_update_slice(lax.empty((NDEV,) + shard.shape, shard.dtype), shard[None],
                                    (dev,) + (0,) * shard.ndim)


HBM_SPEC = pl.BlockSpec(memory_space=pltpu.HBM)
SEM_SPEC = pl.BlockSpec(memory_space=pltpu.SEMAPHORE)
DATAFLOW = pltpu.SideEffectType.DATAFLOW_SIDE_EFFECTING


def _hbm(a):
    return pltpu.with_memory_space_constraint(a, pltpu.HBM)


def _hbm_like(arrs):
    return [pltpu.HBM(a.shape, a.dtype) for a in arrs]


def _ag_start(srcs, lands, after, name):
    n = len(srcs)
    ns = 8 * n

    def body(*refs):
        src, land = refs[:n], refs[n:2 * n]
        sems = refs[2 * n + 1:2 * n + 1 + ns]
        token = refs[-1]
        x, y, c, chips = _place()
        peers = [(x, y, 1 - c)] + [(*chip, c) for chip in chips]
        for t in range(n):
            for k, to in enumerate(peers):
                pltpu.make_async_remote_copy(
                    src_ref=src[t], dst_ref=land[t].at[4 * x + 2 * y + c],
                    send_sem=sems[2 * (4 * t + k)], recv_sem=sems[2 * (4 * t + k) + 1],
                    device_id=to, device_id_type=MESH).start()
        token[...] = jnp.zeros_like(token)

    res = pl.pallas_call(
        body, name=name,
        in_specs=[HBM_SPEC] * (2 * n) + [ANY],
        out_specs=[SEM_SPEC] * ns + [HBM_SPEC] * (2 * n) + [pl.BlockSpec(memory_space=pltpu.VMEM)],
        out_shape=[pltpu.SemaphoreType.DMA(())] * ns + _hbm_like(srcs) + _hbm_like(lands)
        + [jax.ShapeDtypeStruct((8, 128), F32)],
        input_output_aliases={i: ns + i for i in range(2 * n)},
        compiler_params=pltpu.CompilerParams(has_side_effects=DATAFLOW),
    )(*[_hbm(a) for a in srcs], *[_hbm(a) for a in lands], after)
    sems = [[(res[2 * (4 * t + k)], res[2 * (4 * t + k) + 1]) for k in range(4)] for t in range(n)]
    return sems, res[ns:ns + n], res[ns + n:ns + 2 * n], res[-1]


def _ag_forward(srcs, lands, sems1, after, name):
    n = len(srcs)
    flat1 = [s for t in range(n) for k in range(1, 4) for s in sems1[t][k]]
    n1 = len(flat1)

    def body(*refs):
        src, land = refs[:n], refs[n:2 * n]
        s1 = refs[2 * n:2 * n + n1]
        s2 = refs[2 * n + n1 + 1:2 * n + n1 + 1 + 6 * n]
        x, y, c, chips = _place()
        for j, (cx, cy) in enumerate(chips):
            for t in range(n):
                blk = land[t].at[4 * cx + 2 * cy + c]
                pltpu.make_async_remote_copy(
                    src_ref=src[t], dst_ref=blk, send_sem=s1[2 * (3 * t + j)], recv_sem=s1[2 * (3 * t + j) + 1],
                    device_id=(cx, cy, c), device_id_type=MESH).wait_recv()
                pltpu.make_async_remote_copy(
                    src_ref=blk, dst_ref=blk, send_sem=s2[2 * (3 * t + j)], recv_sem=s2[2 * (3 * t + j) + 1],
                    device_id=(x, y, 1 - c), device_id_type=MESH).start()

    res = pl.pallas_call(
        body, name=name,
        in_specs=[HBM_SPEC] * (2 * n) + [SEM_SPEC] * n1 + [ANY],
        out_specs=[SEM_SPEC] * (6 * n) + [HBM_SPEC] * n,
        out_shape=[pltpu.SemaphoreType.DMA(())] * (6 * n) + _hbm_like(lands),
        input_output_aliases={n + i: 6 * n + i for i in range(n)},
        compiler_params=pltpu.CompilerParams(has_side_effects=DATAFLOW),
    )(*srcs, *lands, *flat1, after)
    sems2 = [[(res[2 * (3 * t + j)], res[2 * (3 * t + j) + 1]) for j in range(3)] for t in range(n)]
    return sems2, res[6 * n:]


def _ag_finish(srcs, lands, sems1, sems2, after, name):
    n = len(srcs)
    flat1 = [s for t in range(n) for k in range(4) for s in sems1[t][k]]
    flat2 = [s for t in range(n) for j in range(3) for s in sems2[t][j]]
    n1, n2 = len(flat1), len(flat2)

    def body(*refs):
        src, land = refs[:n], refs[n:2 * n]
        s1 = refs[2 * n:2 * n + n1]
        s2 = refs[2 * n + n1:2 * n + n1 + n2]
        x, y, c, chips = _place()
        sib = (x, y, 1 - c)
        for t in range(n):
            own = land[t].at[4 * x + 2 * y + 1 - c]
            pltpu.make_async_remote_copy(
                src_ref=src[t], dst_ref=own, send_sem=s1[8 * t], recv_sem=s1[8 * t + 1],
                device_id=sib, device_id_type=MESH).wait_recv()
            for k in range(4):
                pltpu.make_async_remote_copy(
                    src_ref=src[t], dst_ref=own, send_sem=s1[2 * (4 * t + k)], recv_sem=s1[2 * (4 * t + k) + 1],
                    device_id=sib, device_id_type=MESH).wait_send()
            for j, (cx, cy) in enumerate(chips):
                blk = land[t].at[4 * cx + 2 * cy + 1 - c]
                cp = pltpu.make_async_remote_copy(
                    src_ref=blk, dst_ref=blk, send_sem=s2[2 * (3 * t + j)], recv_sem=s2[2 * (3 * t + j) + 1],
                    device_id=sib, device_id_type=MESH)
                cp.wait_send()
                cp.wait_recv()

    return pl.pallas_call(
        body, name=name,
        in_specs=[HBM_SPEC] * (2 * n) + [SEM_SPEC] * (n1 + n2) + [ANY],
        out_specs=[HBM_SPEC] * n,
        out_shape=_hbm_like(lands),
        input_output_aliases={n + i: i for i in range(n)},
        compiler_params=pltpu.CompilerParams(has_side_effects=DATAFLOW),
    )(*srcs, *lands, *flat1, *flat2, after)


def _pair_copies(srcs, dsts, sems):
    x, y, c, _ = _place()
    nt = len(srcs)
    return [pltpu.make_async_remote_copy(
        src_ref=srcs[t].at[2 * j + 1 - c], dst_ref=dsts[t].at[j],
        send_sem=sems[2 * (NCHIP * t + j)], recv_sem=sems[2 * (NCHIP * t + j) + 1],
        device_id=(x, y, 1 - c), device_id_type=MESH) for t in range(nt) for j in range(NCHIP)]


def _pair_start(grads, carry, name):
    nt = len(grads)
    ns = 2 * NCHIP * nt
    zones = [_hbm(lax.empty((NCHIP,) + a.shape[1:], a.dtype)) for a in grads]
    extra = [] if carry is None else [_hbm(carry)]
    ne = len(extra)

    def body(*refs):
        for cp in _pair_copies(refs[:nt], refs[nt:2 * nt], refs[2 * nt + ne:2 * nt + ne + ns]):
            cp.start()

    res = pl.pallas_call(
        body, name=name,
        in_specs=[HBM_SPEC] * (2 * nt + ne),
        out_specs=[SEM_SPEC] * ns + [HBM_SPEC] * (2 * nt + ne),
        out_shape=[pltpu.SemaphoreType.DMA(())] * ns + _hbm_like(grads) + _hbm_like(zones) + _hbm_like(extra),
        input_output_aliases={i: ns + i for i in range(2 * nt + ne)},
        compiler_params=pltpu.CompilerParams(has_side_effects=DATAFLOW),
    )(*[_hbm(a) for a in grads], *zones, *extra)
    handle = (list(res[:ns]), list(res[ns:ns + nt]), list(res[ns + nt:ns + 2 * nt]))
    return handle, (res[ns + 2 * nt] if ne else None)


def _pair_wait(handle, after, name):
    sems, srcs, zones = handle
    nt, ns = len(srcs), len(sems)

    def body(*refs):
        for cp in _pair_copies(refs[:nt], refs[nt:2 * nt], refs[2 * nt:2 * nt + ns]):
            cp.wait_send()
            cp.wait_recv()

    return pl.pallas_call(
        body, name=name,
        in_specs=[HBM_SPEC] * (2 * nt) + [SEM_SPEC] * ns + [ANY],
        out_specs=[HBM_SPEC] * nt,
        out_shape=_hbm_like(zones),
        input_output_aliases={nt + i: i for i in range(nt)},
        compiler_params=pltpu.CompilerParams(has_side_effects=DATAFLOW),
    )(*srcs, *zones, *sems, after)


def _rows_tile(r, row_bytes, cap_bytes):
    best = None
    for tr in range(16, r + 1, 16):
        if r % tr == 0 and tr * row_bytes <= cap_bytes:
            best = tr
    return best if best is not None else r


def _pair_sum(own, got, cidx, name):
    _, _, r, cdim = own.shape
    tr = _rows_tile(r, 2 * cdim, 2 * 1024 * 1024)

    def body(c_ref, a_ref, b_ref, o_ref):
        o_ref[...] = (a_ref[...].astype(F32) + b_ref[...].astype(F32)).astype(BF16)

    return pl.pallas_call(
        body, name=name,
        grid_spec=pltpu.PrefetchScalarGridSpec(
            num_scalar_prefetch=1, grid=(NCHIP, r // tr),
            in_specs=[pl.BlockSpec((None, None, tr, cdim), lambda j, i, c_ref: (j, c_ref[0], i, 0)),
                      pl.BlockSpec((None, tr, cdim), lambda j, i, c_ref: (j, i, 0))],
            out_specs=pl.BlockSpec((None, tr, cdim), lambda j, i, c_ref: (j, i, 0))),
        out_shape=jax.ShapeDtypeStruct((NCHIP, r, cdim), BF16),
        compiler_params=_params("arbitrary", "arbitrary"),
    )(cidx, own, got)


def _chip_copies(srcs, zones, slots, sems):
    x, y, c, chips = _place()
    out = []
    for t, (z, l) in enumerate(slots):
        for k, (cx, cy) in enumerate(chips):
            dst = zones[z].at[k] if l is None else zones[z].at[k, l]
            out.append(pltpu.make_async_remote_copy(
                src_ref=srcs[t].at[2 * cx + cy], dst_ref=dst,
                send_sem=sems[2 * (3 * t + k)], recv_sem=sems[2 * (3 * t + k) + 1],
                device_id=(cx, cy, c), device_id_type=MESH))
    return out


def _chip_start(sums, zones, slots, carry, name):
    nt, nz = len(sums), len(zones)
    ns = 6 * nt
    extra = [] if carry is None else [_hbm(carry)]
    ne = len(extra)

    def body(*refs):
        for cp in _chip_copies(refs[:nt], refs[nt:nt + nz], slots, refs[nt + nz + ne:nt + nz + ne + ns]):
            cp.start()

    res = pl.pallas_call(
        body, name=name,
        in_specs=[HBM_SPEC] * (nt + nz + ne),
        out_specs=[SEM_SPEC] * ns + [HBM_SPEC] * (nt + nz + ne),
        out_shape=[pltpu.SemaphoreType.DMA(())] * ns + _hbm_like(sums) + _hbm_like(zones) + _hbm_like(extra),
        input_output_aliases={i: ns + i for i in range(nt + nz + ne)},
        compiler_params=pltpu.CompilerParams(has_side_effects=DATAFLOW),
    )(*[_hbm(a) for a in sums], *zones, *extra)
    return (list(res[:ns]), list(res[ns:ns + nt]), list(res[ns + nt:ns + nt + nz]),
            (res[ns + nt + nz] if ne else None))


def _chip_wait(started, zones, zone_ids, after, name):
    started = [(sums, [(zone_ids.index(z), l) for z, l in slots], sems) for sums, slots, sems in started]
    nz = len(zones)
    flat_src = [a for sums, _, _ in started for a in sums]
    flat_sem = [s for _, _, sems in started for s in sems]
    n_src, n_sem = len(flat_src), len(flat_sem)

    def body(*refs):
        srcs, zs, sems = refs[:n_src], refs[n_src:n_src + nz], refs[n_src + nz:n_src + nz + n_sem]
        so, se = 0, 0
        for sums, slots, sem_list in started:
            for cp in _chip_copies(srcs[so:so + len(sums)], zs, slots, sems[se:se + len(sem_list)]):
                cp.wait_send()
                cp.wait_recv()
            so += len(sums)
            se += len(sem_list)

    return pl.pallas_call(
        body, name=name,
        in_specs=[HBM_SPEC] * (n_src + nz) + [SEM_SPEC] * n_sem + [ANY],
        out_specs=[HBM_SPEC] * nz,
        out_shape=_hbm_like(zones),
        input_output_aliases={n_src + i: i for i in range(nz)},
        compiler_params=pltpu.CompilerParams(has_side_effects=DATAFLOW),
    )(*flat_src, *zones, *flat_sem, after)


def _small_allreduce(parts, after, name):
    nt = len(parts)

    def body(*refs):
        srcs, outs, bufs = refs[:nt], refs[nt + 1:2 * nt + 1], refs[2 * nt + 1:3 * nt + 1]
        send_sems, recv_sems = refs[3 * nt + 1:]
        x, y, c, _ = _place()
        peers = [(x, y, 1 - c), (1 - x, y, c), (x, 1 - y, c)]
        for t in range(nt):
            outs[t][...] = srcs[t][...]
        for step, peer in enumerate(peers):
            copies = [pltpu.make_async_remote_copy(
                src_ref=outs[t], dst_ref=bufs[t].at[step],
                send_sem=send_sems.at[step, t], recv_sem=recv_sems.at[step, t],
                device_id=peer, device_id_type=MESH) for t in range(nt)]
            for cp in copies:
                cp.start()
            for cp in copies:
                cp.wait()
            for t in range(nt):
                outs[t][...] = outs[t][...] + bufs[t][step]

    vm = pl.BlockSpec(memory_space=pltpu.VMEM)
    return pl.pallas_call(
        body, name=name,
        in_specs=[vm] * nt + [ANY], out_specs=[vm] * nt,
        out_shape=[jax.ShapeDtypeStruct(a.shape, F32) for a in parts],
        scratch_shapes=[pltpu.VMEM((3,) + a.shape, F32) for a in parts]
        + [pltpu.SemaphoreType.DMA((3, nt)), pltpu.SemaphoreType.DMA((3, nt))],
        compiler_params=pltpu.CompilerParams(has_side_effects=True, vmem_limit_bytes=VMEM_LIMIT),
    )(*parts, after)


def _adam_math(w, g, m, v):
    m2 = ADAM_B1 * m + (1.0 - ADAM_B1) * g
    v2 = ADAM_B2 * v + (1.0 - ADAM_B2) * (g * g)
    m_hat = m2 / (1.0 - ADAM_B1 ** ADAM_STEP)
    v_hat = v2 / (1.0 - ADAM_B2 ** ADAM_STEP)
    delta = -ADAM_LR * (m_hat / (jnp.sqrt(v_hat) + ADAM_EPS) + ADAM_WD * w)
    return delta, m2, v2


def _adam_big(w, m, v, parts, mine, chip, name):
    nl, r, cdim = w.shape
    tr = _rows_tile(r, 4 * cdim, 3 * 512 * 1024)

    def body(c_ref, w_ref, m_ref, v_ref, p_ref, *rest):
        mine_refs, (g_ref, d_ref, mo_ref, vo_ref) = rest[:nl], rest[nl:]
        own = mine_refs[0][...]
        for l in range(1, nl):
            own = jnp.where(pl.program_id(0) == l, mine_refs[l][...], own)
        g = ((p_ref[0].astype(F32) + p_ref[1].astype(F32)) + p_ref[2].astype(F32)) + own.astype(F32)
        delta, m2, v2 = _adam_math(w_ref[...], g, m_ref[...], v_ref[...])
        g_ref[...] = g
        d_ref[...] = delta
        mo_ref[...] = m2
        vo_ref[...] = v2

    spec = pl.BlockSpec((None, tr, cdim), lambda l, i, c_ref: (l, i, 0))
    mine_specs = [pl.BlockSpec((None, tr, cdim), lambda l, i, c_ref, ll=ll: (c_ref[0], jnp.where(l == ll, i, 0), 0))
                  for ll in range(nl)]
    return pl.pallas_call(
        body, name=name,
        grid_spec=pltpu.PrefetchScalarGridSpec(
            num_scalar_prefetch=1, grid=(nl, r // tr),
            in_specs=[spec, spec, spec, pl.BlockSpec((3, None, tr, cdim), lambda l, i, c_ref: (0, l, i, 0))]
            + mine_specs,
            out_specs=[spec] * 4),
        out_shape=[jax.ShapeDtypeStruct(w.shape, F32)] * 4,
        compiler_params=_params("arbitrary", "arbitrary"),
    )(chip, w, m, v, parts, *mine)


def _adam_small(ws, gs, ms, vs, name):
    n = len(ws)

    def body(*refs):
        w_r, g_r, m_r, v_r = refs[:n], refs[n:2 * n], refs[2 * n:3 * n], refs[3 * n:4 * n]
        d_o, m_o, v_o = refs[4 * n:5 * n], refs[5 * n:6 * n], refs[6 * n:7 * n]
        for t in range(n):
            delta, m2, v2 = _adam_math(w_r[t][...], g_r[t][...], m_r[t][...], v_r[t][...])
            d_o[t][...] = delta
            m_o[t][...] = m2
            v_o[t][...] = v2

    vm = pl.BlockSpec(memory_space=pltpu.VMEM)
    shapes = [jax.ShapeDtypeStruct(a.shape, F32) for a in ws]
    return pl.pallas_call(
        body, name=name, in_specs=[vm] * (4 * n), out_specs=[vm] * (3 * n), out_shape=shapes * 3,
        compiler_params=pltpu.CompilerParams(vmem_limit_bytes=VMEM_LIMIT),
    )(*ws, *gs, *ms, *vs)


def kernel(x, norm_mix, norm_ffn, norm_final, ab_w_in, a_ln_g, a_ln_b, a_w_s, a_b_s, b_conv_w, b_conv_b, b_ln_g, b_ln_b, ab_w_out, c_w_in, c_conv_w, c_w_out, f_w_up, f_conv_w, f_w_down, loss_target, m_norm_mix, m_norm_ffn, m_norm_final, m_ab_w_in, m_a_ln_g, m_a_ln_b, m_a_w_s, m_a_b_s, m_b_conv_w, m_b_conv_b, m_b_ln_g, m_b_ln_b, m_ab_w_out, m_c_w_in, m_c_conv_w, m_c_w_out, m_f_w_up, m_f_conv_w, m_f_w_down, v_norm_mix, v_norm_ffn, v_norm_final, v_ab_w_in, v_a_ln_g, v_a_ln_b, v_a_w_s, v_a_b_s, v_b_conv_w, v_b_conv_b, v_b_ln_g, v_b_ln_b, v_ab_w_out, v_c_w_in, v_c_conv_w, v_c_w_out, v_f_w_up, v_f_conv_w, v_f_w_down):
    s = x.shape[1]
    x0 = x.reshape(s, D)
    tgt = loss_target.reshape(s, D)
    xi, yi, ci = lax.axis_index("x"), lax.axis_index("y"), lax.axis_index("c")
    dev = 4 * xi + 2 * yi + ci
    cidx = ci.astype(jnp.int32).reshape(1)

    bf = lambda a: a.astype(BF16)
    slab_w = 6 * CHUNK
    pad = lambda a, rows: jnp.pad(a, ((0, rows - a.shape[0]), (0, slab_w - a.shape[1])))
    slab = jnp.concatenate([pad(b_conv_w[0], 32), pad(c_conv_w[0], 8), pad(f_conv_w.reshape(6, FB), 8)], axis=0)
    later = [bf(ab_w_in[0]), bf(ab_w_out[0]), slab, bf(f_w_up[0]), bf(f_w_down[0]), bf(c_w_in[0]), bf(c_w_out[0]),
             bf(f_w_up[1]), bf(f_w_down[1])]
    lands = [_zone(a, dev) for a in later]
    groups = [[0], [1, 2], [3, 4], [5, 6], [7, 8]]
    ag_sems, later, lands, ag_token = _ag_start(later, lands, x0, "ag_start")

    causal = jnp.tril(jnp.ones((CHUNK, CHUNK), F32))
    wsm = (a_w_s[0] * causal).astype(BF16)
    bs_col = a_b_s.reshape(HEADS, CHUNK, 1)
    nm = [norm_mix[0:1], norm_mix[1:2]]
    nf = [norm_ffn[0:1], norm_ffn[1:2]]
    nfin = norm_final.reshape(1, D)

    def arrive(g, after_ici, after_d2d, tag):
        srcs = [later[t] for t in groups[g]]
        zone = [lands[t] for t in groups[g]]
        sems1 = [ag_sems[t] for t in groups[g]]
        sems2, zone = _ag_forward(srcs, zone, sems1, after_ici, "ag_forward_" + tag)
        return _ag_finish(srcs, zone, sems1, sems2, after_d2d, "ag_finish_" + tag)

    h0 = _rms_fwd(x0, nm[0], "rms_mix0", after=ag_token)
    (win0,) = arrive(0, h0, h0, "w_in")
    z = _mm_in(h0, win0, "mm_ab_in")
    wout0, slab_g = arrive(1, z, z, "first")
    wout0 = wout0.reshape(D, D)
    bcw = jnp.transpose(slab_g[:, 0:BCONV, 0:DA // NDEV], (1, 0, 2)).reshape(BCONV, DA)
    ccw = jnp.transpose(slab_g[:, 32:35, 0:D // NDEV], (1, 0, 2)).reshape(3, D)
    fcw_g = slab_g[:, 40:46, 0:FB].reshape(2, NG, 2, 3, FB)
    fcws = [fcw_g[:, :, 0], fcw_g[:, :, 1]]
    ycat, yb2 = _ab_fwd(z, a_ln_g, a_ln_b, wsm, bs_col, bcw, b_conv_b, b_ln_g, b_ln_b, "ab_fwd")
    x1, h1 = _mm_out(ycat, wout0, x0, nf[0], "mm_ab_out")
    wup0, wdn0 = arrive(2, x1, x1, "ffn0")
    up0, upc0, x2, h2 = _ffn_fwd(h1, x1, wup0.reshape(2, NG, D, FB), fcws[0], wdn0.reshape(DFF, D), nm[1],
                                 "ffn_fwd0")
    cin, cout = arrive(3, x2, x2, "c")
    cout = cout.reshape(D, D)
    zc = _mm_in(h2, cin, "mm_c_in")
    rc = _c_fwd(zc, ccw, "c_fwd")
    x3, h3 = _mm_out(rc, cout, x2, nf[1], "mm_c_out")
    wup1, wdn1 = arrive(4, rc, x3, "ffn1")
    wups = [wup0.reshape(2, NG, D, FB), wup1.reshape(2, NG, D, FB)]
    wdns = [wdn0.reshape(DFF, D), wdn1.reshape(DFF, D)]
    up1, upc1, x4 = _ffn_fwd(h3, x3, wups[1], fcws[1], wdns[1], None, "ffn_fwd1")
    dx4, dx4b, dnfin, loss_part = _final(x4, tgt, nfin, "final_loss")

    zshape = lambda *sh: _hbm(lax.empty((3,) + sh, BF16))
    zones = [zshape(D, 2 * D // NDEV), zshape(D // NDEV, D), zshape(D, 3 * D // NDEV), zshape(D // NDEV, D),
             zshape(2, FB, D), zshape(2, DFF // NDEV, D)]
    started = []

    def pair_sums(grads, handle, after, tag):
        del grads
        got = _pair_wait(handle, after, "rs_pair_wait_" + tag)
        return [_pair_sum(b.reshape((NCHIP, 2) + b.shape[1:]), g, cidx, "rs_pair_sum_%s%d" % (tag, t))
                for t, (b, g) in enumerate(zip(handle[1], got))]

    def chip_start(sums, slots, carry, tag):
        sems, sums, new_zones, carry = _chip_start(sums, zones, slots, carry, "rs_chip_start_" + tag)
        zones[:] = new_zones
        started.append((sums, slots, sems))
        return sums, carry

    rows8 = lambda g, r: g.reshape(NDEV, r, D)
    a1, dup1, dx3, dx3b, dnf1, dfcw1 = _ffn_bwd(dx4, up1, upc1, wups[1], fcws[1], wdns[1], x3, nf[1], "ffn_bwd1")
    g_f1 = [_dw_up(h3, dup1, "dw_up1"), rows8(_dw_dn(a1, dx4b, "dw_dn1"), DFF // NDEV)]
    hd_f1, dx3b = _pair_start(g_f1, dx3b, "rs_pair_start_f1")
    drc = _mm_nt(dx3b, cout, "mm_c_out_bwd")
    g_cout = rows8(_dw_rows(rc, dx3b, "dw_c_out"), D // NDEV)
    s_f1 = pair_sums(g_f1, hd_f1, g_cout, "f1")
    s_f1, drc = chip_start(s_f1, [(4, 1), (5, 1)], drc, "f1")
    dzc, dccw = _c_bwd(drc, zc, ccw, "c_bwd")
    dx2, dx2b, dnm1 = _mm_nt_rms(dzc, cin, x2, nm[1], dx3, True, "mm_c_in_bwd")
    g_c = [_dw_cols(h2, dzc, NDEV, 3 * D // NDEV, "dw_c_in"), g_cout]
    hd_c, dx2 = _pair_start(g_c, dx2, "rs_pair_start_c")
    a0, dup0, dx1, dx1b, dnf0, dfcw0 = _ffn_bwd(dx2, up0, upc0, wups[0], fcws[0], wdns[0], x1, nf[0], "ffn_bwd0")
    s_c = pair_sums(g_c, hd_c, dx1b, "c")
    s_c, dx1b = chip_start(s_c, [(2, None), (3, None)], dx1b, "c")
    g_f0 = [_dw_up(h1, dup0, "dw_up0"), rows8(_dw_dn(a0, dx2b, "dw_dn0"), DFF // NDEV)]
    hd_f0, dx1b = _pair_start(g_f0, dx1b, "rs_pair_start_f0")
    dycat = _mm_nt(dx1b, wout0, "mm_ab_out_bwd")
    g_wout0 = rows8(_dw_rows(ycat, dx1b, "dw_ab_out"), D // NDEV)
    s_f0 = pair_sums(g_f0, hd_f0, g_wout0, "f0")
    s_f0, dycat = chip_start(s_f0, [(4, 0), (5, 0)], dycat, "f0")
    dz, g512, dws, dbs = _ab_bwd(dycat, z, yb2, a_ln_g, a_ln_b, wsm, bs_col, bcw, b_ln_g, b_ln_b, "ab_bwd")
    grad_x, dnm0 = _mm_nt_rms(dz, win0, x0, nm[0], dx1, False, "mm_ab_in_bwd")
    g_ab = [_dw_cols(h0, dz, NDEV, 2 * D // NDEV, "dw_ab_in"), g_wout0]
    hd_ab, _ = _pair_start(g_ab, None, "rs_pair_start_ab")

    g1024 = jnp.concatenate([dnm0, dnm1, dnf0, dnf1, dnfin, dccw], axis=0)
    gfc = jnp.concatenate([dfcw0, dfcw1], axis=0).reshape(2 * NG * 2 * 3, FB)
    g1024, g512, dws, dbs, gfc, loss_sum = _small_allreduce(
        [g1024, g512, dws.reshape(HEADS * CHUNK, CHUNK), dbs.reshape(HEADS, CHUNK), gfc, loss_part], hd_ab[1][0],
        "small_allreduce")
    loss = loss_sum[0, 0]
    s_ab = pair_sums(g_ab, hd_ab, g1024, "ab")
    s_ab, _ = chip_start(s_ab, [(0, None), (1, None)], None, "ab")
    p_cin, p_cout, p_wup, p_wdn = _chip_wait(started[:3], zones[2:], [2, 3, 4, 5], s_ab[0], "rs_chip_wait_early")

    chip = (2 * xi + yi).astype(jnp.int32).reshape(1)

    def big_update(w, m, v, parts, mine, name):
        shp = w.shape
        w3, m3, v3 = (a.reshape((-1,) + shp[-2:]) for a in (w, m, v))
        p4 = parts.reshape((3,) + w3.shape)
        return [o.reshape(shp) for o in _adam_big(w3, m3, v3, p4, mine, chip, name)]

    u_cin = big_update(c_w_in, m_c_w_in, v_c_w_in, p_cin, [s_c[0]], "adam_c_w_in")
    u_cout = big_update(c_w_out, m_c_w_out, v_c_w_out, p_cout, [s_c[1]], "adam_c_w_out")
    tr_ = lambda a: jnp.swapaxes(a, 1, 2)
    u_wup = [tr_(o) for o in big_update(tr_(f_w_up), tr_(m_f_w_up), tr_(v_f_w_up), p_wup,
                                        [s_f0[0], s_f1[0]], "adam_f_w_up")]
    u_wdn = big_update(f_w_down, m_f_w_down, v_f_w_down, p_wdn, [s_f0[1], s_f1[1]], "adam_f_w_down")
    p_win0, p_wout0 = _chip_wait(started[3:], zones[:2], [0, 1], u_wdn[0], "rs_chip_wait_late")
    u_win0 = big_update(ab_w_in, m_ab_w_in, v_ab_w_in, p_win0, [s_ab[0]], "adam_ab_w_in")
    u_wout0 = big_update(ab_w_out, m_ab_w_out, v_ab_w_out, p_wout0, [s_ab[1]], "adam_ab_w_out")

    g_norm_mix = g1024[0:2]
    g_norm_ffn = g1024[2:4]
    g_norm_final = g1024[4:5]
    g_ccw = lax.dynamic_slice(g1024[5:8], (0, dev * (D // NDEV)), (3, D // NDEV))
    g_bcw = lax.dynamic_slice(g512[8:8 + BCONV], (0, dev * (DA // NDEV)), (BCONV, DA // NDEV))
    gfc = gfc.reshape(2, NG, 2, 3, FB)
    g_fcw = lax.dynamic_slice(gfc, (0, dev % NG, dev // NG, 0, 0), (2, 1, 1, 3, FB)).reshape(2, 3, FB)
    small_w = [norm_mix, norm_ffn, nfin, a_ln_g, a_ln_b, a_w_s[0], a_b_s[0], b_conv_w[0], b_conv_b,
               b_ln_g, b_ln_b, c_conv_w[0], f_conv_w]
    small_g = [g_norm_mix, g_norm_ffn, g_norm_final, g512[0:1], g512[1:2],
               dws.reshape(HEADS, CHUNK, CHUNK), dbs, g_bcw, g512[2:3],
               g512[3:4], g512[4:5], g_ccw, g_fcw]
    small_m = [m_norm_mix, m_norm_ffn, m_norm_final.reshape(1, D), m_a_ln_g, m_a_ln_b, m_a_w_s[0], m_a_b_s[0],
               m_b_conv_w[0], m_b_conv_b, m_b_ln_g, m_b_ln_b, m_c_conv_w[0], m_f_conv_w]
    small_v = [v_norm_mix, v_norm_ffn, v_norm_final.reshape(1, D), v_a_ln_g, v_a_ln_b, v_a_w_s[0], v_a_b_s[0],
               v_b_conv_w[0], v_b_conv_b, v_b_ln_g, v_b_ln_b, v_c_conv_w[0], v_f_conv_w]
    upd = _adam_small(small_w, small_g, small_m, small_v, "adam_small")
    ns = len(small_w)
    orig = [norm_mix, norm_ffn, norm_final, a_ln_g, a_ln_b, a_w_s, a_b_s, b_conv_w, b_conv_b,
            b_ln_g, b_ln_b, c_conv_w, f_conv_w]
    sg_out = [g.reshape(o.shape) for g, o in zip(small_g, orig)]
    sd_out = [a.reshape(o.shape) for a, o in zip(upd[0:ns], orig)]
    sm_out = [a.reshape(o.shape) for a, o in zip(upd[ns:2 * ns], orig)]
    sv_out = [a.reshape(o.shape) for a, o in zip(upd[2 * ns:3 * ns], orig)]

    def assemble(small, k):
        return [small[0], small[1], small[2], u_win0[k], small[3], small[4], small[5], small[6], small[7],
                small[8], small[9], small[10], u_wout0[k], u_cin[k], small[11], u_cout[k], u_wup[k],
                small[12], u_wdn[k]]

    grads = assemble(sg_out, 0)
    deltas = assemble(sd_out, 1)
    new_m = assemble(sm_out, 2)
    new_v = assemble(sv_out, 3)
    return (loss, grad_x.reshape(1, s, D), *grads, *deltas, *new_m, *new_v)
```

```python
import math

import jax
import jax.numpy as jnp
from jax import lax
from jax.experimental import pallas as pl
from jax.experimental.pallas import tpu as pltpu

F32 = jnp.float32
BF16 = jnp.bfloat16

D = 1024
DA = 512
HEADS = 4
CHUNK = 128
DFF = 2816
NDEV = 8
NCHIP = 4
FB = DFF * 2 // NDEV
NG = DFF // FB
BCONV = 31
EPS = 1e-6
HALO = 16
HALO_B = 32
RC = 32
NPART = 2
VMEM_LIMIT = 52 * 1024 * 1024
INV_SQRT2 = 1.0 / math.sqrt(2.0)
INV_SQRT_2PI = 1.0 / math.sqrt(2.0 * math.pi)

ADAM_LR = 0.001
ADAM_B1 = 0.9
ADAM_B2 = 0.999
ADAM_EPS = 1e-08
ADAM_WD = 0.01
ADAM_STEP = 10

MESH = pl.DeviceIdType.MESH
ANY = pl.BlockSpec(memory_space=pl.ANY)
NT_DIMS = (((1,), (1,)), ((), ()))
TN_DIMS = (((0,), (0,)), ((), ()))


def _params(*sem):
    return pltpu.CompilerParams(dimension_semantics=sem, vmem_limit_bytes=VMEM_LIMIT)


def _tile(s, want):
    return min(want, s)


def _sigmoid(x):
    return jax.nn.sigmoid(x)


def _dsilu(x, sg):
    return sg * (1.0 + x * (1.0 - sg))


def _gelu(x):
    return 0.5 * x * (1.0 + lax.erf(x * INV_SQRT2))


def _dgelu(x):
    return 0.5 * (1.0 + lax.erf(x * INV_SQRT2)) + x * jnp.exp(-0.5 * x * x) * INV_SQRT_2PI


def _ln_fwd(x, g, b):
    mu = jnp.mean(x, axis=-1, keepdims=True)
    xc = x - mu
    var = jnp.mean(xc * xc, axis=-1, keepdims=True)
    rstd = lax.rsqrt(var + EPS)
    xhat = xc * rstd
    return xhat * g + b, xhat, rstd


def _ln_bwd(dy, xhat, rstd, g):
    dxh = dy * g
    m1 = jnp.mean(dxh, axis=-1, keepdims=True)
    m2 = jnp.mean(dxh * xhat, axis=-1, keepdims=True)
    return rstd * (dxh - m1 - xhat * m2)


def _rms_bwd_math(dh, x, g):
    r = lax.rsqrt(jnp.mean(x * x, axis=-1, keepdims=True) + EPS)
    xhat = x * r
    dg = jnp.sum(dh * xhat, axis=0, keepdims=True)
    u = dh * g
    dx = r * (u - xhat * jnp.mean(u * xhat, axis=-1, keepdims=True))
    return dx, dg


def _conv3(xe, cw, halo):
    x0 = xe[halo:]
    x1 = pltpu.roll(xe, 1, 0)[halo:]
    x2 = pltpu.roll(xe, 2, 0)[halo:]
    return cw[2] * x0 + cw[1] * x1 + cw[0] * x2, (x0, x1, x2)


def _conv3_bwd_in(dce, cw, ts):
    n = dce.shape[0]
    d1 = pltpu.roll(dce, n - 1, 0)[:ts]
    d2 = pltpu.roll(dce, n - 2, 0)[:ts]
    return cw[2] * dce[:ts] + cw[1] * d1 + cw[0] * d2


def _conv3_bwd_w(dc, taps):
    x0, x1, x2 = taps
    return [jnp.sum(dc * x2, axis=0, keepdims=True), jnp.sum(dc * x1, axis=0, keepdims=True),
            jnp.sum(dc * x0, axis=0, keepdims=True)]


def _rms_fwd(x, g, name, after=None):
    s = x.shape[0]
    ts = _tile(s, 512)

    def body(x_ref, g_ref, *rest):
        h_ref = rest[-1]
        xv = x_ref[...]
        r = lax.rsqrt(jnp.mean(xv * xv, axis=-1, keepdims=True) + EPS)
        h_ref[...] = (xv * r * g_ref[...]).astype(BF16)

    extra = [] if after is None else [after]
    return pl.pallas_call(
        body, grid=(s // ts,), name=name,
        in_specs=[pl.BlockSpec((ts, D), lambda i: (i, 0)), pl.BlockSpec((1, D), lambda i: (0, 0))]
        + [ANY] * len(extra),
        out_specs=pl.BlockSpec((ts, D), lambda i: (i, 0)),
        out_shape=jax.ShapeDtypeStruct((s, D), BF16),
        compiler_params=_params("parallel"),
    )(x, g, *extra)


MXU_COLS = 256


def _pair(bn):
    return 1 if bn % MXU_COLS == 0 else 2


def _cols(w_ref, b, pair):
    return w_ref[b] if pair == 1 else jnp.concatenate([w_ref[b + q] for q in range(pair)], axis=1)


def _mm_in(h, wblk, name):
    s = h.shape[0]
    nb, _, bn = wblk.shape
    pair = _pair(bn)
    ts = _tile(s, 1024)

    def body(h_ref, w_ref, o_ref):
        hv = h_ref[...]
        for b in range(0, nb, pair):
            o_ref[:, b * bn:(b + pair) * bn] = jnp.dot(hv, _cols(w_ref, b, pair),
                                                       preferred_element_type=F32).astype(BF16)

    return pl.pallas_call(
        body, grid=(s // ts,), name=name,
        in_specs=[pl.BlockSpec((ts, D), lambda i: (i, 0)), pl.BlockSpec((nb, D, bn), lambda i: (0, 0, 0))],
        out_specs=pl.BlockSpec((ts, nb * bn), lambda i: (i, 0)),
        out_shape=jax.ShapeDtypeStruct((s, nb * bn), BF16),
        compiler_params=_params("parallel"),
    )(h, wblk)


def _rms_math(xv, g):
    r = lax.rsqrt(jnp.mean(xv * xv, axis=-1, keepdims=True) + EPS)
    return (xv * r * g).astype(BF16)


def _mm_out(y, w, xres, gnext, name):
    s = y.shape[0]
    ts = _tile(s, 1024)

    def body(y_ref, w_ref, x_ref, g_ref, o_ref, h_ref):
        xn = x_ref[...] + jnp.dot(y_ref[...], w_ref[...], preferred_element_type=F32)
        o_ref[...] = xn
        h_ref[...] = _rms_math(xn, g_ref[...])

    return pl.pallas_call(
        body, grid=(s // ts,), name=name,
        in_specs=[pl.BlockSpec((ts, D), lambda i: (i, 0)), pl.BlockSpec((D, D), lambda i: (0, 0)),
                  pl.BlockSpec((ts, D), lambda i: (i, 0)), pl.BlockSpec((1, D), lambda i: (0, 0))],
        out_specs=[pl.BlockSpec((ts, D), lambda i: (i, 0)), pl.BlockSpec((ts, D), lambda i: (i, 0))],
        out_shape=[jax.ShapeDtypeStruct((s, D), F32), jax.ShapeDtypeStruct((s, D), BF16)],
        compiler_params=_params("parallel"),
    )(y, w, xres, gnext)


CONV_ROWS = 32


def _rolled_copies(dst_ref, xe, back):
    n = xe.shape[0]
    dst_ref[0] = xe
    for r in range(1, 8):
        dst_ref[r] = pltpu.roll(xe, n - r if back else r, 0)


def _conv31(rolled_ref, cw_ref, ts, out_ref, bias):
    for o in range(0, ts, CONV_ROWS):
        acc = jnp.zeros((CONV_ROWS, DA), F32) + bias
        for sh in range(BCONV):
            q, r = divmod(sh, 8)
            lo = HALO_B - 8 * q + o
            acc = acc + cw_ref[BCONV - 1 - sh:BCONV - sh, :] * rolled_ref[r, lo:lo + CONV_ROWS, :]
        out_ref[o:o + CONV_ROWS, :] = acc


def _ab_fwd(z, lga, lba, wsm, bs_col, cwb, cbb, lgb, lbb, name):
    s = z.shape[0]
    ts = _tile(s, 256)
    hb = ts // HALO_B

    def body(z_ref, zh_ref, lga_ref, lba_ref, ws_ref, bs_ref, cw_ref, cb_ref, lgb_ref, lbb_ref,
             y_ref, yb2_ref, rolled):
        i = pl.program_id(0)
        z_t = z_ref[...].astype(F32)
        gu = _gelu(z_t[:, 0:DA])
        gv = _gelu(z_t[:, DA:2 * DA])
        vn, _, _ = _ln_fwd(gv, lga_ref[...], lba_ref[...])
        vnb = vn.astype(BF16)
        for c in range(ts // CHUNK):
            for h in range(HEADS):
                rs = slice(c * CHUNK, (c + 1) * CHUNK)
                cs = slice(h * CHUNK, (h + 1) * CHUNK)
                mixed = jnp.dot(ws_ref[h], vnb[rs, cs], preferred_element_type=F32) + bs_ref[h]
                y_ref[rs, cs] = (gu[rs, cs] * mixed).astype(BF16)
        zh = jnp.where(i > 0, zh_ref[...], jnp.zeros_like(zh_ref[...])).astype(F32)
        xb = jnp.concatenate([zh[:, 0:DA], z_t[:, 2 * DA:3 * DA]], axis=0)
        gb = jnp.concatenate([zh[:, DA:2 * DA], z_t[:, 3 * DA:4 * DA]], axis=0)
        _rolled_copies(rolled, xb * _sigmoid(gb), False)
        _conv31(rolled, cw_ref, ts, yb2_ref, cb_ref[...])
        nb_, _, _ = _ln_fwd(yb2_ref[...], lgb_ref[...], lbb_ref[...])
        y_ref[:, DA:2 * DA] = (nb_ * _sigmoid(nb_)).astype(BF16)

    row = lambda i: (0, 0)
    return pl.pallas_call(
        body, grid=(s // ts,), name=name,
        in_specs=[pl.BlockSpec((ts, 4 * DA), lambda i: (i, 0)),
                  pl.BlockSpec((HALO_B, 2 * DA), lambda i: (jnp.maximum(i * hb - 1, 0), 1)),
                  pl.BlockSpec((1, DA), row), pl.BlockSpec((1, DA), row),
                  pl.BlockSpec((HEADS, CHUNK, CHUNK), lambda i: (0, 0, 0)),
                  pl.BlockSpec((HEADS, CHUNK, 1), lambda i: (0, 0, 0)),
                  pl.BlockSpec((BCONV, DA), row), pl.BlockSpec((1, DA), row),
                  pl.BlockSpec((1, DA), row), pl.BlockSpec((1, DA), row)],
        out_specs=[pl.BlockSpec((ts, 2 * DA), lambda i: (i, 0)), pl.BlockSpec((ts, DA), lambda i: (i, 0))],
        out_shape=[jax.ShapeDtypeStruct((s, 2 * DA), BF16), jax.ShapeDtypeStruct((s, DA), F32)],
        scratch_shapes=[pltpu.VMEM((8, ts + HALO_B, DA), F32)],
        compiler_params=_params("parallel"),
    )(z, z, lga, lba, wsm, bs_col, cwb, cbb, lgb, lbb)


def _c_fwd(zc, cw, name):
    s = zc.shape[0]
    ts = _tile(s, 512)
    hb = ts // HALO

    def body(z_ref, ch_ref, xh_ref, cw_ref, r_ref):
        i = pl.program_id(0)
        z_t = z_ref[...].astype(F32)
        ph = jnp.where(i > 0, ch_ref[...].astype(F32) * xh_ref[...].astype(F32), 0.0)
        pe = jnp.concatenate([ph, z_t[:, D:2 * D] * z_t[:, 2 * D:3 * D]], axis=0)
        q, _ = _conv3(pe, [cw_ref[k:k + 1, :] for k in range(3)], HALO)
        r_ref[...] = (z_t[:, 0:D] * q).astype(BF16)

    halo = lambda col: pl.BlockSpec((HALO, D), lambda i: (jnp.maximum(i * hb - 1, 0), col))
    return pl.pallas_call(
        body, grid=(s // ts,), name=name,
        in_specs=[pl.BlockSpec((ts, 3 * D), lambda i: (i, 0)), halo(1), halo(2),
                  pl.BlockSpec((3, D), lambda i: (0, 0))],
        out_specs=pl.BlockSpec((ts, D), lambda i: (i, 0)),
        out_shape=jax.ShapeDtypeStruct((s, D), BF16),
        compiler_params=_params("parallel"),
    )(zc, zc, zc, cw)


def _ffn_fwd(h, xres, wup, fcw, wdn, gnext, name):
    s = h.shape[0]
    ts = _tile(s, 512)
    hb = ts // HALO

    def body(h_ref, hh_ref, w_ref, cw_ref, wd_ref, x_ref, *rest):
        if gnext is not None:
            gn_ref, up_ref, upc_ref, xo_ref, hn_ref, up_s = rest
        else:
            up_ref, upc_ref, xo_ref, up_s = rest
        i = pl.program_id(0)
        m = pl.program_id(1)
        @pl.when(m == 0)
        def _():
            xo_ref[...] = x_ref[...]

        halo = jnp.where(i > 0, hh_ref[...], jnp.zeros_like(hh_ref[...]))
        hx = jnp.concatenate([halo, h_ref[...]], axis=0)
        acts = []
        for gv in range(2):
            up_s[gv] = jnp.dot(hx, w_ref[gv], preferred_element_type=F32)
            x0 = up_s[gv, HALO:HALO + ts, :]
            up_ref[gv] = x0.astype(BF16)
            upc = (cw_ref[gv, 2:3, :] * x0 + cw_ref[gv, 1:2, :] * up_s[gv, HALO - 1:HALO - 1 + ts, :]
                   + cw_ref[gv, 0:1, :] * up_s[gv, HALO - 2:HALO - 2 + ts, :])
            upc_ref[gv] = upc.astype(BF16)
            acts.append(upc)
        a = acts[0] * _sigmoid(acts[0]) * acts[1]
        xo_ref[...] += jnp.dot(a.astype(BF16), wd_ref[...], preferred_element_type=F32)

        if gnext is not None:
            @pl.when(m == NG - 1)
            def _():
                hn_ref[...] = _rms_math(xo_ref[...], gn_ref[...])

    tile = pl.BlockSpec((ts, D), lambda i, m: (i, 0))
    nxt = gnext is not None
    return pl.pallas_call(
        body, grid=(s // ts, NG), name=name,
        in_specs=[tile,
                  pl.BlockSpec((HALO, D), lambda i, m: (jnp.maximum(i * hb - 1, 0), 0)),
                  pl.BlockSpec((2, None, D, FB), lambda i, m: (0, m, 0, 0)),
                  pl.BlockSpec((2, None, 3, FB), lambda i, m: (0, m, 0, 0)),
                  pl.BlockSpec((FB, D), lambda i, m: (m, 0)),
                  tile] + ([pl.BlockSpec((1, D), lambda i, m: (0, 0))] if nxt else []),
        out_specs=[pl.BlockSpec((None, 2, ts, FB), lambda i, m: (m, 0, i, 0)),
                   pl.BlockSpec((None, 2, ts, FB), lambda i, m: (m, 0, i, 0)),
                   tile] + ([tile] if nxt else []),
        out_shape=[jax.ShapeDtypeStruct((NG, 2, s, FB), BF16), jax.ShapeDtypeStruct((NG, 2, s, FB), BF16),
                   jax.ShapeDtypeStruct((s, D), F32)] + ([jax.ShapeDtypeStruct((s, D), BF16)] if nxt else []),
        scratch_shapes=[pltpu.VMEM((2, ts + HALO, FB), F32)],
        compiler_params=_params("arbitrary", "arbitrary"),
    )(h, h, wup, fcw, wdn, xres, *([gnext] if nxt else []))


def _final(x, tgt, g, name):
    s = x.shape[0]
    ts = _tile(s, 512)

    def body(x_ref, t_ref, g_ref, dx_ref, dxb_ref, dg_ref, loss_ref):
        i = pl.program_id(0)
        xv = x_ref[...]
        gv = g_ref[...]
        r = lax.rsqrt(jnp.mean(xv * xv, axis=-1, keepdims=True) + EPS)
        xhat = xv * r
        e = xhat * gv - t_ref[...]
        part = 0.5 * jnp.sum(jnp.mean(e * e, axis=-1, keepdims=True), axis=0, keepdims=True)
        dy = e * (1.0 / D)
        dgp = jnp.sum(dy * xhat, axis=0, keepdims=True)
        u = dy * gv
        dx = r * (u - xhat * jnp.mean(u * xhat, axis=-1, keepdims=True))
        dx_ref[...] = dx
        dxb_ref[...] = dx.astype(BF16)

        @pl.when(i == 0)
        def _():
            dg_ref[...] = dgp
            loss_ref[...] = jnp.broadcast_to(part, (1, 128))

        @pl.when(i > 0)
        def _():
            dg_ref[...] += dgp
            loss_ref[...] += jnp.broadcast_to(part, (1, 128))

    return pl.pallas_call(
        body, grid=(s // ts,), name=name,
        in_specs=[pl.BlockSpec((ts, D), lambda i: (i, 0)), pl.BlockSpec((ts, D), lambda i: (i, 0)),
                  pl.BlockSpec((1, D), lambda i: (0, 0))],
        out_specs=[pl.BlockSpec((ts, D), lambda i: (i, 0)), pl.BlockSpec((ts, D), lambda i: (i, 0)),
                   pl.BlockSpec((1, D), lambda i: (0, 0)), pl.BlockSpec((1, 128), lambda i: (0, 0))],
        out_shape=[jax.ShapeDtypeStruct((s, D), F32), jax.ShapeDtypeStruct((s, D), BF16),
                   jax.ShapeDtypeStruct((1, D), F32), jax.ShapeDtypeStruct((1, 128), F32)],
        compiler_params=_params("arbitrary"),
    )(x, tgt, g)


def _ffn_bwd(df, up, upc, wup, fcw, wdn, xin, g, name):
    s = df.shape[0]
    ts = _tile(s, 512)
    nt = s // ts

    def body(df_ref, up_ref, upc_ref, w_ref, cw_ref, wd_ref, x_ref, g_ref,
             a_ref, dup_ref, dx_ref, dxb_ref, dg_ref, dcw_ref, carry, acc, tacc, dcs_ref):
        i = pl.program_id(0)
        m = pl.program_id(1)
        first = i == 0
        @pl.when(first)
        def _():
            carry[m] = jnp.zeros((2, 8, FB), F32)
            dcw_ref[m] = jnp.zeros((2, 3, FB), F32)

        @pl.when(m == 0)
        def _():
            acc[...] = jnp.zeros((ts, D), F32)

        cws = [[cw_ref[gv, k:k + 1, :] for k in range(3)] for gv in range(2)]
        part = ts // NPART
        das = [lax.dot_general(df_ref[p * part:(p + 1) * part, :].astype(BF16), wd_ref[...], NT_DIMS,
                               preferred_element_type=F32) for p in range(NPART)]

        tacc[...] = jnp.zeros((2, 3, 8, FB), F32)
        dcs_ref[:, ts:ts + 8, :] = carry[m]
        for r in reversed(range(ts // RC)):
            rs = slice(r * RC, (r + 1) * RC)
            gate = upc_ref[0, rs, :].astype(F32)
            val = upc_ref[1, rs, :].astype(F32)
            sg = _sigmoid(gate)
            sl = gate * sg
            a_ref[rs, :] = (sl * val).astype(BF16)
            da_c = das[(r * RC) // part][(r * RC) % part:(r * RC) % part + RC]
            dcs = [da_c * val * _dsilu(gate, sg), da_c * sl]
            for gv in range(2):
                dc = dcs[gv]
                dcs_ref[gv, rs, :] = dc
                d1 = dcs_ref[gv, r * RC + 1:(r + 1) * RC + 1, :]
                d2 = dcs_ref[gv, r * RC + 2:(r + 1) * RC + 2, :]
                du = cws[gv][2] * dc + cws[gv][1] * d1 + cws[gv][0] * d2
                dup_ref[gv, rs, :] = du.astype(BF16)
                x0 = up_ref[gv, rs, :].astype(F32)
                for k, dk in enumerate((d2, d1, dc)):
                    p = x0 * dk
                    tacc[gv, k] += sum(p[j:j + 8] for j in range(0, RC, 8))
            if (r * RC) % part == 0:
                ps = slice(r * RC, r * RC + part)
                acc[ps, :] += (
                    lax.dot_general(dup_ref[0, ps, :], w_ref[0], NT_DIMS, preferred_element_type=F32)
                    + lax.dot_general(dup_ref[1, ps, :], w_ref[1], NT_DIMS, preferred_element_type=F32))
        for gv in range(2):
            carry[m, gv] = dcs_ref[gv, 0:8, :]
            for k in range(3):
                dcw_ref[m, gv, k:k + 1, :] += jnp.sum(tacc[gv, k], axis=0, keepdims=True)

        @pl.when(m == NG - 1)
        def _():
            dx, dgp = _rms_bwd_math(acc[...], x_ref[...], g_ref[...])
            dx = df_ref[...] + dx
            dx_ref[...] = dx
            dxb_ref[...] = dx.astype(BF16)

            @pl.when(first)
            def _():
                dg_ref[...] = dgp

            @pl.when(jnp.logical_not(first))
            def _():
                dg_ref[...] += dgp

    rev = lambda i: nt - 1 - i
    return pl.pallas_call(
        body, grid=(nt, NG), name=name,
        in_specs=[pl.BlockSpec((ts, D), lambda i, m: (rev(i), 0)),
                  pl.BlockSpec((None, 2, ts, FB), lambda i, m: (m, 0, rev(i), 0)),
                  pl.BlockSpec((None, 2, ts, FB), lambda i, m: (m, 0, rev(i), 0)),
                  pl.BlockSpec((2, None, D, FB), lambda i, m: (0, m, 0, 0)),
                  pl.BlockSpec((2, None, 3, FB), lambda i, m: (0, m, 0, 0)),
                  pl.BlockSpec((FB, D), lambda i, m: (m, 0)),
                  pl.BlockSpec((ts, D), lambda i, m: (rev(i), 0)),
                  pl.BlockSpec((1, D), lambda i, m: (0, 0))],
        out_specs=[pl.BlockSpec((None, ts, FB), lambda i, m: (m, rev(i), 0)),
                   pl.BlockSpec((None, 2, ts, FB), lambda i, m: (m, 0, rev(i), 0)),
                   pl.BlockSpec((ts, D), lambda i, m: (rev(i), 0)),
                   pl.BlockSpec((ts, D), lambda i, m: (rev(i), 0)),
                   pl.BlockSpec((1, D), lambda i, m: (0, 0)),
                   pl.BlockSpec((NG, 2, 3, FB), lambda i, m: (0, 0, 0, 0))],
        out_shape=[jax.ShapeDtypeStruct((NG, s, FB), BF16), jax.ShapeDtypeStruct((NG, 2, s, FB), BF16),
                   jax.ShapeDtypeStruct((s, D), F32), jax.ShapeDtypeStruct((s, D), BF16),
                   jax.ShapeDtypeStruct((1, D), F32),
                   jax.ShapeDtypeStruct((NG, 2, 3, FB), F32)],
        scratch_shapes=[pltpu.VMEM((NG, 2, 8, FB), F32), pltpu.VMEM((ts, D), F32),
                        pltpu.VMEM((2, 3, 8, FB), F32), pltpu.VMEM((2, ts + 8, FB), F32)],
        compiler_params=_params("arbitrary", "arbitrary"),
    )(df, up, upc, wup, fcw, wdn, xin, g)


def _mm_nt(dy, w, name):
    s = dy.shape[0]
    ts = _tile(s, 1024)

    def body(dy_ref, w_ref, o_ref):
        o_ref[...] = lax.dot_general(dy_ref[...], w_ref[...], NT_DIMS,
                                     preferred_element_type=F32).astype(BF16)

    return pl.pallas_call(
        body, grid=(s // ts,), name=name,
        in_specs=[pl.BlockSpec((ts, D), lambda i: (i, 0)), pl.BlockSpec((D, D), lambda i: (0, 0))],
        out_specs=pl.BlockSpec((ts, D), lambda i: (i, 0)),
        out_shape=jax.ShapeDtypeStruct((s, D), BF16),
        compiler_params=_params("parallel"),
    )(dy, w)


def _mm_nt_rms(dy, wblk, x, g, dres, bf16_copy, name):
    s = dy.shape[0]
    nb, _, bn = wblk.shape
    pair = _pair(bn)
    ts = _tile(s, 512)

    def body(dy_ref, w_ref, x_ref, g_ref, dr_ref, dx_ref, *rest):
        dg_ref = rest[-1]
        i = pl.program_id(0)
        acc = jnp.zeros((ts, D), F32)
        for b in range(0, nb, pair):
            acc = acc + lax.dot_general(dy_ref[:, b * bn:(b + pair) * bn], _cols(w_ref, b, pair), NT_DIMS,
                                        preferred_element_type=F32)
        dx, dgp = _rms_bwd_math(acc, x_ref[...], g_ref[...])
        dx = dr_ref[...] + dx
        dx_ref[...] = dx
        if bf16_copy:
            rest[0][...] = dx.astype(BF16)

        @pl.when(i == 0)
        def _():
            dg_ref[...] = dgp

        @pl.when(i > 0)
        def _():
            dg_ref[...] += dgp

    tile = pl.BlockSpec((ts, D), lambda i: (i, 0))
    return pl.pallas_call(
        body, grid=(s // ts,), name=name,
        in_specs=[pl.BlockSpec((ts, nb * bn), lambda i: (i, 0)), pl.BlockSpec((nb, D, bn), lambda i: (0, 0, 0)),
                  tile, pl.BlockSpec((1, D), lambda i: (0, 0)), tile],
        out_specs=[tile] + ([tile] if bf16_copy else []) + [pl.BlockSpec((1, D), lambda i: (0, 0))],
        out_shape=[jax.ShapeDtypeStruct((s, D), F32)] + ([jax.ShapeDtypeStruct((s, D), BF16)] if bf16_copy else [])
        + [jax.ShapeDtypeStruct((1, D), F32)],
        compiler_params=_params("arbitrary"),
    )(dy, wblk, x, g, dres)


def _c_bwd(dr, zc, cw, name):
    s = dr.shape[0]
    ts = _tile(s, 512)
    nt = s // ts
    hb = ts // HALO

    def body(dr_ref, drf_ref, z_ref, ch_ref, xh_ref, bf_ref, cw_ref, dz_ref, dcw_ref):
        i = pl.program_id(0)
        cwv = [cw_ref[k:k + 1, :] for k in range(3)]
        z_t = z_ref[...].astype(F32)
        bg, cg, xv = z_t[:, 0:D], z_t[:, D:2 * D], z_t[:, 2 * D:3 * D]
        ph = jnp.where(i > 0, ch_ref[...].astype(F32) * xh_ref[...].astype(F32), 0.0)
        pe = jnp.concatenate([ph, cg * xv], axis=0)
        q, taps = _conv3(pe, cwv, HALO)
        drv = dr_ref[...].astype(F32)
        dq = drv * bg
        dqf = jnp.where(i < nt - 1, drf_ref[...].astype(F32) * bf_ref[...].astype(F32), 0.0)
        dp = _conv3_bwd_in(jnp.concatenate([dq, dqf], axis=0), cwv, ts)
        dz_ref[:, 0:D] = (drv * q).astype(BF16)
        dz_ref[:, D:2 * D] = (dp * xv).astype(BF16)
        dz_ref[:, 2 * D:3 * D] = (dp * cg).astype(BF16)
        rows = _conv3_bwd_w(dq, taps)

        @pl.when(i == 0)
        def _():
            for k in range(3):
                dcw_ref[k:k + 1, :] = rows[k]

        @pl.when(i > 0)
        def _():
            for k in range(3):
                dcw_ref[k:k + 1, :] += rows[k]

    past = lambda col: pl.BlockSpec((HALO, D), lambda i: (jnp.maximum(i * hb - 1, 0), col))
    nxt = lambda i: jnp.minimum((i + 1) * hb, s // HALO - 1)
    return pl.pallas_call(
        body, grid=(nt,), name=name,
        in_specs=[pl.BlockSpec((ts, D), lambda i: (i, 0)),
                  pl.BlockSpec((HALO, D), lambda i: (nxt(i), 0)),
                  pl.BlockSpec((ts, 3 * D), lambda i: (i, 0)), past(1), past(2),
                  pl.BlockSpec((HALO, D), lambda i: (nxt(i), 0)),
                  pl.BlockSpec((3, D), lambda i: (0, 0))],
        out_specs=[pl.BlockSpec((ts, 3 * D), lambda i: (i, 0)), pl.BlockSpec((3, D), lambda i: (0, 0))],
        out_shape=[jax.ShapeDtypeStruct((s, 3 * D), BF16), jax.ShapeDtypeStruct((3, D), F32)],
        compiler_params=_params("arbitrary"),
    )(dr, dr, zc, zc, zc, zc, cw)


G512_ROWS = 40


def _ab_bwd(dy, z, yb2, lga, lba, wsm, bs_col, cwb, lgb, lbb, name):
    s = z.shape[0]
    ts = _tile(s, 256)
    nt = s // ts
    hb = ts // HALO_B
    nch = ts // CHUNK

    def body(z_ref, zh_ref, dy_ref, dyf_ref, yb2_ref, yb2f_ref, lga_ref, lba_ref, ws_ref, bs_ref,
             cw_ref, lgb_ref, lbb_ref, dz_ref, g512_ref, dws_ref, dbs_ref, dvn_ref, fwd_rolled, bwd_rolled, du_s):
        i = pl.program_id(0)
        last = i == nt - 1

        @pl.when(i == 0)
        def _():
            g512_ref[...] = jnp.zeros((G512_ROWS, DA), F32)
            dws_ref[...] = jnp.zeros((HEADS, CHUNK, CHUNK), F32)
            dbs_ref[...] = jnp.zeros((HEADS, CHUNK, 1), F32)

        def add_row(k, v):
            g512_ref[k:k + 1, :] += v

        z_t = z_ref[...].astype(F32)
        dy_t = dy_ref[...].astype(F32)
        ua, va = z_t[:, 0:DA], z_t[:, DA:2 * DA]
        gu = _gelu(ua)
        gv = _gelu(va)
        lga_v = lga_ref[...]
        vn, xhat_a, rstd_a = _ln_fwd(gv, lga_v, lba_ref[...])
        vnb = vn.astype(BF16)
        causal = (lax.broadcasted_iota(jnp.int32, (CHUNK, CHUNK), 0)
                  >= lax.broadcasted_iota(jnp.int32, (CHUNK, CHUNK), 1)).astype(F32)
        for c in range(nch):
            for h in range(HEADS):
                rs = slice(c * CHUNK, (c + 1) * CHUNK)
                cs = slice(h * CHUNK, (h + 1) * CHUNK)
                vblk = vnb[rs, cs]
                mixed = jnp.dot(ws_ref[h], vblk, preferred_element_type=F32) + bs_ref[h]
                dyb_ = dy_t[rs, cs]
                dmix = dyb_ * gu[rs, cs]
                dmb = dmix.astype(BF16)
                dz_ref[rs, cs] = (dyb_ * mixed * _dgelu(ua[rs, cs])).astype(BF16)
                dvn_ref[rs, cs] = lax.dot_general(ws_ref[h], dmb, TN_DIMS, preferred_element_type=F32)
                dws_ref[h] += causal * lax.dot_general(dmb, vblk, NT_DIMS, preferred_element_type=F32)
                dbs_ref[h] += jnp.sum(dmix, axis=1, keepdims=True)
        dvn = dvn_ref[...]
        add_row(0, jnp.sum(dvn * xhat_a, axis=0, keepdims=True))
        add_row(1, jnp.sum(dvn, axis=0, keepdims=True))
        dgv = _ln_bwd(dvn, xhat_a, rstd_a, lga_v)
        dz_ref[:, DA:2 * DA] = (dgv * _dgelu(va)).astype(BF16)
        lgb_v = lgb_ref[...]
        dyb_e = jnp.concatenate(
            [dy_t[:, DA:2 * DA], jnp.where(last, 0.0, dyf_ref[...].astype(F32))], axis=0)
        yb2_e = jnp.concatenate([yb2_ref[...], jnp.where(last, 0.0, yb2f_ref[...])], axis=0)
        n_e, xhat_b, rstd_b = _ln_fwd(yb2_e, lgb_v, lbb_ref[...])
        sgn = _sigmoid(n_e)
        dn = dyb_e * _dsilu(n_e, sgn)
        dy2 = _ln_bwd(dn, xhat_b, rstd_b, lgb_v)
        add_row(2, jnp.sum(dy2[:ts], axis=0, keepdims=True))
        add_row(3, jnp.sum(dn[:ts] * xhat_b[:ts], axis=0, keepdims=True))
        add_row(4, jnp.sum(dn[:ts], axis=0, keepdims=True))
        zh = jnp.where(i > 0, zh_ref[...], jnp.zeros_like(zh_ref[...])).astype(F32)
        xb_t, gb_t = z_t[:, 2 * DA:3 * DA], z_t[:, 3 * DA:4 * DA]
        sgb = _sigmoid(gb_t)
        _rolled_copies(fwd_rolled, jnp.concatenate(
            [zh[:, 0:DA] * _sigmoid(zh[:, DA:2 * DA]), xb_t * sgb], axis=0), False)
        _rolled_copies(bwd_rolled, dy2, True)
        for o in range(0, ts, CONV_ROWS):
            acc = jnp.zeros((CONV_ROWS, DA), F32)
            for sh in range(BCONV):
                q, r = divmod(sh, 8)
                acc = acc + cw_ref[BCONV - 1 - sh:BCONV - sh, :] * bwd_rolled[r, 8 * q + o:8 * q + o + CONV_ROWS, :]
            du_s[o:o + CONV_ROWS, :] = acc
        for sh in range(BCONV):
            q, r = divmod(sh, 8)
            acc = jnp.zeros((CONV_ROWS, DA), F32)
            for o in range(0, ts, CONV_ROWS):
                lo = HALO_B - 8 * q + o
                acc = acc + bwd_rolled[0, o:o + CONV_ROWS, :] * fwd_rolled[r, lo:lo + CONV_ROWS, :]
            add_row(8 + BCONV - 1 - sh, jnp.sum(acc, axis=0, keepdims=True))
        du = du_s[...]
        dz_ref[:, 2 * DA:3 * DA] = (du * sgb).astype(BF16)
        dz_ref[:, 3 * DA:4 * DA] = (du * xb_t * sgb * (1.0 - sgb)).astype(BF16)

    row = lambda i: (0, 0)
    nxt = lambda i: jnp.minimum((i + 1) * hb, s // HALO_B - 1)
    return pl.pallas_call(
        body, grid=(nt,), name=name,
        in_specs=[pl.BlockSpec((ts, 4 * DA), lambda i: (i, 0)),
                  pl.BlockSpec((HALO_B, 2 * DA), lambda i: (jnp.maximum(i * hb - 1, 0), 1)),
                  pl.BlockSpec((ts, 2 * DA), lambda i: (i, 0)),
                  pl.BlockSpec((HALO_B, DA), lambda i: (nxt(i), 1)),
                  pl.BlockSpec((ts, DA), lambda i: (i, 0)),
                  pl.BlockSpec((HALO_B, DA), lambda i: (nxt(i), 0)),
                  pl.BlockSpec((1, DA), row), pl.BlockSpec((1, DA), row),
                  pl.BlockSpec((HEADS, CHUNK, CHUNK), lambda i: (0, 0, 0)),
                  pl.BlockSpec((HEADS, CHUNK, 1), lambda i: (0, 0, 0)),
                  pl.BlockSpec((BCONV, DA), row), pl.BlockSpec((1, DA), row), pl.BlockSpec((1, DA), row)],
        out_specs=[pl.BlockSpec((ts, 4 * DA), lambda i: (i, 0)),
                   pl.BlockSpec((G512_ROWS, DA), row),
                   pl.BlockSpec((HEADS, CHUNK, CHUNK), lambda i: (0, 0, 0)),
                   pl.BlockSpec((HEADS, CHUNK, 1), lambda i: (0, 0, 0))],
        out_shape=[jax.ShapeDtypeStruct((s, 4 * DA), BF16), jax.ShapeDtypeStruct((G512_ROWS, DA), F32),
                   jax.ShapeDtypeStruct((HEADS, CHUNK, CHUNK), F32),
                   jax.ShapeDtypeStruct((HEADS, CHUNK, 1), F32)],
        scratch_shapes=[pltpu.VMEM((ts, DA), F32), pltpu.VMEM((8, ts + HALO_B, DA), F32),
                        pltpu.VMEM((8, ts + HALO_B, DA), F32), pltpu.VMEM((ts, DA), F32)],
        compiler_params=_params("arbitrary"),
    )(z, z, dy, dy, yb2, yb2, lga, lba, wsm, bs_col, cwb, lgb, lbb)


def _dw_cols(a, dy, nb, bn, name):
    s = a.shape[0]
    tm = _tile(s, 2048)
    nt = s // tm
    cpb = 4

    def body(a_ref, dy_ref, o_ref, acc):
        t = pl.program_id(1)
        p = lax.dot_general(a_ref[...], dy_ref[...], TN_DIMS, preferred_element_type=F32)

        @pl.when(t == 0)
        def _():
            for q in range(cpb):
                acc[q] = p[:, q * bn:(q + 1) * bn]

        @pl.when(t > 0)
        def _():
            for q in range(cpb):
                acc[q] += p[:, q * bn:(q + 1) * bn]

        @pl.when(t == nt - 1)
        def _():
            o_ref[...] = acc[...].astype(BF16)

    return pl.pallas_call(
        body, grid=(nb // cpb, nt), name=name,
        in_specs=[pl.BlockSpec((tm, D), lambda j, t: (t, 0)), pl.BlockSpec((tm, cpb * bn), lambda j, t: (t, j))],
        out_specs=pl.BlockSpec((cpb, D, bn), lambda j, t: (j, 0, 0)),
        out_shape=jax.ShapeDtypeStruct((nb, D, bn), BF16),
        scratch_shapes=[pltpu.VMEM((cpb, D, bn), F32)],
        compiler_params=_params("arbitrary", "arbitrary"),
    )(a, dy)


def _dw_rows(a, dy, name):
    s = a.shape[0]
    tm = _tile(s, 4096)
    nt = s // tm
    rb = 512

    def body(a_ref, dy_ref, o_ref, acc):
        t = pl.program_id(1)
        p = lax.dot_general(a_ref[...], dy_ref[...], TN_DIMS, preferred_element_type=F32)

        @pl.when(t == 0)
        def _():
            acc[...] = p

        @pl.when(t > 0)
        def _():
            acc[...] += p

        @pl.when(t == nt - 1)
        def _():
            o_ref[...] = acc[...].astype(BF16)

    return pl.pallas_call(
        body, grid=(D // rb, nt), name=name,
        in_specs=[pl.BlockSpec((tm, rb), lambda j, t: (t, j)), pl.BlockSpec((tm, D), lambda j, t: (t, 0))],
        out_specs=pl.BlockSpec((rb, D), lambda j, t: (j, 0)),
        out_shape=jax.ShapeDtypeStruct((D, D), BF16),
        scratch_shapes=[pltpu.VMEM((rb, D), F32)],
        compiler_params=_params("arbitrary", "arbitrary"),
    )(a, dy)


def _dw_up(h, dup, name):
    s = h.shape[0]
    tm = _tile(s, 4096)
    nt = s // tm

    def body(h_ref, d_ref, o_ref, acc):
        t = pl.program_id(1)
        p = lax.dot_general(d_ref[...], h_ref[...], TN_DIMS, preferred_element_type=F32)

        @pl.when(t == 0)
        def _():
            acc[...] = p

        @pl.when(t > 0)
        def _():
            acc[...] += p

        @pl.when(t == nt - 1)
        def _():
            o_ref[...] = acc[...].astype(BF16)

    return pl.pallas_call(
        body, grid=(NDEV, nt), name=name,
        in_specs=[pl.BlockSpec((tm, D), lambda b, t: (t, 0)),
                  pl.BlockSpec((None, None, tm, FB), lambda b, t: (b % NG, b // NG, t, 0))],
        out_specs=pl.BlockSpec((None, FB, D), lambda b, t: (b, 0, 0)),
        out_shape=jax.ShapeDtypeStruct((NDEV, FB, D), BF16),
        scratch_shapes=[pltpu.VMEM((FB, D), F32)],
        compiler_params=_params("arbitrary", "arbitrary"),
    )(h, dup)


def _dw_dn(a, df, name):
    s = df.shape[0]
    tm = _tile(s, 4096)
    nt = s // tm

    def body(a_ref, d_ref, o_ref, acc):
        t = pl.program_id(1)
        p = lax.dot_general(a_ref[...], d_ref[...], TN_DIMS, preferred_element_type=F32)

        @pl.when(t == 0)
        def _():
            acc[...] = p

        @pl.when(t > 0)
        def _():
            acc[...] += p

        @pl.when(t == nt - 1)
        def _():
            o_ref[...] = acc[...].astype(BF16)

    return pl.pallas_call(
        body, grid=(NG, nt), name=name,
        in_specs=[pl.BlockSpec((None, tm, FB), lambda m, t: (m, t, 0)), pl.BlockSpec((tm, D), lambda m, t: (t, 0))],
        out_specs=pl.BlockSpec((FB, D), lambda m, t: (m, 0)),
        out_shape=jax.ShapeDtypeStruct((DFF, D), BF16),
        scratch_shapes=[pltpu.VMEM((FB, D), F32)],
        compiler_params=_params("arbitrary", "arbitrary"),
    )(a, df)


def _place():
    x, y, c = lax.axis_index("x"), lax.axis_index("y"), lax.axis_index("c")
    chips = [(1 - x, y), (x, 1 - y), (1 - x, 1 - y)]
    return x, y, c, chips


def _zone(shard, dev):
    return lax.dynamic_update_slice(lax.empty((NDEV,) + shard.shape, shard.dtype), shard[None],
                                    (dev,) + (0,) * shard.ndim)


HBM_SPEC = pl.BlockSpec(memory_space=pltpu.HBM)
SEM_SPEC = pl.BlockSpec(memory_space=pltpu.SEMAPHORE)
DATAFLOW = pltpu.SideEffectType.DATAFLOW_SIDE_EFFECTING


def _hbm(a):
    return pltpu.with_memory_space_constraint(a, pltpu.HBM)


def _hbm_like(arrs):
    return [pltpu.HBM(a.shape, a.dtype) for a in arrs]


def _ag_start(srcs, lands, after, name):
    n = len(srcs)
    ns = 8 * n

    def body(*refs):
        src, land = refs[:n], refs[n:2 * n]
        sems = refs[2 * n + 1:2 * n + 1 + ns]
        token = refs[-1]
        x, y, c, chips = _place()
        peers = [(x, y, 1 - c)] + [(*chip, c) for chip in chips]
        for t in range(n):
            for k, to in enumerate(peers):
                pltpu.make_async_remote_copy(
                    src_ref=src[t], dst_ref=land[t].at[4 * x + 2 * y + c],
                    send_sem=sems[2 * (4 * t + k)], recv_sem=sems[2 * (4 * t + k) + 1],
                    device_id=to, device_id_type=MESH).start()
        token[...] = jnp.zeros_like(token)

    res = pl.pallas_call(
        body, name=name,
        in_specs=[HBM_SPEC] * (2 * n) + [ANY],
        out_specs=[SEM_SPEC] * ns + [HBM_SPEC] * (2 * n) + [pl.BlockSpec(memory_space=pltpu.VMEM)],
        out_shape=[pltpu.SemaphoreType.DMA(())] * ns + _hbm_like(srcs) + _hbm_like(lands)
        + [jax.ShapeDtypeStruct((8, 128), F32)],
        input_output_aliases={i: ns + i for i in range(2 * n)},
        compiler_params=pltpu.CompilerParams(has_side_effects=DATAFLOW),
    )(*[_hbm(a) for a in srcs], *[_hbm(a) for a in lands], after)
    sems = [[(res[2 * (4 * t + k)], res[2 * (4 * t + k) + 1]) for k in range(4)] for t in range(n)]
    return sems, res[ns:ns + n], res[ns + n:ns + 2 * n], res[-1]


def _ag_forward(srcs, lands, sems1, after, name):
    n = len(srcs)
    flat1 = [s for t in range(n) for k in range(1, 4) for s in sems1[t][k]]
    n1 = len(flat1)

    def body(*refs):
        src, land = refs[:n], refs[n:2 * n]
        s1 = refs[2 * n:2 * n + n1]
        s2 = refs[2 * n + n1 + 1:2 * n + n1 + 1 + 6 * n]
        x, y, c, chips = _place()
        for j, (cx, cy) in enumerate(chips):
            for t in range(n):
                blk = land[t].at[4 * cx + 2 * cy + c]
                pltpu.make_async_remote_copy(
                    src_ref=src[t], dst_ref=blk, send_sem=s1[2 * (3 * t + j)], recv_sem=s1[2 * (3 * t + j) + 1],
                    device_id=(cx, cy, c), device_id_type=MESH).wait_recv()
                pltpu.make_async_remote_copy(
                    src_ref=blk, dst_ref=blk, send_sem=s2[2 * (3 * t + j)], recv_sem=s2[2 * (3 * t + j) + 1],
                    device_id=(x, y, 1 - c), device_id_type=MESH).start()

    res = pl.pallas_call(
        body, name=name,
        in_specs=[HBM_SPEC] * (2 * n) + [SEM_SPEC] * n1 + [ANY],
        out_specs=[SEM_SPEC] * (6 * n) + [HBM_SPEC] * n,
        out_shape=[pltpu.SemaphoreType.DMA(())] * (6 * n) + _hbm_like(lands),
        input_output_aliases={n + i: 6 * n + i for i in range(n)},
        compiler_params=pltpu.CompilerParams(has_side_effects=DATAFLOW),
    )(*srcs, *lands, *flat1, after)
    sems2 = [[(res[2 * (3 * t + j)], res[2 * (3 * t + j) + 1]) for j in range(3)] for t in range(n)]
    return sems2, res[6 * n:]


def _ag_finish(srcs, lands, sems1, sems2, after, name):
    n = len(srcs)
    flat1 = [s for t in range(n) for k in range(4) for s in sems1[t][k]]
    flat2 = [s for t in range(n) for j in range(3) for s in sems2[t][j]]
    n1, n2 = len(flat1), len(flat2)

    def body(*refs):
        src, land = refs[:n], refs[n:2 * n]
        s1 = refs[2 * n:2 * n + n1]
        s2 = refs[2 * n + n1:2 * n + n1 + n2]
        x, y, c, chips = _place()
        sib = (x, y, 1 - c)
        for t in range(n):
            own = land[t].at[4 * x + 2 * y + 1 - c]
            pltpu.make_async_remote_copy(
                src_ref=src[t], dst_ref=own, send_sem=s1[8 * t], recv_sem=s1[8 * t + 1],
                device_id=sib, device_id_type=MESH).wait_recv()
            for k in range(4):
                pltpu.make_async_remote_copy(
                    src_ref=src[t], dst_ref=own, send_sem=s1[2 * (4 * t + k)], recv_sem=s1[2 * (4 * t + k) + 1],
                    device_id=sib, device_id_type=MESH).wait_send()
            for j, (cx, cy) in enumerate(chips):
                blk = land[t].at[4 * cx + 2 * cy + 1 - c]
                cp = pltpu.make_async_remote_copy(
                    src_ref=blk, dst_ref=blk, send_sem=s2[2 * (3 * t + j)], recv_sem=s2[2 * (3 * t + j) + 1],
                    device_id=sib, device_id_type=MESH)
                cp.wait_send()
                cp.wait_recv()

    return pl.pallas_call(
        body, name=name,
        in_specs=[HBM_SPEC] * (2 * n) + [SEM_SPEC] * (n1 + n2) + [ANY],
        out_specs=[HBM_SPEC] * n,
        out_shape=_hbm_like(lands),
        input_output_aliases={n + i: i for i in range(n)},
        compiler_params=pltpu.CompilerParams(has_side_effects=DATAFLOW),
    )(*srcs, *lands, *flat1, *flat2, after)


def _pair_copies(srcs, dsts, sems):
    x, y, c, _ = _place()
    nt = len(srcs)
    return [pltpu.make_async_remote_copy(
        src_ref=srcs[t].at[2 * j + 1 - c], dst_ref=dsts[t].at[j],
        send_sem=sems[2 * (NCHIP * t + j)], recv_sem=sems[2 * (NCHIP * t + j) + 1],
        device_id=(x, y, 1 - c), device_id_type=MESH) for t in range(nt) for j in range(NCHIP)]


def _pair_start(grads, carry, name):
    nt = len(grads)
    ns = 2 * NCHIP * nt
    zones = [_hbm(lax.empty((NCHIP,) + a.shape[1:], a.dtype)) for a in grads]
    extra = [] if carry is None else [_hbm(carry)]
    ne = len(extra)

    def body(*refs):
        for cp in _pair_copies(refs[:nt], refs[nt:2 * nt], refs[2 * nt + ne:2 * nt + ne + ns]):
            cp.start()

    res = pl.pallas_call(
        body, name=name,
        in_specs=[HBM_SPEC] * (2 * nt + ne),
        out_specs=[SEM_SPEC] * ns + [HBM_SPEC] * (2 * nt + ne),
        out_shape=[pltpu.SemaphoreType.DMA(())] * ns + _hbm_like(grads) + _hbm_like(zones) + _hbm_like(extra),
        input_output_aliases={i: ns + i for i in range(2 * nt + ne)},
        compiler_params=pltpu.CompilerParams(has_side_effects=DATAFLOW),
    )(*[_hbm(a) for a in grads], *zones, *extra)
    handle = (list(res[:ns]), list(res[ns:ns + nt]), list(res[ns + nt:ns + 2 * nt]))
    return handle, (res[ns + 2 * nt] if ne else None)


def _pair_wait(handle, after, name):
    sems, srcs, zones = handle
    nt, ns = len(srcs), len(sems)

    def body(*refs):
        for cp in _pair_copies(refs[:nt], refs[nt:2 * nt], refs[2 * nt:2 * nt + ns]):
            cp.wait_send()
            cp.wait_recv()

    return pl.pallas_call(
        body, name=name,
        in_specs=[HBM_SPEC] * (2 * nt) + [SEM_SPEC] * ns + [ANY],
        out_specs=[HBM_SPEC] * nt,
        out_shape=_hbm_like(zones),
        input_output_aliases={nt + i: i for i in range(nt)},
        compiler_params=pltpu.CompilerParams(has_side_effects=DATAFLOW),
    )(*srcs, *zones, *sems, after)


def _rows_tile(r, row_bytes, cap_bytes):
    best = None
    for tr in range(16, r + 1, 16):
        if r % tr == 0 and tr * row_bytes <= cap_bytes:
            best = tr
    return best if best is not None else r


def _pair_sum(own, got, cidx, name):
    _, _, r, cdim = own.shape
    tr = _rows_tile(r, 2 * cdim, 2 * 1024 * 1024)

    def body(c_ref, a_ref, b_ref, o_ref):
        o_ref[...] = (a_ref[...].astype(F32) + b_ref[...].astype(F32)).astype(BF16)

    return pl.pallas_call(
        body, name=name,
        grid_spec=pltpu.PrefetchScalarGridSpec(
            num_scalar_prefetch=1, grid=(NCHIP, r // tr),
            in_specs=[pl.BlockSpec((None, None, tr, cdim), lambda j, i, c_ref: (j, c_ref[0], i, 0)),
                      pl.BlockSpec((None, tr, cdim), lambda j, i, c_ref: (j, i, 0))],
            out_specs=pl.BlockSpec((None, tr, cdim), lambda j, i, c_ref: (j, i, 0))),
        out_shape=jax.ShapeDtypeStruct((NCHIP, r, cdim), BF16),
        compiler_params=_params("arbitrary", "arbitrary"),
    )(cidx, own, got)


def _chip_copies(srcs, zones, slots, sems):
    x, y, c, chips = _place()
    out = []
    for t, (z, l) in enumerate(slots):
        for k, (cx, cy) in enumerate(chips):
            dst = zones[z].at[k] if l is None else zones[z].at[k, l]
            out.append(pltpu.make_async_remote_copy(
                src_ref=srcs[t].at[2 * cx + cy], dst_ref=dst,
                send_sem=sems[2 * (3 * t + k)], recv_sem=sems[2 * (3 * t + k) + 1],
                device_id=(cx, cy, c), device_id_type=MESH))
    return out


def _chip_start(sums, zones, slots, carry, name):
    nt, nz = len(sums), len(zones)
    ns = 6 * nt
    extra = [] if carry is None else [_hbm(carry)]
    ne = len(extra)

    def body(*refs):
        for cp in _chip_copies(refs[:nt], refs[nt:nt + nz], slots, refs[nt + nz + ne:nt + nz + ne + ns]):
            cp.start()

    res = pl.pallas_call(
        body, name=name,
        in_specs=[HBM_SPEC] * (nt + nz + ne),
        out_specs=[SEM_SPEC] * ns + [HBM_SPEC] * (nt + nz + ne),
        out_shape=[pltpu.SemaphoreType.DMA(())] * ns + _hbm_like(sums) + _hbm_like(zones) + _hbm_like(extra),
        input_output_aliases={i: ns + i for i in range(nt + nz + ne)},
        compiler_params=pltpu.CompilerParams(has_side_effects=DATAFLOW),
    )(*[_hbm(a) for a in sums], *zones, *extra)
    return (list(res[:ns]), list(res[ns:ns + nt]), list(res[ns + nt:ns + nt + nz]),
            (res[ns + nt + nz] if ne else None))


def _chip_wait(started, zones, zone_ids, after, name):
    started = [(sums, [(zone_ids.index(z), l) for z, l in slots], sems) for sums, slots, sems in started]
    nz = len(zones)
    flat_src = [a for sums, _, _ in started for a in sums]
    flat_sem = [s for _, _, sems in started for s in sems]
    n_src, n_sem = len(flat_src), len(flat_sem)

    def body(*refs):
        srcs, zs, sems = refs[:n_src], refs[n_src:n_src + nz], refs[n_src + nz:n_src + nz + n_sem]
        so, se = 0, 0
        for sums, slots, sem_list in started:
            for cp in _chip_copies(srcs[so:so + len(sums)], zs, slots, sems[se:se + len(sem_list)]):
                cp.wait_send()
                cp.wait_recv()
            so += len(sums)
            se += len(sem_list)

    return pl.pallas_call(
        body, name=name,
        in_specs=[HBM_SPEC] * (n_src + nz) + [SEM_SPEC] * n_sem + [ANY],
        out_specs=[HBM_SPEC] * nz,
        out_shape=_hbm_like(zones),
        input_output_aliases={n_src + i: i for i in range(nz)},
        compiler_params=pltpu.CompilerParams(has_side_effects=DATAFLOW),
    )(*flat_src, *zones, *flat_sem, after)


def _small_allreduce(parts, y_first, after, name):
    nt = len(parts)

    def body(*refs):
        srcs, outs, bufs = refs[:nt], refs[nt + 1:2 * nt + 1], refs[2 * nt + 1:3 * nt + 1]
        send_sems, recv_sems = refs[3 * nt + 1:]
        x, y, c, _ = _place()
        along = {"c": (x, y, 1 - c), "x": (1 - x, y, c), "y": (x, 1 - y, c)}
        for t in range(nt):
            outs[t][...] = srcs[t][...]
        for step in range(3):
            order = [("c", "y", "x") if t in y_first else ("c", "x", "y") for t in range(nt)]
            copies = [pltpu.make_async_remote_copy(
                src_ref=outs[t], dst_ref=bufs[t].at[step],
                send_sem=send_sems.at[step, t], recv_sem=recv_sems.at[step, t],
                device_id=along[order[t][step]], device_id_type=MESH) for t in range(nt)]
            for cp in copies:
                cp.start()
            for cp in copies:
                cp.wait()
            for t in range(nt):
                outs[t][...] = outs[t][...] + bufs[t][step]

    vm = pl.BlockSpec(memory_space=pltpu.VMEM)
    return pl.pallas_call(
        body, name=name,
        in_specs=[vm] * nt + [ANY], out_specs=[vm] * nt,
        out_shape=[jax.ShapeDtypeStruct(a.shape, F32) for a in parts],
        scratch_shapes=[pltpu.VMEM((3,) + a.shape, F32) for a in parts]
        + [pltpu.SemaphoreType.DMA((3, nt)), pltpu.SemaphoreType.DMA((3, nt))],
        compiler_params=pltpu.CompilerParams(has_side_effects=True, vmem_limit_bytes=VMEM_LIMIT),
    )(*parts, after)


def _adam_math(w, g, m, v):
    m2 = ADAM_B1 * m + (1.0 - ADAM_B1) * g
    v2 = ADAM_B2 * v + (1.0 - ADAM_B2) * (g * g)
    m_hat = m2 / (1.0 - ADAM_B1 ** ADAM_STEP)
    v_hat = v2 / (1.0 - ADAM_B2 ** ADAM_STEP)
    delta = -ADAM_LR * (m_hat / (jnp.sqrt(v_hat) + ADAM_EPS) + ADAM_WD * w)
    return delta, m2, v2


def _adam_big(w, m, v, parts, mine, chip, name):
    nl, r, cdim = w.shape
    tr = _rows_tile(r, 4 * cdim, 3 * 512 * 1024)

    def body(c_ref, w_ref, m_ref, v_ref, p_ref, *rest):
        mine_refs, (g_ref, d_ref, mo_ref, vo_ref) = rest[:nl], rest[nl:]
        own = mine_refs[0][...]
        for l in range(1, nl):
            own = jnp.where(pl.program_id(0) == l, mine_refs[l][...], own)
        g = ((p_ref[0].astype(F32) + p_ref[1].astype(F32)) + p_ref[2].astype(F32)) + own.astype(F32)
        delta, m2, v2 = _adam_math(w_ref[...], g, m_ref[...], v_ref[...])
        g_ref[...] = g
        d_ref[...] = delta
        mo_ref[...] = m2
        vo_ref[...] = v2

    spec = pl.BlockSpec((None, tr, cdim), lambda l, i, c_ref: (l, i, 0))
    mine_specs = [pl.BlockSpec((None, tr, cdim), lambda l, i, c_ref, ll=ll: (c_ref[0], jnp.where(l == ll, i, 0), 0))
                  for ll in range(nl)]
    return pl.pallas_call(
        body, name=name,
        grid_spec=pltpu.PrefetchScalarGridSpec(
            num_scalar_prefetch=1, grid=(nl, r // tr),
            in_specs=[spec, spec, spec, pl.BlockSpec((3, None, tr, cdim), lambda l, i, c_ref: (0, l, i, 0))]
            + mine_specs,
            out_specs=[spec] * 4),
        out_shape=[jax.ShapeDtypeStruct(w.shape, F32)] * 4,
        compiler_params=_params("arbitrary", "arbitrary"),
    )(chip, w, m, v, parts, *mine)


def _adam_small(ws, gs, ms, vs, name):
    n = len(ws)

    def body(*refs):
        w_r, g_r, m_r, v_r = refs[:n], refs[n:2 * n], refs[2 * n:3 * n], refs[3 * n:4 * n]
        d_o, m_o, v_o = refs[4 * n:5 * n], refs[5 * n:6 * n], refs[6 * n:7 * n]
        for t in range(n):
            delta, m2, v2 = _adam_math(w_r[t][...], g_r[t][...], m_r[t][...], v_r[t][...])
            d_o[t][...] = delta
            m_o[t][...] = m2
            v_o[t][...] = v2

    vm = pl.BlockSpec(memory_space=pltpu.VMEM)
    shapes = [jax.ShapeDtypeStruct(a.shape, F32) for a in ws]
    return pl.pallas_call(
        body, name=name, in_specs=[vm] * (4 * n), out_specs=[vm] * (3 * n), out_shape=shapes * 3,
        compiler_params=pltpu.CompilerParams(vmem_limit_bytes=VMEM_LIMIT),
    )(*ws, *gs, *ms, *vs)


def kernel(x, norm_mix, norm_ffn, norm_final, ab_w_in, a_ln_g, a_ln_b, a_w_s, a_b_s, b_conv_w, b_conv_b, b_ln_g, b_ln_b, ab_w_out, c_w_in, c_conv_w, c_w_out, f_w_up, f_conv_w, f_w_down, loss_target, m_norm_mix, m_norm_ffn, m_norm_final, m_ab_w_in, m_a_ln_g, m_a_ln_b, m_a_w_s, m_a_b_s, m_b_conv_w, m_b_conv_b, m_b_ln_g, m_b_ln_b, m_ab_w_out, m_c_w_in, m_c_conv_w, m_c_w_out, m_f_w_up, m_f_conv_w, m_f_w_down, v_norm_mix, v_norm_ffn, v_norm_final, v_ab_w_in, v_a_ln_g, v_a_ln_b, v_a_w_s, v_a_b_s, v_b_conv_w, v_b_conv_b, v_b_ln_g, v_b_ln_b, v_ab_w_out, v_c_w_in, v_c_conv_w, v_c_w_out, v_f_w_up, v_f_conv_w, v_f_w_down):
    s = x.shape[1]
    x0 = x.reshape(s, D)
    tgt = loss_target.reshape(s, D)
    xi, yi, ci = lax.axis_index("x"), lax.axis_index("y"), lax.axis_index("c")
    dev = 4 * xi + 2 * yi + ci
    cidx = ci.astype(jnp.int32).reshape(1)

    bf = lambda a: a.astype(BF16)
    slab_w = 6 * CHUNK
    pad = lambda a, rows: jnp.pad(a, ((0, rows - a.shape[0]), (0, slab_w - a.shape[1])))
    slab = jnp.concatenate([pad(b_conv_w[0], 32), pad(c_conv_w[0], 8), pad(f_conv_w.reshape(6, FB), 8)], axis=0)
    later = [bf(ab_w_in[0]), bf(ab_w_out[0]), slab, bf(f_w_up[0]), bf(f_w_down[0]), bf(c_w_in[0]), bf(c_w_out[0]),
             bf(f_w_up[1]), bf(f_w_down[1])]
    lands = [_zone(a, dev) for a in later]
    groups = [[0], [1, 2], [3, 4], [5, 6], [7, 8]]
    ag_sems, later, lands, ag_token = _ag_start(later, lands, x0, "ag_start")

    causal = jnp.tril(jnp.ones((CHUNK, CHUNK), F32))
    wsm = (a_w_s[0] * causal).astype(BF16)
    bs_col = a_b_s.reshape(HEADS, CHUNK, 1)
    nm = [norm_mix[0:1], norm_mix[1:2]]
    nf = [norm_ffn[0:1], norm_ffn[1:2]]
    nfin = norm_final.reshape(1, D)

    def arrive(g, after_ici, after_d2d, tag):
        srcs = [later[t] for t in groups[g]]
        zone = [lands[t] for t in groups[g]]
        sems1 = [ag_sems[t] for t in groups[g]]
        sems2, zone = _ag_forward(srcs, zone, sems1, after_ici, "ag_forward_" + tag)
        return _ag_finish(srcs, zone, sems1, sems2, after_d2d, "ag_finish_" + tag)

    h0 = _rms_fwd(x0, nm[0], "rms_mix0", after=ag_token)
    (win0,) = arrive(0, h0, h0, "w_in")
    z = _mm_in(h0, win0, "mm_ab_in")
    wout0, slab_g = arrive(1, z, z, "first")
    wout0 = wout0.reshape(D, D)
    bcw = jnp.transpose(slab_g[:, 0:BCONV, 0:DA // NDEV], (1, 0, 2)).reshape(BCONV, DA)
    ccw = jnp.transpose(slab_g[:, 32:35, 0:D // NDEV], (1, 0, 2)).reshape(3, D)
    fcw_g = slab_g[:, 40:46, 0:FB].reshape(2, NG, 2, 3, FB)
    fcws = [fcw_g[:, :, 0], fcw_g[:, :, 1]]
    ycat, yb2 = _ab_fwd(z, a_ln_g, a_ln_b, wsm, bs_col, bcw, b_conv_b, b_ln_g, b_ln_b, "ab_fwd")
    x1, h1 = _mm_out(ycat, wout0, x0, nf[0], "mm_ab_out")
    wup0, wdn0 = arrive(2, x1, x1, "ffn0")
    up0, upc0, x2, h2 = _ffn_fwd(h1, x1, wup0.reshape(2, NG, D, FB), fcws[0], wdn0.reshape(DFF, D), nm[1],
                                 "ffn_fwd0")
    cin, cout = arrive(3, x2, x2, "c")
    cout = cout.reshape(D, D)
    zc = _mm_in(h2, cin, "mm_c_in")
    rc = _c_fwd(zc, ccw, "c_fwd")
    x3, h3 = _mm_out(rc, cout, x2, nf[1], "mm_c_out")
    wup1, wdn1 = arrive(4, rc, x3, "ffn1")
    wups = [wup0.reshape(2, NG, D, FB), wup1.reshape(2, NG, D, FB)]
    wdns = [wdn0.reshape(DFF, D), wdn1.reshape(DFF, D)]
    up1, upc1, x4 = _ffn_fwd(h3, x3, wups[1], fcws[1], wdns[1], None, "ffn_fwd1")
    dx4, dx4b, dnfin, loss_part = _final(x4, tgt, nfin, "final_loss")

    zshape = lambda *sh: _hbm(lax.empty((3,) + sh, BF16))
    zones = [zshape(D, 2 * D // NDEV), zshape(D // NDEV, D), zshape(D, 3 * D // NDEV), zshape(D // NDEV, D),
             zshape(2, FB, D), zshape(2, DFF // NDEV, D)]
    started = []

    def pair_sums(grads, handle, after, tag):
        del grads
        got = _pair_wait(handle, after, "rs_pair_wait_" + tag)
        return [_pair_sum(b.reshape((NCHIP, 2) + b.shape[1:]), g, cidx, "rs_pair_sum_%s%d" % (tag, t))
                for t, (b, g) in enumerate(zip(handle[1], got))]

    def chip_start(sums, slots, carry, tag):
        sems, sums, new_zones, carry = _chip_start(sums, zones, slots, carry, "rs_chip_start_" + tag)
        zones[:] = new_zones
        started.append((sums, slots, sems))
        return sums, carry

    rows8 = lambda g, r: g.reshape(NDEV, r, D)
    a1, dup1, dx3, dx3b, dnf1, dfcw1 = _ffn_bwd(dx4, up1, upc1, wups[1], fcws[1], wdns[1], x3, nf[1], "ffn_bwd1")
    g_f1 = [_dw_up(h3, dup1, "dw_up1"), rows8(_dw_dn(a1, dx4b, "dw_dn1"), DFF // NDEV)]
    hd_f1, dx3b = _pair_start(g_f1, dx3b, "rs_pair_start_f1")
    drc = _mm_nt(dx3b, cout, "mm_c_out_bwd")
    g_cout = rows8(_dw_rows(rc, dx3b, "dw_c_out"), D // NDEV)
    s_f1 = pair_sums(g_f1, hd_f1, g_cout, "f1")
    s_f1, drc = chip_start(s_f1, [(4, 1), (5, 1)], drc, "f1")
    dzc, dccw = _c_bwd(drc, zc, ccw, "c_bwd")
    dx2, dx2b, dnm1 = _mm_nt_rms(dzc, cin, x2, nm[1], dx3, True, "mm_c_in_bwd")
    g_c = [_dw_cols(h2, dzc, NDEV, 3 * D // NDEV, "dw_c_in"), g_cout]
    hd_c, dx2 = _pair_start(g_c, dx2, "rs_pair_start_c")
    a0, dup0, dx1, dx1b, dnf0, dfcw0 = _ffn_bwd(dx2, up0, upc0, wups[0], fcws[0], wdns[0], x1, nf[0], "ffn_bwd0")
    s_c = pair_sums(g_c, hd_c, dx1b, "c")
    s_c, dx1b = chip_start(s_c, [(2, None), (3, None)], dx1b, "c")
    g_f0 = [_dw_up(h1, dup0, "dw_up0"), rows8(_dw_dn(a0, dx2b, "dw_dn0"), DFF // NDEV)]
    hd_f0, dx1b = _pair_start(g_f0, dx1b, "rs_pair_start_f0")
    dycat = _mm_nt(dx1b, wout0, "mm_ab_out_bwd")
    g_wout0 = rows8(_dw_rows(ycat, dx1b, "dw_ab_out"), D // NDEV)
    s_f0 = pair_sums(g_f0, hd_f0, g_wout0, "f0")
    s_f0, dycat = chip_start(s_f0, [(4, 0), (5, 0)], dycat, "f0")
    dz, g512, dws, dbs = _ab_bwd(dycat, z, yb2, a_ln_g, a_ln_b, wsm, bs_col, bcw, b_ln_g, b_ln_b, "ab_bwd")
    grad_x, dnm0 = _mm_nt_rms(dz, win0, x0, nm[0], dx1, False, "mm_ab_in_bwd")
    g_ab = [_dw_cols(h0, dz, NDEV, 2 * D // NDEV, "dw_ab_in"), g_wout0]
    hd_ab, _ = _pair_start(g_ab, None, "rs_pair_start_ab")

    g1024 = jnp.concatenate([dnm0, dnm1, dnf0, dnf1, dnfin, dccw], axis=0)
    gfc = jnp.concatenate([dfcw0, dfcw1], axis=0).reshape(2 * NG * 2 * 3, FB)
    g1024, g512, dws, dbs, gfc, loss_sum = _small_allreduce(
        [g1024, g512, dws.reshape(HEADS * CHUNK, CHUNK), dbs.reshape(HEADS, CHUNK), gfc, loss_part], (2,),
        hd_ab[1][0], "small_allreduce")
    loss = loss_sum[0, 0]
    s_ab = pair_sums(g_ab, hd_ab, g1024, "ab")
    s_ab, _ = chip_start(s_ab, [(0, None), (1, None)], None, "ab")
    p_cin, p_cout, p_wup, p_wdn = _chip_wait(started[:3], zones[2:], [2, 3, 4, 5], s_ab[0], "rs_chip_wait_early")

    chip = (2 * xi + yi).astype(jnp.int32).reshape(1)

    def big_update(w, m, v, parts, mine, name):
        shp = w.shape
        w3, m3, v3 = (a.reshape((-1,) + shp[-2:]) for a in (w, m, v))
        p4 = parts.reshape((3,) + w3.shape)
        return [o.reshape(shp) for o in _adam_big(w3, m3, v3, p4, mine, chip, name)]

    u_cin = big_update(c_w_in, m_c_w_in, v_c_w_in, p_cin, [s_c[0]], "adam_c_w_in")
    u_cout = big_update(c_w_out, m_c_w_out, v_c_w_out, p_cout, [s_c[1]], "adam_c_w_out")
    tr_ = lambda a: jnp.swapaxes(a, 1, 2)
    u_wup = [tr_(o) for o in big_update(tr_(f_w_up), tr_(m_f_w_up), tr_(v_f_w_up), p_wup,
                                        [s_f0[0], s_f1[0]], "adam_f_w_up")]
    u_wdn = big_update(f_w_down, m_f_w_down, v_f_w_down, p_wdn, [s_f0[1], s_f1[1]], "adam_f_w_down")
    p_win0, p_wout0 = _chip_wait(started[3:], zones[:2], [0, 1], u_wdn[0], "rs_chip_wait_late")
    u_win0 = big_update(ab_w_in, m_ab_w_in, v_ab_w_in, p_win0, [s_ab[0]], "adam_ab_w_in")
    u_wout0 = big_update(ab_w_out, m_ab_w_out, v_ab_w_out, p_wout0, [s_ab[1]], "adam_ab_w_out")

    g_norm_mix = g1024[0:2]
    g_norm_ffn = g1024[2:4]
    g_norm_final = g1024[4:5]
    g_ccw = lax.dynamic_slice(g1024[5:8], (0, dev * (D // NDEV)), (3, D // NDEV))
    g_bcw = lax.dynamic_slice(g512[8:8 + BCONV], (0, dev * (DA // NDEV)), (BCONV, DA // NDEV))
    gfc = gfc.reshape(2, NG, 2, 3, FB)
    g_fcw = lax.dynamic_slice(gfc, (0, dev % NG, dev // NG, 0, 0), (2, 1, 1, 3, FB)).reshape(2, 3, FB)
    small_w = [norm_mix, norm_ffn, nfin, a_ln_g, a_ln_b, a_w_s[0], a_b_s[0], b_conv_w[0], b_conv_b,
               b_ln_g, b_ln_b, c_conv_w[0], f_conv_w]
    small_g = [g_norm_mix, g_norm_ffn, g_norm_final, g512[0:1], g512[1:2],
               dws.reshape(HEADS, CHUNK, CHUNK), dbs, g_bcw, g512[2:3],
               g512[3:4], g512[4:5], g_ccw, g_fcw]
    small_m = [m_norm_mix, m_norm_ffn, m_norm_final.reshape(1, D), m_a_ln_g, m_a_ln_b, m_a_w_s[0], m_a_b_s[0],
               m_b_conv_w[0], m_b_conv_b, m_b_ln_g, m_b_ln_b, m_c_conv_w[0], m_f_conv_w]
    small_v = [v_norm_mix, v_norm_ffn, v_norm_final.reshape(1, D), v_a_ln_g, v_a_ln_b, v_a_w_s[0], v_a_b_s[0],
               v_b_conv_w[0], v_b_conv_b, v_b_ln_g, v_b_ln_b, v_c_conv_w[0], v_f_conv_w]
    upd = _adam_small(small_w, small_g, small_m, small_v, "adam_small")
    ns = len(small_w)
    orig = [norm_mix, norm_ffn, norm_final, a_ln_g, a_ln_b, a_w_s, a_b_s, b_conv_w, b_conv_b,
            b_ln_g, b_ln_b, c_conv_w, f_conv_w]
    sg_out = [g.reshape(o.shape) for g, o in zip(small_g, orig)]
    sd_out = [a.reshape(o.shape) for a, o in zip(upd[0:ns], orig)]
    sm_out = [a.reshape(o.shape) for a, o in zip(upd[ns:2 * ns], orig)]
    sv_out = [a.reshape(o.shape) for a, o in zip(upd[2 * ns:3 * ns], orig)]

    def assemble(small, k):
        return [small[0], small[1], small[2], u_win0[k], small[3], small[4], small[5], small[6], small[7],
                small[8], small[9], small[10], u_wout0[k], u_cin[k], small[11], u_cout[k], u_wup[k],
                small[12], u_wdn[k]]

    grads = assemble(sg_out, 0)
    deltas = assemble(sd_out, 1)
    new_m = assemble(sm_out, 2)
    new_v = assemble(sv_out, 3)
    return (loss, grad_x.reshape(1, s, D), *grads, *deltas, *new_m, *new_v)
```

```python
import math

import jax
import jax.numpy as jnp
from jax import lax
from jax.experimental import pallas as pl
from jax.experimental.pallas import tpu as pltpu

F32 = jnp.float32
BF16 = jnp.bfloat16

D = 1024
DA = 512
HEADS = 4
CHUNK = 128
DFF = 2816
NDEV = 8
NCHIP = 4
FB = DFF * 2 // NDEV
NG = DFF // FB
BCONV = 31
EPS = 1e-6
HALO = 16
HALO_B = 32
RC = 32
NPART = 2
VMEM_LIMIT = 52 * 1024 * 1024
INV_SQRT2 = 1.0 / math.sqrt(2.0)
INV_SQRT_2PI = 1.0 / math.sqrt(2.0 * math.pi)

ADAM_LR = 0.001
ADAM_B1 = 0.9
ADAM_B2 = 0.999
ADAM_EPS = 1e-08
ADAM_WD = 0.01
ADAM_STEP = 10

MESH = pl.DeviceIdType.MESH
ANY = pl.BlockSpec(memory_space=pl.ANY)
NT_DIMS = (((1,), (1,)), ((), ()))
TN_DIMS = (((0,), (0,)), ((), ()))


def _params(*sem):
    return pltpu.CompilerParams(dimension_semantics=sem, vmem_limit_bytes=VMEM_LIMIT)


def _tile(s, want):
    return min(want, s)


def _sigmoid(x):
    return jax.nn.sigmoid(x)


def _dsilu(x, sg):
    return sg * (1.0 + x * (1.0 - sg))


def _gelu(x):
    return 0.5 * x * (1.0 + lax.erf(x * INV_SQRT2))


def _dgelu(x):
    return 0.5 * (1.0 + lax.erf(x * INV_SQRT2)) + x * jnp.exp(-0.5 * x * x) * INV_SQRT_2PI


def _ln_fwd(x, g, b):
    mu = jnp.mean(x, axis=-1, keepdims=True)
    xc = x - mu
    var = jnp.mean(xc * xc, axis=-1, keepdims=True)
    rstd = lax.rsqrt(var + EPS)
    xhat = xc * rstd
    return xhat * g + b, xhat, rstd


def _ln_bwd(dy, xhat, rstd, g):
    dxh = dy * g
    m1 = jnp.mean(dxh, axis=-1, keepdims=True)
    m2 = jnp.mean(dxh * xhat, axis=-1, keepdims=True)
    return rstd * (dxh - m1 - xhat * m2)


def _rms_bwd_math(dh, x, g):
    r = lax.rsqrt(jnp.mean(x * x, axis=-1, keepdims=True) + EPS)
    xhat = x * r
    dg = jnp.sum(dh * xhat, axis=0, keepdims=True)
    u = dh * g
    dx = r * (u - xhat * jnp.mean(u * xhat, axis=-1, keepdims=True))
    return dx, dg


def _conv3(xe, cw, halo):
    x0 = xe[halo:]
    x1 = pltpu.roll(xe, 1, 0)[halo:]
    x2 = pltpu.roll(xe, 2, 0)[halo:]
    return cw[2] * x0 + cw[1] * x1 + cw[0] * x2, (x0, x1, x2)


def _conv3_bwd_in(dce, cw, ts):
    n = dce.shape[0]
    d1 = pltpu.roll(dce, n - 1, 0)[:ts]
    d2 = pltpu.roll(dce, n - 2, 0)[:ts]
    return cw[2] * dce[:ts] + cw[1] * d1 + cw[0] * d2


def _conv3_bwd_w(dc, taps):
    x0, x1, x2 = taps
    return [jnp.sum(dc * x2, axis=0, keepdims=True), jnp.sum(dc * x1, axis=0, keepdims=True),
            jnp.sum(dc * x0, axis=0, keepdims=True)]


def _rms_fwd(x, g, name, after=None):
    s = x.shape[0]
    ts = _tile(s, 512)

    def body(x_ref, g_ref, *rest):
        h_ref = rest[-1]
        xv = x_ref[...]
        r = lax.rsqrt(jnp.mean(xv * xv, axis=-1, keepdims=True) + EPS)
        h_ref[...] = (xv * r * g_ref[...]).astype(BF16)

    extra = [] if after is None else [after]
    return pl.pallas_call(
        body, grid=(s // ts,), name=name,
        in_specs=[pl.BlockSpec((ts, D), lambda i: (i, 0)), pl.BlockSpec((1, D), lambda i: (0, 0))]
        + [ANY] * len(extra),
        out_specs=pl.BlockSpec((ts, D), lambda i: (i, 0)),
        out_shape=jax.ShapeDtypeStruct((s, D), BF16),
        compiler_params=_params("parallel"),
    )(x, g, *extra)


MXU_COLS = 256


def _pair(bn):
    return 1 if bn % MXU_COLS == 0 else 2


def _cols(w_ref, b, pair):
    return w_ref[b] if pair == 1 else jnp.concatenate([w_ref[b + q] for q in range(pair)], axis=1)


def _mm_in(h, wblk, name):
    s = h.shape[0]
    nb, _, bn = wblk.shape
    pair = _pair(bn)
    ts = _tile(s, 1024)

    def body(h_ref, w_ref, o_ref):
        hv = h_ref[...]
        for b in range(0, nb, pair):
            o_ref[:, b * bn:(b + pair) * bn] = jnp.dot(hv, _cols(w_ref, b, pair),
                                                       preferred_element_type=F32).astype(BF16)

    return pl.pallas_call(
        body, grid=(s // ts,), name=name,
        in_specs=[pl.BlockSpec((ts, D), lambda i: (i, 0)), pl.BlockSpec((nb, D, bn), lambda i: (0, 0, 0))],
        out_specs=pl.BlockSpec((ts, nb * bn), lambda i: (i, 0)),
        out_shape=jax.ShapeDtypeStruct((s, nb * bn), BF16),
        compiler_params=_params("parallel"),
    )(h, wblk)


def _rms_math(xv, g):
    r = lax.rsqrt(jnp.mean(xv * xv, axis=-1, keepdims=True) + EPS)
    return (xv * r * g).astype(BF16)


def _mm_out(y, w, xres, gnext, name):
    s = y.shape[0]
    ts = _tile(s, 1024)

    def body(y_ref, w_ref, x_ref, g_ref, o_ref, h_ref):
        xn = x_ref[...] + jnp.dot(y_ref[...], w_ref[...], preferred_element_type=F32)
        o_ref[...] = xn
        h_ref[...] = _rms_math(xn, g_ref[...])

    return pl.pallas_call(
        body, grid=(s // ts,), name=name,
        in_specs=[pl.BlockSpec((ts, D), lambda i: (i, 0)), pl.BlockSpec((D, D), lambda i: (0, 0)),
                  pl.BlockSpec((ts, D), lambda i: (i, 0)), pl.BlockSpec((1, D), lambda i: (0, 0))],
        out_specs=[pl.BlockSpec((ts, D), lambda i: (i, 0)), pl.BlockSpec((ts, D), lambda i: (i, 0))],
        out_shape=[jax.ShapeDtypeStruct((s, D), F32), jax.ShapeDtypeStruct((s, D), BF16)],
        compiler_params=_params("parallel"),
    )(y, w, xres, gnext)


CONV_ROWS = 32


def _rolled_copies(dst_ref, xe, back):
    n = xe.shape[0]
    dst_ref[0] = xe
    for r in range(1, 8):
        dst_ref[r] = pltpu.roll(xe, n - r if back else r, 0)


def _conv31(rolled_ref, cw_ref, ts, out_ref, bias):
    for o in range(0, ts, CONV_ROWS):
        acc = jnp.zeros((CONV_ROWS, DA), F32) + bias
        for sh in range(BCONV):
            q, r = divmod(sh, 8)
            lo = HALO_B - 8 * q + o
            acc = acc + cw_ref[BCONV - 1 - sh:BCONV - sh, :] * rolled_ref[r, lo:lo + CONV_ROWS, :]
        out_ref[o:o + CONV_ROWS, :] = acc


def _ab_fwd(z, lga, lba, wsm, bs_col, cwb, cbb, lgb, lbb, name):
    s = z.shape[0]
    ts = _tile(s, 256)
    hb = ts // HALO_B

    def body(z_ref, zh_ref, lga_ref, lba_ref, ws_ref, bs_ref, cw_ref, cb_ref, lgb_ref, lbb_ref,
             y_ref, yb2_ref, rolled):
        i = pl.program_id(0)
        z_t = z_ref[...].astype(F32)
        gu = _gelu(z_t[:, 0:DA])
        gv = _gelu(z_t[:, DA:2 * DA])
        vn, _, _ = _ln_fwd(gv, lga_ref[...], lba_ref[...])
        vnb = vn.astype(BF16)
        for c in range(ts // CHUNK):
            for h in range(HEADS):
                rs = slice(c * CHUNK, (c + 1) * CHUNK)
                cs = slice(h * CHUNK, (h + 1) * CHUNK)
                mixed = jnp.dot(ws_ref[h], vnb[rs, cs], preferred_element_type=F32) + bs_ref[h]
                y_ref[rs, cs] = (gu[rs, cs] * mixed).astype(BF16)
        zh = jnp.where(i > 0, zh_ref[...], jnp.zeros_like(zh_ref[...])).astype(F32)
        xb = jnp.concatenate([zh[:, 0:DA], z_t[:, 2 * DA:3 * DA]], axis=0)
        gb = jnp.concatenate([zh[:, DA:2 * DA], z_t[:, 3 * DA:4 * DA]], axis=0)
        _rolled_copies(rolled, xb * _sigmoid(gb), False)
        _conv31(rolled, cw_ref, ts, yb2_ref, cb_ref[...])
        nb_, _, _ = _ln_fwd(yb2_ref[...], lgb_ref[...], lbb_ref[...])
        y_ref[:, DA:2 * DA] = (nb_ * _sigmoid(nb_)).astype(BF16)

    row = lambda i: (0, 0)
    return pl.pallas_call(
        body, grid=(s // ts,), name=name,
        in_specs=[pl.BlockSpec((ts, 4 * DA), lambda i: (i, 0)),
                  pl.BlockSpec((HALO_B, 2 * DA), lambda i: (jnp.maximum(i * hb - 1, 0), 1)),
                  pl.BlockSpec((1, DA), row), pl.BlockSpec((1, DA), row),
                  pl.BlockSpec((HEADS, CHUNK, CHUNK), lambda i: (0, 0, 0)),
                  pl.BlockSpec((HEADS, CHUNK, 1), lambda i: (0, 0, 0)),
                  pl.BlockSpec((BCONV, DA), row), pl.BlockSpec((1, DA), row),
                  pl.BlockSpec((1, DA), row), pl.BlockSpec((1, DA), row)],
        out_specs=[pl.BlockSpec((ts, 2 * DA), lambda i: (i, 0)), pl.BlockSpec((ts, DA), lambda i: (i, 0))],
        out_shape=[jax.ShapeDtypeStruct((s, 2 * DA), BF16), jax.ShapeDtypeStruct((s, DA), F32)],
        scratch_shapes=[pltpu.VMEM((8, ts + HALO_B, DA), F32)],
        compiler_params=_params("parallel"),
    )(z, z, lga, lba, wsm, bs_col, cwb, cbb, lgb, lbb)


def _c_fwd(zc, cw, name):
    s = zc.shape[0]
    ts = _tile(s, 512)
    hb = ts // HALO

    def body(z_ref, ch_ref, xh_ref, cw_ref, r_ref):
        i = pl.program_id(0)
        z_t = z_ref[...].astype(F32)
        ph = jnp.where(i > 0, ch_ref[...].astype(F32) * xh_ref[...].astype(F32), 0.0)
        pe = jnp.concatenate([ph, z_t[:, D:2 * D] * z_t[:, 2 * D:3 * D]], axis=0)
        q, _ = _conv3(pe, [cw_ref[k:k + 1, :] for k in range(3)], HALO)
        r_ref[...] = (z_t[:, 0:D] * q).astype(BF16)

    halo = lambda col: pl.BlockSpec((HALO, D), lambda i: (jnp.maximum(i * hb - 1, 0), col))
    return pl.pallas_call(
        body, grid=(s // ts,), name=name,
        in_specs=[pl.BlockSpec((ts, 3 * D), lambda i: (i, 0)), halo(1), halo(2),
                  pl.BlockSpec((3, D), lambda i: (0, 0))],
        out_specs=pl.BlockSpec((ts, D), lambda i: (i, 0)),
        out_shape=jax.ShapeDtypeStruct((s, D), BF16),
        compiler_params=_params("parallel"),
    )(zc, zc, zc, cw)


def _ffn_fwd(h, xres, wup, fcw, wdn, gnext, name):
    s = h.shape[0]
    ts = _tile(s, 512)
    hb = ts // HALO

    def body(h_ref, hh_ref, w_ref, cw_ref, wd_ref, x_ref, *rest):
        if gnext is not None:
            gn_ref, up_ref, upc_ref, xo_ref, hn_ref, up_s = rest
        else:
            up_ref, upc_ref, xo_ref, up_s = rest
        i = pl.program_id(0)
        m = pl.program_id(1)
        @pl.when(m == 0)
        def _():
            xo_ref[...] = x_ref[...]

        halo = jnp.where(i > 0, hh_ref[...], jnp.zeros_like(hh_ref[...]))
        hx = jnp.concatenate([halo, h_ref[...]], axis=0)
        acts = []
        for gv in range(2):
            up_s[gv] = jnp.dot(hx, w_ref[gv], preferred_element_type=F32)
            x0 = up_s[gv, HALO:HALO + ts, :]
            up_ref[gv] = x0.astype(BF16)
            upc = (cw_ref[gv, 2:3, :] * x0 + cw_ref[gv, 1:2, :] * up_s[gv, HALO - 1:HALO - 1 + ts, :]
                   + cw_ref[gv, 0:1, :] * up_s[gv, HALO - 2:HALO - 2 + ts, :])
            upc_ref[gv] = upc.astype(BF16)
            acts.append(upc)
        a = acts[0] * _sigmoid(acts[0]) * acts[1]
        xo_ref[...] += jnp.dot(a.astype(BF16), wd_ref[...], preferred_element_type=F32)

        if gnext is not None:
            @pl.when(m == NG - 1)
            def _():
                hn_ref[...] = _rms_math(xo_ref[...], gn_ref[...])

    tile = pl.BlockSpec((ts, D), lambda i, m: (i, 0))
    nxt = gnext is not None
    return pl.pallas_call(
        body, grid=(s // ts, NG), name=name,
        in_specs=[tile,
                  pl.BlockSpec((HALO, D), lambda i, m: (jnp.maximum(i * hb - 1, 0), 0)),
                  pl.BlockSpec((2, None, D, FB), lambda i, m: (0, m, 0, 0)),
                  pl.BlockSpec((2, None, 3, FB), lambda i, m: (0, m, 0, 0)),
                  pl.BlockSpec((FB, D), lambda i, m: (m, 0)),
                  tile] + ([pl.BlockSpec((1, D), lambda i, m: (0, 0))] if nxt else []),
        out_specs=[pl.BlockSpec((None, 2, ts, FB), lambda i, m: (m, 0, i, 0)),
                   pl.BlockSpec((None, 2, ts, FB), lambda i, m: (m, 0, i, 0)),
                   tile] + ([tile] if nxt else []),
        out_shape=[jax.ShapeDtypeStruct((NG, 2, s, FB), BF16), jax.ShapeDtypeStruct((NG, 2, s, FB), BF16),
                   jax.ShapeDtypeStruct((s, D), F32)] + ([jax.ShapeDtypeStruct((s, D), BF16)] if nxt else []),
        scratch_shapes=[pltpu.VMEM((2, ts + HALO, FB), F32)],
        compiler_params=_params("arbitrary", "arbitrary"),
    )(h, h, wup, fcw, wdn, xres, *([gnext] if nxt else []))


def _final(x, tgt, g, name):
    s = x.shape[0]
    ts = _tile(s, 512)

    def body(x_ref, t_ref, g_ref, dx_ref, dxb_ref, dg_ref, loss_ref):
        i = pl.program_id(0)
        xv = x_ref[...]
        gv = g_ref[...]
        r = lax.rsqrt(jnp.mean(xv * xv, axis=-1, keepdims=True) + EPS)
        xhat = xv * r
        e = xhat * gv - t_ref[...]
        part = 0.5 * jnp.sum(jnp.mean(e * e, axis=-1, keepdims=True), axis=0, keepdims=True)
        dy = e * (1.0 / D)
        dgp = jnp.sum(dy * xhat, axis=0, keepdims=True)
        u = dy * gv
        dx = r * (u - xhat * jnp.mean(u * xhat, axis=-1, keepdims=True))
        dx_ref[...] = dx
        dxb_ref[...] = dx.astype(BF16)

        @pl.when(i == 0)
        def _():
            dg_ref[...] = dgp
            loss_ref[...] = jnp.broadcast_to(part, (1, 128))

        @pl.when(i > 0)
        def _():
            dg_ref[...] += dgp
            loss_ref[...] += jnp.broadcast_to(part, (1, 128))

    return pl.pallas_call(
        body, grid=(s // ts,), name=name,
        in_specs=[pl.BlockSpec((ts, D), lambda i: (i, 0)), pl.BlockSpec((ts, D), lambda i: (i, 0)),
                  pl.BlockSpec((1, D), lambda i: (0, 0))],
        out_specs=[pl.BlockSpec((ts, D), lambda i: (i, 0)), pl.BlockSpec((ts, D), lambda i: (i, 0)),
                   pl.BlockSpec((1, D), lambda i: (0, 0)), pl.BlockSpec((1, 128), lambda i: (0, 0))],
        out_shape=[jax.ShapeDtypeStruct((s, D), F32), jax.ShapeDtypeStruct((s, D), BF16),
                   jax.ShapeDtypeStruct((1, D), F32), jax.ShapeDtypeStruct((1, 128), F32)],
        compiler_params=_params("arbitrary"),
    )(x, tgt, g)


def _ffn_bwd(df, up, upc, wup, fcw, wdn, xin, g, name):
    s = df.shape[0]
    ts = _tile(s, 512)
    nt = s // ts

    def body(df_ref, up_ref, upc_ref, w_ref, cw_ref, wd_ref, x_ref, g_ref,
             a_ref, dup_ref, dx_ref, dxb_ref, dg_ref, dcw_ref, carry, acc, tacc, dcs_ref):
        i = pl.program_id(0)
        m = pl.program_id(1)
        first = i == 0
        @pl.when(first)
        def _():
            carry[m] = jnp.zeros((2, 8, FB), F32)
            dcw_ref[m] = jnp.zeros((2, 3, FB), F32)

        @pl.when(m == 0)
        def _():
            acc[...] = jnp.zeros((ts, D), F32)

        cws = [[cw_ref[gv, k:k + 1, :] for k in range(3)] for gv in range(2)]
        part = ts // NPART
        das = [lax.dot_general(df_ref[p * part:(p + 1) * part, :].astype(BF16), wd_ref[...], NT_DIMS,
                               preferred_element_type=F32) for p in range(NPART)]

        tacc[...] = jnp.zeros((2, 3, 8, FB), F32)
        dcs_ref[:, ts:ts + 8, :] = carry[m]
        for r in reversed(range(ts // RC)):
            rs = slice(r * RC, (r + 1) * RC)
            gate = upc_ref[0, rs, :].astype(F32)
            val = upc_ref[1, rs, :].astype(F32)
            sg = _sigmoid(gate)
            sl = gate * sg
            a_ref[rs, :] = (sl * val).astype(BF16)
            da_c = das[(r * RC) // part][(r * RC) % part:(r * RC) % part + RC]
            dcs = [da_c * val * _dsilu(gate, sg), da_c * sl]
            for gv in range(2):
                dc = dcs[gv]
                dcs_ref[gv, rs, :] = dc
                d1 = dcs_ref[gv, r * RC + 1:(r + 1) * RC + 1, :]
                d2 = dcs_ref[gv, r * RC + 2:(r + 1) * RC + 2, :]
                du = cws[gv][2] * dc + cws[gv][1] * d1 + cws[gv][0] * d2
                dup_ref[gv, rs, :] = du.astype(BF16)
                x0 = up_ref[gv, rs, :].astype(F32)
                for k, dk in enumerate((d2, d1, dc)):
                    p = x0 * dk
                    tacc[gv, k] += sum(p[j:j + 8] for j in range(0, RC, 8))
            if (r * RC) % part == 0:
                ps = slice(r * RC, r * RC + part)
                acc[ps, :] += (
                    lax.dot_general(dup_ref[0, ps, :], w_ref[0], NT_DIMS, preferred_element_type=F32)
                    + lax.dot_general(dup_ref[1, ps, :], w_ref[1], NT_DIMS, preferred_element_type=F32))
        for gv in range(2):
            carry[m, gv] = dcs_ref[gv, 0:8, :]
            for k in range(3):
                dcw_ref[m, gv, k:k + 1, :] += jnp.sum(tacc[gv, k], axis=0, keepdims=True)

        @pl.when(m == NG - 1)
        def _():
            dx, dgp = _rms_bwd_math(acc[...], x_ref[...], g_ref[...])
            dx = df_ref[...] + dx
            dx_ref[...] = dx
            dxb_ref[...] = dx.astype(BF16)

            @pl.when(first)
            def _():
                dg_ref[...] = dgp

            @pl.when(jnp.logical_not(first))
            def _():
                dg_ref[...] += dgp

    rev = lambda i: nt - 1 - i
    return pl.pallas_call(
        body, grid=(nt, NG), name=name,
        in_specs=[pl.BlockSpec((ts, D), lambda i, m: (rev(i), 0)),
                  pl.BlockSpec((None, 2, ts, FB), lambda i, m: (m, 0, rev(i), 0)),
                  pl.BlockSpec((None, 2, ts, FB), lambda i, m: (m, 0, rev(i), 0)),
                  pl.BlockSpec((2, None, D, FB), lambda i, m: (0, m, 0, 0)),
                  pl.BlockSpec((2, None, 3, FB), lambda i, m: (0, m, 0, 0)),
                  pl.BlockSpec((FB, D), lambda i, m: (m, 0)),
                  pl.BlockSpec((ts, D), lambda i, m: (rev(i), 0)),
                  pl.BlockSpec((1, D), lambda i, m: (0, 0))],
        out_specs=[pl.BlockSpec((None, ts, FB), lambda i, m: (m, rev(i), 0)),
                   pl.BlockSpec((None, 2, ts, FB), lambda i, m: (m, 0, rev(i), 0)),
                   pl.BlockSpec((ts, D), lambda i, m: (rev(i), 0)),
                   pl.BlockSpec((ts, D), lambda i, m: (rev(i), 0)),
                   pl.BlockSpec((1, D), lambda i, m: (0, 0)),
                   pl.BlockSpec((NG, 2, 3, FB), lambda i, m: (0, 0, 0, 0))],
        out_shape=[jax.ShapeDtypeStruct((NG, s, FB), BF16), jax.ShapeDtypeStruct((NG, 2, s, FB), BF16),
                   jax.ShapeDtypeStruct((s, D), F32), jax.ShapeDtypeStruct((s, D), BF16),
                   jax.ShapeDtypeStruct((1, D), F32),
                   jax.ShapeDtypeStruct((NG, 2, 3, FB), F32)],
        scratch_shapes=[pltpu.VMEM((NG, 2, 8, FB), F32), pltpu.VMEM((ts, D), F32),
                        pltpu.VMEM((2, 3, 8, FB), F32), pltpu.VMEM((2, ts + 8, FB), F32)],
        compiler_params=_params("arbitrary", "arbitrary"),
    )(df, up, upc, wup, fcw, wdn, xin, g)


def _mm_nt(dy, w, name):
    s = dy.shape[0]
    ts = _tile(s, 1024)

    def body(dy_ref, w_ref, o_ref):
        o_ref[...] = lax.dot_general(dy_ref[...], w_ref[...], NT_DIMS,
                                     preferred_element_type=F32).astype(BF16)

    return pl.pallas_call(
        body, grid=(s // ts,), name=name,
        in_specs=[pl.BlockSpec((ts, D), lambda i: (i, 0)), pl.BlockSpec((D, D), lambda i: (0, 0))],
        out_specs=pl.BlockSpec((ts, D), lambda i: (i, 0)),
        out_shape=jax.ShapeDtypeStruct((s, D), BF16),
        compiler_params=_params("parallel"),
    )(dy, w)


def _mm_nt_rms(dy, wblk, x, g, dres, bf16_copy, name):
    s = dy.shape[0]
    nb, _, bn = wblk.shape
    pair = _pair(bn)
    ts = _tile(s, 512)

    def body(dy_ref, w_ref, x_ref, g_ref, dr_ref, dx_ref, *rest):
        dg_ref = rest[-1]
        i = pl.program_id(0)
        acc = jnp.zeros((ts, D), F32)
        for b in range(0, nb, pair):
            acc = acc + lax.dot_general(dy_ref[:, b * bn:(b + pair) * bn], _cols(w_ref, b, pair), NT_DIMS,
                                        preferred_element_type=F32)
        dx, dgp = _rms_bwd_math(acc, x_ref[...], g_ref[...])
        dx = dr_ref[...] + dx
        dx_ref[...] = dx
        if bf16_copy:
            rest[0][...] = dx.astype(BF16)

        @pl.when(i == 0)
        def _():
            dg_ref[...] = dgp

        @pl.when(i > 0)
        def _():
            dg_ref[...] += dgp

    tile = pl.BlockSpec((ts, D), lambda i: (i, 0))
    return pl.pallas_call(
        body, grid=(s // ts,), name=name,
        in_specs=[pl.BlockSpec((ts, nb * bn), lambda i: (i, 0)), pl.BlockSpec((nb, D, bn), lambda i: (0, 0, 0)),
                  tile, pl.BlockSpec((1, D), lambda i: (0, 0)), tile],
        out_specs=[tile] + ([tile] if bf16_copy else []) + [pl.BlockSpec((1, D), lambda i: (0, 0))],
        out_shape=[jax.ShapeDtypeStruct((s, D), F32)] + ([jax.ShapeDtypeStruct((s, D), BF16)] if bf16_copy else [])
        + [jax.ShapeDtypeStruct((1, D), F32)],
        compiler_params=_params("arbitrary"),
    )(dy, wblk, x, g, dres)


def _c_bwd(dr, zc, cw, name):
    s = dr.shape[0]
    ts = _tile(s, 512)
    nt = s // ts
    hb = ts // HALO

    def body(dr_ref, drf_ref, z_ref, ch_ref, xh_ref, bf_ref, cw_ref, dz_ref, dcw_ref):
        i = pl.program_id(0)
        cwv = [cw_ref[k:k + 1, :] for k in range(3)]
        z_t = z_ref[...].astype(F32)
        bg, cg, xv = z_t[:, 0:D], z_t[:, D:2 * D], z_t[:, 2 * D:3 * D]
        ph = jnp.where(i > 0, ch_ref[...].astype(F32) * xh_ref[...].astype(F32), 0.0)
        pe = jnp.concatenate([ph, cg * xv], axis=0)
        q, taps = _conv3(pe, cwv, HALO)
        drv = dr_ref[...].astype(F32)
        dq = drv * bg
        dqf = jnp.where(i < nt - 1, drf_ref[...].astype(F32) * bf_ref[...].astype(F32), 0.0)
        dp = _conv3_bwd_in(jnp.concatenate([dq, dqf], axis=0), cwv, ts)
        dz_ref[:, 0:D] = (drv * q).astype(BF16)
        dz_ref[:, D:2 * D] = (dp * xv).astype(BF16)
        dz_ref[:, 2 * D:3 * D] = (dp * cg).astype(BF16)
        rows = _conv3_bwd_w(dq, taps)

        @pl.when(i == 0)
        def _():
            for k in range(3):
                dcw_ref[k:k + 1, :] = rows[k]

        @pl.when(i > 0)
        def _():
            for k in range(3):
                dcw_ref[k:k + 1, :] += rows[k]

    past = lambda col: pl.BlockSpec((HALO, D), lambda i: (jnp.maximum(i * hb - 1, 0), col))
    nxt = lambda i: jnp.minimum((i + 1) * hb, s // HALO - 1)
    return pl.pallas_call(
        body, grid=(nt,), name=name,
        in_specs=[pl.BlockSpec((ts, D), lambda i: (i, 0)),
                  pl.BlockSpec((HALO, D), lambda i: (nxt(i), 0)),
                  pl.BlockSpec((ts, 3 * D), lambda i: (i, 0)), past(1), past(2),
                  pl.BlockSpec((HALO, D), lambda i: (nxt(i), 0)),
                  pl.BlockSpec((3, D), lambda i: (0, 0))],
        out_specs=[pl.BlockSpec((ts, 3 * D), lambda i: (i, 0)), pl.BlockSpec((3, D), lambda i: (0, 0))],
        out_shape=[jax.ShapeDtypeStruct((s, 3 * D), BF16), jax.ShapeDtypeStruct((3, D), F32)],
        compiler_params=_params("arbitrary"),
    )(dr, dr, zc, zc, zc, zc, cw)


G512_ROWS = 40


def _ab_bwd(dy, z, yb2, lga, lba, wsm, bs_col, cwb, lgb, lbb, name):
    s = z.shape[0]
    ts = _tile(s, 256)
    nt = s // ts
    hb = ts // HALO_B
    nch = ts // CHUNK

    def body(z_ref, zh_ref, dy_ref, dyf_ref, yb2_ref, yb2f_ref, lga_ref, lba_ref, ws_ref, bs_ref,
             cw_ref, lgb_ref, lbb_ref, dz_ref, g512_ref, dws_ref, dbs_ref, dvn_ref, fwd_rolled, bwd_rolled, du_s):
        i = pl.program_id(0)
        last = i == nt - 1

        @pl.when(i == 0)
        def _():
            g512_ref[...] = jnp.zeros((G512_ROWS, DA), F32)
            dws_ref[...] = jnp.zeros((HEADS, CHUNK, CHUNK), F32)
            dbs_ref[...] = jnp.zeros((HEADS, CHUNK, 1), F32)

        def add_row(k, v):
            g512_ref[k:k + 1, :] += v

        z_t = z_ref[...].astype(F32)
        dy_t = dy_ref[...].astype(F32)
        ua, va = z_t[:, 0:DA], z_t[:, DA:2 * DA]
        gu = _gelu(ua)
        gv = _gelu(va)
        lga_v = lga_ref[...]
        vn, xhat_a, rstd_a = _ln_fwd(gv, lga_v, lba_ref[...])
        vnb = vn.astype(BF16)
        causal = (lax.broadcasted_iota(jnp.int32, (CHUNK, CHUNK), 0)
                  >= lax.broadcasted_iota(jnp.int32, (CHUNK, CHUNK), 1)).astype(F32)
        for c in range(nch):
            for h in range(HEADS):
                rs = slice(c * CHUNK, (c + 1) * CHUNK)
                cs = slice(h * CHUNK, (h + 1) * CHUNK)
                vblk = vnb[rs, cs]
                mixed = jnp.dot(ws_ref[h], vblk, preferred_element_type=F32) + bs_ref[h]
                dyb_ = dy_t[rs, cs]
                dmix = dyb_ * gu[rs, cs]
                dmb = dmix.astype(BF16)
                dz_ref[rs, cs] = (dyb_ * mixed * _dgelu(ua[rs, cs])).astype(BF16)
                dvn_ref[rs, cs] = lax.dot_general(ws_ref[h], dmb, TN_DIMS, preferred_element_type=F32)
                dws_ref[h] += causal * lax.dot_general(dmb, vblk, NT_DIMS, preferred_element_type=F32)
                dbs_ref[h] += jnp.sum(dmix, axis=1, keepdims=True)
        dvn = dvn_ref[...]
        add_row(0, jnp.sum(dvn * xhat_a, axis=0, keepdims=True))
        add_row(1, jnp.sum(dvn, axis=0, keepdims=True))
        dgv = _ln_bwd(dvn, xhat_a, rstd_a, lga_v)
        dz_ref[:, DA:2 * DA] = (dgv * _dgelu(va)).astype(BF16)
        lgb_v = lgb_ref[...]
        dyb_e = jnp.concatenate(
            [dy_t[:, DA:2 * DA], jnp.where(last, 0.0, dyf_ref[...].astype(F32))], axis=0)
        yb2_e = jnp.concatenate([yb2_ref[...], jnp.where(last, 0.0, yb2f_ref[...])], axis=0)
        n_e, xhat_b, rstd_b = _ln_fwd(yb2_e, lgb_v, lbb_ref[...])
        sgn = _sigmoid(n_e)
        dn = dyb_e * _dsilu(n_e, sgn)
        dy2 = _ln_bwd(dn, xhat_b, rstd_b, lgb_v)
        add_row(2, jnp.sum(dy2[:ts], axis=0, keepdims=True))
        add_row(3, jnp.sum(dn[:ts] * xhat_b[:ts], axis=0, keepdims=True))
        add_row(4, jnp.sum(dn[:ts], axis=0, keepdims=True))
        zh = jnp.where(i > 0, zh_ref[...], jnp.zeros_like(zh_ref[...])).astype(F32)
        xb_t, gb_t = z_t[:, 2 * DA:3 * DA], z_t[:, 3 * DA:4 * DA]
        sgb = _sigmoid(gb_t)
        _rolled_copies(fwd_rolled, jnp.concatenate(
            [zh[:, 0:DA] * _sigmoid(zh[:, DA:2 * DA]), xb_t * sgb], axis=0), False)
        _rolled_copies(bwd_rolled, dy2, True)
        for o in range(0, ts, CONV_ROWS):
            acc = jnp.zeros((CONV_ROWS, DA), F32)
            for sh in range(BCONV):
                q, r = divmod(sh, 8)
                acc = acc + cw_ref[BCONV - 1 - sh:BCONV - sh, :] * bwd_rolled[r, 8 * q + o:8 * q + o + CONV_ROWS, :]
            du_s[o:o + CONV_ROWS, :] = acc
        for sh in range(BCONV):
            q, r = divmod(sh, 8)
            acc = jnp.zeros((CONV_ROWS, DA), F32)
            for o in range(0, ts, CONV_ROWS):
                lo = HALO_B - 8 * q + o
                acc = acc + bwd_rolled[0, o:o + CONV_ROWS, :] * fwd_rolled[r, lo:lo + CONV_ROWS, :]
            add_row(8 + BCONV - 1 - sh, jnp.sum(acc, axis=0, keepdims=True))
        du = du_s[...]
        dz_ref[:, 2 * DA:3 * DA] = (du * sgb).astype(BF16)
        dz_ref[:, 3 * DA:4 * DA] = (du * xb_t * sgb * (1.0 - sgb)).astype(BF16)

    row = lambda i: (0, 0)
    nxt = lambda i: jnp.minimum((i + 1) * hb, s // HALO_B - 1)
    return pl.pallas_call(
        body, grid=(nt,), name=name,
        in_specs=[pl.BlockSpec((ts, 4 * DA), lambda i: (i, 0)),
                  pl.BlockSpec((HALO_B, 2 * DA), lambda i: (jnp.maximum(i * hb - 1, 0), 1)),
                  pl.BlockSpec((ts, 2 * DA), lambda i: (i, 0)),
                  pl.BlockSpec((HALO_B, DA), lambda i: (nxt(i), 1)),
                  pl.BlockSpec((ts, DA), lambda i: (i, 0)),
                  pl.BlockSpec((HALO_B, DA), lambda i: (nxt(i), 0)),
                  pl.BlockSpec((1, DA), row), pl.BlockSpec((1, DA), row),
                  pl.BlockSpec((HEADS, CHUNK, CHUNK), lambda i: (0, 0, 0)),
                  pl.BlockSpec((HEADS, CHUNK, 1), lambda i: (0, 0, 0)),
                  pl.BlockSpec((BCONV, DA), row), pl.BlockSpec((1, DA), row), pl.BlockSpec((1, DA), row)],
        out_specs=[pl.BlockSpec((ts, 4 * DA), lambda i: (i, 0)),
                   pl.BlockSpec((G512_ROWS, DA), row),
                   pl.BlockSpec((HEADS, CHUNK, CHUNK), lambda i: (0, 0, 0)),
                   pl.BlockSpec((HEADS, CHUNK, 1), lambda i: (0, 0, 0))],
        out_shape=[jax.ShapeDtypeStruct((s, 4 * DA), BF16), jax.ShapeDtypeStruct((G512_ROWS, DA), F32),
                   jax.ShapeDtypeStruct((HEADS, CHUNK, CHUNK), F32),
                   jax.ShapeDtypeStruct((HEADS, CHUNK, 1), F32)],
        scratch_shapes=[pltpu.VMEM((ts, DA), F32), pltpu.VMEM((8, ts + HALO_B, DA), F32),
                        pltpu.VMEM((8, ts + HALO_B, DA), F32), pltpu.VMEM((ts, DA), F32)],
        compiler_params=_params("arbitrary"),
    )(z, z, dy, dy, yb2, yb2, lga, lba, wsm, bs_col, cwb, lgb, lbb)


def _dw_cols(a, dy, nb, bn, name):
    s = a.shape[0]
    tm = _tile(s, 2048)
    nt = s // tm
    cpb = 4

    def body(a_ref, dy_ref, o_ref, acc):
        t = pl.program_id(1)
        p = lax.dot_general(a_ref[...], dy_ref[...], TN_DIMS, preferred_element_type=F32)

        @pl.when(t == 0)
        def _():
            for q in range(cpb):
                acc[q] = p[:, q * bn:(q + 1) * bn]

        @pl.when(t > 0)
        def _():
            for q in range(cpb):
                acc[q] += p[:, q * bn:(q + 1) * bn]

        @pl.when(t == nt - 1)
        def _():
            o_ref[...] = acc[...].astype(BF16)

    return pl.pallas_call(
        body, grid=(nb // cpb, nt), name=name,
        in_specs=[pl.BlockSpec((tm, D), lambda j, t: (t, 0)), pl.BlockSpec((tm, cpb * bn), lambda j, t: (t, j))],
        out_specs=pl.BlockSpec((cpb, D, bn), lambda j, t: (j, 0, 0)),
        out_shape=jax.ShapeDtypeStruct((nb, D, bn), BF16),
        scratch_shapes=[pltpu.VMEM((cpb, D, bn), F32)],
        compiler_params=_params("arbitrary", "arbitrary"),
    )(a, dy)


def _dw_rows(a, dy, name):
    s = a.shape[0]
    tm = _tile(s, 4096)
    nt = s // tm
    rb = 512

    def body(a_ref, dy_ref, o_ref, acc):
        t = pl.program_id(1)
        p = lax.dot_general(a_ref[...], dy_ref[...], TN_DIMS, preferred_element_type=F32)

        @pl.when(t == 0)
        def _():
            acc[...] = p

        @pl.when(t > 0)
        def _():
            acc[...] += p

        @pl.when(t == nt - 1)
        def _():
            o_ref[...] = acc[...].astype(BF16)

    return pl.pallas_call(
        body, grid=(D // rb, nt), name=name,
        in_specs=[pl.BlockSpec((tm, rb), lambda j, t: (t, j)), pl.BlockSpec((tm, D), lambda j, t: (t, 0))],
        out_specs=pl.BlockSpec((rb, D), lambda j, t: (j, 0)),
        out_shape=jax.ShapeDtypeStruct((D, D), BF16),
        scratch_shapes=[pltpu.VMEM((rb, D), F32)],
        compiler_params=_params("arbitrary", "arbitrary"),
    )(a, dy)


def _dw_up(h, dup, name):
    s = h.shape[0]
    tm = _tile(s, 4096)
    nt = s // tm

    def body(h_ref, d_ref, o_ref, acc):
        t = pl.program_id(1)
        p = lax.dot_general(d_ref[...], h_ref[...], TN_DIMS, preferred_element_type=F32)

        @pl.when(t == 0)
        def _():
            acc[...] = p

        @pl.when(t > 0)
        def _():
            acc[...] += p

        @pl.when(t == nt - 1)
        def _():
            o_ref[...] = acc[...].astype(BF16)

    return pl.pallas_call(
        body, grid=(NDEV, nt), name=name,
        in_specs=[pl.BlockSpec((tm, D), lambda b, t: (t, 0)),
                  pl.BlockSpec((None, None, tm, FB), lambda b, t: (b % NG, b // NG, t, 0))],
        out_specs=pl.BlockSpec((None, FB, D), lambda b, t: (b, 0, 0)),
        out_shape=jax.ShapeDtypeStruct((NDEV, FB, D), BF16),
        scratch_shapes=[pltpu.VMEM((FB, D), F32)],
        compiler_params=_params("arbitrary", "arbitrary"),
    )(h, dup)


def _dw_dn(a, df, name):
    s = df.shape[0]
    tm = _tile(s, 4096)
    nt = s // tm

    def body(a_ref, d_ref, o_ref, acc):
        t = pl.program_id(1)
        p = lax.dot_general(a_ref[...], d_ref[...], TN_DIMS, preferred_element_type=F32)

        @pl.when(t == 0)
        def _():
            acc[...] = p

        @pl.when(t > 0)
        def _():
            acc[...] += p

        @pl.when(t == nt - 1)
        def _():
            o_ref[...] = acc[...].astype(BF16)

    return pl.pallas_call(
        body, grid=(NG, nt), name=name,
        in_specs=[pl.BlockSpec((None, tm, FB), lambda m, t: (m, t, 0)), pl.BlockSpec((tm, D), lambda m, t: (t, 0))],
        out_specs=pl.BlockSpec((FB, D), lambda m, t: (m, 0)),
        out_shape=jax.ShapeDtypeStruct((DFF, D), BF16),
        scratch_shapes=[pltpu.VMEM((FB, D), F32)],
        compiler_params=_params("arbitrary", "arbitrary"),
    )(a, df)


def _place():
    x, y, c = lax.axis_index("x"), lax.axis_index("y"), lax.axis_index("c")
    chips = [(1 - x, y), (x, 1 - y), (1 - x, 1 - y)]
    return x, y, c, chips


def _zone(shard, dev):
    return lax.dynamic_update_slice(lax.empty((NDEV,) + shard.shape, shard.dtype), shard[None],
                                    (dev,) + (0,) * shard.ndim)


HBM_SPEC = pl.BlockSpec(memory_space=pltpu.HBM)
SEM_SPEC = pl.BlockSpec(memory_space=pltpu.SEMAPHORE)
DATAFLOW = pltpu.SideEffectType.DATAFLOW_SIDE_EFFECTING


def _hbm(a):
    return pltpu.with_memory_space_constraint(a, pltpu.HBM)


def _hbm_like(arrs):
    return [pltpu.HBM(a.shape, a.dtype) for a in arrs]


def _ag_start(srcs, lands, after, name):
    n = len(srcs)
    ns = 8 * n

    def body(*refs):
        src, land = refs[:n], refs[n:2 * n]
        sems = refs[2 * n + 1:2 * n + 1 + ns]
        token = refs[-1]
        x, y, c, chips = _place()
        peers = [(x, y, 1 - c)] + [(*chip, c) for chip in chips]
        for t in range(n):
            for k, to in enumerate(peers):
                pltpu.make_async_remote_copy(
                    src_ref=src[t], dst_ref=land[t].at[4 * x + 2 * y + c],
                    send_sem=sems[2 * (4 * t + k)], recv_sem=sems[2 * (4 * t + k) + 1],
                    device_id=to, device_id_type=MESH).start()
        token[...] = jnp.zeros_like(token)

    res = pl.pallas_call(
        body, name=name,
        in_specs=[HBM_SPEC] * (2 * n) + [ANY],
        out_specs=[SEM_SPEC] * ns + [HBM_SPEC] * (2 * n) + [pl.BlockSpec(memory_space=pltpu.VMEM)],
        out_shape=[pltpu.SemaphoreType.DMA(())] * ns + _hbm_like(srcs) + _hbm_like(lands)
        + [jax.ShapeDtypeStruct((8, 128), F32)],
        input_output_aliases={i: ns + i for i in range(2 * n)},
        compiler_params=pltpu.CompilerParams(has_side_effects=DATAFLOW),
    )(*[_hbm(a) for a in srcs], *[_hbm(a) for a in lands], after)
    sems = [[(res[2 * (4 * t + k)], res[2 * (4 * t + k) + 1]) for k in range(4)] for t in range(n)]
    return sems, res[ns:ns + n], res[ns + n:ns + 2 * n], res[-1]


def _ag_forward(srcs, lands, sems1, after, name):
    n = len(srcs)
    flat1 = [s for t in range(n) for k in range(1, 4) for s in sems1[t][k]]
    n1 = len(flat1)

    def body(*refs):
        src, land = refs[:n], refs[n:2 * n]
        s1 = refs[2 * n:2 * n + n1]
        s2 = refs[2 * n + n1 + 1:2 * n + n1 + 1 + 6 * n]
        x, y, c, chips = _place()
        for j, (cx, cy) in enumerate(chips):
            for t in range(n):
                blk = land[t].at[4 * cx + 2 * cy + c]
                pltpu.make_async_remote_copy(
                    src_ref=src[t], dst_ref=blk, send_sem=s1[2 * (3 * t + j)], recv_sem=s1[2 * (3 * t + j) + 1],
                    device_id=(cx, cy, c), device_id_type=MESH).wait_recv()
                pltpu.make_async_remote_copy(
                    src_ref=blk, dst_ref=blk, send_sem=s2[2 * (3 * t + j)], recv_sem=s2[2 * (3 * t + j) + 1],
                    device_id=(x, y, 1 - c), device_id_type=MESH).start()

    res = pl.pallas_call(
        body, name=name,
        in_specs=[HBM_SPEC] * (2 * n) + [SEM_SPEC] * n1 + [ANY],
        out_specs=[SEM_SPEC] * (6 * n) + [HBM_SPEC] * n,
        out_shape=[pltpu.SemaphoreType.DMA(())] * (6 * n) + _hbm_like(lands),
        input_output_aliases={n + i: 6 * n + i for i in range(n)},
        compiler_params=pltpu.CompilerParams(has_side_effects=DATAFLOW),
    )(*srcs, *lands, *flat1, after)
    sems2 = [[(res[2 * (3 * t + j)], res[2 * (3 * t + j) + 1]) for j in range(3)] for t in range(n)]
    return sems2, res[6 * n:]


def _ag_finish(srcs, lands, sems1, sems2, after, name):
    n = len(srcs)
    flat1 = [s for t in range(n) for k in range(4) for s in sems1[t][k]]
    flat2 = [s for t in range(n) for j in range(3) for s in sems2[t][j]]
    n1, n2 = len(flat1), len(flat2)

    def body(*refs):
        src, land = refs[:n], refs[n:2 * n]
        s1 = refs[2 * n:2 * n + n1]
        s2 = refs[2 * n + n1:2 * n + n1 + n2]
        x, y, c, chips = _place()
        sib = (x, y, 1 - c)
        for t in range(n):
            own = land[t].at[4 * x + 2 * y + 1 - c]
            pltpu.make_async_remote_copy(
                src_ref=src[t], dst_ref=own, send_sem=s1[8 * t], recv_sem=s1[8 * t + 1],
                device_id=sib, device_id_type=MESH).wait_recv()
            for k in range(4):
                pltpu.make_async_remote_copy(
                    src_ref=src[t], dst_ref=own, send_sem=s1[2 * (4 * t + k)], recv_sem=s1[2 * (4 * t + k) + 1],
                    device_id=sib, device_id_type=MESH).wait_send()
            for j, (cx, cy) in enumerate(chips):
                blk = land[t].at[4 * cx + 2 * cy + 1 - c]
                cp = pltpu.make_async_remote_copy(
                    src_ref=blk, dst_ref=blk, send_sem=s2[2 * (3 * t + j)], recv_sem=s2[2 * (3 * t + j) + 1],
                    device_id=sib, device_id_type=MESH)
                cp.wait_send()
                cp.wait_recv()

    return pl.pallas_call(
        body, name=name,
        in_specs=[HBM_SPEC] * (2 * n) + [SEM_SPEC] * (n1 + n2) + [ANY],
        out_specs=[HBM_SPEC] * n,
        out_shape=_hbm_like(lands),
        input_output_aliases={n + i: i for i in range(n)},
        compiler_params=pltpu.CompilerParams(has_side_effects=DATAFLOW),
    )(*srcs, *lands, *flat1, *flat2, after)


def _pair_copies(srcs, dsts, sems):
    x, y, c, _ = _place()
    nt = len(srcs)
    return [pltpu.make_async_remote_copy(
        src_ref=srcs[t].at[2 * j + 1 - c], dst_ref=dsts[t].at[j],
        send_sem=sems[2 * (NCHIP * t + j)], recv_sem=sems[2 * (NCHIP * t + j) + 1],
        device_id=(x, y, 1 - c), device_id_type=MESH) for t in range(nt) for j in range(NCHIP)]


def _pair_start(grads, carry, name):
    nt = len(grads)
    ns = 2 * NCHIP * nt
    zones = [_hbm(lax.empty((NCHIP,) + a.shape[1:], a.dtype)) for a in grads]
    extra = [] if carry is None else [_hbm(carry)]
    ne = len(extra)

    def body(*refs):
        for cp in _pair_copies(refs[:nt], refs[nt:2 * nt], refs[2 * nt + ne:2 * nt + ne + ns]):
            cp.start()

    res = pl.pallas_call(
        body, name=name,
        in_specs=[HBM_SPEC] * (2 * nt + ne),
        out_specs=[SEM_SPEC] * ns + [HBM_SPEC] * (2 * nt + ne),
        out_shape=[pltpu.SemaphoreType.DMA(())] * ns + _hbm_like(grads) + _hbm_like(zones) + _hbm_like(extra),
        input_output_aliases={i: ns + i for i in range(2 * nt + ne)},
        compiler_params=pltpu.CompilerParams(has_side_effects=DATAFLOW),
    )(*[_hbm(a) for a in grads], *zones, *extra)
    handle = (list(res[:ns]), list(res[ns:ns + nt]), list(res[ns + nt:ns + 2 * nt]))
    return handle, (res[ns + 2 * nt] if ne else None)


def _pair_wait(handle, after, name):
    sems, srcs, zones = handle
    nt, ns = len(srcs), len(sems)

    def body(*refs):
        for cp in _pair_copies(refs[:nt], refs[nt:2 * nt], refs[2 * nt:2 * nt + ns]):
            cp.wait_send()
            cp.wait_recv()

    return pl.pallas_call(
        body, name=name,
        in_specs=[HBM_SPEC] * (2 * nt) + [SEM_SPEC] * ns + [ANY],
        out_specs=[HBM_SPEC] * nt,
        out_shape=_hbm_like(zones),
        input_output_aliases={nt + i: i for i in range(nt)},
        compiler_params=pltpu.CompilerParams(has_side_effects=DATAFLOW),
    )(*srcs, *zones, *sems, after)


def _rows_tile(r, row_bytes, cap_bytes):
    best = None
    for tr in range(16, r + 1, 16):
        if r % tr == 0 and tr * row_bytes <= cap_bytes:
            best = tr
    return best if best is not None else r


def _pair_sum(own, got, cidx, name):
    _, _, r, cdim = own.shape
    tr = _rows_tile(r, 2 * cdim, 2 * 1024 * 1024)

    def body(c_ref, a_ref, b_ref, o_ref):
        o_ref[...] = (a_ref[...].astype(F32) + b_ref[...].astype(F32)).astype(BF16)

    return pl.pallas_call(
        body, name=name,
        grid_spec=pltpu.PrefetchScalarGridSpec(
            num_scalar_prefetch=1, grid=(NCHIP, r // tr),
            in_specs=[pl.BlockSpec((None, None, tr, cdim), lambda j, i, c_ref: (j, c_ref[0], i, 0)),
                      pl.BlockSpec((None, tr, cdim), lambda j, i, c_ref: (j, i, 0))],
            out_specs=pl.BlockSpec((None, tr, cdim), lambda j, i, c_ref: (j, i, 0))),
        out_shape=jax.ShapeDtypeStruct((NCHIP, r, cdim), BF16),
        compiler_params=_params("arbitrary", "arbitrary"),
    )(cidx, own, got)


def _chip_copies(srcs, zones, slots, sems):
    x, y, c, chips = _place()
    out = []
    for t, (z, l) in enumerate(slots):
        for k, (cx, cy) in enumerate(chips):
            dst = zones[z].at[k] if l is None else zones[z].at[k, l]
            out.append(pltpu.make_async_remote_copy(
                src_ref=srcs[t].at[2 * cx + cy], dst_ref=dst,
                send_sem=sems[2 * (3 * t + k)], recv_sem=sems[2 * (3 * t + k) + 1],
                device_id=(cx, cy, c), device_id_type=MESH))
    return out


def _chip_start(sums, zones, slots, carry, name):
    nt, nz = len(sums), len(zones)
    ns = 6 * nt
    extra = [] if carry is None else [_hbm(carry)]
    ne = len(extra)

    def body(*refs):
        for cp in _chip_copies(refs[:nt], refs[nt:nt + nz], slots, refs[nt + nz + ne:nt + nz + ne + ns]):
            cp.start()

    res = pl.pallas_call(
        body, name=name,
        in_specs=[HBM_SPEC] * (nt + nz + ne),
        out_specs=[SEM_SPEC] * ns + [HBM_SPEC] * (nt + nz + ne),
        out_shape=[pltpu.SemaphoreType.DMA(())] * ns + _hbm_like(sums) + _hbm_like(zones) + _hbm_like(extra),
        input_output_aliases={i: ns + i for i in range(nt + nz + ne)},
        compiler_params=pltpu.CompilerParams(has_side_effects=DATAFLOW),
    )(*[_hbm(a) for a in sums], *zones, *extra)
    return (list(res[:ns]), list(res[ns:ns + nt]), list(res[ns + nt:ns + nt + nz]),
            (res[ns + nt + nz] if ne else None))


def _chip_wait(started, zones, zone_ids, after, name):
    started = [(sums, [(zone_ids.index(z), l) for z, l in slots], sems) for sums, slots, sems in started]
    nz = len(zones)
    flat_src = [a for sums, _, _ in started for a in sums]
    flat_sem = [s for _, _, sems in started for s in sems]
    n_src, n_sem = len(flat_src), len(flat_sem)

    def body(*refs):
        srcs, zs, sems = refs[:n_src], refs[n_src:n_src + nz], refs[n_src + nz:n_src + nz + n_sem]
        so, se = 0, 0
        for sums, slots, sem_list in started:
            for cp in _chip_copies(srcs[so:so + len(sums)], zs, slots, sems[se:se + len(sem_list)]):
                cp.wait_send()
                cp.wait_recv()
            so += len(sums)
            se += len(sem_list)

    return pl.pallas_call(
        body, name=name,
        in_specs=[HBM_SPEC] * (n_src + nz) + [SEM_SPEC] * n_sem + [ANY],
        out_specs=[HBM_SPEC] * nz,
        out_shape=_hbm_like(zones),
        input_output_aliases={n_src + i: i for i in range(nz)},
        compiler_params=pltpu.CompilerParams(has_side_effects=DATAFLOW),
    )(*flat_src, *zones, *flat_sem, after)


def _small_allreduce(parts, y_first, after, name):
    nt = len(parts)

    def body(*refs):
        srcs, outs, bufs = refs[:nt], refs[nt + 1:2 * nt + 1], refs[2 * nt + 1:3 * nt + 1]
        send_sems, recv_sems = refs[3 * nt + 1:]
        x, y, c, _ = _place()
        along = {"c": (x, y, 1 - c), "x": (1 - x, y, c), "y": (x, 1 - y, c)}
        for t in range(nt):
            outs[t][...] = srcs[t][...]
        for step in range(3):
            order = [("c", "y", "x") if t in y_first else ("c", "x", "y") for t in range(nt)]
            copies = [pltpu.make_async_remote_copy(
                src_ref=outs[t], dst_ref=bufs[t].at[step],
                send_sem=send_sems.at[step, t], recv_sem=recv_sems.at[step, t],
                device_id=along[order[t][step]], device_id_type=MESH) for t in range(nt)]
            for cp in copies:
                cp.start()
            for cp in copies:
                cp.wait()
            for t in range(nt):
                outs[t][...] = outs[t][...] + bufs[t][step]

    vm = pl.BlockSpec(memory_space=pltpu.VMEM)
    return pl.pallas_call(
        body, name=name,
        in_specs=[vm] * nt + [ANY], out_specs=[vm] * nt,
        out_shape=[jax.ShapeDtypeStruct(a.shape, F32) for a in parts],
        scratch_shapes=[pltpu.VMEM((3,) + a.shape, F32) for a in parts]
        + [pltpu.SemaphoreType.DMA((3, nt)), pltpu.SemaphoreType.DMA((3, nt))],
        compiler_params=pltpu.CompilerParams(has_side_effects=True, vmem_limit_bytes=VMEM_LIMIT),
    )(*parts, after)


def _adam_math(w, g, m, v):
    m2 = ADAM_B1 * m + (1.0 - ADAM_B1) * g
    v2 = ADAM_B2 * v + (1.0 - ADAM_B2) * (g * g)
    m_hat = m2 / (1.0 - ADAM_B1 ** ADAM_STEP)
    v_hat = v2 / (1.0 - ADAM_B2 ** ADAM_STEP)
    delta = -ADAM_LR * (m_hat / (jnp.sqrt(v_hat) + ADAM_EPS) + ADAM_WD * w)
    return delta, m2, v2


def _adam_big(w, m, v, parts, mine, chip, name):
    nl, r, cdim = w.shape
    tr = _rows_tile(r, 4 * cdim, 3 * 512 * 1024)

    def body(c_ref, w_ref, m_ref, v_ref, p_ref, *rest):
        mine_refs, (g_ref, d_ref, mo_ref, vo_ref) = rest[:nl], rest[nl:]
        own = mine_refs[0][...]
        for l in range(1, nl):
            own = jnp.where(pl.program_id(0) == l, mine_refs[l][...], own)
        g = ((p_ref[0].astype(F32) + p_ref[1].astype(F32)) + p_ref[2].astype(F32)) + own.astype(F32)
        delta, m2, v2 = _adam_math(w_ref[...], g, m_ref[...], v_ref[...])
        g_ref[...] = g
        d_ref[...] = delta
        mo_ref[...] = m2
        vo_ref[...] = v2

    spec = pl.BlockSpec((None, tr, cdim), lambda l, i, c_ref: (l, i, 0))
    mine_specs = [pl.BlockSpec((None, tr, cdim), lambda l, i, c_ref, ll=ll: (c_ref[0], jnp.where(l == ll, i, 0), 0))
                  for ll in range(nl)]
    return pl.pallas_call(
        body, name=name,
        grid_spec=pltpu.PrefetchScalarGridSpec(
            num_scalar_prefetch=1, grid=(nl, r // tr),
            in_specs=[spec, spec, spec, pl.BlockSpec((3, None, tr, cdim), lambda l, i, c_ref: (0, l, i, 0))]
            + mine_specs,
            out_specs=[spec] * 4),
        out_shape=[jax.ShapeDtypeStruct(w.shape, F32)] * 4,
        compiler_params=_params("arbitrary", "arbitrary"),
    )(chip, w, m, v, parts, *mine)


def _adam_small(ws, gs, ms, vs, name):
    n = len(ws)

    def body(*refs):
        w_r, g_r, m_r, v_r = refs[:n], refs[n:2 * n], refs[2 * n:3 * n], refs[3 * n:4 * n]
        d_o, m_o, v_o = refs[4 * n:5 * n], refs[5 * n:6 * n], refs[6 * n:7 * n]
        for t in range(n):
            delta, m2, v2 = _adam_math(w_r[t][...], g_r[t][...], m_r[t][...], v_r[t][...])
            d_o[t][...] = delta
            m_o[t][...] = m2
            v_o[t][...] = v2

    vm = pl.BlockSpec(memory_space=pltpu.VMEM)
    shapes = [jax.ShapeDtypeStruct(a.shape, F32) for a in ws]
    return pl.pallas_call(
        body, name=name, in_specs=[vm] * (4 * n), out_specs=[vm] * (3 * n), out_shape=shapes * 3,
        compiler_params=pltpu.CompilerParams(vmem_limit_bytes=VMEM_LIMIT),
    )(*ws, *gs, *ms, *vs)


def kernel(x, norm_mix, norm_ffn, norm_final, ab_w_in, a_ln_g, a_ln_b, a_w_s, a_b_s, b_conv_w, b_conv_b, b_ln_g, b_ln_b, ab_w_out, c_w_in, c_conv_w, c_w_out, f_w_up, f_conv_w, f_w_down, loss_target, m_norm_mix, m_norm_ffn, m_norm_final, m_ab_w_in, m_a_ln_g, m_a_ln_b, m_a_w_s, m_a_b_s, m_b_conv_w, m_b_conv_b, m_b_ln_g, m_b_ln_b, m_ab_w_out, m_c_w_in, m_c_conv_w, m_c_w_out, m_f_w_up, m_f_conv_w, m_f_w_down, v_norm_mix, v_norm_ffn, v_norm_final, v_ab_w_in, v_a_ln_g, v_a_ln_b, v_a_w_s, v_a_b_s, v_b_conv_w, v_b_conv_b, v_b_ln_g, v_b_ln_b, v_ab_w_out, v_c_w_in, v_c_conv_w, v_c_w_out, v_f_w_up, v_f_conv_w, v_f_w_down):
    s = x.shape[1]
    x0 = x.reshape(s, D)
    tgt = loss_target.reshape(s, D)
    xi, yi, ci = lax.axis_index("x"), lax.axis_index("y"), lax.axis_index("c")
    dev = 4 * xi + 2 * yi + ci
    cidx = ci.astype(jnp.int32).reshape(1)

    bf = lambda a: a.astype(BF16)
    first = [bf(ab_w_in[0])]
    sems_a, first, lands_a, token_a = _ag_start(first, [_zone(first[0], dev)], x0, "ag_start_w_in")
    slab_w = 6 * CHUNK
    pad = lambda a, rows: jnp.pad(a, ((0, rows - a.shape[0]), (0, slab_w - a.shape[1])))
    slab = jnp.concatenate([pad(b_conv_w[0], 32), pad(c_conv_w[0], 8), pad(f_conv_w.reshape(6, FB), 8)], axis=0)
    later = [bf(ab_w_out[0]), slab, bf(f_w_up[0]), bf(f_w_down[0]), bf(c_w_in[0]), bf(c_w_out[0]),
             bf(f_w_up[1]), bf(f_w_down[1])]
    sems_b, later, lands_b, ag_token = _ag_start(later, [_zone(a, dev) for a in later], token_a, "ag_start")
    ag_sems, later, lands = sems_a + sems_b, list(first) + list(later), list(lands_a) + list(lands_b)
    groups = [[0], [1, 2], [3, 4], [5, 6], [7, 8]]

    causal = jnp.tril(jnp.ones((CHUNK, CHUNK), F32))
    wsm = (a_w_s[0] * causal).astype(BF16)
    bs_col = a_b_s.reshape(HEADS, CHUNK, 1)
    nm = [norm_mix[0:1], norm_mix[1:2]]
    nf = [norm_ffn[0:1], norm_ffn[1:2]]
    nfin = norm_final.reshape(1, D)

    def arrive(g, after_ici, after_d2d, tag):
        srcs = [later[t] for t in groups[g]]
        zone = [lands[t] for t in groups[g]]
        sems1 = [ag_sems[t] for t in groups[g]]
        sems2, zone = _ag_forward(srcs, zone, sems1, after_ici, "ag_forward_" + tag)
        return _ag_finish(srcs, zone, sems1, sems2, after_d2d, "ag_finish_" + tag)

    h0 = _rms_fwd(x0, nm[0], "rms_mix0", after=ag_token)
    (win0,) = arrive(0, h0, h0, "w_in")
    z = _mm_in(h0, win0, "mm_ab_in")
    wout0, slab_g = arrive(1, z, z, "first")
    wout0 = wout0.reshape(D, D)
    bcw = jnp.transpose(slab_g[:, 0:BCONV, 0:DA // NDEV], (1, 0, 2)).reshape(BCONV, DA)
    ccw = jnp.transpose(slab_g[:, 32:35, 0:D // NDEV], (1, 0, 2)).reshape(3, D)
    fcw_g = slab_g[:, 40:46, 0:FB].reshape(2, NG, 2, 3, FB)
    fcws = [fcw_g[:, :, 0], fcw_g[:, :, 1]]
    ycat, yb2 = _ab_fwd(z, a_ln_g, a_ln_b, wsm, bs_col, bcw, b_conv_b, b_ln_g, b_ln_b, "ab_fwd")
    x1, h1 = _mm_out(ycat, wout0, x0, nf[0], "mm_ab_out")
    wup0, wdn0 = arrive(2, x1, x1, "ffn0")
    up0, upc0, x2, h2 = _ffn_fwd(h1, x1, wup0.reshape(2, NG, D, FB), fcws[0], wdn0.reshape(DFF, D), nm[1],
                                 "ffn_fwd0")
    cin, cout = arrive(3, x2, x2, "c")
    cout = cout.reshape(D, D)
    zc = _mm_in(h2, cin, "mm_c_in")
    rc = _c_fwd(zc, ccw, "c_fwd")
    x3, h3 = _mm_out(rc, cout, x2, nf[1], "mm_c_out")
    wup1, wdn1 = arrive(4, rc, x3, "ffn1")
    wups = [wup0.reshape(2, NG, D, FB), wup1.reshape(2, NG, D, FB)]
    wdns = [wdn0.reshape(DFF, D), wdn1.reshape(DFF, D)]
    up1, upc1, x4 = _ffn_fwd(h3, x3, wups[1], fcws[1], wdns[1], None, "ffn_fwd1")
    dx4, dx4b, dnfin, loss_part = _final(x4, tgt, nfin, "final_loss")

    zshape = lambda *sh: _hbm(lax.empty((3,) + sh, BF16))
    zones = [zshape(D, 2 * D // NDEV), zshape(D // NDEV, D), zshape(D, 3 * D // NDEV), zshape(D // NDEV, D),
             zshape(2, FB, D), zshape(2, DFF // NDEV, D)]
    started = []

    def pair_sums(grads, handle, after, tag):
        del grads
        got = _pair_wait(handle, after, "rs_pair_wait_" + tag)
        return [_pair_sum(b.reshape((NCHIP, 2) + b.shape[1:]), g, cidx, "rs_pair_sum_%s%d" % (tag, t))
                for t, (b, g) in enumerate(zip(handle[1], got))]

    def chip_start(sums, slots, carry, tag):
        sems, sums, new_zones, carry = _chip_start(sums, zones, slots, carry, "rs_chip_start_" + tag)
        zones[:] = new_zones
        started.append((sums, slots, sems))
        return sums, carry

    rows8 = lambda g, r: g.reshape(NDEV, r, D)
    a1, dup1, dx3, dx3b, dnf1, dfcw1 = _ffn_bwd(dx4, up1, upc1, wups[1], fcws[1], wdns[1], x3, nf[1], "ffn_bwd1")
    g_f1 = [_dw_up(h3, dup1, "dw_up1"), rows8(_dw_dn(a1, dx4b, "dw_dn1"), DFF // NDEV)]
    hd_f1, dx3b = _pair_start(g_f1, dx3b, "rs_pair_start_f1")
    drc = _mm_nt(dx3b, cout, "mm_c_out_bwd")
    g_cout = rows8(_dw_rows(rc, dx3b, "dw_c_out"), D // NDEV)
    s_f1 = pair_sums(g_f1, hd_f1, g_cout, "f1")
    s_f1, drc = chip_start(s_f1, [(4, 1), (5, 1)], drc, "f1")
    dzc, dccw = _c_bwd(drc, zc, ccw, "c_bwd")
    dx2, dx2b, dnm1 = _mm_nt_rms(dzc, cin, x2, nm[1], dx3, True, "mm_c_in_bwd")
    g_c = [_dw_cols(h2, dzc, NDEV, 3 * D // NDEV, "dw_c_in"), g_cout]
    hd_c, dx2 = _pair_start(g_c, dx2, "rs_pair_start_c")
    a0, dup0, dx1, dx1b, dnf0, dfcw0 = _ffn_bwd(dx2, up0, upc0, wups[0], fcws[0], wdns[0], x1, nf[0], "ffn_bwd0")
    s_c = pair_sums(g_c, hd_c, dx1b, "c")
    s_c, dx1b = chip_start(s_c, [(2, None), (3, None)], dx1b, "c")
    g_f0 = [_dw_up(h1, dup0, "dw_up0"), rows8(_dw_dn(a0, dx2b, "dw_dn0"), DFF // NDEV)]
    hd_f0, dx1b = _pair_start(g_f0, dx1b, "rs_pair_start_f0")
    dycat = _mm_nt(dx1b, wout0, "mm_ab_out_bwd")
    g_wout0 = rows8(_dw_rows(ycat, dx1b, "dw_ab_out"), D // NDEV)
    s_f0 = pair_sums(g_f0, hd_f0, g_wout0, "f0")
    s_f0, dycat = chip_start(s_f0, [(4, 0), (5, 0)], dycat, "f0")
    dz, g512, dws, dbs = _ab_bwd(dycat, z, yb2, a_ln_g, a_ln_b, wsm, bs_col, bcw, b_ln_g, b_ln_b, "ab_bwd")
    grad_x, dnm0 = _mm_nt_rms(dz, win0, x0, nm[0], dx1, False, "mm_ab_in_bwd")
    g_ab = [_dw_cols(h0, dz, NDEV, 2 * D // NDEV, "dw_ab_in"), g_wout0]
    hd_ab, _ = _pair_start(g_ab, None, "rs_pair_start_ab")

    g1024 = jnp.concatenate([dnm0, dnm1, dnf0, dnf1, dnfin, dccw], axis=0)
    gfc = jnp.concatenate([dfcw0, dfcw1], axis=0).reshape(2 * NG * 2 * 3, FB)
    g1024, g512, dws, dbs, gfc, loss_sum = _small_allreduce(
        [g1024, g512, dws.reshape(HEADS * CHUNK, CHUNK), dbs.reshape(HEADS, CHUNK), gfc, loss_part], (2,),
        hd_ab[1][0], "small_allreduce")
    loss = loss_sum[0, 0]
    s_ab = pair_sums(g_ab, hd_ab, g1024, "ab")
    s_ab, _ = chip_start(s_ab, [(0, None), (1, None)], None, "ab")
    p_cin, p_cout, p_wup, p_wdn = _chip_wait(started[:3], zones[2:], [2, 3, 4, 5], s_ab[0], "rs_chip_wait_early")

    chip = (2 * xi + yi).astype(jnp.int32).reshape(1)

    def big_update(w, m, v, parts, mine, name):
        shp = w.shape
        w3, m3, v3 = (a.reshape((-1,) + shp[-2:]) for a in (w, m, v))
        p4 = parts.reshape((3,) + w3.shape)
        return [o.reshape(shp) for o in _adam_big(w3, m3, v3, p4, mine, chip, name)]

    u_cin = big_update(c_w_in, m_c_w_in, v_c_w_in, p_cin, [s_c[0]], "adam_c_w_in")
    u_cout = big_update(c_w_out, m_c_w_out, v_c_w_out, p_cout, [s_c[1]], "adam_c_w_out")
    tr_ = lambda a: jnp.swapaxes(a, 1, 2)
    u_wup = [tr_(o) for o in big_update(tr_(f_w_up), tr_(m_f_w_up), tr_(v_f_w_up), p_wup,
                                        [s_f0[0], s_f1[0]], "adam_f_w_up")]
    u_wdn = big_update(f_w_down, m_f_w_down, v_f_w_down, p_wdn, [s_f0[1], s_f1[1]], "adam_f_w_down")
    p_win0, p_wout0 = _chip_wait(started[3:], zones[:2], [0, 1], u_wdn[0], "rs_chip_wait_late")
    u_win0 = big_update(ab_w_in, m_ab_w_in, v_ab_w_in, p_win0, [s_ab[0]], "adam_ab_w_in")
    u_wout0 = big_update(ab_w_out, m_ab_w_out, v_ab_w_out, p_wout0, [s_ab[1]], "adam_ab_w_out")

    g_norm_mix = g1024[0:2]
    g_norm_ffn = g1024[2:4]
    g_norm_final = g1024[4:5]
    g_ccw = lax.dynamic_slice(g1024[5:8], (0, dev * (D // NDEV)), (3, D // NDEV))
    g_bcw = lax.dynamic_slice(g512[8:8 + BCONV], (0, dev * (DA // NDEV)), (BCONV, DA // NDEV))
    gfc = gfc.reshape(2, NG, 2, 3, FB)
    g_fcw = lax.dynamic_slice(gfc, (0, dev % NG, dev // NG, 0, 0), (2, 1, 1, 3, FB)).reshape(2, 3, FB)
    small_w = [norm_mix, norm_ffn, nfin, a_ln_g, a_ln_b, a_w_s[0], a_b_s[0], b_conv_w[0], b_conv_b,
               b_ln_g, b_ln_b, c_conv_w[0], f_conv_w]
    small_g = [g_norm_mix, g_norm_ffn, g_norm_final, g512[0:1], g512[1:2],
               dws.reshape(HEADS, CHUNK, CHUNK), dbs, g_bcw, g512[2:3],
               g512[3:4], g512[4:5], g_ccw, g_fcw]
    small_m = [m_norm_mix, m_norm_ffn, m_norm_final.reshape(1, D), m_a_ln_g, m_a_ln_b, m_a_w_s[0], m_a_b_s[0],
               m_b_conv_w[0], m_b_conv_b, m_b_ln_g, m_b_ln_b, m_c_conv_w[0], m_f_conv_w]
    small_v = [v_norm_mix, v_norm_ffn, v_norm_final.reshape(1, D), v_a_ln_g, v_a_ln_b, v_a_w_s[0], v_a_b_s[0],
               v_b_conv_w[0], v_b_conv_b, v_b_ln_g, v_b_ln_b, v_c_conv_w[0], v_f_conv_w]
    upd = _adam_small(small_w, small_g, small_m, small_v, "adam_small")
    ns = len(small_w)
    orig = [norm_mix, norm_ffn, norm_final, a_ln_g, a_ln_b, a_w_s, a_b_s, b_conv_w, b_conv_b,
            b_ln_g, b_ln_b, c_conv_w, f_conv_w]
    sg_out = [g.reshape(o.shape) for g, o in zip(small_g, orig)]
    sd_out = [a.reshape(o.shape) for a, o in zip(upd[0:ns], orig)]
    sm_out = [a.reshape(o.shape) for a, o in zip(upd[ns:2 * ns], orig)]
    sv_out = [a.reshape(o.shape) for a, o in zip(upd[2 * ns:3 * ns], orig)]

    def assemble(small, k):
        return [small[0], small[1], small[2], u_win0[k], small[3], small[4], small[5], small[6], small[7],
                small[8], small[9], small[10], u_wout0[k], u_cin[k], small[11], u_cout[k], u_wup[k],
                small[12], u_wdn[k]]

    grads = assemble(sg_out, 0)
    deltas = assemble(sd_out, 1)
    new_m = assemble(sm_out, 2)
    new_v = assemble(sv_out, 3)
    return (loss, grad_x.reshape(1, s, D), *grads, *deltas, *new_m, *new_v)
```

```python
import math

import jax
import jax.numpy as jnp
from jax import lax
from jax.experimental import pallas as pl
from jax.experimental.pallas import tpu as pltpu

F32 = jnp.float32
BF16 = jnp.bfloat16

D = 1024
DA = 512
HEADS = 4
CHUNK = 128
DFF = 2816
NDEV = 8
NCHIP = 4
FB = DFF * 2 // NDEV
NG = DFF // FB
BCONV = 31
EPS = 1e-6
HALO = 16
HALO_B = 32
RC = 32
NPART = 2
VMEM_LIMIT = 52 * 1024 * 1024
INV_SQRT2 = 1.0 / math.sqrt(2.0)
INV_SQRT_2PI = 1.0 / math.sqrt(2.0 * math.pi)

ADAM_LR = 0.001
ADAM_B1 = 0.9
ADAM_B2 = 0.999
ADAM_EPS = 1e-08
ADAM_WD = 0.01
ADAM_STEP = 10

MESH = pl.DeviceIdType.MESH
ANY = pl.BlockSpec(memory_space=pl.ANY)
NT_DIMS = (((1,), (1,)), ((), ()))
TN_DIMS = (((0,), (0,)), ((), ()))


def _params(*sem):
    return pltpu.CompilerParams(dimension_semantics=sem, vmem_limit_bytes=VMEM_LIMIT)


def _tile(s, want):
    return min(want, s)


def _sigmoid(x):
    return jax.nn.sigmoid(x)


def _dsilu(x, sg):
    return sg * (1.0 + x * (1.0 - sg))


def _gelu(x):
    return 0.5 * x * (1.0 + lax.erf(x * INV_SQRT2))


def _dgelu(x):
    return 0.5 * (1.0 + lax.erf(x * INV_SQRT2)) + x * jnp.exp(-0.5 * x * x) * INV_SQRT_2PI


def _ln_fwd(x, g, b):
    mu = jnp.mean(x, axis=-1, keepdims=True)
    xc = x - mu
    var = jnp.mean(xc * xc, axis=-1, keepdims=True)
    rstd = lax.rsqrt(var + EPS)
    xhat = xc * rstd
    return xhat * g + b, xhat, rstd


def _ln_bwd(dy, xhat, rstd, g):
    dxh = dy * g
    m1 = jnp.mean(dxh, axis=-1, keepdims=True)
    m2 = jnp.mean(dxh * xhat, axis=-1, keepdims=True)
    return rstd * (dxh - m1 - xhat * m2)


def _rms_bwd_math(dh, x, g):
    r = lax.rsqrt(jnp.mean(x * x, axis=-1, keepdims=True) + EPS)
    xhat = x * r
    dg = jnp.sum(dh * xhat, axis=0, keepdims=True)
    u = dh * g
    dx = r * (u - xhat * jnp.mean(u * xhat, axis=-1, keepdims=True))
    return dx, dg


def _conv3(xe, cw, halo):
    x0 = xe[halo:]
    x1 = pltpu.roll(xe, 1, 0)[halo:]
    x2 = pltpu.roll(xe, 2, 0)[halo:]
    return cw[2] * x0 + cw[1] * x1 + cw[0] * x2, (x0, x1, x2)


def _conv3_bwd_in(dce, cw, ts):
    n = dce.shape[0]
    d1 = pltpu.roll(dce, n - 1, 0)[:ts]
    d2 = pltpu.roll(dce, n - 2, 0)[:ts]
    return cw[2] * dce[:ts] + cw[1] * d1 + cw[0] * d2


def _conv3_bwd_w(dc, taps):
    x0, x1, x2 = taps
    return [jnp.sum(dc * x2, axis=0, keepdims=True), jnp.sum(dc * x1, axis=0, keepdims=True),
            jnp.sum(dc * x0, axis=0, keepdims=True)]


def _rms_fwd(x, g, name, after=None):
    s = x.shape[0]
    ts = _tile(s, 512)

    def body(x_ref, g_ref, *rest):
        h_ref = rest[-1]
        xv = x_ref[...]
        r = lax.rsqrt(jnp.mean(xv * xv, axis=-1, keepdims=True) + EPS)
        h_ref[...] = (xv * r * g_ref[...]).astype(BF16)

    extra = [] if after is None else [after]
    return pl.pallas_call(
        body, grid=(s // ts,), name=name,
        in_specs=[pl.BlockSpec((ts, D), lambda i: (i, 0)), pl.BlockSpec((1, D), lambda i: (0, 0))]
        + [ANY] * len(extra),
        out_specs=pl.BlockSpec((ts, D), lambda i: (i, 0)),
        out_shape=jax.ShapeDtypeStruct((s, D), BF16),
        compiler_params=_params("parallel"),
    )(x, g, *extra)


MXU_COLS = 256


def _pair(bn):
    return 1 if bn % MXU_COLS == 0 else 2


def _cols(w_ref, b, pair):
    return w_ref[b] if pair == 1 else jnp.concatenate([w_ref[b + q] for q in range(pair)], axis=1)


def _mm_in(h, wblk, name):
    s = h.shape[0]
    nb, _, bn = wblk.shape
    pair = _pair(bn)
    ts = _tile(s, 1024)

    def body(h_ref, w_ref, o_ref):
        hv = h_ref[...]
        for b in range(0, nb, pair):
            o_ref[:, b * bn:(b + pair) * bn] = jnp.dot(hv, _cols(w_ref, b, pair),
                                                       preferred_element_type=F32).astype(BF16)

    return pl.pallas_call(
        body, grid=(s // ts,), name=name,
        in_specs=[pl.BlockSpec((ts, D), lambda i: (i, 0)), pl.BlockSpec((nb, D, bn), lambda i: (0, 0, 0))],
        out_specs=pl.BlockSpec((ts, nb * bn), lambda i: (i, 0)),
        out_shape=jax.ShapeDtypeStruct((s, nb * bn), BF16),
        compiler_params=_params("parallel"),
    )(h, wblk)


def _rms_math(xv, g):
    r = lax.rsqrt(jnp.mean(xv * xv, axis=-1, keepdims=True) + EPS)
    return (xv * r * g).astype(BF16)


def _mm_out(y, w, xres, gnext, name):
    s = y.shape[0]
    ts = _tile(s, 1024)

    def body(y_ref, w_ref, x_ref, g_ref, o_ref, h_ref):
        xn = x_ref[...] + jnp.dot(y_ref[...], w_ref[...], preferred_element_type=F32)
        o_ref[...] = xn
        h_ref[...] = _rms_math(xn, g_ref[...])

    return pl.pallas_call(
        body, grid=(s // ts,), name=name,
        in_specs=[pl.BlockSpec((ts, D), lambda i: (i, 0)), pl.BlockSpec((D, D), lambda i: (0, 0)),
                  pl.BlockSpec((ts, D), lambda i: (i, 0)), pl.BlockSpec((1, D), lambda i: (0, 0))],
        out_specs=[pl.BlockSpec((ts, D), lambda i: (i, 0)), pl.BlockSpec((ts, D), lambda i: (i, 0))],
        out_shape=[jax.ShapeDtypeStruct((s, D), F32), jax.ShapeDtypeStruct((s, D), BF16)],
        compiler_params=_params("parallel"),
    )(y, w, xres, gnext)


CONV_ROWS = 32


def _rolled_copies(dst_ref, xe, back):
    n = xe.shape[0]
    dst_ref[0] = xe
    for r in range(1, 8):
        dst_ref[r] = pltpu.roll(xe, n - r if back else r, 0)


def _conv31(rolled_ref, cw_ref, ts, out_ref, bias):
    for o in range(0, ts, CONV_ROWS):
        acc = jnp.zeros((CONV_ROWS, DA), F32) + bias
        for sh in range(BCONV):
            q, r = divmod(sh, 8)
            lo = HALO_B - 8 * q + o
            acc = acc + cw_ref[BCONV - 1 - sh:BCONV - sh, :] * rolled_ref[r, lo:lo + CONV_ROWS, :]
        out_ref[o:o + CONV_ROWS, :] = acc


def _ab_fwd(z, lga, lba, wsm, bs_col, cwb, cbb, lgb, lbb, name):
    s = z.shape[0]
    ts = _tile(s, 256)
    hb = ts // HALO_B

    def body(z_ref, zh_ref, lga_ref, lba_ref, ws_ref, bs_ref, cw_ref, cb_ref, lgb_ref, lbb_ref,
             y_ref, yb2_ref, rolled):
        i = pl.program_id(0)
        z_t = z_ref[...].astype(F32)
        gu = _gelu(z_t[:, 0:DA])
        gv = _gelu(z_t[:, DA:2 * DA])
        vn, _, _ = _ln_fwd(gv, lga_ref[...], lba_ref[...])
        vnb = vn.astype(BF16)
        for c in range(ts // CHUNK):
            for h in range(HEADS):
                rs = slice(c * CHUNK, (c + 1) * CHUNK)
                cs = slice(h * CHUNK, (h + 1) * CHUNK)
                mixed = jnp.dot(ws_ref[h], vnb[rs, cs], preferred_element_type=F32) + bs_ref[h]
                y_ref[rs, cs] = (gu[rs, cs] * mixed).astype(BF16)
        zh = jnp.where(i > 0, zh_ref[...], jnp.zeros_like(zh_ref[...])).astype(F32)
        xb = jnp.concatenate([zh[:, 0:DA], z_t[:, 2 * DA:3 * DA]], axis=0)
        gb = jnp.concatenate([zh[:, DA:2 * DA], z_t[:, 3 * DA:4 * DA]], axis=0)
        _rolled_copies(rolled, xb * _sigmoid(gb), False)
        _conv31(rolled, cw_ref, ts, yb2_ref, cb_ref[...])
        nb_, _, _ = _ln_fwd(yb2_ref[...], lgb_ref[...], lbb_ref[...])
        y_ref[:, DA:2 * DA] = (nb_ * _sigmoid(nb_)).astype(BF16)

    row = lambda i: (0, 0)
    return pl.pallas_call(
        body, grid=(s // ts,), name=name,
        in_specs=[pl.BlockSpec((ts, 4 * DA), lambda i: (i, 0)),
                  pl.BlockSpec((HALO_B, 2 * DA), lambda i: (jnp.maximum(i * hb - 1, 0), 1)),
                  pl.BlockSpec((1, DA), row), pl.BlockSpec((1, DA), row),
                  pl.BlockSpec((HEADS, CHUNK, CHUNK), lambda i: (0, 0, 0)),
                  pl.BlockSpec((HEADS, CHUNK, 1), lambda i: (0, 0, 0)),
                  pl.BlockSpec((BCONV, DA), row), pl.BlockSpec((1, DA), row),
                  pl.BlockSpec((1, DA), row), pl.BlockSpec((1, DA), row)],
        out_specs=[pl.BlockSpec((ts, 2 * DA), lambda i: (i, 0)), pl.BlockSpec((ts, DA), lambda i: (i, 0))],
        out_shape=[jax.ShapeDtypeStruct((s, 2 * DA), BF16), jax.ShapeDtypeStruct((s, DA), F32)],
        scratch_shapes=[pltpu.VMEM((8, ts + HALO_B, DA), F32)],
        compiler_params=_params("parallel"),
    )(z, z, lga, lba, wsm, bs_col, cwb, cbb, lgb, lbb)


def _c_fwd(zc, cw, name):
    s = zc.shape[0]
    ts = _tile(s, 512)
    hb = ts // HALO

    def body(z_ref, ch_ref, xh_ref, cw_ref, r_ref):
        i = pl.program_id(0)
        z_t = z_ref[...].astype(F32)
        ph = jnp.where(i > 0, ch_ref[...].astype(F32) * xh_ref[...].astype(F32), 0.0)
        pe = jnp.concatenate([ph, z_t[:, D:2 * D] * z_t[:, 2 * D:3 * D]], axis=0)
        q, _ = _conv3(pe, [cw_ref[k:k + 1, :] for k in range(3)], HALO)
        r_ref[...] = (z_t[:, 0:D] * q).astype(BF16)

    halo = lambda col: pl.BlockSpec((HALO, D), lambda i: (jnp.maximum(i * hb - 1, 0), col))
    return pl.pallas_call(
        body, grid=(s // ts,), name=name,
        in_specs=[pl.BlockSpec((ts, 3 * D), lambda i: (i, 0)), halo(1), halo(2),
                  pl.BlockSpec((3, D), lambda i: (0, 0))],
        out_specs=pl.BlockSpec((ts, D), lambda i: (i, 0)),
        out_shape=jax.ShapeDtypeStruct((s, D), BF16),
        compiler_params=_params("parallel"),
    )(zc, zc, zc, cw)


def _ffn_fwd(h, xres, wup, fcw, wdn, gnext, name):
    s = h.shape[0]
    ts = _tile(s, 512)
    hb = ts // HALO

    def body(h_ref, hh_ref, w_ref, cw_ref, wd_ref, x_ref, *rest):
        if gnext is not None:
            gn_ref, up_ref, upc_ref, xo_ref, hn_ref, up_s = rest
        else:
            up_ref, upc_ref, xo_ref, up_s = rest
        i = pl.program_id(0)
        m = pl.program_id(1)
        @pl.when(m == 0)
        def _():
            xo_ref[...] = x_ref[...]

        halo = jnp.where(i > 0, hh_ref[...], jnp.zeros_like(hh_ref[...]))
        hx = jnp.concatenate([halo, h_ref[...]], axis=0)
        acts = []
        for gv in range(2):
            up_s[gv] = jnp.dot(hx, w_ref[gv], preferred_element_type=F32)
            x0 = up_s[gv, HALO:HALO + ts, :]
            up_ref[gv] = x0.astype(BF16)
            upc = (cw_ref[gv, 2:3, :] * x0 + cw_ref[gv, 1:2, :] * up_s[gv, HALO - 1:HALO - 1 + ts, :]
                   + cw_ref[gv, 0:1, :] * up_s[gv, HALO - 2:HALO - 2 + ts, :])
            upc_ref[gv] = upc.astype(BF16)
            acts.append(upc)
        a = acts[0] * _sigmoid(acts[0]) * acts[1]
        xo_ref[...] += jnp.dot(a.astype(BF16), wd_ref[...], preferred_element_type=F32)

        if gnext is not None:
            @pl.when(m == NG - 1)
            def _():
                hn_ref[...] = _rms_math(xo_ref[...], gn_ref[...])

    tile = pl.BlockSpec((ts, D), lambda i, m: (i, 0))
    nxt = gnext is not None
    return pl.pallas_call(
        body, grid=(s // ts, NG), name=name,
        in_specs=[tile,
                  pl.BlockSpec((HALO, D), lambda i, m: (jnp.maximum(i * hb - 1, 0), 0)),
                  pl.BlockSpec((2, None, D, FB), lambda i, m: (0, m, 0, 0)),
                  pl.BlockSpec((2, None, 3, FB), lambda i, m: (0, m, 0, 0)),
                  pl.BlockSpec((FB, D), lambda i, m: (m, 0)),
                  tile] + ([pl.BlockSpec((1, D), lambda i, m: (0, 0))] if nxt else []),
        out_specs=[pl.BlockSpec((None, 2, ts, FB), lambda i, m: (m, 0, i, 0)),
                   pl.BlockSpec((None, 2, ts, FB), lambda i, m: (m, 0, i, 0)),
                   tile] + ([tile] if nxt else []),
        out_shape=[jax.ShapeDtypeStruct((NG, 2, s, FB), BF16), jax.ShapeDtypeStruct((NG, 2, s, FB), BF16),
                   jax.ShapeDtypeStruct((s, D), F32)] + ([jax.ShapeDtypeStruct((s, D), BF16)] if nxt else []),
        scratch_shapes=[pltpu.VMEM((2, ts + HALO, FB), F32)],
        compiler_params=_params("arbitrary", "arbitrary"),
    )(h, h, wup, fcw, wdn, xres, *([gnext] if nxt else []))


def _final(x, tgt, g, name):
    s = x.shape[0]
    ts = _tile(s, 512)

    def body(x_ref, t_ref, g_ref, dx_ref, dxb_ref, dg_ref, loss_ref):
        i = pl.program_id(0)
        xv = x_ref[...]
        gv = g_ref[...]
        r = lax.rsqrt(jnp.mean(xv * xv, axis=-1, keepdims=True) + EPS)
        xhat = xv * r
        e = xhat * gv - t_ref[...]
        part = 0.5 * jnp.sum(jnp.mean(e * e, axis=-1, keepdims=True), axis=0, keepdims=True)
        dy = e * (1.0 / D)
        dgp = jnp.sum(dy * xhat, axis=0, keepdims=True)
        u = dy * gv
        dx = r * (u - xhat * jnp.mean(u * xhat, axis=-1, keepdims=True))
        dx_ref[...] = dx
        dxb_ref[...] = dx.astype(BF16)

        @pl.when(i == 0)
        def _():
            dg_ref[...] = dgp
            loss_ref[...] = jnp.broadcast_to(part, (1, 128))

        @pl.when(i > 0)
        def _():
            dg_ref[...] += dgp
            loss_ref[...] += jnp.broadcast_to(part, (1, 128))

    return pl.pallas_call(
        body, grid=(s // ts,), name=name,
        in_specs=[pl.BlockSpec((ts, D), lambda i: (i, 0)), pl.BlockSpec((ts, D), lambda i: (i, 0)),
                  pl.BlockSpec((1, D), lambda i: (0, 0))],
        out_specs=[pl.BlockSpec((ts, D), lambda i: (i, 0)), pl.BlockSpec((ts, D), lambda i: (i, 0)),
                   pl.BlockSpec((1, D), lambda i: (0, 0)), pl.BlockSpec((1, 128), lambda i: (0, 0))],
        out_shape=[jax.ShapeDtypeStruct((s, D), F32), jax.ShapeDtypeStruct((s, D), BF16),
                   jax.ShapeDtypeStruct((1, D), F32), jax.ShapeDtypeStruct((1, 128), F32)],
        compiler_params=_params("arbitrary"),
    )(x, tgt, g)


def _ffn_bwd(df, up, upc, wup, fcw, wdn, xin, g, name):
    s = df.shape[0]
    ts = _tile(s, 512)
    nt = s // ts

    def body(df_ref, up_ref, upc_ref, w_ref, cw_ref, wd_ref, x_ref, g_ref,
             a_ref, dup_ref, dx_ref, dxb_ref, dg_ref, dcw_ref, carry, acc, tacc, dcs_ref):
        i = pl.program_id(0)
        m = pl.program_id(1)
        first = i == 0
        @pl.when(first)
        def _():
            carry[m] = jnp.zeros((2, 8, FB), F32)
            dcw_ref[m] = jnp.zeros((2, 3, FB), F32)

        @pl.when(m == 0)
        def _():
            acc[...] = jnp.zeros((ts, D), F32)

        cws = [[cw_ref[gv, k:k + 1, :] for k in range(3)] for gv in range(2)]
        part = ts // NPART
        das = [lax.dot_general(df_ref[p * part:(p + 1) * part, :].astype(BF16), wd_ref[...], NT_DIMS,
                               preferred_element_type=F32) for p in range(NPART)]

        tacc[...] = jnp.zeros((2, 3, 8, FB), F32)
        dcs_ref[:, ts:ts + 8, :] = carry[m]
        for r in reversed(range(ts // RC)):
            rs = slice(r * RC, (r + 1) * RC)
            gate = upc_ref[0, rs, :].astype(F32)
            val = upc_ref[1, rs, :].astype(F32)
            sg = _sigmoid(gate)
            sl = gate * sg
            a_ref[rs, :] = (sl * val).astype(BF16)
            da_c = das[(r * RC) // part][(r * RC) % part:(r * RC) % part + RC]
            dcs = [da_c * val * _dsilu(gate, sg), da_c * sl]
            for gv in range(2):
                dc = dcs[gv]
                dcs_ref[gv, rs, :] = dc
                d1 = dcs_ref[gv, r * RC + 1:(r + 1) * RC + 1, :]
                d2 = dcs_ref[gv, r * RC + 2:(r + 1) * RC + 2, :]
                du = cws[gv][2] * dc + cws[gv][1] * d1 + cws[gv][0] * d2
                dup_ref[gv, rs, :] = du.astype(BF16)
                x0 = up_ref[gv, rs, :].astype(F32)
                for k, dk in enumerate((d2, d1, dc)):
                    p = x0 * dk
                    tacc[gv, k] += sum(p[j:j + 8] for j in range(0, RC, 8))
            if (r * RC) % part == 0:
                ps = slice(r * RC, r * RC + part)
                acc[ps, :] += (
                    lax.dot_general(dup_ref[0, ps, :], w_ref[0], NT_DIMS, preferred_element_type=F32)
                    + lax.dot_general(dup_ref[1, ps, :], w_ref[1], NT_DIMS, preferred_element_type=F32))
        for gv in range(2):
            carry[m, gv] = dcs_ref[gv, 0:8, :]
            for k in range(3):
                dcw_ref[m, gv, k:k + 1, :] += jnp.sum(tacc[gv, k], axis=0, keepdims=True)

        @pl.when(m == NG - 1)
        def _():
            dx, dgp = _rms_bwd_math(acc[...], x_ref[...], g_ref[...])
            dx = df_ref[...] + dx
            dx_ref[...] = dx
            dxb_ref[...] = dx.astype(BF16)

            @pl.when(first)
            def _():
                dg_ref[...] = dgp

            @pl.when(jnp.logical_not(first))
            def _():
                dg_ref[...] += dgp

    rev = lambda i: nt - 1 - i
    return pl.pallas_call(
        body, grid=(nt, NG), name=name,
        in_specs=[pl.BlockSpec((ts, D), lambda i, m: (rev(i), 0)),
                  pl.BlockSpec((None, 2, ts, FB), lambda i, m: (m, 0, rev(i), 0)),
                  pl.BlockSpec((None, 2, ts, FB), lambda i, m: (m, 0, rev(i), 0)),
                  pl.BlockSpec((2, None, D, FB), lambda i, m: (0, m, 0, 0)),
                  pl.BlockSpec((2, None, 3, FB), lambda i, m: (0, m, 0, 0)),
                  pl.BlockSpec((FB, D), lambda i, m: (m, 0)),
                  pl.BlockSpec((ts, D), lambda i, m: (rev(i), 0)),
                  pl.BlockSpec((1, D), lambda i, m: (0, 0))],
        out_specs=[pl.BlockSpec((None, ts, FB), lambda i, m: (m, rev(i), 0)),
                   pl.BlockSpec((None, 2, ts, FB), lambda i, m: (m, 0, rev(i), 0)),
                   pl.BlockSpec((ts, D), lambda i, m: (rev(i), 0)),
                   pl.BlockSpec((ts, D), lambda i, m: (rev(i), 0)),
                   pl.BlockSpec((1, D), lambda i, m: (0, 0)),
                   pl.BlockSpec((NG, 2, 3, FB), lambda i, m: (0, 0, 0, 0))],
        out_shape=[jax.ShapeDtypeStruct((NG, s, FB), BF16), jax.ShapeDtypeStruct((NG, 2, s, FB), BF16),
                   jax.ShapeDtypeStruct((s, D), F32), jax.ShapeDtypeStruct((s, D), BF16),
                   jax.ShapeDtypeStruct((1, D), F32),
                   jax.ShapeDtypeStruct((NG, 2, 3, FB), F32)],
        scratch_shapes=[pltpu.VMEM((NG, 2, 8, FB), F32), pltpu.VMEM((ts, D), F32),
                        pltpu.VMEM((2, 3, 8, FB), F32), pltpu.VMEM((2, ts + 8, FB), F32)],
        compiler_params=_params("arbitrary", "arbitrary"),
    )(df, up, upc, wup, fcw, wdn, xin, g)


def _mm_nt(dy, w, name):
    s = dy.shape[0]
    ts = _tile(s, 1024)

    def body(dy_ref, w_ref, o_ref):
        o_ref[...] = lax.dot_general(dy_ref[...], w_ref[...], NT_DIMS,
                                     preferred_element_type=F32).astype(BF16)

    return pl.pallas_call(
        body, grid=(s // ts,), name=name,
        in_specs=[pl.BlockSpec((ts, D), lambda i: (i, 0)), pl.BlockSpec((D, D), lambda i: (0, 0))],
        out_specs=pl.BlockSpec((ts, D), lambda i: (i, 0)),
        out_shape=jax.ShapeDtypeStruct((s, D), BF16),
        compiler_params=_params("parallel"),
    )(dy, w)


def _mm_nt_rms(dy, wblk, x, g, dres, bf16_copy, name):
    s = dy.shape[0]
    nb, _, bn = wblk.shape
    pair = _pair(bn)
    ts = _tile(s, 512)

    def body(dy_ref, w_ref, x_ref, g_ref, dr_ref, dx_ref, *rest):
        dg_ref = rest[-1]
        i = pl.program_id(0)
        acc = jnp.zeros((ts, D), F32)
        for b in range(0, nb, pair):
            acc = acc + lax.dot_general(dy_ref[:, b * bn:(b + pair) * bn], _cols(w_ref, b, pair), NT_DIMS,
                                        preferred_element_type=F32)
        dx, dgp = _rms_bwd_math(acc, x_ref[...], g_ref[...])
        dx = dr_ref[...] + dx
        dx_ref[...] = dx
        if bf16_copy:
            rest[0][...] = dx.astype(BF16)

        @pl.when(i == 0)
        def _():
            dg_ref[...] = dgp

        @pl.when(i > 0)
        def _():
            dg_ref[...] += dgp

    tile = pl.BlockSpec((ts, D), lambda i: (i, 0))
    return pl.pallas_call(
        body, grid=(s // ts,), name=name,
        in_specs=[pl.BlockSpec((ts, nb * bn), lambda i: (i, 0)), pl.BlockSpec((nb, D, bn), lambda i: (0, 0, 0)),
                  tile, pl.BlockSpec((1, D), lambda i: (0, 0)), tile],
        out_specs=[tile] + ([tile] if bf16_copy else []) + [pl.BlockSpec((1, D), lambda i: (0, 0))],
        out_shape=[jax.ShapeDtypeStruct((s, D), F32)] + ([jax.ShapeDtypeStruct((s, D), BF16)] if bf16_copy else [])
        + [jax.ShapeDtypeStruct((1, D), F32)],
        compiler_params=_params("arbitrary"),
    )(dy, wblk, x, g, dres)


def _c_bwd(dr, zc, cw, name):
    s = dr.shape[0]
    ts = _tile(s, 512)
    nt = s // ts
    hb = ts // HALO

    def body(dr_ref, drf_ref, z_ref, ch_ref, xh_ref, bf_ref, cw_ref, dz_ref, dcw_ref):
        i = pl.program_id(0)
        cwv = [cw_ref[k:k + 1, :] for k in range(3)]
        z_t = z_ref[...].astype(F32)
        bg, cg, xv = z_t[:, 0:D], z_t[:, D:2 * D], z_t[:, 2 * D:3 * D]
        ph = jnp.where(i > 0, ch_ref[...].astype(F32) * xh_ref[...].astype(F32), 0.0)
        pe = jnp.concatenate([ph, cg * xv], axis=0)
        q, taps = _conv3(pe, cwv, HALO)
        drv = dr_ref[...].astype(F32)
        dq = drv * bg
        dqf = jnp.where(i < nt - 1, drf_ref[...].astype(F32) * bf_ref[...].astype(F32), 0.0)
        dp = _conv3_bwd_in(jnp.concatenate([dq, dqf], axis=0), cwv, ts)
        dz_ref[:, 0:D] = (drv * q).astype(BF16)
        dz_ref[:, D:2 * D] = (dp * xv).astype(BF16)
        dz_ref[:, 2 * D:3 * D] = (dp * cg).astype(BF16)
        rows = _conv3_bwd_w(dq, taps)

        @pl.when(i == 0)
        def _():
            for k in range(3):
                dcw_ref[k:k + 1, :] = rows[k]

        @pl.when(i > 0)
        def _():
            for k in range(3):
                dcw_ref[k:k + 1, :] += rows[k]

    past = lambda col: pl.BlockSpec((HALO, D), lambda i: (jnp.maximum(i * hb - 1, 0), col))
    nxt = lambda i: jnp.minimum((i + 1) * hb, s // HALO - 1)
    return pl.pallas_call(
        body, grid=(nt,), name=name,
        in_specs=[pl.BlockSpec((ts, D), lambda i: (i, 0)),
                  pl.BlockSpec((HALO, D), lambda i: (nxt(i), 0)),
                  pl.BlockSpec((ts, 3 * D), lambda i: (i, 0)), past(1), past(2),
                  pl.BlockSpec((HALO, D), lambda i: (nxt(i), 0)),
                  pl.BlockSpec((3, D), lambda i: (0, 0))],
        out_specs=[pl.BlockSpec((ts, 3 * D), lambda i: (i, 0)), pl.BlockSpec((3, D), lambda i: (0, 0))],
        out_shape=[jax.ShapeDtypeStruct((s, 3 * D), BF16), jax.ShapeDtypeStruct((3, D), F32)],
        compiler_params=_params("arbitrary"),
    )(dr, dr, zc, zc, zc, zc, cw)


G512_ROWS = 40


def _ab_bwd(dy, z, yb2, lga, lba, wsm, bs_col, cwb, lgb, lbb, name):
    s = z.shape[0]
    ts = _tile(s, 256)
    nt = s // ts
    hb = ts // HALO_B
    nch = ts // CHUNK

    def body(z_ref, zh_ref, dy_ref, dyf_ref, yb2_ref, yb2f_ref, lga_ref, lba_ref, ws_ref, bs_ref,
             cw_ref, lgb_ref, lbb_ref, dz_ref, g512_ref, dws_ref, dbs_ref, dvn_ref, fwd_rolled, bwd_rolled, du_s):
        i = pl.program_id(0)
        last = i == nt - 1

        @pl.when(i == 0)
        def _():
            g512_ref[...] = jnp.zeros((G512_ROWS, DA), F32)
            dws_ref[...] = jnp.zeros((HEADS, CHUNK, CHUNK), F32)
            dbs_ref[...] = jnp.zeros((HEADS, CHUNK, 1), F32)

        def add_row(k, v):
            g512_ref[k:k + 1, :] += v

        z_t = z_ref[...].astype(F32)
        dy_t = dy_ref[...].astype(F32)
        ua, va = z_t[:, 0:DA], z_t[:, DA:2 * DA]
        gu = _gelu(ua)
        gv = _gelu(va)
        lga_v = lga_ref[...]
        vn, xhat_a, rstd_a = _ln_fwd(gv, lga_v, lba_ref[...])
        vnb = vn.astype(BF16)
        causal = (lax.broadcasted_iota(jnp.int32, (CHUNK, CHUNK), 0)
                  >= lax.broadcasted_iota(jnp.int32, (CHUNK, CHUNK), 1)).astype(F32)
        for c in range(nch):
            for h in range(HEADS):
                rs = slice(c * CHUNK, (c + 1) * CHUNK)
                cs = slice(h * CHUNK, (h + 1) * CHUNK)
                vblk = vnb[rs, cs]
                mixed = jnp.dot(ws_ref[h], vblk, preferred_element_type=F32) + bs_ref[h]
                dyb_ = dy_t[rs, cs]
                dmix = dyb_ * gu[rs, cs]
                dmb = dmix.astype(BF16)
                dz_ref[rs, cs] = (dyb_ * mixed * _dgelu(ua[rs, cs])).astype(BF16)
                dvn_ref[rs, cs] = lax.dot_general(ws_ref[h], dmb, TN_DIMS, preferred_element_type=F32)
                dws_ref[h] += causal * lax.dot_general(dmb, vblk, NT_DIMS, preferred_element_type=F32)
                dbs_ref[h] += jnp.sum(dmix, axis=1, keepdims=True)
        dvn = dvn_ref[...]
        add_row(0, jnp.sum(dvn * xhat_a, axis=0, keepdims=True))
        add_row(1, jnp.sum(dvn, axis=0, keepdims=True))
        dgv = _ln_bwd(dvn, xhat_a, rstd_a, lga_v)
        dz_ref[:, DA:2 * DA] = (dgv * _dgelu(va)).astype(BF16)
        lgb_v = lgb_ref[...]
        dyb_e = jnp.concatenate(
            [dy_t[:, DA:2 * DA], jnp.where(last, 0.0, dyf_ref[...].astype(F32))], axis=0)
        yb2_e = jnp.concatenate([yb2_ref[...], jnp.where(last, 0.0, yb2f_ref[...])], axis=0)
        n_e, xhat_b, rstd_b = _ln_fwd(yb2_e, lgb_v, lbb_ref[...])
        sgn = _sigmoid(n_e)
        dn = dyb_e * _dsilu(n_e, sgn)
        dy2 = _ln_bwd(dn, xhat_b, rstd_b, lgb_v)
        add_row(2, jnp.sum(dy2[:ts], axis=0, keepdims=True))
        add_row(3, jnp.sum(dn[:ts] * xhat_b[:ts], axis=0, keepdims=True))
        add_row(4, jnp.sum(dn[:ts], axis=0, keepdims=True))
        zh = jnp.where(i > 0, zh_ref[...], jnp.zeros_like(zh_ref[...])).astype(F32)
        xb_t, gb_t = z_t[:, 2 * DA:3 * DA], z_t[:, 3 * DA:4 * DA]
        sgb = _sigmoid(gb_t)
        _rolled_copies(fwd_rolled, jnp.concatenate(
            [zh[:, 0:DA] * _sigmoid(zh[:, DA:2 * DA]), xb_t * sgb], axis=0), False)
        _rolled_copies(bwd_rolled, dy2, True)
        for o in range(0, ts, CONV_ROWS):
            acc = jnp.zeros((CONV_ROWS, DA), F32)
            for sh in range(BCONV):
                q, r = divmod(sh, 8)
                acc = acc + cw_ref[BCONV - 1 - sh:BCONV - sh, :] * bwd_rolled[r, 8 * q + o:8 * q + o + CONV_ROWS, :]
            du_s[o:o + CONV_ROWS, :] = acc
        for sh in range(BCONV):
            q, r = divmod(sh, 8)
            acc = jnp.zeros((CONV_ROWS, DA), F32)
            for o in range(0, ts, CONV_ROWS):
                lo = HALO_B - 8 * q + o
                acc = acc + bwd_rolled[0, o:o + CONV_ROWS, :] * fwd_rolled[r, lo:lo + CONV_ROWS, :]
            add_row(8 + BCONV - 1 - sh, jnp.sum(acc, axis=0, keepdims=True))
        du = du_s[...]
        dz_ref[:, 2 * DA:3 * DA] = (du * sgb).astype(BF16)
        dz_ref[:, 3 * DA:4 * DA] = (du * xb_t * sgb * (1.0 - sgb)).astype(BF16)

    row = lambda i: (0, 0)
    nxt = lambda i: jnp.minimum((i + 1) * hb, s // HALO_B - 1)
    return pl.pallas_call(
        body, grid=(nt,), name=name,
        in_specs=[pl.BlockSpec((ts, 4 * DA), lambda i: (i, 0)),
                  pl.BlockSpec((HALO_B, 2 * DA), lambda i: (jnp.maximum(i * hb - 1, 0), 1)),
                  pl.BlockSpec((ts, 2 * DA), lambda i: (i, 0)),
                  pl.BlockSpec((HALO_B, DA), lambda i: (nxt(i), 1)),
                  pl.BlockSpec((ts, DA), lambda i: (i, 0)),
                  pl.BlockSpec((HALO_B, DA), lambda i: (nxt(i), 0)),
                  pl.BlockSpec((1, DA), row), pl.BlockSpec((1, DA), row),
                  pl.BlockSpec((HEADS, CHUNK, CHUNK), lambda i: (0, 0, 0)),
                  pl.BlockSpec((HEADS, CHUNK, 1), lambda i: (0, 0, 0)),
                  pl.BlockSpec((BCONV, DA), row), pl.BlockSpec((1, DA), row), pl.BlockSpec((1, DA), row)],
        out_specs=[pl.BlockSpec((ts, 4 * DA), lambda i: (i, 0)),
                   pl.BlockSpec((G512_ROWS, DA), row),
                   pl.BlockSpec((HEADS, CHUNK, CHUNK), lambda i: (0, 0, 0)),
                   pl.BlockSpec((HEADS, CHUNK, 1), lambda i: (0, 0, 0))],
        out_shape=[jax.ShapeDtypeStruct((s, 4 * DA), BF16), jax.ShapeDtypeStruct((G512_ROWS, DA), F32),
                   jax.ShapeDtypeStruct((HEADS, CHUNK, CHUNK), F32),
                   jax.ShapeDtypeStruct((HEADS, CHUNK, 1), F32)],
        scratch_shapes=[pltpu.VMEM((ts, DA), F32), pltpu.VMEM((8, ts + HALO_B, DA), F32),
                        pltpu.VMEM((8, ts + HALO_B, DA), F32), pltpu.VMEM((ts, DA), F32)],
        compiler_params=_params("arbitrary"),
    )(z, z, dy, dy, yb2, yb2, lga, lba, wsm, bs_col, cwb, lgb, lbb)


def _dw_cols(a, dy, nb, bn, name):
    s = a.shape[0]
    tm = _tile(s, 2048)
    nt = s // tm
    cpb = 4

    def body(a_ref, dy_ref, o_ref, acc):
        t = pl.program_id(1)
        p = lax.dot_general(a_ref[...], dy_ref[...], TN_DIMS, preferred_element_type=F32)

        @pl.when(t == 0)
        def _():
            for q in range(cpb):
                acc[q] = p[:, q * bn:(q + 1) * bn]

        @pl.when(t > 0)
        def _():
            for q in range(cpb):
                acc[q] += p[:, q * bn:(q + 1) * bn]

        @pl.when(t == nt - 1)
        def _():
            o_ref[...] = acc[...].astype(BF16)

    return pl.pallas_call(
        body, grid=(nb // cpb, nt), name=name,
        in_specs=[pl.BlockSpec((tm, D), lambda j, t: (t, 0)), pl.BlockSpec((tm, cpb * bn), lambda j, t: (t, j))],
        out_specs=pl.BlockSpec((cpb, D, bn), lambda j, t: (j, 0, 0)),
        out_shape=jax.ShapeDtypeStruct((nb, D, bn), BF16),
        scratch_shapes=[pltpu.VMEM((cpb, D, bn), F32)],
        compiler_params=_params("arbitrary", "arbitrary"),
    )(a, dy)


def _dw_rows(a, dy, name):
    s = a.shape[0]
    tm = _tile(s, 4096)
    nt = s // tm
    rb = 512

    def body(a_ref, dy_ref, o_ref, acc):
        t = pl.program_id(1)
        p = lax.dot_general(a_ref[...], dy_ref[...], TN_DIMS, preferred_element_type=F32)

        @pl.when(t == 0)
        def _():
            acc[...] = p

        @pl.when(t > 0)
        def _():
            acc[...] += p

        @pl.when(t == nt - 1)
        def _():
            o_ref[...] = acc[...].astype(BF16)

    return pl.pallas_call(
        body, grid=(D // rb, nt), name=name,
        in_specs=[pl.BlockSpec((tm, rb), lambda j, t: (t, j)), pl.BlockSpec((tm, D), lambda j, t: (t, 0))],
        out_specs=pl.BlockSpec((rb, D), lambda j, t: (j, 0)),
        out_shape=jax.ShapeDtypeStruct((D, D), BF16),
        scratch_shapes=[pltpu.VMEM((rb, D), F32)],
        compiler_params=_params("arbitrary", "arbitrary"),
    )(a, dy)


def _dw_up(h, dup, name):
    s = h.shape[0]
    tm = _tile(s, 4096)
    nt = s // tm

    def body(h_ref, d_ref, o_ref, acc):
        t = pl.program_id(1)
        p = lax.dot_general(d_ref[...], h_ref[...], TN_DIMS, preferred_element_type=F32)

        @pl.when(t == 0)
        def _():
            acc[...] = p

        @pl.when(t > 0)
        def _():
            acc[...] += p

        @pl.when(t == nt - 1)
        def _():
            o_ref[...] = acc[...].astype(BF16)

    return pl.pallas_call(
        body, grid=(NDEV, nt), name=name,
        in_specs=[pl.BlockSpec((tm, D), lambda b, t: (t, 0)),
                  pl.BlockSpec((None, None, tm, FB), lambda b, t: (b % NG, b // NG, t, 0))],
        out_specs=pl.BlockSpec((None, FB, D), lambda b, t: (b, 0, 0)),
        out_shape=jax.ShapeDtypeStruct((NDEV, FB, D), BF16),
        scratch_shapes=[pltpu.VMEM((FB, D), F32)],
        compiler_params=_params("arbitrary", "arbitrary"),
    )(h, dup)


def _dw_dn(a, df, name):
    s = df.shape[0]
    tm = _tile(s, 4096)
    nt = s // tm

    def body(a_ref, d_ref, o_ref, acc):
        t = pl.program_id(1)
        p = lax.dot_general(a_ref[...], d_ref[...], TN_DIMS, preferred_element_type=F32)

        @pl.when(t == 0)
        def _():
            acc[...] = p

        @pl.when(t > 0)
        def _():
            acc[...] += p

        @pl.when(t == nt - 1)
        def _():
            o_ref[...] = acc[...].astype(BF16)

    return pl.pallas_call(
        body, grid=(NG, nt), name=name,
        in_specs=[pl.BlockSpec((None, tm, FB), lambda m, t: (m, t, 0)), pl.BlockSpec((tm, D), lambda m, t: (t, 0))],
        out_specs=pl.BlockSpec((FB, D), lambda m, t: (m, 0)),
        out_shape=jax.ShapeDtypeStruct((DFF, D), BF16),
        scratch_shapes=[pltpu.VMEM((FB, D), F32)],
        compiler_params=_params("arbitrary", "arbitrary"),
    )(a, df)


def _place():
    x, y, c = lax.axis_index("x"), lax.axis_index("y"), lax.axis_index("c")
    chips = [(1 - x, y), (x, 1 - y), (1 - x, 1 - y)]
    return x, y, c, chips


def _zone(shard, dev):
    return lax.dynamic_update_slice(lax.empty((NDEV,) + shard.shape, shard.dtype), shard[None],
                                    (dev,) + (0,) * shard.ndim)


HBM_SPEC = pl.BlockSpec(memory_space=pltpu.HBM)
SEM_SPEC = pl.BlockSpec(memory_space=pltpu.SEMAPHORE)
DATAFLOW = pltpu.SideEffectType.DATAFLOW_SIDE_EFFECTING


def _hbm(a):
    return pltpu.with_memory_space_constraint(a, pltpu.HBM)


def _hbm_like(arrs):
    return [pltpu.HBM(a.shape, a.dtype) for a in arrs]


def _ag_start(srcs, lands, after, name):
    n = len(srcs)
    ns = 8 * n

    def body(*refs):
        src, land = refs[:n], refs[n:2 * n]
        sems = refs[2 * n + 1:2 * n + 1 + ns]
        token = refs[-1]
        x, y, c, chips = _place()
        peers = [(x, y, 1 - c)] + [(*chip, c) for chip in chips]
        for t in range(n):
            for k, to in enumerate(peers):
                pltpu.make_async_remote_copy(
                    src_ref=src[t], dst_ref=land[t].at[4 * x + 2 * y + c],
                    send_sem=sems[2 * (4 * t + k)], recv_sem=sems[2 * (4 * t + k) + 1],
                    device_id=to, device_id_type=MESH).start()
        token[...] = jnp.zeros_like(token)

    res = pl.pallas_call(
        body, name=name,
        in_specs=[HBM_SPEC] * (2 * n) + [ANY],
        out_specs=[SEM_SPEC] * ns + [HBM_SPEC] * (2 * n) + [pl.BlockSpec(memory_space=pltpu.VMEM)],
        out_shape=[pltpu.SemaphoreType.DMA(())] * ns + _hbm_like(srcs) + _hbm_like(lands)
        + [jax.ShapeDtypeStruct((8, 128), F32)],
        input_output_aliases={i: ns + i for i in range(2 * n)},
        compiler_params=pltpu.CompilerParams(has_side_effects=DATAFLOW),
    )(*[_hbm(a) for a in srcs], *[_hbm(a) for a in lands], after)
    sems = [[(res[2 * (4 * t + k)], res[2 * (4 * t + k) + 1]) for k in range(4)] for t in range(n)]
    return sems, res[ns:ns + n], res[ns + n:ns + 2 * n], res[-1]


def _ag_forward(srcs, lands, sems1, after, name):
    n = len(srcs)
    flat1 = [s for t in range(n) for k in range(1, 4) for s in sems1[t][k]]
    n1 = len(flat1)

    def body(*refs):
        src, land = refs[:n], refs[n:2 * n]
        s1 = refs[2 * n:2 * n + n1]
        s2 = refs[2 * n + n1 + 1:2 * n + n1 + 1 + 6 * n]
        x, y, c, chips = _place()
        for j, (cx, cy) in enumerate(chips):
            for t in range(n):
                blk = land[t].at[4 * cx + 2 * cy + c]
                pltpu.make_async_remote_copy(
                    src_ref=src[t], dst_ref=blk, send_sem=s1[2 * (3 * t + j)], recv_sem=s1[2 * (3 * t + j) + 1],
                    device_id=(cx, cy, c), device_id_type=MESH).wait_recv()
                pltpu.make_async_remote_copy(
                    src_ref=blk, dst_ref=blk, send_sem=s2[2 * (3 * t + j)], recv_sem=s2[2 * (3 * t + j) + 1],
                    device_id=(x, y, 1 - c), device_id_type=MESH).start()

    res = pl.pallas_call(
        body, name=name,
        in_specs=[HBM_SPEC] * (2 * n) + [SEM_SPEC] * n1 + [ANY],
        out_specs=[SEM_SPEC] * (6 * n) + [HBM_SPEC] * n,
        out_shape=[pltpu.SemaphoreType.DMA(())] * (6 * n) + _hbm_like(lands),
        input_output_aliases={n + i: 6 * n + i for i in range(n)},
        compiler_params=pltpu.CompilerParams(has_side_effects=DATAFLOW),
    )(*srcs, *lands, *flat1, after)
    sems2 = [[(res[2 * (3 * t + j)], res[2 * (3 * t + j) + 1]) for j in range(3)] for t in range(n)]
    return sems2, res[6 * n:]


def _ag_finish(srcs, lands, sems1, sems2, after, name):
    n = len(srcs)
    flat1 = [s for t in range(n) for k in range(4) for s in sems1[t][k]]
    flat2 = [s for t in range(n) for j in range(3) for s in sems2[t][j]]
    n1, n2 = len(flat1), len(flat2)

    def body(*refs):
        src, land = refs[:n], refs[n:2 * n]
        s1 = refs[2 * n:2 * n + n1]
        s2 = refs[2 * n + n1:2 * n + n1 + n2]
        x, y, c, chips = _place()
        sib = (x, y, 1 - c)
        for t in range(n):
            own = land[t].at[4 * x + 2 * y + 1 - c]
            pltpu.make_async_remote_copy(
                src_ref=src[t], dst_ref=own, send_sem=s1[8 * t], recv_sem=s1[8 * t + 1],
                device_id=sib, device_id_type=MESH).wait_recv()
            for k in range(4):
                pltpu.make_async_remote_copy(
                    src_ref=src[t], dst_ref=own, send_sem=s1[2 * (4 * t + k)], recv_sem=s1[2 * (4 * t + k) + 1],
                    device_id=sib, device_id_type=MESH).wait_send()
            for j, (cx, cy) in enumerate(chips):
                blk = land[t].at[4 * cx + 2 * cy + 1 - c]
                cp = pltpu.make_async_remote_copy(
                    src_ref=blk, dst_ref=blk, send_sem=s2[2 * (3 * t + j)], recv_sem=s2[2 * (3 * t + j) + 1],
                    device_id=sib, device_id_type=MESH)
                cp.wait_send()
                cp.wait_recv()

    return pl.pallas_call(
        body, name=name,
        in_specs=[HBM_SPEC] * (2 * n) + [SEM_SPEC] * (n1 + n2) + [ANY],
        out_specs=[HBM_SPEC] * n,
        out_shape=_hbm_like(lands),
        input_output_aliases={n + i: i for i in range(n)},
        compiler_params=pltpu.CompilerParams(has_side_effects=DATAFLOW),
    )(*srcs, *lands, *flat1, *flat2, after)


def _pair_copies(srcs, dsts, sems):
    x, y, c, _ = _place()
    nt = len(srcs)
    return [pltpu.make_async_remote_copy(
        src_ref=srcs[t].at[2 * j + 1 - c], dst_ref=dsts[t].at[j],
        send_sem=sems[2 * (NCHIP * t + j)], recv_sem=sems[2 * (NCHIP * t + j) + 1],
        device_id=(x, y, 1 - c), device_id_type=MESH) for t in range(nt) for j in range(NCHIP)]


def _pair_start(grads, carry, name):
    nt = len(grads)
    ns = 2 * NCHIP * nt
    zones = [_hbm(lax.empty((NCHIP,) + a.shape[1:], a.dtype)) for a in grads]
    extra = [] if carry is None else [_hbm(carry)]
    ne = len(extra)

    def body(*refs):
        for cp in _pair_copies(refs[:nt], refs[nt:2 * nt], refs[2 * nt + ne:2 * nt + ne + ns]):
            cp.start()

    res = pl.pallas_call(
        body, name=name,
        in_specs=[HBM_SPEC] * (2 * nt + ne),
        out_specs=[SEM_SPEC] * ns + [HBM_SPEC] * (2 * nt + ne),
        out_shape=[pltpu.SemaphoreType.DMA(())] * ns + _hbm_like(grads) + _hbm_like(zones) + _hbm_like(extra),
        input_output_aliases={i: ns + i for i in range(2 * nt + ne)},
        compiler_params=pltpu.CompilerParams(has_side_effects=DATAFLOW),
    )(*[_hbm(a) for a in grads], *zones, *extra)
    handle = (list(res[:ns]), list(res[ns:ns + nt]), list(res[ns + nt:ns + 2 * nt]))
    return handle, (res[ns + 2 * nt] if ne else None)


def _pair_wait(handle, after, name):
    sems, srcs, zones = handle
    nt, ns = len(srcs), len(sems)

    def body(*refs):
        for cp in _pair_copies(refs[:nt], refs[nt:2 * nt], refs[2 * nt:2 * nt + ns]):
            cp.wait_send()
            cp.wait_recv()

    return pl.pallas_call(
        body, name=name,
        in_specs=[HBM_SPEC] * (2 * nt) + [SEM_SPEC] * ns + [ANY],
        out_specs=[HBM_SPEC] * nt,
        out_shape=_hbm_like(zones),
        input_output_aliases={nt + i: i for i in range(nt)},
        compiler_params=pltpu.CompilerParams(has_side_effects=DATAFLOW),
    )(*srcs, *zones, *sems, after)


def _rows_tile(r, row_bytes, cap_bytes):
    best = None
    for tr in range(16, r + 1, 16):
        if r % tr == 0 and tr * row_bytes <= cap_bytes:
            best = tr
    return best if best is not None else r


def _pair_sum(own, got, cidx, name):
    _, _, r, cdim = own.shape
    tr = _rows_tile(r, 2 * cdim, 2 * 1024 * 1024)

    def body(c_ref, a_ref, b_ref, o_ref):
        o_ref[...] = (a_ref[...].astype(F32) + b_ref[...].astype(F32)).astype(BF16)

    return pl.pallas_call(
        body, name=name,
        grid_spec=pltpu.PrefetchScalarGridSpec(
            num_scalar_prefetch=1, grid=(NCHIP, r // tr),
            in_specs=[pl.BlockSpec((None, None, tr, cdim), lambda j, i, c_ref: (j, c_ref[0], i, 0)),
                      pl.BlockSpec((None, tr, cdim), lambda j, i, c_ref: (j, i, 0))],
            out_specs=pl.BlockSpec((None, tr, cdim), lambda j, i, c_ref: (j, i, 0))),
        out_shape=jax.ShapeDtypeStruct((NCHIP, r, cdim), BF16),
        compiler_params=_params("arbitrary", "arbitrary"),
    )(cidx, own, got)


def _chip_copies(srcs, zones, slots, sems):
    x, y, c, chips = _place()
    out = []
    for t, (z, l) in enumerate(slots):
        for k, (cx, cy) in enumerate(chips):
            dst = zones[z].at[k] if l is None else zones[z].at[k, l]
            out.append(pltpu.make_async_remote_copy(
                src_ref=srcs[t].at[2 * cx + cy], dst_ref=dst,
                send_sem=sems[2 * (3 * t + k)], recv_sem=sems[2 * (3 * t + k) + 1],
                device_id=(cx, cy, c), device_id_type=MESH))
    return out


def _chip_start(sums, zones, slots, carry, name):
    nt, nz = len(sums), len(zones)
    ns = 6 * nt
    extra = [] if carry is None else [_hbm(carry)]
    ne = len(extra)

    def body(*refs):
        for cp in _chip_copies(refs[:nt], refs[nt:nt + nz], slots, refs[nt + nz + ne:nt + nz + ne + ns]):
            cp.start()

    res = pl.pallas_call(
        body, name=name,
        in_specs=[HBM_SPEC] * (nt + nz + ne),
        out_specs=[SEM_SPEC] * ns + [HBM_SPEC] * (nt + nz + ne),
        out_shape=[pltpu.SemaphoreType.DMA(())] * ns + _hbm_like(sums) + _hbm_like(zones) + _hbm_like(extra),
        input_output_aliases={i: ns + i for i in range(nt + nz + ne)},
        compiler_params=pltpu.CompilerParams(has_side_effects=DATAFLOW),
    )(*[_hbm(a) for a in sums], *zones, *extra)
    return (list(res[:ns]), list(res[ns:ns + nt]), list(res[ns + nt:ns + nt + nz]),
            (res[ns + nt + nz] if ne else None))


def _chip_wait(started, zones, zone_ids, after, name):
    started = [(sums, [(zone_ids.index(z), l) for z, l in slots], sems) for sums, slots, sems in started]
    nz = len(zones)
    flat_src = [a for sums, _, _ in started for a in sums]
    flat_sem = [s for _, _, sems in started for s in sems]
    n_src, n_sem = len(flat_src), len(flat_sem)

    def body(*refs):
        srcs, zs, sems = refs[:n_src], refs[n_src:n_src + nz], refs[n_src + nz:n_src + nz + n_sem]
        so, se = 0, 0
        for sums, slots, sem_list in started:
            for cp in _chip_copies(srcs[so:so + len(sums)], zs, slots, sems[se:se + len(sem_list)]):
                cp.wait_send()
                cp.wait_recv()
            so += len(sums)
            se += len(sem_list)

    return pl.pallas_call(
        body, name=name,
        in_specs=[HBM_SPEC] * (n_src + nz) + [SEM_SPEC] * n_sem + [ANY],
        out_specs=[HBM_SPEC] * nz,
        out_shape=_hbm_like(zones),
        input_output_aliases={n_src + i: i for i in range(nz)},
        compiler_params=pltpu.CompilerParams(has_side_effects=DATAFLOW),
    )(*flat_src, *zones, *flat_sem, after)


def _small_allreduce(parts, y_first, after, name):
    nt = len(parts)

    def body(*refs):
        srcs, outs, bufs = refs[:nt], refs[nt + 1:2 * nt + 1], refs[2 * nt + 1:3 * nt + 1]
        send_sems, recv_sems = refs[3 * nt + 1:]
        x, y, c, _ = _place()
        along = {"c": (x, y, 1 - c), "x": (1 - x, y, c), "y": (x, 1 - y, c)}
        for t in range(nt):
            outs[t][...] = srcs[t][...]
        for step in range(3):
            order = [("c", "y", "x") if t in y_first else ("c", "x", "y") for t in range(nt)]
            copies = [pltpu.make_async_remote_copy(
                src_ref=outs[t], dst_ref=bufs[t].at[step],
                send_sem=send_sems.at[step, t], recv_sem=recv_sems.at[step, t],
                device_id=along[order[t][step]], device_id_type=MESH) for t in range(nt)]
            for cp in copies:
                cp.start()
            for cp in copies:
                cp.wait()
            for t in range(nt):
                outs[t][...] = outs[t][...] + bufs[t][step]

    vm = pl.BlockSpec(memory_space=pltpu.VMEM)
    return pl.pallas_call(
        body, name=name,
        in_specs=[vm] * nt + [ANY], out_specs=[vm] * nt,
        out_shape=[jax.ShapeDtypeStruct(a.shape, F32) for a in parts],
        scratch_shapes=[pltpu.VMEM((3,) + a.shape, F32) for a in parts]
        + [pltpu.SemaphoreType.DMA((3, nt)), pltpu.SemaphoreType.DMA((3, nt))],
        compiler_params=pltpu.CompilerParams(has_side_effects=True, vmem_limit_bytes=VMEM_LIMIT),
    )(*parts, after)


def _adam_math(w, g, m, v):
    m2 = ADAM_B1 * m + (1.0 - ADAM_B1) * g
    v2 = ADAM_B2 * v + (1.0 - ADAM_B2) * (g * g)
    m_hat = m2 / (1.0 - ADAM_B1 ** ADAM_STEP)
    v_hat = v2 / (1.0 - ADAM_B2 ** ADAM_STEP)
    delta = -ADAM_LR * (m_hat / (jnp.sqrt(v_hat) + ADAM_EPS) + ADAM_WD * w)
    return delta, m2, v2


def _adam_big(w, m, v, parts, mine, chip, name):
    nl, r, cdim = w.shape
    tr = _rows_tile(r, 4 * cdim, 3 * 512 * 1024)

    def body(c_ref, w_ref, m_ref, v_ref, p_ref, *rest):
        mine_refs, (g_ref, d_ref, mo_ref, vo_ref) = rest[:nl], rest[nl:]
        own = mine_refs[0][...]
        for l in range(1, nl):
            own = jnp.where(pl.program_id(0) == l, mine_refs[l][...], own)
        g = ((p_ref[0].astype(F32) + p_ref[1].astype(F32)) + p_ref[2].astype(F32)) + own.astype(F32)
        delta, m2, v2 = _adam_math(w_ref[...], g, m_ref[...], v_ref[...])
        g_ref[...] = g
        d_ref[...] = delta
        mo_ref[...] = m2
        vo_ref[...] = v2

    spec = pl.BlockSpec((None, tr, cdim), lambda l, i, c_ref: (l, i, 0))
    mine_specs = [pl.BlockSpec((None, tr, cdim), lambda l, i, c_ref, ll=ll: (c_ref[0], jnp.where(l == ll, i, 0), 0))
                  for ll in range(nl)]
    return pl.pallas_call(
        body, name=name,
        grid_spec=pltpu.PrefetchScalarGridSpec(
            num_scalar_prefetch=1, grid=(nl, r // tr),
            in_specs=[spec, spec, spec, pl.BlockSpec((3, None, tr, cdim), lambda l, i, c_ref: (0, l, i, 0))]
            + mine_specs,
            out_specs=[spec] * 4),
        out_shape=[jax.ShapeDtypeStruct(w.shape, F32)] * 4,
        compiler_params=_params("arbitrary", "arbitrary"),
    )(chip, w, m, v, parts, *mine)


def _adam_small(ws, gs, ms, vs, name):
    n = len(ws)

    def body(*refs):
        w_r, g_r, m_r, v_r = refs[:n], refs[n:2 * n], refs[2 * n:3 * n], refs[3 * n:4 * n]
        d_o, m_o, v_o = refs[4 * n:5 * n], refs[5 * n:6 * n], refs[6 * n:7 * n]
        for t in range(n):
            delta, m2, v2 = _adam_math(w_r[t][...], g_r[t][...], m_r[t][...], v_r[t][...])
            d_o[t][...] = delta
            m_o[t][...] = m2
            v_o[t][...] = v2

    vm = pl.BlockSpec(memory_space=pltpu.VMEM)
    shapes = [jax.ShapeDtypeStruct(a.shape, F32) for a in ws]
    return pl.pallas_call(
        body, name=name, in_specs=[vm] * (4 * n), out_specs=[vm] * (3 * n), out_shape=shapes * 3,
        compiler_params=pltpu.CompilerParams(vmem_limit_bytes=VMEM_LIMIT),
    )(*ws, *gs, *ms, *vs)


def kernel(x, norm_mix, norm_ffn, norm_final, ab_w_in, a_ln_g, a_ln_b, a_w_s, a_b_s, b_conv_w, b_conv_b, b_ln_g, b_ln_b, ab_w_out, c_w_in, c_conv_w, c_w_out, f_w_up, f_conv_w, f_w_down, loss_target, m_norm_mix, m_norm_ffn, m_norm_final, m_ab_w_in, m_a_ln_g, m_a_ln_b, m_a_w_s, m_a_b_s, m_b_conv_w, m_b_conv_b, m_b_ln_g, m_b_ln_b, m_ab_w_out, m_c_w_in, m_c_conv_w, m_c_w_out, m_f_w_up, m_f_conv_w, m_f_w_down, v_norm_mix, v_norm_ffn, v_norm_final, v_ab_w_in, v_a_ln_g, v_a_ln_b, v_a_w_s, v_a_b_s, v_b_conv_w, v_b_conv_b, v_b_ln_g, v_b_ln_b, v_ab_w_out, v_c_w_in, v_c_conv_w, v_c_w_out, v_f_w_up, v_f_conv_w, v_f_w_down):
    s = x.shape[1]
    x0 = x.reshape(s, D)
    tgt = loss_target.reshape(s, D)
    xi, yi, ci = lax.axis_index("x"), lax.axis_index("y"), lax.axis_index("c")
    dev = 4 * xi + 2 * yi + ci
    cidx = ci.astype(jnp.int32).reshape(1)

    bf = lambda a: a.astype(BF16)
    first = [bf(ab_w_in[0])]
    sems_a, first, lands_a, token_a = _ag_start(first, [_zone(first[0], dev)], x0, "ag_start_w_in")
    dev_b = dev + token_a[0, 0].astype(jnp.int32)
    slab_w = 6 * CHUNK
    pad = lambda a, rows: jnp.pad(a, ((0, rows - a.shape[0]), (0, slab_w - a.shape[1])))
    slab = jnp.concatenate([pad(b_conv_w[0], 32), pad(c_conv_w[0], 8), pad(f_conv_w.reshape(6, FB), 8)], axis=0)
    later = [bf(ab_w_out[0]), slab, bf(f_w_up[0]), bf(f_w_down[0]), bf(c_w_in[0]), bf(c_w_out[0]),
             bf(f_w_up[1]), bf(f_w_down[1])]
    sems_b, later, lands_b, ag_token = _ag_start(later, [_zone(a, dev_b) for a in later], token_a, "ag_start")
    ag_sems, later, lands = sems_a + sems_b, list(first) + list(later), list(lands_a) + list(lands_b)
    groups = [[0], [1, 2], [3, 4], [5, 6], [7, 8]]

    causal = jnp.tril(jnp.ones((CHUNK, CHUNK), F32))
    wsm = (a_w_s[0] * causal).astype(BF16)
    bs_col = a_b_s.reshape(HEADS, CHUNK, 1)
    nm = [norm_mix[0:1], norm_mix[1:2]]
    nf = [norm_ffn[0:1], norm_ffn[1:2]]
    nfin = norm_final.reshape(1, D)

    def arrive(g, after_ici, after_d2d, tag):
        srcs = [later[t] for t in groups[g]]
        zone = [lands[t] for t in groups[g]]
        sems1 = [ag_sems[t] for t in groups[g]]
        sems2, zone = _ag_forward(srcs, zone, sems1, after_ici, "ag_forward_" + tag)
        return _ag_finish(srcs, zone, sems1, sems2, after_d2d, "ag_finish_" + tag)

    h0 = _rms_fwd(x0, nm[0], "rms_mix0", after=ag_token)
    (win0,) = arrive(0, h0, h0, "w_in")
    z = _mm_in(h0, win0, "mm_ab_in")
    wout0, slab_g = arrive(1, z, z, "first")
    wout0 = wout0.reshape(D, D)
    bcw = jnp.transpose(slab_g[:, 0:BCONV, 0:DA // NDEV], (1, 0, 2)).reshape(BCONV, DA)
    ccw = jnp.transpose(slab_g[:, 32:35, 0:D // NDEV], (1, 0, 2)).reshape(3, D)
    fcw_g = slab_g[:, 40:46, 0:FB].reshape(2, NG, 2, 3, FB)
    fcws = [fcw_g[:, :, 0], fcw_g[:, :, 1]]
    ycat, yb2 = _ab_fwd(z, a_ln_g, a_ln_b, wsm, bs_col, bcw, b_conv_b, b_ln_g, b_ln_b, "ab_fwd")
    x1, h1 = _mm_out(ycat, wout0, x0, nf[0], "mm_ab_out")
    wup0, wdn0 = arrive(2, x1, x1, "ffn0")
    up0, upc0, x2, h2 = _ffn_fwd(h1, x1, wup0.reshape(2, NG, D, FB), fcws[0], wdn0.reshape(DFF, D), nm[1],
                                 "ffn_fwd0")
    cin, cout = arrive(3, x2, x2, "c")
    cout = cout.reshape(D, D)
    zc = _mm_in(h2, cin, "mm_c_in")
    rc = _c_fwd(zc, ccw, "c_fwd")
    x3, h3 = _mm_out(rc, cout, x2, nf[1], "mm_c_out")
    wup1, wdn1 = arrive(4, rc, x3, "ffn1")
    wups = [wup0.reshape(2, NG, D, FB), wup1.reshape(2, NG, D, FB)]
    wdns = [wdn0.reshape(DFF, D), wdn1.reshape(DFF, D)]
    up1, upc1, x4 = _ffn_fwd(h3, x3, wups[1], fcws[1], wdns[1], None, "ffn_fwd1")
    dx4, dx4b, dnfin, loss_part = _final(x4, tgt, nfin, "final_loss")

    zshape = lambda *sh: _hbm(lax.empty((3,) + sh, BF16))
    zones = [zshape(D, 2 * D // NDEV), zshape(D // NDEV, D), zshape(D, 3 * D // NDEV), zshape(D // NDEV, D),
             zshape(2, FB, D), zshape(2, DFF // NDEV, D)]
    started = []

    def pair_sums(grads, handle, after, tag):
        del grads
        got = _pair_wait(handle, after, "rs_pair_wait_" + tag)
        return [_pair_sum(b.reshape((NCHIP, 2) + b.shape[1:]), g, cidx, "rs_pair_sum_%s%d" % (tag, t))
                for t, (b, g) in enumerate(zip(handle[1], got))]

    def chip_start(sums, slots, carry, tag):
        sems, sums, new_zones, carry = _chip_start(sums, zones, slots, carry, "rs_chip_start_" + tag)
        zones[:] = new_zones
        started.append((sums, slots, sems))
        return sums, carry

    rows8 = lambda g, r: g.reshape(NDEV, r, D)
    a1, dup1, dx3, dx3b, dnf1, dfcw1 = _ffn_bwd(dx4, up1, upc1, wups[1], fcws[1], wdns[1], x3, nf[1], "ffn_bwd1")
    g_f1 = [_dw_up(h3, dup1, "dw_up1"), rows8(_dw_dn(a1, dx4b, "dw_dn1"), DFF // NDEV)]
    hd_f1, dx3b = _pair_start(g_f1, dx3b, "rs_pair_start_f1")
    drc = _mm_nt(dx3b, cout, "mm_c_out_bwd")
    g_cout = rows8(_dw_rows(rc, dx3b, "dw_c_out"), D // NDEV)
    s_f1 = pair_sums(g_f1, hd_f1, g_cout, "f1")
    s_f1, drc = chip_start(s_f1, [(4, 1), (5, 1)], drc, "f1")
    dzc, dccw = _c_bwd(drc, zc, ccw, "c_bwd")
    dx2, dx2b, dnm1 = _mm_nt_rms(dzc, cin, x2, nm[1], dx3, True, "mm_c_in_bwd")
    g_c = [_dw_cols(h2, dzc, NDEV, 3 * D // NDEV, "dw_c_in"), g_cout]
    hd_c, dx2 = _pair_start(g_c, dx2, "rs_pair_start_c")
    a0, dup0, dx1, dx1b, dnf0, dfcw0 = _ffn_bwd(dx2, up0, upc0, wups[0], fcws[0], wdns[0], x1, nf[0], "ffn_bwd0")
    s_c = pair_sums(g_c, hd_c, dx1b, "c")
    s_c, dx1b = chip_start(s_c, [(2, None), (3, None)], dx1b, "c")
    g_f0 = [_dw_up(h1, dup0, "dw_up0"), rows8(_dw_dn(a0, dx2b, "dw_dn0"), DFF // NDEV)]
    hd_f0, dx1b = _pair_start(g_f0, dx1b, "rs_pair_start_f0")
    dycat = _mm_nt(dx1b, wout0, "mm_ab_out_bwd")
    g_wout0 = rows8(_dw_rows(ycat, dx1b, "dw_ab_out"), D // NDEV)
    s_f0 = pair_sums(g_f0, hd_f0, g_wout0, "f0")
    s_f0, dycat = chip_start(s_f0, [(4, 0), (5, 0)], dycat, "f0")
    dz, g512, dws, dbs = _ab_bwd(dycat, z, yb2, a_ln_g, a_ln_b, wsm, bs_col, bcw, b_ln_g, b_ln_b, "ab_bwd")
    grad_x, dnm0 = _mm_nt_rms(dz, win0, x0, nm[0], dx1, False, "mm_ab_in_bwd")
    g_ab = [_dw_cols(h0, dz, NDEV, 2 * D // NDEV, "dw_ab_in"), g_wout0]
    hd_ab, _ = _pair_start(g_ab, None, "rs_pair_start_ab")

    g1024 = jnp.concatenate([dnm0, dnm1, dnf0, dnf1, dnfin, dccw], axis=0)
    gfc = jnp.concatenate([dfcw0, dfcw1], axis=0).reshape(2 * NG * 2 * 3, FB)
    g1024, g512, dws, dbs, gfc, loss_sum = _small_allreduce(
        [g1024, g512, dws.reshape(HEADS * CHUNK, CHUNK), dbs.reshape(HEADS, CHUNK), gfc, loss_part], (2,),
        hd_ab[1][0], "small_allreduce")
    loss = loss_sum[0, 0]
    s_ab = pair_sums(g_ab, hd_ab, g1024, "ab")
    s_ab, _ = chip_start(s_ab, [(0, None), (1, None)], None, "ab")
    p_cin, p_cout, p_wup, p_wdn = _chip_wait(started[:3], zones[2:], [2, 3, 4, 5], s_ab[0], "rs_chip_wait_early")

    chip = (2 * xi + yi).astype(jnp.int32).reshape(1)

    def big_update(w, m, v, parts, mine, name):
        shp = w.shape
        w3, m3, v3 = (a.reshape((-1,) + shp[-2:]) for a in (w, m, v))
        p4 = parts.reshape((3,) + w3.shape)
        return [o.reshape(shp) for o in _adam_big(w3, m3, v3, p4, mine, chip, name)]

    u_cin = big_update(c_w_in, m_c_w_in, v_c_w_in, p_cin, [s_c[0]], "adam_c_w_in")
    u_cout = big_update(c_w_out, m_c_w_out, v_c_w_out, p_cout, [s_c[1]], "adam_c_w_out")
    tr_ = lambda a: jnp.swapaxes(a, 1, 2)
    u_wup = [tr_(o) for o in big_update(tr_(f_w_up), tr_(m_f_w_up), tr_(v_f_w_up), p_wup,
                                        [s_f0[0], s_f1[0]], "adam_f_w_up")]
    u_wdn = big_update(f_w_down, m_f_w_down, v_f_w_down, p_wdn, [s_f0[1], s_f1[1]], "adam_f_w_down")
    p_win0, p_wout0 = _chip_wait(started[3:], zones[:2], [0, 1], u_wdn[0], "rs_chip_wait_late")
    u_win0 = big_update(ab_w_in, m_ab_w_in, v_ab_w_in, p_win0, [s_ab[0]], "adam_ab_w_in")
    u_wout0 = big_update(ab_w_out, m_ab_w_out, v_ab_w_out, p_wout0, [s_ab[1]], "adam_ab_w_out")

    g_norm_mix = g1024[0:2]
    g_norm_ffn = g1024[2:4]
    g_norm_final = g1024[4:5]
    g_ccw = lax.dynamic_slice(g1024[5:8], (0, dev * (D // NDEV)), (3, D // NDEV))
    g_bcw = lax.dynamic_slice(g512[8:8 + BCONV], (0, dev * (DA // NDEV)), (BCONV, DA // NDEV))
    gfc = gfc.reshape(2, NG, 2, 3, FB)
    g_fcw = lax.dynamic_slice(gfc, (0, dev % NG, dev // NG, 0, 0), (2, 1, 1, 3, FB)).reshape(2, 3, FB)
    small_w = [norm_mix, norm_ffn, nfin, a_ln_g, a_ln_b, a_w_s[0], a_b_s[0], b_conv_w[0], b_conv_b,
               b_ln_g, b_ln_b, c_conv_w[0], f_conv_w]
    small_g = [g_norm_mix, g_norm_ffn, g_norm_final, g512[0:1], g512[1:2],
               dws.reshape(HEADS, CHUNK, CHUNK), dbs, g_bcw, g512[2:3],
               g512[3:4], g512[4:5], g_ccw, g_fcw]
    small_m = [m_norm_mix, m_norm_ffn, m_norm_final.reshape(1, D), m_a_ln_g, m_a_ln_b, m_a_w_s[0], m_a_b_s[0],
               m_b_conv_w[0], m_b_conv_b, m_b_ln_g, m_b_ln_b, m_c_conv_w[0], m_f_conv_w]
    small_v = [v_norm_mix, v_norm_ffn, v_norm_final.reshape(1, D), v_a_ln_g, v_a_ln_b, v_a_w_s[0], v_a_b_s[0],
               v_b_conv_w[0], v_b_conv_b, v_b_ln_g, v_b_ln_b, v_c_conv_w[0], v_f_conv_w]
    upd = _adam_small(small_w, small_g, small_m, small_v, "adam_small")
    ns = len(small_w)
    orig = [norm_mix, norm_ffn, norm_final, a_ln_g, a_ln_b, a_w_s, a_b_s, b_conv_w, b_conv_b,
            b_ln_g, b_ln_b, c_conv_w, f_conv_w]
    sg_out = [g.reshape(o.shape) for g, o in zip(small_g, orig)]
    sd_out = [a.reshape(o.shape) for a, o in zip(upd[0:ns], orig)]
    sm_out = [a.reshape(o.shape) for a, o in zip(upd[ns:2 * ns], orig)]
    sv_out = [a.reshape(o.shape) for a, o in zip(upd[2 * ns:3 * ns], orig)]

    def assemble(small, k):
        return [small[0], small[1], small[2], u_win0[k], small[3], small[4], small[5], small[6], small[7],
                small[8], small[9], small[10], u_wout0[k], u_cin[k], small[11], u_cout[k], u_wup[k],
                small[12], u_wdn[k]]

    grads = assemble(sg_out, 0)
    deltas = assemble(sd_out, 1)
    new_m = assemble(sm_out, 2)
    new_v = assemble(sv_out, 3)
    return (loss, grad_x.reshape(1, s, D), *grads, *deltas, *new_m, *new_v)
```

```python
import math

import jax
import jax.numpy as jnp
from jax import lax
from jax.experimental import pallas as pl
from jax.experimental.pallas import tpu as pltpu

F32 = jnp.float32
BF16 = jnp.bfloat16

D = 1024
DA = 512
HEADS = 4
CHUNK = 128
DFF = 2816
NDEV = 8
NCHIP = 4
FB = DFF * 2 // NDEV
NG = DFF // FB
BCONV = 31
EPS = 1e-6
HALO = 16
HALO_B = 32
RC = 32
NPART = 2
VMEM_LIMIT = 52 * 1024 * 1024
INV_SQRT2 = 1.0 / math.sqrt(2.0)
INV_SQRT_2PI = 1.0 / math.sqrt(2.0 * math.pi)

ADAM_LR = 0.001
ADAM_B1 = 0.9
ADAM_B2 = 0.999
ADAM_EPS = 1e-08
ADAM_WD = 0.01
ADAM_STEP = 10

MESH = pl.DeviceIdType.MESH
ANY = pl.BlockSpec(memory_space=pl.ANY)
NT_DIMS = (((1,), (1,)), ((), ()))
TN_DIMS = (((0,), (0,)), ((), ()))


def _params(*sem):
    return pltpu.CompilerParams(dimension_semantics=sem, vmem_limit_bytes=VMEM_LIMIT)


def _tile(s, want):
    return min(want, s)


def _sigmoid(x):
    return jax.nn.sigmoid(x)


def _dsilu(x, sg):
    return sg * (1.0 + x * (1.0 - sg))


def _gelu(x):
    return 0.5 * x * (1.0 + lax.erf(x * INV_SQRT2))


def _dgelu(x):
    return 0.5 * (1.0 + lax.erf(x * INV_SQRT2)) + x * jnp.exp(-0.5 * x * x) * INV_SQRT_2PI


def _ln_fwd(x, g, b):
    mu = jnp.mean(x, axis=-1, keepdims=True)
    xc = x - mu
    var = jnp.mean(xc * xc, axis=-1, keepdims=True)
    rstd = lax.rsqrt(var + EPS)
    xhat = xc * rstd
    return xhat * g + b, xhat, rstd


def _ln_bwd(dy, xhat, rstd, g):
    dxh = dy * g
    m1 = jnp.mean(dxh, axis=-1, keepdims=True)
    m2 = jnp.mean(dxh * xhat, axis=-1, keepdims=True)
    return rstd * (dxh - m1 - xhat * m2)


def _rms_bwd_math(dh, x, g):
    r = lax.rsqrt(jnp.mean(x * x, axis=-1, keepdims=True) + EPS)
    xhat = x * r
    dg = jnp.sum(dh * xhat, axis=0, keepdims=True)
    u = dh * g
    dx = r * (u - xhat * jnp.mean(u * xhat, axis=-1, keepdims=True))
    return dx, dg


def _conv3(xe, cw, halo):
    x0 = xe[halo:]
    x1 = pltpu.roll(xe, 1, 0)[halo:]
    x2 = pltpu.roll(xe, 2, 0)[halo:]
    return cw[2] * x0 + cw[1] * x1 + cw[0] * x2, (x0, x1, x2)


def _conv3_bwd_in(dce, cw, ts):
    n = dce.shape[0]
    d1 = pltpu.roll(dce, n - 1, 0)[:ts]
    d2 = pltpu.roll(dce, n - 2, 0)[:ts]
    return cw[2] * dce[:ts] + cw[1] * d1 + cw[0] * d2


def _conv3_bwd_w(dc, taps):
    x0, x1, x2 = taps
    return [jnp.sum(dc * x2, axis=0, keepdims=True), jnp.sum(dc * x1, axis=0, keepdims=True),
            jnp.sum(dc * x0, axis=0, keepdims=True)]


def _rms_fwd(x, g, name, after=None):
    s = x.shape[0]
    ts = _tile(s, 512)

    def body(x_ref, g_ref, *rest):
        h_ref = rest[-1]
        xv = x_ref[...]
        r = lax.rsqrt(jnp.mean(xv * xv, axis=-1, keepdims=True) + EPS)
        h_ref[...] = (xv * r * g_ref[...]).astype(BF16)

    extra = [] if after is None else [after]
    return pl.pallas_call(
        body, grid=(s // ts,), name=name,
        in_specs=[pl.BlockSpec((ts, D), lambda i: (i, 0)), pl.BlockSpec((1, D), lambda i: (0, 0))]
        + [ANY] * len(extra),
        out_specs=pl.BlockSpec((ts, D), lambda i: (i, 0)),
        out_shape=jax.ShapeDtypeStruct((s, D), BF16),
        compiler_params=_params("parallel"),
    )(x, g, *extra)


MXU_COLS = 256


def _pair(bn):
    return 1 if bn % MXU_COLS == 0 else 2


def _cols(w_ref, b, pair):
    return w_ref[b] if pair == 1 else jnp.concatenate([w_ref[b + q] for q in range(pair)], axis=1)


def _mm_in(h, wblk, name):
    s = h.shape[0]
    nb, _, bn = wblk.shape
    pair = _pair(bn)
    ts = _tile(s, 1024)

    def body(h_ref, w_ref, o_ref):
        hv = h_ref[...]
        for b in range(0, nb, pair):
            o_ref[:, b * bn:(b + pair) * bn] = jnp.dot(hv, _cols(w_ref, b, pair),
                                                       preferred_element_type=F32).astype(BF16)

    return pl.pallas_call(
        body, grid=(s // ts,), name=name,
        in_specs=[pl.BlockSpec((ts, D), lambda i: (i, 0)), pl.BlockSpec((nb, D, bn), lambda i: (0, 0, 0))],
        out_specs=pl.BlockSpec((ts, nb * bn), lambda i: (i, 0)),
        out_shape=jax.ShapeDtypeStruct((s, nb * bn), BF16),
        compiler_params=_params("parallel"),
    )(h, wblk)


def _rms_math(xv, g):
    r = lax.rsqrt(jnp.mean(xv * xv, axis=-1, keepdims=True) + EPS)
    return (xv * r * g).astype(BF16)


def _mm_out(y, w, xres, gnext, name):
    s = y.shape[0]
    ts = _tile(s, 1024)

    def body(y_ref, w_ref, x_ref, g_ref, o_ref, h_ref):
        xn = x_ref[...] + jnp.dot(y_ref[...], w_ref[...], preferred_element_type=F32)
        o_ref[...] = xn
        h_ref[...] = _rms_math(xn, g_ref[...])

    return pl.pallas_call(
        body, grid=(s // ts,), name=name,
        in_specs=[pl.BlockSpec((ts, D), lambda i: (i, 0)), pl.BlockSpec((D, D), lambda i: (0, 0)),
                  pl.BlockSpec((ts, D), lambda i: (i, 0)), pl.BlockSpec((1, D), lambda i: (0, 0))],
        out_specs=[pl.BlockSpec((ts, D), lambda i: (i, 0)), pl.BlockSpec((ts, D), lambda i: (i, 0))],
        out_shape=[jax.ShapeDtypeStruct((s, D), F32), jax.ShapeDtypeStruct((s, D), BF16)],
        compiler_params=_params("parallel"),
    )(y, w, xres, gnext)


CONV_ROWS = 32


def _rolled_copies(dst_ref, xe, back):
    n = xe.shape[0]
    dst_ref[0] = xe
    for r in range(1, 8):
        dst_ref[r] = pltpu.roll(xe, n - r if back else r, 0)


def _conv31(rolled_ref, cw_ref, ts, out_ref, bias):
    for o in range(0, ts, CONV_ROWS):
        acc = jnp.zeros((CONV_ROWS, DA), F32) + bias
        for sh in range(BCONV):
            q, r = divmod(sh, 8)
            lo = HALO_B - 8 * q + o
            acc = acc + cw_ref[BCONV - 1 - sh:BCONV - sh, :] * rolled_ref[r, lo:lo + CONV_ROWS, :]
        out_ref[o:o + CONV_ROWS, :] = acc


def _ab_fwd(z, lga, lba, wsm, bs_col, cwb, cbb, lgb, lbb, name):
    s = z.shape[0]
    ts = _tile(s, 256)
    hb = ts // HALO_B

    def body(z_ref, zh_ref, lga_ref, lba_ref, ws_ref, bs_ref, cw_ref, cb_ref, lgb_ref, lbb_ref,
             y_ref, yb2_ref, rolled):
        i = pl.program_id(0)
        z_t = z_ref[...].astype(F32)
        gu = _gelu(z_t[:, 0:DA])
        gv = _gelu(z_t[:, DA:2 * DA])
        vn, _, _ = _ln_fwd(gv, lga_ref[...], lba_ref[...])
        vnb = vn.astype(BF16)
        for c in range(ts // CHUNK):
            for h in range(HEADS):
                rs = slice(c * CHUNK, (c + 1) * CHUNK)
                cs = slice(h * CHUNK, (h + 1) * CHUNK)
                mixed = jnp.dot(ws_ref[h], vnb[rs, cs], preferred_element_type=F32) + bs_ref[h]
                y_ref[rs, cs] = (gu[rs, cs] * mixed).astype(BF16)
        zh = jnp.where(i > 0, zh_ref[...], jnp.zeros_like(zh_ref[...])).astype(F32)
        xb = jnp.concatenate([zh[:, 0:DA], z_t[:, 2 * DA:3 * DA]], axis=0)
        gb = jnp.concatenate([zh[:, DA:2 * DA], z_t[:, 3 * DA:4 * DA]], axis=0)
        _rolled_copies(rolled, xb * _sigmoid(gb), False)
        _conv31(rolled, cw_ref, ts, yb2_ref, cb_ref[...])
        nb_, _, _ = _ln_fwd(yb2_ref[...], lgb_ref[...], lbb_ref[...])
        y_ref[:, DA:2 * DA] = (nb_ * _sigmoid(nb_)).astype(BF16)

    row = lambda i: (0, 0)
    return pl.pallas_call(
        body, grid=(s // ts,), name=name,
        in_specs=[pl.BlockSpec((ts, 4 * DA), lambda i: (i, 0)),
                  pl.BlockSpec((HALO_B, 2 * DA), lambda i: (jnp.maximum(i * hb - 1, 0), 1)),
                  pl.BlockSpec((1, DA), row), pl.BlockSpec((1, DA), row),
                  pl.BlockSpec((HEADS, CHUNK, CHUNK), lambda i: (0, 0, 0)),
                  pl.BlockSpec((HEADS, CHUNK, 1), lambda i: (0, 0, 0)),
                  pl.BlockSpec((BCONV, DA), row), pl.BlockSpec((1, DA), row),
                  pl.BlockSpec((1, DA), row), pl.BlockSpec((1, DA), row)],
        out_specs=[pl.BlockSpec((ts, 2 * DA), lambda i: (i, 0)), pl.BlockSpec((ts, DA), lambda i: (i, 0))],
        out_shape=[jax.ShapeDtypeStruct((s, 2 * DA), BF16), jax.ShapeDtypeStruct((s, DA), F32)],
        scratch_shapes=[pltpu.VMEM((8, ts + HALO_B, DA), F32)],
        compiler_params=_params("parallel"),
    )(z, z, lga, lba, wsm, bs_col, cwb, cbb, lgb, lbb)


def _c_fwd(zc, cw, name):
    s = zc.shape[0]
    ts = _tile(s, 512)
    hb = ts // HALO

    def body(z_ref, ch_ref, xh_ref, cw_ref, r_ref):
        i = pl.program_id(0)
        z_t = z_ref[...].astype(F32)
        ph = jnp.where(i > 0, ch_ref[...].astype(F32) * xh_ref[...].astype(F32), 0.0)
        pe = jnp.concatenate([ph, z_t[:, D:2 * D] * z_t[:, 2 * D:3 * D]], axis=0)
        q, _ = _conv3(pe, [cw_ref[k:k + 1, :] for k in range(3)], HALO)
        r_ref[...] = (z_t[:, 0:D] * q).astype(BF16)

    halo = lambda col: pl.BlockSpec((HALO, D), lambda i: (jnp.maximum(i * hb - 1, 0), col))
    return pl.pallas_call(
        body, grid=(s // ts,), name=name,
        in_specs=[pl.BlockSpec((ts, 3 * D), lambda i: (i, 0)), halo(1), halo(2),
                  pl.BlockSpec((3, D), lambda i: (0, 0))],
        out_specs=pl.BlockSpec((ts, D), lambda i: (i, 0)),
        out_shape=jax.ShapeDtypeStruct((s, D), BF16),
        compiler_params=_params("parallel"),
    )(zc, zc, zc, cw)


def _ffn_fwd(h, xres, wup, fcw, wdn, gnext, name):
    s = h.shape[0]
    ts = _tile(s, 512)
    hb = ts // HALO
    n = (s // ts) * NG

    def cur(q):
        c = jnp.minimum(q, n - 1)
        return c // NG, c % NG

    def prev(q):
        p = jnp.maximum(q - 1, 0)
        return p // NG, p % NG

    def body(h_ref, hh_ref, w_ref, cw_ref, wd_ref, x_ref, *rest):
        if gnext is not None:
            gn_ref, up_ref, upc_ref, xo_ref, hn_ref, up_s, a_s = rest
        else:
            up_ref, upc_ref, xo_ref, up_s, a_s = rest
        q = pl.program_id(0)
        i, _ = cur(q)
        _, mp = prev(q)

        @pl.when(q == 0)
        def _():
            a_s[1] = jnp.zeros((ts, FB), BF16)

        halo = jnp.where(i > 0, hh_ref[...], jnp.zeros_like(hh_ref[...]))
        hx = jnp.concatenate([halo, h_ref[...]], axis=0)
        acts = []
        for gv in range(2):
            up_s[gv] = jnp.dot(hx, w_ref[gv], preferred_element_type=F32)
            if gv == 1:
                base = jnp.where(mp == 0, x_ref[...], xo_ref[...])
                xo_ref[...] = base + jnp.dot(a_s[(q + 1) % 2], wd_ref[...], preferred_element_type=F32)
            x0 = up_s[gv, HALO:HALO + ts, :]
            up_ref[gv] = x0.astype(BF16)
            upc = (cw_ref[gv, 2:3, :] * x0 + cw_ref[gv, 1:2, :] * up_s[gv, HALO - 1:HALO - 1 + ts, :]
                   + cw_ref[gv, 0:1, :] * up_s[gv, HALO - 2:HALO - 2 + ts, :])
            upc_ref[gv] = upc.astype(BF16)
            acts.append(upc)
        a_s[q % 2] = (acts[0] * _sigmoid(acts[0]) * acts[1]).astype(BF16)

        if gnext is not None:
            @pl.when(jnp.logical_and(q > 0, mp == NG - 1))
            def _():
                hn_ref[...] = _rms_math(xo_ref[...], gn_ref[...])

    tile = pl.BlockSpec((ts, D), lambda q: (cur(q)[0], 0))
    lag = pl.BlockSpec((ts, D), lambda q: (prev(q)[0], 0))
    nxt = gnext is not None
    return pl.pallas_call(
        body, grid=(n + 1,), name=name,
        in_specs=[tile,
                  pl.BlockSpec((HALO, D), lambda q: (jnp.maximum(cur(q)[0] * hb - 1, 0), 0)),
                  pl.BlockSpec((2, None, D, FB), lambda q: (0, cur(q)[1], 0, 0)),
                  pl.BlockSpec((2, None, 3, FB), lambda q: (0, cur(q)[1], 0, 0)),
                  pl.BlockSpec((FB, D), lambda q: (prev(q)[1], 0)),
                  lag] + ([pl.BlockSpec((1, D), lambda q: (0, 0))] if nxt else []),
        out_specs=[pl.BlockSpec((None, 2, ts, FB), lambda q: (cur(q)[1], 0, cur(q)[0], 0)),
                   pl.BlockSpec((None, 2, ts, FB), lambda q: (cur(q)[1], 0, cur(q)[0], 0)),
                   lag] + ([lag] if nxt else []),
        out_shape=[jax.ShapeDtypeStruct((NG, 2, s, FB), BF16), jax.ShapeDtypeStruct((NG, 2, s, FB), BF16),
                   jax.ShapeDtypeStruct((s, D), F32)] + ([jax.ShapeDtypeStruct((s, D), BF16)] if nxt else []),
        scratch_shapes=[pltpu.VMEM((2, ts + HALO, FB), F32), pltpu.VMEM((2, ts, FB), BF16)],
        compiler_params=_params("arbitrary"),
    )(h, h, wup, fcw, wdn, xres, *([gnext] if nxt else []))


def _final(x, tgt, g, name):
    s = x.shape[0]
    ts = _tile(s, 512)

    def body(x_ref, t_ref, g_ref, dx_ref, dxb_ref, dg_ref, loss_ref):
        i = pl.program_id(0)
        xv = x_ref[...]
        gv = g_ref[...]
        r = lax.rsqrt(jnp.mean(xv * xv, axis=-1, keepdims=True) + EPS)
        xhat = xv * r
        e = xhat * gv - t_ref[...]
        part = 0.5 * jnp.sum(jnp.mean(e * e, axis=-1, keepdims=True), axis=0, keepdims=True)
        dy = e * (1.0 / D)
        dgp = jnp.sum(dy * xhat, axis=0, keepdims=True)
        u = dy * gv
        dx = r * (u - xhat * jnp.mean(u * xhat, axis=-1, keepdims=True))
        dx_ref[...] = dx
        dxb_ref[...] = dx.astype(BF16)

        @pl.when(i == 0)
        def _():
            dg_ref[...] = dgp
            loss_ref[...] = jnp.broadcast_to(part, (1, 128))

        @pl.when(i > 0)
        def _():
            dg_ref[...] += dgp
            loss_ref[...] += jnp.broadcast_to(part, (1, 128))

    return pl.pallas_call(
        body, grid=(s // ts,), name=name,
        in_specs=[pl.BlockSpec((ts, D), lambda i: (i, 0)), pl.BlockSpec((ts, D), lambda i: (i, 0)),
                  pl.BlockSpec((1, D), lambda i: (0, 0))],
        out_specs=[pl.BlockSpec((ts, D), lambda i: (i, 0)), pl.BlockSpec((ts, D), lambda i: (i, 0)),
                   pl.BlockSpec((1, D), lambda i: (0, 0)), pl.BlockSpec((1, 128), lambda i: (0, 0))],
        out_shape=[jax.ShapeDtypeStruct((s, D), F32), jax.ShapeDtypeStruct((s, D), BF16),
                   jax.ShapeDtypeStruct((1, D), F32), jax.ShapeDtypeStruct((1, 128), F32)],
        compiler_params=_params("arbitrary"),
    )(x, tgt, g)


def _ffn_bwd(df, up, upc, wup, fcw, wdn, xin, g, name):
    s = df.shape[0]
    ts = _tile(s, 512)
    nt = s // ts

    def body(df_ref, up_ref, upc_ref, w_ref, cw_ref, wd_ref, x_ref, g_ref,
             a_ref, dup_ref, dx_ref, dxb_ref, dg_ref, dcw_ref, carry, acc, tacc, dcs_ref):
        i = pl.program_id(0)
        m = pl.program_id(1)
        first = i == 0
        @pl.when(first)
        def _():
            carry[m] = jnp.zeros((2, 8, FB), F32)
            dcw_ref[m] = jnp.zeros((2, 3, FB), F32)

        @pl.when(m == 0)
        def _():
            acc[...] = jnp.zeros((ts, D), F32)

        cws = [[cw_ref[gv, k:k + 1, :] for k in range(3)] for gv in range(2)]
        part = ts // NPART
        das = [lax.dot_general(df_ref[p * part:(p + 1) * part, :].astype(BF16), wd_ref[...], NT_DIMS,
                               preferred_element_type=F32) for p in range(NPART)]

        tacc[...] = jnp.zeros((2, 3, 8, FB), F32)
        dcs_ref[:, ts:ts + 8, :] = carry[m]
        for r in reversed(range(ts // RC)):
            rs = slice(r * RC, (r + 1) * RC)
            gate = upc_ref[0, rs, :].astype(F32)
            val = upc_ref[1, rs, :].astype(F32)
            sg = _sigmoid(gate)
            sl = gate * sg
            a_ref[rs, :] = (sl * val).astype(BF16)
            da_c = das[(r * RC) // part][(r * RC) % part:(r * RC) % part + RC]
            dcs = [da_c * val * _dsilu(gate, sg), da_c * sl]
            for gv in range(2):
                dc = dcs[gv]
                dcs_ref[gv, rs, :] = dc
                d1 = dcs_ref[gv, r * RC + 1:(r + 1) * RC + 1, :]
                d2 = dcs_ref[gv, r * RC + 2:(r + 1) * RC + 2, :]
                du = cws[gv][2] * dc + cws[gv][1] * d1 + cws[gv][0] * d2
                dup_ref[gv, rs, :] = du.astype(BF16)
                x0 = up_ref[gv, rs, :].astype(F32)
                for k, dk in enumerate((d2, d1, dc)):
                    p = x0 * dk
                    tacc[gv, k] += sum(p[j:j + 8] for j in range(0, RC, 8))
            if (r * RC) % part == 0:
                ps = slice(r * RC, r * RC + part)
                acc[ps, :] += (
                    lax.dot_general(dup_ref[0, ps, :], w_ref[0], NT_DIMS, preferred_element_type=F32)
                    + lax.dot_general(dup_ref[1, ps, :], w_ref[1], NT_DIMS, preferred_element_type=F32))
        for gv in range(2):
            carry[m, gv] = dcs_ref[gv, 0:8, :]
            for k in range(3):
                dcw_ref[m, gv, k:k + 1, :] += jnp.sum(tacc[gv, k], axis=0, keepdims=True)

        @pl.when(m == NG - 1)
        def _():
            dx, dgp = _rms_bwd_math(acc[...], x_ref[...], g_ref[...])
            dx = df_ref[...] + dx
            dx_ref[...] = dx
            dxb_ref[...] = dx.astype(BF16)

            @pl.when(first)
            def _():
                dg_ref[...] = dgp

            @pl.when(jnp.logical_not(first))
            def _():
                dg_ref[...] += dgp

    rev = lambda i: nt - 1 - i
    return pl.pallas_call(
        body, grid=(nt, NG), name=name,
        in_specs=[pl.BlockSpec((ts, D), lambda i, m: (rev(i), 0)),
                  pl.BlockSpec((None, 2, ts, FB), lambda i, m: (m, 0, rev(i), 0)),
                  pl.BlockSpec((None, 2, ts, FB), lambda i, m: (m, 0, rev(i), 0)),
                  pl.BlockSpec((2, None, D, FB), lambda i, m: (0, m, 0, 0)),
                  pl.BlockSpec((2, None, 3, FB), lambda i, m: (0, m, 0, 0)),
                  pl.BlockSpec((FB, D), lambda i, m: (m, 0)),
                  pl.BlockSpec((ts, D), lambda i, m: (rev(i), 0)),
                  pl.BlockSpec((1, D), lambda i, m: (0, 0))],
        out_specs=[pl.BlockSpec((None, ts, FB), lambda i, m: (m, rev(i), 0)),
                   pl.BlockSpec((None, 2, ts, FB), lambda i, m: (m, 0, rev(i), 0)),
                   pl.BlockSpec((ts, D), lambda i, m: (rev(i), 0)),
                   pl.BlockSpec((ts, D), lambda i, m: (rev(i), 0)),
                   pl.BlockSpec((1, D), lambda i, m: (0, 0)),
                   pl.BlockSpec((NG, 2, 3, FB), lambda i, m: (0, 0, 0, 0))],
        out_shape=[jax.ShapeDtypeStruct((NG, s, FB), BF16), jax.ShapeDtypeStruct((NG, 2, s, FB), BF16),
                   jax.ShapeDtypeStruct((s, D), F32), jax.ShapeDtypeStruct((s, D), BF16),
                   jax.ShapeDtypeStruct((1, D), F32),
                   jax.ShapeDtypeStruct((NG, 2, 3, FB), F32)],
        scratch_shapes=[pltpu.VMEM((NG, 2, 8, FB), F32), pltpu.VMEM((ts, D), F32),
                        pltpu.VMEM((2, 3, 8, FB), F32), pltpu.VMEM((2, ts + 8, FB), F32)],
        compiler_params=_params("arbitrary", "arbitrary"),
    )(df, up, upc, wup, fcw, wdn, xin, g)


def _mm_nt(dy, w, name):
    s = dy.shape[0]
    ts = _tile(s, 1024)

    def body(dy_ref, w_ref, o_ref):
        o_ref[...] = lax.dot_general(dy_ref[...], w_ref[...], NT_DIMS,
                                     preferred_element_type=F32).astype(BF16)

    return pl.pallas_call(
        body, grid=(s // ts,), name=name,
        in_specs=[pl.BlockSpec((ts, D), lambda i: (i, 0)), pl.BlockSpec((D, D), lambda i: (0, 0))],
        out_specs=pl.BlockSpec((ts, D), lambda i: (i, 0)),
        out_shape=jax.ShapeDtypeStruct((s, D), BF16),
        compiler_params=_params("parallel"),
    )(dy, w)


def _mm_nt_rms(dy, wblk, x, g, dres, bf16_copy, name):
    s = dy.shape[0]
    nb, _, bn = wblk.shape
    pair = _pair(bn)
    ts = _tile(s, 512)

    def body(dy_ref, w_ref, x_ref, g_ref, dr_ref, dx_ref, *rest):
        dg_ref = rest[-1]
        i = pl.program_id(0)
        acc = jnp.zeros((ts, D), F32)
        for b in range(0, nb, pair):
            acc = acc + lax.dot_general(dy_ref[:, b * bn:(b + pair) * bn], _cols(w_ref, b, pair), NT_DIMS,
                                        preferred_element_type=F32)
        dx, dgp = _rms_bwd_math(acc, x_ref[...], g_ref[...])
        dx = dr_ref[...] + dx
        dx_ref[...] = dx
        if bf16_copy:
            rest[0][...] = dx.astype(BF16)

        @pl.when(i == 0)
        def _():
            dg_ref[...] = dgp

        @pl.when(i > 0)
        def _():
            dg_ref[...] += dgp

    tile = pl.BlockSpec((ts, D), lambda i: (i, 0))
    return pl.pallas_call(
        body, grid=(s // ts,), name=name,
        in_specs=[pl.BlockSpec((ts, nb * bn), lambda i: (i, 0)), pl.BlockSpec((nb, D, bn), lambda i: (0, 0, 0)),
                  tile, pl.BlockSpec((1, D), lambda i: (0, 0)), tile],
        out_specs=[tile] + ([tile] if bf16_copy else []) + [pl.BlockSpec((1, D), lambda i: (0, 0))],
        out_shape=[jax.ShapeDtypeStruct((s, D), F32)] + ([jax.ShapeDtypeStruct((s, D), BF16)] if bf16_copy else [])
        + [jax.ShapeDtypeStruct((1, D), F32)],
        compiler_params=_params("arbitrary"),
    )(dy, wblk, x, g, dres)


def _c_bwd(dr, zc, cw, name):
    s = dr.shape[0]
    ts = _tile(s, 512)
    nt = s // ts
    hb = ts // HALO

    def body(dr_ref, drf_ref, z_ref, ch_ref, xh_ref, bf_ref, cw_ref, dz_ref, dcw_ref):
        i = pl.program_id(0)
        cwv = [cw_ref[k:k + 1, :] for k in range(3)]
        z_t = z_ref[...].astype(F32)
        bg, cg, xv = z_t[:, 0:D], z_t[:, D:2 * D], z_t[:, 2 * D:3 * D]
        ph = jnp.where(i > 0, ch_ref[...].astype(F32) * xh_ref[...].astype(F32), 0.0)
        pe = jnp.concatenate([ph, cg * xv], axis=0)
        q, taps = _conv3(pe, cwv, HALO)
        drv = dr_ref[...].astype(F32)
        dq = drv * bg
        dqf = jnp.where(i < nt - 1, drf_ref[...].astype(F32) * bf_ref[...].astype(F32), 0.0)
        dp = _conv3_bwd_in(jnp.concatenate([dq, dqf], axis=0), cwv, ts)
        dz_ref[:, 0:D] = (drv * q).astype(BF16)
        dz_ref[:, D:2 * D] = (dp * xv).astype(BF16)
        dz_ref[:, 2 * D:3 * D] = (dp * cg).astype(BF16)
        rows = _conv3_bwd_w(dq, taps)

        @pl.when(i == 0)
        def _():
            for k in range(3):
                dcw_ref[k:k + 1, :] = rows[k]

        @pl.when(i > 0)
        def _():
            for k in range(3):
                dcw_ref[k:k + 1, :] += rows[k]

    past = lambda col: pl.BlockSpec((HALO, D), lambda i: (jnp.maximum(i * hb - 1, 0), col))
    nxt = lambda i: jnp.minimum((i + 1) * hb, s // HALO - 1)
    return pl.pallas_call(
        body, grid=(nt,), name=name,
        in_specs=[pl.BlockSpec((ts, D), lambda i: (i, 0)),
                  pl.BlockSpec((HALO, D), lambda i: (nxt(i), 0)),
                  pl.BlockSpec((ts, 3 * D), lambda i: (i, 0)), past(1), past(2),
                  pl.BlockSpec((HALO, D), lambda i: (nxt(i), 0)),
                  pl.BlockSpec((3, D), lambda i: (0, 0))],
        out_specs=[pl.BlockSpec((ts, 3 * D), lambda i: (i, 0)), pl.BlockSpec((3, D), lambda i: (0, 0))],
        out_shape=[jax.ShapeDtypeStruct((s, 3 * D), BF16), jax.ShapeDtypeStruct((3, D), F32)],
        compiler_params=_params("arbitrary"),
    )(dr, dr, zc, zc, zc, zc, cw)


G512_ROWS = 40


def _ab_bwd(dy, z, yb2, lga, lba, wsm, bs_col, cwb, lgb, lbb, name):
    s = z.shape[0]
    ts = _tile(s, 256)
    nt = s // ts
    hb = ts // HALO_B
    nch = ts // CHUNK

    def body(z_ref, zh_ref, dy_ref, dyf_ref, yb2_ref, yb2f_ref, lga_ref, lba_ref, ws_ref, bs_ref,
             cw_ref, lgb_ref, lbb_ref, dz_ref, g512_ref, dws_ref, dbs_ref, dvn_ref, fwd_rolled, bwd_rolled, du_s):
        i = pl.program_id(0)
        last = i == nt - 1

        @pl.when(i == 0)
        def _():
            g512_ref[...] = jnp.zeros((G512_ROWS, DA), F32)
            dws_ref[...] = jnp.zeros((HEADS, CHUNK, CHUNK), F32)
            dbs_ref[...] = jnp.zeros((HEADS, CHUNK, 1), F32)

        def add_row(k, v):
            g512_ref[k:k + 1, :] += v

        z_t = z_ref[...].astype(F32)
        dy_t = dy_ref[...].astype(F32)
        ua, va = z_t[:, 0:DA], z_t[:, DA:2 * DA]
        gu = _gelu(ua)
        gv = _gelu(va)
        lga_v = lga_ref[...]
        vn, xhat_a, rstd_a = _ln_fwd(gv, lga_v, lba_ref[...])
        vnb = vn.astype(BF16)
        causal = (lax.broadcasted_iota(jnp.int32, (CHUNK, CHUNK), 0)
                  >= lax.broadcasted_iota(jnp.int32, (CHUNK, CHUNK), 1)).astype(F32)
        for c in range(nch):
            for h in range(HEADS):
                rs = slice(c * CHUNK, (c + 1) * CHUNK)
                cs = slice(h * CHUNK, (h + 1) * CHUNK)
                vblk = vnb[rs, cs]
                mixed = jnp.dot(ws_ref[h], vblk, preferred_element_type=F32) + bs_ref[h]
                dyb_ = dy_t[rs, cs]
                dmix = dyb_ * gu[rs, cs]
                dmb = dmix.astype(BF16)
                dz_ref[rs, cs] = (dyb_ * mixed * _dgelu(ua[rs, cs])).astype(BF16)
                dvn_ref[rs, cs] = lax.dot_general(ws_ref[h], dmb, TN_DIMS, preferred_element_type=F32)
                dws_ref[h] += causal * lax.dot_general(dmb, vblk, NT_DIMS, preferred_element_type=F32)
                dbs_ref[h] += jnp.sum(dmix, axis=1, keepdims=True)
        dvn = dvn_ref[...]
        add_row(0, jnp.sum(dvn * xhat_a, axis=0, keepdims=True))
        add_row(1, jnp.sum(dvn, axis=0, keepdims=True))
        dgv = _ln_bwd(dvn, xhat_a, rstd_a, lga_v)
        dz_ref[:, DA:2 * DA] = (dgv * _dgelu(va)).astype(BF16)
        lgb_v = lgb_ref[...]
        dyb_e = jnp.concatenate(
            [dy_t[:, DA:2 * DA], jnp.where(last, 0.0, dyf_ref[...].astype(F32))], axis=0)
        yb2_e = jnp.concatenate([yb2_ref[...], jnp.where(last, 0.0, yb2f_ref[...])], axis=0)
        n_e, xhat_b, rstd_b = _ln_fwd(yb2_e, lgb_v, lbb_ref[...])
        sgn = _sigmoid(n_e)
        dn = dyb_e * _dsilu(n_e, sgn)
        dy2 = _ln_bwd(dn, xhat_b, rstd_b, lgb_v)
        add_row(2, jnp.sum(dy2[:ts], axis=0, keepdims=True))
        add_row(3, jnp.sum(dn[:ts] * xhat_b[:ts], axis=0, keepdims=True))
        add_row(4, jnp.sum(dn[:ts], axis=0, keepdims=True))
        zh = jnp.where(i > 0, zh_ref[...], jnp.zeros_like(zh_ref[...])).astype(F32)
        xb_t, gb_t = z_t[:, 2 * DA:3 * DA], z_t[:, 3 * DA:4 * DA]
        sgb = _sigmoid(gb_t)
        _rolled_copies(fwd_rolled, jnp.concatenate(
            [zh[:, 0:DA] * _sigmoid(zh[:, DA:2 * DA]), xb_t * sgb], axis=0), False)
        _rolled_copies(bwd_rolled, dy2, True)
        for o in range(0, ts, CONV_ROWS):
            acc = jnp.zeros((CONV_ROWS, DA), F32)
            for sh in range(BCONV):
                q, r = divmod(sh, 8)
                acc = acc + cw_ref[BCONV - 1 - sh:BCONV - sh, :] * bwd_rolled[r, 8 * q + o:8 * q + o + CONV_ROWS, :]
            du_s[o:o + CONV_ROWS, :] = acc
        for sh in range(BCONV):
            q, r = divmod(sh, 8)
            acc = jnp.zeros((CONV_ROWS, DA), F32)
            for o in range(0, ts, CONV_ROWS):
                lo = HALO_B - 8 * q + o
                acc = acc + bwd_rolled[0, o:o + CONV_ROWS, :] * fwd_rolled[r, lo:lo + CONV_ROWS, :]
            add_row(8 + BCONV - 1 - sh, jnp.sum(acc, axis=0, keepdims=True))
        du = du_s[...]
        dz_ref[:, 2 * DA:3 * DA] = (du * sgb).astype(BF16)
        dz_ref[:, 3 * DA:4 * DA] = (du * xb_t * sgb * (1.0 - sgb)).astype(BF16)

    row = lambda i: (0, 0)
    nxt = lambda i: jnp.minimum((i + 1) * hb, s // HALO_B - 1)
    return pl.pallas_call(
        body, grid=(nt,), name=name,
        in_specs=[pl.BlockSpec((ts, 4 * DA), lambda i: (i, 0)),
                  pl.BlockSpec((HALO_B, 2 * DA), lambda i: (jnp.maximum(i * hb - 1, 0), 1)),
                  pl.BlockSpec((ts, 2 * DA), lambda i: (i, 0)),
                  pl.BlockSpec((HALO_B, DA), lambda i: (nxt(i), 1)),
                  pl.BlockSpec((ts, DA), lambda i: (i, 0)),
                  pl.BlockSpec((HALO_B, DA), lambda i: (nxt(i), 0)),
                  pl.BlockSpec((1, DA), row), pl.BlockSpec((1, DA), row),
                  pl.BlockSpec((HEADS, CHUNK, CHUNK), lambda i: (0, 0, 0)),
                  pl.BlockSpec((HEADS, CHUNK, 1), lambda i: (0, 0, 0)),
                  pl.BlockSpec((BCONV, DA), row), pl.BlockSpec((1, DA), row), pl.BlockSpec((1, DA), row)],
        out_specs=[pl.BlockSpec((ts, 4 * DA), lambda i: (i, 0)),
                   pl.BlockSpec((G512_ROWS, DA), row),
                   pl.BlockSpec((HEADS, CHUNK, CHUNK), lambda i: (0, 0, 0)),
                   pl.BlockSpec((HEADS, CHUNK, 1), lambda i: (0, 0, 0))],
        out_shape=[jax.ShapeDtypeStruct((s, 4 * DA), BF16), jax.ShapeDtypeStruct((G512_ROWS, DA), F32),
                   jax.ShapeDtypeStruct((HEADS, CHUNK, CHUNK), F32),
                   jax.ShapeDtypeStruct((HEADS, CHUNK, 1), F32)],
        scratch_shapes=[pltpu.VMEM((ts, DA), F32), pltpu.VMEM((8, ts + HALO_B, DA), F32),
                        pltpu.VMEM((8, ts + HALO_B, DA), F32), pltpu.VMEM((ts, DA), F32)],
        compiler_params=_params("arbitrary"),
    )(z, z, dy, dy, yb2, yb2, lga, lba, wsm, bs_col, cwb, lgb, lbb)


def _dw_cols(a, dy, nb, bn, name):
    s = a.shape[0]
    tm = _tile(s, 2048)
    nt = s // tm
    cpb = 4

    def body(a_ref, dy_ref, o_ref, acc):
        t = pl.program_id(1)
        p = lax.dot_general(a_ref[...], dy_ref[...], TN_DIMS, preferred_element_type=F32)

        @pl.when(t == 0)
        def _():
            for q in range(cpb):
                acc[q] = p[:, q * bn:(q + 1) * bn]

        @pl.when(t > 0)
        def _():
            for q in range(cpb):
                acc[q] += p[:, q * bn:(q + 1) * bn]

        @pl.when(t == nt - 1)
        def _():
            o_ref[...] = acc[...].astype(BF16)

    return pl.pallas_call(
        body, grid=(nb // cpb, nt), name=name,
        in_specs=[pl.BlockSpec((tm, D), lambda j, t: (t, 0)), pl.BlockSpec((tm, cpb * bn), lambda j, t: (t, j))],
        out_specs=pl.BlockSpec((cpb, D, bn), lambda j, t: (j, 0, 0)),
        out_shape=jax.ShapeDtypeStruct((nb, D, bn), BF16),
        scratch_shapes=[pltpu.VMEM((cpb, D, bn), F32)],
        compiler_params=_params("arbitrary", "arbitrary"),
    )(a, dy)


def _dw_rows(a, dy, name):
    s = a.shape[0]
    tm = _tile(s, 4096)
    nt = s // tm
    rb = 512

    def body(a_ref, dy_ref, o_ref, acc):
        t = pl.program_id(1)
        p = lax.dot_general(a_ref[...], dy_ref[...], TN_DIMS, preferred_element_type=F32)

        @pl.when(t == 0)
        def _():
            acc[...] = p

        @pl.when(t > 0)
        def _():
            acc[...] += p

        @pl.when(t == nt - 1)
        def _():
            o_ref[...] = acc[...].astype(BF16)

    return pl.pallas_call(
        body, grid=(D // rb, nt), name=name,
        in_specs=[pl.BlockSpec((tm, rb), lambda j, t: (t, j)), pl.BlockSpec((tm, D), lambda j, t: (t, 0))],
        out_specs=pl.BlockSpec((rb, D), lambda j, t: (j, 0)),
        out_shape=jax.ShapeDtypeStruct((D, D), BF16),
        scratch_shapes=[pltpu.VMEM((rb, D), F32)],
        compiler_params=_params("arbitrary", "arbitrary"),
    )(a, dy)


def _dw_up(h, dup, name):
    s = h.shape[0]
    tm = _tile(s, 4096)
    nt = s // tm

    def body(h_ref, d_ref, o_ref, acc):
        t = pl.program_id(1)
        p = lax.dot_general(d_ref[...], h_ref[...], TN_DIMS, preferred_element_type=F32)

        @pl.when(t == 0)
        def _():
            acc[...] = p

        @pl.when(t > 0)
        def _():
            acc[...] += p

        @pl.when(t == nt - 1)
        def _():
            o_ref[...] = acc[...].astype(BF16)

    return pl.pallas_call(
        body, grid=(NDEV, nt), name=name,
        in_specs=[pl.BlockSpec((tm, D), lambda b, t: (t, 0)),
                  pl.BlockSpec((None, None, tm, FB), lambda b, t: (b % NG, b // NG, t, 0))],
        out_specs=pl.BlockSpec((None, FB, D), lambda b, t: (b, 0, 0)),
        out_shape=jax.ShapeDtypeStruct((NDEV, FB, D), BF16),
        scratch_shapes=[pltpu.VMEM((FB, D), F32)],
        compiler_params=_params("arbitrary", "arbitrary"),
    )(h, dup)


def _dw_dn(a, df, name):
    s = df.shape[0]
    tm = _tile(s, 4096)
    nt = s // tm

    def body(a_ref, d_ref, o_ref, acc):
        t = pl.program_id(1)
        p = lax.dot_general(a_ref[...], d_ref[...], TN_DIMS, preferred_element_type=F32)

        @pl.when(t == 0)
        def _():
            acc[...] = p

        @pl.when(t > 0)
        def _():
            acc[...] += p

        @pl.when(t == nt - 1)
        def _():
            o_ref[...] = acc[...].astype(BF16)

    return pl.pallas_call(
        body, grid=(NG, nt), name=name,
        in_specs=[pl.BlockSpec((None, tm, FB), lambda m, t: (m, t, 0)), pl.BlockSpec((tm, D), lambda m, t: (t, 0))],
        out_specs=pl.BlockSpec((FB, D), lambda m, t: (m, 0)),
        out_shape=jax.ShapeDtypeStruct((DFF, D), BF16),
        scratch_shapes=[pltpu.VMEM((FB, D), F32)],
        compiler_params=_params("arbitrary", "arbitrary"),
    )(a, df)


def _place():
    x, y, c = lax.axis_index("x"), lax.axis_index("y"), lax.axis_index("c")
    chips = [(1 - x, y), (x, 1 - y), (1 - x, 1 - y)]
    return x, y, c, chips


def _zone(shard, dev):
    return lax.dynamic_update_slice(lax.empty((NDEV,) + shard.shape, shard.dtype), shard[None],
                                    (dev,) + (0,) * shard.ndim)


HBM_SPEC = pl.BlockSpec(memory_space=pltpu.HBM)
SEM_SPEC = pl.BlockSpec(memory_space=pltpu.SEMAPHORE)
DATAFLOW = pltpu.SideEffectType.DATAFLOW_SIDE_EFFECTING


def _hbm(a):
    return pltpu.with_memory_space_constraint(a, pltpu.HBM)


def _hbm_like(arrs):
    return [pltpu.HBM(a.shape, a.dtype) for a in arrs]


def _ag_start(srcs, lands, after, name):
    n = len(srcs)
    ns = 8 * n

    def body(*refs):
        src, land = refs[:n], refs[n:2 * n]
        sems = refs[2 * n + 1:2 * n + 1 + ns]
        token = refs[-1]
        x, y, c, chips = _place()
        peers = [(x, y, 1 - c)] + [(*chip, c) for chip in chips]
        for t in range(n):
            for k, to in enumerate(peers):
                pltpu.make_async_remote_copy(
                    src_ref=src[t], dst_ref=land[t].at[4 * x + 2 * y + c],
                    send_sem=sems[2 * (4 * t + k)], recv_sem=sems[2 * (4 * t + k) + 1],
                    device_id=to, device_id_type=MESH).start()
        token[...] = jnp.zeros_like(token)

    res = pl.pallas_call(
        body, name=name,
        in_specs=[HBM_SPEC] * (2 * n) + [ANY],
        out_specs=[SEM_SPEC] * ns + [HBM_SPEC] * (2 * n) + [pl.BlockSpec(memory_space=pltpu.VMEM)],
        out_shape=[pltpu.SemaphoreType.DMA(())] * ns + _hbm_like(srcs) + _hbm_like(lands)
        + [jax.ShapeDtypeStruct((8, 128), F32)],
        input_output_aliases={i: ns + i for i in range(2 * n)},
        compiler_params=pltpu.CompilerParams(has_side_effects=DATAFLOW),
    )(*[_hbm(a) for a in srcs], *[_hbm(a) for a in lands], after)
    sems = [[(res[2 * (4 * t + k)], res[2 * (4 * t + k) + 1]) for k in range(4)] for t in range(n)]
    return sems, res[ns:ns + n], res[ns + n:ns + 2 * n], res[-1]


def _ag_forward(srcs, lands, sems1, after, name):
    n = len(srcs)
    flat1 = [s for t in range(n) for k in range(1, 4) for s in sems1[t][k]]
    n1 = len(flat1)

    def body(*refs):
        src, land = refs[:n], refs[n:2 * n]
        s1 = refs[2 * n:2 * n + n1]
        s2 = refs[2 * n + n1 + 1:2 * n + n1 + 1 + 6 * n]
        x, y, c, chips = _place()
        for j, (cx, cy) in enumerate(chips):
            for t in range(n):
                blk = land[t].at[4 * cx + 2 * cy + c]
                pltpu.make_async_remote_copy(
                    src_ref=src[t], dst_ref=blk, send_sem=s1[2 * (3 * t + j)], recv_sem=s1[2 * (3 * t + j) + 1],
                    device_id=(cx, cy, c), device_id_type=MESH).wait_recv()
                pltpu.make_async_remote_copy(
                    src_ref=blk, dst_ref=blk, send_sem=s2[2 * (3 * t + j)], recv_sem=s2[2 * (3 * t + j) + 1],
                    device_id=(x, y, 1 - c), device_id_type=MESH).start()

    res = pl.pallas_call(
        body, name=name,
        in_specs=[HBM_SPEC] * (2 * n) + [SEM_SPEC] * n1 + [ANY],
        out_specs=[SEM_SPEC] * (6 * n) + [HBM_SPEC] * n,
        out_shape=[pltpu.SemaphoreType.DMA(())] * (6 * n) + _hbm_like(lands),
        input_output_aliases={n + i: 6 * n + i for i in range(n)},
        compiler_params=pltpu.CompilerParams(has_side_effects=DATAFLOW),
    )(*srcs, *lands, *flat1, after)
    sems2 = [[(res[2 * (3 * t + j)], res[2 * (3 * t + j) + 1]) for j in range(3)] for t in range(n)]
    return sems2, res[6 * n:]


def _ag_finish(srcs, lands, sems1, sems2, after, name):
    n = len(srcs)
    flat1 = [s for t in range(n) for k in range(4) for s in sems1[t][k]]
    flat2 = [s for t in range(n) for j in range(3) for s in sems2[t][j]]
    n1, n2 = len(flat1), len(flat2)

    def body(*refs):
        src, land = refs[:n], refs[n:2 * n]
        s1 = refs[2 * n:2 * n + n1]
        s2 = refs[2 * n + n1:2 * n + n1 + n2]
        x, y, c, chips = _place()
        sib = (x, y, 1 - c)
        for t in range(n):
            own = land[t].at[4 * x + 2 * y + 1 - c]
            pltpu.make_async_remote_copy(
                src_ref=src[t], dst_ref=own, send_sem=s1[8 * t], recv_sem=s1[8 * t + 1],
                device_id=sib, device_id_type=MESH).wait_recv()
            for k in range(4):
                pltpu.make_async_remote_copy(
                    src_ref=src[t], dst_ref=own, send_sem=s1[2 * (4 * t + k)], recv_sem=s1[2 * (4 * t + k) + 1],
                    device_id=sib, device_id_type=MESH).wait_send()
            for j, (cx, cy) in enumerate(chips):
                blk = land[t].at[4 * cx + 2 * cy + 1 - c]
                cp = pltpu.make_async_remote_copy(
                    src_ref=blk, dst_ref=blk, send_sem=s2[2 * (3 * t + j)], recv_sem=s2[2 * (3 * t + j) + 1],
                    device_id=sib, device_id_type=MESH)
                cp.wait_send()
                cp.wait_recv()

    return pl.pallas_call(
        body, name=name,
        in_specs=[HBM_SPEC] * (2 * n) + [SEM_SPEC] * (n1 + n2) + [ANY],
        out_specs=[HBM_SPEC] * n,
        out_shape=_hbm_like(lands),
        input_output_aliases={n + i: i for i in range(n)},
        compiler_params=pltpu.CompilerParams(has_side_effects=DATAFLOW),
    )(*srcs, *lands, *flat1, *flat2, after)


def _pair_copies(srcs, dsts, sems):
    x, y, c, _ = _place()
    nt = len(srcs)
    return [pltpu.make_async_remote_copy(
        src_ref=srcs[t].at[2 * j + 1 - c], dst_ref=dsts[t].at[j],
        send_sem=sems[2 * (NCHIP * t + j)], recv_sem=sems[2 * (NCHIP * t + j) + 1],
        device_id=(x, y, 1 - c), device_id_type=MESH) for t in range(nt) for j in range(NCHIP)]


def _pair_start(grads, carry, name):
    nt = len(grads)
    ns = 2 * NCHIP * nt
    zones = [_hbm(lax.empty((NCHIP,) + a.shape[1:], a.dtype)) for a in grads]
    extra = [] if carry is None else [_hbm(carry)]
    ne = len(extra)

    def body(*refs):
        for cp in _pair_copies(refs[:nt], refs[nt:2 * nt], refs[2 * nt + ne:2 * nt + ne + ns]):
            cp.start()

    res = pl.pallas_call(
        body, name=name,
        in_specs=[HBM_SPEC] * (2 * nt + ne),
        out_specs=[SEM_SPEC] * ns + [HBM_SPEC] * (2 * nt + ne),
        out_shape=[pltpu.SemaphoreType.DMA(())] * ns + _hbm_like(grads) + _hbm_like(zones) + _hbm_like(extra),
        input_output_aliases={i: ns + i for i in range(2 * nt + ne)},
        compiler_params=pltpu.CompilerParams(has_side_effects=DATAFLOW),
    )(*[_hbm(a) for a in grads], *zones, *extra)
    handle = (list(res[:ns]), list(res[ns:ns + nt]), list(res[ns + nt:ns + 2 * nt]))
    return handle, (res[ns + 2 * nt] if ne else None)


def _pair_wait(handle, after, name):
    sems, srcs, zones = handle
    nt, ns = len(srcs), len(sems)

    def body(*refs):
        for cp in _pair_copies(refs[:nt], refs[nt:2 * nt], refs[2 * nt:2 * nt + ns]):
            cp.wait_send()
            cp.wait_recv()

    return pl.pallas_call(
        body, name=name,
        in_specs=[HBM_SPEC] * (2 * nt) + [SEM_SPEC] * ns + [ANY],
        out_specs=[HBM_SPEC] * nt,
        out_shape=_hbm_like(zones),
        input_output_aliases={nt + i: i for i in range(nt)},
        compiler_params=pltpu.CompilerParams(has_side_effects=DATAFLOW),
    )(*srcs, *zones, *sems, after)


def _rows_tile(r, row_bytes, cap_bytes):
    best = None
    for tr in range(16, r + 1, 16):
        if r % tr == 0 and tr * row_bytes <= cap_bytes:
            best = tr
    return best if best is not None else r


def _pair_sum(own, got, cidx, name):
    _, _, r, cdim = own.shape
    tr = _rows_tile(r, 2 * cdim, 2 * 1024 * 1024)

    def body(c_ref, a_ref, b_ref, o_ref):
        o_ref[...] = (a_ref[...].astype(F32) + b_ref[...].astype(F32)).astype(BF16)

    return pl.pallas_call(
        body, name=name,
        grid_spec=pltpu.PrefetchScalarGridSpec(
            num_scalar_prefetch=1, grid=(NCHIP, r // tr),
            in_specs=[pl.BlockSpec((None, None, tr, cdim), lambda j, i, c_ref: (j, c_ref[0], i, 0)),
                      pl.BlockSpec((None, tr, cdim), lambda j, i, c_ref: (j, i, 0))],
            out_specs=pl.BlockSpec((None, tr, cdim), lambda j, i, c_ref: (j, i, 0))),
        out_shape=jax.ShapeDtypeStruct((NCHIP, r, cdim), BF16),
        compiler_params=_params("arbitrary", "arbitrary"),
    )(cidx, own, got)


def _chip_copies(srcs, zones, slots, sems):
    x, y, c, chips = _place()
    out = []
    for t, (z, l) in enumerate(slots):
        for k, (cx, cy) in enumerate(chips):
            dst = zones[z].at[k] if l is None else zones[z].at[k, l]
            out.append(pltpu.make_async_remote_copy(
                src_ref=srcs[t].at[2 * cx + cy], dst_ref=dst,
                send_sem=sems[2 * (3 * t + k)], recv_sem=sems[2 * (3 * t + k) + 1],
                device_id=(cx, cy, c), device_id_type=MESH))
    return out


def _chip_start(sums, zones, slots, carry, name):
    nt, nz = len(sums), len(zones)
    ns = 6 * nt
    extra = [] if carry is None else [_hbm(carry)]
    ne = len(extra)

    def body(*refs):
        for cp in _chip_copies(refs[:nt], refs[nt:nt + nz], slots, refs[nt + nz + ne:nt + nz + ne + ns]):
            cp.start()

    res = pl.pallas_call(
        body, name=name,
        in_specs=[HBM_SPEC] * (nt + nz + ne),
        out_specs=[SEM_SPEC] * ns + [HBM_SPEC] * (nt + nz + ne),
        out_shape=[pltpu.SemaphoreType.DMA(())] * ns + _hbm_like(sums) + _hbm_like(zones) + _hbm_like(extra),
        input_output_aliases={i: ns + i for i in range(nt + nz + ne)},
        compiler_params=pltpu.CompilerParams(has_side_effects=DATAFLOW),
    )(*[_hbm(a) for a in sums], *zones, *extra)
    return (list(res[:ns]), list(res[ns:ns + nt]), list(res[ns + nt:ns + nt + nz]),
            (res[ns + nt + nz] if ne else None))


def _chip_wait(started, zones, zone_ids, after, name):
    started = [(sums, [(zone_ids.index(z), l) for z, l in slots], sems) for sums, slots, sems in started]
    nz = len(zones)
    flat_src = [a for sums, _, _ in started for a in sums]
    flat_sem = [s for _, _, sems in started for s in sems]
    n_src, n_sem = len(flat_src), len(flat_sem)

    def body(*refs):
        srcs, zs, sems = refs[:n_src], refs[n_src:n_src + nz], refs[n_src + nz:n_src + nz + n_sem]
        so, se = 0, 0
        for sums, slots, sem_list in started:
            for cp in _chip_copies(srcs[so:so + len(sums)], zs, slots, sems[se:se + len(sem_list)]):
                cp.wait_send()
                cp.wait_recv()
            so += len(sums)
            se += len(sem_list)

    return pl.pallas_call(
        body, name=name,
        in_specs=[HBM_SPEC] * (n_src + nz) + [SEM_SPEC] * n_sem + [ANY],
        out_specs=[HBM_SPEC] * nz,
        out_shape=_hbm_like(zones),
        input_output_aliases={n_src + i: i for i in range(nz)},
        compiler_params=pltpu.CompilerParams(has_side_effects=DATAFLOW),
    )(*flat_src, *zones, *flat_sem, after)


def _small_allreduce(parts, y_first, after, name):
    nt = len(parts)

    def body(*refs):
        srcs, outs, bufs = refs[:nt], refs[nt + 1:2 * nt + 1], refs[2 * nt + 1:3 * nt + 1]
        send_sems, recv_sems = refs[3 * nt + 1:]
        x, y, c, _ = _place()
        along = {"c": (x, y, 1 - c), "x": (1 - x, y, c), "y": (x, 1 - y, c)}
        for t in range(nt):
            outs[t][...] = srcs[t][...]
        for step in range(3):
            order = [("c", "y", "x") if t in y_first else ("c", "x", "y") for t in range(nt)]
            copies = [pltpu.make_async_remote_copy(
                src_ref=outs[t], dst_ref=bufs[t].at[step],
                send_sem=send_sems.at[step, t], recv_sem=recv_sems.at[step, t],
                device_id=along[order[t][step]], device_id_type=MESH) for t in range(nt)]
            for cp in copies:
                cp.start()
            for cp in copies:
                cp.wait()
            for t in range(nt):
                outs[t][...] = outs[t][...] + bufs[t][step]

    vm = pl.BlockSpec(memory_space=pltpu.VMEM)
    return pl.pallas_call(
        body, name=name,
        in_specs=[vm] * nt + [ANY], out_specs=[vm] * nt,
        out_shape=[jax.ShapeDtypeStruct(a.shape, F32) for a in parts],
        scratch_shapes=[pltpu.VMEM((3,) + a.shape, F32) for a in parts]
        + [pltpu.SemaphoreType.DMA((3, nt)), pltpu.SemaphoreType.DMA((3, nt))],
        compiler_params=pltpu.CompilerParams(has_side_effects=True, vmem_limit_bytes=VMEM_LIMIT),
    )(*parts, after)


def _adam_math(w, g, m, v):
    m2 = ADAM_B1 * m + (1.0 - ADAM_B1) * g
    v2 = ADAM_B2 * v + (1.0 - ADAM_B2) * (g * g)
    m_hat = m2 / (1.0 - ADAM_B1 ** ADAM_STEP)
    v_hat = v2 / (1.0 - ADAM_B2 ** ADAM_STEP)
    delta = -ADAM_LR * (m_hat / (jnp.sqrt(v_hat) + ADAM_EPS) + ADAM_WD * w)
    return delta, m2, v2


def _adam_big(w, m, v, parts, mine, chip, name):
    nl, r, cdim = w.shape
    tr = _rows_tile(r, 4 * cdim, 3 * 512 * 1024)

    def body(c_ref, w_ref, m_ref, v_ref, p_ref, *rest):
        mine_refs, (g_ref, d_ref, mo_ref, vo_ref) = rest[:nl], rest[nl:]
        own = mine_refs[0][...]
        for l in range(1, nl):
            own = jnp.where(pl.program_id(0) == l, mine_refs[l][...], own)
        g = ((p_ref[0].astype(F32) + p_ref[1].astype(F32)) + p_ref[2].astype(F32)) + own.astype(F32)
        delta, m2, v2 = _adam_math(w_ref[...], g, m_ref[...], v_ref[...])
        g_ref[...] = g
        d_ref[...] = delta
        mo_ref[...] = m2
        vo_ref[...] = v2

    spec = pl.BlockSpec((None, tr, cdim), lambda l, i, c_ref: (l, i, 0))
    mine_specs = [pl.BlockSpec((None, tr, cdim), lambda l, i, c_ref, ll=ll: (c_ref[0], jnp.where(l == ll, i, 0), 0))
                  for ll in range(nl)]
    return pl.pallas_call(
        body, name=name,
        grid_spec=pltpu.PrefetchScalarGridSpec(
            num_scalar_prefetch=1, grid=(nl, r // tr),
            in_specs=[spec, spec, spec, pl.BlockSpec((3, None, tr, cdim), lambda l, i, c_ref: (0, l, i, 0))]
            + mine_specs,
            out_specs=[spec] * 4),
        out_shape=[jax.ShapeDtypeStruct(w.shape, F32)] * 4,
        compiler_params=_params("arbitrary", "arbitrary"),
    )(chip, w, m, v, parts, *mine)


def _adam_small(ws, gs, ms, vs, name):
    n = len(ws)

    def body(*refs):
        w_r, g_r, m_r, v_r = refs[:n], refs[n:2 * n], refs[2 * n:3 * n], refs[3 * n:4 * n]
        d_o, m_o, v_o = refs[4 * n:5 * n], refs[5 * n:6 * n], refs[6 * n:7 * n]
        for t in range(n):
            delta, m2, v2 = _adam_math(w_r[t][...], g_r[t][...], m_r[t][...], v_r[t][...])
            d_o[t][...] = delta
            m_o[t][...] = m2
            v_o[t][...] = v2

    vm = pl.BlockSpec(memory_space=pltpu.VMEM)
    shapes = [jax.ShapeDtypeStruct(a.shape, F32) for a in ws]
    return pl.pallas_call(
        body, name=name, in_specs=[vm] * (4 * n), out_specs=[vm] * (3 * n), out_shape=shapes * 3,
        compiler_params=pltpu.CompilerParams(vmem_limit_bytes=VMEM_LIMIT),
    )(*ws, *gs, *ms, *vs)


def kernel(x, norm_mix, norm_ffn, norm_final, ab_w_in, a_ln_g, a_ln_b, a_w_s, a_b_s, b_conv_w, b_conv_b, b_ln_g, b_ln_b, ab_w_out, c_w_in, c_conv_w, c_w_out, f_w_up, f_conv_w, f_w_down, loss_target, m_norm_mix, m_norm_ffn, m_norm_final, m_ab_w_in, m_a_ln_g, m_a_ln_b, m_a_w_s, m_a_b_s, m_b_conv_w, m_b_conv_b, m_b_ln_g, m_b_ln_b, m_ab_w_out, m_c_w_in, m_c_conv_w, m_c_w_out, m_f_w_up, m_f_conv_w, m_f_w_down, v_norm_mix, v_norm_ffn, v_norm_final, v_ab_w_in, v_a_ln_g, v_a_ln_b, v_a_w_s, v_a_b_s, v_b_conv_w, v_b_conv_b, v_b_ln_g, v_b_ln_b, v_ab_w_out, v_c_w_in, v_c_conv_w, v_c_w_out, v_f_w_up, v_f_conv_w, v_f_w_down):
    s = x.shape[1]
    x0 = x.reshape(s, D)
    tgt = loss_target.reshape(s, D)
    xi, yi, ci = lax.axis_index("x"), lax.axis_index("y"), lax.axis_index("c")
    dev = 4 * xi + 2 * yi + ci
    cidx = ci.astype(jnp.int32).reshape(1)

    bf = lambda a: a.astype(BF16)
    slab_w = 6 * CHUNK
    pad = lambda a, rows: jnp.pad(a, ((0, rows - a.shape[0]), (0, slab_w - a.shape[1])))
    slab = jnp.concatenate([pad(b_conv_w[0], 32), pad(c_conv_w[0], 8), pad(f_conv_w.reshape(6, FB), 8)], axis=0)
    later = [bf(ab_w_in[0]), bf(ab_w_out[0]), slab, bf(f_w_up[0]), bf(f_w_down[0]), bf(c_w_in[0]), bf(c_w_out[0]),
             bf(f_w_up[1]), bf(f_w_down[1])]
    lands = [_zone(a, dev) for a in later]
    groups = [[0], [1, 2], [3, 4], [5, 6], [7, 8]]
    ag_sems, later, lands, ag_token = _ag_start(later, lands, x0, "ag_start")

    causal = jnp.tril(jnp.ones((CHUNK, CHUNK), F32))
    wsm = (a_w_s[0] * causal).astype(BF16)
    bs_col = a_b_s.reshape(HEADS, CHUNK, 1)
    nm = [norm_mix[0:1], norm_mix[1:2]]
    nf = [norm_ffn[0:1], norm_ffn[1:2]]
    nfin = norm_final.reshape(1, D)

    def arrive(g, after_ici, after_d2d, tag):
        srcs = [later[t] for t in groups[g]]
        zone = [lands[t] for t in groups[g]]
        sems1 = [ag_sems[t] for t in groups[g]]
        sems2, zone = _ag_forward(srcs, zone, sems1, after_ici, "ag_forward_" + tag)
        return _ag_finish(srcs, zone, sems1, sems2, after_d2d, "ag_finish_" + tag)

    h0 = _rms_fwd(x0, nm[0], "rms_mix0", after=ag_token)
    (win0,) = arrive(0, h0, h0, "w_in")
    z = _mm_in(h0, win0, "mm_ab_in")
    wout0, slab_g = arrive(1, z, z, "first")
    wout0 = wout0.reshape(D, D)
    bcw = jnp.transpose(slab_g[:, 0:BCONV, 0:DA // NDEV], (1, 0, 2)).reshape(BCONV, DA)
    ccw = jnp.transpose(slab_g[:, 32:35, 0:D // NDEV], (1, 0, 2)).reshape(3, D)
    fcw_g = slab_g[:, 40:46, 0:FB].reshape(2, NG, 2, 3, FB)
    fcws = [fcw_g[:, :, 0], fcw_g[:, :, 1]]
    ycat, yb2 = _ab_fwd(z, a_ln_g, a_ln_b, wsm, bs_col, bcw, b_conv_b, b_ln_g, b_ln_b, "ab_fwd")
    x1, h1 = _mm_out(ycat, wout0, x0, nf[0], "mm_ab_out")
    wup0, wdn0 = arrive(2, x1, x1, "ffn0")
    up0, upc0, x2, h2 = _ffn_fwd(h1, x1, wup0.reshape(2, NG, D, FB), fcws[0], wdn0.reshape(DFF, D), nm[1],
                                 "ffn_fwd0")
    cin, cout = arrive(3, x2, x2, "c")
    cout = cout.reshape(D, D)
    zc = _mm_in(h2, cin, "mm_c_in")
    rc = _c_fwd(zc, ccw, "c_fwd")
    x3, h3 = _mm_out(rc, cout, x2, nf[1], "mm_c_out")
    wup1, wdn1 = arrive(4, rc, x3, "ffn1")
    wups = [wup0.reshape(2, NG, D, FB), wup1.reshape(2, NG, D, FB)]
    wdns = [wdn0.reshape(DFF, D), wdn1.reshape(DFF, D)]
    up1, upc1, x4 = _ffn_fwd(h3, x3, wups[1], fcws[1], wdns[1], None, "ffn_fwd1")
    dx4, dx4b, dnfin, loss_part = _final(x4, tgt, nfin, "final_loss")

    zshape = lambda *sh: _hbm(lax.empty((3,) + sh, BF16))
    zones = [zshape(D, 2 * D // NDEV), zshape(D // NDEV, D), zshape(D, 3 * D // NDEV), zshape(D // NDEV, D),
             zshape(2, FB, D), zshape(2, DFF // NDEV, D)]
    started = []

    def pair_sums(grads, handle, after, tag):
        del grads
        got = _pair_wait(handle, after, "rs_pair_wait_" + tag)
        return [_pair_sum(b.reshape((NCHIP, 2) + b.shape[1:]), g, cidx, "rs_pair_sum_%s%d" % (tag, t))
                for t, (b, g) in enumerate(zip(handle[1], got))]

    def chip_start(sums, slots, carry, tag):
        sems, sums, new_zones, carry = _chip_start(sums, zones, slots, carry, "rs_chip_start_" + tag)
        zones[:] = new_zones
        started.append((sums, slots, sems))
        return sums, carry

    rows8 = lambda g, r: g.reshape(NDEV, r, D)
    a1, dup1, dx3, dx3b, dnf1, dfcw1 = _ffn_bwd(dx4, up1, upc1, wups[1], fcws[1], wdns[1], x3, nf[1], "ffn_bwd1")
    g_f1 = [_dw_up(h3, dup1, "dw_up1"), rows8(_dw_dn(a1, dx4b, "dw_dn1"), DFF // NDEV)]
    hd_f1, dx3b = _pair_start(g_f1, dx3b, "rs_pair_start_f1")
    drc = _mm_nt(dx3b, cout, "mm_c_out_bwd")
    g_cout = rows8(_dw_rows(rc, dx3b, "dw_c_out"), D // NDEV)
    s_f1 = pair_sums(g_f1, hd_f1, g_cout, "f1")
    s_f1, drc = chip_start(s_f1, [(4, 1), (5, 1)], drc, "f1")
    dzc, dccw = _c_bwd(drc, zc, ccw, "c_bwd")
    dx2, dx2b, dnm1 = _mm_nt_rms(dzc, cin, x2, nm[1], dx3, True, "mm_c_in_bwd")
    g_c = [_dw_cols(h2, dzc, NDEV, 3 * D // NDEV, "dw_c_in"), g_cout]
    hd_c, dx2 = _pair_start(g_c, dx2, "rs_pair_start_c")
    a0, dup0, dx1, dx1b, dnf0, dfcw0 = _ffn_bwd(dx2, up0, upc0, wups[0], fcws[0], wdns[0], x1, nf[0], "ffn_bwd0")
    s_c = pair_sums(g_c, hd_c, dx1b, "c")
    s_c, dx1b = chip_start(s_c, [(2, None), (3, None)], dx1b, "c")
    g_f0 = [_dw_up(h1, dup0, "dw_up0"), rows8(_dw_dn(a0, dx2b, "dw_dn0"), DFF // NDEV)]
    hd_f0, dx1b = _pair_start(g_f0, dx1b, "rs_pair_start_f0")
    dycat = _mm_nt(dx1b, wout0, "mm_ab_out_bwd")
    g_wout0 = rows8(_dw_rows(ycat, dx1b, "dw_ab_out"), D // NDEV)
    s_f0 = pair_sums(g_f0, hd_f0, g_wout0, "f0")
    s_f0, dycat = chip_start(s_f0, [(4, 0), (5, 0)], dycat, "f0")
    dz, g512, dws, dbs = _ab_bwd(dycat, z, yb2, a_ln_g, a_ln_b, wsm, bs_col, bcw, b_ln_g, b_ln_b, "ab_bwd")
    grad_x, dnm0 = _mm_nt_rms(dz, win0, x0, nm[0], dx1, False, "mm_ab_in_bwd")
    g_ab = [_dw_cols(h0, dz, NDEV, 2 * D // NDEV, "dw_ab_in"), g_wout0]
    hd_ab, _ = _pair_start(g_ab, None, "rs_pair_start_ab")

    g1024 = jnp.concatenate([dnm0, dnm1, dnf0, dnf1, dnfin, dccw], axis=0)
    gfc = jnp.concatenate([dfcw0, dfcw1], axis=0).reshape(2 * NG * 2 * 3, FB)
    g1024, g512, dws, dbs, gfc, loss_sum = _small_allreduce(
        [g1024, g512, dws.reshape(HEADS * CHUNK, CHUNK), dbs.reshape(HEADS, CHUNK), gfc, loss_part], (2,),
        hd_ab[1][0], "small_allreduce")
    loss = loss_sum[0, 0]
    s_ab = pair_sums(g_ab, hd_ab, g1024, "ab")
    s_ab, _ = chip_start(s_ab, [(0, None), (1, None)], None, "ab")
    p_cin, p_cout, p_wup, p_wdn = _chip_wait(started[:3], zones[2:], [2, 3, 4, 5], s_ab[0], "rs_chip_wait_early")

    chip = (2 * xi + yi).astype(jnp.int32).reshape(1)

    def big_update(w, m, v, parts, mine, name):
        shp = w.shape
        w3, m3, v3 = (a.reshape((-1,) + shp[-2:]) for a in (w, m, v))
        p4 = parts.reshape((3,) + w3.shape)
        return [o.reshape(shp) for o in _adam_big(w3, m3, v3, p4, mine, chip, name)]

    u_cin = big_update(c_w_in, m_c_w_in, v_c_w_in, p_cin, [s_c[0]], "adam_c_w_in")
    u_cout = big_update(c_w_out, m_c_w_out, v_c_w_out, p_cout, [s_c[1]], "adam_c_w_out")
    tr_ = lambda a: jnp.swapaxes(a, 1, 2)
    u_wup = [tr_(o) for o in big_update(tr_(f_w_up), tr_(m_f_w_up), tr_(v_f_w_up), p_wup,
                                        [s_f0[0], s_f1[0]], "adam_f_w_up")]
    u_wdn = big_update(f_w_down, m_f_w_down, v_f_w_down, p_wdn, [s_f0[1], s_f1[1]], "adam_f_w_down")
    p_win0, p_wout0 = _chip_wait(started[3:], zones[:2], [0, 1], u_wdn[0], "rs_chip_wait_late")
    u_win0 = big_update(ab_w_in, m_ab_w_in, v_ab_w_in, p_win0, [s_ab[0]], "adam_ab_w_in")
    u_wout0 = big_update(ab_w_out, m_ab_w_out, v_ab_w_out, p_wout0, [s_ab[1]], "adam_ab_w_out")

    g_norm_mix = g1024[0:2]
    g_norm_ffn = g1024[2:4]
    g_norm_final = g1024[4:5]
    g_ccw = lax.dynamic_slice(g1024[5:8], (0, dev * (D // NDEV)), (3, D // NDEV))
    g_bcw = lax.dynamic_slice(g512[8:8 + BCONV], (0, dev * (DA // NDEV)), (BCONV, DA // NDEV))
    gfc = gfc.reshape(2, NG, 2, 3, FB)
    g_fcw = lax.dynamic_slice(gfc, (0, dev % NG, dev // NG, 0, 0), (2, 1, 1, 3, FB)).reshape(2, 3, FB)
    small_w = [norm_mix, norm_ffn, nfin, a_ln_g, a_ln_b, a_w_s[0], a_b_s[0], b_conv_w[0], b_conv_b,
               b_ln_g, b_ln_b, c_conv_w[0], f_conv_w]
    small_g = [g_norm_mix, g_norm_ffn, g_norm_final, g512[0:1], g512[1:2],
               dws.reshape(HEADS, CHUNK, CHUNK), dbs, g_bcw, g512[2:3],
               g512[3:4], g512[4:5], g_ccw, g_fcw]
    small_m = [m_norm_mix, m_norm_ffn, m_norm_final.reshape(1, D), m_a_ln_g, m_a_ln_b, m_a_w_s[0], m_a_b_s[0],
               m_b_conv_w[0], m_b_conv_b, m_b_ln_g, m_b_ln_b, m_c_conv_w[0], m_f_conv_w]
    small_v = [v_norm_mix, v_norm_ffn, v_norm_final.reshape(1, D), v_a_ln_g, v_a_ln_b, v_a_w_s[0], v_a_b_s[0],
               v_b_conv_w[0], v_b_conv_b, v_b_ln_g, v_b_ln_b, v_c_conv_w[0], v_f_conv_w]
    upd = _adam_small(small_w, small_g, small_m, small_v, "adam_small")
    ns = len(small_w)
    orig = [norm_mix, norm_ffn, norm_final, a_ln_g, a_ln_b, a_w_s, a_b_s, b_conv_w, b_conv_b,
            b_ln_g, b_ln_b, c_conv_w, f_conv_w]
    sg_out = [g.reshape(o.shape) for g, o in zip(small_g, orig)]
    sd_out = [a.reshape(o.shape) for a, o in zip(upd[0:ns], orig)]
    sm_out = [a.reshape(o.shape) for a, o in zip(upd[ns:2 * ns], orig)]
    sv_out = [a.reshape(o.shape) for a, o in zip(upd[2 * ns:3 * ns], orig)]

    def assemble(small, k):
        return [small[0], small[1], small[2], u_win0[k], small[3], small[4], small[5], small[6], small[7],
                small[8], small[9], small[10], u_wout0[k], u_cin[k], small[11], u_cout[k], u_wup[k],
                small[12], u_wdn[k]]

    grads = assemble(sg_out, 0)
    deltas = assemble(sd_out, 1)
    new_m = assemble(sm_out, 2)
    new_v = assemble(sv_out, 3)
    return (loss, grad_x.reshape(1, s, D), *grads, *deltas, *new_m, *new_v)
```

```python
import math

import jax
import jax.numpy as jnp
from jax import lax
from jax.experimental import pallas as pl
from jax.experimental.pallas import tpu as pltpu

F32 = jnp.float32
BF16 = jnp.bfloat16

D = 1024
DA = 512
HEADS = 4
CHUNK = 128
DFF = 2816
NDEV = 8
NCHIP = 4
FB = DFF * 2 // NDEV
NG = DFF // FB
BCONV = 31
EPS = 1e-6
HALO = 16
HALO_B = 32
RC = 32
NPART = 2
VMEM_LIMIT = 52 * 1024 * 1024
INV_SQRT2 = 1.0 / math.sqrt(2.0)
INV_SQRT_2PI = 1.0 / math.sqrt(2.0 * math.pi)

ADAM_LR = 0.001
ADAM_B1 = 0.9
ADAM_B2 = 0.999
ADAM_EPS = 1e-08
ADAM_WD = 0.01
ADAM_STEP = 10

MESH = pl.DeviceIdType.MESH
ANY = pl.BlockSpec(memory_space=pl.ANY)
NT_DIMS = (((1,), (1,)), ((), ()))
TN_DIMS = (((0,), (0,)), ((), ()))


def _params(*sem):
    return pltpu.CompilerParams(dimension_semantics=sem, vmem_limit_bytes=VMEM_LIMIT)


def _tile(s, want):
    return min(want, s)


def _sigmoid(x):
    return jax.nn.sigmoid(x)


def _dsilu(x, sg):
    return sg * (1.0 + x * (1.0 - sg))


def _gelu(x):
    return 0.5 * x * (1.0 + lax.erf(x * INV_SQRT2))


def _dgelu(x):
    return 0.5 * (1.0 + lax.erf(x * INV_SQRT2)) + x * jnp.exp(-0.5 * x * x) * INV_SQRT_2PI


def _ln_fwd(x, g, b):
    mu = jnp.mean(x, axis=-1, keepdims=True)
    xc = x - mu
    var = jnp.mean(xc * xc, axis=-1, keepdims=True)
    rstd = lax.rsqrt(var + EPS)
    xhat = xc * rstd
    return xhat * g + b, xhat, rstd


def _ln_bwd(dy, xhat, rstd, g):
    dxh = dy * g
    m1 = jnp.mean(dxh, axis=-1, keepdims=True)
    m2 = jnp.mean(dxh * xhat, axis=-1, keepdims=True)
    return rstd * (dxh - m1 - xhat * m2)


def _rms_bwd_math(dh, x, g):
    r = lax.rsqrt(jnp.mean(x * x, axis=-1, keepdims=True) + EPS)
    xhat = x * r
    dg = jnp.sum(dh * xhat, axis=0, keepdims=True)
    u = dh * g
    dx = r * (u - xhat * jnp.mean(u * xhat, axis=-1, keepdims=True))
    return dx, dg


def _conv3(xe, cw, halo):
    x0 = xe[halo:]
    x1 = pltpu.roll(xe, 1, 0)[halo:]
    x2 = pltpu.roll(xe, 2, 0)[halo:]
    return cw[2] * x0 + cw[1] * x1 + cw[0] * x2, (x0, x1, x2)


def _conv3_bwd_in(dce, cw, ts):
    n = dce.shape[0]
    d1 = pltpu.roll(dce, n - 1, 0)[:ts]
    d2 = pltpu.roll(dce, n - 2, 0)[:ts]
    return cw[2] * dce[:ts] + cw[1] * d1 + cw[0] * d2


def _conv3_bwd_w(dc, taps):
    x0, x1, x2 = taps
    return [jnp.sum(dc * x2, axis=0, keepdims=True), jnp.sum(dc * x1, axis=0, keepdims=True),
            jnp.sum(dc * x0, axis=0, keepdims=True)]


def _rms_fwd(x, g, name, after=None):
    s = x.shape[0]
    ts = _tile(s, 512)

    def body(x_ref, g_ref, *rest):
        h_ref = rest[-1]
        xv = x_ref[...]
        r = lax.rsqrt(jnp.mean(xv * xv, axis=-1, keepdims=True) + EPS)
        h_ref[...] = (xv * r * g_ref[...]).astype(BF16)

    extra = [] if after is None else [after]
    return pl.pallas_call(
        body, grid=(s // ts,), name=name,
        in_specs=[pl.BlockSpec((ts, D), lambda i: (i, 0)), pl.BlockSpec((1, D), lambda i: (0, 0))]
        + [ANY] * len(extra),
        out_specs=pl.BlockSpec((ts, D), lambda i: (i, 0)),
        out_shape=jax.ShapeDtypeStruct((s, D), BF16),
        compiler_params=_params("parallel"),
    )(x, g, *extra)


MXU_COLS = 256


def _pair(bn):
    return 1 if bn % MXU_COLS == 0 else 2


def _cols(w_ref, b, pair):
    return w_ref[b] if pair == 1 else jnp.concatenate([w_ref[b + q] for q in range(pair)], axis=1)


def _mm_in(h, wblk, name):
    s = h.shape[0]
    nb, _, bn = wblk.shape
    pair = _pair(bn)
    ts = _tile(s, 1024)

    def body(h_ref, w_ref, o_ref):
        hv = h_ref[...]
        for b in range(0, nb, pair):
            o_ref[:, b * bn:(b + pair) * bn] = jnp.dot(hv, _cols(w_ref, b, pair),
                                                       preferred_element_type=F32).astype(BF16)

    return pl.pallas_call(
        body, grid=(s // ts,), name=name,
        in_specs=[pl.BlockSpec((ts, D), lambda i: (i, 0)), pl.BlockSpec((nb, D, bn), lambda i: (0, 0, 0))],
        out_specs=pl.BlockSpec((ts, nb * bn), lambda i: (i, 0)),
        out_shape=jax.ShapeDtypeStruct((s, nb * bn), BF16),
        compiler_params=_params("parallel"),
    )(h, wblk)


def _rms_math(xv, g):
    r = lax.rsqrt(jnp.mean(xv * xv, axis=-1, keepdims=True) + EPS)
    return (xv * r * g).astype(BF16)


def _mm_out(y, w, xres, gnext, name):
    s = y.shape[0]
    ts = _tile(s, 1024)

    def body(y_ref, w_ref, x_ref, g_ref, o_ref, h_ref):
        xn = x_ref[...] + jnp.dot(y_ref[...], w_ref[...], preferred_element_type=F32)
        o_ref[...] = xn
        h_ref[...] = _rms_math(xn, g_ref[...])

    return pl.pallas_call(
        body, grid=(s // ts,), name=name,
        in_specs=[pl.BlockSpec((ts, D), lambda i: (i, 0)), pl.BlockSpec((D, D), lambda i: (0, 0)),
                  pl.BlockSpec((ts, D), lambda i: (i, 0)), pl.BlockSpec((1, D), lambda i: (0, 0))],
        out_specs=[pl.BlockSpec((ts, D), lambda i: (i, 0)), pl.BlockSpec((ts, D), lambda i: (i, 0))],
        out_shape=[jax.ShapeDtypeStruct((s, D), F32), jax.ShapeDtypeStruct((s, D), BF16)],
        compiler_params=_params("parallel"),
    )(y, w, xres, gnext)


CONV_ROWS = 32


def _rolled_copies(dst_ref, xe, back):
    n = xe.shape[0]
    dst_ref[0] = xe
    for r in range(1, 8):
        dst_ref[r] = pltpu.roll(xe, n - r if back else r, 0)


def _conv31(rolled_ref, cw_ref, ts, out_ref, bias):
    for o in range(0, ts, CONV_ROWS):
        acc = jnp.zeros((CONV_ROWS, DA), F32) + bias
        for sh in range(BCONV):
            q, r = divmod(sh, 8)
            lo = HALO_B - 8 * q + o
            acc = acc + cw_ref[BCONV - 1 - sh:BCONV - sh, :] * rolled_ref[r, lo:lo + CONV_ROWS, :]
        out_ref[o:o + CONV_ROWS, :] = acc


def _ab_fwd(z, lga, lba, wsm, bs_col, cwb, cbb, lgb, lbb, name):
    s = z.shape[0]
    ts = _tile(s, 256)
    hb = ts // HALO_B

    def body(z_ref, zh_ref, lga_ref, lba_ref, ws_ref, bs_ref, cw_ref, cb_ref, lgb_ref, lbb_ref,
             y_ref, yb2_ref, rolled):
        i = pl.program_id(0)
        z_t = z_ref[...].astype(F32)
        gu = _gelu(z_t[:, 0:DA])
        gv = _gelu(z_t[:, DA:2 * DA])
        vn, _, _ = _ln_fwd(gv, lga_ref[...], lba_ref[...])
        vnb = vn.astype(BF16)
        for c in range(ts // CHUNK):
            for h in range(HEADS):
                rs = slice(c * CHUNK, (c + 1) * CHUNK)
                cs = slice(h * CHUNK, (h + 1) * CHUNK)
                mixed = jnp.dot(ws_ref[h], vnb[rs, cs], preferred_element_type=F32) + bs_ref[h]
                y_ref[rs, cs] = (gu[rs, cs] * mixed).astype(BF16)
        zh = jnp.where(i > 0, zh_ref[...], jnp.zeros_like(zh_ref[...])).astype(F32)
        xb = jnp.concatenate([zh[:, 0:DA], z_t[:, 2 * DA:3 * DA]], axis=0)
        gb = jnp.concatenate([zh[:, DA:2 * DA], z_t[:, 3 * DA:4 * DA]], axis=0)
        _rolled_copies(rolled, xb * _sigmoid(gb), False)
        _conv31(rolled, cw_ref, ts, yb2_ref, cb_ref[...])
        nb_, _, _ = _ln_fwd(yb2_ref[...], lgb_ref[...], lbb_ref[...])
        y_ref[:, DA:2 * DA] = (nb_ * _sigmoid(nb_)).astype(BF16)

    row = lambda i: (0, 0)
    return pl.pallas_call(
        body, grid=(s // ts,), name=name,
        in_specs=[pl.BlockSpec((ts, 4 * DA), lambda i: (i, 0)),
                  pl.BlockSpec((HALO_B, 2 * DA), lambda i: (jnp.maximum(i * hb - 1, 0), 1)),
                  pl.BlockSpec((1, DA), row), pl.BlockSpec((1, DA), row),
                  pl.BlockSpec((HEADS, CHUNK, CHUNK), lambda i: (0, 0, 0)),
                  pl.BlockSpec((HEADS, CHUNK, 1), lambda i: (0, 0, 0)),
                  pl.BlockSpec((BCONV, DA), row), pl.BlockSpec((1, DA), row),
                  pl.BlockSpec((1, DA), row), pl.BlockSpec((1, DA), row)],
        out_specs=[pl.BlockSpec((ts, 2 * DA), lambda i: (i, 0)), pl.BlockSpec((ts, DA), lambda i: (i, 0))],
        out_shape=[jax.ShapeDtypeStruct((s, 2 * DA), BF16), jax.ShapeDtypeStruct((s, DA), F32)],
        scratch_shapes=[pltpu.VMEM((8, ts + HALO_B, DA), F32)],
        compiler_params=_params("parallel"),
    )(z, z, lga, lba, wsm, bs_col, cwb, cbb, lgb, lbb)


def _c_fwd(zc, cw, name):
    s = zc.shape[0]
    ts = _tile(s, 512)
    hb = ts // HALO

    def body(z_ref, ch_ref, xh_ref, cw_ref, r_ref):
        i = pl.program_id(0)
        z_t = z_ref[...].astype(F32)
        ph = jnp.where(i > 0, ch_ref[...].astype(F32) * xh_ref[...].astype(F32), 0.0)
        pe = jnp.concatenate([ph, z_t[:, D:2 * D] * z_t[:, 2 * D:3 * D]], axis=0)
        q, _ = _conv3(pe, [cw_ref[k:k + 1, :] for k in range(3)], HALO)
        r_ref[...] = (z_t[:, 0:D] * q).astype(BF16)

    halo = lambda col: pl.BlockSpec((HALO, D), lambda i: (jnp.maximum(i * hb - 1, 0), col))
    return pl.pallas_call(
        body, grid=(s // ts,), name=name,
        in_specs=[pl.BlockSpec((ts, 3 * D), lambda i: (i, 0)), halo(1), halo(2),
                  pl.BlockSpec((3, D), lambda i: (0, 0))],
        out_specs=pl.BlockSpec((ts, D), lambda i: (i, 0)),
        out_shape=jax.ShapeDtypeStruct((s, D), BF16),
        compiler_params=_params("parallel"),
    )(zc, zc, zc, cw)


def _final_math(xv, tv, gv):
    r = lax.rsqrt(jnp.mean(xv * xv, axis=-1, keepdims=True) + EPS)
    xhat = xv * r
    e = xhat * gv - tv
    part = 0.5 * jnp.sum(jnp.mean(e * e, axis=-1, keepdims=True), axis=0, keepdims=True)
    dy = e * (1.0 / D)
    dgp = jnp.sum(dy * xhat, axis=0, keepdims=True)
    u = dy * gv
    dx = r * (u - xhat * jnp.mean(u * xhat, axis=-1, keepdims=True))
    return dx, dgp, jnp.broadcast_to(part, (1, 128))


def _ffn_fwd(h, xres, wup, fcw, wdn, gnext, name, final=None):
    s = h.shape[0]
    ts = _tile(s, 512)
    hb = ts // HALO

    def body(h_ref, hh_ref, w_ref, cw_ref, wd_ref, x_ref, *rest):
        if final is not None:
            t_ref, gf_ref, up_ref, upc_ref, dx_ref, dxb_ref, dg_ref, loss_ref, up_s, xo_ref = rest
        elif gnext is not None:
            gn_ref, up_ref, upc_ref, xo_ref, hn_ref, up_s = rest
        else:
            up_ref, upc_ref, xo_ref, up_s = rest
        i = pl.program_id(0)
        m = pl.program_id(1)
        @pl.when(m == 0)
        def _():
            xo_ref[...] = x_ref[...]

        halo = jnp.where(i > 0, hh_ref[...], jnp.zeros_like(hh_ref[...]))
        hx = jnp.concatenate([halo, h_ref[...]], axis=0)
        acts = []
        for gv in range(2):
            up_s[gv] = jnp.dot(hx, w_ref[gv], preferred_element_type=F32)
            x0 = up_s[gv, HALO:HALO + ts, :]
            up_ref[gv] = x0.astype(BF16)
            upc = (cw_ref[gv, 2:3, :] * x0 + cw_ref[gv, 1:2, :] * up_s[gv, HALO - 1:HALO - 1 + ts, :]
                   + cw_ref[gv, 0:1, :] * up_s[gv, HALO - 2:HALO - 2 + ts, :])
            upc_ref[gv] = upc.astype(BF16)
            acts.append(upc)
        a = acts[0] * _sigmoid(acts[0]) * acts[1]
        xo_ref[...] += jnp.dot(a.astype(BF16), wd_ref[...], preferred_element_type=F32)

        if final is not None:
            @pl.when(m == NG - 1)
            def _():
                dx, dgp, part = _final_math(xo_ref[...], t_ref[...], gf_ref[...])
                dx_ref[...] = dx
                dxb_ref[...] = dx.astype(BF16)

                @pl.when(i == 0)
                def _():
                    dg_ref[...] = dgp
                    loss_ref[...] = part

                @pl.when(i > 0)
                def _():
                    dg_ref[...] += dgp
                    loss_ref[...] += part

        elif gnext is not None:
            @pl.when(m == NG - 1)
            def _():
                hn_ref[...] = _rms_math(xo_ref[...], gn_ref[...])

    tile = pl.BlockSpec((ts, D), lambda i, m: (i, 0))
    row = lambda n: pl.BlockSpec((1, n), lambda i, m: (0, 0))
    scratch = [pltpu.VMEM((2, ts + HALO, FB), F32)]
    if final is not None:
        more_in, more_ops = [tile, row(D)], list(final)
        more_out = [tile, tile, row(D), row(128)]
        more_shape = [jax.ShapeDtypeStruct((s, D), F32), jax.ShapeDtypeStruct((s, D), BF16),
                      jax.ShapeDtypeStruct((1, D), F32), jax.ShapeDtypeStruct((1, 128), F32)]
        scratch.append(pltpu.VMEM((ts, D), F32))
    else:
        nxt = gnext is not None
        more_in, more_ops = ([row(D)], [gnext]) if nxt else ([], [])
        more_out = [tile] + ([tile] if nxt else [])
        more_shape = [jax.ShapeDtypeStruct((s, D), F32)] + ([jax.ShapeDtypeStruct((s, D), BF16)] if nxt else [])
    return pl.pallas_call(
        body, grid=(s // ts, NG), name=name,
        in_specs=[tile,
                  pl.BlockSpec((HALO, D), lambda i, m: (jnp.maximum(i * hb - 1, 0), 0)),
                  pl.BlockSpec((2, None, D, FB), lambda i, m: (0, m, 0, 0)),
                  pl.BlockSpec((2, None, 3, FB), lambda i, m: (0, m, 0, 0)),
                  pl.BlockSpec((FB, D), lambda i, m: (m, 0)),
                  tile] + more_in,
        out_specs=[pl.BlockSpec((None, 2, ts, FB), lambda i, m: (m, 0, i, 0)),
                   pl.BlockSpec((None, 2, ts, FB), lambda i, m: (m, 0, i, 0))] + more_out,
        out_shape=[jax.ShapeDtypeStruct((NG, 2, s, FB), BF16), jax.ShapeDtypeStruct((NG, 2, s, FB), BF16)] + more_shape,
        scratch_shapes=scratch,
        compiler_params=_params("arbitrary", "arbitrary"),
    )(h, h, wup, fcw, wdn, xres, *more_ops)


def _ffn_bwd(df, up, upc, wup, fcw, wdn, xin, g, name):
    s = df.shape[0]
    ts = _tile(s, 512)
    nt = s // ts

    def body(df_ref, up_ref, upc_ref, w_ref, cw_ref, wd_ref, x_ref, g_ref,
             a_ref, dup_ref, dx_ref, dxb_ref, dg_ref, dcw_ref, carry, acc, tacc, dcs_ref):
        i = pl.program_id(0)
        m = pl.program_id(1)
        first = i == 0
        @pl.when(first)
        def _():
            carry[m] = jnp.zeros((2, 8, FB), F32)
            dcw_ref[m] = jnp.zeros((2, 3, FB), F32)

        @pl.when(m == 0)
        def _():
            acc[...] = jnp.zeros((ts, D), F32)

        cws = [[cw_ref[gv, k:k + 1, :] for k in range(3)] for gv in range(2)]
        part = ts // NPART
        das = [lax.dot_general(df_ref[p * part:(p + 1) * part, :].astype(BF16), wd_ref[...], NT_DIMS,
                               preferred_element_type=F32) for p in range(NPART)]

        tacc[...] = jnp.zeros((2, 3, 8, FB), F32)
        dcs_ref[:, ts:ts + 8, :] = carry[m]
        for r in reversed(range(ts // RC)):
            rs = slice(r * RC, (r + 1) * RC)
            gate = upc_ref[0, rs, :].astype(F32)
            val = upc_ref[1, rs, :].astype(F32)
            sg = _sigmoid(gate)
            sl = gate * sg
            a_ref[rs, :] = (sl * val).astype(BF16)
            da_c = das[(r * RC) // part][(r * RC) % part:(r * RC) % part + RC]
            dcs = [da_c * val * _dsilu(gate, sg), da_c * sl]
            for gv in range(2):
                dc = dcs[gv]
                dcs_ref[gv, rs, :] = dc
                d1 = dcs_ref[gv, r * RC + 1:(r + 1) * RC + 1, :]
                d2 = dcs_ref[gv, r * RC + 2:(r + 1) * RC + 2, :]
                du = cws[gv][2] * dc + cws[gv][1] * d1 + cws[gv][0] * d2
                dup_ref[gv, rs, :] = du.astype(BF16)
                x0 = up_ref[gv, rs, :].astype(F32)
                for k, dk in enumerate((d2, d1, dc)):
                    p = x0 * dk
                    tacc[gv, k] += sum(p[j:j + 8] for j in range(0, RC, 8))
            if (r * RC) % part == 0:
                ps = slice(r * RC, r * RC + part)
                acc[ps, :] += (
                    lax.dot_general(dup_ref[0, ps, :], w_ref[0], NT_DIMS, preferred_element_type=F32)
                    + lax.dot_general(dup_ref[1, ps, :], w_ref[1], NT_DIMS, preferred_element_type=F32))
        for gv in range(2):
            carry[m, gv] = dcs_ref[gv, 0:8, :]
            for k in range(3):
                dcw_ref[m, gv, k:k + 1, :] += jnp.sum(tacc[gv, k], axis=0, keepdims=True)

        @pl.when(m == NG - 1)
        def _():
            dx, dgp = _rms_bwd_math(acc[...], x_ref[...], g_ref[...])
            dx = df_ref[...] + dx
            dx_ref[...] = dx
            dxb_ref[...] = dx.astype(BF16)

            @pl.when(first)
            def _():
                dg_ref[...] = dgp

            @pl.when(jnp.logical_not(first))
            def _():
                dg_ref[...] += dgp

    rev = lambda i: nt - 1 - i
    return pl.pallas_call(
        body, grid=(nt, NG), name=name,
        in_specs=[pl.BlockSpec((ts, D), lambda i, m: (rev(i), 0)),
                  pl.BlockSpec((None, 2, ts, FB), lambda i, m: (m, 0, rev(i), 0)),
                  pl.BlockSpec((None, 2, ts, FB), lambda i, m: (m, 0, rev(i), 0)),
                  pl.BlockSpec((2, None, D, FB), lambda i, m: (0, m, 0, 0)),
                  pl.BlockSpec((2, None, 3, FB), lambda i, m: (0, m, 0, 0)),
                  pl.BlockSpec((FB, D), lambda i, m: (m, 0)),
                  pl.BlockSpec((ts, D), lambda i, m: (rev(i), 0)),
                  pl.BlockSpec((1, D), lambda i, m: (0, 0))],
        out_specs=[pl.BlockSpec((None, ts, FB), lambda i, m: (m, rev(i), 0)),
                   pl.BlockSpec((None, 2, ts, FB), lambda i, m: (m, 0, rev(i), 0)),
                   pl.BlockSpec((ts, D), lambda i, m: (rev(i), 0)),
                   pl.BlockSpec((ts, D), lambda i, m: (rev(i), 0)),
                   pl.BlockSpec((1, D), lambda i, m: (0, 0)),
                   pl.BlockSpec((NG, 2, 3, FB), lambda i, m: (0, 0, 0, 0))],
        out_shape=[jax.ShapeDtypeStruct((NG, s, FB), BF16), jax.ShapeDtypeStruct((NG, 2, s, FB), BF16),
                   jax.ShapeDtypeStruct((s, D), F32), jax.ShapeDtypeStruct((s, D), BF16),
                   jax.ShapeDtypeStruct((1, D), F32),
                   jax.ShapeDtypeStruct((NG, 2, 3, FB), F32)],
        scratch_shapes=[pltpu.VMEM((NG, 2, 8, FB), F32), pltpu.VMEM((ts, D), F32),
                        pltpu.VMEM((2, 3, 8, FB), F32), pltpu.VMEM((2, ts + 8, FB), F32)],
        compiler_params=_params("arbitrary", "arbitrary"),
    )(df, up, upc, wup, fcw, wdn, xin, g)


def _mm_nt(dy, w, name):
    s = dy.shape[0]
    ts = _tile(s, 1024)

    def body(dy_ref, w_ref, o_ref):
        o_ref[...] = lax.dot_general(dy_ref[...], w_ref[...], NT_DIMS,
                                     preferred_element_type=F32).astype(BF16)

    return pl.pallas_call(
        body, grid=(s // ts,), name=name,
        in_specs=[pl.BlockSpec((ts, D), lambda i: (i, 0)), pl.BlockSpec((D, D), lambda i: (0, 0))],
        out_specs=pl.BlockSpec((ts, D), lambda i: (i, 0)),
        out_shape=jax.ShapeDtypeStruct((s, D), BF16),
        compiler_params=_params("parallel"),
    )(dy, w)


def _mm_nt_rms(dy, wblk, x, g, dres, bf16_copy, name):
    s = dy.shape[0]
    nb, _, bn = wblk.shape
    pair = _pair(bn)
    ts = _tile(s, 512)

    def body(dy_ref, w_ref, x_ref, g_ref, dr_ref, dx_ref, *rest):
        dg_ref = rest[-1]
        i = pl.program_id(0)
        acc = jnp.zeros((ts, D), F32)
        for b in range(0, nb, pair):
            acc = acc + lax.dot_general(dy_ref[:, b * bn:(b + pair) * bn], _cols(w_ref, b, pair), NT_DIMS,
                                        preferred_element_type=F32)
        dx, dgp = _rms_bwd_math(acc, x_ref[...], g_ref[...])
        dx = dr_ref[...] + dx
        dx_ref[...] = dx
        if bf16_copy:
            rest[0][...] = dx.astype(BF16)

        @pl.when(i == 0)
        def _():
            dg_ref[...] = dgp

        @pl.when(i > 0)
        def _():
            dg_ref[...] += dgp

    tile = pl.BlockSpec((ts, D), lambda i: (i, 0))
    return pl.pallas_call(
        body, grid=(s // ts,), name=name,
        in_specs=[pl.BlockSpec((ts, nb * bn), lambda i: (i, 0)), pl.BlockSpec((nb, D, bn), lambda i: (0, 0, 0)),
                  tile, pl.BlockSpec((1, D), lambda i: (0, 0)), tile],
        out_specs=[tile] + ([tile] if bf16_copy else []) + [pl.BlockSpec((1, D), lambda i: (0, 0))],
        out_shape=[jax.ShapeDtypeStruct((s, D), F32)] + ([jax.ShapeDtypeStruct((s, D), BF16)] if bf16_copy else [])
        + [jax.ShapeDtypeStruct((1, D), F32)],
        compiler_params=_params("arbitrary"),
    )(dy, wblk, x, g, dres)


def _c_bwd(dr, zc, cw, name):
    s = dr.shape[0]
    ts = _tile(s, 512)
    nt = s // ts
    hb = ts // HALO

    def body(dr_ref, drf_ref, z_ref, ch_ref, xh_ref, bf_ref, cw_ref, dz_ref, dcw_ref):
        i = pl.program_id(0)
        cwv = [cw_ref[k:k + 1, :] for k in range(3)]
        z_t = z_ref[...].astype(F32)
        bg, cg, xv = z_t[:, 0:D], z_t[:, D:2 * D], z_t[:, 2 * D:3 * D]
        ph = jnp.where(i > 0, ch_ref[...].astype(F32) * xh_ref[...].astype(F32), 0.0)
        pe = jnp.concatenate([ph, cg * xv], axis=0)
        q, taps = _conv3(pe, cwv, HALO)
        drv = dr_ref[...].astype(F32)
        dq = drv * bg
        dqf = jnp.where(i < nt - 1, drf_ref[...].astype(F32) * bf_ref[...].astype(F32), 0.0)
        dp = _conv3_bwd_in(jnp.concatenate([dq, dqf], axis=0), cwv, ts)
        dz_ref[:, 0:D] = (drv * q).astype(BF16)
        dz_ref[:, D:2 * D] = (dp * xv).astype(BF16)
        dz_ref[:, 2 * D:3 * D] = (dp * cg).astype(BF16)
        rows = _conv3_bwd_w(dq, taps)

        @pl.when(i == 0)
        def _():
            for k in range(3):
                dcw_ref[k:k + 1, :] = rows[k]

        @pl.when(i > 0)
        def _():
            for k in range(3):
                dcw_ref[k:k + 1, :] += rows[k]

    past = lambda col: pl.BlockSpec((HALO, D), lambda i: (jnp.maximum(i * hb - 1, 0), col))
    nxt = lambda i: jnp.minimum((i + 1) * hb, s // HALO - 1)
    return pl.pallas_call(
        body, grid=(nt,), name=name,
        in_specs=[pl.BlockSpec((ts, D), lambda i: (i, 0)),
                  pl.BlockSpec((HALO, D), lambda i: (nxt(i), 0)),
                  pl.BlockSpec((ts, 3 * D), lambda i: (i, 0)), past(1), past(2),
                  pl.BlockSpec((HALO, D), lambda i: (nxt(i), 0)),
                  pl.BlockSpec((3, D), lambda i: (0, 0))],
        out_specs=[pl.BlockSpec((ts, 3 * D), lambda i: (i, 0)), pl.BlockSpec((3, D), lambda i: (0, 0))],
        out_shape=[jax.ShapeDtypeStruct((s, 3 * D), BF16), jax.ShapeDtypeStruct((3, D), F32)],
        compiler_params=_params("arbitrary"),
    )(dr, dr, zc, zc, zc, zc, cw)


G512_ROWS = 40


def _ab_bwd(dy, z, yb2, lga, lba, wsm, bs_col, cwb, lgb, lbb, name):
    s = z.shape[0]
    ts = _tile(s, 256)
    nt = s // ts
    hb = ts // HALO_B
    nch = ts // CHUNK

    def body(z_ref, zh_ref, dy_ref, dyf_ref, yb2_ref, yb2f_ref, lga_ref, lba_ref, ws_ref, bs_ref,
             cw_ref, lgb_ref, lbb_ref, dz_ref, g512_ref, dws_ref, dbs_ref, dvn_ref, fwd_rolled, bwd_rolled, du_s):
        i = pl.program_id(0)
        last = i == nt - 1

        @pl.when(i == 0)
        def _():
            g512_ref[...] = jnp.zeros((G512_ROWS, DA), F32)
            dws_ref[...] = jnp.zeros((HEADS, CHUNK, CHUNK), F32)
            dbs_ref[...] = jnp.zeros((HEADS, CHUNK, 1), F32)

        def add_row(k, v):
            g512_ref[k:k + 1, :] += v

        z_t = z_ref[...].astype(F32)
        dy_t = dy_ref[...].astype(F32)
        ua, va = z_t[:, 0:DA], z_t[:, DA:2 * DA]
        gu = _gelu(ua)
        gv = _gelu(va)
        lga_v = lga_ref[...]
        vn, xhat_a, rstd_a = _ln_fwd(gv, lga_v, lba_ref[...])
        vnb = vn.astype(BF16)
        causal = (lax.broadcasted_iota(jnp.int32, (CHUNK, CHUNK), 0)
                  >= lax.broadcasted_iota(jnp.int32, (CHUNK, CHUNK), 1)).astype(F32)
        for c in range(nch):
            for h in range(HEADS):
                rs = slice(c * CHUNK, (c + 1) * CHUNK)
                cs = slice(h * CHUNK, (h + 1) * CHUNK)
                vblk = vnb[rs, cs]
                mixed = jnp.dot(ws_ref[h], vblk, preferred_element_type=F32) + bs_ref[h]
                dyb_ = dy_t[rs, cs]
                dmix = dyb_ * gu[rs, cs]
                dmb = dmix.astype(BF16)
                dz_ref[rs, cs] = (dyb_ * mixed * _dgelu(ua[rs, cs])).astype(BF16)
                dvn_ref[rs, cs] = lax.dot_general(ws_ref[h], dmb, TN_DIMS, preferred_element_type=F32)
                dws_ref[h] += causal * lax.dot_general(dmb, vblk, NT_DIMS, preferred_element_type=F32)
                dbs_ref[h] += jnp.sum(dmix, axis=1, keepdims=True)
        dvn = dvn_ref[...]
        add_row(0, jnp.sum(dvn * xhat_a, axis=0, keepdims=True))
        add_row(1, jnp.sum(dvn, axis=0, keepdims=True))
        dgv = _ln_bwd(dvn, xhat_a, rstd_a, lga_v)
        dz_ref[:, DA:2 * DA] = (dgv * _dgelu(va)).astype(BF16)
        lgb_v = lgb_ref[...]
        dyb_e = jnp.concatenate(
            [dy_t[:, DA:2 * DA], jnp.where(last, 0.0, dyf_ref[...].astype(F32))], axis=0)
        yb2_e = jnp.concatenate([yb2_ref[...], jnp.where(last, 0.0, yb2f_ref[...])], axis=0)
        n_e, xhat_b, rstd_b = _ln_fwd(yb2_e, lgb_v, lbb_ref[...])
        sgn = _sigmoid(n_e)
        dn = dyb_e * _dsilu(n_e, sgn)
        dy2 = _ln_bwd(dn, xhat_b, rstd_b, lgb_v)
        add_row(2, jnp.sum(dy2[:ts], axis=0, keepdims=True))
        add_row(3, jnp.sum(dn[:ts] * xhat_b[:ts], axis=0, keepdims=True))
        add_row(4, jnp.sum(dn[:ts], axis=0, keepdims=True))
        zh = jnp.where(i > 0, zh_ref[...], jnp.zeros_like(zh_ref[...])).astype(F32)
        xb_t, gb_t = z_t[:, 2 * DA:3 * DA], z_t[:, 3 * DA:4 * DA]
        sgb = _sigmoid(gb_t)
        _rolled_copies(fwd_rolled, jnp.concatenate(
            [zh[:, 0:DA] * _sigmoid(zh[:, DA:2 * DA]), xb_t * sgb], axis=0), False)
        _rolled_copies(bwd_rolled, dy2, True)
        for o in range(0, ts, CONV_ROWS):
            acc = jnp.zeros((CONV_ROWS, DA), F32)
            for sh in range(BCONV):
                q, r = divmod(sh, 8)
                acc = acc + cw_ref[BCONV - 1 - sh:BCONV - sh, :] * bwd_rolled[r, 8 * q + o:8 * q + o + CONV_ROWS, :]
            du_s[o:o + CONV_ROWS, :] = acc
        for sh in range(BCONV):
            q, r = divmod(sh, 8)
            acc = jnp.zeros((CONV_ROWS, DA), F32)
            for o in range(0, ts, CONV_ROWS):
                lo = HALO_B - 8 * q + o
                acc = acc + bwd_rolled[0, o:o + CONV_ROWS, :] * fwd_rolled[r, lo:lo + CONV_ROWS, :]
            add_row(8 + BCONV - 1 - sh, jnp.sum(acc, axis=0, keepdims=True))
        du = du_s[...]
        dz_ref[:, 2 * DA:3 * DA] = (du * sgb).astype(BF16)
        dz_ref[:, 3 * DA:4 * DA] = (du * xb_t * sgb * (1.0 - sgb)).astype(BF16)

    row = lambda i: (0, 0)
    nxt = lambda i: jnp.minimum((i + 1) * hb, s // HALO_B - 1)
    return pl.pallas_call(
        body, grid=(nt,), name=name,
        in_specs=[pl.BlockSpec((ts, 4 * DA), lambda i: (i, 0)),
                  pl.BlockSpec((HALO_B, 2 * DA), lambda i: (jnp.maximum(i * hb - 1, 0), 1)),
                  pl.BlockSpec((ts, 2 * DA), lambda i: (i, 0)),
                  pl.BlockSpec((HALO_B, DA), lambda i: (nxt(i), 1)),
                  pl.BlockSpec((ts, DA), lambda i: (i, 0)),
                  pl.BlockSpec((HALO_B, DA), lambda i: (nxt(i), 0)),
                  pl.BlockSpec((1, DA), row), pl.BlockSpec((1, DA), row),
                  pl.BlockSpec((HEADS, CHUNK, CHUNK), lambda i: (0, 0, 0)),
                  pl.BlockSpec((HEADS, CHUNK, 1), lambda i: (0, 0, 0)),
                  pl.BlockSpec((BCONV, DA), row), pl.BlockSpec((1, DA), row), pl.BlockSpec((1, DA), row)],
        out_specs=[pl.BlockSpec((ts, 4 * DA), lambda i: (i, 0)),
                   pl.BlockSpec((G512_ROWS, DA), row),
                   pl.BlockSpec((HEADS, CHUNK, CHUNK), lambda i: (0, 0, 0)),
                   pl.BlockSpec((HEADS, CHUNK, 1), lambda i: (0, 0, 0))],
        out_shape=[jax.ShapeDtypeStruct((s, 4 * DA), BF16), jax.ShapeDtypeStruct((G512_ROWS, DA), F32),
                   jax.ShapeDtypeStruct((HEADS, CHUNK, CHUNK), F32),
                   jax.ShapeDtypeStruct((HEADS, CHUNK, 1), F32)],
        scratch_shapes=[pltpu.VMEM((ts, DA), F32), pltpu.VMEM((8, ts + HALO_B, DA), F32),
                        pltpu.VMEM((8, ts + HALO_B, DA), F32), pltpu.VMEM((ts, DA), F32)],
        compiler_params=_params("arbitrary"),
    )(z, z, dy, dy, yb2, yb2, lga, lba, wsm, bs_col, cwb, lgb, lbb)


def _dw_cols(a, dy, nb, bn, name):
    s = a.shape[0]
    tm = _tile(s, 2048)
    nt = s // tm
    cpb = 4

    def body(a_ref, dy_ref, o_ref, acc):
        t = pl.program_id(1)
        p = lax.dot_general(a_ref[...], dy_ref[...], TN_DIMS, preferred_element_type=F32)

        @pl.when(t == 0)
        def _():
            for q in range(cpb):
                acc[q] = p[:, q * bn:(q + 1) * bn]

        @pl.when(t > 0)
        def _():
            for q in range(cpb):
                acc[q] += p[:, q * bn:(q + 1) * bn]

        @pl.when(t == nt - 1)
        def _():
            o_ref[...] = acc[...].astype(BF16)

    return pl.pallas_call(
        body, grid=(nb // cpb, nt), name=name,
        in_specs=[pl.BlockSpec((tm, D), lambda j, t: (t, 0)), pl.BlockSpec((tm, cpb * bn), lambda j, t: (t, j))],
        out_specs=pl.BlockSpec((cpb, D, bn), lambda j, t: (j, 0, 0)),
        out_shape=jax.ShapeDtypeStruct((nb, D, bn), BF16),
        scratch_shapes=[pltpu.VMEM((cpb, D, bn), F32)],
        compiler_params=_params("arbitrary", "arbitrary"),
    )(a, dy)


def _dw_rows(a, dy, name):
    s = a.shape[0]
    tm = _tile(s, 4096)
    nt = s // tm
    rb = 512

    def body(a_ref, dy_ref, o_ref, acc):
        t = pl.program_id(1)
        p = lax.dot_general(a_ref[...], dy_ref[...], TN_DIMS, preferred_element_type=F32)

        @pl.when(t == 0)
        def _():
            acc[...] = p

        @pl.when(t > 0)
        def _():
            acc[...] += p

        @pl.when(t == nt - 1)
        def _():
            o_ref[...] = acc[...].astype(BF16)

    return pl.pallas_call(
        body, grid=(D // rb, nt), name=name,
        in_specs=[pl.BlockSpec((tm, rb), lambda j, t: (t, j)), pl.BlockSpec((tm, D), lambda j, t: (t, 0))],
        out_specs=pl.BlockSpec((rb, D), lambda j, t: (j, 0)),
        out_shape=jax.ShapeDtypeStruct((D, D), BF16),
        scratch_shapes=[pltpu.VMEM((rb, D), F32)],
        compiler_params=_params("arbitrary", "arbitrary"),
    )(a, dy)


def _dw_up(h, dup, name):
    s = h.shape[0]
    tm = _tile(s, 4096)
    nt = s // tm

    def body(h_ref, d_ref, o_ref, acc):
        t = pl.program_id(1)
        p = lax.dot_general(d_ref[...], h_ref[...], TN_DIMS, preferred_element_type=F32)

        @pl.when(t == 0)
        def _():
            acc[...] = p

        @pl.when(t > 0)
        def _():
            acc[...] += p

        @pl.when(t == nt - 1)
        def _():
            o_ref[...] = acc[...].astype(BF16)

    return pl.pallas_call(
        body, grid=(NDEV, nt), name=name,
        in_specs=[pl.BlockSpec((tm, D), lambda b, t: (t, 0)),
                  pl.BlockSpec((None, None, tm, FB), lambda b, t: (b % NG, b // NG, t, 0))],
        out_specs=pl.BlockSpec((None, FB, D), lambda b, t: (b, 0, 0)),
        out_shape=jax.ShapeDtypeStruct((NDEV, FB, D), BF16),
        scratch_shapes=[pltpu.VMEM((FB, D), F32)],
        compiler_params=_params("arbitrary", "arbitrary"),
    )(h, dup)


def _dw_dn(a, df, name):
    s = df.shape[0]
    tm = _tile(s, 4096)
    nt = s // tm

    def body(a_ref, d_ref, o_ref, acc):
        t = pl.program_id(1)
        p = lax.dot_general(a_ref[...], d_ref[...], TN_DIMS, preferred_element_type=F32)

        @pl.when(t == 0)
        def _():
            acc[...] = p

        @pl.when(t > 0)
        def _():
            acc[...] += p

        @pl.when(t == nt - 1)
        def _():
            o_ref[...] = acc[...].astype(BF16)

    return pl.pallas_call(
        body, grid=(NG, nt), name=name,
        in_specs=[pl.BlockSpec((None, tm, FB), lambda m, t: (m, t, 0)), pl.BlockSpec((tm, D), lambda m, t: (t, 0))],
        out_specs=pl.BlockSpec((FB, D), lambda m, t: (m, 0)),
        out_shape=jax.ShapeDtypeStruct((DFF, D), BF16),
        scratch_shapes=[pltpu.VMEM((FB, D), F32)],
        compiler_params=_params("arbitrary", "arbitrary"),
    )(a, df)


def _place():
    x, y, c = lax.axis_index("x"), lax.axis_index("y"), lax.axis_index("c")
    chips = [(1 - x, y), (x, 1 - y), (1 - x, 1 - y)]
    return x, y, c, chips


def _zone(shard, dev):
    return lax.dynamic_update_slice(lax.empty((NDEV,) + shard.shape, shard.dtype), shard[None],
                                    (dev,) + (0,) * shard.ndim)


HBM_SPEC = pl.BlockSpec(memory_space=pltpu.HBM)
SEM_SPEC = pl.BlockSpec(memory_space=pltpu.SEMAPHORE)
DATAFLOW = pltpu.SideEffectType.DATAFLOW_SIDE_EFFECTING


def _hbm(a):
    return pltpu.with_memory_space_constraint(a, pltpu.HBM)


def _hbm_like(arrs):
    return [pltpu.HBM(a.shape, a.dtype) for a in arrs]


def _ag_start(srcs, lands, after, name):
    n = len(srcs)
    ns = 8 * n

    def body(*refs):
        src, land = refs[:n], refs[n:2 * n]
        sems = refs[2 * n + 1:2 * n + 1 + ns]
        token = refs[-1]
        x, y, c, chips = _place()
        peers = [(x, y, 1 - c)] + [(*chip, c) for chip in chips]
        for t in range(n):
            for k, to in enumerate(peers):
                pltpu.make_async_remote_copy(
                    src_ref=src[t], dst_ref=land[t].at[4 * x + 2 * y + c],
                    send_sem=sems[2 * (4 * t + k)], recv_sem=sems[2 * (4 * t + k) + 1],
                    device_id=to, device_id_type=MESH).start()
        token[...] = jnp.zeros_like(token)

    res = pl.pallas_call(
        body, name=name,
        in_specs=[HBM_SPEC] * (2 * n) + [ANY],
        out_specs=[SEM_SPEC] * ns + [HBM_SPEC] * (2 * n) + [pl.BlockSpec(memory_space=pltpu.VMEM)],
        out_shape=[pltpu.SemaphoreType.DMA(())] * ns + _hbm_like(srcs) + _hbm_like(lands)
        + [jax.ShapeDtypeStruct((8, 128), F32)],
        input_output_aliases={i: ns + i for i in range(2 * n)},
        compiler_params=pltpu.CompilerParams(has_side_effects=DATAFLOW),
    )(*[_hbm(a) for a in srcs], *[_hbm(a) for a in lands], after)
    sems = [[(res[2 * (4 * t + k)], res[2 * (4 * t + k) + 1]) for k in range(4)] for t in range(n)]
    return sems, res[ns:ns + n], res[ns + n:ns + 2 * n], res[-1]


def _ag_forward(srcs, lands, sems1, after, name):
    n = len(srcs)
    flat1 = [s for t in range(n) for k in range(1, 4) for s in sems1[t][k]]
    n1 = len(flat1)

    def body(*refs):
        src, land = refs[:n], refs[n:2 * n]
        s1 = refs[2 * n:2 * n + n1]
        s2 = refs[2 * n + n1 + 1:2 * n + n1 + 1 + 6 * n]
        x, y, c, chips = _place()
        for j, (cx, cy) in enumerate(chips):
            for t in range(n):
                blk = land[t].at[4 * cx + 2 * cy + c]
                pltpu.make_async_remote_copy(
                    src_ref=src[t], dst_ref=blk, send_sem=s1[2 * (3 * t + j)], recv_sem=s1[2 * (3 * t + j) + 1],
                    device_id=(cx, cy, c), device_id_type=MESH).wait_recv()
                pltpu.make_async_remote_copy(
                    src_ref=blk, dst_ref=blk, send_sem=s2[2 * (3 * t + j)], recv_sem=s2[2 * (3 * t + j) + 1],
                    device_id=(x, y, 1 - c), device_id_type=MESH).start()

    res = pl.pallas_call(
        body, name=name,
        in_specs=[HBM_SPEC] * (2 * n) + [SEM_SPEC] * n1 + [ANY],
        out_specs=[SEM_SPEC] * (6 * n) + [HBM_SPEC] * n,
        out_shape=[pltpu.SemaphoreType.DMA(())] * (6 * n) + _hbm_like(lands),
        input_output_aliases={n + i: 6 * n + i for i in range(n)},
        compiler_params=pltpu.CompilerParams(has_side_effects=DATAFLOW),
    )(*srcs, *lands, *flat1, after)
    sems2 = [[(res[2 * (3 * t + j)], res[2 * (3 * t + j) + 1]) for j in range(3)] for t in range(n)]
    return sems2, res[6 * n:]


def _ag_finish(srcs, lands, sems1, sems2, after, name):
    n = len(srcs)
    flat1 = [s for t in range(n) for k in range(4) for s in sems1[t][k]]
    flat2 = [s for t in range(n) for j in range(3) for s in sems2[t][j]]
    n1, n2 = len(flat1), len(flat2)

    def body(*refs):
        src, land = refs[:n], refs[n:2 * n]
        s1 = refs[2 * n:2 * n + n1]
        s2 = refs[2 * n + n1:2 * n + n1 + n2]
        x, y, c, chips = _place()
        sib = (x, y, 1 - c)
        for t in range(n):
            own = land[t].at[4 * x + 2 * y + 1 - c]
            pltpu.make_async_remote_copy(
                src_ref=src[t], dst_ref=own, send_sem=s1[8 * t], recv_sem=s1[8 * t + 1],
                device_id=sib, device_id_type=MESH).wait_recv()
            for k in range(4):
                pltpu.make_async_remote_copy(
                    src_ref=src[t], dst_ref=own, send_sem=s1[2 * (4 * t + k)], recv_sem=s1[2 * (4 * t + k) + 1],
                    device_id=sib, device_id_type=MESH).wait_send()
            for j, (cx, cy) in enumerate(chips):
                blk = land[t].at[4 * cx + 2 * cy + 1 - c]
                cp = pltpu.make_async_remote_copy(
                    src_ref=blk, dst_ref=blk, send_sem=s2[2 * (3 * t + j)], recv_sem=s2[2 * (3 * t + j) + 1],
                    device_id=sib, device_id_type=MESH)
                cp.wait_send()
                cp.wait_recv()

    return pl.pallas_call(
        body, name=name,
        in_specs=[HBM_SPEC] * (2 * n) + [SEM_SPEC] * (n1 + n2) + [ANY],
        out_specs=[HBM_SPEC] * n,
        out_shape=_hbm_like(lands),
        input_output_aliases={n + i: i for i in range(n)},
        compiler_params=pltpu.CompilerParams(has_side_effects=DATAFLOW),
    )(*srcs, *lands, *flat1, *flat2, after)


def _pair_copies(srcs, dsts, sems):
    x, y, c, _ = _place()
    nt = len(srcs)
    return [pltpu.make_async_remote_copy(
        src_ref=srcs[t].at[2 * j + 1 - c], dst_ref=dsts[t].at[j],
        send_sem=sems[2 * (NCHIP * t + j)], recv_sem=sems[2 * (NCHIP * t + j) + 1],
        device_id=(x, y, 1 - c), device_id_type=MESH) for t in range(nt) for j in range(NCHIP)]


def _pair_start(grads, carry, name):
    nt = len(grads)
    ns = 2 * NCHIP * nt
    zones = [_hbm(lax.empty((NCHIP,) + a.shape[1:], a.dtype)) for a in grads]
    extra = [] if carry is None else [_hbm(carry)]
    ne = len(extra)

    def body(*refs):
        for cp in _pair_copies(refs[:nt], refs[nt:2 * nt], refs[2 * nt + ne:2 * nt + ne + ns]):
            cp.start()

    res = pl.pallas_call(
        body, name=name,
        in_specs=[HBM_SPEC] * (2 * nt + ne),
        out_specs=[SEM_SPEC] * ns + [HBM_SPEC] * (2 * nt + ne),
        out_shape=[pltpu.SemaphoreType.DMA(())] * ns + _hbm_like(grads) + _hbm_like(zones) + _hbm_like(extra),
        input_output_aliases={i: ns + i for i in range(2 * nt + ne)},
        compiler_params=pltpu.CompilerParams(has_side_effects=DATAFLOW),
    )(*[_hbm(a) for a in grads], *zones, *extra)
    handle = (list(res[:ns]), list(res[ns:ns + nt]), list(res[ns + nt:ns + 2 * nt]))
    return handle, (res[ns + 2 * nt] if ne else None)


def _pair_wait(handle, after, name):
    sems, srcs, zones = handle
    nt, ns = len(srcs), len(sems)

    def body(*refs):
        for cp in _pair_copies(refs[:nt], refs[nt:2 * nt], refs[2 * nt:2 * nt + ns]):
            cp.wait_send()
            cp.wait_recv()

    return pl.pallas_call(
        body, name=name,
        in_specs=[HBM_SPEC] * (2 * nt) + [SEM_SPEC] * ns + [ANY],
        out_specs=[HBM_SPEC] * nt,
        out_shape=_hbm_like(zones),
        input_output_aliases={nt + i: i for i in range(nt)},
        compiler_params=pltpu.CompilerParams(has_side_effects=DATAFLOW),
    )(*srcs, *zones, *sems, after)


def _rows_tile(r, row_bytes, cap_bytes):
    best = None
    for tr in range(16, r + 1, 16):
        if r % tr == 0 and tr * row_bytes <= cap_bytes:
            best = tr
    return best if best is not None else r


def _pair_sum(own, got, cidx, name):
    _, _, r, cdim = own.shape
    tr = _rows_tile(r, 2 * cdim, 2 * 1024 * 1024)

    def body(c_ref, a_ref, b_ref, o_ref):
        o_ref[...] = (a_ref[...].astype(F32) + b_ref[...].astype(F32)).astype(BF16)

    return pl.pallas_call(
        body, name=name,
        grid_spec=pltpu.PrefetchScalarGridSpec(
            num_scalar_prefetch=1, grid=(NCHIP, r // tr),
            in_specs=[pl.BlockSpec((None, None, tr, cdim), lambda j, i, c_ref: (j, c_ref[0], i, 0)),
                      pl.BlockSpec((None, tr, cdim), lambda j, i, c_ref: (j, i, 0))],
            out_specs=pl.BlockSpec((None, tr, cdim), lambda j, i, c_ref: (j, i, 0))),
        out_shape=jax.ShapeDtypeStruct((NCHIP, r, cdim), BF16),
        compiler_params=_params("arbitrary", "arbitrary"),
    )(cidx, own, got)


def _chip_copies(srcs, zones, slots, sems):
    x, y, c, chips = _place()
    out = []
    for t, (z, l) in enumerate(slots):
        for k, (cx, cy) in enumerate(chips):
            dst = zones[z].at[k] if l is None else zones[z].at[k, l]
            out.append(pltpu.make_async_remote_copy(
                src_ref=srcs[t].at[2 * cx + cy], dst_ref=dst,
                send_sem=sems[2 * (3 * t + k)], recv_sem=sems[2 * (3 * t + k) + 1],
                device_id=(cx, cy, c), device_id_type=MESH))
    return out


def _chip_start(sums, zones, slots, carry, name):
    nt, nz = len(sums), len(zones)
    ns = 6 * nt
    extra = [] if carry is None else [_hbm(carry)]
    ne = len(extra)

    def body(*refs):
        for cp in _chip_copies(refs[:nt], refs[nt:nt + nz], slots, refs[nt + nz + ne:nt + nz + ne + ns]):
            cp.start()

    res = pl.pallas_call(
        body, name=name,
        in_specs=[HBM_SPEC] * (nt + nz + ne),
        out_specs=[SEM_SPEC] * ns + [HBM_SPEC] * (nt + nz + ne),
        out_shape=[pltpu.SemaphoreType.DMA(())] * ns + _hbm_like(sums) + _hbm_like(zones) + _hbm_like(extra),
        input_output_aliases={i: ns + i for i in range(nt + nz + ne)},
        compiler_params=pltpu.CompilerParams(has_side_effects=DATAFLOW),
    )(*[_hbm(a) for a in sums], *zones, *extra)
    return (list(res[:ns]), list(res[ns:ns + nt]), list(res[ns + nt:ns + nt + nz]),
            (res[ns + nt + nz] if ne else None))


def _chip_wait(started, zones, zone_ids, after, name):
    started = [(sums, [(zone_ids.index(z), l) for z, l in slots], sems) for sums, slots, sems in started]
    nz = len(zones)
    flat_src = [a for sums, _, _ in started for a in sums]
    flat_sem = [s for _, _, sems in started for s in sems]
    n_src, n_sem = len(flat_src), len(flat_sem)

    def body(*refs):
        srcs, zs, sems = refs[:n_src], refs[n_src:n_src + nz], refs[n_src + nz:n_src + nz + n_sem]
        so, se = 0, 0
        for sums, slots, sem_list in started:
            for cp in _chip_copies(srcs[so:so + len(sums)], zs, slots, sems[se:se + len(sem_list)]):
                cp.wait_send()
                cp.wait_recv()
            so += len(sums)
            se += len(sem_list)

    return pl.pallas_call(
        body, name=name,
        in_specs=[HBM_SPEC] * (n_src + nz) + [SEM_SPEC] * n_sem + [ANY],
        out_specs=[HBM_SPEC] * nz,
        out_shape=_hbm_like(zones),
        input_output_aliases={n_src + i: i for i in range(nz)},
        compiler_params=pltpu.CompilerParams(has_side_effects=DATAFLOW),
    )(*flat_src, *zones, *flat_sem, after)


def _small_allreduce(parts, y_first, after, name):
    nt = len(parts)

    def body(*refs):
        srcs, outs, bufs = refs[:nt], refs[nt + 1:2 * nt + 1], refs[2 * nt + 1:3 * nt + 1]
        send_sems, recv_sems = refs[3 * nt + 1:]
        x, y, c, _ = _place()
        along = {"c": (x, y, 1 - c), "x": (1 - x, y, c), "y": (x, 1 - y, c)}
        for t in range(nt):
            outs[t][...] = srcs[t][...]
        for step in range(3):
            order = [("c", "y", "x") if t in y_first else ("c", "x", "y") for t in range(nt)]
            copies = [pltpu.make_async_remote_copy(
                src_ref=outs[t], dst_ref=bufs[t].at[step],
                send_sem=send_sems.at[step, t], recv_sem=recv_sems.at[step, t],
                device_id=along[order[t][step]], device_id_type=MESH) for t in range(nt)]
            for cp in copies:
                cp.start()
            for cp in copies:
                cp.wait()
            for t in range(nt):
                outs[t][...] = outs[t][...] + bufs[t][step]

    vm = pl.BlockSpec(memory_space=pltpu.VMEM)
    return pl.pallas_call(
        body, name=name,
        in_specs=[vm] * nt + [ANY], out_specs=[vm] * nt,
        out_shape=[jax.ShapeDtypeStruct(a.shape, F32) for a in parts],
        scratch_shapes=[pltpu.VMEM((3,) + a.shape, F32) for a in parts]
        + [pltpu.SemaphoreType.DMA((3, nt)), pltpu.SemaphoreType.DMA((3, nt))],
        compiler_params=pltpu.CompilerParams(has_side_effects=True, vmem_limit_bytes=VMEM_LIMIT),
    )(*parts, after)


def _adam_math(w, g, m, v):
    m2 = ADAM_B1 * m + (1.0 - ADAM_B1) * g
    v2 = ADAM_B2 * v + (1.0 - ADAM_B2) * (g * g)
    m_hat = m2 / (1.0 - ADAM_B1 ** ADAM_STEP)
    v_hat = v2 / (1.0 - ADAM_B2 ** ADAM_STEP)
    delta = -ADAM_LR * (m_hat / (jnp.sqrt(v_hat) + ADAM_EPS) + ADAM_WD * w)
    return delta, m2, v2


def _adam_big(w, m, v, parts, mine, chip, name):
    nl, r, cdim = w.shape
    tr = _rows_tile(r, 4 * cdim, 3 * 512 * 1024)

    def body(c_ref, w_ref, m_ref, v_ref, p_ref, *rest):
        mine_refs, (g_ref, d_ref, mo_ref, vo_ref) = rest[:nl], rest[nl:]
        own = mine_refs[0][...]
        for l in range(1, nl):
            own = jnp.where(pl.program_id(0) == l, mine_refs[l][...], own)
        g = ((p_ref[0].astype(F32) + p_ref[1].astype(F32)) + p_ref[2].astype(F32)) + own.astype(F32)
        delta, m2, v2 = _adam_math(w_ref[...], g, m_ref[...], v_ref[...])
        g_ref[...] = g
        d_ref[...] = delta
        mo_ref[...] = m2
        vo_ref[...] = v2

    spec = pl.BlockSpec((None, tr, cdim), lambda l, i, c_ref: (l, i, 0))
    mine_specs = [pl.BlockSpec((None, tr, cdim), lambda l, i, c_ref, ll=ll: (c_ref[0], jnp.where(l == ll, i, 0), 0))
                  for ll in range(nl)]
    return pl.pallas_call(
        body, name=name,
        grid_spec=pltpu.PrefetchScalarGridSpec(
            num_scalar_prefetch=1, grid=(nl, r // tr),
            in_specs=[spec, spec, spec, pl.BlockSpec((3, None, tr, cdim), lambda l, i, c_ref: (0, l, i, 0))]
            + mine_specs,
            out_specs=[spec] * 4),
        out_shape=[jax.ShapeDtypeStruct(w.shape, F32)] * 4,
        compiler_params=_params("arbitrary", "arbitrary"),
    )(chip, w, m, v, parts, *mine)


def _adam_small(ws, gs, ms, vs, name):
    n = len(ws)

    def body(*refs):
        w_r, g_r, m_r, v_r = refs[:n], refs[n:2 * n], refs[2 * n:3 * n], refs[3 * n:4 * n]
        d_o, m_o, v_o = refs[4 * n:5 * n], refs[5 * n:6 * n], refs[6 * n:7 * n]
        for t in range(n):
            delta, m2, v2 = _adam_math(w_r[t][...], g_r[t][...], m_r[t][...], v_r[t][...])
            d_o[t][...] = delta
            m_o[t][...] = m2
            v_o[t][...] = v2

    vm = pl.BlockSpec(memory_space=pltpu.VMEM)
    shapes = [jax.ShapeDtypeStruct(a.shape, F32) for a in ws]
    return pl.pallas_call(
        body, name=name, in_specs=[vm] * (4 * n), out_specs=[vm] * (3 * n), out_shape=shapes * 3,
        compiler_params=pltpu.CompilerParams(vmem_limit_bytes=VMEM_LIMIT),
    )(*ws, *gs, *ms, *vs)


def kernel(x, norm_mix, norm_ffn, norm_final, ab_w_in, a_ln_g, a_ln_b, a_w_s, a_b_s, b_conv_w, b_conv_b, b_ln_g, b_ln_b, ab_w_out, c_w_in, c_conv_w, c_w_out, f_w_up, f_conv_w, f_w_down, loss_target, m_norm_mix, m_norm_ffn, m_norm_final, m_ab_w_in, m_a_ln_g, m_a_ln_b, m_a_w_s, m_a_b_s, m_b_conv_w, m_b_conv_b, m_b_ln_g, m_b_ln_b, m_ab_w_out, m_c_w_in, m_c_conv_w, m_c_w_out, m_f_w_up, m_f_conv_w, m_f_w_down, v_norm_mix, v_norm_ffn, v_norm_final, v_ab_w_in, v_a_ln_g, v_a_ln_b, v_a_w_s, v_a_b_s, v_b_conv_w, v_b_conv_b, v_b_ln_g, v_b_ln_b, v_ab_w_out, v_c_w_in, v_c_conv_w, v_c_w_out, v_f_w_up, v_f_conv_w, v_f_w_down):
    s = x.shape[1]
    x0 = x.reshape(s, D)
    tgt = loss_target.reshape(s, D)
    xi, yi, ci = lax.axis_index("x"), lax.axis_index("y"), lax.axis_index("c")
    dev = 4 * xi + 2 * yi + ci
    cidx = ci.astype(jnp.int32).reshape(1)

    bf = lambda a: a.astype(BF16)
    slab_w = 6 * CHUNK
    pad = lambda a, rows: jnp.pad(a, ((0, rows - a.shape[0]), (0, slab_w - a.shape[1])))
    slab = jnp.concatenate([pad(b_conv_w[0], 32), pad(c_conv_w[0], 8), pad(f_conv_w.reshape(6, FB), 8)], axis=0)
    later = [bf(ab_w_in[0]), bf(ab_w_out[0]), slab, bf(f_w_up[0]), bf(f_w_down[0]), bf(c_w_in[0]), bf(c_w_out[0]),
             bf(f_w_up[1]), bf(f_w_down[1])]
    lands = [_zone(a, dev) for a in later]
    groups = [[0], [1, 2], [3, 4], [5, 6], [7, 8]]
    ag_sems, later, lands, ag_token = _ag_start(later, lands, x0, "ag_start")

    causal = jnp.tril(jnp.ones((CHUNK, CHUNK), F32))
    wsm = (a_w_s[0] * causal).astype(BF16)
    bs_col = a_b_s.reshape(HEADS, CHUNK, 1)
    nm = [norm_mix[0:1], norm_mix[1:2]]
    nf = [norm_ffn[0:1], norm_ffn[1:2]]
    nfin = norm_final.reshape(1, D)

    def arrive(g, after_ici, after_d2d, tag):
        srcs = [later[t] for t in groups[g]]
        zone = [lands[t] for t in groups[g]]
        sems1 = [ag_sems[t] for t in groups[g]]
        sems2, zone = _ag_forward(srcs, zone, sems1, after_ici, "ag_forward_" + tag)
        return _ag_finish(srcs, zone, sems1, sems2, after_d2d, "ag_finish_" + tag)

    h0 = _rms_fwd(x0, nm[0], "rms_mix0", after=ag_token)
    (win0,) = arrive(0, h0, h0, "w_in")
    z = _mm_in(h0, win0, "mm_ab_in")
    wout0, slab_g = arrive(1, z, z, "first")
    wout0 = wout0.reshape(D, D)
    bcw = jnp.transpose(slab_g[:, 0:BCONV, 0:DA // NDEV], (1, 0, 2)).reshape(BCONV, DA)
    ccw = jnp.transpose(slab_g[:, 32:35, 0:D // NDEV], (1, 0, 2)).reshape(3, D)
    fcw_g = slab_g[:, 40:46, 0:FB].reshape(2, NG, 2, 3, FB)
    fcws = [fcw_g[:, :, 0], fcw_g[:, :, 1]]
    ycat, yb2 = _ab_fwd(z, a_ln_g, a_ln_b, wsm, bs_col, bcw, b_conv_b, b_ln_g, b_ln_b, "ab_fwd")
    x1, h1 = _mm_out(ycat, wout0, x0, nf[0], "mm_ab_out")
    wup0, wdn0 = arrive(2, x1, x1, "ffn0")
    up0, upc0, x2, h2 = _ffn_fwd(h1, x1, wup0.reshape(2, NG, D, FB), fcws[0], wdn0.reshape(DFF, D), nm[1],
                                 "ffn_fwd0")
    cin, cout = arrive(3, x2, x2, "c")
    cout = cout.reshape(D, D)
    zc = _mm_in(h2, cin, "mm_c_in")
    rc = _c_fwd(zc, ccw, "c_fwd")
    x3, h3 = _mm_out(rc, cout, x2, nf[1], "mm_c_out")
    wup1, wdn1 = arrive(4, rc, x3, "ffn1")
    wups = [wup0.reshape(2, NG, D, FB), wup1.reshape(2, NG, D, FB)]
    wdns = [wdn0.reshape(DFF, D), wdn1.reshape(DFF, D)]
    up1, upc1, dx4, dx4b, dnfin, loss_part = _ffn_fwd(h3, x3, wups[1], fcws[1], wdns[1], None, "ffn_fwd1_loss",
                                                      final=(tgt, nfin))

    zshape = lambda *sh: _hbm(lax.empty((3,) + sh, BF16))
    zones = [zshape(D, 2 * D // NDEV), zshape(D // NDEV, D), zshape(D, 3 * D // NDEV), zshape(D // NDEV, D),
             zshape(2, FB, D), zshape(2, DFF // NDEV, D)]
    started = []

    def pair_sums(grads, handle, after, tag):
        del grads
        got = _pair_wait(handle, after, "rs_pair_wait_" + tag)
        return [_pair_sum(b.reshape((NCHIP, 2) + b.shape[1:]), g, cidx, "rs_pair_sum_%s%d" % (tag, t))
                for t, (b, g) in enumerate(zip(handle[1], got))]

    def chip_start(sums, slots, carry, tag):
        sems, sums, new_zones, carry = _chip_start(sums, zones, slots, carry, "rs_chip_start_" + tag)
        zones[:] = new_zones
        started.append((sums, slots, sems))
        return sums, carry

    rows8 = lambda g, r: g.reshape(NDEV, r, D)
    a1, dup1, dx3, dx3b, dnf1, dfcw1 = _ffn_bwd(dx4, up1, upc1, wups[1], fcws[1], wdns[1], x3, nf[1], "ffn_bwd1")
    g_f1 = [_dw_up(h3, dup1, "dw_up1"), rows8(_dw_dn(a1, dx4b, "dw_dn1"), DFF // NDEV)]
    hd_f1, dx3b = _pair_start(g_f1, dx3b, "rs_pair_start_f1")
    drc = _mm_nt(dx3b, cout, "mm_c_out_bwd")
    g_cout = rows8(_dw_rows(rc, dx3b, "dw_c_out"), D // NDEV)
    s_f1 = pair_sums(g_f1, hd_f1, g_cout, "f1")
    s_f1, drc = chip_start(s_f1, [(4, 1), (5, 1)], drc, "f1")
    dzc, dccw = _c_bwd(drc, zc, ccw, "c_bwd")
    dx2, dx2b, dnm1 = _mm_nt_rms(dzc, cin, x2, nm[1], dx3, True, "mm_c_in_bwd")
    g_c = [_dw_cols(h2, dzc, NDEV, 3 * D // NDEV, "dw_c_in"), g_cout]
    hd_c, dx2 = _pair_start(g_c, dx2, "rs_pair_start_c")
    a0, dup0, dx1, dx1b, dnf0, dfcw0 = _ffn_bwd(dx2, up0, upc0, wups[0], fcws[0], wdns[0], x1, nf[0], "ffn_bwd0")
    s_c = pair_sums(g_c, hd_c, dx1b, "c")
    s_c, dx1b = chip_start(s_c, [(2, None), (3, None)], dx1b, "c")
    g_f0 = [_dw_up(h1, dup0, "dw_up0"), rows8(_dw_dn(a0, dx2b, "dw_dn0"), DFF // NDEV)]
    hd_f0, dx1b = _pair_start(g_f0, dx1b, "rs_pair_start_f0")
    dycat = _mm_nt(dx1b, wout0, "mm_ab_out_bwd")
    g_wout0 = rows8(_dw_rows(ycat, dx1b, "dw_ab_out"), D // NDEV)
    s_f0 = pair_sums(g_f0, hd_f0, g_wout0, "f0")
    s_f0, dycat = chip_start(s_f0, [(4, 0), (5, 0)], dycat, "f0")
    dz, g512, dws, dbs = _ab_bwd(dycat, z, yb2, a_ln_g, a_ln_b, wsm, bs_col, bcw, b_ln_g, b_ln_b, "ab_bwd")
    grad_x, dnm0 = _mm_nt_rms(dz, win0, x0, nm[0], dx1, False, "mm_ab_in_bwd")
    g_ab = [_dw_cols(h0, dz, NDEV, 2 * D // NDEV, "dw_ab_in"), g_wout0]
    hd_ab, _ = _pair_start(g_ab, None, "rs_pair_start_ab")

    g1024 = jnp.concatenate([dnm0, dnm1, dnf0, dnf1, dnfin, dccw], axis=0)
    gfc = jnp.concatenate([dfcw0, dfcw1], axis=0).reshape(2 * NG * 2 * 3, FB)
    g1024, g512, dws, dbs, gfc, loss_sum = _small_allreduce(
        [g1024, g512, dws.reshape(HEADS * CHUNK, CHUNK), dbs.reshape(HEADS, CHUNK), gfc, loss_part], (2,),
        hd_ab[1][0], "small_allreduce")
    loss = loss_sum[0, 0]
    s_ab = pair_sums(g_ab, hd_ab, g1024, "ab")
    s_ab, _ = chip_start(s_ab, [(0, None), (1, None)], None, "ab")
    p_cin, p_cout, p_wup, p_wdn = _chip_wait(started[:3], zones[2:], [2, 3, 4, 5], s_ab[0], "rs_chip_wait_early")

    chip = (2 * xi + yi).astype(jnp.int32).reshape(1)

    def big_update(w, m, v, parts, mine, name):
        shp = w.shape
        w3, m3, v3 = (a.reshape((-1,) + shp[-2:]) for a in (w, m, v))
        p4 = parts.reshape((3,) + w3.shape)
        return [o.reshape(shp) for o in _adam_big(w3, m3, v3, p4, mine, chip, name)]

    u_cin = big_update(c_w_in, m_c_w_in, v_c_w_in, p_cin, [s_c[0]], "adam_c_w_in")
    u_cout = big_update(c_w_out, m_c_w_out, v_c_w_out, p_cout, [s_c[1]], "adam_c_w_out")
    tr_ = lambda a: jnp.swapaxes(a, 1, 2)
    u_wup = [tr_(o) for o in big_update(tr_(f_w_up), tr_(m_f_w_up), tr_(v_f_w_up), p_wup,
                                        [s_f0[0], s_f1[0]], "adam_f_w_up")]
    u_wdn = big_update(f_w_down, m_f_w_down, v_f_w_down, p_wdn, [s_f0[1], s_f1[1]], "adam_f_w_down")
    p_win0, p_wout0 = _chip_wait(started[3:], zones[:2], [0, 1], u_wdn[0], "rs_chip_wait_late")
    u_win0 = big_update(ab_w_in, m_ab_w_in, v_ab_w_in, p_win0, [s_ab[0]], "adam_ab_w_in")
    u_wout0 = big_update(ab_w_out, m_ab_w_out, v_ab_w_out, p_wout0, [s_ab[1]], "adam_ab_w_out")

    g_norm_mix = g1024[0:2]
    g_norm_ffn = g1024[2:4]
    g_norm_final = g1024[4:5]
    g_ccw = lax.dynamic_slice(g1024[5:8], (0, dev * (D // NDEV)), (3, D // NDEV))
    g_bcw = lax.dynamic_slice(g512[8:8 + BCONV], (0, dev * (DA // NDEV)), (BCONV, DA // NDEV))
    gfc = gfc.reshape(2, NG, 2, 3, FB)
    g_fcw = lax.dynamic_slice(gfc, (0, dev % NG, dev // NG, 0, 0), (2, 1, 1, 3, FB)).reshape(2, 3, FB)
    small_w = [norm_mix, norm_ffn, nfin, a_ln_g, a_ln_b, a_w_s[0], a_b_s[0], b_conv_w[0], b_conv_b,
               b_ln_g, b_ln_b, c_conv_w[0], f_conv_w]
    small_g = [g_norm_mix, g_norm_ffn, g_norm_final, g512[0:1], g512[1:2],
               dws.reshape(HEADS, CHUNK, CHUNK), dbs, g_bcw, g512[2:3],
               g512[3:4], g512[4:5], g_ccw, g_fcw]
    small_m = [m_norm_mix, m_norm_ffn, m_norm_final.reshape(1, D), m_a_ln_g, m_a_ln_b, m_a_w_s[0], m_a_b_s[0],
               m_b_conv_w[0], m_b_conv_b, m_b_ln_g, m_b_ln_b, m_c_conv_w[0], m_f_conv_w]
    small_v = [v_norm_mix, v_norm_ffn, v_norm_final.reshape(1, D), v_a_ln_g, v_a_ln_b, v_a_w_s[0], v_a_b_s[0],
               v_b_conv_w[0], v_b_conv_b, v_b_ln_g, v_b_ln_b, v_c_conv_w[0], v_f_conv_w]
    upd = _adam_small(small_w, small_g, small_m, small_v, "adam_small")
    ns = len(small_w)
    orig = [norm_mix, norm_ffn, norm_final, a_ln_g, a_ln_b, a_w_s, a_b_s, b_conv_w, b_conv_b,
            b_ln_g, b_ln_b, c_conv_w, f_conv_w]
    sg_out = [g.reshape(o.shape) for g, o in zip(small_g, orig)]
    sd_out = [a.reshape(o.shape) for a, o in zip(upd[0:ns], orig)]
    sm_out = [a.reshape(o.shape) for a, o in zip(upd[ns:2 * ns], orig)]
    sv_out = [a.reshape(o.shape) for a, o in zip(upd[2 * ns:3 * ns], orig)]

    def assemble(small, k):
        return [small[0], small[1], small[2], u_win0[k], small[3], small[4], small[5], small[6], small[7],
                small[8], small[9], small[10], u_wout0[k], u_cin[k], small[11], u_cout[k], u_wup[k],
                small[12], u_wdn[k]]

    grads = assemble(sg_out, 0)
    deltas = assemble(sd_out, 1)
    new_m = assemble(sm_out, 2)
    new_v = assemble(sv_out, 3)
    return (loss, grad_x.reshape(1, s, D), *grads, *deltas, *new_m, *new_v)
```

```python
import math

import jax
import jax.numpy as jnp
from jax import lax
from jax.experimental import pallas as pl
from jax.experimental.pallas import tpu as pltpu

F32 = jnp.float32
BF16 = jnp.bfloat16

D = 1024
DA = 512
HEADS = 4
CHUNK = 128
DFF = 2816
NDEV = 8
NCHIP = 4
FB = DFF * 2 // NDEV
NG = DFF // FB
BCONV = 31
EPS = 1e-6
HALO = 16
HALO_B = 32
RC = 32
NPART = 2
VMEM_LIMIT = 52 * 1024 * 1024
INV_SQRT2 = 1.0 / math.sqrt(2.0)
INV_SQRT_2PI = 1.0 / math.sqrt(2.0 * math.pi)

ADAM_LR = 0.001
ADAM_B1 = 0.9
ADAM_B2 = 0.999
ADAM_EPS = 1e-08
ADAM_WD = 0.01
ADAM_STEP = 10

MESH = pl.DeviceIdType.MESH
ANY = pl.BlockSpec(memory_space=pl.ANY)
NT_DIMS = (((1,), (1,)), ((), ()))
TN_DIMS = (((0,), (0,)), ((), ()))


def _params(*sem):
    return pltpu.CompilerParams(dimension_semantics=sem, vmem_limit_bytes=VMEM_LIMIT)


def _tile(s, want):
    return min(want, s)


def _sigmoid(x):
    return jax.nn.sigmoid(x)


def _dsilu(x, sg):
    return sg * (1.0 + x * (1.0 - sg))


def _gelu(x):
    return 0.5 * x * (1.0 + lax.erf(x * INV_SQRT2))


def _dgelu(x):
    return 0.5 * (1.0 + lax.erf(x * INV_SQRT2)) + x * jnp.exp(-0.5 * x * x) * INV_SQRT_2PI


def _ln_fwd(x, g, b):
    mu = jnp.mean(x, axis=-1, keepdims=True)
    xc = x - mu
    var = jnp.mean(xc * xc, axis=-1, keepdims=True)
    rstd = lax.rsqrt(var + EPS)
    xhat = xc * rstd
    return xhat * g + b, xhat, rstd


def _ln_bwd(dy, xhat, rstd, g):
    dxh = dy * g
    m1 = jnp.mean(dxh, axis=-1, keepdims=True)
    m2 = jnp.mean(dxh * xhat, axis=-1, keepdims=True)
    return rstd * (dxh - m1 - xhat * m2)


def _rms_bwd_math(dh, x, g):
    r = lax.rsqrt(jnp.mean(x * x, axis=-1, keepdims=True) + EPS)
    xhat = x * r
    dg = jnp.sum(dh * xhat, axis=0, keepdims=True)
    u = dh * g
    dx = r * (u - xhat * jnp.mean(u * xhat, axis=-1, keepdims=True))
    return dx, dg


def _conv3(xe, cw, halo):
    x0 = xe[halo:]
    x1 = pltpu.roll(xe, 1, 0)[halo:]
    x2 = pltpu.roll(xe, 2, 0)[halo:]
    return cw[2] * x0 + cw[1] * x1 + cw[0] * x2, (x0, x1, x2)


def _conv3_bwd_in(dce, cw, ts):
    n = dce.shape[0]
    d1 = pltpu.roll(dce, n - 1, 0)[:ts]
    d2 = pltpu.roll(dce, n - 2, 0)[:ts]
    return cw[2] * dce[:ts] + cw[1] * d1 + cw[0] * d2


def _conv3_bwd_w(dc, taps):
    x0, x1, x2 = taps
    return [jnp.sum(dc * x2, axis=0, keepdims=True), jnp.sum(dc * x1, axis=0, keepdims=True),
            jnp.sum(dc * x0, axis=0, keepdims=True)]


def _rms_fwd(x, g, name, after=None):
    s = x.shape[0]
    ts = _tile(s, 512)

    def body(x_ref, g_ref, *rest):
        h_ref = rest[-1]
        xv = x_ref[...]
        r = lax.rsqrt(jnp.mean(xv * xv, axis=-1, keepdims=True) + EPS)
        h_ref[...] = (xv * r * g_ref[...]).astype(BF16)

    extra = [] if after is None else [after]
    return pl.pallas_call(
        body, grid=(s // ts,), name=name,
        in_specs=[pl.BlockSpec((ts, D), lambda i: (i, 0)), pl.BlockSpec((1, D), lambda i: (0, 0))]
        + [ANY] * len(extra),
        out_specs=pl.BlockSpec((ts, D), lambda i: (i, 0)),
        out_shape=jax.ShapeDtypeStruct((s, D), BF16),
        compiler_params=_params("parallel"),
    )(x, g, *extra)


MXU_COLS = 256


def _pair(bn):
    return 1 if bn % MXU_COLS == 0 else 2


def _cols(w_ref, b, pair):
    return w_ref[b] if pair == 1 else jnp.concatenate([w_ref[b + q] for q in range(pair)], axis=1)


def _mm_in(h, wblk, name):
    s = h.shape[0]
    nb, _, bn = wblk.shape
    pair = _pair(bn)
    ts = _tile(s, 1024)

    def body(h_ref, w_ref, o_ref):
        hv = h_ref[...]
        for b in range(0, nb, pair):
            o_ref[:, b * bn:(b + pair) * bn] = jnp.dot(hv, _cols(w_ref, b, pair),
                                                       preferred_element_type=F32).astype(BF16)

    return pl.pallas_call(
        body, grid=(s // ts,), name=name,
        in_specs=[pl.BlockSpec((ts, D), lambda i: (i, 0)), pl.BlockSpec((nb, D, bn), lambda i: (0, 0, 0))],
        out_specs=pl.BlockSpec((ts, nb * bn), lambda i: (i, 0)),
        out_shape=jax.ShapeDtypeStruct((s, nb * bn), BF16),
        compiler_params=_params("parallel"),
    )(h, wblk)


def _rms_math(xv, g):
    r = lax.rsqrt(jnp.mean(xv * xv, axis=-1, keepdims=True) + EPS)
    return (xv * r * g).astype(BF16)


def _mm_out(y, w, xres, gnext, name):
    s = y.shape[0]
    ts = _tile(s, 1024)

    def body(y_ref, w_ref, x_ref, g_ref, o_ref, h_ref):
        xn = x_ref[...] + jnp.dot(y_ref[...], w_ref[...], preferred_element_type=F32)
        o_ref[...] = xn
        h_ref[...] = _rms_math(xn, g_ref[...])

    return pl.pallas_call(
        body, grid=(s // ts,), name=name,
        in_specs=[pl.BlockSpec((ts, D), lambda i: (i, 0)), pl.BlockSpec((D, D), lambda i: (0, 0)),
                  pl.BlockSpec((ts, D), lambda i: (i, 0)), pl.BlockSpec((1, D), lambda i: (0, 0))],
        out_specs=[pl.BlockSpec((ts, D), lambda i: (i, 0)), pl.BlockSpec((ts, D), lambda i: (i, 0))],
        out_shape=[jax.ShapeDtypeStruct((s, D), F32), jax.ShapeDtypeStruct((s, D), BF16)],
        compiler_params=_params("parallel"),
    )(y, w, xres, gnext)


CONV_ROWS = 32


def _rolled_copies(dst_ref, xe, back):
    n = xe.shape[0]
    dst_ref[0] = xe
    for r in range(1, 8):
        dst_ref[r] = pltpu.roll(xe, n - r if back else r, 0)


def _conv31(rolled_ref, cw_ref, ts, out_ref, bias):
    for o in range(0, ts, CONV_ROWS):
        acc = jnp.zeros((CONV_ROWS, DA), F32) + bias
        for sh in range(BCONV):
            q, r = divmod(sh, 8)
            lo = HALO_B - 8 * q + o
            acc = acc + cw_ref[BCONV - 1 - sh:BCONV - sh, :] * rolled_ref[r, lo:lo + CONV_ROWS, :]
        out_ref[o:o + CONV_ROWS, :] = acc


def _ab_fwd(z, lga, lba, wsm, bs_col, cwb, cbb, lgb, lbb, name):
    s = z.shape[0]
    ts = _tile(s, 256)
    hb = ts // HALO_B

    def body(z_ref, zh_ref, lga_ref, lba_ref, ws_ref, bs_ref, cw_ref, cb_ref, lgb_ref, lbb_ref,
             y_ref, yb2_ref, rolled):
        i = pl.program_id(0)
        z_t = z_ref[...].astype(F32)
        gu = _gelu(z_t[:, 0:DA])
        gv = _gelu(z_t[:, DA:2 * DA])
        vn, _, _ = _ln_fwd(gv, lga_ref[...], lba_ref[...])
        vnb = vn.astype(BF16)
        for c in range(ts // CHUNK):
            for h in range(HEADS):
                rs = slice(c * CHUNK, (c + 1) * CHUNK)
                cs = slice(h * CHUNK, (h + 1) * CHUNK)
                mixed = jnp.dot(ws_ref[h], vnb[rs, cs], preferred_element_type=F32) + bs_ref[h]
                y_ref[rs, cs] = (gu[rs, cs] * mixed).astype(BF16)
        zh = jnp.where(i > 0, zh_ref[...], jnp.zeros_like(zh_ref[...])).astype(F32)
        xb = jnp.concatenate([zh[:, 0:DA], z_t[:, 2 * DA:3 * DA]], axis=0)
        gb = jnp.concatenate([zh[:, DA:2 * DA], z_t[:, 3 * DA:4 * DA]], axis=0)
        _rolled_copies(rolled, xb * _sigmoid(gb), False)
        _conv31(rolled, cw_ref, ts, yb2_ref, cb_ref[...])
        nb_, _, _ = _ln_fwd(yb2_ref[...], lgb_ref[...], lbb_ref[...])
        y_ref[:, DA:2 * DA] = (nb_ * _sigmoid(nb_)).astype(BF16)

    row = lambda i: (0, 0)
    return pl.pallas_call(
        body, grid=(s // ts,), name=name,
        in_specs=[pl.BlockSpec((ts, 4 * DA), lambda i: (i, 0)),
                  pl.BlockSpec((HALO_B, 2 * DA), lambda i: (jnp.maximum(i * hb - 1, 0), 1)),
                  pl.BlockSpec((1, DA), row), pl.BlockSpec((1, DA), row),
                  pl.BlockSpec((HEADS, CHUNK, CHUNK), lambda i: (0, 0, 0)),
                  pl.BlockSpec((HEADS, CHUNK, 1), lambda i: (0, 0, 0)),
                  pl.BlockSpec((BCONV, DA), row), pl.BlockSpec((1, DA), row),
                  pl.BlockSpec((1, DA), row), pl.BlockSpec((1, DA), row)],
        out_specs=[pl.BlockSpec((ts, 2 * DA), lambda i: (i, 0)), pl.BlockSpec((ts, DA), lambda i: (i, 0))],
        out_shape=[jax.ShapeDtypeStruct((s, 2 * DA), BF16), jax.ShapeDtypeStruct((s, DA), F32)],
        scratch_shapes=[pltpu.VMEM((8, ts + HALO_B, DA), F32)],
        compiler_params=_params("parallel"),
    )(z, z, lga, lba, wsm, bs_col, cwb, cbb, lgb, lbb)


def _c_fwd(zc, cw, name):
    s = zc.shape[0]
    ts = _tile(s, 512)
    hb = ts // HALO

    def body(z_ref, ch_ref, xh_ref, cw_ref, r_ref):
        i = pl.program_id(0)
        z_t = z_ref[...].astype(F32)
        ph = jnp.where(i > 0, ch_ref[...].astype(F32) * xh_ref[...].astype(F32), 0.0)
        pe = jnp.concatenate([ph, z_t[:, D:2 * D] * z_t[:, 2 * D:3 * D]], axis=0)
        q, _ = _conv3(pe, [cw_ref[k:k + 1, :] for k in range(3)], HALO)
        r_ref[...] = (z_t[:, 0:D] * q).astype(BF16)

    halo = lambda col: pl.BlockSpec((HALO, D), lambda i: (jnp.maximum(i * hb - 1, 0), col))
    return pl.pallas_call(
        body, grid=(s // ts,), name=name,
        in_specs=[pl.BlockSpec((ts, 3 * D), lambda i: (i, 0)), halo(1), halo(2),
                  pl.BlockSpec((3, D), lambda i: (0, 0))],
        out_specs=pl.BlockSpec((ts, D), lambda i: (i, 0)),
        out_shape=jax.ShapeDtypeStruct((s, D), BF16),
        compiler_params=_params("parallel"),
    )(zc, zc, zc, cw)


def _final_math(xv, tv, gv):
    r = lax.rsqrt(jnp.mean(xv * xv, axis=-1, keepdims=True) + EPS)
    xhat = xv * r
    e = xhat * gv - tv
    part = 0.5 * jnp.sum(jnp.mean(e * e, axis=-1, keepdims=True), axis=0, keepdims=True)
    dy = e * (1.0 / D)
    dgp = jnp.sum(dy * xhat, axis=0, keepdims=True)
    u = dy * gv
    dx = r * (u - xhat * jnp.mean(u * xhat, axis=-1, keepdims=True))
    return dx, dgp, jnp.broadcast_to(part, (1, 128))


def _ffn_fwd(h, xres, wup, fcw, wdn, gnext, name, final=None):
    s = h.shape[0]
    ts = _tile(s, 512)
    hb = ts // HALO

    def body(h_ref, hh_ref, w_ref, cw_ref, wd_ref, x_ref, *rest):
        if final is not None:
            t_ref, gf_ref, up_ref, upc_ref, dx_ref, dxb_ref, dg_ref, loss_ref, up_s, xo_ref = rest
        elif gnext is not None:
            gn_ref, up_ref, upc_ref, xo_ref, hn_ref, up_s = rest
        else:
            up_ref, upc_ref, xo_ref, up_s = rest
        i = pl.program_id(0)
        m = pl.program_id(1)
        @pl.when(m == 0)
        def _():
            xo_ref[...] = x_ref[...]

        halo = jnp.where(i > 0, hh_ref[...], jnp.zeros_like(hh_ref[...]))
        hx = jnp.concatenate([halo, h_ref[...]], axis=0)
        acts = []
        for gv in range(2):
            up_s[gv] = jnp.dot(hx, w_ref[gv], preferred_element_type=F32)
            x0 = up_s[gv, HALO:HALO + ts, :]
            up_ref[gv] = x0.astype(BF16)
            upc = (cw_ref[gv, 2:3, :] * x0 + cw_ref[gv, 1:2, :] * up_s[gv, HALO - 1:HALO - 1 + ts, :]
                   + cw_ref[gv, 0:1, :] * up_s[gv, HALO - 2:HALO - 2 + ts, :])
            upc_ref[gv] = upc.astype(BF16)
            acts.append(upc)
        a = acts[0] * _sigmoid(acts[0]) * acts[1]
        xo_ref[...] += jnp.dot(a.astype(BF16), wd_ref[...], preferred_element_type=F32)

        if final is not None:
            @pl.when(m == NG - 1)
            def _():
                dx, dgp, part = _final_math(xo_ref[...], t_ref[...], gf_ref[...])
                dx_ref[...] = dx
                dxb_ref[...] = dx.astype(BF16)

                @pl.when(i == 0)
                def _():
                    dg_ref[...] = dgp
                    loss_ref[...] = part

                @pl.when(i > 0)
                def _():
                    dg_ref[...] += dgp
                    loss_ref[...] += part

        elif gnext is not None:
            @pl.when(m == NG - 1)
            def _():
                hn_ref[...] = _rms_math(xo_ref[...], gn_ref[...])

    tile = pl.BlockSpec((ts, D), lambda i, m: (i, 0))
    row = lambda n: pl.BlockSpec((1, n), lambda i, m: (0, 0))
    scratch = [pltpu.VMEM((2, ts + HALO, FB), F32)]
    if final is not None:
        more_in, more_ops = [tile, row(D)], list(final)
        more_out = [tile, tile, row(D), row(128)]
        more_shape = [jax.ShapeDtypeStruct((s, D), F32), jax.ShapeDtypeStruct((s, D), BF16),
                      jax.ShapeDtypeStruct((1, D), F32), jax.ShapeDtypeStruct((1, 128), F32)]
        scratch.append(pltpu.VMEM((ts, D), F32))
    else:
        nxt = gnext is not None
        more_in, more_ops = ([row(D)], [gnext]) if nxt else ([], [])
        more_out = [tile] + ([tile] if nxt else [])
        more_shape = [jax.ShapeDtypeStruct((s, D), F32)] + ([jax.ShapeDtypeStruct((s, D), BF16)] if nxt else [])
    return pl.pallas_call(
        body, grid=(s // ts, NG), name=name,
        in_specs=[tile,
                  pl.BlockSpec((HALO, D), lambda i, m: (jnp.maximum(i * hb - 1, 0), 0)),
                  pl.BlockSpec((2, None, D, FB), lambda i, m: (0, m, 0, 0)),
                  pl.BlockSpec((2, None, 3, FB), lambda i, m: (0, m, 0, 0)),
                  pl.BlockSpec((FB, D), lambda i, m: (m, 0)),
                  tile] + more_in,
        out_specs=[pl.BlockSpec((None, 2, ts, FB), lambda i, m: (m, 0, i, 0)),
                   pl.BlockSpec((None, 2, ts, FB), lambda i, m: (m, 0, i, 0))] + more_out,
        out_shape=[jax.ShapeDtypeStruct((NG, 2, s, FB), BF16), jax.ShapeDtypeStruct((NG, 2, s, FB), BF16)] + more_shape,
        scratch_shapes=scratch,
        compiler_params=_params("arbitrary", "arbitrary"),
    )(h, h, wup, fcw, wdn, xres, *more_ops)


def _ffn_bwd(df, up, upc, wup, fcw, wdn, xin, g, name):
    s = df.shape[0]
    ts = _tile(s, 512)
    nt = s // ts

    def body(df_ref, up_ref, upc_ref, w_ref, cw_ref, wd_ref, x_ref, g_ref,
             a_ref, dup_ref, dx_ref, dxb_ref, dg_ref, dcw_ref, carry, acc, tacc, dcs_ref):
        i = pl.program_id(0)
        m = pl.program_id(1)
        first = i == 0
        @pl.when(first)
        def _():
            carry[m] = jnp.zeros((2, 8, FB), F32)
            dcw_ref[m] = jnp.zeros((2, 3, FB), F32)

        @pl.when(m == 0)
        def _():
            acc[...] = jnp.zeros((ts, D), F32)

        cws = [[cw_ref[gv, k:k + 1, :] for k in range(3)] for gv in range(2)]
        part = ts // NPART
        das = [lax.dot_general(df_ref[p * part:(p + 1) * part, :].astype(BF16), wd_ref[...], NT_DIMS,
                               preferred_element_type=F32) for p in range(NPART)]

        tacc[...] = jnp.zeros((2, 3, 8, FB), F32)
        dcs_ref[:, ts:ts + 8, :] = carry[m]
        for r in reversed(range(ts // RC)):
            rs = slice(r * RC, (r + 1) * RC)
            gate = upc_ref[0, rs, :].astype(F32)
            val = upc_ref[1, rs, :].astype(F32)
            sg = _sigmoid(gate)
            sl = gate * sg
            a_ref[rs, :] = (sl * val).astype(BF16)
            da_c = das[(r * RC) // part][(r * RC) % part:(r * RC) % part + RC]
            dcs = [da_c * val * _dsilu(gate, sg), da_c * sl]
            for gv in range(2):
                dc = dcs[gv]
                dcs_ref[gv, rs, :] = dc
                d1 = dcs_ref[gv, r * RC + 1:(r + 1) * RC + 1, :]
                d2 = dcs_ref[gv, r * RC + 2:(r + 1) * RC + 2, :]
                du = cws[gv][2] * dc + cws[gv][1] * d1 + cws[gv][0] * d2
                dup_ref[gv, rs, :] = du.astype(BF16)
                x0 = up_ref[gv, rs, :].astype(F32)
                for k, dk in enumerate((d2, d1, dc)):
                    p = x0 * dk
                    tacc[gv, k] += sum(p[j:j + 8] for j in range(0, RC, 8))
            if (r * RC) % part == 0:
                ps = slice(r * RC, r * RC + part)
                acc[ps, :] += (
                    lax.dot_general(dup_ref[0, ps, :], w_ref[0], NT_DIMS, preferred_element_type=F32)
                    + lax.dot_general(dup_ref[1, ps, :], w_ref[1], NT_DIMS, preferred_element_type=F32))
        for gv in range(2):
            carry[m, gv] = dcs_ref[gv, 0:8, :]
            for k in range(3):
                dcw_ref[m, gv, k:k + 1, :] += jnp.sum(tacc[gv, k], axis=0, keepdims=True)

        @pl.when(m == NG - 1)
        def _():
            dx, dgp = _rms_bwd_math(acc[...], x_ref[...], g_ref[...])
            dx = df_ref[...] + dx
            dx_ref[...] = dx
            dxb_ref[...] = dx.astype(BF16)

            @pl.when(first)
            def _():
                dg_ref[...] = dgp

            @pl.when(jnp.logical_not(first))
            def _():
                dg_ref[...] += dgp

    rev = lambda i: nt - 1 - i
    return pl.pallas_call(
        body, grid=(nt, NG), name=name,
        in_specs=[pl.BlockSpec((ts, D), lambda i, m: (rev(i), 0)),
                  pl.BlockSpec((None, 2, ts, FB), lambda i, m: (m, 0, rev(i), 0)),
                  pl.BlockSpec((None, 2, ts, FB), lambda i, m: (m, 0, rev(i), 0)),
                  pl.BlockSpec((2, None, D, FB), lambda i, m: (0, m, 0, 0)),
                  pl.BlockSpec((2, None, 3, FB), lambda i, m: (0, m, 0, 0)),
                  pl.BlockSpec((FB, D), lambda i, m: (m, 0)),
                  pl.BlockSpec((ts, D), lambda i, m: (rev(i), 0)),
                  pl.BlockSpec((1, D), lambda i, m: (0, 0))],
        out_specs=[pl.BlockSpec((None, ts, FB), lambda i, m: (m, rev(i), 0)),
                   pl.BlockSpec((None, 2, ts, FB), lambda i, m: (m, 0, rev(i), 0)),
                   pl.BlockSpec((ts, D), lambda i, m: (rev(i), 0)),
                   pl.BlockSpec((ts, D), lambda i, m: (rev(i), 0)),
                   pl.BlockSpec((1, D), lambda i, m: (0, 0)),
                   pl.BlockSpec((NG, 2, 3, FB), lambda i, m: (0, 0, 0, 0))],
        out_shape=[jax.ShapeDtypeStruct((NG, s, FB), BF16), jax.ShapeDtypeStruct((NG, 2, s, FB), BF16),
                   jax.ShapeDtypeStruct((s, D), F32), jax.ShapeDtypeStruct((s, D), BF16),
                   jax.ShapeDtypeStruct((1, D), F32),
                   jax.ShapeDtypeStruct((NG, 2, 3, FB), F32)],
        scratch_shapes=[pltpu.VMEM((NG, 2, 8, FB), F32), pltpu.VMEM((ts, D), F32),
                        pltpu.VMEM((2, 3, 8, FB), F32), pltpu.VMEM((2, ts + 8, FB), F32)],
        compiler_params=_params("arbitrary", "arbitrary"),
    )(df, up, upc, wup, fcw, wdn, xin, g)


def _mm_nt_rms(dy, wblk, x, g, dres, bf16_copy, name):
    s = dy.shape[0]
    nb, _, bn = wblk.shape
    pair = _pair(bn)
    ts = _tile(s, 512)

    def body(dy_ref, w_ref, x_ref, g_ref, dr_ref, dx_ref, *rest):
        dg_ref = rest[-1]
        i = pl.program_id(0)
        acc = jnp.zeros((ts, D), F32)
        for b in range(0, nb, pair):
            acc = acc + lax.dot_general(dy_ref[:, b * bn:(b + pair) * bn], _cols(w_ref, b, pair), NT_DIMS,
                                        preferred_element_type=F32)
        dx, dgp = _rms_bwd_math(acc, x_ref[...], g_ref[...])
        dx = dr_ref[...] + dx
        dx_ref[...] = dx
        if bf16_copy:
            rest[0][...] = dx.astype(BF16)

        @pl.when(i == 0)
        def _():
            dg_ref[...] = dgp

        @pl.when(i > 0)
        def _():
            dg_ref[...] += dgp

    tile = pl.BlockSpec((ts, D), lambda i: (i, 0))
    return pl.pallas_call(
        body, grid=(s // ts,), name=name,
        in_specs=[pl.BlockSpec((ts, nb * bn), lambda i: (i, 0)), pl.BlockSpec((nb, D, bn), lambda i: (0, 0, 0)),
                  tile, pl.BlockSpec((1, D), lambda i: (0, 0)), tile],
        out_specs=[tile] + ([tile] if bf16_copy else []) + [pl.BlockSpec((1, D), lambda i: (0, 0))],
        out_shape=[jax.ShapeDtypeStruct((s, D), F32)] + ([jax.ShapeDtypeStruct((s, D), BF16)] if bf16_copy else [])
        + [jax.ShapeDtypeStruct((1, D), F32)],
        compiler_params=_params("arbitrary"),
    )(dy, wblk, x, g, dres)


def _c_bwd(dx, w, zc, cw, name):
    s = dx.shape[0]
    ts = _tile(s, 512)
    nt = s // ts
    hb = ts // HALO

    def body(dx_ref, dxf_ref, w_ref, z_ref, ch_ref, xh_ref, bf_ref, cw_ref, dz_ref, dcw_ref):
        i = pl.program_id(0)
        cwv = [cw_ref[k:k + 1, :] for k in range(3)]
        dre = lax.dot_general(jnp.concatenate([dx_ref[...], dxf_ref[...]], axis=0), w_ref[...], NT_DIMS,
                              preferred_element_type=F32).astype(BF16).astype(F32)
        z_t = z_ref[...].astype(F32)
        bg, cg, xv = z_t[:, 0:D], z_t[:, D:2 * D], z_t[:, 2 * D:3 * D]
        ph = jnp.where(i > 0, ch_ref[...].astype(F32) * xh_ref[...].astype(F32), 0.0)
        pe = jnp.concatenate([ph, cg * xv], axis=0)
        q, taps = _conv3(pe, cwv, HALO)
        drv = dre[0:ts]
        dq = drv * bg
        dqf = jnp.where(i < nt - 1, dre[ts:ts + HALO] * bf_ref[...].astype(F32), 0.0)
        dp = _conv3_bwd_in(jnp.concatenate([dq, dqf], axis=0), cwv, ts)
        dz_ref[:, 0:D] = (drv * q).astype(BF16)
        dz_ref[:, D:2 * D] = (dp * xv).astype(BF16)
        dz_ref[:, 2 * D:3 * D] = (dp * cg).astype(BF16)
        rows = _conv3_bwd_w(dq, taps)

        @pl.when(i == 0)
        def _():
            for k in range(3):
                dcw_ref[k:k + 1, :] = rows[k]

        @pl.when(i > 0)
        def _():
            for k in range(3):
                dcw_ref[k:k + 1, :] += rows[k]

    past = lambda col: pl.BlockSpec((HALO, D), lambda i: (jnp.maximum(i * hb - 1, 0), col))
    nxt = lambda i: jnp.minimum((i + 1) * hb, s // HALO - 1)
    return pl.pallas_call(
        body, grid=(nt,), name=name,
        in_specs=[pl.BlockSpec((ts, D), lambda i: (i, 0)),
                  pl.BlockSpec((HALO, D), lambda i: (nxt(i), 0)),
                  pl.BlockSpec((D, D), lambda i: (0, 0)),
                  pl.BlockSpec((ts, 3 * D), lambda i: (i, 0)), past(1), past(2),
                  pl.BlockSpec((HALO, D), lambda i: (nxt(i), 0)),
                  pl.BlockSpec((3, D), lambda i: (0, 0))],
        out_specs=[pl.BlockSpec((ts, 3 * D), lambda i: (i, 0)), pl.BlockSpec((3, D), lambda i: (0, 0))],
        out_shape=[jax.ShapeDtypeStruct((s, 3 * D), BF16), jax.ShapeDtypeStruct((3, D), F32)],
        compiler_params=_params("arbitrary"),
    )(dx, dx, w, zc, zc, zc, zc, cw)


G512_ROWS = 40


def _ab_bwd(dx, w, z, yb2, lga, lba, wsm, bs_col, cwb, lgb, lbb, name):
    s = z.shape[0]
    ts = _tile(s, 256)
    nt = s // ts
    hb = ts // HALO_B
    nch = ts // CHUNK

    def body(z_ref, zh_ref, dx_ref, dxf_ref, w_ref, yb2_ref, yb2f_ref, lga_ref, lba_ref, ws_ref, bs_ref,
             cw_ref, lgb_ref, lbb_ref, dz_ref, g512_ref, dws_ref, dbs_ref, dvn_ref, fwd_rolled, bwd_rolled, du_s):
        i = pl.program_id(0)
        last = i == nt - 1

        @pl.when(i == 0)
        def _():
            g512_ref[...] = jnp.zeros((G512_ROWS, DA), F32)
            dws_ref[...] = jnp.zeros((HEADS, CHUNK, CHUNK), F32)
            dbs_ref[...] = jnp.zeros((HEADS, CHUNK, 1), F32)

        def add_row(k, v):
            g512_ref[k:k + 1, :] += v

        z_t = z_ref[...].astype(F32)
        nt_dot = lambda a, b: lax.dot_general(a, b, NT_DIMS, preferred_element_type=F32).astype(BF16).astype(F32)
        dy_t = nt_dot(dx_ref[...], w_ref[...])
        dyf = nt_dot(dxf_ref[...], w_ref[DA:2 * DA, :])
        ua, va = z_t[:, 0:DA], z_t[:, DA:2 * DA]
        gu = _gelu(ua)
        gv = _gelu(va)
        lga_v = lga_ref[...]
        vn, xhat_a, rstd_a = _ln_fwd(gv, lga_v, lba_ref[...])
        vnb = vn.astype(BF16)
        causal = (lax.broadcasted_iota(jnp.int32, (CHUNK, CHUNK), 0)
                  >= lax.broadcasted_iota(jnp.int32, (CHUNK, CHUNK), 1)).astype(F32)
        for c in range(nch):
            for h in range(HEADS):
                rs = slice(c * CHUNK, (c + 1) * CHUNK)
                cs = slice(h * CHUNK, (h + 1) * CHUNK)
                vblk = vnb[rs, cs]
                mixed = jnp.dot(ws_ref[h], vblk, preferred_element_type=F32) + bs_ref[h]
                dyb_ = dy_t[rs, cs]
                dmix = dyb_ * gu[rs, cs]
                dmb = dmix.astype(BF16)
                dz_ref[rs, cs] = (dyb_ * mixed * _dgelu(ua[rs, cs])).astype(BF16)
                dvn_ref[rs, cs] = lax.dot_general(ws_ref[h], dmb, TN_DIMS, preferred_element_type=F32)
                dws_ref[h] += causal * lax.dot_general(dmb, vblk, NT_DIMS, preferred_element_type=F32)
                dbs_ref[h] += jnp.sum(dmix, axis=1, keepdims=True)
        dvn = dvn_ref[...]
        add_row(0, jnp.sum(dvn * xhat_a, axis=0, keepdims=True))
        add_row(1, jnp.sum(dvn, axis=0, keepdims=True))
        dgv = _ln_bwd(dvn, xhat_a, rstd_a, lga_v)
        dz_ref[:, DA:2 * DA] = (dgv * _dgelu(va)).astype(BF16)
        lgb_v = lgb_ref[...]
        dyb_e = jnp.concatenate(
            [dy_t[:, DA:2 * DA], jnp.where(last, 0.0, dyf)], axis=0)
        yb2_e = jnp.concatenate([yb2_ref[...], jnp.where(last, 0.0, yb2f_ref[...])], axis=0)
        n_e, xhat_b, rstd_b = _ln_fwd(yb2_e, lgb_v, lbb_ref[...])
        sgn = _sigmoid(n_e)
        dn = dyb_e * _dsilu(n_e, sgn)
        dy2 = _ln_bwd(dn, xhat_b, rstd_b, lgb_v)
        add_row(2, jnp.sum(dy2[:ts], axis=0, keepdims=True))
        add_row(3, jnp.sum(dn[:ts] * xhat_b[:ts], axis=0, keepdims=True))
        add_row(4, jnp.sum(dn[:ts], axis=0, keepdims=True))
        zh = jnp.where(i > 0, zh_ref[...], jnp.zeros_like(zh_ref[...])).astype(F32)
        xb_t, gb_t = z_t[:, 2 * DA:3 * DA], z_t[:, 3 * DA:4 * DA]
        sgb = _sigmoid(gb_t)
        _rolled_copies(fwd_rolled, jnp.concatenate(
            [zh[:, 0:DA] * _sigmoid(zh[:, DA:2 * DA]), xb_t * sgb], axis=0), False)
        _rolled_copies(bwd_rolled, dy2, True)
        for o in range(0, ts, CONV_ROWS):
            acc = jnp.zeros((CONV_ROWS, DA), F32)
            for sh in range(BCONV):
                q, r = divmod(sh, 8)
                acc = acc + cw_ref[BCONV - 1 - sh:BCONV - sh, :] * bwd_rolled[r, 8 * q + o:8 * q + o + CONV_ROWS, :]
            du_s[o:o + CONV_ROWS, :] = acc
        for sh in range(BCONV):
            q, r = divmod(sh, 8)
            acc = jnp.zeros((CONV_ROWS, DA), F32)
            for o in range(0, ts, CONV_ROWS):
                lo = HALO_B - 8 * q + o
                acc = acc + bwd_rolled[0, o:o + CONV_ROWS, :] * fwd_rolled[r, lo:lo + CONV_ROWS, :]
            add_row(8 + BCONV - 1 - sh, jnp.sum(acc, axis=0, keepdims=True))
        du = du_s[...]
        dz_ref[:, 2 * DA:3 * DA] = (du * sgb).astype(BF16)
        dz_ref[:, 3 * DA:4 * DA] = (du * xb_t * sgb * (1.0 - sgb)).astype(BF16)

    row = lambda i: (0, 0)
    nxt = lambda i: jnp.minimum((i + 1) * hb, s // HALO_B - 1)
    return pl.pallas_call(
        body, grid=(nt,), name=name,
        in_specs=[pl.BlockSpec((ts, 4 * DA), lambda i: (i, 0)),
                  pl.BlockSpec((HALO_B, 2 * DA), lambda i: (jnp.maximum(i * hb - 1, 0), 1)),
                  pl.BlockSpec((ts, D), lambda i: (i, 0)),
                  pl.BlockSpec((HALO_B, D), lambda i: (nxt(i), 0)),
                  pl.BlockSpec((D, D), lambda i: (0, 0)),
                  pl.BlockSpec((ts, DA), lambda i: (i, 0)),
                  pl.BlockSpec((HALO_B, DA), lambda i: (nxt(i), 0)),
                  pl.BlockSpec((1, DA), row), pl.BlockSpec((1, DA), row),
                  pl.BlockSpec((HEADS, CHUNK, CHUNK), lambda i: (0, 0, 0)),
                  pl.BlockSpec((HEADS, CHUNK, 1), lambda i: (0, 0, 0)),
                  pl.BlockSpec((BCONV, DA), row), pl.BlockSpec((1, DA), row), pl.BlockSpec((1, DA), row)],
        out_specs=[pl.BlockSpec((ts, 4 * DA), lambda i: (i, 0)),
                   pl.BlockSpec((G512_ROWS, DA), row),
                   pl.BlockSpec((HEADS, CHUNK, CHUNK), lambda i: (0, 0, 0)),
                   pl.BlockSpec((HEADS, CHUNK, 1), lambda i: (0, 0, 0))],
        out_shape=[jax.ShapeDtypeStruct((s, 4 * DA), BF16), jax.ShapeDtypeStruct((G512_ROWS, DA), F32),
                   jax.ShapeDtypeStruct((HEADS, CHUNK, CHUNK), F32),
                   jax.ShapeDtypeStruct((HEADS, CHUNK, 1), F32)],
        scratch_shapes=[pltpu.VMEM((ts, DA), F32), pltpu.VMEM((8, ts + HALO_B, DA), F32),
                        pltpu.VMEM((8, ts + HALO_B, DA), F32), pltpu.VMEM((ts, DA), F32)],
        compiler_params=_params("arbitrary"),
    )(z, z, dx, dx, w, yb2, yb2, lga, lba, wsm, bs_col, cwb, lgb, lbb)


def _dw_cols(a, dy, nb, bn, name):
    s = a.shape[0]
    tm = _tile(s, 2048)
    nt = s // tm
    cpb = 4

    def body(a_ref, dy_ref, o_ref, acc):
        t = pl.program_id(1)
        p = lax.dot_general(a_ref[...], dy_ref[...], TN_DIMS, preferred_element_type=F32)

        @pl.when(t == 0)
        def _():
            for q in range(cpb):
                acc[q] = p[:, q * bn:(q + 1) * bn]

        @pl.when(t > 0)
        def _():
            for q in range(cpb):
                acc[q] += p[:, q * bn:(q + 1) * bn]

        @pl.when(t == nt - 1)
        def _():
            o_ref[...] = acc[...].astype(BF16)

    return pl.pallas_call(
        body, grid=(nb // cpb, nt), name=name,
        in_specs=[pl.BlockSpec((tm, D), lambda j, t: (t, 0)), pl.BlockSpec((tm, cpb * bn), lambda j, t: (t, j))],
        out_specs=pl.BlockSpec((cpb, D, bn), lambda j, t: (j, 0, 0)),
        out_shape=jax.ShapeDtypeStruct((nb, D, bn), BF16),
        scratch_shapes=[pltpu.VMEM((cpb, D, bn), F32)],
        compiler_params=_params("arbitrary", "arbitrary"),
    )(a, dy)


def _dw_rows(a, dy, name):
    s = a.shape[0]
    tm = _tile(s, 4096)
    nt = s // tm
    rb = 512

    def body(a_ref, dy_ref, o_ref, acc):
        t = pl.program_id(1)
        p = lax.dot_general(a_ref[...], dy_ref[...], TN_DIMS, preferred_element_type=F32)

        @pl.when(t == 0)
        def _():
            acc[...] = p

        @pl.when(t > 0)
        def _():
            acc[...] += p

        @pl.when(t == nt - 1)
        def _():
            o_ref[...] = acc[...].astype(BF16)

    return pl.pallas_call(
        body, grid=(D // rb, nt), name=name,
        in_specs=[pl.BlockSpec((tm, rb), lambda j, t: (t, j)), pl.BlockSpec((tm, D), lambda j, t: (t, 0))],
        out_specs=pl.BlockSpec((rb, D), lambda j, t: (j, 0)),
        out_shape=jax.ShapeDtypeStruct((D, D), BF16),
        scratch_shapes=[pltpu.VMEM((rb, D), F32)],
        compiler_params=_params("arbitrary", "arbitrary"),
    )(a, dy)


def _dw_up(h, dup, name):
    s = h.shape[0]
    tm = _tile(s, 4096)
    nt = s // tm

    def body(h_ref, d_ref, o_ref, acc):
        t = pl.program_id(1)
        p = lax.dot_general(d_ref[...], h_ref[...], TN_DIMS, preferred_element_type=F32)

        @pl.when(t == 0)
        def _():
            acc[...] = p

        @pl.when(t > 0)
        def _():
            acc[...] += p

        @pl.when(t == nt - 1)
        def _():
            o_ref[...] = acc[...].astype(BF16)

    return pl.pallas_call(
        body, grid=(NDEV, nt), name=name,
        in_specs=[pl.BlockSpec((tm, D), lambda b, t: (t, 0)),
                  pl.BlockSpec((None, None, tm, FB), lambda b, t: (b % NG, b // NG, t, 0))],
        out_specs=pl.BlockSpec((None, FB, D), lambda b, t: (b, 0, 0)),
        out_shape=jax.ShapeDtypeStruct((NDEV, FB, D), BF16),
        scratch_shapes=[pltpu.VMEM((FB, D), F32)],
        compiler_params=_params("arbitrary", "arbitrary"),
    )(h, dup)


def _dw_dn(a, df, name):
    s = df.shape[0]
    tm = _tile(s, 4096)
    nt = s // tm

    def body(a_ref, d_ref, o_ref, acc):
        t = pl.program_id(1)
        p = lax.dot_general(a_ref[...], d_ref[...], TN_DIMS, preferred_element_type=F32)

        @pl.when(t == 0)
        def _():
            acc[...] = p

        @pl.when(t > 0)
        def _():
            acc[...] += p

        @pl.when(t == nt - 1)
        def _():
            o_ref[...] = acc[...].astype(BF16)

    return pl.pallas_call(
        body, grid=(NG, nt), name=name,
        in_specs=[pl.BlockSpec((None, tm, FB), lambda m, t: (m, t, 0)), pl.BlockSpec((tm, D), lambda m, t: (t, 0))],
        out_specs=pl.BlockSpec((FB, D), lambda m, t: (m, 0)),
        out_shape=jax.ShapeDtypeStruct((DFF, D), BF16),
        scratch_shapes=[pltpu.VMEM((FB, D), F32)],
        compiler_params=_params("arbitrary", "arbitrary"),
    )(a, df)


def _place():
    x, y, c = lax.axis_index("x"), lax.axis_index("y"), lax.axis_index("c")
    chips = [(1 - x, y), (x, 1 - y), (1 - x, 1 - y)]
    return x, y, c, chips


def _zone(shard, dev):
    return lax.dynamic_update_slice(lax.empty((NDEV,) + shard.shape, shard.dtype), shard[None],
                                    (dev,) + (0,) * shard.ndim)


HBM_SPEC = pl.BlockSpec(memory_space=pltpu.HBM)
SEM_SPEC = pl.BlockSpec(memory_space=pltpu.SEMAPHORE)
DATAFLOW = pltpu.SideEffectType.DATAFLOW_SIDE_EFFECTING


def _hbm(a):
    return pltpu.with_memory_space_constraint(a, pltpu.HBM)


def _hbm_like(arrs):
    return [pltpu.HBM(a.shape, a.dtype) for a in arrs]


def _ag_start(srcs, lands, after, name):
    n = len(srcs)
    ns = 8 * n

    def body(*refs):
        src, land = refs[:n], refs[n:2 * n]
        sems = refs[2 * n + 1:2 * n + 1 + ns]
        token = refs[-1]
        x, y, c, chips = _place()
        peers = [(x, y, 1 - c)] + [(*chip, c) for chip in chips]
        for t in range(n):
            for k, to in enumerate(peers):
                pltpu.make_async_remote_copy(
                    src_ref=src[t], dst_ref=land[t].at[4 * x + 2 * y + c],
                    send_sem=sems[2 * (4 * t + k)], recv_sem=sems[2 * (4 * t + k) + 1],
                    device_id=to, device_id_type=MESH).start()
        token[...] = jnp.zeros_like(token)

    res = pl.pallas_call(
        body, name=name,
        in_specs=[HBM_SPEC] * (2 * n) + [ANY],
        out_specs=[SEM_SPEC] * ns + [HBM_SPEC] * (2 * n) + [pl.BlockSpec(memory_space=pltpu.VMEM)],
        out_shape=[pltpu.SemaphoreType.DMA(())] * ns + _hbm_like(srcs) + _hbm_like(lands)
        + [jax.ShapeDtypeStruct((8, 128), F32)],
        input_output_aliases={i: ns + i for i in range(2 * n)},
        compiler_params=pltpu.CompilerParams(has_side_effects=DATAFLOW),
    )(*[_hbm(a) for a in srcs], *[_hbm(a) for a in lands], after)
    sems = [[(res[2 * (4 * t + k)], res[2 * (4 * t + k) + 1]) for k in range(4)] for t in range(n)]
    return sems, res[ns:ns + n], res[ns + n:ns + 2 * n], res[-1]


def _ag_forward(srcs, lands, sems1, after, name):
    n = len(srcs)
    flat1 = [s for t in range(n) for k in range(1, 4) for s in sems1[t][k]]
    n1 = len(flat1)

    def body(*refs):
        src, land = refs[:n], refs[n:2 * n]
        s1 = refs[2 * n:2 * n + n1]
        s2 = refs[2 * n + n1 + 1:2 * n + n1 + 1 + 6 * n]
        x, y, c, chips = _place()
        for j, (cx, cy) in enumerate(chips):
            for t in range(n):
                blk = land[t].at[4 * cx + 2 * cy + c]
                pltpu.make_async_remote_copy(
                    src_ref=src[t], dst_ref=blk, send_sem=s1[2 * (3 * t + j)], recv_sem=s1[2 * (3 * t + j) + 1],
                    device_id=(cx, cy, c), device_id_type=MESH).wait_recv()
                pltpu.make_async_remote_copy(
                    src_ref=blk, dst_ref=blk, send_sem=s2[2 * (3 * t + j)], recv_sem=s2[2 * (3 * t + j) + 1],
                    device_id=(x, y, 1 - c), device_id_type=MESH).start()

    res = pl.pallas_call(
        body, name=name,
        in_specs=[HBM_SPEC] * (2 * n) + [SEM_SPEC] * n1 + [ANY],
        out_specs=[SEM_SPEC] * (6 * n) + [HBM_SPEC] * n,
        out_shape=[pltpu.SemaphoreType.DMA(())] * (6 * n) + _hbm_like(lands),
        input_output_aliases={n + i: 6 * n + i for i in range(n)},
        compiler_params=pltpu.CompilerParams(has_side_effects=DATAFLOW),
    )(*srcs, *lands, *flat1, after)
    sems2 = [[(res[2 * (3 * t + j)], res[2 * (3 * t + j) + 1]) for j in range(3)] for t in range(n)]
    return sems2, res[6 * n:]


def _ag_finish(srcs, lands, sems1, sems2, after, name):
    n = len(srcs)
    flat1 = [s for t in range(n) for k in range(4) for s in sems1[t][k]]
    flat2 = [s for t in range(n) for j in range(3) for s in sems2[t][j]]
    n1, n2 = len(flat1), len(flat2)

    def body(*refs):
        src, land = refs[:n], refs[n:2 * n]
        s1 = refs[2 * n:2 * n + n1]
        s2 = refs[2 * n + n1:2 * n + n1 + n2]
        x, y, c, chips = _place()
        sib = (x, y, 1 - c)
        for t in range(n):
            own = land[t].at[4 * x + 2 * y + 1 - c]
            pltpu.make_async_remote_copy(
                src_ref=src[t], dst_ref=own, send_sem=s1[8 * t], recv_sem=s1[8 * t + 1],
                device_id=sib, device_id_type=MESH).wait_recv()
            for k in range(4):
                pltpu.make_async_remote_copy(
                    src_ref=src[t], dst_ref=own, send_sem=s1[2 * (4 * t + k)], recv_sem=s1[2 * (4 * t + k) + 1],
                    device_id=sib, device_id_type=MESH).wait_send()
            for j, (cx, cy) in enumerate(chips):
                blk = land[t].at[4 * cx + 2 * cy + 1 - c]
                cp = pltpu.make_async_remote_copy(
                    src_ref=blk, dst_ref=blk, send_sem=s2[2 * (3 * t + j)], recv_sem=s2[2 * (3 * t + j) + 1],
                    device_id=sib, device_id_type=MESH)
                cp.wait_send()
                cp.wait_recv()

    return pl.pallas_call(
        body, name=name,
        in_specs=[HBM_SPEC] * (2 * n) + [SEM_SPEC] * (n1 + n2) + [ANY],
        out_specs=[HBM_SPEC] * n,
        out_shape=_hbm_like(lands),
        input_output_aliases={n + i: i for i in range(n)},
        compiler_params=pltpu.CompilerParams(has_side_effects=DATAFLOW),
    )(*srcs, *lands, *flat1, *flat2, after)


def _pair_copies(srcs, dsts, sems):
    x, y, c, _ = _place()
    nt = len(srcs)
    return [pltpu.make_async_remote_copy(
        src_ref=srcs[t].at[2 * j + 1 - c], dst_ref=dsts[t].at[j],
        send_sem=sems[2 * (NCHIP * t + j)], recv_sem=sems[2 * (NCHIP * t + j) + 1],
        device_id=(x, y, 1 - c), device_id_type=MESH) for t in range(nt) for j in range(NCHIP)]


def _pair_start(grads, carry, name):
    nt = len(grads)
    ns = 2 * NCHIP * nt
    zones = [_hbm(lax.empty((NCHIP,) + a.shape[1:], a.dtype)) for a in grads]
    extra = [] if carry is None else [_hbm(carry)]
    ne = len(extra)

    def body(*refs):
        for cp in _pair_copies(refs[:nt], refs[nt:2 * nt], refs[2 * nt + ne:2 * nt + ne + ns]):
            cp.start()

    res = pl.pallas_call(
        body, name=name,
        in_specs=[HBM_SPEC] * (2 * nt + ne),
        out_specs=[SEM_SPEC] * ns + [HBM_SPEC] * (2 * nt + ne),
        out_shape=[pltpu.SemaphoreType.DMA(())] * ns + _hbm_like(grads) + _hbm_like(zones) + _hbm_like(extra),
        input_output_aliases={i: ns + i for i in range(2 * nt + ne)},
        compiler_params=pltpu.CompilerParams(has_side_effects=DATAFLOW),
    )(*[_hbm(a) for a in grads], *zones, *extra)
    handle = (list(res[:ns]), list(res[ns:ns + nt]), list(res[ns + nt:ns + 2 * nt]))
    return handle, (res[ns + 2 * nt] if ne else None)


def _pair_wait(handle, after, name):
    sems, srcs, zones = handle
    nt, ns = len(srcs), len(sems)

    def body(*refs):
        for cp in _pair_copies(refs[:nt], refs[nt:2 * nt], refs[2 * nt:2 * nt + ns]):
            cp.wait_send()
            cp.wait_recv()

    return pl.pallas_call(
        body, name=name,
        in_specs=[HBM_SPEC] * (2 * nt) + [SEM_SPEC] * ns + [ANY],
        out_specs=[HBM_SPEC] * nt,
        out_shape=_hbm_like(zones),
        input_output_aliases={nt + i: i for i in range(nt)},
        compiler_params=pltpu.CompilerParams(has_side_effects=DATAFLOW),
    )(*srcs, *zones, *sems, after)


def _rows_tile(r, row_bytes, cap_bytes):
    best = None
    for tr in range(16, r + 1, 16):
        if r % tr == 0 and tr * row_bytes <= cap_bytes:
            best = tr
    return best if best is not None else r


def _pair_sum(own, got, cidx, name):
    _, _, r, cdim = own.shape
    tr = _rows_tile(r, 2 * cdim, 2 * 1024 * 1024)

    def body(c_ref, a_ref, b_ref, o_ref):
        o_ref[...] = (a_ref[...].astype(F32) + b_ref[...].astype(F32)).astype(BF16)

    return pl.pallas_call(
        body, name=name,
        grid_spec=pltpu.PrefetchScalarGridSpec(
            num_scalar_prefetch=1, grid=(NCHIP, r // tr),
            in_specs=[pl.BlockSpec((None, None, tr, cdim), lambda j, i, c_ref: (j, c_ref[0], i, 0)),
                      pl.BlockSpec((None, tr, cdim), lambda j, i, c_ref: (j, i, 0))],
            out_specs=pl.BlockSpec((None, tr, cdim), lambda j, i, c_ref: (j, i, 0))),
        out_shape=jax.ShapeDtypeStruct((NCHIP, r, cdim), BF16),
        compiler_params=_params("arbitrary", "arbitrary"),
    )(cidx, own, got)


def _chip_copies(srcs, zones, slots, sems):
    x, y, c, chips = _place()
    out = []
    for t, (z, l) in enumerate(slots):
        for k, (cx, cy) in enumerate(chips):
            dst = zones[z].at[k] if l is None else zones[z].at[k, l]
            out.append(pltpu.make_async_remote_copy(
                src_ref=srcs[t].at[2 * cx + cy], dst_ref=dst,
                send_sem=sems[2 * (3 * t + k)], recv_sem=sems[2 * (3 * t + k) + 1],
                device_id=(cx, cy, c), device_id_type=MESH))
    return out


def _chip_start(sums, zones, slots, carry, name):
    nt, nz = len(sums), len(zones)
    ns = 6 * nt
    extra = [] if carry is None else [_hbm(carry)]
    ne = len(extra)

    def body(*refs):
        for cp in _chip_copies(refs[:nt], refs[nt:nt + nz], slots, refs[nt + nz + ne:nt + nz + ne + ns]):
            cp.start()

    res = pl.pallas_call(
        body, name=name,
        in_specs=[HBM_SPEC] * (nt + nz + ne),
        out_specs=[SEM_SPEC] * ns + [HBM_SPEC] * (nt + nz + ne),
        out_shape=[pltpu.SemaphoreType.DMA(())] * ns + _hbm_like(sums) + _hbm_like(zones) + _hbm_like(extra),
        input_output_aliases={i: ns + i for i in range(nt + nz + ne)},
        compiler_params=pltpu.CompilerParams(has_side_effects=DATAFLOW),
    )(*[_hbm(a) for a in sums], *zones, *extra)
    return (list(res[:ns]), list(res[ns:ns + nt]), list(res[ns + nt:ns + nt + nz]),
            (res[ns + nt + nz] if ne else None))


def _chip_wait(started, zones, zone_ids, after, name):
    started = [(sums, [(zone_ids.index(z), l) for z, l in slots], sems) for sums, slots, sems in started]
    nz = len(zones)
    flat_src = [a for sums, _, _ in started for a in sums]
    flat_sem = [s for _, _, sems in started for s in sems]
    n_src, n_sem = len(flat_src), len(flat_sem)

    def body(*refs):
        srcs, zs, sems = refs[:n_src], refs[n_src:n_src + nz], refs[n_src + nz:n_src + nz + n_sem]
        so, se = 0, 0
        for sums, slots, sem_list in started:
            for cp in _chip_copies(srcs[so:so + len(sums)], zs, slots, sems[se:se + len(sem_list)]):
                cp.wait_send()
                cp.wait_recv()
            so += len(sums)
            se += len(sem_list)

    return pl.pallas_call(
        body, name=name,
        in_specs=[HBM_SPEC] * (n_src + nz) + [SEM_SPEC] * n_sem + [ANY],
        out_specs=[HBM_SPEC] * nz,
        out_shape=_hbm_like(zones),
        input_output_aliases={n_src + i: i for i in range(nz)},
        compiler_params=pltpu.CompilerParams(has_side_effects=DATAFLOW),
    )(*flat_src, *zones, *flat_sem, after)


def _small_allreduce(parts, y_first, after, name):
    nt = len(parts)

    def body(*refs):
        srcs, outs, bufs = refs[:nt], refs[nt + 1:2 * nt + 1], refs[2 * nt + 1:3 * nt + 1]
        send_sems, recv_sems = refs[3 * nt + 1:]
        x, y, c, _ = _place()
        along = {"c": (x, y, 1 - c), "x": (1 - x, y, c), "y": (x, 1 - y, c)}
        for t in range(nt):
            outs[t][...] = srcs[t][...]
        for step in range(3):
            order = [("c", "y", "x") if t in y_first else ("c", "x", "y") for t in range(nt)]
            copies = [pltpu.make_async_remote_copy(
                src_ref=outs[t], dst_ref=bufs[t].at[step],
                send_sem=send_sems.at[step, t], recv_sem=recv_sems.at[step, t],
                device_id=along[order[t][step]], device_id_type=MESH) for t in range(nt)]
            for cp in copies:
                cp.start()
            for cp in copies:
                cp.wait()
            for t in range(nt):
                outs[t][...] = outs[t][...] + bufs[t][step]

    vm = pl.BlockSpec(memory_space=pltpu.VMEM)
    return pl.pallas_call(
        body, name=name,
        in_specs=[vm] * nt + [ANY], out_specs=[vm] * nt,
        out_shape=[jax.ShapeDtypeStruct(a.shape, F32) for a in parts],
        scratch_shapes=[pltpu.VMEM((3,) + a.shape, F32) for a in parts]
        + [pltpu.SemaphoreType.DMA((3, nt)), pltpu.SemaphoreType.DMA((3, nt))],
        compiler_params=pltpu.CompilerParams(has_side_effects=True, vmem_limit_bytes=VMEM_LIMIT),
    )(*parts, after)


def _adam_math(w, g, m, v):
    m2 = ADAM_B1 * m + (1.0 - ADAM_B1) * g
    v2 = ADAM_B2 * v + (1.0 - ADAM_B2) * (g * g)
    m_hat = m2 / (1.0 - ADAM_B1 ** ADAM_STEP)
    v_hat = v2 / (1.0 - ADAM_B2 ** ADAM_STEP)
    delta = -ADAM_LR * (m_hat / (jnp.sqrt(v_hat) + ADAM_EPS) + ADAM_WD * w)
    return delta, m2, v2


def _adam_big(w, m, v, parts, mine, chip, name):
    nl, r, cdim = w.shape
    tr = _rows_tile(r, 4 * cdim, 3 * 512 * 1024)

    def body(c_ref, w_ref, m_ref, v_ref, p_ref, *rest):
        mine_refs, (g_ref, d_ref, mo_ref, vo_ref) = rest[:nl], rest[nl:]
        own = mine_refs[0][...]
        for l in range(1, nl):
            own = jnp.where(pl.program_id(0) == l, mine_refs[l][...], own)
        g = ((p_ref[0].astype(F32) + p_ref[1].astype(F32)) + p_ref[2].astype(F32)) + own.astype(F32)
        delta, m2, v2 = _adam_math(w_ref[...], g, m_ref[...], v_ref[...])
        g_ref[...] = g
        d_ref[...] = delta
        mo_ref[...] = m2
        vo_ref[...] = v2

    spec = pl.BlockSpec((None, tr, cdim), lambda l, i, c_ref: (l, i, 0))
    mine_specs = [pl.BlockSpec((None, tr, cdim), lambda l, i, c_ref, ll=ll: (c_ref[0], jnp.where(l == ll, i, 0), 0))
                  for ll in range(nl)]
    return pl.pallas_call(
        body, name=name,
        grid_spec=pltpu.PrefetchScalarGridSpec(
            num_scalar_prefetch=1, grid=(nl, r // tr),
            in_specs=[spec, spec, spec, pl.BlockSpec((3, None, tr, cdim), lambda l, i, c_ref: (0, l, i, 0))]
            + mine_specs,
            out_specs=[spec] * 4),
        out_shape=[jax.ShapeDtypeStruct(w.shape, F32)] * 4,
        compiler_params=_params("arbitrary", "arbitrary"),
    )(chip, w, m, v, parts, *mine)


def _adam_small(ws, gs, ms, vs, name):
    n = len(ws)

    def body(*refs):
        w_r, g_r, m_r, v_r = refs[:n], refs[n:2 * n], refs[2 * n:3 * n], refs[3 * n:4 * n]
        d_o, m_o, v_o = refs[4 * n:5 * n], refs[5 * n:6 * n], refs[6 * n:7 * n]
        for t in range(n):
            delta, m2, v2 = _adam_math(w_r[t][...], g_r[t][...], m_r[t][...], v_r[t][...])
            d_o[t][...] = delta
            m_o[t][...] = m2
            v_o[t][...] = v2

    vm = pl.BlockSpec(memory_space=pltpu.VMEM)
    shapes = [jax.ShapeDtypeStruct(a.shape, F32) for a in ws]
    return pl.pallas_call(
        body, name=name, in_specs=[vm] * (4 * n), out_specs=[vm] * (3 * n), out_shape=shapes * 3,
        compiler_params=pltpu.CompilerParams(vmem_limit_bytes=VMEM_LIMIT),
    )(*ws, *gs, *ms, *vs)


def kernel(x, norm_mix, norm_ffn, norm_final, ab_w_in, a_ln_g, a_ln_b, a_w_s, a_b_s, b_conv_w, b_conv_b, b_ln_g, b_ln_b, ab_w_out, c_w_in, c_conv_w, c_w_out, f_w_up, f_conv_w, f_w_down, loss_target, m_norm_mix, m_norm_ffn, m_norm_final, m_ab_w_in, m_a_ln_g, m_a_ln_b, m_a_w_s, m_a_b_s, m_b_conv_w, m_b_conv_b, m_b_ln_g, m_b_ln_b, m_ab_w_out, m_c_w_in, m_c_conv_w, m_c_w_out, m_f_w_up, m_f_conv_w, m_f_w_down, v_norm_mix, v_norm_ffn, v_norm_final, v_ab_w_in, v_a_ln_g, v_a_ln_b, v_a_w_s, v_a_b_s, v_b_conv_w, v_b_conv_b, v_b_ln_g, v_b_ln_b, v_ab_w_out, v_c_w_in, v_c_conv_w, v_c_w_out, v_f_w_up, v_f_conv_w, v_f_w_down):
    s = x.shape[1]
    x0 = x.reshape(s, D)
    tgt = loss_target.reshape(s, D)
    xi, yi, ci = lax.axis_index("x"), lax.axis_index("y"), lax.axis_index("c")
    dev = 4 * xi + 2 * yi + ci
    cidx = ci.astype(jnp.int32).reshape(1)

    bf = lambda a: a.astype(BF16)
    slab_w = 6 * CHUNK
    pad = lambda a, rows: jnp.pad(a, ((0, rows - a.shape[0]), (0, slab_w - a.shape[1])))
    slab = jnp.concatenate([pad(b_conv_w[0], 32), pad(c_conv_w[0], 8), pad(f_conv_w.reshape(6, FB), 8)], axis=0)
    later = [bf(ab_w_in[0]), bf(ab_w_out[0]), slab, bf(f_w_up[0]), bf(f_w_down[0]), bf(c_w_in[0]), bf(c_w_out[0]),
             bf(f_w_up[1]), bf(f_w_down[1])]
    lands = [_zone(a, dev) for a in later]
    groups = [[0], [1, 2], [3, 4], [5, 6], [7, 8]]
    ag_sems, later, lands, ag_token = _ag_start(later, lands, x0, "ag_start")

    causal = jnp.tril(jnp.ones((CHUNK, CHUNK), F32))
    wsm = (a_w_s[0] * causal).astype(BF16)
    bs_col = a_b_s.reshape(HEADS, CHUNK, 1)
    nm = [norm_mix[0:1], norm_mix[1:2]]
    nf = [norm_ffn[0:1], norm_ffn[1:2]]
    nfin = norm_final.reshape(1, D)

    def arrive(g, after_ici, after_d2d, tag):
        srcs = [later[t] for t in groups[g]]
        zone = [lands[t] for t in groups[g]]
        sems1 = [ag_sems[t] for t in groups[g]]
        sems2, zone = _ag_forward(srcs, zone, sems1, after_ici, "ag_forward_" + tag)
        return _ag_finish(srcs, zone, sems1, sems2, after_d2d, "ag_finish_" + tag)

    h0 = _rms_fwd(x0, nm[0], "rms_mix0", after=ag_token)
    (win0,) = arrive(0, h0, h0, "w_in")
    z = _mm_in(h0, win0, "mm_ab_in")
    wout0, slab_g = arrive(1, z, z, "first")
    wout0 = wout0.reshape(D, D)
    bcw = jnp.transpose(slab_g[:, 0:BCONV, 0:DA // NDEV], (1, 0, 2)).reshape(BCONV, DA)
    ccw = jnp.transpose(slab_g[:, 32:35, 0:D // NDEV], (1, 0, 2)).reshape(3, D)
    fcw_g = slab_g[:, 40:46, 0:FB].reshape(2, NG, 2, 3, FB)
    fcws = [fcw_g[:, :, 0], fcw_g[:, :, 1]]
    ycat, yb2 = _ab_fwd(z, a_ln_g, a_ln_b, wsm, bs_col, bcw, b_conv_b, b_ln_g, b_ln_b, "ab_fwd")
    x1, h1 = _mm_out(ycat, wout0, x0, nf[0], "mm_ab_out")
    wup0, wdn0 = arrive(2, x1, x1, "ffn0")
    up0, upc0, x2, h2 = _ffn_fwd(h1, x1, wup0.reshape(2, NG, D, FB), fcws[0], wdn0.reshape(DFF, D), nm[1],
                                 "ffn_fwd0")
    cin, cout = arrive(3, x2, x2, "c")
    cout = cout.reshape(D, D)
    zc = _mm_in(h2, cin, "mm_c_in")
    rc = _c_fwd(zc, ccw, "c_fwd")
    x3, h3 = _mm_out(rc, cout, x2, nf[1], "mm_c_out")
    wup1, wdn1 = arrive(4, rc, x3, "ffn1")
    wups = [wup0.reshape(2, NG, D, FB), wup1.reshape(2, NG, D, FB)]
    wdns = [wdn0.reshape(DFF, D), wdn1.reshape(DFF, D)]
    up1, upc1, dx4, dx4b, dnfin, loss_part = _ffn_fwd(h3, x3, wups[1], fcws[1], wdns[1], None, "ffn_fwd1_loss",
                                                      final=(tgt, nfin))

    zshape = lambda *sh: _hbm(lax.empty((3,) + sh, BF16))
    zones = [zshape(D, 2 * D // NDEV), zshape(D // NDEV, D), zshape(D, 3 * D // NDEV), zshape(D // NDEV, D),
             zshape(2, FB, D), zshape(2, DFF // NDEV, D)]
    started = []

    def pair_sums(grads, handle, after, tag):
        del grads
        got = _pair_wait(handle, after, "rs_pair_wait_" + tag)
        return [_pair_sum(b.reshape((NCHIP, 2) + b.shape[1:]), g, cidx, "rs_pair_sum_%s%d" % (tag, t))
                for t, (b, g) in enumerate(zip(handle[1], got))]

    def chip_start(sums, slots, carry, tag):
        sems, sums, new_zones, carry = _chip_start(sums, zones, slots, carry, "rs_chip_start_" + tag)
        zones[:] = new_zones
        started.append((sums, slots, sems))
        return sums, carry

    rows8 = lambda g, r: g.reshape(NDEV, r, D)
    a1, dup1, dx3, dx3b, dnf1, dfcw1 = _ffn_bwd(dx4, up1, upc1, wups[1], fcws[1], wdns[1], x3, nf[1], "ffn_bwd1")
    g_f1 = [_dw_up(h3, dup1, "dw_up1"), rows8(_dw_dn(a1, dx4b, "dw_dn1"), DFF // NDEV)]
    hd_f1, dx3b = _pair_start(g_f1, dx3b, "rs_pair_start_f1")
    g_cout = rows8(_dw_rows(rc, dx3b, "dw_c_out"), D // NDEV)
    s_f1 = pair_sums(g_f1, hd_f1, g_cout, "f1")
    s_f1, dx3b = chip_start(s_f1, [(4, 1), (5, 1)], dx3b, "f1")
    dzc, dccw = _c_bwd(dx3b, cout, zc, ccw, "c_bwd")
    dx2, dx2b, dnm1 = _mm_nt_rms(dzc, cin, x2, nm[1], dx3, True, "mm_c_in_bwd")
    g_c = [_dw_cols(h2, dzc, NDEV, 3 * D // NDEV, "dw_c_in"), g_cout]
    hd_c, dx2 = _pair_start(g_c, dx2, "rs_pair_start_c")
    a0, dup0, dx1, dx1b, dnf0, dfcw0 = _ffn_bwd(dx2, up0, upc0, wups[0], fcws[0], wdns[0], x1, nf[0], "ffn_bwd0")
    s_c = pair_sums(g_c, hd_c, dx1b, "c")
    s_c, dx1b = chip_start(s_c, [(2, None), (3, None)], dx1b, "c")
    g_f0 = [_dw_up(h1, dup0, "dw_up0"), rows8(_dw_dn(a0, dx2b, "dw_dn0"), DFF // NDEV)]
    hd_f0, dx1b = _pair_start(g_f0, dx1b, "rs_pair_start_f0")
    g_wout0 = rows8(_dw_rows(ycat, dx1b, "dw_ab_out"), D // NDEV)
    s_f0 = pair_sums(g_f0, hd_f0, g_wout0, "f0")
    s_f0, dx1b = chip_start(s_f0, [(4, 0), (5, 0)], dx1b, "f0")
    dz, g512, dws, dbs = _ab_bwd(dx1b, wout0, z, yb2, a_ln_g, a_ln_b, wsm, bs_col, bcw, b_ln_g, b_ln_b, "ab_bwd")
    grad_x, dnm0 = _mm_nt_rms(dz, win0, x0, nm[0], dx1, False, "mm_ab_in_bwd")
    g_ab = [_dw_cols(h0, dz, NDEV, 2 * D // NDEV, "dw_ab_in"), g_wout0]
    hd_ab, _ = _pair_start(g_ab, None, "rs_pair_start_ab")

    g1024 = jnp.concatenate([dnm0, dnm1, dnf0, dnf1, dnfin, dccw], axis=0)
    gfc = jnp.concatenate([dfcw0, dfcw1], axis=0).reshape(2 * NG * 2 * 3, FB)
    g1024, g512, dws, dbs, gfc, loss_sum = _small_allreduce(
        [g1024, g512, dws.reshape(HEADS * CHUNK, CHUNK), dbs.reshape(HEADS, CHUNK), gfc, loss_part], (2,),
        hd_ab[1][0], "small_allreduce")
    loss = loss_sum[0, 0]
    s_ab = pair_sums(g_ab, hd_ab, g1024, "ab")
    s_ab, _ = chip_start(s_ab, [(0, None), (1, None)], None, "ab")
    p_cin, p_cout, p_wup, p_wdn = _chip_wait(started[:3], zones[2:], [2, 3, 4, 5], s_ab[0], "rs_chip_wait_early")

    chip = (2 * xi + yi).astype(jnp.int32).reshape(1)

    def big_update(w, m, v, parts, mine, name):
        shp = w.shape
        w3, m3, v3 = (a.reshape((-1,) + shp[-2:]) for a in (w, m, v))
        p4 = parts.reshape((3,) + w3.shape)
        return [o.reshape(shp) for o in _adam_big(w3, m3, v3, p4, mine, chip, name)]

    u_cin = big_update(c_w_in, m_c_w_in, v_c_w_in, p_cin, [s_c[0]], "adam_c_w_in")
    u_cout = big_update(c_w_out, m_c_w_out, v_c_w_out, p_cout, [s_c[1]], "adam_c_w_out")
    tr_ = lambda a: jnp.swapaxes(a, 1, 2)
    u_wup = [tr_(o) for o in big_update(tr_(f_w_up), tr_(m_f_w_up), tr_(v_f_w_up), p_wup,
                                        [s_f0[0], s_f1[0]], "adam_f_w_up")]
    u_wdn = big_update(f_w_down, m_f_w_down, v_f_w_down, p_wdn, [s_f0[1], s_f1[1]], "adam_f_w_down")
    p_win0, p_wout0 = _chip_wait(started[3:], zones[:2], [0, 1], u_wdn[0], "rs_chip_wait_late")
    u_win0 = big_update(ab_w_in, m_ab_w_in, v_ab_w_in, p_win0, [s_ab[0]], "adam_ab_w_in")
    u_wout0 = big_update(ab_w_out, m_ab_w_out, v_ab_w_out, p_wout0, [s_ab[1]], "adam_ab_w_out")

    g_norm_mix = g1024[0:2]
    g_norm_ffn = g1024[2:4]
    g_norm_final = g1024[4:5]
    g_ccw = lax.dynamic_slice(g1024[5:8], (0, dev * (D // NDEV)), (3, D // NDEV))
    g_bcw = lax.dynamic_slice(g512[8:8 + BCONV], (0, dev * (DA // NDEV)), (BCONV, DA // NDEV))
    gfc = gfc.reshape(2, NG, 2, 3, FB)
    g_fcw = lax.dynamic_slice(gfc, (0, dev % NG, dev // NG, 0, 0), (2, 1, 1, 3, FB)).reshape(2, 3, FB)
    small_w = [norm_mix, norm_ffn, nfin, a_ln_g, a_ln_b, a_w_s[0], a_b_s[0], b_conv_w[0], b_conv_b,
               b_ln_g, b_ln_b, c_conv_w[0], f_conv_w]
    small_g = [g_norm_mix, g_norm_ffn, g_norm_final, g512[0:1], g512[1:2],
               dws.reshape(HEADS, CHUNK, CHUNK), dbs, g_bcw, g512[2:3],
               g512[3:4], g512[4:5], g_ccw, g_fcw]
    small_m = [m_norm_mix, m_norm_ffn, m_norm_final.reshape(1, D), m_a_ln_g, m_a_ln_b, m_a_w_s[0], m_a_b_s[0],
               m_b_conv_w[0], m_b_conv_b, m_b_ln_g, m_b_ln_b, m_c_conv_w[0], m_f_conv_w]
    small_v = [v_norm_mix, v_norm_ffn, v_norm_final.reshape(1, D), v_a_ln_g, v_a_ln_b, v_a_w_s[0], v_a_b_s[0],
               v_b_conv_w[0], v_b_conv_b, v_b_ln_g, v_b_ln_b, v_c_conv_w[0], v_f_conv_w]
    upd = _adam_small(small_w, small_g, small_m, small_v, "adam_small")
    ns = len(small_w)
    orig = [norm_mix, norm_ffn, norm_final, a_ln_g, a_ln_b, a_w_s, a_b_s, b_conv_w, b_conv_b,
            b_ln_g, b_ln_b, c_conv_w, f_conv_w]
    sg_out = [g.reshape(o.shape) for g, o in zip(small_g, orig)]
    sd_out = [a.reshape(o.shape) for a, o in zip(upd[0:ns], orig)]
    sm_out = [a.reshape(o.shape) for a, o in zip(upd[ns:2 * ns], orig)]
    sv_out = [a.reshape(o.shape) for a, o in zip(upd[2 * ns:3 * ns], orig)]

    def assemble(small, k):
        return [small[0], small[1], small[2], u_win0[k], small[3], small[4], small[5], small[6], small[7],
                small[8], small[9], small[10], u_wout0[k], u_cin[k], small[11], u_cout[k], u_wup[k],
                small[12], u_wdn[k]]

    grads = assemble(sg_out, 0)
    deltas = assemble(sd_out, 1)
    new_m = assemble(sm_out, 2)
    new_v = assemble(sv_out, 3)
    return (loss, grad_x.reshape(1, s, D), *grads, *deltas, *new_m, *new_v)
```

```python
import math

import jax
import jax.numpy as jnp
from jax import lax
from jax.experimental import pallas as pl
from jax.experimental.pallas import tpu as pltpu

F32 = jnp.float32
BF16 = jnp.bfloat16

D = 1024
DA = 512
HEADS = 4
CHUNK = 128
DFF = 2816
NDEV = 8
NCHIP = 4
FB = DFF * 2 // NDEV
NG = DFF // FB
BCONV = 31
EPS = 1e-6
HALO = 16
HALO_B = 32
RC = 32
NPART = 2
VMEM_LIMIT = 52 * 1024 * 1024
INV_SQRT2 = 1.0 / math.sqrt(2.0)
INV_SQRT_2PI = 1.0 / math.sqrt(2.0 * math.pi)

ADAM_LR = 0.001
ADAM_B1 = 0.9
ADAM_B2 = 0.999
ADAM_EPS = 1e-08
ADAM_WD = 0.01
ADAM_STEP = 10

MESH = pl.DeviceIdType.MESH
ANY = pl.BlockSpec(memory_space=pl.ANY)
NT_DIMS = (((1,), (1,)), ((), ()))
TN_DIMS = (((0,), (0,)), ((), ()))


def _params(*sem):
    return pltpu.CompilerParams(dimension_semantics=sem, vmem_limit_bytes=VMEM_LIMIT)


def _tile(s, want):
    return min(want, s)


def _sigmoid(x):
    return jax.nn.sigmoid(x)


def _dsilu(x, sg):
    return sg * (1.0 + x * (1.0 - sg))


def _gelu(x):
    return 0.5 * x * (1.0 + lax.erf(x * INV_SQRT2))


def _dgelu(x):
    return 0.5 * (1.0 + lax.erf(x * INV_SQRT2)) + x * jnp.exp(-0.5 * x * x) * INV_SQRT_2PI


def _ln_fwd(x, g, b):
    mu = jnp.mean(x, axis=-1, keepdims=True)
    xc = x - mu
    var = jnp.mean(xc * xc, axis=-1, keepdims=True)
    rstd = lax.rsqrt(var + EPS)
    xhat = xc * rstd
    return xhat * g + b, xhat, rstd


def _ln_bwd(dy, xhat, rstd, g):
    dxh = dy * g
    m1 = jnp.mean(dxh, axis=-1, keepdims=True)
    m2 = jnp.mean(dxh * xhat, axis=-1, keepdims=True)
    return rstd * (dxh - m1 - xhat * m2)


def _rms_bwd_math(dh, x, g):
    r = lax.rsqrt(jnp.mean(x * x, axis=-1, keepdims=True) + EPS)
    xhat = x * r
    dg = jnp.sum(dh * xhat, axis=0, keepdims=True)
    u = dh * g
    dx = r * (u - xhat * jnp.mean(u * xhat, axis=-1, keepdims=True))
    return dx, dg


def _conv3(xe, cw, halo):
    x0 = xe[halo:]
    x1 = pltpu.roll(xe, 1, 0)[halo:]
    x2 = pltpu.roll(xe, 2, 0)[halo:]
    return cw[2] * x0 + cw[1] * x1 + cw[0] * x2, (x0, x1, x2)


def _conv3_bwd_in(dce, cw, ts):
    n = dce.shape[0]
    d1 = pltpu.roll(dce, n - 1, 0)[:ts]
    d2 = pltpu.roll(dce, n - 2, 0)[:ts]
    return cw[2] * dce[:ts] + cw[1] * d1 + cw[0] * d2


def _conv3_bwd_w(dc, taps):
    x0, x1, x2 = taps
    return [jnp.sum(dc * x2, axis=0, keepdims=True), jnp.sum(dc * x1, axis=0, keepdims=True),
            jnp.sum(dc * x0, axis=0, keepdims=True)]


def _rms_fwd(x, g, name, after=None):
    s = x.shape[0]
    ts = _tile(s, 512)

    def body(x_ref, g_ref, *rest):
        h_ref = rest[-1]
        xv = x_ref[...]
        r = lax.rsqrt(jnp.mean(xv * xv, axis=-1, keepdims=True) + EPS)
        h_ref[...] = (xv * r * g_ref[...]).astype(BF16)

    extra = [] if after is None else [after]
    return pl.pallas_call(
        body, grid=(s // ts,), name=name,
        in_specs=[pl.BlockSpec((ts, D), lambda i: (i, 0)), pl.BlockSpec((1, D), lambda i: (0, 0))]
        + [ANY] * len(extra),
        out_specs=pl.BlockSpec((ts, D), lambda i: (i, 0)),
        out_shape=jax.ShapeDtypeStruct((s, D), BF16),
        compiler_params=_params("parallel"),
    )(x, g, *extra)


MXU_COLS = 256


def _pair(bn):
    return 1 if bn % MXU_COLS == 0 else 2


def _cols(w_ref, b, pair):
    return w_ref[b] if pair == 1 else jnp.concatenate([w_ref[b + q] for q in range(pair)], axis=1)


def _mm_in(h, wblk, name):
    s = h.shape[0]
    nb, _, bn = wblk.shape
    pair = _pair(bn)
    ts = _tile(s, 1024)

    def body(h_ref, w_ref, o_ref):
        hv = h_ref[...]
        for b in range(0, nb, pair):
            o_ref[:, b * bn:(b + pair) * bn] = jnp.dot(hv, _cols(w_ref, b, pair),
                                                       preferred_element_type=F32).astype(BF16)

    return pl.pallas_call(
        body, grid=(s // ts,), name=name,
        in_specs=[pl.BlockSpec((ts, D), lambda i: (i, 0)), pl.BlockSpec((nb, D, bn), lambda i: (0, 0, 0))],
        out_specs=pl.BlockSpec((ts, nb * bn), lambda i: (i, 0)),
        out_shape=jax.ShapeDtypeStruct((s, nb * bn), BF16),
        compiler_params=_params("parallel"),
    )(h, wblk)


def _rms_math(xv, g):
    r = lax.rsqrt(jnp.mean(xv * xv, axis=-1, keepdims=True) + EPS)
    return (xv * r * g).astype(BF16)


def _mm_out(y, w, xres, gnext, name):
    s = y.shape[0]
    ts = _tile(s, 1024)

    def body(y_ref, w_ref, x_ref, g_ref, o_ref, h_ref):
        xn = x_ref[...] + jnp.dot(y_ref[...], w_ref[...], preferred_element_type=F32)
        o_ref[...] = xn
        h_ref[...] = _rms_math(xn, g_ref[...])

    return pl.pallas_call(
        body, grid=(s // ts,), name=name,
        in_specs=[pl.BlockSpec((ts, D), lambda i: (i, 0)), pl.BlockSpec((D, D), lambda i: (0, 0)),
                  pl.BlockSpec((ts, D), lambda i: (i, 0)), pl.BlockSpec((1, D), lambda i: (0, 0))],
        out_specs=[pl.BlockSpec((ts, D), lambda i: (i, 0)), pl.BlockSpec((ts, D), lambda i: (i, 0))],
        out_shape=[jax.ShapeDtypeStruct((s, D), F32), jax.ShapeDtypeStruct((s, D), BF16)],
        compiler_params=_params("parallel"),
    )(y, w, xres, gnext)


CONV_ROWS = 32


def _rolled_copies(dst_ref, xe, back):
    n = xe.shape[0]
    dst_ref[0] = xe
    for r in range(1, 8):
        dst_ref[r] = pltpu.roll(xe, n - r if back else r, 0)


def _conv31(rolled_ref, cw_ref, ts, out_ref, bias):
    for o in range(0, ts, CONV_ROWS):
        acc = jnp.zeros((CONV_ROWS, DA), F32) + bias
        for sh in range(BCONV):
            q, r = divmod(sh, 8)
            lo = HALO_B - 8 * q + o
            acc = acc + cw_ref[BCONV - 1 - sh:BCONV - sh, :] * rolled_ref[r, lo:lo + CONV_ROWS, :]
        out_ref[o:o + CONV_ROWS, :] = acc


def _ab_fwd(z, lga, lba, wsm, bs_col, cwb, cbb, lgb, lbb, name):
    s = z.shape[0]
    ts = _tile(s, 256)
    hb = ts // HALO_B

    def body(z_ref, zh_ref, lga_ref, lba_ref, ws_ref, bs_ref, cw_ref, cb_ref, lgb_ref, lbb_ref,
             y_ref, yb2_ref, rolled):
        i = pl.program_id(0)
        z_t = z_ref[...].astype(F32)
        gu = _gelu(z_t[:, 0:DA])
        gv = _gelu(z_t[:, DA:2 * DA])
        vn, _, _ = _ln_fwd(gv, lga_ref[...], lba_ref[...])
        vnb = vn.astype(BF16)
        for c in range(ts // CHUNK):
            for h in range(HEADS):
                rs = slice(c * CHUNK, (c + 1) * CHUNK)
                cs = slice(h * CHUNK, (h + 1) * CHUNK)
                mixed = jnp.dot(ws_ref[h], vnb[rs, cs], preferred_element_type=F32) + bs_ref[h]
                y_ref[rs, cs] = (gu[rs, cs] * mixed).astype(BF16)
        zh = jnp.where(i > 0, zh_ref[...], jnp.zeros_like(zh_ref[...])).astype(F32)
        xb = jnp.concatenate([zh[:, 0:DA], z_t[:, 2 * DA:3 * DA]], axis=0)
        gb = jnp.concatenate([zh[:, DA:2 * DA], z_t[:, 3 * DA:4 * DA]], axis=0)
        _rolled_copies(rolled, xb * _sigmoid(gb), False)
        _conv31(rolled, cw_ref, ts, yb2_ref, cb_ref[...])
        nb_, _, _ = _ln_fwd(yb2_ref[...], lgb_ref[...], lbb_ref[...])
        y_ref[:, DA:2 * DA] = (nb_ * _sigmoid(nb_)).astype(BF16)

    row = lambda i: (0, 0)
    return pl.pallas_call(
        body, grid=(s // ts,), name=name,
        in_specs=[pl.BlockSpec((ts, 4 * DA), lambda i: (i, 0)),
                  pl.BlockSpec((HALO_B, 2 * DA), lambda i: (jnp.maximum(i * hb - 1, 0), 1)),
                  pl.BlockSpec((1, DA), row), pl.BlockSpec((1, DA), row),
                  pl.BlockSpec((HEADS, CHUNK, CHUNK), lambda i: (0, 0, 0)),
                  pl.BlockSpec((HEADS, CHUNK, 1), lambda i: (0, 0, 0)),
                  pl.BlockSpec((BCONV, DA), row), pl.BlockSpec((1, DA), row),
                  pl.BlockSpec((1, DA), row), pl.BlockSpec((1, DA), row)],
        out_specs=[pl.BlockSpec((ts, 2 * DA), lambda i: (i, 0)), pl.BlockSpec((ts, DA), lambda i: (i, 0))],
        out_shape=[jax.ShapeDtypeStruct((s, 2 * DA), BF16), jax.ShapeDtypeStruct((s, DA), F32)],
        scratch_shapes=[pltpu.VMEM((8, ts + HALO_B, DA), F32)],
        compiler_params=_params("parallel"),
    )(z, z, lga, lba, wsm, bs_col, cwb, cbb, lgb, lbb)


def _c_fwd(zc, cw, w, xres, gnext, name):
    s = zc.shape[0]
    ts = _tile(s, 512)
    hb = ts // HALO

    def body(z_ref, ch_ref, xh_ref, cw_ref, w_ref, x_ref, g_ref, r_ref, o_ref, h_ref):
        i = pl.program_id(0)
        z_t = z_ref[...].astype(F32)
        ph = jnp.where(i > 0, ch_ref[...].astype(F32) * xh_ref[...].astype(F32), 0.0)
        pe = jnp.concatenate([ph, z_t[:, D:2 * D] * z_t[:, 2 * D:3 * D]], axis=0)
        q, _ = _conv3(pe, [cw_ref[k:k + 1, :] for k in range(3)], HALO)
        r = (z_t[:, 0:D] * q).astype(BF16)
        r_ref[...] = r
        xn = x_ref[...] + jnp.dot(r, w_ref[...], preferred_element_type=F32)
        o_ref[...] = xn
        h_ref[...] = _rms_math(xn, g_ref[...])

    halo = lambda col: pl.BlockSpec((HALO, D), lambda i: (jnp.maximum(i * hb - 1, 0), col))
    tile = pl.BlockSpec((ts, D), lambda i: (i, 0))
    return pl.pallas_call(
        body, grid=(s // ts,), name=name,
        in_specs=[pl.BlockSpec((ts, 3 * D), lambda i: (i, 0)), halo(1), halo(2),
                  pl.BlockSpec((3, D), lambda i: (0, 0)), pl.BlockSpec((D, D), lambda i: (0, 0)), tile,
                  pl.BlockSpec((1, D), lambda i: (0, 0))],
        out_specs=[tile, tile, tile],
        out_shape=[jax.ShapeDtypeStruct((s, D), BF16), jax.ShapeDtypeStruct((s, D), F32),
                   jax.ShapeDtypeStruct((s, D), BF16)],
        compiler_params=_params("parallel"),
    )(zc, zc, zc, cw, w, xres, gnext)


def _final_math(xv, tv, gv):
    r = lax.rsqrt(jnp.mean(xv * xv, axis=-1, keepdims=True) + EPS)
    xhat = xv * r
    e = xhat * gv - tv
    part = 0.5 * jnp.sum(jnp.mean(e * e, axis=-1, keepdims=True), axis=0, keepdims=True)
    dy = e * (1.0 / D)
    dgp = jnp.sum(dy * xhat, axis=0, keepdims=True)
    u = dy * gv
    dx = r * (u - xhat * jnp.mean(u * xhat, axis=-1, keepdims=True))
    return dx, dgp, jnp.broadcast_to(part, (1, 128))


def _ffn_fwd(h, xres, wup, fcw, wdn, gnext, name, final=None):
    s = h.shape[0]
    ts = _tile(s, 512)
    hb = ts // HALO

    def body(h_ref, hh_ref, w_ref, cw_ref, wd_ref, x_ref, *rest):
        if final is not None:
            t_ref, gf_ref, up_ref, upc_ref, dx_ref, dxb_ref, dg_ref, loss_ref, up_s, xo_ref = rest
        elif gnext is not None:
            gn_ref, up_ref, upc_ref, xo_ref, hn_ref, up_s = rest
        else:
            up_ref, upc_ref, xo_ref, up_s = rest
        i = pl.program_id(0)
        m = pl.program_id(1)
        @pl.when(m == 0)
        def _():
            xo_ref[...] = x_ref[...]

        halo = jnp.where(i > 0, hh_ref[...], jnp.zeros_like(hh_ref[...]))
        hx = jnp.concatenate([halo, h_ref[...]], axis=0)
        acts = []
        for gv in range(2):
            up_s[gv] = jnp.dot(hx, w_ref[gv], preferred_element_type=F32)
            x0 = up_s[gv, HALO:HALO + ts, :]
            up_ref[gv] = x0.astype(BF16)
            upc = (cw_ref[gv, 2:3, :] * x0 + cw_ref[gv, 1:2, :] * up_s[gv, HALO - 1:HALO - 1 + ts, :]
                   + cw_ref[gv, 0:1, :] * up_s[gv, HALO - 2:HALO - 2 + ts, :])
            upc_ref[gv] = upc.astype(BF16)
            acts.append(upc)
        a = acts[0] * _sigmoid(acts[0]) * acts[1]
        xo_ref[...] += jnp.dot(a.astype(BF16), wd_ref[...], preferred_element_type=F32)

        if final is not None:
            @pl.when(m == NG - 1)
            def _():
                dx, dgp, part = _final_math(xo_ref[...], t_ref[...], gf_ref[...])
                dx_ref[...] = dx
                dxb_ref[...] = dx.astype(BF16)

                @pl.when(i == 0)
                def _():
                    dg_ref[...] = dgp
                    loss_ref[...] = part

                @pl.when(i > 0)
                def _():
                    dg_ref[...] += dgp
                    loss_ref[...] += part

        elif gnext is not None:
            @pl.when(m == NG - 1)
            def _():
                hn_ref[...] = _rms_math(xo_ref[...], gn_ref[...])

    tile = pl.BlockSpec((ts, D), lambda i, m: (i, 0))
    row = lambda n: pl.BlockSpec((1, n), lambda i, m: (0, 0))
    scratch = [pltpu.VMEM((2, ts + HALO, FB), F32)]
    if final is not None:
        more_in, more_ops = [tile, row(D)], list(final)
        more_out = [tile, tile, row(D), row(128)]
        more_shape = [jax.ShapeDtypeStruct((s, D), F32), jax.ShapeDtypeStruct((s, D), BF16),
                      jax.ShapeDtypeStruct((1, D), F32), jax.ShapeDtypeStruct((1, 128), F32)]
        scratch.append(pltpu.VMEM((ts, D), F32))
    else:
        nxt = gnext is not None
        more_in, more_ops = ([row(D)], [gnext]) if nxt else ([], [])
        more_out = [tile] + ([tile] if nxt else [])
        more_shape = [jax.ShapeDtypeStruct((s, D), F32)] + ([jax.ShapeDtypeStruct((s, D), BF16)] if nxt else [])
    return pl.pallas_call(
        body, grid=(s // ts, NG), name=name,
        in_specs=[tile,
                  pl.BlockSpec((HALO, D), lambda i, m: (jnp.maximum(i * hb - 1, 0), 0)),
                  pl.BlockSpec((2, None, D, FB), lambda i, m: (0, m, 0, 0)),
                  pl.BlockSpec((2, None, 3, FB), lambda i, m: (0, m, 0, 0)),
                  pl.BlockSpec((FB, D), lambda i, m: (m, 0)),
                  tile] + more_in,
        out_specs=[pl.BlockSpec((None, 2, ts, FB), lambda i, m: (m, 0, i, 0)),
                   pl.BlockSpec((None, 2, ts, FB), lambda i, m: (m, 0, i, 0))] + more_out,
        out_shape=[jax.ShapeDtypeStruct((NG, 2, s, FB), BF16), jax.ShapeDtypeStruct((NG, 2, s, FB), BF16)] + more_shape,
        scratch_shapes=scratch,
        compiler_params=_params("arbitrary", "arbitrary"),
    )(h, h, wup, fcw, wdn, xres, *more_ops)


def _ffn_bwd(df, up, upc, wup, fcw, wdn, xin, g, name):
    s = df.shape[0]
    ts = _tile(s, 512)
    nt = s // ts

    def body(df_ref, up_ref, upc_ref, w_ref, cw_ref, wd_ref, x_ref, g_ref,
             a_ref, dup_ref, dx_ref, dxb_ref, dg_ref, dcw_ref, carry, acc, tacc, dcs_ref):
        i = pl.program_id(0)
        m = pl.program_id(1)
        first = i == 0
        @pl.when(first)
        def _():
            carry[m] = jnp.zeros((2, 8, FB), F32)
            dcw_ref[m] = jnp.zeros((2, 3, FB), F32)

        @pl.when(m == 0)
        def _():
            acc[...] = jnp.zeros((ts, D), F32)

        cws = [[cw_ref[gv, k:k + 1, :] for k in range(3)] for gv in range(2)]
        part = ts // NPART
        das = [lax.dot_general(df_ref[p * part:(p + 1) * part, :].astype(BF16), wd_ref[...], NT_DIMS,
                               preferred_element_type=F32) for p in range(NPART)]

        tacc[...] = jnp.zeros((2, 3, 8, FB), F32)
        dcs_ref[:, ts:ts + 8, :] = carry[m]
        for r in reversed(range(ts // RC)):
            rs = slice(r * RC, (r + 1) * RC)
            gate = upc_ref[0, rs, :].astype(F32)
            val = upc_ref[1, rs, :].astype(F32)
            sg = _sigmoid(gate)
            sl = gate * sg
            a_ref[rs, :] = (sl * val).astype(BF16)
            da_c = das[(r * RC) // part][(r * RC) % part:(r * RC) % part + RC]
            dcs = [da_c * val * _dsilu(gate, sg), da_c * sl]
            for gv in range(2):
                dc = dcs[gv]
                dcs_ref[gv, rs, :] = dc
                d1 = dcs_ref[gv, r * RC + 1:(r + 1) * RC + 1, :]
                d2 = dcs_ref[gv, r * RC + 2:(r + 1) * RC + 2, :]
                du = cws[gv][2] * dc + cws[gv][1] * d1 + cws[gv][0] * d2
                dup_ref[gv, rs, :] = du.astype(BF16)
                x0 = up_ref[gv, rs, :].astype(F32)
                for k, dk in enumerate((d2, d1, dc)):
                    p = x0 * dk
                    tacc[gv, k] += sum(p[j:j + 8] for j in range(0, RC, 8))
            if (r * RC) % part == 0:
                ps = slice(r * RC, r * RC + part)
                acc[ps, :] += (
                    lax.dot_general(dup_ref[0, ps, :], w_ref[0], NT_DIMS, preferred_element_type=F32)
                    + lax.dot_general(dup_ref[1, ps, :], w_ref[1], NT_DIMS, preferred_element_type=F32))
        for gv in range(2):
            carry[m, gv] = dcs_ref[gv, 0:8, :]
            for k in range(3):
                dcw_ref[m, gv, k:k + 1, :] += jnp.sum(tacc[gv, k], axis=0, keepdims=True)

        @pl.when(m == NG - 1)
        def _():
            dx, dgp = _rms_bwd_math(acc[...], x_ref[...], g_ref[...])
            dx = df_ref[...] + dx
            dx_ref[...] = dx
            dxb_ref[...] = dx.astype(BF16)

            @pl.when(first)
            def _():
                dg_ref[...] = dgp

            @pl.when(jnp.logical_not(first))
            def _():
                dg_ref[...] += dgp

    rev = lambda i: nt - 1 - i
    return pl.pallas_call(
        body, grid=(nt, NG), name=name,
        in_specs=[pl.BlockSpec((ts, D), lambda i, m: (rev(i), 0)),
                  pl.BlockSpec((None, 2, ts, FB), lambda i, m: (m, 0, rev(i), 0)),
                  pl.BlockSpec((None, 2, ts, FB), lambda i, m: (m, 0, rev(i), 0)),
                  pl.BlockSpec((2, None, D, FB), lambda i, m: (0, m, 0, 0)),
                  pl.BlockSpec((2, None, 3, FB), lambda i, m: (0, m, 0, 0)),
                  pl.BlockSpec((FB, D), lambda i, m: (m, 0)),
                  pl.BlockSpec((ts, D), lambda i, m: (rev(i), 0)),
                  pl.BlockSpec((1, D), lambda i, m: (0, 0))],
        out_specs=[pl.BlockSpec((None, ts, FB), lambda i, m: (m, rev(i), 0)),
                   pl.BlockSpec((None, 2, ts, FB), lambda i, m: (m, 0, rev(i), 0)),
                   pl.BlockSpec((ts, D), lambda i, m: (rev(i), 0)),
                   pl.BlockSpec((ts, D), lambda i, m: (rev(i), 0)),
                   pl.BlockSpec((1, D), lambda i, m: (0, 0)),
                   pl.BlockSpec((NG, 2, 3, FB), lambda i, m: (0, 0, 0, 0))],
        out_shape=[jax.ShapeDtypeStruct((NG, s, FB), BF16), jax.ShapeDtypeStruct((NG, 2, s, FB), BF16),
                   jax.ShapeDtypeStruct((s, D), F32), jax.ShapeDtypeStruct((s, D), BF16),
                   jax.ShapeDtypeStruct((1, D), F32),
                   jax.ShapeDtypeStruct((NG, 2, 3, FB), F32)],
        scratch_shapes=[pltpu.VMEM((NG, 2, 8, FB), F32), pltpu.VMEM((ts, D), F32),
                        pltpu.VMEM((2, 3, 8, FB), F32), pltpu.VMEM((2, ts + 8, FB), F32)],
        compiler_params=_params("arbitrary", "arbitrary"),
    )(df, up, upc, wup, fcw, wdn, xin, g)


def _mm_nt_rms(dy, wblk, x, g, dres, bf16_copy, name):
    s = dy.shape[0]
    nb, _, bn = wblk.shape
    pair = _pair(bn)
    ts = _tile(s, 512)

    def body(dy_ref, w_ref, x_ref, g_ref, dr_ref, dx_ref, *rest):
        dg_ref = rest[-1]
        i = pl.program_id(0)
        acc = jnp.zeros((ts, D), F32)
        for b in range(0, nb, pair):
            acc = acc + lax.dot_general(dy_ref[:, b * bn:(b + pair) * bn], _cols(w_ref, b, pair), NT_DIMS,
                                        preferred_element_type=F32)
        dx, dgp = _rms_bwd_math(acc, x_ref[...], g_ref[...])
        dx = dr_ref[...] + dx
        dx_ref[...] = dx
        if bf16_copy:
            rest[0][...] = dx.astype(BF16)

        @pl.when(i == 0)
        def _():
            dg_ref[...] = dgp

        @pl.when(i > 0)
        def _():
            dg_ref[...] += dgp

    tile = pl.BlockSpec((ts, D), lambda i: (i, 0))
    return pl.pallas_call(
        body, grid=(s // ts,), name=name,
        in_specs=[pl.BlockSpec((ts, nb * bn), lambda i: (i, 0)), pl.BlockSpec((nb, D, bn), lambda i: (0, 0, 0)),
                  tile, pl.BlockSpec((1, D), lambda i: (0, 0)), tile],
        out_specs=[tile] + ([tile] if bf16_copy else []) + [pl.BlockSpec((1, D), lambda i: (0, 0))],
        out_shape=[jax.ShapeDtypeStruct((s, D), F32)] + ([jax.ShapeDtypeStruct((s, D), BF16)] if bf16_copy else [])
        + [jax.ShapeDtypeStruct((1, D), F32)],
        compiler_params=_params("arbitrary"),
    )(dy, wblk, x, g, dres)


def _c_bwd(dx, w, zc, cw, name):
    s = dx.shape[0]
    ts = _tile(s, 512)
    nt = s // ts
    hb = ts // HALO

    def body(dx_ref, dxf_ref, w_ref, z_ref, ch_ref, xh_ref, bf_ref, cw_ref, dz_ref, dcw_ref):
        i = pl.program_id(0)
        cwv = [cw_ref[k:k + 1, :] for k in range(3)]
        dre = lax.dot_general(jnp.concatenate([dx_ref[...], dxf_ref[...]], axis=0), w_ref[...], NT_DIMS,
                              preferred_element_type=F32).astype(BF16).astype(F32)
        z_t = z_ref[...].astype(F32)
        bg, cg, xv = z_t[:, 0:D], z_t[:, D:2 * D], z_t[:, 2 * D:3 * D]
        ph = jnp.where(i > 0, ch_ref[...].astype(F32) * xh_ref[...].astype(F32), 0.0)
        pe = jnp.concatenate([ph, cg * xv], axis=0)
        q, taps = _conv3(pe, cwv, HALO)
        drv = dre[0:ts]
        dq = drv * bg
        dqf = jnp.where(i < nt - 1, dre[ts:ts + HALO] * bf_ref[...].astype(F32), 0.0)
        dp = _conv3_bwd_in(jnp.concatenate([dq, dqf], axis=0), cwv, ts)
        dz_ref[:, 0:D] = (drv * q).astype(BF16)
        dz_ref[:, D:2 * D] = (dp * xv).astype(BF16)
        dz_ref[:, 2 * D:3 * D] = (dp * cg).astype(BF16)
        rows = _conv3_bwd_w(dq, taps)

        @pl.when(i == 0)
        def _():
            for k in range(3):
                dcw_ref[k:k + 1, :] = rows[k]

        @pl.when(i > 0)
        def _():
            for k in range(3):
                dcw_ref[k:k + 1, :] += rows[k]

    past = lambda col: pl.BlockSpec((HALO, D), lambda i: (jnp.maximum(i * hb - 1, 0), col))
    nxt = lambda i: jnp.minimum((i + 1) * hb, s // HALO - 1)
    return pl.pallas_call(
        body, grid=(nt,), name=name,
        in_specs=[pl.BlockSpec((ts, D), lambda i: (i, 0)),
                  pl.BlockSpec((HALO, D), lambda i: (nxt(i), 0)),
                  pl.BlockSpec((D, D), lambda i: (0, 0)),
                  pl.BlockSpec((ts, 3 * D), lambda i: (i, 0)), past(1), past(2),
                  pl.BlockSpec((HALO, D), lambda i: (nxt(i), 0)),
                  pl.BlockSpec((3, D), lambda i: (0, 0))],
        out_specs=[pl.BlockSpec((ts, 3 * D), lambda i: (i, 0)), pl.BlockSpec((3, D), lambda i: (0, 0))],
        out_shape=[jax.ShapeDtypeStruct((s, 3 * D), BF16), jax.ShapeDtypeStruct((3, D), F32)],
        compiler_params=_params("arbitrary"),
    )(dx, dx, w, zc, zc, zc, zc, cw)


G512_ROWS = 40


def _ab_bwd(dx, w, z, yb2, lga, lba, wsm, bs_col, cwb, lgb, lbb, name):
    s = z.shape[0]
    ts = _tile(s, 256)
    nt = s // ts
    hb = ts // HALO_B
    nch = ts // CHUNK

    def body(z_ref, zh_ref, dx_ref, dxf_ref, w_ref, yb2_ref, yb2f_ref, lga_ref, lba_ref, ws_ref, bs_ref,
             cw_ref, lgb_ref, lbb_ref, dz_ref, g512_ref, dws_ref, dbs_ref, dvn_ref, fwd_rolled, bwd_rolled, du_s):
        i = pl.program_id(0)
        last = i == nt - 1

        @pl.when(i == 0)
        def _():
            g512_ref[...] = jnp.zeros((G512_ROWS, DA), F32)
            dws_ref[...] = jnp.zeros((HEADS, CHUNK, CHUNK), F32)
            dbs_ref[...] = jnp.zeros((HEADS, CHUNK, 1), F32)

        def add_row(k, v):
            g512_ref[k:k + 1, :] += v

        z_t = z_ref[...].astype(F32)
        nt_dot = lambda a, b: lax.dot_general(a, b, NT_DIMS, preferred_element_type=F32).astype(BF16).astype(F32)
        dy_t = nt_dot(dx_ref[...], w_ref[...])
        dyf = nt_dot(dxf_ref[...], w_ref[DA:2 * DA, :])
        ua, va = z_t[:, 0:DA], z_t[:, DA:2 * DA]
        gu = _gelu(ua)
        gv = _gelu(va)
        lga_v = lga_ref[...]
        vn, xhat_a, rstd_a = _ln_fwd(gv, lga_v, lba_ref[...])
        vnb = vn.astype(BF16)
        causal = (lax.broadcasted_iota(jnp.int32, (CHUNK, CHUNK), 0)
                  >= lax.broadcasted_iota(jnp.int32, (CHUNK, CHUNK), 1)).astype(F32)
        for c in range(nch):
            for h in range(HEADS):
                rs = slice(c * CHUNK, (c + 1) * CHUNK)
                cs = slice(h * CHUNK, (h + 1) * CHUNK)
                vblk = vnb[rs, cs]
                mixed = jnp.dot(ws_ref[h], vblk, preferred_element_type=F32) + bs_ref[h]
                dyb_ = dy_t[rs, cs]
                dmix = dyb_ * gu[rs, cs]
                dmb = dmix.astype(BF16)
                dz_ref[rs, cs] = (dyb_ * mixed * _dgelu(ua[rs, cs])).astype(BF16)
                dvn_ref[rs, cs] = lax.dot_general(ws_ref[h], dmb, TN_DIMS, preferred_element_type=F32)
                dws_ref[h] += causal * lax.dot_general(dmb, vblk, NT_DIMS, preferred_element_type=F32)
                dbs_ref[h] += jnp.sum(dmix, axis=1, keepdims=True)
        dvn = dvn_ref[...]
        add_row(0, jnp.sum(dvn * xhat_a, axis=0, keepdims=True))
        add_row(1, jnp.sum(dvn, axis=0, keepdims=True))
        dgv = _ln_bwd(dvn, xhat_a, rstd_a, lga_v)
        dz_ref[:, DA:2 * DA] = (dgv * _dgelu(va)).astype(BF16)
        lgb_v = lgb_ref[...]
        dyb_e = jnp.concatenate(
            [dy_t[:, DA:2 * DA], jnp.where(last, 0.0, dyf)], axis=0)
        yb2_e = jnp.concatenate([yb2_ref[...], jnp.where(last, 0.0, yb2f_ref[...])], axis=0)
        n_e, xhat_b, rstd_b = _ln_fwd(yb2_e, lgb_v, lbb_ref[...])
        sgn = _sigmoid(n_e)
        dn = dyb_e * _dsilu(n_e, sgn)
        dy2 = _ln_bwd(dn, xhat_b, rstd_b, lgb_v)
        add_row(2, jnp.sum(dy2[:ts], axis=0, keepdims=True))
        add_row(3, jnp.sum(dn[:ts] * xhat_b[:ts], axis=0, keepdims=True))
        add_row(4, jnp.sum(dn[:ts], axis=0, keepdims=True))
        zh = jnp.where(i > 0, zh_ref[...], jnp.zeros_like(zh_ref[...])).astype(F32)
        xb_t, gb_t = z_t[:, 2 * DA:3 * DA], z_t[:, 3 * DA:4 * DA]
        sgb = _sigmoid(gb_t)
        _rolled_copies(fwd_rolled, jnp.concatenate(
            [zh[:, 0:DA] * _sigmoid(zh[:, DA:2 * DA]), xb_t * sgb], axis=0), False)
        _rolled_copies(bwd_rolled, dy2, True)
        for o in range(0, ts, CONV_ROWS):
            acc = jnp.zeros((CONV_ROWS, DA), F32)
            for sh in range(BCONV):
                q, r = divmod(sh, 8)
                acc = acc + cw_ref[BCONV - 1 - sh:BCONV - sh, :] * bwd_rolled[r, 8 * q + o:8 * q + o + CONV_ROWS, :]
            du_s[o:o + CONV_ROWS, :] = acc
        for sh in range(BCONV):
            q, r = divmod(sh, 8)
            acc = jnp.zeros((CONV_ROWS, DA), F32)
            for o in range(0, ts, CONV_ROWS):
                lo = HALO_B - 8 * q + o
                acc = acc + bwd_rolled[0, o:o + CONV_ROWS, :] * fwd_rolled[r, lo:lo + CONV_ROWS, :]
            add_row(8 + BCONV - 1 - sh, jnp.sum(acc, axis=0, keepdims=True))
        du = du_s[...]
        dz_ref[:, 2 * DA:3 * DA] = (du * sgb).astype(BF16)
        dz_ref[:, 3 * DA:4 * DA] = (du * xb_t * sgb * (1.0 - sgb)).astype(BF16)

    row = lambda i: (0, 0)
    nxt = lambda i: jnp.minimum((i + 1) * hb, s // HALO_B - 1)
    return pl.pallas_call(
        body, grid=(nt,), name=name,
        in_specs=[pl.BlockSpec((ts, 4 * DA), lambda i: (i, 0)),
                  pl.BlockSpec((HALO_B, 2 * DA), lambda i: (jnp.maximum(i * hb - 1, 0), 1)),
                  pl.BlockSpec((ts, D), lambda i: (i, 0)),
                  pl.BlockSpec((HALO_B, D), lambda i: (nxt(i), 0)),
                  pl.BlockSpec((D, D), lambda i: (0, 0)),
                  pl.BlockSpec((ts, DA), lambda i: (i, 0)),
                  pl.BlockSpec((HALO_B, DA), lambda i: (nxt(i), 0)),
                  pl.BlockSpec((1, DA), row), pl.BlockSpec((1, DA), row),
                  pl.BlockSpec((HEADS, CHUNK, CHUNK), lambda i: (0, 0, 0)),
                  pl.BlockSpec((HEADS, CHUNK, 1), lambda i: (0, 0, 0)),
                  pl.BlockSpec((BCONV, DA), row), pl.BlockSpec((1, DA), row), pl.BlockSpec((1, DA), row)],
        out_specs=[pl.BlockSpec((ts, 4 * DA), lambda i: (i, 0)),
                   pl.BlockSpec((G512_ROWS, DA), row),
                   pl.BlockSpec((HEADS, CHUNK, CHUNK), lambda i: (0, 0, 0)),
                   pl.BlockSpec((HEADS, CHUNK, 1), lambda i: (0, 0, 0))],
        out_shape=[jax.ShapeDtypeStruct((s, 4 * DA), BF16), jax.ShapeDtypeStruct((G512_ROWS, DA), F32),
                   jax.ShapeDtypeStruct((HEADS, CHUNK, CHUNK), F32),
                   jax.ShapeDtypeStruct((HEADS, CHUNK, 1), F32)],
        scratch_shapes=[pltpu.VMEM((ts, DA), F32), pltpu.VMEM((8, ts + HALO_B, DA), F32),
                        pltpu.VMEM((8, ts + HALO_B, DA), F32), pltpu.VMEM((ts, DA), F32)],
        compiler_params=_params("arbitrary"),
    )(z, z, dx, dx, w, yb2, yb2, lga, lba, wsm, bs_col, cwb, lgb, lbb)


def _dw_cols(a, dy, nb, bn, name):
    s = a.shape[0]
    tm = _tile(s, 2048)
    nt = s // tm
    cpb = 4

    def body(a_ref, dy_ref, o_ref, acc):
        t = pl.program_id(1)
        p = lax.dot_general(a_ref[...], dy_ref[...], TN_DIMS, preferred_element_type=F32)

        @pl.when(t == 0)
        def _():
            for q in range(cpb):
                acc[q] = p[:, q * bn:(q + 1) * bn]

        @pl.when(t > 0)
        def _():
            for q in range(cpb):
                acc[q] += p[:, q * bn:(q + 1) * bn]

        @pl.when(t == nt - 1)
        def _():
            o_ref[...] = acc[...].astype(BF16)

    return pl.pallas_call(
        body, grid=(nb // cpb, nt), name=name,
        in_specs=[pl.BlockSpec((tm, D), lambda j, t: (t, 0)), pl.BlockSpec((tm, cpb * bn), lambda j, t: (t, j))],
        out_specs=pl.BlockSpec((cpb, D, bn), lambda j, t: (j, 0, 0)),
        out_shape=jax.ShapeDtypeStruct((nb, D, bn), BF16),
        scratch_shapes=[pltpu.VMEM((cpb, D, bn), F32)],
        compiler_params=_params("arbitrary", "arbitrary"),
    )(a, dy)


def _dw_rows(a, dy, name):
    s = a.shape[0]
    tm = _tile(s, 4096)
    nt = s // tm
    rb = 512

    def body(a_ref, dy_ref, o_ref, acc):
        t = pl.program_id(1)
        p = lax.dot_general(a_ref[...], dy_ref[...], TN_DIMS, preferred_element_type=F32)

        @pl.when(t == 0)
        def _():
            acc[...] = p

        @pl.when(t > 0)
        def _():
            acc[...] += p

        @pl.when(t == nt - 1)
        def _():
            o_ref[...] = acc[...].astype(BF16)

    return pl.pallas_call(
        body, grid=(D // rb, nt), name=name,
        in_specs=[pl.BlockSpec((tm, rb), lambda j, t: (t, j)), pl.BlockSpec((tm, D), lambda j, t: (t, 0))],
        out_specs=pl.BlockSpec((rb, D), lambda j, t: (j, 0)),
        out_shape=jax.ShapeDtypeStruct((D, D), BF16),
        scratch_shapes=[pltpu.VMEM((rb, D), F32)],
        compiler_params=_params("arbitrary", "arbitrary"),
    )(a, dy)


def _dw_up(h, dup, name):
    s = h.shape[0]
    tm = _tile(s, 4096)
    nt = s // tm

    def body(h_ref, d_ref, o_ref, acc):
        t = pl.program_id(1)
        p = lax.dot_general(d_ref[...], h_ref[...], TN_DIMS, preferred_element_type=F32)

        @pl.when(t == 0)
        def _():
            acc[...] = p

        @pl.when(t > 0)
        def _():
            acc[...] += p

        @pl.when(t == nt - 1)
        def _():
            o_ref[...] = acc[...].astype(BF16)

    return pl.pallas_call(
        body, grid=(NDEV, nt), name=name,
        in_specs=[pl.BlockSpec((tm, D), lambda b, t: (t, 0)),
                  pl.BlockSpec((None, None, tm, FB), lambda b, t: (b % NG, b // NG, t, 0))],
        out_specs=pl.BlockSpec((None, FB, D), lambda b, t: (b, 0, 0)),
        out_shape=jax.ShapeDtypeStruct((NDEV, FB, D), BF16),
        scratch_shapes=[pltpu.VMEM((FB, D), F32)],
        compiler_params=_params("arbitrary", "arbitrary"),
    )(h, dup)


def _dw_dn(a, df, name):
    s = df.shape[0]
    tm = _tile(s, 4096)
    nt = s // tm

    def body(a_ref, d_ref, o_ref, acc):
        t = pl.program_id(1)
        p = lax.dot_general(a_ref[...], d_ref[...], TN_DIMS, preferred_element_type=F32)

        @pl.when(t == 0)
        def _():
            acc[...] = p

        @pl.when(t > 0)
        def _():
            acc[...] += p

        @pl.when(t == nt - 1)
        def _():
            o_ref[...] = acc[...].astype(BF16)

    return pl.pallas_call(
        body, grid=(NG, nt), name=name,
        in_specs=[pl.BlockSpec((None, tm, FB), lambda m, t: (m, t, 0)), pl.BlockSpec((tm, D), lambda m, t: (t, 0))],
        out_specs=pl.BlockSpec((FB, D), lambda m, t: (m, 0)),
        out_shape=jax.ShapeDtypeStruct((DFF, D), BF16),
        scratch_shapes=[pltpu.VMEM((FB, D), F32)],
        compiler_params=_params("arbitrary", "arbitrary"),
    )(a, df)


def _place():
    x, y, c = lax.axis_index("x"), lax.axis_index("y"), lax.axis_index("c")
    chips = [(1 - x, y), (x, 1 - y), (1 - x, 1 - y)]
    return x, y, c, chips


def _zone(shard, dev):
    return lax.dynamic_update_slice(lax.empty((NDEV,) + shard.shape, shard.dtype), shard[None],
                                    (dev,) + (0,) * shard.ndim)


HBM_SPEC = pl.BlockSpec(memory_space=pltpu.HBM)
SEM_SPEC = pl.BlockSpec(memory_space=pltpu.SEMAPHORE)
DATAFLOW = pltpu.SideEffectType.DATAFLOW_SIDE_EFFECTING


def _hbm(a):
    return pltpu.with_memory_space_constraint(a, pltpu.HBM)


def _hbm_like(arrs):
    return [pltpu.HBM(a.shape, a.dtype) for a in arrs]


def _ag_start(srcs, lands, after, name):
    n = len(srcs)
    ns = 8 * n

    def body(*refs):
        src, land = refs[:n], refs[n:2 * n]
        sems = refs[2 * n + 1:2 * n + 1 + ns]
        token = refs[-1]
        x, y, c, chips = _place()
        peers = [(x, y, 1 - c)] + [(*chip, c) for chip in chips]
        for t in range(n):
            for k, to in enumerate(peers):
                pltpu.make_async_remote_copy(
                    src_ref=src[t], dst_ref=land[t].at[4 * x + 2 * y + c],
                    send_sem=sems[2 * (4 * t + k)], recv_sem=sems[2 * (4 * t + k) + 1],
                    device_id=to, device_id_type=MESH).start()
        token[...] = jnp.zeros_like(token)

    res = pl.pallas_call(
        body, name=name,
        in_specs=[HBM_SPEC] * (2 * n) + [ANY],
        out_specs=[SEM_SPEC] * ns + [HBM_SPEC] * (2 * n) + [pl.BlockSpec(memory_space=pltpu.VMEM)],
        out_shape=[pltpu.SemaphoreType.DMA(())] * ns + _hbm_like(srcs) + _hbm_like(lands)
        + [jax.ShapeDtypeStruct((8, 128), F32)],
        input_output_aliases={i: ns + i for i in range(2 * n)},
        compiler_params=pltpu.CompilerParams(has_side_effects=DATAFLOW),
    )(*[_hbm(a) for a in srcs], *[_hbm(a) for a in lands], after)
    sems = [[(res[2 * (4 * t + k)], res[2 * (4 * t + k) + 1]) for k in range(4)] for t in range(n)]
    return sems, res[ns:ns + n], res[ns + n:ns + 2 * n], res[-1]


def _ag_forward(srcs, lands, sems1, after, name):
    n = len(srcs)
    flat1 = [s for t in range(n) for k in range(1, 4) for s in sems1[t][k]]
    n1 = len(flat1)

    def body(*refs):
        src, land = refs[:n], refs[n:2 * n]
        s1 = refs[2 * n:2 * n + n1]
        s2 = refs[2 * n + n1 + 1:2 * n + n1 + 1 + 6 * n]
        x, y, c, chips = _place()
        for j, (cx, cy) in enumerate(chips):
            for t in range(n):
                blk = land[t].at[4 * cx + 2 * cy + c]
                pltpu.make_async_remote_copy(
                    src_ref=src[t], dst_ref=blk, send_sem=s1[2 * (3 * t + j)], recv_sem=s1[2 * (3 * t + j) + 1],
                    device_id=(cx, cy, c), device_id_type=MESH).wait_recv()
                pltpu.make_async_remote_copy(
                    src_ref=blk, dst_ref=blk, send_sem=s2[2 * (3 * t + j)], recv_sem=s2[2 * (3 * t + j) + 1],
                    device_id=(x, y, 1 - c), device_id_type=MESH).start()

    res = pl.pallas_call(
        body, name=name,
        in_specs=[HBM_SPEC] * (2 * n) + [SEM_SPEC] * n1 + [ANY],
        out_specs=[SEM_SPEC] * (6 * n) + [HBM_SPEC] * n,
        out_shape=[pltpu.SemaphoreType.DMA(())] * (6 * n) + _hbm_like(lands),
        input_output_aliases={n + i: 6 * n + i for i in range(n)},
        compiler_params=pltpu.CompilerParams(has_side_effects=DATAFLOW),
    )(*srcs, *lands, *flat1, after)
    sems2 = [[(res[2 * (3 * t + j)], res[2 * (3 * t + j) + 1]) for j in range(3)] for t in range(n)]
    return sems2, res[6 * n:]


def _ag_finish(srcs, lands, sems1, sems2, after, name):
    n = len(srcs)
    flat1 = [s for t in range(n) for k in range(4) for s in sems1[t][k]]
    flat2 = [s for t in range(n) for j in range(3) for s in sems2[t][j]]
    n1, n2 = len(flat1), len(flat2)

    def body(*refs):
        src, land = refs[:n], refs[n:2 * n]
        s1 = refs[2 * n:2 * n + n1]
        s2 = refs[2 * n + n1:2 * n + n1 + n2]
        x, y, c, chips = _place()
        sib = (x, y, 1 - c)
        for t in range(n):
            own = land[t].at[4 * x + 2 * y + 1 - c]
            pltpu.make_async_remote_copy(
                src_ref=src[t], dst_ref=own, send_sem=s1[8 * t], recv_sem=s1[8 * t + 1],
                device_id=sib, device_id_type=MESH).wait_recv()
            for k in range(4):
                pltpu.make_async_remote_copy(
                    src_ref=src[t], dst_ref=own, send_sem=s1[2 * (4 * t + k)], recv_sem=s1[2 * (4 * t + k) + 1],
                    device_id=sib, device_id_type=MESH).wait_send()
            for j, (cx, cy) in enumerate(chips):
                blk = land[t].at[4 * cx + 2 * cy + 1 - c]
                cp = pltpu.make_async_remote_copy(
                    src_ref=blk, dst_ref=blk, send_sem=s2[2 * (3 * t + j)], recv_sem=s2[2 * (3 * t + j) + 1],
                    device_id=sib, device_id_type=MESH)
                cp.wait_send()
                cp.wait_recv()

    return pl.pallas_call(
        body, name=name,
        in_specs=[HBM_SPEC] * (2 * n) + [SEM_SPEC] * (n1 + n2) + [ANY],
        out_specs=[HBM_SPEC] * n,
        out_shape=_hbm_like(lands),
        input_output_aliases={n + i: i for i in range(n)},
        compiler_params=pltpu.CompilerParams(has_side_effects=DATAFLOW),
    )(*srcs, *lands, *flat1, *flat2, after)


def _pair_copies(srcs, dsts, sems):
    x, y, c, _ = _place()
    nt = len(srcs)
    return [pltpu.make_async_remote_copy(
        src_ref=srcs[t].at[2 * j + 1 - c], dst_ref=dsts[t].at[j],
        send_sem=sems[2 * (NCHIP * t + j)], recv_sem=sems[2 * (NCHIP * t + j) + 1],
        device_id=(x, y, 1 - c), device_id_type=MESH) for t in range(nt) for j in range(NCHIP)]


def _pair_start(grads, carry, name):
    nt = len(grads)
    ns = 2 * NCHIP * nt
    zones = [_hbm(lax.empty((NCHIP,) + a.shape[1:], a.dtype)) for a in grads]
    extra = [] if carry is None else [_hbm(carry)]
    ne = len(extra)

    def body(*refs):
        for cp in _pair_copies(refs[:nt], refs[nt:2 * nt], refs[2 * nt + ne:2 * nt + ne + ns]):
            cp.start()

    res = pl.pallas_call(
        body, name=name,
        in_specs=[HBM_SPEC] * (2 * nt + ne),
        out_specs=[SEM_SPEC] * ns + [HBM_SPEC] * (2 * nt + ne),
        out_shape=[pltpu.SemaphoreType.DMA(())] * ns + _hbm_like(grads) + _hbm_like(zones) + _hbm_like(extra),
        input_output_aliases={i: ns + i for i in range(2 * nt + ne)},
        compiler_params=pltpu.CompilerParams(has_side_effects=DATAFLOW),
    )(*[_hbm(a) for a in grads], *zones, *extra)
    handle = (list(res[:ns]), list(res[ns:ns + nt]), list(res[ns + nt:ns + 2 * nt]))
    return handle, (res[ns + 2 * nt] if ne else None)


def _pair_wait(handle, after, name):
    sems, srcs, zones = handle
    nt, ns = len(srcs), len(sems)

    def body(*refs):
        for cp in _pair_copies(refs[:nt], refs[nt:2 * nt], refs[2 * nt:2 * nt + ns]):
            cp.wait_send()
            cp.wait_recv()

    return pl.pallas_call(
        body, name=name,
        in_specs=[HBM_SPEC] * (2 * nt) + [SEM_SPEC] * ns + [ANY],
        out_specs=[HBM_SPEC] * nt,
        out_shape=_hbm_like(zones),
        input_output_aliases={nt + i: i for i in range(nt)},
        compiler_params=pltpu.CompilerParams(has_side_effects=DATAFLOW),
    )(*srcs, *zones, *sems, after)


def _rows_tile(r, row_bytes, cap_bytes):
    best = None
    for tr in range(16, r + 1, 16):
        if r % tr == 0 and tr * row_bytes <= cap_bytes:
            best = tr
    return best if best is not None else r


def _pair_sum(own, got, cidx, name):
    _, _, r, cdim = own.shape
    tr = _rows_tile(r, 2 * cdim, 2 * 1024 * 1024)

    def body(c_ref, a_ref, b_ref, o_ref):
        o_ref[...] = (a_ref[...].astype(F32) + b_ref[...].astype(F32)).astype(BF16)

    return pl.pallas_call(
        body, name=name,
        grid_spec=pltpu.PrefetchScalarGridSpec(
            num_scalar_prefetch=1, grid=(NCHIP, r // tr),
            in_specs=[pl.BlockSpec((None, None, tr, cdim), lambda j, i, c_ref: (j, c_ref[0], i, 0)),
                      pl.BlockSpec((None, tr, cdim), lambda j, i, c_ref: (j, i, 0))],
            out_specs=pl.BlockSpec((None, tr, cdim), lambda j, i, c_ref: (j, i, 0))),
        out_shape=jax.ShapeDtypeStruct((NCHIP, r, cdim), BF16),
        compiler_params=_params("arbitrary", "arbitrary"),
    )(cidx, own, got)


def _chip_copies(srcs, zones, slots, sems):
    x, y, c, chips = _place()
    out = []
    for t, (z, l) in enumerate(slots):
        for k, (cx, cy) in enumerate(chips):
            dst = zones[z].at[k] if l is None else zones[z].at[k, l]
            out.append(pltpu.make_async_remote_copy(
                src_ref=srcs[t].at[2 * cx + cy], dst_ref=dst,
                send_sem=sems[2 * (3 * t + k)], recv_sem=sems[2 * (3 * t + k) + 1],
                device_id=(cx, cy, c), device_id_type=MESH))
    return out


def _chip_start(sums, zones, slots, carry, name):
    nt, nz = len(sums), len(zones)
    ns = 6 * nt
    extra = [] if carry is None else [_hbm(carry)]
    ne = len(extra)

    def body(*refs):
        for cp in _chip_copies(refs[:nt], refs[nt:nt + nz], slots, refs[nt + nz + ne:nt + nz + ne + ns]):
            cp.start()

    res = pl.pallas_call(
        body, name=name,
        in_specs=[HBM_SPEC] * (nt + nz + ne),
        out_specs=[SEM_SPEC] * ns + [HBM_SPEC] * (nt + nz + ne),
        out_shape=[pltpu.SemaphoreType.DMA(())] * ns + _hbm_like(sums) + _hbm_like(zones) + _hbm_like(extra),
        input_output_aliases={i: ns + i for i in range(nt + nz + ne)},
        compiler_params=pltpu.CompilerParams(has_side_effects=DATAFLOW),
    )(*[_hbm(a) for a in sums], *zones, *extra)
    return (list(res[:ns]), list(res[ns:ns + nt]), list(res[ns + nt:ns + nt + nz]),
            (res[ns + nt + nz] if ne else None))


def _chip_wait(started, zones, zone_ids, after, name):
    started = [(sums, [(zone_ids.index(z), l) for z, l in slots], sems) for sums, slots, sems in started]
    nz = len(zones)
    flat_src = [a for sums, _, _ in started for a in sums]
    flat_sem = [s for _, _, sems in started for s in sems]
    n_src, n_sem = len(flat_src), len(flat_sem)

    def body(*refs):
        srcs, zs, sems = refs[:n_src], refs[n_src:n_src + nz], refs[n_src + nz:n_src + nz + n_sem]
        so, se = 0, 0
        for sums, slots, sem_list in started:
            for cp in _chip_copies(srcs[so:so + len(sums)], zs, slots, sems[se:se + len(sem_list)]):
                cp.wait_send()
                cp.wait_recv()
            so += len(sums)
            se += len(sem_list)

    return pl.pallas_call(
        body, name=name,
        in_specs=[HBM_SPEC] * (n_src + nz) + [SEM_SPEC] * n_sem + [ANY],
        out_specs=[HBM_SPEC] * nz,
        out_shape=_hbm_like(zones),
        input_output_aliases={n_src + i: i for i in range(nz)},
        compiler_params=pltpu.CompilerParams(has_side_effects=DATAFLOW),
    )(*flat_src, *zones, *flat_sem, after)


def _small_allreduce(parts, y_first, after, name):
    nt = len(parts)

    def body(*refs):
        srcs, outs, bufs = refs[:nt], refs[nt + 1:2 * nt + 1], refs[2 * nt + 1:3 * nt + 1]
        send_sems, recv_sems = refs[3 * nt + 1:]
        x, y, c, _ = _place()
        along = {"c": (x, y, 1 - c), "x": (1 - x, y, c), "y": (x, 1 - y, c)}
        for t in range(nt):
            outs[t][...] = srcs[t][...]
        for step in range(3):
            order = [("c", "y", "x") if t in y_first else ("c", "x", "y") for t in range(nt)]
            copies = [pltpu.make_async_remote_copy(
                src_ref=outs[t], dst_ref=bufs[t].at[step],
                send_sem=send_sems.at[step, t], recv_sem=recv_sems.at[step, t],
                device_id=along[order[t][step]], device_id_type=MESH) for t in range(nt)]
            for cp in copies:
                cp.start()
            for cp in copies:
                cp.wait()
            for t in range(nt):
                outs[t][...] = outs[t][...] + bufs[t][step]

    vm = pl.BlockSpec(memory_space=pltpu.VMEM)
    return pl.pallas_call(
        body, name=name,
        in_specs=[vm] * nt + [ANY], out_specs=[vm] * nt,
        out_shape=[jax.ShapeDtypeStruct(a.shape, F32) for a in parts],
        scratch_shapes=[pltpu.VMEM((3,) + a.shape, F32) for a in parts]
        + [pltpu.SemaphoreType.DMA((3, nt)), pltpu.SemaphoreType.DMA((3, nt))],
        compiler_params=pltpu.CompilerParams(has_side_effects=True, vmem_limit_bytes=VMEM_LIMIT),
    )(*parts, after)


def _adam_math(w, g, m, v):
    m2 = ADAM_B1 * m + (1.0 - ADAM_B1) * g
    v2 = ADAM_B2 * v + (1.0 - ADAM_B2) * (g * g)
    m_hat = m2 / (1.0 - ADAM_B1 ** ADAM_STEP)
    v_hat = v2 / (1.0 - ADAM_B2 ** ADAM_STEP)
    delta = -ADAM_LR * (m_hat / (jnp.sqrt(v_hat) + ADAM_EPS) + ADAM_WD * w)
    return delta, m2, v2


def _adam_big(w, m, v, parts, mine, chip, name):
    nl, r, cdim = w.shape
    tr = _rows_tile(r, 4 * cdim, 3 * 512 * 1024)

    def body(c_ref, w_ref, m_ref, v_ref, p_ref, *rest):
        mine_refs, (g_ref, d_ref, mo_ref, vo_ref) = rest[:nl], rest[nl:]
        own = mine_refs[0][...]
        for l in range(1, nl):
            own = jnp.where(pl.program_id(0) == l, mine_refs[l][...], own)
        g = ((p_ref[0].astype(F32) + p_ref[1].astype(F32)) + p_ref[2].astype(F32)) + own.astype(F32)
        delta, m2, v2 = _adam_math(w_ref[...], g, m_ref[...], v_ref[...])
        g_ref[...] = g
        d_ref[...] = delta
        mo_ref[...] = m2
        vo_ref[...] = v2

    spec = pl.BlockSpec((None, tr, cdim), lambda l, i, c_ref: (l, i, 0))
    mine_specs = [pl.BlockSpec((None, tr, cdim), lambda l, i, c_ref, ll=ll: (c_ref[0], jnp.where(l == ll, i, 0), 0))
                  for ll in range(nl)]
    return pl.pallas_call(
        body, name=name,
        grid_spec=pltpu.PrefetchScalarGridSpec(
            num_scalar_prefetch=1, grid=(nl, r // tr),
            in_specs=[spec, spec, spec, pl.BlockSpec((3, None, tr, cdim), lambda l, i, c_ref: (0, l, i, 0))]
            + mine_specs,
            out_specs=[spec] * 4),
        out_shape=[jax.ShapeDtypeStruct(w.shape, F32)] * 4,
        compiler_params=_params("arbitrary", "arbitrary"),
    )(chip, w, m, v, parts, *mine)


def _adam_small(ws, gs, ms, vs, name):
    n = len(ws)

    def body(*refs):
        w_r, g_r, m_r, v_r = refs[:n], refs[n:2 * n], refs[2 * n:3 * n], refs[3 * n:4 * n]
        d_o, m_o, v_o = refs[4 * n:5 * n], refs[5 * n:6 * n], refs[6 * n:7 * n]
        for t in range(n):
            delta, m2, v2 = _adam_math(w_r[t][...], g_r[t][...], m_r[t][...], v_r[t][...])
            d_o[t][...] = delta
            m_o[t][...] = m2
            v_o[t][...] = v2

    vm = pl.BlockSpec(memory_space=pltpu.VMEM)
    shapes = [jax.ShapeDtypeStruct(a.shape, F32) for a in ws]
    return pl.pallas_call(
        body, name=name, in_specs=[vm] * (4 * n), out_specs=[vm] * (3 * n), out_shape=shapes * 3,
        compiler_params=pltpu.CompilerParams(vmem_limit_bytes=VMEM_LIMIT),
    )(*ws, *gs, *ms, *vs)


def kernel(x, norm_mix, norm_ffn, norm_final, ab_w_in, a_ln_g, a_ln_b, a_w_s, a_b_s, b_conv_w, b_conv_b, b_ln_g, b_ln_b, ab_w_out, c_w_in, c_conv_w, c_w_out, f_w_up, f_conv_w, f_w_down, loss_target, m_norm_mix, m_norm_ffn, m_norm_final, m_ab_w_in, m_a_ln_g, m_a_ln_b, m_a_w_s, m_a_b_s, m_b_conv_w, m_b_conv_b, m_b_ln_g, m_b_ln_b, m_ab_w_out, m_c_w_in, m_c_conv_w, m_c_w_out, m_f_w_up, m_f_conv_w, m_f_w_down, v_norm_mix, v_norm_ffn, v_norm_final, v_ab_w_in, v_a_ln_g, v_a_ln_b, v_a_w_s, v_a_b_s, v_b_conv_w, v_b_conv_b, v_b_ln_g, v_b_ln_b, v_ab_w_out, v_c_w_in, v_c_conv_w, v_c_w_out, v_f_w_up, v_f_conv_w, v_f_w_down):
    s = x.shape[1]
    x0 = x.reshape(s, D)
    tgt = loss_target.reshape(s, D)
    xi, yi, ci = lax.axis_index("x"), lax.axis_index("y"), lax.axis_index("c")
    dev = 4 * xi + 2 * yi + ci
    cidx = ci.astype(jnp.int32).reshape(1)

    bf = lambda a: a.astype(BF16)
    slab_w = 6 * CHUNK
    pad = lambda a, rows: jnp.pad(a, ((0, rows - a.shape[0]), (0, slab_w - a.shape[1])))
    slab = jnp.concatenate([pad(b_conv_w[0], 32), pad(c_conv_w[0], 8), pad(f_conv_w.reshape(6, FB), 8)], axis=0)
    later = [bf(ab_w_in[0]), bf(ab_w_out[0]), slab, bf(f_w_up[0]), bf(f_w_down[0]), bf(c_w_in[0]), bf(c_w_out[0]),
             bf(f_w_up[1]), bf(f_w_down[1])]
    lands = [_zone(a, dev) for a in later]
    groups = [[0], [1, 2], [3, 4], [5, 6], [7, 8]]
    ag_sems, later, lands, ag_token = _ag_start(later, lands, x0, "ag_start")

    causal = jnp.tril(jnp.ones((CHUNK, CHUNK), F32))
    wsm = (a_w_s[0] * causal).astype(BF16)
    bs_col = a_b_s.reshape(HEADS, CHUNK, 1)
    nm = [norm_mix[0:1], norm_mix[1:2]]
    nf = [norm_ffn[0:1], norm_ffn[1:2]]
    nfin = norm_final.reshape(1, D)

    def arrive(g, after_ici, after_d2d, tag):
        srcs = [later[t] for t in groups[g]]
        zone = [lands[t] for t in groups[g]]
        sems1 = [ag_sems[t] for t in groups[g]]
        sems2, zone = _ag_forward(srcs, zone, sems1, after_ici, "ag_forward_" + tag)
        return _ag_finish(srcs, zone, sems1, sems2, after_d2d, "ag_finish_" + tag)

    h0 = _rms_fwd(x0, nm[0], "rms_mix0", after=ag_token)
    (win0,) = arrive(0, h0, h0, "w_in")
    z = _mm_in(h0, win0, "mm_ab_in")
    wout0, slab_g = arrive(1, z, z, "first")
    wout0 = wout0.reshape(D, D)
    bcw = jnp.transpose(slab_g[:, 0:BCONV, 0:DA // NDEV], (1, 0, 2)).reshape(BCONV, DA)
    ccw = jnp.transpose(slab_g[:, 32:35, 0:D // NDEV], (1, 0, 2)).reshape(3, D)
    fcw_g = slab_g[:, 40:46, 0:FB].reshape(2, NG, 2, 3, FB)
    fcws = [fcw_g[:, :, 0], fcw_g[:, :, 1]]
    ycat, yb2 = _ab_fwd(z, a_ln_g, a_ln_b, wsm, bs_col, bcw, b_conv_b, b_ln_g, b_ln_b, "ab_fwd")
    x1, h1 = _mm_out(ycat, wout0, x0, nf[0], "mm_ab_out")
    wup0, wdn0 = arrive(2, x1, x1, "ffn0")
    up0, upc0, x2, h2 = _ffn_fwd(h1, x1, wup0.reshape(2, NG, D, FB), fcws[0], wdn0.reshape(DFF, D), nm[1],
                                 "ffn_fwd0")
    cin, cout = arrive(3, x2, x2, "c")
    cout = cout.reshape(D, D)
    zc = _mm_in(h2, cin, "mm_c_in")
    rc, x3, h3 = _c_fwd(zc, ccw, cout, x2, nf[1], "c_fwd_out")
    wup1, wdn1 = arrive(4, rc, x3, "ffn1")
    wups = [wup0.reshape(2, NG, D, FB), wup1.reshape(2, NG, D, FB)]
    wdns = [wdn0.reshape(DFF, D), wdn1.reshape(DFF, D)]
    up1, upc1, dx4, dx4b, dnfin, loss_part = _ffn_fwd(h3, x3, wups[1], fcws[1], wdns[1], None, "ffn_fwd1_loss",
                                                      final=(tgt, nfin))

    zshape = lambda *sh: _hbm(lax.empty((3,) + sh, BF16))
    zones = [zshape(D, 2 * D // NDEV), zshape(D // NDEV, D), zshape(D, 3 * D // NDEV), zshape(D // NDEV, D),
             zshape(2, FB, D), zshape(2, DFF // NDEV, D)]
    started = []

    def pair_sums(grads, handle, after, tag):
        del grads
        got = _pair_wait(handle, after, "rs_pair_wait_" + tag)
        return [_pair_sum(b.reshape((NCHIP, 2) + b.shape[1:]), g, cidx, "rs_pair_sum_%s%d" % (tag, t))
                for t, (b, g) in enumerate(zip(handle[1], got))]

    def chip_start(sums, slots, carry, tag):
        sems, sums, new_zones, carry = _chip_start(sums, zones, slots, carry, "rs_chip_start_" + tag)
        zones[:] = new_zones
        started.append((sums, slots, sems))
        return sums, carry

    rows8 = lambda g, r: g.reshape(NDEV, r, D)
    a1, dup1, dx3, dx3b, dnf1, dfcw1 = _ffn_bwd(dx4, up1, upc1, wups[1], fcws[1], wdns[1], x3, nf[1], "ffn_bwd1")
    g_f1 = [_dw_up(h3, dup1, "dw_up1"), rows8(_dw_dn(a1, dx4b, "dw_dn1"), DFF // NDEV)]
    hd_f1, dx3b = _pair_start(g_f1, dx3b, "rs_pair_start_f1")
    g_cout = rows8(_dw_rows(rc, dx3b, "dw_c_out"), D // NDEV)
    s_f1 = pair_sums(g_f1, hd_f1, g_cout, "f1")
    s_f1, dx3b = chip_start(s_f1, [(4, 1), (5, 1)], dx3b, "f1")
    dzc, dccw = _c_bwd(dx3b, cout, zc, ccw, "c_bwd")
    dx2, dx2b, dnm1 = _mm_nt_rms(dzc, cin, x2, nm[1], dx3, True, "mm_c_in_bwd")
    g_c = [_dw_cols(h2, dzc, NDEV, 3 * D // NDEV, "dw_c_in"), g_cout]
    hd_c, dx2 = _pair_start(g_c, dx2, "rs_pair_start_c")
    a0, dup0, dx1, dx1b, dnf0, dfcw0 = _ffn_bwd(dx2, up0, upc0, wups[0], fcws[0], wdns[0], x1, nf[0], "ffn_bwd0")
    s_c = pair_sums(g_c, hd_c, dx1b, "c")
    s_c, dx1b = chip_start(s_c, [(2, None), (3, None)], dx1b, "c")
    g_f0 = [_dw_up(h1, dup0, "dw_up0"), rows8(_dw_dn(a0, dx2b, "dw_dn0"), DFF // NDEV)]
    hd_f0, dx1b = _pair_start(g_f0, dx1b, "rs_pair_start_f0")
    g_wout0 = rows8(_dw_rows(ycat, dx1b, "dw_ab_out"), D // NDEV)
    s_f0 = pair_sums(g_f0, hd_f0, g_wout0, "f0")
    s_f0, dx1b = chip_start(s_f0, [(4, 0), (5, 0)], dx1b, "f0")
    dz, g512, dws, dbs = _ab_bwd(dx1b, wout0, z, yb2, a_ln_g, a_ln_b, wsm, bs_col, bcw, b_ln_g, b_ln_b, "ab_bwd")
    grad_x, dnm0 = _mm_nt_rms(dz, win0, x0, nm[0], dx1, False, "mm_ab_in_bwd")
    g_ab = [_dw_cols(h0, dz, NDEV, 2 * D // NDEV, "dw_ab_in"), g_wout0]
    hd_ab, _ = _pair_start(g_ab, None, "rs_pair_start_ab")

    g1024 = jnp.concatenate([dnm0, dnm1, dnf0, dnf1, dnfin, dccw], axis=0)
    gfc = jnp.concatenate([dfcw0, dfcw1], axis=0).reshape(2 * NG * 2 * 3, FB)
    g1024, g512, dws, dbs, gfc, loss_sum = _small_allreduce(
        [g1024, g512, dws.reshape(HEADS * CHUNK, CHUNK), dbs.reshape(HEADS, CHUNK), gfc, loss_part], (2,),
        hd_ab[1][0], "small_allreduce")
    loss = loss_sum[0, 0]
    s_ab = pair_sums(g_ab, hd_ab, g1024, "ab")
    s_ab, _ = chip_start(s_ab, [(0, None), (1, None)], None, "ab")
    p_cin, p_cout, p_wup, p_wdn = _chip_wait(started[:3], zones[2:], [2, 3, 4, 5], s_ab[0], "rs_chip_wait_early")

    chip = (2 * xi + yi).astype(jnp.int32).reshape(1)

    def big_update(w, m, v, parts, mine, name):
        shp = w.shape
        w3, m3, v3 = (a.reshape((-1,) + shp[-2:]) for a in (w, m, v))
        p4 = parts.reshape((3,) + w3.shape)
        return [o.reshape(shp) for o in _adam_big(w3, m3, v3, p4, mine, chip, name)]

    u_cin = big_update(c_w_in, m_c_w_in, v_c_w_in, p_cin, [s_c[0]], "adam_c_w_in")
    u_cout = big_update(c_w_out, m_c_w_out, v_c_w_out, p_cout, [s_c[1]], "adam_c_w_out")
    tr_ = lambda a: jnp.swapaxes(a, 1, 2)
    u_wup = [tr_(o) for o in big_update(tr_(f_w_up), tr_(m_f_w_up), tr_(v_f_w_up), p_wup,
                                        [s_f0[0], s_f1[0]], "adam_f_w_up")]
    u_wdn = big_update(f_w_down, m_f_w_down, v_f_w_down, p_wdn, [s_f0[1], s_f1[1]], "adam_f_w_down")
    p_win0, p_wout0 = _chip_wait(started[3:], zones[:2], [0, 1], u_wdn[0], "rs_chip_wait_late")
    u_win0 = big_update(ab_w_in, m_ab_w_in, v_ab_w_in, p_win0, [s_ab[0]], "adam_ab_w_in")
    u_wout0 = big_update(ab_w_out, m_ab_w_out, v_ab_w_out, p_wout0, [s_ab[1]], "adam_ab_w_out")

    g_norm_mix = g1024[0:2]
    g_norm_ffn = g1024[2:4]
    g_norm_final = g1024[4:5]
    g_ccw = lax.dynamic_slice(g1024[5:8], (0, dev * (D // NDEV)), (3, D // NDEV))
    g_bcw = lax.dynamic_slice(g512[8:8 + BCONV], (0, dev * (DA // NDEV)), (BCONV, DA // NDEV))
    gfc = gfc.reshape(2, NG, 2, 3, FB)
    g_fcw = lax.dynamic_slice(gfc, (0, dev % NG, dev // NG, 0, 0), (2, 1, 1, 3, FB)).reshape(2, 3, FB)
    small_w = [norm_mix, norm_ffn, nfin, a_ln_g, a_ln_b, a_w_s[0], a_b_s[0], b_conv_w[0], b_conv_b,
               b_ln_g, b_ln_b, c_conv_w[0], f_conv_w]
    small_g = [g_norm_mix, g_norm_ffn, g_norm_final, g512[0:1], g512[1:2],
               dws.reshape(HEADS, CHUNK, CHUNK), dbs, g_bcw, g512[2:3],
               g512[3:4], g512[4:5], g_ccw, g_fcw]
    small_m = [m_norm_mix, m_norm_ffn, m_norm_final.reshape(1, D), m_a_ln_g, m_a_ln_b, m_a_w_s[0], m_a_b_s[0],
               m_b_conv_w[0], m_b_conv_b, m_b_ln_g, m_b_ln_b, m_c_conv_w[0], m_f_conv_w]
    small_v = [v_norm_mix, v_norm_ffn, v_norm_final.reshape(1, D), v_a_ln_g, v_a_ln_b, v_a_w_s[0], v_a_b_s[0],
               v_b_conv_w[0], v_b_conv_b, v_b_ln_g, v_b_ln_b, v_c_conv_w[0], v_f_conv_w]
    upd = _adam_small(small_w, small_g, small_m, small_v, "adam_small")
    ns = len(small_w)
    orig = [norm_mix, norm_ffn, norm_final, a_ln_g, a_ln_b, a_w_s, a_b_s, b_conv_w, b_conv_b,
            b_ln_g, b_ln_b, c_conv_w, f_conv_w]
    sg_out = [g.reshape(o.shape) for g, o in zip(small_g, orig)]
    sd_out = [a.reshape(o.shape) for a, o in zip(upd[0:ns], orig)]
    sm_out = [a.reshape(o.shape) for a, o in zip(upd[ns:2 * ns], orig)]
    sv_out = [a.reshape(o.shape) for a, o in zip(upd[2 * ns:3 * ns], orig)]

    def assemble(small, k):
        return [small[0], small[1], small[2], u_win0[k], small[3], small[4], small[5], small[6], small[7],
                small[8], small[9], small[10], u_wout0[k], u_cin[k], small[11], u_cout[k], u_wup[k],
                small[12], u_wdn[k]]

    grads = assemble(sg_out, 0)
    deltas = assemble(sd_out, 1)
    new_m = assemble(sm_out, 2)
    new_v = assemble(sv_out, 3)
    return (loss, grad_x.reshape(1, s, D), *grads, *deltas, *new_m, *new_v)
```

```python
import math

import jax
import jax.numpy as jnp
from jax import lax
from jax.experimental import pallas as pl
from jax.experimental.pallas import tpu as pltpu

F32 = jnp.float32
BF16 = jnp.bfloat16

D = 1024
DA = 512
HEADS = 4
CHUNK = 128
DFF = 2816
NDEV = 8
NCHIP = 4
FB = DFF * 2 // NDEV
NG = DFF // FB
BCONV = 31
EPS = 1e-6
HALO = 16
HALO_B = 32
RC = 32
NPART = 2
VMEM_LIMIT = 52 * 1024 * 1024
INV_SQRT2 = 1.0 / math.sqrt(2.0)
INV_SQRT_2PI = 1.0 / math.sqrt(2.0 * math.pi)

ADAM_LR = 0.001
ADAM_B1 = 0.9
ADAM_B2 = 0.999
ADAM_EPS = 1e-08
ADAM_WD = 0.01
ADAM_STEP = 10

MESH = pl.DeviceIdType.MESH
ANY = pl.BlockSpec(memory_space=pl.ANY)
NT_DIMS = (((1,), (1,)), ((), ()))
TN_DIMS = (((0,), (0,)), ((), ()))


def _params(*sem):
    return pltpu.CompilerParams(dimension_semantics=sem, vmem_limit_bytes=VMEM_LIMIT)


def _tile(s, want):
    return min(want, s)


def _sigmoid(x):
    return jax.nn.sigmoid(x)


def _dsilu(x, sg):
    return sg * (1.0 + x * (1.0 - sg))


def _gelu(x):
    return 0.5 * x * (1.0 + lax.erf(x * INV_SQRT2))


def _dgelu(x):
    return 0.5 * (1.0 + lax.erf(x * INV_SQRT2)) + x * jnp.exp(-0.5 * x * x) * INV_SQRT_2PI


def _ln_fwd(x, g, b):
    mu = jnp.mean(x, axis=-1, keepdims=True)
    xc = x - mu
    var = jnp.mean(xc * xc, axis=-1, keepdims=True)
    rstd = lax.rsqrt(var + EPS)
    xhat = xc * rstd
    return xhat * g + b, xhat, rstd


def _ln_bwd(dy, xhat, rstd, g):
    dxh = dy * g
    m1 = jnp.mean(dxh, axis=-1, keepdims=True)
    m2 = jnp.mean(dxh * xhat, axis=-1, keepdims=True)
    return rstd * (dxh - m1 - xhat * m2)


def _rms_bwd_math(dh, x, g):
    r = lax.rsqrt(jnp.mean(x * x, axis=-1, keepdims=True) + EPS)
    xhat = x * r
    dg = jnp.sum(dh * xhat, axis=0, keepdims=True)
    u = dh * g
    dx = r * (u - xhat * jnp.mean(u * xhat, axis=-1, keepdims=True))
    return dx, dg


def _conv3(xe, cw, halo):
    x0 = xe[halo:]
    x1 = pltpu.roll(xe, 1, 0)[halo:]
    x2 = pltpu.roll(xe, 2, 0)[halo:]
    return cw[2] * x0 + cw[1] * x1 + cw[0] * x2, (x0, x1, x2)


def _conv3_bwd_in(dce, cw, ts):
    n = dce.shape[0]
    d1 = pltpu.roll(dce, n - 1, 0)[:ts]
    d2 = pltpu.roll(dce, n - 2, 0)[:ts]
    return cw[2] * dce[:ts] + cw[1] * d1 + cw[0] * d2


def _conv3_bwd_w(dc, taps):
    x0, x1, x2 = taps
    return [jnp.sum(dc * x2, axis=0, keepdims=True), jnp.sum(dc * x1, axis=0, keepdims=True),
            jnp.sum(dc * x0, axis=0, keepdims=True)]


def _rms_fwd(x, g, name, after=None):
    s = x.shape[0]
    ts = _tile(s, 512)

    def body(x_ref, g_ref, *rest):
        h_ref = rest[-1]
        xv = x_ref[...]
        r = lax.rsqrt(jnp.mean(xv * xv, axis=-1, keepdims=True) + EPS)
        h_ref[...] = (xv * r * g_ref[...]).astype(BF16)

    extra = [] if after is None else [after]
    return pl.pallas_call(
        body, grid=(s // ts,), name=name,
        in_specs=[pl.BlockSpec((ts, D), lambda i: (i, 0)), pl.BlockSpec((1, D), lambda i: (0, 0))]
        + [ANY] * len(extra),
        out_specs=pl.BlockSpec((ts, D), lambda i: (i, 0)),
        out_shape=jax.ShapeDtypeStruct((s, D), BF16),
        compiler_params=_params("parallel"),
    )(x, g, *extra)


MXU_COLS = 256


def _pair(bn):
    return 1 if bn % MXU_COLS == 0 else 2


def _cols(w_ref, b, pair):
    return w_ref[b] if pair == 1 else jnp.concatenate([w_ref[b + q] for q in range(pair)], axis=1)


def _mm_in(h, wblk, name):
    s = h.shape[0]
    nb, _, bn = wblk.shape
    pair = _pair(bn)
    ts = _tile(s, 1024)

    def body(h_ref, w_ref, o_ref):
        hv = h_ref[...]
        for b in range(0, nb, pair):
            o_ref[:, b * bn:(b + pair) * bn] = jnp.dot(hv, _cols(w_ref, b, pair),
                                                       preferred_element_type=F32).astype(BF16)

    return pl.pallas_call(
        body, grid=(s // ts,), name=name,
        in_specs=[pl.BlockSpec((ts, D), lambda i: (i, 0)), pl.BlockSpec((nb, D, bn), lambda i: (0, 0, 0))],
        out_specs=pl.BlockSpec((ts, nb * bn), lambda i: (i, 0)),
        out_shape=jax.ShapeDtypeStruct((s, nb * bn), BF16),
        compiler_params=_params("parallel"),
    )(h, wblk)


def _rms_math(xv, g):
    r = lax.rsqrt(jnp.mean(xv * xv, axis=-1, keepdims=True) + EPS)
    return (xv * r * g).astype(BF16)


def _mm_out(y, w, xres, gnext, name):
    s = y.shape[0]
    ts = _tile(s, 1024)

    def body(y_ref, w_ref, x_ref, g_ref, o_ref, h_ref):
        xn = x_ref[...] + jnp.dot(y_ref[...], w_ref[...], preferred_element_type=F32)
        o_ref[...] = xn
        h_ref[...] = _rms_math(xn, g_ref[...])

    return pl.pallas_call(
        body, grid=(s // ts,), name=name,
        in_specs=[pl.BlockSpec((ts, D), lambda i: (i, 0)), pl.BlockSpec((D, D), lambda i: (0, 0)),
                  pl.BlockSpec((ts, D), lambda i: (i, 0)), pl.BlockSpec((1, D), lambda i: (0, 0))],
        out_specs=[pl.BlockSpec((ts, D), lambda i: (i, 0)), pl.BlockSpec((ts, D), lambda i: (i, 0))],
        out_shape=[jax.ShapeDtypeStruct((s, D), F32), jax.ShapeDtypeStruct((s, D), BF16)],
        compiler_params=_params("parallel"),
    )(y, w, xres, gnext)


CONV_ROWS = 32


def _rolled_copies(dst_ref, xe, back):
    n = xe.shape[0]
    dst_ref[0] = xe
    for r in range(1, 8):
        dst_ref[r] = pltpu.roll(xe, n - r if back else r, 0)


def _conv31(rolled_ref, cw_ref, ts, out_ref, bias):
    for o in range(0, ts, CONV_ROWS):
        acc = jnp.zeros((CONV_ROWS, DA), F32) + bias
        for sh in range(BCONV):
            q, r = divmod(sh, 8)
            lo = HALO_B - 8 * q + o
            acc = acc + cw_ref[BCONV - 1 - sh:BCONV - sh, :] * rolled_ref[r, lo:lo + CONV_ROWS, :]
        out_ref[o:o + CONV_ROWS, :] = acc


def _ab_fwd(z, lga, lba, wsm, bs_col, cwb, cbb, lgb, lbb, name):
    s = z.shape[0]
    ts = _tile(s, 256)
    hb = ts // HALO_B

    def body(z_ref, zh_ref, lga_ref, lba_ref, ws_ref, bs_ref, cw_ref, cb_ref, lgb_ref, lbb_ref,
             y_ref, yb2_ref, rolled):
        i = pl.program_id(0)
        z_t = z_ref[...].astype(F32)
        gu = _gelu(z_t[:, 0:DA])
        gv = _gelu(z_t[:, DA:2 * DA])
        vn, _, _ = _ln_fwd(gv, lga_ref[...], lba_ref[...])
        vnb = vn.astype(BF16)
        for c in range(ts // CHUNK):
            for h in range(HEADS):
                rs = slice(c * CHUNK, (c + 1) * CHUNK)
                cs = slice(h * CHUNK, (h + 1) * CHUNK)
                mixed = jnp.dot(ws_ref[h], vnb[rs, cs], preferred_element_type=F32) + bs_ref[h]
                y_ref[rs, cs] = (gu[rs, cs] * mixed).astype(BF16)
        zh = jnp.where(i > 0, zh_ref[...], jnp.zeros_like(zh_ref[...])).astype(F32)
        xb = jnp.concatenate([zh[:, 0:DA], z_t[:, 2 * DA:3 * DA]], axis=0)
        gb = jnp.concatenate([zh[:, DA:2 * DA], z_t[:, 3 * DA:4 * DA]], axis=0)
        _rolled_copies(rolled, xb * _sigmoid(gb), False)
        _conv31(rolled, cw_ref, ts, yb2_ref, cb_ref[...])
        nb_, _, _ = _ln_fwd(yb2_ref[...], lgb_ref[...], lbb_ref[...])
        y_ref[:, DA:2 * DA] = (nb_ * _sigmoid(nb_)).astype(BF16)

    row = lambda i: (0, 0)
    return pl.pallas_call(
        body, grid=(s // ts,), name=name,
        in_specs=[pl.BlockSpec((ts, 4 * DA), lambda i: (i, 0)),
                  pl.BlockSpec((HALO_B, 2 * DA), lambda i: (jnp.maximum(i * hb - 1, 0), 1)),
                  pl.BlockSpec((1, DA), row), pl.BlockSpec((1, DA), row),
                  pl.BlockSpec((HEADS, CHUNK, CHUNK), lambda i: (0, 0, 0)),
                  pl.BlockSpec((HEADS, CHUNK, 1), lambda i: (0, 0, 0)),
                  pl.BlockSpec((BCONV, DA), row), pl.BlockSpec((1, DA), row),
                  pl.BlockSpec((1, DA), row), pl.BlockSpec((1, DA), row)],
        out_specs=[pl.BlockSpec((ts, 2 * DA), lambda i: (i, 0)), pl.BlockSpec((ts, DA), lambda i: (i, 0))],
        out_shape=[jax.ShapeDtypeStruct((s, 2 * DA), BF16), jax.ShapeDtypeStruct((s, DA), F32)],
        scratch_shapes=[pltpu.VMEM((8, ts + HALO_B, DA), F32)],
        compiler_params=_params("parallel"),
    )(z, z, lga, lba, wsm, bs_col, cwb, cbb, lgb, lbb)


def _c_fwd(zc, cw, w, xres, gnext, name):
    s = zc.shape[0]
    ts = _tile(s, 512)
    hb = ts // HALO

    def body(z_ref, ch_ref, xh_ref, cw_ref, w_ref, x_ref, g_ref, r_ref, o_ref, h_ref):
        i = pl.program_id(0)
        z_t = z_ref[...].astype(F32)
        ph = jnp.where(i > 0, ch_ref[...].astype(F32) * xh_ref[...].astype(F32), 0.0)
        pe = jnp.concatenate([ph, z_t[:, D:2 * D] * z_t[:, 2 * D:3 * D]], axis=0)
        q, _ = _conv3(pe, [cw_ref[k:k + 1, :] for k in range(3)], HALO)
        r = (z_t[:, 0:D] * q).astype(BF16)
        r_ref[...] = r
        xn = x_ref[...] + jnp.dot(r, w_ref[...], preferred_element_type=F32)
        o_ref[...] = xn
        h_ref[...] = _rms_math(xn, g_ref[...])

    halo = lambda col: pl.BlockSpec((HALO, D), lambda i: (jnp.maximum(i * hb - 1, 0), col))
    tile = pl.BlockSpec((ts, D), lambda i: (i, 0))
    return pl.pallas_call(
        body, grid=(s // ts,), name=name,
        in_specs=[pl.BlockSpec((ts, 3 * D), lambda i: (i, 0)), halo(1), halo(2),
                  pl.BlockSpec((3, D), lambda i: (0, 0)), pl.BlockSpec((D, D), lambda i: (0, 0)), tile,
                  pl.BlockSpec((1, D), lambda i: (0, 0))],
        out_specs=[tile, tile, tile],
        out_shape=[jax.ShapeDtypeStruct((s, D), BF16), jax.ShapeDtypeStruct((s, D), F32),
                   jax.ShapeDtypeStruct((s, D), BF16)],
        compiler_params=_params("parallel"),
    )(zc, zc, zc, cw, w, xres, gnext)


def _final_math(xv, tv, gv):
    r = lax.rsqrt(jnp.mean(xv * xv, axis=-1, keepdims=True) + EPS)
    xhat = xv * r
    e = xhat * gv - tv
    part = 0.5 * jnp.sum(jnp.mean(e * e, axis=-1, keepdims=True), axis=0, keepdims=True)
    dy = e * (1.0 / D)
    dgp = jnp.sum(dy * xhat, axis=0, keepdims=True)
    u = dy * gv
    dx = r * (u - xhat * jnp.mean(u * xhat, axis=-1, keepdims=True))
    return dx, dgp, jnp.broadcast_to(part, (1, 128))


def _ffn_fwd(h, xres, wup, fcw, wdn, gnext, name, final=None):
    s = h.shape[0]
    ts = _tile(s, 512)
    hb = ts // HALO

    def body(h_ref, hh_ref, w_ref, cw_ref, wd_ref, x_ref, *rest):
        if final is not None:
            t_ref, gf_ref, up_ref, upc_ref, dx_ref, dxb_ref, dg_ref, loss_ref, up_s, xo_ref = rest
        elif gnext is not None:
            gn_ref, up_ref, upc_ref, xo_ref, hn_ref, up_s = rest
        else:
            up_ref, upc_ref, xo_ref, up_s = rest
        i = pl.program_id(0)
        m = pl.program_id(1)
        @pl.when(m == 0)
        def _():
            xo_ref[...] = x_ref[...]

        halo = jnp.where(i > 0, hh_ref[...], jnp.zeros_like(hh_ref[...]))
        hx = jnp.concatenate([halo, h_ref[...]], axis=0)
        acts = []
        for gv in range(2):
            up_s[gv] = jnp.dot(hx, w_ref[gv], preferred_element_type=F32)
            x0 = up_s[gv, HALO:HALO + ts, :]
            up_ref[gv] = x0.astype(BF16)
            upc = (cw_ref[gv, 2:3, :] * x0 + cw_ref[gv, 1:2, :] * up_s[gv, HALO - 1:HALO - 1 + ts, :]
                   + cw_ref[gv, 0:1, :] * up_s[gv, HALO - 2:HALO - 2 + ts, :])
            upc_ref[gv] = upc.astype(BF16)
            acts.append(upc)
        a = acts[0] * _sigmoid(acts[0]) * acts[1]
        xo_ref[...] += jnp.dot(a.astype(BF16), wd_ref[...], preferred_element_type=F32)

        if final is not None:
            @pl.when(m == NG - 1)
            def _():
                dx, dgp, part = _final_math(xo_ref[...], t_ref[...], gf_ref[...])
                dx_ref[...] = dx
                dxb_ref[...] = dx.astype(BF16)

                @pl.when(i == 0)
                def _():
                    dg_ref[...] = dgp
                    loss_ref[...] = part

                @pl.when(i > 0)
                def _():
                    dg_ref[...] += dgp
                    loss_ref[...] += part

        elif gnext is not None:
            @pl.when(m == NG - 1)
            def _():
                hn_ref[...] = _rms_math(xo_ref[...], gn_ref[...])

    tile = pl.BlockSpec((ts, D), lambda i, m: (i, 0))
    row = lambda n: pl.BlockSpec((1, n), lambda i, m: (0, 0))
    scratch = [pltpu.VMEM((2, ts + HALO, FB), F32)]
    if final is not None:
        more_in, more_ops = [tile, row(D)], list(final)
        more_out = [tile, tile, row(D), row(128)]
        more_shape = [jax.ShapeDtypeStruct((s, D), F32), jax.ShapeDtypeStruct((s, D), BF16),
                      jax.ShapeDtypeStruct((1, D), F32), jax.ShapeDtypeStruct((1, 128), F32)]
        scratch.append(pltpu.VMEM((ts, D), F32))
    else:
        nxt = gnext is not None
        more_in, more_ops = ([row(D)], [gnext]) if nxt else ([], [])
        more_out = [tile] + ([tile] if nxt else [])
        more_shape = [jax.ShapeDtypeStruct((s, D), F32)] + ([jax.ShapeDtypeStruct((s, D), BF16)] if nxt else [])
    return pl.pallas_call(
        body, grid=(s // ts, NG), name=name,
        in_specs=[tile,
                  pl.BlockSpec((HALO, D), lambda i, m: (jnp.maximum(i * hb - 1, 0), 0)),
                  pl.BlockSpec((2, None, D, FB), lambda i, m: (0, m, 0, 0)),
                  pl.BlockSpec((2, None, 3, FB), lambda i, m: (0, m, 0, 0)),
                  pl.BlockSpec((FB, D), lambda i, m: (m, 0)),
                  tile] + more_in,
        out_specs=[pl.BlockSpec((None, 2, ts, FB), lambda i, m: (m, 0, i, 0)),
                   pl.BlockSpec((None, 2, ts, FB), lambda i, m: (m, 0, i, 0))] + more_out,
        out_shape=[jax.ShapeDtypeStruct((NG, 2, s, FB), BF16), jax.ShapeDtypeStruct((NG, 2, s, FB), BF16)] + more_shape,
        scratch_shapes=scratch,
        compiler_params=_params("arbitrary", "arbitrary"),
    )(h, h, wup, fcw, wdn, xres, *more_ops)


def _ffn_bwd(df, up, upc, wup, fcw, wdn, xin, g, name):
    s = df.shape[0]
    ts = _tile(s, 512)
    nt = s // ts

    def body(df_ref, up_ref, upc_ref, w_ref, cw_ref, wd_ref, x_ref, g_ref,
             a_ref, dup_ref, dx_ref, dxb_ref, dg_ref, dcw_ref, carry, acc, tacc, dcs_ref):
        i = pl.program_id(0)
        m = pl.program_id(1)
        first = i == 0
        @pl.when(first)
        def _():
            carry[m] = jnp.zeros((2, 8, FB), F32)
            dcw_ref[m] = jnp.zeros((2, 3, FB), F32)

        @pl.when(m == 0)
        def _():
            acc[...] = jnp.zeros((ts, D), F32)

        cws = [[cw_ref[gv, k:k + 1, :] for k in range(3)] for gv in range(2)]
        part = ts // NPART
        das = [lax.dot_general(df_ref[p * part:(p + 1) * part, :].astype(BF16), wd_ref[...], NT_DIMS,
                               preferred_element_type=F32) for p in range(NPART)]

        tacc[...] = jnp.zeros((2, 3, 8, FB), F32)
        dcs_ref[:, ts:ts + 8, :] = carry[m]
        for r in reversed(range(ts // RC)):
            rs = slice(r * RC, (r + 1) * RC)
            gate = upc_ref[0, rs, :].astype(F32)
            val = upc_ref[1, rs, :].astype(F32)
            sg = _sigmoid(gate)
            sl = gate * sg
            a_ref[rs, :] = (sl * val).astype(BF16)
            da_c = das[(r * RC) // part][(r * RC) % part:(r * RC) % part + RC]
            dcs = [da_c * val * _dsilu(gate, sg), da_c * sl]
            for gv in range(2):
                dc = dcs[gv]
                dcs_ref[gv, rs, :] = dc
                d1 = dcs_ref[gv, r * RC + 1:(r + 1) * RC + 1, :]
                d2 = dcs_ref[gv, r * RC + 2:(r + 1) * RC + 2, :]
                du = cws[gv][2] * dc + cws[gv][1] * d1 + cws[gv][0] * d2
                dup_ref[gv, rs, :] = du.astype(BF16)
                x0 = up_ref[gv, rs, :].astype(F32)
                for k, dk in enumerate((d2, d1, dc)):
                    p = x0 * dk
                    tacc[gv, k] += sum(p[j:j + 8] for j in range(0, RC, 8))
            if (r * RC) % part == 0:
                ps = slice(r * RC, r * RC + part)
                acc[ps, :] += (
                    lax.dot_general(dup_ref[0, ps, :], w_ref[0], NT_DIMS, preferred_element_type=F32)
                    + lax.dot_general(dup_ref[1, ps, :], w_ref[1], NT_DIMS, preferred_element_type=F32))
        for gv in range(2):
            carry[m, gv] = dcs_ref[gv, 0:8, :]
            for k in range(3):
                dcw_ref[m, gv, k:k + 1, :] += jnp.sum(tacc[gv, k], axis=0, keepdims=True)

        @pl.when(m == NG - 1)
        def _():
            dx, dgp = _rms_bwd_math(acc[...], x_ref[...], g_ref[...])
            dx = df_ref[...] + dx
            dx_ref[...] = dx
            dxb_ref[...] = dx.astype(BF16)

            @pl.when(first)
            def _():
                dg_ref[...] = dgp

            @pl.when(jnp.logical_not(first))
            def _():
                dg_ref[...] += dgp

    rev = lambda i: nt - 1 - i
    return pl.pallas_call(
        body, grid=(nt, NG), name=name,
        in_specs=[pl.BlockSpec((ts, D), lambda i, m: (rev(i), 0)),
                  pl.BlockSpec((None, 2, ts, FB), lambda i, m: (m, 0, rev(i), 0)),
                  pl.BlockSpec((None, 2, ts, FB), lambda i, m: (m, 0, rev(i), 0)),
                  pl.BlockSpec((2, None, D, FB), lambda i, m: (0, m, 0, 0)),
                  pl.BlockSpec((2, None, 3, FB), lambda i, m: (0, m, 0, 0)),
                  pl.BlockSpec((FB, D), lambda i, m: (m, 0)),
                  pl.BlockSpec((ts, D), lambda i, m: (rev(i), 0)),
                  pl.BlockSpec((1, D), lambda i, m: (0, 0))],
        out_specs=[pl.BlockSpec((None, ts, FB), lambda i, m: (m, rev(i), 0)),
                   pl.BlockSpec((None, 2, ts, FB), lambda i, m: (m, 0, rev(i), 0)),
                   pl.BlockSpec((ts, D), lambda i, m: (rev(i), 0)),
                   pl.BlockSpec((ts, D), lambda i, m: (rev(i), 0)),
                   pl.BlockSpec((1, D), lambda i, m: (0, 0)),
                   pl.BlockSpec((NG, 2, 3, FB), lambda i, m: (0, 0, 0, 0))],
        out_shape=[jax.ShapeDtypeStruct((NG, s, FB), BF16), jax.ShapeDtypeStruct((NG, 2, s, FB), BF16),
                   jax.ShapeDtypeStruct((s, D), F32), jax.ShapeDtypeStruct((s, D), BF16),
                   jax.ShapeDtypeStruct((1, D), F32),
                   jax.ShapeDtypeStruct((NG, 2, 3, FB), F32)],
        scratch_shapes=[pltpu.VMEM((NG, 2, 8, FB), F32), pltpu.VMEM((ts, D), F32),
                        pltpu.VMEM((2, 3, 8, FB), F32), pltpu.VMEM((2, ts + 8, FB), F32)],
        compiler_params=_params("arbitrary", "arbitrary"),
    )(df, up, upc, wup, fcw, wdn, xin, g)


def _mm_nt_rms(dy, wblk, x, g, dres, bf16_copy, name):
    s = dy.shape[0]
    nb, _, bn = wblk.shape
    pair = _pair(bn)
    ts = _tile(s, 512)

    def body(dy_ref, w_ref, x_ref, g_ref, dr_ref, dx_ref, *rest):
        dg_ref = rest[-1]
        i = pl.program_id(0)
        acc = jnp.zeros((ts, D), F32)
        for b in range(0, nb, pair):
            acc = acc + lax.dot_general(dy_ref[:, b * bn:(b + pair) * bn], _cols(w_ref, b, pair), NT_DIMS,
                                        preferred_element_type=F32)
        dx, dgp = _rms_bwd_math(acc, x_ref[...], g_ref[...])
        dx = dr_ref[...] + dx
        dx_ref[...] = dx
        if bf16_copy:
            rest[0][...] = dx.astype(BF16)

        @pl.when(i == 0)
        def _():
            dg_ref[...] = dgp

        @pl.when(i > 0)
        def _():
            dg_ref[...] += dgp

    tile = pl.BlockSpec((ts, D), lambda i: (i, 0))
    return pl.pallas_call(
        body, grid=(s // ts,), name=name,
        in_specs=[pl.BlockSpec((ts, nb * bn), lambda i: (i, 0)), pl.BlockSpec((nb, D, bn), lambda i: (0, 0, 0)),
                  tile, pl.BlockSpec((1, D), lambda i: (0, 0)), tile],
        out_specs=[tile] + ([tile] if bf16_copy else []) + [pl.BlockSpec((1, D), lambda i: (0, 0))],
        out_shape=[jax.ShapeDtypeStruct((s, D), F32)] + ([jax.ShapeDtypeStruct((s, D), BF16)] if bf16_copy else [])
        + [jax.ShapeDtypeStruct((1, D), F32)],
        compiler_params=_params("arbitrary"),
    )(dy, wblk, x, g, dres)


def _c_bwd(dx, w, zc, cw, name):
    s = dx.shape[0]
    ts = _tile(s, 512)
    nt = s // ts
    hb = ts // HALO

    def body(dx_ref, dxf_ref, w_ref, z_ref, ch_ref, xh_ref, bf_ref, cw_ref, dz_ref, dcw_ref):
        i = pl.program_id(0)
        cwv = [cw_ref[k:k + 1, :] for k in range(3)]
        dre = lax.dot_general(jnp.concatenate([dx_ref[...], dxf_ref[...]], axis=0), w_ref[...], NT_DIMS,
                              preferred_element_type=F32).astype(BF16).astype(F32)
        z_t = z_ref[...].astype(F32)
        bg, cg, xv = z_t[:, 0:D], z_t[:, D:2 * D], z_t[:, 2 * D:3 * D]
        ph = jnp.where(i > 0, ch_ref[...].astype(F32) * xh_ref[...].astype(F32), 0.0)
        pe = jnp.concatenate([ph, cg * xv], axis=0)
        q, taps = _conv3(pe, cwv, HALO)
        drv = dre[0:ts]
        dq = drv * bg
        dqf = jnp.where(i < nt - 1, dre[ts:ts + HALO] * bf_ref[...].astype(F32), 0.0)
        dp = _conv3_bwd_in(jnp.concatenate([dq, dqf], axis=0), cwv, ts)
        dz_ref[:, 0:D] = (drv * q).astype(BF16)
        dz_ref[:, D:2 * D] = (dp * xv).astype(BF16)
        dz_ref[:, 2 * D:3 * D] = (dp * cg).astype(BF16)
        rows = _conv3_bwd_w(dq, taps)

        @pl.when(i == 0)
        def _():
            for k in range(3):
                dcw_ref[k:k + 1, :] = rows[k]

        @pl.when(i > 0)
        def _():
            for k in range(3):
                dcw_ref[k:k + 1, :] += rows[k]

    past = lambda col: pl.BlockSpec((HALO, D), lambda i: (jnp.maximum(i * hb - 1, 0), col))
    nxt = lambda i: jnp.minimum((i + 1) * hb, s // HALO - 1)
    return pl.pallas_call(
        body, grid=(nt,), name=name,
        in_specs=[pl.BlockSpec((ts, D), lambda i: (i, 0)),
                  pl.BlockSpec((HALO, D), lambda i: (nxt(i), 0)),
                  pl.BlockSpec((D, D), lambda i: (0, 0)),
                  pl.BlockSpec((ts, 3 * D), lambda i: (i, 0)), past(1), past(2),
                  pl.BlockSpec((HALO, D), lambda i: (nxt(i), 0)),
                  pl.BlockSpec((3, D), lambda i: (0, 0))],
        out_specs=[pl.BlockSpec((ts, 3 * D), lambda i: (i, 0)), pl.BlockSpec((3, D), lambda i: (0, 0))],
        out_shape=[jax.ShapeDtypeStruct((s, 3 * D), BF16), jax.ShapeDtypeStruct((3, D), F32)],
        compiler_params=_params("arbitrary"),
    )(dx, dx, w, zc, zc, zc, zc, cw)


G512_ROWS = 40


def _ab_bwd(dx, w, z, yb2, lga, lba, wsm, bs_col, cwb, lgb, lbb, name):
    s = z.shape[0]
    ts = _tile(s, 256)
    nt = s // ts
    hb = ts // HALO_B
    nch = ts // CHUNK

    def body(z_ref, zh_ref, dx_ref, dxf_ref, w_ref, yb2_ref, yb2f_ref, lga_ref, lba_ref, ws_ref, bs_ref,
             cw_ref, lgb_ref, lbb_ref, dz_ref, g512_ref, dws_ref, dbs_ref, dvn_ref, fwd_rolled, bwd_rolled, du_s):
        i = pl.program_id(0)
        last = i == nt - 1

        @pl.when(i == 0)
        def _():
            g512_ref[...] = jnp.zeros((G512_ROWS, DA), F32)
            dws_ref[...] = jnp.zeros((HEADS, CHUNK, CHUNK), F32)
            dbs_ref[...] = jnp.zeros((HEADS, CHUNK, 1), F32)

        def add_row(k, v):
            g512_ref[k:k + 1, :] += v

        z_t = z_ref[...].astype(F32)
        nt_dot = lambda a, b: lax.dot_general(a, b, NT_DIMS, preferred_element_type=F32).astype(BF16).astype(F32)
        dy_t = nt_dot(dx_ref[...], w_ref[...])
        dyf = nt_dot(dxf_ref[...], w_ref[DA:2 * DA, :])
        ua, va = z_t[:, 0:DA], z_t[:, DA:2 * DA]
        gu = _gelu(ua)
        gv = _gelu(va)
        lga_v = lga_ref[...]
        vn, xhat_a, rstd_a = _ln_fwd(gv, lga_v, lba_ref[...])
        vnb = vn.astype(BF16)
        causal = (lax.broadcasted_iota(jnp.int32, (CHUNK, CHUNK), 0)
                  >= lax.broadcasted_iota(jnp.int32, (CHUNK, CHUNK), 1)).astype(F32)
        for c in range(nch):
            for h in range(HEADS):
                rs = slice(c * CHUNK, (c + 1) * CHUNK)
                cs = slice(h * CHUNK, (h + 1) * CHUNK)
                vblk = vnb[rs, cs]
                mixed = jnp.dot(ws_ref[h], vblk, preferred_element_type=F32) + bs_ref[h]
                dyb_ = dy_t[rs, cs]
                dmix = dyb_ * gu[rs, cs]
                dmb = dmix.astype(BF16)
                dz_ref[rs, cs] = (dyb_ * mixed * _dgelu(ua[rs, cs])).astype(BF16)
                dvn_ref[rs, cs] = lax.dot_general(ws_ref[h], dmb, TN_DIMS, preferred_element_type=F32)
                dws_ref[h] += causal * lax.dot_general(dmb, vblk, NT_DIMS, preferred_element_type=F32)
                dbs_ref[h] += jnp.sum(dmix, axis=1, keepdims=True)
        dvn = dvn_ref[...]
        add_row(0, jnp.sum(dvn * xhat_a, axis=0, keepdims=True))
        add_row(1, jnp.sum(dvn, axis=0, keepdims=True))
        dgv = _ln_bwd(dvn, xhat_a, rstd_a, lga_v)
        dz_ref[:, DA:2 * DA] = (dgv * _dgelu(va)).astype(BF16)
        lgb_v = lgb_ref[...]
        dyb_e = jnp.concatenate(
            [dy_t[:, DA:2 * DA], jnp.where(last, 0.0, dyf)], axis=0)
        yb2_e = jnp.concatenate([yb2_ref[...], jnp.where(last, 0.0, yb2f_ref[...])], axis=0)
        n_e, xhat_b, rstd_b = _ln_fwd(yb2_e, lgb_v, lbb_ref[...])
        sgn = _sigmoid(n_e)
        dn = dyb_e * _dsilu(n_e, sgn)
        dy2 = _ln_bwd(dn, xhat_b, rstd_b, lgb_v)
        add_row(2, jnp.sum(dy2[:ts], axis=0, keepdims=True))
        add_row(3, jnp.sum(dn[:ts] * xhat_b[:ts], axis=0, keepdims=True))
        add_row(4, jnp.sum(dn[:ts], axis=0, keepdims=True))
        zh = jnp.where(i > 0, zh_ref[...], jnp.zeros_like(zh_ref[...])).astype(F32)
        xb_t, gb_t = z_t[:, 2 * DA:3 * DA], z_t[:, 3 * DA:4 * DA]
        sgb = _sigmoid(gb_t)
        _rolled_copies(fwd_rolled, jnp.concatenate(
            [zh[:, 0:DA] * _sigmoid(zh[:, DA:2 * DA]), xb_t * sgb], axis=0), False)
        _rolled_copies(bwd_rolled, dy2, True)
        for o in range(0, ts, CONV_ROWS):
            acc = jnp.zeros((CONV_ROWS, DA), F32)
            for sh in range(BCONV):
                q, r = divmod(sh, 8)
                acc = acc + cw_ref[BCONV - 1 - sh:BCONV - sh, :] * bwd_rolled[r, 8 * q + o:8 * q + o + CONV_ROWS, :]
            du_s[o:o + CONV_ROWS, :] = acc
        for sh in range(BCONV):
            q, r = divmod(sh, 8)
            acc = jnp.zeros((CONV_ROWS, DA), F32)
            for o in range(0, ts, CONV_ROWS):
                lo = HALO_B - 8 * q + o
                acc = acc + bwd_rolled[0, o:o + CONV_ROWS, :] * fwd_rolled[r, lo:lo + CONV_ROWS, :]
            add_row(8 + BCONV - 1 - sh, jnp.sum(acc, axis=0, keepdims=True))
        du = du_s[...]
        dz_ref[:, 2 * DA:3 * DA] = (du * sgb).astype(BF16)
        dz_ref[:, 3 * DA:4 * DA] = (du * xb_t * sgb * (1.0 - sgb)).astype(BF16)

    row = lambda i: (0, 0)
    nxt = lambda i: jnp.minimum((i + 1) * hb, s // HALO_B - 1)
    return pl.pallas_call(
        body, grid=(nt,), name=name,
        in_specs=[pl.BlockSpec((ts, 4 * DA), lambda i: (i, 0)),
                  pl.BlockSpec((HALO_B, 2 * DA), lambda i: (jnp.maximum(i * hb - 1, 0), 1)),
                  pl.BlockSpec((ts, D), lambda i: (i, 0)),
                  pl.BlockSpec((HALO_B, D), lambda i: (nxt(i), 0)),
                  pl.BlockSpec((D, D), lambda i: (0, 0)),
                  pl.BlockSpec((ts, DA), lambda i: (i, 0)),
                  pl.BlockSpec((HALO_B, DA), lambda i: (nxt(i), 0)),
                  pl.BlockSpec((1, DA), row), pl.BlockSpec((1, DA), row),
                  pl.BlockSpec((HEADS, CHUNK, CHUNK), lambda i: (0, 0, 0)),
                  pl.BlockSpec((HEADS, CHUNK, 1), lambda i: (0, 0, 0)),
                  pl.BlockSpec((BCONV, DA), row), pl.BlockSpec((1, DA), row), pl.BlockSpec((1, DA), row)],
        out_specs=[pl.BlockSpec((ts, 4 * DA), lambda i: (i, 0)),
                   pl.BlockSpec((G512_ROWS, DA), row),
                   pl.BlockSpec((HEADS, CHUNK, CHUNK), lambda i: (0, 0, 0)),
                   pl.BlockSpec((HEADS, CHUNK, 1), lambda i: (0, 0, 0))],
        out_shape=[jax.ShapeDtypeStruct((s, 4 * DA), BF16), jax.ShapeDtypeStruct((G512_ROWS, DA), F32),
                   jax.ShapeDtypeStruct((HEADS, CHUNK, CHUNK), F32),
                   jax.ShapeDtypeStruct((HEADS, CHUNK, 1), F32)],
        scratch_shapes=[pltpu.VMEM((ts, DA), F32), pltpu.VMEM((8, ts + HALO_B, DA), F32),
                        pltpu.VMEM((8, ts + HALO_B, DA), F32), pltpu.VMEM((ts, DA), F32)],
        compiler_params=_params("arbitrary"),
    )(z, z, dx, dx, w, yb2, yb2, lga, lba, wsm, bs_col, cwb, lgb, lbb)


def _dw_cols(a, dy, nb, bn, name):
    s = a.shape[0]
    tm = _tile(s, 2048)
    nt = s // tm
    cpb = 4

    def body(a_ref, dy_ref, o_ref, acc):
        t = pl.program_id(1)
        p = lax.dot_general(a_ref[...], dy_ref[...], TN_DIMS, preferred_element_type=F32)

        @pl.when(t == 0)
        def _():
            for q in range(cpb):
                acc[q] = p[:, q * bn:(q + 1) * bn]

        @pl.when(t > 0)
        def _():
            for q in range(cpb):
                acc[q] += p[:, q * bn:(q + 1) * bn]

        @pl.when(t == nt - 1)
        def _():
            o_ref[...] = acc[...].astype(BF16)

    return pl.pallas_call(
        body, grid=(nb // cpb, nt), name=name,
        in_specs=[pl.BlockSpec((tm, D), lambda j, t: (t, 0)), pl.BlockSpec((tm, cpb * bn), lambda j, t: (t, j))],
        out_specs=pl.BlockSpec((cpb, D, bn), lambda j, t: (j, 0, 0)),
        out_shape=jax.ShapeDtypeStruct((nb, D, bn), BF16),
        scratch_shapes=[pltpu.VMEM((cpb, D, bn), F32)],
        compiler_params=_params("arbitrary", "arbitrary"),
    )(a, dy)


def _dw_rows(a, dy, name):
    s = a.shape[0]
    tm = _tile(s, 4096)
    nt = s // tm
    rb = 512

    def body(a_ref, dy_ref, o_ref, acc):
        t = pl.program_id(1)
        p = lax.dot_general(a_ref[...], dy_ref[...], TN_DIMS, preferred_element_type=F32)

        @pl.when(t == 0)
        def _():
            acc[...] = p

        @pl.when(t > 0)
        def _():
            acc[...] += p

        @pl.when(t == nt - 1)
        def _():
            o_ref[...] = acc[...].astype(BF16)

    return pl.pallas_call(
        body, grid=(D // rb, nt), name=name,
        in_specs=[pl.BlockSpec((tm, rb), lambda j, t: (t, j)), pl.BlockSpec((tm, D), lambda j, t: (t, 0))],
        out_specs=pl.BlockSpec((rb, D), lambda j, t: (j, 0)),
        out_shape=jax.ShapeDtypeStruct((D, D), BF16),
        scratch_shapes=[pltpu.VMEM((rb, D), F32)],
        compiler_params=_params("arbitrary", "arbitrary"),
    )(a, dy)


def _dw_up(h, dup, name):
    s = h.shape[0]
    tm = _tile(s, 4096)
    nt = s // tm

    def body(h_ref, d_ref, o_ref, acc):
        t = pl.program_id(1)
        p = lax.dot_general(d_ref[...], h_ref[...], TN_DIMS, preferred_element_type=F32)

        @pl.when(t == 0)
        def _():
            acc[...] = p

        @pl.when(t > 0)
        def _():
            acc[...] += p

        @pl.when(t == nt - 1)
        def _():
            o_ref[...] = acc[...].astype(BF16)

    return pl.pallas_call(
        body, grid=(NDEV, nt), name=name,
        in_specs=[pl.BlockSpec((tm, D), lambda b, t: (t, 0)),
                  pl.BlockSpec((None, None, tm, FB), lambda b, t: (b % NG, b // NG, t, 0))],
        out_specs=pl.BlockSpec((None, FB, D), lambda b, t: (b, 0, 0)),
        out_shape=jax.ShapeDtypeStruct((NDEV, FB, D), BF16),
        scratch_shapes=[pltpu.VMEM((FB, D), F32)],
        compiler_params=_params("arbitrary", "arbitrary"),
    )(h, dup)


def _dw_dn(a, df, name):
    s = df.shape[0]
    tm = _tile(s, 4096)
    nt = s // tm

    def body(a_ref, d_ref, o_ref, acc):
        t = pl.program_id(1)
        p = lax.dot_general(a_ref[...], d_ref[...], TN_DIMS, preferred_element_type=F32)

        @pl.when(t == 0)
        def _():
            acc[...] = p

        @pl.when(t > 0)
        def _():
            acc[...] += p

        @pl.when(t == nt - 1)
        def _():
            o_ref[...] = acc[...].astype(BF16)

    return pl.pallas_call(
        body, grid=(NG, nt), name=name,
        in_specs=[pl.BlockSpec((None, tm, FB), lambda m, t: (m, t, 0)), pl.BlockSpec((tm, D), lambda m, t: (t, 0))],
        out_specs=pl.BlockSpec((FB, D), lambda m, t: (m, 0)),
        out_shape=jax.ShapeDtypeStruct((DFF, D), BF16),
        scratch_shapes=[pltpu.VMEM((FB, D), F32)],
        compiler_params=_params("arbitrary", "arbitrary"),
    )(a, df)


def _place():
    x, y, c = lax.axis_index("x"), lax.axis_index("y"), lax.axis_index("c")
    chips = [(1 - x, y), (x, 1 - y), (1 - x, 1 - y)]
    return x, y, c, chips


def _zone(shard, dev):
    return lax.dynamic_update_slice(lax.empty((NDEV,) + shard.shape, shard.dtype), shard[None],
                                    (dev,) + (0,) * shard.ndim)


HBM_SPEC = pl.BlockSpec(memory_space=pltpu.HBM)
SEM_SPEC = pl.BlockSpec(memory_space=pltpu.SEMAPHORE)
DATAFLOW = pltpu.SideEffectType.DATAFLOW_SIDE_EFFECTING


def _hbm(a):
    return pltpu.with_memory_space_constraint(a, pltpu.HBM)


def _hbm_like(arrs):
    return [pltpu.HBM(a.shape, a.dtype) for a in arrs]


def _ag_start(srcs, lands, after, name):
    n = len(srcs)
    ns = 8 * n

    def body(*refs):
        src, land = refs[:n], refs[n:2 * n]
        sems = refs[2 * n + 1:2 * n + 1 + ns]
        token = refs[-1]
        x, y, c, chips = _place()
        peers = [(x, y, 1 - c)] + [(*chip, c) for chip in chips]
        for t in range(n):
            for k, to in enumerate(peers):
                pltpu.make_async_remote_copy(
                    src_ref=src[t], dst_ref=land[t].at[4 * x + 2 * y + c],
                    send_sem=sems[2 * (4 * t + k)], recv_sem=sems[2 * (4 * t + k) + 1],
                    device_id=to, device_id_type=MESH).start()
        token[...] = jnp.zeros_like(token)

    res = pl.pallas_call(
        body, name=name,
        in_specs=[HBM_SPEC] * (2 * n) + [ANY],
        out_specs=[SEM_SPEC] * ns + [HBM_SPEC] * (2 * n) + [pl.BlockSpec(memory_space=pltpu.VMEM)],
        out_shape=[pltpu.SemaphoreType.DMA(())] * ns + _hbm_like(srcs) + _hbm_like(lands)
        + [jax.ShapeDtypeStruct((8, 128), F32)],
        input_output_aliases={i: ns + i for i in range(2 * n)},
        compiler_params=pltpu.CompilerParams(has_side_effects=DATAFLOW),
    )(*[_hbm(a) for a in srcs], *[_hbm(a) for a in lands], after)
    sems = [[(res[2 * (4 * t + k)], res[2 * (4 * t + k) + 1]) for k in range(4)] for t in range(n)]
    return sems, res[ns:ns + n], res[ns + n:ns + 2 * n], res[-1]


def _ag_forward(srcs, lands, sems1, after, name):
    n = len(srcs)
    flat1 = [s for t in range(n) for k in range(1, 4) for s in sems1[t][k]]
    n1 = len(flat1)

    def body(*refs):
        src, land = refs[:n], refs[n:2 * n]
        s1 = refs[2 * n:2 * n + n1]
        s2 = refs[2 * n + n1 + 1:2 * n + n1 + 1 + 6 * n]
        x, y, c, chips = _place()
        for j, (cx, cy) in enumerate(chips):
            for t in range(n):
                blk = land[t].at[4 * cx + 2 * cy + c]
                pltpu.make_async_remote_copy(
                    src_ref=src[t], dst_ref=blk, send_sem=s1[2 * (3 * t + j)], recv_sem=s1[2 * (3 * t + j) + 1],
                    device_id=(cx, cy, c), device_id_type=MESH).wait_recv()
                pltpu.make_async_remote_copy(
                    src_ref=blk, dst_ref=blk, send_sem=s2[2 * (3 * t + j)], recv_sem=s2[2 * (3 * t + j) + 1],
                    device_id=(x, y, 1 - c), device_id_type=MESH).start()

    res = pl.pallas_call(
        body, name=name,
        in_specs=[HBM_SPEC] * (2 * n) + [SEM_SPEC] * n1 + [ANY],
        out_specs=[SEM_SPEC] * (6 * n) + [HBM_SPEC] * n,
        out_shape=[pltpu.SemaphoreType.DMA(())] * (6 * n) + _hbm_like(lands),
        input_output_aliases={n + i: 6 * n + i for i in range(n)},
        compiler_params=pltpu.CompilerParams(has_side_effects=DATAFLOW),
    )(*srcs, *lands, *flat1, after)
    sems2 = [[(res[2 * (3 * t + j)], res[2 * (3 * t + j) + 1]) for j in range(3)] for t in range(n)]
    return sems2, res[6 * n:]


def _ag_finish(srcs, lands, sems1, sems2, after, name):
    n = len(srcs)
    flat1 = [s for t in range(n) for k in range(4) for s in sems1[t][k]]
    flat2 = [s for t in range(n) for j in range(3) for s in sems2[t][j]]
    n1, n2 = len(flat1), len(flat2)

    def body(*refs):
        src, land = refs[:n], refs[n:2 * n]
        s1 = refs[2 * n:2 * n + n1]
        s2 = refs[2 * n + n1:2 * n + n1 + n2]
        x, y, c, chips = _place()
        sib = (x, y, 1 - c)
        for t in range(n):
            own = land[t].at[4 * x + 2 * y + 1 - c]
            pltpu.make_async_remote_copy(
                src_ref=src[t], dst_ref=own, send_sem=s1[8 * t], recv_sem=s1[8 * t + 1],
                device_id=sib, device_id_type=MESH).wait_recv()
            for k in range(4):
                pltpu.make_async_remote_copy(
                    src_ref=src[t], dst_ref=own, send_sem=s1[2 * (4 * t + k)], recv_sem=s1[2 * (4 * t + k) + 1],
                    device_id=sib, device_id_type=MESH).wait_send()
            for j, (cx, cy) in enumerate(chips):
                blk = land[t].at[4 * cx + 2 * cy + 1 - c]
                cp = pltpu.make_async_remote_copy(
                    src_ref=blk, dst_ref=blk, send_sem=s2[2 * (3 * t + j)], recv_sem=s2[2 * (3 * t + j) + 1],
                    device_id=sib, device_id_type=MESH)
                cp.wait_send()
                cp.wait_recv()

    return pl.pallas_call(
        body, name=name,
        in_specs=[HBM_SPEC] * (2 * n) + [SEM_SPEC] * (n1 + n2) + [ANY],
        out_specs=[HBM_SPEC] * n,
        out_shape=_hbm_like(lands),
        input_output_aliases={n + i: i for i in range(n)},
        compiler_params=pltpu.CompilerParams(has_side_effects=DATAFLOW),
    )(*srcs, *lands, *flat1, *flat2, after)


def _pair_copies(srcs, dsts, sems):
    x, y, c, _ = _place()
    nt = len(srcs)
    return [pltpu.make_async_remote_copy(
        src_ref=srcs[t].at[2 * j + 1 - c], dst_ref=dsts[t].at[j],
        send_sem=sems[2 * (NCHIP * t + j)], recv_sem=sems[2 * (NCHIP * t + j) + 1],
        device_id=(x, y, 1 - c), device_id_type=MESH) for t in range(nt) for j in range(NCHIP)]


def _pair_start(grads, carry, name):
    nt = len(grads)
    ns = 2 * NCHIP * nt
    zones = [_hbm(lax.empty((NCHIP,) + a.shape[1:], a.dtype)) for a in grads]
    extra = [] if carry is None else [_hbm(carry)]
    ne = len(extra)

    def body(*refs):
        for cp in _pair_copies(refs[:nt], refs[nt:2 * nt], refs[2 * nt + ne:2 * nt + ne + ns]):
            cp.start()

    res = pl.pallas_call(
        body, name=name,
        in_specs=[HBM_SPEC] * (2 * nt + ne),
        out_specs=[SEM_SPEC] * ns + [HBM_SPEC] * (2 * nt + ne),
        out_shape=[pltpu.SemaphoreType.DMA(())] * ns + _hbm_like(grads) + _hbm_like(zones) + _hbm_like(extra),
        input_output_aliases={i: ns + i for i in range(2 * nt + ne)},
        compiler_params=pltpu.CompilerParams(has_side_effects=DATAFLOW),
    )(*[_hbm(a) for a in grads], *zones, *extra)
    handle = (list(res[:ns]), list(res[ns:ns + nt]), list(res[ns + nt:ns + 2 * nt]))
    return handle, (res[ns + 2 * nt] if ne else None)


def _pair_wait(handle, after, name):
    sems, srcs, zones = handle
    nt, ns = len(srcs), len(sems)

    def body(*refs):
        for cp in _pair_copies(refs[:nt], refs[nt:2 * nt], refs[2 * nt:2 * nt + ns]):
            cp.wait_send()
            cp.wait_recv()

    return pl.pallas_call(
        body, name=name,
        in_specs=[HBM_SPEC] * (2 * nt) + [SEM_SPEC] * ns + [ANY],
        out_specs=[HBM_SPEC] * nt,
        out_shape=_hbm_like(zones),
        input_output_aliases={nt + i: i for i in range(nt)},
        compiler_params=pltpu.CompilerParams(has_side_effects=DATAFLOW),
    )(*srcs, *zones, *sems, after)


def _rows_tile(r, row_bytes, cap_bytes):
    best = None
    for tr in range(16, r + 1, 16):
        if r % tr == 0 and tr * row_bytes <= cap_bytes:
            best = tr
    return best if best is not None else r


def _pair_sum(own, got, cidx, name):
    _, _, r, cdim = own.shape
    tr = _rows_tile(r, 2 * cdim, 2 * 1024 * 1024)

    def body(c_ref, a_ref, b_ref, o_ref):
        o_ref[...] = (a_ref[...].astype(F32) + b_ref[...].astype(F32)).astype(BF16)

    return pl.pallas_call(
        body, name=name,
        grid_spec=pltpu.PrefetchScalarGridSpec(
            num_scalar_prefetch=1, grid=(NCHIP, r // tr),
            in_specs=[pl.BlockSpec((None, None, tr, cdim), lambda j, i, c_ref: (j, c_ref[0], i, 0)),
                      pl.BlockSpec((None, tr, cdim), lambda j, i, c_ref: (j, i, 0))],
            out_specs=pl.BlockSpec((None, tr, cdim), lambda j, i, c_ref: (j, i, 0))),
        out_shape=jax.ShapeDtypeStruct((NCHIP, r, cdim), BF16),
        compiler_params=_params("arbitrary", "arbitrary"),
    )(cidx, own, got)


def _chip_copies(srcs, zones, slots, sems):
    x, y, c, chips = _place()
    out = []
    for t, (z, l) in enumerate(slots):
        for k, (cx, cy) in enumerate(chips):
            dst = zones[z].at[k] if l is None else zones[z].at[k, l]
            out.append(pltpu.make_async_remote_copy(
                src_ref=srcs[t].at[2 * cx + cy], dst_ref=dst,
                send_sem=sems[2 * (3 * t + k)], recv_sem=sems[2 * (3 * t + k) + 1],
                device_id=(cx, cy, c), device_id_type=MESH))
    return out


def _chip_start(sums, zones, slots, carry, name):
    nt, nz = len(sums), len(zones)
    ns = 6 * nt
    extra = [] if carry is None else [_hbm(carry)]
    ne = len(extra)

    def body(*refs):
        for cp in _chip_copies(refs[:nt], refs[nt:nt + nz], slots, refs[nt + nz + ne:nt + nz + ne + ns]):
            cp.start()

    res = pl.pallas_call(
        body, name=name,
        in_specs=[HBM_SPEC] * (nt + nz + ne),
        out_specs=[SEM_SPEC] * ns + [HBM_SPEC] * (nt + nz + ne),
        out_shape=[pltpu.SemaphoreType.DMA(())] * ns + _hbm_like(sums) + _hbm_like(zones) + _hbm_like(extra),
        input_output_aliases={i: ns + i for i in range(nt + nz + ne)},
        compiler_params=pltpu.CompilerParams(has_side_effects=DATAFLOW),
    )(*[_hbm(a) for a in sums], *zones, *extra)
    return (list(res[:ns]), list(res[ns:ns + nt]), list(res[ns + nt:ns + nt + nz]),
            (res[ns + nt + nz] if ne else None))


def _chip_wait(started, zones, zone_ids, after, name):
    started = [(sums, [(zone_ids.index(z), l) for z, l in slots], sems) for sums, slots, sems in started]
    nz = len(zones)
    flat_src = [a for sums, _, _ in started for a in sums]
    flat_sem = [s for _, _, sems in started for s in sems]
    n_src, n_sem = len(flat_src), len(flat_sem)

    def body(*refs):
        srcs, zs, sems = refs[:n_src], refs[n_src:n_src + nz], refs[n_src + nz:n_src + nz + n_sem]
        so, se = 0, 0
        for sums, slots, sem_list in started:
            for cp in _chip_copies(srcs[so:so + len(sums)], zs, slots, sems[se:se + len(sem_list)]):
                cp.wait_send()
                cp.wait_recv()
            so += len(sums)
            se += len(sem_list)

    return pl.pallas_call(
        body, name=name,
        in_specs=[HBM_SPEC] * (n_src + nz) + [SEM_SPEC] * n_sem + [ANY],
        out_specs=[HBM_SPEC] * nz,
        out_shape=_hbm_like(zones),
        input_output_aliases={n_src + i: i for i in range(nz)},
        compiler_params=pltpu.CompilerParams(has_side_effects=DATAFLOW),
    )(*flat_src, *zones, *flat_sem, after)


def _small_allreduce(parts, y_first, after, name):
    nt = len(parts)

    def body(*refs):
        srcs, outs, bufs = refs[:nt], refs[nt + 1:2 * nt + 1], refs[2 * nt + 1:3 * nt + 1]
        send_sems, recv_sems = refs[3 * nt + 1:]
        x, y, c, _ = _place()
        along = {"c": (x, y, 1 - c), "x": (1 - x, y, c), "y": (x, 1 - y, c)}
        for t in range(nt):
            outs[t][...] = srcs[t][...]
        for step in range(3):
            order = [("c", "y", "x") if t in y_first else ("c", "x", "y") for t in range(nt)]
            copies = [pltpu.make_async_remote_copy(
                src_ref=outs[t], dst_ref=bufs[t].at[step],
                send_sem=send_sems.at[step, t], recv_sem=recv_sems.at[step, t],
                device_id=along[order[t][step]], device_id_type=MESH) for t in range(nt)]
            for cp in copies:
                cp.start()
            for cp in copies:
                cp.wait()
            for t in range(nt):
                outs[t][...] = outs[t][...] + bufs[t][step]

    vm = pl.BlockSpec(memory_space=pltpu.VMEM)
    return pl.pallas_call(
        body, name=name,
        in_specs=[vm] * nt + [ANY], out_specs=[vm] * nt,
        out_shape=[jax.ShapeDtypeStruct(a.shape, F32) for a in parts],
        scratch_shapes=[pltpu.VMEM((3,) + a.shape, F32) for a in parts]
        + [pltpu.SemaphoreType.DMA((3, nt)), pltpu.SemaphoreType.DMA((3, nt))],
        compiler_params=pltpu.CompilerParams(has_side_effects=True, vmem_limit_bytes=VMEM_LIMIT),
    )(*parts, after)


def _adam_math(w, g, m, v):
    m2 = ADAM_B1 * m + (1.0 - ADAM_B1) * g
    v2 = ADAM_B2 * v + (1.0 - ADAM_B2) * (g * g)
    m_hat = m2 / (1.0 - ADAM_B1 ** ADAM_STEP)
    v_hat = v2 / (1.0 - ADAM_B2 ** ADAM_STEP)
    delta = -ADAM_LR * (m_hat / (jnp.sqrt(v_hat) + ADAM_EPS) + ADAM_WD * w)
    return delta, m2, v2


def _adam_big(w, m, v, parts, mine, chip, name):
    nl, r, cdim = w.shape
    tr = _rows_tile(r, 4 * cdim, 3 * 512 * 1024)

    def body(c_ref, w_ref, m_ref, v_ref, p_ref, *rest):
        mine_refs, (g_ref, d_ref, mo_ref, vo_ref) = rest[:nl], rest[nl:]
        own = mine_refs[0][...]
        for l in range(1, nl):
            own = jnp.where(pl.program_id(0) == l, mine_refs[l][...], own)
        g = ((p_ref[0].astype(F32) + p_ref[1].astype(F32)) + p_ref[2].astype(F32)) + own.astype(F32)
        delta, m2, v2 = _adam_math(w_ref[...], g, m_ref[...], v_ref[...])
        g_ref[...] = g
        d_ref[...] = delta
        mo_ref[...] = m2
        vo_ref[...] = v2

    spec = pl.BlockSpec((None, tr, cdim), lambda l, i, c_ref: (l, i, 0))
    mine_specs = [pl.BlockSpec((None, tr, cdim), lambda l, i, c_ref, ll=ll: (c_ref[0], jnp.where(l == ll, i, 0), 0))
                  for ll in range(nl)]
    return pl.pallas_call(
        body, name=name,
        grid_spec=pltpu.PrefetchScalarGridSpec(
            num_scalar_prefetch=1, grid=(nl, r // tr),
            in_specs=[spec, spec, spec, pl.BlockSpec((3, None, tr, cdim), lambda l, i, c_ref: (0, l, i, 0))]
            + mine_specs,
            out_specs=[spec] * 4),
        out_shape=[jax.ShapeDtypeStruct(w.shape, F32)] * 4,
        compiler_params=_params("arbitrary", "arbitrary"),
    )(chip, w, m, v, parts, *mine)


def _adam_small(ws, gs, ms, vs, name):
    n = len(ws)

    def body(*refs):
        w_r, g_r, m_r, v_r = refs[:n], refs[n:2 * n], refs[2 * n:3 * n], refs[3 * n:4 * n]
        d_o, m_o, v_o = refs[4 * n:5 * n], refs[5 * n:6 * n], refs[6 * n:7 * n]
        for t in range(n):
            delta, m2, v2 = _adam_math(w_r[t][...], g_r[t][...], m_r[t][...], v_r[t][...])
            d_o[t][...] = delta
            m_o[t][...] = m2
            v_o[t][...] = v2

    vm = pl.BlockSpec(memory_space=pltpu.VMEM)
    shapes = [jax.ShapeDtypeStruct(a.shape, F32) for a in ws]
    return pl.pallas_call(
        body, name=name, in_specs=[vm] * (4 * n), out_specs=[vm] * (3 * n), out_shape=shapes * 3,
        compiler_params=pltpu.CompilerParams(vmem_limit_bytes=VMEM_LIMIT),
    )(*ws, *gs, *ms, *vs)


def kernel(x, norm_mix, norm_ffn, norm_final, ab_w_in, a_ln_g, a_ln_b, a_w_s, a_b_s, b_conv_w, b_conv_b, b_ln_g, b_ln_b, ab_w_out, c_w_in, c_conv_w, c_w_out, f_w_up, f_conv_w, f_w_down, loss_target, m_norm_mix, m_norm_ffn, m_norm_final, m_ab_w_in, m_a_ln_g, m_a_ln_b, m_a_w_s, m_a_b_s, m_b_conv_w, m_b_conv_b, m_b_ln_g, m_b_ln_b, m_ab_w_out, m_c_w_in, m_c_conv_w, m_c_w_out, m_f_w_up, m_f_conv_w, m_f_w_down, v_norm_mix, v_norm_ffn, v_norm_final, v_ab_w_in, v_a_ln_g, v_a_ln_b, v_a_w_s, v_a_b_s, v_b_conv_w, v_b_conv_b, v_b_ln_g, v_b_ln_b, v_ab_w_out, v_c_w_in, v_c_conv_w, v_c_w_out, v_f_w_up, v_f_conv_w, v_f_w_down):
    s = x.shape[1]
    x0 = x.reshape(s, D)
    tgt = loss_target.reshape(s, D)
    xi, yi, ci = lax.axis_index("x"), lax.axis_index("y"), lax.axis_index("c")
    dev = 4 * xi + 2 * yi + ci
    cidx = ci.astype(jnp.int32).reshape(1)

    bf = lambda a: a.astype(BF16)
    slab_w = 6 * CHUNK
    pad = lambda a, rows: jnp.pad(a, ((0, rows - a.shape[0]), (0, slab_w - a.shape[1])))
    slab = jnp.concatenate([pad(b_conv_w[0], 32), pad(c_conv_w[0], 8), pad(f_conv_w.reshape(6, FB), 8)], axis=0)
    later = [bf(ab_w_in[0]), bf(ab_w_out[0]), slab, bf(f_w_up[0]), bf(f_w_down[0]), bf(c_w_in[0]), bf(c_w_out[0]),
             bf(f_w_up[1]), bf(f_w_down[1])]
    lands = [_zone(a, dev) for a in later]
    groups = [[0], [1, 2], [3, 4], [5, 6], [7, 8]]
    ag_sems, later, lands, ag_token = _ag_start(later, lands, x0, "ag_start")

    causal = jnp.tril(jnp.ones((CHUNK, CHUNK), F32))
    wsm = (a_w_s[0] * causal).astype(BF16)
    bs_col = a_b_s.reshape(HEADS, CHUNK, 1)
    nm = [norm_mix[0:1], norm_mix[1:2]]
    nf = [norm_ffn[0:1], norm_ffn[1:2]]
    nfin = norm_final.reshape(1, D)

    def pass_on(ts_, after_ici, tag):
        srcs = [later[t] for t in ts_]
        sems1 = [ag_sems[t] for t in ts_]
        sems2, zone = _ag_forward(srcs, [lands[t] for t in ts_], sems1, after_ici, "ag_forward_" + tag)
        return srcs, zone, sems1, sems2

    def arrive(g, after_ici, after_d2d, tag):
        srcs, zone, sems1, sems2 = pass_on(groups[g], after_ici, tag)
        return _ag_finish(srcs, zone, sems1, sems2, after_d2d, "ag_finish_" + tag)

    h0 = _rms_fwd(x0, nm[0], "rms_mix0", after=ag_token)
    (win0,) = arrive(0, h0, h0, "w_in")
    z = _mm_in(h0, win0, "mm_ab_in")
    wout0, slab_g = arrive(1, z, z, "first")
    wout0 = wout0.reshape(D, D)
    bcw = jnp.transpose(slab_g[:, 0:BCONV, 0:DA // NDEV], (1, 0, 2)).reshape(BCONV, DA)
    ccw = jnp.transpose(slab_g[:, 32:35, 0:D // NDEV], (1, 0, 2)).reshape(3, D)
    fcw_g = slab_g[:, 40:46, 0:FB].reshape(2, NG, 2, 3, FB)
    fcws = [fcw_g[:, :, 0], fcw_g[:, :, 1]]
    ycat, yb2 = _ab_fwd(z, a_ln_g, a_ln_b, wsm, bs_col, bcw, b_conv_b, b_ln_g, b_ln_b, "ab_fwd")
    x1, h1 = _mm_out(ycat, wout0, x0, nf[0], "mm_ab_out")
    wup0, wdn0 = arrive(2, x1, x1, "ffn0")
    up0, upc0, x2, h2 = _ffn_fwd(h1, x1, wup0.reshape(2, NG, D, FB), fcws[0], wdn0.reshape(DFF, D), nm[1],
                                 "ffn_fwd0")
    cin, cout = arrive(3, x2, x2, "c")
    cout = cout.reshape(D, D)
    zc = _mm_in(h2, cin, "mm_c_in")
    up_part = pass_on(groups[4][:1], zc, "ffn1_up")
    rc, x3, h3 = _c_fwd(zc, ccw, cout, x2, nf[1], "c_fwd_out")
    dn_part = pass_on(groups[4][1:], rc, "ffn1_down")
    wup1, wdn1 = _ag_finish(*[a + b for a, b in zip(up_part, dn_part)], x3, "ag_finish_ffn1")
    wups = [wup0.reshape(2, NG, D, FB), wup1.reshape(2, NG, D, FB)]
    wdns = [wdn0.reshape(DFF, D), wdn1.reshape(DFF, D)]
    up1, upc1, dx4, dx4b, dnfin, loss_part = _ffn_fwd(h3, x3, wups[1], fcws[1], wdns[1], None, "ffn_fwd1_loss",
                                                      final=(tgt, nfin))

    zshape = lambda *sh: _hbm(lax.empty((3,) + sh, BF16))
    zones = [zshape(D, 2 * D // NDEV), zshape(D // NDEV, D), zshape(D, 3 * D // NDEV), zshape(D // NDEV, D),
             zshape(2, FB, D), zshape(2, DFF // NDEV, D)]
    started = []

    def pair_sums(grads, handle, after, tag):
        del grads
        got = _pair_wait(handle, after, "rs_pair_wait_" + tag)
        return [_pair_sum(b.reshape((NCHIP, 2) + b.shape[1:]), g, cidx, "rs_pair_sum_%s%d" % (tag, t))
                for t, (b, g) in enumerate(zip(handle[1], got))]

    def chip_start(sums, slots, carry, tag):
        sems, sums, new_zones, carry = _chip_start(sums, zones, slots, carry, "rs_chip_start_" + tag)
        zones[:] = new_zones
        started.append((sums, slots, sems))
        return sums, carry

    rows8 = lambda g, r: g.reshape(NDEV, r, D)
    a1, dup1, dx3, dx3b, dnf1, dfcw1 = _ffn_bwd(dx4, up1, upc1, wups[1], fcws[1], wdns[1], x3, nf[1], "ffn_bwd1")
    g_f1 = [_dw_up(h3, dup1, "dw_up1"), rows8(_dw_dn(a1, dx4b, "dw_dn1"), DFF // NDEV)]
    hd_f1, dx3b = _pair_start(g_f1, dx3b, "rs_pair_start_f1")
    g_cout = rows8(_dw_rows(rc, dx3b, "dw_c_out"), D // NDEV)
    s_f1 = pair_sums(g_f1, hd_f1, g_cout, "f1")
    s_f1, dx3b = chip_start(s_f1, [(4, 1), (5, 1)], dx3b, "f1")
    dzc, dccw = _c_bwd(dx3b, cout, zc, ccw, "c_bwd")
    dx2, dx2b, dnm1 = _mm_nt_rms(dzc, cin, x2, nm[1], dx3, True, "mm_c_in_bwd")
    g_c = [_dw_cols(h2, dzc, NDEV, 3 * D // NDEV, "dw_c_in"), g_cout]
    hd_c, dx2 = _pair_start(g_c, dx2, "rs_pair_start_c")
    a0, dup0, dx1, dx1b, dnf0, dfcw0 = _ffn_bwd(dx2, up0, upc0, wups[0], fcws[0], wdns[0], x1, nf[0], "ffn_bwd0")
    s_c = pair_sums(g_c, hd_c, dx1b, "c")
    s_c, dx1b = chip_start(s_c, [(2, None), (3, None)], dx1b, "c")
    g_f0 = [_dw_up(h1, dup0, "dw_up0"), rows8(_dw_dn(a0, dx2b, "dw_dn0"), DFF // NDEV)]
    hd_f0, dx1b = _pair_start(g_f0, dx1b, "rs_pair_start_f0")
    g_wout0 = rows8(_dw_rows(ycat, dx1b, "dw_ab_out"), D // NDEV)
    s_f0 = pair_sums(g_f0, hd_f0, g_wout0, "f0")
    s_f0, dx1b = chip_start(s_f0, [(4, 0), (5, 0)], dx1b, "f0")
    dz, g512, dws, dbs = _ab_bwd(dx1b, wout0, z, yb2, a_ln_g, a_ln_b, wsm, bs_col, bcw, b_ln_g, b_ln_b, "ab_bwd")
    grad_x, dnm0 = _mm_nt_rms(dz, win0, x0, nm[0], dx1, False, "mm_ab_in_bwd")
    g_ab = [_dw_cols(h0, dz, NDEV, 2 * D // NDEV, "dw_ab_in"), g_wout0]
    hd_ab, _ = _pair_start(g_ab, None, "rs_pair_start_ab")

    g1024 = jnp.concatenate([dnm0, dnm1, dnf0, dnf1, dnfin, dccw], axis=0)
    gfc = jnp.concatenate([dfcw0, dfcw1], axis=0).reshape(2 * NG * 2 * 3, FB)
    g1024, g512, dws, dbs, gfc, loss_sum = _small_allreduce(
        [g1024, g512, dws.reshape(HEADS * CHUNK, CHUNK), dbs.reshape(HEADS, CHUNK), gfc, loss_part], (2,),
        hd_ab[1][0], "small_allreduce")
    loss = loss_sum[0, 0]
    s_ab = pair_sums(g_ab, hd_ab, g1024, "ab")
    s_ab, _ = chip_start(s_ab, [(0, None), (1, None)], None, "ab")
    p_cin, p_cout, p_wup, p_wdn = _chip_wait(started[:3], zones[2:], [2, 3, 4, 5], s_ab[0], "rs_chip_wait_early")

    chip = (2 * xi + yi).astype(jnp.int32).reshape(1)

    def big_update(w, m, v, parts, mine, name):
        shp = w.shape
        w3, m3, v3 = (a.reshape((-1,) + shp[-2:]) for a in (w, m, v))
        p4 = parts.reshape((3,) + w3.shape)
        return [o.reshape(shp) for o in _adam_big(w3, m3, v3, p4, mine, chip, name)]

    u_cin = big_update(c_w_in, m_c_w_in, v_c_w_in, p_cin, [s_c[0]], "adam_c_w_in")
    u_cout = big_update(c_w_out, m_c_w_out, v_c_w_out, p_cout, [s_c[1]], "adam_c_w_out")
    tr_ = lambda a: jnp.swapaxes(a, 1, 2)
    u_wup = [tr_(o) for o in big_update(tr_(f_w_up), tr_(m_f_w_up), tr_(v_f_w_up), p_wup,
                                        [s_f0[0], s_f1[0]], "adam_f_w_up")]
    u_wdn = big_update(f_w_down, m_f_w_down, v_f_w_down, p_wdn, [s_f0[1], s_f1[1]], "adam_f_w_down")
    p_win0, p_wout0 = _chip_wait(started[3:], zones[:2], [0, 1], u_wdn[0], "rs_chip_wait_late")
    u_win0 = big_update(ab_w_in, m_ab_w_in, v_ab_w_in, p_win0, [s_ab[0]], "adam_ab_w_in")
    u_wout0 = big_update(ab_w_out, m_ab_w_out, v_ab_w_out, p_wout0, [s_ab[1]], "adam_ab_w_out")

    g_norm_mix = g1024[0:2]
    g_norm_ffn = g1024[2:4]
    g_norm_final = g1024[4:5]
    g_ccw = lax.dynamic_slice(g1024[5:8], (0, dev * (D // NDEV)), (3, D // NDEV))
    g_bcw = lax.dynamic_slice(g512[8:8 + BCONV], (0, dev * (DA // NDEV)), (BCONV, DA // NDEV))
    gfc = gfc.reshape(2, NG, 2, 3, FB)
    g_fcw = lax.dynamic_slice(gfc, (0, dev % NG, dev // NG, 0, 0), (2, 1, 1, 3, FB)).reshape(2, 3, FB)
    small_w = [norm_mix, norm_ffn, nfin, a_ln_g, a_ln_b, a_w_s[0], a_b_s[0], b_conv_w[0], b_conv_b,
               b_ln_g, b_ln_b, c_conv_w[0], f_conv_w]
    small_g = [g_norm_mix, g_norm_ffn, g_norm_final, g512[0:1], g512[1:2],
               dws.reshape(HEADS, CHUNK, CHUNK), dbs, g_bcw, g512[2:3],
               g512[3:4], g512[4:5], g_ccw, g_fcw]
    small_m = [m_norm_mix, m_norm_ffn, m_norm_final.reshape(1, D), m_a_ln_g, m_a_ln_b, m_a_w_s[0], m_a_b_s[0],
               m_b_conv_w[0], m_b_conv_b, m_b_ln_g, m_b_ln_b, m_c_conv_w[0], m_f_conv_w]
    small_v = [v_norm_mix, v_norm_ffn, v_norm_final.reshape(1, D), v_a_ln_g, v_a_ln_b, v_a_w_s[0], v_a_b_s[0],
               v_b_conv_w[0], v_b_conv_b, v_b_ln_g, v_b_ln_b, v_c_conv_w[0], v_f_conv_w]
    upd = _adam_small(small_w, small_g, small_m, small_v, "adam_small")
    ns = len(small_w)
    orig = [norm_mix, norm_ffn, norm_final, a_ln_g, a_ln_b, a_w_s, a_b_s, b_conv_w, b_conv_b,
            b_ln_g, b_ln_b, c_conv_w, f_conv_w]
    sg_out = [g.reshape(o.shape) for g, o in zip(small_g, orig)]
    sd_out = [a.reshape(o.shape) for a, o in zip(upd[0:ns], orig)]
    sm_out = [a.reshape(o.shape) for a, o in zip(upd[ns:2 * ns], orig)]
    sv_out = [a.reshape(o.shape) for a, o in zip(upd[2 * ns:3 * ns], orig)]

    def assemble(small, k):
        return [small[0], small[1], small[2], u_win0[k], small[3], small[4], small[5], small[6], small[7],
                small[8], small[9], small[10], u_wout0[k], u_cin[k], small[11], u_cout[k], u_wup[k],
                small[12], u_wdn[k]]

    grads = assemble(sg_out, 0)
    deltas = assemble(sd_out, 1)
    new_m = assemble(sm_out, 2)
    new_v = assemble(sv_out, 3)
    return (loss, grad_x.reshape(1, s, D), *grads, *deltas, *new_m, *new_v)
```

```python
import math

import jax
import jax.numpy as jnp
from jax import lax
from jax.experimental import pallas as pl
from jax.experimental.pallas import tpu as pltpu

F32 = jnp.float32
BF16 = jnp.bfloat16

D = 1024
DA = 512
HEADS = 4
CHUNK = 128
DFF = 2816
NDEV = 8
NCHIP = 4
FB = DFF * 2 // NDEV
NG = DFF // FB
BCONV = 31
EPS = 1e-6
HALO = 16
HALO_B = 32
RC = 32
NPART = 2
VMEM_LIMIT = 52 * 1024 * 1024
INV_SQRT2 = 1.0 / math.sqrt(2.0)
INV_SQRT_2PI = 1.0 / math.sqrt(2.0 * math.pi)

ADAM_LR = 0.001
ADAM_B1 = 0.9
ADAM_B2 = 0.999
ADAM_EPS = 1e-08
ADAM_WD = 0.01
ADAM_STEP = 10

MESH = pl.DeviceIdType.MESH
ANY = pl.BlockSpec(memory_space=pl.ANY)
NT_DIMS = (((1,), (1,)), ((), ()))
TN_DIMS = (((0,), (0,)), ((), ()))


def _params(*sem):
    return pltpu.CompilerParams(dimension_semantics=sem, vmem_limit_bytes=VMEM_LIMIT)


def _tile(s, want):
    return min(want, s)


def _sigmoid(x):
    return jax.nn.sigmoid(x)


def _dsilu(x, sg):
    return sg * (1.0 + x * (1.0 - sg))


def _gelu(x):
    return 0.5 * x * (1.0 + lax.erf(x * INV_SQRT2))


def _dgelu(x):
    return 0.5 * (1.0 + lax.erf(x * INV_SQRT2)) + x * jnp.exp(-0.5 * x * x) * INV_SQRT_2PI


def _ln_fwd(x, g, b):
    mu = jnp.mean(x, axis=-1, keepdims=True)
    xc = x - mu
    var = jnp.mean(xc * xc, axis=-1, keepdims=True)
    rstd = lax.rsqrt(var + EPS)
    xhat = xc * rstd
    return xhat * g + b, xhat, rstd


def _ln_bwd(dy, xhat, rstd, g):
    dxh = dy * g
    m1 = jnp.mean(dxh, axis=-1, keepdims=True)
    m2 = jnp.mean(dxh * xhat, axis=-1, keepdims=True)
    return rstd * (dxh - m1 - xhat * m2)


def _rms_bwd_math(dh, x, g):
    r = lax.rsqrt(jnp.mean(x * x, axis=-1, keepdims=True) + EPS)
    xhat = x * r
    dg = jnp.sum(dh * xhat, axis=0, keepdims=True)
    u = dh * g
    dx = r * (u - xhat * jnp.mean(u * xhat, axis=-1, keepdims=True))
    return dx, dg


def _conv3(xe, cw, halo):
    x0 = xe[halo:]
    x1 = pltpu.roll(xe, 1, 0)[halo:]
    x2 = pltpu.roll(xe, 2, 0)[halo:]
    return cw[2] * x0 + cw[1] * x1 + cw[0] * x2, (x0, x1, x2)


def _conv3_bwd_in(dce, cw, ts):
    n = dce.shape[0]
    d1 = pltpu.roll(dce, n - 1, 0)[:ts]
    d2 = pltpu.roll(dce, n - 2, 0)[:ts]
    return cw[2] * dce[:ts] + cw[1] * d1 + cw[0] * d2


def _conv3_bwd_w(dc, taps):
    x0, x1, x2 = taps
    return [jnp.sum(dc * x2, axis=0, keepdims=True), jnp.sum(dc * x1, axis=0, keepdims=True),
            jnp.sum(dc * x0, axis=0, keepdims=True)]


def _rms_fwd(x, g, name, after=None):
    s = x.shape[0]
    ts = _tile(s, 512)

    def body(x_ref, g_ref, *rest):
        h_ref = rest[-1]
        xv = x_ref[...]
        r = lax.rsqrt(jnp.mean(xv * xv, axis=-1, keepdims=True) + EPS)
        h_ref[...] = (xv * r * g_ref[...]).astype(BF16)

    extra = [] if after is None else [after]
    return pl.pallas_call(
        body, grid=(s // ts,), name=name,
        in_specs=[pl.BlockSpec((ts, D), lambda i: (i, 0)), pl.BlockSpec((1, D), lambda i: (0, 0))]
        + [ANY] * len(extra),
        out_specs=pl.BlockSpec((ts, D), lambda i: (i, 0)),
        out_shape=jax.ShapeDtypeStruct((s, D), BF16),
        compiler_params=_params("parallel"),
    )(x, g, *extra)


MXU_COLS = 256


def _pair(bn):
    return 1 if bn % MXU_COLS == 0 else 2


def _cols(w_ref, b, pair):
    return w_ref[b] if pair == 1 else jnp.concatenate([w_ref[b + q] for q in range(pair)], axis=1)


def _mm_in(h, wblk, name):
    s = h.shape[0]
    nb, _, bn = wblk.shape
    pair = _pair(bn)
    ts = _tile(s, 1024)

    def body(h_ref, w_ref, o_ref):
        hv = h_ref[...]
        for b in range(0, nb, pair):
            o_ref[:, b * bn:(b + pair) * bn] = jnp.dot(hv, _cols(w_ref, b, pair),
                                                       preferred_element_type=F32).astype(BF16)

    return pl.pallas_call(
        body, grid=(s // ts,), name=name,
        in_specs=[pl.BlockSpec((ts, D), lambda i: (i, 0)), pl.BlockSpec((nb, D, bn), lambda i: (0, 0, 0))],
        out_specs=pl.BlockSpec((ts, nb * bn), lambda i: (i, 0)),
        out_shape=jax.ShapeDtypeStruct((s, nb * bn), BF16),
        compiler_params=_params("parallel"),
    )(h, wblk)


def _rms_math(xv, g):
    r = lax.rsqrt(jnp.mean(xv * xv, axis=-1, keepdims=True) + EPS)
    return (xv * r * g).astype(BF16)


def _mm_out(y, w, xres, gnext, name):
    s = y.shape[0]
    ts = _tile(s, 1024)

    def body(y_ref, w_ref, x_ref, g_ref, o_ref, h_ref):
        xn = x_ref[...] + jnp.dot(y_ref[...], w_ref[...], preferred_element_type=F32)
        o_ref[...] = xn
        h_ref[...] = _rms_math(xn, g_ref[...])

    return pl.pallas_call(
        body, grid=(s // ts,), name=name,
        in_specs=[pl.BlockSpec((ts, D), lambda i: (i, 0)), pl.BlockSpec((D, D), lambda i: (0, 0)),
                  pl.BlockSpec((ts, D), lambda i: (i, 0)), pl.BlockSpec((1, D), lambda i: (0, 0))],
        out_specs=[pl.BlockSpec((ts, D), lambda i: (i, 0)), pl.BlockSpec((ts, D), lambda i: (i, 0))],
        out_shape=[jax.ShapeDtypeStruct((s, D), F32), jax.ShapeDtypeStruct((s, D), BF16)],
        compiler_params=_params("parallel"),
    )(y, w, xres, gnext)


CONV_ROWS = 32


def _rolled_copies(dst_ref, xe, back):
    n = xe.shape[0]
    dst_ref[0] = xe
    for r in range(1, 8):
        dst_ref[r] = pltpu.roll(xe, n - r if back else r, 0)


def _conv31(rolled_ref, cw_ref, ts, out_ref, bias):
    for o in range(0, ts, CONV_ROWS):
        acc = jnp.zeros((CONV_ROWS, DA), F32) + bias
        for sh in range(BCONV):
            q, r = divmod(sh, 8)
            lo = HALO_B - 8 * q + o
            acc = acc + cw_ref[BCONV - 1 - sh:BCONV - sh, :] * rolled_ref[r, lo:lo + CONV_ROWS, :]
        out_ref[o:o + CONV_ROWS, :] = acc


def _ab_fwd(z, lga, lba, wsm, bs_col, cwb, cbb, lgb, lbb, name):
    s = z.shape[0]
    ts = _tile(s, 256)
    hb = ts // HALO_B

    def body(z_ref, zh_ref, lga_ref, lba_ref, ws_ref, bs_ref, cw_ref, cb_ref, lgb_ref, lbb_ref,
             y_ref, yb2_ref, rolled):
        i = pl.program_id(0)
        z_t = z_ref[...].astype(F32)
        gu = _gelu(z_t[:, 0:DA])
        gv = _gelu(z_t[:, DA:2 * DA])
        vn, _, _ = _ln_fwd(gv, lga_ref[...], lba_ref[...])
        vnb = vn.astype(BF16)
        for c in range(ts // CHUNK):
            for h in range(HEADS):
                rs = slice(c * CHUNK, (c + 1) * CHUNK)
                cs = slice(h * CHUNK, (h + 1) * CHUNK)
                mixed = jnp.dot(ws_ref[h], vnb[rs, cs], preferred_element_type=F32) + bs_ref[h]
                y_ref[rs, cs] = (gu[rs, cs] * mixed).astype(BF16)
        zh = jnp.where(i > 0, zh_ref[...], jnp.zeros_like(zh_ref[...])).astype(F32)
        xb = jnp.concatenate([zh[:, 0:DA], z_t[:, 2 * DA:3 * DA]], axis=0)
        gb = jnp.concatenate([zh[:, DA:2 * DA], z_t[:, 3 * DA:4 * DA]], axis=0)
        _rolled_copies(rolled, xb * _sigmoid(gb), False)
        _conv31(rolled, cw_ref, ts, yb2_ref, cb_ref[...])
        nb_, _, _ = _ln_fwd(yb2_ref[...], lgb_ref[...], lbb_ref[...])
        y_ref[:, DA:2 * DA] = (nb_ * _sigmoid(nb_)).astype(BF16)

    row = lambda i: (0, 0)
    return pl.pallas_call(
        body, grid=(s // ts,), name=name,
        in_specs=[pl.BlockSpec((ts, 4 * DA), lambda i: (i, 0)),
                  pl.BlockSpec((HALO_B, 2 * DA), lambda i: (jnp.maximum(i * hb - 1, 0), 1)),
                  pl.BlockSpec((1, DA), row), pl.BlockSpec((1, DA), row),
                  pl.BlockSpec((HEADS, CHUNK, CHUNK), lambda i: (0, 0, 0)),
                  pl.BlockSpec((HEADS, CHUNK, 1), lambda i: (0, 0, 0)),
                  pl.BlockSpec((BCONV, DA), row), pl.BlockSpec((1, DA), row),
                  pl.BlockSpec((1, DA), row), pl.BlockSpec((1, DA), row)],
        out_specs=[pl.BlockSpec((ts, 2 * DA), lambda i: (i, 0)), pl.BlockSpec((ts, DA), lambda i: (i, 0))],
        out_shape=[jax.ShapeDtypeStruct((s, 2 * DA), BF16), jax.ShapeDtypeStruct((s, DA), F32)],
        scratch_shapes=[pltpu.VMEM((8, ts + HALO_B, DA), F32)],
        compiler_params=_params("parallel"),
    )(z, z, lga, lba, wsm, bs_col, cwb, cbb, lgb, lbb)


def _c_fwd(zc, cw, w, xres, gnext, name):
    s = zc.shape[0]
    ts = _tile(s, 512)
    hb = ts // HALO

    def body(z_ref, ch_ref, xh_ref, cw_ref, w_ref, x_ref, g_ref, r_ref, o_ref, h_ref):
        i = pl.program_id(0)
        z_t = z_ref[...].astype(F32)
        ph = jnp.where(i > 0, ch_ref[...].astype(F32) * xh_ref[...].astype(F32), 0.0)
        pe = jnp.concatenate([ph, z_t[:, D:2 * D] * z_t[:, 2 * D:3 * D]], axis=0)
        q, _ = _conv3(pe, [cw_ref[k:k + 1, :] for k in range(3)], HALO)
        r = (z_t[:, 0:D] * q).astype(BF16)
        r_ref[...] = r
        xn = x_ref[...] + jnp.dot(r, w_ref[...], preferred_element_type=F32)
        o_ref[...] = xn
        h_ref[...] = _rms_math(xn, g_ref[...])

    halo = lambda col: pl.BlockSpec((HALO, D), lambda i: (jnp.maximum(i * hb - 1, 0), col))
    tile = pl.BlockSpec((ts, D), lambda i: (i, 0))
    return pl.pallas_call(
        body, grid=(s // ts,), name=name,
        in_specs=[pl.BlockSpec((ts, 3 * D), lambda i: (i, 0)), halo(1), halo(2),
                  pl.BlockSpec((3, D), lambda i: (0, 0)), pl.BlockSpec((D, D), lambda i: (0, 0)), tile,
                  pl.BlockSpec((1, D), lambda i: (0, 0))],
        out_specs=[tile, tile, tile],
        out_shape=[jax.ShapeDtypeStruct((s, D), BF16), jax.ShapeDtypeStruct((s, D), F32),
                   jax.ShapeDtypeStruct((s, D), BF16)],
        compiler_params=_params("parallel"),
    )(zc, zc, zc, cw, w, xres, gnext)


def _final_math(xv, tv, gv):
    r = lax.rsqrt(jnp.mean(xv * xv, axis=-1, keepdims=True) + EPS)
    xhat = xv * r
    e = xhat * gv - tv
    part = 0.5 * jnp.sum(jnp.mean(e * e, axis=-1, keepdims=True), axis=0, keepdims=True)
    dy = e * (1.0 / D)
    dgp = jnp.sum(dy * xhat, axis=0, keepdims=True)
    u = dy * gv
    dx = r * (u - xhat * jnp.mean(u * xhat, axis=-1, keepdims=True))
    return dx, dgp, jnp.broadcast_to(part, (1, 128))


def _ffn_fwd(h, xres, wup, fcw, wdn, gnext, name, final=None):
    s = h.shape[0]
    ts = _tile(s, 512)
    hb = ts // HALO

    def body(h_ref, hh_ref, w_ref, cw_ref, wd_ref, x_ref, *rest):
        if final is not None:
            t_ref, gf_ref, up_ref, upc_ref, dx_ref, dxb_ref, dg_ref, loss_ref, up_s, xo_ref = rest
        elif gnext is not None:
            gn_ref, up_ref, upc_ref, xo_ref, hn_ref, up_s = rest
        else:
            up_ref, upc_ref, xo_ref, up_s = rest
        i = pl.program_id(0)
        m = pl.program_id(1)
        @pl.when(m == 0)
        def _():
            xo_ref[...] = x_ref[...]

        halo = jnp.where(i > 0, hh_ref[...], jnp.zeros_like(hh_ref[...]))
        hx = jnp.concatenate([halo, h_ref[...]], axis=0)
        acts = []
        for gv in range(2):
            up_s[gv] = jnp.dot(hx, w_ref[gv], preferred_element_type=F32)
            x0 = up_s[gv, HALO:HALO + ts, :]
            up_ref[gv] = x0.astype(BF16)
            upc = (cw_ref[gv, 2:3, :] * x0 + cw_ref[gv, 1:2, :] * up_s[gv, HALO - 1:HALO - 1 + ts, :]
                   + cw_ref[gv, 0:1, :] * up_s[gv, HALO - 2:HALO - 2 + ts, :])
            upc_ref[gv] = upc.astype(BF16)
            acts.append(upc)
        a = acts[0] * _sigmoid(acts[0]) * acts[1]
        xo_ref[...] += jnp.dot(a.astype(BF16), wd_ref[...], preferred_element_type=F32)

        if final is not None:
            @pl.when(m == NG - 1)
            def _():
                dx, dgp, part = _final_math(xo_ref[...], t_ref[...], gf_ref[...])
                dx_ref[...] = dx
                dxb_ref[...] = dx.astype(BF16)

                @pl.when(i == 0)
                def _():
                    dg_ref[...] = dgp
                    loss_ref[...] = part

                @pl.when(i > 0)
                def _():
                    dg_ref[...] += dgp
                    loss_ref[...] += part

        elif gnext is not None:
            @pl.when(m == NG - 1)
            def _():
                hn_ref[...] = _rms_math(xo_ref[...], gn_ref[...])

    tile = pl.BlockSpec((ts, D), lambda i, m: (i, 0))
    row = lambda n: pl.BlockSpec((1, n), lambda i, m: (0, 0))
    scratch = [pltpu.VMEM((2, ts + HALO, FB), F32)]
    if final is not None:
        more_in, more_ops = [tile, row(D)], list(final)
        more_out = [tile, tile, row(D), row(128)]
        more_shape = [jax.ShapeDtypeStruct((s, D), F32), jax.ShapeDtypeStruct((s, D), BF16),
                      jax.ShapeDtypeStruct((1, D), F32), jax.ShapeDtypeStruct((1, 128), F32)]
        scratch.append(pltpu.VMEM((ts, D), F32))
    else:
        nxt = gnext is not None
        more_in, more_ops = ([row(D)], [gnext]) if nxt else ([], [])
        more_out = [tile] + ([tile] if nxt else [])
        more_shape = [jax.ShapeDtypeStruct((s, D), F32)] + ([jax.ShapeDtypeStruct((s, D), BF16)] if nxt else [])
    return pl.pallas_call(
        body, grid=(s // ts, NG), name=name,
        in_specs=[tile,
                  pl.BlockSpec((HALO, D), lambda i, m: (jnp.maximum(i * hb - 1, 0), 0)),
                  pl.BlockSpec((2, None, D, FB), lambda i, m: (0, m, 0, 0)),
                  pl.BlockSpec((2, None, 3, FB), lambda i, m: (0, m, 0, 0)),
                  pl.BlockSpec((FB, D), lambda i, m: (m, 0)),
                  tile] + more_in,
        out_specs=[pl.BlockSpec((None, 2, ts, FB), lambda i, m: (m, 0, i, 0)),
                   pl.BlockSpec((None, 2, ts, FB), lambda i, m: (m, 0, i, 0))] + more_out,
        out_shape=[jax.ShapeDtypeStruct((NG, 2, s, FB), BF16), jax.ShapeDtypeStruct((NG, 2, s, FB), BF16)] + more_shape,
        scratch_shapes=scratch,
        compiler_params=_params("arbitrary", "arbitrary"),
    )(h, h, wup, fcw, wdn, xres, *more_ops)


def _ffn_bwd(df, up, upc, wup, fcw, wdn, xin, g, name):
    s = df.shape[0]
    ts = _tile(s, 512)
    nt = s // ts

    def body(df_ref, up_ref, upc_ref, w_ref, cw_ref, wd_ref, x_ref, g_ref,
             a_ref, dup_ref, dx_ref, dxb_ref, dg_ref, dcw_ref, carry, acc, tacc, dcs_ref):
        i = pl.program_id(0)
        m = pl.program_id(1)
        first = i == 0
        @pl.when(first)
        def _():
            carry[m] = jnp.zeros((2, 8, FB), F32)
            dcw_ref[m] = jnp.zeros((2, 3, FB), F32)

        @pl.when(m == 0)
        def _():
            acc[...] = jnp.zeros((ts, D), F32)

        cws = [[cw_ref[gv, k:k + 1, :] for k in range(3)] for gv in range(2)]
        part = ts // NPART
        das = [lax.dot_general(df_ref[p * part:(p + 1) * part, :].astype(BF16), wd_ref[...], NT_DIMS,
                               preferred_element_type=F32) for p in range(NPART)]

        tacc[...] = jnp.zeros((2, 3, 8, FB), F32)
        dcs_ref[:, ts:ts + 8, :] = carry[m]
        for r in reversed(range(ts // RC)):
            rs = slice(r * RC, (r + 1) * RC)
            gate = upc_ref[0, rs, :].astype(F32)
            val = upc_ref[1, rs, :].astype(F32)
            sg = _sigmoid(gate)
            sl = gate * sg
            a_ref[rs, :] = (sl * val).astype(BF16)
            da_c = das[(r * RC) // part][(r * RC) % part:(r * RC) % part + RC]
            dcs = [da_c * val * _dsilu(gate, sg), da_c * sl]
            for gv in range(2):
                dc = dcs[gv]
                dcs_ref[gv, rs, :] = dc
                d1 = dcs_ref[gv, r * RC + 1:(r + 1) * RC + 1, :]
                d2 = dcs_ref[gv, r * RC + 2:(r + 1) * RC + 2, :]
                du = cws[gv][2] * dc + cws[gv][1] * d1 + cws[gv][0] * d2
                dup_ref[gv, rs, :] = du.astype(BF16)
                x0 = up_ref[gv, rs, :].astype(F32)
                for k, dk in enumerate((d2, d1, dc)):
                    p = x0 * dk
                    tacc[gv, k] += sum(p[j:j + 8] for j in range(0, RC, 8))
            if (r * RC) % part == 0:
                ps = slice(r * RC, r * RC + part)
                acc[ps, :] += (
                    lax.dot_general(dup_ref[0, ps, :], w_ref[0], NT_DIMS, preferred_element_type=F32)
                    + lax.dot_general(dup_ref[1, ps, :], w_ref[1], NT_DIMS, preferred_element_type=F32))
        for gv in range(2):
            carry[m, gv] = dcs_ref[gv, 0:8, :]
            for k in range(3):
                dcw_ref[m, gv, k:k + 1, :] += jnp.sum(tacc[gv, k], axis=0, keepdims=True)

        @pl.when(m == NG - 1)
        def _():
            dx, dgp = _rms_bwd_math(acc[...], x_ref[...], g_ref[...])
            dx = df_ref[...] + dx
            dx_ref[...] = dx
            dxb_ref[...] = dx.astype(BF16)

            @pl.when(first)
            def _():
                dg_ref[...] = dgp

            @pl.when(jnp.logical_not(first))
            def _():
                dg_ref[...] += dgp

    rev = lambda i: nt - 1 - i
    return pl.pallas_call(
        body, grid=(nt, NG), name=name,
        in_specs=[pl.BlockSpec((ts, D), lambda i, m: (rev(i), 0)),
                  pl.BlockSpec((None, 2, ts, FB), lambda i, m: (m, 0, rev(i), 0)),
                  pl.BlockSpec((None, 2, ts, FB), lambda i, m: (m, 0, rev(i), 0)),
                  pl.BlockSpec((2, None, D, FB), lambda i, m: (0, m, 0, 0)),
                  pl.BlockSpec((2, None, 3, FB), lambda i, m: (0, m, 0, 0)),
                  pl.BlockSpec((FB, D), lambda i, m: (m, 0)),
                  pl.BlockSpec((ts, D), lambda i, m: (rev(i), 0)),
                  pl.BlockSpec((1, D), lambda i, m: (0, 0))],
        out_specs=[pl.BlockSpec((None, ts, FB), lambda i, m: (m, rev(i), 0)),
                   pl.BlockSpec((None, 2, ts, FB), lambda i, m: (m, 0, rev(i), 0)),
                   pl.BlockSpec((ts, D), lambda i, m: (rev(i), 0)),
                   pl.BlockSpec((ts, D), lambda i, m: (rev(i), 0)),
                   pl.BlockSpec((1, D), lambda i, m: (0, 0)),
                   pl.BlockSpec((NG, 2, 3, FB), lambda i, m: (0, 0, 0, 0))],
        out_shape=[jax.ShapeDtypeStruct((NG, s, FB), BF16), jax.ShapeDtypeStruct((NG, 2, s, FB), BF16),
                   jax.ShapeDtypeStruct((s, D), F32), jax.ShapeDtypeStruct((s, D), BF16),
                   jax.ShapeDtypeStruct((1, D), F32),
                   jax.ShapeDtypeStruct((NG, 2, 3, FB), F32)],
        scratch_shapes=[pltpu.VMEM((NG, 2, 8, FB), F32), pltpu.VMEM((ts, D), F32),
                        pltpu.VMEM((2, 3, 8, FB), F32), pltpu.VMEM((2, ts + 8, FB), F32)],
        compiler_params=_params("arbitrary", "arbitrary"),
    )(df, up, upc, wup, fcw, wdn, xin, g)


def _mm_nt_rms(dy, wblk, x, g, dres, bf16_copy, name):
    s = dy.shape[0]
    nb, _, bn = wblk.shape
    pair = _pair(bn)
    ts = _tile(s, 512)

    def body(dy_ref, w_ref, x_ref, g_ref, dr_ref, dx_ref, *rest):
        dg_ref = rest[-1]
        i = pl.program_id(0)
        acc = jnp.zeros((ts, D), F32)
        for b in range(0, nb, pair):
            acc = acc + lax.dot_general(dy_ref[:, b * bn:(b + pair) * bn], _cols(w_ref, b, pair), NT_DIMS,
                                        preferred_element_type=F32)
        dx, dgp = _rms_bwd_math(acc, x_ref[...], g_ref[...])
        dx = dr_ref[...] + dx
        dx_ref[...] = dx
        if bf16_copy:
            rest[0][...] = dx.astype(BF16)

        @pl.when(i == 0)
        def _():
            dg_ref[...] = dgp

        @pl.when(i > 0)
        def _():
            dg_ref[...] += dgp

    tile = pl.BlockSpec((ts, D), lambda i: (i, 0))
    return pl.pallas_call(
        body, grid=(s // ts,), name=name,
        in_specs=[pl.BlockSpec((ts, nb * bn), lambda i: (i, 0)), pl.BlockSpec((nb, D, bn), lambda i: (0, 0, 0)),
                  tile, pl.BlockSpec((1, D), lambda i: (0, 0)), tile],
        out_specs=[tile] + ([tile] if bf16_copy else []) + [pl.BlockSpec((1, D), lambda i: (0, 0))],
        out_shape=[jax.ShapeDtypeStruct((s, D), F32)] + ([jax.ShapeDtypeStruct((s, D), BF16)] if bf16_copy else [])
        + [jax.ShapeDtypeStruct((1, D), F32)],
        compiler_params=_params("arbitrary"),
    )(dy, wblk, x, g, dres)


def _c_bwd(dx, w, zc, cw, name):
    s = dx.shape[0]
    ts = _tile(s, 512)
    nt = s // ts
    hb = ts // HALO

    def body(dx_ref, dxf_ref, w_ref, z_ref, ch_ref, xh_ref, bf_ref, cw_ref, dz_ref, dcw_ref):
        i = pl.program_id(0)
        cwv = [cw_ref[k:k + 1, :] for k in range(3)]
        dre = lax.dot_general(jnp.concatenate([dx_ref[...], dxf_ref[...]], axis=0), w_ref[...], NT_DIMS,
                              preferred_element_type=F32).astype(BF16).astype(F32)
        z_t = z_ref[...].astype(F32)
        bg, cg, xv = z_t[:, 0:D], z_t[:, D:2 * D], z_t[:, 2 * D:3 * D]
        ph = jnp.where(i > 0, ch_ref[...].astype(F32) * xh_ref[...].astype(F32), 0.0)
        pe = jnp.concatenate([ph, cg * xv], axis=0)
        q, taps = _conv3(pe, cwv, HALO)
        drv = dre[0:ts]
        dq = drv * bg
        dqf = jnp.where(i < nt - 1, dre[ts:ts + HALO] * bf_ref[...].astype(F32), 0.0)
        dp = _conv3_bwd_in(jnp.concatenate([dq, dqf], axis=0), cwv, ts)
        dz_ref[:, 0:D] = (drv * q).astype(BF16)
        dz_ref[:, D:2 * D] = (dp * xv).astype(BF16)
        dz_ref[:, 2 * D:3 * D] = (dp * cg).astype(BF16)
        rows = _conv3_bwd_w(dq, taps)

        @pl.when(i == 0)
        def _():
            for k in range(3):
                dcw_ref[k:k + 1, :] = rows[k]

        @pl.when(i > 0)
        def _():
            for k in range(3):
                dcw_ref[k:k + 1, :] += rows[k]

    past = lambda col: pl.BlockSpec((HALO, D), lambda i: (jnp.maximum(i * hb - 1, 0), col))
    nxt = lambda i: jnp.minimum((i + 1) * hb, s // HALO - 1)
    return pl.pallas_call(
        body, grid=(nt,), name=name,
        in_specs=[pl.BlockSpec((ts, D), lambda i: (i, 0)),
                  pl.BlockSpec((HALO, D), lambda i: (nxt(i), 0)),
                  pl.BlockSpec((D, D), lambda i: (0, 0)),
                  pl.BlockSpec((ts, 3 * D), lambda i: (i, 0)), past(1), past(2),
                  pl.BlockSpec((HALO, D), lambda i: (nxt(i), 0)),
                  pl.BlockSpec((3, D), lambda i: (0, 0))],
        out_specs=[pl.BlockSpec((ts, 3 * D), lambda i: (i, 0)), pl.BlockSpec((3, D), lambda i: (0, 0))],
        out_shape=[jax.ShapeDtypeStruct((s, 3 * D), BF16), jax.ShapeDtypeStruct((3, D), F32)],
        compiler_params=_params("arbitrary"),
    )(dx, dx, w, zc, zc, zc, zc, cw)


G512_ROWS = 40


def _ab_bwd(dx, w, z, yb2, lga, lba, wsm, bs_col, cwb, lgb, lbb, name):
    s = z.shape[0]
    ts = _tile(s, 256)
    nt = s // ts
    hb = ts // HALO_B
    nch = ts // CHUNK

    def body(z_ref, zh_ref, dx_ref, dxf_ref, w_ref, yb2_ref, yb2f_ref, lga_ref, lba_ref, ws_ref, bs_ref,
             cw_ref, lgb_ref, lbb_ref, dz_ref, g512_ref, dws_ref, dbs_ref, dvn_ref, fwd_rolled, bwd_rolled, du_s):
        i = pl.program_id(0)
        last = i == nt - 1

        @pl.when(i == 0)
        def _():
            g512_ref[...] = jnp.zeros((G512_ROWS, DA), F32)
            dws_ref[...] = jnp.zeros((HEADS, CHUNK, CHUNK), F32)
            dbs_ref[...] = jnp.zeros((HEADS, CHUNK, 1), F32)

        def add_row(k, v):
            g512_ref[k:k + 1, :] += v

        z_t = z_ref[...].astype(F32)
        nt_dot = lambda a, b: lax.dot_general(a, b, NT_DIMS, preferred_element_type=F32).astype(BF16).astype(F32)
        dy_t = nt_dot(dx_ref[...], w_ref[...])
        dyf = nt_dot(dxf_ref[...], w_ref[DA:2 * DA, :])
        ua, va = z_t[:, 0:DA], z_t[:, DA:2 * DA]
        gu = _gelu(ua)
        gv = _gelu(va)
        lga_v = lga_ref[...]
        vn, xhat_a, rstd_a = _ln_fwd(gv, lga_v, lba_ref[...])
        vnb = vn.astype(BF16)
        causal = (lax.broadcasted_iota(jnp.int32, (CHUNK, CHUNK), 0)
                  >= lax.broadcasted_iota(jnp.int32, (CHUNK, CHUNK), 1)).astype(F32)
        for c in range(nch):
            for h in range(HEADS):
                rs = slice(c * CHUNK, (c + 1) * CHUNK)
                cs = slice(h * CHUNK, (h + 1) * CHUNK)
                vblk = vnb[rs, cs]
                mixed = jnp.dot(ws_ref[h], vblk, preferred_element_type=F32) + bs_ref[h]
                dyb_ = dy_t[rs, cs]
                dmix = dyb_ * gu[rs, cs]
                dmb = dmix.astype(BF16)
                dz_ref[rs, cs] = (dyb_ * mixed * _dgelu(ua[rs, cs])).astype(BF16)
                dvn_ref[rs, cs] = lax.dot_general(ws_ref[h], dmb, TN_DIMS, preferred_element_type=F32)
                dws_ref[h] += causal * lax.dot_general(dmb, vblk, NT_DIMS, preferred_element_type=F32)
                dbs_ref[h] += jnp.sum(dmix, axis=1, keepdims=True)
        dvn = dvn_ref[...]
        add_row(0, jnp.sum(dvn * xhat_a, axis=0, keepdims=True))
        add_row(1, jnp.sum(dvn, axis=0, keepdims=True))
        dgv = _ln_bwd(dvn, xhat_a, rstd_a, lga_v)
        dz_ref[:, DA:2 * DA] = (dgv * _dgelu(va)).astype(BF16)
        lgb_v = lgb_ref[...]
        dyb_e = jnp.concatenate(
            [dy_t[:, DA:2 * DA], jnp.where(last, 0.0, dyf)], axis=0)
        yb2_e = jnp.concatenate([yb2_ref[...], jnp.where(last, 0.0, yb2f_ref[...])], axis=0)
        n_e, xhat_b, rstd_b = _ln_fwd(yb2_e, lgb_v, lbb_ref[...])
        sgn = _sigmoid(n_e)
        dn = dyb_e * _dsilu(n_e, sgn)
        dy2 = _ln_bwd(dn, xhat_b, rstd_b, lgb_v)
        add_row(2, jnp.sum(dy2[:ts], axis=0, keepdims=True))
        add_row(3, jnp.sum(dn[:ts] * xhat_b[:ts], axis=0, keepdims=True))
        add_row(4, jnp.sum(dn[:ts], axis=0, keepdims=True))
        zh = jnp.where(i > 0, zh_ref[...], jnp.zeros_like(zh_ref[...])).astype(F32)
        xb_t, gb_t = z_t[:, 2 * DA:3 * DA], z_t[:, 3 * DA:4 * DA]
        sgb = _sigmoid(gb_t)
        _rolled_copies(fwd_rolled, jnp.concatenate(
            [zh[:, 0:DA] * _sigmoid(zh[:, DA:2 * DA]), xb_t * sgb], axis=0), False)
        _rolled_copies(bwd_rolled, dy2, True)
        for o in range(0, ts, CONV_ROWS):
            acc = jnp.zeros((CONV_ROWS, DA), F32)
            for sh in range(BCONV):
                q, r = divmod(sh, 8)
                acc = acc + cw_ref[BCONV - 1 - sh:BCONV - sh, :] * bwd_rolled[r, 8 * q + o:8 * q + o + CONV_ROWS, :]
            du_s[o:o + CONV_ROWS, :] = acc
        for sh in range(BCONV):
            q, r = divmod(sh, 8)
            acc = jnp.zeros((CONV_ROWS, DA), F32)
            for o in range(0, ts, CONV_ROWS):
                lo = HALO_B - 8 * q + o
                acc = acc + bwd_rolled[0, o:o + CONV_ROWS, :] * fwd_rolled[r, lo:lo + CONV_ROWS, :]
            add_row(8 + BCONV - 1 - sh, jnp.sum(acc, axis=0, keepdims=True))
        du = du_s[...]
        dz_ref[:, 2 * DA:3 * DA] = (du * sgb).astype(BF16)
        dz_ref[:, 3 * DA:4 * DA] = (du * xb_t * sgb * (1.0 - sgb)).astype(BF16)

    row = lambda i: (0, 0)
    nxt = lambda i: jnp.minimum((i + 1) * hb, s // HALO_B - 1)
    return pl.pallas_call(
        body, grid=(nt,), name=name,
        in_specs=[pl.BlockSpec((ts, 4 * DA), lambda i: (i, 0)),
                  pl.BlockSpec((HALO_B, 2 * DA), lambda i: (jnp.maximum(i * hb - 1, 0), 1)),
                  pl.BlockSpec((ts, D), lambda i: (i, 0)),
                  pl.BlockSpec((HALO_B, D), lambda i: (nxt(i), 0)),
                  pl.BlockSpec((D, D), lambda i: (0, 0)),
                  pl.BlockSpec((ts, DA), lambda i: (i, 0)),
                  pl.BlockSpec((HALO_B, DA), lambda i: (nxt(i), 0)),
                  pl.BlockSpec((1, DA), row), pl.BlockSpec((1, DA), row),
                  pl.BlockSpec((HEADS, CHUNK, CHUNK), lambda i: (0, 0, 0)),
                  pl.BlockSpec((HEADS, CHUNK, 1), lambda i: (0, 0, 0)),
                  pl.BlockSpec((BCONV, DA), row), pl.BlockSpec((1, DA), row), pl.BlockSpec((1, DA), row)],
        out_specs=[pl.BlockSpec((ts, 4 * DA), lambda i: (i, 0)),
                   pl.BlockSpec((G512_ROWS, DA), row),
                   pl.BlockSpec((HEADS, CHUNK, CHUNK), lambda i: (0, 0, 0)),
                   pl.BlockSpec((HEADS, CHUNK, 1), lambda i: (0, 0, 0))],
        out_shape=[jax.ShapeDtypeStruct((s, 4 * DA), BF16), jax.ShapeDtypeStruct((G512_ROWS, DA), F32),
                   jax.ShapeDtypeStruct((HEADS, CHUNK, CHUNK), F32),
                   jax.ShapeDtypeStruct((HEADS, CHUNK, 1), F32)],
        scratch_shapes=[pltpu.VMEM((ts, DA), F32), pltpu.VMEM((8, ts + HALO_B, DA), F32),
                        pltpu.VMEM((8, ts + HALO_B, DA), F32), pltpu.VMEM((ts, DA), F32)],
        compiler_params=_params("arbitrary"),
    )(z, z, dx, dx, w, yb2, yb2, lga, lba, wsm, bs_col, cwb, lgb, lbb)


def _dw_cols(a, dy, nb, bn, name):
    s = a.shape[0]
    tm = _tile(s, 2048)
    nt = s // tm
    cpb = 4

    def body(a_ref, dy_ref, o_ref, acc):
        t = pl.program_id(1)
        p = lax.dot_general(a_ref[...], dy_ref[...], TN_DIMS, preferred_element_type=F32)

        @pl.when(t == 0)
        def _():
            for q in range(cpb):
                acc[q] = p[:, q * bn:(q + 1) * bn]

        @pl.when(t > 0)
        def _():
            for q in range(cpb):
                acc[q] += p[:, q * bn:(q + 1) * bn]

        @pl.when(t == nt - 1)
        def _():
            o_ref[...] = acc[...].astype(BF16)

    return pl.pallas_call(
        body, grid=(nb // cpb, nt), name=name,
        in_specs=[pl.BlockSpec((tm, D), lambda j, t: (t, 0)), pl.BlockSpec((tm, cpb * bn), lambda j, t: (t, j))],
        out_specs=pl.BlockSpec((cpb, D, bn), lambda j, t: (j, 0, 0)),
        out_shape=jax.ShapeDtypeStruct((nb, D, bn), BF16),
        scratch_shapes=[pltpu.VMEM((cpb, D, bn), F32)],
        compiler_params=_params("arbitrary", "arbitrary"),
    )(a, dy)


def _dw_rows(a, dy, name):
    s = a.shape[0]
    tm = _tile(s, 4096)
    nt = s // tm
    rb = 512

    def body(a_ref, dy_ref, o_ref, acc):
        t = pl.program_id(1)
        p = lax.dot_general(a_ref[...], dy_ref[...], TN_DIMS, preferred_element_type=F32)

        @pl.when(t == 0)
        def _():
            acc[...] = p

        @pl.when(t > 0)
        def _():
            acc[...] += p

        @pl.when(t == nt - 1)
        def _():
            o_ref[...] = acc[...].astype(BF16)

    return pl.pallas_call(
        body, grid=(D // rb, nt), name=name,
        in_specs=[pl.BlockSpec((tm, rb), lambda j, t: (t, j)), pl.BlockSpec((tm, D), lambda j, t: (t, 0))],
        out_specs=pl.BlockSpec((rb, D), lambda j, t: (j, 0)),
        out_shape=jax.ShapeDtypeStruct((D, D), BF16),
        scratch_shapes=[pltpu.VMEM((rb, D), F32)],
        compiler_params=_params("arbitrary", "arbitrary"),
    )(a, dy)


def _dw_up(h, dup, name):
    s = h.shape[0]
    tm = _tile(s, 4096)
    nt = s // tm

    def body(h_ref, d_ref, o_ref, acc):
        t = pl.program_id(1)
        p = lax.dot_general(d_ref[...], h_ref[...], TN_DIMS, preferred_element_type=F32)

        @pl.when(t == 0)
        def _():
            acc[...] = p

        @pl.when(t > 0)
        def _():
            acc[...] += p

        @pl.when(t == nt - 1)
        def _():
            o_ref[...] = acc[...].astype(BF16)

    return pl.pallas_call(
        body, grid=(NDEV, nt), name=name,
        in_specs=[pl.BlockSpec((tm, D), lambda b, t: (t, 0)),
                  pl.BlockSpec((None, None, tm, FB), lambda b, t: (b % NG, b // NG, t, 0))],
        out_specs=pl.BlockSpec((None, FB, D), lambda b, t: (b, 0, 0)),
        out_shape=jax.ShapeDtypeStruct((NDEV, FB, D), BF16),
        scratch_shapes=[pltpu.VMEM((FB, D), F32)],
        compiler_params=_params("arbitrary", "arbitrary"),
    )(h, dup)


def _dw_dn(a, df, name):
    s = df.shape[0]
    tm = _tile(s, 4096)
    nt = s // tm

    def body(a_ref, d_ref, o_ref, acc):
        t = pl.program_id(1)
        p = lax.dot_general(a_ref[...], d_ref[...], TN_DIMS, preferred_element_type=F32)

        @pl.when(t == 0)
        def _():
            acc[...] = p

        @pl.when(t > 0)
        def _():
            acc[...] += p

        @pl.when(t == nt - 1)
        def _():
            o_ref[...] = acc[...].astype(BF16)

    return pl.pallas_call(
        body, grid=(NG, nt), name=name,
        in_specs=[pl.BlockSpec((None, tm, FB), lambda m, t: (m, t, 0)), pl.BlockSpec((tm, D), lambda m, t: (t, 0))],
        out_specs=pl.BlockSpec((FB, D), lambda m, t: (m, 0)),
        out_shape=jax.ShapeDtypeStruct((DFF, D), BF16),
        scratch_shapes=[pltpu.VMEM((FB, D), F32)],
        compiler_params=_params("arbitrary", "arbitrary"),
    )(a, df)


def _place():
    x, y, c = lax.axis_index("x"), lax.axis_index("y"), lax.axis_index("c")
    chips = [(1 - x, y), (x, 1 - y), (1 - x, 1 - y)]
    return x, y, c, chips


def _zone(shard, dev):
    return lax.dynamic_update_slice(lax.empty((NDEV,) + shard.shape, shard.dtype), shard[None],
                                    (dev,) + (0,) * shard.ndim)


HBM_SPEC = pl.BlockSpec(memory_space=pltpu.HBM)
SEM_SPEC = pl.BlockSpec(memory_space=pltpu.SEMAPHORE)
DATAFLOW = pltpu.SideEffectType.DATAFLOW_SIDE_EFFECTING


def _hbm(a):
    return pltpu.with_memory_space_constraint(a, pltpu.HBM)


def _hbm_like(arrs):
    return [pltpu.HBM(a.shape, a.dtype) for a in arrs]


def _ag_start(srcs, lands, after, name):
    n = len(srcs)
    ns = 8 * n

    def body(*refs):
        src, land = refs[:n], refs[n:2 * n]
        sems = refs[2 * n + 1:2 * n + 1 + ns]
        token = refs[-1]
        x, y, c, chips = _place()
        peers = [(x, y, 1 - c)] + [(*chip, c) for chip in chips]
        for t in range(n):
            for k, to in enumerate(peers):
                pltpu.make_async_remote_copy(
                    src_ref=src[t], dst_ref=land[t].at[4 * x + 2 * y + c],
                    send_sem=sems[2 * (4 * t + k)], recv_sem=sems[2 * (4 * t + k) + 1],
                    device_id=to, device_id_type=MESH).start()
        token[...] = jnp.zeros_like(token)

    res = pl.pallas_call(
        body, name=name,
        in_specs=[HBM_SPEC] * (2 * n) + [ANY],
        out_specs=[SEM_SPEC] * ns + [HBM_SPEC] * (2 * n) + [pl.BlockSpec(memory_space=pltpu.VMEM)],
        out_shape=[pltpu.SemaphoreType.DMA(())] * ns + _hbm_like(srcs) + _hbm_like(lands)
        + [jax.ShapeDtypeStruct((8, 128), F32)],
        input_output_aliases={i: ns + i for i in range(2 * n)},
        compiler_params=pltpu.CompilerParams(has_side_effects=DATAFLOW),
    )(*[_hbm(a) for a in srcs], *[_hbm(a) for a in lands], after)
    sems = [[(res[2 * (4 * t + k)], res[2 * (4 * t + k) + 1]) for k in range(4)] for t in range(n)]
    return sems, res[ns:ns + n], res[ns + n:ns + 2 * n], res[-1]


def _ag_forward(srcs, lands, sems1, after, name):
    n = len(srcs)
    flat1 = [s for t in range(n) for k in range(1, 4) for s in sems1[t][k]]
    n1 = len(flat1)

    def body(*refs):
        src, land = refs[:n], refs[n:2 * n]
        s1 = refs[2 * n:2 * n + n1]
        s2 = refs[2 * n + n1 + 1:2 * n + n1 + 1 + 6 * n]
        x, y, c, chips = _place()
        for j, (cx, cy) in enumerate(chips):
            for t in range(n):
                blk = land[t].at[4 * cx + 2 * cy + c]
                pltpu.make_async_remote_copy(
                    src_ref=src[t], dst_ref=blk, send_sem=s1[2 * (3 * t + j)], recv_sem=s1[2 * (3 * t + j) + 1],
                    device_id=(cx, cy, c), device_id_type=MESH).wait_recv()
                pltpu.make_async_remote_copy(
                    src_ref=blk, dst_ref=blk, send_sem=s2[2 * (3 * t + j)], recv_sem=s2[2 * (3 * t + j) + 1],
                    device_id=(x, y, 1 - c), device_id_type=MESH).start()

    res = pl.pallas_call(
        body, name=name,
        in_specs=[HBM_SPEC] * (2 * n) + [SEM_SPEC] * n1 + [ANY],
        out_specs=[SEM_SPEC] * (6 * n) + [HBM_SPEC] * n,
        out_shape=[pltpu.SemaphoreType.DMA(())] * (6 * n) + _hbm_like(lands),
        input_output_aliases={n + i: 6 * n + i for i in range(n)},
        compiler_params=pltpu.CompilerParams(has_side_effects=DATAFLOW),
    )(*srcs, *lands, *flat1, after)
    sems2 = [[(res[2 * (3 * t + j)], res[2 * (3 * t + j) + 1]) for j in range(3)] for t in range(n)]
    return sems2, res[6 * n:]


def _ag_finish(srcs, lands, sems1, sems2, after, name):
    n = len(srcs)
    flat1 = [s for t in range(n) for k in range(4) for s in sems1[t][k]]
    flat2 = [s for t in range(n) for j in range(3) for s in sems2[t][j]]
    n1, n2 = len(flat1), len(flat2)

    def body(*refs):
        src, land = refs[:n], refs[n:2 * n]
        s1 = refs[2 * n:2 * n + n1]
        s2 = refs[2 * n + n1:2 * n + n1 + n2]
        x, y, c, chips = _place()
        sib = (x, y, 1 - c)
        for t in range(n):
            own = land[t].at[4 * x + 2 * y + 1 - c]
            pltpu.make_async_remote_copy(
                src_ref=src[t], dst_ref=own, send_sem=s1[8 * t], recv_sem=s1[8 * t + 1],
                device_id=sib, device_id_type=MESH).wait_recv()
            for k in range(4):
                pltpu.make_async_remote_copy(
                    src_ref=src[t], dst_ref=own, send_sem=s1[2 * (4 * t + k)], recv_sem=s1[2 * (4 * t + k) + 1],
                    device_id=sib, device_id_type=MESH).wait_send()
            for j, (cx, cy) in enumerate(chips):
                blk = land[t].at[4 * cx + 2 * cy + 1 - c]
                cp = pltpu.make_async_remote_copy(
                    src_ref=blk, dst_ref=blk, send_sem=s2[2 * (3 * t + j)], recv_sem=s2[2 * (3 * t + j) + 1],
                    device_id=sib, device_id_type=MESH)
                cp.wait_send()
                cp.wait_recv()

    return pl.pallas_call(
        body, name=name,
        in_specs=[HBM_SPEC] * (2 * n) + [SEM_SPEC] * (n1 + n2) + [ANY],
        out_specs=[HBM_SPEC] * n,
        out_shape=_hbm_like(lands),
        input_output_aliases={n + i: i for i in range(n)},
        compiler_params=pltpu.CompilerParams(has_side_effects=DATAFLOW),
    )(*srcs, *lands, *flat1, *flat2, after)


def _pair_copies(srcs, dsts, sems):
    x, y, c, _ = _place()
    nt = len(srcs)
    return [pltpu.make_async_remote_copy(
        src_ref=srcs[t].at[2 * j + 1 - c], dst_ref=dsts[t].at[j],
        send_sem=sems[2 * (NCHIP * t + j)], recv_sem=sems[2 * (NCHIP * t + j) + 1],
        device_id=(x, y, 1 - c), device_id_type=MESH) for t in range(nt) for j in range(NCHIP)]


def _pair_start(grads, carry, name):
    nt = len(grads)
    ns = 2 * NCHIP * nt
    zones = [_hbm(lax.empty((NCHIP,) + a.shape[1:], a.dtype)) for a in grads]
    extra = [] if carry is None else [_hbm(carry)]
    ne = len(extra)

    def body(*refs):
        for cp in _pair_copies(refs[:nt], refs[nt:2 * nt], refs[2 * nt + ne:2 * nt + ne + ns]):
            cp.start()

    res = pl.pallas_call(
        body, name=name,
        in_specs=[HBM_SPEC] * (2 * nt + ne),
        out_specs=[SEM_SPEC] * ns + [HBM_SPEC] * (2 * nt + ne),
        out_shape=[pltpu.SemaphoreType.DMA(())] * ns + _hbm_like(grads) + _hbm_like(zones) + _hbm_like(extra),
        input_output_aliases={i: ns + i for i in range(2 * nt + ne)},
        compiler_params=pltpu.CompilerParams(has_side_effects=DATAFLOW),
    )(*[_hbm(a) for a in grads], *zones, *extra)
    handle = (list(res[:ns]), list(res[ns:ns + nt]), list(res[ns + nt:ns + 2 * nt]))
    return handle, (res[ns + 2 * nt] if ne else None)


def _pair_wait(handle, after, name):
    sems, srcs, zones = handle
    nt, ns = len(srcs), len(sems)

    def body(*refs):
        for cp in _pair_copies(refs[:nt], refs[nt:2 * nt], refs[2 * nt:2 * nt + ns]):
            cp.wait_send()
            cp.wait_recv()

    return pl.pallas_call(
        body, name=name,
        in_specs=[HBM_SPEC] * (2 * nt) + [SEM_SPEC] * ns + [ANY],
        out_specs=[HBM_SPEC] * nt,
        out_shape=_hbm_like(zones),
        input_output_aliases={nt + i: i for i in range(nt)},
        compiler_params=pltpu.CompilerParams(has_side_effects=DATAFLOW),
    )(*srcs, *zones, *sems, after)


def _rows_tile(r, row_bytes, cap_bytes):
    best = None
    for tr in range(16, r + 1, 16):
        if r % tr == 0 and tr * row_bytes <= cap_bytes:
            best = tr
    return best if best is not None else r


def _pair_sum(own, got, cidx, name):
    _, _, r, cdim = own.shape
    tr = _rows_tile(r, 2 * cdim, 2 * 1024 * 1024)

    def body(c_ref, a_ref, b_ref, o_ref):
        o_ref[...] = (a_ref[...].astype(F32) + b_ref[...].astype(F32)).astype(BF16)

    return pl.pallas_call(
        body, name=name,
        grid_spec=pltpu.PrefetchScalarGridSpec(
            num_scalar_prefetch=1, grid=(NCHIP, r // tr),
            in_specs=[pl.BlockSpec((None, None, tr, cdim), lambda j, i, c_ref: (j, c_ref[0], i, 0)),
                      pl.BlockSpec((None, tr, cdim), lambda j, i, c_ref: (j, i, 0))],
            out_specs=pl.BlockSpec((None, tr, cdim), lambda j, i, c_ref: (j, i, 0))),
        out_shape=jax.ShapeDtypeStruct((NCHIP, r, cdim), BF16),
        compiler_params=_params("arbitrary", "arbitrary"),
    )(cidx, own, got)


def _chip_copies(srcs, zones, slots, sems):
    x, y, c, chips = _place()
    out = []
    for t, (z, l) in enumerate(slots):
        for k, (cx, cy) in enumerate(chips):
            dst = zones[z].at[k] if l is None else zones[z].at[k, l]
            out.append(pltpu.make_async_remote_copy(
                src_ref=srcs[t].at[2 * cx + cy], dst_ref=dst,
                send_sem=sems[2 * (3 * t + k)], recv_sem=sems[2 * (3 * t + k) + 1],
                device_id=(cx, cy, c), device_id_type=MESH))
    return out


def _chip_start(sums, zones, slots, carry, name):
    nt, nz = len(sums), len(zones)
    ns = 6 * nt
    extra = [] if carry is None else [_hbm(carry)]
    ne = len(extra)

    def body(*refs):
        for cp in _chip_copies(refs[:nt], refs[nt:nt + nz], slots, refs[nt + nz + ne:nt + nz + ne + ns]):
            cp.start()

    res = pl.pallas_call(
        body, name=name,
        in_specs=[HBM_SPEC] * (nt + nz + ne),
        out_specs=[SEM_SPEC] * ns + [HBM_SPEC] * (nt + nz + ne),
        out_shape=[pltpu.SemaphoreType.DMA(())] * ns + _hbm_like(sums) + _hbm_like(zones) + _hbm_like(extra),
        input_output_aliases={i: ns + i for i in range(nt + nz + ne)},
        compiler_params=pltpu.CompilerParams(has_side_effects=DATAFLOW),
    )(*[_hbm(a) for a in sums], *zones, *extra)
    return (list(res[:ns]), list(res[ns:ns + nt]), list(res[ns + nt:ns + nt + nz]),
            (res[ns + nt + nz] if ne else None))


def _chip_wait(started, zones, zone_ids, after, name):
    started = [(sums, [(zone_ids.index(z), l) for z, l in slots], sems) for sums, slots, sems in started]
    nz = len(zones)
    flat_src = [a for sums, _, _ in started for a in sums]
    flat_sem = [s for _, _, sems in started for s in sems]
    n_src, n_sem = len(flat_src), len(flat_sem)

    def body(*refs):
        srcs, zs, sems = refs[:n_src], refs[n_src:n_src + nz], refs[n_src + nz:n_src + nz + n_sem]
        so, se = 0, 0
        for sums, slots, sem_list in started:
            for cp in _chip_copies(srcs[so:so + len(sums)], zs, slots, sems[se:se + len(sem_list)]):
                cp.wait_send()
                cp.wait_recv()
            so += len(sums)
            se += len(sem_list)

    return pl.pallas_call(
        body, name=name,
        in_specs=[HBM_SPEC] * (n_src + nz) + [SEM_SPEC] * n_sem + [ANY],
        out_specs=[HBM_SPEC] * nz,
        out_shape=_hbm_like(zones),
        input_output_aliases={n_src + i: i for i in range(nz)},
        compiler_params=pltpu.CompilerParams(has_side_effects=DATAFLOW),
    )(*flat_src, *zones, *flat_sem, after)


def _small_allreduce(parts, y_first, after, name):
    nt = len(parts)

    def body(*refs):
        srcs, outs, bufs = refs[:nt], refs[nt + 1:2 * nt + 1], refs[2 * nt + 1:3 * nt + 1]
        send_sems, recv_sems = refs[3 * nt + 1:]
        x, y, c, _ = _place()
        along = {"c": (x, y, 1 - c), "x": (1 - x, y, c), "y": (x, 1 - y, c)}
        for t in range(nt):
            outs[t][...] = srcs[t][...]
        for step in range(3):
            order = [("c", "y", "x") if t in y_first else ("c", "x", "y") for t in range(nt)]
            copies = [pltpu.make_async_remote_copy(
                src_ref=outs[t], dst_ref=bufs[t].at[step],
                send_sem=send_sems.at[step, t], recv_sem=recv_sems.at[step, t],
                device_id=along[order[t][step]], device_id_type=MESH) for t in range(nt)]
            for cp in copies:
                cp.start()
            for cp in copies:
                cp.wait()
            for t in range(nt):
                outs[t][...] = outs[t][...] + bufs[t][step]

    vm = pl.BlockSpec(memory_space=pltpu.VMEM)
    return pl.pallas_call(
        body, name=name,
        in_specs=[vm] * nt + [ANY], out_specs=[vm] * nt,
        out_shape=[jax.ShapeDtypeStruct(a.shape, F32) for a in parts],
        scratch_shapes=[pltpu.VMEM((3,) + a.shape, F32) for a in parts]
        + [pltpu.SemaphoreType.DMA((3, nt)), pltpu.SemaphoreType.DMA((3, nt))],
        compiler_params=pltpu.CompilerParams(has_side_effects=True, vmem_limit_bytes=VMEM_LIMIT),
    )(*parts, after)


def _adam_math(w, g, m, v):
    m2 = ADAM_B1 * m + (1.0 - ADAM_B1) * g
    v2 = ADAM_B2 * v + (1.0 - ADAM_B2) * (g * g)
    m_hat = m2 / (1.0 - ADAM_B1 ** ADAM_STEP)
    v_hat = v2 / (1.0 - ADAM_B2 ** ADAM_STEP)
    delta = -ADAM_LR * (m_hat / (jnp.sqrt(v_hat) + ADAM_EPS) + ADAM_WD * w)
    return delta, m2, v2


def _adam_big(w, m, v, parts, mine, chip, name):
    nl, r, cdim = w.shape
    tr = _rows_tile(r, 4 * cdim, 3 * 512 * 1024)

    def body(c_ref, w_ref, m_ref, v_ref, p_ref, *rest):
        mine_refs, (g_ref, d_ref, mo_ref, vo_ref) = rest[:nl], rest[nl:]
        own = mine_refs[0][...]
        for l in range(1, nl):
            own = jnp.where(pl.program_id(0) == l, mine_refs[l][...], own)
        g = ((p_ref[0].astype(F32) + p_ref[1].astype(F32)) + p_ref[2].astype(F32)) + own.astype(F32)
        delta, m2, v2 = _adam_math(w_ref[...], g, m_ref[...], v_ref[...])
        g_ref[...] = g
        d_ref[...] = delta
        mo_ref[...] = m2
        vo_ref[...] = v2

    spec = pl.BlockSpec((None, tr, cdim), lambda l, i, c_ref: (l, i, 0))
    mine_specs = [pl.BlockSpec((None, tr, cdim), lambda l, i, c_ref, ll=ll: (c_ref[0], jnp.where(l == ll, i, 0), 0))
                  for ll in range(nl)]
    return pl.pallas_call(
        body, name=name,
        grid_spec=pltpu.PrefetchScalarGridSpec(
            num_scalar_prefetch=1, grid=(nl, r // tr),
            in_specs=[spec, spec, spec, pl.BlockSpec((3, None, tr, cdim), lambda l, i, c_ref: (0, l, i, 0))]
            + mine_specs,
            out_specs=[spec] * 4),
        out_shape=[jax.ShapeDtypeStruct(w.shape, F32)] * 4,
        compiler_params=_params("arbitrary", "arbitrary"),
    )(chip, w, m, v, parts, *mine)


def _adam_small(ws, gs, ms, vs, name):
    n = len(ws)

    def body(*refs):
        w_r, g_r, m_r, v_r = refs[:n], refs[n:2 * n], refs[2 * n:3 * n], refs[3 * n:4 * n]
        d_o, m_o, v_o = refs[4 * n:5 * n], refs[5 * n:6 * n], refs[6 * n:7 * n]
        for t in range(n):
            delta, m2, v2 = _adam_math(w_r[t][...], g_r[t][...], m_r[t][...], v_r[t][...])
            d_o[t][...] = delta
            m_o[t][...] = m2
            v_o[t][...] = v2

    vm = pl.BlockSpec(memory_space=pltpu.VMEM)
    shapes = [jax.ShapeDtypeStruct(a.shape, F32) for a in ws]
    return pl.pallas_call(
        body, name=name, in_specs=[vm] * (4 * n), out_specs=[vm] * (3 * n), out_shape=shapes * 3,
        compiler_params=pltpu.CompilerParams(vmem_limit_bytes=VMEM_LIMIT),
    )(*ws, *gs, *ms, *vs)


def kernel(x, norm_mix, norm_ffn, norm_final, ab_w_in, a_ln_g, a_ln_b, a_w_s, a_b_s, b_conv_w, b_conv_b, b_ln_g, b_ln_b, ab_w_out, c_w_in, c_conv_w, c_w_out, f_w_up, f_conv_w, f_w_down, loss_target, m_norm_mix, m_norm_ffn, m_norm_final, m_ab_w_in, m_a_ln_g, m_a_ln_b, m_a_w_s, m_a_b_s, m_b_conv_w, m_b_conv_b, m_b_ln_g, m_b_ln_b, m_ab_w_out, m_c_w_in, m_c_conv_w, m_c_w_out, m_f_w_up, m_f_conv_w, m_f_w_down, v_norm_mix, v_norm_ffn, v_norm_final, v_ab_w_in, v_a_ln_g, v_a_ln_b, v_a_w_s, v_a_b_s, v_b_conv_w, v_b_conv_b, v_b_ln_g, v_b_ln_b, v_ab_w_out, v_c_w_in, v_c_conv_w, v_c_w_out, v_f_w_up, v_f_conv_w, v_f_w_down):
    s = x.shape[1]
    x0 = x.reshape(s, D)
    tgt = loss_target.reshape(s, D)
    xi, yi, ci = lax.axis_index("x"), lax.axis_index("y"), lax.axis_index("c")
    dev = 4 * xi + 2 * yi + ci
    cidx = ci.astype(jnp.int32).reshape(1)

    bf = lambda a: a.astype(BF16)
    slab_w = 6 * CHUNK
    pad = lambda a, rows: jnp.pad(a, ((0, rows - a.shape[0]), (0, slab_w - a.shape[1])))
    slab = jnp.concatenate([pad(b_conv_w[0], 32), pad(c_conv_w[0], 8), pad(f_conv_w.reshape(6, FB), 8)], axis=0)
    later = [bf(ab_w_in[0]), bf(ab_w_out[0]), slab, bf(f_w_up[0]), bf(f_w_down[0]), bf(c_w_in[0]), bf(c_w_out[0]),
             bf(f_w_up[1]), bf(f_w_down[1])]
    lands = [_zone(a, dev) for a in later]
    groups = [[0], [1, 2], [3, 4], [5, 6], [7, 8]]
    ag_sems, later, lands, ag_token = _ag_start(later, lands, x0, "ag_start")

    causal = jnp.tril(jnp.ones((CHUNK, CHUNK), F32))
    wsm = (a_w_s[0] * causal).astype(BF16)
    bs_col = a_b_s.reshape(HEADS, CHUNK, 1)
    nm = [norm_mix[0:1], norm_mix[1:2]]
    nf = [norm_ffn[0:1], norm_ffn[1:2]]
    nfin = norm_final.reshape(1, D)

    def pass_on(ts_, after_ici, tag):
        srcs = [later[t] for t in ts_]
        sems1 = [ag_sems[t] for t in ts_]
        sems2, zone = _ag_forward(srcs, [lands[t] for t in ts_], sems1, after_ici, "ag_forward_" + tag)
        return srcs, zone, sems1, sems2

    def arrive(g, after_ici, after_d2d, tag):
        srcs, zone, sems1, sems2 = pass_on(groups[g], after_ici, tag)
        return _ag_finish(srcs, zone, sems1, sems2, after_d2d, "ag_finish_" + tag)

    h0 = _rms_fwd(x0, nm[0], "rms_mix0", after=ag_token)
    (win0,) = arrive(0, h0, h0, "w_in")
    z = _mm_in(h0, win0, "mm_ab_in")
    wout0, slab_g = arrive(1, z, z, "first")
    wout0 = wout0.reshape(D, D)
    bcw = jnp.transpose(slab_g[:, 0:BCONV, 0:DA // NDEV], (1, 0, 2)).reshape(BCONV, DA)
    ccw = jnp.transpose(slab_g[:, 32:35, 0:D // NDEV], (1, 0, 2)).reshape(3, D)
    fcw_g = slab_g[:, 40:46, 0:FB].reshape(2, NG, 2, 3, FB)
    fcws = [fcw_g[:, :, 0], fcw_g[:, :, 1]]
    ycat, yb2 = _ab_fwd(z, a_ln_g, a_ln_b, wsm, bs_col, bcw, b_conv_b, b_ln_g, b_ln_b, "ab_fwd")
    x1, h1 = _mm_out(ycat, wout0, x0, nf[0], "mm_ab_out")
    wup0, wdn0 = arrive(2, x1, x1, "ffn0")
    up0, upc0, x2, h2 = _ffn_fwd(h1, x1, wup0.reshape(2, NG, D, FB), fcws[0], wdn0.reshape(DFF, D), nm[1],
                                 "ffn_fwd0")
    cin, cout = arrive(3, x2, x2, "c")
    cout = cout.reshape(D, D)
    zc = _mm_in(h2, cin, "mm_c_in")
    last_part = pass_on(groups[4], zc, "ffn1")
    rc, x3, h3 = _c_fwd(zc, ccw, cout, x2, nf[1], "c_fwd_out")
    wup1, wdn1 = _ag_finish(*last_part, x3, "ag_finish_ffn1")
    wups = [wup0.reshape(2, NG, D, FB), wup1.reshape(2, NG, D, FB)]
    wdns = [wdn0.reshape(DFF, D), wdn1.reshape(DFF, D)]
    up1, upc1, dx4, dx4b, dnfin, loss_part = _ffn_fwd(h3, x3, wups[1], fcws[1], wdns[1], None, "ffn_fwd1_loss",
                                                      final=(tgt, nfin))

    zshape = lambda *sh: _hbm(lax.empty((3,) + sh, BF16))
    zones = [zshape(D, 2 * D // NDEV), zshape(D // NDEV, D), zshape(D, 3 * D // NDEV), zshape(D // NDEV, D),
             zshape(2, FB, D), zshape(2, DFF // NDEV, D)]
    started = []

    def pair_sums(grads, handle, after, tag):
        del grads
        got = _pair_wait(handle, after, "rs_pair_wait_" + tag)
        return [_pair_sum(b.reshape((NCHIP, 2) + b.shape[1:]), g, cidx, "rs_pair_sum_%s%d" % (tag, t))
                for t, (b, g) in enumerate(zip(handle[1], got))]

    def chip_start(sums, slots, carry, tag):
        sems, sums, new_zones, carry = _chip_start(sums, zones, slots, carry, "rs_chip_start_" + tag)
        zones[:] = new_zones
        started.append((sums, slots, sems))
        return sums, carry

    rows8 = lambda g, r: g.reshape(NDEV, r, D)
    a1, dup1, dx3, dx3b, dnf1, dfcw1 = _ffn_bwd(dx4, up1, upc1, wups[1], fcws[1], wdns[1], x3, nf[1], "ffn_bwd1")
    g_f1 = [_dw_up(h3, dup1, "dw_up1"), rows8(_dw_dn(a1, dx4b, "dw_dn1"), DFF // NDEV)]
    hd_f1, dx3b = _pair_start(g_f1, dx3b, "rs_pair_start_f1")
    g_cout = rows8(_dw_rows(rc, dx3b, "dw_c_out"), D // NDEV)
    s_f1 = pair_sums(g_f1, hd_f1, g_cout, "f1")
    s_f1, dx3b = chip_start(s_f1, [(4, 1), (5, 1)], dx3b, "f1")
    dzc, dccw = _c_bwd(dx3b, cout, zc, ccw, "c_bwd")
    dx2, dx2b, dnm1 = _mm_nt_rms(dzc, cin, x2, nm[1], dx3, True, "mm_c_in_bwd")
    g_c = [_dw_cols(h2, dzc, NDEV, 3 * D // NDEV, "dw_c_in"), g_cout]
    hd_c, dx2 = _pair_start(g_c, dx2, "rs_pair_start_c")
    a0, dup0, dx1, dx1b, dnf0, dfcw0 = _ffn_bwd(dx2, up0, upc0, wups[0], fcws[0], wdns[0], x1, nf[0], "ffn_bwd0")
    s_c = pair_sums(g_c, hd_c, dx1b, "c")
    s_c, dx1b = chip_start(s_c, [(2, None), (3, None)], dx1b, "c")
    g_f0 = [_dw_up(h1, dup0, "dw_up0"), rows8(_dw_dn(a0, dx2b, "dw_dn0"), DFF // NDEV)]
    hd_f0, dx1b = _pair_start(g_f0, dx1b, "rs_pair_start_f0")
    g_wout0 = rows8(_dw_rows(ycat, dx1b, "dw_ab_out"), D // NDEV)
    s_f0 = pair_sums(g_f0, hd_f0, g_wout0, "f0")
    s_f0, dx1b = chip_start(s_f0, [(4, 0), (5, 0)], dx1b, "f0")
    dz, g512, dws, dbs = _ab_bwd(dx1b, wout0, z, yb2, a_ln_g, a_ln_b, wsm, bs_col, bcw, b_ln_g, b_ln_b, "ab_bwd")
    grad_x, dnm0 = _mm_nt_rms(dz, win0, x0, nm[0], dx1, False, "mm_ab_in_bwd")
    g_ab = [_dw_cols(h0, dz, NDEV, 2 * D // NDEV, "dw_ab_in"), g_wout0]
    hd_ab, _ = _pair_start(g_ab, None, "rs_pair_start_ab")

    g1024 = jnp.concatenate([dnm0, dnm1, dnf0, dnf1, dnfin, dccw], axis=0)
    gfc = jnp.concatenate([dfcw0, dfcw1], axis=0).reshape(2 * NG * 2 * 3, FB)
    g1024, g512, dws, dbs, gfc, loss_sum = _small_allreduce(
        [g1024, g512, dws.reshape(HEADS * CHUNK, CHUNK), dbs.reshape(HEADS, CHUNK), gfc, loss_part], (2,),
        hd_ab[1][0], "small_allreduce")
    loss = loss_sum[0, 0]
    s_ab = pair_sums(g_ab, hd_ab, g1024, "ab")
    s_ab, _ = chip_start(s_ab, [(0, None), (1, None)], None, "ab")
    p_cin, p_cout, p_wup, p_wdn = _chip_wait(started[:3], zones[2:], [2, 3, 4, 5], s_ab[0], "rs_chip_wait_early")

    chip = (2 * xi + yi).astype(jnp.int32).reshape(1)

    def big_update(w, m, v, parts, mine, name):
        shp = w.shape
        w3, m3, v3 = (a.reshape((-1,) + shp[-2:]) for a in (w, m, v))
        p4 = parts.reshape((3,) + w3.shape)
        return [o.reshape(shp) for o in _adam_big(w3, m3, v3, p4, mine, chip, name)]

    u_cin = big_update(c_w_in, m_c_w_in, v_c_w_in, p_cin, [s_c[0]], "adam_c_w_in")
    u_cout = big_update(c_w_out, m_c_w_out, v_c_w_out, p_cout, [s_c[1]], "adam_c_w_out")
    tr_ = lambda a: jnp.swapaxes(a, 1, 2)
    u_wup = [tr_(o) for o in big_update(tr_(f_w_up), tr_(m_f_w_up), tr_(v_f_w_up), p_wup,
                                        [s_f0[0], s_f1[0]], "adam_f_w_up")]
    u_wdn = big_update(f_w_down, m_f_w_down, v_f_w_down, p_wdn, [s_f0[1], s_f1[1]], "adam_f_w_down")
    p_win0, p_wout0 = _chip_wait(started[3:], zones[:2], [0, 1], u_wdn[0], "rs_chip_wait_late")
    u_win0 = big_update(ab_w_in, m_ab_w_in, v_ab_w_in, p_win0, [s_ab[0]], "adam_ab_w_in")
    u_wout0 = big_update(ab_w_out, m_ab_w_out, v_ab_w_out, p_wout0, [s_ab[1]], "adam_ab_w_out")

    g_norm_mix = g1024[0:2]
    g_norm_ffn = g1024[2:4]
    g_norm_final = g1024[4:5]
    g_ccw = lax.dynamic_slice(g1024[5:8], (0, dev * (D // NDEV)), (3, D // NDEV))
    g_bcw = lax.dynamic_slice(g512[8:8 + BCONV], (0, dev * (DA // NDEV)), (BCONV, DA // NDEV))
    gfc = gfc.reshape(2, NG, 2, 3, FB)
    g_fcw = lax.dynamic_slice(gfc, (0, dev % NG, dev // NG, 0, 0), (2, 1, 1, 3, FB)).reshape(2, 3, FB)
    small_w = [norm_mix, norm_ffn, nfin, a_ln_g, a_ln_b, a_w_s[0], a_b_s[0], b_conv_w[0], b_conv_b,
               b_ln_g, b_ln_b, c_conv_w[0], f_conv_w]
    small_g = [g_norm_mix, g_norm_ffn, g_norm_final, g512[0:1], g512[1:2],
               dws.reshape(HEADS, CHUNK, CHUNK), dbs, g_bcw, g512[2:3],
               g512[3:4], g512[4:5], g_ccw, g_fcw]
    small_m = [m_norm_mix, m_norm_ffn, m_norm_final.reshape(1, D), m_a_ln_g, m_a_ln_b, m_a_w_s[0], m_a_b_s[0],
               m_b_conv_w[0], m_b_conv_b, m_b_ln_g, m_b_ln_b, m_c_conv_w[0], m_f_conv_w]
    small_v = [v_norm_mix, v_norm_ffn, v_norm_final.reshape(1, D), v_a_ln_g, v_a_ln_b, v_a_w_s[0], v_a_b_s[0],
               v_b_conv_w[0], v_b_conv_b, v_b_ln_g, v_b_ln_b, v_c_conv_w[0], v_f_conv_w]
    upd = _adam_small(small_w, small_g, small_m, small_v, "adam_small")
    ns = len(small_w)
    orig = [norm_mix, norm_ffn, norm_final, a_ln_g, a_ln_b, a_w_s, a_b_s, b_conv_w, b_conv_b,
            b_ln_g, b_ln_b, c_conv_w, f_conv_w]
    sg_out = [g.reshape(o.shape) for g, o in zip(small_g, orig)]
    sd_out = [a.reshape(o.shape) for a, o in zip(upd[0:ns], orig)]
    sm_out = [a.reshape(o.shape) for a, o in zip(upd[ns:2 * ns], orig)]
    sv_out = [a.reshape(o.shape) for a, o in zip(upd[2 * ns:3 * ns], orig)]

    def assemble(small, k):
        return [small[0], small[1], small[2], u_win0[k], small[3], small[4], small[5], small[6], small[7],
                small[8], small[9], small[10], u_wout0[k], u_cin[k], small[11], u_cout[k], u_wup[k],
                small[12], u_wdn[k]]

    grads = assemble(sg_out, 0)
    deltas = assemble(sd_out, 1)
    new_m = assemble(sm_out, 2)
    new_v = assemble(sv_out, 3)
    return (loss, grad_x.reshape(1, s, D), *grads, *deltas, *new_m, *new_v)
```

```python
import math

import jax
import jax.numpy as jnp
from jax import lax
from jax.experimental import pallas as pl
from jax.experimental.pallas import tpu as pltpu

F32 = jnp.float32
BF16 = jnp.bfloat16

D = 1024
DA = 512
HEADS = 4
CHUNK = 128
DFF = 2816
NDEV = 8
NCHIP = 4
FB = DFF * 2 // NDEV
NG = DFF // FB
BCONV = 31
EPS = 1e-6
HALO = 16
HALO_B = 32
RC = 32
NPART = 2
VMEM_LIMIT = 52 * 1024 * 1024
INV_SQRT2 = 1.0 / math.sqrt(2.0)
INV_SQRT_2PI = 1.0 / math.sqrt(2.0 * math.pi)

ADAM_LR = 0.001
ADAM_B1 = 0.9
ADAM_B2 = 0.999
ADAM_EPS = 1e-08
ADAM_WD = 0.01
ADAM_STEP = 10

MESH = pl.DeviceIdType.MESH
ANY = pl.BlockSpec(memory_space=pl.ANY)
NT_DIMS = (((1,), (1,)), ((), ()))
TN_DIMS = (((0,), (0,)), ((), ()))


def _params(*sem):
    return pltpu.CompilerParams(dimension_semantics=sem, vmem_limit_bytes=VMEM_LIMIT)


def _tile(s, want):
    return min(want, s)


def _sigmoid(x):
    return jax.nn.sigmoid(x)


def _dsilu(x, sg):
    return sg * (1.0 + x * (1.0 - sg))


def _gelu(x):
    return 0.5 * x * (1.0 + lax.erf(x * INV_SQRT2))


def _dgelu(x):
    return 0.5 * (1.0 + lax.erf(x * INV_SQRT2)) + x * jnp.exp(-0.5 * x * x) * INV_SQRT_2PI


def _ln_fwd(x, g, b):
    mu = jnp.mean(x, axis=-1, keepdims=True)
    xc = x - mu
    var = jnp.mean(xc * xc, axis=-1, keepdims=True)
    rstd = lax.rsqrt(var + EPS)
    xhat = xc * rstd
    return xhat * g + b, xhat, rstd


def _ln_bwd(dy, xhat, rstd, g):
    dxh = dy * g
    m1 = jnp.mean(dxh, axis=-1, keepdims=True)
    m2 = jnp.mean(dxh * xhat, axis=-1, keepdims=True)
    return rstd * (dxh - m1 - xhat * m2)


def _rms_bwd_math(dh, x, g):
    r = lax.rsqrt(jnp.mean(x * x, axis=-1, keepdims=True) + EPS)
    xhat = x * r
    dg = jnp.sum(dh * xhat, axis=0, keepdims=True)
    u = dh * g
    dx = r * (u - xhat * jnp.mean(u * xhat, axis=-1, keepdims=True))
    return dx, dg


def _conv3(xe, cw, halo):
    x0 = xe[halo:]
    x1 = pltpu.roll(xe, 1, 0)[halo:]
    x2 = pltpu.roll(xe, 2, 0)[halo:]
    return cw[2] * x0 + cw[1] * x1 + cw[0] * x2, (x0, x1, x2)


def _conv3_bwd_in(dce, cw, ts):
    n = dce.shape[0]
    d1 = pltpu.roll(dce, n - 1, 0)[:ts]
    d2 = pltpu.roll(dce, n - 2, 0)[:ts]
    return cw[2] * dce[:ts] + cw[1] * d1 + cw[0] * d2


def _conv3_bwd_w(dc, taps):
    x0, x1, x2 = taps
    return [jnp.sum(dc * x2, axis=0, keepdims=True), jnp.sum(dc * x1, axis=0, keepdims=True),
            jnp.sum(dc * x0, axis=0, keepdims=True)]


def _rms_fwd(x, g, name, after=None):
    s = x.shape[0]
    ts = _tile(s, 512)

    def body(x_ref, g_ref, *rest):
        h_ref = rest[-1]
        xv = x_ref[...]
        r = lax.rsqrt(jnp.mean(xv * xv, axis=-1, keepdims=True) + EPS)
        h_ref[...] = (xv * r * g_ref[...]).astype(BF16)

    extra = [] if after is None else [after]
    return pl.pallas_call(
        body, grid=(s // ts,), name=name,
        in_specs=[pl.BlockSpec((ts, D), lambda i: (i, 0)), pl.BlockSpec((1, D), lambda i: (0, 0))]
        + [ANY] * len(extra),
        out_specs=pl.BlockSpec((ts, D), lambda i: (i, 0)),
        out_shape=jax.ShapeDtypeStruct((s, D), BF16),
        compiler_params=_params("parallel"),
    )(x, g, *extra)


MXU_COLS = 256


def _pair(bn):
    return 1 if bn % MXU_COLS == 0 else 2


def _cols(w_ref, b, pair):
    return w_ref[b] if pair == 1 else jnp.concatenate([w_ref[b + q] for q in range(pair)], axis=1)


def _mm_in(h, wblk, name):
    s = h.shape[0]
    nb, _, bn = wblk.shape
    pair = _pair(bn)
    ts = _tile(s, 1024)

    def body(h_ref, w_ref, o_ref):
        hv = h_ref[...]
        for b in range(0, nb, pair):
            o_ref[:, b * bn:(b + pair) * bn] = jnp.dot(hv, _cols(w_ref, b, pair),
                                                       preferred_element_type=F32).astype(BF16)

    return pl.pallas_call(
        body, grid=(s // ts,), name=name,
        in_specs=[pl.BlockSpec((ts, D), lambda i: (i, 0)), pl.BlockSpec((nb, D, bn), lambda i: (0, 0, 0))],
        out_specs=pl.BlockSpec((ts, nb * bn), lambda i: (i, 0)),
        out_shape=jax.ShapeDtypeStruct((s, nb * bn), BF16),
        compiler_params=_params("parallel"),
    )(h, wblk)


def _rms_math(xv, g):
    r = lax.rsqrt(jnp.mean(xv * xv, axis=-1, keepdims=True) + EPS)
    return (xv * r * g).astype(BF16)


def _mm_out(y, w, xres, gnext, name):
    s = y.shape[0]
    ts = _tile(s, 1024)

    def body(y_ref, w_ref, x_ref, g_ref, o_ref, h_ref):
        xn = x_ref[...] + jnp.dot(y_ref[...], w_ref[...], preferred_element_type=F32)
        o_ref[...] = xn
        h_ref[...] = _rms_math(xn, g_ref[...])

    return pl.pallas_call(
        body, grid=(s // ts,), name=name,
        in_specs=[pl.BlockSpec((ts, D), lambda i: (i, 0)), pl.BlockSpec((D, D), lambda i: (0, 0)),
                  pl.BlockSpec((ts, D), lambda i: (i, 0)), pl.BlockSpec((1, D), lambda i: (0, 0))],
        out_specs=[pl.BlockSpec((ts, D), lambda i: (i, 0)), pl.BlockSpec((ts, D), lambda i: (i, 0))],
        out_shape=[jax.ShapeDtypeStruct((s, D), F32), jax.ShapeDtypeStruct((s, D), BF16)],
        compiler_params=_params("parallel"),
    )(y, w, xres, gnext)


CONV_ROWS = 32


def _rolled_copies(dst_ref, xe, back):
    n = xe.shape[0]
    dst_ref[0] = xe
    for r in range(1, 8):
        dst_ref[r] = pltpu.roll(xe, n - r if back else r, 0)


def _conv31(rolled_ref, cw_ref, ts, out_ref, bias):
    for o in range(0, ts, CONV_ROWS):
        acc = jnp.zeros((CONV_ROWS, DA), F32) + bias
        for sh in range(BCONV):
            q, r = divmod(sh, 8)
            lo = HALO_B - 8 * q + o
            acc = acc + cw_ref[BCONV - 1 - sh:BCONV - sh, :] * rolled_ref[r, lo:lo + CONV_ROWS, :]
        out_ref[o:o + CONV_ROWS, :] = acc


def _ab_fwd(z, lga, lba, wsm, bs_col, cwb, cbb, lgb, lbb, name):
    s = z.shape[0]
    ts = _tile(s, 256)
    hb = ts // HALO_B

    def body(z_ref, zh_ref, lga_ref, lba_ref, ws_ref, bs_ref, cw_ref, cb_ref, lgb_ref, lbb_ref,
             y_ref, yb2_ref, rolled):
        i = pl.program_id(0)
        z_t = z_ref[...].astype(F32)
        gu = _gelu(z_t[:, 0:DA])
        gv = _gelu(z_t[:, DA:2 * DA])
        vn, _, _ = _ln_fwd(gv, lga_ref[...], lba_ref[...])
        vnb = vn.astype(BF16)
        for c in range(ts // CHUNK):
            for h in range(HEADS):
                rs = slice(c * CHUNK, (c + 1) * CHUNK)
                cs = slice(h * CHUNK, (h + 1) * CHUNK)
                mixed = jnp.dot(ws_ref[h], vnb[rs, cs], preferred_element_type=F32) + bs_ref[h]
                y_ref[rs, cs] = (gu[rs, cs] * mixed).astype(BF16)
        zh = jnp.where(i > 0, zh_ref[...], jnp.zeros_like(zh_ref[...])).astype(F32)
        xb = jnp.concatenate([zh[:, 0:DA], z_t[:, 2 * DA:3 * DA]], axis=0)
        gb = jnp.concatenate([zh[:, DA:2 * DA], z_t[:, 3 * DA:4 * DA]], axis=0)
        _rolled_copies(rolled, xb * _sigmoid(gb), False)
        _conv31(rolled, cw_ref, ts, yb2_ref, cb_ref[...])
        nb_, _, _ = _ln_fwd(yb2_ref[...], lgb_ref[...], lbb_ref[...])
        y_ref[:, DA:2 * DA] = (nb_ * _sigmoid(nb_)).astype(BF16)

    row = lambda i: (0, 0)
    return pl.pallas_call(
        body, grid=(s // ts,), name=name,
        in_specs=[pl.BlockSpec((ts, 4 * DA), lambda i: (i, 0)),
                  pl.BlockSpec((HALO_B, 2 * DA), lambda i: (jnp.maximum(i * hb - 1, 0), 1)),
                  pl.BlockSpec((1, DA), row), pl.BlockSpec((1, DA), row),
                  pl.BlockSpec((HEADS, CHUNK, CHUNK), lambda i: (0, 0, 0)),
                  pl.BlockSpec((HEADS, CHUNK, 1), lambda i: (0, 0, 0)),
                  pl.BlockSpec((BCONV, DA), row), pl.BlockSpec((1, DA), row),
                  pl.BlockSpec((1, DA), row), pl.BlockSpec((1, DA), row)],
        out_specs=[pl.BlockSpec((ts, 2 * DA), lambda i: (i, 0)), pl.BlockSpec((ts, DA), lambda i: (i, 0))],
        out_shape=[jax.ShapeDtypeStruct((s, 2 * DA), BF16), jax.ShapeDtypeStruct((s, DA), F32)],
        scratch_shapes=[pltpu.VMEM((8, ts + HALO_B, DA), F32)],
        compiler_params=_params("parallel"),
    )(z, z, lga, lba, wsm, bs_col, cwb, cbb, lgb, lbb)


def _c_fwd(zc, cw, w, xres, gnext, name):
    s = zc.shape[0]
    ts = _tile(s, 512)
    hb = ts // HALO

    def body(z_ref, ch_ref, xh_ref, cw_ref, w_ref, x_ref, g_ref, r_ref, o_ref, h_ref):
        i = pl.program_id(0)
        z_t = z_ref[...].astype(F32)
        ph = jnp.where(i > 0, ch_ref[...].astype(F32) * xh_ref[...].astype(F32), 0.0)
        pe = jnp.concatenate([ph, z_t[:, D:2 * D] * z_t[:, 2 * D:3 * D]], axis=0)
        q, _ = _conv3(pe, [cw_ref[k:k + 1, :] for k in range(3)], HALO)
        r = (z_t[:, 0:D] * q).astype(BF16)
        r_ref[...] = r
        xn = x_ref[...] + jnp.dot(r, w_ref[...], preferred_element_type=F32)
        o_ref[...] = xn
        h_ref[...] = _rms_math(xn, g_ref[...])

    halo = lambda col: pl.BlockSpec((HALO, D), lambda i: (jnp.maximum(i * hb - 1, 0), col))
    tile = pl.BlockSpec((ts, D), lambda i: (i, 0))
    return pl.pallas_call(
        body, grid=(s // ts,), name=name,
        in_specs=[pl.BlockSpec((ts, 3 * D), lambda i: (i, 0)), halo(1), halo(2),
                  pl.BlockSpec((3, D), lambda i: (0, 0)), pl.BlockSpec((D, D), lambda i: (0, 0)), tile,
                  pl.BlockSpec((1, D), lambda i: (0, 0))],
        out_specs=[tile, tile, tile],
        out_shape=[jax.ShapeDtypeStruct((s, D), BF16), jax.ShapeDtypeStruct((s, D), F32),
                   jax.ShapeDtypeStruct((s, D), BF16)],
        compiler_params=_params("parallel"),
    )(zc, zc, zc, cw, w, xres, gnext)


def _final_math(xv, tv, gv):
    r = lax.rsqrt(jnp.mean(xv * xv, axis=-1, keepdims=True) + EPS)
    xhat = xv * r
    e = xhat * gv - tv
    part = 0.5 * jnp.sum(jnp.mean(e * e, axis=-1, keepdims=True), axis=0, keepdims=True)
    dy = e * (1.0 / D)
    dgp = jnp.sum(dy * xhat, axis=0, keepdims=True)
    u = dy * gv
    dx = r * (u - xhat * jnp.mean(u * xhat, axis=-1, keepdims=True))
    return dx, dgp, jnp.broadcast_to(part, (1, 128))


def _ffn_fwd(h, xres, wup, fcw, wdn, gnext, name, final=None):
    s = h.shape[0]
    ts = _tile(s, 512)
    hb = ts // HALO

    def body(h_ref, hh_ref, w_ref, cw_ref, wd_ref, x_ref, *rest):
        if final is not None:
            t_ref, gf_ref, up_ref, upc_ref, dx_ref, dxb_ref, dg_ref, loss_ref, up_s, xo_ref = rest
        elif gnext is not None:
            gn_ref, up_ref, upc_ref, xo_ref, hn_ref, up_s = rest
        else:
            up_ref, upc_ref, xo_ref, up_s = rest
        i = pl.program_id(0)
        m = pl.program_id(1)
        @pl.when(m == 0)
        def _():
            xo_ref[...] = x_ref[...]

        halo = jnp.where(i > 0, hh_ref[...], jnp.zeros_like(hh_ref[...]))
        hx = jnp.concatenate([halo, h_ref[...]], axis=0)
        acts = []
        for gv in range(2):
            up_s[gv] = jnp.dot(hx, w_ref[gv], preferred_element_type=F32)
            x0 = up_s[gv, HALO:HALO + ts, :]
            up_ref[gv] = x0.astype(BF16)
            upc = (cw_ref[gv, 2:3, :] * x0 + cw_ref[gv, 1:2, :] * up_s[gv, HALO - 1:HALO - 1 + ts, :]
                   + cw_ref[gv, 0:1, :] * up_s[gv, HALO - 2:HALO - 2 + ts, :])
            upc_ref[gv] = upc.astype(BF16)
            acts.append(upc)
        a = acts[0] * _sigmoid(acts[0]) * acts[1]
        xo_ref[...] += jnp.dot(a.astype(BF16), wd_ref[...], preferred_element_type=F32)

        if final is not None:
            @pl.when(m == NG - 1)
            def _():
                dx, dgp, part = _final_math(xo_ref[...], t_ref[...], gf_ref[...])
                dx_ref[...] = dx
                dxb_ref[...] = dx.astype(BF16)

                @pl.when(i == 0)
                def _():
                    dg_ref[...] = dgp
                    loss_ref[...] = part

                @pl.when(i > 0)
                def _():
                    dg_ref[...] += dgp
                    loss_ref[...] += part

        elif gnext is not None:
            @pl.when(m == NG - 1)
            def _():
                hn_ref[...] = _rms_math(xo_ref[...], gn_ref[...])

    tile = pl.BlockSpec((ts, D), lambda i, m: (i, 0))
    row = lambda n: pl.BlockSpec((1, n), lambda i, m: (0, 0))
    scratch = [pltpu.VMEM((2, ts + HALO, FB), F32)]
    if final is not None:
        more_in, more_ops = [tile, row(D)], list(final)
        more_out = [tile, tile, row(D), row(128)]
        more_shape = [jax.ShapeDtypeStruct((s, D), F32), jax.ShapeDtypeStruct((s, D), BF16),
                      jax.ShapeDtypeStruct((1, D), F32), jax.ShapeDtypeStruct((1, 128), F32)]
        scratch.append(pltpu.VMEM((ts, D), F32))
    else:
        nxt = gnext is not None
        more_in, more_ops = ([row(D)], [gnext]) if nxt else ([], [])
        more_out = [tile] + ([tile] if nxt else [])
        more_shape = [jax.ShapeDtypeStruct((s, D), F32)] + ([jax.ShapeDtypeStruct((s, D), BF16)] if nxt else [])
    return pl.pallas_call(
        body, grid=(s // ts, NG), name=name,
        in_specs=[tile,
                  pl.BlockSpec((HALO, D), lambda i, m: (jnp.maximum(i * hb - 1, 0), 0)),
                  pl.BlockSpec((2, None, D, FB), lambda i, m: (0, m, 0, 0)),
                  pl.BlockSpec((2, None, 3, FB), lambda i, m: (0, m, 0, 0)),
                  pl.BlockSpec((FB, D), lambda i, m: (m, 0)),
                  tile] + more_in,
        out_specs=[pl.BlockSpec((None, 2, ts, FB), lambda i, m: (m, 0, i, 0)),
                   pl.BlockSpec((None, 2, ts, FB), lambda i, m: (m, 0, i, 0))] + more_out,
        out_shape=[jax.ShapeDtypeStruct((NG, 2, s, FB), BF16), jax.ShapeDtypeStruct((NG, 2, s, FB), BF16)] + more_shape,
        scratch_shapes=scratch,
        compiler_params=_params("arbitrary", "arbitrary"),
    )(h, h, wup, fcw, wdn, xres, *more_ops)


def _ffn_bwd(df, up, upc, wup, fcw, wdn, xin, g, name):
    s = df.shape[0]
    ts = _tile(s, 512)
    nt = s // ts

    def body(df_ref, up_ref, upc_ref, w_ref, cw_ref, wd_ref, x_ref, g_ref,
             a_ref, dup_ref, dx_ref, dxb_ref, dg_ref, dcw_ref, carry, acc, tacc, dcs_ref):
        i = pl.program_id(0)
        m = pl.program_id(1)
        first = i == 0
        @pl.when(first)
        def _():
            carry[m] = jnp.zeros((2, 8, FB), F32)
            dcw_ref[m] = jnp.zeros((2, 3, FB), F32)

        @pl.when(m == 0)
        def _():
            acc[...] = jnp.zeros((ts, D), F32)

        cws = [[cw_ref[gv, k:k + 1, :] for k in range(3)] for gv in range(2)]
        part = ts // NPART
        das = [lax.dot_general(df_ref[p * part:(p + 1) * part, :].astype(BF16), wd_ref[...], NT_DIMS,
                               preferred_element_type=F32) for p in range(NPART)]

        tacc[...] = jnp.zeros((2, 3, 8, FB), F32)
        dcs_ref[:, ts:ts + 8, :] = carry[m]
        for r in reversed(range(ts // RC)):
            rs = slice(r * RC, (r + 1) * RC)
            gate = upc_ref[0, rs, :].astype(F32)
            val = upc_ref[1, rs, :].astype(F32)
            sg = _sigmoid(gate)
            sl = gate * sg
            a_ref[rs, :] = (sl * val).astype(BF16)
            da_c = das[(r * RC) // part][(r * RC) % part:(r * RC) % part + RC]
            dcs = [da_c * val * _dsilu(gate, sg), da_c * sl]
            for gv in range(2):
                dc = dcs[gv]
                dcs_ref[gv, rs, :] = dc
                d1 = dcs_ref[gv, r * RC + 1:(r + 1) * RC + 1, :]
                d2 = dcs_ref[gv, r * RC + 2:(r + 1) * RC + 2, :]
                du = cws[gv][2] * dc + cws[gv][1] * d1 + cws[gv][0] * d2
                dup_ref[gv, rs, :] = du.astype(BF16)
                x0 = up_ref[gv, rs, :].astype(F32)
                for k, dk in enumerate((d2, d1, dc)):
                    p = x0 * dk
                    tacc[gv, k] += sum(p[j:j + 8] for j in range(0, RC, 8))
            if (r * RC) % part == 0:
                ps = slice(r * RC, r * RC + part)
                acc[ps, :] += (
                    lax.dot_general(dup_ref[0, ps, :], w_ref[0], NT_DIMS, preferred_element_type=F32)
                    + lax.dot_general(dup_ref[1, ps, :], w_ref[1], NT_DIMS, preferred_element_type=F32))
        for gv in range(2):
            carry[m, gv] = dcs_ref[gv, 0:8, :]
            for k in range(3):
                dcw_ref[m, gv, k:k + 1, :] += jnp.sum(tacc[gv, k], axis=0, keepdims=True)

        @pl.when(m == NG - 1)
        def _():
            dx, dgp = _rms_bwd_math(acc[...], x_ref[...], g_ref[...])
            dx = df_ref[...] + dx
            dx_ref[...] = dx
            dxb_ref[...] = dx.astype(BF16)

            @pl.when(first)
            def _():
                dg_ref[...] = dgp

            @pl.when(jnp.logical_not(first))
            def _():
                dg_ref[...] += dgp

    rev = lambda i: nt - 1 - i
    return pl.pallas_call(
        body, grid=(nt, NG), name=name,
        in_specs=[pl.BlockSpec((ts, D), lambda i, m: (rev(i), 0)),
                  pl.BlockSpec((None, 2, ts, FB), lambda i, m: (m, 0, rev(i), 0)),
                  pl.BlockSpec((None, 2, ts, FB), lambda i, m: (m, 0, rev(i), 0)),
                  pl.BlockSpec((2, None, D, FB), lambda i, m: (0, m, 0, 0)),
                  pl.BlockSpec((2, None, 3, FB), lambda i, m: (0, m, 0, 0)),
                  pl.BlockSpec((FB, D), lambda i, m: (m, 0)),
                  pl.BlockSpec((ts, D), lambda i, m: (rev(i), 0)),
                  pl.BlockSpec((1, D), lambda i, m: (0, 0))],
        out_specs=[pl.BlockSpec((None, ts, FB), lambda i, m: (m, rev(i), 0)),
                   pl.BlockSpec((None, 2, ts, FB), lambda i, m: (m, 0, rev(i), 0)),
                   pl.BlockSpec((ts, D), lambda i, m: (rev(i), 0)),
                   pl.BlockSpec((ts, D), lambda i, m: (rev(i), 0)),
                   pl.BlockSpec((1, D), lambda i, m: (0, 0)),
                   pl.BlockSpec((NG, 2, 3, FB), lambda i, m: (0, 0, 0, 0))],
        out_shape=[jax.ShapeDtypeStruct((NG, s, FB), BF16), jax.ShapeDtypeStruct((NG, 2, s, FB), BF16),
                   jax.ShapeDtypeStruct((s, D), F32), jax.ShapeDtypeStruct((s, D), BF16),
                   jax.ShapeDtypeStruct((1, D), F32),
                   jax.ShapeDtypeStruct((NG, 2, 3, FB), F32)],
        scratch_shapes=[pltpu.VMEM((NG, 2, 8, FB), F32), pltpu.VMEM((ts, D), F32),
                        pltpu.VMEM((2, 3, 8, FB), F32), pltpu.VMEM((2, ts + 8, FB), F32)],
        compiler_params=_params("arbitrary", "arbitrary"),
    )(df, up, upc, wup, fcw, wdn, xin, g)


def _mm_nt_rms(dy, wblk, x, g, dres, bf16_copy, name):
    s = dy.shape[0]
    nb, _, bn = wblk.shape
    pair = _pair(bn)
    ts = _tile(s, 512)

    def body(dy_ref, w_ref, x_ref, g_ref, dr_ref, dx_ref, *rest):
        dg_ref = rest[-1]
        i = pl.program_id(0)
        acc = jnp.zeros((ts, D), F32)
        for b in range(0, nb, pair):
            acc = acc + lax.dot_general(dy_ref[:, b * bn:(b + pair) * bn], _cols(w_ref, b, pair), NT_DIMS,
                                        preferred_element_type=F32)
        dx, dgp = _rms_bwd_math(acc, x_ref[...], g_ref[...])
        dx = dr_ref[...] + dx
        dx_ref[...] = dx
        if bf16_copy:
            rest[0][...] = dx.astype(BF16)

        @pl.when(i == 0)
        def _():
            dg_ref[...] = dgp

        @pl.when(i > 0)
        def _():
            dg_ref[...] += dgp

    tile = pl.BlockSpec((ts, D), lambda i: (i, 0))
    return pl.pallas_call(
        body, grid=(s // ts,), name=name,
        in_specs=[pl.BlockSpec((ts, nb * bn), lambda i: (i, 0)), pl.BlockSpec((nb, D, bn), lambda i: (0, 0, 0)),
                  tile, pl.BlockSpec((1, D), lambda i: (0, 0)), tile],
        out_specs=[tile] + ([tile] if bf16_copy else []) + [pl.BlockSpec((1, D), lambda i: (0, 0))],
        out_shape=[jax.ShapeDtypeStruct((s, D), F32)] + ([jax.ShapeDtypeStruct((s, D), BF16)] if bf16_copy else [])
        + [jax.ShapeDtypeStruct((1, D), F32)],
        compiler_params=_params("arbitrary"),
    )(dy, wblk, x, g, dres)


def _c_bwd(dx, w, r, zc, cw, name):
    s = dx.shape[0]
    ts = _tile(s, 512)
    nt = s // ts
    hb = ts // HALO

    def body(dx_ref, dxf_ref, w_ref, r_ref, z_ref, ch_ref, xh_ref, bf_ref, cw_ref, dz_ref, dcw_ref, dwo_ref, acc):
        i = pl.program_id(0)
        cwv = [cw_ref[k:k + 1, :] for k in range(3)]
        dre = lax.dot_general(jnp.concatenate([dx_ref[...], dxf_ref[...]], axis=0), w_ref[...], NT_DIMS,
                              preferred_element_type=F32).astype(BF16).astype(F32)
        z_t = z_ref[...].astype(F32)
        bg, cg, xv = z_t[:, 0:D], z_t[:, D:2 * D], z_t[:, 2 * D:3 * D]
        ph = jnp.where(i > 0, ch_ref[...].astype(F32) * xh_ref[...].astype(F32), 0.0)
        pe = jnp.concatenate([ph, cg * xv], axis=0)
        q, taps = _conv3(pe, cwv, HALO)
        drv = dre[0:ts]
        dq = drv * bg
        dqf = jnp.where(i < nt - 1, dre[ts:ts + HALO] * bf_ref[...].astype(F32), 0.0)
        dp = _conv3_bwd_in(jnp.concatenate([dq, dqf], axis=0), cwv, ts)
        dz_ref[:, 0:D] = (drv * q).astype(BF16)
        dz_ref[:, D:2 * D] = (dp * xv).astype(BF16)
        dz_ref[:, 2 * D:3 * D] = (dp * cg).astype(BF16)
        rows = _conv3_bwd_w(dq, taps)

        @pl.when(i == 0)
        def _():
            for k in range(3):
                dcw_ref[k:k + 1, :] = rows[k]

        @pl.when(i > 0)
        def _():
            for k in range(3):
                dcw_ref[k:k + 1, :] += rows[k]

        p = lax.dot_general(r_ref[...], dx_ref[...], TN_DIMS, preferred_element_type=F32)

        @pl.when(i == 0)
        def _():
            acc[...] = p

        @pl.when(i > 0)
        def _():
            acc[...] += p

        @pl.when(i == nt - 1)
        def _():
            dwo_ref[...] = acc[...].astype(BF16)

    past = lambda col: pl.BlockSpec((HALO, D), lambda i: (jnp.maximum(i * hb - 1, 0), col))
    nxt = lambda i: jnp.minimum((i + 1) * hb, s // HALO - 1)
    return pl.pallas_call(
        body, grid=(nt,), name=name,
        in_specs=[pl.BlockSpec((ts, D), lambda i: (i, 0)),
                  pl.BlockSpec((HALO, D), lambda i: (nxt(i), 0)),
                  pl.BlockSpec((D, D), lambda i: (0, 0)),
                  pl.BlockSpec((ts, D), lambda i: (i, 0)),
                  pl.BlockSpec((ts, 3 * D), lambda i: (i, 0)), past(1), past(2),
                  pl.BlockSpec((HALO, D), lambda i: (nxt(i), 0)),
                  pl.BlockSpec((3, D), lambda i: (0, 0))],
        out_specs=[pl.BlockSpec((ts, 3 * D), lambda i: (i, 0)), pl.BlockSpec((3, D), lambda i: (0, 0)),
                   pl.BlockSpec((D, D), lambda i: (0, 0))],
        out_shape=[jax.ShapeDtypeStruct((s, 3 * D), BF16), jax.ShapeDtypeStruct((3, D), F32),
                   jax.ShapeDtypeStruct((D, D), BF16)],
        scratch_shapes=[pltpu.VMEM((D, D), F32)],
        compiler_params=_params("arbitrary"),
    )(dx, dx, w, r, zc, zc, zc, zc, cw)


G512_ROWS = 40


def _ab_bwd(dx, w, z, yb2, lga, lba, wsm, bs_col, cwb, lgb, lbb, name):
    s = z.shape[0]
    ts = _tile(s, 256)
    nt = s // ts
    hb = ts // HALO_B
    nch = ts // CHUNK

    def body(z_ref, zh_ref, dx_ref, dxf_ref, w_ref, yb2_ref, yb2f_ref, lga_ref, lba_ref, ws_ref, bs_ref,
             cw_ref, lgb_ref, lbb_ref, dz_ref, g512_ref, dws_ref, dbs_ref, dvn_ref, fwd_rolled, bwd_rolled, du_s):
        i = pl.program_id(0)
        last = i == nt - 1

        @pl.when(i == 0)
        def _():
            g512_ref[...] = jnp.zeros((G512_ROWS, DA), F32)
            dws_ref[...] = jnp.zeros((HEADS, CHUNK, CHUNK), F32)
            dbs_ref[...] = jnp.zeros((HEADS, CHUNK, 1), F32)

        def add_row(k, v):
            g512_ref[k:k + 1, :] += v

        z_t = z_ref[...].astype(F32)
        nt_dot = lambda a, b: lax.dot_general(a, b, NT_DIMS, preferred_element_type=F32).astype(BF16).astype(F32)
        dy_t = nt_dot(dx_ref[...], w_ref[...])
        dyf = nt_dot(dxf_ref[...], w_ref[DA:2 * DA, :])
        ua, va = z_t[:, 0:DA], z_t[:, DA:2 * DA]
        gu = _gelu(ua)
        gv = _gelu(va)
        lga_v = lga_ref[...]
        vn, xhat_a, rstd_a = _ln_fwd(gv, lga_v, lba_ref[...])
        vnb = vn.astype(BF16)
        causal = (lax.broadcasted_iota(jnp.int32, (CHUNK, CHUNK), 0)
                  >= lax.broadcasted_iota(jnp.int32, (CHUNK, CHUNK), 1)).astype(F32)
        for c in range(nch):
            for h in range(HEADS):
                rs = slice(c * CHUNK, (c + 1) * CHUNK)
                cs = slice(h * CHUNK, (h + 1) * CHUNK)
                vblk = vnb[rs, cs]
                mixed = jnp.dot(ws_ref[h], vblk, preferred_element_type=F32) + bs_ref[h]
                dyb_ = dy_t[rs, cs]
                dmix = dyb_ * gu[rs, cs]
                dmb = dmix.astype(BF16)
                dz_ref[rs, cs] = (dyb_ * mixed * _dgelu(ua[rs, cs])).astype(BF16)
                dvn_ref[rs, cs] = lax.dot_general(ws_ref[h], dmb, TN_DIMS, preferred_element_type=F32)
                dws_ref[h] += causal * lax.dot_general(dmb, vblk, NT_DIMS, preferred_element_type=F32)
                dbs_ref[h] += jnp.sum(dmix, axis=1, keepdims=True)
        dvn = dvn_ref[...]
        add_row(0, jnp.sum(dvn * xhat_a, axis=0, keepdims=True))
        add_row(1, jnp.sum(dvn, axis=0, keepdims=True))
        dgv = _ln_bwd(dvn, xhat_a, rstd_a, lga_v)
        dz_ref[:, DA:2 * DA] = (dgv * _dgelu(va)).astype(BF16)
        lgb_v = lgb_ref[...]
        dyb_e = jnp.concatenate(
            [dy_t[:, DA:2 * DA], jnp.where(last, 0.0, dyf)], axis=0)
        yb2_e = jnp.concatenate([yb2_ref[...], jnp.where(last, 0.0, yb2f_ref[...])], axis=0)
        n_e, xhat_b, rstd_b = _ln_fwd(yb2_e, lgb_v, lbb_ref[...])
        sgn = _sigmoid(n_e)
        dn = dyb_e * _dsilu(n_e, sgn)
        dy2 = _ln_bwd(dn, xhat_b, rstd_b, lgb_v)
        add_row(2, jnp.sum(dy2[:ts], axis=0, keepdims=True))
        add_row(3, jnp.sum(dn[:ts] * xhat_b[:ts], axis=0, keepdims=True))
        add_row(4, jnp.sum(dn[:ts], axis=0, keepdims=True))
        zh = jnp.where(i > 0, zh_ref[...], jnp.zeros_like(zh_ref[...])).astype(F32)
        xb_t, gb_t = z_t[:, 2 * DA:3 * DA], z_t[:, 3 * DA:4 * DA]
        sgb = _sigmoid(gb_t)
        _rolled_copies(fwd_rolled, jnp.concatenate(
            [zh[:, 0:DA] * _sigmoid(zh[:, DA:2 * DA]), xb_t * sgb], axis=0), False)
        _rolled_copies(bwd_rolled, dy2, True)
        for o in range(0, ts, CONV_ROWS):
            acc = jnp.zeros((CONV_ROWS, DA), F32)
            for sh in range(BCONV):
                q, r = divmod(sh, 8)
                acc = acc + cw_ref[BCONV - 1 - sh:BCONV - sh, :] * bwd_rolled[r, 8 * q + o:8 * q + o + CONV_ROWS, :]
            du_s[o:o + CONV_ROWS, :] = acc
        for sh in range(BCONV):
            q, r = divmod(sh, 8)
            acc = jnp.zeros((CONV_ROWS, DA), F32)
            for o in range(0, ts, CONV_ROWS):
                lo = HALO_B - 8 * q + o
                acc = acc + bwd_rolled[0, o:o + CONV_ROWS, :] * fwd_rolled[r, lo:lo + CONV_ROWS, :]
            add_row(8 + BCONV - 1 - sh, jnp.sum(acc, axis=0, keepdims=True))
        du = du_s[...]
        dz_ref[:, 2 * DA:3 * DA] = (du * sgb).astype(BF16)
        dz_ref[:, 3 * DA:4 * DA] = (du * xb_t * sgb * (1.0 - sgb)).astype(BF16)

    row = lambda i: (0, 0)
    nxt = lambda i: jnp.minimum((i + 1) * hb, s // HALO_B - 1)
    return pl.pallas_call(
        body, grid=(nt,), name=name,
        in_specs=[pl.BlockSpec((ts, 4 * DA), lambda i: (i, 0)),
                  pl.BlockSpec((HALO_B, 2 * DA), lambda i: (jnp.maximum(i * hb - 1, 0), 1)),
                  pl.BlockSpec((ts, D), lambda i: (i, 0)),
                  pl.BlockSpec((HALO_B, D), lambda i: (nxt(i), 0)),
                  pl.BlockSpec((D, D), lambda i: (0, 0)),
                  pl.BlockSpec((ts, DA), lambda i: (i, 0)),
                  pl.BlockSpec((HALO_B, DA), lambda i: (nxt(i), 0)),
                  pl.BlockSpec((1, DA), row), pl.BlockSpec((1, DA), row),
                  pl.BlockSpec((HEADS, CHUNK, CHUNK), lambda i: (0, 0, 0)),
                  pl.BlockSpec((HEADS, CHUNK, 1), lambda i: (0, 0, 0)),
                  pl.BlockSpec((BCONV, DA), row), pl.BlockSpec((1, DA), row), pl.BlockSpec((1, DA), row)],
        out_specs=[pl.BlockSpec((ts, 4 * DA), lambda i: (i, 0)),
                   pl.BlockSpec((G512_ROWS, DA), row),
                   pl.BlockSpec((HEADS, CHUNK, CHUNK), lambda i: (0, 0, 0)),
                   pl.BlockSpec((HEADS, CHUNK, 1), lambda i: (0, 0, 0))],
        out_shape=[jax.ShapeDtypeStruct((s, 4 * DA), BF16), jax.ShapeDtypeStruct((G512_ROWS, DA), F32),
                   jax.ShapeDtypeStruct((HEADS, CHUNK, CHUNK), F32),
                   jax.ShapeDtypeStruct((HEADS, CHUNK, 1), F32)],
        scratch_shapes=[pltpu.VMEM((ts, DA), F32), pltpu.VMEM((8, ts + HALO_B, DA), F32),
                        pltpu.VMEM((8, ts + HALO_B, DA), F32), pltpu.VMEM((ts, DA), F32)],
        compiler_params=_params("arbitrary"),
    )(z, z, dx, dx, w, yb2, yb2, lga, lba, wsm, bs_col, cwb, lgb, lbb)


def _dw_cols(a, dy, nb, bn, name):
    s = a.shape[0]
    tm = _tile(s, 2048)
    nt = s // tm
    cpb = 4

    def body(a_ref, dy_ref, o_ref, acc):
        t = pl.program_id(1)
        p = lax.dot_general(a_ref[...], dy_ref[...], TN_DIMS, preferred_element_type=F32)

        @pl.when(t == 0)
        def _():
            for q in range(cpb):
                acc[q] = p[:, q * bn:(q + 1) * bn]

        @pl.when(t > 0)
        def _():
            for q in range(cpb):
                acc[q] += p[:, q * bn:(q + 1) * bn]

        @pl.when(t == nt - 1)
        def _():
            o_ref[...] = acc[...].astype(BF16)

    return pl.pallas_call(
        body, grid=(nb // cpb, nt), name=name,
        in_specs=[pl.BlockSpec((tm, D), lambda j, t: (t, 0)), pl.BlockSpec((tm, cpb * bn), lambda j, t: (t, j))],
        out_specs=pl.BlockSpec((cpb, D, bn), lambda j, t: (j, 0, 0)),
        out_shape=jax.ShapeDtypeStruct((nb, D, bn), BF16),
        scratch_shapes=[pltpu.VMEM((cpb, D, bn), F32)],
        compiler_params=_params("arbitrary", "arbitrary"),
    )(a, dy)


def _dw_rows(a, dy, name):
    s = a.shape[0]
    tm = _tile(s, 4096)
    nt = s // tm
    rb = 512

    def body(a_ref, dy_ref, o_ref, acc):
        t = pl.program_id(1)
        p = lax.dot_general(a_ref[...], dy_ref[...], TN_DIMS, preferred_element_type=F32)

        @pl.when(t == 0)
        def _():
            acc[...] = p

        @pl.when(t > 0)
        def _():
            acc[...] += p

        @pl.when(t == nt - 1)
        def _():
            o_ref[...] = acc[...].astype(BF16)

    return pl.pallas_call(
        body, grid=(D // rb, nt), name=name,
        in_specs=[pl.BlockSpec((tm, rb), lambda j, t: (t, j)), pl.BlockSpec((tm, D), lambda j, t: (t, 0))],
        out_specs=pl.BlockSpec((rb, D), lambda j, t: (j, 0)),
        out_shape=jax.ShapeDtypeStruct((D, D), BF16),
        scratch_shapes=[pltpu.VMEM((rb, D), F32)],
        compiler_params=_params("arbitrary", "arbitrary"),
    )(a, dy)


def _dw_up(h, dup, name):
    s = h.shape[0]
    tm = _tile(s, 4096)
    nt = s // tm

    def body(h_ref, d_ref, o_ref, acc):
        t = pl.program_id(1)
        p = lax.dot_general(d_ref[...], h_ref[...], TN_DIMS, preferred_element_type=F32)

        @pl.when(t == 0)
        def _():
            acc[...] = p

        @pl.when(t > 0)
        def _():
            acc[...] += p

        @pl.when(t == nt - 1)
        def _():
            o_ref[...] = acc[...].astype(BF16)

    return pl.pallas_call(
        body, grid=(NDEV, nt), name=name,
        in_specs=[pl.BlockSpec((tm, D), lambda b, t: (t, 0)),
                  pl.BlockSpec((None, None, tm, FB), lambda b, t: (b % NG, b // NG, t, 0))],
        out_specs=pl.BlockSpec((None, FB, D), lambda b, t: (b, 0, 0)),
        out_shape=jax.ShapeDtypeStruct((NDEV, FB, D), BF16),
        scratch_shapes=[pltpu.VMEM((FB, D), F32)],
        compiler_params=_params("arbitrary", "arbitrary"),
    )(h, dup)


def _dw_dn(a, df, name):
    s = df.shape[0]
    tm = _tile(s, 4096)
    nt = s // tm

    def body(a_ref, d_ref, o_ref, acc):
        t = pl.program_id(1)
        p = lax.dot_general(a_ref[...], d_ref[...], TN_DIMS, preferred_element_type=F32)

        @pl.when(t == 0)
        def _():
            acc[...] = p

        @pl.when(t > 0)
        def _():
            acc[...] += p

        @pl.when(t == nt - 1)
        def _():
            o_ref[...] = acc[...].astype(BF16)

    return pl.pallas_call(
        body, grid=(NG, nt), name=name,
        in_specs=[pl.BlockSpec((None, tm, FB), lambda m, t: (m, t, 0)), pl.BlockSpec((tm, D), lambda m, t: (t, 0))],
        out_specs=pl.BlockSpec((FB, D), lambda m, t: (m, 0)),
        out_shape=jax.ShapeDtypeStruct((DFF, D), BF16),
        scratch_shapes=[pltpu.VMEM((FB, D), F32)],
        compiler_params=_params("arbitrary", "arbitrary"),
    )(a, df)


def _place():
    x, y, c = lax.axis_index("x"), lax.axis_index("y"), lax.axis_index("c")
    chips = [(1 - x, y), (x, 1 - y), (1 - x, 1 - y)]
    return x, y, c, chips


def _zone(shard, dev):
    return lax.dynamic_update_slice(lax.empty((NDEV,) + shard.shape, shard.dtype), shard[None],
                                    (dev,) + (0,) * shard.ndim)


HBM_SPEC = pl.BlockSpec(memory_space=pltpu.HBM)
SEM_SPEC = pl.BlockSpec(memory_space=pltpu.SEMAPHORE)
DATAFLOW = pltpu.SideEffectType.DATAFLOW_SIDE_EFFECTING


def _hbm(a):
    return pltpu.with_memory_space_constraint(a, pltpu.HBM)


def _hbm_like(arrs):
    return [pltpu.HBM(a.shape, a.dtype) for a in arrs]


def _ag_start(srcs, lands, after, name):
    n = len(srcs)
    ns = 8 * n

    def body(*refs):
        src, land = refs[:n], refs[n:2 * n]
        sems = refs[2 * n + 1:2 * n + 1 + ns]
        token = refs[-1]
        x, y, c, chips = _place()
        peers = [(x, y, 1 - c)] + [(*chip, c) for chip in chips]
        for t in range(n):
            for k, to in enumerate(peers):
                pltpu.make_async_remote_copy(
                    src_ref=src[t], dst_ref=land[t].at[4 * x + 2 * y + c],
                    send_sem=sems[2 * (4 * t + k)], recv_sem=sems[2 * (4 * t + k) + 1],
                    device_id=to, device_id_type=MESH).start()
        token[...] = jnp.zeros_like(token)

    res = pl.pallas_call(
        body, name=name,
        in_specs=[HBM_SPEC] * (2 * n) + [ANY],
        out_specs=[SEM_SPEC] * ns + [HBM_SPEC] * (2 * n) + [pl.BlockSpec(memory_space=pltpu.VMEM)],
        out_shape=[pltpu.SemaphoreType.DMA(())] * ns + _hbm_like(srcs) + _hbm_like(lands)
        + [jax.ShapeDtypeStruct((8, 128), F32)],
        input_output_aliases={i: ns + i for i in range(2 * n)},
        compiler_params=pltpu.CompilerParams(has_side_effects=DATAFLOW),
    )(*[_hbm(a) for a in srcs], *[_hbm(a) for a in lands], after)
    sems = [[(res[2 * (4 * t + k)], res[2 * (4 * t + k) + 1]) for k in range(4)] for t in range(n)]
    return sems, res[ns:ns + n], res[ns + n:ns + 2 * n], res[-1]


def _ag_forward(srcs, lands, sems1, after, name):
    n = len(srcs)
    flat1 = [s for t in range(n) for k in range(1, 4) for s in sems1[t][k]]
    n1 = len(flat1)

    def body(*refs):
        src, land = refs[:n], refs[n:2 * n]
        s1 = refs[2 * n:2 * n + n1]
        s2 = refs[2 * n + n1 + 1:2 * n + n1 + 1 + 6 * n]
        x, y, c, chips = _place()
        for j, (cx, cy) in enumerate(chips):
            for t in range(n):
                blk = land[t].at[4 * cx + 2 * cy + c]
                pltpu.make_async_remote_copy(
                    src_ref=src[t], dst_ref=blk, send_sem=s1[2 * (3 * t + j)], recv_sem=s1[2 * (3 * t + j) + 1],
                    device_id=(cx, cy, c), device_id_type=MESH).wait_recv()
                pltpu.make_async_remote_copy(
                    src_ref=blk, dst_ref=blk, send_sem=s2[2 * (3 * t + j)], recv_sem=s2[2 * (3 * t + j) + 1],
                    device_id=(x, y, 1 - c), device_id_type=MESH).start()

    res = pl.pallas_call(
        body, name=name,
        in_specs=[HBM_SPEC] * (2 * n) + [SEM_SPEC] * n1 + [ANY],
        out_specs=[SEM_SPEC] * (6 * n) + [HBM_SPEC] * n,
        out_shape=[pltpu.SemaphoreType.DMA(())] * (6 * n) + _hbm_like(lands),
        input_output_aliases={n + i: 6 * n + i for i in range(n)},
        compiler_params=pltpu.CompilerParams(has_side_effects=DATAFLOW),
    )(*srcs, *lands, *flat1, after)
    sems2 = [[(res[2 * (3 * t + j)], res[2 * (3 * t + j) + 1]) for j in range(3)] for t in range(n)]
    return sems2, res[6 * n:]


def _ag_finish(srcs, lands, sems1, sems2, after, name):
    n = len(srcs)
    flat1 = [s for t in range(n) for k in range(4) for s in sems1[t][k]]
    flat2 = [s for t in range(n) for j in range(3) for s in sems2[t][j]]
    n1, n2 = len(flat1), len(flat2)

    def body(*refs):
        src, land = refs[:n], refs[n:2 * n]
        s1 = refs[2 * n:2 * n + n1]
        s2 = refs[2 * n + n1:2 * n + n1 + n2]
        x, y, c, chips = _place()
        sib = (x, y, 1 - c)
        for t in range(n):
            own = land[t].at[4 * x + 2 * y + 1 - c]
            pltpu.make_async_remote_copy(
                src_ref=src[t], dst_ref=own, send_sem=s1[8 * t], recv_sem=s1[8 * t + 1],
                device_id=sib, device_id_type=MESH).wait_recv()
            for k in range(4):
                pltpu.make_async_remote_copy(
                    src_ref=src[t], dst_ref=own, send_sem=s1[2 * (4 * t + k)], recv_sem=s1[2 * (4 * t + k) + 1],
                    device_id=sib, device_id_type=MESH).wait_send()
            for j, (cx, cy) in enumerate(chips):
                blk = land[t].at[4 * cx + 2 * cy + 1 - c]
                cp = pltpu.make_async_remote_copy(
                    src_ref=blk, dst_ref=blk, send_sem=s2[2 * (3 * t + j)], recv_sem=s2[2 * (3 * t + j) + 1],
                    device_id=sib, device_id_type=MESH)
                cp.wait_send()
                cp.wait_recv()

    return pl.pallas_call(
        body, name=name,
        in_specs=[HBM_SPEC] * (2 * n) + [SEM_SPEC] * (n1 + n2) + [ANY],
        out_specs=[HBM_SPEC] * n,
        out_shape=_hbm_like(lands),
        input_output_aliases={n + i: i for i in range(n)},
        compiler_params=pltpu.CompilerParams(has_side_effects=DATAFLOW),
    )(*srcs, *lands, *flat1, *flat2, after)


def _pair_copies(srcs, dsts, sems):
    x, y, c, _ = _place()
    nt = len(srcs)
    return [pltpu.make_async_remote_copy(
        src_ref=srcs[t].at[2 * j + 1 - c], dst_ref=dsts[t].at[j],
        send_sem=sems[2 * (NCHIP * t + j)], recv_sem=sems[2 * (NCHIP * t + j) + 1],
        device_id=(x, y, 1 - c), device_id_type=MESH) for t in range(nt) for j in range(NCHIP)]


def _pair_start(grads, carry, name):
    nt = len(grads)
    ns = 2 * NCHIP * nt
    zones = [_hbm(lax.empty((NCHIP,) + a.shape[1:], a.dtype)) for a in grads]
    extra = [] if carry is None else [_hbm(carry)]
    ne = len(extra)

    def body(*refs):
        for cp in _pair_copies(refs[:nt], refs[nt:2 * nt], refs[2 * nt + ne:2 * nt + ne + ns]):
            cp.start()

    res = pl.pallas_call(
        body, name=name,
        in_specs=[HBM_SPEC] * (2 * nt + ne),
        out_specs=[SEM_SPEC] * ns + [HBM_SPEC] * (2 * nt + ne),
        out_shape=[pltpu.SemaphoreType.DMA(())] * ns + _hbm_like(grads) + _hbm_like(zones) + _hbm_like(extra),
        input_output_aliases={i: ns + i for i in range(2 * nt + ne)},
        compiler_params=pltpu.CompilerParams(has_side_effects=DATAFLOW),
    )(*[_hbm(a) for a in grads], *zones, *extra)
    handle = (list(res[:ns]), list(res[ns:ns + nt]), list(res[ns + nt:ns + 2 * nt]))
    return handle, (res[ns + 2 * nt] if ne else None)


def _pair_wait(handle, after, name):
    sems, srcs, zones = handle
    nt, ns = len(srcs), len(sems)

    def body(*refs):
        for cp in _pair_copies(refs[:nt], refs[nt:2 * nt], refs[2 * nt:2 * nt + ns]):
            cp.wait_send()
            cp.wait_recv()

    return pl.pallas_call(
        body, name=name,
        in_specs=[HBM_SPEC] * (2 * nt) + [SEM_SPEC] * ns + [ANY],
        out_specs=[HBM_SPEC] * nt,
        out_shape=_hbm_like(zones),
        input_output_aliases={nt + i: i for i in range(nt)},
        compiler_params=pltpu.CompilerParams(has_side_effects=DATAFLOW),
    )(*srcs, *zones, *sems, after)


def _rows_tile(r, row_bytes, cap_bytes):
    best = None
    for tr in range(16, r + 1, 16):
        if r % tr == 0 and tr * row_bytes <= cap_bytes:
            best = tr
    return best if best is not None else r


def _pair_sum(own, got, cidx, name):
    _, _, r, cdim = own.shape
    tr = _rows_tile(r, 2 * cdim, 2 * 1024 * 1024)

    def body(c_ref, a_ref, b_ref, o_ref):
        o_ref[...] = (a_ref[...].astype(F32) + b_ref[...].astype(F32)).astype(BF16)

    return pl.pallas_call(
        body, name=name,
        grid_spec=pltpu.PrefetchScalarGridSpec(
            num_scalar_prefetch=1, grid=(NCHIP, r // tr),
            in_specs=[pl.BlockSpec((None, None, tr, cdim), lambda j, i, c_ref: (j, c_ref[0], i, 0)),
                      pl.BlockSpec((None, tr, cdim), lambda j, i, c_ref: (j, i, 0))],
            out_specs=pl.BlockSpec((None, tr, cdim), lambda j, i, c_ref: (j, i, 0))),
        out_shape=jax.ShapeDtypeStruct((NCHIP, r, cdim), BF16),
        compiler_params=_params("arbitrary", "arbitrary"),
    )(cidx, own, got)


def _chip_copies(srcs, zones, slots, sems):
    x, y, c, chips = _place()
    out = []
    for t, (z, l) in enumerate(slots):
        for k, (cx, cy) in enumerate(chips):
            dst = zones[z].at[k] if l is None else zones[z].at[k, l]
            out.append(pltpu.make_async_remote_copy(
                src_ref=srcs[t].at[2 * cx + cy], dst_ref=dst,
                send_sem=sems[2 * (3 * t + k)], recv_sem=sems[2 * (3 * t + k) + 1],
                device_id=(cx, cy, c), device_id_type=MESH))
    return out


def _chip_start(sums, zones, slots, carry, name):
    nt, nz = len(sums), len(zones)
    ns = 6 * nt
    extra = [] if carry is None else [_hbm(carry)]
    ne = len(extra)

    def body(*refs):
        for cp in _chip_copies(refs[:nt], refs[nt:nt + nz], slots, refs[nt + nz + ne:nt + nz + ne + ns]):
            cp.start()

    res = pl.pallas_call(
        body, name=name,
        in_specs=[HBM_SPEC] * (nt + nz + ne),
        out_specs=[SEM_SPEC] * ns + [HBM_SPEC] * (nt + nz + ne),
        out_shape=[pltpu.SemaphoreType.DMA(())] * ns + _hbm_like(sums) + _hbm_like(zones) + _hbm_like(extra),
        input_output_aliases={i: ns + i for i in range(nt + nz + ne)},
        compiler_params=pltpu.CompilerParams(has_side_effects=DATAFLOW),
    )(*[_hbm(a) for a in sums], *zones, *extra)
    return (list(res[:ns]), list(res[ns:ns + nt]), list(res[ns + nt:ns + nt + nz]),
            (res[ns + nt + nz] if ne else None))


def _chip_wait(started, zones, zone_ids, after, name):
    started = [(sums, [(zone_ids.index(z), l) for z, l in slots], sems) for sums, slots, sems in started]
    nz = len(zones)
    flat_src = [a for sums, _, _ in started for a in sums]
    flat_sem = [s for _, _, sems in started for s in sems]
    n_src, n_sem = len(flat_src), len(flat_sem)

    def body(*refs):
        srcs, zs, sems = refs[:n_src], refs[n_src:n_src + nz], refs[n_src + nz:n_src + nz + n_sem]
        so, se = 0, 0
        for sums, slots, sem_list in started:
            for cp in _chip_copies(srcs[so:so + len(sums)], zs, slots, sems[se:se + len(sem_list)]):
                cp.wait_send()
                cp.wait_recv()
            so += len(sums)
            se += len(sem_list)

    return pl.pallas_call(
        body, name=name,
        in_specs=[HBM_SPEC] * (n_src + nz) + [SEM_SPEC] * n_sem + [ANY],
        out_specs=[HBM_SPEC] * nz,
        out_shape=_hbm_like(zones),
        input_output_aliases={n_src + i: i for i in range(nz)},
        compiler_params=pltpu.CompilerParams(has_side_effects=DATAFLOW),
    )(*flat_src, *zones, *flat_sem, after)


def _small_allreduce(parts, y_first, after, name):
    nt = len(parts)

    def body(*refs):
        srcs, outs, bufs = refs[:nt], refs[nt + 1:2 * nt + 1], refs[2 * nt + 1:3 * nt + 1]
        send_sems, recv_sems = refs[3 * nt + 1:]
        x, y, c, _ = _place()
        along = {"c": (x, y, 1 - c), "x": (1 - x, y, c), "y": (x, 1 - y, c)}
        for t in range(nt):
            outs[t][...] = srcs[t][...]
        for step in range(3):
            order = [("c", "y", "x") if t in y_first else ("c", "x", "y") for t in range(nt)]
            copies = [pltpu.make_async_remote_copy(
                src_ref=outs[t], dst_ref=bufs[t].at[step],
                send_sem=send_sems.at[step, t], recv_sem=recv_sems.at[step, t],
                device_id=along[order[t][step]], device_id_type=MESH) for t in range(nt)]
            for cp in copies:
                cp.start()
            for cp in copies:
                cp.wait()
            for t in range(nt):
                outs[t][...] = outs[t][...] + bufs[t][step]

    vm = pl.BlockSpec(memory_space=pltpu.VMEM)
    return pl.pallas_call(
        body, name=name,
        in_specs=[vm] * nt + [ANY], out_specs=[vm] * nt,
        out_shape=[jax.ShapeDtypeStruct(a.shape, F32) for a in parts],
        scratch_shapes=[pltpu.VMEM((3,) + a.shape, F32) for a in parts]
        + [pltpu.SemaphoreType.DMA((3, nt)), pltpu.SemaphoreType.DMA((3, nt))],
        compiler_params=pltpu.CompilerParams(has_side_effects=True, vmem_limit_bytes=VMEM_LIMIT),
    )(*parts, after)


def _adam_math(w, g, m, v):
    m2 = ADAM_B1 * m + (1.0 - ADAM_B1) * g
    v2 = ADAM_B2 * v + (1.0 - ADAM_B2) * (g * g)
    m_hat = m2 / (1.0 - ADAM_B1 ** ADAM_STEP)
    v_hat = v2 / (1.0 - ADAM_B2 ** ADAM_STEP)
    delta = -ADAM_LR * (m_hat / (jnp.sqrt(v_hat) + ADAM_EPS) + ADAM_WD * w)
    return delta, m2, v2


def _adam_big(w, m, v, parts, mine, chip, name):
    nl, r, cdim = w.shape
    tr = _rows_tile(r, 4 * cdim, 3 * 512 * 1024)

    def body(c_ref, w_ref, m_ref, v_ref, p_ref, *rest):
        mine_refs, (g_ref, d_ref, mo_ref, vo_ref) = rest[:nl], rest[nl:]
        own = mine_refs[0][...]
        for l in range(1, nl):
            own = jnp.where(pl.program_id(0) == l, mine_refs[l][...], own)
        g = ((p_ref[0].astype(F32) + p_ref[1].astype(F32)) + p_ref[2].astype(F32)) + own.astype(F32)
        delta, m2, v2 = _adam_math(w_ref[...], g, m_ref[...], v_ref[...])
        g_ref[...] = g
        d_ref[...] = delta
        mo_ref[...] = m2
        vo_ref[...] = v2

    spec = pl.BlockSpec((None, tr, cdim), lambda l, i, c_ref: (l, i, 0))
    mine_specs = [pl.BlockSpec((None, tr, cdim), lambda l, i, c_ref, ll=ll: (c_ref[0], jnp.where(l == ll, i, 0), 0))
                  for ll in range(nl)]
    return pl.pallas_call(
        body, name=name,
        grid_spec=pltpu.PrefetchScalarGridSpec(
            num_scalar_prefetch=1, grid=(nl, r // tr),
            in_specs=[spec, spec, spec, pl.BlockSpec((3, None, tr, cdim), lambda l, i, c_ref: (0, l, i, 0))]
            + mine_specs,
            out_specs=[spec] * 4),
        out_shape=[jax.ShapeDtypeStruct(w.shape, F32)] * 4,
        compiler_params=_params("arbitrary", "arbitrary"),
    )(chip, w, m, v, parts, *mine)


def _adam_small(ws, gs, ms, vs, name):
    n = len(ws)

    def body(*refs):
        w_r, g_r, m_r, v_r = refs[:n], refs[n:2 * n], refs[2 * n:3 * n], refs[3 * n:4 * n]
        d_o, m_o, v_o = refs[4 * n:5 * n], refs[5 * n:6 * n], refs[6 * n:7 * n]
        for t in range(n):
            delta, m2, v2 = _adam_math(w_r[t][...], g_r[t][...], m_r[t][...], v_r[t][...])
            d_o[t][...] = delta
            m_o[t][...] = m2
            v_o[t][...] = v2

    vm = pl.BlockSpec(memory_space=pltpu.VMEM)
    shapes = [jax.ShapeDtypeStruct(a.shape, F32) for a in ws]
    return pl.pallas_call(
        body, name=name, in_specs=[vm] * (4 * n), out_specs=[vm] * (3 * n), out_shape=shapes * 3,
        compiler_params=pltpu.CompilerParams(vmem_limit_bytes=VMEM_LIMIT),
    )(*ws, *gs, *ms, *vs)


def kernel(x, norm_mix, norm_ffn, norm_final, ab_w_in, a_ln_g, a_ln_b, a_w_s, a_b_s, b_conv_w, b_conv_b, b_ln_g, b_ln_b, ab_w_out, c_w_in, c_conv_w, c_w_out, f_w_up, f_conv_w, f_w_down, loss_target, m_norm_mix, m_norm_ffn, m_norm_final, m_ab_w_in, m_a_ln_g, m_a_ln_b, m_a_w_s, m_a_b_s, m_b_conv_w, m_b_conv_b, m_b_ln_g, m_b_ln_b, m_ab_w_out, m_c_w_in, m_c_conv_w, m_c_w_out, m_f_w_up, m_f_conv_w, m_f_w_down, v_norm_mix, v_norm_ffn, v_norm_final, v_ab_w_in, v_a_ln_g, v_a_ln_b, v_a_w_s, v_a_b_s, v_b_conv_w, v_b_conv_b, v_b_ln_g, v_b_ln_b, v_ab_w_out, v_c_w_in, v_c_conv_w, v_c_w_out, v_f_w_up, v_f_conv_w, v_f_w_down):
    s = x.shape[1]
    x0 = x.reshape(s, D)
    tgt = loss_target.reshape(s, D)
    xi, yi, ci = lax.axis_index("x"), lax.axis_index("y"), lax.axis_index("c")
    dev = 4 * xi + 2 * yi + ci
    cidx = ci.astype(jnp.int32).reshape(1)

    bf = lambda a: a.astype(BF16)
    slab_w = 6 * CHUNK
    pad = lambda a, rows: jnp.pad(a, ((0, rows - a.shape[0]), (0, slab_w - a.shape[1])))
    slab = jnp.concatenate([pad(b_conv_w[0], 32), pad(c_conv_w[0], 8), pad(f_conv_w.reshape(6, FB), 8)], axis=0)
    later = [bf(ab_w_in[0]), bf(ab_w_out[0]), slab, bf(f_w_up[0]), bf(f_w_down[0]), bf(c_w_in[0]), bf(c_w_out[0]),
             bf(f_w_up[1]), bf(f_w_down[1])]
    lands = [_zone(a, dev) for a in later]
    groups = [[0], [1, 2], [3, 4], [5, 6], [7, 8]]
    ag_sems, later, lands, ag_token = _ag_start(later, lands, x0, "ag_start")

    causal = jnp.tril(jnp.ones((CHUNK, CHUNK), F32))
    wsm = (a_w_s[0] * causal).astype(BF16)
    bs_col = a_b_s.reshape(HEADS, CHUNK, 1)
    nm = [norm_mix[0:1], norm_mix[1:2]]
    nf = [norm_ffn[0:1], norm_ffn[1:2]]
    nfin = norm_final.reshape(1, D)

    def pass_on(ts_, after_ici, tag):
        srcs = [later[t] for t in ts_]
        sems1 = [ag_sems[t] for t in ts_]
        sems2, zone = _ag_forward(srcs, [lands[t] for t in ts_], sems1, after_ici, "ag_forward_" + tag)
        return srcs, zone, sems1, sems2

    def arrive(g, after_ici, after_d2d, tag):
        srcs, zone, sems1, sems2 = pass_on(groups[g], after_ici, tag)
        return _ag_finish(srcs, zone, sems1, sems2, after_d2d, "ag_finish_" + tag)

    h0 = _rms_fwd(x0, nm[0], "rms_mix0", after=ag_token)
    (win0,) = arrive(0, h0, h0, "w_in")
    z = _mm_in(h0, win0, "mm_ab_in")
    wout0, slab_g = arrive(1, z, z, "first")
    wout0 = wout0.reshape(D, D)
    bcw = jnp.transpose(slab_g[:, 0:BCONV, 0:DA // NDEV], (1, 0, 2)).reshape(BCONV, DA)
    ccw = jnp.transpose(slab_g[:, 32:35, 0:D // NDEV], (1, 0, 2)).reshape(3, D)
    fcw_g = slab_g[:, 40:46, 0:FB].reshape(2, NG, 2, 3, FB)
    fcws = [fcw_g[:, :, 0], fcw_g[:, :, 1]]
    ycat, yb2 = _ab_fwd(z, a_ln_g, a_ln_b, wsm, bs_col, bcw, b_conv_b, b_ln_g, b_ln_b, "ab_fwd")
    x1, h1 = _mm_out(ycat, wout0, x0, nf[0], "mm_ab_out")
    wup0, wdn0 = arrive(2, x1, x1, "ffn0")
    up0, upc0, x2, h2 = _ffn_fwd(h1, x1, wup0.reshape(2, NG, D, FB), fcws[0], wdn0.reshape(DFF, D), nm[1],
                                 "ffn_fwd0")
    cin, cout = arrive(3, x2, x2, "c")
    cout = cout.reshape(D, D)
    zc = _mm_in(h2, cin, "mm_c_in")
    last_part = pass_on(groups[4], zc, "ffn1")
    rc, x3, h3 = _c_fwd(zc, ccw, cout, x2, nf[1], "c_fwd_out")
    wup1, wdn1 = _ag_finish(*last_part, x3, "ag_finish_ffn1")
    wups = [wup0.reshape(2, NG, D, FB), wup1.reshape(2, NG, D, FB)]
    wdns = [wdn0.reshape(DFF, D), wdn1.reshape(DFF, D)]
    up1, upc1, dx4, dx4b, dnfin, loss_part = _ffn_fwd(h3, x3, wups[1], fcws[1], wdns[1], None, "ffn_fwd1_loss",
                                                      final=(tgt, nfin))

    zshape = lambda *sh: _hbm(lax.empty((3,) + sh, BF16))
    zones = [zshape(D, 2 * D // NDEV), zshape(D // NDEV, D), zshape(D, 3 * D // NDEV), zshape(D // NDEV, D),
             zshape(2, FB, D), zshape(2, DFF // NDEV, D)]
    started = []

    def pair_sums(grads, handle, after, tag):
        del grads
        got = _pair_wait(handle, after, "rs_pair_wait_" + tag)
        return [_pair_sum(b.reshape((NCHIP, 2) + b.shape[1:]), g, cidx, "rs_pair_sum_%s%d" % (tag, t))
                for t, (b, g) in enumerate(zip(handle[1], got))]

    def chip_start(sums, slots, carry, tag):
        sems, sums, new_zones, carry = _chip_start(sums, zones, slots, carry, "rs_chip_start_" + tag)
        zones[:] = new_zones
        started.append((sums, slots, sems))
        return sums, carry

    rows8 = lambda g, r: g.reshape(NDEV, r, D)
    a1, dup1, dx3, dx3b, dnf1, dfcw1 = _ffn_bwd(dx4, up1, upc1, wups[1], fcws[1], wdns[1], x3, nf[1], "ffn_bwd1")
    g_f1 = [_dw_up(h3, dup1, "dw_up1"), rows8(_dw_dn(a1, dx4b, "dw_dn1"), DFF // NDEV)]
    hd_f1, dx3b = _pair_start(g_f1, dx3b, "rs_pair_start_f1")
    dzc, dccw, g_cout = _c_bwd(dx3b, cout, rc, zc, ccw, "c_bwd")
    g_cout = rows8(g_cout, D // NDEV)
    s_f1 = pair_sums(g_f1, hd_f1, g_cout, "f1")
    s_f1, dzc = chip_start(s_f1, [(4, 1), (5, 1)], dzc, "f1")
    dx2, dx2b, dnm1 = _mm_nt_rms(dzc, cin, x2, nm[1], dx3, True, "mm_c_in_bwd")
    g_c = [_dw_cols(h2, dzc, NDEV, 3 * D // NDEV, "dw_c_in"), g_cout]
    hd_c, dx2 = _pair_start(g_c, dx2, "rs_pair_start_c")
    a0, dup0, dx1, dx1b, dnf0, dfcw0 = _ffn_bwd(dx2, up0, upc0, wups[0], fcws[0], wdns[0], x1, nf[0], "ffn_bwd0")
    s_c = pair_sums(g_c, hd_c, dx1b, "c")
    s_c, dx1b = chip_start(s_c, [(2, None), (3, None)], dx1b, "c")
    g_f0 = [_dw_up(h1, dup0, "dw_up0"), rows8(_dw_dn(a0, dx2b, "dw_dn0"), DFF // NDEV)]
    hd_f0, dx1b = _pair_start(g_f0, dx1b, "rs_pair_start_f0")
    g_wout0 = rows8(_dw_rows(ycat, dx1b, "dw_ab_out"), D // NDEV)
    s_f0 = pair_sums(g_f0, hd_f0, g_wout0, "f0")
    s_f0, dx1b = chip_start(s_f0, [(4, 0), (5, 0)], dx1b, "f0")
    dz, g512, dws, dbs = _ab_bwd(dx1b, wout0, z, yb2, a_ln_g, a_ln_b, wsm, bs_col, bcw, b_ln_g, b_ln_b, "ab_bwd")
    grad_x, dnm0 = _mm_nt_rms(dz, win0, x0, nm[0], dx1, False, "mm_ab_in_bwd")
    g_ab = [_dw_cols(h0, dz, NDEV, 2 * D // NDEV, "dw_ab_in"), g_wout0]
    hd_ab, _ = _pair_start(g_ab, None, "rs_pair_start_ab")

    g1024 = jnp.concatenate([dnm0, dnm1, dnf0, dnf1, dnfin, dccw], axis=0)
    gfc = jnp.concatenate([dfcw0, dfcw1], axis=0).reshape(2 * NG * 2 * 3, FB)
    g1024, g512, dws, dbs, gfc, loss_sum = _small_allreduce(
        [g1024, g512, dws.reshape(HEADS * CHUNK, CHUNK), dbs.reshape(HEADS, CHUNK), gfc, loss_part], (2,),
        hd_ab[1][0], "small_allreduce")
    loss = loss_sum[0, 0]
    s_ab = pair_sums(g_ab, hd_ab, g1024, "ab")
    s_ab, _ = chip_start(s_ab, [(0, None), (1, None)], None, "ab")
    p_cin, p_cout, p_wup, p_wdn = _chip_wait(started[:3], zones[2:], [2, 3, 4, 5], s_ab[0], "rs_chip_wait_early")

    chip = (2 * xi + yi).astype(jnp.int32).reshape(1)

    def big_update(w, m, v, parts, mine, name):
        shp = w.shape
        w3, m3, v3 = (a.reshape((-1,) + shp[-2:]) for a in (w, m, v))
        p4 = parts.reshape((3,) + w3.shape)
        return [o.reshape(shp) for o in _adam_big(w3, m3, v3, p4, mine, chip, name)]

    u_cin = big_update(c_w_in, m_c_w_in, v_c_w_in, p_cin, [s_c[0]], "adam_c_w_in")
    u_cout = big_update(c_w_out, m_c_w_out, v_c_w_out, p_cout, [s_c[1]], "adam_c_w_out")
    tr_ = lambda a: jnp.swapaxes(a, 1, 2)
    u_wup = [tr_(o) for o in big_update(tr_(f_w_up), tr_(m_f_w_up), tr_(v_f_w_up), p_wup,
                                        [s_f0[0], s_f1[0]], "adam_f_w_up")]
    u_wdn = big_update(f_w_down, m_f_w_down, v_f_w_down, p_wdn, [s_f0[1], s_f1[1]], "adam_f_w_down")
    p_win0, p_wout0 = _chip_wait(started[3:], zones[:2], [0, 1], u_wdn[0], "rs_chip_wait_late")
    u_win0 = big_update(ab_w_in, m_ab_w_in, v_ab_w_in, p_win0, [s_ab[0]], "adam_ab_w_in")
    u_wout0 = big_update(ab_w_out, m_ab_w_out, v_ab_w_out, p_wout0, [s_ab[1]], "adam_ab_w_out")

    g_norm_mix = g1024[0:2]
    g_norm_ffn = g1024[2:4]
    g_norm_final = g1024[4:5]
    g_ccw = lax.dynamic_slice(g1024[5:8], (0, dev * (D // NDEV)), (3, D // NDEV))
    g_bcw = lax.dynamic_slice(g512[8:8 + BCONV], (0, dev * (DA // NDEV)), (BCONV, DA // NDEV))
    gfc = gfc.reshape(2, NG, 2, 3, FB)
    g_fcw = lax.dynamic_slice(gfc, (0, dev % NG, dev // NG, 0, 0), (2, 1, 1, 3, FB)).reshape(2, 3, FB)
    small_w = [norm_mix, norm_ffn, nfin, a_ln_g, a_ln_b, a_w_s[0], a_b_s[0], b_conv_w[0], b_conv_b,
               b_ln_g, b_ln_b, c_conv_w[0], f_conv_w]
    small_g = [g_norm_mix, g_norm_ffn, g_norm_final, g512[0:1], g512[1:2],
               dws.reshape(HEADS, CHUNK, CHUNK), dbs, g_bcw, g512[2:3],
               g512[3:4], g512[4:5], g_ccw, g_fcw]
    small_m = [m_norm_mix, m_norm_ffn, m_norm_final.reshape(1, D), m_a_ln_g, m_a_ln_b, m_a_w_s[0], m_a_b_s[0],
               m_b_conv_w[0], m_b_conv_b, m_b_ln_g, m_b_ln_b, m_c_conv_w[0], m_f_conv_w]
    small_v = [v_norm_mix, v_norm_ffn, v_norm_final.reshape(1, D), v_a_ln_g, v_a_ln_b, v_a_w_s[0], v_a_b_s[0],
               v_b_conv_w[0], v_b_conv_b, v_b_ln_g, v_b_ln_b, v_c_conv_w[0], v_f_conv_w]
    upd = _adam_small(small_w, small_g, small_m, small_v, "adam_small")
    ns = len(small_w)
    orig = [norm_mix, norm_ffn, norm_final, a_ln_g, a_ln_b, a_w_s, a_b_s, b_conv_w, b_conv_b,
            b_ln_g, b_ln_b, c_conv_w, f_conv_w]
    sg_out = [g.reshape(o.shape) for g, o in zip(small_g, orig)]
    sd_out = [a.reshape(o.shape) for a, o in zip(upd[0:ns], orig)]
    sm_out = [a.reshape(o.shape) for a, o in zip(upd[ns:2 * ns], orig)]
    sv_out = [a.reshape(o.shape) for a, o in zip(upd[2 * ns:3 * ns], orig)]

    def assemble(small, k):
        return [small[0], small[1], small[2], u_win0[k], small[3], small[4], small[5], small[6], small[7],
                small[8], small[9], small[10], u_wout0[k], u_cin[k], small[11], u_cout[k], u_wup[k],
                small[12], u_wdn[k]]

    grads = assemble(sg_out, 0)
    deltas = assemble(sd_out, 1)
    new_m = assemble(sm_out, 2)
    new_v = assemble(sv_out, 3)
    return (loss, grad_x.reshape(1, s, D), *grads, *deltas, *new_m, *new_v)
```

```python
import math

import jax
import jax.numpy as jnp
from jax import lax
from jax.experimental import pallas as pl
from jax.experimental.pallas import tpu as pltpu

F32 = jnp.float32
BF16 = jnp.bfloat16

D = 1024
DA = 512
HEADS = 4
CHUNK = 128
DFF = 2816
NDEV = 8
NCHIP = 4
FB = DFF * 2 // NDEV
NG = DFF // FB
BCONV = 31
EPS = 1e-6
HALO = 16
HALO_B = 32
RC = 32
NPART = 2
VMEM_LIMIT = 52 * 1024 * 1024
INV_SQRT2 = 1.0 / math.sqrt(2.0)
INV_SQRT_2PI = 1.0 / math.sqrt(2.0 * math.pi)

ADAM_LR = 0.001
ADAM_B1 = 0.9
ADAM_B2 = 0.999
ADAM_EPS = 1e-08
ADAM_WD = 0.01
ADAM_STEP = 10

MESH = pl.DeviceIdType.MESH
ANY = pl.BlockSpec(memory_space=pl.ANY)
NT_DIMS = (((1,), (1,)), ((), ()))
TN_DIMS = (((0,), (0,)), ((), ()))


def _params(*sem):
    return pltpu.CompilerParams(dimension_semantics=sem, vmem_limit_bytes=VMEM_LIMIT)


def _tile(s, want):
    return min(want, s)


def _sigmoid(x):
    return jax.nn.sigmoid(x)


def _dsilu(x, sg):
    return sg * (1.0 + x * (1.0 - sg))


def _gelu(x):
    return 0.5 * x * (1.0 + lax.erf(x * INV_SQRT2))


def _dgelu(x):
    return 0.5 * (1.0 + lax.erf(x * INV_SQRT2)) + x * jnp.exp(-0.5 * x * x) * INV_SQRT_2PI


def _ln_fwd(x, g, b):
    mu = jnp.mean(x, axis=-1, keepdims=True)
    xc = x - mu
    var = jnp.mean(xc * xc, axis=-1, keepdims=True)
    rstd = lax.rsqrt(var + EPS)
    xhat = xc * rstd
    return xhat * g + b, xhat, rstd


def _ln_bwd(dy, xhat, rstd, g):
    dxh = dy * g
    m1 = jnp.mean(dxh, axis=-1, keepdims=True)
    m2 = jnp.mean(dxh * xhat, axis=-1, keepdims=True)
    return rstd * (dxh - m1 - xhat * m2)


def _rms_bwd_math(dh, x, g):
    r = lax.rsqrt(jnp.mean(x * x, axis=-1, keepdims=True) + EPS)
    xhat = x * r
    dg = jnp.sum(dh * xhat, axis=0, keepdims=True)
    u = dh * g
    dx = r * (u - xhat * jnp.mean(u * xhat, axis=-1, keepdims=True))
    return dx, dg


def _conv3(xe, cw, halo):
    x0 = xe[halo:]
    x1 = pltpu.roll(xe, 1, 0)[halo:]
    x2 = pltpu.roll(xe, 2, 0)[halo:]
    return cw[2] * x0 + cw[1] * x1 + cw[0] * x2, (x0, x1, x2)


def _conv3_bwd_in(dce, cw, ts):
    n = dce.shape[0]
    d1 = pltpu.roll(dce, n - 1, 0)[:ts]
    d2 = pltpu.roll(dce, n - 2, 0)[:ts]
    return cw[2] * dce[:ts] + cw[1] * d1 + cw[0] * d2


def _conv3_bwd_w(dc, taps):
    x0, x1, x2 = taps
    return [jnp.sum(dc * x2, axis=0, keepdims=True), jnp.sum(dc * x1, axis=0, keepdims=True),
            jnp.sum(dc * x0, axis=0, keepdims=True)]


def _rms_fwd(x, g, name, after=None):
    s = x.shape[0]
    ts = _tile(s, 512)

    def body(x_ref, g_ref, *rest):
        h_ref = rest[-1]
        xv = x_ref[...]
        r = lax.rsqrt(jnp.mean(xv * xv, axis=-1, keepdims=True) + EPS)
        h_ref[...] = (xv * r * g_ref[...]).astype(BF16)

    extra = [] if after is None else [after]
    return pl.pallas_call(
        body, grid=(s // ts,), name=name,
        in_specs=[pl.BlockSpec((ts, D), lambda i: (i, 0)), pl.BlockSpec((1, D), lambda i: (0, 0))]
        + [ANY] * len(extra),
        out_specs=pl.BlockSpec((ts, D), lambda i: (i, 0)),
        out_shape=jax.ShapeDtypeStruct((s, D), BF16),
        compiler_params=_params("parallel"),
    )(x, g, *extra)


MXU_COLS = 256


def _pair(bn):
    return 1 if bn % MXU_COLS == 0 else 2


def _cols(w_ref, b, pair):
    return w_ref[b] if pair == 1 else jnp.concatenate([w_ref[b + q] for q in range(pair)], axis=1)


def _mm_in(h, wblk, name):
    s = h.shape[0]
    nb, _, bn = wblk.shape
    pair = _pair(bn)
    ts = _tile(s, 1024)

    def body(h_ref, w_ref, o_ref):
        hv = h_ref[...]
        for b in range(0, nb, pair):
            o_ref[:, b * bn:(b + pair) * bn] = jnp.dot(hv, _cols(w_ref, b, pair),
                                                       preferred_element_type=F32).astype(BF16)

    return pl.pallas_call(
        body, grid=(s // ts,), name=name,
        in_specs=[pl.BlockSpec((ts, D), lambda i: (i, 0)), pl.BlockSpec((nb, D, bn), lambda i: (0, 0, 0))],
        out_specs=pl.BlockSpec((ts, nb * bn), lambda i: (i, 0)),
        out_shape=jax.ShapeDtypeStruct((s, nb * bn), BF16),
        compiler_params=_params("parallel"),
    )(h, wblk)


def _rms_math(xv, g):
    r = lax.rsqrt(jnp.mean(xv * xv, axis=-1, keepdims=True) + EPS)
    return (xv * r * g).astype(BF16)


def _mm_out(y, w, xres, gnext, name):
    s = y.shape[0]
    ts = _tile(s, 1024)

    def body(y_ref, w_ref, x_ref, g_ref, o_ref, h_ref):
        xn = x_ref[...] + jnp.dot(y_ref[...], w_ref[...], preferred_element_type=F32)
        o_ref[...] = xn
        h_ref[...] = _rms_math(xn, g_ref[...])

    return pl.pallas_call(
        body, grid=(s // ts,), name=name,
        in_specs=[pl.BlockSpec((ts, D), lambda i: (i, 0)), pl.BlockSpec((D, D), lambda i: (0, 0)),
                  pl.BlockSpec((ts, D), lambda i: (i, 0)), pl.BlockSpec((1, D), lambda i: (0, 0))],
        out_specs=[pl.BlockSpec((ts, D), lambda i: (i, 0)), pl.BlockSpec((ts, D), lambda i: (i, 0))],
        out_shape=[jax.ShapeDtypeStruct((s, D), F32), jax.ShapeDtypeStruct((s, D), BF16)],
        compiler_params=_params("parallel"),
    )(y, w, xres, gnext)


CONV_ROWS = 32


def _rolled_copies(dst_ref, xe, back):
    n = xe.shape[0]
    dst_ref[0] = xe
    for r in range(1, 8):
        dst_ref[r] = pltpu.roll(xe, n - r if back else r, 0)


def _conv31(rolled_ref, cw_ref, ts, out_ref, bias):
    for o in range(0, ts, CONV_ROWS):
        acc = jnp.zeros((CONV_ROWS, DA), F32) + bias
        for sh in range(BCONV):
            q, r = divmod(sh, 8)
            lo = HALO_B - 8 * q + o
            acc = acc + cw_ref[BCONV - 1 - sh:BCONV - sh, :] * rolled_ref[r, lo:lo + CONV_ROWS, :]
        out_ref[o:o + CONV_ROWS, :] = acc


def _ab_fwd(z, lga, lba, wsm, bs_col, cwb, cbb, lgb, lbb, name):
    s = z.shape[0]
    ts = _tile(s, 256)
    hb = ts // HALO_B

    def body(z_ref, zh_ref, lga_ref, lba_ref, ws_ref, bs_ref, cw_ref, cb_ref, lgb_ref, lbb_ref,
             y_ref, yb2_ref, rolled):
        i = pl.program_id(0)
        z_t = z_ref[...].astype(F32)
        gu = _gelu(z_t[:, 0:DA])
        gv = _gelu(z_t[:, DA:2 * DA])
        vn, _, _ = _ln_fwd(gv, lga_ref[...], lba_ref[...])
        vnb = vn.astype(BF16)
        for c in range(ts // CHUNK):
            for h in range(HEADS):
                rs = slice(c * CHUNK, (c + 1) * CHUNK)
                cs = slice(h * CHUNK, (h + 1) * CHUNK)
                mixed = jnp.dot(ws_ref[h], vnb[rs, cs], preferred_element_type=F32) + bs_ref[h]
                y_ref[rs, cs] = (gu[rs, cs] * mixed).astype(BF16)
        zh = jnp.where(i > 0, zh_ref[...], jnp.zeros_like(zh_ref[...])).astype(F32)
        xb = jnp.concatenate([zh[:, 0:DA], z_t[:, 2 * DA:3 * DA]], axis=0)
        gb = jnp.concatenate([zh[:, DA:2 * DA], z_t[:, 3 * DA:4 * DA]], axis=0)
        _rolled_copies(rolled, xb * _sigmoid(gb), False)
        _conv31(rolled, cw_ref, ts, yb2_ref, cb_ref[...])
        nb_, _, _ = _ln_fwd(yb2_ref[...], lgb_ref[...], lbb_ref[...])
        y_ref[:, DA:2 * DA] = (nb_ * _sigmoid(nb_)).astype(BF16)

    row = lambda i: (0, 0)
    return pl.pallas_call(
        body, grid=(s // ts,), name=name,
        in_specs=[pl.BlockSpec((ts, 4 * DA), lambda i: (i, 0)),
                  pl.BlockSpec((HALO_B, 2 * DA), lambda i: (jnp.maximum(i * hb - 1, 0), 1)),
                  pl.BlockSpec((1, DA), row), pl.BlockSpec((1, DA), row),
                  pl.BlockSpec((HEADS, CHUNK, CHUNK), lambda i: (0, 0, 0)),
                  pl.BlockSpec((HEADS, CHUNK, 1), lambda i: (0, 0, 0)),
                  pl.BlockSpec((BCONV, DA), row), pl.BlockSpec((1, DA), row),
                  pl.BlockSpec((1, DA), row), pl.BlockSpec((1, DA), row)],
        out_specs=[pl.BlockSpec((ts, 2 * DA), lambda i: (i, 0)), pl.BlockSpec((ts, DA), lambda i: (i, 0))],
        out_shape=[jax.ShapeDtypeStruct((s, 2 * DA), BF16), jax.ShapeDtypeStruct((s, DA), F32)],
        scratch_shapes=[pltpu.VMEM((8, ts + HALO_B, DA), F32)],
        compiler_params=_params("parallel"),
    )(z, z, lga, lba, wsm, bs_col, cwb, cbb, lgb, lbb)


def _c_fwd(zc, cw, w, xres, gnext, name):
    s = zc.shape[0]
    ts = _tile(s, 512)
    hb = ts // HALO

    def body(z_ref, ch_ref, xh_ref, cw_ref, w_ref, x_ref, g_ref, r_ref, o_ref, h_ref):
        i = pl.program_id(0)
        z_t = z_ref[...].astype(F32)
        ph = jnp.where(i > 0, ch_ref[...].astype(F32) * xh_ref[...].astype(F32), 0.0)
        pe = jnp.concatenate([ph, z_t[:, D:2 * D] * z_t[:, 2 * D:3 * D]], axis=0)
        q, _ = _conv3(pe, [cw_ref[k:k + 1, :] for k in range(3)], HALO)
        r = (z_t[:, 0:D] * q).astype(BF16)
        r_ref[...] = r
        xn = x_ref[...] + jnp.dot(r, w_ref[...], preferred_element_type=F32)
        o_ref[...] = xn
        h_ref[...] = _rms_math(xn, g_ref[...])

    halo = lambda col: pl.BlockSpec((HALO, D), lambda i: (jnp.maximum(i * hb - 1, 0), col))
    tile = pl.BlockSpec((ts, D), lambda i: (i, 0))
    return pl.pallas_call(
        body, grid=(s // ts,), name=name,
        in_specs=[pl.BlockSpec((ts, 3 * D), lambda i: (i, 0)), halo(1), halo(2),
                  pl.BlockSpec((3, D), lambda i: (0, 0)), pl.BlockSpec((D, D), lambda i: (0, 0)), tile,
                  pl.BlockSpec((1, D), lambda i: (0, 0))],
        out_specs=[tile, tile, tile],
        out_shape=[jax.ShapeDtypeStruct((s, D), BF16), jax.ShapeDtypeStruct((s, D), F32),
                   jax.ShapeDtypeStruct((s, D), BF16)],
        compiler_params=_params("parallel"),
    )(zc, zc, zc, cw, w, xres, gnext)


def _final_math(xv, tv, gv):
    r = lax.rsqrt(jnp.mean(xv * xv, axis=-1, keepdims=True) + EPS)
    xhat = xv * r
    e = xhat * gv - tv
    part = 0.5 * jnp.sum(jnp.mean(e * e, axis=-1, keepdims=True), axis=0, keepdims=True)
    dy = e * (1.0 / D)
    dgp = jnp.sum(dy * xhat, axis=0, keepdims=True)
    u = dy * gv
    dx = r * (u - xhat * jnp.mean(u * xhat, axis=-1, keepdims=True))
    return dx, dgp, jnp.broadcast_to(part, (1, 128))


def _ffn_fwd(h, xres, wup, fcw, wdn, gnext, name, final=None):
    s = h.shape[0]
    ts = _tile(s, 512)
    hb = ts // HALO

    def body(h_ref, hh_ref, w_ref, cw_ref, wd_ref, x_ref, *rest):
        if final is not None:
            t_ref, gf_ref, up_ref, upc_ref, dx_ref, dxb_ref, dg_ref, loss_ref, up_s, xo_ref = rest
        elif gnext is not None:
            gn_ref, up_ref, upc_ref, xo_ref, hn_ref, up_s = rest
        else:
            up_ref, upc_ref, xo_ref, up_s = rest
        i = pl.program_id(0)
        m = pl.program_id(1)
        @pl.when(m == 0)
        def _():
            xo_ref[...] = x_ref[...]

        halo = jnp.where(i > 0, hh_ref[...], jnp.zeros_like(hh_ref[...]))
        hx = jnp.concatenate([halo, h_ref[...]], axis=0)
        acts = []
        for gv in range(2):
            up_s[gv] = jnp.dot(hx, w_ref[gv], preferred_element_type=F32)
            x0 = up_s[gv, HALO:HALO + ts, :]
            up_ref[gv] = x0.astype(BF16)
            upc = (cw_ref[gv, 2:3, :] * x0 + cw_ref[gv, 1:2, :] * up_s[gv, HALO - 1:HALO - 1 + ts, :]
                   + cw_ref[gv, 0:1, :] * up_s[gv, HALO - 2:HALO - 2 + ts, :])
            upc_ref[gv] = upc.astype(BF16)
            acts.append(upc)
        a = acts[0] * _sigmoid(acts[0]) * acts[1]
        xo_ref[...] += jnp.dot(a.astype(BF16), wd_ref[...], preferred_element_type=F32)

        if final is not None:
            @pl.when(m == NG - 1)
            def _():
                dx, dgp, part = _final_math(xo_ref[...], t_ref[...], gf_ref[...])
                dx_ref[...] = dx
                dxb_ref[...] = dx.astype(BF16)

                @pl.when(i == 0)
                def _():
                    dg_ref[...] = dgp
                    loss_ref[...] = part

                @pl.when(i > 0)
                def _():
                    dg_ref[...] += dgp
                    loss_ref[...] += part

        elif gnext is not None:
            @pl.when(m == NG - 1)
            def _():
                hn_ref[...] = _rms_math(xo_ref[...], gn_ref[...])

    tile = pl.BlockSpec((ts, D), lambda i, m: (i, 0))
    row = lambda n: pl.BlockSpec((1, n), lambda i, m: (0, 0))
    scratch = [pltpu.VMEM((2, ts + HALO, FB), F32)]
    if final is not None:
        more_in, more_ops = [tile, row(D)], list(final)
        more_out = [tile, tile, row(D), row(128)]
        more_shape = [jax.ShapeDtypeStruct((s, D), F32), jax.ShapeDtypeStruct((s, D), BF16),
                      jax.ShapeDtypeStruct((1, D), F32), jax.ShapeDtypeStruct((1, 128), F32)]
        scratch.append(pltpu.VMEM((ts, D), F32))
    else:
        nxt = gnext is not None
        more_in, more_ops = ([row(D)], [gnext]) if nxt else ([], [])
        more_out = [tile] + ([tile] if nxt else [])
        more_shape = [jax.ShapeDtypeStruct((s, D), F32)] + ([jax.ShapeDtypeStruct((s, D), BF16)] if nxt else [])
    return pl.pallas_call(
        body, grid=(s // ts, NG), name=name,
        in_specs=[tile,
                  pl.BlockSpec((HALO, D), lambda i, m: (jnp.maximum(i * hb - 1, 0), 0)),
                  pl.BlockSpec((2, None, D, FB), lambda i, m: (0, m, 0, 0)),
                  pl.BlockSpec((2, None, 3, FB), lambda i, m: (0, m, 0, 0)),
                  pl.BlockSpec((FB, D), lambda i, m: (m, 0)),
                  tile] + more_in,
        out_specs=[pl.BlockSpec((None, 2, ts, FB), lambda i, m: (m, 0, i, 0)),
                   pl.BlockSpec((None, 2, ts, FB), lambda i, m: (m, 0, i, 0))] + more_out,
        out_shape=[jax.ShapeDtypeStruct((NG, 2, s, FB), BF16), jax.ShapeDtypeStruct((NG, 2, s, FB), BF16)] + more_shape,
        scratch_shapes=scratch,
        compiler_params=_params("arbitrary", "arbitrary"),
    )(h, h, wup, fcw, wdn, xres, *more_ops)


def _ffn_bwd(df, up, upc, wup, fcw, wdn, xin, g, name):
    s = df.shape[0]
    ts = _tile(s, 512)
    nt = s // ts

    def body(df_ref, up_ref, upc_ref, w_ref, cw_ref, wd_ref, x_ref, g_ref,
             a_ref, dup_ref, dx_ref, dxb_ref, dg_ref, dcw_ref, carry, acc, tacc, dcs_ref):
        i = pl.program_id(0)
        m = pl.program_id(1)
        first = i == 0
        @pl.when(first)
        def _():
            carry[m] = jnp.zeros((2, 8, FB), F32)
            dcw_ref[m] = jnp.zeros((2, 3, FB), F32)

        @pl.when(m == 0)
        def _():
            acc[...] = jnp.zeros((ts, D), F32)

        cws = [[cw_ref[gv, k:k + 1, :] for k in range(3)] for gv in range(2)]
        part = ts // NPART
        das = [lax.dot_general(df_ref[p * part:(p + 1) * part, :].astype(BF16), wd_ref[...], NT_DIMS,
                               preferred_element_type=F32) for p in range(NPART)]

        tacc[...] = jnp.zeros((2, 3, 8, FB), F32)
        dcs_ref[:, ts:ts + 8, :] = carry[m]
        for r in reversed(range(ts // RC)):
            rs = slice(r * RC, (r + 1) * RC)
            gate = upc_ref[0, rs, :].astype(F32)
            val = upc_ref[1, rs, :].astype(F32)
            sg = _sigmoid(gate)
            sl = gate * sg
            a_ref[rs, :] = (sl * val).astype(BF16)
            da_c = das[(r * RC) // part][(r * RC) % part:(r * RC) % part + RC]
            dcs = [da_c * val * _dsilu(gate, sg), da_c * sl]
            for gv in range(2):
                dc = dcs[gv]
                dcs_ref[gv, rs, :] = dc
                d1 = dcs_ref[gv, r * RC + 1:(r + 1) * RC + 1, :]
                d2 = dcs_ref[gv, r * RC + 2:(r + 1) * RC + 2, :]
                du = cws[gv][2] * dc + cws[gv][1] * d1 + cws[gv][0] * d2
                dup_ref[gv, rs, :] = du.astype(BF16)
                x0 = up_ref[gv, rs, :].astype(F32)
                for k, dk in enumerate((d2, d1, dc)):
                    p = x0 * dk
                    tacc[gv, k] += sum(p[j:j + 8] for j in range(0, RC, 8))
            if (r * RC) % part == 0:
                ps = slice(r * RC, r * RC + part)
                acc[ps, :] += (
                    lax.dot_general(dup_ref[0, ps, :], w_ref[0], NT_DIMS, preferred_element_type=F32)
                    + lax.dot_general(dup_ref[1, ps, :], w_ref[1], NT_DIMS, preferred_element_type=F32))
        for gv in range(2):
            carry[m, gv] = dcs_ref[gv, 0:8, :]
            for k in range(3):
                dcw_ref[m, gv, k:k + 1, :] += jnp.sum(tacc[gv, k], axis=0, keepdims=True)

        @pl.when(m == NG - 1)
        def _():
            dx, dgp = _rms_bwd_math(acc[...], x_ref[...], g_ref[...])
            dx = df_ref[...] + dx
            dx_ref[...] = dx
            dxb_ref[...] = dx.astype(BF16)

            @pl.when(first)
            def _():
                dg_ref[...] = dgp

            @pl.when(jnp.logical_not(first))
            def _():
                dg_ref[...] += dgp

    rev = lambda i: nt - 1 - i
    return pl.pallas_call(
        body, grid=(nt, NG), name=name,
        in_specs=[pl.BlockSpec((ts, D), lambda i, m: (rev(i), 0)),
                  pl.BlockSpec((None, 2, ts, FB), lambda i, m: (m, 0, rev(i), 0)),
                  pl.BlockSpec((None, 2, ts, FB), lambda i, m: (m, 0, rev(i), 0)),
                  pl.BlockSpec((2, None, D, FB), lambda i, m: (0, m, 0, 0)),
                  pl.BlockSpec((2, None, 3, FB), lambda i, m: (0, m, 0, 0)),
                  pl.BlockSpec((FB, D), lambda i, m: (m, 0)),
                  pl.BlockSpec((ts, D), lambda i, m: (rev(i), 0)),
                  pl.BlockSpec((1, D), lambda i, m: (0, 0))],
        out_specs=[pl.BlockSpec((None, ts, FB), lambda i, m: (m, rev(i), 0)),
                   pl.BlockSpec((None, 2, ts, FB), lambda i, m: (m, 0, rev(i), 0)),
                   pl.BlockSpec((ts, D), lambda i, m: (rev(i), 0)),
                   pl.BlockSpec((ts, D), lambda i, m: (rev(i), 0)),
                   pl.BlockSpec((1, D), lambda i, m: (0, 0)),
                   pl.BlockSpec((NG, 2, 3, FB), lambda i, m: (0, 0, 0, 0))],
        out_shape=[jax.ShapeDtypeStruct((NG, s, FB), BF16), jax.ShapeDtypeStruct((NG, 2, s, FB), BF16),
                   jax.ShapeDtypeStruct((s, D), F32), jax.ShapeDtypeStruct((s, D), BF16),
                   jax.ShapeDtypeStruct((1, D), F32),
                   jax.ShapeDtypeStruct((NG, 2, 3, FB), F32)],
        scratch_shapes=[pltpu.VMEM((NG, 2, 8, FB), F32), pltpu.VMEM((ts, D), F32),
                        pltpu.VMEM((2, 3, 8, FB), F32), pltpu.VMEM((2, ts + 8, FB), F32)],
        compiler_params=_params("arbitrary", "arbitrary"),
    )(df, up, upc, wup, fcw, wdn, xin, g)


def _mm_nt_rms(dy, wblk, x, g, dres, bf16_copy, name):
    s = dy.shape[0]
    nb, _, bn = wblk.shape
    pair = _pair(bn)
    ts = _tile(s, 512)

    def body(dy_ref, w_ref, x_ref, g_ref, dr_ref, dx_ref, *rest):
        dg_ref = rest[-1]
        i = pl.program_id(0)
        acc = jnp.zeros((ts, D), F32)
        for b in range(0, nb, pair):
            acc = acc + lax.dot_general(dy_ref[:, b * bn:(b + pair) * bn], _cols(w_ref, b, pair), NT_DIMS,
                                        preferred_element_type=F32)
        dx, dgp = _rms_bwd_math(acc, x_ref[...], g_ref[...])
        dx = dr_ref[...] + dx
        dx_ref[...] = dx
        if bf16_copy:
            rest[0][...] = dx.astype(BF16)

        @pl.when(i == 0)
        def _():
            dg_ref[...] = dgp

        @pl.when(i > 0)
        def _():
            dg_ref[...] += dgp

    tile = pl.BlockSpec((ts, D), lambda i: (i, 0))
    return pl.pallas_call(
        body, grid=(s // ts,), name=name,
        in_specs=[pl.BlockSpec((ts, nb * bn), lambda i: (i, 0)), pl.BlockSpec((nb, D, bn), lambda i: (0, 0, 0)),
                  tile, pl.BlockSpec((1, D), lambda i: (0, 0)), tile],
        out_specs=[tile] + ([tile] if bf16_copy else []) + [pl.BlockSpec((1, D), lambda i: (0, 0))],
        out_shape=[jax.ShapeDtypeStruct((s, D), F32)] + ([jax.ShapeDtypeStruct((s, D), BF16)] if bf16_copy else [])
        + [jax.ShapeDtypeStruct((1, D), F32)],
        compiler_params=_params("arbitrary"),
    )(dy, wblk, x, g, dres)


def _c_bwd(dx, w, zc, cw, name):
    s = dx.shape[0]
    ts = _tile(s, 512)
    nt = s // ts
    hb = ts // HALO

    def body(dx_ref, dxf_ref, w_ref, z_ref, ch_ref, xh_ref, bf_ref, cw_ref, dz_ref, dcw_ref):
        i = pl.program_id(0)
        cwv = [cw_ref[k:k + 1, :] for k in range(3)]
        dre = lax.dot_general(jnp.concatenate([dx_ref[...], dxf_ref[...]], axis=0), w_ref[...], NT_DIMS,
                              preferred_element_type=F32).astype(BF16).astype(F32)
        z_t = z_ref[...].astype(F32)
        bg, cg, xv = z_t[:, 0:D], z_t[:, D:2 * D], z_t[:, 2 * D:3 * D]
        ph = jnp.where(i > 0, ch_ref[...].astype(F32) * xh_ref[...].astype(F32), 0.0)
        pe = jnp.concatenate([ph, cg * xv], axis=0)
        q, taps = _conv3(pe, cwv, HALO)
        drv = dre[0:ts]
        dq = drv * bg
        dqf = jnp.where(i < nt - 1, dre[ts:ts + HALO] * bf_ref[...].astype(F32), 0.0)
        dp = _conv3_bwd_in(jnp.concatenate([dq, dqf], axis=0), cwv, ts)
        dz_ref[:, 0:D] = (drv * q).astype(BF16)
        dz_ref[:, D:2 * D] = (dp * xv).astype(BF16)
        dz_ref[:, 2 * D:3 * D] = (dp * cg).astype(BF16)
        rows = _conv3_bwd_w(dq, taps)

        @pl.when(i == 0)
        def _():
            for k in range(3):
                dcw_ref[k:k + 1, :] = rows[k]

        @pl.when(i > 0)
        def _():
            for k in range(3):
                dcw_ref[k:k + 1, :] += rows[k]

    past = lambda col: pl.BlockSpec((HALO, D), lambda i: (jnp.maximum(i * hb - 1, 0), col))
    nxt = lambda i: jnp.minimum((i + 1) * hb, s // HALO - 1)
    return pl.pallas_call(
        body, grid=(nt,), name=name,
        in_specs=[pl.BlockSpec((ts, D), lambda i: (i, 0)),
                  pl.BlockSpec((HALO, D), lambda i: (nxt(i), 0)),
                  pl.BlockSpec((D, D), lambda i: (0, 0)),
                  pl.BlockSpec((ts, 3 * D), lambda i: (i, 0)), past(1), past(2),
                  pl.BlockSpec((HALO, D), lambda i: (nxt(i), 0)),
                  pl.BlockSpec((3, D), lambda i: (0, 0))],
        out_specs=[pl.BlockSpec((ts, 3 * D), lambda i: (i, 0)), pl.BlockSpec((3, D), lambda i: (0, 0))],
        out_shape=[jax.ShapeDtypeStruct((s, 3 * D), BF16), jax.ShapeDtypeStruct((3, D), F32)],
        compiler_params=_params("arbitrary"),
    )(dx, dx, w, zc, zc, zc, zc, cw)


G512_ROWS = 40


def _ab_bwd(dx, w, z, yb2, lga, lba, wsm, bs_col, cwb, lgb, lbb, name):
    s = z.shape[0]
    ts = _tile(s, 256)
    nt = s // ts
    hb = ts // HALO_B
    nch = ts // CHUNK

    def body(z_ref, zh_ref, dx_ref, dxf_ref, w_ref, yb2_ref, yb2f_ref, lga_ref, lba_ref, ws_ref, bs_ref,
             cw_ref, lgb_ref, lbb_ref, dz_ref, g512_ref, dws_ref, dbs_ref, dvn_ref, fwd_rolled, bwd_rolled, du_s):
        i = pl.program_id(0)
        last = i == nt - 1

        @pl.when(i == 0)
        def _():
            g512_ref[...] = jnp.zeros((G512_ROWS, DA), F32)
            dws_ref[...] = jnp.zeros((HEADS, CHUNK, CHUNK), F32)
            dbs_ref[...] = jnp.zeros((HEADS, CHUNK, 1), F32)

        def add_row(k, v):
            g512_ref[k:k + 1, :] += v

        z_t = z_ref[...].astype(F32)
        nt_dot = lambda a, b: lax.dot_general(a, b, NT_DIMS, preferred_element_type=F32).astype(BF16).astype(F32)
        dy_t = nt_dot(dx_ref[...], w_ref[...])
        dyf = nt_dot(dxf_ref[...], w_ref[DA:2 * DA, :])
        ua, va = z_t[:, 0:DA], z_t[:, DA:2 * DA]
        gu = _gelu(ua)
        gv = _gelu(va)
        lga_v = lga_ref[...]
        vn, xhat_a, rstd_a = _ln_fwd(gv, lga_v, lba_ref[...])
        vnb = vn.astype(BF16)
        causal = (lax.broadcasted_iota(jnp.int32, (CHUNK, CHUNK), 0)
                  >= lax.broadcasted_iota(jnp.int32, (CHUNK, CHUNK), 1)).astype(F32)
        for c in range(nch):
            for h in range(HEADS):
                rs = slice(c * CHUNK, (c + 1) * CHUNK)
                cs = slice(h * CHUNK, (h + 1) * CHUNK)
                vblk = vnb[rs, cs]
                mixed = jnp.dot(ws_ref[h], vblk, preferred_element_type=F32) + bs_ref[h]
                dyb_ = dy_t[rs, cs]
                dmix = dyb_ * gu[rs, cs]
                dmb = dmix.astype(BF16)
                dz_ref[rs, cs] = (dyb_ * mixed * _dgelu(ua[rs, cs])).astype(BF16)
                dvn_ref[rs, cs] = lax.dot_general(ws_ref[h], dmb, TN_DIMS, preferred_element_type=F32)
                dws_ref[h] += causal * lax.dot_general(dmb, vblk, NT_DIMS, preferred_element_type=F32)
                dbs_ref[h] += jnp.sum(dmix, axis=1, keepdims=True)
        dvn = dvn_ref[...]
        add_row(0, jnp.sum(dvn * xhat_a, axis=0, keepdims=True))
        add_row(1, jnp.sum(dvn, axis=0, keepdims=True))
        dgv = _ln_bwd(dvn, xhat_a, rstd_a, lga_v)
        dz_ref[:, DA:2 * DA] = (dgv * _dgelu(va)).astype(BF16)
        lgb_v = lgb_ref[...]
        dyb_e = jnp.concatenate(
            [dy_t[:, DA:2 * DA], jnp.where(last, 0.0, dyf)], axis=0)
        yb2_e = jnp.concatenate([yb2_ref[...], jnp.where(last, 0.0, yb2f_ref[...])], axis=0)
        n_e, xhat_b, rstd_b = _ln_fwd(yb2_e, lgb_v, lbb_ref[...])
        sgn = _sigmoid(n_e)
        dn = dyb_e * _dsilu(n_e, sgn)
        dy2 = _ln_bwd(dn, xhat_b, rstd_b, lgb_v)
        add_row(2, jnp.sum(dy2[:ts], axis=0, keepdims=True))
        add_row(3, jnp.sum(dn[:ts] * xhat_b[:ts], axis=0, keepdims=True))
        add_row(4, jnp.sum(dn[:ts], axis=0, keepdims=True))
        zh = jnp.where(i > 0, zh_ref[...], jnp.zeros_like(zh_ref[...])).astype(F32)
        xb_t, gb_t = z_t[:, 2 * DA:3 * DA], z_t[:, 3 * DA:4 * DA]
        sgb = _sigmoid(gb_t)
        _rolled_copies(fwd_rolled, jnp.concatenate(
            [zh[:, 0:DA] * _sigmoid(zh[:, DA:2 * DA]), xb_t * sgb], axis=0), False)
        _rolled_copies(bwd_rolled, dy2, True)
        for o in range(0, ts, CONV_ROWS):
            acc = jnp.zeros((CONV_ROWS, DA), F32)
            for sh in range(BCONV):
                q, r = divmod(sh, 8)
                acc = acc + cw_ref[BCONV - 1 - sh:BCONV - sh, :] * bwd_rolled[r, 8 * q + o:8 * q + o + CONV_ROWS, :]
            du_s[o:o + CONV_ROWS, :] = acc
        for sh in range(BCONV):
            q, r = divmod(sh, 8)
            acc = jnp.zeros((CONV_ROWS, DA), F32)
            for o in range(0, ts, CONV_ROWS):
                lo = HALO_B - 8 * q + o
                acc = acc + bwd_rolled[0, o:o + CONV_ROWS, :] * fwd_rolled[r, lo:lo + CONV_ROWS, :]
            add_row(8 + BCONV - 1 - sh, jnp.sum(acc, axis=0, keepdims=True))
        du = du_s[...]
        dz_ref[:, 2 * DA:3 * DA] = (du * sgb).astype(BF16)
        dz_ref[:, 3 * DA:4 * DA] = (du * xb_t * sgb * (1.0 - sgb)).astype(BF16)

    row = lambda i: (0, 0)
    nxt = lambda i: jnp.minimum((i + 1) * hb, s // HALO_B - 1)
    return pl.pallas_call(
        body, grid=(nt,), name=name,
        in_specs=[pl.BlockSpec((ts, 4 * DA), lambda i: (i, 0)),
                  pl.BlockSpec((HALO_B, 2 * DA), lambda i: (jnp.maximum(i * hb - 1, 0), 1)),
                  pl.BlockSpec((ts, D), lambda i: (i, 0)),
                  pl.BlockSpec((HALO_B, D), lambda i: (nxt(i), 0)),
                  pl.BlockSpec((D, D), lambda i: (0, 0)),
                  pl.BlockSpec((ts, DA), lambda i: (i, 0)),
                  pl.BlockSpec((HALO_B, DA), lambda i: (nxt(i), 0)),
                  pl.BlockSpec((1, DA), row), pl.BlockSpec((1, DA), row),
                  pl.BlockSpec((HEADS, CHUNK, CHUNK), lambda i: (0, 0, 0)),
                  pl.BlockSpec((HEADS, CHUNK, 1), lambda i: (0, 0, 0)),
                  pl.BlockSpec((BCONV, DA), row), pl.BlockSpec((1, DA), row), pl.BlockSpec((1, DA), row)],
        out_specs=[pl.BlockSpec((ts, 4 * DA), lambda i: (i, 0)),
                   pl.BlockSpec((G512_ROWS, DA), row),
                   pl.BlockSpec((HEADS, CHUNK, CHUNK), lambda i: (0, 0, 0)),
                   pl.BlockSpec((HEADS, CHUNK, 1), lambda i: (0, 0, 0))],
        out_shape=[jax.ShapeDtypeStruct((s, 4 * DA), BF16), jax.ShapeDtypeStruct((G512_ROWS, DA), F32),
                   jax.ShapeDtypeStruct((HEADS, CHUNK, CHUNK), F32),
                   jax.ShapeDtypeStruct((HEADS, CHUNK, 1), F32)],
        scratch_shapes=[pltpu.VMEM((ts, DA), F32), pltpu.VMEM((8, ts + HALO_B, DA), F32),
                        pltpu.VMEM((8, ts + HALO_B, DA), F32), pltpu.VMEM((ts, DA), F32)],
        compiler_params=_params("arbitrary"),
    )(z, z, dx, dx, w, yb2, yb2, lga, lba, wsm, bs_col, cwb, lgb, lbb)


def _dw_cols(a, dy, nb, bn, name):
    s = a.shape[0]
    tm = _tile(s, 2048)
    nt = s // tm
    cpb = 4

    def body(a_ref, dy_ref, o_ref, acc):
        t = pl.program_id(1)
        p = lax.dot_general(a_ref[...], dy_ref[...], TN_DIMS, preferred_element_type=F32)

        @pl.when(t == 0)
        def _():
            for q in range(cpb):
                acc[q] = p[:, q * bn:(q + 1) * bn]

        @pl.when(t > 0)
        def _():
            for q in range(cpb):
                acc[q] += p[:, q * bn:(q + 1) * bn]

        @pl.when(t == nt - 1)
        def _():
            o_ref[...] = acc[...].astype(BF16)

    return pl.pallas_call(
        body, grid=(nb // cpb, nt), name=name,
        in_specs=[pl.BlockSpec((tm, D), lambda j, t: (t, 0)), pl.BlockSpec((tm, cpb * bn), lambda j, t: (t, j))],
        out_specs=pl.BlockSpec((cpb, D, bn), lambda j, t: (j, 0, 0)),
        out_shape=jax.ShapeDtypeStruct((nb, D, bn), BF16),
        scratch_shapes=[pltpu.VMEM((cpb, D, bn), F32)],
        compiler_params=_params("arbitrary", "arbitrary"),
    )(a, dy)


def _dw_rows(a, dy, name):
    s = a.shape[0]
    tm = _tile(s, 4096)
    nt = s // tm
    rb = 512

    def body(a_ref, dy_ref, o_ref, acc):
        t = pl.program_id(1)
        p = lax.dot_general(a_ref[...], dy_ref[...], TN_DIMS, preferred_element_type=F32)

        @pl.when(t == 0)
        def _():
            acc[...] = p

        @pl.when(t > 0)
        def _():
            acc[...] += p

        @pl.when(t == nt - 1)
        def _():
            o_ref[...] = acc[...].astype(BF16)

    return pl.pallas_call(
        body, grid=(D // rb, nt), name=name,
        in_specs=[pl.BlockSpec((tm, rb), lambda j, t: (t, j)), pl.BlockSpec((tm, D), lambda j, t: (t, 0))],
        out_specs=pl.BlockSpec((rb, D), lambda j, t: (j, 0)),
        out_shape=jax.ShapeDtypeStruct((D, D), BF16),
        scratch_shapes=[pltpu.VMEM((rb, D), F32)],
        compiler_params=_params("arbitrary", "arbitrary"),
    )(a, dy)


def _dw_up(h, dup, name):
    s = h.shape[0]
    tm = _tile(s, 4096)
    nt = s // tm

    def body(h_ref, d_ref, o_ref, acc):
        t = pl.program_id(1)
        p = lax.dot_general(d_ref[...], h_ref[...], TN_DIMS, preferred_element_type=F32)

        @pl.when(t == 0)
        def _():
            acc[...] = p

        @pl.when(t > 0)
        def _():
            acc[...] += p

        @pl.when(t == nt - 1)
        def _():
            o_ref[...] = acc[...].astype(BF16)

    return pl.pallas_call(
        body, grid=(NDEV, nt), name=name,
        in_specs=[pl.BlockSpec((tm, D), lambda b, t: (t, 0)),
                  pl.BlockSpec((None, None, tm, FB), lambda b, t: (b % NG, b // NG, t, 0))],
        out_specs=pl.BlockSpec((None, FB, D), lambda b, t: (b, 0, 0)),
        out_shape=jax.ShapeDtypeStruct((NDEV, FB, D), BF16),
        scratch_shapes=[pltpu.VMEM((FB, D), F32)],
        compiler_params=_params("arbitrary", "arbitrary"),
    )(h, dup)


def _dw_dn(a, df, name):
    s = df.shape[0]
    tm = _tile(s, 4096)
    nt = s // tm

    def body(a_ref, d_ref, o_ref, acc):
        t = pl.program_id(1)
        p = lax.dot_general(a_ref[...], d_ref[...], TN_DIMS, preferred_element_type=F32)

        @pl.when(t == 0)
        def _():
            acc[...] = p

        @pl.when(t > 0)
        def _():
            acc[...] += p

        @pl.when(t == nt - 1)
        def _():
            o_ref[...] = acc[...].astype(BF16)

    return pl.pallas_call(
        body, grid=(NG, nt), name=name,
        in_specs=[pl.BlockSpec((None, tm, FB), lambda m, t: (m, t, 0)), pl.BlockSpec((tm, D), lambda m, t: (t, 0))],
        out_specs=pl.BlockSpec((FB, D), lambda m, t: (m, 0)),
        out_shape=jax.ShapeDtypeStruct((DFF, D), BF16),
        scratch_shapes=[pltpu.VMEM((FB, D), F32)],
        compiler_params=_params("arbitrary", "arbitrary"),
    )(a, df)


def _place():
    x, y, c = lax.axis_index("x"), lax.axis_index("y"), lax.axis_index("c")
    chips = [(1 - x, y), (x, 1 - y), (1 - x, 1 - y)]
    return x, y, c, chips


def _zone(shard, dev):
    return lax.dynamic_update_slice(lax.empty((NDEV,) + shard.shape, shard.dtype), shard[None],
                                    (dev,) + (0,) * shard.ndim)


HBM_SPEC = pl.BlockSpec(memory_space=pltpu.HBM)
SEM_SPEC = pl.BlockSpec(memory_space=pltpu.SEMAPHORE)
DATAFLOW = pltpu.SideEffectType.DATAFLOW_SIDE_EFFECTING


def _hbm(a):
    return pltpu.with_memory_space_constraint(a, pltpu.HBM)


def _hbm_like(arrs):
    return [pltpu.HBM(a.shape, a.dtype) for a in arrs]


def _ag_start(srcs, lands, after, name):
    n = len(srcs)
    ns = 8 * n

    def body(*refs):
        src, land = refs[:n], refs[n:2 * n]
        sems = refs[2 * n + 1:2 * n + 1 + ns]
        token = refs[-1]
        x, y, c, chips = _place()
        peers = [(x, y, 1 - c)] + [(*chip, c) for chip in chips]
        for t in range(n):
            for k, to in enumerate(peers):
                pltpu.make_async_remote_copy(
                    src_ref=src[t], dst_ref=land[t].at[4 * x + 2 * y + c],
                    send_sem=sems[2 * (4 * t + k)], recv_sem=sems[2 * (4 * t + k) + 1],
                    device_id=to, device_id_type=MESH).start()
        token[...] = jnp.zeros_like(token)

    res = pl.pallas_call(
        body, name=name,
        in_specs=[HBM_SPEC] * (2 * n) + [ANY],
        out_specs=[SEM_SPEC] * ns + [HBM_SPEC] * (2 * n) + [pl.BlockSpec(memory_space=pltpu.VMEM)],
        out_shape=[pltpu.SemaphoreType.DMA(())] * ns + _hbm_like(srcs) + _hbm_like(lands)
        + [jax.ShapeDtypeStruct((8, 128), F32)],
        input_output_aliases={i: ns + i for i in range(2 * n)},
        compiler_params=pltpu.CompilerParams(has_side_effects=DATAFLOW),
    )(*[_hbm(a) for a in srcs], *[_hbm(a) for a in lands], after)
    sems = [[(res[2 * (4 * t + k)], res[2 * (4 * t + k) + 1]) for k in range(4)] for t in range(n)]
    return sems, res[ns:ns + n], res[ns + n:ns + 2 * n], res[-1]


def _ag_forward(srcs, lands, sems1, after, name):
    n = len(srcs)
    flat1 = [s for t in range(n) for k in range(1, 4) for s in sems1[t][k]]
    n1 = len(flat1)

    def body(*refs):
        src, land = refs[:n], refs[n:2 * n]
        s1 = refs[2 * n:2 * n + n1]
        s2 = refs[2 * n + n1 + 1:2 * n + n1 + 1 + 6 * n]
        x, y, c, chips = _place()
        for j, (cx, cy) in enumerate(chips):
            for t in range(n):
                blk = land[t].at[4 * cx + 2 * cy + c]
                pltpu.make_async_remote_copy(
                    src_ref=src[t], dst_ref=blk, send_sem=s1[2 * (3 * t + j)], recv_sem=s1[2 * (3 * t + j) + 1],
                    device_id=(cx, cy, c), device_id_type=MESH).wait_recv()
                pltpu.make_async_remote_copy(
                    src_ref=blk, dst_ref=blk, send_sem=s2[2 * (3 * t + j)], recv_sem=s2[2 * (3 * t + j) + 1],
                    device_id=(x, y, 1 - c), device_id_type=MESH).start()

    res = pl.pallas_call(
        body, name=name,
        in_specs=[HBM_SPEC] * (2 * n) + [SEM_SPEC] * n1 + [ANY],
        out_specs=[SEM_SPEC] * (6 * n) + [HBM_SPEC] * n,
        out_shape=[pltpu.SemaphoreType.DMA(())] * (6 * n) + _hbm_like(lands),
        input_output_aliases={n + i: 6 * n + i for i in range(n)},
        compiler_params=pltpu.CompilerParams(has_side_effects=DATAFLOW),
    )(*srcs, *lands, *flat1, after)
    sems2 = [[(res[2 * (3 * t + j)], res[2 * (3 * t + j) + 1]) for j in range(3)] for t in range(n)]
    return sems2, res[6 * n:]


def _ag_finish(srcs, lands, sems1, sems2, after, name):
    n = len(srcs)
    flat1 = [s for t in range(n) for k in range(4) for s in sems1[t][k]]
    flat2 = [s for t in range(n) for j in range(3) for s in sems2[t][j]]
    n1, n2 = len(flat1), len(flat2)

    def body(*refs):
        src, land = refs[:n], refs[n:2 * n]
        s1 = refs[2 * n:2 * n + n1]
        s2 = refs[2 * n + n1:2 * n + n1 + n2]
        x, y, c, chips = _place()
        sib = (x, y, 1 - c)
        for t in range(n):
            own = land[t].at[4 * x + 2 * y + 1 - c]
            pltpu.make_async_remote_copy(
                src_ref=src[t], dst_ref=own, send_sem=s1[8 * t], recv_sem=s1[8 * t + 1],
                device_id=sib, device_id_type=MESH).wait_recv()
            for k in range(4):
                pltpu.make_async_remote_copy(
                    src_ref=src[t], dst_ref=own, send_sem=s1[2 * (4 * t + k)], recv_sem=s1[2 * (4 * t + k) + 1],
                    device_id=sib, device_id_type=MESH).wait_send()
            for j, (cx, cy) in enumerate(chips):
                blk = land[t].at[4 * cx + 2 * cy + 1 - c]
                cp = pltpu.make_async_remote_copy(
                    src_ref=blk, dst_ref=blk, send_sem=s2[2 * (3 * t + j)], recv_sem=s2[2 * (3 * t + j) + 1],
                    device_id=sib, device_id_type=MESH)
                cp.wait_send()
                cp.wait_recv()

    return pl.pallas_call(
        body, name=name,
        in_specs=[HBM_SPEC] * (2 * n) + [SEM_SPEC] * (n1 + n2) + [ANY],
        out_specs=[HBM_SPEC] * n,
        out_shape=_hbm_like(lands),
        input_output_aliases={n + i: i for i in range(n)},
        compiler_params=pltpu.CompilerParams(has_side_effects=DATAFLOW),
    )(*srcs, *lands, *flat1, *flat2, after)


def _pair_copies(srcs, dsts, sems):
    x, y, c, _ = _place()
    nt = len(srcs)
    return [pltpu.make_async_remote_copy(
        src_ref=srcs[t].at[2 * j + 1 - c], dst_ref=dsts[t].at[j],
        send_sem=sems[2 * (NCHIP * t + j)], recv_sem=sems[2 * (NCHIP * t + j) + 1],
        device_id=(x, y, 1 - c), device_id_type=MESH) for t in range(nt) for j in range(NCHIP)]


def _pair_start(grads, carry, name):
    nt = len(grads)
    ns = 2 * NCHIP * nt
    zones = [_hbm(lax.empty((NCHIP,) + a.shape[1:], a.dtype)) for a in grads]
    extra = [] if carry is None else [_hbm(carry)]
    ne = len(extra)

    def body(*refs):
        for cp in _pair_copies(refs[:nt], refs[nt:2 * nt], refs[2 * nt + ne:2 * nt + ne + ns]):
            cp.start()

    res = pl.pallas_call(
        body, name=name,
        in_specs=[HBM_SPEC] * (2 * nt + ne),
        out_specs=[SEM_SPEC] * ns + [HBM_SPEC] * (2 * nt + ne),
        out_shape=[pltpu.SemaphoreType.DMA(())] * ns + _hbm_like(grads) + _hbm_like(zones) + _hbm_like(extra),
        input_output_aliases={i: ns + i for i in range(2 * nt + ne)},
        compiler_params=pltpu.CompilerParams(has_side_effects=DATAFLOW),
    )(*[_hbm(a) for a in grads], *zones, *extra)
    handle = (list(res[:ns]), list(res[ns:ns + nt]), list(res[ns + nt:ns + 2 * nt]))
    return handle, (res[ns + 2 * nt] if ne else None)


def _pair_wait(handle, after, name):
    sems, srcs, zones = handle
    nt, ns = len(srcs), len(sems)

    def body(*refs):
        for cp in _pair_copies(refs[:nt], refs[nt:2 * nt], refs[2 * nt:2 * nt + ns]):
            cp.wait_send()
            cp.wait_recv()

    return pl.pallas_call(
        body, name=name,
        in_specs=[HBM_SPEC] * (2 * nt) + [SEM_SPEC] * ns + [ANY],
        out_specs=[HBM_SPEC] * nt,
        out_shape=_hbm_like(zones),
        input_output_aliases={nt + i: i for i in range(nt)},
        compiler_params=pltpu.CompilerParams(has_side_effects=DATAFLOW),
    )(*srcs, *zones, *sems, after)


def _rows_tile(r, row_bytes, cap_bytes):
    best = None
    for tr in range(16, r + 1, 16):
        if r % tr == 0 and tr * row_bytes <= cap_bytes:
            best = tr
    return best if best is not None else r


def _pair_sum(own, got, cidx, name):
    _, _, r, cdim = own.shape
    tr = _rows_tile(r, 2 * cdim, 2 * 1024 * 1024)

    def body(c_ref, a_ref, b_ref, o_ref):
        o_ref[...] = (a_ref[...].astype(F32) + b_ref[...].astype(F32)).astype(BF16)

    return pl.pallas_call(
        body, name=name,
        grid_spec=pltpu.PrefetchScalarGridSpec(
            num_scalar_prefetch=1, grid=(NCHIP, r // tr),
            in_specs=[pl.BlockSpec((None, None, tr, cdim), lambda j, i, c_ref: (j, c_ref[0], i, 0)),
                      pl.BlockSpec((None, tr, cdim), lambda j, i, c_ref: (j, i, 0))],
            out_specs=pl.BlockSpec((None, tr, cdim), lambda j, i, c_ref: (j, i, 0))),
        out_shape=jax.ShapeDtypeStruct((NCHIP, r, cdim), BF16),
        compiler_params=_params("arbitrary", "arbitrary"),
    )(cidx, own, got)


def _chip_copies(srcs, zones, slots, sems):
    x, y, c, chips = _place()
    out = []
    for t, (z, l) in enumerate(slots):
        for k, (cx, cy) in enumerate(chips):
            dst = zones[z].at[k] if l is None else zones[z].at[k, l]
            out.append(pltpu.make_async_remote_copy(
                src_ref=srcs[t].at[2 * cx + cy], dst_ref=dst,
                send_sem=sems[2 * (3 * t + k)], recv_sem=sems[2 * (3 * t + k) + 1],
                device_id=(cx, cy, c), device_id_type=MESH))
    return out


def _chip_start(sums, zones, slots, carry, name):
    nt, nz = len(sums), len(zones)
    ns = 6 * nt
    extra = [] if carry is None else [_hbm(carry)]
    ne = len(extra)

    def body(*refs):
        for cp in _chip_copies(refs[:nt], refs[nt:nt + nz], slots, refs[nt + nz + ne:nt + nz + ne + ns]):
            cp.start()

    res = pl.pallas_call(
        body, name=name,
        in_specs=[HBM_SPEC] * (nt + nz + ne),
        out_specs=[SEM_SPEC] * ns + [HBM_SPEC] * (nt + nz + ne),
        out_shape=[pltpu.SemaphoreType.DMA(())] * ns + _hbm_like(sums) + _hbm_like(zones) + _hbm_like(extra),
        input_output_aliases={i: ns + i for i in range(nt + nz + ne)},
        compiler_params=pltpu.CompilerParams(has_side_effects=DATAFLOW),
    )(*[_hbm(a) for a in sums], *zones, *extra)
    return (list(res[:ns]), list(res[ns:ns + nt]), list(res[ns + nt:ns + nt + nz]),
            (res[ns + nt + nz] if ne else None))


def _chip_wait(started, zones, zone_ids, after, name):
    started = [(sums, [(zone_ids.index(z), l) for z, l in slots], sems) for sums, slots, sems in started]
    nz = len(zones)
    flat_src = [a for sums, _, _ in started for a in sums]
    flat_sem = [s for _, _, sems in started for s in sems]
    n_src, n_sem = len(flat_src), len(flat_sem)

    def body(*refs):
        srcs, zs, sems = refs[:n_src], refs[n_src:n_src + nz], refs[n_src + nz:n_src + nz + n_sem]
        so, se = 0, 0
        for sums, slots, sem_list in started:
            for cp in _chip_copies(srcs[so:so + len(sums)], zs, slots, sems[se:se + len(sem_list)]):
                cp.wait_send()
                cp.wait_recv()
            so += len(sums)
            se += len(sem_list)

    return pl.pallas_call(
        body, name=name,
        in_specs=[HBM_SPEC] * (n_src + nz) + [SEM_SPEC] * n_sem + [ANY],
        out_specs=[HBM_SPEC] * nz,
        out_shape=_hbm_like(zones),
        input_output_aliases={n_src + i: i for i in range(nz)},
        compiler_params=pltpu.CompilerParams(has_side_effects=DATAFLOW),
    )(*flat_src, *zones, *flat_sem, after)


def _small_allreduce(parts, y_first, after, name):
    nt = len(parts)

    def body(*refs):
        srcs, outs, bufs = refs[:nt], refs[nt + 1:2 * nt + 1], refs[2 * nt + 1:3 * nt + 1]
        send_sems, recv_sems = refs[3 * nt + 1:]
        x, y, c, _ = _place()
        along = {"c": (x, y, 1 - c), "x": (1 - x, y, c), "y": (x, 1 - y, c)}
        for t in range(nt):
            outs[t][...] = srcs[t][...]
        for step in range(3):
            order = [("c", "y", "x") if t in y_first else ("c", "x", "y") for t in range(nt)]
            copies = [pltpu.make_async_remote_copy(
                src_ref=outs[t], dst_ref=bufs[t].at[step],
                send_sem=send_sems.at[step, t], recv_sem=recv_sems.at[step, t],
                device_id=along[order[t][step]], device_id_type=MESH) for t in range(nt)]
            for cp in copies:
                cp.start()
            for cp in copies:
                cp.wait()
            for t in range(nt):
                outs[t][...] = outs[t][...] + bufs[t][step]

    vm = pl.BlockSpec(memory_space=pltpu.VMEM)
    return pl.pallas_call(
        body, name=name,
        in_specs=[vm] * nt + [ANY], out_specs=[vm] * nt,
        out_shape=[jax.ShapeDtypeStruct(a.shape, F32) for a in parts],
        scratch_shapes=[pltpu.VMEM((3,) + a.shape, F32) for a in parts]
        + [pltpu.SemaphoreType.DMA((3, nt)), pltpu.SemaphoreType.DMA((3, nt))],
        compiler_params=pltpu.CompilerParams(has_side_effects=True, vmem_limit_bytes=VMEM_LIMIT),
    )(*parts, after)


def _adam_math(w, g, m, v):
    m2 = ADAM_B1 * m + (1.0 - ADAM_B1) * g
    v2 = ADAM_B2 * v + (1.0 - ADAM_B2) * (g * g)
    m_hat = m2 / (1.0 - ADAM_B1 ** ADAM_STEP)
    v_hat = v2 / (1.0 - ADAM_B2 ** ADAM_STEP)
    delta = -ADAM_LR * (m_hat / (jnp.sqrt(v_hat) + ADAM_EPS) + ADAM_WD * w)
    return delta, m2, v2


def _adam_big(w, m, v, parts, mine, chip, name):
    nl, r, cdim = w.shape
    tr = _rows_tile(r, 4 * cdim, 3 * 512 * 1024)

    def body(c_ref, w_ref, m_ref, v_ref, p_ref, *rest):
        mine_refs, (g_ref, d_ref, mo_ref, vo_ref) = rest[:nl], rest[nl:]
        own = mine_refs[0][...]
        for l in range(1, nl):
            own = jnp.where(pl.program_id(0) == l, mine_refs[l][...], own)
        g = ((p_ref[0].astype(F32) + p_ref[1].astype(F32)) + p_ref[2].astype(F32)) + own.astype(F32)
        delta, m2, v2 = _adam_math(w_ref[...], g, m_ref[...], v_ref[...])
        g_ref[...] = g
        d_ref[...] = delta
        mo_ref[...] = m2
        vo_ref[...] = v2

    spec = pl.BlockSpec((None, tr, cdim), lambda l, i, c_ref: (l, i, 0))
    mine_specs = [pl.BlockSpec((None, tr, cdim), lambda l, i, c_ref, ll=ll: (c_ref[0], jnp.where(l == ll, i, 0), 0))
                  for ll in range(nl)]
    return pl.pallas_call(
        body, name=name,
        grid_spec=pltpu.PrefetchScalarGridSpec(
            num_scalar_prefetch=1, grid=(nl, r // tr),
            in_specs=[spec, spec, spec, pl.BlockSpec((3, None, tr, cdim), lambda l, i, c_ref: (0, l, i, 0))]
            + mine_specs,
            out_specs=[spec] * 4),
        out_shape=[jax.ShapeDtypeStruct(w.shape, F32)] * 4,
        compiler_params=_params("arbitrary", "arbitrary"),
    )(chip, w, m, v, parts, *mine)


def _adam_small(ws, gs, ms, vs, name):
    n = len(ws)

    def body(*refs):
        w_r, g_r, m_r, v_r = refs[:n], refs[n:2 * n], refs[2 * n:3 * n], refs[3 * n:4 * n]
        d_o, m_o, v_o = refs[4 * n:5 * n], refs[5 * n:6 * n], refs[6 * n:7 * n]
        for t in range(n):
            delta, m2, v2 = _adam_math(w_r[t][...], g_r[t][...], m_r[t][...], v_r[t][...])
            d_o[t][...] = delta
            m_o[t][...] = m2
            v_o[t][...] = v2

    vm = pl.BlockSpec(memory_space=pltpu.VMEM)
    shapes = [jax.ShapeDtypeStruct(a.shape, F32) for a in ws]
    return pl.pallas_call(
        body, name=name, in_specs=[vm] * (4 * n), out_specs=[vm] * (3 * n), out_shape=shapes * 3,
        compiler_params=pltpu.CompilerParams(vmem_limit_bytes=VMEM_LIMIT),
    )(*ws, *gs, *ms, *vs)


def kernel(x, norm_mix, norm_ffn, norm_final, ab_w_in, a_ln_g, a_ln_b, a_w_s, a_b_s, b_conv_w, b_conv_b, b_ln_g, b_ln_b, ab_w_out, c_w_in, c_conv_w, c_w_out, f_w_up, f_conv_w, f_w_down, loss_target, m_norm_mix, m_norm_ffn, m_norm_final, m_ab_w_in, m_a_ln_g, m_a_ln_b, m_a_w_s, m_a_b_s, m_b_conv_w, m_b_conv_b, m_b_ln_g, m_b_ln_b, m_ab_w_out, m_c_w_in, m_c_conv_w, m_c_w_out, m_f_w_up, m_f_conv_w, m_f_w_down, v_norm_mix, v_norm_ffn, v_norm_final, v_ab_w_in, v_a_ln_g, v_a_ln_b, v_a_w_s, v_a_b_s, v_b_conv_w, v_b_conv_b, v_b_ln_g, v_b_ln_b, v_ab_w_out, v_c_w_in, v_c_conv_w, v_c_w_out, v_f_w_up, v_f_conv_w, v_f_w_down):
    s = x.shape[1]
    x0 = x.reshape(s, D)
    tgt = loss_target.reshape(s, D)
    xi, yi, ci = lax.axis_index("x"), lax.axis_index("y"), lax.axis_index("c")
    dev = 4 * xi + 2 * yi + ci
    cidx = ci.astype(jnp.int32).reshape(1)

    bf = lambda a: a.astype(BF16)
    slab_w = 6 * CHUNK
    pad = lambda a, rows: jnp.pad(a, ((0, rows - a.shape[0]), (0, slab_w - a.shape[1])))
    slab = jnp.concatenate([pad(b_conv_w[0], 32), pad(c_conv_w[0], 8), pad(f_conv_w.reshape(6, FB), 8)], axis=0)
    later = [bf(ab_w_in[0]), bf(ab_w_out[0]), slab, bf(f_w_up[0]), bf(f_w_down[0]), bf(c_w_in[0]), bf(c_w_out[0]),
             bf(f_w_up[1]), bf(f_w_down[1])]
    lands = [_zone(a, dev) for a in later]
    groups = [[0], [1, 2], [3, 4], [5, 6], [7, 8]]
    ag_sems, later, lands, ag_token = _ag_start(later, lands, x0, "ag_start")

    causal = jnp.tril(jnp.ones((CHUNK, CHUNK), F32))
    wsm = (a_w_s[0] * causal).astype(BF16)
    bs_col = a_b_s.reshape(HEADS, CHUNK, 1)
    nm = [norm_mix[0:1], norm_mix[1:2]]
    nf = [norm_ffn[0:1], norm_ffn[1:2]]
    nfin = norm_final.reshape(1, D)

    def pass_on(ts_, after_ici, tag):
        srcs = [later[t] for t in ts_]
        sems1 = [ag_sems[t] for t in ts_]
        sems2, zone = _ag_forward(srcs, [lands[t] for t in ts_], sems1, after_ici, "ag_forward_" + tag)
        return srcs, zone, sems1, sems2

    def arrive(g, after_ici, after_d2d, tag):
        srcs, zone, sems1, sems2 = pass_on(groups[g], after_ici, tag)
        return _ag_finish(srcs, zone, sems1, sems2, after_d2d, "ag_finish_" + tag)

    h0 = _rms_fwd(x0, nm[0], "rms_mix0", after=ag_token)
    (win0,) = arrive(0, h0, h0, "w_in")
    z = _mm_in(h0, win0, "mm_ab_in")
    wout0, slab_g = arrive(1, z, z, "first")
    wout0 = wout0.reshape(D, D)
    bcw = jnp.transpose(slab_g[:, 0:BCONV, 0:DA // NDEV], (1, 0, 2)).reshape(BCONV, DA)
    ccw = jnp.transpose(slab_g[:, 32:35, 0:D // NDEV], (1, 0, 2)).reshape(3, D)
    fcw_g = slab_g[:, 40:46, 0:FB].reshape(2, NG, 2, 3, FB)
    fcws = [fcw_g[:, :, 0], fcw_g[:, :, 1]]
    ycat, yb2 = _ab_fwd(z, a_ln_g, a_ln_b, wsm, bs_col, bcw, b_conv_b, b_ln_g, b_ln_b, "ab_fwd")
    up_part = pass_on(groups[2][:1], ycat, "ffn0_up")
    x1, h1 = _mm_out(ycat, wout0, x0, nf[0], "mm_ab_out")
    dn_part = pass_on(groups[2][1:], x1, "ffn0_down")
    wup0, wdn0 = _ag_finish(*[a + b for a, b in zip(up_part, dn_part)], x1, "ag_finish_ffn0")
    up0, upc0, x2, h2 = _ffn_fwd(h1, x1, wup0.reshape(2, NG, D, FB), fcws[0], wdn0.reshape(DFF, D), nm[1],
                                 "ffn_fwd0")
    cin, cout = arrive(3, x2, x2, "c")
    cout = cout.reshape(D, D)
    zc = _mm_in(h2, cin, "mm_c_in")
    last_part = pass_on(groups[4], zc, "ffn1")
    rc, x3, h3 = _c_fwd(zc, ccw, cout, x2, nf[1], "c_fwd_out")
    wup1, wdn1 = _ag_finish(*last_part, x3, "ag_finish_ffn1")
    wups = [wup0.reshape(2, NG, D, FB), wup1.reshape(2, NG, D, FB)]
    wdns = [wdn0.reshape(DFF, D), wdn1.reshape(DFF, D)]
    up1, upc1, dx4, dx4b, dnfin, loss_part = _ffn_fwd(h3, x3, wups[1], fcws[1], wdns[1], None, "ffn_fwd1_loss",
                                                      final=(tgt, nfin))

    zshape = lambda *sh: _hbm(lax.empty((3,) + sh, BF16))
    zones = [zshape(D, 2 * D // NDEV), zshape(D // NDEV, D), zshape(D, 3 * D // NDEV), zshape(D // NDEV, D),
             zshape(2, FB, D), zshape(2, DFF // NDEV, D)]
    started = []

    def pair_sums(grads, handle, after, tag):
        del grads
        got = _pair_wait(handle, after, "rs_pair_wait_" + tag)
        return [_pair_sum(b.reshape((NCHIP, 2) + b.shape[1:]), g, cidx, "rs_pair_sum_%s%d" % (tag, t))
                for t, (b, g) in enumerate(zip(handle[1], got))]

    def chip_start(sums, slots, carry, tag):
        sems, sums, new_zones, carry = _chip_start(sums, zones, slots, carry, "rs_chip_start_" + tag)
        zones[:] = new_zones
        started.append((sums, slots, sems))
        return sums, carry

    rows8 = lambda g, r: g.reshape(NDEV, r, D)
    a1, dup1, dx3, dx3b, dnf1, dfcw1 = _ffn_bwd(dx4, up1, upc1, wups[1], fcws[1], wdns[1], x3, nf[1], "ffn_bwd1")
    g_f1 = [_dw_up(h3, dup1, "dw_up1"), rows8(_dw_dn(a1, dx4b, "dw_dn1"), DFF // NDEV)]
    hd_f1, dx3b = _pair_start(g_f1, dx3b, "rs_pair_start_f1")
    g_cout = rows8(_dw_rows(rc, dx3b, "dw_c_out"), D // NDEV)
    s_f1 = pair_sums(g_f1, hd_f1, g_cout, "f1")
    s_f1, dx3b = chip_start(s_f1, [(4, 1), (5, 1)], dx3b, "f1")
    dzc, dccw = _c_bwd(dx3b, cout, zc, ccw, "c_bwd")
    dx2, dx2b, dnm1 = _mm_nt_rms(dzc, cin, x2, nm[1], dx3, True, "mm_c_in_bwd")
    g_c = [_dw_cols(h2, dzc, NDEV, 3 * D // NDEV, "dw_c_in"), g_cout]
    hd_c, dx2 = _pair_start(g_c, dx2, "rs_pair_start_c")
    a0, dup0, dx1, dx1b, dnf0, dfcw0 = _ffn_bwd(dx2, up0, upc0, wups[0], fcws[0], wdns[0], x1, nf[0], "ffn_bwd0")
    s_c = pair_sums(g_c, hd_c, dx1b, "c")
    s_c, dx1b = chip_start(s_c, [(2, None), (3, None)], dx1b, "c")
    g_f0 = [_dw_up(h1, dup0, "dw_up0"), rows8(_dw_dn(a0, dx2b, "dw_dn0"), DFF // NDEV)]
    hd_f0, dx1b = _pair_start(g_f0, dx1b, "rs_pair_start_f0")
    g_wout0 = rows8(_dw_rows(ycat, dx1b, "dw_ab_out"), D // NDEV)
    s_f0 = pair_sums(g_f0, hd_f0, g_wout0, "f0")
    s_f0, dx1b = chip_start(s_f0, [(4, 0), (5, 0)], dx1b, "f0")
    dz, g512, dws, dbs = _ab_bwd(dx1b, wout0, z, yb2, a_ln_g, a_ln_b, wsm, bs_col, bcw, b_ln_g, b_ln_b, "ab_bwd")
    grad_x, dnm0 = _mm_nt_rms(dz, win0, x0, nm[0], dx1, False, "mm_ab_in_bwd")
    g_ab = [_dw_cols(h0, dz, NDEV, 2 * D // NDEV, "dw_ab_in"), g_wout0]
    hd_ab, _ = _pair_start(g_ab, None, "rs_pair_start_ab")

    g1024 = jnp.concatenate([dnm0, dnm1, dnf0, dnf1, dnfin, dccw], axis=0)
    gfc = jnp.concatenate([dfcw0, dfcw1], axis=0).reshape(2 * NG * 2 * 3, FB)
    g1024, g512, dws, dbs, gfc, loss_sum = _small_allreduce(
        [g1024, g512, dws.reshape(HEADS * CHUNK, CHUNK), dbs.reshape(HEADS, CHUNK), gfc, loss_part], (2,),
        hd_ab[1][0], "small_allreduce")
    loss = loss_sum[0, 0]
    s_ab = pair_sums(g_ab, hd_ab, g1024, "ab")
    s_ab, _ = chip_start(s_ab, [(0, None), (1, None)], None, "ab")
    p_cin, p_cout, p_wup, p_wdn = _chip_wait(started[:3], zones[2:], [2, 3, 4, 5], s_ab[0], "rs_chip_wait_early")

    chip = (2 * xi + yi).astype(jnp.int32).reshape(1)

    def big_update(w, m, v, parts, mine, name):
        shp = w.shape
        w3, m3, v3 = (a.reshape((-1,) + shp[-2:]) for a in (w, m, v))
        p4 = parts.reshape((3,) + w3.shape)
        return [o.reshape(shp) for o in _adam_big(w3, m3, v3, p4, mine, chip, name)]

    u_cin = big_update(c_w_in, m_c_w_in, v_c_w_in, p_cin, [s_c[0]], "adam_c_w_in")
    u_cout = big_update(c_w_out, m_c_w_out, v_c_w_out, p_cout, [s_c[1]], "adam_c_w_out")
    tr_ = lambda a: jnp.swapaxes(a, 1, 2)
    u_wup = [tr_(o) for o in big_update(tr_(f_w_up), tr_(m_f_w_up), tr_(v_f_w_up), p_wup,
                                        [s_f0[0], s_f1[0]], "adam_f_w_up")]
    u_wdn = big_update(f_w_down, m_f_w_down, v_f_w_down, p_wdn, [s_f0[1], s_f1[1]], "adam_f_w_down")
    p_win0, p_wout0 = _chip_wait(started[3:], zones[:2], [0, 1], u_wdn[0], "rs_chip_wait_late")
    u_win0 = big_update(ab_w_in, m_ab_w_in, v_ab_w_in, p_win0, [s_ab[0]], "adam_ab_w_in")
    u_wout0 = big_update(ab_w_out, m_ab_w_out, v_ab_w_out, p_wout0, [s_ab[1]], "adam_ab_w_out")

    g_norm_mix = g1024[0:2]
    g_norm_ffn = g1024[2:4]
    g_norm_final = g1024[4:5]
    g_ccw = lax.dynamic_slice(g1024[5:8], (0, dev * (D // NDEV)), (3, D // NDEV))
    g_bcw = lax.dynamic_slice(g512[8:8 + BCONV], (0, dev * (DA // NDEV)), (BCONV, DA // NDEV))
    gfc = gfc.reshape(2, NG, 2, 3, FB)
    g_fcw = lax.dynamic_slice(gfc, (0, dev % NG, dev // NG, 0, 0), (2, 1, 1, 3, FB)).reshape(2, 3, FB)
    small_w = [norm_mix, norm_ffn, nfin, a_ln_g, a_ln_b, a_w_s[0], a_b_s[0], b_conv_w[0], b_conv_b,
               b_ln_g, b_ln_b, c_conv_w[0], f_conv_w]
    small_g = [g_norm_mix, g_norm_ffn, g_norm_final, g512[0:1], g512[1:2],
               dws.reshape(HEADS, CHUNK, CHUNK), dbs, g_bcw, g512[2:3],
               g512[3:4], g512[4:5], g_ccw, g_fcw]
    small_m = [m_norm_mix, m_norm_ffn, m_norm_final.reshape(1, D), m_a_ln_g, m_a_ln_b, m_a_w_s[0], m_a_b_s[0],
               m_b_conv_w[0], m_b_conv_b, m_b_ln_g, m_b_ln_b, m_c_conv_w[0], m_f_conv_w]
    small_v = [v_norm_mix, v_norm_ffn, v_norm_final.reshape(1, D), v_a_ln_g, v_a_ln_b, v_a_w_s[0], v_a_b_s[0],
               v_b_conv_w[0], v_b_conv_b, v_b_ln_g, v_b_ln_b, v_c_conv_w[0], v_f_conv_w]
    upd = _adam_small(small_w, small_g, small_m, small_v, "adam_small")
    ns = len(small_w)
    orig = [norm_mix, norm_ffn, norm_final, a_ln_g, a_ln_b, a_w_s, a_b_s, b_conv_w, b_conv_b,
            b_ln_g, b_ln_b, c_conv_w, f_conv_w]
    sg_out = [g.reshape(o.shape) for g, o in zip(small_g, orig)]
    sd_out = [a.reshape(o.shape) for a, o in zip(upd[0:ns], orig)]
    sm_out = [a.reshape(o.shape) for a, o in zip(upd[ns:2 * ns], orig)]
    sv_out = [a.reshape(o.shape) for a, o in zip(upd[2 * ns:3 * ns], orig)]

    def assemble(small, k):
        return [small[0], small[1], small[2], u_win0[k], small[3], small[4], small[5], small[6], small[7],
                small[8], small[9], small[10], u_wout0[k], u_cin[k], small[11], u_cout[k], u_wup[k],
                small[12], u_wdn[k]]

    grads = assemble(sg_out, 0)
    deltas = assemble(sd_out, 1)
    new_m = assemble(sm_out, 2)
    new_v = assemble(sv_out, 3)
    return (loss, grad_x.reshape(1, s, D), *grads, *deltas, *new_m, *new_v)
```

```python
import math

import jax
import jax.numpy as jnp
from jax import lax
from jax.experimental import pallas as pl
from jax.experimental.pallas import tpu as pltpu

F32 = jnp.float32
BF16 = jnp.bfloat16

D = 1024
DA = 512
HEADS = 4
CHUNK = 128
DFF = 2816
NDEV = 8
NCHIP = 4
FB = DFF * 2 // NDEV
NG = DFF // FB
BCONV = 31
EPS = 1e-6
HALO = 16
HALO_B = 32
RC = 32
NPART = 2
VMEM_LIMIT = 52 * 1024 * 1024
INV_SQRT2 = 1.0 / math.sqrt(2.0)
INV_SQRT_2PI = 1.0 / math.sqrt(2.0 * math.pi)

ADAM_LR = 0.001
ADAM_B1 = 0.9
ADAM_B2 = 0.999
ADAM_EPS = 1e-08
ADAM_WD = 0.01
ADAM_STEP = 10

MESH = pl.DeviceIdType.MESH
ANY = pl.BlockSpec(memory_space=pl.ANY)
NT_DIMS = (((1,), (1,)), ((), ()))
TN_DIMS = (((0,), (0,)), ((), ()))


def _params(*sem):
    return pltpu.CompilerParams(dimension_semantics=sem, vmem_limit_bytes=VMEM_LIMIT)


def _tile(s, want):
    return min(want, s)


def _sigmoid(x):
    return jax.nn.sigmoid(x)


def _dsilu(x, sg):
    return sg * (1.0 + x * (1.0 - sg))


def _gelu(x):
    return 0.5 * x * (1.0 + lax.erf(x * INV_SQRT2))


def _dgelu(x):
    return 0.5 * (1.0 + lax.erf(x * INV_SQRT2)) + x * jnp.exp(-0.5 * x * x) * INV_SQRT_2PI


def _ln_fwd(x, g, b):
    mu = jnp.mean(x, axis=-1, keepdims=True)
    xc = x - mu
    var = jnp.mean(xc * xc, axis=-1, keepdims=True)
    rstd = lax.rsqrt(var + EPS)
    xhat = xc * rstd
    return xhat * g + b, xhat, rstd


def _ln_bwd(dy, xhat, rstd, g):
    dxh = dy * g
    m1 = jnp.mean(dxh, axis=-1, keepdims=True)
    m2 = jnp.mean(dxh * xhat, axis=-1, keepdims=True)
    return rstd * (dxh - m1 - xhat * m2)


def _rms_bwd_math(dh, x, g):
    r = lax.rsqrt(jnp.mean(x * x, axis=-1, keepdims=True) + EPS)
    xhat = x * r
    dg = jnp.sum(dh * xhat, axis=0, keepdims=True)
    u = dh * g
    dx = r * (u - xhat * jnp.mean(u * xhat, axis=-1, keepdims=True))
    return dx, dg


def _conv3(xe, cw, halo):
    x0 = xe[halo:]
    x1 = pltpu.roll(xe, 1, 0)[halo:]
    x2 = pltpu.roll(xe, 2, 0)[halo:]
    return cw[2] * x0 + cw[1] * x1 + cw[0] * x2, (x0, x1, x2)


def _conv3_bwd_in(dce, cw, ts):
    n = dce.shape[0]
    d1 = pltpu.roll(dce, n - 1, 0)[:ts]
    d2 = pltpu.roll(dce, n - 2, 0)[:ts]
    return cw[2] * dce[:ts] + cw[1] * d1 + cw[0] * d2


def _conv3_bwd_w(dc, taps):
    x0, x1, x2 = taps
    return [jnp.sum(dc * x2, axis=0, keepdims=True), jnp.sum(dc * x1, axis=0, keepdims=True),
            jnp.sum(dc * x0, axis=0, keepdims=True)]


def _rms_fwd(x, g, name, after=None):
    s = x.shape[0]
    ts = _tile(s, 512)

    def body(x_ref, g_ref, *rest):
        h_ref = rest[-1]
        xv = x_ref[...]
        r = lax.rsqrt(jnp.mean(xv * xv, axis=-1, keepdims=True) + EPS)
        h_ref[...] = (xv * r * g_ref[...]).astype(BF16)

    extra = [] if after is None else [after]
    return pl.pallas_call(
        body, grid=(s // ts,), name=name,
        in_specs=[pl.BlockSpec((ts, D), lambda i: (i, 0)), pl.BlockSpec((1, D), lambda i: (0, 0))]
        + [ANY] * len(extra),
        out_specs=pl.BlockSpec((ts, D), lambda i: (i, 0)),
        out_shape=jax.ShapeDtypeStruct((s, D), BF16),
        compiler_params=_params("parallel"),
    )(x, g, *extra)


MXU_COLS = 256


def _pair(bn):
    return 1 if bn % MXU_COLS == 0 else 2


def _cols(w_ref, b, pair):
    return w_ref[b] if pair == 1 else jnp.concatenate([w_ref[b + q] for q in range(pair)], axis=1)


def _mm_in(h, wblk, name):
    s = h.shape[0]
    nb, _, bn = wblk.shape
    pair = _pair(bn)
    ts = _tile(s, 1024)

    def body(h_ref, w_ref, o_ref):
        hv = h_ref[...]
        for b in range(0, nb, pair):
            o_ref[:, b * bn:(b + pair) * bn] = jnp.dot(hv, _cols(w_ref, b, pair),
                                                       preferred_element_type=F32).astype(BF16)

    return pl.pallas_call(
        body, grid=(s // ts,), name=name,
        in_specs=[pl.BlockSpec((ts, D), lambda i: (i, 0)), pl.BlockSpec((nb, D, bn), lambda i: (0, 0, 0))],
        out_specs=pl.BlockSpec((ts, nb * bn), lambda i: (i, 0)),
        out_shape=jax.ShapeDtypeStruct((s, nb * bn), BF16),
        compiler_params=_params("parallel"),
    )(h, wblk)


def _rms_math(xv, g):
    r = lax.rsqrt(jnp.mean(xv * xv, axis=-1, keepdims=True) + EPS)
    return (xv * r * g).astype(BF16)


def _mm_out(y, w, xres, gnext, name):
    s = y.shape[0]
    ts = _tile(s, 1024)

    def body(y_ref, w_ref, x_ref, g_ref, o_ref, h_ref):
        xn = x_ref[...] + jnp.dot(y_ref[...], w_ref[...], preferred_element_type=F32)
        o_ref[...] = xn
        h_ref[...] = _rms_math(xn, g_ref[...])

    return pl.pallas_call(
        body, grid=(s // ts,), name=name,
        in_specs=[pl.BlockSpec((ts, D), lambda i: (i, 0)), pl.BlockSpec((D, D), lambda i: (0, 0)),
                  pl.BlockSpec((ts, D), lambda i: (i, 0)), pl.BlockSpec((1, D), lambda i: (0, 0))],
        out_specs=[pl.BlockSpec((ts, D), lambda i: (i, 0)), pl.BlockSpec((ts, D), lambda i: (i, 0))],
        out_shape=[jax.ShapeDtypeStruct((s, D), F32), jax.ShapeDtypeStruct((s, D), BF16)],
        compiler_params=_params("parallel"),
    )(y, w, xres, gnext)


CONV_ROWS = 32


def _rolled_copies(dst_ref, xe, back):
    n = xe.shape[0]
    dst_ref[0] = xe
    for r in range(1, 8):
        dst_ref[r] = pltpu.roll(xe, n - r if back else r, 0)


def _conv31(rolled_ref, cw_ref, ts, out_ref, bias):
    for o in range(0, ts, CONV_ROWS):
        acc = jnp.zeros((CONV_ROWS, DA), F32) + bias
        for sh in range(BCONV):
            q, r = divmod(sh, 8)
            lo = HALO_B - 8 * q + o
            acc = acc + cw_ref[BCONV - 1 - sh:BCONV - sh, :] * rolled_ref[r, lo:lo + CONV_ROWS, :]
        out_ref[o:o + CONV_ROWS, :] = acc


def _ab_fwd(z, lga, lba, wsm, bs_col, cwb, cbb, lgb, lbb, name):
    s = z.shape[0]
    ts = _tile(s, 256)
    hb = ts // HALO_B

    def body(z_ref, zh_ref, lga_ref, lba_ref, ws_ref, bs_ref, cw_ref, cb_ref, lgb_ref, lbb_ref,
             y_ref, yb2_ref, rolled):
        i = pl.program_id(0)
        z_t = z_ref[...].astype(F32)
        gu = _gelu(z_t[:, 0:DA])
        gv = _gelu(z_t[:, DA:2 * DA])
        vn, _, _ = _ln_fwd(gv, lga_ref[...], lba_ref[...])
        vnb = vn.astype(BF16)
        for c in range(ts // CHUNK):
            for h in range(HEADS):
                rs = slice(c * CHUNK, (c + 1) * CHUNK)
                cs = slice(h * CHUNK, (h + 1) * CHUNK)
                mixed = jnp.dot(ws_ref[h], vnb[rs, cs], preferred_element_type=F32) + bs_ref[h]
                y_ref[rs, cs] = (gu[rs, cs] * mixed).astype(BF16)
        zh = jnp.where(i > 0, zh_ref[...], jnp.zeros_like(zh_ref[...])).astype(F32)
        xb = jnp.concatenate([zh[:, 0:DA], z_t[:, 2 * DA:3 * DA]], axis=0)
        gb = jnp.concatenate([zh[:, DA:2 * DA], z_t[:, 3 * DA:4 * DA]], axis=0)
        _rolled_copies(rolled, xb * _sigmoid(gb), False)
        _conv31(rolled, cw_ref, ts, yb2_ref, cb_ref[...])
        nb_, _, _ = _ln_fwd(yb2_ref[...], lgb_ref[...], lbb_ref[...])
        y_ref[:, DA:2 * DA] = (nb_ * _sigmoid(nb_)).astype(BF16)

    row = lambda i: (0, 0)
    return pl.pallas_call(
        body, grid=(s // ts,), name=name,
        in_specs=[pl.BlockSpec((ts, 4 * DA), lambda i: (i, 0)),
                  pl.BlockSpec((HALO_B, 2 * DA), lambda i: (jnp.maximum(i * hb - 1, 0), 1)),
                  pl.BlockSpec((1, DA), row), pl.BlockSpec((1, DA), row),
                  pl.BlockSpec((HEADS, CHUNK, CHUNK), lambda i: (0, 0, 0)),
                  pl.BlockSpec((HEADS, CHUNK, 1), lambda i: (0, 0, 0)),
                  pl.BlockSpec((BCONV, DA), row), pl.BlockSpec((1, DA), row),
                  pl.BlockSpec((1, DA), row), pl.BlockSpec((1, DA), row)],
        out_specs=[pl.BlockSpec((ts, 2 * DA), lambda i: (i, 0)), pl.BlockSpec((ts, DA), lambda i: (i, 0))],
        out_shape=[jax.ShapeDtypeStruct((s, 2 * DA), BF16), jax.ShapeDtypeStruct((s, DA), F32)],
        scratch_shapes=[pltpu.VMEM((8, ts + HALO_B, DA), F32)],
        compiler_params=_params("parallel"),
    )(z, z, lga, lba, wsm, bs_col, cwb, cbb, lgb, lbb)


def _c_fwd(zc, cw, w, xres, gnext, name):
    s = zc.shape[0]
    ts = _tile(s, 512)
    hb = ts // HALO

    def body(z_ref, ch_ref, xh_ref, cw_ref, w_ref, x_ref, g_ref, r_ref, o_ref, h_ref):
        i = pl.program_id(0)
        z_t = z_ref[...].astype(F32)
        ph = jnp.where(i > 0, ch_ref[...].astype(F32) * xh_ref[...].astype(F32), 0.0)
        pe = jnp.concatenate([ph, z_t[:, D:2 * D] * z_t[:, 2 * D:3 * D]], axis=0)
        q, _ = _conv3(pe, [cw_ref[k:k + 1, :] for k in range(3)], HALO)
        r = (z_t[:, 0:D] * q).astype(BF16)
        r_ref[...] = r
        xn = x_ref[...] + jnp.dot(r, w_ref[...], preferred_element_type=F32)
        o_ref[...] = xn
        h_ref[...] = _rms_math(xn, g_ref[...])

    halo = lambda col: pl.BlockSpec((HALO, D), lambda i: (jnp.maximum(i * hb - 1, 0), col))
    tile = pl.BlockSpec((ts, D), lambda i: (i, 0))
    return pl.pallas_call(
        body, grid=(s // ts,), name=name,
        in_specs=[pl.BlockSpec((ts, 3 * D), lambda i: (i, 0)), halo(1), halo(2),
                  pl.BlockSpec((3, D), lambda i: (0, 0)), pl.BlockSpec((D, D), lambda i: (0, 0)), tile,
                  pl.BlockSpec((1, D), lambda i: (0, 0))],
        out_specs=[tile, tile, tile],
        out_shape=[jax.ShapeDtypeStruct((s, D), BF16), jax.ShapeDtypeStruct((s, D), F32),
                   jax.ShapeDtypeStruct((s, D), BF16)],
        compiler_params=_params("parallel"),
    )(zc, zc, zc, cw, w, xres, gnext)


def _final_math(xv, tv, gv):
    r = lax.rsqrt(jnp.mean(xv * xv, axis=-1, keepdims=True) + EPS)
    xhat = xv * r
    e = xhat * gv - tv
    part = 0.5 * jnp.sum(jnp.mean(e * e, axis=-1, keepdims=True), axis=0, keepdims=True)
    dy = e * (1.0 / D)
    dgp = jnp.sum(dy * xhat, axis=0, keepdims=True)
    u = dy * gv
    dx = r * (u - xhat * jnp.mean(u * xhat, axis=-1, keepdims=True))
    return dx, dgp, jnp.broadcast_to(part, (1, 128))


def _ffn_fwd(h, xres, wup, fcw, wdn, gnext, name, final=None):
    s = h.shape[0]
    ts = _tile(s, 512)
    hb = ts // HALO

    def body(h_ref, hh_ref, w_ref, cw_ref, wd_ref, x_ref, *rest):
        if final is not None:
            t_ref, gf_ref, up_ref, upc_ref, dx_ref, dxb_ref, dg_ref, loss_ref, up_s, xo_ref = rest
        elif gnext is not None:
            gn_ref, up_ref, upc_ref, xo_ref, hn_ref, up_s = rest
        else:
            up_ref, upc_ref, xo_ref, up_s = rest
        i = pl.program_id(0)
        m = pl.program_id(1)
        @pl.when(m == 0)
        def _():
            xo_ref[...] = x_ref[...]

        halo = jnp.where(i > 0, hh_ref[...], jnp.zeros_like(hh_ref[...]))
        hx = jnp.concatenate([halo, h_ref[...]], axis=0)
        acts = []
        for gv in range(2):
            up_s[gv] = jnp.dot(hx, w_ref[gv], preferred_element_type=F32)
            x0 = up_s[gv, HALO:HALO + ts, :]
            up_ref[gv] = x0.astype(BF16)
            upc = (cw_ref[gv, 2:3, :] * x0 + cw_ref[gv, 1:2, :] * up_s[gv, HALO - 1:HALO - 1 + ts, :]
                   + cw_ref[gv, 0:1, :] * up_s[gv, HALO - 2:HALO - 2 + ts, :])
            upc_ref[gv] = upc.astype(BF16)
            acts.append(upc)
        a = acts[0] * _sigmoid(acts[0]) * acts[1]
        xo_ref[...] += jnp.dot(a.astype(BF16), wd_ref[...], preferred_element_type=F32)

        if final is not None:
            @pl.when(m == NG - 1)
            def _():
                dx, dgp, part = _final_math(xo_ref[...], t_ref[...], gf_ref[...])
                dx_ref[...] = dx
                dxb_ref[...] = dx.astype(BF16)

                @pl.when(i == 0)
                def _():
                    dg_ref[...] = dgp
                    loss_ref[...] = part

                @pl.when(i > 0)
                def _():
                    dg_ref[...] += dgp
                    loss_ref[...] += part

        elif gnext is not None:
            @pl.when(m == NG - 1)
            def _():
                hn_ref[...] = _rms_math(xo_ref[...], gn_ref[...])

    tile = pl.BlockSpec((ts, D), lambda i, m: (i, 0))
    row = lambda n: pl.BlockSpec((1, n), lambda i, m: (0, 0))
    scratch = [pltpu.VMEM((2, ts + HALO, FB), F32)]
    if final is not None:
        more_in, more_ops = [tile, row(D)], list(final)
        more_out = [tile, tile, row(D), row(128)]
        more_shape = [jax.ShapeDtypeStruct((s, D), F32), jax.ShapeDtypeStruct((s, D), BF16),
                      jax.ShapeDtypeStruct((1, D), F32), jax.ShapeDtypeStruct((1, 128), F32)]
        scratch.append(pltpu.VMEM((ts, D), F32))
    else:
        nxt = gnext is not None
        more_in, more_ops = ([row(D)], [gnext]) if nxt else ([], [])
        more_out = [tile] + ([tile] if nxt else [])
        more_shape = [jax.ShapeDtypeStruct((s, D), F32)] + ([jax.ShapeDtypeStruct((s, D), BF16)] if nxt else [])
    return pl.pallas_call(
        body, grid=(s // ts, NG), name=name,
        in_specs=[tile,
                  pl.BlockSpec((HALO, D), lambda i, m: (jnp.maximum(i * hb - 1, 0), 0)),
                  pl.BlockSpec((2, None, D, FB), lambda i, m: (0, m, 0, 0)),
                  pl.BlockSpec((2, None, 3, FB), lambda i, m: (0, m, 0, 0)),
                  pl.BlockSpec((FB, D), lambda i, m: (m, 0)),
                  tile] + more_in,
        out_specs=[pl.BlockSpec((None, 2, ts, FB), lambda i, m: (m, 0, i, 0)),
                   pl.BlockSpec((None, 2, ts, FB), lambda i, m: (m, 0, i, 0))] + more_out,
        out_shape=[jax.ShapeDtypeStruct((NG, 2, s, FB), BF16), jax.ShapeDtypeStruct((NG, 2, s, FB), BF16)] + more_shape,
        scratch_shapes=scratch,
        compiler_params=_params("arbitrary", "arbitrary"),
    )(h, h, wup, fcw, wdn, xres, *more_ops)


def _ffn_bwd(df, up, upc, wup, fcw, wdn, xin, g, name):
    s = df.shape[0]
    ts = _tile(s, 512)
    nt = s // ts

    def body(df_ref, up_ref, upc_ref, w_ref, cw_ref, wd_ref, x_ref, g_ref,
             a_ref, dup_ref, dx_ref, dxb_ref, dg_ref, dcw_ref, carry, acc, tacc, dcs_ref):
        i = pl.program_id(0)
        m = pl.program_id(1)
        first = i == 0
        @pl.when(first)
        def _():
            carry[m] = jnp.zeros((2, 8, FB), F32)
            dcw_ref[m] = jnp.zeros((2, 3, FB), F32)

        @pl.when(m == 0)
        def _():
            acc[...] = jnp.zeros((ts, D), F32)

        cws = [[cw_ref[gv, k:k + 1, :] for k in range(3)] for gv in range(2)]
        part = ts // NPART
        das = [lax.dot_general(df_ref[p * part:(p + 1) * part, :].astype(BF16), wd_ref[...], NT_DIMS,
                               preferred_element_type=F32) for p in range(NPART)]

        tacc[...] = jnp.zeros((2, 3, 8, FB), F32)
        dcs_ref[:, ts:ts + 8, :] = carry[m]
        for r in reversed(range(ts // RC)):
            rs = slice(r * RC, (r + 1) * RC)
            gate = upc_ref[0, rs, :].astype(F32)
            val = upc_ref[1, rs, :].astype(F32)
            sg = _sigmoid(gate)
            sl = gate * sg
            a_ref[rs, :] = (sl * val).astype(BF16)
            da_c = das[(r * RC) // part][(r * RC) % part:(r * RC) % part + RC]
            dcs = [da_c * val * _dsilu(gate, sg), da_c * sl]
            for gv in range(2):
                dc = dcs[gv]
                dcs_ref[gv, rs, :] = dc
                d1 = dcs_ref[gv, r * RC + 1:(r + 1) * RC + 1, :]
                d2 = dcs_ref[gv, r * RC + 2:(r + 1) * RC + 2, :]
                du = cws[gv][2] * dc + cws[gv][1] * d1 + cws[gv][0] * d2
                dup_ref[gv, rs, :] = du.astype(BF16)
                x0 = up_ref[gv, rs, :].astype(F32)
                for k, dk in enumerate((d2, d1, dc)):
                    p = x0 * dk
                    tacc[gv, k] += sum(p[j:j + 8] for j in range(0, RC, 8))
            if (r * RC) % part == 0:
                ps = slice(r * RC, r * RC + part)
                acc[ps, :] += (
                    lax.dot_general(dup_ref[0, ps, :], w_ref[0], NT_DIMS, preferred_element_type=F32)
                    + lax.dot_general(dup_ref[1, ps, :], w_ref[1], NT_DIMS, preferred_element_type=F32))
        for gv in range(2):
            carry[m, gv] = dcs_ref[gv, 0:8, :]
            for k in range(3):
                dcw_ref[m, gv, k:k + 1, :] += jnp.sum(tacc[gv, k], axis=0, keepdims=True)

        @pl.when(m == NG - 1)
        def _():
            dx, dgp = _rms_bwd_math(acc[...], x_ref[...], g_ref[...])
            dx = df_ref[...] + dx
            dx_ref[...] = dx
            dxb_ref[...] = dx.astype(BF16)

            @pl.when(first)
            def _():
                dg_ref[...] = dgp

            @pl.when(jnp.logical_not(first))
            def _():
                dg_ref[...] += dgp

    rev = lambda i: nt - 1 - i
    return pl.pallas_call(
        body, grid=(nt, NG), name=name,
        in_specs=[pl.BlockSpec((ts, D), lambda i, m: (rev(i), 0)),
                  pl.BlockSpec((None, 2, ts, FB), lambda i, m: (m, 0, rev(i), 0)),
                  pl.BlockSpec((None, 2, ts, FB), lambda i, m: (m, 0, rev(i), 0)),
                  pl.BlockSpec((2, None, D, FB), lambda i, m: (0, m, 0, 0)),
                  pl.BlockSpec((2, None, 3, FB), lambda i, m: (0, m, 0, 0)),
                  pl.BlockSpec((FB, D), lambda i, m: (m, 0)),
                  pl.BlockSpec((ts, D), lambda i, m: (rev(i), 0)),
                  pl.BlockSpec((1, D), lambda i, m: (0, 0))],
        out_specs=[pl.BlockSpec((None, ts, FB), lambda i, m: (m, rev(i), 0)),
                   pl.BlockSpec((None, 2, ts, FB), lambda i, m: (m, 0, rev(i), 0)),
                   pl.BlockSpec((ts, D), lambda i, m: (rev(i), 0)),
                   pl.BlockSpec((ts, D), lambda i, m: (rev(i), 0)),
                   pl.BlockSpec((1, D), lambda i, m: (0, 0)),
                   pl.BlockSpec((NG, 2, 3, FB), lambda i, m: (0, 0, 0, 0))],
        out_shape=[jax.ShapeDtypeStruct((NG, s, FB), BF16), jax.ShapeDtypeStruct((NG, 2, s, FB), BF16),
                   jax.ShapeDtypeStruct((s, D), F32), jax.ShapeDtypeStruct((s, D), BF16),
                   jax.ShapeDtypeStruct((1, D), F32),
                   jax.ShapeDtypeStruct((NG, 2, 3, FB), F32)],
        scratch_shapes=[pltpu.VMEM((NG, 2, 8, FB), F32), pltpu.VMEM((ts, D), F32),
                        pltpu.VMEM((2, 3, 8, FB), F32), pltpu.VMEM((2, ts + 8, FB), F32)],
        compiler_params=_params("arbitrary", "arbitrary"),
    )(df, up, upc, wup, fcw, wdn, xin, g)


def _mm_nt_rms(dy, wblk, x, g, dres, bf16_copy, name):
    s = dy.shape[0]
    nb, _, bn = wblk.shape
    pair = _pair(bn)
    ts = _tile(s, 512)

    def body(dy_ref, w_ref, x_ref, g_ref, dr_ref, dx_ref, *rest):
        dg_ref = rest[-1]
        i = pl.program_id(0)
        acc = jnp.zeros((ts, D), F32)
        for b in range(0, nb, pair):
            acc = acc + lax.dot_general(dy_ref[:, b * bn:(b + pair) * bn], _cols(w_ref, b, pair), NT_DIMS,
                                        preferred_element_type=F32)
        dx, dgp = _rms_bwd_math(acc, x_ref[...], g_ref[...])
        dx = dr_ref[...] + dx
        dx_ref[...] = dx
        if bf16_copy:
            rest[0][...] = dx.astype(BF16)

        @pl.when(i == 0)
        def _():
            dg_ref[...] = dgp

        @pl.when(i > 0)
        def _():
            dg_ref[...] += dgp

    tile = pl.BlockSpec((ts, D), lambda i: (i, 0))
    return pl.pallas_call(
        body, grid=(s // ts,), name=name,
        in_specs=[pl.BlockSpec((ts, nb * bn), lambda i: (i, 0)), pl.BlockSpec((nb, D, bn), lambda i: (0, 0, 0)),
                  tile, pl.BlockSpec((1, D), lambda i: (0, 0)), tile],
        out_specs=[tile] + ([tile] if bf16_copy else []) + [pl.BlockSpec((1, D), lambda i: (0, 0))],
        out_shape=[jax.ShapeDtypeStruct((s, D), F32)] + ([jax.ShapeDtypeStruct((s, D), BF16)] if bf16_copy else [])
        + [jax.ShapeDtypeStruct((1, D), F32)],
        compiler_params=_params("arbitrary"),
    )(dy, wblk, x, g, dres)


def _c_bwd(dx, w, zc, cw, name):
    s = dx.shape[0]
    ts = _tile(s, 512)
    nt = s // ts
    hb = ts // HALO

    def body(dx_ref, dxf_ref, w_ref, z_ref, ch_ref, xh_ref, bf_ref, cw_ref, dz_ref, dcw_ref):
        i = pl.program_id(0)
        cwv = [cw_ref[k:k + 1, :] for k in range(3)]
        dre = lax.dot_general(jnp.concatenate([dx_ref[...], dxf_ref[...]], axis=0), w_ref[...], NT_DIMS,
                              preferred_element_type=F32).astype(BF16).astype(F32)
        z_t = z_ref[...].astype(F32)
        bg, cg, xv = z_t[:, 0:D], z_t[:, D:2 * D], z_t[:, 2 * D:3 * D]
        ph = jnp.where(i > 0, ch_ref[...].astype(F32) * xh_ref[...].astype(F32), 0.0)
        pe = jnp.concatenate([ph, cg * xv], axis=0)
        q, taps = _conv3(pe, cwv, HALO)
        drv = dre[0:ts]
        dq = drv * bg
        dqf = jnp.where(i < nt - 1, dre[ts:ts + HALO] * bf_ref[...].astype(F32), 0.0)
        dp = _conv3_bwd_in(jnp.concatenate([dq, dqf], axis=0), cwv, ts)
        dz_ref[:, 0:D] = (drv * q).astype(BF16)
        dz_ref[:, D:2 * D] = (dp * xv).astype(BF16)
        dz_ref[:, 2 * D:3 * D] = (dp * cg).astype(BF16)
        rows = _conv3_bwd_w(dq, taps)

        @pl.when(i == 0)
        def _():
            for k in range(3):
                dcw_ref[k:k + 1, :] = rows[k]

        @pl.when(i > 0)
        def _():
            for k in range(3):
                dcw_ref[k:k + 1, :] += rows[k]

    past = lambda col: pl.BlockSpec((HALO, D), lambda i: (jnp.maximum(i * hb - 1, 0), col))
    nxt = lambda i: jnp.minimum((i + 1) * hb, s // HALO - 1)
    return pl.pallas_call(
        body, grid=(nt,), name=name,
        in_specs=[pl.BlockSpec((ts, D), lambda i: (i, 0)),
                  pl.BlockSpec((HALO, D), lambda i: (nxt(i), 0)),
                  pl.BlockSpec((D, D), lambda i: (0, 0)),
                  pl.BlockSpec((ts, 3 * D), lambda i: (i, 0)), past(1), past(2),
                  pl.BlockSpec((HALO, D), lambda i: (nxt(i), 0)),
                  pl.BlockSpec((3, D), lambda i: (0, 0))],
        out_specs=[pl.BlockSpec((ts, 3 * D), lambda i: (i, 0)), pl.BlockSpec((3, D), lambda i: (0, 0))],
        out_shape=[jax.ShapeDtypeStruct((s, 3 * D), BF16), jax.ShapeDtypeStruct((3, D), F32)],
        compiler_params=_params("arbitrary"),
    )(dx, dx, w, zc, zc, zc, zc, cw)


G512_ROWS = 40


def _ab_bwd(dx, w, z, yb2, lga, lba, wsm, bs_col, cwb, lgb, lbb, name):
    s = z.shape[0]
    ts = _tile(s, 256)
    nt = s // ts
    hb = ts // HALO_B
    nch = ts // CHUNK

    def body(z_ref, zh_ref, dx_ref, dxf_ref, w_ref, yb2_ref, yb2f_ref, lga_ref, lba_ref, ws_ref, bs_ref,
             cw_ref, lgb_ref, lbb_ref, dz_ref, g512_ref, dws_ref, dbs_ref, dvn_ref, fwd_rolled, bwd_rolled, du_s):
        i = pl.program_id(0)
        last = i == nt - 1

        @pl.when(i == 0)
        def _():
            g512_ref[...] = jnp.zeros((G512_ROWS, DA), F32)
            dws_ref[...] = jnp.zeros((HEADS, CHUNK, CHUNK), F32)
            dbs_ref[...] = jnp.zeros((HEADS, CHUNK, 1), F32)

        def add_row(k, v):
            g512_ref[k:k + 1, :] += v

        z_t = z_ref[...].astype(F32)
        nt_dot = lambda a, b: lax.dot_general(a, b, NT_DIMS, preferred_element_type=F32).astype(BF16).astype(F32)
        dy_t = nt_dot(dx_ref[...], w_ref[...])
        dyf = nt_dot(dxf_ref[...], w_ref[DA:2 * DA, :])
        ua, va = z_t[:, 0:DA], z_t[:, DA:2 * DA]
        gu = _gelu(ua)
        gv = _gelu(va)
        lga_v = lga_ref[...]
        vn, xhat_a, rstd_a = _ln_fwd(gv, lga_v, lba_ref[...])
        vnb = vn.astype(BF16)
        causal = (lax.broadcasted_iota(jnp.int32, (CHUNK, CHUNK), 0)
                  >= lax.broadcasted_iota(jnp.int32, (CHUNK, CHUNK), 1)).astype(F32)
        for c in range(nch):
            for h in range(HEADS):
                rs = slice(c * CHUNK, (c + 1) * CHUNK)
                cs = slice(h * CHUNK, (h + 1) * CHUNK)
                vblk = vnb[rs, cs]
                mixed = jnp.dot(ws_ref[h], vblk, preferred_element_type=F32) + bs_ref[h]
                dyb_ = dy_t[rs, cs]
                dmix = dyb_ * gu[rs, cs]
                dmb = dmix.astype(BF16)
                dz_ref[rs, cs] = (dyb_ * mixed * _dgelu(ua[rs, cs])).astype(BF16)
                dvn_ref[rs, cs] = lax.dot_general(ws_ref[h], dmb, TN_DIMS, preferred_element_type=F32)
                dws_ref[h] += causal * lax.dot_general(dmb, vblk, NT_DIMS, preferred_element_type=F32)
                dbs_ref[h] += jnp.sum(dmix, axis=1, keepdims=True)
        dvn = dvn_ref[...]
        add_row(0, jnp.sum(dvn * xhat_a, axis=0, keepdims=True))
        add_row(1, jnp.sum(dvn, axis=0, keepdims=True))
        dgv = _ln_bwd(dvn, xhat_a, rstd_a, lga_v)
        dz_ref[:, DA:2 * DA] = (dgv * _dgelu(va)).astype(BF16)
        lgb_v = lgb_ref[...]
        dyb_e = jnp.concatenate(
            [dy_t[:, DA:2 * DA], jnp.where(last, 0.0, dyf)], axis=0)
        yb2_e = jnp.concatenate([yb2_ref[...], jnp.where(last, 0.0, yb2f_ref[...])], axis=0)
        n_e, xhat_b, rstd_b = _ln_fwd(yb2_e, lgb_v, lbb_ref[...])
        sgn = _sigmoid(n_e)
        dn = dyb_e * _dsilu(n_e, sgn)
        dy2 = _ln_bwd(dn, xhat_b, rstd_b, lgb_v)
        add_row(2, jnp.sum(dy2[:ts], axis=0, keepdims=True))
        add_row(3, jnp.sum(dn[:ts] * xhat_b[:ts], axis=0, keepdims=True))
        add_row(4, jnp.sum(dn[:ts], axis=0, keepdims=True))
        zh = jnp.where(i > 0, zh_ref[...], jnp.zeros_like(zh_ref[...])).astype(F32)
        xb_t, gb_t = z_t[:, 2 * DA:3 * DA], z_t[:, 3 * DA:4 * DA]
        sgb = _sigmoid(gb_t)
        _rolled_copies(fwd_rolled, jnp.concatenate(
            [zh[:, 0:DA] * _sigmoid(zh[:, DA:2 * DA]), xb_t * sgb], axis=0), False)
        _rolled_copies(bwd_rolled, dy2, True)
        for o in range(0, ts, CONV_ROWS):
            acc = jnp.zeros((CONV_ROWS, DA), F32)
            for sh in range(BCONV):
                q, r = divmod(sh, 8)
                acc = acc + cw_ref[BCONV - 1 - sh:BCONV - sh, :] * bwd_rolled[r, 8 * q + o:8 * q + o + CONV_ROWS, :]
            du_s[o:o + CONV_ROWS, :] = acc
        for sh in range(BCONV):
            q, r = divmod(sh, 8)
            acc = jnp.zeros((CONV_ROWS, DA), F32)
            for o in range(0, ts, CONV_ROWS):
                lo = HALO_B - 8 * q + o
                acc = acc + bwd_rolled[0, o:o + CONV_ROWS, :] * fwd_rolled[r, lo:lo + CONV_ROWS, :]
            add_row(8 + BCONV - 1 - sh, jnp.sum(acc, axis=0, keepdims=True))
        du = du_s[...]
        dz_ref[:, 2 * DA:3 * DA] = (du * sgb).astype(BF16)
        dz_ref[:, 3 * DA:4 * DA] = (du * xb_t * sgb * (1.0 - sgb)).astype(BF16)

    row = lambda i: (0, 0)
    nxt = lambda i: jnp.minimum((i + 1) * hb, s // HALO_B - 1)
    return pl.pallas_call(
        body, grid=(nt,), name=name,
        in_specs=[pl.BlockSpec((ts, 4 * DA), lambda i: (i, 0)),
                  pl.BlockSpec((HALO_B, 2 * DA), lambda i: (jnp.maximum(i * hb - 1, 0), 1)),
                  pl.BlockSpec((ts, D), lambda i: (i, 0)),
                  pl.BlockSpec((HALO_B, D), lambda i: (nxt(i), 0)),
                  pl.BlockSpec((D, D), lambda i: (0, 0)),
                  pl.BlockSpec((ts, DA), lambda i: (i, 0)),
                  pl.BlockSpec((HALO_B, DA), lambda i: (nxt(i), 0)),
                  pl.BlockSpec((1, DA), row), pl.BlockSpec((1, DA), row),
                  pl.BlockSpec((HEADS, CHUNK, CHUNK), lambda i: (0, 0, 0)),
                  pl.BlockSpec((HEADS, CHUNK, 1), lambda i: (0, 0, 0)),
                  pl.BlockSpec((BCONV, DA), row), pl.BlockSpec((1, DA), row), pl.BlockSpec((1, DA), row)],
        out_specs=[pl.BlockSpec((ts, 4 * DA), lambda i: (i, 0)),
                   pl.BlockSpec((G512_ROWS, DA), row),
                   pl.BlockSpec((HEADS, CHUNK, CHUNK), lambda i: (0, 0, 0)),
                   pl.BlockSpec((HEADS, CHUNK, 1), lambda i: (0, 0, 0))],
        out_shape=[jax.ShapeDtypeStruct((s, 4 * DA), BF16), jax.ShapeDtypeStruct((G512_ROWS, DA), F32),
                   jax.ShapeDtypeStruct((HEADS, CHUNK, CHUNK), F32),
                   jax.ShapeDtypeStruct((HEADS, CHUNK, 1), F32)],
        scratch_shapes=[pltpu.VMEM((ts, DA), F32), pltpu.VMEM((8, ts + HALO_B, DA), F32),
                        pltpu.VMEM((8, ts + HALO_B, DA), F32), pltpu.VMEM((ts, DA), F32)],
        compiler_params=_params("arbitrary"),
    )(z, z, dx, dx, w, yb2, yb2, lga, lba, wsm, bs_col, cwb, lgb, lbb)


def _dw_cols(a, dy, nb, bn, name):
    s = a.shape[0]
    n = nb * bn
    ch = _tile(s, 512)
    nch = s // ch
    nbuf = min(3, nch)

    def body(a_hbm, dy_hbm, o_ref, a_buf, dy_buf, acc, sem):
        def copies(c, slot):
            rows = pl.ds(c * ch, ch)
            return (pltpu.make_async_copy(a_hbm.at[rows], a_buf.at[slot], sem.at[0, slot]),
                    pltpu.make_async_copy(dy_hbm.at[rows], dy_buf.at[slot], sem.at[1, slot]))

        for c in range(nbuf):
            for cp in copies(c, c):
                cp.start()
        acc[...] = jnp.zeros((D, n), F32)

        def step(c, carry):
            slot = c % nbuf
            for cp in copies(c, slot):
                cp.wait()
            acc[...] += lax.dot_general(a_buf[slot], dy_buf[slot], TN_DIMS, preferred_element_type=F32)

            @pl.when(c + nbuf < nch)
            def _():
                for cp in copies(c + nbuf, slot):
                    cp.start()
            return carry

        lax.fori_loop(0, nch, step, 0)
        for q in range(nb):
            o_ref[q] = acc[:, q * bn:(q + 1) * bn].astype(BF16)

    return pl.pallas_call(
        body, name=name,
        in_specs=[ANY, ANY],
        out_shape=jax.ShapeDtypeStruct((nb, D, bn), BF16),
        scratch_shapes=[pltpu.VMEM((nbuf, ch, D), BF16), pltpu.VMEM((nbuf, ch, n), BF16),
                        pltpu.VMEM((D, n), F32), pltpu.SemaphoreType.DMA((2, nbuf))],
        compiler_params=_params(),
    )(a, dy)


def _dw_rows(a, dy, name):
    s = a.shape[0]
    tm = _tile(s, 4096)
    nt = s // tm
    rb = 512

    def body(a_ref, dy_ref, o_ref, acc):
        t = pl.program_id(1)
        p = lax.dot_general(a_ref[...], dy_ref[...], TN_DIMS, preferred_element_type=F32)

        @pl.when(t == 0)
        def _():
            acc[...] = p

        @pl.when(t > 0)
        def _():
            acc[...] += p

        @pl.when(t == nt - 1)
        def _():
            o_ref[...] = acc[...].astype(BF16)

    return pl.pallas_call(
        body, grid=(D // rb, nt), name=name,
        in_specs=[pl.BlockSpec((tm, rb), lambda j, t: (t, j)), pl.BlockSpec((tm, D), lambda j, t: (t, 0))],
        out_specs=pl.BlockSpec((rb, D), lambda j, t: (j, 0)),
        out_shape=jax.ShapeDtypeStruct((D, D), BF16),
        scratch_shapes=[pltpu.VMEM((rb, D), F32)],
        compiler_params=_params("arbitrary", "arbitrary"),
    )(a, dy)


def _dw_up(h, dup, name):
    s = h.shape[0]
    tm = _tile(s, 4096)
    nt = s // tm

    def body(h_ref, d_ref, o_ref, acc):
        t = pl.program_id(1)
        p = lax.dot_general(d_ref[...], h_ref[...], TN_DIMS, preferred_element_type=F32)

        @pl.when(t == 0)
        def _():
            acc[...] = p

        @pl.when(t > 0)
        def _():
            acc[...] += p

        @pl.when(t == nt - 1)
        def _():
            o_ref[...] = acc[...].astype(BF16)

    return pl.pallas_call(
        body, grid=(NDEV, nt), name=name,
        in_specs=[pl.BlockSpec((tm, D), lambda b, t: (t, 0)),
                  pl.BlockSpec((None, None, tm, FB), lambda b, t: (b % NG, b // NG, t, 0))],
        out_specs=pl.BlockSpec((None, FB, D), lambda b, t: (b, 0, 0)),
        out_shape=jax.ShapeDtypeStruct((NDEV, FB, D), BF16),
        scratch_shapes=[pltpu.VMEM((FB, D), F32)],
        compiler_params=_params("arbitrary", "arbitrary"),
    )(h, dup)


def _dw_dn(a, df, name):
    s = df.shape[0]
    tm = _tile(s, 4096)
    nt = s // tm

    def body(a_ref, d_ref, o_ref, acc):
        t = pl.program_id(1)
        p = lax.dot_general(a_ref[...], d_ref[...], TN_DIMS, preferred_element_type=F32)

        @pl.when(t == 0)
        def _():
            acc[...] = p

        @pl.when(t > 0)
        def _():
            acc[...] += p

        @pl.when(t == nt - 1)
        def _():
            o_ref[...] = acc[...].astype(BF16)

    return pl.pallas_call(
        body, grid=(NG, nt), name=name,
        in_specs=[pl.BlockSpec((None, tm, FB), lambda m, t: (m, t, 0)), pl.BlockSpec((tm, D), lambda m, t: (t, 0))],
        out_specs=pl.BlockSpec((FB, D), lambda m, t: (m, 0)),
        out_shape=jax.ShapeDtypeStruct((DFF, D), BF16),
        scratch_shapes=[pltpu.VMEM((FB, D), F32)],
        compiler_params=_params("arbitrary", "arbitrary"),
    )(a, df)


def _place():
    x, y, c = lax.axis_index("x"), lax.axis_index("y"), lax.axis_index("c")
    chips = [(1 - x, y), (x, 1 - y), (1 - x, 1 - y)]
    return x, y, c, chips


def _zone(shard, dev):
    return lax.dynamic_update_slice(lax.empty((NDEV,) + shard.shape, shard.dtype), shard[None],
                                    (dev,) + (0,) * shard.ndim)


HBM_SPEC = pl.BlockSpec(memory_space=pltpu.HBM)
SEM_SPEC = pl.BlockSpec(memory_space=pltpu.SEMAPHORE)
DATAFLOW = pltpu.SideEffectType.DATAFLOW_SIDE_EFFECTING


def _hbm(a):
    return pltpu.with_memory_space_constraint(a, pltpu.HBM)


def _hbm_like(arrs):
    return [pltpu.HBM(a.shape, a.dtype) for a in arrs]


def _ag_start(srcs, lands, after, name):
    n = len(srcs)
    ns = 8 * n

    def body(*refs):
        src, land = refs[:n], refs[n:2 * n]
        sems = refs[2 * n + 1:2 * n + 1 + ns]
        token = refs[-1]
        x, y, c, chips = _place()
        peers = [(x, y, 1 - c)] + [(*chip, c) for chip in chips]
        for t in range(n):
            for k, to in enumerate(peers):
                pltpu.make_async_remote_copy(
                    src_ref=src[t], dst_ref=land[t].at[4 * x + 2 * y + c],
                    send_sem=sems[2 * (4 * t + k)], recv_sem=sems[2 * (4 * t + k) + 1],
                    device_id=to, device_id_type=MESH).start()
        token[...] = jnp.zeros_like(token)

    res = pl.pallas_call(
        body, name=name,
        in_specs=[HBM_SPEC] * (2 * n) + [ANY],
        out_specs=[SEM_SPEC] * ns + [HBM_SPEC] * (2 * n) + [pl.BlockSpec(memory_space=pltpu.VMEM)],
        out_shape=[pltpu.SemaphoreType.DMA(())] * ns + _hbm_like(srcs) + _hbm_like(lands)
        + [jax.ShapeDtypeStruct((8, 128), F32)],
        input_output_aliases={i: ns + i for i in range(2 * n)},
        compiler_params=pltpu.CompilerParams(has_side_effects=DATAFLOW),
    )(*[_hbm(a) for a in srcs], *[_hbm(a) for a in lands], after)
    sems = [[(res[2 * (4 * t + k)], res[2 * (4 * t + k) + 1]) for k in range(4)] for t in range(n)]
    return sems, res[ns:ns + n], res[ns + n:ns + 2 * n], res[-1]


def _ag_forward(srcs, lands, sems1, after, name):
    n = len(srcs)
    flat1 = [s for t in range(n) for k in range(1, 4) for s in sems1[t][k]]
    n1 = len(flat1)

    def body(*refs):
        src, land = refs[:n], refs[n:2 * n]
        s1 = refs[2 * n:2 * n + n1]
        s2 = refs[2 * n + n1 + 1:2 * n + n1 + 1 + 6 * n]
        x, y, c, chips = _place()
        for j, (cx, cy) in enumerate(chips):
            for t in range(n):
                blk = land[t].at[4 * cx + 2 * cy + c]
                pltpu.make_async_remote_copy(
                    src_ref=src[t], dst_ref=blk, send_sem=s1[2 * (3 * t + j)], recv_sem=s1[2 * (3 * t + j) + 1],
                    device_id=(cx, cy, c), device_id_type=MESH).wait_recv()
                pltpu.make_async_remote_copy(
                    src_ref=blk, dst_ref=blk, send_sem=s2[2 * (3 * t + j)], recv_sem=s2[2 * (3 * t + j) + 1],
                    device_id=(x, y, 1 - c), device_id_type=MESH).start()

    res = pl.pallas_call(
        body, name=name,
        in_specs=[HBM_SPEC] * (2 * n) + [SEM_SPEC] * n1 + [ANY],
        out_specs=[SEM_SPEC] * (6 * n) + [HBM_SPEC] * n,
        out_shape=[pltpu.SemaphoreType.DMA(())] * (6 * n) + _hbm_like(lands),
        input_output_aliases={n + i: 6 * n + i for i in range(n)},
        compiler_params=pltpu.CompilerParams(has_side_effects=DATAFLOW),
    )(*srcs, *lands, *flat1, after)
    sems2 = [[(res[2 * (3 * t + j)], res[2 * (3 * t + j) + 1]) for j in range(3)] for t in range(n)]
    return sems2, res[6 * n:]


def _ag_finish(srcs, lands, sems1, sems2, after, name):
    n = len(srcs)
    flat1 = [s for t in range(n) for k in range(4) for s in sems1[t][k]]
    flat2 = [s for t in range(n) for j in range(3) for s in sems2[t][j]]
    n1, n2 = len(flat1), len(flat2)

    def body(*refs):
        src, land = refs[:n], refs[n:2 * n]
        s1 = refs[2 * n:2 * n + n1]
        s2 = refs[2 * n + n1:2 * n + n1 + n2]
        x, y, c, chips = _place()
        sib = (x, y, 1 - c)
        for t in range(n):
            own = land[t].at[4 * x + 2 * y + 1 - c]
            pltpu.make_async_remote_copy(
                src_ref=src[t], dst_ref=own, send_sem=s1[8 * t], recv_sem=s1[8 * t + 1],
                device_id=sib, device_id_type=MESH).wait_recv()
            for k in range(4):
                pltpu.make_async_remote_copy(
                    src_ref=src[t], dst_ref=own, send_sem=s1[2 * (4 * t + k)], recv_sem=s1[2 * (4 * t + k) + 1],
                    device_id=sib, device_id_type=MESH).wait_send()
            for j, (cx, cy) in enumerate(chips):
                blk = land[t].at[4 * cx + 2 * cy + 1 - c]
                cp = pltpu.make_async_remote_copy(
                    src_ref=blk, dst_ref=blk, send_sem=s2[2 * (3 * t + j)], recv_sem=s2[2 * (3 * t + j) + 1],
                    device_id=sib, device_id_type=MESH)
                cp.wait_send()
                cp.wait_recv()

    return pl.pallas_call(
        body, name=name,
        in_specs=[HBM_SPEC] * (2 * n) + [SEM_SPEC] * (n1 + n2) + [ANY],
        out_specs=[HBM_SPEC] * n,
        out_shape=_hbm_like(lands),
        input_output_aliases={n + i: i for i in range(n)},
        compiler_params=pltpu.CompilerParams(has_side_effects=DATAFLOW),
    )(*srcs, *lands, *flat1, *flat2, after)


def _pair_copies(srcs, dsts, sems):
    x, y, c, _ = _place()
    nt = len(srcs)
    return [pltpu.make_async_remote_copy(
        src_ref=srcs[t].at[2 * j + 1 - c], dst_ref=dsts[t].at[j],
        send_sem=sems[2 * (NCHIP * t + j)], recv_sem=sems[2 * (NCHIP * t + j) + 1],
        device_id=(x, y, 1 - c), device_id_type=MESH) for t in range(nt) for j in range(NCHIP)]


def _pair_start(grads, carry, name):
    nt = len(grads)
    ns = 2 * NCHIP * nt
    zones = [_hbm(lax.empty((NCHIP,) + a.shape[1:], a.dtype)) for a in grads]
    extra = [] if carry is None else [_hbm(carry)]
    ne = len(extra)

    def body(*refs):
        for cp in _pair_copies(refs[:nt], refs[nt:2 * nt], refs[2 * nt + ne:2 * nt + ne + ns]):
            cp.start()

    res = pl.pallas_call(
        body, name=name,
        in_specs=[HBM_SPEC] * (2 * nt + ne),
        out_specs=[SEM_SPEC] * ns + [HBM_SPEC] * (2 * nt + ne),
        out_shape=[pltpu.SemaphoreType.DMA(())] * ns + _hbm_like(grads) + _hbm_like(zones) + _hbm_like(extra),
        input_output_aliases={i: ns + i for i in range(2 * nt + ne)},
        compiler_params=pltpu.CompilerParams(has_side_effects=DATAFLOW),
    )(*[_hbm(a) for a in grads], *zones, *extra)
    handle = (list(res[:ns]), list(res[ns:ns + nt]), list(res[ns + nt:ns + 2 * nt]))
    return handle, (res[ns + 2 * nt] if ne else None)


def _pair_wait(handle, after, name):
    sems, srcs, zones = handle
    nt, ns = len(srcs), len(sems)

    def body(*refs):
        for cp in _pair_copies(refs[:nt], refs[nt:2 * nt], refs[2 * nt:2 * nt + ns]):
            cp.wait_send()
            cp.wait_recv()

    return pl.pallas_call(
        body, name=name,
        in_specs=[HBM_SPEC] * (2 * nt) + [SEM_SPEC] * ns + [ANY],
        out_specs=[HBM_SPEC] * nt,
        out_shape=_hbm_like(zones),
        input_output_aliases={nt + i: i for i in range(nt)},
        compiler_params=pltpu.CompilerParams(has_side_effects=DATAFLOW),
    )(*srcs, *zones, *sems, after)


def _rows_tile(r, row_bytes, cap_bytes):
    best = None
    for tr in range(16, r + 1, 16):
        if r % tr == 0 and tr * row_bytes <= cap_bytes:
            best = tr
    return best if best is not None else r


def _pair_sum(own, got, cidx, name):
    _, _, r, cdim = own.shape
    tr = _rows_tile(r, 2 * cdim, 2 * 1024 * 1024)

    def body(c_ref, a_ref, b_ref, o_ref):
        o_ref[...] = (a_ref[...].astype(F32) + b_ref[...].astype(F32)).astype(BF16)

    return pl.pallas_call(
        body, name=name,
        grid_spec=pltpu.PrefetchScalarGridSpec(
            num_scalar_prefetch=1, grid=(NCHIP, r // tr),
            in_specs=[pl.BlockSpec((None, None, tr, cdim), lambda j, i, c_ref: (j, c_ref[0], i, 0)),
                      pl.BlockSpec((None, tr, cdim), lambda j, i, c_ref: (j, i, 0))],
            out_specs=pl.BlockSpec((None, tr, cdim), lambda j, i, c_ref: (j, i, 0))),
        out_shape=jax.ShapeDtypeStruct((NCHIP, r, cdim), BF16),
        compiler_params=_params("arbitrary", "arbitrary"),
    )(cidx, own, got)


def _chip_copies(srcs, zones, slots, sems):
    x, y, c, chips = _place()
    out = []
    for t, (z, l) in enumerate(slots):
        for k, (cx, cy) in enumerate(chips):
            dst = zones[z].at[k] if l is None else zones[z].at[k, l]
            out.append(pltpu.make_async_remote_copy(
                src_ref=srcs[t].at[2 * cx + cy], dst_ref=dst,
                send_sem=sems[2 * (3 * t + k)], recv_sem=sems[2 * (3 * t + k) + 1],
                device_id=(cx, cy, c), device_id_type=MESH))
    return out


def _chip_start(sums, zones, slots, carry, name):
    nt, nz = len(sums), len(zones)
    ns = 6 * nt
    extra = [] if carry is None else [_hbm(carry)]
    ne = len(extra)

    def body(*refs):
        for cp in _chip_copies(refs[:nt], refs[nt:nt + nz], slots, refs[nt + nz + ne:nt + nz + ne + ns]):
            cp.start()

    res = pl.pallas_call(
        body, name=name,
        in_specs=[HBM_SPEC] * (nt + nz + ne),
        out_specs=[SEM_SPEC] * ns + [HBM_SPEC] * (nt + nz + ne),
        out_shape=[pltpu.SemaphoreType.DMA(())] * ns + _hbm_like(sums) + _hbm_like(zones) + _hbm_like(extra),
        input_output_aliases={i: ns + i for i in range(nt + nz + ne)},
        compiler_params=pltpu.CompilerParams(has_side_effects=DATAFLOW),
    )(*[_hbm(a) for a in sums], *zones, *extra)
    return (list(res[:ns]), list(res[ns:ns + nt]), list(res[ns + nt:ns + nt + nz]),
            (res[ns + nt + nz] if ne else None))


def _chip_wait(started, zones, zone_ids, after, name):
    started = [(sums, [(zone_ids.index(z), l) for z, l in slots], sems) for sums, slots, sems in started]
    nz = len(zones)
    flat_src = [a for sums, _, _ in started for a in sums]
    flat_sem = [s for _, _, sems in started for s in sems]
    n_src, n_sem = len(flat_src), len(flat_sem)

    def body(*refs):
        srcs, zs, sems = refs[:n_src], refs[n_src:n_src + nz], refs[n_src + nz:n_src + nz + n_sem]
        so, se = 0, 0
        for sums, slots, sem_list in started:
            for cp in _chip_copies(srcs[so:so + len(sums)], zs, slots, sems[se:se + len(sem_list)]):
                cp.wait_send()
                cp.wait_recv()
            so += len(sums)
            se += len(sem_list)

    return pl.pallas_call(
        body, name=name,
        in_specs=[HBM_SPEC] * (n_src + nz) + [SEM_SPEC] * n_sem + [ANY],
        out_specs=[HBM_SPEC] * nz,
        out_shape=_hbm_like(zones),
        input_output_aliases={n_src + i: i for i in range(nz)},
        compiler_params=pltpu.CompilerParams(has_side_effects=DATAFLOW),
    )(*flat_src, *zones, *flat_sem, after)


def _small_allreduce(parts, y_first, after, name):
    nt = len(parts)

    def body(*refs):
        srcs, outs, bufs = refs[:nt], refs[nt + 1:2 * nt + 1], refs[2 * nt + 1:3 * nt + 1]
        send_sems, recv_sems = refs[3 * nt + 1:]
        x, y, c, _ = _place()
        along = {"c": (x, y, 1 - c), "x": (1 - x, y, c), "y": (x, 1 - y, c)}
        for t in range(nt):
            outs[t][...] = srcs[t][...]
        for step in range(3):
            order = [("c", "y", "x") if t in y_first else ("c", "x", "y") for t in range(nt)]
            copies = [pltpu.make_async_remote_copy(
                src_ref=outs[t], dst_ref=bufs[t].at[step],
                send_sem=send_sems.at[step, t], recv_sem=recv_sems.at[step, t],
                device_id=along[order[t][step]], device_id_type=MESH) for t in range(nt)]
            for cp in copies:
                cp.start()
            for cp in copies:
                cp.wait()
            for t in range(nt):
                outs[t][...] = outs[t][...] + bufs[t][step]

    vm = pl.BlockSpec(memory_space=pltpu.VMEM)
    return pl.pallas_call(
        body, name=name,
        in_specs=[vm] * nt + [ANY], out_specs=[vm] * nt,
        out_shape=[jax.ShapeDtypeStruct(a.shape, F32) for a in parts],
        scratch_shapes=[pltpu.VMEM((3,) + a.shape, F32) for a in parts]
        + [pltpu.SemaphoreType.DMA((3, nt)), pltpu.SemaphoreType.DMA((3, nt))],
        compiler_params=pltpu.CompilerParams(has_side_effects=True, vmem_limit_bytes=VMEM_LIMIT),
    )(*parts, after)


def _adam_math(w, g, m, v):
    m2 = ADAM_B1 * m + (1.0 - ADAM_B1) * g
    v2 = ADAM_B2 * v + (1.0 - ADAM_B2) * (g * g)
    m_hat = m2 / (1.0 - ADAM_B1 ** ADAM_STEP)
    v_hat = v2 / (1.0 - ADAM_B2 ** ADAM_STEP)
    delta = -ADAM_LR * (m_hat / (jnp.sqrt(v_hat) + ADAM_EPS) + ADAM_WD * w)
    return delta, m2, v2


def _adam_big(w, m, v, parts, mine, chip, name):
    nl, r, cdim = w.shape
    tr = _rows_tile(r, 4 * cdim, 3 * 512 * 1024)

    def body(c_ref, w_ref, m_ref, v_ref, p_ref, *rest):
        mine_refs, (g_ref, d_ref, mo_ref, vo_ref) = rest[:nl], rest[nl:]
        own = mine_refs[0][...]
        for l in range(1, nl):
            own = jnp.where(pl.program_id(0) == l, mine_refs[l][...], own)
        g = ((p_ref[0].astype(F32) + p_ref[1].astype(F32)) + p_ref[2].astype(F32)) + own.astype(F32)
        delta, m2, v2 = _adam_math(w_ref[...], g, m_ref[...], v_ref[...])
        g_ref[...] = g
        d_ref[...] = delta
        mo_ref[...] = m2
        vo_ref[...] = v2

    spec = pl.BlockSpec((None, tr, cdim), lambda l, i, c_ref: (l, i, 0))
    mine_specs = [pl.BlockSpec((None, tr, cdim), lambda l, i, c_ref, ll=ll: (c_ref[0], jnp.where(l == ll, i, 0), 0))
                  for ll in range(nl)]
    return pl.pallas_call(
        body, name=name,
        grid_spec=pltpu.PrefetchScalarGridSpec(
            num_scalar_prefetch=1, grid=(nl, r // tr),
            in_specs=[spec, spec, spec, pl.BlockSpec((3, None, tr, cdim), lambda l, i, c_ref: (0, l, i, 0))]
            + mine_specs,
            out_specs=[spec] * 4),
        out_shape=[jax.ShapeDtypeStruct(w.shape, F32)] * 4,
        compiler_params=_params("arbitrary", "arbitrary"),
    )(chip, w, m, v, parts, *mine)


def _adam_small(ws, gs, ms, vs, name):
    n = len(ws)

    def body(*refs):
        w_r, g_r, m_r, v_r = refs[:n], refs[n:2 * n], refs[2 * n:3 * n], refs[3 * n:4 * n]
        d_o, m_o, v_o = refs[4 * n:5 * n], refs[5 * n:6 * n], refs[6 * n:7 * n]
        for t in range(n):
            delta, m2, v2 = _adam_math(w_r[t][...], g_r[t][...], m_r[t][...], v_r[t][...])
            d_o[t][...] = delta
            m_o[t][...] = m2
            v_o[t][...] = v2

    vm = pl.BlockSpec(memory_space=pltpu.VMEM)
    shapes = [jax.ShapeDtypeStruct(a.shape, F32) for a in ws]
    return pl.pallas_call(
        body, name=name, in_specs=[vm] * (4 * n), out_specs=[vm] * (3 * n), out_shape=shapes * 3,
        compiler_params=pltpu.CompilerParams(vmem_limit_bytes=VMEM_LIMIT),
    )(*ws, *gs, *ms, *vs)


def kernel(x, norm_mix, norm_ffn, norm_final, ab_w_in, a_ln_g, a_ln_b, a_w_s, a_b_s, b_conv_w, b_conv_b, b_ln_g, b_ln_b, ab_w_out, c_w_in, c_conv_w, c_w_out, f_w_up, f_conv_w, f_w_down, loss_target, m_norm_mix, m_norm_ffn, m_norm_final, m_ab_w_in, m_a_ln_g, m_a_ln_b, m_a_w_s, m_a_b_s, m_b_conv_w, m_b_conv_b, m_b_ln_g, m_b_ln_b, m_ab_w_out, m_c_w_in, m_c_conv_w, m_c_w_out, m_f_w_up, m_f_conv_w, m_f_w_down, v_norm_mix, v_norm_ffn, v_norm_final, v_ab_w_in, v_a_ln_g, v_a_ln_b, v_a_w_s, v_a_b_s, v_b_conv_w, v_b_conv_b, v_b_ln_g, v_b_ln_b, v_ab_w_out, v_c_w_in, v_c_conv_w, v_c_w_out, v_f_w_up, v_f_conv_w, v_f_w_down):
    s = x.shape[1]
    x0 = x.reshape(s, D)
    tgt = loss_target.reshape(s, D)
    xi, yi, ci = lax.axis_index("x"), lax.axis_index("y"), lax.axis_index("c")
    dev = 4 * xi + 2 * yi + ci
    cidx = ci.astype(jnp.int32).reshape(1)

    bf = lambda a: a.astype(BF16)
    slab_w = 6 * CHUNK
    pad = lambda a, rows: jnp.pad(a, ((0, rows - a.shape[0]), (0, slab_w - a.shape[1])))
    slab = jnp.concatenate([pad(b_conv_w[0], 32), pad(c_conv_w[0], 8), pad(f_conv_w.reshape(6, FB), 8)], axis=0)
    later = [bf(ab_w_in[0]), bf(ab_w_out[0]), slab, bf(f_w_up[0]), bf(f_w_down[0]), bf(c_w_in[0]), bf(c_w_out[0]),
             bf(f_w_up[1]), bf(f_w_down[1])]
    lands = [_zone(a, dev) for a in later]
    groups = [[0], [1, 2], [3, 4], [5, 6], [7, 8]]
    ag_sems, later, lands, ag_token = _ag_start(later, lands, x0, "ag_start")

    causal = jnp.tril(jnp.ones((CHUNK, CHUNK), F32))
    wsm = (a_w_s[0] * causal).astype(BF16)
    bs_col = a_b_s.reshape(HEADS, CHUNK, 1)
    nm = [norm_mix[0:1], norm_mix[1:2]]
    nf = [norm_ffn[0:1], norm_ffn[1:2]]
    nfin = norm_final.reshape(1, D)

    def pass_on(ts_, after_ici, tag):
        srcs = [later[t] for t in ts_]
        sems1 = [ag_sems[t] for t in ts_]
        sems2, zone = _ag_forward(srcs, [lands[t] for t in ts_], sems1, after_ici, "ag_forward_" + tag)
        return srcs, zone, sems1, sems2

    def arrive(g, after_ici, after_d2d, tag):
        srcs, zone, sems1, sems2 = pass_on(groups[g], after_ici, tag)
        return _ag_finish(srcs, zone, sems1, sems2, after_d2d, "ag_finish_" + tag)

    h0 = _rms_fwd(x0, nm[0], "rms_mix0", after=ag_token)
    (win0,) = arrive(0, h0, h0, "w_in")
    z = _mm_in(h0, win0, "mm_ab_in")
    wout0, slab_g = arrive(1, z, z, "first")
    wout0 = wout0.reshape(D, D)
    bcw = jnp.transpose(slab_g[:, 0:BCONV, 0:DA // NDEV], (1, 0, 2)).reshape(BCONV, DA)
    ccw = jnp.transpose(slab_g[:, 32:35, 0:D // NDEV], (1, 0, 2)).reshape(3, D)
    fcw_g = slab_g[:, 40:46, 0:FB].reshape(2, NG, 2, 3, FB)
    fcws = [fcw_g[:, :, 0], fcw_g[:, :, 1]]
    ycat, yb2 = _ab_fwd(z, a_ln_g, a_ln_b, wsm, bs_col, bcw, b_conv_b, b_ln_g, b_ln_b, "ab_fwd")
    x1, h1 = _mm_out(ycat, wout0, x0, nf[0], "mm_ab_out")
    wup0, wdn0 = arrive(2, x1, x1, "ffn0")
    up0, upc0, x2, h2 = _ffn_fwd(h1, x1, wup0.reshape(2, NG, D, FB), fcws[0], wdn0.reshape(DFF, D), nm[1],
                                 "ffn_fwd0")
    cin, cout = arrive(3, x2, x2, "c")
    cout = cout.reshape(D, D)
    zc = _mm_in(h2, cin, "mm_c_in")
    last_part = pass_on(groups[4], zc, "ffn1")
    rc, x3, h3 = _c_fwd(zc, ccw, cout, x2, nf[1], "c_fwd_out")
    wup1, wdn1 = _ag_finish(*last_part, x3, "ag_finish_ffn1")
    wups = [wup0.reshape(2, NG, D, FB), wup1.reshape(2, NG, D, FB)]
    wdns = [wdn0.reshape(DFF, D), wdn1.reshape(DFF, D)]
    up1, upc1, dx4, dx4b, dnfin, loss_part = _ffn_fwd(h3, x3, wups[1], fcws[1], wdns[1], None, "ffn_fwd1_loss",
                                                      final=(tgt, nfin))

    zshape = lambda *sh: _hbm(lax.empty((3,) + sh, BF16))
    zones = [zshape(D, 2 * D // NDEV), zshape(D // NDEV, D), zshape(D, 3 * D // NDEV), zshape(D // NDEV, D),
             zshape(2, FB, D), zshape(2, DFF // NDEV, D)]
    started = []

    def pair_sums(grads, handle, after, tag):
        del grads
        got = _pair_wait(handle, after, "rs_pair_wait_" + tag)
        return [_pair_sum(b.reshape((NCHIP, 2) + b.shape[1:]), g, cidx, "rs_pair_sum_%s%d" % (tag, t))
                for t, (b, g) in enumerate(zip(handle[1], got))]

    def chip_start(sums, slots, carry, tag):
        sems, sums, new_zones, carry = _chip_start(sums, zones, slots, carry, "rs_chip_start_" + tag)
        zones[:] = new_zones
        started.append((sums, slots, sems))
        return sums, carry

    rows8 = lambda g, r: g.reshape(NDEV, r, D)
    a1, dup1, dx3, dx3b, dnf1, dfcw1 = _ffn_bwd(dx4, up1, upc1, wups[1], fcws[1], wdns[1], x3, nf[1], "ffn_bwd1")
    g_f1 = [_dw_up(h3, dup1, "dw_up1"), rows8(_dw_dn(a1, dx4b, "dw_dn1"), DFF // NDEV)]
    hd_f1, dx3b = _pair_start(g_f1, dx3b, "rs_pair_start_f1")
    g_cout = rows8(_dw_rows(rc, dx3b, "dw_c_out"), D // NDEV)
    s_f1 = pair_sums(g_f1, hd_f1, g_cout, "f1")
    s_f1, dx3b = chip_start(s_f1, [(4, 1), (5, 1)], dx3b, "f1")
    dzc, dccw = _c_bwd(dx3b, cout, zc, ccw, "c_bwd")
    dx2, dx2b, dnm1 = _mm_nt_rms(dzc, cin, x2, nm[1], dx3, True, "mm_c_in_bwd")
    g_c = [_dw_cols(h2, dzc, NDEV, 3 * D // NDEV, "dw_c_in"), g_cout]
    hd_c, dx2 = _pair_start(g_c, dx2, "rs_pair_start_c")
    a0, dup0, dx1, dx1b, dnf0, dfcw0 = _ffn_bwd(dx2, up0, upc0, wups[0], fcws[0], wdns[0], x1, nf[0], "ffn_bwd0")
    s_c = pair_sums(g_c, hd_c, dx1b, "c")
    s_c, dx1b = chip_start(s_c, [(2, None), (3, None)], dx1b, "c")
    g_f0 = [_dw_up(h1, dup0, "dw_up0"), rows8(_dw_dn(a0, dx2b, "dw_dn0"), DFF // NDEV)]
    hd_f0, dx1b = _pair_start(g_f0, dx1b, "rs_pair_start_f0")
    g_wout0 = rows8(_dw_rows(ycat, dx1b, "dw_ab_out"), D // NDEV)
    s_f0 = pair_sums(g_f0, hd_f0, g_wout0, "f0")
    s_f0, dx1b = chip_start(s_f0, [(4, 0), (5, 0)], dx1b, "f0")
    dz, g512, dws, dbs = _ab_bwd(dx1b, wout0, z, yb2, a_ln_g, a_ln_b, wsm, bs_col, bcw, b_ln_g, b_ln_b, "ab_bwd")
    grad_x, dnm0 = _mm_nt_rms(dz, win0, x0, nm[0], dx1, False, "mm_ab_in_bwd")
    g_ab = [_dw_cols(h0, dz, NDEV, 2 * D // NDEV, "dw_ab_in"), g_wout0]
    hd_ab, _ = _pair_start(g_ab, None, "rs_pair_start_ab")

    g1024 = jnp.concatenate([dnm0, dnm1, dnf0, dnf1, dnfin, dccw], axis=0)
    gfc = jnp.concatenate([dfcw0, dfcw1], axis=0).reshape(2 * NG * 2 * 3, FB)
    g1024, g512, dws, dbs, gfc, loss_sum = _small_allreduce(
        [g1024, g512, dws.reshape(HEADS * CHUNK, CHUNK), dbs.reshape(HEADS, CHUNK), gfc, loss_part], (2,),
        hd_ab[1][0], "small_allreduce")
    loss = loss_sum[0, 0]
    s_ab = pair_sums(g_ab, hd_ab, g1024, "ab")
    s_ab, _ = chip_start(s_ab, [(0, None), (1, None)], None, "ab")
    p_cin, p_cout, p_wup, p_wdn = _chip_wait(started[:3], zones[2:], [2, 3, 4, 5], s_ab[0], "rs_chip_wait_early")

    chip = (2 * xi + yi).astype(jnp.int32).reshape(1)

    def big_update(w, m, v, parts, mine, name):
        shp = w.shape
        w3, m3, v3 = (a.reshape((-1,) + shp[-2:]) for a in (w, m, v))
        p4 = parts.reshape((3,) + w3.shape)
        return [o.reshape(shp) for o in _adam_big(w3, m3, v3, p4, mine, chip, name)]

    u_cin = big_update(c_w_in, m_c_w_in, v_c_w_in, p_cin, [s_c[0]], "adam_c_w_in")
    u_cout = big_update(c_w_out, m_c_w_out, v_c_w_out, p_cout, [s_c[1]], "adam_c_w_out")
    tr_ = lambda a: jnp.swapaxes(a, 1, 2)
    u_wup = [tr_(o) for o in big_update(tr_(f_w_up), tr_(m_f_w_up), tr_(v_f_w_up), p_wup,
                                        [s_f0[0], s_f1[0]], "adam_f_w_up")]
    u_wdn = big_update(f_w_down, m_f_w_down, v_f_w_down, p_wdn, [s_f0[1], s_f1[1]], "adam_f_w_down")
    p_win0, p_wout0 = _chip_wait(started[3:], zones[:2], [0, 1], u_wdn[0], "rs_chip_wait_late")
    u_win0 = big_update(ab_w_in, m_ab_w_in, v_ab_w_in, p_win0, [s_ab[0]], "adam_ab_w_in")
    u_wout0 = big_update(ab_w_out, m_ab_w_out, v_ab_w_out, p_wout0, [s_ab[1]], "adam_ab_w_out")

    g_norm_mix = g1024[0:2]
    g_norm_ffn = g1024[2:4]
    g_norm_final = g1024[4:5]
    g_ccw = lax.dynamic_slice(g1024[5:8], (0, dev * (D // NDEV)), (3, D // NDEV))
    g_bcw = lax.dynamic_slice(g512[8:8 + BCONV], (0, dev * (DA // NDEV)), (BCONV, DA // NDEV))
    gfc = gfc.reshape(2, NG, 2, 3, FB)
    g_fcw = lax.dynamic_slice(gfc, (0, dev % NG, dev // NG, 0, 0), (2, 1, 1, 3, FB)).reshape(2, 3, FB)
    small_w = [norm_mix, norm_ffn, nfin, a_ln_g, a_ln_b, a_w_s[0], a_b_s[0], b_conv_w[0], b_conv_b,
               b_ln_g, b_ln_b, c_conv_w[0], f_conv_w]
    small_g = [g_norm_mix, g_norm_ffn, g_norm_final, g512[0:1], g512[1:2],
               dws.reshape(HEADS, CHUNK, CHUNK), dbs, g_bcw, g512[2:3],
               g512[3:4], g512[4:5], g_ccw, g_fcw]
    small_m = [m_norm_mix, m_norm_ffn, m_norm_final.reshape(1, D), m_a_ln_g, m_a_ln_b, m_a_w_s[0], m_a_b_s[0],
               m_b_conv_w[0], m_b_conv_b, m_b_ln_g, m_b_ln_b, m_c_conv_w[0], m_f_conv_w]
    small_v = [v_norm_mix, v_norm_ffn, v_norm_final.reshape(1, D), v_a_ln_g, v_a_ln_b, v_a_w_s[0], v_a_b_s[0],
               v_b_conv_w[0], v_b_conv_b, v_b_ln_g, v_b_ln_b, v_c_conv_w[0], v_f_conv_w]
    upd = _adam_small(small_w, small_g, small_m, small_v, "adam_small")
    ns = len(small_w)
    orig = [norm_mix, norm_ffn, norm_final, a_ln_g, a_ln_b, a_w_s, a_b_s, b_conv_w, b_conv_b,
            b_ln_g, b_ln_b, c_conv_w, f_conv_w]
    sg_out = [g.reshape(o.shape) for g, o in zip(small_g, orig)]
    sd_out = [a.reshape(o.shape) for a, o in zip(upd[0:ns], orig)]
    sm_out = [a.reshape(o.shape) for a, o in zip(upd[ns:2 * ns], orig)]
    sv_out = [a.reshape(o.shape) for a, o in zip(upd[2 * ns:3 * ns], orig)]

    def assemble(small, k):
        return [small[0], small[1], small[2], u_win0[k], small[3], small[4], small[5], small[6], small[7],
                small[8], small[9], small[10], u_wout0[k], u_cin[k], small[11], u_cout[k], u_wup[k],
                small[12], u_wdn[k]]

    grads = assemble(sg_out, 0)
    deltas = assemble(sd_out, 1)
    new_m = assemble(sm_out, 2)
    new_v = assemble(sv_out, 3)
    return (loss, grad_x.reshape(1, s, D), *grads, *deltas, *new_m, *new_v)
```
